```python
import jax, jax.numpy as jnp
from jax import lax
import numpy as np

D_MODEL = 1024
BATCH = 16
SEQ = 2048
DEPTH = 1

D_MIX = D_MODEL
ATTN_HEADS = 8
HEAD_DIM = 64
D_ATTN = ATTN_HEADS * HEAD_DIM
D_CONV = D_MIX - D_ATTN
CONV_GROUPS = 8
CONV_K = 31
DILATED_PATTERNS = ((128, 1), (512, 4), (2048, 16))
BLK = 128
D_FF = 2816
D_IN = 3 * D_ATTN + 2 * D_CONV
EPS = 1e-6

kernel_name = "hybrid_dilated_attn_conformer_conv_macaron"


def _rms(x, g):
    xf = x.astype(jnp.float32)
    y = xf * lax.rsqrt(jnp.mean(xf * xf, axis=-1, keepdims=True) + EPS)
    return (y * g.astype(jnp.float32)).astype(x.dtype)


def _layernorm(x, g, b):
    xf = x.astype(jnp.float32)
    mu = jnp.mean(xf, axis=-1, keepdims=True)
    var = jnp.mean(jnp.square(xf - mu), axis=-1, keepdims=True)
    y = (xf - mu) * lax.rsqrt(var + EPS)
    return (y * g.astype(jnp.float32) + b.astype(jnp.float32)).astype(x.dtype)


def _swiglu(x, w_gate, w_up, w_down):
    return (jax.nn.silu(x @ w_gate) * (x @ w_up)) @ w_down


def _banded_causal_attn(q, k, v, steps):
    Bp, L, H, Dh = q.shape
    N = L // BLK
    qb = q.reshape(Bp, N, BLK, H, Dh)
    kb = k.reshape(Bp, N, BLK, H, Dh)
    vb = v.reshape(Bp, N, BLK, H, Dh)

    def with_prev(a):
        prev = jnp.concatenate([jnp.zeros_like(a[:, :1]), a[:, :-1]], axis=1)
        return jnp.concatenate([prev, a], axis=2)

    kk, vv = with_prev(kb), with_prev(vb)
    s = jnp.einsum('bnqhd,bnkhd->bnhqk', qb, kk,
                   preferred_element_type=jnp.float32) * (Dh ** -0.5)
    qi = jnp.arange(BLK)[:, None]
    ci = jnp.arange(2 * BLK)[None, :]
    dist = BLK + qi - ci
    band = (dist >= 0) & (dist <= steps)
    first = (jnp.arange(N) == 0)[:, None, None] & (ci < BLK)[None]
    mask = band[None] & jnp.logical_not(first)
    s = jnp.where(mask[None, :, None], s, jnp.float32(-1e30))
    m = jnp.max(s, axis=-1, keepdims=True)
    p = jnp.exp(s - m)
    l = jnp.sum(p, axis=-1, keepdims=True)
    o = jnp.einsum('bnhqk,bnkhd->bnqhd', (p / l).astype(v.dtype), vv,
                   preferred_element_type=jnp.float32)
    lse = (m + jnp.log(l))[..., 0]
    return o.reshape(Bp, L, H, Dh), lse.transpose(0, 1, 3, 2).reshape(Bp, L, H)


def _dilated_causal_attn(q, k, v, window, dilation):
    B, S, H, Dh = q.shape
    span = dilation * BLK
    S_pad = -(-S // span) * span
    L = S_pad // dilation

    def strided(a):
        a = jnp.pad(a, ((0, 0), (0, S_pad - S), (0, 0), (0, 0)))
        return a.reshape(B, L, dilation, H, Dh).transpose(0, 2, 1, 3, 4).reshape(B * dilation, L, H, Dh)

    o, lse = _banded_causal_attn(strided(q), strided(k), strided(v), window // dilation)
    o = o.reshape(B, dilation, L, H, Dh).transpose(0, 2, 1, 3, 4).reshape(B, S_pad, H, Dh)[:, :S]
    lse = lse.reshape(B, dilation, L, H).transpose(0, 2, 1, 3).reshape(B, S_pad, H)[:, :S]
    return o, lse


def _conformer_conv(a, gate, conv_w, conv_b, ln_g, ln_b):
    glu = a * jax.nn.sigmoid(gate)
    y = lax.conv_general_dilated(
        glu, conv_w[:, None, :], window_strides=(1,), padding=[(CONV_K - 1, 0)],
        dimension_numbers=('NWC', 'WIO', 'NWC'), feature_group_count=D_CONV)
    y = y + conv_b
    return jax.nn.silu(_layernorm(y, ln_g, ln_b))


def _fwd_setup_inputs(seed: int = 0) -> dict:
    key = jax.random.key(seed)
    ks = jax.random.split(key, 20)
    L = DEPTH

    def nrm(k, shape, scale):
        return jax.random.normal(k, shape, jnp.float32) * scale

    def gain(k, shape):
        return 1.0 + 0.02 * jax.random.normal(k, shape, jnp.float32)

    return {
        "x": jax.random.normal(ks[0], (BATCH, SEQ, D_MODEL), jnp.float32),
        "ffn1_norm": gain(ks[1], (L, D_MODEL)),
        "ffn1_w_gate": nrm(ks[2], (L, D_MODEL, D_FF), D_MODEL ** -0.5),
        "ffn1_w_up": nrm(ks[3], (L, D_MODEL, D_FF), D_MODEL ** -0.5),
        "ffn1_w_down": nrm(ks[4], (L, D_FF, D_MODEL), D_FF ** -0.5),
        "mix_norm": gain(ks[5], (L, D_MODEL)),
        "w_in": nrm(ks[6], (L, D_MODEL, D_IN), D_MODEL ** -0.5),
        "q_norm": gain(ks[7], (L, HEAD_DIM)),
        "k_norm": gain(ks[8], (L, HEAD_DIM)),
        "conv_w": nrm(ks[9], (L, CONV_K, D_CONV), CONV_K ** -0.5),
        "conv_b": nrm(ks[10], (L, D_CONV), 0.02),
        "conv_ln_g": gain(ks[11], (L, D_CONV)),
        "conv_ln_b": nrm(ks[12], (L, D_CONV), 0.02),
        "w_out": nrm(ks[13], (L, D_MIX, D_MODEL), D_MIX ** -0.5),
        "ffn2_norm": gain(ks[14], (L, D_MODEL)),
        "ffn2_w_gate": nrm(ks[15], (L, D_MODEL, D_FF), D_MODEL ** -0.5),
        "ffn2_w_up": nrm(ks[16], (L, D_MODEL, D_FF), D_MODEL ** -0.5),
        "ffn2_w_down": nrm(ks[17], (L, D_FF, D_MODEL), D_FF ** -0.5),
    }


def _fwd_reference(x, ffn1_norm, ffn1_w_gate, ffn1_w_up, ffn1_w_down, mix_norm, w_in, q_norm, k_norm,
              conv_w, conv_b, conv_ln_g, conv_ln_b, w_out, ffn2_norm, ffn2_w_gate, ffn2_w_up,
              ffn2_w_down):
    B, S, _ = x.shape
    h = x
    for l in range(DEPTH):
        h = h + 0.5 * _swiglu(_rms(h, ffn1_norm[l]), ffn1_w_gate[l], ffn1_w_up[l], ffn1_w_down[l])

        u = _rms(h, mix_norm[l]) @ w_in[l]
        q, k, v, ca, cg = jnp.split(
            u, np.cumsum([D_ATTN, D_ATTN, D_ATTN, D_CONV]).tolist(), axis=-1)

        q = _rms(q.reshape(B, S, ATTN_HEADS, HEAD_DIM), q_norm[l])
        k = _rms(k.reshape(B, S, ATTN_HEADS, HEAD_DIM), k_norm[l])
        v = v.reshape(B, S, ATTN_HEADS, HEAD_DIM)
        outs, lses = [], []
        for window, dilation in DILATED_PATTERNS:
            o, lse = _dilated_causal_attn(q, k, v, window, dilation)
            outs.append(o)
            lses.append(lse)
        wts = jax.nn.softmax(jnp.stack(lses, axis=0), axis=0)
        attn = jnp.sum(wts[..., None] * jnp.stack(outs, axis=0), axis=0)
        attn = attn.reshape(B, S, D_ATTN).astype(h.dtype)

        conv = _conformer_conv(ca, cg, conv_w[l], conv_b[l], conv_ln_g[l], conv_ln_b[l])

        h = h + jnp.concatenate([attn, conv], axis=-1) @ w_out[l]

        h = h + 0.5 * _swiglu(_rms(h, ffn2_norm[l]), ffn2_w_gate[l], ffn2_w_up[l], ffn2_w_down[l])
    return h


import jax as _jax
import jax.numpy as _jnp

TWIN_FORMAT = 'train_step'
FWD_PARAMS = ['x', 'ffn1_norm', 'ffn1_w_gate', 'ffn1_w_up', 'ffn1_w_down', 'mix_norm', 'w_in', 'q_norm', 'k_norm', 'conv_w', 'conv_b', 'conv_ln_g', 'conv_ln_b', 'w_out', 'ffn2_norm', 'ffn2_w_gate', 'ffn2_w_up', 'ffn2_w_down']
TWIN_WEIGHTS = ['ffn1_norm', 'ffn1_w_gate', 'ffn1_w_up', 'ffn1_w_down', 'mix_norm', 'w_in', 'q_norm', 'k_norm', 'conv_w', 'conv_b', 'conv_ln_g', 'conv_ln_b', 'w_out', 'ffn2_norm', 'ffn2_w_gate', 'ffn2_w_up', 'ffn2_w_down']
TWIN_DIFF_INPUT = 'x'
TWIN_INPUTS = ['x', 'ffn1_norm', 'ffn1_w_gate', 'ffn1_w_up', 'ffn1_w_down', 'mix_norm', 'w_in', 'q_norm', 'k_norm', 'conv_w', 'conv_b', 'conv_ln_g', 'conv_ln_b', 'w_out', 'ffn2_norm', 'ffn2_w_gate', 'ffn2_w_up', 'ffn2_w_down', 'loss_target', 'm_ffn1_norm', 'm_ffn1_w_gate', 'm_ffn1_w_up', 'm_ffn1_w_down', 'm_mix_norm', 'm_w_in', 'm_q_norm', 'm_k_norm', 'm_conv_w', 'm_conv_b', 'm_conv_ln_g', 'm_conv_ln_b', 'm_w_out', 'm_ffn2_norm', 'm_ffn2_w_gate', 'm_ffn2_w_up', 'm_ffn2_w_down', 'v_ffn1_norm', 'v_ffn1_w_gate', 'v_ffn1_w_up', 'v_ffn1_w_down', 'v_mix_norm', 'v_w_in', 'v_q_norm', 'v_k_norm', 'v_conv_w', 'v_conv_b', 'v_conv_ln_g', 'v_conv_ln_b', 'v_w_out', 'v_ffn2_norm', 'v_ffn2_w_gate', 'v_ffn2_w_up', 'v_ffn2_w_down']
TWIN_OUTPUTS = ['loss', 'grad_x', 'grad_ffn1_norm', 'grad_ffn1_w_gate', 'grad_ffn1_w_up', 'grad_ffn1_w_down', 'grad_mix_norm', 'grad_w_in', 'grad_q_norm', 'grad_k_norm', 'grad_conv_w', 'grad_conv_b', 'grad_conv_ln_g', 'grad_conv_ln_b', 'grad_w_out', 'grad_ffn2_norm', 'grad_ffn2_w_gate', 'grad_ffn2_w_up', 'grad_ffn2_w_down', 'delta_ffn1_norm', 'delta_ffn1_w_gate', 'delta_ffn1_w_up', 'delta_ffn1_w_down', 'delta_mix_norm', 'delta_w_in', 'delta_q_norm', 'delta_k_norm', 'delta_conv_w', 'delta_conv_b', 'delta_conv_ln_g', 'delta_conv_ln_b', 'delta_w_out', 'delta_ffn2_norm', 'delta_ffn2_w_gate', 'delta_ffn2_w_up', 'delta_ffn2_w_down', 'new_m_ffn1_norm', 'new_m_ffn1_w_gate', 'new_m_ffn1_w_up', 'new_m_ffn1_w_down', 'new_m_mix_norm', 'new_m_w_in', 'new_m_q_norm', 'new_m_k_norm', 'new_m_conv_w', 'new_m_conv_b', 'new_m_conv_ln_g', 'new_m_conv_ln_b', 'new_m_w_out', 'new_m_ffn2_norm', 'new_m_ffn2_w_gate', 'new_m_ffn2_w_up', 'new_m_ffn2_w_down', 'new_v_ffn1_norm', 'new_v_ffn1_w_gate', 'new_v_ffn1_w_up', 'new_v_ffn1_w_down', 'new_v_mix_norm', 'new_v_w_in', 'new_v_q_norm', 'new_v_k_norm', 'new_v_conv_w', 'new_v_conv_b', 'new_v_conv_ln_g', 'new_v_conv_ln_b', 'new_v_w_out', 'new_v_ffn2_norm', 'new_v_ffn2_w_gate', 'new_v_ffn2_w_up', 'new_v_ffn2_w_down']
TWIN_LEAF_KINDS = {'loss': 'loss', 'grad_x': 'grad_x', 'grad_ffn1_norm': 'grad_w', 'grad_ffn1_w_gate': 'grad_w', 'grad_ffn1_w_up': 'grad_w', 'grad_ffn1_w_down': 'grad_w', 'grad_mix_norm': 'grad_w', 'grad_w_in': 'grad_w', 'grad_q_norm': 'grad_w', 'grad_k_norm': 'grad_w', 'grad_conv_w': 'grad_w', 'grad_conv_b': 'grad_w', 'grad_conv_ln_g': 'grad_w', 'grad_conv_ln_b': 'grad_w', 'grad_w_out': 'grad_w', 'grad_ffn2_norm': 'grad_w', 'grad_ffn2_w_gate': 'grad_w', 'grad_ffn2_w_up': 'grad_w', 'grad_ffn2_w_down': 'grad_w', 'delta_ffn1_norm': 'delta_w', 'delta_ffn1_w_gate': 'delta_w', 'delta_ffn1_w_up': 'delta_w', 'delta_ffn1_w_down': 'delta_w', 'delta_mix_norm': 'delta_w', 'delta_w_in': 'delta_w', 'delta_q_norm': 'delta_w', 'delta_k_norm': 'delta_w', 'delta_conv_w': 'delta_w', 'delta_conv_b': 'delta_w', 'delta_conv_ln_g': 'delta_w', 'delta_conv_ln_b': 'delta_w', 'delta_w_out': 'delta_w', 'delta_ffn2_norm': 'delta_w', 'delta_ffn2_w_gate': 'delta_w', 'delta_ffn2_w_up': 'delta_w', 'delta_ffn2_w_down': 'delta_w', 'new_m_ffn1_norm': 'new_m', 'new_m_ffn1_w_gate': 'new_m', 'new_m_ffn1_w_up': 'new_m', 'new_m_ffn1_w_down': 'new_m', 'new_m_mix_norm': 'new_m', 'new_m_w_in': 'new_m', 'new_m_q_norm': 'new_m', 'new_m_k_norm': 'new_m', 'new_m_conv_w': 'new_m', 'new_m_conv_b': 'new_m', 'new_m_conv_ln_g': 'new_m', 'new_m_conv_ln_b': 'new_m', 'new_m_w_out': 'new_m', 'new_m_ffn2_norm': 'new_m', 'new_m_ffn2_w_gate': 'new_m', 'new_m_ffn2_w_up': 'new_m', 'new_m_ffn2_w_down': 'new_m', 'new_v_ffn1_norm': 'new_v', 'new_v_ffn1_w_gate': 'new_v', 'new_v_ffn1_w_up': 'new_v', 'new_v_ffn1_w_down': 'new_v', 'new_v_mix_norm': 'new_v', 'new_v_w_in': 'new_v', 'new_v_q_norm': 'new_v', 'new_v_k_norm': 'new_v', 'new_v_conv_w': 'new_v', 'new_v_conv_b': 'new_v', 'new_v_conv_ln_g': 'new_v', 'new_v_conv_ln_b': 'new_v', 'new_v_w_out': 'new_v', 'new_v_ffn2_norm': 'new_v', 'new_v_ffn2_w_gate': 'new_v', 'new_v_ffn2_w_up': 'new_v', 'new_v_ffn2_w_down': 'new_v'}


def _forward(args):
    return _fwd_reference(*[args[k] for k in FWD_PARAMS])


def _output_shape():
    out = _jax.eval_shape(lambda: _forward(_fwd_setup_inputs(0)))
    return out.shape, out.dtype

N_MICROBATCH = 1
ADAM_LR = 0.001
ADAM_B1 = 0.9
ADAM_B2 = 0.999
ADAM_EPS = 1e-08
ADAM_WD = 0.01
ADAM_STEP = 10
PER_EXAMPLE_BATCH_AXIS = {'x': 0, 'loss_target': 0}
SHARED_INPUTS = []
_WEIGHT_DTYPES = {'ffn1_norm': _jnp.float32, 'ffn1_w_gate': _jnp.float32, 'ffn1_w_up': _jnp.float32, 'ffn1_w_down': _jnp.float32, 'mix_norm': _jnp.float32, 'w_in': _jnp.float32, 'q_norm': _jnp.float32, 'k_norm': _jnp.float32, 'conv_w': _jnp.float32, 'conv_b': _jnp.float32, 'conv_ln_g': _jnp.float32, 'conv_ln_b': _jnp.float32, 'w_out': _jnp.float32, 'ffn2_norm': _jnp.float32, 'ffn2_w_gate': _jnp.float32, 'ffn2_w_up': _jnp.float32, 'ffn2_w_down': _jnp.float32}
MOMENT_SCALE = {'ffn1_norm': 6.144196e+00, 'ffn1_w_gate': 6.643349e-02, 'ffn1_w_up': 7.201469e-02, 'ffn1_w_down': 1.170226e-01, 'mix_norm': 2.371596e-01, 'w_in': 1.114223e-01, 'q_norm': 1.690596e+00, 'k_norm': 1.689805e+00, 'conv_w': 2.893530e-01, 'conv_b': 4.654965e+00, 'conv_ln_g': 1.408755e+01, 'conv_ln_b': 9.329029e+00, 'w_out': 6.131084e-01, 'ffn2_norm': 6.213876e+00, 'ffn2_w_gate': 9.151037e-02, 'ffn2_w_up': 7.763490e-02, 'ffn2_w_down': 1.240510e-01}


def _to_microbatches(a, axis):
    t = _jnp.moveaxis(a, axis, 0)
    t = t.reshape((N_MICROBATCH, t.shape[0] // N_MICROBATCH) + t.shape[1:])
    return _jnp.moveaxis(t, 1, axis + 1)


def setup_inputs(seed: int = 0) -> dict:
    inp = _fwd_setup_inputs(seed)
    key = _jax.random.fold_in(_jax.random.key(seed), 7919)
    shape, _ = _output_shape()
    out = dict(inp)
    out["loss_target"] = _jax.random.normal(_jax.random.fold_in(key, 0), shape, _jnp.float32)
    for i, name in enumerate(TWIN_WEIGHTS):
        w = inp[name].astype(_jnp.float32)
        if MOMENT_SCALE is None:
            s = _jnp.sqrt(_jnp.mean(_jnp.square(w)) + 1e-30)
        else:
            s = MOMENT_SCALE[name]
        km, kv = _jax.random.split(_jax.random.fold_in(key, i + 1))
        out[name] = w
        out["m_" + name] = s * _jax.random.normal(km, w.shape, _jnp.float32)
        out["v_" + name] = (s * s) * _jax.random.uniform(kv, w.shape, _jnp.float32, 0.5, 1.5)
    if N_MICROBATCH > 1:
        for name, axis in PER_EXAMPLE_BATCH_AXIS.items():
            out[name] = _to_microbatches(out[name], axis)
    return {'x': out['x'], 'ffn1_norm': out['ffn1_norm'], 'ffn1_w_gate': out['ffn1_w_gate'], 'ffn1_w_up': out['ffn1_w_up'], 'ffn1_w_down': out['ffn1_w_down'], 'mix_norm': out['mix_norm'], 'w_in': out['w_in'], 'q_norm': out['q_norm'], 'k_norm': out['k_norm'], 'conv_w': out['conv_w'], 'conv_b': out['conv_b'], 'conv_ln_g': out['conv_ln_g'], 'conv_ln_b': out['conv_ln_b'], 'w_out': out['w_out'], 'ffn2_norm': out['ffn2_norm'], 'ffn2_w_gate': out['ffn2_w_gate'], 'ffn2_w_up': out['ffn2_w_up'], 'ffn2_w_down': out['ffn2_w_down'], 'loss_target': out['loss_target'], 'm_ffn1_norm': out['m_ffn1_norm'], 'm_ffn1_w_gate': out['m_ffn1_w_gate'], 'm_ffn1_w_up': out['m_ffn1_w_up'], 'm_ffn1_w_down': out['m_ffn1_w_down'], 'm_mix_norm': out['m_mix_norm'], 'm_w_in': out['m_w_in'], 'm_q_norm': out['m_q_norm'], 'm_k_norm': out['m_k_norm'], 'm_conv_w': out['m_conv_w'], 'm_conv_b': out['m_conv_b'], 'm_conv_ln_g': out['m_conv_ln_g'], 'm_conv_ln_b': out['m_conv_ln_b'], 'm_w_out': out['m_w_out'], 'm_ffn2_norm': out['m_ffn2_norm'], 'm_ffn2_w_gate': out['m_ffn2_w_gate'], 'm_ffn2_w_up': out['m_ffn2_w_up'], 'm_ffn2_w_down': out['m_ffn2_w_down'], 'v_ffn1_norm': out['v_ffn1_norm'], 'v_ffn1_w_gate': out['v_ffn1_w_gate'], 'v_ffn1_w_up': out['v_ffn1_w_up'], 'v_ffn1_w_down': out['v_ffn1_w_down'], 'v_mix_norm': out['v_mix_norm'], 'v_w_in': out['v_w_in'], 'v_q_norm': out['v_q_norm'], 'v_k_norm': out['v_k_norm'], 'v_conv_w': out['v_conv_w'], 'v_conv_b': out['v_conv_b'], 'v_conv_ln_g': out['v_conv_ln_g'], 'v_conv_ln_b': out['v_conv_ln_b'], 'v_w_out': out['v_w_out'], 'v_ffn2_norm': out['v_ffn2_norm'], 'v_ffn2_w_gate': out['v_ffn2_w_gate'], 'v_ffn2_w_up': out['v_ffn2_w_up'], 'v_ffn2_w_down': out['v_ffn2_w_down']}


def _loss(weights, diff, rest, loss_target):
    with _jax.named_scope("forward"):
        args = {**rest, TWIN_DIFF_INPUT: diff, **{k: w.astype(_WEIGHT_DTYPES[k]) for k, w in weights.items()}}
        y = _forward(args)
    with _jax.named_scope("loss_head"):
        err = _jnp.square(y.astype(_jnp.float32) - loss_target)
        return 0.5 * _jnp.sum(_jnp.mean(err, axis=-1)) if err.ndim else 0.5 * err


def _adamw(w, g, m, v):
    m = ADAM_B1 * m + (1.0 - ADAM_B1) * g
    v = ADAM_B2 * v + (1.0 - ADAM_B2) * _jnp.square(g)
    m_hat = m / (1.0 - ADAM_B1 ** ADAM_STEP)
    v_hat = v / (1.0 - ADAM_B2 ** ADAM_STEP)
    delta = -ADAM_LR * (m_hat / (_jnp.sqrt(v_hat) + ADAM_EPS) + ADAM_WD * w)
    return delta, m, v


def reference(x, ffn1_norm, ffn1_w_gate, ffn1_w_up, ffn1_w_down, mix_norm, w_in, q_norm, k_norm, conv_w, conv_b, conv_ln_g, conv_ln_b, w_out, ffn2_norm, ffn2_w_gate, ffn2_w_up, ffn2_w_down, loss_target, m_ffn1_norm, m_ffn1_w_gate, m_ffn1_w_up, m_ffn1_w_down, m_mix_norm, m_w_in, m_q_norm, m_k_norm, m_conv_w, m_conv_b, m_conv_ln_g, m_conv_ln_b, m_w_out, m_ffn2_norm, m_ffn2_w_gate, m_ffn2_w_up, m_ffn2_w_down, v_ffn1_norm, v_ffn1_w_gate, v_ffn1_w_up, v_ffn1_w_down, v_mix_norm, v_w_in, v_q_norm, v_k_norm, v_conv_w, v_conv_b, v_conv_ln_g, v_conv_ln_b, v_w_out, v_ffn2_norm, v_ffn2_w_gate, v_ffn2_w_up, v_ffn2_w_down):
    given = dict(x=x, ffn1_norm=ffn1_norm, ffn1_w_gate=ffn1_w_gate, ffn1_w_up=ffn1_w_up, ffn1_w_down=ffn1_w_down, mix_norm=mix_norm, w_in=w_in, q_norm=q_norm, k_norm=k_norm, conv_w=conv_w, conv_b=conv_b, conv_ln_g=conv_ln_g, conv_ln_b=conv_ln_b, w_out=w_out, ffn2_norm=ffn2_norm, ffn2_w_gate=ffn2_w_gate, ffn2_w_up=ffn2_w_up, ffn2_w_down=ffn2_w_down, loss_target=loss_target, m_ffn1_norm=m_ffn1_norm, m_ffn1_w_gate=m_ffn1_w_gate, m_ffn1_w_up=m_ffn1_w_up, m_ffn1_w_down=m_ffn1_w_down, m_mix_norm=m_mix_norm, m_w_in=m_w_in, m_q_norm=m_q_norm, m_k_norm=m_k_norm, m_conv_w=m_conv_w, m_conv_b=m_conv_b, m_conv_ln_g=m_conv_ln_g, m_conv_ln_b=m_conv_ln_b, m_w_out=m_w_out, m_ffn2_norm=m_ffn2_norm, m_ffn2_w_gate=m_ffn2_w_gate, m_ffn2_w_up=m_ffn2_w_up, m_ffn2_w_down=m_ffn2_w_down, v_ffn1_norm=v_ffn1_norm, v_ffn1_w_gate=v_ffn1_w_gate, v_ffn1_w_up=v_ffn1_w_up, v_ffn1_w_down=v_ffn1_w_down, v_mix_norm=v_mix_norm, v_w_in=v_w_in, v_q_norm=v_q_norm, v_k_norm=v_k_norm, v_conv_w=v_conv_w, v_conv_b=v_conv_b, v_conv_ln_g=v_conv_ln_g, v_conv_ln_b=v_conv_ln_b, v_w_out=v_w_out, v_ffn2_norm=v_ffn2_norm, v_ffn2_w_gate=v_ffn2_w_gate, v_ffn2_w_up=v_ffn2_w_up, v_ffn2_w_down=v_ffn2_w_down)
    weights = {n: given[n] for n in TWIN_WEIGHTS}
    shared = {n: given[n] for n in SHARED_INPUTS}
    per_example = {n: given[n] for n in ['x']}
    grad_fn = _jax.value_and_grad(_loss, argnums=(0, 1))

    def one_microbatch(ex, loss_target):
        ex = dict(ex)
        diff = ex.pop(TWIN_DIFF_INPUT)
        return grad_fn(weights, diff, {**shared, **ex}, loss_target)

    if N_MICROBATCH == 1:
        loss, (grad_w, grad_x) = one_microbatch(per_example, given["loss_target"])
    else:
        def body(carry, xs):
            loss_sum, grad_sum = carry
            l_k, (gw_k, gx_k) = one_microbatch(xs[0], xs[1])
            with _jax.named_scope("update"):
                return (loss_sum + l_k, _jax.tree.map(_jnp.add, grad_sum, gw_k)), gx_k

        init = (_jnp.zeros((), _jnp.float32), _jax.tree.map(_jnp.zeros_like, weights))
        (loss, grad_w), grad_x = _jax.lax.scan(body, init, (per_example, given["loss_target"]))
    with _jax.named_scope("update"):
        delta_w, new_m, new_v = {}, {}, {}
        for n in TWIN_WEIGHTS:
            delta_w[n], new_m[n], new_v[n] = _adamw(weights[n], grad_w[n], given["m_" + n], given["v_" + n])
    return (loss, grad_x, *[grad_w[n] for n in TWIN_WEIGHTS], *[delta_w[n] for n in TWIN_WEIGHTS],
            *[new_m[n] for n in TWIN_WEIGHTS], *[new_v[n] for n in TWIN_WEIGHTS])
```

```python
import functools

import jax
import jax.numpy as jnp
from jax import lax
from jax.experimental import pallas as pl
from jax.experimental.pallas import tpu as pltpu

F32 = jnp.float32
BF16 = jnp.bfloat16

D = 1024
FF = 2816
HD = 64
DA = 512
DC = 512
DIN = 2560
CK = 31
BLK = 128
DILS = (1, 4, 16)
EPS = 1e-6
NDEV = 8
MESH = pl.DeviceIdType.MESH

LR, B1, B2, AEPS, WD, STEP = 0.001, 0.9, 0.999, 1e-08, 0.01, 10

NT = (((1,), (1,)), ((), ()))
TN = (((0,), (0,)), ((), ()))

VMEM_LIMIT = 56 * 1024 * 1024


def _cp(sem=None):
    return pltpu.CompilerParams(dimension_semantics=sem, vmem_limit_bytes=VMEM_LIMIT)


def _sigmoid(x):
    return 1.0 / (1.0 + jnp.exp(-x))


def _ffn_fwd(x, gain, wg, wu, wd, target, name):
    T = x.shape[0]
    tm, tf = 1024, 256
    nt, nf = T // tm, FF // tf
    with_loss = target is not None

    def body(*refs):
        if with_loss:
            (x_ref, gain_ref, wg_ref, wu_ref, wd_ref, t_ref,
             h_ref, n_ref, g_ref, u_ref, dout_ref, dyb_ref, sq_ref, nb_scr, acc_scr) = refs
        else:
            (x_ref, gain_ref, wg_ref, wu_ref, wd_ref,
             h_ref, n_ref, g_ref, u_ref, nb_scr, acc_scr) = refs
        f = pl.program_id(1)

        @pl.when(f == 0)
        def _():
            xv = x_ref[...]
            r = lax.rsqrt(jnp.mean(xv * xv, axis=-1, keepdims=True) + EPS)
            nb = (xv * r * gain_ref[...]).astype(BF16)
            nb_scr[...] = nb
            n_ref[...] = nb
            acc_scr[...] = jnp.zeros_like(acc_scr)

        nb = nb_scr[...]
        g = lax.dot_general(nb, wg_ref[...], NT, preferred_element_type=F32)
        u = lax.dot_general(nb, wu_ref[...], NT, preferred_element_type=F32)
        a = g * _sigmoid(g) * u
        g_ref[...] = g.astype(BF16)
        u_ref[...] = u.astype(BF16)
        acc_scr[...] += jnp.dot(a.astype(BF16), wd_ref[...], preferred_element_type=F32)

        @pl.when(f == nf - 1)
        def _():
            h = x_ref[...] + 0.5 * acc_scr[...]
            h_ref[...] = h
            if with_loss:
                e = h - t_ref[...]
                dout = e * (1.0 / D)
                dout_ref[...] = dout
                dyb_ref[...] = (0.5 * dout).astype(BF16)
                sq_ref[...] = jnp.sum(e * e, axis=0, keepdims=True)[None]

    row = pl.BlockSpec((tm, D), lambda t, f: (t, 0))
    wspec = pl.BlockSpec((tf, D), lambda t, f: (f, 0))
    gspec = pl.BlockSpec((tm, tf), lambda t, f: (t, f))
    in_specs = [row, pl.BlockSpec((1, D), lambda t, f: (0, 0)), wspec, wspec, wspec]
    out_shape = [jax.ShapeDtypeStruct((T, D), F32), jax.ShapeDtypeStruct((T, D), BF16),
                 jax.ShapeDtypeStruct((T, FF), BF16), jax.ShapeDtypeStruct((T, FF), BF16)]
    out_specs = [row, row, gspec, gspec]
    args = [x, gain, wg, wu, wd]
    if with_loss:
        in_specs.append(row)
        args.append(target)
        out_shape += [jax.ShapeDtypeStruct((T, D), F32), jax.ShapeDtypeStruct((T, D), BF16),
                      jax.ShapeDtypeStruct((nt, 1, D), F32)]
        out_specs += [row, row, pl.BlockSpec((1, 1, D), lambda t, f: (t, 0, 0))]
    return pl.pallas_call(
        body, grid=(nt, nf), in_specs=in_specs, out_specs=out_specs, out_shape=out_shape,
        scratch_shapes=[pltpu.VMEM((tm, D), BF16), pltpu.VMEM((tm, D), F32)],
        compiler_params=_cp(("parallel", "arbitrary")), name=name)(*args)


def _ffn_bwd(dyb, nb, g, u, wg, wu, wd, name):
    T = dyb.shape[0]
    tm, tf = 512, 256
    nt, nf = T // tm, FF // tf

    def body(dy_ref, n_ref, g_ref, u_ref, wg_ref, wu_ref, wd_ref,
             dwg_ref, dwu_ref, dwd_ref, dn_ref, ag_scr, au_scr, ad_scr):
        f, t = pl.program_id(0), pl.program_id(1)

        @pl.when(t == 0)
        def _():
            ag_scr[...] = jnp.zeros_like(ag_scr)
            au_scr[...] = jnp.zeros_like(au_scr)
            ad_scr[...] = jnp.zeros_like(ad_scr)

        dy = dy_ref[...]
        n = n_ref[...]
        gv = g_ref[...].astype(F32)
        uv = u_ref[...].astype(F32)
        da = lax.dot_general(dy, wd_ref[...], NT, preferred_element_type=F32)
        sg = _sigmoid(gv)
        silu = gv * sg
        ab = (silu * uv).astype(BF16)
        dgb = (da * uv * (sg * (1.0 + gv * (1.0 - sg)))).astype(BF16)
        dub = (da * silu).astype(BF16)
        ad_scr[...] += lax.dot_general(ab, dy, TN, preferred_element_type=F32)
        ag_scr[...] += lax.dot_general(dgb, n, TN, preferred_element_type=F32)
        au_scr[...] += lax.dot_general(dub, n, TN, preferred_element_type=F32)
        dn = (jnp.dot(dgb, wg_ref[...], preferred_element_type=F32)
              + jnp.dot(dub, wu_ref[...], preferred_element_type=F32))
        rows = pl.ds(pl.multiple_of(t * tm, tm), tm)

        @pl.when(f == 0)
        def _():
            dn_ref[rows, :] = dn

        @pl.when(f > 0)
        def _():
            dn_ref[rows, :] += dn

        @pl.when(t == nt - 1)
        def _():
            dwg_ref[...] = ag_scr[...].astype(BF16)
            dwu_ref[...] = au_scr[...].astype(BF16)
            dwd_ref[...] = ad_scr[...].astype(BF16)

    row = pl.BlockSpec((tm, D), lambda f, t: (t, 0))
    gspec = pl.BlockSpec((tm, tf), lambda f, t: (t, f))
    wspec = pl.BlockSpec((tf, D), lambda f, t: (f, 0))
    return pl.pallas_call(
        body, grid=(nf, nt),
        in_specs=[row, row, gspec, gspec, wspec, wspec, wspec],
        out_specs=[wspec, wspec, wspec, pl.BlockSpec((T, D), lambda f, t: (0, 0))],
        out_shape=[jax.ShapeDtypeStruct((FF, D), BF16)] * 3 + [jax.ShapeDtypeStruct((T, D), F32)],
        scratch_shapes=[pltpu.VMEM((tf, D), F32)] * 3,
        compiler_params=_cp(("arbitrary", "arbitrary")), name=name)(dyb, nb, g, u, wg, wu, wd)


def _rms_bwd_rows(dn, xv, gain):
    r = lax.rsqrt(jnp.mean(xv * xv, axis=-1, keepdims=True) + EPS)
    xhat = xv * r
    dxhat = dn * gain
    dx = r * (dxhat - xhat * jnp.mean(dxhat * xhat, axis=-1, keepdims=True))
    return dx, jnp.sum(dn * xhat, axis=0, keepdims=True)


def _norm_bwd(dn, x, dout, gain, name):
    T = x.shape[0]
    tm = 512
    nt = T // tm

    def body(dn_ref, x_ref, dout_ref, gain_ref, dx_ref, dg_ref):
        dx, dgain = _rms_bwd_rows(dn_ref[...], x_ref[...], gain_ref[...])
        dx_ref[...] = dout_ref[...] + dx
        dg_ref[...] = dgain[None]

    row = pl.BlockSpec((tm, D), lambda t: (t, 0))
    return pl.pallas_call(
        body, grid=(nt,), in_specs=[row, row, row, pl.BlockSpec((1, D), lambda t: (0, 0))],
        out_specs=[row, pl.BlockSpec((1, 1, D), lambda t: (t, 0, 0))],
        out_shape=[jax.ShapeDtypeStruct((T, D), F32), jax.ShapeDtypeStruct((nt, 1, D), F32)],
        compiler_params=_cp(("parallel",)), name=name)(dn, x, dout, gain)


def _mix_in(h, gain, win):
    T = h.shape[0]
    tm = 512

    def body(h_ref, gain_ref, w_ref, u_ref, n_ref):
        xv = h_ref[...]
        r = lax.rsqrt(jnp.mean(xv * xv, axis=-1, keepdims=True) + EPS)
        nb = (xv * r * gain_ref[...]).astype(BF16)
        n_ref[...] = nb
        u_ref[...] = lax.dot_general(nb, w_ref[...], NT, preferred_element_type=F32).astype(BF16)

    row = pl.BlockSpec((tm, D), lambda t: (t, 0))
    return pl.pallas_call(
        body, grid=(T // tm,),
        in_specs=[row, pl.BlockSpec((1, D), lambda t: (0, 0)), pl.BlockSpec((DIN, D), lambda t: (0, 0))],
        out_specs=[pl.BlockSpec((tm, DIN), lambda t: (t, 0)), row],
        out_shape=[jax.ShapeDtypeStruct((T, DIN), BF16), jax.ShapeDtypeStruct((T, D), BF16)],
        compiler_params=_cp(("parallel",)), name="mix_in")(h, gain, win)


def _mix_out(h, attn, conv, wout):
    T = h.shape[0]
    tm = 512

    def body(h_ref, a_ref, c_ref, w_ref, o_ref):
        o_ref[...] = (h_ref[...]
                      + jnp.dot(a_ref[...], w_ref[0:DA, :], preferred_element_type=F32)
                      + jnp.dot(c_ref[...], w_ref[DA:D, :], preferred_element_type=F32))

    row = pl.BlockSpec((tm, D), lambda t: (t, 0))
    half = pl.BlockSpec((tm, DA), lambda t: (t, 0))
    return pl.pallas_call(
        body, grid=(T // tm,),
        in_specs=[row, half, half, pl.BlockSpec((D, D), lambda t: (0, 0))],
        out_specs=row, out_shape=jax.ShapeDtypeStruct((T, D), F32),
        compiler_params=_cp(("parallel",)), name="mix_out")(h, attn, conv, wout)


def _mix_out_bwd(dh, attn, conv, wout):
    T = dh.shape[0]
    tm = 512
    nt = T // tm

    def body(dh_ref, a_ref, c_ref, w_ref, da_ref, dc_ref, dw_ref, acc_scr):
        t = pl.program_id(0)

        @pl.when(t == 0)
        def _():
            acc_scr[...] = jnp.zeros_like(acc_scr)

        dhb = dh_ref[...].astype(BF16)
        dmix = lax.dot_general(dhb, w_ref[...], NT, preferred_element_type=F32)
        da_ref[...] = dmix[:, 0:DA].astype(BF16)
        dc_ref[...] = dmix[:, DA:D].astype(BF16)
        acc_scr[0:DA, :] += lax.dot_general(a_ref[...], dhb, TN, preferred_element_type=F32)
        acc_scr[DA:D, :] += lax.dot_general(c_ref[...], dhb, TN, preferred_element_type=F32)

        @pl.when(t == nt - 1)
        def _():
            dw_ref[...] = acc_scr[...].astype(BF16)

    row = pl.BlockSpec((tm, D), lambda t: (t, 0))
    half = pl.BlockSpec((tm, DA), lambda t: (t, 0))
    full = pl.BlockSpec((D, D), lambda t: (0, 0))
    return pl.pallas_call(
        body, grid=(nt,), in_specs=[row, half, half, full], out_specs=[half, half, full],
        out_shape=[jax.ShapeDtypeStruct((T, DA), BF16)] * 2 + [jax.ShapeDtypeStruct((D, D), BF16)],
        scratch_shapes=[pltpu.VMEM((D, D), F32)],
        compiler_params=_cp(("arbitrary",)), name="mix_out_bwd")(dh, attn, conv, wout)


def _mix_in_bwd(dparts, win, nb, h, dh, gain):
    T = h.shape[0]
    tm = 512
    nt = T // tm

    def body(d0, d1, d2, d3, d4, w_ref, n_ref, h_ref, dh_ref, gain_ref,
             dw_ref, dx_ref, dyb_ref, dg_ref, acc_scr):
        t = pl.program_id(0)

        @pl.when(t == 0)
        def _():
            acc_scr[...] = jnp.zeros_like(acc_scr)

        n = n_ref[...]
        dn = jnp.zeros((tm, D), F32)
        for i, d_ref in enumerate((d0, d1, d2, d3, d4)):
            dv = d_ref[...]
            dn = dn + jnp.dot(dv, w_ref[i * DA:(i + 1) * DA, :], preferred_element_type=F32)
            acc_scr[i * DA:(i + 1) * DA, :] += lax.dot_general(dv, n, TN, preferred_element_type=F32)
        dx, dgain = _rms_bwd_rows(dn, h_ref[...], gain_ref[...])
        tot = dh_ref[...] + dx
        dx_ref[...] = tot
        dyb_ref[...] = (0.5 * tot).astype(BF16)
        dg_ref[...] = dgain[None]

        @pl.when(t == nt - 1)
        def _():
            dw_ref[...] = acc_scr[...].astype(BF16)

    row = pl.BlockSpec((tm, D), lambda t: (t, 0))
    half = pl.BlockSpec((tm, DA), lambda t: (t, 0))
    full = pl.BlockSpec((DIN, D), lambda t: (0, 0))
    return pl.pallas_call(
        body, grid=(nt,),
        in_specs=[half] * 5 + [full, row, row, row, pl.BlockSpec((1, D), lambda t: (0, 0))],
        out_specs=[full, row, row, pl.BlockSpec((1, 1, D), lambda t: (t, 0, 0))],
        out_shape=[jax.ShapeDtypeStruct((DIN, D), BF16), jax.ShapeDtypeStruct((T, D), F32),
                   jax.ShapeDtypeStruct((T, D), BF16), jax.ShapeDtypeStruct((nt, 1, D), F32)],
        scratch_shapes=[pltpu.VMEM((DIN, D), F32)],
        compiler_params=_cp(("arbitrary",)), name="mix_in_bwd")(*dparts, win, nb, h, dh, gain)


def _head_masks():
    lane = lax.broadcasted_iota(jnp.int32, (1, 2 * HD), 1)
    m0 = lane < HD
    return m0, jnp.logical_not(m0)


def _head_rms(xv, m0):
    x2 = xv * xv
    s0 = jnp.sum(jnp.where(m0, x2, 0.0), axis=-1, keepdims=True)
    s1 = jnp.sum(jnp.where(m0, 0.0, x2), axis=-1, keepdims=True)
    return jnp.where(m0, lax.rsqrt(s0 * (1.0 / HD) + EPS), lax.rsqrt(s1 * (1.0 / HD) + EPS))


def _band_mask(first):
    qi = lax.broadcasted_iota(jnp.int32, (BLK, 2 * BLK), 0)
    ci = lax.broadcasted_iota(jnp.int32, (BLK, 2 * BLK), 1)
    band = (ci >= qi) & (ci <= qi + BLK)
    return band & ((ci >= BLK) | jnp.logical_not(first))


def _deinterleave(src, dst, i, d, S, off):
    L = S // d
    for r in range(d):
        if d == 1:
            dst[i, pl.ds(off, S), :] = src[...].astype(dst.dtype)
        else:
            dst[i, pl.ds(off + r * L, L), :] = src[pl.ds(r, L, stride=d), :].astype(dst.dtype)


def _attn_fwd(u, qg2, kg2, B, S):
    T = B * S
    NB = S // BLK
    scale = HD ** -0.5

    def body(q_ref, k_ref, v_ref, qg_ref, kg_ref, o_ref, lse_ref, tmp, qb, kb, vb, os_, ls_, on_, ln_):
        m0, m1 = _head_masks()
        qv = q_ref[...].astype(F32)
        tmp[...] = qv * _head_rms(qv, m0) * (qg_ref[...] * scale)
        for i, d in enumerate(DILS):
            _deinterleave(tmp, qb, i, d, S, 0)
        kv = k_ref[...].astype(F32)
        tmp[...] = kv * _head_rms(kv, m0) * kg_ref[...]
        for i, d in enumerate(DILS):
            _deinterleave(tmp, kb, i, d, S, BLK)
        tmp[...] = v_ref[...].astype(F32)
        for i, d in enumerate(DILS):
            _deinterleave(tmp, vb, i, d, S, BLK)
            kb[i, pl.ds(0, BLK), :] = jnp.zeros((BLK, 2 * HD), BF16)
            vb[i, pl.ds(0, BLK), :] = jnp.zeros((BLK, 2 * HD), BF16)

        for i, d in enumerate(DILS):
            seg = NB // d

            def blk(j, c, i=i, seg=seg):
                q0 = pl.multiple_of(j * BLK, BLK)
                qv_ = qb[i, pl.ds(q0, BLK), :]
                kk = kb[i, pl.ds(q0, 2 * BLK), :]
                vv = vb[i, pl.ds(q0, 2 * BLK), :]
                mask = _band_mask((j % seg) == 0)
                outs, lses = [], []
                for mh in (m0, m1):
                    qh = jnp.where(mh, qv_, jnp.zeros_like(qv_))
                    s = lax.dot_general(qh, kk, NT, preferred_element_type=F32)
                    s = jnp.where(mask, s, -1e30)
                    mx = jnp.max(s, axis=-1, keepdims=True)
                    p = jnp.exp(s - mx)
                    l = jnp.sum(p, axis=-1, keepdims=True)
                    outs.append(jnp.dot((p * (1.0 / l)).astype(BF16), vv, preferred_element_type=F32))
                    lses.append(mx + jnp.log(l))
                os_[i, pl.ds(q0, BLK), :] = jnp.where(m0, outs[0], outs[1])
                ls_[i, pl.ds(q0, BLK), :] = jnp.where(m0, lses[0], lses[1])
                return c

            lax.fori_loop(0, NB, blk, 0)

        for i, d in enumerate(DILS):
            if d == 1:
                continue
            L = S // d
            for r in range(d):
                on_[i - 1, pl.ds(r, L, stride=d), :] = os_[i, pl.ds(r * L, L), :]
                ln_[i - 1, pl.ds(r, L, stride=d), :] = ls_[i, pl.ds(r * L, L), :]

        def comb(c, carry):
            rows = pl.ds(pl.multiple_of(c * 256, 256), 256)
            l0, l1, l2 = ls_[0, rows, :], ln_[0, rows, :], ln_[1, rows, :]
            mx = jnp.maximum(jnp.maximum(l0, l1), l2)
            e0, e1, e2 = jnp.exp(l0 - mx), jnp.exp(l1 - mx), jnp.exp(l2 - mx)
            tot = e0 + e1 + e2
            inv = 1.0 / tot
            o = (e0 * os_[0, rows, :] + e1 * on_[0, rows, :] + e2 * on_[1, rows, :]) * inv
            o_ref[rows, :] = o.astype(BF16)
            lse_ref[rows, :] = mx + jnp.log(tot)
            return carry

        lax.fori_loop(0, S // 256, comb, 0)

    pair = 2 * HD
    blk_spec = lambda off: pl.BlockSpec((S, pair), lambda b, p, off=off: (b, off + p))
    gspec = pl.BlockSpec((1, pair), lambda b, p: (0, 0))
    return pl.pallas_call(
        body, grid=(B, DA // pair),
        in_specs=[blk_spec(0), blk_spec(DA // pair), blk_spec(2 * DA // pair), gspec, gspec],
        out_specs=[blk_spec(0), blk_spec(0)],
        out_shape=[jax.ShapeDtypeStruct((T, DA), BF16), jax.ShapeDtypeStruct((T, DA), F32)],
        scratch_shapes=[pltpu.VMEM((S, pair), F32),
                        pltpu.VMEM((3, S, pair), BF16),
                        pltpu.VMEM((3, BLK + S, pair), BF16),
                        pltpu.VMEM((3, BLK + S, pair), BF16),
                        pltpu.VMEM((3, S, pair), F32),
                        pltpu.VMEM((3, S, pair), F32),
                        pltpu.VMEM((2, S, pair), F32),
                        pltpu.VMEM((2, S, pair), F32)],
        compiler_params=_cp(("parallel", "parallel")), name="attn_fwd")(u, u, u, qg2, kg2)


def _attn_bwd(u, attn, dattn, lse, qg2, kg2, B, S):
    T = B * S
    NB = S // BLK
    scale = HD ** -0.5
    pair = 2 * HD

    def body(q_ref, k_ref, v_ref, o_ref, do_ref, lse_ref, qg_ref, kg_ref,
             dq_ref, dk_ref, dv_ref, dgn_ref,
             tmp, qb, kb, vb, dob, lsb, dlb, dqs, dks, dvs, accq, acck, accv):
        m0, m1 = _head_masks()
        qv = q_ref[...].astype(F32)
        tmp[...] = qv * _head_rms(qv, m0) * (qg_ref[...] * scale)
        for i, d in enumerate(DILS):
            _deinterleave(tmp, qb, i, d, S, 0)
        kv = k_ref[...].astype(F32)
        tmp[...] = kv * _head_rms(kv, m0) * kg_ref[...]
        for i, d in enumerate(DILS):
            _deinterleave(tmp, kb, i, d, S, BLK)
        tmp[...] = v_ref[...].astype(F32)
        for i, d in enumerate(DILS):
            _deinterleave(tmp, vb, i, d, S, BLK)
            kb[i, pl.ds(0, BLK), :] = jnp.zeros((BLK, pair), BF16)
            vb[i, pl.ds(0, BLK), :] = jnp.zeros((BLK, pair), BF16)
        dov = do_ref[...].astype(F32)
        tmp[...] = dov
        for i, d in enumerate(DILS):
            _deinterleave(tmp, dob, i, d, S, 0)
        tmp[...] = lse_ref[...]
        for i, d in enumerate(DILS):
            _deinterleave(tmp, lsb, i, d, S, 0)
        prod = dov * o_ref[...].astype(F32)
        d0 = jnp.sum(jnp.where(m0, prod, 0.0), axis=-1, keepdims=True)
        d1 = jnp.sum(jnp.where(m0, 0.0, prod), axis=-1, keepdims=True)
        tmp[...] = jnp.where(m0, d0, d1)
        for i, d in enumerate(DILS):
            _deinterleave(tmp, dlb, i, d, S, 0)
        dks[...] = jnp.zeros_like(dks)
        dvs[...] = jnp.zeros_like(dvs)

        for i, d in enumerate(DILS):
            seg = NB // d

            def blk(j, c, i=i, seg=seg):
                q0 = pl.multiple_of(j * BLK, BLK)
                qv_ = qb[i, pl.ds(q0, BLK), :]
                kk = kb[i, pl.ds(q0, 2 * BLK), :]
                vv = vb[i, pl.ds(q0, 2 * BLK), :]
                dov_ = dob[i, pl.ds(q0, BLK), :]
                lsv = lsb[i, pl.ds(q0, BLK), :]
                dlv = dlb[i, pl.ds(q0, BLK), :]
                mask = _band_mask((j % seg) == 0)
                dq_acc = jnp.zeros((BLK, pair), F32)
                dk_acc = jnp.zeros((2 * BLK, pair), F32)
                dv_acc = jnp.zeros((2 * BLK, pair), F32)
                for hi, mh in enumerate((m0, m1)):
                    col = slice(hi * HD, hi * HD + 1)
                    qh = jnp.where(mh, qv_, jnp.zeros_like(qv_))
                    doh = jnp.where(mh, dov_, jnp.zeros_like(dov_))
                    s = lax.dot_general(qh, kk, NT, preferred_element_type=F32)
                    p = jnp.where(mask, jnp.exp(s - lsv[:, col]), 0.0)
                    dp = lax.dot_general(doh, vv, NT, preferred_element_type=F32)
                    ds = (p * (dp - dlv[:, col])).astype(BF16)
                    pb = p.astype(BF16)
                    dq_acc = dq_acc + jnp.where(mh, jnp.dot(ds, kk, preferred_element_type=F32), 0.0)
                    dk_acc = dk_acc + lax.dot_general(ds, qh, TN, preferred_element_type=F32)
                    dv_acc = dv_acc + lax.dot_general(pb, doh, TN, preferred_element_type=F32)
                dqs[i, pl.ds(q0, BLK), :] = dq_acc
                dks[i, pl.ds(q0, 2 * BLK), :] += dk_acc
                dvs[i, pl.ds(q0, 2 * BLK), :] += dv_acc
                return c

            lax.fori_loop(0, NB, blk, 0)

        accq[...] = dqs[0]
        acck[...] = dks[0, pl.ds(BLK, S), :]
        accv[...] = dvs[0, pl.ds(BLK, S), :]
        for i, d in enumerate(DILS):
            if d == 1:
                continue
            L = S // d
            for r in range(d):
                rows = pl.ds(r, L, stride=d)
                accq[rows, :] += dqs[i, pl.ds(r * L, L), :]
                acck[rows, :] += dks[i, pl.ds(BLK + r * L, L), :]
                accv[rows, :] += dvs[i, pl.ds(BLK + r * L, L), :]

        def head_mean(xv):
            a0 = jnp.sum(jnp.where(m0, xv, 0.0), axis=-1, keepdims=True)
            a1 = jnp.sum(jnp.where(m0, 0.0, xv), axis=-1, keepdims=True)
            return jnp.where(m0, a0, a1) * (1.0 / HD)

        def norm_bwd(x_ref, dn, gain):
            xv = x_ref[...].astype(F32)
            r = _head_rms(xv, m0)
            xhat = xv * r
            dxhat = dn * gain
            dx = r * (dxhat - xhat * head_mean(dxhat * xhat))
            return dx, jnp.sum(dn * xhat, axis=0, keepdims=True)

        dq, dgq = norm_bwd(q_ref, accq[...], qg_ref[...] * scale)
        dk, dgk = norm_bwd(k_ref, acck[...], kg_ref[...])
        dq_ref[...] = dq.astype(BF16)
        dk_ref[...] = dk.astype(BF16)
        dv_ref[...] = accv[...].astype(BF16)
        dgn_ref[...] = jnp.concatenate([dgq * scale, dgk, jnp.zeros((6, pair), F32)], axis=0)[None]

    blk_spec = lambda off: pl.BlockSpec((S, pair), lambda b, p, off=off: (b, off + p))
    gspec = pl.BlockSpec((1, pair), lambda b, p: (0, 0))
    np_ = DA // pair
    return pl.pallas_call(
        body, grid=(B, np_),
        in_specs=[blk_spec(0), blk_spec(np_), blk_spec(2 * np_), blk_spec(0), blk_spec(0), blk_spec(0),
                  gspec, gspec],
        out_specs=[blk_spec(0), blk_spec(0), blk_spec(0),
                   pl.BlockSpec((1, 8, pair), lambda b, p: (b * np_ + p, 0, 0))],
        out_shape=[jax.ShapeDtypeStruct((T, DA), BF16)] * 3 + [jax.ShapeDtypeStruct((B * np_, 8, pair), F32)],
        scratch_shapes=[pltpu.VMEM((S, pair), F32),
                        pltpu.VMEM((3, S, pair), BF16),
                        pltpu.VMEM((3, BLK + S, pair), BF16),
                        pltpu.VMEM((3, BLK + S, pair), BF16),
                        pltpu.VMEM((3, S, pair), BF16),
                        pltpu.VMEM((3, S, pair), F32),
                        pltpu.VMEM((3, S, pair), F32),
                        pltpu.VMEM((3, S, pair), F32),
                        pltpu.VMEM((3, BLK + S, pair), F32),
                        pltpu.VMEM((3, BLK + S, pair), F32),
                        pltpu.VMEM((S, pair), F32),
                        pltpu.VMEM((S, pair), F32),
                        pltpu.VMEM((S, pair), F32)],
        compiler_params=_cp(("parallel", "parallel")), name="attn_bwd")(u, u, u, attn, dattn, lse, qg2, kg2)


CT = 32
CPAD = 32


def _ln_fwd(y, g, b):
    mu = jnp.mean(y, axis=-1, keepdims=True)
    yc = y - mu
    rstd = lax.rsqrt(jnp.mean(yc * yc, axis=-1, keepdims=True) + EPS)
    xhat = yc * rstd
    return xhat, rstd, xhat * g + b


def _fill_glu(ca_ref, cg_ref, glu, S):
    glu[pl.ds(0, CPAD), :] = jnp.zeros((CPAD, DC), F32)

    def fill(i, c):
        rows = pl.ds(pl.multiple_of(i * 256, 256), 256)
        a = ca_ref[rows, :].astype(F32)
        gt = cg_ref[rows, :].astype(F32)
        glu[pl.ds(pl.multiple_of(CPAD + i * 256, CT), 256), :] = a * _sigmoid(gt)
        return c

    lax.fori_loop(0, S // 256, fill, 0)


def _conv_fwd(u, cw, cb, lg, lb, B, S):
    T = B * S

    def body(ca_ref, cg_ref, w_ref, b_ref, lg_ref, lb_ref, o_ref, y_ref, glu):
        _fill_glu(ca_ref, cg_ref, glu, S)

        def step(i, c):
            t0 = pl.multiple_of(i * CT, CT)
            win = glu[pl.ds(t0, 2 * CT), :]
            acc = jnp.zeros((CT, DC), F32) + b_ref[...]
            for k in range(CK):
                acc = acc + win[k + 2:k + 2 + CT, :] * w_ref[k:k + 1, :]
            y_ref[pl.ds(t0, CT), :] = acc
            _, _, z = _ln_fwd(acc, lg_ref[...], lb_ref[...])
            o_ref[pl.ds(t0, CT), :] = (z * _sigmoid(z)).astype(BF16)
            return c

        lax.fori_loop(0, S // CT, step, 0)

    vec = pl.BlockSpec((1, DC), lambda b: (0, 0))
    return pl.pallas_call(
        body, grid=(B,),
        in_specs=[pl.BlockSpec((S, DC), lambda b: (b, 3)), pl.BlockSpec((S, DC), lambda b: (b, 4)),
                  pl.BlockSpec((CT, DC), lambda b: (0, 0)), vec, vec, vec],
        out_specs=[pl.BlockSpec((S, DC), lambda b: (b, 0))] * 2,
        out_shape=[jax.ShapeDtypeStruct((T, DC), BF16), jax.ShapeDtypeStruct((T, DC), F32)],
        scratch_shapes=[pltpu.VMEM((CPAD + S, DC), F32)],
        compiler_params=_cp(("parallel",)), name="conv_fwd")(u, u, cw, cb, lg, lb)


def _conv_bwd(u, y, dconv, cw, lg, lb, B, S):
    T = B * S

    def body(ca_ref, cg_ref, y_ref, dc_ref, w_ref, lg_ref, lb_ref,
             dca_ref, dcg_ref, dw_ref, ds_ref, glu, dyp, dwacc):
        _fill_glu(ca_ref, cg_ref, glu, S)
        dyp[pl.ds(S, CPAD), :] = jnp.zeros((CPAD, DC), F32)
        lgv, lbv = lg_ref[...], lb_ref[...]

        def sum8(v):
            return v[0:8] + v[8:16] + v[16:24] + v[24:32]

        def p1(i, carry):
            sb, sg, sl = carry
            t0 = pl.multiple_of(i * CT, CT)
            xhat, rstd, z = _ln_fwd(y_ref[pl.ds(t0, CT), :], lgv, lbv)
            sz = _sigmoid(z)
            dz = dc_ref[pl.ds(t0, CT), :].astype(F32) * (sz * (1.0 + z * (1.0 - sz)))
            dxhat = dz * lgv
            dy = rstd * (dxhat - jnp.mean(dxhat, axis=-1, keepdims=True)
                         - xhat * jnp.mean(dxhat * xhat, axis=-1, keepdims=True))
            dyp[pl.ds(t0, CT), :] = dy
            return sb + sum8(dy), sg + sum8(dz * xhat), sl + sum8(dz)

        z8 = jnp.zeros((8, DC), F32)
        sb, sg, sl = lax.fori_loop(0, S // CT, p1, (z8, z8, z8))
        rs = lambda v: jnp.sum(v, axis=0, keepdims=True)
        ds_ref[...] = jnp.concatenate([rs(sb), rs(sg), rs(sl), jnp.zeros((5, DC), F32)], axis=0)[None]

        def p2(i, c):
            t0 = pl.multiple_of(i * CT, CT)
            win = dyp[pl.ds(t0, 2 * CT), :]
            acc = jnp.zeros((CT, DC), F32)
            for k in range(CK):
                acc = acc + win[30 - k:30 - k + CT, :] * w_ref[k:k + 1, :]
            a = ca_ref[pl.ds(t0, CT), :].astype(F32)
            sgt = _sigmoid(cg_ref[pl.ds(t0, CT), :].astype(F32))
            dca_ref[pl.ds(t0, CT), :] = (acc * sgt).astype(BF16)
            dcg_ref[pl.ds(t0, CT), :] = (acc * a * sgt * (1.0 - sgt)).astype(BF16)
            return c

        lax.fori_loop(0, S // CT, p2, 0)

        dwacc[...] = jnp.zeros_like(dwacc)

        def p3(i, c):
            t0 = pl.multiple_of(i * CT, CT)
            win = glu[pl.ds(t0, 2 * CT), :]
            dy = dyp[pl.ds(t0, CT), :]
            for k in range(CK):
                dwacc[k] += sum8(dy * win[k + 2:k + 2 + CT, :])
            return c

        lax.fori_loop(0, S // CT, p3, 0)
        dw_ref[...] = jnp.sum(dwacc[...], axis=1)[None]

    vec = pl.BlockSpec((1, DC), lambda b: (0, 0))
    seq = pl.BlockSpec((S, DC), lambda b: (b, 0))
    return pl.pallas_call(
        body, grid=(B,),
        in_specs=[pl.BlockSpec((S, DC), lambda b: (b, 3)), pl.BlockSpec((S, DC), lambda b: (b, 4)),
                  seq, seq, pl.BlockSpec((CT, DC), lambda b: (0, 0)), vec, vec],
        out_specs=[seq, seq, pl.BlockSpec((1, CT, DC), lambda b: (b, 0, 0)),
                   pl.BlockSpec((1, 8, DC), lambda b: (b, 0, 0))],
        out_shape=[jax.ShapeDtypeStruct((T, DC), BF16)] * 2
                  + [jax.ShapeDtypeStruct((B, CT, DC), F32), jax.ShapeDtypeStruct((B, 8, DC), F32)],
        scratch_shapes=[pltpu.VMEM((CPAD + S, DC), F32), pltpu.VMEM((S + CPAD, DC), F32),
                        pltpu.VMEM((CT, 8, DC), F32)],
        compiler_params=_cp(("parallel",)), name="conv_bwd")(u, u, y, dconv, cw, lg, lb)


def _local_step(x, target, norms, W, B, S):
    qg2 = jnp.concatenate([norms["q_norm"], norms["q_norm"]], axis=1)
    kg2 = jnp.concatenate([norms["k_norm"], norms["k_norm"]], axis=1)
    cw = jnp.concatenate([W["conv_w"], jnp.zeros((1, DC), F32)], axis=0)

    h1, n1, g1, u1 = _ffn_fwd(x, norms["ffn1_norm"], W["wg1"], W["wu1"], W["wd1"], None, "ffn1_fwd")
    u, n2 = _mix_in(h1, norms["mix_norm"], W["win"])
    attn, lse = _attn_fwd(u, qg2, kg2, B, S)
    conv, y = _conv_fwd(u, cw, norms["conv_b"], norms["conv_ln_g"], norms["conv_ln_b"], B, S)
    h2 = _mix_out(h1, attn, conv, W["wout"])
    h3, n3, g2, u2, dout, dyb, sq = _ffn_fwd(h2, norms["ffn2_norm"], W["wg2"], W["wu2"], W["wd2"], target, "ffn2_fwd")
    del h3
    loss = (0.5 / D) * jnp.sum(sq)

    dwg2, dwu2, dwd2, dn3 = _ffn_bwd(dyb, n3, g2, u2, W["wg2"], W["wu2"], W["wd2"], "ffn2_bwd")
    dh2, dgn_ffn2 = _norm_bwd(dn3, h2, dout, norms["ffn2_norm"], "ffn2_norm_bwd")
    dattn, dconv, dwout = _mix_out_bwd(dh2, attn, conv, W["wout"])
    dq, dk, dv, dgn_qk = _attn_bwd(u, attn, dattn, lse, qg2, kg2, B, S)
    dca, dcg, dcw, dcs = _conv_bwd(u, y, dconv, cw, norms["conv_ln_g"], norms["conv_ln_b"], B, S)
    dwin, dh1, dyb1, dgn_mix = _mix_in_bwd((dq, dk, dv, dca, dcg), W["win"], n2, h1, dh2, norms["mix_norm"])
    dwg1, dwu1, dwd1, dn1 = _ffn_bwd(dyb1, n1, g1, u1, W["wg1"], W["wu1"], W["wd1"], "ffn1_bwd")
    gx, dgn_ffn1 = _norm_bwd(dn1, x, dh1, norms["ffn1_norm"], "ffn1_norm_bwd")

    qk = jnp.sum(dgn_qk, axis=0)
    cs = jnp.sum(dcs, axis=0)
    small = {
        "ffn1_norm": jnp.sum(dgn_ffn1, axis=0),
        "mix_norm": jnp.sum(dgn_mix, axis=0),
        "q_norm": qk[0:1, 0:HD] + qk[0:1, HD:2 * HD],
        "k_norm": qk[1:2, 0:HD] + qk[1:2, HD:2 * HD],
        "conv_w": jnp.sum(dcw, axis=0)[0:CK],
        "conv_b": cs[0:1],
        "conv_ln_g": cs[1:2],
        "conv_ln_b": cs[2:3],
        "ffn2_norm": jnp.sum(dgn_ffn2, axis=0),
    }
    big = {"wg1": dwg1, "wu1": dwu1, "wd1": dwd1, "win": dwin, "wout": dwout,
           "wg2": dwg2, "wu2": dwu2, "wd2": dwd2}
    return loss, gx, big, small


HBM = pl.BlockSpec(memory_space=pltpu.HBM)
VMEM = pl.BlockSpec(memory_space=pltpu.VMEM)


def _place():
    return lax.axis_index("x"), lax.axis_index("y"), lax.axis_index("c")


def _gather_weights(shards):
    nt = len(shards)
    tshape = [((s.shape[1], s.shape[0]) if tr else s.shape) for s, tr, _ in shards]

    def body(*refs):
        ins, outs = refs[:nt], refs[nt:2 * nt]
        stages = refs[2 * nt:3 * nt]
        send_sems, recv_sems, local_sems = refs[3 * nt:]
        x, y, c = _place()
        me, sibling = (x, y, c), (x, y, 1 - c)
        chips = [(1 - x, y), (x, 1 - y), (1 - x, 1 - y)]

        def rows(t, px, py, pc):
            r = tshape[t][0]
            return outs[t].at[pl.ds((4 * px + 2 * py + pc) * r, r), :]

        def copy(t, k, block, to, src=None):
            return pltpu.make_async_remote_copy(
                src_ref=rows(t, *block) if src is None else src, dst_ref=rows(t, *block),
                send_sem=send_sems.at[t, k], recv_sem=recv_sems.at[t, k],
                device_id=to, device_id_type=MESH)

        first, mine = [], []
        for t, (_, tr, dt) in enumerate(shards):
            v = ins[t][...]
            stages[t][...] = (v.T if tr else v).astype(dt)
            cp = pltpu.make_async_copy(stages[t], rows(t, *me), local_sems.at[t])
            cp.start()
            mine.append(cp)
            first.append(copy(t, 0, me, sibling, src=stages[t]))
            first += [copy(t, 1 + j, me, (*chip, c), src=stages[t]) for j, chip in enumerate(chips)]
        for cp in first:
            cp.start()
        passed = []
        for j, chip in enumerate(chips):
            for t in range(nt):
                copy(t, 1 + j, (*chip, c), me).wait_recv()
                cp = copy(t, 4 + j, (*chip, c), sibling)
                cp.start()
                passed.append(cp)
        for t in range(nt):
            copy(t, 0, sibling, me).wait_recv()
            for j, chip in enumerate(chips):
                copy(t, 4 + j, (*chip, 1 - c), me).wait_recv()
        for cp in first + passed:
            cp.wait_send()
        for cp in mine:
            cp.wait()

    return pl.pallas_call(
        body, in_specs=[VMEM] * nt, out_specs=[HBM] * nt,
        out_shape=[jax.ShapeDtypeStruct((NDEV * r, n), sh[2]) for (r, n), sh in zip(tshape, shards)],
        scratch_shapes=[pltpu.VMEM(s, sh[2]) for s, sh in zip(tshape, shards)]
                       + [pltpu.SemaphoreType.DMA((nt, 7)), pltpu.SemaphoreType.DMA((nt, 7)),
                          pltpu.SemaphoreType.DMA((nt,))],
        compiler_params=pltpu.CompilerParams(vmem_limit_bytes=VMEM_LIMIT),
        name="gather_weights")(*[s for s, _, _ in shards])


def _sibling_exchange(grads):
    nt = len(grads)
    g4 = [g.reshape(4, 2, g.shape[0] // NDEV, g.shape[1]) for g in grads]

    def body(*refs):
        ins, outs = refs[:nt], refs[nt:2 * nt]
        send_sems, recv_sems = refs[2 * nt:]
        x, y, c = _place()
        cps = []
        for t in range(nt):
            cp = pltpu.make_async_remote_copy(
                src_ref=ins[t].at[:, 1 - c], dst_ref=outs[t],
                send_sem=send_sems.at[t], recv_sem=recv_sems.at[t],
                device_id=(x, y, 1 - c), device_id_type=MESH)
            cp.start()
            cps.append(cp)
        for cp in cps:
            cp.wait()

    return pl.pallas_call(
        body, in_specs=[HBM] * nt, out_specs=[HBM] * nt,
        out_shape=[jax.ShapeDtypeStruct((4,) + g.shape[2:], BF16) for g in g4],
        scratch_shapes=[pltpu.SemaphoreType.DMA((nt,)), pltpu.SemaphoreType.DMA((nt,))],
        name="sibling_exchange")(*g4), g4


def _chip_partial(g4, recv, cidx, name):
    _, _, rows, n = g4.shape

    def body(c_ref, a_ref, b_ref, o_ref):
        o_ref[...] = (a_ref[...].astype(F32) + b_ref[...].astype(F32)).astype(BF16)

    return pl.pallas_call(
        body,
        grid_spec=pltpu.PrefetchScalarGridSpec(
            num_scalar_prefetch=1, grid=(4,),
            in_specs=[pl.BlockSpec((None, None, rows, n), lambda q, c_ref: (q, c_ref[0], 0, 0)),
                      pl.BlockSpec((None, rows, n), lambda q, c_ref: (q, 0, 0))],
            out_specs=pl.BlockSpec((None, rows, n), lambda q, c_ref: (q, 0, 0))),
        out_shape=jax.ShapeDtypeStruct((4, rows, n), BF16),
        compiler_params=_cp(("parallel",)), name=name)(cidx, g4, recv)


def _chip_exchange(parts):
    nt = len(parts)

    def body(*refs):
        ins, outs = refs[:nt], refs[nt:2 * nt]
        send_sems, recv_sems, local_sems = refs[2 * nt:]
        x, y, c = _place()
        qme = 2 * x + y
        chips = [(1 - x, y), (x, 1 - y), (1 - x, 1 - y)]
        cps, mine = [], []
        for t in range(nt):
            cp = pltpu.make_async_copy(ins[t].at[qme], outs[t].at[qme], local_sems.at[t])
            cp.start()
            mine.append(cp)
            for k, (cx, cy) in enumerate(chips):
                cp = pltpu.make_async_remote_copy(
                    src_ref=ins[t].at[2 * cx + cy], dst_ref=outs[t].at[qme],
                    send_sem=send_sems.at[t, k], recv_sem=recv_sems.at[t, k],
                    device_id=(cx, cy, c), device_id_type=MESH)
                cp.start()
                cps.append(cp)
        for cp in cps:
            cp.wait()
        for cp in mine:
            cp.wait()

    return pl.pallas_call(
        body, in_specs=[HBM] * nt, out_specs=[HBM] * nt,
        out_shape=[jax.ShapeDtypeStruct(p.shape, BF16) for p in parts],
        scratch_shapes=[pltpu.SemaphoreType.DMA((nt, 3)), pltpu.SemaphoreType.DMA((nt, 3)),
                        pltpu.SemaphoreType.DMA((nt,))],
        name="chip_exchange")(*parts)


def _adamw_math(w, g, m, v):
    m = B1 * m + (1.0 - B1) * g
    v = B2 * v + (1.0 - B2) * (g * g)
    m_hat = m / (1.0 - B1 ** STEP)
    v_hat = v / (1.0 - B2 ** STEP)
    delta = -LR * (m_hat / (jnp.sqrt(v_hat) + AEPS) + WD * w)
    return delta, m, v


def _adamw_big(recv, w, m, v, tr, name):
    def body(r_ref, w_ref, m_ref, v_ref, g_ref, d_ref, mo_ref, vo_ref):
        gs = r_ref[0].astype(F32)
        for q in range(1, 4):
            gs = gs + r_ref[q].astype(F32)
        g = gs.T if tr else gs
        d, mn, vn = _adamw_math(w_ref[...], g, m_ref[...], v_ref[...])
        g_ref[...] = g
        d_ref[...] = d
        mo_ref[...] = mn
        vo_ref[...] = vn

    return pl.pallas_call(
        body, out_shape=[jax.ShapeDtypeStruct(w.shape, F32)] * 4,
        compiler_params=pltpu.CompilerParams(vmem_limit_bytes=VMEM_LIMIT), name=name)(recv, w, m, v)


SMALL_ROWS = 40
CW_ROWS = 16
SMALL_TOTAL = SMALL_ROWS + NDEV * CW_ROWS


def _small_step(gvec, wvec, mvec, vvec):
    nr = SMALL_ROWS + CW_ROWS

    def body(g_ref, w_ref, m_ref, v_ref, go_ref, d_ref, mo_ref, vo_ref, slots, send_sems, recv_sems):
        x, y, c = _place()
        me = 4 * x + 2 * y + c
        slots[me] = g_ref[...]
        cps = []
        for k in range(1, NDEV):
            fx, fy, fc = (k >> 2) & 1, (k >> 1) & 1, k & 1
            peer = (x ^ fx, y ^ fy, c ^ fc)
            cp = pltpu.make_async_remote_copy(
                src_ref=g_ref, dst_ref=slots.at[me],
                send_sem=send_sems.at[k - 1], recv_sem=recv_sems.at[k - 1],
                device_id=peer, device_id_type=MESH)
            cp.start()
            cps.append(cp)
        for cp in cps:
            cp.wait()
        tot = slots[0]
        for j in range(1, NDEV):
            tot = tot + slots[j]
        slots[0] = tot
        g = jnp.concatenate(
            [tot[0:SMALL_ROWS], slots[0, pl.ds(pl.multiple_of(SMALL_ROWS + me * CW_ROWS, 8), CW_ROWS), :]], axis=0)
        d, mn, vn = _adamw_math(w_ref[...], g, m_ref[...], v_ref[...])
        go_ref[...] = g
        d_ref[...] = d
        mo_ref[...] = mn
        vo_ref[...] = vn

    return pl.pallas_call(
        body, in_specs=[VMEM] * 4, out_specs=[VMEM] * 4,
        out_shape=[jax.ShapeDtypeStruct((nr, 128), F32)] * 4,
        scratch_shapes=[pltpu.VMEM((NDEV, SMALL_TOTAL, 128), F32),
                        pltpu.SemaphoreType.DMA((NDEV - 1,)), pltpu.SemaphoreType.DMA((NDEV - 1,))],
        name="small_step")(gvec, wvec, mvec, vvec)


SMALL_NAMES = ("ffn1_norm", "mix_norm", "ffn2_norm", "conv_b", "conv_ln_g", "conv_ln_b", "q_norm", "k_norm")


def _pack_small(vals):
    parts = []
    for n in SMALL_NAMES:
        a = vals[n].reshape(-1)
        if a.shape[0] < 128:
            a = jnp.concatenate([a, jnp.zeros((128 - a.shape[0],), F32)])
        parts.append(a.reshape(-1, 128))
    used = sum(p.shape[0] for p in parts)
    parts.append(jnp.zeros((SMALL_ROWS - used, 128), F32))
    return jnp.concatenate(parts, axis=0)


def _unpack_small(vec, shapes):
    out, r = {}, 0
    for n in SMALL_NAMES:
        size = shapes[n][1]
        nr = max(size // 128, 1)
        out[n] = vec[r:r + nr].reshape(-1)[:size].reshape(1, size)
        r += nr
    return out


def _pack_cw(a):
    flat = a.reshape(-1)
    return jnp.concatenate([flat, jnp.zeros((CW_ROWS * 128 - flat.shape[0],), F32)]).reshape(CW_ROWS, 128)


def _unpack_cw(v):
    return v.reshape(-1)[:CK * HD].reshape(1, CK, HD)


def kernel(x, ffn1_norm, ffn1_w_gate, ffn1_w_up, ffn1_w_down, mix_norm, w_in, q_norm, k_norm, conv_w, conv_b, conv_ln_g, conv_ln_b, w_out, ffn2_norm, ffn2_w_gate, ffn2_w_up, ffn2_w_down, loss_target, m_ffn1_norm, m_ffn1_w_gate, m_ffn1_w_up, m_ffn1_w_down, m_mix_norm, m_w_in, m_q_norm, m_k_norm, m_conv_w, m_conv_b, m_conv_ln_g, m_conv_ln_b, m_w_out, m_ffn2_norm, m_ffn2_w_gate, m_ffn2_w_up, m_ffn2_w_down, v_ffn1_norm, v_ffn1_w_gate, v_ffn1_w_up, v_ffn1_w_down, v_mix_norm, v_w_in, v_q_norm, v_k_norm, v_conv_w, v_conv_b, v_conv_ln_g, v_conv_ln_b, v_w_out, v_ffn2_norm, v_ffn2_w_gate, v_ffn2_w_up, v_ffn2_w_down):
    P = dict(ffn1_norm=ffn1_norm, ffn1_w_gate=ffn1_w_gate, ffn1_w_up=ffn1_w_up, ffn1_w_down=ffn1_w_down,
             mix_norm=mix_norm, w_in=w_in, q_norm=q_norm, k_norm=k_norm, conv_w=conv_w, conv_b=conv_b,
             conv_ln_g=conv_ln_g, conv_ln_b=conv_ln_b, w_out=w_out, ffn2_norm=ffn2_norm,
             ffn2_w_gate=ffn2_w_gate, ffn2_w_up=ffn2_w_up, ffn2_w_down=ffn2_w_down)
    M = dict(ffn1_norm=m_ffn1_norm, ffn1_w_gate=m_ffn1_w_gate, ffn1_w_up=m_ffn1_w_up, ffn1_w_down=m_ffn1_w_down,
             mix_norm=m_mix_norm, w_in=m_w_in, q_norm=m_q_norm, k_norm=m_k_norm, conv_w=m_conv_w, conv_b=m_conv_b,
             conv_ln_g=m_conv_ln_g, conv_ln_b=m_conv_ln_b, w_out=m_w_out, ffn2_norm=m_ffn2_norm,
             ffn2_w_gate=m_ffn2_w_gate, ffn2_w_up=m_ffn2_w_up, ffn2_w_down=m_ffn2_w_down)
    V = dict(ffn1_norm=v_ffn1_norm, ffn1_w_gate=v_ffn1_w_gate, ffn1_w_up=v_ffn1_w_up, ffn1_w_down=v_ffn1_w_down,
             mix_norm=v_mix_norm, w_in=v_w_in, q_norm=v_q_norm, k_norm=v_k_norm, conv_w=v_conv_w, conv_b=v_conv_b,
             conv_ln_g=v_conv_ln_g, conv_ln_b=v_conv_ln_b, w_out=v_w_out, ffn2_norm=v_ffn2_norm,
             ffn2_w_gate=v_ffn2_w_gate, ffn2_w_up=v_ffn2_w_up, ffn2_w_down=v_ffn2_w_down)
    order = ["ffn1_norm", "ffn1_w_gate", "ffn1_w_up", "ffn1_w_down", "mix_norm", "w_in", "q_norm", "k_norm",
             "conv_w", "conv_b", "conv_ln_g", "conv_ln_b", "w_out", "ffn2_norm", "ffn2_w_gate", "ffn2_w_up",
             "ffn2_w_down"]
    B, S, _ = x.shape
    T = B * S
    cidx = lax.axis_index("c").astype(jnp.int32).reshape(1)

    bigs = [("wg1", "ffn1_w_gate", True), ("wu1", "ffn1_w_up", True), ("wd1", "ffn1_w_down", False),
            ("win", "w_in", True), ("wout", "w_out", False),
            ("wg2", "ffn2_w_gate", True), ("wu2", "ffn2_w_up", True), ("wd2", "ffn2_w_down", False)]
    cw_pad = jnp.zeros((32, 128), F32).at[0:CK, 0:HD].set(conv_w[0])
    gathered = _gather_weights([(P[pn][0], tr, BF16) for _, pn, tr in bigs] + [(cw_pad, False, F32)])
    W = {ln: gathered[i] for i, (ln, _, _) in enumerate(bigs)}
    cwg = gathered[-1].reshape(NDEV, 32, 128)[:, 0:CK, 0:HD]
    W["conv_w"] = jnp.transpose(cwg, (1, 0, 2)).reshape(CK, DC)
    norms = {n: P[n] for n in SMALL_NAMES}

    loss, gx, big, small = _local_step(x.reshape(T, D), loss_target.reshape(T, D), norms, W, B, S)
    loss = lax.psum(loss, ("x", "y", "c"))

    names = [ln for ln, _, _ in bigs]
    recv_a, g4 = _sibling_exchange([big[n] for n in names])
    parts = [_chip_partial(g4[i], recv_a[i], cidx, "chip_partial_" + n) for i, n in enumerate(names)]
    recv_b = _chip_exchange(parts)
    G, Dl, Mn, Vn = {}, {}, {}, {}
    for i, (ln, pn, tr) in enumerate(bigs):
        g, d, mn, vn = _adamw_big(recv_b[i], P[pn][0], M[pn][0], V[pn][0], tr, "adamw_" + ln)
        G[pn], Dl[pn], Mn[pn], Vn[pn] = g[None], d[None], mn[None], vn[None]

    dcw = small["conv_w"].reshape(CK, NDEV, HD).transpose(1, 0, 2)
    gvec = jnp.concatenate([_pack_small(small)] + [_pack_cw(dcw[j]) for j in range(NDEV)], axis=0)
    pack = lambda dct: jnp.concatenate([_pack_small({n: dct[n] for n in SMALL_NAMES}), _pack_cw(dct["conv_w"][0])], axis=0)
    go, do, mo, vo = _small_step(gvec, pack(P), pack(M), pack(V))
    shapes = {n: P[n].shape for n in SMALL_NAMES}
    for dst, vec in ((G, go), (Dl, do), (Mn, mo), (Vn, vo)):
        dst.update(_unpack_small(vec[0:SMALL_ROWS], shapes))
        dst["conv_w"] = _unpack_cw(vec[SMALL_ROWS:])

    return (loss, gx.reshape(B, S, D), *[G[n] for n in order], *[Dl[n] for n in order],
            *[Mn[n] for n in order], *[Vn[n] for n in order])
```

```python
import functools

import jax
import jax.numpy as jnp
from jax import lax
from jax.experimental import pallas as pl
from jax.experimental.pallas import tpu as pltpu

F32 = jnp.float32
BF16 = jnp.bfloat16

D = 1024
FF = 2816
HD = 64
DA = 512
DC = 512
DIN = 2560
CK = 31
BLK = 128
DILS = (1, 4, 16)
EPS = 1e-6
NDEV = 8
MESH = pl.DeviceIdType.MESH

LR, B1, B2, AEPS, WD, STEP = 0.001, 0.9, 0.999, 1e-08, 0.01, 10

NT = (((1,), (1,)), ((), ()))
TN = (((0,), (0,)), ((), ()))

VMEM_LIMIT = 56 * 1024 * 1024


def _cp(sem=None):
    return pltpu.CompilerParams(dimension_semantics=sem, vmem_limit_bytes=VMEM_LIMIT)


def _sigmoid(x):
    return 1.0 / (1.0 + jnp.exp(-x))


def _ffn_fwd(x, gain, wg, wu, wd, target, name):
    T = x.shape[0]
    tm, tf = 1024, 256
    nt, nf = T // tm, FF // tf
    with_loss = target is not None

    def body(*refs):
        if with_loss:
            (x_ref, gain_ref, wg_ref, wu_ref, wd_ref, t_ref,
             h_ref, n_ref, g_ref, u_ref, dout_ref, dyb_ref, sq_ref, nb_scr, acc_scr) = refs
        else:
            (x_ref, gain_ref, wg_ref, wu_ref, wd_ref,
             h_ref, n_ref, g_ref, u_ref, nb_scr, acc_scr) = refs
        f = pl.program_id(1)

        @pl.when(f == 0)
        def _():
            xv = x_ref[...]
            r = lax.rsqrt(jnp.mean(xv * xv, axis=-1, keepdims=True) + EPS)
            nb = (xv * r * gain_ref[...]).astype(BF16)
            nb_scr[...] = nb
            n_ref[...] = nb
            acc_scr[...] = jnp.zeros_like(acc_scr)

        nb = nb_scr[...]
        g = lax.dot_general(nb, wg_ref[...], NT, preferred_element_type=F32)
        u = lax.dot_general(nb, wu_ref[...], NT, preferred_element_type=F32)
        a = g * _sigmoid(g) * u
        g_ref[...] = g.astype(BF16)
        u_ref[...] = u.astype(BF16)
        acc_scr[...] += jnp.dot(a.astype(BF16), wd_ref[...], preferred_element_type=F32)

        @pl.when(f == nf - 1)
        def _():
            h = x_ref[...] + 0.5 * acc_scr[...]
            h_ref[...] = h
            if with_loss:
                e = h - t_ref[...]
                dout = e * (1.0 / D)
                dout_ref[...] = dout
                dyb_ref[...] = (0.5 * dout).astype(BF16)
                sq_ref[...] = jnp.sum(e * e, axis=0, keepdims=True)[None]

    row = pl.BlockSpec((tm, D), lambda t, f: (t, 0))
    wspec = pl.BlockSpec((tf, D), lambda t, f: (f, 0))
    gspec = pl.BlockSpec((tm, tf), lambda t, f: (t, f))
    in_specs = [row, pl.BlockSpec((1, D), lambda t, f: (0, 0)), wspec, wspec, wspec]
    out_shape = [jax.ShapeDtypeStruct((T, D), F32), jax.ShapeDtypeStruct((T, D), BF16),
                 jax.ShapeDtypeStruct((T, FF), BF16), jax.ShapeDtypeStruct((T, FF), BF16)]
    out_specs = [row, row, gspec, gspec]
    args = [x, gain, wg, wu, wd]
    if with_loss:
        in_specs.append(row)
        args.append(target)
        out_shape += [jax.ShapeDtypeStruct((T, D), F32), jax.ShapeDtypeStruct((T, D), BF16),
                      jax.ShapeDtypeStruct((nt, 1, D), F32)]
        out_specs += [row, row, pl.BlockSpec((1, 1, D), lambda t, f: (t, 0, 0))]
    return pl.pallas_call(
        body, grid=(nt, nf), in_specs=in_specs, out_specs=out_specs, out_shape=out_shape,
        scratch_shapes=[pltpu.VMEM((tm, D), BF16), pltpu.VMEM((tm, D), F32)],
        compiler_params=_cp(("parallel", "arbitrary")), name=name)(*args)


def _ffn_bwd(dyb, nb, g, u, wg, wu, wd, name):
    T = dyb.shape[0]
    tm, tf = 512, 256
    nt, nf = T // tm, FF // tf

    def body(dy_ref, n_ref, g_ref, u_ref, wg_ref, wu_ref, wd_ref,
             dwg_ref, dwu_ref, dwd_ref, dn_ref, ag_scr, au_scr, ad_scr):
        f, t = pl.program_id(0), pl.program_id(1)

        @pl.when(t == 0)
        def _():
            ag_scr[...] = jnp.zeros_like(ag_scr)
            au_scr[...] = jnp.zeros_like(au_scr)
            ad_scr[...] = jnp.zeros_like(ad_scr)

        dy = dy_ref[...]
        n = n_ref[...]
        gv = g_ref[...].astype(F32)
        uv = u_ref[...].astype(F32)
        da = lax.dot_general(dy, wd_ref[...], NT, preferred_element_type=F32)
        sg = _sigmoid(gv)
        silu = gv * sg
        ab = (silu * uv).astype(BF16)
        dgb = (da * uv * (sg * (1.0 + gv * (1.0 - sg)))).astype(BF16)
        dub = (da * silu).astype(BF16)
        ad_scr[...] += lax.dot_general(ab, dy, TN, preferred_element_type=F32)
        ag_scr[...] += lax.dot_general(dgb, n, TN, preferred_element_type=F32)
        au_scr[...] += lax.dot_general(dub, n, TN, preferred_element_type=F32)
        dn = (jnp.dot(dgb, wg_ref[...], preferred_element_type=F32)
              + jnp.dot(dub, wu_ref[...], preferred_element_type=F32))
        rows = pl.ds(pl.multiple_of(t * tm, tm), tm)

        @pl.when(f == 0)
        def _():
            dn_ref[rows, :] = dn

        @pl.when(f > 0)
        def _():
            dn_ref[rows, :] += dn

        @pl.when(t == nt - 1)
        def _():
            dwg_ref[...] = ag_scr[...].astype(BF16)
            dwu_ref[...] = au_scr[...].astype(BF16)
            dwd_ref[...] = ad_scr[...].astype(BF16)

    row = pl.BlockSpec((tm, D), lambda f, t: (t, 0))
    gspec = pl.BlockSpec((tm, tf), lambda f, t: (t, f))
    wspec = pl.BlockSpec((tf, D), lambda f, t: (f, 0))
    return pl.pallas_call(
        body, grid=(nf, nt),
        in_specs=[row, row, gspec, gspec, wspec, wspec, wspec],
        out_specs=[wspec, wspec, wspec, pl.BlockSpec((T, D), lambda f, t: (0, 0))],
        out_shape=[jax.ShapeDtypeStruct((FF, D), BF16)] * 3 + [jax.ShapeDtypeStruct((T, D), F32)],
        scratch_shapes=[pltpu.VMEM((tf, D), F32)] * 3,
        compiler_params=_cp(("arbitrary", "arbitrary")), name=name)(dyb, nb, g, u, wg, wu, wd)


def _rms_bwd_rows(dn, xv, gain):
    r = lax.rsqrt(jnp.mean(xv * xv, axis=-1, keepdims=True) + EPS)
    xhat = xv * r
    dxhat = dn * gain
    dx = r * (dxhat - xhat * jnp.mean(dxhat * xhat, axis=-1, keepdims=True))
    return dx, jnp.sum(dn * xhat, axis=0, keepdims=True)


def _norm_bwd(dn, x, dout, gain, name):
    T = x.shape[0]
    tm = 512
    nt = T // tm

    def body(dn_ref, x_ref, dout_ref, gain_ref, dx_ref, dg_ref):
        dx, dgain = _rms_bwd_rows(dn_ref[...], x_ref[...], gain_ref[...])
        dx_ref[...] = dout_ref[...] + dx
        dg_ref[...] = dgain[None]

    row = pl.BlockSpec((tm, D), lambda t: (t, 0))
    return pl.pallas_call(
        body, grid=(nt,), in_specs=[row, row, row, pl.BlockSpec((1, D), lambda t: (0, 0))],
        out_specs=[row, pl.BlockSpec((1, 1, D), lambda t: (t, 0, 0))],
        out_shape=[jax.ShapeDtypeStruct((T, D), F32), jax.ShapeDtypeStruct((nt, 1, D), F32)],
        compiler_params=_cp(("parallel",)), name=name)(dn, x, dout, gain)


def _mix_in(h, gain, win):
    T = h.shape[0]
    tm = 512

    def body(h_ref, gain_ref, w_ref, u_ref, n_ref):
        xv = h_ref[...]
        r = lax.rsqrt(jnp.mean(xv * xv, axis=-1, keepdims=True) + EPS)
        nb = (xv * r * gain_ref[...]).astype(BF16)
        n_ref[...] = nb
        u_ref[...] = lax.dot_general(nb, w_ref[...], NT, preferred_element_type=F32).astype(BF16)

    row = pl.BlockSpec((tm, D), lambda t: (t, 0))
    return pl.pallas_call(
        body, grid=(T // tm,),
        in_specs=[row, pl.BlockSpec((1, D), lambda t: (0, 0)), pl.BlockSpec((DIN, D), lambda t: (0, 0))],
        out_specs=[pl.BlockSpec((tm, DIN), lambda t: (t, 0)), row],
        out_shape=[jax.ShapeDtypeStruct((T, DIN), BF16), jax.ShapeDtypeStruct((T, D), BF16)],
        compiler_params=_cp(("parallel",)), name="mix_in")(h, gain, win)


def _mix_out(h, attn, conv, wout):
    T = h.shape[0]
    tm = 512

    def body(h_ref, a_ref, c_ref, w_ref, o_ref):
        o_ref[...] = (h_ref[...]
                      + jnp.dot(a_ref[...], w_ref[0:DA, :], preferred_element_type=F32)
                      + jnp.dot(c_ref[...], w_ref[DA:D, :], preferred_element_type=F32))

    row = pl.BlockSpec((tm, D), lambda t: (t, 0))
    half = pl.BlockSpec((tm, DA), lambda t: (t, 0))
    return pl.pallas_call(
        body, grid=(T // tm,),
        in_specs=[row, half, half, pl.BlockSpec((D, D), lambda t: (0, 0))],
        out_specs=row, out_shape=jax.ShapeDtypeStruct((T, D), F32),
        compiler_params=_cp(("parallel",)), name="mix_out")(h, attn, conv, wout)


def _mix_out_bwd(dh, attn, conv, wout):
    T = dh.shape[0]
    tm = 512
    nt = T // tm

    def body(dh_ref, a_ref, c_ref, w_ref, da_ref, dc_ref, dw_ref, acc_scr):
        t = pl.program_id(0)

        @pl.when(t == 0)
        def _():
            acc_scr[...] = jnp.zeros_like(acc_scr)

        dhb = dh_ref[...].astype(BF16)
        dmix = lax.dot_general(dhb, w_ref[...], NT, preferred_element_type=F32)
        da_ref[...] = dmix[:, 0:DA].astype(BF16)
        dc_ref[...] = dmix[:, DA:D].astype(BF16)
        acc_scr[0:DA, :] += lax.dot_general(a_ref[...], dhb, TN, preferred_element_type=F32)
        acc_scr[DA:D, :] += lax.dot_general(c_ref[...], dhb, TN, preferred_element_type=F32)

        @pl.when(t == nt - 1)
        def _():
            dw_ref[...] = acc_scr[...].astype(BF16)

    row = pl.BlockSpec((tm, D), lambda t: (t, 0))
    half = pl.BlockSpec((tm, DA), lambda t: (t, 0))
    full = pl.BlockSpec((D, D), lambda t: (0, 0))
    return pl.pallas_call(
        body, grid=(nt,), in_specs=[row, half, half, full], out_specs=[half, half, full],
        out_shape=[jax.ShapeDtypeStruct((T, DA), BF16)] * 2 + [jax.ShapeDtypeStruct((D, D), BF16)],
        scratch_shapes=[pltpu.VMEM((D, D), F32)],
        compiler_params=_cp(("arbitrary",)), name="mix_out_bwd")(dh, attn, conv, wout)


def _mix_in_bwd(dparts, win, nb, h, dh, gain):
    T = h.shape[0]
    tm = 512
    nt = T // tm

    def body(d0, d1, d2, d3, d4, w_ref, n_ref, h_ref, dh_ref, gain_ref,
             dw_ref, dx_ref, dyb_ref, dg_ref, acc_scr):
        t = pl.program_id(0)

        @pl.when(t == 0)
        def _():
            acc_scr[...] = jnp.zeros_like(acc_scr)

        n = n_ref[...]
        dn = jnp.zeros((tm, D), F32)
        for i, d_ref in enumerate((d0, d1, d2, d3, d4)):
            dv = d_ref[...]
            dn = dn + jnp.dot(dv, w_ref[i * DA:(i + 1) * DA, :], preferred_element_type=F32)
            acc_scr[i * DA:(i + 1) * DA, :] += lax.dot_general(dv, n, TN, preferred_element_type=F32)
        dx, dgain = _rms_bwd_rows(dn, h_ref[...], gain_ref[...])
        tot = dh_ref[...] + dx
        dx_ref[...] = tot
        dyb_ref[...] = (0.5 * tot).astype(BF16)
        dg_ref[...] = dgain[None]

        @pl.when(t == nt - 1)
        def _():
            dw_ref[...] = acc_scr[...].astype(BF16)

    row = pl.BlockSpec((tm, D), lambda t: (t, 0))
    half = pl.BlockSpec((tm, DA), lambda t: (t, 0))
    full = pl.BlockSpec((DIN, D), lambda t: (0, 0))
    return pl.pallas_call(
        body, grid=(nt,),
        in_specs=[half] * 5 + [full, row, row, row, pl.BlockSpec((1, D), lambda t: (0, 0))],
        out_specs=[full, row, row, pl.BlockSpec((1, 1, D), lambda t: (t, 0, 0))],
        out_shape=[jax.ShapeDtypeStruct((DIN, D), BF16), jax.ShapeDtypeStruct((T, D), F32),
                   jax.ShapeDtypeStruct((T, D), BF16), jax.ShapeDtypeStruct((nt, 1, D), F32)],
        scratch_shapes=[pltpu.VMEM((DIN, D), F32)],
        compiler_params=_cp(("arbitrary",)), name="mix_in_bwd")(*dparts, win, nb, h, dh, gain)


def _head_masks():
    lane = lax.broadcasted_iota(jnp.int32, (1, 2 * HD), 1)
    m0 = lane < HD
    return m0, jnp.logical_not(m0)


def _head_rms(xv, m0):
    x2 = xv * xv
    s0 = jnp.sum(jnp.where(m0, x2, 0.0), axis=-1, keepdims=True)
    s1 = jnp.sum(jnp.where(m0, 0.0, x2), axis=-1, keepdims=True)
    return jnp.where(m0, lax.rsqrt(s0 * (1.0 / HD) + EPS), lax.rsqrt(s1 * (1.0 / HD) + EPS))


def _band_mask(first):
    qi = lax.broadcasted_iota(jnp.int32, (BLK, 2 * BLK), 0)
    ci = lax.broadcasted_iota(jnp.int32, (BLK, 2 * BLK), 1)
    band = (ci >= qi) & (ci <= qi + BLK)
    return band & ((ci >= BLK) | jnp.logical_not(first))


def _deinterleave(src, dst, i, d, S, off):
    L = S // d
    for r in range(d):
        if d == 1:
            dst[i, pl.ds(off, S), :] = src[...].astype(dst.dtype)
        else:
            dst[i, pl.ds(off + r * L, L), :] = src[pl.ds(r, L, stride=d), :].astype(dst.dtype)


def _attn_fwd(u, qg2, kg2, B, S):
    T = B * S
    NB = S // BLK
    scale = HD ** -0.5

    def body(q_ref, k_ref, v_ref, qg_ref, kg_ref, o_ref, lse_ref, tmp, qb, kb, vb, os_, ls_, on_, ln_):
        m0, m1 = _head_masks()
        qv = q_ref[...].astype(F32)
        tmp[...] = qv * _head_rms(qv, m0) * (qg_ref[...] * scale)
        for i, d in enumerate(DILS):
            _deinterleave(tmp, qb, i, d, S, 0)
        kv = k_ref[...].astype(F32)
        tmp[...] = kv * _head_rms(kv, m0) * kg_ref[...]
        for i, d in enumerate(DILS):
            _deinterleave(tmp, kb, i, d, S, BLK)
        tmp[...] = v_ref[...].astype(F32)
        for i, d in enumerate(DILS):
            _deinterleave(tmp, vb, i, d, S, BLK)
            kb[i, pl.ds(0, BLK), :] = jnp.zeros((BLK, 2 * HD), BF16)
            vb[i, pl.ds(0, BLK), :] = jnp.zeros((BLK, 2 * HD), BF16)

        for i, d in enumerate(DILS):
            seg = NB // d

            def blk(j, c, i=i, seg=seg):
                q0 = pl.multiple_of(j * BLK, BLK)
                qv_ = qb[i, pl.ds(q0, BLK), :]
                kk = kb[i, pl.ds(q0, 2 * BLK), :]
                vv = vb[i, pl.ds(q0, 2 * BLK), :]
                mask = _band_mask((j % seg) == 0)
                outs, lses = [], []
                for mh in (m0, m1):
                    qh = jnp.where(mh, qv_, jnp.zeros_like(qv_))
                    s = lax.dot_general(qh, kk, NT, preferred_element_type=F32)
                    s = jnp.where(mask, s, -1e30)
                    mx = jnp.max(s, axis=-1, keepdims=True)
                    p = jnp.exp(s - mx)
                    l = jnp.sum(p, axis=-1, keepdims=True)
                    outs.append(jnp.dot((p * (1.0 / l)).astype(BF16), vv, preferred_element_type=F32))
                    lses.append(mx + jnp.log(l))
                os_[i, pl.ds(q0, BLK), :] = jnp.where(m0, outs[0], outs[1])
                ls_[i, pl.ds(q0, BLK), :] = jnp.where(m0, lses[0], lses[1])
                return c

            lax.fori_loop(0, NB, blk, 0, unroll=4)

        for i, d in enumerate(DILS):
            if d == 1:
                continue
            L = S // d
            for r in range(d):
                on_[i - 1, pl.ds(r, L, stride=d), :] = os_[i, pl.ds(r * L, L), :]
                ln_[i - 1, pl.ds(r, L, stride=d), :] = ls_[i, pl.ds(r * L, L), :]

        def comb(c, carry):
            rows = pl.ds(pl.multiple_of(c * 256, 256), 256)
            l0, l1, l2 = ls_[0, rows, :], ln_[0, rows, :], ln_[1, rows, :]
            mx = jnp.maximum(jnp.maximum(l0, l1), l2)
            e0, e1, e2 = jnp.exp(l0 - mx), jnp.exp(l1 - mx), jnp.exp(l2 - mx)
            tot = e0 + e1 + e2
            inv = 1.0 / tot
            o = (e0 * os_[0, rows, :] + e1 * on_[0, rows, :] + e2 * on_[1, rows, :]) * inv
            o_ref[rows, :] = o.astype(BF16)
            lse_ref[rows, :] = mx + jnp.log(tot)
            return carry

        lax.fori_loop(0, S // 256, comb, 0)

    pair = 2 * HD
    blk_spec = lambda off: pl.BlockSpec((S, pair), lambda b, p, off=off: (b, off + p))
    gspec = pl.BlockSpec((1, pair), lambda b, p: (0, 0))
    return pl.pallas_call(
        body, grid=(B, DA // pair),
        in_specs=[blk_spec(0), blk_spec(DA // pair), blk_spec(2 * DA // pair), gspec, gspec],
        out_specs=[blk_spec(0), blk_spec(0)],
        out_shape=[jax.ShapeDtypeStruct((T, DA), BF16), jax.ShapeDtypeStruct((T, DA), F32)],
        scratch_shapes=[pltpu.VMEM((S, pair), F32),
                        pltpu.VMEM((3, S, pair), BF16),
                        pltpu.VMEM((3, BLK + S, pair), BF16),
                        pltpu.VMEM((3, BLK + S, pair), BF16),
                        pltpu.VMEM((3, S, pair), F32),
                        pltpu.VMEM((3, S, pair), F32),
                        pltpu.VMEM((2, S, pair), F32),
                        pltpu.VMEM((2, S, pair), F32)],
        compiler_params=_cp(("parallel", "parallel")), name="attn_fwd")(u, u, u, qg2, kg2)


def _attn_bwd(u, attn, dattn, lse, qg2, kg2, B, S):
    T = B * S
    NB = S // BLK
    scale = HD ** -0.5
    pair = 2 * HD

    def body(q_ref, k_ref, v_ref, o_ref, do_ref, lse_ref, qg_ref, kg_ref,
             dq_ref, dk_ref, dv_ref, dgn_ref,
             tmp, qb, kb, vb, dob, lsb, dlb, dqs, dks, dvs, accq, acck, accv):
        m0, m1 = _head_masks()
        qv = q_ref[...].astype(F32)
        tmp[...] = qv * _head_rms(qv, m0) * (qg_ref[...] * scale)
        for i, d in enumerate(DILS):
            _deinterleave(tmp, qb, i, d, S, 0)
        kv = k_ref[...].astype(F32)
        tmp[...] = kv * _head_rms(kv, m0) * kg_ref[...]
        for i, d in enumerate(DILS):
            _deinterleave(tmp, kb, i, d, S, BLK)
        tmp[...] = v_ref[...].astype(F32)
        for i, d in enumerate(DILS):
            _deinterleave(tmp, vb, i, d, S, BLK)
            kb[i, pl.ds(0, BLK), :] = jnp.zeros((BLK, pair), BF16)
            vb[i, pl.ds(0, BLK), :] = jnp.zeros((BLK, pair), BF16)
        dov = do_ref[...].astype(F32)
        tmp[...] = dov
        for i, d in enumerate(DILS):
            _deinterleave(tmp, dob, i, d, S, 0)
        tmp[...] = lse_ref[...]
        for i, d in enumerate(DILS):
            _deinterleave(tmp, lsb, i, d, S, 0)
        prod = dov * o_ref[...].astype(F32)
        d0 = jnp.sum(jnp.where(m0, prod, 0.0), axis=-1, keepdims=True)
        d1 = jnp.sum(jnp.where(m0, 0.0, prod), axis=-1, keepdims=True)
        tmp[...] = jnp.where(m0, d0, d1)
        for i, d in enumerate(DILS):
            _deinterleave(tmp, dlb, i, d, S, 0)
        dks[...] = jnp.zeros_like(dks)
        dvs[...] = jnp.zeros_like(dvs)

        for i, d in enumerate(DILS):
            seg = NB // d

            def blk(j, c, i=i, seg=seg):
                q0 = pl.multiple_of(j * BLK, BLK)
                qv_ = qb[i, pl.ds(q0, BLK), :]
                kk = kb[i, pl.ds(q0, 2 * BLK), :]
                vv = vb[i, pl.ds(q0, 2 * BLK), :]
                dov_ = dob[i, pl.ds(q0, BLK), :]
                lsv = lsb[i, pl.ds(q0, BLK), :]
                dlv = dlb[i, pl.ds(q0, BLK), :]
                mask = _band_mask((j % seg) == 0)
                dq_acc = jnp.zeros((BLK, pair), F32)
                dk_acc = jnp.zeros((2 * BLK, pair), F32)
                dv_acc = jnp.zeros((2 * BLK, pair), F32)
                for hi, mh in enumerate((m0, m1)):
                    col = slice(hi * HD, hi * HD + 1)
                    qh = jnp.where(mh, qv_, jnp.zeros_like(qv_))
                    doh = jnp.where(mh, dov_, jnp.zeros_like(dov_))
                    s = lax.dot_general(qh, kk, NT, preferred_element_type=F32)
                    p = jnp.where(mask, jnp.exp(s - lsv[:, col]), 0.0)
                    dp = lax.dot_general(doh, vv, NT, preferred_element_type=F32)
                    ds = (p * (dp - dlv[:, col])).astype(BF16)
                    pb = p.astype(BF16)
                    dq_acc = dq_acc + jnp.where(mh, jnp.dot(ds, kk, preferred_element_type=F32), 0.0)
                    dk_acc = dk_acc + lax.dot_general(ds, qh, TN, preferred_element_type=F32)
                    dv_acc = dv_acc + lax.dot_general(pb, doh, TN, preferred_element_type=F32)
                dqs[i, pl.ds(q0, BLK), :] = dq_acc
                dks[i, pl.ds(q0, 2 * BLK), :] += dk_acc
                dvs[i, pl.ds(q0, 2 * BLK), :] += dv_acc
                return c

            lax.fori_loop(0, NB, blk, 0, unroll=4)

        accq[...] = dqs[0]
        acck[...] = dks[0, pl.ds(BLK, S), :]
        accv[...] = dvs[0, pl.ds(BLK, S), :]
        for i, d in enumerate(DILS):
            if d == 1:
                continue
            L = S // d
            for r in range(d):
                rows = pl.ds(r, L, stride=d)
                accq[rows, :] += dqs[i, pl.ds(r * L, L), :]
                acck[rows, :] += dks[i, pl.ds(BLK + r * L, L), :]
                accv[rows, :] += dvs[i, pl.ds(BLK + r * L, L), :]

        def head_mean(xv):
            a0 = jnp.sum(jnp.where(m0, xv, 0.0), axis=-1, keepdims=True)
            a1 = jnp.sum(jnp.where(m0, 0.0, xv), axis=-1, keepdims=True)
            return jnp.where(m0, a0, a1) * (1.0 / HD)

        def norm_bwd(x_ref, dn, gain):
            xv = x_ref[...].astype(F32)
            r = _head_rms(xv, m0)
            xhat = xv * r
            dxhat = dn * gain
            dx = r * (dxhat - xhat * head_mean(dxhat * xhat))
            return dx, jnp.sum(dn * xhat, axis=0, keepdims=True)

        dq, dgq = norm_bwd(q_ref, accq[...], qg_ref[...] * scale)
        dk, dgk = norm_bwd(k_ref, acck[...], kg_ref[...])
        dq_ref[...] = dq.astype(BF16)
        dk_ref[...] = dk.astype(BF16)
        dv_ref[...] = accv[...].astype(BF16)
        dgn_ref[...] = jnp.concatenate([dgq * scale, dgk, jnp.zeros((6, pair), F32)], axis=0)[None]

    blk_spec = lambda off: pl.BlockSpec((S, pair), lambda b, p, off=off: (b, off + p))
    gspec = pl.BlockSpec((1, pair), lambda b, p: (0, 0))
    np_ = DA // pair
    return pl.pallas_call(
        body, grid=(B, np_),
        in_specs=[blk_spec(0), blk_spec(np_), blk_spec(2 * np_), blk_spec(0), blk_spec(0), blk_spec(0),
                  gspec, gspec],
        out_specs=[blk_spec(0), blk_spec(0), blk_spec(0),
                   pl.BlockSpec((1, 8, pair), lambda b, p: (b * np_ + p, 0, 0))],
        out_shape=[jax.ShapeDtypeStruct((T, DA), BF16)] * 3 + [jax.ShapeDtypeStruct((B * np_, 8, pair), F32)],
        scratch_shapes=[pltpu.VMEM((S, pair), F32),
                        pltpu.VMEM((3, S, pair), BF16),
                        pltpu.VMEM((3, BLK + S, pair), BF16),
                        pltpu.VMEM((3, BLK + S, pair), BF16),
                        pltpu.VMEM((3, S, pair), BF16),
                        pltpu.VMEM((3, S, pair), F32),
                        pltpu.VMEM((3, S, pair), F32),
                        pltpu.VMEM((3, S, pair), F32),
                        pltpu.VMEM((3, BLK + S, pair), F32),
                        pltpu.VMEM((3, BLK + S, pair), F32),
                        pltpu.VMEM((S, pair), F32),
                        pltpu.VMEM((S, pair), F32),
                        pltpu.VMEM((S, pair), F32)],
        compiler_params=_cp(("parallel", "parallel")), name="attn_bwd")(u, u, u, attn, dattn, lse, qg2, kg2)


CT = 32
CPAD = 32


def _ln_fwd(y, g, b):
    mu = jnp.mean(y, axis=-1, keepdims=True)
    yc = y - mu
    rstd = lax.rsqrt(jnp.mean(yc * yc, axis=-1, keepdims=True) + EPS)
    xhat = yc * rstd
    return xhat, rstd, xhat * g + b


def _fill_glu(ca_ref, cg_ref, glu, S):
    glu[pl.ds(0, CPAD), :] = jnp.zeros((CPAD, DC), F32)

    def fill(i, c):
        rows = pl.ds(pl.multiple_of(i * 256, 256), 256)
        a = ca_ref[rows, :].astype(F32)
        gt = cg_ref[rows, :].astype(F32)
        glu[pl.ds(pl.multiple_of(CPAD + i * 256, CT), 256), :] = a * _sigmoid(gt)
        return c

    lax.fori_loop(0, S // 256, fill, 0)


def _conv_fwd(u, cw, cb, lg, lb, B, S):
    T = B * S

    def body(ca_ref, cg_ref, w_ref, b_ref, lg_ref, lb_ref, o_ref, y_ref, glu):
        _fill_glu(ca_ref, cg_ref, glu, S)

        def step(i, c):
            t0 = pl.multiple_of(i * CT, CT)
            win = glu[pl.ds(t0, 2 * CT), :]
            acc = jnp.zeros((CT, DC), F32) + b_ref[...]
            for k in range(CK):
                acc = acc + win[k + 2:k + 2 + CT, :] * w_ref[k:k + 1, :]
            y_ref[pl.ds(t0, CT), :] = acc
            _, _, z = _ln_fwd(acc, lg_ref[...], lb_ref[...])
            o_ref[pl.ds(t0, CT), :] = (z * _sigmoid(z)).astype(BF16)
            return c

        lax.fori_loop(0, S // CT, step, 0)

    vec = pl.BlockSpec((1, DC), lambda b: (0, 0))
    return pl.pallas_call(
        body, grid=(B,),
        in_specs=[pl.BlockSpec((S, DC), lambda b: (b, 3)), pl.BlockSpec((S, DC), lambda b: (b, 4)),
                  pl.BlockSpec((CT, DC), lambda b: (0, 0)), vec, vec, vec],
        out_specs=[pl.BlockSpec((S, DC), lambda b: (b, 0))] * 2,
        out_shape=[jax.ShapeDtypeStruct((T, DC), BF16), jax.ShapeDtypeStruct((T, DC), F32)],
        scratch_shapes=[pltpu.VMEM((CPAD + S, DC), F32)],
        compiler_params=_cp(("parallel",)), name="conv_fwd")(u, u, cw, cb, lg, lb)


def _conv_bwd(u, y, dconv, cw, lg, lb, B, S):
    T = B * S

    def body(ca_ref, cg_ref, y_ref, dc_ref, w_ref, lg_ref, lb_ref,
             dca_ref, dcg_ref, dw_ref, ds_ref, glu, dyp, dwacc):
        _fill_glu(ca_ref, cg_ref, glu, S)
        dyp[pl.ds(S, CPAD), :] = jnp.zeros((CPAD, DC), F32)
        lgv, lbv = lg_ref[...], lb_ref[...]

        def sum8(v):
            return v[0:8] + v[8:16] + v[16:24] + v[24:32]

        def p1(i, carry):
            sb, sg, sl = carry
            t0 = pl.multiple_of(i * CT, CT)
            xhat, rstd, z = _ln_fwd(y_ref[pl.ds(t0, CT), :], lgv, lbv)
            sz = _sigmoid(z)
            dz = dc_ref[pl.ds(t0, CT), :].astype(F32) * (sz * (1.0 + z * (1.0 - sz)))
            dxhat = dz * lgv
            dy = rstd * (dxhat - jnp.mean(dxhat, axis=-1, keepdims=True)
                         - xhat * jnp.mean(dxhat * xhat, axis=-1, keepdims=True))
            dyp[pl.ds(t0, CT), :] = dy
            return sb + sum8(dy), sg + sum8(dz * xhat), sl + sum8(dz)

        z8 = jnp.zeros((8, DC), F32)
        sb, sg, sl = lax.fori_loop(0, S // CT, p1, (z8, z8, z8))
        rs = lambda v: jnp.sum(v, axis=0, keepdims=True)
        ds_ref[...] = jnp.concatenate([rs(sb), rs(sg), rs(sl), jnp.zeros((5, DC), F32)], axis=0)[None]

        def p2(i, c):
            t0 = pl.multiple_of(i * CT, CT)
            win = dyp[pl.ds(t0, 2 * CT), :]
            acc = jnp.zeros((CT, DC), F32)
            for k in range(CK):
                acc = acc + win[30 - k:30 - k + CT, :] * w_ref[k:k + 1, :]
            a = ca_ref[pl.ds(t0, CT), :].astype(F32)
            sgt = _sigmoid(cg_ref[pl.ds(t0, CT), :].astype(F32))
            dca_ref[pl.ds(t0, CT), :] = (acc * sgt).astype(BF16)
            dcg_ref[pl.ds(t0, CT), :] = (acc * a * sgt * (1.0 - sgt)).astype(BF16)
            return c

        lax.fori_loop(0, S // CT, p2, 0)

        dwacc[...] = jnp.zeros_like(dwacc)

        def p3(i, c):
            t0 = pl.multiple_of(i * CT, CT)
            win = glu[pl.ds(t0, 2 * CT), :]
            dy = dyp[pl.ds(t0, CT), :]
            for k in range(CK):
                dwacc[k] += sum8(dy * win[k + 2:k + 2 + CT, :])
            return c

        lax.fori_loop(0, S // CT, p3, 0)
        dw_ref[...] = jnp.sum(dwacc[...], axis=1)[None]

    vec = pl.BlockSpec((1, DC), lambda b: (0, 0))
    seq = pl.BlockSpec((S, DC), lambda b: (b, 0))
    return pl.pallas_call(
        body, grid=(B,),
        in_specs=[pl.BlockSpec((S, DC), lambda b: (b, 3)), pl.BlockSpec((S, DC), lambda b: (b, 4)),
                  seq, seq, pl.BlockSpec((CT, DC), lambda b: (0, 0)), vec, vec],
        out_specs=[seq, seq, pl.BlockSpec((1, CT, DC), lambda b: (b, 0, 0)),
                   pl.BlockSpec((1, 8, DC), lambda b: (b, 0, 0))],
        out_shape=[jax.ShapeDtypeStruct((T, DC), BF16)] * 2
                  + [jax.ShapeDtypeStruct((B, CT, DC), F32), jax.ShapeDtypeStruct((B, 8, DC), F32)],
        scratch_shapes=[pltpu.VMEM((CPAD + S, DC), F32), pltpu.VMEM((S + CPAD, DC), F32),
                        pltpu.VMEM((CT, 8, DC), F32)],
        compiler_params=_cp(("parallel",)), name="conv_bwd")(u, u, y, dconv, cw, lg, lb)


def _local_step(x, target, norms, W, B, S):
    qg2 = jnp.concatenate([norms["q_norm"], norms["q_norm"]], axis=1)
    kg2 = jnp.concatenate([norms["k_norm"], norms["k_norm"]], axis=1)
    cw = jnp.concatenate([W["conv_w"], jnp.zeros((1, DC), F32)], axis=0)

    h1, n1, g1, u1 = _ffn_fwd(x, norms["ffn1_norm"], W["wg1"], W["wu1"], W["wd1"], None, "ffn1_fwd")
    u, n2 = _mix_in(h1, norms["mix_norm"], W["win"])
    attn, lse = _attn_fwd(u, qg2, kg2, B, S)
    conv, y = _conv_fwd(u, cw, norms["conv_b"], norms["conv_ln_g"], norms["conv_ln_b"], B, S)
    h2 = _mix_out(h1, attn, conv, W["wout"])
    h3, n3, g2, u2, dout, dyb, sq = _ffn_fwd(h2, norms["ffn2_norm"], W["wg2"], W["wu2"], W["wd2"], target, "ffn2_fwd")
    del h3
    loss = (0.5 / D) * jnp.sum(sq)

    dwg2, dwu2, dwd2, dn3 = _ffn_bwd(dyb, n3, g2, u2, W["wg2"], W["wu2"], W["wd2"], "ffn2_bwd")
    dh2, dgn_ffn2 = _norm_bwd(dn3, h2, dout, norms["ffn2_norm"], "ffn2_norm_bwd")
    dattn, dconv, dwout = _mix_out_bwd(dh2, attn, conv, W["wout"])
    dq, dk, dv, dgn_qk = _attn_bwd(u, attn, dattn, lse, qg2, kg2, B, S)
    dca, dcg, dcw, dcs = _conv_bwd(u, y, dconv, cw, norms["conv_ln_g"], norms["conv_ln_b"], B, S)
    dwin, dh1, dyb1, dgn_mix = _mix_in_bwd((dq, dk, dv, dca, dcg), W["win"], n2, h1, dh2, norms["mix_norm"])
    dwg1, dwu1, dwd1, dn1 = _ffn_bwd(dyb1, n1, g1, u1, W["wg1"], W["wu1"], W["wd1"], "ffn1_bwd")
    gx, dgn_ffn1 = _norm_bwd(dn1, x, dh1, norms["ffn1_norm"], "ffn1_norm_bwd")

    qk = jnp.sum(dgn_qk, axis=0)
    cs = jnp.sum(dcs, axis=0)
    small = {
        "ffn1_norm": jnp.sum(dgn_ffn1, axis=0),
        "mix_norm": jnp.sum(dgn_mix, axis=0),
        "q_norm": qk[0:1, 0:HD] + qk[0:1, HD:2 * HD],
        "k_norm": qk[1:2, 0:HD] + qk[1:2, HD:2 * HD],
        "conv_w": jnp.sum(dcw, axis=0)[0:CK],
        "conv_b": cs[0:1],
        "conv_ln_g": cs[1:2],
        "conv_ln_b": cs[2:3],
        "ffn2_norm": jnp.sum(dgn_ffn2, axis=0),
    }
    big = {"wg1": dwg1, "wu1": dwu1, "wd1": dwd1, "win": dwin, "wout": dwout,
           "wg2": dwg2, "wu2": dwu2, "wd2": dwd2}
    return loss, gx, big, small


HBM = pl.BlockSpec(memory_space=pltpu.HBM)
VMEM = pl.BlockSpec(memory_space=pltpu.VMEM)


def _place():
    return lax.axis_index("x"), lax.axis_index("y"), lax.axis_index("c")


def _gather_weights(shards):
    nt = len(shards)
    tshape = [s.shape for s, _ in shards]

    def body(*refs):
        ins, outs = refs[:nt], refs[nt:2 * nt]
        stages = refs[2 * nt:3 * nt]
        send_sems, recv_sems, local_sems = refs[3 * nt:]
        x, y, c = _place()
        me, sibling = (x, y, c), (x, y, 1 - c)
        chips = [(1 - x, y), (x, 1 - y), (1 - x, 1 - y)]

        def rows(t, px, py, pc):
            r = tshape[t][0]
            return outs[t].at[pl.ds((4 * px + 2 * py + pc) * r, r), :]

        def copy(t, k, block, to, src=None):
            return pltpu.make_async_remote_copy(
                src_ref=rows(t, *block) if src is None else src, dst_ref=rows(t, *block),
                send_sem=send_sems.at[t, k], recv_sem=recv_sems.at[t, k],
                device_id=to, device_id_type=MESH)

        first, mine = [], []
        for t, (_, dt) in enumerate(shards):
            stages[t][...] = ins[t][...].astype(dt)
            cp = pltpu.make_async_copy(stages[t], rows(t, *me), local_sems.at[t])
            cp.start()
            mine.append(cp)
            first.append(copy(t, 0, me, sibling, src=stages[t]))
            first += [copy(t, 1 + j, me, (*chip, c), src=stages[t]) for j, chip in enumerate(chips)]
        for cp in first:
            cp.start()
        passed = []
        for j, chip in enumerate(chips):
            for t in range(nt):
                copy(t, 1 + j, (*chip, c), me).wait_recv()
                cp = copy(t, 4 + j, (*chip, c), sibling)
                cp.start()
                passed.append(cp)
        for t in range(nt):
            copy(t, 0, sibling, me).wait_recv()
            for j, chip in enumerate(chips):
                copy(t, 4 + j, (*chip, 1 - c), me).wait_recv()
        for cp in first + passed:
            cp.wait_send()
        for cp in mine:
            cp.wait()

    return pl.pallas_call(
        body, in_specs=[VMEM] * nt, out_specs=[HBM] * nt,
        out_shape=[jax.ShapeDtypeStruct((NDEV * r, n), dt) for (r, n), (_, dt) in zip(tshape, shards)],
        scratch_shapes=[pltpu.VMEM(s, dt) for s, (_, dt) in zip(tshape, shards)]
                       + [pltpu.SemaphoreType.DMA((nt, 7)), pltpu.SemaphoreType.DMA((nt, 7)),
                          pltpu.SemaphoreType.DMA((nt,))],
        compiler_params=pltpu.CompilerParams(vmem_limit_bytes=VMEM_LIMIT),
        name="gather_weights")(*[s for s, _ in shards])


def _sibling_exchange(grads):
    nt = len(grads)
    g4 = [g.reshape(4, 2, g.shape[0] // NDEV, g.shape[1]) for g in grads]

    def body(*refs):
        ins, outs = refs[:nt], refs[nt:2 * nt]
        send_sems, recv_sems = refs[2 * nt:]
        x, y, c = _place()
        cps = []
        for t in range(nt):
            cp = pltpu.make_async_remote_copy(
                src_ref=ins[t].at[:, 1 - c], dst_ref=outs[t],
                send_sem=send_sems.at[t], recv_sem=recv_sems.at[t],
                device_id=(x, y, 1 - c), device_id_type=MESH)
            cp.start()
            cps.append(cp)
        for cp in cps:
            cp.wait()

    return pl.pallas_call(
        body, in_specs=[HBM] * nt, out_specs=[HBM] * nt,
        out_shape=[jax.ShapeDtypeStruct((4,) + g.shape[2:], BF16) for g in g4],
        scratch_shapes=[pltpu.SemaphoreType.DMA((nt,)), pltpu.SemaphoreType.DMA((nt,))],
        name="sibling_exchange")(*g4), g4


def _chip_partial(g4, recv, cidx, name):
    _, _, rows, n = g4.shape

    def body(c_ref, a_ref, b_ref, o_ref):
        o_ref[...] = (a_ref[...].astype(F32) + b_ref[...].astype(F32)).astype(BF16)

    return pl.pallas_call(
        body,
        grid_spec=pltpu.PrefetchScalarGridSpec(
            num_scalar_prefetch=1, grid=(4,),
            in_specs=[pl.BlockSpec((None, None, rows, n), lambda q, c_ref: (q, c_ref[0], 0, 0)),
                      pl.BlockSpec((None, rows, n), lambda q, c_ref: (q, 0, 0))],
            out_specs=pl.BlockSpec((None, rows, n), lambda q, c_ref: (q, 0, 0))),
        out_shape=jax.ShapeDtypeStruct((4, rows, n), BF16),
        compiler_params=_cp(("parallel",)), name=name)(cidx, g4, recv)


def _chip_exchange(parts):
    nt = len(parts)

    def body(*refs):
        ins, outs = refs[:nt], refs[nt:2 * nt]
        send_sems, recv_sems, local_sems = refs[2 * nt:]
        x, y, c = _place()
        qme = 2 * x + y
        chips = [(1 - x, y), (x, 1 - y), (1 - x, 1 - y)]
        cps, mine = [], []
        for t in range(nt):
            cp = pltpu.make_async_copy(ins[t].at[qme], outs[t].at[qme], local_sems.at[t])
            cp.start()
            mine.append(cp)
            for k, (cx, cy) in enumerate(chips):
                cp = pltpu.make_async_remote_copy(
                    src_ref=ins[t].at[2 * cx + cy], dst_ref=outs[t].at[qme],
                    send_sem=send_sems.at[t, k], recv_sem=recv_sems.at[t, k],
                    device_id=(cx, cy, c), device_id_type=MESH)
                cp.start()
                cps.append(cp)
        for cp in cps:
            cp.wait()
        for cp in mine:
            cp.wait()

    return pl.pallas_call(
        body, in_specs=[HBM] * nt, out_specs=[HBM] * nt,
        out_shape=[jax.ShapeDtypeStruct(p.shape, BF16) for p in parts],
        scratch_shapes=[pltpu.SemaphoreType.DMA((nt, 3)), pltpu.SemaphoreType.DMA((nt, 3)),
                        pltpu.SemaphoreType.DMA((nt,))],
        name="chip_exchange")(*parts)


def _adamw_math(w, g, m, v):
    m = B1 * m + (1.0 - B1) * g
    v = B2 * v + (1.0 - B2) * (g * g)
    m_hat = m / (1.0 - B1 ** STEP)
    v_hat = v / (1.0 - B2 ** STEP)
    delta = -LR * (m_hat / (jnp.sqrt(v_hat) + AEPS) + WD * w)
    return delta, m, v


def _adamw_big(recv, w, m, v, name):
    def body(r_ref, w_ref, m_ref, v_ref, g_ref, d_ref, mo_ref, vo_ref):
        g = r_ref[0].astype(F32)
        for q in range(1, 4):
            g = g + r_ref[q].astype(F32)
        d, mn, vn = _adamw_math(w_ref[...], g, m_ref[...], v_ref[...])
        g_ref[...] = g
        d_ref[...] = d
        mo_ref[...] = mn
        vo_ref[...] = vn

    return pl.pallas_call(
        body, out_shape=[jax.ShapeDtypeStruct(w.shape, F32)] * 4,
        compiler_params=pltpu.CompilerParams(vmem_limit_bytes=VMEM_LIMIT), name=name)(recv, w, m, v)


SMALL_ROWS = 40
CW_ROWS = 16
SMALL_TOTAL = SMALL_ROWS + NDEV * CW_ROWS


def _small_step(gvec, wvec, mvec, vvec):
    nr = SMALL_ROWS + CW_ROWS

    def body(g_ref, w_ref, m_ref, v_ref, go_ref, d_ref, mo_ref, vo_ref, slots, send_sems, recv_sems):
        x, y, c = _place()
        me = 4 * x + 2 * y + c
        slots[me] = g_ref[...]
        cps = []
        for k in range(1, NDEV):
            fx, fy, fc = (k >> 2) & 1, (k >> 1) & 1, k & 1
            peer = (x ^ fx, y ^ fy, c ^ fc)
            cp = pltpu.make_async_remote_copy(
                src_ref=g_ref, dst_ref=slots.at[me],
                send_sem=send_sems.at[k - 1], recv_sem=recv_sems.at[k - 1],
                device_id=peer, device_id_type=MESH)
            cp.start()
            cps.append(cp)
        for cp in cps:
            cp.wait()
        tot = slots[0]
        for j in range(1, NDEV):
            tot = tot + slots[j]
        slots[0] = tot
        g = jnp.concatenate(
            [tot[0:SMALL_ROWS], slots[0, pl.ds(pl.multiple_of(SMALL_ROWS + me * CW_ROWS, 8), CW_ROWS), :]], axis=0)
        d, mn, vn = _adamw_math(w_ref[...], g, m_ref[...], v_ref[...])
        go_ref[...] = g
        d_ref[...] = d
        mo_ref[...] = mn
        vo_ref[...] = vn

    return pl.pallas_call(
        body, in_specs=[VMEM] * 4, out_specs=[VMEM] * 4,
        out_shape=[jax.ShapeDtypeStruct((nr, 128), F32)] * 4,
        scratch_shapes=[pltpu.VMEM((NDEV, SMALL_TOTAL, 128), F32),
                        pltpu.SemaphoreType.DMA((NDEV - 1,)), pltpu.SemaphoreType.DMA((NDEV - 1,))],
        name="small_step")(gvec, wvec, mvec, vvec)


SMALL_NAMES = ("ffn1_norm", "mix_norm", "ffn2_norm", "conv_b", "conv_ln_g", "conv_ln_b", "q_norm", "k_norm")


def _pack_small(vals):
    parts = []
    for n in SMALL_NAMES:
        a = vals[n].reshape(-1)
        if a.shape[0] < 128:
            a = jnp.concatenate([a, jnp.zeros((128 - a.shape[0],), F32)])
        parts.append(a.reshape(-1, 128))
    used = sum(p.shape[0] for p in parts)
    parts.append(jnp.zeros((SMALL_ROWS - used, 128), F32))
    return jnp.concatenate(parts, axis=0)


def _unpack_small(vec, shapes):
    out, r = {}, 0
    for n in SMALL_NAMES:
        size = shapes[n][1]
        nr = max(size // 128, 1)
        out[n] = vec[r:r + nr].reshape(-1)[:size].reshape(1, size)
        r += nr
    return out


def _pack_cw(a):
    flat = a.reshape(-1)
    return jnp.concatenate([flat, jnp.zeros((CW_ROWS * 128 - flat.shape[0],), F32)]).reshape(CW_ROWS, 128)


def _unpack_cw(v):
    return v.reshape(-1)[:CK * HD].reshape(1, CK, HD)


def kernel(x, ffn1_norm, ffn1_w_gate, ffn1_w_up, ffn1_w_down, mix_norm, w_in, q_norm, k_norm, conv_w, conv_b, conv_ln_g, conv_ln_b, w_out, ffn2_norm, ffn2_w_gate, ffn2_w_up, ffn2_w_down, loss_target, m_ffn1_norm, m_ffn1_w_gate, m_ffn1_w_up, m_ffn1_w_down, m_mix_norm, m_w_in, m_q_norm, m_k_norm, m_conv_w, m_conv_b, m_conv_ln_g, m_conv_ln_b, m_w_out, m_ffn2_norm, m_ffn2_w_gate, m_ffn2_w_up, m_ffn2_w_down, v_ffn1_norm, v_ffn1_w_gate, v_ffn1_w_up, v_ffn1_w_down, v_mix_norm, v_w_in, v_q_norm, v_k_norm, v_conv_w, v_conv_b, v_conv_ln_g, v_conv_ln_b, v_w_out, v_ffn2_norm, v_ffn2_w_gate, v_ffn2_w_up, v_ffn2_w_down):
    P = dict(ffn1_norm=ffn1_norm, ffn1_w_gate=ffn1_w_gate, ffn1_w_up=ffn1_w_up, ffn1_w_down=ffn1_w_down,
             mix_norm=mix_norm, w_in=w_in, q_norm=q_norm, k_norm=k_norm, conv_w=conv_w, conv_b=conv_b,
             conv_ln_g=conv_ln_g, conv_ln_b=conv_ln_b, w_out=w_out, ffn2_norm=ffn2_norm,
             ffn2_w_gate=ffn2_w_gate, ffn2_w_up=ffn2_w_up, ffn2_w_down=ffn2_w_down)
    M = dict(ffn1_norm=m_ffn1_norm, ffn1_w_gate=m_ffn1_w_gate, ffn1_w_up=m_ffn1_w_up, ffn1_w_down=m_ffn1_w_down,
             mix_norm=m_mix_norm, w_in=m_w_in, q_norm=m_q_norm, k_norm=m_k_norm, conv_w=m_conv_w, conv_b=m_conv_b,
             conv_ln_g=m_conv_ln_g, conv_ln_b=m_conv_ln_b, w_out=m_w_out, ffn2_norm=m_ffn2_norm,
             ffn2_w_gate=m_ffn2_w_gate, ffn2_w_up=m_ffn2_w_up, ffn2_w_down=m_ffn2_w_down)
    V = dict(ffn1_norm=v_ffn1_norm, ffn1_w_gate=v_ffn1_w_gate, ffn1_w_up=v_ffn1_w_up, ffn1_w_down=v_ffn1_w_down,
             mix_norm=v_mix_norm, w_in=v_w_in, q_norm=v_q_norm, k_norm=v_k_norm, conv_w=v_conv_w, conv_b=v_conv_b,
             conv_ln_g=v_conv_ln_g, conv_ln_b=v_conv_ln_b, w_out=v_w_out, ffn2_norm=v_ffn2_norm,
             ffn2_w_gate=v_ffn2_w_gate, ffn2_w_up=v_ffn2_w_up, ffn2_w_down=v_ffn2_w_down)
    order = ["ffn1_norm", "ffn1_w_gate", "ffn1_w_up", "ffn1_w_down", "mix_norm", "w_in", "q_norm", "k_norm",
             "conv_w", "conv_b", "conv_ln_g", "conv_ln_b", "w_out", "ffn2_norm", "ffn2_w_gate", "ffn2_w_up",
             "ffn2_w_down"]
    B, S, _ = x.shape
    T = B * S
    cidx = lax.axis_index("c").astype(jnp.int32).reshape(1)

    bigs = [("wg1", "ffn1_w_gate", True), ("wu1", "ffn1_w_up", True), ("wd1", "ffn1_w_down", False),
            ("win", "w_in", True), ("wout", "w_out", False),
            ("wg2", "ffn2_w_gate", True), ("wu2", "ffn2_w_up", True), ("wd2", "ffn2_w_down", False)]
    hm = lambda a, tr: jnp.transpose(a[0]) if tr else a[0]
    cw_pad = jnp.zeros((32, 128), F32).at[0:CK, 0:HD].set(conv_w[0])
    gathered = _gather_weights([(hm(P[pn], tr), BF16) for _, pn, tr in bigs] + [(cw_pad, F32)])
    W = {ln: gathered[i] for i, (ln, _, _) in enumerate(bigs)}
    cwg = gathered[-1].reshape(NDEV, 32, 128)[:, 0:CK, 0:HD]
    W["conv_w"] = jnp.transpose(cwg, (1, 0, 2)).reshape(CK, DC)
    norms = {n: P[n] for n in SMALL_NAMES}

    loss, gx, big, small = _local_step(x.reshape(T, D), loss_target.reshape(T, D), norms, W, B, S)
    loss = lax.psum(loss, ("x", "y", "c"))

    names = [ln for ln, _, _ in bigs]
    recv_a, g4 = _sibling_exchange([big[n] for n in names])
    parts = [_chip_partial(g4[i], recv_a[i], cidx, "chip_partial_" + n) for i, n in enumerate(names)]
    recv_b = _chip_exchange(parts)
    G, Dl, Mn, Vn = {}, {}, {}, {}
    for i, (ln, pn, tr) in enumerate(bigs):
        outs = _adamw_big(recv_b[i], hm(P[pn], tr), hm(M[pn], tr), hm(V[pn], tr), "adamw_" + ln)
        G[pn], Dl[pn], Mn[pn], Vn[pn] = [(jnp.transpose(o) if tr else o)[None] for o in outs]

    dcw = small["conv_w"].reshape(CK, NDEV, HD).transpose(1, 0, 2)
    gvec = jnp.concatenate([_pack_small(small)] + [_pack_cw(dcw[j]) for j in range(NDEV)], axis=0)
    pack = lambda dct: jnp.concatenate([_pack_small({n: dct[n] for n in SMALL_NAMES}), _pack_cw(dct["conv_w"][0])], axis=0)
    go, do, mo, vo = _small_step(gvec, pack(P), pack(M), pack(V))
    shapes = {n: P[n].shape for n in SMALL_NAMES}
    for dst, vec in ((G, go), (Dl, do), (Mn, mo), (Vn, vo)):
        dst.update(_unpack_small(vec[0:SMALL_ROWS], shapes))
        dst["conv_w"] = _unpack_cw(vec[SMALL_ROWS:])

    return (loss, gx.reshape(B, S, D), *[G[n] for n in order], *[Dl[n] for n in order],
            *[Mn[n] for n in order], *[Vn[n] for n in order])
```

```python
import functools

import jax
import jax.numpy as jnp
from jax import lax
from jax.experimental import pallas as pl
from jax.experimental.pallas import tpu as pltpu

F32 = jnp.float32
BF16 = jnp.bfloat16

D = 1024
FF = 2816
HD = 64
DA = 512
DC = 512
DIN = 2560
CK = 31
BLK = 128
DILS = (1, 4, 16)
EPS = 1e-6
NDEV = 8
MESH = pl.DeviceIdType.MESH

LR, B1, B2, AEPS, WD, STEP = 0.001, 0.9, 0.999, 1e-08, 0.01, 10

NT = (((1,), (1,)), ((), ()))
TN = (((0,), (0,)), ((), ()))

VMEM_LIMIT = 56 * 1024 * 1024


def _cp(sem=None):
    return pltpu.CompilerParams(dimension_semantics=sem, vmem_limit_bytes=VMEM_LIMIT)


def _sigmoid(x):
    return 1.0 / (1.0 + jnp.exp(-x))


def _pallas(body, args, *, grid, in_specs, out_specs, out_shape, scratch_shapes, sem, name, carry=None):
    if carry is None:
        outs = pl.pallas_call(body, grid=grid, in_specs=in_specs, out_specs=out_specs, out_shape=out_shape,
                              scratch_shapes=scratch_shapes, compiler_params=_cp(sem), name=name)(*args)
        return outs, None
    n_in, n_out, n_scr = len(in_specs), len(out_shape), len(scratch_shapes)
    c_in, c_out = len(carry.in_arrays), len(carry.out_shape)

    def wrapped(*refs):
        ins, refs = refs[:n_in], refs[n_in:]
        cins, refs = refs[:c_in], refs[c_in:]
        outs, refs = refs[:n_out], refs[n_out:]
        couts, refs = refs[:c_out], refs[c_out:]
        scr, cscr = refs[:n_scr], refs[n_scr:]
        ids = [pl.program_id(a) for a in range(len(grid))]
        is_first = functools.reduce(jnp.logical_and, [i == 0 for i in ids])
        is_last = functools.reduce(jnp.logical_and, [i == n - 1 for i, n in zip(ids, grid)])

        @pl.when(is_first)
        def _():
            carry.start(cins, couts, cscr)

        body(*ins, *outs, *scr)

        @pl.when(is_last)
        def _():
            carry.finish(cins, couts, cscr)

    outs = pl.pallas_call(
        wrapped, grid=grid, in_specs=list(in_specs) + carry.in_specs, out_specs=list(out_specs) + carry.out_specs,
        out_shape=list(out_shape) + carry.out_shape, scratch_shapes=list(scratch_shapes) + carry.scratch,
        compiler_params=_cp(("arbitrary",) * len(grid)), name=name)(*args, *carry.in_arrays)
    return outs[:n_out], outs[n_out:]


def _ffn_fwd(x, gain, wg, wu, wd, target, name, carry=None):
    T = x.shape[0]
    tm, tf = 1024, 256
    nt, nf = T // tm, FF // tf
    with_loss = target is not None

    def body(*refs):
        if with_loss:
            (x_ref, gain_ref, wg_ref, wu_ref, wd_ref, t_ref,
             h_ref, n_ref, g_ref, u_ref, dout_ref, dyb_ref, sq_ref, nb_scr, acc_scr) = refs
        else:
            (x_ref, gain_ref, wg_ref, wu_ref, wd_ref,
             h_ref, n_ref, g_ref, u_ref, nb_scr, acc_scr) = refs
        f = pl.program_id(1)

        @pl.when(f == 0)
        def _():
            xv = x_ref[...]
            r = lax.rsqrt(jnp.mean(xv * xv, axis=-1, keepdims=True) + EPS)
            nb = (xv * r * gain_ref[...]).astype(BF16)
            nb_scr[...] = nb
            n_ref[...] = nb
            acc_scr[...] = jnp.zeros_like(acc_scr)

        nb = nb_scr[...]
        g = lax.dot_general(nb, wg_ref[...], NT, preferred_element_type=F32)
        u = lax.dot_general(nb, wu_ref[...], NT, preferred_element_type=F32)
        a = g * _sigmoid(g) * u
        g_ref[...] = g.astype(BF16)
        u_ref[...] = u.astype(BF16)
        acc_scr[...] += jnp.dot(a.astype(BF16), wd_ref[...], preferred_element_type=F32)

        @pl.when(f == nf - 1)
        def _():
            h = x_ref[...] + 0.5 * acc_scr[...]
            h_ref[...] = h
            if with_loss:
                e = h - t_ref[...]
                dout = e * (1.0 / D)
                dout_ref[...] = dout
                dyb_ref[...] = (0.5 * dout).astype(BF16)
                sq_ref[...] = jnp.sum(e * e, axis=0, keepdims=True)[None]

    row = pl.BlockSpec((tm, D), lambda t, f: (t, 0))
    wspec = pl.BlockSpec((tf, D), lambda t, f: (f, 0))
    gspec = pl.BlockSpec((tm, tf), lambda t, f: (t, f))
    in_specs = [row, pl.BlockSpec((1, D), lambda t, f: (0, 0)), wspec, wspec, wspec]
    out_shape = [jax.ShapeDtypeStruct((T, D), F32), jax.ShapeDtypeStruct((T, D), BF16),
                 jax.ShapeDtypeStruct((T, FF), BF16), jax.ShapeDtypeStruct((T, FF), BF16)]
    out_specs = [row, row, gspec, gspec]
    args = [x, gain, wg, wu, wd]
    if with_loss:
        in_specs.append(row)
        args.append(target)
        out_shape += [jax.ShapeDtypeStruct((T, D), F32), jax.ShapeDtypeStruct((T, D), BF16),
                      jax.ShapeDtypeStruct((nt, 1, D), F32)]
        out_specs += [row, row, pl.BlockSpec((1, 1, D), lambda t, f: (t, 0, 0))]
    return _pallas(
        body, args, grid=(nt, nf), in_specs=in_specs, out_specs=out_specs, out_shape=out_shape,
        scratch_shapes=[pltpu.VMEM((tm, D), BF16), pltpu.VMEM((tm, D), F32)],
        sem=("parallel", "arbitrary"), name=name, carry=carry)


def _ffn_bwd(dyb, nb, g, u, wg, wu, wd, name, carry=None):
    T = dyb.shape[0]
    tm, tf = 512, 256
    nt, nf = T // tm, FF // tf

    def body(dy_ref, n_ref, g_ref, u_ref, wg_ref, wu_ref, wd_ref,
             dwg_ref, dwu_ref, dwd_ref, dn_ref, ag_scr, au_scr, ad_scr):
        f, t = pl.program_id(0), pl.program_id(1)

        @pl.when(t == 0)
        def _():
            ag_scr[...] = jnp.zeros_like(ag_scr)
            au_scr[...] = jnp.zeros_like(au_scr)
            ad_scr[...] = jnp.zeros_like(ad_scr)

        dy = dy_ref[...]
        n = n_ref[...]
        gv = g_ref[...].astype(F32)
        uv = u_ref[...].astype(F32)
        da = lax.dot_general(dy, wd_ref[...], NT, preferred_element_type=F32)
        sg = _sigmoid(gv)
        silu = gv * sg
        ab = (silu * uv).astype(BF16)
        dgb = (da * uv * (sg * (1.0 + gv * (1.0 - sg)))).astype(BF16)
        dub = (da * silu).astype(BF16)
        ad_scr[...] += lax.dot_general(ab, dy, TN, preferred_element_type=F32)
        ag_scr[...] += lax.dot_general(dgb, n, TN, preferred_element_type=F32)
        au_scr[...] += lax.dot_general(dub, n, TN, preferred_element_type=F32)
        dn = (jnp.dot(dgb, wg_ref[...], preferred_element_type=F32)
              + jnp.dot(dub, wu_ref[...], preferred_element_type=F32))
        rows = pl.ds(pl.multiple_of(t * tm, tm), tm)

        @pl.when(f == 0)
        def _():
            dn_ref[rows, :] = dn

        @pl.when(f > 0)
        def _():
            dn_ref[rows, :] += dn

        @pl.when(t == nt - 1)
        def _():
            dwg_ref[...] = ag_scr[...].astype(BF16)
            dwu_ref[...] = au_scr[...].astype(BF16)
            dwd_ref[...] = ad_scr[...].astype(BF16)

    row = pl.BlockSpec((tm, D), lambda f, t: (t, 0))
    gspec = pl.BlockSpec((tm, tf), lambda f, t: (t, f))
    wspec = pl.BlockSpec((tf, D), lambda f, t: (f, 0))
    return _pallas(
        body, (dyb, nb, g, u, wg, wu, wd), grid=(nf, nt),
        in_specs=[row, row, gspec, gspec, wspec, wspec, wspec],
        out_specs=[wspec, wspec, wspec, pl.BlockSpec((T, D), lambda f, t: (0, 0))],
        out_shape=[jax.ShapeDtypeStruct((FF, D), BF16)] * 3 + [jax.ShapeDtypeStruct((T, D), F32)],
        scratch_shapes=[pltpu.VMEM((tf, D), F32)] * 3,
        sem=("arbitrary", "arbitrary"), name=name, carry=carry)


def _rms_bwd_rows(dn, xv, gain):
    r = lax.rsqrt(jnp.mean(xv * xv, axis=-1, keepdims=True) + EPS)
    xhat = xv * r
    dxhat = dn * gain
    dx = r * (dxhat - xhat * jnp.mean(dxhat * xhat, axis=-1, keepdims=True))
    return dx, jnp.sum(dn * xhat, axis=0, keepdims=True)


def _norm_bwd(dn, x, dout, gain, name):
    T = x.shape[0]
    tm = 512
    nt = T // tm

    def body(dn_ref, x_ref, dout_ref, gain_ref, dx_ref, dg_ref):
        dx, dgain = _rms_bwd_rows(dn_ref[...], x_ref[...], gain_ref[...])
        dx_ref[...] = dout_ref[...] + dx
        dg_ref[...] = dgain[None]

    row = pl.BlockSpec((tm, D), lambda t: (t, 0))
    return pl.pallas_call(
        body, grid=(nt,), in_specs=[row, row, row, pl.BlockSpec((1, D), lambda t: (0, 0))],
        out_specs=[row, pl.BlockSpec((1, 1, D), lambda t: (t, 0, 0))],
        out_shape=[jax.ShapeDtypeStruct((T, D), F32), jax.ShapeDtypeStruct((nt, 1, D), F32)],
        compiler_params=_cp(("parallel",)), name=name)(dn, x, dout, gain)


def _mix_in(h, gain, win):
    T = h.shape[0]
    tm = 512

    def body(h_ref, gain_ref, w_ref, u_ref, n_ref):
        xv = h_ref[...]
        r = lax.rsqrt(jnp.mean(xv * xv, axis=-1, keepdims=True) + EPS)
        nb = (xv * r * gain_ref[...]).astype(BF16)
        n_ref[...] = nb
        u_ref[...] = lax.dot_general(nb, w_ref[...], NT, preferred_element_type=F32).astype(BF16)

    row = pl.BlockSpec((tm, D), lambda t: (t, 0))
    return pl.pallas_call(
        body, grid=(T // tm,),
        in_specs=[row, pl.BlockSpec((1, D), lambda t: (0, 0)), pl.BlockSpec((DIN, D), lambda t: (0, 0))],
        out_specs=[pl.BlockSpec((tm, DIN), lambda t: (t, 0)), row],
        out_shape=[jax.ShapeDtypeStruct((T, DIN), BF16), jax.ShapeDtypeStruct((T, D), BF16)],
        compiler_params=_cp(("parallel",)), name="mix_in")(h, gain, win)


def _mix_out(h, attn, conv, wout):
    T = h.shape[0]
    tm = 512

    def body(h_ref, a_ref, c_ref, w_ref, o_ref):
        o_ref[...] = (h_ref[...]
                      + jnp.dot(a_ref[...], w_ref[0:DA, :], preferred_element_type=F32)
                      + jnp.dot(c_ref[...], w_ref[DA:D, :], preferred_element_type=F32))

    row = pl.BlockSpec((tm, D), lambda t: (t, 0))
    half = pl.BlockSpec((tm, DA), lambda t: (t, 0))
    return pl.pallas_call(
        body, grid=(T // tm,),
        in_specs=[row, half, half, pl.BlockSpec((D, D), lambda t: (0, 0))],
        out_specs=row, out_shape=jax.ShapeDtypeStruct((T, D), F32),
        compiler_params=_cp(("parallel",)), name="mix_out")(h, attn, conv, wout)


def _mix_out_bwd(dh, attn, conv, wout):
    T = dh.shape[0]
    tm = 512
    nt = T // tm

    def body(dh_ref, a_ref, c_ref, w_ref, da_ref, dc_ref, dw_ref, acc_scr):
        t = pl.program_id(0)

        @pl.when(t == 0)
        def _():
            acc_scr[...] = jnp.zeros_like(acc_scr)

        dhb = dh_ref[...].astype(BF16)
        dmix = lax.dot_general(dhb, w_ref[...], NT, preferred_element_type=F32)
        da_ref[...] = dmix[:, 0:DA].astype(BF16)
        dc_ref[...] = dmix[:, DA:D].astype(BF16)
        acc_scr[0:DA, :] += lax.dot_general(a_ref[...], dhb, TN, preferred_element_type=F32)
        acc_scr[DA:D, :] += lax.dot_general(c_ref[...], dhb, TN, preferred_element_type=F32)

        @pl.when(t == nt - 1)
        def _():
            dw_ref[...] = acc_scr[...].astype(BF16)

    row = pl.BlockSpec((tm, D), lambda t: (t, 0))
    half = pl.BlockSpec((tm, DA), lambda t: (t, 0))
    full = pl.BlockSpec((D, D), lambda t: (0, 0))
    return pl.pallas_call(
        body, grid=(nt,), in_specs=[row, half, half, full], out_specs=[half, half, full],
        out_shape=[jax.ShapeDtypeStruct((T, DA), BF16)] * 2 + [jax.ShapeDtypeStruct((D, D), BF16)],
        scratch_shapes=[pltpu.VMEM((D, D), F32)],
        compiler_params=_cp(("arbitrary",)), name="mix_out_bwd")(dh, attn, conv, wout)


def _mix_in_bwd(dparts, win, nb, h, dh, gain):
    T = h.shape[0]
    tm = 512
    nt = T // tm

    def body(d0, d1, d2, d3, d4, w_ref, n_ref, h_ref, dh_ref, gain_ref,
             dw_ref, dx_ref, dyb_ref, dg_ref, acc_scr):
        t = pl.program_id(0)

        @pl.when(t == 0)
        def _():
            acc_scr[...] = jnp.zeros_like(acc_scr)

        n = n_ref[...]
        dn = jnp.zeros((tm, D), F32)
        for i, d_ref in enumerate((d0, d1, d2, d3, d4)):
            dv = d_ref[...]
            dn = dn + jnp.dot(dv, w_ref[i * DA:(i + 1) * DA, :], preferred_element_type=F32)
            acc_scr[i * DA:(i + 1) * DA, :] += lax.dot_general(dv, n, TN, preferred_element_type=F32)
        dx, dgain = _rms_bwd_rows(dn, h_ref[...], gain_ref[...])
        tot = dh_ref[...] + dx
        dx_ref[...] = tot
        dyb_ref[...] = (0.5 * tot).astype(BF16)
        dg_ref[...] = dgain[None]

        @pl.when(t == nt - 1)
        def _():
            dw_ref[...] = acc_scr[...].astype(BF16)

    row = pl.BlockSpec((tm, D), lambda t: (t, 0))
    half = pl.BlockSpec((tm, DA), lambda t: (t, 0))
    full = pl.BlockSpec((DIN, D), lambda t: (0, 0))
    return pl.pallas_call(
        body, grid=(nt,),
        in_specs=[half] * 5 + [full, row, row, row, pl.BlockSpec((1, D), lambda t: (0, 0))],
        out_specs=[full, row, row, pl.BlockSpec((1, 1, D), lambda t: (t, 0, 0))],
        out_shape=[jax.ShapeDtypeStruct((DIN, D), BF16), jax.ShapeDtypeStruct((T, D), F32),
                   jax.ShapeDtypeStruct((T, D), BF16), jax.ShapeDtypeStruct((nt, 1, D), F32)],
        scratch_shapes=[pltpu.VMEM((DIN, D), F32)],
        compiler_params=_cp(("arbitrary",)), name="mix_in_bwd")(*dparts, win, nb, h, dh, gain)


def _head_masks():
    lane = lax.broadcasted_iota(jnp.int32, (1, 2 * HD), 1)
    m0 = lane < HD
    return m0, jnp.logical_not(m0)


def _head_rms(xv, m0):
    x2 = xv * xv
    s0 = jnp.sum(jnp.where(m0, x2, 0.0), axis=-1, keepdims=True)
    s1 = jnp.sum(jnp.where(m0, 0.0, x2), axis=-1, keepdims=True)
    return jnp.where(m0, lax.rsqrt(s0 * (1.0 / HD) + EPS), lax.rsqrt(s1 * (1.0 / HD) + EPS))


def _band_mask(first):
    qi = lax.broadcasted_iota(jnp.int32, (BLK, 2 * BLK), 0)
    ci = lax.broadcasted_iota(jnp.int32, (BLK, 2 * BLK), 1)
    band = (ci >= qi) & (ci <= qi + BLK)
    return band & ((ci >= BLK) | jnp.logical_not(first))


def _deinterleave(src, dst, i, d, S, off):
    L = S // d
    for r in range(d):
        if d == 1:
            dst[i, pl.ds(off, S), :] = src[...].astype(dst.dtype)
        else:
            dst[i, pl.ds(off + r * L, L), :] = src[pl.ds(r, L, stride=d), :].astype(dst.dtype)


def _attn_fwd(u, qg2, kg2, B, S):
    T = B * S
    NB = S // BLK
    scale = HD ** -0.5

    def body(q_ref, k_ref, v_ref, qg_ref, kg_ref, o_ref, lse_ref, tmp, qb, kb, vb, os_, ls_, on_, ln_):
        m0, m1 = _head_masks()
        qv = q_ref[...].astype(F32)
        tmp[...] = qv * _head_rms(qv, m0) * (qg_ref[...] * scale)
        for i, d in enumerate(DILS):
            _deinterleave(tmp, qb, i, d, S, 0)
        kv = k_ref[...].astype(F32)
        tmp[...] = kv * _head_rms(kv, m0) * kg_ref[...]
        for i, d in enumerate(DILS):
            _deinterleave(tmp, kb, i, d, S, BLK)
        tmp[...] = v_ref[...].astype(F32)
        for i, d in enumerate(DILS):
            _deinterleave(tmp, vb, i, d, S, BLK)
            kb[i, pl.ds(0, BLK), :] = jnp.zeros((BLK, 2 * HD), BF16)
            vb[i, pl.ds(0, BLK), :] = jnp.zeros((BLK, 2 * HD), BF16)

        for i, d in enumerate(DILS):
            seg = NB // d

            def blk(j, c, i=i, seg=seg):
                q0 = pl.multiple_of(j * BLK, BLK)
                qv_ = qb[i, pl.ds(q0, BLK), :]
                kk = kb[i, pl.ds(q0, 2 * BLK), :]
                vv = vb[i, pl.ds(q0, 2 * BLK), :]
                mask = _band_mask((j % seg) == 0)
                outs, lses = [], []
                for mh in (m0, m1):
                    qh = jnp.where(mh, qv_, jnp.zeros_like(qv_))
                    s = lax.dot_general(qh, kk, NT, preferred_element_type=F32)
                    s = jnp.where(mask, s, -1e30)
                    mx = jnp.max(s, axis=-1, keepdims=True)
                    p = jnp.exp(s - mx)
                    l = jnp.sum(p, axis=-1, keepdims=True)
                    outs.append(jnp.dot((p * (1.0 / l)).astype(BF16), vv, preferred_element_type=F32))
                    lses.append(mx + jnp.log(l))
                os_[i, pl.ds(q0, BLK), :] = jnp.where(m0, outs[0], outs[1])
                ls_[i, pl.ds(q0, BLK), :] = jnp.where(m0, lses[0], lses[1])
                return c

            lax.fori_loop(0, NB, blk, 0, unroll=4)

        for i, d in enumerate(DILS):
            if d == 1:
                continue
            L = S // d
            for r in range(d):
                on_[i - 1, pl.ds(r, L, stride=d), :] = os_[i, pl.ds(r * L, L), :]
                ln_[i - 1, pl.ds(r, L, stride=d), :] = ls_[i, pl.ds(r * L, L), :]

        def comb(c, carry):
            rows = pl.ds(pl.multiple_of(c * 256, 256), 256)
            l0, l1, l2 = ls_[0, rows, :], ln_[0, rows, :], ln_[1, rows, :]
            mx = jnp.maximum(jnp.maximum(l0, l1), l2)
            e0, e1, e2 = jnp.exp(l0 - mx), jnp.exp(l1 - mx), jnp.exp(l2 - mx)
            tot = e0 + e1 + e2
            inv = 1.0 / tot
            o = (e0 * os_[0, rows, :] + e1 * on_[0, rows, :] + e2 * on_[1, rows, :]) * inv
            o_ref[rows, :] = o.astype(BF16)
            lse_ref[rows, :] = mx + jnp.log(tot)
            return carry

        lax.fori_loop(0, S // 256, comb, 0)

    pair = 2 * HD
    blk_spec = lambda off: pl.BlockSpec((S, pair), lambda b, p, off=off: (b, off + p))
    gspec = pl.BlockSpec((1, pair), lambda b, p: (0, 0))
    return pl.pallas_call(
        body, grid=(B, DA // pair),
        in_specs=[blk_spec(0), blk_spec(DA // pair), blk_spec(2 * DA // pair), gspec, gspec],
        out_specs=[blk_spec(0), blk_spec(0)],
        out_shape=[jax.ShapeDtypeStruct((T, DA), BF16), jax.ShapeDtypeStruct((T, DA), F32)],
        scratch_shapes=[pltpu.VMEM((S, pair), F32),
                        pltpu.VMEM((3, S, pair), BF16),
                        pltpu.VMEM((3, BLK + S, pair), BF16),
                        pltpu.VMEM((3, BLK + S, pair), BF16),
                        pltpu.VMEM((3, S, pair), F32),
                        pltpu.VMEM((3, S, pair), F32),
                        pltpu.VMEM((2, S, pair), F32),
                        pltpu.VMEM((2, S, pair), F32)],
        compiler_params=_cp(("parallel", "parallel")), name="attn_fwd")(u, u, u, qg2, kg2)


def _attn_bwd(u, attn, dattn, lse, qg2, kg2, B, S, carry=None):
    T = B * S
    NB = S // BLK
    scale = HD ** -0.5
    pair = 2 * HD

    def body(q_ref, k_ref, v_ref, o_ref, do_ref, lse_ref, qg_ref, kg_ref,
             dq_ref, dk_ref, dv_ref, dgn_ref,
             tmp, qb, kb, vb, dob, lsb, dlb, dqs, dks, dvs, accq, acck, accv):
        m0, m1 = _head_masks()
        qv = q_ref[...].astype(F32)
        tmp[...] = qv * _head_rms(qv, m0) * (qg_ref[...] * scale)
        for i, d in enumerate(DILS):
            _deinterleave(tmp, qb, i, d, S, 0)
        kv = k_ref[...].astype(F32)
        tmp[...] = kv * _head_rms(kv, m0) * kg_ref[...]
        for i, d in enumerate(DILS):
            _deinterleave(tmp, kb, i, d, S, BLK)
        tmp[...] = v_ref[...].astype(F32)
        for i, d in enumerate(DILS):
            _deinterleave(tmp, vb, i, d, S, BLK)
            kb[i, pl.ds(0, BLK), :] = jnp.zeros((BLK, pair), BF16)
            vb[i, pl.ds(0, BLK), :] = jnp.zeros((BLK, pair), BF16)
        dov = do_ref[...].astype(F32)
        tmp[...] = dov
        for i, d in enumerate(DILS):
            _deinterleave(tmp, dob, i, d, S, 0)
        tmp[...] = lse_ref[...]
        for i, d in enumerate(DILS):
            _deinterleave(tmp, lsb, i, d, S, 0)
        prod = dov * o_ref[...].astype(F32)
        d0 = jnp.sum(jnp.where(m0, prod, 0.0), axis=-1, keepdims=True)
        d1 = jnp.sum(jnp.where(m0, 0.0, prod), axis=-1, keepdims=True)
        tmp[...] = jnp.where(m0, d0, d1)
        for i, d in enumerate(DILS):
            _deinterleave(tmp, dlb, i, d, S, 0)
        dks[...] = jnp.zeros_like(dks)
        dvs[...] = jnp.zeros_like(dvs)

        for i, d in enumerate(DILS):
            seg = NB // d

            def blk(j, c, i=i, seg=seg):
                q0 = pl.multiple_of(j * BLK, BLK)
                qv_ = qb[i, pl.ds(q0, BLK), :]
                kk = kb[i, pl.ds(q0, 2 * BLK), :]
                vv = vb[i, pl.ds(q0, 2 * BLK), :]
                dov_ = dob[i, pl.ds(q0, BLK), :]
                lsv = lsb[i, pl.ds(q0, BLK), :]
                dlv = dlb[i, pl.ds(q0, BLK), :]
                mask = _band_mask((j % seg) == 0)
                dq_acc = jnp.zeros((BLK, pair), F32)
                dk_acc = jnp.zeros((2 * BLK, pair), F32)
                dv_acc = jnp.zeros((2 * BLK, pair), F32)
                for hi, mh in enumerate((m0, m1)):
                    col = slice(hi * HD, hi * HD + 1)
                    qh = jnp.where(mh, qv_, jnp.zeros_like(qv_))
                    doh = jnp.where(mh, dov_, jnp.zeros_like(dov_))
                    s = lax.dot_general(qh, kk, NT, preferred_element_type=F32)
                    p = jnp.where(mask, jnp.exp(s - lsv[:, col]), 0.0)
                    dp = lax.dot_general(doh, vv, NT, preferred_element_type=F32)
                    ds = (p * (dp - dlv[:, col])).astype(BF16)
                    pb = p.astype(BF16)
                    dq_acc = dq_acc + jnp.where(mh, jnp.dot(ds, kk, preferred_element_type=F32), 0.0)
                    dk_acc = dk_acc + lax.dot_general(ds, qh, TN, preferred_element_type=F32)
                    dv_acc = dv_acc + lax.dot_general(pb, doh, TN, preferred_element_type=F32)
                dqs[i, pl.ds(q0, BLK), :] = dq_acc
                dks[i, pl.ds(q0, 2 * BLK), :] += dk_acc
                dvs[i, pl.ds(q0, 2 * BLK), :] += dv_acc
                return c

            lax.fori_loop(0, NB, blk, 0, unroll=4)

        accq[...] = dqs[0]
        acck[...] = dks[0, pl.ds(BLK, S), :]
        accv[...] = dvs[0, pl.ds(BLK, S), :]
        for i, d in enumerate(DILS):
            if d == 1:
                continue
            L = S // d
            for r in range(d):
                rows = pl.ds(r, L, stride=d)
                accq[rows, :] += dqs[i, pl.ds(r * L, L), :]
                acck[rows, :] += dks[i, pl.ds(BLK + r * L, L), :]
                accv[rows, :] += dvs[i, pl.ds(BLK + r * L, L), :]

        def head_mean(xv):
            a0 = jnp.sum(jnp.where(m0, xv, 0.0), axis=-1, keepdims=True)
            a1 = jnp.sum(jnp.where(m0, 0.0, xv), axis=-1, keepdims=True)
            return jnp.where(m0, a0, a1) * (1.0 / HD)

        def norm_bwd(x_ref, dn, gain):
            xv = x_ref[...].astype(F32)
            r = _head_rms(xv, m0)
            xhat = xv * r
            dxhat = dn * gain
            dx = r * (dxhat - xhat * head_mean(dxhat * xhat))
            return dx, jnp.sum(dn * xhat, axis=0, keepdims=True)

        dq, dgq = norm_bwd(q_ref, accq[...], qg_ref[...] * scale)
        dk, dgk = norm_bwd(k_ref, acck[...], kg_ref[...])
        dq_ref[...] = dq.astype(BF16)
        dk_ref[...] = dk.astype(BF16)
        dv_ref[...] = accv[...].astype(BF16)
        dgn_ref[...] = jnp.concatenate([dgq * scale, dgk, jnp.zeros((6, pair), F32)], axis=0)[None]

    blk_spec = lambda off: pl.BlockSpec((S, pair), lambda b, p, off=off: (b, off + p))
    gspec = pl.BlockSpec((1, pair), lambda b, p: (0, 0))
    np_ = DA // pair
    return _pallas(
        body, (u, u, u, attn, dattn, lse, qg2, kg2), grid=(B, np_),
        in_specs=[blk_spec(0), blk_spec(np_), blk_spec(2 * np_), blk_spec(0), blk_spec(0), blk_spec(0),
                  gspec, gspec],
        out_specs=[blk_spec(0), blk_spec(0), blk_spec(0),
                   pl.BlockSpec((1, 8, pair), lambda b, p: (b * np_ + p, 0, 0))],
        out_shape=[jax.ShapeDtypeStruct((T, DA), BF16)] * 3 + [jax.ShapeDtypeStruct((B * np_, 8, pair), F32)],
        scratch_shapes=[pltpu.VMEM((S, pair), F32),
                        pltpu.VMEM((3, S, pair), BF16),
                        pltpu.VMEM((3, BLK + S, pair), BF16),
                        pltpu.VMEM((3, BLK + S, pair), BF16),
                        pltpu.VMEM((3, S, pair), BF16),
                        pltpu.VMEM((3, S, pair), F32),
                        pltpu.VMEM((3, S, pair), F32),
                        pltpu.VMEM((3, S, pair), F32),
                        pltpu.VMEM((3, BLK + S, pair), F32),
                        pltpu.VMEM((3, BLK + S, pair), F32),
                        pltpu.VMEM((S, pair), F32),
                        pltpu.VMEM((S, pair), F32),
                        pltpu.VMEM((S, pair), F32)],
        sem=("parallel", "parallel"), name="attn_bwd", carry=carry)


CT = 32
CPAD = 32


def _ln_fwd(y, g, b):
    mu = jnp.mean(y, axis=-1, keepdims=True)
    yc = y - mu
    rstd = lax.rsqrt(jnp.mean(yc * yc, axis=-1, keepdims=True) + EPS)
    xhat = yc * rstd
    return xhat, rstd, xhat * g + b


def _fill_glu(ca_ref, cg_ref, glu, S):
    glu[pl.ds(0, CPAD), :] = jnp.zeros((CPAD, DC), F32)

    def fill(i, c):
        rows = pl.ds(pl.multiple_of(i * 256, 256), 256)
        a = ca_ref[rows, :].astype(F32)
        gt = cg_ref[rows, :].astype(F32)
        glu[pl.ds(pl.multiple_of(CPAD + i * 256, CT), 256), :] = a * _sigmoid(gt)
        return c

    lax.fori_loop(0, S // 256, fill, 0)


def _conv_fwd(u, cw, cb, lg, lb, B, S):
    T = B * S

    def body(ca_ref, cg_ref, w_ref, b_ref, lg_ref, lb_ref, o_ref, y_ref, glu):
        _fill_glu(ca_ref, cg_ref, glu, S)

        def step(i, c):
            t0 = pl.multiple_of(i * CT, CT)
            win = glu[pl.ds(t0, 2 * CT), :]
            acc = jnp.zeros((CT, DC), F32) + b_ref[...]
            for k in range(CK):
                acc = acc + win[k + 2:k + 2 + CT, :] * w_ref[k:k + 1, :]
            y_ref[pl.ds(t0, CT), :] = acc
            _, _, z = _ln_fwd(acc, lg_ref[...], lb_ref[...])
            o_ref[pl.ds(t0, CT), :] = (z * _sigmoid(z)).astype(BF16)
            return c

        lax.fori_loop(0, S // CT, step, 0)

    vec = pl.BlockSpec((1, DC), lambda b: (0, 0))
    return pl.pallas_call(
        body, grid=(B,),
        in_specs=[pl.BlockSpec((S, DC), lambda b: (b, 3)), pl.BlockSpec((S, DC), lambda b: (b, 4)),
                  pl.BlockSpec((CT, DC), lambda b: (0, 0)), vec, vec, vec],
        out_specs=[pl.BlockSpec((S, DC), lambda b: (b, 0))] * 2,
        out_shape=[jax.ShapeDtypeStruct((T, DC), BF16), jax.ShapeDtypeStruct((T, DC), F32)],
        scratch_shapes=[pltpu.VMEM((CPAD + S, DC), F32)],
        compiler_params=_cp(("parallel",)), name="conv_fwd")(u, u, cw, cb, lg, lb)


def _conv_bwd(u, y, dconv, cw, lg, lb, B, S):
    T = B * S

    def body(ca_ref, cg_ref, y_ref, dc_ref, w_ref, lg_ref, lb_ref,
             dca_ref, dcg_ref, dw_ref, ds_ref, glu, dyp, dwacc):
        _fill_glu(ca_ref, cg_ref, glu, S)
        dyp[pl.ds(S, CPAD), :] = jnp.zeros((CPAD, DC), F32)
        lgv, lbv = lg_ref[...], lb_ref[...]

        def sum8(v):
            return v[0:8] + v[8:16] + v[16:24] + v[24:32]

        def p1(i, carry):
            sb, sg, sl = carry
            t0 = pl.multiple_of(i * CT, CT)
            xhat, rstd, z = _ln_fwd(y_ref[pl.ds(t0, CT), :], lgv, lbv)
            sz = _sigmoid(z)
            dz = dc_ref[pl.ds(t0, CT), :].astype(F32) * (sz * (1.0 + z * (1.0 - sz)))
            dxhat = dz * lgv
            dy = rstd * (dxhat - jnp.mean(dxhat, axis=-1, keepdims=True)
                         - xhat * jnp.mean(dxhat * xhat, axis=-1, keepdims=True))
            dyp[pl.ds(t0, CT), :] = dy
            return sb + sum8(dy), sg + sum8(dz * xhat), sl + sum8(dz)

        z8 = jnp.zeros((8, DC), F32)
        sb, sg, sl = lax.fori_loop(0, S // CT, p1, (z8, z8, z8))
        rs = lambda v: jnp.sum(v, axis=0, keepdims=True)
        ds_ref[...] = jnp.concatenate([rs(sb), rs(sg), rs(sl), jnp.zeros((5, DC), F32)], axis=0)[None]

        def p2(i, c):
            t0 = pl.multiple_of(i * CT, CT)
            win = dyp[pl.ds(t0, 2 * CT), :]
            acc = jnp.zeros((CT, DC), F32)
            for k in range(CK):
                acc = acc + win[30 - k:30 - k + CT, :] * w_ref[k:k + 1, :]
            a = ca_ref[pl.ds(t0, CT), :].astype(F32)
            sgt = _sigmoid(cg_ref[pl.ds(t0, CT), :].astype(F32))
            dca_ref[pl.ds(t0, CT), :] = (acc * sgt).astype(BF16)
            dcg_ref[pl.ds(t0, CT), :] = (acc * a * sgt * (1.0 - sgt)).astype(BF16)
            return c

        lax.fori_loop(0, S // CT, p2, 0)

        dwacc[...] = jnp.zeros_like(dwacc)

        def p3(i, c):
            t0 = pl.multiple_of(i * CT, CT)
            win = glu[pl.ds(t0, 2 * CT), :]
            dy = dyp[pl.ds(t0, CT), :]
            for k in range(CK):
                dwacc[k] += sum8(dy * win[k + 2:k + 2 + CT, :])
            return c

        lax.fori_loop(0, S // CT, p3, 0)
        dw_ref[...] = jnp.sum(dwacc[...], axis=1)[None]

    vec = pl.BlockSpec((1, DC), lambda b: (0, 0))
    seq = pl.BlockSpec((S, DC), lambda b: (b, 0))
    return pl.pallas_call(
        body, grid=(B,),
        in_specs=[pl.BlockSpec((S, DC), lambda b: (b, 3)), pl.BlockSpec((S, DC), lambda b: (b, 4)),
                  seq, seq, pl.BlockSpec((CT, DC), lambda b: (0, 0)), vec, vec],
        out_specs=[seq, seq, pl.BlockSpec((1, CT, DC), lambda b: (b, 0, 0)),
                   pl.BlockSpec((1, 8, DC), lambda b: (b, 0, 0))],
        out_shape=[jax.ShapeDtypeStruct((T, DC), BF16)] * 2
                  + [jax.ShapeDtypeStruct((B, CT, DC), F32), jax.ShapeDtypeStruct((B, 8, DC), F32)],
        scratch_shapes=[pltpu.VMEM((CPAD + S, DC), F32), pltpu.VMEM((S + CPAD, DC), F32),
                        pltpu.VMEM((CT, 8, DC), F32)],
        compiler_params=_cp(("parallel",)), name="conv_bwd")(u, u, y, dconv, cw, lg, lb)


def _local_step(x, target, norms, W, B, S, comm=None):
    qg2 = jnp.concatenate([norms["q_norm"], norms["q_norm"]], axis=1)
    kg2 = jnp.concatenate([norms["k_norm"], norms["k_norm"]], axis=1)
    cw = jnp.concatenate([W["conv_w"], jnp.zeros((1, DC), F32)], axis=0)

    (h1, n1, g1, u1), got = _ffn_fwd(x, norms["ffn1_norm"], W["wg1"], W["wu1"], W["wd1"], None, "ffn1_fwd",
                                     carry=comm.gather_rest if comm else None)
    if comm:
        W = dict(W, **comm.gathered_rest(got))
    u, n2 = _mix_in(h1, norms["mix_norm"], W["win"])
    attn, lse = _attn_fwd(u, qg2, kg2, B, S)
    conv, y = _conv_fwd(u, cw, norms["conv_b"], norms["conv_ln_g"], norms["conv_ln_b"], B, S)
    h2 = _mix_out(h1, attn, conv, W["wout"])
    (h3, n3, g2, u2, dout, dyb, sq), _ = _ffn_fwd(h2, norms["ffn2_norm"], W["wg2"], W["wu2"], W["wd2"], target,
                                                 "ffn2_fwd")
    del h3
    loss = (0.5 / D) * jnp.sum(sq)

    (dwg2, dwu2, dwd2, dn3), _ = _ffn_bwd(dyb, n3, g2, u2, W["wg2"], W["wu2"], W["wd2"], "ffn2_bwd")
    carry = comm.reduce_start("ffn2", {"wg2": dwg2, "wu2": dwu2, "wd2": dwd2}) if comm else None
    dh2, dgn_ffn2 = _norm_bwd(dn3, h2, dout, norms["ffn2_norm"], "ffn2_norm_bwd")
    dattn, dconv, dwout = _mix_out_bwd(dh2, attn, conv, W["wout"])
    (dq, dk, dv, dgn_qk), got = _attn_bwd(u, attn, dattn, lse, qg2, kg2, B, S, carry=carry)
    if comm:
        comm.reduce_done(carry, got)
    dca, dcg, dcw, dcs = _conv_bwd(u, y, dconv, cw, norms["conv_ln_g"], norms["conv_ln_b"], B, S)
    dwin, dh1, dyb1, dgn_mix = _mix_in_bwd((dq, dk, dv, dca, dcg), W["win"], n2, h1, dh2, norms["mix_norm"])
    carry = comm.reduce_start("mix", {"win": dwin, "wout": dwout}) if comm else None
    (dwg1, dwu1, dwd1, dn1), got = _ffn_bwd(dyb1, n1, g1, u1, W["wg1"], W["wu1"], W["wd1"], "ffn1_bwd", carry=carry)
    if comm:
        comm.reduce_done(carry, got)
        comm.reduce_now("ffn1", {"wg1": dwg1, "wu1": dwu1, "wd1": dwd1})
    gx, dgn_ffn1 = _norm_bwd(dn1, x, dh1, norms["ffn1_norm"], "ffn1_norm_bwd")

    qk = jnp.sum(dgn_qk, axis=0)
    cs = jnp.sum(dcs, axis=0)
    small = {
        "ffn1_norm": jnp.sum(dgn_ffn1, axis=0),
        "mix_norm": jnp.sum(dgn_mix, axis=0),
        "q_norm": qk[0:1, 0:HD] + qk[0:1, HD:2 * HD],
        "k_norm": qk[1:2, 0:HD] + qk[1:2, HD:2 * HD],
        "conv_w": jnp.sum(dcw, axis=0)[0:CK],
        "conv_b": cs[0:1],
        "conv_ln_g": cs[1:2],
        "conv_ln_b": cs[2:3],
        "ffn2_norm": jnp.sum(dgn_ffn2, axis=0),
    }
    big = {"wg1": dwg1, "wu1": dwu1, "wd1": dwd1, "win": dwin, "wout": dwout,
           "wg2": dwg2, "wu2": dwu2, "wd2": dwd2}
    return loss, gx, big, small


HBM = pl.BlockSpec(memory_space=pltpu.HBM)
VMEM = pl.BlockSpec(memory_space=pltpu.VMEM)


def _place():
    return lax.axis_index("x"), lax.axis_index("y"), lax.axis_index("c")


class _GatherCarry:
    def __init__(self, shards):
        nt = len(shards)
        self.shards = shards
        self.in_arrays = [s for s, _ in shards]
        self.in_specs = [VMEM] * nt
        self.out_shape = [jax.ShapeDtypeStruct((NDEV * s.shape[0], s.shape[1]), dt) for s, dt in shards]
        self.out_specs = [HBM] * nt
        self.scratch = ([pltpu.VMEM(s.shape, dt) for s, dt in shards]
                        + [pltpu.SemaphoreType.DMA((nt, 7)), pltpu.SemaphoreType.DMA((nt, 7)),
                           pltpu.SemaphoreType.DMA((nt,))])

    def _copies(self, outs, scr):
        nt = len(self.shards)
        stages = scr[:nt]
        send_sems, recv_sems, local_sems = scr[nt:]
        x, y, c = _place()
        me, sibling = (x, y, c), (x, y, 1 - c)
        chips = [(1 - x, y), (x, 1 - y), (1 - x, 1 - y)]

        def rows(t, px, py, pc):
            r = self.shards[t][0].shape[0]
            return outs[t].at[pl.ds((4 * px + 2 * py + pc) * r, r), :]

        def copy(t, k, block, to, src=None):
            return pltpu.make_async_remote_copy(
                src_ref=rows(t, *block) if src is None else src, dst_ref=rows(t, *block),
                send_sem=send_sems.at[t, k], recv_sem=recv_sems.at[t, k],
                device_id=to, device_id_type=MESH)

        mine = [pltpu.make_async_copy(stages[t], rows(t, *me), local_sems.at[t]) for t in range(nt)]
        first = []
        for t in range(nt):
            first.append(copy(t, 0, me, sibling, src=stages[t]))
            first += [copy(t, 1 + j, me, (*chip, c), src=stages[t]) for j, chip in enumerate(chips)]
        return copy, mine, first, me, sibling, chips, c

    def start(self, ins, outs, scr):
        _, mine, first, *_ = self._copies(outs, scr)
        for t, (_, dt) in enumerate(self.shards):
            scr[t][...] = ins[t][...].astype(dt)
        for cp in mine + first:
            cp.start()

    def finish(self, ins, outs, scr):
        copy, mine, first, me, sibling, chips, c = self._copies(outs, scr)
        nt = len(self.shards)
        passed = []
        for j, chip in enumerate(chips):
            for t in range(nt):
                copy(t, 1 + j, (*chip, c), me).wait_recv()
                cp = copy(t, 4 + j, (*chip, c), sibling)
                cp.start()
                passed.append(cp)
        for t in range(nt):
            copy(t, 0, sibling, me).wait_recv()
            for j, chip in enumerate(chips):
                copy(t, 4 + j, (*chip, 1 - c), me).wait_recv()
        for cp in first + passed:
            cp.wait_send()
        for cp in mine:
            cp.wait()


def _run_carry(carry, name):
    def body(*refs):
        n_in, n_out = len(carry.in_arrays), len(carry.out_shape)
        ins, outs, scr = refs[:n_in], refs[n_in:n_in + n_out], refs[n_in + n_out:]
        carry.start(ins, outs, scr)
        carry.finish(ins, outs, scr)

    return pl.pallas_call(
        body, in_specs=carry.in_specs, out_specs=carry.out_specs, out_shape=carry.out_shape,
        scratch_shapes=carry.scratch, compiler_params=pltpu.CompilerParams(vmem_limit_bytes=VMEM_LIMIT),
        name=name)(*carry.in_arrays)


def _sibling_exchange(grads, name):
    nt = len(grads)
    g4 = [g.reshape(4, 2, g.shape[0] // NDEV, g.shape[1]) for g in grads]

    def body(*refs):
        ins, outs = refs[:nt], refs[nt:2 * nt]
        send_sems, recv_sems = refs[2 * nt:]
        x, y, c = _place()
        cps = []
        for t in range(nt):
            cp = pltpu.make_async_remote_copy(
                src_ref=ins[t].at[:, 1 - c], dst_ref=outs[t],
                send_sem=send_sems.at[t], recv_sem=recv_sems.at[t],
                device_id=(x, y, 1 - c), device_id_type=MESH)
            cp.start()
            cps.append(cp)
        for cp in cps:
            cp.wait()

    return pl.pallas_call(
        body, in_specs=[HBM] * nt, out_specs=[HBM] * nt,
        out_shape=[jax.ShapeDtypeStruct((4,) + g.shape[2:], BF16) for g in g4],
        scratch_shapes=[pltpu.SemaphoreType.DMA((nt,)), pltpu.SemaphoreType.DMA((nt,))],
        name=name)(*g4), g4


def _chip_partial(g4, recv, cidx, name):
    _, _, rows, n = g4.shape

    def body(c_ref, a_ref, b_ref, o_ref):
        o_ref[...] = (a_ref[...].astype(F32) + b_ref[...].astype(F32)).astype(BF16)

    return pl.pallas_call(
        body,
        grid_spec=pltpu.PrefetchScalarGridSpec(
            num_scalar_prefetch=1, grid=(4,),
            in_specs=[pl.BlockSpec((None, None, rows, n), lambda q, c_ref: (q, c_ref[0], 0, 0)),
                      pl.BlockSpec((None, rows, n), lambda q, c_ref: (q, 0, 0))],
            out_specs=pl.BlockSpec((None, rows, n), lambda q, c_ref: (q, 0, 0))),
        out_shape=jax.ShapeDtypeStruct((4, rows, n), BF16),
        compiler_params=_cp(("parallel",)), name=name)(cidx, g4, recv)


class _ExchangeCarry:
    def __init__(self, names, parts):
        nt = len(parts)
        self.names = names
        self.in_arrays = list(parts)
        self.in_specs = [HBM] * nt
        self.out_shape = [jax.ShapeDtypeStruct(p.shape, BF16) for p in parts]
        self.out_specs = [HBM] * nt
        self.scratch = [pltpu.SemaphoreType.DMA((nt, 3)), pltpu.SemaphoreType.DMA((nt, 3)),
                        pltpu.SemaphoreType.DMA((nt,))]

    def _copies(self, ins, outs, scr):
        send_sems, recv_sems, local_sems = scr
        x, y, c = _place()
        qme = 2 * x + y
        chips = [(1 - x, y), (x, 1 - y), (1 - x, 1 - y)]
        cps = []
        for t in range(len(ins)):
            cps.append(pltpu.make_async_copy(ins[t].at[qme], outs[t].at[qme], local_sems.at[t]))
            for k, (cx, cy) in enumerate(chips):
                cps.append(pltpu.make_async_remote_copy(
                    src_ref=ins[t].at[2 * cx + cy], dst_ref=outs[t].at[qme],
                    send_sem=send_sems.at[t, k], recv_sem=recv_sems.at[t, k],
                    device_id=(cx, cy, c), device_id_type=MESH))
        return cps

    def start(self, ins, outs, scr):
        for cp in self._copies(ins, outs, scr):
            cp.start()

    def finish(self, ins, outs, scr):
        for cp in self._copies(ins, outs, scr):
            cp.wait()


class _Comm:
    def __init__(self, rest_names, rest_shards, cidx):
        self.rest_names = rest_names
        self.gather_rest = _GatherCarry(rest_shards)
        self.cidx = cidx
        self.reduced = {}

    def gathered_rest(self, outs):
        return dict(zip(self.rest_names, outs))

    def reduce_start(self, tag, grads):
        names = list(grads)
        recv, g4 = _sibling_exchange([grads[n] for n in names], "sibling_exchange_" + tag)
        parts = [_chip_partial(g4[i], recv[i], self.cidx, "chip_partial_" + n) for i, n in enumerate(names)]
        return _ExchangeCarry(names, parts)

    def reduce_done(self, carry, outs):
        self.reduced.update(zip(carry.names, outs))

    def reduce_now(self, tag, grads):
        carry = self.reduce_start(tag, grads)
        self.reduce_done(carry, _run_carry(carry, "chip_exchange_" + tag))


def _adamw_math(w, g, m, v):
    m = B1 * m + (1.0 - B1) * g
    v = B2 * v + (1.0 - B2) * (g * g)
    m_hat = m / (1.0 - B1 ** STEP)
    v_hat = v / (1.0 - B2 ** STEP)
    delta = -LR * (m_hat / (jnp.sqrt(v_hat) + AEPS) + WD * w)
    return delta, m, v


def _adamw_big(recv, w, m, v, name):
    def body(r_ref, w_ref, m_ref, v_ref, g_ref, d_ref, mo_ref, vo_ref):
        g = r_ref[0].astype(F32)
        for q in range(1, 4):
            g = g + r_ref[q].astype(F32)
        d, mn, vn = _adamw_math(w_ref[...], g, m_ref[...], v_ref[...])
        g_ref[...] = g
        d_ref[...] = d
        mo_ref[...] = mn
        vo_ref[...] = vn

    return pl.pallas_call(
        body, out_shape=[jax.ShapeDtypeStruct(w.shape, F32)] * 4,
        compiler_params=pltpu.CompilerParams(vmem_limit_bytes=VMEM_LIMIT), name=name)(recv, w, m, v)


SMALL_ROWS = 40
CW_ROWS = 16
SMALL_TOTAL = SMALL_ROWS + NDEV * CW_ROWS


def _small_step(gvec, wvec, mvec, vvec):
    nr = SMALL_ROWS + CW_ROWS

    def body(g_ref, w_ref, m_ref, v_ref, go_ref, d_ref, mo_ref, vo_ref, slots, send_sems, recv_sems):
        x, y, c = _place()
        me = 4 * x + 2 * y + c
        slots[me] = g_ref[...]
        cps = []
        for k in range(1, NDEV):
            fx, fy, fc = (k >> 2) & 1, (k >> 1) & 1, k & 1
            peer = (x ^ fx, y ^ fy, c ^ fc)
            cp = pltpu.make_async_remote_copy(
                src_ref=g_ref, dst_ref=slots.at[me],
                send_sem=send_sems.at[k - 1], recv_sem=recv_sems.at[k - 1],
                device_id=peer, device_id_type=MESH)
            cp.start()
            cps.append(cp)
        for cp in cps:
            cp.wait()
        tot = slots[0]
        for j in range(1, NDEV):
            tot = tot + slots[j]
        slots[0] = tot
        g = jnp.concatenate(
            [tot[0:SMALL_ROWS], slots[0, pl.ds(pl.multiple_of(SMALL_ROWS + me * CW_ROWS, 8), CW_ROWS), :]], axis=0)
        d, mn, vn = _adamw_math(w_ref[...], g, m_ref[...], v_ref[...])
        go_ref[...] = g
        d_ref[...] = d
        mo_ref[...] = mn
        vo_ref[...] = vn

    return pl.pallas_call(
        body, in_specs=[VMEM] * 4, out_specs=[VMEM] * 4,
        out_shape=[jax.ShapeDtypeStruct((nr, 128), F32)] * 4,
        scratch_shapes=[pltpu.VMEM((NDEV, SMALL_TOTAL, 128), F32),
                        pltpu.SemaphoreType.DMA((NDEV - 1,)), pltpu.SemaphoreType.DMA((NDEV - 1,))],
        name="small_step")(gvec, wvec, mvec, vvec)


SMALL_NAMES = ("ffn1_norm", "mix_norm", "ffn2_norm", "conv_b", "conv_ln_g", "conv_ln_b", "q_norm", "k_norm")


def _pack_small(vals):
    parts = []
    for n in SMALL_NAMES:
        a = vals[n].reshape(-1)
        if a.shape[0] < 128:
            a = jnp.concatenate([a, jnp.zeros((128 - a.shape[0],), F32)])
        parts.append(a.reshape(-1, 128))
    used = sum(p.shape[0] for p in parts)
    parts.append(jnp.zeros((SMALL_ROWS - used, 128), F32))
    return jnp.concatenate(parts, axis=0)


def _unpack_small(vec, shapes):
    out, r = {}, 0
    for n in SMALL_NAMES:
        size = shapes[n][1]
        nr = max(size // 128, 1)
        out[n] = vec[r:r + nr].reshape(-1)[:size].reshape(1, size)
        r += nr
    return out


def _pack_cw(a):
    flat = a.reshape(-1)
    return jnp.concatenate([flat, jnp.zeros((CW_ROWS * 128 - flat.shape[0],), F32)]).reshape(CW_ROWS, 128)


def _unpack_cw(v):
    return v.reshape(-1)[:CK * HD].reshape(1, CK, HD)


def kernel(x, ffn1_norm, ffn1_w_gate, ffn1_w_up, ffn1_w_down, mix_norm, w_in, q_norm, k_norm, conv_w, conv_b, conv_ln_g, conv_ln_b, w_out, ffn2_norm, ffn2_w_gate, ffn2_w_up, ffn2_w_down, loss_target, m_ffn1_norm, m_ffn1_w_gate, m_ffn1_w_up, m_ffn1_w_down, m_mix_norm, m_w_in, m_q_norm, m_k_norm, m_conv_w, m_conv_b, m_conv_ln_g, m_conv_ln_b, m_w_out, m_ffn2_norm, m_ffn2_w_gate, m_ffn2_w_up, m_ffn2_w_down, v_ffn1_norm, v_ffn1_w_gate, v_ffn1_w_up, v_ffn1_w_down, v_mix_norm, v_w_in, v_q_norm, v_k_norm, v_conv_w, v_conv_b, v_conv_ln_g, v_conv_ln_b, v_w_out, v_ffn2_norm, v_ffn2_w_gate, v_ffn2_w_up, v_ffn2_w_down):
    P = dict(ffn1_norm=ffn1_norm, ffn1_w_gate=ffn1_w_gate, ffn1_w_up=ffn1_w_up, ffn1_w_down=ffn1_w_down,
             mix_norm=mix_norm, w_in=w_in, q_norm=q_norm, k_norm=k_norm, conv_w=conv_w, conv_b=conv_b,
             conv_ln_g=conv_ln_g, conv_ln_b=conv_ln_b, w_out=w_out, ffn2_norm=ffn2_norm,
             ffn2_w_gate=ffn2_w_gate, ffn2_w_up=ffn2_w_up, ffn2_w_down=ffn2_w_down)
    M = dict(ffn1_norm=m_ffn1_norm, ffn1_w_gate=m_ffn1_w_gate, ffn1_w_up=m_ffn1_w_up, ffn1_w_down=m_ffn1_w_down,
             mix_norm=m_mix_norm, w_in=m_w_in, q_norm=m_q_norm, k_norm=m_k_norm, conv_w=m_conv_w, conv_b=m_conv_b,
             conv_ln_g=m_conv_ln_g, conv_ln_b=m_conv_ln_b, w_out=m_w_out, ffn2_norm=m_ffn2_norm,
             ffn2_w_gate=m_ffn2_w_gate, ffn2_w_up=m_ffn2_w_up, ffn2_w_down=m_ffn2_w_down)
    V = dict(ffn1_norm=v_ffn1_norm, ffn1_w_gate=v_ffn1_w_gate, ffn1_w_up=v_ffn1_w_up, ffn1_w_down=v_ffn1_w_down,
             mix_norm=v_mix_norm, w_in=v_w_in, q_norm=v_q_norm, k_norm=v_k_norm, conv_w=v_conv_w, conv_b=v_conv_b,
             conv_ln_g=v_conv_ln_g, conv_ln_b=v_conv_ln_b, w_out=v_w_out, ffn2_norm=v_ffn2_norm,
             ffn2_w_gate=v_ffn2_w_gate, ffn2_w_up=v_ffn2_w_up, ffn2_w_down=v_ffn2_w_down)
    order = ["ffn1_norm", "ffn1_w_gate", "ffn1_w_up", "ffn1_w_down", "mix_norm", "w_in", "q_norm", "k_norm",
             "conv_w", "conv_b", "conv_ln_g", "conv_ln_b", "w_out", "ffn2_norm", "ffn2_w_gate", "ffn2_w_up",
             "ffn2_w_down"]
    B, S, _ = x.shape
    T = B * S
    cidx = lax.axis_index("c").astype(jnp.int32).reshape(1)

    bigs = [("wg1", "ffn1_w_gate", True), ("wu1", "ffn1_w_up", True), ("wd1", "ffn1_w_down", False),
            ("win", "w_in", True), ("wout", "w_out", False),
            ("wg2", "ffn2_w_gate", True), ("wu2", "ffn2_w_up", True), ("wd2", "ffn2_w_down", False)]
    hm = lambda a, tr: jnp.transpose(a[0]) if tr else a[0]
    cw_pad = jnp.zeros((32, 128), F32).at[0:CK, 0:HD].set(conv_w[0])
    now, rest = bigs[:3], bigs[3:]
    gathered = _run_carry(_GatherCarry([(hm(P[pn], tr), BF16) for _, pn, tr in now] + [(cw_pad, F32)]),
                          "gather_first")
    W = {ln: gathered[i] for i, (ln, _, _) in enumerate(now)}
    cwg = gathered[-1].reshape(NDEV, 32, 128)[:, 0:CK, 0:HD]
    W["conv_w"] = jnp.transpose(cwg, (1, 0, 2)).reshape(CK, DC)
    norms = {n: P[n] for n in SMALL_NAMES}
    comm = _Comm([ln for ln, _, _ in rest], [(hm(P[pn], tr), BF16) for _, pn, tr in rest], cidx)

    loss, gx, _, small = _local_step(x.reshape(T, D), loss_target.reshape(T, D), norms, W, B, S, comm)
    loss = lax.psum(loss, ("x", "y", "c"))

    G, Dl, Mn, Vn = {}, {}, {}, {}
    for ln, pn, tr in bigs:
        outs = _adamw_big(comm.reduced[ln], hm(P[pn], tr), hm(M[pn], tr), hm(V[pn], tr), "adamw_" + ln)
        G[pn], Dl[pn], Mn[pn], Vn[pn] = [(jnp.transpose(o) if tr else o)[None] for o in outs]

    dcw = small["conv_w"].reshape(CK, NDEV, HD).transpose(1, 0, 2)
    gvec = jnp.concatenate([_pack_small(small)] + [_pack_cw(dcw[j]) for j in range(NDEV)], axis=0)
    pack = lambda dct: jnp.concatenate([_pack_small({n: dct[n] for n in SMALL_NAMES}), _pack_cw(dct["conv_w"][0])], axis=0)
    go, do, mo, vo = _small_step(gvec, pack(P), pack(M), pack(V))
    shapes = {n: P[n].shape for n in SMALL_NAMES}
    for dst, vec in ((G, go), (Dl, do), (Mn, mo), (Vn, vo)):
        dst.update(_unpack_small(vec[0:SMALL_ROWS], shapes))
        dst["conv_w"] = _unpack_cw(vec[SMALL_ROWS:])

    return (loss, gx.reshape(B, S, D), *[G[n] for n in order], *[Dl[n] for n in order],
            *[Mn[n] for n in order], *[Vn[n] for n in order])
```

```python
import functools

import jax
import jax.numpy as jnp
from jax import lax
from jax.experimental import pallas as pl
from jax.experimental.pallas import tpu as pltpu

F32 = jnp.float32
BF16 = jnp.bfloat16

D = 1024
FF = 2816
HD = 64
DA = 512
DC = 512
DIN = 2560
CK = 31
BLK = 128
DILS = (1, 4, 16)
EPS = 1e-6
NDEV = 8
MESH = pl.DeviceIdType.MESH

LR, B1, B2, AEPS, WD, STEP = 0.001, 0.9, 0.999, 1e-08, 0.01, 10

NT = (((1,), (1,)), ((), ()))
TN = (((0,), (0,)), ((), ()))

VMEM_LIMIT = 56 * 1024 * 1024


def _cp(sem=None):
    return pltpu.CompilerParams(dimension_semantics=sem, vmem_limit_bytes=VMEM_LIMIT)


def _sigmoid(x):
    return 0.5 * (jnp.tanh(0.5 * x) + 1.0)


def _pallas(body, args, *, grid, in_specs, out_specs, out_shape, scratch_shapes, sem, name, carry=None):
    if carry is None:
        outs = pl.pallas_call(body, grid=grid, in_specs=in_specs, out_specs=out_specs, out_shape=out_shape,
                              scratch_shapes=scratch_shapes, compiler_params=_cp(sem), name=name)(*args)
        return outs, None
    n_in, n_out, n_scr = len(in_specs), len(out_shape), len(scratch_shapes)
    c_in, c_out = len(carry.in_arrays), len(carry.out_shape)

    def wrapped(*refs):
        ins, refs = refs[:n_in], refs[n_in:]
        cins, refs = refs[:c_in], refs[c_in:]
        outs, refs = refs[:n_out], refs[n_out:]
        couts, refs = refs[:c_out], refs[c_out:]
        scr, cscr = refs[:n_scr], refs[n_scr:]
        ids = [pl.program_id(a) for a in range(len(grid))]
        is_first = functools.reduce(jnp.logical_and, [i == 0 for i in ids])
        is_last = functools.reduce(jnp.logical_and, [i == n - 1 for i, n in zip(ids, grid)])

        @pl.when(is_first)
        def _():
            carry.start(cins, couts, cscr)

        body(*ins, *outs, *scr)

        @pl.when(is_last)
        def _():
            carry.finish(cins, couts, cscr)

    outs = pl.pallas_call(
        wrapped, grid=grid, in_specs=list(in_specs) + carry.in_specs, out_specs=list(out_specs) + carry.out_specs,
        out_shape=list(out_shape) + carry.out_shape, scratch_shapes=list(scratch_shapes) + carry.scratch,
        compiler_params=_cp(("arbitrary",) * len(grid)), name=name)(*args, *carry.in_arrays)
    return outs[:n_out], outs[n_out:]


def _ffn_fwd(x, gain, wg, wu, wd, target, name, carry=None):
    T = x.shape[0]
    tm, tf = 1024, 256
    nt, nf = T // tm, FF // tf
    with_loss = target is not None

    def body(*refs):
        if with_loss:
            (x_ref, gain_ref, wg_ref, wu_ref, wd_ref, t_ref,
             h_ref, n_ref, g_ref, u_ref, dout_ref, dyb_ref, sq_ref, nb_scr, acc_scr) = refs
        else:
            (x_ref, gain_ref, wg_ref, wu_ref, wd_ref,
             h_ref, n_ref, g_ref, u_ref, nb_scr, acc_scr) = refs
        f = pl.program_id(1)

        @pl.when(f == 0)
        def _():
            xv = x_ref[...]
            r = lax.rsqrt(jnp.mean(xv * xv, axis=-1, keepdims=True) + EPS)
            nb = (xv * r * gain_ref[...]).astype(BF16)
            nb_scr[...] = nb
            n_ref[...] = nb
            acc_scr[...] = jnp.zeros_like(acc_scr)

        nb = nb_scr[...]
        g = lax.dot_general(nb, wg_ref[...], NT, preferred_element_type=F32)
        u = lax.dot_general(nb, wu_ref[...], NT, preferred_element_type=F32)
        a = g * _sigmoid(g) * u
        g_ref[...] = g.astype(BF16)
        u_ref[...] = u.astype(BF16)
        acc_scr[...] += jnp.dot(a.astype(BF16), wd_ref[...], preferred_element_type=F32)

        @pl.when(f == nf - 1)
        def _():
            h = x_ref[...] + 0.5 * acc_scr[...]
            h_ref[...] = h
            if with_loss:
                e = h - t_ref[...]
                dout = e * (1.0 / D)
                dout_ref[...] = dout
                dyb_ref[...] = (0.5 * dout).astype(BF16)
                sq_ref[...] = jnp.sum(e * e, axis=0, keepdims=True)[None]

    row = pl.BlockSpec((tm, D), lambda t, f: (t, 0))
    wspec = pl.BlockSpec((tf, D), lambda t, f: (f, 0))
    gspec = pl.BlockSpec((tm, tf), lambda t, f: (t, f))
    in_specs = [row, pl.BlockSpec((1, D), lambda t, f: (0, 0)), wspec, wspec, wspec]
    out_shape = [jax.ShapeDtypeStruct((T, D), F32), jax.ShapeDtypeStruct((T, D), BF16),
                 jax.ShapeDtypeStruct((T, FF), BF16), jax.ShapeDtypeStruct((T, FF), BF16)]
    out_specs = [row, row, gspec, gspec]
    args = [x, gain, wg, wu, wd]
    if with_loss:
        in_specs.append(row)
        args.append(target)
        out_shape += [jax.ShapeDtypeStruct((T, D), F32), jax.ShapeDtypeStruct((T, D), BF16),
                      jax.ShapeDtypeStruct((nt, 1, D), F32)]
        out_specs += [row, row, pl.BlockSpec((1, 1, D), lambda t, f: (t, 0, 0))]
    return _pallas(
        body, args, grid=(nt, nf), in_specs=in_specs, out_specs=out_specs, out_shape=out_shape,
        scratch_shapes=[pltpu.VMEM((tm, D), BF16), pltpu.VMEM((tm, D), F32)],
        sem=("parallel", "arbitrary"), name=name, carry=carry)


def _ffn_bwd(dyb, nb, g, u, wg, wu, wd, name, carry=None):
    T = dyb.shape[0]
    tm, tf = 512, 256
    nt, nf = T // tm, FF // tf

    def body(dy_ref, n_ref, g_ref, u_ref, wg_ref, wu_ref, wd_ref,
             dwg_ref, dwu_ref, dwd_ref, dn_ref, ag_scr, au_scr, ad_scr):
        f, t = pl.program_id(0), pl.program_id(1)

        @pl.when(t == 0)
        def _():
            ag_scr[...] = jnp.zeros_like(ag_scr)
            au_scr[...] = jnp.zeros_like(au_scr)
            ad_scr[...] = jnp.zeros_like(ad_scr)

        dy = dy_ref[...]
        n = n_ref[...]
        gv = g_ref[...].astype(F32)
        uv = u_ref[...].astype(F32)
        da = lax.dot_general(dy, wd_ref[...], NT, preferred_element_type=F32)
        sg = _sigmoid(gv)
        silu = gv * sg
        ab = (silu * uv).astype(BF16)
        dgb = (da * uv * (sg * (1.0 + gv * (1.0 - sg)))).astype(BF16)
        dub = (da * silu).astype(BF16)
        ad_scr[...] += lax.dot_general(ab, dy, TN, preferred_element_type=F32)
        ag_scr[...] += lax.dot_general(dgb, n, TN, preferred_element_type=F32)
        au_scr[...] += lax.dot_general(dub, n, TN, preferred_element_type=F32)
        dn = (jnp.dot(dgb, wg_ref[...], preferred_element_type=F32)
              + jnp.dot(dub, wu_ref[...], preferred_element_type=F32))
        rows = pl.ds(pl.multiple_of(t * tm, tm), tm)

        @pl.when(f == 0)
        def _():
            dn_ref[rows, :] = dn

        @pl.when(f > 0)
        def _():
            dn_ref[rows, :] += dn

        @pl.when(t == nt - 1)
        def _():
            dwg_ref[...] = ag_scr[...].astype(BF16)
            dwu_ref[...] = au_scr[...].astype(BF16)
            dwd_ref[...] = ad_scr[...].astype(BF16)

    row = pl.BlockSpec((tm, D), lambda f, t: (t, 0))
    gspec = pl.BlockSpec((tm, tf), lambda f, t: (t, f))
    wspec = pl.BlockSpec((tf, D), lambda f, t: (f, 0))
    return _pallas(
        body, (dyb, nb, g, u, wg, wu, wd), grid=(nf, nt),
        in_specs=[row, row, gspec, gspec, wspec, wspec, wspec],
        out_specs=[wspec, wspec, wspec, pl.BlockSpec((T, D), lambda f, t: (0, 0))],
        out_shape=[jax.ShapeDtypeStruct((FF, D), BF16)] * 3 + [jax.ShapeDtypeStruct((T, D), F32)],
        scratch_shapes=[pltpu.VMEM((tf, D), F32)] * 3,
        sem=("arbitrary", "arbitrary"), name=name, carry=carry)


def _rms_bwd_rows(dn, xv, gain):
    r = lax.rsqrt(jnp.mean(xv * xv, axis=-1, keepdims=True) + EPS)
    xhat = xv * r
    dxhat = dn * gain
    dx = r * (dxhat - xhat * jnp.mean(dxhat * xhat, axis=-1, keepdims=True))
    return dx, jnp.sum(dn * xhat, axis=0, keepdims=True)


def _norm_bwd(dn, x, dout, gain, name):
    T = x.shape[0]
    tm = 512
    nt = T // tm

    def body(dn_ref, x_ref, dout_ref, gain_ref, dx_ref, dg_ref):
        dx, dgain = _rms_bwd_rows(dn_ref[...], x_ref[...], gain_ref[...])
        dx_ref[...] = dout_ref[...] + dx
        dg_ref[...] = dgain[None]

    row = pl.BlockSpec((tm, D), lambda t: (t, 0))
    return pl.pallas_call(
        body, grid=(nt,), in_specs=[row, row, row, pl.BlockSpec((1, D), lambda t: (0, 0))],
        out_specs=[row, pl.BlockSpec((1, 1, D), lambda t: (t, 0, 0))],
        out_shape=[jax.ShapeDtypeStruct((T, D), F32), jax.ShapeDtypeStruct((nt, 1, D), F32)],
        compiler_params=_cp(("parallel",)), name=name)(dn, x, dout, gain)


def _mix_in(h, gain, win):
    T = h.shape[0]
    tm = 512

    def body(h_ref, gain_ref, w_ref, u_ref, n_ref):
        xv = h_ref[...]
        r = lax.rsqrt(jnp.mean(xv * xv, axis=-1, keepdims=True) + EPS)
        nb = (xv * r * gain_ref[...]).astype(BF16)
        n_ref[...] = nb
        u_ref[...] = lax.dot_general(nb, w_ref[...], NT, preferred_element_type=F32).astype(BF16)

    row = pl.BlockSpec((tm, D), lambda t: (t, 0))
    return pl.pallas_call(
        body, grid=(T // tm,),
        in_specs=[row, pl.BlockSpec((1, D), lambda t: (0, 0)), pl.BlockSpec((DIN, D), lambda t: (0, 0))],
        out_specs=[pl.BlockSpec((tm, DIN), lambda t: (t, 0)), row],
        out_shape=[jax.ShapeDtypeStruct((T, DIN), BF16), jax.ShapeDtypeStruct((T, D), BF16)],
        compiler_params=_cp(("parallel",)), name="mix_in")(h, gain, win)


def _mix_out(h, attn, conv, wout):
    T = h.shape[0]
    tm = 512

    def body(h_ref, a_ref, c_ref, w_ref, o_ref):
        o_ref[...] = (h_ref[...]
                      + jnp.dot(a_ref[...], w_ref[0:DA, :], preferred_element_type=F32)
                      + jnp.dot(c_ref[...], w_ref[DA:D, :], preferred_element_type=F32))

    row = pl.BlockSpec((tm, D), lambda t: (t, 0))
    half = pl.BlockSpec((tm, DA), lambda t: (t, 0))
    return pl.pallas_call(
        body, grid=(T // tm,),
        in_specs=[row, half, half, pl.BlockSpec((D, D), lambda t: (0, 0))],
        out_specs=row, out_shape=jax.ShapeDtypeStruct((T, D), F32),
        compiler_params=_cp(("parallel",)), name="mix_out")(h, attn, conv, wout)


def _mix_out_bwd(dh, attn, conv, wout):
    T = dh.shape[0]
    tm = 512
    nt = T // tm

    def body(dh_ref, a_ref, c_ref, w_ref, da_ref, dc_ref, dw_ref, acc_scr):
        t = pl.program_id(0)

        @pl.when(t == 0)
        def _():
            acc_scr[...] = jnp.zeros_like(acc_scr)

        dhb = dh_ref[...].astype(BF16)
        dmix = lax.dot_general(dhb, w_ref[...], NT, preferred_element_type=F32)
        da_ref[...] = dmix[:, 0:DA].astype(BF16)
        dc_ref[...] = dmix[:, DA:D].astype(BF16)
        acc_scr[0:DA, :] += lax.dot_general(a_ref[...], dhb, TN, preferred_element_type=F32)
        acc_scr[DA:D, :] += lax.dot_general(c_ref[...], dhb, TN, preferred_element_type=F32)

        @pl.when(t == nt - 1)
        def _():
            dw_ref[...] = acc_scr[...].astype(BF16)

    row = pl.BlockSpec((tm, D), lambda t: (t, 0))
    half = pl.BlockSpec((tm, DA), lambda t: (t, 0))
    full = pl.BlockSpec((D, D), lambda t: (0, 0))
    return pl.pallas_call(
        body, grid=(nt,), in_specs=[row, half, half, full], out_specs=[half, half, full],
        out_shape=[jax.ShapeDtypeStruct((T, DA), BF16)] * 2 + [jax.ShapeDtypeStruct((D, D), BF16)],
        scratch_shapes=[pltpu.VMEM((D, D), F32)],
        compiler_params=_cp(("arbitrary",)), name="mix_out_bwd")(dh, attn, conv, wout)


def _mix_in_bwd(dparts, win, nb, h, dh, gain):
    T = h.shape[0]
    tm = 512
    nt = T // tm

    def body(d0, d1, d2, d3, d4, w_ref, n_ref, h_ref, dh_ref, gain_ref,
             dw_ref, dx_ref, dyb_ref, dg_ref, acc_scr):
        t = pl.program_id(0)

        @pl.when(t == 0)
        def _():
            acc_scr[...] = jnp.zeros_like(acc_scr)

        n = n_ref[...]
        dn = jnp.zeros((tm, D), F32)
        for i, d_ref in enumerate((d0, d1, d2, d3, d4)):
            dv = d_ref[...]
            dn = dn + jnp.dot(dv, w_ref[i * DA:(i + 1) * DA, :], preferred_element_type=F32)
            acc_scr[i * DA:(i + 1) * DA, :] += lax.dot_general(dv, n, TN, preferred_element_type=F32)
        dx, dgain = _rms_bwd_rows(dn, h_ref[...], gain_ref[...])
        tot = dh_ref[...] + dx
        dx_ref[...] = tot
        dyb_ref[...] = (0.5 * tot).astype(BF16)
        dg_ref[...] = dgain[None]

        @pl.when(t == nt - 1)
        def _():
            dw_ref[...] = acc_scr[...].astype(BF16)

    row = pl.BlockSpec((tm, D), lambda t: (t, 0))
    half = pl.BlockSpec((tm, DA), lambda t: (t, 0))
    full = pl.BlockSpec((DIN, D), lambda t: (0, 0))
    return pl.pallas_call(
        body, grid=(nt,),
        in_specs=[half] * 5 + [full, row, row, row, pl.BlockSpec((1, D), lambda t: (0, 0))],
        out_specs=[full, row, row, pl.BlockSpec((1, 1, D), lambda t: (t, 0, 0))],
        out_shape=[jax.ShapeDtypeStruct((DIN, D), BF16), jax.ShapeDtypeStruct((T, D), F32),
                   jax.ShapeDtypeStruct((T, D), BF16), jax.ShapeDtypeStruct((nt, 1, D), F32)],
        scratch_shapes=[pltpu.VMEM((DIN, D), F32)],
        compiler_params=_cp(("arbitrary",)), name="mix_in_bwd")(*dparts, win, nb, h, dh, gain)


def _head_masks():
    lane = lax.broadcasted_iota(jnp.int32, (1, 2 * HD), 1)
    m0 = lane < HD
    return m0, jnp.logical_not(m0)


def _head_sums(xv):
    ri = lax.broadcasted_iota(jnp.int32, (2 * HD, 2 * HD), 0)
    ci = lax.broadcasted_iota(jnp.int32, (2 * HD, 2 * HD), 1)
    ones = jnp.where((ri < HD) == (ci < HD), 1.0, 0.0).astype(BF16)
    hi = xv.astype(BF16)
    lo = (xv - hi.astype(F32)).astype(BF16)
    return (jnp.dot(hi, ones, preferred_element_type=F32) + jnp.dot(lo, ones, preferred_element_type=F32))


def _head_rms(xv):
    return lax.rsqrt(_head_sums(xv * xv) * (1.0 / HD) + EPS)


def _band_mask(first):
    qi = lax.broadcasted_iota(jnp.int32, (BLK, 2 * BLK), 0)
    ci = lax.broadcasted_iota(jnp.int32, (BLK, 2 * BLK), 1)
    band = (ci >= qi) & (ci <= qi + BLK)
    return band & ((ci >= BLK) | jnp.logical_not(first))


def _block_rows(j, d, seg):
    r, n = j // seg, j % seg
    start = r + (d * BLK) * n
    first = n == 0
    prev = jnp.where(first, start, start - d * BLK)
    return pl.ds(start, BLK, stride=d), pl.ds(prev, BLK, stride=d), first


def _block_keys(refs, cur, prev, first, single):
    if single:
        qi = lax.broadcasted_iota(jnp.int32, (BLK, BLK), 0)
        ci = lax.broadcasted_iota(jnp.int32, (BLK, BLK), 1)
        return [r[cur, :].astype(BF16) for r in refs], ci <= qi
    return ([jnp.concatenate([r[prev, :], r[cur, :]], axis=0).astype(BF16) for r in refs], _band_mask(first))


def _attn_fwd(u, qg2, kg2, B, S):
    T = B * S
    NB = S // BLK
    scale = HD ** -0.5

    def body(q_ref, k_ref, v_ref, qg_ref, kg_ref, o_ref, lse_ref, qn, kn, vn, os_, ls_):
        m0, m1 = _head_masks()
        qv = q_ref[...].astype(F32)
        qn[...] = qv * _head_rms(qv) * (qg_ref[...] * scale)
        kv = k_ref[...].astype(F32)
        kn[...] = kv * _head_rms(kv) * kg_ref[...]
        vn[...] = v_ref[...].astype(F32)

        for i, d in enumerate(DILS):
            seg = NB // d

            def blk(j, c, i=i, d=d, seg=seg):
                cur, prev, first = _block_rows(j, d, seg)
                qv_ = qn[cur, :].astype(BF16)
                (kk, vv), mask = _block_keys((kn, vn), cur, prev, first, False)
                outs, lses = [], []
                for mh in (m0, m1):
                    qh = jnp.where(mh, qv_, jnp.zeros_like(qv_))
                    s = lax.dot_general(qh, kk, NT, preferred_element_type=F32)
                    s = jnp.where(mask, s, -1e30)
                    mx = jnp.max(s, axis=-1, keepdims=True)
                    p = jnp.exp(s - mx)
                    l = jnp.sum(p, axis=-1, keepdims=True)
                    outs.append(jnp.dot((p * (1.0 / l)).astype(BF16), vv, preferred_element_type=F32))
                    lses.append(mx + jnp.log(l))
                os_[i, cur, :] = jnp.where(m0, outs[0], outs[1])
                ls_[i, cur, :] = jnp.where(m0, lses[0], lses[1])
                return c

            lax.fori_loop(0, NB, blk, 0, unroll=8)

        def comb(c, carry):
            rows = pl.ds(pl.multiple_of(c * 256, 256), 256)
            l0, l1, l2 = ls_[0, rows, :], ls_[1, rows, :], ls_[2, rows, :]
            mx = jnp.maximum(jnp.maximum(l0, l1), l2)
            e0, e1, e2 = jnp.exp(l0 - mx), jnp.exp(l1 - mx), jnp.exp(l2 - mx)
            tot = e0 + e1 + e2
            inv = 1.0 / tot
            o = (e0 * os_[0, rows, :] + e1 * os_[1, rows, :] + e2 * os_[2, rows, :]) * inv
            o_ref[rows, :] = o.astype(BF16)
            lse_ref[rows, :] = mx + jnp.log(tot)
            return carry

        lax.fori_loop(0, S // 256, comb, 0)

    pair = 2 * HD
    blk_spec = lambda off: pl.BlockSpec((S, pair), lambda b, p, off=off: (b, off + p))
    gspec = pl.BlockSpec((1, pair), lambda b, p: (0, 0))
    return pl.pallas_call(
        body, grid=(B, DA // pair),
        in_specs=[blk_spec(0), blk_spec(DA // pair), blk_spec(2 * DA // pair), gspec, gspec],
        out_specs=[blk_spec(0), blk_spec(0)],
        out_shape=[jax.ShapeDtypeStruct((T, DA), BF16), jax.ShapeDtypeStruct((T, DA), F32)],
        scratch_shapes=[pltpu.VMEM((S, pair), F32)] * 3 + [pltpu.VMEM((3, S, pair), F32)] * 2,
        compiler_params=_cp(("parallel", "parallel")), name="attn_fwd")(u, u, u, qg2, kg2)


def _attn_bwd(u, attn, dattn, lse, qg2, kg2, B, S, carry=None):
    T = B * S
    NB = S // BLK
    scale = HD ** -0.5
    pair = 2 * HD

    def body(q_ref, k_ref, v_ref, o_ref, do_ref, lse_ref, qg_ref, kg_ref,
             dq_ref, dk_ref, dv_ref, dgn_ref,
             qn, kn, vn, don, ldl, accq, acck, accv):
        m0, m1 = _head_masks()
        lane = lax.broadcasted_iota(jnp.int32, (1, pair), 1)
        qv = q_ref[...].astype(F32)
        qn[...] = qv * _head_rms(qv) * (qg_ref[...] * scale)
        kv = k_ref[...].astype(F32)
        kn[...] = kv * _head_rms(kv) * kg_ref[...]
        vn[...] = v_ref[...].astype(F32)
        dov = do_ref[...].astype(F32)
        don[...] = dov
        ldl[...] = jnp.where((lane % HD) < HD // 2, lse_ref[...], _head_sums(dov * o_ref[...].astype(F32)))
        accq[...] = jnp.zeros_like(accq)
        acck[...] = jnp.zeros_like(acck)
        accv[...] = jnp.zeros_like(accv)

        for i, d in enumerate(DILS):
            seg = NB // d

            def blk(j, c, d=d, seg=seg):
                cur, prev, first = _block_rows(j, d, seg)
                qv_ = qn[cur, :].astype(BF16)
                (kk, vv), mask = _block_keys((kn, vn), cur, prev, first, seg == 1)
                dov_ = don[cur, :].astype(BF16)
                ldv = ldl[cur, :]
                dq_acc = jnp.zeros((BLK, pair), F32)
                dk_acc = jnp.zeros(kk.shape, F32)
                dv_acc = jnp.zeros(kk.shape, F32)
                for hi, mh in enumerate((m0, m1)):
                    lcol = slice(hi * HD, hi * HD + 1)
                    dcol = slice(hi * HD + HD // 2, hi * HD + HD // 2 + 1)
                    qh = jnp.where(mh, qv_, jnp.zeros_like(qv_))
                    doh = jnp.where(mh, dov_, jnp.zeros_like(dov_))
                    s = lax.dot_general(qh, kk, NT, preferred_element_type=F32)
                    p = jnp.where(mask, jnp.exp(s - ldv[:, lcol]), 0.0)
                    dp = lax.dot_general(doh, vv, NT, preferred_element_type=F32)
                    ds = (p * (dp - ldv[:, dcol])).astype(BF16)
                    pb = p.astype(BF16)
                    dq_acc = dq_acc + jnp.where(mh, jnp.dot(ds, kk, preferred_element_type=F32), 0.0)
                    dk_acc = dk_acc + lax.dot_general(ds, qh, TN, preferred_element_type=F32)
                    dv_acc = dv_acc + lax.dot_general(pb, doh, TN, preferred_element_type=F32)
                accq[cur, :] += dq_acc
                if seg == 1:
                    acck[cur, :] += dk_acc
                    accv[cur, :] += dv_acc
                else:
                    acck[prev, :] += dk_acc[0:BLK]
                    acck[cur, :] += dk_acc[BLK:2 * BLK]
                    accv[prev, :] += dv_acc[0:BLK]
                    accv[cur, :] += dv_acc[BLK:2 * BLK]
                return c

            lax.fori_loop(0, NB, blk, 0, unroll=4)

        def norm_bwd(x_ref, dn, gain):
            xv = x_ref[...].astype(F32)
            r = _head_rms(xv)
            xhat = xv * r
            dxhat = dn * gain
            dx = r * (dxhat - xhat * (_head_sums(dxhat * xhat) * (1.0 / HD)))
            return dx, jnp.sum(dn * xhat, axis=0, keepdims=True)

        dq, dgq = norm_bwd(q_ref, accq[...], qg_ref[...] * scale)
        dk, dgk = norm_bwd(k_ref, acck[...], kg_ref[...])
        dq_ref[...] = dq.astype(BF16)
        dk_ref[...] = dk.astype(BF16)
        dv_ref[...] = accv[...].astype(BF16)
        dgn_ref[...] = jnp.concatenate([dgq * scale, dgk, jnp.zeros((6, pair), F32)], axis=0)[None]

    blk_spec = lambda off: pl.BlockSpec((S, pair), lambda b, p, off=off: (b, off + p))
    gspec = pl.BlockSpec((1, pair), lambda b, p: (0, 0))
    np_ = DA // pair
    return _pallas(
        body, (u, u, u, attn, dattn, lse, qg2, kg2), grid=(B, np_),
        in_specs=[blk_spec(0), blk_spec(np_), blk_spec(2 * np_), blk_spec(0), blk_spec(0), blk_spec(0),
                  gspec, gspec],
        out_specs=[blk_spec(0), blk_spec(0), blk_spec(0),
                   pl.BlockSpec((1, 8, pair), lambda b, p: (b * np_ + p, 0, 0))],
        out_shape=[jax.ShapeDtypeStruct((T, DA), BF16)] * 3 + [jax.ShapeDtypeStruct((B * np_, 8, pair), F32)],
        scratch_shapes=[pltpu.VMEM((S, pair), F32)] * 8,
        sem=("parallel", "parallel"), name="attn_bwd", carry=carry)


CT = 32
CPAD = 32


def _shifted(win, offsets):
    rolled, out = {}, {}
    n = win.shape[0]
    for o in offsets:
        sub = o % 8
        if sub not in rolled:
            rolled[sub] = win if sub == 0 else pltpu.roll(win, n - sub, 0)
        out[o] = rolled[sub][o - sub:o - sub + CT, :]
    return out


def _ln_fwd(y, g, b):
    mu = jnp.mean(y, axis=-1, keepdims=True)
    yc = y - mu
    rstd = lax.rsqrt(jnp.mean(yc * yc, axis=-1, keepdims=True) + EPS)
    xhat = yc * rstd
    return xhat, rstd, xhat * g + b


def _fill_glu(ca_ref, cg_ref, glu, S):
    glu[pl.ds(0, CPAD), :] = jnp.zeros((CPAD, DC), F32)

    def fill(i, c):
        rows = pl.ds(pl.multiple_of(i * 256, 256), 256)
        a = ca_ref[rows, :].astype(F32)
        gt = cg_ref[rows, :].astype(F32)
        glu[pl.ds(pl.multiple_of(CPAD + i * 256, CT), 256), :] = a * _sigmoid(gt)
        return c

    lax.fori_loop(0, S // 256, fill, 0)


def _conv_fwd(u, cw, cb, lg, lb, B, S):
    T = B * S

    def body(ca_ref, cg_ref, w_ref, b_ref, lg_ref, lb_ref, o_ref, y_ref, glu):
        _fill_glu(ca_ref, cg_ref, glu, S)

        def step(i, c):
            t0 = pl.multiple_of(i * CT, CT)
            win = glu[pl.ds(t0, 2 * CT), :]
            acc = jnp.zeros((CT, DC), F32) + b_ref[...]
            taps = _shifted(win, [k + 2 for k in range(CK)])
            for k in range(CK):
                acc = acc + taps[k + 2] * w_ref[k:k + 1, :]
            y_ref[pl.ds(t0, CT), :] = acc
            _, _, z = _ln_fwd(acc, lg_ref[...], lb_ref[...])
            o_ref[pl.ds(t0, CT), :] = (z * _sigmoid(z)).astype(BF16)
            return c

        lax.fori_loop(0, S // CT, step, 0, unroll=2)

    vec = pl.BlockSpec((1, DC), lambda b: (0, 0))
    return pl.pallas_call(
        body, grid=(B,),
        in_specs=[pl.BlockSpec((S, DC), lambda b: (b, 3)), pl.BlockSpec((S, DC), lambda b: (b, 4)),
                  pl.BlockSpec((CT, DC), lambda b: (0, 0)), vec, vec, vec],
        out_specs=[pl.BlockSpec((S, DC), lambda b: (b, 0))] * 2,
        out_shape=[jax.ShapeDtypeStruct((T, DC), BF16), jax.ShapeDtypeStruct((T, DC), F32)],
        scratch_shapes=[pltpu.VMEM((CPAD + S, DC), F32)],
        compiler_params=_cp(("parallel",)), name="conv_fwd")(u, u, cw, cb, lg, lb)


def _conv_bwd(u, y, dconv, cw, lg, lb, B, S):
    T = B * S

    def body(ca_ref, cg_ref, y_ref, dc_ref, w_ref, lg_ref, lb_ref,
             dca_ref, dcg_ref, dw_ref, ds_ref, glu, dyp, dwacc):
        _fill_glu(ca_ref, cg_ref, glu, S)
        dyp[pl.ds(S, CPAD), :] = jnp.zeros((CPAD, DC), F32)
        lgv, lbv = lg_ref[...], lb_ref[...]

        def sum8(v):
            return v[0:8] + v[8:16] + v[16:24] + v[24:32]

        def p1(i, carry):
            sb, sg, sl = carry
            t0 = pl.multiple_of(i * CT, CT)
            xhat, rstd, z = _ln_fwd(y_ref[pl.ds(t0, CT), :], lgv, lbv)
            sz = _sigmoid(z)
            dz = dc_ref[pl.ds(t0, CT), :].astype(F32) * (sz * (1.0 + z * (1.0 - sz)))
            dxhat = dz * lgv
            dy = rstd * (dxhat - jnp.mean(dxhat, axis=-1, keepdims=True)
                         - xhat * jnp.mean(dxhat * xhat, axis=-1, keepdims=True))
            dyp[pl.ds(t0, CT), :] = dy
            return sb + sum8(dy), sg + sum8(dz * xhat), sl + sum8(dz)

        z8 = jnp.zeros((8, DC), F32)
        sb, sg, sl = lax.fori_loop(0, S // CT, p1, (z8, z8, z8))
        rs = lambda v: jnp.sum(v, axis=0, keepdims=True)
        ds_ref[...] = jnp.concatenate([rs(sb), rs(sg), rs(sl), jnp.zeros((5, DC), F32)], axis=0)[None]

        def p2(i, c):
            t0 = pl.multiple_of(i * CT, CT)
            win = dyp[pl.ds(t0, 2 * CT), :]
            acc = jnp.zeros((CT, DC), F32)
            taps = _shifted(win, [30 - k for k in range(CK)])
            for k in range(CK):
                acc = acc + taps[30 - k] * w_ref[k:k + 1, :]
            a = ca_ref[pl.ds(t0, CT), :].astype(F32)
            sgt = _sigmoid(cg_ref[pl.ds(t0, CT), :].astype(F32))
            dca_ref[pl.ds(t0, CT), :] = (acc * sgt).astype(BF16)
            dcg_ref[pl.ds(t0, CT), :] = (acc * a * sgt * (1.0 - sgt)).astype(BF16)
            return c

        lax.fori_loop(0, S // CT, p2, 0)

        dwacc[...] = jnp.zeros_like(dwacc)

        def p3(i, c):
            t0 = pl.multiple_of(i * CT, CT)
            win = glu[pl.ds(t0, 2 * CT), :]
            dy = dyp[pl.ds(t0, CT), :]
            for k in range(CK):
                dwacc[k] += sum8(dy * win[k + 2:k + 2 + CT, :])
            return c

        lax.fori_loop(0, S // CT, p3, 0)
        dw_ref[...] = jnp.sum(dwacc[...], axis=1)[None]

    vec = pl.BlockSpec((1, DC), lambda b: (0, 0))
    seq = pl.BlockSpec((S, DC), lambda b: (b, 0))
    return pl.pallas_call(
        body, grid=(B,),
        in_specs=[pl.BlockSpec((S, DC), lambda b: (b, 3)), pl.BlockSpec((S, DC), lambda b: (b, 4)),
                  seq, seq, pl.BlockSpec((CT, DC), lambda b: (0, 0)), vec, vec],
        out_specs=[seq, seq, pl.BlockSpec((1, CT, DC), lambda b: (b, 0, 0)),
                   pl.BlockSpec((1, 8, DC), lambda b: (b, 0, 0))],
        out_shape=[jax.ShapeDtypeStruct((T, DC), BF16)] * 2
                  + [jax.ShapeDtypeStruct((B, CT, DC), F32), jax.ShapeDtypeStruct((B, 8, DC), F32)],
        scratch_shapes=[pltpu.VMEM((CPAD + S, DC), F32), pltpu.VMEM((S + CPAD, DC), F32),
                        pltpu.VMEM((CT, 8, DC), F32)],
        compiler_params=_cp(("parallel",)), name="conv_bwd")(u, u, y, dconv, cw, lg, lb)


def _local_step(x, target, norms, W, B, S, comm=None):
    qg2 = jnp.concatenate([norms["q_norm"], norms["q_norm"]], axis=1)
    kg2 = jnp.concatenate([norms["k_norm"], norms["k_norm"]], axis=1)
    cw = jnp.concatenate([W["conv_w"], jnp.zeros((1, DC), F32)], axis=0)

    (h1, n1, g1, u1), got = _ffn_fwd(x, norms["ffn1_norm"], W["wg1"], W["wu1"], W["wd1"], None, "ffn1_fwd",
                                     carry=comm.gather_rest if comm else None)
    if comm:
        W = dict(W, **comm.gathered_rest(got))
    u, n2 = _mix_in(h1, norms["mix_norm"], W["win"])
    attn, lse = _attn_fwd(u, qg2, kg2, B, S)
    conv, y = _conv_fwd(u, cw, norms["conv_b"], norms["conv_ln_g"], norms["conv_ln_b"], B, S)
    h2 = _mix_out(h1, attn, conv, W["wout"])
    (h3, n3, g2, u2, dout, dyb, sq), _ = _ffn_fwd(h2, norms["ffn2_norm"], W["wg2"], W["wu2"], W["wd2"], target,
                                                 "ffn2_fwd")
    del h3
    loss = (0.5 / D) * jnp.sum(sq)

    (dwg2, dwu2, dwd2, dn3), _ = _ffn_bwd(dyb, n3, g2, u2, W["wg2"], W["wu2"], W["wd2"], "ffn2_bwd")
    carry = comm.reduce_start("ffn2", {"wg2": dwg2, "wu2": dwu2, "wd2": dwd2}) if comm else None
    dh2, dgn_ffn2 = _norm_bwd(dn3, h2, dout, norms["ffn2_norm"], "ffn2_norm_bwd")
    dattn, dconv, dwout = _mix_out_bwd(dh2, attn, conv, W["wout"])
    (dq, dk, dv, dgn_qk), got = _attn_bwd(u, attn, dattn, lse, qg2, kg2, B, S, carry=carry)
    if comm:
        comm.reduce_done(carry, got)
    dca, dcg, dcw, dcs = _conv_bwd(u, y, dconv, cw, norms["conv_ln_g"], norms["conv_ln_b"], B, S)
    dwin, dh1, dyb1, dgn_mix = _mix_in_bwd((dq, dk, dv, dca, dcg), W["win"], n2, h1, dh2, norms["mix_norm"])
    carry = comm.reduce_start("mix", {"win": dwin, "wout": dwout}) if comm else None
    (dwg1, dwu1, dwd1, dn1), got = _ffn_bwd(dyb1, n1, g1, u1, W["wg1"], W["wu1"], W["wd1"], "ffn1_bwd", carry=carry)
    if comm:
        comm.reduce_done(carry, got)
        comm.reduce_now("ffn1", {"wg1": dwg1, "wu1": dwu1, "wd1": dwd1})
    gx, dgn_ffn1 = _norm_bwd(dn1, x, dh1, norms["ffn1_norm"], "ffn1_norm_bwd")

    qk = jnp.sum(dgn_qk, axis=0)
    cs = jnp.sum(dcs, axis=0)
    small = {
        "ffn1_norm": jnp.sum(dgn_ffn1, axis=0),
        "mix_norm": jnp.sum(dgn_mix, axis=0),
        "q_norm": qk[0:1, 0:HD] + qk[0:1, HD:2 * HD],
        "k_norm": qk[1:2, 0:HD] + qk[1:2, HD:2 * HD],
        "conv_w": jnp.sum(dcw, axis=0)[0:CK],
        "conv_b": cs[0:1],
        "conv_ln_g": cs[1:2],
        "conv_ln_b": cs[2:3],
        "ffn2_norm": jnp.sum(dgn_ffn2, axis=0),
    }
    big = {"wg1": dwg1, "wu1": dwu1, "wd1": dwd1, "win": dwin, "wout": dwout,
           "wg2": dwg2, "wu2": dwu2, "wd2": dwd2}
    return loss, gx, big, small


HBM = pl.BlockSpec(memory_space=pltpu.HBM)
VMEM = pl.BlockSpec(memory_space=pltpu.VMEM)


def _place():
    return lax.axis_index("x"), lax.axis_index("y"), lax.axis_index("c")


class _GatherCarry:
    def __init__(self, shards):
        nt = len(shards)
        self.shards = shards
        self.in_arrays = [s for s, _ in shards]
        self.in_specs = [VMEM] * nt
        self.out_shape = [jax.ShapeDtypeStruct((NDEV * s.shape[0], s.shape[1]), dt) for s, dt in shards]
        self.out_specs = [HBM] * nt
        self.scratch = ([pltpu.VMEM(s.shape, dt) for s, dt in shards]
                        + [pltpu.SemaphoreType.DMA((nt, 7)), pltpu.SemaphoreType.DMA((nt, 7)),
                           pltpu.SemaphoreType.DMA((nt,))])

    def _copies(self, outs, scr):
        nt = len(self.shards)
        stages = scr[:nt]
        send_sems, recv_sems, local_sems = scr[nt:]
        x, y, c = _place()
        me, sibling = (x, y, c), (x, y, 1 - c)
        chips = [(1 - x, y), (x, 1 - y), (1 - x, 1 - y)]

        def rows(t, px, py, pc):
            r = self.shards[t][0].shape[0]
            return outs[t].at[pl.ds((4 * px + 2 * py + pc) * r, r), :]

        def copy(t, k, block, to, src=None):
            return pltpu.make_async_remote_copy(
                src_ref=rows(t, *block) if src is None else src, dst_ref=rows(t, *block),
                send_sem=send_sems.at[t, k], recv_sem=recv_sems.at[t, k],
                device_id=to, device_id_type=MESH)

        mine = [pltpu.make_async_copy(stages[t], rows(t, *me), local_sems.at[t]) for t in range(nt)]
        first = []
        for t in range(nt):
            first.append(copy(t, 0, me, sibling, src=stages[t]))
            first += [copy(t, 1 + j, me, (*chip, c), src=stages[t]) for j, chip in enumerate(chips)]
        return copy, mine, first, me, sibling, chips, c

    def start(self, ins, outs, scr):
        _, mine, first, *_ = self._copies(outs, scr)
        for t, (_, dt) in enumerate(self.shards):
            scr[t][...] = ins[t][...].astype(dt)
        for cp in mine + first:
            cp.start()

    def finish(self, ins, outs, scr):
        copy, mine, first, me, sibling, chips, c = self._copies(outs, scr)
        nt = len(self.shards)
        passed = []
        for j, chip in enumerate(chips):
            for t in range(nt):
                copy(t, 1 + j, (*chip, c), me).wait_recv()
                cp = copy(t, 4 + j, (*chip, c), sibling)
                cp.start()
                passed.append(cp)
        for t in range(nt):
            copy(t, 0, sibling, me).wait_recv()
            for j, chip in enumerate(chips):
                copy(t, 4 + j, (*chip, 1 - c), me).wait_recv()
        for cp in first + passed:
            cp.wait_send()
        for cp in mine:
            cp.wait()


def _run_carry(carry, name):
    def body(*refs):
        n_in, n_out = len(carry.in_arrays), len(carry.out_shape)
        ins, outs, scr = refs[:n_in], refs[n_in:n_in + n_out], refs[n_in + n_out:]
        carry.start(ins, outs, scr)
        carry.finish(ins, outs, scr)

    return pl.pallas_call(
        body, in_specs=carry.in_specs, out_specs=carry.out_specs, out_shape=carry.out_shape,
        scratch_shapes=carry.scratch, compiler_params=pltpu.CompilerParams(vmem_limit_bytes=VMEM_LIMIT),
        name=name)(*carry.in_arrays)


def _sibling_exchange(grads, name):
    nt = len(grads)
    g4 = [g.reshape(4, 2, g.shape[0] // NDEV, g.shape[1]) for g in grads]

    def body(*refs):
        ins, outs = refs[:nt], refs[nt:2 * nt]
        send_sems, recv_sems = refs[2 * nt:]
        x, y, c = _place()
        cps = []
        for t in range(nt):
            cp = pltpu.make_async_remote_copy(
                src_ref=ins[t].at[:, 1 - c], dst_ref=outs[t],
                send_sem=send_sems.at[t], recv_sem=recv_sems.at[t],
                device_id=(x, y, 1 - c), device_id_type=MESH)
            cp.start()
            cps.append(cp)
        for cp in cps:
            cp.wait()

    return pl.pallas_call(
        body, in_specs=[HBM] * nt, out_specs=[HBM] * nt,
        out_shape=[jax.ShapeDtypeStruct((4,) + g.shape[2:], BF16) for g in g4],
        scratch_shapes=[pltpu.SemaphoreType.DMA((nt,)), pltpu.SemaphoreType.DMA((nt,))],
        name=name)(*g4), g4


def _chip_partial(g4, recv, cidx, name):
    _, _, rows, n = g4.shape

    def body(c_ref, a_ref, b_ref, o_ref):
        o_ref[...] = (a_ref[...].astype(F32) + b_ref[...].astype(F32)).astype(BF16)

    return pl.pallas_call(
        body,
        grid_spec=pltpu.PrefetchScalarGridSpec(
            num_scalar_prefetch=1, grid=(4,),
            in_specs=[pl.BlockSpec((None, None, rows, n), lambda q, c_ref: (q, c_ref[0], 0, 0)),
                      pl.BlockSpec((None, rows, n), lambda q, c_ref: (q, 0, 0))],
            out_specs=pl.BlockSpec((None, rows, n), lambda q, c_ref: (q, 0, 0))),
        out_shape=jax.ShapeDtypeStruct((4, rows, n), BF16),
        compiler_params=_cp(("parallel",)), name=name)(cidx, g4, recv)


class _ExchangeCarry:
    def __init__(self, names, parts):
        nt = len(parts)
        self.names = names
        self.in_arrays = list(parts)
        self.in_specs = [HBM] * nt
        self.out_shape = [jax.ShapeDtypeStruct(p.shape, BF16) for p in parts]
        self.out_specs = [HBM] * nt
        self.scratch = [pltpu.SemaphoreType.DMA((nt, 3)), pltpu.SemaphoreType.DMA((nt, 3)),
                        pltpu.SemaphoreType.DMA((nt,))]

    def _copies(self, ins, outs, scr):
        send_sems, recv_sems, local_sems = scr
        x, y, c = _place()
        qme = 2 * x + y
        chips = [(1 - x, y), (x, 1 - y), (1 - x, 1 - y)]
        cps = []
        for t in range(len(ins)):
            cps.append(pltpu.make_async_copy(ins[t].at[qme], outs[t].at[qme], local_sems.at[t]))
            for k, (cx, cy) in enumerate(chips):
                cps.append(pltpu.make_async_remote_copy(
                    src_ref=ins[t].at[2 * cx + cy], dst_ref=outs[t].at[qme],
                    send_sem=send_sems.at[t, k], recv_sem=recv_sems.at[t, k],
                    device_id=(cx, cy, c), device_id_type=MESH))
        return cps

    def start(self, ins, outs, scr):
        for cp in self._copies(ins, outs, scr):
            cp.start()

    def finish(self, ins, outs, scr):
        for cp in self._copies(ins, outs, scr):
            cp.wait()


class _Comm:
    def __init__(self, rest_names, rest_shards, cidx):
        self.rest_names = rest_names
        self.gather_rest = _GatherCarry(rest_shards)
        self.cidx = cidx
        self.reduced = {}

    def gathered_rest(self, outs):
        return dict(zip(self.rest_names, outs))

    def reduce_start(self, tag, grads):
        names = list(grads)
        recv, g4 = _sibling_exchange([grads[n] for n in names], "sibling_exchange_" + tag)
        parts = [_chip_partial(g4[i], recv[i], self.cidx, "chip_partial_" + n) for i, n in enumerate(names)]
        return _ExchangeCarry(names, parts)

    def reduce_done(self, carry, outs):
        self.reduced.update(zip(carry.names, outs))

    def reduce_now(self, tag, grads):
        carry = self.reduce_start(tag, grads)
        self.reduce_done(carry, _run_carry(carry, "chip_exchange_" + tag))


def _adamw_math(w, g, m, v):
    m = B1 * m + (1.0 - B1) * g
    v = B2 * v + (1.0 - B2) * (g * g)
    m_hat = m / (1.0 - B1 ** STEP)
    v_hat = v / (1.0 - B2 ** STEP)
    delta = -LR * (m_hat / (jnp.sqrt(v_hat) + AEPS) + WD * w)
    return delta, m, v


def _adamw_big(recv, w, m, v, name):
    def body(r_ref, w_ref, m_ref, v_ref, g_ref, d_ref, mo_ref, vo_ref):
        g = r_ref[0].astype(F32)
        for q in range(1, 4):
            g = g + r_ref[q].astype(F32)
        d, mn, vn = _adamw_math(w_ref[...], g, m_ref[...], v_ref[...])
        g_ref[...] = g
        d_ref[...] = d
        mo_ref[...] = mn
        vo_ref[...] = vn

    return pl.pallas_call(
        body, out_shape=[jax.ShapeDtypeStruct(w.shape, F32)] * 4,
        compiler_params=pltpu.CompilerParams(vmem_limit_bytes=VMEM_LIMIT), name=name)(recv, w, m, v)


SMALL_ROWS = 40
CW_ROWS = 16
SMALL_TOTAL = SMALL_ROWS + NDEV * CW_ROWS


def _small_step(gvec, wvec, mvec, vvec):
    nr = SMALL_ROWS + CW_ROWS

    def body(g_ref, w_ref, m_ref, v_ref, go_ref, d_ref, mo_ref, vo_ref, slots, send_sems, recv_sems):
        x, y, c = _place()
        me = 4 * x + 2 * y + c
        slots[me] = g_ref[...]
        cps = []
        for k in range(1, NDEV):
            fx, fy, fc = (k >> 2) & 1, (k >> 1) & 1, k & 1
            peer = (x ^ fx, y ^ fy, c ^ fc)
            cp = pltpu.make_async_remote_copy(
                src_ref=g_ref, dst_ref=slots.at[me],
                send_sem=send_sems.at[k - 1], recv_sem=recv_sems.at[k - 1],
                device_id=peer, device_id_type=MESH)
            cp.start()
            cps.append(cp)
        for cp in cps:
            cp.wait()
        tot = slots[0]
        for j in range(1, NDEV):
            tot = tot + slots[j]
        slots[0] = tot
        g = jnp.concatenate(
            [tot[0:SMALL_ROWS], slots[0, pl.ds(pl.multiple_of(SMALL_ROWS + me * CW_ROWS, 8), CW_ROWS), :]], axis=0)
        d, mn, vn = _adamw_math(w_ref[...], g, m_ref[...], v_ref[...])
        go_ref[...] = g
        d_ref[...] = d
        mo_ref[...] = mn
        vo_ref[...] = vn

    return pl.pallas_call(
        body, in_specs=[VMEM] * 4, out_specs=[VMEM] * 4,
        out_shape=[jax.ShapeDtypeStruct((nr, 128), F32)] * 4,
        scratch_shapes=[pltpu.VMEM((NDEV, SMALL_TOTAL, 128), F32),
                        pltpu.SemaphoreType.DMA((NDEV - 1,)), pltpu.SemaphoreType.DMA((NDEV - 1,))],
        name="small_step")(gvec, wvec, mvec, vvec)


SMALL_NAMES = ("ffn1_norm", "mix_norm", "ffn2_norm", "conv_b", "conv_ln_g", "conv_ln_b", "q_norm", "k_norm")


LOSS_ROW = 38


def _pack_small(vals, loss=None):
    parts = []
    for n in SMALL_NAMES:
        a = vals[n].reshape(-1)
        if a.shape[0] < 128:
            a = jnp.concatenate([a, jnp.zeros((128 - a.shape[0],), F32)])
        parts.append(a.reshape(-1, 128))
    used = sum(p.shape[0] for p in parts)
    assert used == LOSS_ROW
    tail = jnp.zeros((SMALL_ROWS - used, 128), F32)
    if loss is not None:
        tail = tail.at[0, 0].set(loss)
    parts.append(tail)
    return jnp.concatenate(parts, axis=0)


def _unpack_small(vec, shapes):
    out, r = {}, 0
    for n in SMALL_NAMES:
        size = shapes[n][1]
        nr = max(size // 128, 1)
        out[n] = vec[r:r + nr].reshape(-1)[:size].reshape(1, size)
        r += nr
    return out


def _pack_cw(a):
    flat = a.reshape(-1)
    return jnp.concatenate([flat, jnp.zeros((CW_ROWS * 128 - flat.shape[0],), F32)]).reshape(CW_ROWS, 128)


def _unpack_cw(v):
    return v.reshape(-1)[:CK * HD].reshape(1, CK, HD)


def kernel(x, ffn1_norm, ffn1_w_gate, ffn1_w_up, ffn1_w_down, mix_norm, w_in, q_norm, k_norm, conv_w, conv_b, conv_ln_g, conv_ln_b, w_out, ffn2_norm, ffn2_w_gate, ffn2_w_up, ffn2_w_down, loss_target, m_ffn1_norm, m_ffn1_w_gate, m_ffn1_w_up, m_ffn1_w_down, m_mix_norm, m_w_in, m_q_norm, m_k_norm, m_conv_w, m_conv_b, m_conv_ln_g, m_conv_ln_b, m_w_out, m_ffn2_norm, m_ffn2_w_gate, m_ffn2_w_up, m_ffn2_w_down, v_ffn1_norm, v_ffn1_w_gate, v_ffn1_w_up, v_ffn1_w_down, v_mix_norm, v_w_in, v_q_norm, v_k_norm, v_conv_w, v_conv_b, v_conv_ln_g, v_conv_ln_b, v_w_out, v_ffn2_norm, v_ffn2_w_gate, v_ffn2_w_up, v_ffn2_w_down):
    P = dict(ffn1_norm=ffn1_norm, ffn1_w_gate=ffn1_w_gate, ffn1_w_up=ffn1_w_up, ffn1_w_down=ffn1_w_down,
             mix_norm=mix_norm, w_in=w_in, q_norm=q_norm, k_norm=k_norm, conv_w=conv_w, conv_b=conv_b,
             conv_ln_g=conv_ln_g, conv_ln_b=conv_ln_b, w_out=w_out, ffn2_norm=ffn2_norm,
             ffn2_w_gate=ffn2_w_gate, ffn2_w_up=ffn2_w_up, ffn2_w_down=ffn2_w_down)
    M = dict(ffn1_norm=m_ffn1_norm, ffn1_w_gate=m_ffn1_w_gate, ffn1_w_up=m_ffn1_w_up, ffn1_w_down=m_ffn1_w_down,
             mix_norm=m_mix_norm, w_in=m_w_in, q_norm=m_q_norm, k_norm=m_k_norm, conv_w=m_conv_w, conv_b=m_conv_b,
             conv_ln_g=m_conv_ln_g, conv_ln_b=m_conv_ln_b, w_out=m_w_out, ffn2_norm=m_ffn2_norm,
             ffn2_w_gate=m_ffn2_w_gate, ffn2_w_up=m_ffn2_w_up, ffn2_w_down=m_ffn2_w_down)
    V = dict(ffn1_norm=v_ffn1_norm, ffn1_w_gate=v_ffn1_w_gate, ffn1_w_up=v_ffn1_w_up, ffn1_w_down=v_ffn1_w_down,
             mix_norm=v_mix_norm, w_in=v_w_in, q_norm=v_q_norm, k_norm=v_k_norm, conv_w=v_conv_w, conv_b=v_conv_b,
             conv_ln_g=v_conv_ln_g, conv_ln_b=v_conv_ln_b, w_out=v_w_out, ffn2_norm=v_ffn2_norm,
             ffn2_w_gate=v_ffn2_w_gate, ffn2_w_up=v_ffn2_w_up, ffn2_w_down=v_ffn2_w_down)
    order = ["ffn1_norm", "ffn1_w_gate", "ffn1_w_up", "ffn1_w_down", "mix_norm", "w_in", "q_norm", "k_norm",
             "conv_w", "conv_b", "conv_ln_g", "conv_ln_b", "w_out", "ffn2_norm", "ffn2_w_gate", "ffn2_w_up",
             "ffn2_w_down"]
    B, S, _ = x.shape
    T = B * S
    cidx = lax.axis_index("c").astype(jnp.int32).reshape(1)

    bigs = [("wg1", "ffn1_w_gate", True), ("wu1", "ffn1_w_up", True), ("wd1", "ffn1_w_down", False),
            ("win", "w_in", True), ("wout", "w_out", False),
            ("wg2", "ffn2_w_gate", True), ("wu2", "ffn2_w_up", True), ("wd2", "ffn2_w_down", False)]
    hm = lambda a, tr: jnp.transpose(a[0]) if tr else a[0]
    cw_pad = jnp.zeros((32, 128), F32).at[0:CK, 0:HD].set(conv_w[0])
    now, rest = bigs[:3], bigs[3:]
    gathered = _run_carry(_GatherCarry([(hm(P[pn], tr), BF16) for _, pn, tr in now] + [(cw_pad, F32)]),
                          "gather_first")
    W = {ln: gathered[i] for i, (ln, _, _) in enumerate(now)}
    cwg = gathered[-1].reshape(NDEV, 32, 128)[:, 0:CK, 0:HD]
    W["conv_w"] = jnp.transpose(cwg, (1, 0, 2)).reshape(CK, DC)
    norms = {n: P[n] for n in SMALL_NAMES}
    comm = _Comm([ln for ln, _, _ in rest], [(hm(P[pn], tr), BF16) for _, pn, tr in rest], cidx)

    loss_part, gx, _, small = _local_step(x.reshape(T, D), loss_target.reshape(T, D), norms, W, B, S, comm)

    G, Dl, Mn, Vn = {}, {}, {}, {}
    for ln, pn, tr in bigs:
        outs = _adamw_big(comm.reduced[ln], hm(P[pn], tr), hm(M[pn], tr), hm(V[pn], tr), "adamw_" + ln)
        G[pn], Dl[pn], Mn[pn], Vn[pn] = [(jnp.transpose(o) if tr else o)[None] for o in outs]

    dcw = small["conv_w"].reshape(CK, NDEV, HD).transpose(1, 0, 2)
    gvec = jnp.concatenate([_pack_small(small, loss_part)] + [_pack_cw(dcw[j]) for j in range(NDEV)], axis=0)
    pack = lambda dct: jnp.concatenate([_pack_small({n: dct[n] for n in SMALL_NAMES}), _pack_cw(dct["conv_w"][0])], axis=0)
    go, do, mo, vo = _small_step(gvec, pack(P), pack(M), pack(V))
    loss = go[LOSS_ROW, 0]
    shapes = {n: P[n].shape for n in SMALL_NAMES}
    for dst, vec in ((G, go), (Dl, do), (Mn, mo), (Vn, vo)):
        dst.update(_unpack_small(vec[0:SMALL_ROWS], shapes))
        dst["conv_w"] = _unpack_cw(vec[SMALL_ROWS:])

    return (loss, gx.reshape(B, S, D), *[G[n] for n in order], *[Dl[n] for n in order],
            *[Mn[n] for n in order], *[Vn[n] for n in order])
```

```python
import functools

import jax
import jax.numpy as jnp
from jax import lax
from jax.experimental import pallas as pl
from jax.experimental.pallas import tpu as pltpu

F32 = jnp.float32
BF16 = jnp.bfloat16

D = 1024
FF = 2816
HD = 64
DA = 512
DC = 512
DIN = 2560
CK = 31
BLK = 128
DILS = (1, 4, 16)
EPS = 1e-6
NDEV = 8
MESH = pl.DeviceIdType.MESH

LR, B1, B2, AEPS, WD, STEP = 0.001, 0.9, 0.999, 1e-08, 0.01, 10

NT = (((1,), (1,)), ((), ()))
TN = (((0,), (0,)), ((), ()))

VMEM_LIMIT = 56 * 1024 * 1024


def _cp(sem=None):
    return pltpu.CompilerParams(dimension_semantics=sem, vmem_limit_bytes=VMEM_LIMIT)


def _sigmoid(x):
    return 0.5 * (jnp.tanh(0.5 * x) + 1.0)


def _pallas(body, args, *, grid, in_specs, out_specs, out_shape, scratch_shapes, sem, name, carry=None):
    if carry is None:
        outs = pl.pallas_call(body, grid=grid, in_specs=in_specs, out_specs=out_specs, out_shape=out_shape,
                              scratch_shapes=scratch_shapes, compiler_params=_cp(sem), name=name)(*args)
        return outs, None
    n_in, n_out, n_scr = len(in_specs), len(out_shape), len(scratch_shapes)
    c_in, c_out = len(carry.in_arrays), len(carry.out_shape)

    def wrapped(*refs):
        ins, refs = refs[:n_in], refs[n_in:]
        cins, refs = refs[:c_in], refs[c_in:]
        outs, refs = refs[:n_out], refs[n_out:]
        couts, refs = refs[:c_out], refs[c_out:]
        scr, cscr = refs[:n_scr], refs[n_scr:]
        ids = [pl.program_id(a) for a in range(len(grid))]
        is_first = functools.reduce(jnp.logical_and, [i == 0 for i in ids])
        is_last = functools.reduce(jnp.logical_and, [i == n - 1 for i, n in zip(ids, grid)])

        @pl.when(is_first)
        def _():
            carry.start(cins, couts, cscr)

        body(*ins, *outs, *scr)

        @pl.when(is_last)
        def _():
            carry.finish(cins, couts, cscr)

    outs = pl.pallas_call(
        wrapped, grid=grid, in_specs=list(in_specs) + carry.in_specs, out_specs=list(out_specs) + carry.out_specs,
        out_shape=list(out_shape) + carry.out_shape, scratch_shapes=list(scratch_shapes) + carry.scratch,
        compiler_params=_cp(("arbitrary",) * len(grid)), name=name)(*args, *carry.in_arrays)
    return outs[:n_out], outs[n_out:]


def _ffn_fwd(x, gain, wg, wu, wd, target, name, carry=None):
    T = x.shape[0]
    tm, tf = 1024, 256
    nt, nf = T // tm, FF // tf
    with_loss = target is not None

    def body(*refs):
        if with_loss:
            (x_ref, gain_ref, wg_ref, wu_ref, wd_ref, t_ref,
             h_ref, n_ref, g_ref, u_ref, dout_ref, dyb_ref, sq_ref, nb_scr, acc_scr) = refs
        else:
            (x_ref, gain_ref, wg_ref, wu_ref, wd_ref,
             h_ref, n_ref, g_ref, u_ref, nb_scr, acc_scr) = refs
        f = pl.program_id(1)

        @pl.when(f == 0)
        def _():
            xv = x_ref[...]
            r = lax.rsqrt(jnp.mean(xv * xv, axis=-1, keepdims=True) + EPS)
            nb = (xv * r * gain_ref[...]).astype(BF16)
            nb_scr[...] = nb
            n_ref[...] = nb
            acc_scr[...] = jnp.zeros_like(acc_scr)

        nb = nb_scr[...]
        g = lax.dot_general(nb, wg_ref[...], NT, preferred_element_type=F32)
        u = lax.dot_general(nb, wu_ref[...], NT, preferred_element_type=F32)
        a = g * _sigmoid(g) * u
        g_ref[...] = g.astype(BF16)
        u_ref[...] = u.astype(BF16)
        acc_scr[...] += jnp.dot(a.astype(BF16), wd_ref[...], preferred_element_type=F32)

        @pl.when(f == nf - 1)
        def _():
            h = x_ref[...] + 0.5 * acc_scr[...]
            h_ref[...] = h
            if with_loss:
                e = h - t_ref[...]
                dout = e * (1.0 / D)
                dout_ref[...] = dout
                dyb_ref[...] = (0.5 * dout).astype(BF16)
                sq_ref[...] = jnp.sum(e * e, axis=0, keepdims=True)[None]

    row = pl.BlockSpec((tm, D), lambda t, f: (t, 0))
    wspec = pl.BlockSpec((tf, D), lambda t, f: (f, 0))
    gspec = pl.BlockSpec((tm, tf), lambda t, f: (t, f))
    in_specs = [row, pl.BlockSpec((1, D), lambda t, f: (0, 0)), wspec, wspec, wspec]
    out_shape = [jax.ShapeDtypeStruct((T, D), F32), jax.ShapeDtypeStruct((T, D), BF16),
                 jax.ShapeDtypeStruct((T, FF), BF16), jax.ShapeDtypeStruct((T, FF), BF16)]
    out_specs = [row, row, gspec, gspec]
    args = [x, gain, wg, wu, wd]
    if with_loss:
        in_specs.append(row)
        args.append(target)
        out_shape += [jax.ShapeDtypeStruct((T, D), F32), jax.ShapeDtypeStruct((T, D), BF16),
                      jax.ShapeDtypeStruct((nt, 1, D), F32)]
        out_specs += [row, row, pl.BlockSpec((1, 1, D), lambda t, f: (t, 0, 0))]
    return _pallas(
        body, args, grid=(nt, nf), in_specs=in_specs, out_specs=out_specs, out_shape=out_shape,
        scratch_shapes=[pltpu.VMEM((tm, D), BF16), pltpu.VMEM((tm, D), F32)],
        sem=("parallel", "arbitrary"), name=name, carry=carry)


def _ffn_bwd(dyb, nb, g, u, wg, wu, wd, name, carry=None):
    T = dyb.shape[0]
    tm, tf = 512, 256
    nt, nf = T // tm, FF // tf

    def body(dy_ref, n_ref, g_ref, u_ref, wg_ref, wu_ref, wd_ref,
             dwg_ref, dwu_ref, dwd_ref, dn_ref, ag_scr, au_scr, ad_scr):
        f, t = pl.program_id(0), pl.program_id(1)

        @pl.when(t == 0)
        def _():
            ag_scr[...] = jnp.zeros_like(ag_scr)
            au_scr[...] = jnp.zeros_like(au_scr)
            ad_scr[...] = jnp.zeros_like(ad_scr)

        dy = dy_ref[...]
        n = n_ref[...]
        gv = g_ref[...].astype(F32)
        uv = u_ref[...].astype(F32)
        da = lax.dot_general(dy, wd_ref[...], NT, preferred_element_type=F32)
        sg = _sigmoid(gv)
        silu = gv * sg
        ab = (silu * uv).astype(BF16)
        dgb = (da * uv * (sg * (1.0 + gv * (1.0 - sg)))).astype(BF16)
        dub = (da * silu).astype(BF16)
        ad_scr[...] += lax.dot_general(ab, dy, TN, preferred_element_type=F32)
        ag_scr[...] += lax.dot_general(dgb, n, TN, preferred_element_type=F32)
        au_scr[...] += lax.dot_general(dub, n, TN, preferred_element_type=F32)
        dn = (jnp.dot(dgb, wg_ref[...], preferred_element_type=F32)
              + jnp.dot(dub, wu_ref[...], preferred_element_type=F32))
        rows = pl.ds(pl.multiple_of(t * tm, tm), tm)

        @pl.when(f == 0)
        def _():
            dn_ref[rows, :] = dn

        @pl.when(f > 0)
        def _():
            dn_ref[rows, :] += dn

        @pl.when(t == nt - 1)
        def _():
            dwg_ref[...] = ag_scr[...].astype(BF16)
            dwu_ref[...] = au_scr[...].astype(BF16)
            dwd_ref[...] = ad_scr[...].astype(BF16)

    row = pl.BlockSpec((tm, D), lambda f, t: (t, 0))
    gspec = pl.BlockSpec((tm, tf), lambda f, t: (t, f))
    wspec = pl.BlockSpec((tf, D), lambda f, t: (f, 0))
    return _pallas(
        body, (dyb, nb, g, u, wg, wu, wd), grid=(nf, nt),
        in_specs=[row, row, gspec, gspec, wspec, wspec, wspec],
        out_specs=[wspec, wspec, wspec, pl.BlockSpec((T, D), lambda f, t: (0, 0))],
        out_shape=[jax.ShapeDtypeStruct((FF, D), BF16)] * 3 + [jax.ShapeDtypeStruct((T, D), F32)],
        scratch_shapes=[pltpu.VMEM((tf, D), F32)] * 3,
        sem=("arbitrary", "arbitrary"), name=name, carry=carry)


FC = 256


def _resident(shape):
    return pl.BlockSpec(shape, lambda *_: (0,) * len(shape), pipeline_mode=pl.Buffered(1))


def _ffn_forward(x, gain, wg, wu, wd, target, name, carry=None):
    T = x.shape[0]
    tm = 512
    nt = T // tm
    with_loss = target is not None

    def body(*refs):
        if with_loss:
            (x_ref, gain_ref, wg_ref, wu_ref, wd_ref, t_ref,
             h_ref, n_ref, g_ref, u_ref, dout_ref, dyb_ref, sq_ref, a_scr) = refs
        else:
            x_ref, gain_ref, wg_ref, wu_ref, wd_ref, h_ref, n_ref, g_ref, u_ref, a_scr = refs
        xv = x_ref[...]
        r = lax.rsqrt(jnp.mean(xv * xv, axis=-1, keepdims=True) + EPS)
        n_ref[...] = (xv * r * gain_ref[...]).astype(BF16)
        for c in range(FF // FC):
            cols = slice(c * FC, (c + 1) * FC)
            nb = n_ref[...]
            g = lax.dot_general(nb, wg_ref[cols, :], NT, preferred_element_type=F32)
            u = lax.dot_general(nb, wu_ref[cols, :], NT, preferred_element_type=F32)
            g_ref[:, cols] = g.astype(BF16)
            u_ref[:, cols] = u.astype(BF16)
            a_scr[:, cols] = (g * _sigmoid(g) * u).astype(BF16)
        h = xv + 0.5 * jnp.dot(a_scr[...], wd_ref[...], preferred_element_type=F32)
        h_ref[...] = h
        if with_loss:
            e = h - t_ref[...]
            dout = e * (1.0 / D)
            dout_ref[...] = dout
            dyb_ref[...] = (0.5 * dout).astype(BF16)
            sq_ref[...] = jnp.sum(e * e, axis=0, keepdims=True)[None]

    row = pl.BlockSpec((tm, D), lambda t: (t, 0))
    wide = pl.BlockSpec((tm, FF), lambda t: (t, 0))
    in_specs = [row, _resident((1, D)), _resident((FF, D)), _resident((FF, D)), _resident((FF, D))]
    out_shape = [jax.ShapeDtypeStruct((T, D), F32), jax.ShapeDtypeStruct((T, D), BF16),
                 jax.ShapeDtypeStruct((T, FF), BF16), jax.ShapeDtypeStruct((T, FF), BF16)]
    out_specs = [row, row, wide, wide]
    args = [x, gain, wg, wu, wd]
    if with_loss:
        in_specs.append(row)
        args.append(target)
        out_shape += [jax.ShapeDtypeStruct((T, D), F32), jax.ShapeDtypeStruct((T, D), BF16),
                      jax.ShapeDtypeStruct((nt, 1, D), F32)]
        out_specs += [row, row, pl.BlockSpec((1, 1, D), lambda t: (t, 0, 0))]
    return _pallas(
        body, args, grid=(nt,), in_specs=in_specs, out_specs=out_specs, out_shape=out_shape,
        scratch_shapes=[pltpu.VMEM((tm, FF), BF16)], sem=("parallel",), name=name, carry=carry)


def _ffn_bwd_act(dyb, g, u, x, dout, gain, wg, wu, wd, name, carry=None):
    T = x.shape[0]
    tm = 256
    nt = T // tm

    def body(dy_ref, g_ref, u_ref, x_ref, dout_ref, gain_ref, wg_ref, wu_ref, wd_ref,
             a_ref, dg_ref, du_ref, dx_ref, dgn_ref):
        for c in range(FF // FC):
            cols = slice(c * FC, (c + 1) * FC)
            da = lax.dot_general(dy_ref[...], wd_ref[cols, :], NT, preferred_element_type=F32)
            gv = g_ref[:, cols].astype(F32)
            uv = u_ref[:, cols].astype(F32)
            sg = _sigmoid(gv)
            silu = gv * sg
            a_ref[:, cols] = (silu * uv).astype(BF16)
            dg_ref[:, cols] = (da * uv * (sg * (1.0 + gv * (1.0 - sg)))).astype(BF16)
            du_ref[:, cols] = (da * silu).astype(BF16)
        dn = (jnp.dot(dg_ref[...], wg_ref[...], preferred_element_type=F32)
              + jnp.dot(du_ref[...], wu_ref[...], preferred_element_type=F32))
        dx, dgain = _rms_bwd_rows(dn, x_ref[...], gain_ref[...])
        dx_ref[...] = dout_ref[...] + dx
        dgn_ref[...] = dgain[None]

    row = pl.BlockSpec((tm, D), lambda t: (t, 0))
    wide = pl.BlockSpec((tm, FF), lambda t: (t, 0))
    return _pallas(
        body, (dyb, g, u, x, dout, gain, wg, wu, wd), grid=(nt,),
        in_specs=[row, wide, wide, row, row, _resident((1, D)), _resident((FF, D)), _resident((FF, D)),
                  _resident((FF, D))],
        out_specs=[wide, wide, wide, row, pl.BlockSpec((1, 1, D), lambda t: (t, 0, 0))],
        out_shape=[jax.ShapeDtypeStruct((T, FF), BF16)] * 3
                  + [jax.ShapeDtypeStruct((T, D), F32), jax.ShapeDtypeStruct((nt, 1, D), F32)],
        scratch_shapes=[], sem=("parallel",), name=name, carry=carry)


def _ffn_bwd_w(a, dg, du, dyb, nb, name, carry=None):
    T = nb.shape[0]
    tf = 256

    def body(a_ref, dg_ref, du_ref, dy_ref, n_ref, dwg_ref, dwu_ref, dwd_ref):
        dwd_ref[...] = lax.dot_general(a_ref[...], dy_ref[...], TN, preferred_element_type=F32).astype(BF16)
        dwg_ref[...] = lax.dot_general(dg_ref[...], n_ref[...], TN, preferred_element_type=F32).astype(BF16)
        dwu_ref[...] = lax.dot_general(du_ref[...], n_ref[...], TN, preferred_element_type=F32).astype(BF16)

    col = pl.BlockSpec((T, tf), lambda f: (0, f))
    wspec = pl.BlockSpec((tf, D), lambda f: (f, 0))
    return _pallas(
        body, (a, dg, du, dyb, nb), grid=(FF // tf,),
        in_specs=[col, col, col, _resident((T, D)), _resident((T, D))],
        out_specs=[wspec, wspec, wspec], out_shape=[jax.ShapeDtypeStruct((FF, D), BF16)] * 3,
        scratch_shapes=[], sem=("parallel",), name=name, carry=carry)


def _rms_bwd_rows(dn, xv, gain):
    r = lax.rsqrt(jnp.mean(xv * xv, axis=-1, keepdims=True) + EPS)
    xhat = xv * r
    dxhat = dn * gain
    dx = r * (dxhat - xhat * jnp.mean(dxhat * xhat, axis=-1, keepdims=True))
    return dx, jnp.sum(dn * xhat, axis=0, keepdims=True)


def _norm_bwd(dn, x, dout, gain, name):
    T = x.shape[0]
    tm = 512
    nt = T // tm

    def body(dn_ref, x_ref, dout_ref, gain_ref, dx_ref, dg_ref):
        dx, dgain = _rms_bwd_rows(dn_ref[...], x_ref[...], gain_ref[...])
        dx_ref[...] = dout_ref[...] + dx
        dg_ref[...] = dgain[None]

    row = pl.BlockSpec((tm, D), lambda t: (t, 0))
    return pl.pallas_call(
        body, grid=(nt,), in_specs=[row, row, row, pl.BlockSpec((1, D), lambda t: (0, 0))],
        out_specs=[row, pl.BlockSpec((1, 1, D), lambda t: (t, 0, 0))],
        out_shape=[jax.ShapeDtypeStruct((T, D), F32), jax.ShapeDtypeStruct((nt, 1, D), F32)],
        compiler_params=_cp(("parallel",)), name=name)(dn, x, dout, gain)


def _mix_in(h, gain, win):
    T = h.shape[0]
    tm = 512

    def body(h_ref, gain_ref, w_ref, u_ref, n_ref):
        xv = h_ref[...]
        r = lax.rsqrt(jnp.mean(xv * xv, axis=-1, keepdims=True) + EPS)
        nb = (xv * r * gain_ref[...]).astype(BF16)
        n_ref[...] = nb
        u_ref[...] = lax.dot_general(nb, w_ref[...], NT, preferred_element_type=F32).astype(BF16)

    row = pl.BlockSpec((tm, D), lambda t: (t, 0))
    return pl.pallas_call(
        body, grid=(T // tm,),
        in_specs=[row, pl.BlockSpec((1, D), lambda t: (0, 0)), pl.BlockSpec((DIN, D), lambda t: (0, 0))],
        out_specs=[pl.BlockSpec((tm, DIN), lambda t: (t, 0)), row],
        out_shape=[jax.ShapeDtypeStruct((T, DIN), BF16), jax.ShapeDtypeStruct((T, D), BF16)],
        compiler_params=_cp(("parallel",)), name="mix_in")(h, gain, win)


def _mix_out(h, attn, conv, wout):
    T = h.shape[0]
    tm = 512

    def body(h_ref, a_ref, c_ref, w_ref, o_ref):
        o_ref[...] = (h_ref[...]
                      + jnp.dot(a_ref[...], w_ref[0:DA, :], preferred_element_type=F32)
                      + jnp.dot(c_ref[...], w_ref[DA:D, :], preferred_element_type=F32))

    row = pl.BlockSpec((tm, D), lambda t: (t, 0))
    half = pl.BlockSpec((tm, DA), lambda t: (t, 0))
    return pl.pallas_call(
        body, grid=(T // tm,),
        in_specs=[row, half, half, pl.BlockSpec((D, D), lambda t: (0, 0))],
        out_specs=row, out_shape=jax.ShapeDtypeStruct((T, D), F32),
        compiler_params=_cp(("parallel",)), name="mix_out")(h, attn, conv, wout)


def _mix_out_bwd(dh, attn, conv, wout):
    T = dh.shape[0]
    tm = 512
    nt = T // tm

    def body(dh_ref, a_ref, c_ref, w_ref, da_ref, dc_ref, dw_ref, acc_scr):
        t = pl.program_id(0)

        @pl.when(t == 0)
        def _():
            acc_scr[...] = jnp.zeros_like(acc_scr)

        dhb = dh_ref[...].astype(BF16)
        dmix = lax.dot_general(dhb, w_ref[...], NT, preferred_element_type=F32)
        da_ref[...] = dmix[:, 0:DA].astype(BF16)
        dc_ref[...] = dmix[:, DA:D].astype(BF16)
        acc_scr[0:DA, :] += lax.dot_general(a_ref[...], dhb, TN, preferred_element_type=F32)
        acc_scr[DA:D, :] += lax.dot_general(c_ref[...], dhb, TN, preferred_element_type=F32)

        @pl.when(t == nt - 1)
        def _():
            dw_ref[...] = acc_scr[...].astype(BF16)

    row = pl.BlockSpec((tm, D), lambda t: (t, 0))
    half = pl.BlockSpec((tm, DA), lambda t: (t, 0))
    full = pl.BlockSpec((D, D), lambda t: (0, 0))
    return pl.pallas_call(
        body, grid=(nt,), in_specs=[row, half, half, full], out_specs=[half, half, full],
        out_shape=[jax.ShapeDtypeStruct((T, DA), BF16)] * 2 + [jax.ShapeDtypeStruct((D, D), BF16)],
        scratch_shapes=[pltpu.VMEM((D, D), F32)],
        compiler_params=_cp(("arbitrary",)), name="mix_out_bwd")(dh, attn, conv, wout)


def _mix_in_bwd(dparts, win, nb, h, dh, gain):
    T = h.shape[0]
    tm = 512
    nt = T // tm

    def body(d0, d1, d2, d3, d4, w_ref, n_ref, h_ref, dh_ref, gain_ref,
             dw_ref, dx_ref, dyb_ref, dg_ref, acc_scr):
        t = pl.program_id(0)

        @pl.when(t == 0)
        def _():
            acc_scr[...] = jnp.zeros_like(acc_scr)

        n = n_ref[...]
        dn = jnp.zeros((tm, D), F32)
        for i, d_ref in enumerate((d0, d1, d2, d3, d4)):
            dv = d_ref[...]
            dn = dn + jnp.dot(dv, w_ref[i * DA:(i + 1) * DA, :], preferred_element_type=F32)
            acc_scr[i * DA:(i + 1) * DA, :] += lax.dot_general(dv, n, TN, preferred_element_type=F32)
        dx, dgain = _rms_bwd_rows(dn, h_ref[...], gain_ref[...])
        tot = dh_ref[...] + dx
        dx_ref[...] = tot
        dyb_ref[...] = (0.5 * tot).astype(BF16)
        dg_ref[...] = dgain[None]

        @pl.when(t == nt - 1)
        def _():
            dw_ref[...] = acc_scr[...].astype(BF16)

    row = pl.BlockSpec((tm, D), lambda t: (t, 0))
    half = pl.BlockSpec((tm, DA), lambda t: (t, 0))
    full = pl.BlockSpec((DIN, D), lambda t: (0, 0))
    return pl.pallas_call(
        body, grid=(nt,),
        in_specs=[half] * 5 + [full, row, row, row, pl.BlockSpec((1, D), lambda t: (0, 0))],
        out_specs=[full, row, row, pl.BlockSpec((1, 1, D), lambda t: (t, 0, 0))],
        out_shape=[jax.ShapeDtypeStruct((DIN, D), BF16), jax.ShapeDtypeStruct((T, D), F32),
                   jax.ShapeDtypeStruct((T, D), BF16), jax.ShapeDtypeStruct((nt, 1, D), F32)],
        scratch_shapes=[pltpu.VMEM((DIN, D), F32)],
        compiler_params=_cp(("arbitrary",)), name="mix_in_bwd")(*dparts, win, nb, h, dh, gain)


def _head_masks():
    lane = lax.broadcasted_iota(jnp.int32, (1, 2 * HD), 1)
    m0 = lane < HD
    return m0, jnp.logical_not(m0)


def _head_sums(xv):
    ri = lax.broadcasted_iota(jnp.int32, (2 * HD, 2 * HD), 0)
    ci = lax.broadcasted_iota(jnp.int32, (2 * HD, 2 * HD), 1)
    ones = jnp.where((ri < HD) == (ci < HD), 1.0, 0.0).astype(BF16)
    hi = xv.astype(BF16)
    lo = (xv - hi.astype(F32)).astype(BF16)
    return (jnp.dot(hi, ones, preferred_element_type=F32) + jnp.dot(lo, ones, preferred_element_type=F32))


def _head_rms(xv):
    return lax.rsqrt(_head_sums(xv * xv) * (1.0 / HD) + EPS)


def _band_mask(first):
    qi = lax.broadcasted_iota(jnp.int32, (BLK, 2 * BLK), 0)
    ci = lax.broadcasted_iota(jnp.int32, (BLK, 2 * BLK), 1)
    band = (ci >= qi) & (ci <= qi + BLK)
    return band & ((ci >= BLK) | jnp.logical_not(first))


def _block_rows(j, d, seg):
    r, n = j // seg, j % seg
    start = r + (d * BLK) * n
    first = n == 0
    prev = jnp.where(first, start, start - d * BLK)
    return pl.ds(start, BLK, stride=d), pl.ds(prev, BLK, stride=d), first


def _block_keys(refs, cur, prev, first, single):
    if single:
        qi = lax.broadcasted_iota(jnp.int32, (BLK, BLK), 0)
        ci = lax.broadcasted_iota(jnp.int32, (BLK, BLK), 1)
        return [r[cur, :].astype(BF16) for r in refs], ci <= qi
    return ([jnp.concatenate([r[prev, :], r[cur, :]], axis=0).astype(BF16) for r in refs], _band_mask(first))


def _attn_fwd(u, qg2, kg2, B, S):
    T = B * S
    NB = S // BLK
    scale = HD ** -0.5

    def body(q_ref, k_ref, v_ref, qg_ref, kg_ref, o_ref, lse_ref, qn, kn, vn, os_, ls_):
        m0, m1 = _head_masks()
        qv = q_ref[...].astype(F32)
        qn[...] = qv * _head_rms(qv) * (qg_ref[...] * scale)
        kv = k_ref[...].astype(F32)
        kn[...] = kv * _head_rms(kv) * kg_ref[...]
        vn[...] = v_ref[...].astype(F32)

        for i, d in enumerate(DILS):
            seg = NB // d

            def blk(j, c, i=i, d=d, seg=seg):
                cur, prev, first = _block_rows(j, d, seg)
                qv_ = qn[cur, :].astype(BF16)
                (kk, vv), mask = _block_keys((kn, vn), cur, prev, first, False)
                outs, lses = [], []
                for mh in (m0, m1):
                    qh = jnp.where(mh, qv_, jnp.zeros_like(qv_))
                    s = lax.dot_general(qh, kk, NT, preferred_element_type=F32)
                    s = jnp.where(mask, s, -1e30)
                    mx = jnp.max(s, axis=-1, keepdims=True)
                    p = jnp.exp(s - mx)
                    l = jnp.sum(p, axis=-1, keepdims=True)
                    outs.append(jnp.dot((p * (1.0 / l)).astype(BF16), vv, preferred_element_type=F32))
                    lses.append(mx + jnp.log(l))
                os_[i, cur, :] = jnp.where(m0, outs[0], outs[1])
                ls_[i, cur, :] = jnp.where(m0, lses[0], lses[1])
                return c

            lax.fori_loop(0, NB, blk, 0, unroll=8)

        def comb(c, carry):
            rows = pl.ds(pl.multiple_of(c * 256, 256), 256)
            l0, l1, l2 = ls_[0, rows, :], ls_[1, rows, :], ls_[2, rows, :]
            mx = jnp.maximum(jnp.maximum(l0, l1), l2)
            e0, e1, e2 = jnp.exp(l0 - mx), jnp.exp(l1 - mx), jnp.exp(l2 - mx)
            tot = e0 + e1 + e2
            inv = 1.0 / tot
            o = (e0 * os_[0, rows, :] + e1 * os_[1, rows, :] + e2 * os_[2, rows, :]) * inv
            o_ref[rows, :] = o.astype(BF16)
            lse_ref[rows, :] = mx + jnp.log(tot)
            return carry

        lax.fori_loop(0, S // 256, comb, 0)

    pair = 2 * HD
    blk_spec = lambda off: pl.BlockSpec((S, pair), lambda b, p, off=off: (b, off + p))
    gspec = pl.BlockSpec((1, pair), lambda b, p: (0, 0))
    return pl.pallas_call(
        body, grid=(B, DA // pair),
        in_specs=[blk_spec(0), blk_spec(DA // pair), blk_spec(2 * DA // pair), gspec, gspec],
        out_specs=[blk_spec(0), blk_spec(0)],
        out_shape=[jax.ShapeDtypeStruct((T, DA), BF16), jax.ShapeDtypeStruct((T, DA), F32)],
        scratch_shapes=[pltpu.VMEM((S, pair), F32)] * 3 + [pltpu.VMEM((3, S, pair), F32)] * 2,
        compiler_params=_cp(("parallel", "parallel")), name="attn_fwd")(u, u, u, qg2, kg2)


def _attn_bwd(u, attn, dattn, lse, qg2, kg2, B, S, carry=None):
    T = B * S
    NB = S // BLK
    scale = HD ** -0.5
    pair = 2 * HD

    def body(q_ref, k_ref, v_ref, o_ref, do_ref, lse_ref, qg_ref, kg_ref,
             dq_ref, dk_ref, dv_ref, dgn_ref,
             qn, kn, vn, don, ldl, accq, acck, accv):
        m0, m1 = _head_masks()
        lane = lax.broadcasted_iota(jnp.int32, (1, pair), 1)
        qv = q_ref[...].astype(F32)
        qn[...] = qv * _head_rms(qv) * (qg_ref[...] * scale)
        kv = k_ref[...].astype(F32)
        kn[...] = kv * _head_rms(kv) * kg_ref[...]
        vn[...] = v_ref[...].astype(F32)
        dov = do_ref[...].astype(F32)
        don[...] = dov
        ldl[...] = jnp.where((lane % HD) < HD // 2, lse_ref[...], _head_sums(dov * o_ref[...].astype(F32)))
        accq[...] = jnp.zeros_like(accq)
        acck[...] = jnp.zeros_like(acck)
        accv[...] = jnp.zeros_like(accv)

        for i, d in enumerate(DILS):
            seg = NB // d

            def blk(j, c, d=d, seg=seg):
                cur, prev, first = _block_rows(j, d, seg)
                qv_ = qn[cur, :].astype(BF16)
                (kk, vv), mask = _block_keys((kn, vn), cur, prev, first, seg == 1)
                dov_ = don[cur, :].astype(BF16)
                ldv = ldl[cur, :]
                dq_acc = jnp.zeros((BLK, pair), F32)
                dk_acc = jnp.zeros(kk.shape, F32)
                dv_acc = jnp.zeros(kk.shape, F32)
                for hi, mh in enumerate((m0, m1)):
                    lcol = slice(hi * HD, hi * HD + 1)
                    dcol = slice(hi * HD + HD // 2, hi * HD + HD // 2 + 1)
                    qh = jnp.where(mh, qv_, jnp.zeros_like(qv_))
                    doh = jnp.where(mh, dov_, jnp.zeros_like(dov_))
                    s = lax.dot_general(qh, kk, NT, preferred_element_type=F32)
                    p = jnp.where(mask, jnp.exp(s - ldv[:, lcol]), 0.0)
                    dp = lax.dot_general(doh, vv, NT, preferred_element_type=F32)
                    ds = (p * (dp - ldv[:, dcol])).astype(BF16)
                    pb = p.astype(BF16)
                    dq_acc = dq_acc + jnp.where(mh, jnp.dot(ds, kk, preferred_element_type=F32), 0.0)
                    dk_acc = dk_acc + lax.dot_general(ds, qh, TN, preferred_element_type=F32)
                    dv_acc = dv_acc + lax.dot_general(pb, doh, TN, preferred_element_type=F32)
                accq[cur, :] += dq_acc
                if seg == 1:
                    acck[cur, :] += dk_acc
                    accv[cur, :] += dv_acc
                else:
                    acck[prev, :] += dk_acc[0:BLK]
                    acck[cur, :] += dk_acc[BLK:2 * BLK]
                    accv[prev, :] += dv_acc[0:BLK]
                    accv[cur, :] += dv_acc[BLK:2 * BLK]
                return c

            lax.fori_loop(0, NB, blk, 0, unroll=4)

        def norm_bwd(x_ref, dn, gain):
            xv = x_ref[...].astype(F32)
            r = _head_rms(xv)
            xhat = xv * r
            dxhat = dn * gain
            dx = r * (dxhat - xhat * (_head_sums(dxhat * xhat) * (1.0 / HD)))
            return dx, jnp.sum(dn * xhat, axis=0, keepdims=True)

        dq, dgq = norm_bwd(q_ref, accq[...], qg_ref[...] * scale)
        dk, dgk = norm_bwd(k_ref, acck[...], kg_ref[...])
        dq_ref[...] = dq.astype(BF16)
        dk_ref[...] = dk.astype(BF16)
        dv_ref[...] = accv[...].astype(BF16)
        dgn_ref[...] = jnp.concatenate([dgq * scale, dgk, jnp.zeros((6, pair), F32)], axis=0)[None]

    blk_spec = lambda off: pl.BlockSpec((S, pair), lambda b, p, off=off: (b, off + p))
    gspec = pl.BlockSpec((1, pair), lambda b, p: (0, 0))
    np_ = DA // pair
    return _pallas(
        body, (u, u, u, attn, dattn, lse, qg2, kg2), grid=(B, np_),
        in_specs=[blk_spec(0), blk_spec(np_), blk_spec(2 * np_), blk_spec(0), blk_spec(0), blk_spec(0),
                  gspec, gspec],
        out_specs=[blk_spec(0), blk_spec(0), blk_spec(0),
                   pl.BlockSpec((1, 8, pair), lambda b, p: (b * np_ + p, 0, 0))],
        out_shape=[jax.ShapeDtypeStruct((T, DA), BF16)] * 3 + [jax.ShapeDtypeStruct((B * np_, 8, pair), F32)],
        scratch_shapes=[pltpu.VMEM((S, pair), F32)] * 8,
        sem=("parallel", "parallel"), name="attn_bwd", carry=carry)


CT = 32
CPAD = 32


def _shifted(win, offsets):
    rolled, out = {}, {}
    n = win.shape[0]
    for o in offsets:
        sub = o % 8
        if sub not in rolled:
            rolled[sub] = win if sub == 0 else pltpu.roll(win, n - sub, 0)
        out[o] = rolled[sub][o - sub:o - sub + CT, :]
    return out


def _ln_fwd(y, g, b):
    mu = jnp.mean(y, axis=-1, keepdims=True)
    yc = y - mu
    rstd = lax.rsqrt(jnp.mean(yc * yc, axis=-1, keepdims=True) + EPS)
    xhat = yc * rstd
    return xhat, rstd, xhat * g + b


def _fill_glu(ca_ref, cg_ref, glu, S):
    glu[pl.ds(0, CPAD), :] = jnp.zeros((CPAD, DC), F32)

    def fill(i, c):
        rows = pl.ds(pl.multiple_of(i * 256, 256), 256)
        a = ca_ref[rows, :].astype(F32)
        gt = cg_ref[rows, :].astype(F32)
        glu[pl.ds(pl.multiple_of(CPAD + i * 256, CT), 256), :] = a * _sigmoid(gt)
        return c

    lax.fori_loop(0, S // 256, fill, 0)


def _conv_fwd(u, cw, cb, lg, lb, B, S):
    T = B * S

    def body(ca_ref, cg_ref, w_ref, b_ref, lg_ref, lb_ref, o_ref, y_ref, glu):
        _fill_glu(ca_ref, cg_ref, glu, S)

        def step(i, c):
            t0 = pl.multiple_of(i * CT, CT)
            win = glu[pl.ds(t0, 2 * CT), :]
            acc = jnp.zeros((CT, DC), F32) + b_ref[...]
            taps = _shifted(win, [k + 2 for k in range(CK)])
            for k in range(CK):
                acc = acc + taps[k + 2] * w_ref[k:k + 1, :]
            y_ref[pl.ds(t0, CT), :] = acc
            _, _, z = _ln_fwd(acc, lg_ref[...], lb_ref[...])
            o_ref[pl.ds(t0, CT), :] = (z * _sigmoid(z)).astype(BF16)
            return c

        lax.fori_loop(0, S // CT, step, 0, unroll=2)

    vec = pl.BlockSpec((1, DC), lambda b: (0, 0))
    return pl.pallas_call(
        body, grid=(B,),
        in_specs=[pl.BlockSpec((S, DC), lambda b: (b, 3)), pl.BlockSpec((S, DC), lambda b: (b, 4)),
                  pl.BlockSpec((CT, DC), lambda b: (0, 0)), vec, vec, vec],
        out_specs=[pl.BlockSpec((S, DC), lambda b: (b, 0))] * 2,
        out_shape=[jax.ShapeDtypeStruct((T, DC), BF16), jax.ShapeDtypeStruct((T, DC), F32)],
        scratch_shapes=[pltpu.VMEM((CPAD + S, DC), F32)],
        compiler_params=_cp(("parallel",)), name="conv_fwd")(u, u, cw, cb, lg, lb)


def _conv_bwd(u, y, dconv, cw, lg, lb, B, S):
    T = B * S

    def body(ca_ref, cg_ref, y_ref, dc_ref, w_ref, lg_ref, lb_ref,
             dca_ref, dcg_ref, dw_ref, ds_ref, glu, dyp, dwacc):
        _fill_glu(ca_ref, cg_ref, glu, S)
        dyp[pl.ds(S, CPAD), :] = jnp.zeros((CPAD, DC), F32)
        lgv, lbv = lg_ref[...], lb_ref[...]

        def sum8(v):
            return v[0:8] + v[8:16] + v[16:24] + v[24:32]

        def p1(i, carry):
            sb, sg, sl = carry
            t0 = pl.multiple_of(i * CT, CT)
            xhat, rstd, z = _ln_fwd(y_ref[pl.ds(t0, CT), :], lgv, lbv)
            sz = _sigmoid(z)
            dz = dc_ref[pl.ds(t0, CT), :].astype(F32) * (sz * (1.0 + z * (1.0 - sz)))
            dxhat = dz * lgv
            dy = rstd * (dxhat - jnp.mean(dxhat, axis=-1, keepdims=True)
                         - xhat * jnp.mean(dxhat * xhat, axis=-1, keepdims=True))
            dyp[pl.ds(t0, CT), :] = dy
            return sb + sum8(dy), sg + sum8(dz * xhat), sl + sum8(dz)

        z8 = jnp.zeros((8, DC), F32)
        sb, sg, sl = lax.fori_loop(0, S // CT, p1, (z8, z8, z8))
        rs = lambda v: jnp.sum(v, axis=0, keepdims=True)
        ds_ref[...] = jnp.concatenate([rs(sb), rs(sg), rs(sl), jnp.zeros((5, DC), F32)], axis=0)[None]

        def p2(i, c):
            t0 = pl.multiple_of(i * CT, CT)
            win = dyp[pl.ds(t0, 2 * CT), :]
            acc = jnp.zeros((CT, DC), F32)
            taps = _shifted(win, [30 - k for k in range(CK)])
            for k in range(CK):
                acc = acc + taps[30 - k] * w_ref[k:k + 1, :]
            a = ca_ref[pl.ds(t0, CT), :].astype(F32)
            sgt = _sigmoid(cg_ref[pl.ds(t0, CT), :].astype(F32))
            dca_ref[pl.ds(t0, CT), :] = (acc * sgt).astype(BF16)
            dcg_ref[pl.ds(t0, CT), :] = (acc * a * sgt * (1.0 - sgt)).astype(BF16)
            return c

        lax.fori_loop(0, S // CT, p2, 0)

        dwacc[...] = jnp.zeros_like(dwacc)

        def p3(i, c):
            t0 = pl.multiple_of(i * CT, CT)
            win = glu[pl.ds(t0, 2 * CT), :]
            dy = dyp[pl.ds(t0, CT), :]
            for k in range(CK):
                dwacc[k] += sum8(dy * win[k + 2:k + 2 + CT, :])
            return c

        lax.fori_loop(0, S // CT, p3, 0)
        dw_ref[...] = jnp.sum(dwacc[...], axis=1)[None]

    vec = pl.BlockSpec((1, DC), lambda b: (0, 0))
    seq = pl.BlockSpec((S, DC), lambda b: (b, 0))
    return pl.pallas_call(
        body, grid=(B,),
        in_specs=[pl.BlockSpec((S, DC), lambda b: (b, 3)), pl.BlockSpec((S, DC), lambda b: (b, 4)),
                  seq, seq, pl.BlockSpec((CT, DC), lambda b: (0, 0)), vec, vec],
        out_specs=[seq, seq, pl.BlockSpec((1, CT, DC), lambda b: (b, 0, 0)),
                   pl.BlockSpec((1, 8, DC), lambda b: (b, 0, 0))],
        out_shape=[jax.ShapeDtypeStruct((T, DC), BF16)] * 2
                  + [jax.ShapeDtypeStruct((B, CT, DC), F32), jax.ShapeDtypeStruct((B, 8, DC), F32)],
        scratch_shapes=[pltpu.VMEM((CPAD + S, DC), F32), pltpu.VMEM((S + CPAD, DC), F32),
                        pltpu.VMEM((CT, 8, DC), F32)],
        compiler_params=_cp(("parallel",)), name="conv_bwd")(u, u, y, dconv, cw, lg, lb)


def _local_step(x, target, norms, W, B, S, comm=None):
    qg2 = jnp.concatenate([norms["q_norm"], norms["q_norm"]], axis=1)
    kg2 = jnp.concatenate([norms["k_norm"], norms["k_norm"]], axis=1)
    cw = jnp.concatenate([W["conv_w"], jnp.zeros((1, DC), F32)], axis=0)

    (h1, n1, g1, u1), got = _ffn_forward(x, norms["ffn1_norm"], W["wg1"], W["wu1"], W["wd1"], None, "ffn1_fwd",
                                         carry=comm.gather_rest if comm else None)
    if comm:
        W = dict(W, **comm.gathered_rest(got))
    u, n2 = _mix_in(h1, norms["mix_norm"], W["win"])
    attn, lse = _attn_fwd(u, qg2, kg2, B, S)
    conv, y = _conv_fwd(u, cw, norms["conv_b"], norms["conv_ln_g"], norms["conv_ln_b"], B, S)
    h2 = _mix_out(h1, attn, conv, W["wout"])
    (h3, n3, g2, u2, dout, dyb, sq), _ = _ffn_forward(h2, norms["ffn2_norm"], W["wg2"], W["wu2"], W["wd2"], target,
                                                     "ffn2_fwd")
    del h3
    loss = (0.5 / D) * jnp.sum(sq)

    (a2, dg2, du2, dh2, dgn_ffn2), _ = _ffn_bwd_act(dyb, g2, u2, h2, dout, norms["ffn2_norm"],
                                                   W["wg2"], W["wu2"], W["wd2"], "ffn2_bwd_act")
    (dwg2, dwu2, dwd2), _ = _ffn_bwd_w(a2, dg2, du2, dyb, n3, "ffn2_bwd_w")
    carry = comm.reduce_start("ffn2", {"wg2": dwg2, "wu2": dwu2, "wd2": dwd2}) if comm else None
    dattn, dconv, dwout = _mix_out_bwd(dh2, attn, conv, W["wout"])
    (dq, dk, dv, dgn_qk), got = _attn_bwd(u, attn, dattn, lse, qg2, kg2, B, S, carry=carry)
    if comm:
        comm.reduce_done(carry, got)
    dca, dcg, dcw, dcs = _conv_bwd(u, y, dconv, cw, norms["conv_ln_g"], norms["conv_ln_b"], B, S)
    dwin, dh1, dyb1, dgn_mix = _mix_in_bwd((dq, dk, dv, dca, dcg), W["win"], n2, h1, dh2, norms["mix_norm"])
    carry = comm.reduce_start("mix", {"win": dwin, "wout": dwout}) if comm else None
    (a1, dg1, du1, gx, dgn_ffn1), got = _ffn_bwd_act(dyb1, g1, u1, x, dh1, norms["ffn1_norm"],
                                                    W["wg1"], W["wu1"], W["wd1"], "ffn1_bwd_act", carry=carry)
    if comm:
        comm.reduce_done(carry, got)
    (dwg1, dwu1, dwd1), _ = _ffn_bwd_w(a1, dg1, du1, dyb1, n1, "ffn1_bwd_w")
    if comm:
        comm.reduce_now("ffn1", {"wg1": dwg1, "wu1": dwu1, "wd1": dwd1})

    qk = jnp.sum(dgn_qk, axis=0)
    cs = jnp.sum(dcs, axis=0)
    small = {
        "ffn1_norm": jnp.sum(dgn_ffn1, axis=0),
        "mix_norm": jnp.sum(dgn_mix, axis=0),
        "q_norm": qk[0:1, 0:HD] + qk[0:1, HD:2 * HD],
        "k_norm": qk[1:2, 0:HD] + qk[1:2, HD:2 * HD],
        "conv_w": jnp.sum(dcw, axis=0)[0:CK],
        "conv_b": cs[0:1],
        "conv_ln_g": cs[1:2],
        "conv_ln_b": cs[2:3],
        "ffn2_norm": jnp.sum(dgn_ffn2, axis=0),
    }
    big = {"wg1": dwg1, "wu1": dwu1, "wd1": dwd1, "win": dwin, "wout": dwout,
           "wg2": dwg2, "wu2": dwu2, "wd2": dwd2}
    return loss, gx, big, small


HBM = pl.BlockSpec(memory_space=pltpu.HBM)
VMEM = pl.BlockSpec(memory_space=pltpu.VMEM)


def _place():
    return lax.axis_index("x"), lax.axis_index("y"), lax.axis_index("c")


class _GatherCarry:
    def __init__(self, shards):
        nt = len(shards)
        self.shards = shards
        self.in_arrays = [s for s, _ in shards]
        self.in_specs = [VMEM] * nt
        self.out_shape = [jax.ShapeDtypeStruct((NDEV * s.shape[0], s.shape[1]), dt) for s, dt in shards]
        self.out_specs = [HBM] * nt
        self.scratch = ([pltpu.VMEM(s.shape, dt) for s, dt in shards]
                        + [pltpu.SemaphoreType.DMA((nt, 7)), pltpu.SemaphoreType.DMA((nt, 7)),
                           pltpu.SemaphoreType.DMA((nt,))])

    def _copies(self, outs, scr):
        nt = len(self.shards)
        stages = scr[:nt]
        send_sems, recv_sems, local_sems = scr[nt:]
        x, y, c = _place()
        me, sibling = (x, y, c), (x, y, 1 - c)
        chips = [(1 - x, y), (x, 1 - y), (1 - x, 1 - y)]

        def rows(t, px, py, pc):
            r = self.shards[t][0].shape[0]
            return outs[t].at[pl.ds((4 * px + 2 * py + pc) * r, r), :]

        def copy(t, k, block, to, src=None):
            return pltpu.make_async_remote_copy(
                src_ref=rows(t, *block) if src is None else src, dst_ref=rows(t, *block),
                send_sem=send_sems.at[t, k], recv_sem=recv_sems.at[t, k],
                device_id=to, device_id_type=MESH)

        mine = [pltpu.make_async_copy(stages[t], rows(t, *me), local_sems.at[t]) for t in range(nt)]
        first = []
        for t in range(nt):
            first.append(copy(t, 0, me, sibling, src=stages[t]))
            first += [copy(t, 1 + j, me, (*chip, c), src=stages[t]) for j, chip in enumerate(chips)]
        return copy, mine, first, me, sibling, chips, c

    def start(self, ins, outs, scr):
        _, mine, first, *_ = self._copies(outs, scr)
        for t, (_, dt) in enumerate(self.shards):
            scr[t][...] = ins[t][...].astype(dt)
        for cp in mine + first:
            cp.start()

    def finish(self, ins, outs, scr):
        copy, mine, first, me, sibling, chips, c = self._copies(outs, scr)
        nt = len(self.shards)
        passed = []
        for j, chip in enumerate(chips):
            for t in range(nt):
                copy(t, 1 + j, (*chip, c), me).wait_recv()
                cp = copy(t, 4 + j, (*chip, c), sibling)
                cp.start()
                passed.append(cp)
        for t in range(nt):
            copy(t, 0, sibling, me).wait_recv()
            for j, chip in enumerate(chips):
                copy(t, 4 + j, (*chip, 1 - c), me).wait_recv()
        for cp in first + passed:
            cp.wait_send()
        for cp in mine:
            cp.wait()


def _run_carry(carry, name):
    def body(*refs):
        n_in, n_out = len(carry.in_arrays), len(carry.out_shape)
        ins, outs, scr = refs[:n_in], refs[n_in:n_in + n_out], refs[n_in + n_out:]
        carry.start(ins, outs, scr)
        carry.finish(ins, outs, scr)

    return pl.pallas_call(
        body, in_specs=carry.in_specs, out_specs=carry.out_specs, out_shape=carry.out_shape,
        scratch_shapes=carry.scratch, compiler_params=pltpu.CompilerParams(vmem_limit_bytes=VMEM_LIMIT),
        name=name)(*carry.in_arrays)


def _sibling_exchange(grads, name):
    nt = len(grads)
    g4 = [g.reshape(4, 2, g.shape[0] // NDEV, g.shape[1]) for g in grads]

    def body(*refs):
        ins, outs = refs[:nt], refs[nt:2 * nt]
        send_sems, recv_sems = refs[2 * nt:]
        x, y, c = _place()
        cps = []
        for t in range(nt):
            cp = pltpu.make_async_remote_copy(
                src_ref=ins[t].at[:, 1 - c], dst_ref=outs[t],
                send_sem=send_sems.at[t], recv_sem=recv_sems.at[t],
                device_id=(x, y, 1 - c), device_id_type=MESH)
            cp.start()
            cps.append(cp)
        for cp in cps:
            cp.wait()

    return pl.pallas_call(
        body, in_specs=[HBM] * nt, out_specs=[HBM] * nt,
        out_shape=[jax.ShapeDtypeStruct((4,) + g.shape[2:], BF16) for g in g4],
        scratch_shapes=[pltpu.SemaphoreType.DMA((nt,)), pltpu.SemaphoreType.DMA((nt,))],
        name=name)(*g4), g4


def _chip_partial(g4, recv, cidx, name):
    _, _, rows, n = g4.shape

    def body(c_ref, a_ref, b_ref, o_ref):
        o_ref[...] = (a_ref[...].astype(F32) + b_ref[...].astype(F32)).astype(BF16)

    return pl.pallas_call(
        body,
        grid_spec=pltpu.PrefetchScalarGridSpec(
            num_scalar_prefetch=1, grid=(4,),
            in_specs=[pl.BlockSpec((None, None, rows, n), lambda q, c_ref: (q, c_ref[0], 0, 0)),
                      pl.BlockSpec((None, rows, n), lambda q, c_ref: (q, 0, 0))],
            out_specs=pl.BlockSpec((None, rows, n), lambda q, c_ref: (q, 0, 0))),
        out_shape=jax.ShapeDtypeStruct((4, rows, n), BF16),
        compiler_params=_cp(("parallel",)), name=name)(cidx, g4, recv)


class _ExchangeCarry:
    def __init__(self, names, parts):
        nt = len(parts)
        self.names = names
        self.in_arrays = list(parts)
        self.in_specs = [HBM] * nt
        self.out_shape = [jax.ShapeDtypeStruct(p.shape, BF16) for p in parts]
        self.out_specs = [HBM] * nt
        self.scratch = [pltpu.SemaphoreType.DMA((nt, 3)), pltpu.SemaphoreType.DMA((nt, 3)),
                        pltpu.SemaphoreType.DMA((nt,))]

    def _copies(self, ins, outs, scr):
        send_sems, recv_sems, local_sems = scr
        x, y, c = _place()
        qme = 2 * x + y
        chips = [(1 - x, y), (x, 1 - y), (1 - x, 1 - y)]
        cps = []
        for t in range(len(ins)):
            cps.append(pltpu.make_async_copy(ins[t].at[qme], outs[t].at[qme], local_sems.at[t]))
            for k, (cx, cy) in enumerate(chips):
                cps.append(pltpu.make_async_remote_copy(
                    src_ref=ins[t].at[2 * cx + cy], dst_ref=outs[t].at[qme],
                    send_sem=send_sems.at[t, k], recv_sem=recv_sems.at[t, k],
                    device_id=(cx, cy, c), device_id_type=MESH))
        return cps

    def start(self, ins, outs, scr):
        for cp in self._copies(ins, outs, scr):
            cp.start()

    def finish(self, ins, outs, scr):
        for cp in self._copies(ins, outs, scr):
            cp.wait()


class _Comm:
    def __init__(self, rest_names, rest_shards, cidx):
        self.rest_names = rest_names
        self.gather_rest = _GatherCarry(rest_shards)
        self.cidx = cidx
        self.reduced = {}

    def gathered_rest(self, outs):
        return dict(zip(self.rest_names, outs))

    def reduce_start(self, tag, grads):
        names = list(grads)
        recv, g4 = _sibling_exchange([grads[n] for n in names], "sibling_exchange_" + tag)
        parts = [_chip_partial(g4[i], recv[i], self.cidx, "chip_partial_" + n) for i, n in enumerate(names)]
        return _ExchangeCarry(names, parts)

    def reduce_done(self, carry, outs):
        self.reduced.update(zip(carry.names, outs))

    def reduce_now(self, tag, grads):
        carry = self.reduce_start(tag, grads)
        self.reduce_done(carry, _run_carry(carry, "chip_exchange_" + tag))


def _adamw_math(w, g, m, v):
    m = B1 * m + (1.0 - B1) * g
    v = B2 * v + (1.0 - B2) * (g * g)
    m_hat = m / (1.0 - B1 ** STEP)
    v_hat = v / (1.0 - B2 ** STEP)
    delta = -LR * (m_hat / (jnp.sqrt(v_hat) + AEPS) + WD * w)
    return delta, m, v


def _adamw_big(recv, w, m, v, name):
    def body(r_ref, w_ref, m_ref, v_ref, g_ref, d_ref, mo_ref, vo_ref):
        g = r_ref[0].astype(F32)
        for q in range(1, 4):
            g = g + r_ref[q].astype(F32)
        d, mn, vn = _adamw_math(w_ref[...], g, m_ref[...], v_ref[...])
        g_ref[...] = g
        d_ref[...] = d
        mo_ref[...] = mn
        vo_ref[...] = vn

    return pl.pallas_call(
        body, out_shape=[jax.ShapeDtypeStruct(w.shape, F32)] * 4,
        compiler_params=pltpu.CompilerParams(vmem_limit_bytes=VMEM_LIMIT), name=name)(recv, w, m, v)


SMALL_ROWS = 40
CW_ROWS = 16
SMALL_TOTAL = SMALL_ROWS + NDEV * CW_ROWS


def _small_step(gvec, wvec, mvec, vvec):
    nr = SMALL_ROWS + CW_ROWS

    def body(g_ref, w_ref, m_ref, v_ref, go_ref, d_ref, mo_ref, vo_ref, slots, send_sems, recv_sems):
        x, y, c = _place()
        me = 4 * x + 2 * y + c
        slots[me] = g_ref[...]
        cps = []
        for k in range(1, NDEV):
            fx, fy, fc = (k >> 2) & 1, (k >> 1) & 1, k & 1
            peer = (x ^ fx, y ^ fy, c ^ fc)
            cp = pltpu.make_async_remote_copy(
                src_ref=g_ref, dst_ref=slots.at[me],
                send_sem=send_sems.at[k - 1], recv_sem=recv_sems.at[k - 1],
                device_id=peer, device_id_type=MESH)
            cp.start()
            cps.append(cp)
        for cp in cps:
            cp.wait()
        tot = slots[0]
        for j in range(1, NDEV):
            tot = tot + slots[j]
        slots[0] = tot
        g = jnp.concatenate(
            [tot[0:SMALL_ROWS], slots[0, pl.ds(pl.multiple_of(SMALL_ROWS + me * CW_ROWS, 8), CW_ROWS), :]], axis=0)
        d, mn, vn = _adamw_math(w_ref[...], g, m_ref[...], v_ref[...])
        go_ref[...] = g
        d_ref[...] = d
        mo_ref[...] = mn
        vo_ref[...] = vn

    return pl.pallas_call(
        body, in_specs=[VMEM] * 4, out_specs=[VMEM] * 4,
        out_shape=[jax.ShapeDtypeStruct((nr, 128), F32)] * 4,
        scratch_shapes=[pltpu.VMEM((NDEV, SMALL_TOTAL, 128), F32),
                        pltpu.SemaphoreType.DMA((NDEV - 1,)), pltpu.SemaphoreType.DMA((NDEV - 1,))],
        name="small_step")(gvec, wvec, mvec, vvec)


SMALL_NAMES = ("ffn1_norm", "mix_norm", "ffn2_norm", "conv_b", "conv_ln_g", "conv_ln_b", "q_norm", "k_norm")


LOSS_ROW = 38


def _pack_small(vals, loss=None):
    parts = []
    for n in SMALL_NAMES:
        a = vals[n].reshape(-1)
        if a.shape[0] < 128:
            a = jnp.concatenate([a, jnp.zeros((128 - a.shape[0],), F32)])
        parts.append(a.reshape(-1, 128))
    used = sum(p.shape[0] for p in parts)
    assert used == LOSS_ROW
    tail = jnp.zeros((SMALL_ROWS - used, 128), F32)
    if loss is not None:
        tail = tail.at[0, 0].set(loss)
    parts.append(tail)
    return jnp.concatenate(parts, axis=0)


def _unpack_small(vec, shapes):
    out, r = {}, 0
    for n in SMALL_NAMES:
        size = shapes[n][1]
        nr = max(size // 128, 1)
        out[n] = vec[r:r + nr].reshape(-1)[:size].reshape(1, size)
        r += nr
    return out


def _pack_cw(a):
    flat = a.reshape(-1)
    return jnp.concatenate([flat, jnp.zeros((CW_ROWS * 128 - flat.shape[0],), F32)]).reshape(CW_ROWS, 128)


def _unpack_cw(v):
    return v.reshape(-1)[:CK * HD].reshape(1, CK, HD)


def kernel(x, ffn1_norm, ffn1_w_gate, ffn1_w_up, ffn1_w_down, mix_norm, w_in, q_norm, k_norm, conv_w, conv_b, conv_ln_g, conv_ln_b, w_out, ffn2_norm, ffn2_w_gate, ffn2_w_up, ffn2_w_down, loss_target, m_ffn1_norm, m_ffn1_w_gate, m_ffn1_w_up, m_ffn1_w_down, m_mix_norm, m_w_in, m_q_norm, m_k_norm, m_conv_w, m_conv_b, m_conv_ln_g, m_conv_ln_b, m_w_out, m_ffn2_norm, m_ffn2_w_gate, m_ffn2_w_up, m_ffn2_w_down, v_ffn1_norm, v_ffn1_w_gate, v_ffn1_w_up, v_ffn1_w_down, v_mix_norm, v_w_in, v_q_norm, v_k_norm, v_conv_w, v_conv_b, v_conv_ln_g, v_conv_ln_b, v_w_out, v_ffn2_norm, v_ffn2_w_gate, v_ffn2_w_up, v_ffn2_w_down):
    P = dict(ffn1_norm=ffn1_norm, ffn1_w_gate=ffn1_w_gate, ffn1_w_up=ffn1_w_up, ffn1_w_down=ffn1_w_down,
             mix_norm=mix_norm, w_in=w_in, q_norm=q_norm, k_norm=k_norm, conv_w=conv_w, conv_b=conv_b,
             conv_ln_g=conv_ln_g, conv_ln_b=conv_ln_b, w_out=w_out, ffn2_norm=ffn2_norm,
             ffn2_w_gate=ffn2_w_gate, ffn2_w_up=ffn2_w_up, ffn2_w_down=ffn2_w_down)
    M = dict(ffn1_norm=m_ffn1_norm, ffn1_w_gate=m_ffn1_w_gate, ffn1_w_up=m_ffn1_w_up, ffn1_w_down=m_ffn1_w_down,
             mix_norm=m_mix_norm, w_in=m_w_in, q_norm=m_q_norm, k_norm=m_k_norm, conv_w=m_conv_w, conv_b=m_conv_b,
             conv_ln_g=m_conv_ln_g, conv_ln_b=m_conv_ln_b, w_out=m_w_out, ffn2_norm=m_ffn2_norm,
             ffn2_w_gate=m_ffn2_w_gate, ffn2_w_up=m_ffn2_w_up, ffn2_w_down=m_ffn2_w_down)
    V = dict(ffn1_norm=v_ffn1_norm, ffn1_w_gate=v_ffn1_w_gate, ffn1_w_up=v_ffn1_w_up, ffn1_w_down=v_ffn1_w_down,
             mix_norm=v_mix_norm, w_in=v_w_in, q_norm=v_q_norm, k_norm=v_k_norm, conv_w=v_conv_w, conv_b=v_conv_b,
             conv_ln_g=v_conv_ln_g, conv_ln_b=v_conv_ln_b, w_out=v_w_out, ffn2_norm=v_ffn2_norm,
             ffn2_w_gate=v_ffn2_w_gate, ffn2_w_up=v_ffn2_w_up, ffn2_w_down=v_ffn2_w_down)
    order = ["ffn1_norm", "ffn1_w_gate", "ffn1_w_up", "ffn1_w_down", "mix_norm", "w_in", "q_norm", "k_norm",
             "conv_w", "conv_b", "conv_ln_g", "conv_ln_b", "w_out", "ffn2_norm", "ffn2_w_gate", "ffn2_w_up",
             "ffn2_w_down"]
    B, S, _ = x.shape
    T = B * S
    cidx = lax.axis_index("c").astype(jnp.int32).reshape(1)

    bigs = [("wg1", "ffn1_w_gate", True), ("wu1", "ffn1_w_up", True), ("wd1", "ffn1_w_down", False),
            ("win", "w_in", True), ("wout", "w_out", False),
            ("wg2", "ffn2_w_gate", True), ("wu2", "ffn2_w_up", True), ("wd2", "ffn2_w_down", False)]
    hm = lambda a, tr: jnp.transpose(a[0]) if tr else a[0]
    cw_pad = jnp.zeros((32, 128), F32).at[0:CK, 0:HD].set(conv_w[0])
    now, rest = bigs[:3], bigs[3:]
    gathered = _run_carry(_GatherCarry([(hm(P[pn], tr), BF16) for _, pn, tr in now] + [(cw_pad, F32)]),
                          "gather_first")
    W = {ln: gathered[i] for i, (ln, _, _) in enumerate(now)}
    cwg = gathered[-1].reshape(NDEV, 32, 128)[:, 0:CK, 0:HD]
    W["conv_w"] = jnp.transpose(cwg, (1, 0, 2)).reshape(CK, DC)
    norms = {n: P[n] for n in SMALL_NAMES}
    comm = _Comm([ln for ln, _, _ in rest], [(hm(P[pn], tr), BF16) for _, pn, tr in rest], cidx)

    loss_part, gx, _, small = _local_step(x.reshape(T, D), loss_target.reshape(T, D), norms, W, B, S, comm)

    G, Dl, Mn, Vn = {}, {}, {}, {}
    for ln, pn, tr in bigs:
        outs = _adamw_big(comm.reduced[ln], hm(P[pn], tr), hm(M[pn], tr), hm(V[pn], tr), "adamw_" + ln)
        G[pn], Dl[pn], Mn[pn], Vn[pn] = [(jnp.transpose(o) if tr else o)[None] for o in outs]

    dcw = small["conv_w"].reshape(CK, NDEV, HD).transpose(1, 0, 2)
    gvec = jnp.concatenate([_pack_small(small, loss_part)] + [_pack_cw(dcw[j]) for j in range(NDEV)], axis=0)
    pack = lambda dct: jnp.concatenate([_pack_small({n: dct[n] for n in SMALL_NAMES}), _pack_cw(dct["conv_w"][0])], axis=0)
    go, do, mo, vo = _small_step(gvec, pack(P), pack(M), pack(V))
    loss = go[LOSS_ROW, 0]
    shapes = {n: P[n].shape for n in SMALL_NAMES}
    for dst, vec in ((G, go), (Dl, do), (Mn, mo), (Vn, vo)):
        dst.update(_unpack_small(vec[0:SMALL_ROWS], shapes))
        dst["conv_w"] = _unpack_cw(vec[SMALL_ROWS:])

    return (loss, gx.reshape(B, S, D), *[G[n] for n in order], *[Dl[n] for n in order],
            *[Mn[n] for n in order], *[Vn[n] for n in order])
```

```python
import functools

import jax
import jax.numpy as jnp
from jax import lax
from jax.experimental import pallas as pl
from jax.experimental.pallas import tpu as pltpu

F32 = jnp.float32
BF16 = jnp.bfloat16

D = 1024
FF = 2816
HD = 64
DA = 512
DC = 512
DIN = 2560
CK = 31
BLK = 128
DILS = (1, 4, 16)
EPS = 1e-6
NDEV = 8
MESH = pl.DeviceIdType.MESH

LR, B1, B2, AEPS, WD, STEP = 0.001, 0.9, 0.999, 1e-08, 0.01, 10

NT = (((1,), (1,)), ((), ()))
TN = (((0,), (0,)), ((), ()))

VMEM_LIMIT = 56 * 1024 * 1024


def _cp(sem=None):
    return pltpu.CompilerParams(dimension_semantics=sem, vmem_limit_bytes=VMEM_LIMIT)


def _sigmoid(x):
    return 0.5 * (jnp.tanh(0.5 * x) + 1.0)


def _pallas(body, args, *, grid, in_specs, out_specs, out_shape, scratch_shapes, sem, name, carry=None):
    if carry is None:
        outs = pl.pallas_call(body, grid=grid, in_specs=in_specs, out_specs=out_specs, out_shape=out_shape,
                              scratch_shapes=scratch_shapes, compiler_params=_cp(sem), name=name)(*args)
        return outs, None
    n_in, n_out, n_scr = len(in_specs), len(out_shape), len(scratch_shapes)
    c_in, c_out = len(carry.in_arrays), len(carry.out_shape)

    def wrapped(*refs):
        ins, refs = refs[:n_in], refs[n_in:]
        cins, refs = refs[:c_in], refs[c_in:]
        outs, refs = refs[:n_out], refs[n_out:]
        couts, refs = refs[:c_out], refs[c_out:]
        scr, cscr = refs[:n_scr], refs[n_scr:]
        ids = [pl.program_id(a) for a in range(len(grid))]
        is_first = functools.reduce(jnp.logical_and, [i == 0 for i in ids])
        is_last = functools.reduce(jnp.logical_and, [i == n - 1 for i, n in zip(ids, grid)])

        @pl.when(is_first)
        def _():
            carry.start(cins, couts, cscr)

        body(*ins, *outs, *scr)

        @pl.when(is_last)
        def _():
            carry.finish(cins, couts, cscr)

    outs = pl.pallas_call(
        wrapped, grid=grid, in_specs=list(in_specs) + carry.in_specs, out_specs=list(out_specs) + carry.out_specs,
        out_shape=list(out_shape) + carry.out_shape, scratch_shapes=list(scratch_shapes) + carry.scratch,
        compiler_params=_cp(("arbitrary",) * len(grid)), name=name)(*args, *carry.in_arrays)
    return outs[:n_out], outs[n_out:]


def _ffn_fwd(x, gain, wg, wu, wd, target, name, carry=None):
    T = x.shape[0]
    tm, tf = 1024, 256
    nt, nf = T // tm, FF // tf
    with_loss = target is not None

    def body(*refs):
        if with_loss:
            (x_ref, gain_ref, wg_ref, wu_ref, wd_ref, t_ref,
             h_ref, n_ref, g_ref, u_ref, dout_ref, dyb_ref, sq_ref, nb_scr, acc_scr) = refs
        else:
            (x_ref, gain_ref, wg_ref, wu_ref, wd_ref,
             h_ref, n_ref, g_ref, u_ref, nb_scr, acc_scr) = refs
        f = pl.program_id(1)

        @pl.when(f == 0)
        def _():
            xv = x_ref[...]
            r = lax.rsqrt(jnp.mean(xv * xv, axis=-1, keepdims=True) + EPS)
            nb = (xv * r * gain_ref[...]).astype(BF16)
            nb_scr[...] = nb
            n_ref[...] = nb
            acc_scr[...] = jnp.zeros_like(acc_scr)

        nb = nb_scr[...]
        g = lax.dot_general(nb, wg_ref[...], NT, preferred_element_type=F32)
        u = lax.dot_general(nb, wu_ref[...], NT, preferred_element_type=F32)
        a = g * _sigmoid(g) * u
        g_ref[...] = g.astype(BF16)
        u_ref[...] = u.astype(BF16)
        acc_scr[...] += jnp.dot(a.astype(BF16), wd_ref[...], preferred_element_type=F32)

        @pl.when(f == nf - 1)
        def _():
            h = x_ref[...] + 0.5 * acc_scr[...]
            h_ref[...] = h
            if with_loss:
                e = h - t_ref[...]
                dout = e * (1.0 / D)
                dout_ref[...] = dout
                dyb_ref[...] = (0.5 * dout).astype(BF16)
                sq_ref[...] = jnp.sum(e * e, axis=0, keepdims=True)[None]

    row = pl.BlockSpec((tm, D), lambda t, f: (t, 0))
    wspec = pl.BlockSpec((tf, D), lambda t, f: (f, 0))
    gspec = pl.BlockSpec((tm, tf), lambda t, f: (t, f))
    in_specs = [row, pl.BlockSpec((1, D), lambda t, f: (0, 0)), wspec, wspec, wspec]
    out_shape = [jax.ShapeDtypeStruct((T, D), F32), jax.ShapeDtypeStruct((T, D), BF16),
                 jax.ShapeDtypeStruct((T, FF), BF16), jax.ShapeDtypeStruct((T, FF), BF16)]
    out_specs = [row, row, gspec, gspec]
    args = [x, gain, wg, wu, wd]
    if with_loss:
        in_specs.append(row)
        args.append(target)
        out_shape += [jax.ShapeDtypeStruct((T, D), F32), jax.ShapeDtypeStruct((T, D), BF16),
                      jax.ShapeDtypeStruct((nt, 1, D), F32)]
        out_specs += [row, row, pl.BlockSpec((1, 1, D), lambda t, f: (t, 0, 0))]
    return _pallas(
        body, args, grid=(nt, nf), in_specs=in_specs, out_specs=out_specs, out_shape=out_shape,
        scratch_shapes=[pltpu.VMEM((tm, D), BF16), pltpu.VMEM((tm, D), F32)],
        sem=("parallel", "arbitrary"), name=name, carry=carry)


def _ffn_bwd(dyb, nb, g, u, wg, wu, wd, name, carry=None):
    T = dyb.shape[0]
    tm, tf = 512, 256
    nt, nf = T // tm, FF // tf

    def body(dy_ref, n_ref, g_ref, u_ref, wg_ref, wu_ref, wd_ref,
             dwg_ref, dwu_ref, dwd_ref, dn_ref, ag_scr, au_scr, ad_scr):
        f, t = pl.program_id(0), pl.program_id(1)

        @pl.when(t == 0)
        def _():
            ag_scr[...] = jnp.zeros_like(ag_scr)
            au_scr[...] = jnp.zeros_like(au_scr)
            ad_scr[...] = jnp.zeros_like(ad_scr)

        dy = dy_ref[...]
        n = n_ref[...]
        gv = g_ref[...].astype(F32)
        uv = u_ref[...].astype(F32)
        da = lax.dot_general(dy, wd_ref[...], NT, preferred_element_type=F32)
        sg = _sigmoid(gv)
        silu = gv * sg
        ab = (silu * uv).astype(BF16)
        dgb = (da * uv * (sg * (1.0 + gv * (1.0 - sg)))).astype(BF16)
        dub = (da * silu).astype(BF16)
        ad_scr[...] += lax.dot_general(ab, dy, TN, preferred_element_type=F32)
        ag_scr[...] += lax.dot_general(dgb, n, TN, preferred_element_type=F32)
        au_scr[...] += lax.dot_general(dub, n, TN, preferred_element_type=F32)
        dn = (jnp.dot(dgb, wg_ref[...], preferred_element_type=F32)
              + jnp.dot(dub, wu_ref[...], preferred_element_type=F32))
        rows = pl.ds(pl.multiple_of(t * tm, tm), tm)

        @pl.when(f == 0)
        def _():
            dn_ref[rows, :] = dn

        @pl.when(f > 0)
        def _():
            dn_ref[rows, :] += dn

        @pl.when(t == nt - 1)
        def _():
            dwg_ref[...] = ag_scr[...].astype(BF16)
            dwu_ref[...] = au_scr[...].astype(BF16)
            dwd_ref[...] = ad_scr[...].astype(BF16)

    row = pl.BlockSpec((tm, D), lambda f, t: (t, 0))
    gspec = pl.BlockSpec((tm, tf), lambda f, t: (t, f))
    wspec = pl.BlockSpec((tf, D), lambda f, t: (f, 0))
    return _pallas(
        body, (dyb, nb, g, u, wg, wu, wd), grid=(nf, nt),
        in_specs=[row, row, gspec, gspec, wspec, wspec, wspec],
        out_specs=[wspec, wspec, wspec, pl.BlockSpec((T, D), lambda f, t: (0, 0))],
        out_shape=[jax.ShapeDtypeStruct((FF, D), BF16)] * 3 + [jax.ShapeDtypeStruct((T, D), F32)],
        scratch_shapes=[pltpu.VMEM((tf, D), F32)] * 3,
        sem=("arbitrary", "arbitrary"), name=name, carry=carry)


FC = 256


def _resident(shape):
    return pl.BlockSpec(shape, lambda *_: (0,) * len(shape), pipeline_mode=pl.Buffered(1))


def _ffn_forward(x, gain, wg, wu, wd, target, name, carry=None):
    T = x.shape[0]
    tm = 512
    nt = T // tm
    with_loss = target is not None

    def body(*refs):
        if with_loss:
            (x_ref, gain_ref, wg_ref, wu_ref, wd_ref, t_ref,
             h_ref, n_ref, g_ref, u_ref, dout_ref, dyb_ref, sq_ref, a_scr) = refs
        else:
            x_ref, gain_ref, wg_ref, wu_ref, wd_ref, h_ref, n_ref, g_ref, u_ref, a_scr = refs
        xv = x_ref[...]
        r = lax.rsqrt(jnp.mean(xv * xv, axis=-1, keepdims=True) + EPS)
        n_ref[...] = (xv * r * gain_ref[...]).astype(BF16)
        for c in range(FF // FC):
            cols = slice(c * FC, (c + 1) * FC)
            nb = n_ref[...]
            g = lax.dot_general(nb, wg_ref[cols, :], NT, preferred_element_type=F32)
            u = lax.dot_general(nb, wu_ref[cols, :], NT, preferred_element_type=F32)
            g_ref[:, cols] = g.astype(BF16)
            u_ref[:, cols] = u.astype(BF16)
            a_scr[:, cols] = (g * _sigmoid(g) * u).astype(BF16)
        h = xv + 0.5 * jnp.dot(a_scr[...], wd_ref[...], preferred_element_type=F32)
        h_ref[...] = h
        if with_loss:
            e = h - t_ref[...]
            dout = e * (1.0 / D)
            dout_ref[...] = dout
            dyb_ref[...] = (0.5 * dout).astype(BF16)
            sq_ref[...] = jnp.sum(e * e, axis=0, keepdims=True)[None]

    row = pl.BlockSpec((tm, D), lambda t: (t, 0))
    wide = pl.BlockSpec((tm, FF), lambda t: (t, 0))
    in_specs = [row, _resident((1, D)), _resident((FF, D)), _resident((FF, D)), _resident((FF, D))]
    out_shape = [jax.ShapeDtypeStruct((T, D), F32), jax.ShapeDtypeStruct((T, D), BF16),
                 jax.ShapeDtypeStruct((T, FF), BF16), jax.ShapeDtypeStruct((T, FF), BF16)]
    out_specs = [row, row, wide, wide]
    args = [x, gain, wg, wu, wd]
    if with_loss:
        in_specs.append(row)
        args.append(target)
        out_shape += [jax.ShapeDtypeStruct((T, D), F32), jax.ShapeDtypeStruct((T, D), BF16),
                      jax.ShapeDtypeStruct((nt, 1, D), F32)]
        out_specs += [row, row, pl.BlockSpec((1, 1, D), lambda t: (t, 0, 0))]
    return _pallas(
        body, args, grid=(nt,), in_specs=in_specs, out_specs=out_specs, out_shape=out_shape,
        scratch_shapes=[pltpu.VMEM((tm, FF), BF16)], sem=("parallel",), name=name, carry=carry)


def _ffn_bwd_act(dyb, g, u, x, dout, gain, wg, wu, wd, name, carry=None):
    T = x.shape[0]
    tm = 256
    nt = T // tm

    def body(dy_ref, g_ref, u_ref, x_ref, dout_ref, gain_ref, wg_ref, wu_ref, wd_ref,
             a_ref, dg_ref, du_ref, dx_ref, dgn_ref):
        for c in range(FF // FC):
            cols = slice(c * FC, (c + 1) * FC)
            da = lax.dot_general(dy_ref[...], wd_ref[cols, :], NT, preferred_element_type=F32)
            gv = g_ref[:, cols].astype(F32)
            uv = u_ref[:, cols].astype(F32)
            sg = _sigmoid(gv)
            silu = gv * sg
            a_ref[:, cols] = (silu * uv).astype(BF16)
            dg_ref[:, cols] = (da * uv * (sg * (1.0 + gv * (1.0 - sg)))).astype(BF16)
            du_ref[:, cols] = (da * silu).astype(BF16)
        dn = (jnp.dot(dg_ref[...], wg_ref[...], preferred_element_type=F32)
              + jnp.dot(du_ref[...], wu_ref[...], preferred_element_type=F32))
        dx, dgain = _rms_bwd_rows(dn, x_ref[...], gain_ref[...])
        dx_ref[...] = dout_ref[...] + dx
        dgn_ref[...] = dgain[None]

    row = pl.BlockSpec((tm, D), lambda t: (t, 0))
    wide = pl.BlockSpec((tm, FF), lambda t: (t, 0))
    return _pallas(
        body, (dyb, g, u, x, dout, gain, wg, wu, wd), grid=(nt,),
        in_specs=[row, wide, wide, row, row, _resident((1, D)), _resident((FF, D)), _resident((FF, D)),
                  _resident((FF, D))],
        out_specs=[wide, wide, wide, row, pl.BlockSpec((1, 1, D), lambda t: (t, 0, 0))],
        out_shape=[jax.ShapeDtypeStruct((T, FF), BF16)] * 3
                  + [jax.ShapeDtypeStruct((T, D), F32), jax.ShapeDtypeStruct((nt, 1, D), F32)],
        scratch_shapes=[], sem=("parallel",), name=name, carry=carry)


def _ffn_bwd_w(lhs, rhs, name, carry=None):
    T = rhs.shape[0]
    tf = 256

    def body(l_ref, r_ref, dw_ref):
        dw_ref[...] = lax.dot_general(l_ref[...], r_ref[...], TN, preferred_element_type=F32).astype(BF16)

    (dw,), got = _pallas(
        body, (lhs, rhs), grid=(FF // tf,),
        in_specs=[pl.BlockSpec((T, tf), lambda f: (0, f)), _resident((T, D))],
        out_specs=[pl.BlockSpec((tf, D), lambda f: (f, 0))], out_shape=[jax.ShapeDtypeStruct((FF, D), BF16)],
        scratch_shapes=[], sem=("parallel",), name=name, carry=carry)
    return dw, got


def _rms_bwd_rows(dn, xv, gain):
    r = lax.rsqrt(jnp.mean(xv * xv, axis=-1, keepdims=True) + EPS)
    xhat = xv * r
    dxhat = dn * gain
    dx = r * (dxhat - xhat * jnp.mean(dxhat * xhat, axis=-1, keepdims=True))
    return dx, jnp.sum(dn * xhat, axis=0, keepdims=True)


def _norm_bwd(dn, x, dout, gain, name):
    T = x.shape[0]
    tm = 512
    nt = T // tm

    def body(dn_ref, x_ref, dout_ref, gain_ref, dx_ref, dg_ref):
        dx, dgain = _rms_bwd_rows(dn_ref[...], x_ref[...], gain_ref[...])
        dx_ref[...] = dout_ref[...] + dx
        dg_ref[...] = dgain[None]

    row = pl.BlockSpec((tm, D), lambda t: (t, 0))
    return pl.pallas_call(
        body, grid=(nt,), in_specs=[row, row, row, pl.BlockSpec((1, D), lambda t: (0, 0))],
        out_specs=[row, pl.BlockSpec((1, 1, D), lambda t: (t, 0, 0))],
        out_shape=[jax.ShapeDtypeStruct((T, D), F32), jax.ShapeDtypeStruct((nt, 1, D), F32)],
        compiler_params=_cp(("parallel",)), name=name)(dn, x, dout, gain)


def _mix_in(h, gain, win):
    T = h.shape[0]
    tm = 512

    def body(h_ref, gain_ref, w_ref, u_ref, n_ref):
        xv = h_ref[...]
        r = lax.rsqrt(jnp.mean(xv * xv, axis=-1, keepdims=True) + EPS)
        nb = (xv * r * gain_ref[...]).astype(BF16)
        n_ref[...] = nb
        u_ref[...] = lax.dot_general(nb, w_ref[...], NT, preferred_element_type=F32).astype(BF16)

    row = pl.BlockSpec((tm, D), lambda t: (t, 0))
    return pl.pallas_call(
        body, grid=(T // tm,),
        in_specs=[row, pl.BlockSpec((1, D), lambda t: (0, 0)), pl.BlockSpec((DIN, D), lambda t: (0, 0))],
        out_specs=[pl.BlockSpec((tm, DIN), lambda t: (t, 0)), row],
        out_shape=[jax.ShapeDtypeStruct((T, DIN), BF16), jax.ShapeDtypeStruct((T, D), BF16)],
        compiler_params=_cp(("parallel",)), name="mix_in")(h, gain, win)


def _mix_out(h, attn, conv, wout):
    T = h.shape[0]
    tm = 512

    def body(h_ref, a_ref, c_ref, w_ref, o_ref):
        o_ref[...] = (h_ref[...]
                      + jnp.dot(a_ref[...], w_ref[0:DA, :], preferred_element_type=F32)
                      + jnp.dot(c_ref[...], w_ref[DA:D, :], preferred_element_type=F32))

    row = pl.BlockSpec((tm, D), lambda t: (t, 0))
    half = pl.BlockSpec((tm, DA), lambda t: (t, 0))
    return pl.pallas_call(
        body, grid=(T // tm,),
        in_specs=[row, half, half, pl.BlockSpec((D, D), lambda t: (0, 0))],
        out_specs=row, out_shape=jax.ShapeDtypeStruct((T, D), F32),
        compiler_params=_cp(("parallel",)), name="mix_out")(h, attn, conv, wout)


def _mix_out_bwd(dh, attn, conv, wout):
    T = dh.shape[0]
    tm = 512
    nt = T // tm

    def body(dh_ref, a_ref, c_ref, w_ref, da_ref, dc_ref, dw_ref, acc_scr):
        t = pl.program_id(0)

        @pl.when(t == 0)
        def _():
            acc_scr[...] = jnp.zeros_like(acc_scr)

        dhb = dh_ref[...].astype(BF16)
        dmix = lax.dot_general(dhb, w_ref[...], NT, preferred_element_type=F32)
        da_ref[...] = dmix[:, 0:DA].astype(BF16)
        dc_ref[...] = dmix[:, DA:D].astype(BF16)
        acc_scr[0:DA, :] += lax.dot_general(a_ref[...], dhb, TN, preferred_element_type=F32)
        acc_scr[DA:D, :] += lax.dot_general(c_ref[...], dhb, TN, preferred_element_type=F32)

        @pl.when(t == nt - 1)
        def _():
            dw_ref[...] = acc_scr[...].astype(BF16)

    row = pl.BlockSpec((tm, D), lambda t: (t, 0))
    half = pl.BlockSpec((tm, DA), lambda t: (t, 0))
    full = pl.BlockSpec((D, D), lambda t: (0, 0))
    return pl.pallas_call(
        body, grid=(nt,), in_specs=[row, half, half, full], out_specs=[half, half, full],
        out_shape=[jax.ShapeDtypeStruct((T, DA), BF16)] * 2 + [jax.ShapeDtypeStruct((D, D), BF16)],
        scratch_shapes=[pltpu.VMEM((D, D), F32)],
        compiler_params=_cp(("arbitrary",)), name="mix_out_bwd")(dh, attn, conv, wout)


def _mix_in_bwd(dparts, win, nb, h, dh, gain):
    T = h.shape[0]
    tm = 512
    nt = T // tm

    def body(d0, d1, d2, d3, d4, w_ref, n_ref, h_ref, dh_ref, gain_ref,
             dw_ref, dx_ref, dyb_ref, dg_ref, acc_scr):
        t = pl.program_id(0)

        @pl.when(t == 0)
        def _():
            acc_scr[...] = jnp.zeros_like(acc_scr)

        n = n_ref[...]
        dn = jnp.zeros((tm, D), F32)
        for i, d_ref in enumerate((d0, d1, d2, d3, d4)):
            dv = d_ref[...]
            dn = dn + jnp.dot(dv, w_ref[i * DA:(i + 1) * DA, :], preferred_element_type=F32)
            acc_scr[i * DA:(i + 1) * DA, :] += lax.dot_general(dv, n, TN, preferred_element_type=F32)
        dx, dgain = _rms_bwd_rows(dn, h_ref[...], gain_ref[...])
        tot = dh_ref[...] + dx
        dx_ref[...] = tot
        dyb_ref[...] = (0.5 * tot).astype(BF16)
        dg_ref[...] = dgain[None]

        @pl.when(t == nt - 1)
        def _():
            dw_ref[...] = acc_scr[...].astype(BF16)

    row = pl.BlockSpec((tm, D), lambda t: (t, 0))
    half = pl.BlockSpec((tm, DA), lambda t: (t, 0))
    full = pl.BlockSpec((DIN, D), lambda t: (0, 0))
    return pl.pallas_call(
        body, grid=(nt,),
        in_specs=[half] * 5 + [full, row, row, row, pl.BlockSpec((1, D), lambda t: (0, 0))],
        out_specs=[full, row, row, pl.BlockSpec((1, 1, D), lambda t: (t, 0, 0))],
        out_shape=[jax.ShapeDtypeStruct((DIN, D), BF16), jax.ShapeDtypeStruct((T, D), F32),
                   jax.ShapeDtypeStruct((T, D), BF16), jax.ShapeDtypeStruct((nt, 1, D), F32)],
        scratch_shapes=[pltpu.VMEM((DIN, D), F32)],
        compiler_params=_cp(("arbitrary",)), name="mix_in_bwd")(*dparts, win, nb, h, dh, gain)


def _head_masks():
    lane = lax.broadcasted_iota(jnp.int32, (1, 2 * HD), 1)
    m0 = lane < HD
    return m0, jnp.logical_not(m0)


def _head_sums(xv):
    ri = lax.broadcasted_iota(jnp.int32, (2 * HD, 2 * HD), 0)
    ci = lax.broadcasted_iota(jnp.int32, (2 * HD, 2 * HD), 1)
    ones = jnp.where((ri < HD) == (ci < HD), 1.0, 0.0).astype(BF16)
    hi = xv.astype(BF16)
    lo = (xv - hi.astype(F32)).astype(BF16)
    return (jnp.dot(hi, ones, preferred_element_type=F32) + jnp.dot(lo, ones, preferred_element_type=F32))


def _head_rms(xv):
    return lax.rsqrt(_head_sums(xv * xv) * (1.0 / HD) + EPS)


def _band_mask(first):
    qi = lax.broadcasted_iota(jnp.int32, (BLK, 2 * BLK), 0)
    ci = lax.broadcasted_iota(jnp.int32, (BLK, 2 * BLK), 1)
    band = (ci >= qi) & (ci <= qi + BLK)
    return band & ((ci >= BLK) | jnp.logical_not(first))


def _block_rows(j, d, seg):
    r, n = j // seg, j % seg
    start = r + (d * BLK) * n
    first = n == 0
    prev = jnp.where(first, start, start - d * BLK)
    return pl.ds(start, BLK, stride=d), pl.ds(prev, BLK, stride=d), first


def _block_keys(refs, cur, prev, first, single):
    if single:
        qi = lax.broadcasted_iota(jnp.int32, (BLK, BLK), 0)
        ci = lax.broadcasted_iota(jnp.int32, (BLK, BLK), 1)
        return [r[cur, :].astype(BF16) for r in refs], ci <= qi
    return ([jnp.concatenate([r[prev, :], r[cur, :]], axis=0).astype(BF16) for r in refs], _band_mask(first))


def _attn_fwd(u, qg2, kg2, B, S, carry=None):
    T = B * S
    NB = S // BLK
    scale = HD ** -0.5

    def body(q_ref, k_ref, v_ref, qg_ref, kg_ref, o_ref, lse_ref, qn, kn, vn, os_, ls_):
        m0, m1 = _head_masks()
        qv = q_ref[...].astype(F32)
        qn[...] = qv * _head_rms(qv) * (qg_ref[...] * scale)
        kv = k_ref[...].astype(F32)
        kn[...] = kv * _head_rms(kv) * kg_ref[...]
        vn[...] = v_ref[...].astype(F32)

        for i, d in enumerate(DILS):
            seg = NB // d

            def blk(j, c, i=i, d=d, seg=seg):
                cur, prev, first = _block_rows(j, d, seg)
                qv_ = qn[cur, :].astype(BF16)
                (kk, vv), mask = _block_keys((kn, vn), cur, prev, first, False)
                outs, lses = [], []
                for mh in (m0, m1):
                    qh = jnp.where(mh, qv_, jnp.zeros_like(qv_))
                    s = lax.dot_general(qh, kk, NT, preferred_element_type=F32)
                    s = jnp.where(mask, s, -1e30)
                    mx = jnp.max(s, axis=-1, keepdims=True)
                    p = jnp.exp(s - mx)
                    l = jnp.sum(p, axis=-1, keepdims=True)
                    outs.append(jnp.dot((p * (1.0 / l)).astype(BF16), vv, preferred_element_type=F32))
                    lses.append(mx + jnp.log(l))
                os_[i, cur, :] = jnp.where(m0, outs[0], outs[1])
                ls_[i, cur, :] = jnp.where(m0, lses[0], lses[1])
                return c

            lax.fori_loop(0, NB, blk, 0, unroll=8)

        def comb(c, carry):
            rows = pl.ds(pl.multiple_of(c * 256, 256), 256)
            l0, l1, l2 = ls_[0, rows, :], ls_[1, rows, :], ls_[2, rows, :]
            mx = jnp.maximum(jnp.maximum(l0, l1), l2)
            e0, e1, e2 = jnp.exp(l0 - mx), jnp.exp(l1 - mx), jnp.exp(l2 - mx)
            tot = e0 + e1 + e2
            inv = 1.0 / tot
            o = (e0 * os_[0, rows, :] + e1 * os_[1, rows, :] + e2 * os_[2, rows, :]) * inv
            o_ref[rows, :] = o.astype(BF16)
            lse_ref[rows, :] = mx + jnp.log(tot)
            return carry

        lax.fori_loop(0, S // 256, comb, 0)

    pair = 2 * HD
    blk_spec = lambda off: pl.BlockSpec((S, pair), lambda b, p, off=off: (b, off + p))
    gspec = pl.BlockSpec((1, pair), lambda b, p: (0, 0))
    return _pallas(
        body, (u, u, u, qg2, kg2), grid=(B, DA // pair),
        in_specs=[blk_spec(0), blk_spec(DA // pair), blk_spec(2 * DA // pair), gspec, gspec],
        out_specs=[blk_spec(0), blk_spec(0)],
        out_shape=[jax.ShapeDtypeStruct((T, DA), BF16), jax.ShapeDtypeStruct((T, DA), F32)],
        scratch_shapes=[pltpu.VMEM((S, pair), F32)] * 3 + [pltpu.VMEM((3, S, pair), F32)] * 2,
        sem=("parallel", "parallel"), name="attn_fwd", carry=carry)


def _attn_bwd(u, attn, dattn, lse, qg2, kg2, B, S, carry=None):
    T = B * S
    NB = S // BLK
    scale = HD ** -0.5
    pair = 2 * HD

    def body(q_ref, k_ref, v_ref, o_ref, do_ref, lse_ref, qg_ref, kg_ref,
             dq_ref, dk_ref, dv_ref, dgn_ref,
             qn, kn, vn, don, ldl, accq, acck, accv):
        m0, m1 = _head_masks()
        lane = lax.broadcasted_iota(jnp.int32, (1, pair), 1)
        qv = q_ref[...].astype(F32)
        qn[...] = qv * _head_rms(qv) * (qg_ref[...] * scale)
        kv = k_ref[...].astype(F32)
        kn[...] = kv * _head_rms(kv) * kg_ref[...]
        vn[...] = v_ref[...].astype(F32)
        dov = do_ref[...].astype(F32)
        don[...] = dov
        ldl[...] = jnp.where((lane % HD) < HD // 2, lse_ref[...], _head_sums(dov * o_ref[...].astype(F32)))
        accq[...] = jnp.zeros_like(accq)
        acck[...] = jnp.zeros_like(acck)
        accv[...] = jnp.zeros_like(accv)

        for i, d in enumerate(DILS):
            seg = NB // d

            def blk(j, c, d=d, seg=seg):
                cur, prev, first = _block_rows(j, d, seg)
                qv_ = qn[cur, :].astype(BF16)
                (kk, vv), mask = _block_keys((kn, vn), cur, prev, first, seg == 1)
                dov_ = don[cur, :].astype(BF16)
                ldv = ldl[cur, :]
                dq_acc = jnp.zeros((BLK, pair), F32)
                dk_acc = jnp.zeros(kk.shape, F32)
                dv_acc = jnp.zeros(kk.shape, F32)
                for hi, mh in enumerate((m0, m1)):
                    lcol = slice(hi * HD, hi * HD + 1)
                    dcol = slice(hi * HD + HD // 2, hi * HD + HD // 2 + 1)
                    qh = jnp.where(mh, qv_, jnp.zeros_like(qv_))
                    doh = jnp.where(mh, dov_, jnp.zeros_like(dov_))
                    s = lax.dot_general(qh, kk, NT, preferred_element_type=F32)
                    p = jnp.where(mask, jnp.exp(s - ldv[:, lcol]), 0.0)
                    dp = lax.dot_general(doh, vv, NT, preferred_element_type=F32)
                    ds = (p * (dp - ldv[:, dcol])).astype(BF16)
                    pb = p.astype(BF16)
                    dq_acc = dq_acc + jnp.where(mh, jnp.dot(ds, kk, preferred_element_type=F32), 0.0)
                    dk_acc = dk_acc + lax.dot_general(ds, qh, TN, preferred_element_type=F32)
                    dv_acc = dv_acc + lax.dot_general(pb, doh, TN, preferred_element_type=F32)
                accq[cur, :] += dq_acc
                if seg == 1:
                    acck[cur, :] += dk_acc
                    accv[cur, :] += dv_acc
                else:
                    acck[prev, :] += dk_acc[0:BLK]
                    acck[cur, :] += dk_acc[BLK:2 * BLK]
                    accv[prev, :] += dv_acc[0:BLK]
                    accv[cur, :] += dv_acc[BLK:2 * BLK]
                return c

            lax.fori_loop(0, NB, blk, 0, unroll=4)

        def norm_bwd(x_ref, dn, gain):
            xv = x_ref[...].astype(F32)
            r = _head_rms(xv)
            xhat = xv * r
            dxhat = dn * gain
            dx = r * (dxhat - xhat * (_head_sums(dxhat * xhat) * (1.0 / HD)))
            return dx, jnp.sum(dn * xhat, axis=0, keepdims=True)

        dq, dgq = norm_bwd(q_ref, accq[...], qg_ref[...] * scale)
        dk, dgk = norm_bwd(k_ref, acck[...], kg_ref[...])
        dq_ref[...] = dq.astype(BF16)
        dk_ref[...] = dk.astype(BF16)
        dv_ref[...] = accv[...].astype(BF16)
        dgn_ref[...] = jnp.concatenate([dgq * scale, dgk, jnp.zeros((6, pair), F32)], axis=0)[None]

    blk_spec = lambda off: pl.BlockSpec((S, pair), lambda b, p, off=off: (b, off + p))
    gspec = pl.BlockSpec((1, pair), lambda b, p: (0, 0))
    np_ = DA // pair
    return _pallas(
        body, (u, u, u, attn, dattn, lse, qg2, kg2), grid=(B, np_),
        in_specs=[blk_spec(0), blk_spec(np_), blk_spec(2 * np_), blk_spec(0), blk_spec(0), blk_spec(0),
                  gspec, gspec],
        out_specs=[blk_spec(0), blk_spec(0), blk_spec(0),
                   pl.BlockSpec((1, 8, pair), lambda b, p: (b * np_ + p, 0, 0))],
        out_shape=[jax.ShapeDtypeStruct((T, DA), BF16)] * 3 + [jax.ShapeDtypeStruct((B * np_, 8, pair), F32)],
        scratch_shapes=[pltpu.VMEM((S, pair), F32)] * 8,
        sem=("parallel", "parallel"), name="attn_bwd", carry=carry)


CT = 32
CPAD = 32


def _shifted(win, offsets):
    rolled, out = {}, {}
    n = win.shape[0]
    for o in offsets:
        sub = o % 8
        if sub not in rolled:
            rolled[sub] = win if sub == 0 else pltpu.roll(win, n - sub, 0)
        out[o] = rolled[sub][o - sub:o - sub + CT, :]
    return out


def _ln_fwd(y, g, b):
    mu = jnp.mean(y, axis=-1, keepdims=True)
    yc = y - mu
    rstd = lax.rsqrt(jnp.mean(yc * yc, axis=-1, keepdims=True) + EPS)
    xhat = yc * rstd
    return xhat, rstd, xhat * g + b


def _fill_glu(ca_ref, cg_ref, glu, S):
    glu[pl.ds(0, CPAD), :] = jnp.zeros((CPAD, DC), F32)

    def fill(i, c):
        rows = pl.ds(pl.multiple_of(i * 256, 256), 256)
        a = ca_ref[rows, :].astype(F32)
        gt = cg_ref[rows, :].astype(F32)
        glu[pl.ds(pl.multiple_of(CPAD + i * 256, CT), 256), :] = a * _sigmoid(gt)
        return c

    lax.fori_loop(0, S // 256, fill, 0)


def _conv_fwd(u, cw, cb, lg, lb, B, S):
    T = B * S

    def body(ca_ref, cg_ref, w_ref, b_ref, lg_ref, lb_ref, o_ref, y_ref, glu):
        _fill_glu(ca_ref, cg_ref, glu, S)

        def step(i, c):
            t0 = pl.multiple_of(i * CT, CT)
            win = glu[pl.ds(t0, 2 * CT), :]
            acc = jnp.zeros((CT, DC), F32) + b_ref[...]
            taps = _shifted(win, [k + 2 for k in range(CK)])
            for k in range(CK):
                acc = acc + taps[k + 2] * w_ref[k:k + 1, :]
            y_ref[pl.ds(t0, CT), :] = acc
            _, _, z = _ln_fwd(acc, lg_ref[...], lb_ref[...])
            o_ref[pl.ds(t0, CT), :] = (z * _sigmoid(z)).astype(BF16)
            return c

        lax.fori_loop(0, S // CT, step, 0, unroll=2)

    vec = pl.BlockSpec((1, DC), lambda b: (0, 0))
    return pl.pallas_call(
        body, grid=(B,),
        in_specs=[pl.BlockSpec((S, DC), lambda b: (b, 3)), pl.BlockSpec((S, DC), lambda b: (b, 4)),
                  pl.BlockSpec((CT, DC), lambda b: (0, 0)), vec, vec, vec],
        out_specs=[pl.BlockSpec((S, DC), lambda b: (b, 0))] * 2,
        out_shape=[jax.ShapeDtypeStruct((T, DC), BF16), jax.ShapeDtypeStruct((T, DC), F32)],
        scratch_shapes=[pltpu.VMEM((CPAD + S, DC), F32)],
        compiler_params=_cp(("parallel",)), name="conv_fwd")(u, u, cw, cb, lg, lb)


def _conv_bwd(u, y, dconv, cw, lg, lb, B, S):
    T = B * S

    def body(ca_ref, cg_ref, y_ref, dc_ref, w_ref, lg_ref, lb_ref,
             dca_ref, dcg_ref, dw_ref, ds_ref, glu, dyp, dwacc):
        _fill_glu(ca_ref, cg_ref, glu, S)
        dyp[pl.ds(S, CPAD), :] = jnp.zeros((CPAD, DC), F32)
        lgv, lbv = lg_ref[...], lb_ref[...]

        def sum8(v):
            return v[0:8] + v[8:16] + v[16:24] + v[24:32]

        def p1(i, carry):
            sb, sg, sl = carry
            t0 = pl.multiple_of(i * CT, CT)
            xhat, rstd, z = _ln_fwd(y_ref[pl.ds(t0, CT), :], lgv, lbv)
            sz = _sigmoid(z)
            dz = dc_ref[pl.ds(t0, CT), :].astype(F32) * (sz * (1.0 + z * (1.0 - sz)))
            dxhat = dz * lgv
            dy = rstd * (dxhat - jnp.mean(dxhat, axis=-1, keepdims=True)
                         - xhat * jnp.mean(dxhat * xhat, axis=-1, keepdims=True))
            dyp[pl.ds(t0, CT), :] = dy
            return sb + sum8(dy), sg + sum8(dz * xhat), sl + sum8(dz)

        z8 = jnp.zeros((8, DC), F32)
        sb, sg, sl = lax.fori_loop(0, S // CT, p1, (z8, z8, z8))
        rs = lambda v: jnp.sum(v, axis=0, keepdims=True)
        ds_ref[...] = jnp.concatenate([rs(sb), rs(sg), rs(sl), jnp.zeros((5, DC), F32)], axis=0)[None]

        def p2(i, c):
            t0 = pl.multiple_of(i * CT, CT)
            win = dyp[pl.ds(t0, 2 * CT), :]
            acc = jnp.zeros((CT, DC), F32)
            taps = _shifted(win, [30 - k for k in range(CK)])
            for k in range(CK):
                acc = acc + taps[30 - k] * w_ref[k:k + 1, :]
            a = ca_ref[pl.ds(t0, CT), :].astype(F32)
            sgt = _sigmoid(cg_ref[pl.ds(t0, CT), :].astype(F32))
            dca_ref[pl.ds(t0, CT), :] = (acc * sgt).astype(BF16)
            dcg_ref[pl.ds(t0, CT), :] = (acc * a * sgt * (1.0 - sgt)).astype(BF16)
            return c

        lax.fori_loop(0, S // CT, p2, 0)

        dwacc[...] = jnp.zeros_like(dwacc)

        def p3(i, c):
            t0 = pl.multiple_of(i * CT, CT)
            win = glu[pl.ds(t0, 2 * CT), :]
            dy = dyp[pl.ds(t0, CT), :]
            for k in range(CK):
                dwacc[k] += sum8(dy * win[k + 2:k + 2 + CT, :])
            return c

        lax.fori_loop(0, S // CT, p3, 0)
        dw_ref[...] = jnp.sum(dwacc[...], axis=1)[None]

    vec = pl.BlockSpec((1, DC), lambda b: (0, 0))
    seq = pl.BlockSpec((S, DC), lambda b: (b, 0))
    return pl.pallas_call(
        body, grid=(B,),
        in_specs=[pl.BlockSpec((S, DC), lambda b: (b, 3)), pl.BlockSpec((S, DC), lambda b: (b, 4)),
                  seq, seq, pl.BlockSpec((CT, DC), lambda b: (0, 0)), vec, vec],
        out_specs=[seq, seq, pl.BlockSpec((1, CT, DC), lambda b: (b, 0, 0)),
                   pl.BlockSpec((1, 8, DC), lambda b: (b, 0, 0))],
        out_shape=[jax.ShapeDtypeStruct((T, DC), BF16)] * 2
                  + [jax.ShapeDtypeStruct((B, CT, DC), F32), jax.ShapeDtypeStruct((B, 8, DC), F32)],
        scratch_shapes=[pltpu.VMEM((CPAD + S, DC), F32), pltpu.VMEM((S + CPAD, DC), F32),
                        pltpu.VMEM((CT, 8, DC), F32)],
        compiler_params=_cp(("parallel",)), name="conv_bwd")(u, u, y, dconv, cw, lg, lb)


def _local_step(x, target, norms, W, B, S, comm=None):
    qg2 = jnp.concatenate([norms["q_norm"], norms["q_norm"]], axis=1)
    kg2 = jnp.concatenate([norms["k_norm"], norms["k_norm"]], axis=1)
    cw = jnp.concatenate([W["conv_w"], jnp.zeros((1, DC), F32)], axis=0)

    (h1, n1, g1, u1), got = _ffn_forward(x, norms["ffn1_norm"], W["wg1"], W["wu1"], W["wd1"], None, "ffn1_fwd",
                                         carry=comm.gathers["mix"] if comm else None)
    if comm:
        W = dict(W, **comm.gathered("mix", got))
    u, n2 = _mix_in(h1, norms["mix_norm"], W["win"])
    (attn, lse), got = _attn_fwd(u, qg2, kg2, B, S, carry=comm.gathers["ffn2"] if comm else None)
    if comm:
        W = dict(W, **comm.gathered("ffn2", got))
    conv, y = _conv_fwd(u, cw, norms["conv_b"], norms["conv_ln_g"], norms["conv_ln_b"], B, S)
    h2 = _mix_out(h1, attn, conv, W["wout"])
    (h3, n3, g2, u2, dout, dyb, sq), _ = _ffn_forward(h2, norms["ffn2_norm"], W["wg2"], W["wu2"], W["wd2"], target,
                                                     "ffn2_fwd")
    del h3
    loss = (0.5 / D) * jnp.sum(sq)

    (a2, dg2, du2, dh2, dgn_ffn2), _ = _ffn_bwd_act(dyb, g2, u2, h2, dout, norms["ffn2_norm"],
                                                   W["wg2"], W["wu2"], W["wd2"], "ffn2_bwd_act")
    dwd2, _ = _ffn_bwd_w(a2, dyb, "ffn2_bwd_wd")
    dwg2, _ = _ffn_bwd_w(dg2, n3, "ffn2_bwd_wg")
    dwu2, _ = _ffn_bwd_w(du2, n3, "ffn2_bwd_wu")
    dattn, dconv, dwout = _mix_out_bwd(dh2, attn, conv, W["wout"])
    carry = comm.reduce_start("ffn2", {"wg2": dwg2, "wu2": dwu2, "wd2": dwd2, "wout": dwout}) if comm else None
    (dq, dk, dv, dgn_qk), got = _attn_bwd(u, attn, dattn, lse, qg2, kg2, B, S, carry=carry)
    if comm:
        comm.reduce_done(carry, got)
    dca, dcg, dcw, dcs = _conv_bwd(u, y, dconv, cw, norms["conv_ln_g"], norms["conv_ln_b"], B, S)
    dwin, dh1, dyb1, dgn_mix = _mix_in_bwd((dq, dk, dv, dca, dcg), W["win"], n2, h1, dh2, norms["mix_norm"])
    (a1, dg1, du1, gx, dgn_ffn1), _ = _ffn_bwd_act(dyb1, g1, u1, x, dh1, norms["ffn1_norm"],
                                                  W["wg1"], W["wu1"], W["wd1"], "ffn1_bwd_act")
    carry = comm.reduce_start("win", {"win": dwin}) if comm else None
    dwd1, got = _ffn_bwd_w(a1, dyb1, "ffn1_bwd_wd", carry=carry)
    if comm:
        comm.reduce_done(carry, got)
        carry = comm.reduce_start("wd1", {"wd1": dwd1})
    dwg1, got = _ffn_bwd_w(dg1, n1, "ffn1_bwd_wg", carry=carry)
    if comm:
        comm.reduce_done(carry, got)
        carry = comm.reduce_start("wg1", {"wg1": dwg1})
    dwu1, got = _ffn_bwd_w(du1, n1, "ffn1_bwd_wu", carry=carry)
    if comm:
        comm.reduce_done(carry, got)
        comm.reduce_now("wu1", {"wu1": dwu1})

    qk = jnp.sum(dgn_qk, axis=0)
    cs = jnp.sum(dcs, axis=0)
    small = {
        "ffn1_norm": jnp.sum(dgn_ffn1, axis=0),
        "mix_norm": jnp.sum(dgn_mix, axis=0),
        "q_norm": qk[0:1, 0:HD] + qk[0:1, HD:2 * HD],
        "k_norm": qk[1:2, 0:HD] + qk[1:2, HD:2 * HD],
        "conv_w": jnp.sum(dcw, axis=0)[0:CK],
        "conv_b": cs[0:1],
        "conv_ln_g": cs[1:2],
        "conv_ln_b": cs[2:3],
        "ffn2_norm": jnp.sum(dgn_ffn2, axis=0),
    }
    big = {"wg1": dwg1, "wu1": dwu1, "wd1": dwd1, "win": dwin, "wout": dwout,
           "wg2": dwg2, "wu2": dwu2, "wd2": dwd2}
    return loss, gx, big, small


HBM = pl.BlockSpec(memory_space=pltpu.HBM)
VMEM = pl.BlockSpec(memory_space=pltpu.VMEM)


def _place():
    return lax.axis_index("x"), lax.axis_index("y"), lax.axis_index("c")


class _GatherCarry:
    def __init__(self, shards):
        nt = len(shards)
        self.shards = shards
        self.in_arrays = [s for s, _ in shards]
        self.in_specs = [VMEM] * nt
        self.out_shape = [jax.ShapeDtypeStruct((NDEV * s.shape[0], s.shape[1]), dt) for s, dt in shards]
        self.out_specs = [HBM] * nt
        self.scratch = ([pltpu.VMEM(s.shape, dt) for s, dt in shards]
                        + [pltpu.SemaphoreType.DMA((nt, 7)), pltpu.SemaphoreType.DMA((nt, 7)),
                           pltpu.SemaphoreType.DMA((nt,))])

    def _copies(self, outs, scr):
        nt = len(self.shards)
        stages = scr[:nt]
        send_sems, recv_sems, local_sems = scr[nt:]
        x, y, c = _place()
        me, sibling = (x, y, c), (x, y, 1 - c)
        chips = [(1 - x, y), (x, 1 - y), (1 - x, 1 - y)]

        def rows(t, px, py, pc):
            r = self.shards[t][0].shape[0]
            return outs[t].at[pl.ds((4 * px + 2 * py + pc) * r, r), :]

        def copy(t, k, block, to, src=None):
            return pltpu.make_async_remote_copy(
                src_ref=rows(t, *block) if src is None else src, dst_ref=rows(t, *block),
                send_sem=send_sems.at[t, k], recv_sem=recv_sems.at[t, k],
                device_id=to, device_id_type=MESH)

        mine = [pltpu.make_async_copy(stages[t], rows(t, *me), local_sems.at[t]) for t in range(nt)]
        first = []
        for t in range(nt):
            first.append(copy(t, 0, me, sibling, src=stages[t]))
            first += [copy(t, 1 + j, me, (*chip, c), src=stages[t]) for j, chip in enumerate(chips)]
        return copy, mine, first, me, sibling, chips, c

    def start(self, ins, outs, scr):
        _, mine, first, *_ = self._copies(outs, scr)
        for t, (_, dt) in enumerate(self.shards):
            scr[t][...] = ins[t][...].astype(dt)
        for cp in mine + first:
            cp.start()

    def finish(self, ins, outs, scr):
        copy, mine, first, me, sibling, chips, c = self._copies(outs, scr)
        nt = len(self.shards)
        passed = []
        for j, chip in enumerate(chips):
            for t in range(nt):
                copy(t, 1 + j, (*chip, c), me).wait_recv()
                cp = copy(t, 4 + j, (*chip, c), sibling)
                cp.start()
                passed.append(cp)
        for t in range(nt):
            copy(t, 0, sibling, me).wait_recv()
            for j, chip in enumerate(chips):
                copy(t, 4 + j, (*chip, 1 - c), me).wait_recv()
        for cp in first + passed:
            cp.wait_send()
        for cp in mine:
            cp.wait()


def _run_carry(carry, name):
    def body(*refs):
        n_in, n_out = len(carry.in_arrays), len(carry.out_shape)
        ins, outs, scr = refs[:n_in], refs[n_in:n_in + n_out], refs[n_in + n_out:]
        carry.start(ins, outs, scr)
        carry.finish(ins, outs, scr)

    return pl.pallas_call(
        body, in_specs=carry.in_specs, out_specs=carry.out_specs, out_shape=carry.out_shape,
        scratch_shapes=carry.scratch, compiler_params=pltpu.CompilerParams(vmem_limit_bytes=VMEM_LIMIT),
        name=name)(*carry.in_arrays)


def _sibling_exchange(grads, name):
    nt = len(grads)
    g4 = [g.reshape(4, 2, g.shape[0] // NDEV, g.shape[1]) for g in grads]

    def body(*refs):
        ins, outs = refs[:nt], refs[nt:2 * nt]
        send_sems, recv_sems = refs[2 * nt:]
        x, y, c = _place()
        cps = []
        for t in range(nt):
            cp = pltpu.make_async_remote_copy(
                src_ref=ins[t].at[:, 1 - c], dst_ref=outs[t],
                send_sem=send_sems.at[t], recv_sem=recv_sems.at[t],
                device_id=(x, y, 1 - c), device_id_type=MESH)
            cp.start()
            cps.append(cp)
        for cp in cps:
            cp.wait()

    return pl.pallas_call(
        body, in_specs=[HBM] * nt, out_specs=[HBM] * nt,
        out_shape=[jax.ShapeDtypeStruct((4,) + g.shape[2:], BF16) for g in g4],
        scratch_shapes=[pltpu.SemaphoreType.DMA((nt,)), pltpu.SemaphoreType.DMA((nt,))],
        name=name)(*g4), g4


def _chip_partial(g4, recv, cidx, name):
    _, _, rows, n = g4.shape

    def body(c_ref, a_ref, b_ref, o_ref):
        o_ref[...] = (a_ref[...].astype(F32) + b_ref[...].astype(F32)).astype(BF16)

    return pl.pallas_call(
        body,
        grid_spec=pltpu.PrefetchScalarGridSpec(
            num_scalar_prefetch=1, grid=(4,),
            in_specs=[pl.BlockSpec((None, None, rows, n), lambda q, c_ref: (q, c_ref[0], 0, 0)),
                      pl.BlockSpec((None, rows, n), lambda q, c_ref: (q, 0, 0))],
            out_specs=pl.BlockSpec((None, rows, n), lambda q, c_ref: (q, 0, 0))),
        out_shape=jax.ShapeDtypeStruct((4, rows, n), BF16),
        compiler_params=_cp(("parallel",)), name=name)(cidx, g4, recv)


class _ExchangeCarry:
    def __init__(self, names, parts):
        nt = len(parts)
        self.names = names
        self.in_arrays = list(parts)
        self.in_specs = [HBM] * nt
        self.out_shape = [jax.ShapeDtypeStruct(p.shape, BF16) for p in parts]
        self.out_specs = [HBM] * nt
        self.scratch = [pltpu.SemaphoreType.DMA((nt, 3)), pltpu.SemaphoreType.DMA((nt, 3)),
                        pltpu.SemaphoreType.DMA((nt,))]

    def _copies(self, ins, outs, scr):
        send_sems, recv_sems, local_sems = scr
        x, y, c = _place()
        qme = 2 * x + y
        chips = [(1 - x, y), (x, 1 - y), (1 - x, 1 - y)]
        cps = []
        for t in range(len(ins)):
            cps.append(pltpu.make_async_copy(ins[t].at[qme], outs[t].at[qme], local_sems.at[t]))
            for k, (cx, cy) in enumerate(chips):
                cps.append(pltpu.make_async_remote_copy(
                    src_ref=ins[t].at[2 * cx + cy], dst_ref=outs[t].at[qme],
                    send_sem=send_sems.at[t, k], recv_sem=recv_sems.at[t, k],
                    device_id=(cx, cy, c), device_id_type=MESH))
        return cps

    def start(self, ins, outs, scr):
        for cp in self._copies(ins, outs, scr):
            cp.start()

    def finish(self, ins, outs, scr):
        for cp in self._copies(ins, outs, scr):
            cp.wait()


class _Comm:
    def __init__(self, groups, cidx):
        self.names = {tag: list(g) for tag, g in groups.items()}
        self.gathers = {tag: _GatherCarry(list(g.values())) for tag, g in groups.items()}
        self.cidx = cidx
        self.reduced = {}

    def gathered(self, tag, outs):
        return dict(zip(self.names[tag], outs))

    def reduce_start(self, tag, grads):
        names = list(grads)
        recv, g4 = _sibling_exchange([grads[n] for n in names], "sibling_exchange_" + tag)
        parts = [_chip_partial(g4[i], recv[i], self.cidx, "chip_partial_" + n) for i, n in enumerate(names)]
        return _ExchangeCarry(names, parts)

    def reduce_done(self, carry, outs):
        self.reduced.update(zip(carry.names, outs))

    def reduce_now(self, tag, grads):
        carry = self.reduce_start(tag, grads)
        self.reduce_done(carry, _run_carry(carry, "chip_exchange_" + tag))


def _adamw_math(w, g, m, v):
    m = B1 * m + (1.0 - B1) * g
    v = B2 * v + (1.0 - B2) * (g * g)
    m_hat = m / (1.0 - B1 ** STEP)
    v_hat = v / (1.0 - B2 ** STEP)
    delta = -LR * (m_hat / (jnp.sqrt(v_hat) + AEPS) + WD * w)
    return delta, m, v


def _adamw_big(recv, w, m, v, name):
    def body(r_ref, w_ref, m_ref, v_ref, g_ref, d_ref, mo_ref, vo_ref):
        g = r_ref[0].astype(F32)
        for q in range(1, 4):
            g = g + r_ref[q].astype(F32)
        d, mn, vn = _adamw_math(w_ref[...], g, m_ref[...], v_ref[...])
        g_ref[...] = g
        d_ref[...] = d
        mo_ref[...] = mn
        vo_ref[...] = vn

    return pl.pallas_call(
        body, out_shape=[jax.ShapeDtypeStruct(w.shape, F32)] * 4,
        compiler_params=pltpu.CompilerParams(vmem_limit_bytes=VMEM_LIMIT), name=name)(recv, w, m, v)


SMALL_ROWS = 40
CW_ROWS = 16
SMALL_TOTAL = SMALL_ROWS + NDEV * CW_ROWS


def _small_step(gvec, wvec, mvec, vvec):
    nr = SMALL_ROWS + CW_ROWS

    def body(g_ref, w_ref, m_ref, v_ref, go_ref, d_ref, mo_ref, vo_ref, slots, send_sems, recv_sems):
        x, y, c = _place()
        me = 4 * x + 2 * y + c
        slots[me] = g_ref[...]
        cps = []
        for k in range(1, NDEV):
            fx, fy, fc = (k >> 2) & 1, (k >> 1) & 1, k & 1
            peer = (x ^ fx, y ^ fy, c ^ fc)
            cp = pltpu.make_async_remote_copy(
                src_ref=g_ref, dst_ref=slots.at[me],
                send_sem=send_sems.at[k - 1], recv_sem=recv_sems.at[k - 1],
                device_id=peer, device_id_type=MESH)
            cp.start()
            cps.append(cp)
        for cp in cps:
            cp.wait()
        tot = slots[0]
        for j in range(1, NDEV):
            tot = tot + slots[j]
        slots[0] = tot
        g = jnp.concatenate(
            [tot[0:SMALL_ROWS], slots[0, pl.ds(pl.multiple_of(SMALL_ROWS + me * CW_ROWS, 8), CW_ROWS), :]], axis=0)
        d, mn, vn = _adamw_math(w_ref[...], g, m_ref[...], v_ref[...])
        go_ref[...] = g
        d_ref[...] = d
        mo_ref[...] = mn
        vo_ref[...] = vn

    return pl.pallas_call(
        body, in_specs=[VMEM] * 4, out_specs=[VMEM] * 4,
        out_shape=[jax.ShapeDtypeStruct((nr, 128), F32)] * 4,
        scratch_shapes=[pltpu.VMEM((NDEV, SMALL_TOTAL, 128), F32),
                        pltpu.SemaphoreType.DMA((NDEV - 1,)), pltpu.SemaphoreType.DMA((NDEV - 1,))],
        name="small_step")(gvec, wvec, mvec, vvec)


SMALL_NAMES = ("ffn1_norm", "mix_norm", "ffn2_norm", "conv_b", "conv_ln_g", "conv_ln_b", "q_norm", "k_norm")


LOSS_ROW = 38


def _pack_small(vals, loss=None):
    parts = []
    for n in SMALL_NAMES:
        a = vals[n].reshape(-1)
        if a.shape[0] < 128:
            a = jnp.concatenate([a, jnp.zeros((128 - a.shape[0],), F32)])
        parts.append(a.reshape(-1, 128))
    used = sum(p.shape[0] for p in parts)
    assert used == LOSS_ROW
    tail = jnp.zeros((SMALL_ROWS - used, 128), F32)
    if loss is not None:
        tail = tail.at[0, 0].set(loss)
    parts.append(tail)
    return jnp.concatenate(parts, axis=0)


def _unpack_small(vec, shapes):
    out, r = {}, 0
    for n in SMALL_NAMES:
        size = shapes[n][1]
        nr = max(size // 128, 1)
        out[n] = vec[r:r + nr].reshape(-1)[:size].reshape(1, size)
        r += nr
    return out


def _pack_cw(a):
    flat = a.reshape(-1)
    return jnp.concatenate([flat, jnp.zeros((CW_ROWS * 128 - flat.shape[0],), F32)]).reshape(CW_ROWS, 128)


def _unpack_cw(v):
    return v.reshape(-1)[:CK * HD].reshape(1, CK, HD)


def kernel(x, ffn1_norm, ffn1_w_gate, ffn1_w_up, ffn1_w_down, mix_norm, w_in, q_norm, k_norm, conv_w, conv_b, conv_ln_g, conv_ln_b, w_out, ffn2_norm, ffn2_w_gate, ffn2_w_up, ffn2_w_down, loss_target, m_ffn1_norm, m_ffn1_w_gate, m_ffn1_w_up, m_ffn1_w_down, m_mix_norm, m_w_in, m_q_norm, m_k_norm, m_conv_w, m_conv_b, m_conv_ln_g, m_conv_ln_b, m_w_out, m_ffn2_norm, m_ffn2_w_gate, m_ffn2_w_up, m_ffn2_w_down, v_ffn1_norm, v_ffn1_w_gate, v_ffn1_w_up, v_ffn1_w_down, v_mix_norm, v_w_in, v_q_norm, v_k_norm, v_conv_w, v_conv_b, v_conv_ln_g, v_conv_ln_b, v_w_out, v_ffn2_norm, v_ffn2_w_gate, v_ffn2_w_up, v_ffn2_w_down):
    P = dict(ffn1_norm=ffn1_norm, ffn1_w_gate=ffn1_w_gate, ffn1_w_up=ffn1_w_up, ffn1_w_down=ffn1_w_down,
             mix_norm=mix_norm, w_in=w_in, q_norm=q_norm, k_norm=k_norm, conv_w=conv_w, conv_b=conv_b,
             conv_ln_g=conv_ln_g, conv_ln_b=conv_ln_b, w_out=w_out, ffn2_norm=ffn2_norm,
             ffn2_w_gate=ffn2_w_gate, ffn2_w_up=ffn2_w_up, ffn2_w_down=ffn2_w_down)
    M = dict(ffn1_norm=m_ffn1_norm, ffn1_w_gate=m_ffn1_w_gate, ffn1_w_up=m_ffn1_w_up, ffn1_w_down=m_ffn1_w_down,
             mix_norm=m_mix_norm, w_in=m_w_in, q_norm=m_q_norm, k_norm=m_k_norm, conv_w=m_conv_w, conv_b=m_conv_b,
             conv_ln_g=m_conv_ln_g, conv_ln_b=m_conv_ln_b, w_out=m_w_out, ffn2_norm=m_ffn2_norm,
             ffn2_w_gate=m_ffn2_w_gate, ffn2_w_up=m_ffn2_w_up, ffn2_w_down=m_ffn2_w_down)
    V = dict(ffn1_norm=v_ffn1_norm, ffn1_w_gate=v_ffn1_w_gate, ffn1_w_up=v_ffn1_w_up, ffn1_w_down=v_ffn1_w_down,
             mix_norm=v_mix_norm, w_in=v_w_in, q_norm=v_q_norm, k_norm=v_k_norm, conv_w=v_conv_w, conv_b=v_conv_b,
             conv_ln_g=v_conv_ln_g, conv_ln_b=v_conv_ln_b, w_out=v_w_out, ffn2_norm=v_ffn2_norm,
             ffn2_w_gate=v_ffn2_w_gate, ffn2_w_up=v_ffn2_w_up, ffn2_w_down=v_ffn2_w_down)
    order = ["ffn1_norm", "ffn1_w_gate", "ffn1_w_up", "ffn1_w_down", "mix_norm", "w_in", "q_norm", "k_norm",
             "conv_w", "conv_b", "conv_ln_g", "conv_ln_b", "w_out", "ffn2_norm", "ffn2_w_gate", "ffn2_w_up",
             "ffn2_w_down"]
    B, S, _ = x.shape
    T = B * S
    cidx = lax.axis_index("c").astype(jnp.int32).reshape(1)

    bigs = [("wg1", "ffn1_w_gate", True), ("wu1", "ffn1_w_up", True), ("wd1", "ffn1_w_down", False),
            ("win", "w_in", True), ("wout", "w_out", False),
            ("wg2", "ffn2_w_gate", True), ("wu2", "ffn2_w_up", True), ("wd2", "ffn2_w_down", False)]
    hm = lambda a, tr: jnp.transpose(a[0]) if tr else a[0]
    cw_pad = jnp.zeros((32, 128), F32).at[0:CK, 0:HD].set(conv_w[0])
    now, rest = bigs[:3], bigs[3:]
    gathered = _run_carry(_GatherCarry([(hm(P[pn], tr), BF16) for _, pn, tr in now] + [(cw_pad, F32)]),
                          "gather_first")
    W = {ln: gathered[i] for i, (ln, _, _) in enumerate(now)}
    cwg = gathered[-1].reshape(NDEV, 32, 128)[:, 0:CK, 0:HD]
    W["conv_w"] = jnp.transpose(cwg, (1, 0, 2)).reshape(CK, DC)
    norms = {n: P[n] for n in SMALL_NAMES}
    shard = {ln: (hm(P[pn], tr), BF16) for ln, pn, tr in rest}
    comm = _Comm({"mix": {n: shard[n] for n in ("win", "wout")},
                  "ffn2": {n: shard[n] for n in ("wg2", "wu2", "wd2")}}, cidx)

    loss_part, gx, _, small = _local_step(x.reshape(T, D), loss_target.reshape(T, D), norms, W, B, S, comm)

    G, Dl, Mn, Vn = {}, {}, {}, {}
    for ln, pn, tr in bigs:
        outs = _adamw_big(comm.reduced[ln], hm(P[pn], tr), hm(M[pn], tr), hm(V[pn], tr), "adamw_" + ln)
        G[pn], Dl[pn], Mn[pn], Vn[pn] = [(jnp.transpose(o) if tr else o)[None] for o in outs]

    dcw = small["conv_w"].reshape(CK, NDEV, HD).transpose(1, 0, 2)
    gvec = jnp.concatenate([_pack_small(small, loss_part)] + [_pack_cw(dcw[j]) for j in range(NDEV)], axis=0)
    pack = lambda dct: jnp.concatenate([_pack_small({n: dct[n] for n in SMALL_NAMES}), _pack_cw(dct["conv_w"][0])], axis=0)
    go, do, mo, vo = _small_step(gvec, pack(P), pack(M), pack(V))
    loss = go[LOSS_ROW, 0]
    shapes = {n: P[n].shape for n in SMALL_NAMES}
    for dst, vec in ((G, go), (Dl, do), (Mn, mo), (Vn, vo)):
        dst.update(_unpack_small(vec[0:SMALL_ROWS], shapes))
        dst["conv_w"] = _unpack_cw(vec[SMALL_ROWS:])

    return (loss, gx.reshape(B, S, D), *[G[n] for n in order], *[Dl[n] for n in order],
            *[Mn[n] for n in order], *[Vn[n] for n in order])
```

```python
import functools

import jax
import jax.numpy as jnp
from jax import lax
from jax.experimental import pallas as pl
from jax.experimental.pallas import tpu as pltpu

F32 = jnp.float32
BF16 = jnp.bfloat16

D = 1024
FF = 2816
HD = 64
DA = 512
DC = 512
DIN = 2560
CK = 31
BLK = 128
DILS = (1, 4, 16)
EPS = 1e-6
NDEV = 8
MESH = pl.DeviceIdType.MESH

LR, B1, B2, AEPS, WD, STEP = 0.001, 0.9, 0.999, 1e-08, 0.01, 10

NT = (((1,), (1,)), ((), ()))
TN = (((0,), (0,)), ((), ()))

VMEM_LIMIT = 56 * 1024 * 1024


def _cp(sem=None):
    return pltpu.CompilerParams(dimension_semantics=sem, vmem_limit_bytes=VMEM_LIMIT)


def _sigmoid(x):
    return 0.5 * (jnp.tanh(0.5 * x) + 1.0)


def _pallas(body, args, *, grid, in_specs, out_specs, out_shape, scratch_shapes, sem, name, carry=None):
    if carry is None:
        outs = pl.pallas_call(body, grid=grid, in_specs=in_specs, out_specs=out_specs, out_shape=out_shape,
                              scratch_shapes=scratch_shapes, compiler_params=_cp(sem), name=name)(*args)
        return outs, None
    n_in, n_out, n_scr = len(in_specs), len(out_shape), len(scratch_shapes)
    c_in, c_out = len(carry.in_arrays), len(carry.out_shape)

    def wrapped(*refs):
        ins, refs = refs[:n_in], refs[n_in:]
        cins, refs = refs[:c_in], refs[c_in:]
        outs, refs = refs[:n_out], refs[n_out:]
        couts, refs = refs[:c_out], refs[c_out:]
        scr, cscr = refs[:n_scr], refs[n_scr:]
        ids = [pl.program_id(a) for a in range(len(grid))]
        is_first = functools.reduce(jnp.logical_and, [i == 0 for i in ids])
        is_last = functools.reduce(jnp.logical_and, [i == n - 1 for i, n in zip(ids, grid)])

        @pl.when(is_first)
        def _():
            carry.start(cins, couts, cscr)

        body(*ins, *outs, *scr)

        @pl.when(is_last)
        def _():
            carry.finish(cins, couts, cscr)

    outs = pl.pallas_call(
        wrapped, grid=grid, in_specs=list(in_specs) + carry.in_specs, out_specs=list(out_specs) + carry.out_specs,
        out_shape=list(out_shape) + carry.out_shape, scratch_shapes=list(scratch_shapes) + carry.scratch,
        compiler_params=_cp(("arbitrary",) * len(grid)), name=name)(*args, *carry.in_arrays)
    return outs[:n_out], outs[n_out:]


def _ffn_fwd(x, gain, wg, wu, wd, target, name, carry=None):
    T = x.shape[0]
    tm, tf = 1024, 256
    nt, nf = T // tm, FF // tf
    with_loss = target is not None

    def body(*refs):
        if with_loss:
            (x_ref, gain_ref, wg_ref, wu_ref, wd_ref, t_ref,
             h_ref, n_ref, g_ref, u_ref, dout_ref, dyb_ref, sq_ref, nb_scr, acc_scr) = refs
        else:
            (x_ref, gain_ref, wg_ref, wu_ref, wd_ref,
             h_ref, n_ref, g_ref, u_ref, nb_scr, acc_scr) = refs
        f = pl.program_id(1)

        @pl.when(f == 0)
        def _():
            xv = x_ref[...]
            r = lax.rsqrt(jnp.mean(xv * xv, axis=-1, keepdims=True) + EPS)
            nb = (xv * r * gain_ref[...]).astype(BF16)
            nb_scr[...] = nb
            n_ref[...] = nb
            acc_scr[...] = jnp.zeros_like(acc_scr)

        nb = nb_scr[...]
        g = lax.dot_general(nb, wg_ref[...], NT, preferred_element_type=F32)
        u = lax.dot_general(nb, wu_ref[...], NT, preferred_element_type=F32)
        a = g * _sigmoid(g) * u
        g_ref[...] = g.astype(BF16)
        u_ref[...] = u.astype(BF16)
        acc_scr[...] += jnp.dot(a.astype(BF16), wd_ref[...], preferred_element_type=F32)

        @pl.when(f == nf - 1)
        def _():
            h = x_ref[...] + 0.5 * acc_scr[...]
            h_ref[...] = h
            if with_loss:
                e = h - t_ref[...]
                dout = e * (1.0 / D)
                dout_ref[...] = dout
                dyb_ref[...] = (0.5 * dout).astype(BF16)
                sq_ref[...] = jnp.sum(e * e, axis=0, keepdims=True)[None]

    row = pl.BlockSpec((tm, D), lambda t, f: (t, 0))
    wspec = pl.BlockSpec((tf, D), lambda t, f: (f, 0))
    gspec = pl.BlockSpec((tm, tf), lambda t, f: (t, f))
    in_specs = [row, pl.BlockSpec((1, D), lambda t, f: (0, 0)), wspec, wspec, wspec]
    out_shape = [jax.ShapeDtypeStruct((T, D), F32), jax.ShapeDtypeStruct((T, D), BF16),
                 jax.ShapeDtypeStruct((T, FF), BF16), jax.ShapeDtypeStruct((T, FF), BF16)]
    out_specs = [row, row, gspec, gspec]
    args = [x, gain, wg, wu, wd]
    if with_loss:
        in_specs.append(row)
        args.append(target)
        out_shape += [jax.ShapeDtypeStruct((T, D), F32), jax.ShapeDtypeStruct((T, D), BF16),
                      jax.ShapeDtypeStruct((nt, 1, D), F32)]
        out_specs += [row, row, pl.BlockSpec((1, 1, D), lambda t, f: (t, 0, 0))]
    return _pallas(
        body, args, grid=(nt, nf), in_specs=in_specs, out_specs=out_specs, out_shape=out_shape,
        scratch_shapes=[pltpu.VMEM((tm, D), BF16), pltpu.VMEM((tm, D), F32)],
        sem=("parallel", "arbitrary"), name=name, carry=carry)


def _ffn_bwd(dyb, nb, g, u, wg, wu, wd, name, carry=None):
    T = dyb.shape[0]
    tm, tf = 512, 256
    nt, nf = T // tm, FF // tf

    def body(dy_ref, n_ref, g_ref, u_ref, wg_ref, wu_ref, wd_ref,
             dwg_ref, dwu_ref, dwd_ref, dn_ref, ag_scr, au_scr, ad_scr):
        f, t = pl.program_id(0), pl.program_id(1)

        @pl.when(t == 0)
        def _():
            ag_scr[...] = jnp.zeros_like(ag_scr)
            au_scr[...] = jnp.zeros_like(au_scr)
            ad_scr[...] = jnp.zeros_like(ad_scr)

        dy = dy_ref[...]
        n = n_ref[...]
        gv = g_ref[...].astype(F32)
        uv = u_ref[...].astype(F32)
        da = lax.dot_general(dy, wd_ref[...], NT, preferred_element_type=F32)
        sg = _sigmoid(gv)
        silu = gv * sg
        ab = (silu * uv).astype(BF16)
        dgb = (da * uv * (sg * (1.0 + gv * (1.0 - sg)))).astype(BF16)
        dub = (da * silu).astype(BF16)
        ad_scr[...] += lax.dot_general(ab, dy, TN, preferred_element_type=F32)
        ag_scr[...] += lax.dot_general(dgb, n, TN, preferred_element_type=F32)
        au_scr[...] += lax.dot_general(dub, n, TN, preferred_element_type=F32)
        dn = (jnp.dot(dgb, wg_ref[...], preferred_element_type=F32)
              + jnp.dot(dub, wu_ref[...], preferred_element_type=F32))
        rows = pl.ds(pl.multiple_of(t * tm, tm), tm)

        @pl.when(f == 0)
        def _():
            dn_ref[rows, :] = dn

        @pl.when(f > 0)
        def _():
            dn_ref[rows, :] += dn

        @pl.when(t == nt - 1)
        def _():
            dwg_ref[...] = ag_scr[...].astype(BF16)
            dwu_ref[...] = au_scr[...].astype(BF16)
            dwd_ref[...] = ad_scr[...].astype(BF16)

    row = pl.BlockSpec((tm, D), lambda f, t: (t, 0))
    gspec = pl.BlockSpec((tm, tf), lambda f, t: (t, f))
    wspec = pl.BlockSpec((tf, D), lambda f, t: (f, 0))
    return _pallas(
        body, (dyb, nb, g, u, wg, wu, wd), grid=(nf, nt),
        in_specs=[row, row, gspec, gspec, wspec, wspec, wspec],
        out_specs=[wspec, wspec, wspec, pl.BlockSpec((T, D), lambda f, t: (0, 0))],
        out_shape=[jax.ShapeDtypeStruct((FF, D), BF16)] * 3 + [jax.ShapeDtypeStruct((T, D), F32)],
        scratch_shapes=[pltpu.VMEM((tf, D), F32)] * 3,
        sem=("arbitrary", "arbitrary"), name=name, carry=carry)


FC = 256


def _resident(shape):
    return pl.BlockSpec(shape, lambda *_: (0,) * len(shape), pipeline_mode=pl.Buffered(1))


def _ffn_forward(x, gain, wg, wu, wd, target, name, carry=None):
    T = x.shape[0]
    tm = 512
    nt = T // tm
    with_loss = target is not None

    def body(*refs):
        if with_loss:
            (x_ref, gain_ref, wg_ref, wu_ref, wd_ref, t_ref,
             h_ref, n_ref, g_ref, u_ref, dout_ref, dyb_ref, sq_ref, a_scr) = refs
        else:
            x_ref, gain_ref, wg_ref, wu_ref, wd_ref, h_ref, n_ref, g_ref, u_ref, a_scr = refs
        xv = x_ref[...]
        r = lax.rsqrt(jnp.mean(xv * xv, axis=-1, keepdims=True) + EPS)
        n_ref[...] = (xv * r * gain_ref[...]).astype(BF16)
        for c in range(FF // FC):
            cols = slice(c * FC, (c + 1) * FC)
            nb = n_ref[...]
            g = lax.dot_general(nb, wg_ref[cols, :], NT, preferred_element_type=F32)
            u = lax.dot_general(nb, wu_ref[cols, :], NT, preferred_element_type=F32)
            g_ref[:, cols] = g.astype(BF16)
            u_ref[:, cols] = u.astype(BF16)
            a_scr[:, cols] = (g * _sigmoid(g) * u).astype(BF16)
        h = xv + 0.5 * jnp.dot(a_scr[...], wd_ref[...], preferred_element_type=F32)
        h_ref[...] = h
        if with_loss:
            e = h - t_ref[...]
            dout = e * (1.0 / D)
            dout_ref[...] = dout
            dyb_ref[...] = (0.5 * dout).astype(BF16)
            sq_ref[...] = jnp.sum(e * e, axis=0, keepdims=True)[None]

    row = pl.BlockSpec((tm, D), lambda t: (t, 0))
    wide = pl.BlockSpec((tm, FF), lambda t: (t, 0))
    in_specs = [row, _resident((1, D)), _resident((FF, D)), _resident((FF, D)), _resident((FF, D))]
    out_shape = [jax.ShapeDtypeStruct((T, D), F32), jax.ShapeDtypeStruct((T, D), BF16),
                 jax.ShapeDtypeStruct((T, FF), BF16), jax.ShapeDtypeStruct((T, FF), BF16)]
    out_specs = [row, row, wide, wide]
    args = [x, gain, wg, wu, wd]
    if with_loss:
        in_specs.append(row)
        args.append(target)
        out_shape += [jax.ShapeDtypeStruct((T, D), F32), jax.ShapeDtypeStruct((T, D), BF16),
                      jax.ShapeDtypeStruct((nt, 1, D), F32)]
        out_specs += [row, row, pl.BlockSpec((1, 1, D), lambda t: (t, 0, 0))]
    return _pallas(
        body, args, grid=(nt,), in_specs=in_specs, out_specs=out_specs, out_shape=out_shape,
        scratch_shapes=[pltpu.VMEM((tm, FF), BF16)], sem=("parallel",), name=name, carry=carry)


def _ffn_gate(x, gain, wg, name, carry=None):
    T = x.shape[0]
    tm = 512

    def body(x_ref, gain_ref, wg_ref, n_ref, g_ref):
        xv = x_ref[...]
        r = lax.rsqrt(jnp.mean(xv * xv, axis=-1, keepdims=True) + EPS)
        n_ref[...] = (xv * r * gain_ref[...]).astype(BF16)
        for c in range(FF // FC):
            cols = slice(c * FC, (c + 1) * FC)
            g_ref[:, cols] = lax.dot_general(n_ref[...], wg_ref[cols, :], NT,
                                             preferred_element_type=F32).astype(BF16)

    row = pl.BlockSpec((tm, D), lambda t: (t, 0))
    wide = pl.BlockSpec((tm, FF), lambda t: (t, 0))
    return _pallas(
        body, (x, gain, wg), grid=(T // tm,), in_specs=[row, _resident((1, D)), _resident((FF, D))],
        out_specs=[row, wide], out_shape=[jax.ShapeDtypeStruct((T, D), BF16), jax.ShapeDtypeStruct((T, FF), BF16)],
        scratch_shapes=[], sem=("parallel",), name=name, carry=carry)


def _ffn_up(nb, g, wu, name, carry=None):
    T = nb.shape[0]
    tm = 512

    def body(n_ref, g_ref, wu_ref, u_ref, a_ref):
        for c in range(FF // FC):
            cols = slice(c * FC, (c + 1) * FC)
            u = lax.dot_general(n_ref[...], wu_ref[cols, :], NT, preferred_element_type=F32)
            gv = g_ref[:, cols].astype(F32)
            u_ref[:, cols] = u.astype(BF16)
            a_ref[:, cols] = (gv * _sigmoid(gv) * u).astype(BF16)

    row = pl.BlockSpec((tm, D), lambda t: (t, 0))
    wide = pl.BlockSpec((tm, FF), lambda t: (t, 0))
    return _pallas(
        body, (nb, g, wu), grid=(T // tm,), in_specs=[row, wide, _resident((FF, D))],
        out_specs=[wide, wide], out_shape=[jax.ShapeDtypeStruct((T, FF), BF16)] * 2,
        scratch_shapes=[], sem=("parallel",), name=name, carry=carry)


def _ffn_down(x, a, wd, name, carry=None):
    T = x.shape[0]
    tm = 512

    def body(x_ref, a_ref, wd_ref, h_ref):
        h_ref[...] = x_ref[...] + 0.5 * jnp.dot(a_ref[...], wd_ref[...], preferred_element_type=F32)

    row = pl.BlockSpec((tm, D), lambda t: (t, 0))
    wide = pl.BlockSpec((tm, FF), lambda t: (t, 0))
    return _pallas(
        body, (x, a, wd), grid=(T // tm,), in_specs=[row, wide, _resident((FF, D))],
        out_specs=[row], out_shape=[jax.ShapeDtypeStruct((T, D), F32)],
        scratch_shapes=[], sem=("parallel",), name=name, carry=carry)


def _ffn_bwd_act(dyb, g, u, x, dout, gain, wg, wu, wd, name, carry=None):
    T = x.shape[0]
    tm = 256
    nt = T // tm

    def body(dy_ref, g_ref, u_ref, x_ref, dout_ref, gain_ref, wg_ref, wu_ref, wd_ref,
             a_ref, dg_ref, du_ref, dx_ref, dgn_ref):
        for c in range(FF // FC):
            cols = slice(c * FC, (c + 1) * FC)
            da = lax.dot_general(dy_ref[...], wd_ref[cols, :], NT, preferred_element_type=F32)
            gv = g_ref[:, cols].astype(F32)
            uv = u_ref[:, cols].astype(F32)
            sg = _sigmoid(gv)
            silu = gv * sg
            a_ref[:, cols] = (silu * uv).astype(BF16)
            dg_ref[:, cols] = (da * uv * (sg * (1.0 + gv * (1.0 - sg)))).astype(BF16)
            du_ref[:, cols] = (da * silu).astype(BF16)
        dn = (jnp.dot(dg_ref[...], wg_ref[...], preferred_element_type=F32)
              + jnp.dot(du_ref[...], wu_ref[...], preferred_element_type=F32))
        dx, dgain = _rms_bwd_rows(dn, x_ref[...], gain_ref[...])
        dx_ref[...] = dout_ref[...] + dx
        dgn_ref[...] = dgain[None]

    row = pl.BlockSpec((tm, D), lambda t: (t, 0))
    wide = pl.BlockSpec((tm, FF), lambda t: (t, 0))
    return _pallas(
        body, (dyb, g, u, x, dout, gain, wg, wu, wd), grid=(nt,),
        in_specs=[row, wide, wide, row, row, _resident((1, D)), _resident((FF, D)), _resident((FF, D)),
                  _resident((FF, D))],
        out_specs=[wide, wide, wide, row, pl.BlockSpec((1, 1, D), lambda t: (t, 0, 0))],
        out_shape=[jax.ShapeDtypeStruct((T, FF), BF16)] * 3
                  + [jax.ShapeDtypeStruct((T, D), F32), jax.ShapeDtypeStruct((nt, 1, D), F32)],
        scratch_shapes=[], sem=("parallel",), name=name, carry=carry)


def _ffn_bwd_w(lhs, rhs, name, carry=None):
    T = rhs.shape[0]
    tf = 256

    def body(l_ref, r_ref, dw_ref):
        dw_ref[...] = lax.dot_general(l_ref[...], r_ref[...], TN, preferred_element_type=F32).astype(BF16)

    (dw,), got = _pallas(
        body, (lhs, rhs), grid=(FF // tf,),
        in_specs=[pl.BlockSpec((T, tf), lambda f: (0, f)), _resident((T, D))],
        out_specs=[pl.BlockSpec((tf, D), lambda f: (f, 0))], out_shape=[jax.ShapeDtypeStruct((FF, D), BF16)],
        scratch_shapes=[], sem=("parallel",), name=name, carry=carry)
    return dw, got


def _rms_bwd_rows(dn, xv, gain):
    r = lax.rsqrt(jnp.mean(xv * xv, axis=-1, keepdims=True) + EPS)
    xhat = xv * r
    dxhat = dn * gain
    dx = r * (dxhat - xhat * jnp.mean(dxhat * xhat, axis=-1, keepdims=True))
    return dx, jnp.sum(dn * xhat, axis=0, keepdims=True)


def _norm_bwd(dn, x, dout, gain, name):
    T = x.shape[0]
    tm = 512
    nt = T // tm

    def body(dn_ref, x_ref, dout_ref, gain_ref, dx_ref, dg_ref):
        dx, dgain = _rms_bwd_rows(dn_ref[...], x_ref[...], gain_ref[...])
        dx_ref[...] = dout_ref[...] + dx
        dg_ref[...] = dgain[None]

    row = pl.BlockSpec((tm, D), lambda t: (t, 0))
    return pl.pallas_call(
        body, grid=(nt,), in_specs=[row, row, row, pl.BlockSpec((1, D), lambda t: (0, 0))],
        out_specs=[row, pl.BlockSpec((1, 1, D), lambda t: (t, 0, 0))],
        out_shape=[jax.ShapeDtypeStruct((T, D), F32), jax.ShapeDtypeStruct((nt, 1, D), F32)],
        compiler_params=_cp(("parallel",)), name=name)(dn, x, dout, gain)


def _mix_in(h, gain, win):
    T = h.shape[0]
    tm = 512

    def body(h_ref, gain_ref, w_ref, u_ref, n_ref):
        xv = h_ref[...]
        r = lax.rsqrt(jnp.mean(xv * xv, axis=-1, keepdims=True) + EPS)
        nb = (xv * r * gain_ref[...]).astype(BF16)
        n_ref[...] = nb
        u_ref[...] = lax.dot_general(nb, w_ref[...], NT, preferred_element_type=F32).astype(BF16)

    row = pl.BlockSpec((tm, D), lambda t: (t, 0))
    return pl.pallas_call(
        body, grid=(T // tm,),
        in_specs=[row, pl.BlockSpec((1, D), lambda t: (0, 0)), pl.BlockSpec((DIN, D), lambda t: (0, 0))],
        out_specs=[pl.BlockSpec((tm, DIN), lambda t: (t, 0)), row],
        out_shape=[jax.ShapeDtypeStruct((T, DIN), BF16), jax.ShapeDtypeStruct((T, D), BF16)],
        compiler_params=_cp(("parallel",)), name="mix_in")(h, gain, win)


def _mix_out(h, attn, conv, wout):
    T = h.shape[0]
    tm = 512

    def body(h_ref, a_ref, c_ref, w_ref, o_ref):
        o_ref[...] = (h_ref[...]
                      + jnp.dot(a_ref[...], w_ref[0:DA, :], preferred_element_type=F32)
                      + jnp.dot(c_ref[...], w_ref[DA:D, :], preferred_element_type=F32))

    row = pl.BlockSpec((tm, D), lambda t: (t, 0))
    half = pl.BlockSpec((tm, DA), lambda t: (t, 0))
    return pl.pallas_call(
        body, grid=(T // tm,),
        in_specs=[row, half, half, pl.BlockSpec((D, D), lambda t: (0, 0))],
        out_specs=row, out_shape=jax.ShapeDtypeStruct((T, D), F32),
        compiler_params=_cp(("parallel",)), name="mix_out")(h, attn, conv, wout)


def _mix_out_bwd(dh, attn, conv, wout):
    T = dh.shape[0]
    tm = 512
    nt = T // tm

    def body(dh_ref, a_ref, c_ref, w_ref, da_ref, dc_ref, dw_ref, acc_scr):
        t = pl.program_id(0)

        @pl.when(t == 0)
        def _():
            acc_scr[...] = jnp.zeros_like(acc_scr)

        dhb = dh_ref[...].astype(BF16)
        dmix = lax.dot_general(dhb, w_ref[...], NT, preferred_element_type=F32)
        da_ref[...] = dmix[:, 0:DA].astype(BF16)
        dc_ref[...] = dmix[:, DA:D].astype(BF16)
        acc_scr[0:DA, :] += lax.dot_general(a_ref[...], dhb, TN, preferred_element_type=F32)
        acc_scr[DA:D, :] += lax.dot_general(c_ref[...], dhb, TN, preferred_element_type=F32)

        @pl.when(t == nt - 1)
        def _():
            dw_ref[...] = acc_scr[...].astype(BF16)

    row = pl.BlockSpec((tm, D), lambda t: (t, 0))
    half = pl.BlockSpec((tm, DA), lambda t: (t, 0))
    full = pl.BlockSpec((D, D), lambda t: (0, 0))
    return pl.pallas_call(
        body, grid=(nt,), in_specs=[row, half, half, full], out_specs=[half, half, full],
        out_shape=[jax.ShapeDtypeStruct((T, DA), BF16)] * 2 + [jax.ShapeDtypeStruct((D, D), BF16)],
        scratch_shapes=[pltpu.VMEM((D, D), F32)],
        compiler_params=_cp(("arbitrary",)), name="mix_out_bwd")(dh, attn, conv, wout)


def _mix_in_bwd(dparts, win, nb, h, dh, gain):
    T = h.shape[0]
    tm = 512
    nt = T // tm

    def body(d0, d1, d2, d3, d4, w_ref, n_ref, h_ref, dh_ref, gain_ref,
             dw_ref, dx_ref, dyb_ref, dg_ref, acc_scr):
        t = pl.program_id(0)

        @pl.when(t == 0)
        def _():
            acc_scr[...] = jnp.zeros_like(acc_scr)

        n = n_ref[...]
        dn = jnp.zeros((tm, D), F32)
        for i, d_ref in enumerate((d0, d1, d2, d3, d4)):
            dv = d_ref[...]
            dn = dn + jnp.dot(dv, w_ref[i * DA:(i + 1) * DA, :], preferred_element_type=F32)
            acc_scr[i * DA:(i + 1) * DA, :] += lax.dot_general(dv, n, TN, preferred_element_type=F32)
        dx, dgain = _rms_bwd_rows(dn, h_ref[...], gain_ref[...])
        tot = dh_ref[...] + dx
        dx_ref[...] = tot
        dyb_ref[...] = (0.5 * tot).astype(BF16)
        dg_ref[...] = dgain[None]

        @pl.when(t == nt - 1)
        def _():
            dw_ref[...] = acc_scr[...].astype(BF16)

    row = pl.BlockSpec((tm, D), lambda t: (t, 0))
    half = pl.BlockSpec((tm, DA), lambda t: (t, 0))
    full = pl.BlockSpec((DIN, D), lambda t: (0, 0))
    return pl.pallas_call(
        body, grid=(nt,),
        in_specs=[half] * 5 + [full, row, row, row, pl.BlockSpec((1, D), lambda t: (0, 0))],
        out_specs=[full, row, row, pl.BlockSpec((1, 1, D), lambda t: (t, 0, 0))],
        out_shape=[jax.ShapeDtypeStruct((DIN, D), BF16), jax.ShapeDtypeStruct((T, D), F32),
                   jax.ShapeDtypeStruct((T, D), BF16), jax.ShapeDtypeStruct((nt, 1, D), F32)],
        scratch_shapes=[pltpu.VMEM((DIN, D), F32)],
        compiler_params=_cp(("arbitrary",)), name="mix_in_bwd")(*dparts, win, nb, h, dh, gain)


def _head_masks():
    lane = lax.broadcasted_iota(jnp.int32, (1, 2 * HD), 1)
    m0 = lane < HD
    return m0, jnp.logical_not(m0)


def _head_sums(xv):
    ri = lax.broadcasted_iota(jnp.int32, (2 * HD, 2 * HD), 0)
    ci = lax.broadcasted_iota(jnp.int32, (2 * HD, 2 * HD), 1)
    ones = jnp.where((ri < HD) == (ci < HD), 1.0, 0.0).astype(BF16)
    hi = xv.astype(BF16)
    lo = (xv - hi.astype(F32)).astype(BF16)
    return (jnp.dot(hi, ones, preferred_element_type=F32) + jnp.dot(lo, ones, preferred_element_type=F32))


def _head_rms(xv):
    return lax.rsqrt(_head_sums(xv * xv) * (1.0 / HD) + EPS)


def _band_mask(first):
    qi = lax.broadcasted_iota(jnp.int32, (BLK, 2 * BLK), 0)
    ci = lax.broadcasted_iota(jnp.int32, (BLK, 2 * BLK), 1)
    band = (ci >= qi) & (ci <= qi + BLK)
    return band & ((ci >= BLK) | jnp.logical_not(first))


def _block_rows(j, d, seg):
    r, n = j // seg, j % seg
    start = r + (d * BLK) * n
    first = n == 0
    prev = jnp.where(first, start, start - d * BLK)
    return pl.ds(start, BLK, stride=d), pl.ds(prev, BLK, stride=d), first


def _block_keys(refs, cur, prev, first, single):
    if single:
        qi = lax.broadcasted_iota(jnp.int32, (BLK, BLK), 0)
        ci = lax.broadcasted_iota(jnp.int32, (BLK, BLK), 1)
        return [r[cur, :].astype(BF16) for r in refs], ci <= qi
    return ([jnp.concatenate([r[prev, :], r[cur, :]], axis=0).astype(BF16) for r in refs], _band_mask(first))


def _attn_fwd(u, qg2, kg2, B, S, carry=None):
    T = B * S
    NB = S // BLK
    scale = HD ** -0.5

    def body(q_ref, k_ref, v_ref, qg_ref, kg_ref, o_ref, lse_ref, qn, kn, vn, os_, ls_):
        m0, m1 = _head_masks()
        qv = q_ref[...].astype(F32)
        qn[...] = qv * _head_rms(qv) * (qg_ref[...] * scale)
        kv = k_ref[...].astype(F32)
        kn[...] = kv * _head_rms(kv) * kg_ref[...]
        vn[...] = v_ref[...].astype(F32)

        for i, d in enumerate(DILS):
            seg = NB // d

            def blk(j, c, i=i, d=d, seg=seg):
                cur, prev, first = _block_rows(j, d, seg)
                qv_ = qn[cur, :].astype(BF16)
                (kk, vv), mask = _block_keys((kn, vn), cur, prev, first, False)
                outs, lses = [], []
                for mh in (m0, m1):
                    qh = jnp.where(mh, qv_, jnp.zeros_like(qv_))
                    s = lax.dot_general(qh, kk, NT, preferred_element_type=F32)
                    s = jnp.where(mask, s, -1e30)
                    mx = jnp.max(s, axis=-1, keepdims=True)
                    p = jnp.exp(s - mx)
                    l = jnp.sum(p, axis=-1, keepdims=True)
                    outs.append(jnp.dot((p * (1.0 / l)).astype(BF16), vv, preferred_element_type=F32))
                    lses.append(mx + jnp.log(l))
                os_[i, cur, :] = jnp.where(m0, outs[0], outs[1])
                ls_[i, cur, :] = jnp.where(m0, lses[0], lses[1])
                return c

            lax.fori_loop(0, NB, blk, 0, unroll=8)

        def comb(c, carry):
            rows = pl.ds(pl.multiple_of(c * 256, 256), 256)
            l0, l1, l2 = ls_[0, rows, :], ls_[1, rows, :], ls_[2, rows, :]
            mx = jnp.maximum(jnp.maximum(l0, l1), l2)
            e0, e1, e2 = jnp.exp(l0 - mx), jnp.exp(l1 - mx), jnp.exp(l2 - mx)
            tot = e0 + e1 + e2
            inv = 1.0 / tot
            o = (e0 * os_[0, rows, :] + e1 * os_[1, rows, :] + e2 * os_[2, rows, :]) * inv
            o_ref[rows, :] = o.astype(BF16)
            lse_ref[rows, :] = mx + jnp.log(tot)
            return carry

        lax.fori_loop(0, S // 256, comb, 0)

    pair = 2 * HD
    blk_spec = lambda off: pl.BlockSpec((S, pair), lambda b, p, off=off: (b, off + p))
    gspec = pl.BlockSpec((1, pair), lambda b, p: (0, 0))
    return _pallas(
        body, (u, u, u, qg2, kg2), grid=(B, DA // pair),
        in_specs=[blk_spec(0), blk_spec(DA // pair), blk_spec(2 * DA // pair), gspec, gspec],
        out_specs=[blk_spec(0), blk_spec(0)],
        out_shape=[jax.ShapeDtypeStruct((T, DA), BF16), jax.ShapeDtypeStruct((T, DA), F32)],
        scratch_shapes=[pltpu.VMEM((S, pair), F32)] * 3 + [pltpu.VMEM((3, S, pair), F32)] * 2,
        sem=("parallel", "parallel"), name="attn_fwd", carry=carry)


def _attn_bwd(u, attn, dattn, lse, qg2, kg2, B, S, carry=None):
    T = B * S
    NB = S // BLK
    scale = HD ** -0.5
    pair = 2 * HD

    def body(q_ref, k_ref, v_ref, o_ref, do_ref, lse_ref, qg_ref, kg_ref,
             dq_ref, dk_ref, dv_ref, dgn_ref,
             qn, kn, vn, don, ldl, accq, acck, accv):
        m0, m1 = _head_masks()
        lane = lax.broadcasted_iota(jnp.int32, (1, pair), 1)
        qv = q_ref[...].astype(F32)
        qn[...] = qv * _head_rms(qv) * (qg_ref[...] * scale)
        kv = k_ref[...].astype(F32)
        kn[...] = kv * _head_rms(kv) * kg_ref[...]
        vn[...] = v_ref[...].astype(F32)
        dov = do_ref[...].astype(F32)
        don[...] = dov
        ldl[...] = jnp.where((lane % HD) < HD // 2, lse_ref[...], _head_sums(dov * o_ref[...].astype(F32)))
        accq[...] = jnp.zeros_like(accq)
        acck[...] = jnp.zeros_like(acck)
        accv[...] = jnp.zeros_like(accv)

        for i, d in enumerate(DILS):
            seg = NB // d

            def blk(j, c, d=d, seg=seg):
                cur, prev, first = _block_rows(j, d, seg)
                qv_ = qn[cur, :].astype(BF16)
                (kk, vv), mask = _block_keys((kn, vn), cur, prev, first, seg == 1)
                dov_ = don[cur, :].astype(BF16)
                ldv = ldl[cur, :]
                dq_acc = jnp.zeros((BLK, pair), F32)
                dk_acc = jnp.zeros(kk.shape, F32)
                dv_acc = jnp.zeros(kk.shape, F32)
                for hi, mh in enumerate((m0, m1)):
                    lcol = slice(hi * HD, hi * HD + 1)
                    dcol = slice(hi * HD + HD // 2, hi * HD + HD // 2 + 1)
                    qh = jnp.where(mh, qv_, jnp.zeros_like(qv_))
                    doh = jnp.where(mh, dov_, jnp.zeros_like(dov_))
                    s = lax.dot_general(qh, kk, NT, preferred_element_type=F32)
                    p = jnp.where(mask, jnp.exp(s - ldv[:, lcol]), 0.0)
                    dp = lax.dot_general(doh, vv, NT, preferred_element_type=F32)
                    ds = (p * (dp - ldv[:, dcol])).astype(BF16)
                    pb = p.astype(BF16)
                    dq_acc = dq_acc + jnp.where(mh, jnp.dot(ds, kk, preferred_element_type=F32), 0.0)
                    dk_acc = dk_acc + lax.dot_general(ds, qh, TN, preferred_element_type=F32)
                    dv_acc = dv_acc + lax.dot_general(pb, doh, TN, preferred_element_type=F32)
                accq[cur, :] += dq_acc
                if seg == 1:
                    acck[cur, :] += dk_acc
                    accv[cur, :] += dv_acc
                else:
                    acck[prev, :] += dk_acc[0:BLK]
                    acck[cur, :] += dk_acc[BLK:2 * BLK]
                    accv[prev, :] += dv_acc[0:BLK]
                    accv[cur, :] += dv_acc[BLK:2 * BLK]
                return c

            lax.fori_loop(0, NB, blk, 0, unroll=4)

        def norm_bwd(x_ref, dn, gain):
            xv = x_ref[...].astype(F32)
            r = _head_rms(xv)
            xhat = xv * r
            dxhat = dn * gain
            dx = r * (dxhat - xhat * (_head_sums(dxhat * xhat) * (1.0 / HD)))
            return dx, jnp.sum(dn * xhat, axis=0, keepdims=True)

        dq, dgq = norm_bwd(q_ref, accq[...], qg_ref[...] * scale)
        dk, dgk = norm_bwd(k_ref, acck[...], kg_ref[...])
        dq_ref[...] = dq.astype(BF16)
        dk_ref[...] = dk.astype(BF16)
        dv_ref[...] = accv[...].astype(BF16)
        dgn_ref[...] = jnp.concatenate([dgq * scale, dgk, jnp.zeros((6, pair), F32)], axis=0)[None]

    blk_spec = lambda off: pl.BlockSpec((S, pair), lambda b, p, off=off: (b, off + p))
    gspec = pl.BlockSpec((1, pair), lambda b, p: (0, 0))
    np_ = DA // pair
    return _pallas(
        body, (u, u, u, attn, dattn, lse, qg2, kg2), grid=(B, np_),
        in_specs=[blk_spec(0), blk_spec(np_), blk_spec(2 * np_), blk_spec(0), blk_spec(0), blk_spec(0),
                  gspec, gspec],
        out_specs=[blk_spec(0), blk_spec(0), blk_spec(0),
                   pl.BlockSpec((1, 8, pair), lambda b, p: (b * np_ + p, 0, 0))],
        out_shape=[jax.ShapeDtypeStruct((T, DA), BF16)] * 3 + [jax.ShapeDtypeStruct((B * np_, 8, pair), F32)],
        scratch_shapes=[pltpu.VMEM((S, pair), F32)] * 8,
        sem=("parallel", "parallel"), name="attn_bwd", carry=carry)


CT = 32
CPAD = 32


def _shifted(win, offsets):
    rolled, out = {}, {}
    n = win.shape[0]
    for o in offsets:
        sub = o % 8
        if sub not in rolled:
            rolled[sub] = win if sub == 0 else pltpu.roll(win, n - sub, 0)
        out[o] = rolled[sub][o - sub:o - sub + CT, :]
    return out


def _ln_fwd(y, g, b):
    mu = jnp.mean(y, axis=-1, keepdims=True)
    yc = y - mu
    rstd = lax.rsqrt(jnp.mean(yc * yc, axis=-1, keepdims=True) + EPS)
    xhat = yc * rstd
    return xhat, rstd, xhat * g + b


def _fill_glu(ca_ref, cg_ref, glu, S):
    glu[pl.ds(0, CPAD), :] = jnp.zeros((CPAD, DC), F32)

    def fill(i, c):
        rows = pl.ds(pl.multiple_of(i * 256, 256), 256)
        a = ca_ref[rows, :].astype(F32)
        gt = cg_ref[rows, :].astype(F32)
        glu[pl.ds(pl.multiple_of(CPAD + i * 256, CT), 256), :] = a * _sigmoid(gt)
        return c

    lax.fori_loop(0, S // 256, fill, 0)


def _conv_fwd(u, cw, cb, lg, lb, B, S):
    T = B * S

    def body(ca_ref, cg_ref, w_ref, b_ref, lg_ref, lb_ref, o_ref, y_ref, glu):
        _fill_glu(ca_ref, cg_ref, glu, S)

        def step(i, c):
            t0 = pl.multiple_of(i * CT, CT)
            win = glu[pl.ds(t0, 2 * CT), :]
            acc = jnp.zeros((CT, DC), F32) + b_ref[...]
            taps = _shifted(win, [k + 2 for k in range(CK)])
            for k in range(CK):
                acc = acc + taps[k + 2] * w_ref[k:k + 1, :]
            y_ref[pl.ds(t0, CT), :] = acc
            _, _, z = _ln_fwd(acc, lg_ref[...], lb_ref[...])
            o_ref[pl.ds(t0, CT), :] = (z * _sigmoid(z)).astype(BF16)
            return c

        lax.fori_loop(0, S // CT, step, 0, unroll=2)

    vec = pl.BlockSpec((1, DC), lambda b: (0, 0))
    return pl.pallas_call(
        body, grid=(B,),
        in_specs=[pl.BlockSpec((S, DC), lambda b: (b, 3)), pl.BlockSpec((S, DC), lambda b: (b, 4)),
                  pl.BlockSpec((CT, DC), lambda b: (0, 0)), vec, vec, vec],
        out_specs=[pl.BlockSpec((S, DC), lambda b: (b, 0))] * 2,
        out_shape=[jax.ShapeDtypeStruct((T, DC), BF16), jax.ShapeDtypeStruct((T, DC), F32)],
        scratch_shapes=[pltpu.VMEM((CPAD + S, DC), F32)],
        compiler_params=_cp(("parallel",)), name="conv_fwd")(u, u, cw, cb, lg, lb)


def _conv_bwd(u, y, dconv, cw, lg, lb, B, S):
    T = B * S

    def body(ca_ref, cg_ref, y_ref, dc_ref, w_ref, lg_ref, lb_ref,
             dca_ref, dcg_ref, dw_ref, ds_ref, glu, dyp, dwacc):
        _fill_glu(ca_ref, cg_ref, glu, S)
        dyp[pl.ds(S, CPAD), :] = jnp.zeros((CPAD, DC), F32)
        lgv, lbv = lg_ref[...], lb_ref[...]

        def sum8(v):
            return functools.reduce(jnp.add, [v[r:r + 8] for r in range(0, v.shape[0], 8)])

        P1 = 4 * CT

        def p1(i, carry):
            sb, sg, sl = carry
            t0 = pl.multiple_of(i * P1, P1)
            xhat, rstd, z = _ln_fwd(y_ref[pl.ds(t0, P1), :], lgv, lbv)
            sz = _sigmoid(z)
            dz = dc_ref[pl.ds(t0, P1), :].astype(F32) * (sz * (1.0 + z * (1.0 - sz)))
            dxhat = dz * lgv
            dy = rstd * (dxhat - jnp.mean(dxhat, axis=-1, keepdims=True)
                         - xhat * jnp.mean(dxhat * xhat, axis=-1, keepdims=True))
            dyp[pl.ds(t0, P1), :] = dy
            return sb + sum8(dy), sg + sum8(dz * xhat), sl + sum8(dz)

        z8 = jnp.zeros((8, DC), F32)
        sb, sg, sl = lax.fori_loop(0, S // P1, p1, (z8, z8, z8))
        rs = lambda v: jnp.sum(v, axis=0, keepdims=True)
        ds_ref[...] = jnp.concatenate([rs(sb), rs(sg), rs(sl), jnp.zeros((5, DC), F32)], axis=0)[None]

        def p2(i, c):
            t0 = pl.multiple_of(i * CT, CT)
            win = dyp[pl.ds(t0, 2 * CT), :]
            acc = jnp.zeros((CT, DC), F32)
            taps = _shifted(win, [30 - k for k in range(CK)])
            for k in range(CK):
                acc = acc + taps[30 - k] * w_ref[k:k + 1, :]
            a = ca_ref[pl.ds(t0, CT), :].astype(F32)
            sgt = _sigmoid(cg_ref[pl.ds(t0, CT), :].astype(F32))
            dca_ref[pl.ds(t0, CT), :] = (acc * sgt).astype(BF16)
            dcg_ref[pl.ds(t0, CT), :] = (acc * a * sgt * (1.0 - sgt)).astype(BF16)
            return c

        lax.fori_loop(0, S // CT, p2, 0)

        dwacc[...] = jnp.zeros_like(dwacc)

        def p3(i, c):
            t0 = pl.multiple_of(i * CT, CT)
            win = glu[pl.ds(t0, 2 * CT), :]
            dy = dyp[pl.ds(t0, CT), :]
            for k in range(CK):
                dwacc[k] += sum8(dy * win[k + 2:k + 2 + CT, :])
            return c

        lax.fori_loop(0, S // CT, p3, 0)
        dw_ref[...] = jnp.sum(dwacc[...], axis=1)[None]

    vec = pl.BlockSpec((1, DC), lambda b: (0, 0))
    seq = pl.BlockSpec((S, DC), lambda b: (b, 0))
    return pl.pallas_call(
        body, grid=(B,),
        in_specs=[pl.BlockSpec((S, DC), lambda b: (b, 3)), pl.BlockSpec((S, DC), lambda b: (b, 4)),
                  seq, seq, pl.BlockSpec((CT, DC), lambda b: (0, 0)), vec, vec],
        out_specs=[seq, seq, pl.BlockSpec((1, CT, DC), lambda b: (b, 0, 0)),
                   pl.BlockSpec((1, 8, DC), lambda b: (b, 0, 0))],
        out_shape=[jax.ShapeDtypeStruct((T, DC), BF16)] * 2
                  + [jax.ShapeDtypeStruct((B, CT, DC), F32), jax.ShapeDtypeStruct((B, 8, DC), F32)],
        scratch_shapes=[pltpu.VMEM((CPAD + S, DC), F32), pltpu.VMEM((S + CPAD, DC), F32),
                        pltpu.VMEM((CT, 8, DC), F32)],
        compiler_params=_cp(("parallel",)), name="conv_bwd")(u, u, y, dconv, cw, lg, lb)


def _local_step(x, target, norms, W, B, S, comm=None):
    qg2 = jnp.concatenate([norms["q_norm"], norms["q_norm"]], axis=1)
    kg2 = jnp.concatenate([norms["k_norm"], norms["k_norm"]], axis=1)
    cw = jnp.concatenate([W["conv_w"], jnp.zeros((1, DC), F32)], axis=0)

    W = dict(W)
    for kern, tag in (("gate", "wu1"), ("up", "wd1"), ("down", "mix")):
        carry = comm.gathers[tag] if comm else None
        if kern == "gate":
            (n1, g1), got = _ffn_gate(x, norms["ffn1_norm"], W["wg1"], "ffn1_gate", carry=carry)
        elif kern == "up":
            (u1, act1), got = _ffn_up(n1, g1, W["wu1"], "ffn1_up", carry=carry)
        else:
            (h1,), got = _ffn_down(x, act1, W["wd1"], "ffn1_down", carry=carry)
        if comm:
            W.update(comm.gathered(tag, got))
    u, n2 = _mix_in(h1, norms["mix_norm"], W["win"])
    (attn, lse), got = _attn_fwd(u, qg2, kg2, B, S, carry=comm.gathers["ffn2"] if comm else None)
    if comm:
        W = dict(W, **comm.gathered("ffn2", got))
    conv, y = _conv_fwd(u, cw, norms["conv_b"], norms["conv_ln_g"], norms["conv_ln_b"], B, S)
    h2 = _mix_out(h1, attn, conv, W["wout"])
    (h3, n3, g2, u2, dout, dyb, sq), _ = _ffn_forward(h2, norms["ffn2_norm"], W["wg2"], W["wu2"], W["wd2"], target,
                                                     "ffn2_fwd")
    del h3
    loss = (0.5 / D) * jnp.sum(sq)

    (a2, dg2, du2, dh2, dgn_ffn2), _ = _ffn_bwd_act(dyb, g2, u2, h2, dout, norms["ffn2_norm"],
                                                   W["wg2"], W["wu2"], W["wd2"], "ffn2_bwd_act")
    dwd2, _ = _ffn_bwd_w(a2, dyb, "ffn2_bwd_wd")
    dwg2, _ = _ffn_bwd_w(dg2, n3, "ffn2_bwd_wg")
    dwu2, _ = _ffn_bwd_w(du2, n3, "ffn2_bwd_wu")
    dattn, dconv, dwout = _mix_out_bwd(dh2, attn, conv, W["wout"])
    carry = comm.reduce_start("ffn2", {"wg2": dwg2, "wu2": dwu2, "wd2": dwd2, "wout": dwout}) if comm else None
    (dq, dk, dv, dgn_qk), got = _attn_bwd(u, attn, dattn, lse, qg2, kg2, B, S, carry=carry)
    if comm:
        comm.reduce_done(carry, got)
    dca, dcg, dcw, dcs = _conv_bwd(u, y, dconv, cw, norms["conv_ln_g"], norms["conv_ln_b"], B, S)
    dwin, dh1, dyb1, dgn_mix = _mix_in_bwd((dq, dk, dv, dca, dcg), W["win"], n2, h1, dh2, norms["mix_norm"])
    (a1, dg1, du1, gx, dgn_ffn1), _ = _ffn_bwd_act(dyb1, g1, u1, x, dh1, norms["ffn1_norm"],
                                                  W["wg1"], W["wu1"], W["wd1"], "ffn1_bwd_act")
    carry = comm.reduce_start("win", {"win": dwin}) if comm else None
    dwd1, got = _ffn_bwd_w(a1, dyb1, "ffn1_bwd_wd", carry=carry)
    if comm:
        comm.reduce_done(carry, got)
        carry = comm.reduce_start("wd1", {"wd1": dwd1})
    dwg1, got = _ffn_bwd_w(dg1, n1, "ffn1_bwd_wg", carry=carry)
    if comm:
        comm.reduce_done(carry, got)
        carry = comm.reduce_start("wg1", {"wg1": dwg1})
    dwu1, got = _ffn_bwd_w(du1, n1, "ffn1_bwd_wu", carry=carry)
    if comm:
        comm.reduce_done(carry, got)
        comm.reduce_now("wu1", {"wu1": dwu1})

    qk = jnp.sum(dgn_qk, axis=0)
    cs = jnp.sum(dcs, axis=0)
    small = {
        "ffn1_norm": jnp.sum(dgn_ffn1, axis=0),
        "mix_norm": jnp.sum(dgn_mix, axis=0),
        "q_norm": qk[0:1, 0:HD] + qk[0:1, HD:2 * HD],
        "k_norm": qk[1:2, 0:HD] + qk[1:2, HD:2 * HD],
        "conv_w": jnp.sum(dcw, axis=0)[0:CK],
        "conv_b": cs[0:1],
        "conv_ln_g": cs[1:2],
        "conv_ln_b": cs[2:3],
        "ffn2_norm": jnp.sum(dgn_ffn2, axis=0),
    }
    big = {"wg1": dwg1, "wu1": dwu1, "wd1": dwd1, "win": dwin, "wout": dwout,
           "wg2": dwg2, "wu2": dwu2, "wd2": dwd2}
    return loss, gx, big, small


HBM = pl.BlockSpec(memory_space=pltpu.HBM)
VMEM = pl.BlockSpec(memory_space=pltpu.VMEM)


def _place():
    return lax.axis_index("x"), lax.axis_index("y"), lax.axis_index("c")


class _GatherCarry:
    def __init__(self, shards):
        nt = len(shards)
        self.shards = shards
        self.in_arrays = [s for s, _ in shards]
        self.in_specs = [VMEM] * nt
        self.out_shape = [jax.ShapeDtypeStruct((NDEV * s.shape[0], s.shape[1]), dt) for s, dt in shards]
        self.out_specs = [HBM] * nt
        self.scratch = ([pltpu.VMEM(s.shape, dt) for s, dt in shards]
                        + [pltpu.SemaphoreType.DMA((nt, 7)), pltpu.SemaphoreType.DMA((nt, 7)),
                           pltpu.SemaphoreType.DMA((nt,))])

    def _copies(self, outs, scr):
        nt = len(self.shards)
        stages = scr[:nt]
        send_sems, recv_sems, local_sems = scr[nt:]
        x, y, c = _place()
        me, sibling = (x, y, c), (x, y, 1 - c)
        chips = [(1 - x, y), (x, 1 - y), (1 - x, 1 - y)]

        def rows(t, px, py, pc):
            r = self.shards[t][0].shape[0]
            return outs[t].at[pl.ds((4 * px + 2 * py + pc) * r, r), :]

        def copy(t, k, block, to, src=None):
            return pltpu.make_async_remote_copy(
                src_ref=rows(t, *block) if src is None else src, dst_ref=rows(t, *block),
                send_sem=send_sems.at[t, k], recv_sem=recv_sems.at[t, k],
                device_id=to, device_id_type=MESH)

        mine = [pltpu.make_async_copy(stages[t], rows(t, *me), local_sems.at[t]) for t in range(nt)]
        first = []
        for t in range(nt):
            first.append(copy(t, 0, me, sibling, src=stages[t]))
            first += [copy(t, 1 + j, me, (*chip, c), src=stages[t]) for j, chip in enumerate(chips)]
        return copy, mine, first, me, sibling, chips, c

    def start(self, ins, outs, scr):
        _, mine, first, *_ = self._copies(outs, scr)
        for t, (_, dt) in enumerate(self.shards):
            scr[t][...] = ins[t][...].astype(dt)
        for cp in mine + first:
            cp.start()

    def finish(self, ins, outs, scr):
        copy, mine, first, me, sibling, chips, c = self._copies(outs, scr)
        nt = len(self.shards)
        passed = []
        for j, chip in enumerate(chips):
            for t in range(nt):
                copy(t, 1 + j, (*chip, c), me).wait_recv()
                cp = copy(t, 4 + j, (*chip, c), sibling)
                cp.start()
                passed.append(cp)
        for t in range(nt):
            copy(t, 0, sibling, me).wait_recv()
            for j, chip in enumerate(chips):
                copy(t, 4 + j, (*chip, 1 - c), me).wait_recv()
        for cp in first + passed:
            cp.wait_send()
        for cp in mine:
            cp.wait()


def _run_carry(carry, name):
    def body(*refs):
        n_in, n_out = len(carry.in_arrays), len(carry.out_shape)
        ins, outs, scr = refs[:n_in], refs[n_in:n_in + n_out], refs[n_in + n_out:]
        carry.start(ins, outs, scr)
        carry.finish(ins, outs, scr)

    return pl.pallas_call(
        body, in_specs=carry.in_specs, out_specs=carry.out_specs, out_shape=carry.out_shape,
        scratch_shapes=carry.scratch, compiler_params=pltpu.CompilerParams(vmem_limit_bytes=VMEM_LIMIT),
        name=name)(*carry.in_arrays)


def _sibling_exchange(grads, name):
    nt = len(grads)
    g4 = [g.reshape(4, 2, g.shape[0] // NDEV, g.shape[1]) for g in grads]

    def body(*refs):
        ins, outs = refs[:nt], refs[nt:2 * nt]
        send_sems, recv_sems = refs[2 * nt:]
        x, y, c = _place()
        cps = []
        for t in range(nt):
            cp = pltpu.make_async_remote_copy(
                src_ref=ins[t].at[:, 1 - c], dst_ref=outs[t],
                send_sem=send_sems.at[t], recv_sem=recv_sems.at[t],
                device_id=(x, y, 1 - c), device_id_type=MESH)
            cp.start()
            cps.append(cp)
        for cp in cps:
            cp.wait()

    return pl.pallas_call(
        body, in_specs=[HBM] * nt, out_specs=[HBM] * nt,
        out_shape=[jax.ShapeDtypeStruct((4,) + g.shape[2:], BF16) for g in g4],
        scratch_shapes=[pltpu.SemaphoreType.DMA((nt,)), pltpu.SemaphoreType.DMA((nt,))],
        name=name)(*g4), g4


def _chip_partial(g4, recv, cidx, name):
    _, _, rows, n = g4.shape

    def body(c_ref, a_ref, b_ref, o_ref):
        o_ref[...] = (a_ref[...].astype(F32) + b_ref[...].astype(F32)).astype(BF16)

    return pl.pallas_call(
        body,
        grid_spec=pltpu.PrefetchScalarGridSpec(
            num_scalar_prefetch=1, grid=(4,),
            in_specs=[pl.BlockSpec((None, None, rows, n), lambda q, c_ref: (q, c_ref[0], 0, 0)),
                      pl.BlockSpec((None, rows, n), lambda q, c_ref: (q, 0, 0))],
            out_specs=pl.BlockSpec((None, rows, n), lambda q, c_ref: (q, 0, 0))),
        out_shape=jax.ShapeDtypeStruct((4, rows, n), BF16),
        compiler_params=_cp(("parallel",)), name=name)(cidx, g4, recv)


class _ExchangeCarry:
    def __init__(self, names, parts):
        nt = len(parts)
        self.names = names
        self.in_arrays = list(parts)
        self.in_specs = [HBM] * nt
        self.out_shape = [jax.ShapeDtypeStruct(p.shape, BF16) for p in parts]
        self.out_specs = [HBM] * nt
        self.scratch = [pltpu.SemaphoreType.DMA((nt, 3)), pltpu.SemaphoreType.DMA((nt, 3)),
                        pltpu.SemaphoreType.DMA((nt,))]

    def _copies(self, ins, outs, scr):
        send_sems, recv_sems, local_sems = scr
        x, y, c = _place()
        qme = 2 * x + y
        chips = [(1 - x, y), (x, 1 - y), (1 - x, 1 - y)]
        cps = []
        for t in range(len(ins)):
            cps.append(pltpu.make_async_copy(ins[t].at[qme], outs[t].at[qme], local_sems.at[t]))
            for k, (cx, cy) in enumerate(chips):
                cps.append(pltpu.make_async_remote_copy(
                    src_ref=ins[t].at[2 * cx + cy], dst_ref=outs[t].at[qme],
                    send_sem=send_sems.at[t, k], recv_sem=recv_sems.at[t, k],
                    device_id=(cx, cy, c), device_id_type=MESH))
        return cps

    def start(self, ins, outs, scr):
        for cp in self._copies(ins, outs, scr):
            cp.start()

    def finish(self, ins, outs, scr):
        for cp in self._copies(ins, outs, scr):
            cp.wait()


class _Comm:
    def __init__(self, groups, cidx):
        self.names = {tag: list(g) for tag, g in groups.items()}
        self.gathers = {tag: _GatherCarry(list(g.values())) for tag, g in groups.items()}
        self.cidx = cidx
        self.reduced = {}

    def gathered(self, tag, outs):
        return dict(zip(self.names[tag], outs))

    def reduce_start(self, tag, grads):
        names = list(grads)
        recv, g4 = _sibling_exchange([grads[n] for n in names], "sibling_exchange_" + tag)
        parts = [_chip_partial(g4[i], recv[i], self.cidx, "chip_partial_" + n) for i, n in enumerate(names)]
        return _ExchangeCarry(names, parts)

    def reduce_done(self, carry, outs):
        self.reduced.update(zip(carry.names, outs))

    def reduce_now(self, tag, grads):
        carry = self.reduce_start(tag, grads)
        self.reduce_done(carry, _run_carry(carry, "chip_exchange_" + tag))


def _adamw_math(w, g, m, v):
    m = B1 * m + (1.0 - B1) * g
    v = B2 * v + (1.0 - B2) * (g * g)
    m_hat = m / (1.0 - B1 ** STEP)
    v_hat = v / (1.0 - B2 ** STEP)
    delta = -LR * (m_hat / (jnp.sqrt(v_hat) + AEPS) + WD * w)
    return delta, m, v


def _adamw_big(recv, w, m, v, name):
    def body(r_ref, w_ref, m_ref, v_ref, g_ref, d_ref, mo_ref, vo_ref):
        g = r_ref[0].astype(F32)
        for q in range(1, 4):
            g = g + r_ref[q].astype(F32)
        d, mn, vn = _adamw_math(w_ref[...], g, m_ref[...], v_ref[...])
        g_ref[...] = g
        d_ref[...] = d
        mo_ref[...] = mn
        vo_ref[...] = vn

    return pl.pallas_call(
        body, out_shape=[jax.ShapeDtypeStruct(w.shape, F32)] * 4,
        compiler_params=pltpu.CompilerParams(vmem_limit_bytes=VMEM_LIMIT), name=name)(recv, w, m, v)


SMALL_ROWS = 40
CW_ROWS = 16
SMALL_TOTAL = SMALL_ROWS + NDEV * CW_ROWS


def _small_step(gvec, wvec, mvec, vvec):
    nr = SMALL_ROWS + CW_ROWS

    def body(g_ref, w_ref, m_ref, v_ref, go_ref, d_ref, mo_ref, vo_ref, slots, send_sems, recv_sems):
        x, y, c = _place()
        me = 4 * x + 2 * y + c
        slots[me] = g_ref[...]
        cps = []
        for k in range(1, NDEV):
            fx, fy, fc = (k >> 2) & 1, (k >> 1) & 1, k & 1
            peer = (x ^ fx, y ^ fy, c ^ fc)
            cp = pltpu.make_async_remote_copy(
                src_ref=g_ref, dst_ref=slots.at[me],
                send_sem=send_sems.at[k - 1], recv_sem=recv_sems.at[k - 1],
                device_id=peer, device_id_type=MESH)
            cp.start()
            cps.append(cp)
        for cp in cps:
            cp.wait()
        tot = slots[0]
        for j in range(1, NDEV):
            tot = tot + slots[j]
        slots[0] = tot
        g = jnp.concatenate(
            [tot[0:SMALL_ROWS], slots[0, pl.ds(pl.multiple_of(SMALL_ROWS + me * CW_ROWS, 8), CW_ROWS), :]], axis=0)
        d, mn, vn = _adamw_math(w_ref[...], g, m_ref[...], v_ref[...])
        go_ref[...] = g
        d_ref[...] = d
        mo_ref[...] = mn
        vo_ref[...] = vn

    return pl.pallas_call(
        body, in_specs=[VMEM] * 4, out_specs=[VMEM] * 4,
        out_shape=[jax.ShapeDtypeStruct((nr, 128), F32)] * 4,
        scratch_shapes=[pltpu.VMEM((NDEV, SMALL_TOTAL, 128), F32),
                        pltpu.SemaphoreType.DMA((NDEV - 1,)), pltpu.SemaphoreType.DMA((NDEV - 1,))],
        name="small_step")(gvec, wvec, mvec, vvec)


SMALL_NAMES = ("ffn1_norm", "mix_norm", "ffn2_norm", "conv_b", "conv_ln_g", "conv_ln_b", "q_norm", "k_norm")


LOSS_ROW = 38


def _pack_small(vals, loss=None):
    parts = []
    for n in SMALL_NAMES:
        a = vals[n].reshape(-1)
        if a.shape[0] < 128:
            a = jnp.concatenate([a, jnp.zeros((128 - a.shape[0],), F32)])
        parts.append(a.reshape(-1, 128))
    used = sum(p.shape[0] for p in parts)
    assert used == LOSS_ROW
    tail = jnp.zeros((SMALL_ROWS - used, 128), F32)
    if loss is not None:
        tail = tail.at[0, 0].set(loss)
    parts.append(tail)
    return jnp.concatenate(parts, axis=0)


def _unpack_small(vec, shapes):
    out, r = {}, 0
    for n in SMALL_NAMES:
        size = shapes[n][1]
        nr = max(size // 128, 1)
        out[n] = vec[r:r + nr].reshape(-1)[:size].reshape(1, size)
        r += nr
    return out


def _pack_cw(a):
    flat = a.reshape(-1)
    return jnp.concatenate([flat, jnp.zeros((CW_ROWS * 128 - flat.shape[0],), F32)]).reshape(CW_ROWS, 128)


def _unpack_cw(v):
    return v.reshape(-1)[:CK * HD].reshape(1, CK, HD)


def kernel(x, ffn1_norm, ffn1_w_gate, ffn1_w_up, ffn1_w_down, mix_norm, w_in, q_norm, k_norm, conv_w, conv_b, conv_ln_g, conv_ln_b, w_out, ffn2_norm, ffn2_w_gate, ffn2_w_up, ffn2_w_down, loss_target, m_ffn1_norm, m_ffn1_w_gate, m_ffn1_w_up, m_ffn1_w_down, m_mix_norm, m_w_in, m_q_norm, m_k_norm, m_conv_w, m_conv_b, m_conv_ln_g, m_conv_ln_b, m_w_out, m_ffn2_norm, m_ffn2_w_gate, m_ffn2_w_up, m_ffn2_w_down, v_ffn1_norm, v_ffn1_w_gate, v_ffn1_w_up, v_ffn1_w_down, v_mix_norm, v_w_in, v_q_norm, v_k_norm, v_conv_w, v_conv_b, v_conv_ln_g, v_conv_ln_b, v_w_out, v_ffn2_norm, v_ffn2_w_gate, v_ffn2_w_up, v_ffn2_w_down):
    P = dict(ffn1_norm=ffn1_norm, ffn1_w_gate=ffn1_w_gate, ffn1_w_up=ffn1_w_up, ffn1_w_down=ffn1_w_down,
             mix_norm=mix_norm, w_in=w_in, q_norm=q_norm, k_norm=k_norm, conv_w=conv_w, conv_b=conv_b,
             conv_ln_g=conv_ln_g, conv_ln_b=conv_ln_b, w_out=w_out, ffn2_norm=ffn2_norm,
             ffn2_w_gate=ffn2_w_gate, ffn2_w_up=ffn2_w_up, ffn2_w_down=ffn2_w_down)
    M = dict(ffn1_norm=m_ffn1_norm, ffn1_w_gate=m_ffn1_w_gate, ffn1_w_up=m_ffn1_w_up, ffn1_w_down=m_ffn1_w_down,
             mix_norm=m_mix_norm, w_in=m_w_in, q_norm=m_q_norm, k_norm=m_k_norm, conv_w=m_conv_w, conv_b=m_conv_b,
             conv_ln_g=m_conv_ln_g, conv_ln_b=m_conv_ln_b, w_out=m_w_out, ffn2_norm=m_ffn2_norm,
             ffn2_w_gate=m_ffn2_w_gate, ffn2_w_up=m_ffn2_w_up, ffn2_w_down=m_ffn2_w_down)
    V = dict(ffn1_norm=v_ffn1_norm, ffn1_w_gate=v_ffn1_w_gate, ffn1_w_up=v_ffn1_w_up, ffn1_w_down=v_ffn1_w_down,
             mix_norm=v_mix_norm, w_in=v_w_in, q_norm=v_q_norm, k_norm=v_k_norm, conv_w=v_conv_w, conv_b=v_conv_b,
             conv_ln_g=v_conv_ln_g, conv_ln_b=v_conv_ln_b, w_out=v_w_out, ffn2_norm=v_ffn2_norm,
             ffn2_w_gate=v_ffn2_w_gate, ffn2_w_up=v_ffn2_w_up, ffn2_w_down=v_ffn2_w_down)
    order = ["ffn1_norm", "ffn1_w_gate", "ffn1_w_up", "ffn1_w_down", "mix_norm", "w_in", "q_norm", "k_norm",
             "conv_w", "conv_b", "conv_ln_g", "conv_ln_b", "w_out", "ffn2_norm", "ffn2_w_gate", "ffn2_w_up",
             "ffn2_w_down"]
    B, S, _ = x.shape
    T = B * S
    cidx = lax.axis_index("c").astype(jnp.int32).reshape(1)

    bigs = [("wg1", "ffn1_w_gate", True), ("wu1", "ffn1_w_up", True), ("wd1", "ffn1_w_down", False),
            ("win", "w_in", True), ("wout", "w_out", False),
            ("wg2", "ffn2_w_gate", True), ("wu2", "ffn2_w_up", True), ("wd2", "ffn2_w_down", False)]
    hm = lambda a, tr: jnp.transpose(a[0]) if tr else a[0]
    cw_pad = jnp.zeros((32, 128), F32).at[0:CK, 0:HD].set(conv_w[0])
    shard = {ln: (hm(P[pn], tr), BF16) for ln, pn, tr in bigs}
    gathered = _run_carry(_GatherCarry([shard["wg1"], (cw_pad, F32)]), "gather_first")
    W = {"wg1": gathered[0]}
    cwg = gathered[1].reshape(NDEV, 32, 128)[:, 0:CK, 0:HD]
    W["conv_w"] = jnp.transpose(cwg, (1, 0, 2)).reshape(CK, DC)
    norms = {n: P[n] for n in SMALL_NAMES}
    comm = _Comm({"wu1": {"wu1": shard["wu1"]}, "wd1": {"wd1": shard["wd1"]},
                  "mix": {n: shard[n] for n in ("win", "wout")},
                  "ffn2": {n: shard[n] for n in ("wg2", "wu2", "wd2")}}, cidx)

    loss_part, gx, _, small = _local_step(x.reshape(T, D), loss_target.reshape(T, D), norms, W, B, S, comm)

    G, Dl, Mn, Vn = {}, {}, {}, {}
    for ln, pn, tr in bigs:
        outs = _adamw_big(comm.reduced[ln], hm(P[pn], tr), hm(M[pn], tr), hm(V[pn], tr), "adamw_" + ln)
        G[pn], Dl[pn], Mn[pn], Vn[pn] = [(jnp.transpose(o) if tr else o)[None] for o in outs]

    dcw = small["conv_w"].reshape(CK, NDEV, HD).transpose(1, 0, 2)
    gvec = jnp.concatenate([_pack_small(small, loss_part)] + [_pack_cw(dcw[j]) for j in range(NDEV)], axis=0)
    pack = lambda dct: jnp.concatenate([_pack_small({n: dct[n] for n in SMALL_NAMES}), _pack_cw(dct["conv_w"][0])], axis=0)
    go, do, mo, vo = _small_step(gvec, pack(P), pack(M), pack(V))
    loss = go[LOSS_ROW, 0]
    shapes = {n: P[n].shape for n in SMALL_NAMES}
    for dst, vec in ((G, go), (Dl, do), (Mn, mo), (Vn, vo)):
        dst.update(_unpack_small(vec[0:SMALL_ROWS], shapes))
        dst["conv_w"] = _unpack_cw(vec[SMALL_ROWS:])

    return (loss, gx.reshape(B, S, D), *[G[n] for n in order], *[Dl[n] for n in order],
            *[Mn[n] for n in order], *[Vn[n] for n in order])
```

```python
import functools

import jax
import jax.numpy as jnp
from jax import lax
from jax.experimental import pallas as pl
from jax.experimental.pallas import tpu as pltpu

F32 = jnp.float32
BF16 = jnp.bfloat16

D = 1024
FF = 2816
HD = 64
DA = 512
DC = 512
DIN = 2560
CK = 31
BLK = 128
DILS = (1, 4, 16)
EPS = 1e-6
NDEV = 8
MESH = pl.DeviceIdType.MESH

LR, B1, B2, AEPS, WD, STEP = 0.001, 0.9, 0.999, 1e-08, 0.01, 10

NT = (((1,), (1,)), ((), ()))
TN = (((0,), (0,)), ((), ()))

VMEM_LIMIT = 56 * 1024 * 1024


def _cp(sem=None):
    return pltpu.CompilerParams(dimension_semantics=sem, vmem_limit_bytes=VMEM_LIMIT)


def _sigmoid(x):
    return 0.5 * (jnp.tanh(0.5 * x) + 1.0)


def _pallas(body, args, *, grid, in_specs, out_specs, out_shape, scratch_shapes, sem, name, carry=None):
    if carry is None:
        outs = pl.pallas_call(body, grid=grid, in_specs=in_specs, out_specs=out_specs, out_shape=out_shape,
                              scratch_shapes=scratch_shapes, compiler_params=_cp(sem), name=name)(*args)
        return outs, None
    n_in, n_out, n_scr = len(in_specs), len(out_shape), len(scratch_shapes)
    c_in, c_out = len(carry.in_arrays), len(carry.out_shape)

    def wrapped(*refs):
        ins, refs = refs[:n_in], refs[n_in:]
        cins, refs = refs[:c_in], refs[c_in:]
        outs, refs = refs[:n_out], refs[n_out:]
        couts, refs = refs[:c_out], refs[c_out:]
        scr, cscr = refs[:n_scr], refs[n_scr:]
        ids = [pl.program_id(a) for a in range(len(grid))]
        is_first = functools.reduce(jnp.logical_and, [i == 0 for i in ids])
        is_last = functools.reduce(jnp.logical_and, [i == n - 1 for i, n in zip(ids, grid)])

        @pl.when(is_first)
        def _():
            carry.start(cins, couts, cscr)

        body(*ins, *outs, *scr)

        @pl.when(is_last)
        def _():
            carry.finish(cins, couts, cscr)

    outs = pl.pallas_call(
        wrapped, grid=grid, in_specs=list(in_specs) + carry.in_specs, out_specs=list(out_specs) + carry.out_specs,
        out_shape=list(out_shape) + carry.out_shape, scratch_shapes=list(scratch_shapes) + carry.scratch,
        compiler_params=_cp(("arbitrary",) * len(grid)), name=name)(*args, *carry.in_arrays)
    return outs[:n_out], outs[n_out:]


def _ffn_fwd(x, gain, wg, wu, wd, target, name, carry=None):
    T = x.shape[0]
    tm, tf = 1024, 256
    nt, nf = T // tm, FF // tf
    with_loss = target is not None

    def body(*refs):
        if with_loss:
            (x_ref, gain_ref, wg_ref, wu_ref, wd_ref, t_ref,
             h_ref, n_ref, g_ref, u_ref, dout_ref, dyb_ref, sq_ref, nb_scr, acc_scr) = refs
        else:
            (x_ref, gain_ref, wg_ref, wu_ref, wd_ref,
             h_ref, n_ref, g_ref, u_ref, nb_scr, acc_scr) = refs
        f = pl.program_id(1)

        @pl.when(f == 0)
        def _():
            xv = x_ref[...]
            r = lax.rsqrt(jnp.mean(xv * xv, axis=-1, keepdims=True) + EPS)
            nb = (xv * r * gain_ref[...]).astype(BF16)
            nb_scr[...] = nb
            n_ref[...] = nb
            acc_scr[...] = jnp.zeros_like(acc_scr)

        nb = nb_scr[...]
        g = lax.dot_general(nb, wg_ref[...], NT, preferred_element_type=F32)
        u = lax.dot_general(nb, wu_ref[...], NT, preferred_element_type=F32)
        a = g * _sigmoid(g) * u
        g_ref[...] = g.astype(BF16)
        u_ref[...] = u.astype(BF16)
        acc_scr[...] += jnp.dot(a.astype(BF16), wd_ref[...], preferred_element_type=F32)

        @pl.when(f == nf - 1)
        def _():
            h = x_ref[...] + 0.5 * acc_scr[...]
            h_ref[...] = h
            if with_loss:
                e = h - t_ref[...]
                dout = e * (1.0 / D)
                dout_ref[...] = dout
                dyb_ref[...] = (0.5 * dout).astype(BF16)
                sq_ref[...] = jnp.sum(e * e, axis=0, keepdims=True)[None]

    row = pl.BlockSpec((tm, D), lambda t, f: (t, 0))
    wspec = pl.BlockSpec((tf, D), lambda t, f: (f, 0))
    gspec = pl.BlockSpec((tm, tf), lambda t, f: (t, f))
    in_specs = [row, pl.BlockSpec((1, D), lambda t, f: (0, 0)), wspec, wspec, wspec]
    out_shape = [jax.ShapeDtypeStruct((T, D), F32), jax.ShapeDtypeStruct((T, D), BF16),
                 jax.ShapeDtypeStruct((T, FF), BF16), jax.ShapeDtypeStruct((T, FF), BF16)]
    out_specs = [row, row, gspec, gspec]
    args = [x, gain, wg, wu, wd]
    if with_loss:
        in_specs.append(row)
        args.append(target)
        out_shape += [jax.ShapeDtypeStruct((T, D), F32), jax.ShapeDtypeStruct((T, D), BF16),
                      jax.ShapeDtypeStruct((nt, 1, D), F32)]
        out_specs += [row, row, pl.BlockSpec((1, 1, D), lambda t, f: (t, 0, 0))]
    return _pallas(
        body, args, grid=(nt, nf), in_specs=in_specs, out_specs=out_specs, out_shape=out_shape,
        scratch_shapes=[pltpu.VMEM((tm, D), BF16), pltpu.VMEM((tm, D), F32)],
        sem=("parallel", "arbitrary"), name=name, carry=carry)


def _ffn_bwd(dyb, nb, g, u, wg, wu, wd, name, carry=None):
    T = dyb.shape[0]
    tm, tf = 512, 256
    nt, nf = T // tm, FF // tf

    def body(dy_ref, n_ref, g_ref, u_ref, wg_ref, wu_ref, wd_ref,
             dwg_ref, dwu_ref, dwd_ref, dn_ref, ag_scr, au_scr, ad_scr):
        f, t = pl.program_id(0), pl.program_id(1)

        @pl.when(t == 0)
        def _():
            ag_scr[...] = jnp.zeros_like(ag_scr)
            au_scr[...] = jnp.zeros_like(au_scr)
            ad_scr[...] = jnp.zeros_like(ad_scr)

        dy = dy_ref[...]
        n = n_ref[...]
        gv = g_ref[...].astype(F32)
        uv = u_ref[...].astype(F32)
        da = lax.dot_general(dy, wd_ref[...], NT, preferred_element_type=F32)
        sg = _sigmoid(gv)
        silu = gv * sg
        ab = (silu * uv).astype(BF16)
        dgb = (da * uv * (sg * (1.0 + gv * (1.0 - sg)))).astype(BF16)
        dub = (da * silu).astype(BF16)
        ad_scr[...] += lax.dot_general(ab, dy, TN, preferred_element_type=F32)
        ag_scr[...] += lax.dot_general(dgb, n, TN, preferred_element_type=F32)
        au_scr[...] += lax.dot_general(dub, n, TN, preferred_element_type=F32)
        dn = (jnp.dot(dgb, wg_ref[...], preferred_element_type=F32)
              + jnp.dot(dub, wu_ref[...], preferred_element_type=F32))
        rows = pl.ds(pl.multiple_of(t * tm, tm), tm)

        @pl.when(f == 0)
        def _():
            dn_ref[rows, :] = dn

        @pl.when(f > 0)
        def _():
            dn_ref[rows, :] += dn

        @pl.when(t == nt - 1)
        def _():
            dwg_ref[...] = ag_scr[...].astype(BF16)
            dwu_ref[...] = au_scr[...].astype(BF16)
            dwd_ref[...] = ad_scr[...].astype(BF16)

    row = pl.BlockSpec((tm, D), lambda f, t: (t, 0))
    gspec = pl.BlockSpec((tm, tf), lambda f, t: (t, f))
    wspec = pl.BlockSpec((tf, D), lambda f, t: (f, 0))
    return _pallas(
        body, (dyb, nb, g, u, wg, wu, wd), grid=(nf, nt),
        in_specs=[row, row, gspec, gspec, wspec, wspec, wspec],
        out_specs=[wspec, wspec, wspec, pl.BlockSpec((T, D), lambda f, t: (0, 0))],
        out_shape=[jax.ShapeDtypeStruct((FF, D), BF16)] * 3 + [jax.ShapeDtypeStruct((T, D), F32)],
        scratch_shapes=[pltpu.VMEM((tf, D), F32)] * 3,
        sem=("arbitrary", "arbitrary"), name=name, carry=carry)


FC = 256


def _resident(shape):
    return pl.BlockSpec(shape, lambda *_: (0,) * len(shape), pipeline_mode=pl.Buffered(1))


def _ffn_forward(x, gain, wg, wu, wd, target, name, carry=None):
    T = x.shape[0]
    tm = 512
    nt = T // tm
    with_loss = target is not None

    def body(*refs):
        if with_loss:
            (x_ref, gain_ref, wg_ref, wu_ref, wd_ref, t_ref,
             h_ref, n_ref, g_ref, u_ref, dout_ref, dyb_ref, sq_ref, a_scr) = refs
        else:
            x_ref, gain_ref, wg_ref, wu_ref, wd_ref, h_ref, n_ref, g_ref, u_ref, a_scr = refs
        xv = x_ref[...]
        r = lax.rsqrt(jnp.mean(xv * xv, axis=-1, keepdims=True) + EPS)
        n_ref[...] = (xv * r * gain_ref[...]).astype(BF16)
        for c in range(FF // FC):
            cols = slice(c * FC, (c + 1) * FC)
            nb = n_ref[...]
            g = lax.dot_general(nb, wg_ref[cols, :], NT, preferred_element_type=F32)
            u = lax.dot_general(nb, wu_ref[cols, :], NT, preferred_element_type=F32)
            g_ref[:, cols] = g.astype(BF16)
            u_ref[:, cols] = u.astype(BF16)
            a_scr[:, cols] = (g * _sigmoid(g) * u).astype(BF16)
        h = xv + 0.5 * jnp.dot(a_scr[...], wd_ref[...], preferred_element_type=F32)
        h_ref[...] = h
        if with_loss:
            e = h - t_ref[...]
            dout = e * (1.0 / D)
            dout_ref[...] = dout
            dyb_ref[...] = (0.5 * dout).astype(BF16)
            sq_ref[...] = jnp.sum(e * e, axis=0, keepdims=True)[None]

    row = pl.BlockSpec((tm, D), lambda t: (t, 0))
    wide = pl.BlockSpec((tm, FF), lambda t: (t, 0))
    in_specs = [row, _resident((1, D)), _resident((FF, D)), _resident((FF, D)), _resident((FF, D))]
    out_shape = [jax.ShapeDtypeStruct((T, D), F32), jax.ShapeDtypeStruct((T, D), BF16),
                 jax.ShapeDtypeStruct((T, FF), BF16), jax.ShapeDtypeStruct((T, FF), BF16)]
    out_specs = [row, row, wide, wide]
    args = [x, gain, wg, wu, wd]
    if with_loss:
        in_specs.append(row)
        args.append(target)
        out_shape += [jax.ShapeDtypeStruct((T, D), F32), jax.ShapeDtypeStruct((T, D), BF16),
                      jax.ShapeDtypeStruct((nt, 1, D), F32)]
        out_specs += [row, row, pl.BlockSpec((1, 1, D), lambda t: (t, 0, 0))]
    return _pallas(
        body, args, grid=(nt,), in_specs=in_specs, out_specs=out_specs, out_shape=out_shape,
        scratch_shapes=[pltpu.VMEM((tm, FF), BF16)], sem=("parallel",), name=name, carry=carry)


def _ffn_gate(x, gain, wg, name, carry=None):
    T = x.shape[0]
    tm = 512

    def body(x_ref, gain_ref, wg_ref, n_ref, g_ref):
        xv = x_ref[...]
        r = lax.rsqrt(jnp.mean(xv * xv, axis=-1, keepdims=True) + EPS)
        n_ref[...] = (xv * r * gain_ref[...]).astype(BF16)
        for c in range(FF // FC):
            cols = slice(c * FC, (c + 1) * FC)
            g_ref[:, cols] = lax.dot_general(n_ref[...], wg_ref[cols, :], NT,
                                             preferred_element_type=F32).astype(BF16)

    row = pl.BlockSpec((tm, D), lambda t: (t, 0))
    wide = pl.BlockSpec((tm, FF), lambda t: (t, 0))
    return _pallas(
        body, (x, gain, wg), grid=(T // tm,), in_specs=[row, _resident((1, D)), _resident((FF, D))],
        out_specs=[row, wide], out_shape=[jax.ShapeDtypeStruct((T, D), BF16), jax.ShapeDtypeStruct((T, FF), BF16)],
        scratch_shapes=[], sem=("parallel",), name=name, carry=carry)


def _ffn_up(nb, g, wu, name, carry=None):
    T = nb.shape[0]
    tm = 512

    def body(n_ref, g_ref, wu_ref, u_ref, a_ref):
        for c in range(FF // FC):
            cols = slice(c * FC, (c + 1) * FC)
            u = lax.dot_general(n_ref[...], wu_ref[cols, :], NT, preferred_element_type=F32)
            gv = g_ref[:, cols].astype(F32)
            u_ref[:, cols] = u.astype(BF16)
            a_ref[:, cols] = (gv * _sigmoid(gv) * u).astype(BF16)

    row = pl.BlockSpec((tm, D), lambda t: (t, 0))
    wide = pl.BlockSpec((tm, FF), lambda t: (t, 0))
    return _pallas(
        body, (nb, g, wu), grid=(T // tm,), in_specs=[row, wide, _resident((FF, D))],
        out_specs=[wide, wide], out_shape=[jax.ShapeDtypeStruct((T, FF), BF16)] * 2,
        scratch_shapes=[], sem=("parallel",), name=name, carry=carry)


def _ffn_down(x, a, wd, name, carry=None):
    T = x.shape[0]
    tm = 512

    def body(x_ref, a_ref, wd_ref, h_ref):
        h_ref[...] = x_ref[...] + 0.5 * jnp.dot(a_ref[...], wd_ref[...], preferred_element_type=F32)

    row = pl.BlockSpec((tm, D), lambda t: (t, 0))
    wide = pl.BlockSpec((tm, FF), lambda t: (t, 0))
    return _pallas(
        body, (x, a, wd), grid=(T // tm,), in_specs=[row, wide, _resident((FF, D))],
        out_specs=[row], out_shape=[jax.ShapeDtypeStruct((T, D), F32)],
        scratch_shapes=[], sem=("parallel",), name=name, carry=carry)


def _ffn_bwd_act(dyb, g, u, x, dout, gain, wg, wu, wd, name, carry=None):
    T = x.shape[0]
    tm = 256
    nt = T // tm

    def body(dy_ref, g_ref, u_ref, x_ref, dout_ref, gain_ref, wg_ref, wu_ref, wd_ref,
             a_ref, dg_ref, du_ref, dx_ref, dgn_ref):
        for c in range(FF // FC):
            cols = slice(c * FC, (c + 1) * FC)
            da = lax.dot_general(dy_ref[...], wd_ref[cols, :], NT, preferred_element_type=F32)
            gv = g_ref[:, cols].astype(F32)
            uv = u_ref[:, cols].astype(F32)
            sg = _sigmoid(gv)
            silu = gv * sg
            a_ref[:, cols] = (silu * uv).astype(BF16)
            dg_ref[:, cols] = (da * uv * (sg * (1.0 + gv * (1.0 - sg)))).astype(BF16)
            du_ref[:, cols] = (da * silu).astype(BF16)
        dn = (jnp.dot(dg_ref[...], wg_ref[...], preferred_element_type=F32)
              + jnp.dot(du_ref[...], wu_ref[...], preferred_element_type=F32))
        dx, dgain = _rms_bwd_rows(dn, x_ref[...], gain_ref[...])
        dx_ref[...] = dout_ref[...] + dx
        dgn_ref[...] = dgain[None]

    row = pl.BlockSpec((tm, D), lambda t: (t, 0))
    wide = pl.BlockSpec((tm, FF), lambda t: (t, 0))
    return _pallas(
        body, (dyb, g, u, x, dout, gain, wg, wu, wd), grid=(nt,),
        in_specs=[row, wide, wide, row, row, _resident((1, D)), _resident((FF, D)), _resident((FF, D)),
                  _resident((FF, D))],
        out_specs=[wide, wide, wide, row, pl.BlockSpec((1, 1, D), lambda t: (t, 0, 0))],
        out_shape=[jax.ShapeDtypeStruct((T, FF), BF16)] * 3
                  + [jax.ShapeDtypeStruct((T, D), F32), jax.ShapeDtypeStruct((nt, 1, D), F32)],
        scratch_shapes=[], sem=("parallel",), name=name, carry=carry)


def _ffn_bwd_w(lhs, rhs, name, carry=None):
    T = rhs.shape[0]
    tf = 256

    def body(l_ref, r_ref, dw_ref):
        dw_ref[...] = lax.dot_general(l_ref[...], r_ref[...], TN, preferred_element_type=F32).astype(BF16)

    (dw,), got = _pallas(
        body, (lhs, rhs), grid=(FF // tf,),
        in_specs=[pl.BlockSpec((T, tf), lambda f: (0, f)), _resident((T, D))],
        out_specs=[pl.BlockSpec((tf, D), lambda f: (f, 0))], out_shape=[jax.ShapeDtypeStruct((FF, D), BF16)],
        scratch_shapes=[], sem=("parallel",), name=name, carry=carry)
    return dw, got


def _rms_bwd_rows(dn, xv, gain):
    r = lax.rsqrt(jnp.mean(xv * xv, axis=-1, keepdims=True) + EPS)
    xhat = xv * r
    dxhat = dn * gain
    dx = r * (dxhat - xhat * jnp.mean(dxhat * xhat, axis=-1, keepdims=True))
    return dx, jnp.sum(dn * xhat, axis=0, keepdims=True)


def _norm_bwd(dn, x, dout, gain, name):
    T = x.shape[0]
    tm = 512
    nt = T // tm

    def body(dn_ref, x_ref, dout_ref, gain_ref, dx_ref, dg_ref):
        dx, dgain = _rms_bwd_rows(dn_ref[...], x_ref[...], gain_ref[...])
        dx_ref[...] = dout_ref[...] + dx
        dg_ref[...] = dgain[None]

    row = pl.BlockSpec((tm, D), lambda t: (t, 0))
    return pl.pallas_call(
        body, grid=(nt,), in_specs=[row, row, row, pl.BlockSpec((1, D), lambda t: (0, 0))],
        out_specs=[row, pl.BlockSpec((1, 1, D), lambda t: (t, 0, 0))],
        out_shape=[jax.ShapeDtypeStruct((T, D), F32), jax.ShapeDtypeStruct((nt, 1, D), F32)],
        compiler_params=_cp(("parallel",)), name=name)(dn, x, dout, gain)


def _mix_in(h, gain, win):
    T = h.shape[0]
    tm = 512

    def body(h_ref, gain_ref, w_ref, u_ref, n_ref):
        xv = h_ref[...]
        r = lax.rsqrt(jnp.mean(xv * xv, axis=-1, keepdims=True) + EPS)
        nb = (xv * r * gain_ref[...]).astype(BF16)
        n_ref[...] = nb
        u_ref[...] = lax.dot_general(nb, w_ref[...], NT, preferred_element_type=F32).astype(BF16)

    row = pl.BlockSpec((tm, D), lambda t: (t, 0))
    return pl.pallas_call(
        body, grid=(T // tm,),
        in_specs=[row, pl.BlockSpec((1, D), lambda t: (0, 0)), pl.BlockSpec((DIN, D), lambda t: (0, 0))],
        out_specs=[pl.BlockSpec((tm, DIN), lambda t: (t, 0)), row],
        out_shape=[jax.ShapeDtypeStruct((T, DIN), BF16), jax.ShapeDtypeStruct((T, D), BF16)],
        compiler_params=_cp(("parallel",)), name="mix_in")(h, gain, win)


def _mix_out(h, attn, conv, wout):
    T = h.shape[0]
    tm = 512

    def body(h_ref, a_ref, c_ref, w_ref, o_ref):
        o_ref[...] = (h_ref[...]
                      + jnp.dot(a_ref[...], w_ref[0:DA, :], preferred_element_type=F32)
                      + jnp.dot(c_ref[...], w_ref[DA:D, :], preferred_element_type=F32))

    row = pl.BlockSpec((tm, D), lambda t: (t, 0))
    half = pl.BlockSpec((tm, DA), lambda t: (t, 0))
    return pl.pallas_call(
        body, grid=(T // tm,),
        in_specs=[row, half, half, pl.BlockSpec((D, D), lambda t: (0, 0))],
        out_specs=row, out_shape=jax.ShapeDtypeStruct((T, D), F32),
        compiler_params=_cp(("parallel",)), name="mix_out")(h, attn, conv, wout)


def _mix_out_bwd(dh, attn, conv, wout):
    T = dh.shape[0]
    tm = 512
    nt = T // tm

    def body(dh_ref, a_ref, c_ref, w_ref, da_ref, dc_ref, dw_ref, acc_scr):
        t = pl.program_id(0)

        @pl.when(t == 0)
        def _():
            acc_scr[...] = jnp.zeros_like(acc_scr)

        dhb = dh_ref[...].astype(BF16)
        dmix = lax.dot_general(dhb, w_ref[...], NT, preferred_element_type=F32)
        da_ref[...] = dmix[:, 0:DA].astype(BF16)
        dc_ref[...] = dmix[:, DA:D].astype(BF16)
        acc_scr[0:DA, :] += lax.dot_general(a_ref[...], dhb, TN, preferred_element_type=F32)
        acc_scr[DA:D, :] += lax.dot_general(c_ref[...], dhb, TN, preferred_element_type=F32)

        @pl.when(t == nt - 1)
        def _():
            dw_ref[...] = acc_scr[...].astype(BF16)

    row = pl.BlockSpec((tm, D), lambda t: (t, 0))
    half = pl.BlockSpec((tm, DA), lambda t: (t, 0))
    full = pl.BlockSpec((D, D), lambda t: (0, 0))
    return pl.pallas_call(
        body, grid=(nt,), in_specs=[row, half, half, full], out_specs=[half, half, full],
        out_shape=[jax.ShapeDtypeStruct((T, DA), BF16)] * 2 + [jax.ShapeDtypeStruct((D, D), BF16)],
        scratch_shapes=[pltpu.VMEM((D, D), F32)],
        compiler_params=_cp(("arbitrary",)), name="mix_out_bwd")(dh, attn, conv, wout)


def _mix_in_bwd(dparts, win, nb, h, dh, gain):
    T = h.shape[0]
    tm = 512
    nt = T // tm

    def body(d0, d1, d2, d3, d4, w_ref, n_ref, h_ref, dh_ref, gain_ref,
             dw_ref, dx_ref, dyb_ref, dg_ref, acc_scr):
        t = pl.program_id(0)

        @pl.when(t == 0)
        def _():
            acc_scr[...] = jnp.zeros_like(acc_scr)

        n = n_ref[...]
        dn = jnp.zeros((tm, D), F32)
        for i, d_ref in enumerate((d0, d1, d2, d3, d4)):
            dv = d_ref[...]
            dn = dn + jnp.dot(dv, w_ref[i * DA:(i + 1) * DA, :], preferred_element_type=F32)
            acc_scr[i * DA:(i + 1) * DA, :] += lax.dot_general(dv, n, TN, preferred_element_type=F32)
        dx, dgain = _rms_bwd_rows(dn, h_ref[...], gain_ref[...])
        tot = dh_ref[...] + dx
        dx_ref[...] = tot
        dyb_ref[...] = (0.5 * tot).astype(BF16)
        dg_ref[...] = dgain[None]

        @pl.when(t == nt - 1)
        def _():
            dw_ref[...] = acc_scr[...].astype(BF16)

    row = pl.BlockSpec((tm, D), lambda t: (t, 0))
    half = pl.BlockSpec((tm, DA), lambda t: (t, 0))
    full = pl.BlockSpec((DIN, D), lambda t: (0, 0))
    return pl.pallas_call(
        body, grid=(nt,),
        in_specs=[half] * 5 + [full, row, row, row, pl.BlockSpec((1, D), lambda t: (0, 0))],
        out_specs=[full, row, row, pl.BlockSpec((1, 1, D), lambda t: (t, 0, 0))],
        out_shape=[jax.ShapeDtypeStruct((DIN, D), BF16), jax.ShapeDtypeStruct((T, D), F32),
                   jax.ShapeDtypeStruct((T, D), BF16), jax.ShapeDtypeStruct((nt, 1, D), F32)],
        scratch_shapes=[pltpu.VMEM((DIN, D), F32)],
        compiler_params=_cp(("arbitrary",)), name="mix_in_bwd")(*dparts, win, nb, h, dh, gain)


def _head_masks():
    lane = lax.broadcasted_iota(jnp.int32, (1, 2 * HD), 1)
    m0 = lane < HD
    return m0, jnp.logical_not(m0)


def _stack_heads(v, m0, m1):
    z = jnp.zeros_like(v)
    return jnp.concatenate([jnp.where(m0, v, z), jnp.where(m1, v, z)], axis=0)


def _unstack_heads(v2, m0):
    return jnp.where(m0, v2[0:BLK], v2[BLK:2 * BLK])


def _head_sums(xv):
    ri = lax.broadcasted_iota(jnp.int32, (2 * HD, 2 * HD), 0)
    ci = lax.broadcasted_iota(jnp.int32, (2 * HD, 2 * HD), 1)
    ones = jnp.where((ri < HD) == (ci < HD), 1.0, 0.0).astype(BF16)
    hi = xv.astype(BF16)
    lo = (xv - hi.astype(F32)).astype(BF16)
    return (jnp.dot(hi, ones, preferred_element_type=F32) + jnp.dot(lo, ones, preferred_element_type=F32))


def _head_rms(xv):
    return lax.rsqrt(_head_sums(xv * xv) * (1.0 / HD) + EPS)


def _band_mask(first):
    qi = lax.broadcasted_iota(jnp.int32, (BLK, 2 * BLK), 0)
    ci = lax.broadcasted_iota(jnp.int32, (BLK, 2 * BLK), 1)
    band = (ci >= qi) & (ci <= qi + BLK)
    return band & ((ci >= BLK) | jnp.logical_not(first))


def _block_rows(j, d, seg):
    r, n = j // seg, j % seg
    start = r + (d * BLK) * n
    first = n == 0
    prev = jnp.where(first, start, start - d * BLK)
    return pl.ds(start, BLK, stride=d), pl.ds(prev, BLK, stride=d), first


def _block_keys(refs, cur, prev, first, single):
    if single:
        qi = lax.broadcasted_iota(jnp.int32, (BLK, BLK), 0)
        ci = lax.broadcasted_iota(jnp.int32, (BLK, BLK), 1)
        return [r[cur, :].astype(BF16) for r in refs], ci <= qi
    return ([jnp.concatenate([r[prev, :], r[cur, :]], axis=0).astype(BF16) for r in refs], _band_mask(first))


def _attn_fwd(u, qg2, kg2, B, S, carry=None):
    T = B * S
    NB = S // BLK
    scale = HD ** -0.5

    def body(q_ref, k_ref, v_ref, qg_ref, kg_ref, o_ref, lse_ref, qn, kn, vn, os_, ls_):
        m0, m1 = _head_masks()
        qv = q_ref[...].astype(F32)
        qn[...] = qv * _head_rms(qv) * (qg_ref[...] * scale)
        kv = k_ref[...].astype(F32)
        kn[...] = kv * _head_rms(kv) * kg_ref[...]
        vn[...] = v_ref[...].astype(F32)

        for i, d in enumerate(DILS):
            seg = NB // d

            def blk(j, c, i=i, d=d, seg=seg):
                cur, prev, first = _block_rows(j, d, seg)
                q2 = _stack_heads(qn[cur, :].astype(BF16), m0, m1)
                (kk, vv), mask = _block_keys((kn, vn), cur, prev, first, False)
                s = lax.dot_general(q2, kk, NT, preferred_element_type=F32)
                s = jnp.where(jnp.concatenate([mask, mask], axis=0), s, -1e30)
                mx = jnp.max(s, axis=-1, keepdims=True)
                p = jnp.exp(s - mx)
                l = jnp.sum(p, axis=-1, keepdims=True)
                o2 = jnp.dot((p * (1.0 / l)).astype(BF16), vv, preferred_element_type=F32)
                os_[i, cur, :] = _unstack_heads(o2, m0)
                ls_[i, cur, :] = _unstack_heads(mx + jnp.log(l), m0)
                return c

            lax.fori_loop(0, NB, blk, 0, unroll=8)

        def comb(c, carry):
            rows = pl.ds(pl.multiple_of(c * 256, 256), 256)
            l0, l1, l2 = ls_[0, rows, :], ls_[1, rows, :], ls_[2, rows, :]
            mx = jnp.maximum(jnp.maximum(l0, l1), l2)
            e0, e1, e2 = jnp.exp(l0 - mx), jnp.exp(l1 - mx), jnp.exp(l2 - mx)
            tot = e0 + e1 + e2
            inv = 1.0 / tot
            o = (e0 * os_[0, rows, :] + e1 * os_[1, rows, :] + e2 * os_[2, rows, :]) * inv
            o_ref[rows, :] = o.astype(BF16)
            lse_ref[rows, :] = mx + jnp.log(tot)
            return carry

        lax.fori_loop(0, S // 256, comb, 0)

    pair = 2 * HD
    blk_spec = lambda off: pl.BlockSpec((S, pair), lambda b, p, off=off: (b, off + p))
    gspec = pl.BlockSpec((1, pair), lambda b, p: (0, 0))
    return _pallas(
        body, (u, u, u, qg2, kg2), grid=(B, DA // pair),
        in_specs=[blk_spec(0), blk_spec(DA // pair), blk_spec(2 * DA // pair), gspec, gspec],
        out_specs=[blk_spec(0), blk_spec(0)],
        out_shape=[jax.ShapeDtypeStruct((T, DA), BF16), jax.ShapeDtypeStruct((T, DA), F32)],
        scratch_shapes=[pltpu.VMEM((S, pair), F32)] * 3 + [pltpu.VMEM((3, S, pair), F32)] * 2,
        sem=("parallel", "parallel"), name="attn_fwd", carry=carry)


def _attn_bwd(u, attn, dattn, lse, qg2, kg2, B, S, carry=None):
    T = B * S
    NB = S // BLK
    scale = HD ** -0.5
    pair = 2 * HD

    def body(q_ref, k_ref, v_ref, o_ref, do_ref, lse_ref, qg_ref, kg_ref,
             dq_ref, dk_ref, dv_ref, dgn_ref,
             qn, kn, vn, don, ldl, accq, acck, accv):
        m0, m1 = _head_masks()
        lane = lax.broadcasted_iota(jnp.int32, (1, pair), 1)
        qv = q_ref[...].astype(F32)
        qn[...] = qv * _head_rms(qv) * (qg_ref[...] * scale)
        kv = k_ref[...].astype(F32)
        kn[...] = kv * _head_rms(kv) * kg_ref[...]
        vn[...] = v_ref[...].astype(F32)
        dov = do_ref[...].astype(F32)
        don[...] = dov
        ldl[...] = jnp.where((lane % HD) < HD // 2, lse_ref[...], _head_sums(dov * o_ref[...].astype(F32)))
        accq[...] = jnp.zeros_like(accq)
        acck[...] = jnp.zeros_like(acck)
        accv[...] = jnp.zeros_like(accv)

        for i, d in enumerate(DILS):
            seg = NB // d

            def blk(j, c, d=d, seg=seg):
                cur, prev, first = _block_rows(j, d, seg)
                q2 = _stack_heads(qn[cur, :].astype(BF16), m0, m1)
                do2 = _stack_heads(don[cur, :].astype(BF16), m0, m1)
                (kk, vv), mask = _block_keys((kn, vn), cur, prev, first, seg == 1)
                ldv = ldl[cur, :]
                lse2 = jnp.concatenate([ldv[:, 0:1], ldv[:, HD:HD + 1]], axis=0)
                dl2 = jnp.concatenate([ldv[:, HD // 2:HD // 2 + 1], ldv[:, HD + HD // 2:HD + HD // 2 + 1]], axis=0)
                s = lax.dot_general(q2, kk, NT, preferred_element_type=F32)
                p = jnp.where(jnp.concatenate([mask, mask], axis=0), jnp.exp(s - lse2), 0.0)
                dp = lax.dot_general(do2, vv, NT, preferred_element_type=F32)
                ds = (p * (dp - dl2)).astype(BF16)
                dq_acc = _unstack_heads(jnp.dot(ds, kk, preferred_element_type=F32), m0)
                dk_acc = lax.dot_general(ds, q2, TN, preferred_element_type=F32)
                dv_acc = lax.dot_general(p.astype(BF16), do2, TN, preferred_element_type=F32)
                accq[cur, :] += dq_acc
                if seg == 1:
                    acck[cur, :] += dk_acc
                    accv[cur, :] += dv_acc
                else:
                    acck[prev, :] += dk_acc[0:BLK]
                    acck[cur, :] += dk_acc[BLK:2 * BLK]
                    accv[prev, :] += dv_acc[0:BLK]
                    accv[cur, :] += dv_acc[BLK:2 * BLK]
                return c

            lax.fori_loop(0, NB, blk, 0, unroll=4)

        def norm_bwd(x_ref, dn, gain):
            xv = x_ref[...].astype(F32)
            r = _head_rms(xv)
            xhat = xv * r
            dxhat = dn * gain
            dx = r * (dxhat - xhat * (_head_sums(dxhat * xhat) * (1.0 / HD)))
            return dx, jnp.sum(dn * xhat, axis=0, keepdims=True)

        dq, dgq = norm_bwd(q_ref, accq[...], qg_ref[...] * scale)
        dk, dgk = norm_bwd(k_ref, acck[...], kg_ref[...])
        dq_ref[...] = dq.astype(BF16)
        dk_ref[...] = dk.astype(BF16)
        dv_ref[...] = accv[...].astype(BF16)
        dgn_ref[...] = jnp.concatenate([dgq * scale, dgk, jnp.zeros((6, pair), F32)], axis=0)[None]

    blk_spec = lambda off: pl.BlockSpec((S, pair), lambda b, p, off=off: (b, off + p))
    gspec = pl.BlockSpec((1, pair), lambda b, p: (0, 0))
    np_ = DA // pair
    return _pallas(
        body, (u, u, u, attn, dattn, lse, qg2, kg2), grid=(B, np_),
        in_specs=[blk_spec(0), blk_spec(np_), blk_spec(2 * np_), blk_spec(0), blk_spec(0), blk_spec(0),
                  gspec, gspec],
        out_specs=[blk_spec(0), blk_spec(0), blk_spec(0),
                   pl.BlockSpec((1, 8, pair), lambda b, p: (b * np_ + p, 0, 0))],
        out_shape=[jax.ShapeDtypeStruct((T, DA), BF16)] * 3 + [jax.ShapeDtypeStruct((B * np_, 8, pair), F32)],
        scratch_shapes=[pltpu.VMEM((S, pair), F32)] * 8,
        sem=("parallel", "parallel"), name="attn_bwd", carry=carry)


CT = 32
CPAD = 32


def _shifted(win, offsets):
    rolled, out = {}, {}
    n = win.shape[0]
    for o in offsets:
        sub = o % 8
        if sub not in rolled:
            rolled[sub] = win if sub == 0 else pltpu.roll(win, n - sub, 0)
        out[o] = rolled[sub][o - sub:o - sub + CT, :]
    return out


def _ln_fwd(y, g, b):
    mu = jnp.mean(y, axis=-1, keepdims=True)
    yc = y - mu
    rstd = lax.rsqrt(jnp.mean(yc * yc, axis=-1, keepdims=True) + EPS)
    xhat = yc * rstd
    return xhat, rstd, xhat * g + b


def _fill_glu(ca_ref, cg_ref, glu, S):
    glu[pl.ds(0, CPAD), :] = jnp.zeros((CPAD, DC), F32)

    def fill(i, c):
        rows = pl.ds(pl.multiple_of(i * 256, 256), 256)
        a = ca_ref[rows, :].astype(F32)
        gt = cg_ref[rows, :].astype(F32)
        glu[pl.ds(pl.multiple_of(CPAD + i * 256, CT), 256), :] = a * _sigmoid(gt)
        return c

    lax.fori_loop(0, S // 256, fill, 0)


def _conv_fwd(u, cw, cb, lg, lb, B, S):
    T = B * S

    def body(ca_ref, cg_ref, w_ref, b_ref, lg_ref, lb_ref, o_ref, y_ref, glu):
        _fill_glu(ca_ref, cg_ref, glu, S)

        def step(i, c):
            t0 = pl.multiple_of(i * CT, CT)
            win = glu[pl.ds(t0, 2 * CT), :]
            acc = jnp.zeros((CT, DC), F32) + b_ref[...]
            taps = _shifted(win, [k + 2 for k in range(CK)])
            for k in range(CK):
                acc = acc + taps[k + 2] * w_ref[k:k + 1, :]
            y_ref[pl.ds(t0, CT), :] = acc
            _, _, z = _ln_fwd(acc, lg_ref[...], lb_ref[...])
            o_ref[pl.ds(t0, CT), :] = (z * _sigmoid(z)).astype(BF16)
            return c

        lax.fori_loop(0, S // CT, step, 0, unroll=2)

    vec = pl.BlockSpec((1, DC), lambda b: (0, 0))
    return pl.pallas_call(
        body, grid=(B,),
        in_specs=[pl.BlockSpec((S, DC), lambda b: (b, 3)), pl.BlockSpec((S, DC), lambda b: (b, 4)),
                  pl.BlockSpec((CT, DC), lambda b: (0, 0)), vec, vec, vec],
        out_specs=[pl.BlockSpec((S, DC), lambda b: (b, 0))] * 2,
        out_shape=[jax.ShapeDtypeStruct((T, DC), BF16), jax.ShapeDtypeStruct((T, DC), F32)],
        scratch_shapes=[pltpu.VMEM((CPAD + S, DC), F32)],
        compiler_params=_cp(("parallel",)), name="conv_fwd")(u, u, cw, cb, lg, lb)


def _conv_bwd(u, y, dconv, cw, lg, lb, B, S):
    T = B * S

    def body(ca_ref, cg_ref, y_ref, dc_ref, w_ref, lg_ref, lb_ref,
             dca_ref, dcg_ref, dw_ref, ds_ref, glu, dyp, dwacc):
        _fill_glu(ca_ref, cg_ref, glu, S)
        dyp[pl.ds(S, CPAD), :] = jnp.zeros((CPAD, DC), F32)
        lgv, lbv = lg_ref[...], lb_ref[...]

        def sum8(v):
            return functools.reduce(jnp.add, [v[r:r + 8] for r in range(0, v.shape[0], 8)])

        P1 = 4 * CT

        def p1(i, carry):
            sb, sg, sl = carry
            t0 = pl.multiple_of(i * P1, P1)
            xhat, rstd, z = _ln_fwd(y_ref[pl.ds(t0, P1), :], lgv, lbv)
            sz = _sigmoid(z)
            dz = dc_ref[pl.ds(t0, P1), :].astype(F32) * (sz * (1.0 + z * (1.0 - sz)))
            dxhat = dz * lgv
            dy = rstd * (dxhat - jnp.mean(dxhat, axis=-1, keepdims=True)
                         - xhat * jnp.mean(dxhat * xhat, axis=-1, keepdims=True))
            dyp[pl.ds(t0, P1), :] = dy
            return sb + sum8(dy), sg + sum8(dz * xhat), sl + sum8(dz)

        z8 = jnp.zeros((8, DC), F32)
        sb, sg, sl = lax.fori_loop(0, S // P1, p1, (z8, z8, z8))
        rs = lambda v: jnp.sum(v, axis=0, keepdims=True)
        ds_ref[...] = jnp.concatenate([rs(sb), rs(sg), rs(sl), jnp.zeros((5, DC), F32)], axis=0)[None]

        def p2(i, c):
            t0 = pl.multiple_of(i * CT, CT)
            win = dyp[pl.ds(t0, 2 * CT), :]
            acc = jnp.zeros((CT, DC), F32)
            taps = _shifted(win, [30 - k for k in range(CK)])
            for k in range(CK):
                acc = acc + taps[30 - k] * w_ref[k:k + 1, :]
            a = ca_ref[pl.ds(t0, CT), :].astype(F32)
            sgt = _sigmoid(cg_ref[pl.ds(t0, CT), :].astype(F32))
            dca_ref[pl.ds(t0, CT), :] = (acc * sgt).astype(BF16)
            dcg_ref[pl.ds(t0, CT), :] = (acc * a * sgt * (1.0 - sgt)).astype(BF16)
            return c

        lax.fori_loop(0, S // CT, p2, 0)

        dwacc[...] = jnp.zeros_like(dwacc)

        def p3(i, c):
            t0 = pl.multiple_of(i * CT, CT)
            win = glu[pl.ds(t0, 2 * CT), :]
            dy = dyp[pl.ds(t0, CT), :]
            for k in range(CK):
                dwacc[k] += sum8(dy * win[k + 2:k + 2 + CT, :])
            return c

        lax.fori_loop(0, S // CT, p3, 0)
        dw_ref[...] = jnp.sum(dwacc[...], axis=1)[None]

    vec = pl.BlockSpec((1, DC), lambda b: (0, 0))
    seq = pl.BlockSpec((S, DC), lambda b: (b, 0))
    return pl.pallas_call(
        body, grid=(B,),
        in_specs=[pl.BlockSpec((S, DC), lambda b: (b, 3)), pl.BlockSpec((S, DC), lambda b: (b, 4)),
                  seq, seq, pl.BlockSpec((CT, DC), lambda b: (0, 0)), vec, vec],
        out_specs=[seq, seq, pl.BlockSpec((1, CT, DC), lambda b: (b, 0, 0)),
                   pl.BlockSpec((1, 8, DC), lambda b: (b, 0, 0))],
        out_shape=[jax.ShapeDtypeStruct((T, DC), BF16)] * 2
                  + [jax.ShapeDtypeStruct((B, CT, DC), F32), jax.ShapeDtypeStruct((B, 8, DC), F32)],
        scratch_shapes=[pltpu.VMEM((CPAD + S, DC), F32), pltpu.VMEM((S + CPAD, DC), F32),
                        pltpu.VMEM((CT, 8, DC), F32)],
        compiler_params=_cp(("parallel",)), name="conv_bwd")(u, u, y, dconv, cw, lg, lb)


def _local_step(x, target, norms, W, B, S, comm=None):
    qg2 = jnp.concatenate([norms["q_norm"], norms["q_norm"]], axis=1)
    kg2 = jnp.concatenate([norms["k_norm"], norms["k_norm"]], axis=1)
    cw = jnp.concatenate([W["conv_w"], jnp.zeros((1, DC), F32)], axis=0)

    W = dict(W)
    for kern, tag in (("gate", "wu1"), ("up", "wd1"), ("down", "mix")):
        carry = comm.gathers[tag] if comm else None
        if kern == "gate":
            (n1, g1), got = _ffn_gate(x, norms["ffn1_norm"], W["wg1"], "ffn1_gate", carry=carry)
        elif kern == "up":
            (u1, act1), got = _ffn_up(n1, g1, W["wu1"], "ffn1_up", carry=carry)
        else:
            (h1,), got = _ffn_down(x, act1, W["wd1"], "ffn1_down", carry=carry)
        if comm:
            W.update(comm.gathered(tag, got))
    u, n2 = _mix_in(h1, norms["mix_norm"], W["win"])
    (attn, lse), got = _attn_fwd(u, qg2, kg2, B, S, carry=comm.gathers["ffn2"] if comm else None)
    if comm:
        W = dict(W, **comm.gathered("ffn2", got))
    conv, y = _conv_fwd(u, cw, norms["conv_b"], norms["conv_ln_g"], norms["conv_ln_b"], B, S)
    h2 = _mix_out(h1, attn, conv, W["wout"])
    (h3, n3, g2, u2, dout, dyb, sq), _ = _ffn_forward(h2, norms["ffn2_norm"], W["wg2"], W["wu2"], W["wd2"], target,
                                                     "ffn2_fwd")
    del h3
    loss = (0.5 / D) * jnp.sum(sq)

    (a2, dg2, du2, dh2, dgn_ffn2), _ = _ffn_bwd_act(dyb, g2, u2, h2, dout, norms["ffn2_norm"],
                                                   W["wg2"], W["wu2"], W["wd2"], "ffn2_bwd_act")
    dwd2, _ = _ffn_bwd_w(a2, dyb, "ffn2_bwd_wd")
    dwg2, _ = _ffn_bwd_w(dg2, n3, "ffn2_bwd_wg")
    dwu2, _ = _ffn_bwd_w(du2, n3, "ffn2_bwd_wu")
    dattn, dconv, dwout = _mix_out_bwd(dh2, attn, conv, W["wout"])
    carry = comm.reduce_start("ffn2", {"wg2": dwg2, "wu2": dwu2, "wd2": dwd2, "wout": dwout}) if comm else None
    (dq, dk, dv, dgn_qk), got = _attn_bwd(u, attn, dattn, lse, qg2, kg2, B, S, carry=carry)
    if comm:
        comm.reduce_done(carry, got)
    dca, dcg, dcw, dcs = _conv_bwd(u, y, dconv, cw, norms["conv_ln_g"], norms["conv_ln_b"], B, S)
    dwin, dh1, dyb1, dgn_mix = _mix_in_bwd((dq, dk, dv, dca, dcg), W["win"], n2, h1, dh2, norms["mix_norm"])
    (a1, dg1, du1, gx, dgn_ffn1), _ = _ffn_bwd_act(dyb1, g1, u1, x, dh1, norms["ffn1_norm"],
                                                  W["wg1"], W["wu1"], W["wd1"], "ffn1_bwd_act")
    carry = comm.reduce_start("win", {"win": dwin}) if comm else None
    dwd1, got = _ffn_bwd_w(a1, dyb1, "ffn1_bwd_wd", carry=carry)
    if comm:
        comm.reduce_done(carry, got)
        carry = comm.reduce_start("wd1", {"wd1": dwd1})
    dwg1, got = _ffn_bwd_w(dg1, n1, "ffn1_bwd_wg", carry=carry)
    if comm:
        comm.reduce_done(carry, got)
        carry = comm.reduce_start("wg1", {"wg1": dwg1})
    dwu1, got = _ffn_bwd_w(du1, n1, "ffn1_bwd_wu", carry=carry)
    if comm:
        comm.reduce_done(carry, got)
        comm.reduce_now("wu1", {"wu1": dwu1})

    qk = jnp.sum(dgn_qk, axis=0)
    cs = jnp.sum(dcs, axis=0)
    small = {
        "ffn1_norm": jnp.sum(dgn_ffn1, axis=0),
        "mix_norm": jnp.sum(dgn_mix, axis=0),
        "q_norm": qk[0:1, 0:HD] + qk[0:1, HD:2 * HD],
        "k_norm": qk[1:2, 0:HD] + qk[1:2, HD:2 * HD],
        "conv_w": jnp.sum(dcw, axis=0)[0:CK],
        "conv_b": cs[0:1],
        "conv_ln_g": cs[1:2],
        "conv_ln_b": cs[2:3],
        "ffn2_norm": jnp.sum(dgn_ffn2, axis=0),
    }
    big = {"wg1": dwg1, "wu1": dwu1, "wd1": dwd1, "win": dwin, "wout": dwout,
           "wg2": dwg2, "wu2": dwu2, "wd2": dwd2}
    return loss, gx, big, small


HBM = pl.BlockSpec(memory_space=pltpu.HBM)
VMEM = pl.BlockSpec(memory_space=pltpu.VMEM)


def _place():
    return lax.axis_index("x"), lax.axis_index("y"), lax.axis_index("c")


class _GatherCarry:
    def __init__(self, shards):
        nt = len(shards)
        self.shards = shards
        self.in_arrays = [s for s, _ in shards]
        self.in_specs = [VMEM] * nt
        self.out_shape = [jax.ShapeDtypeStruct((NDEV * s.shape[0], s.shape[1]), dt) for s, dt in shards]
        self.out_specs = [HBM] * nt
        self.scratch = ([pltpu.VMEM(s.shape, dt) for s, dt in shards]
                        + [pltpu.SemaphoreType.DMA((nt, 7)), pltpu.SemaphoreType.DMA((nt, 7)),
                           pltpu.SemaphoreType.DMA((nt,))])

    def _copies(self, outs, scr):
        nt = len(self.shards)
        stages = scr[:nt]
        send_sems, recv_sems, local_sems = scr[nt:]
        x, y, c = _place()
        me, sibling = (x, y, c), (x, y, 1 - c)
        chips = [(1 - x, y), (x, 1 - y), (1 - x, 1 - y)]

        def rows(t, px, py, pc):
            r = self.shards[t][0].shape[0]
            return outs[t].at[pl.ds((4 * px + 2 * py + pc) * r, r), :]

        def copy(t, k, block, to, src=None):
            return pltpu.make_async_remote_copy(
                src_ref=rows(t, *block) if src is None else src, dst_ref=rows(t, *block),
                send_sem=send_sems.at[t, k], recv_sem=recv_sems.at[t, k],
                device_id=to, device_id_type=MESH)

        mine = [pltpu.make_async_copy(stages[t], rows(t, *me), local_sems.at[t]) for t in range(nt)]
        first = []
        for t in range(nt):
            first.append(copy(t, 0, me, sibling, src=stages[t]))
            first += [copy(t, 1 + j, me, (*chip, c), src=stages[t]) for j, chip in enumerate(chips)]
        return copy, mine, first, me, sibling, chips, c

    def start(self, ins, outs, scr):
        _, mine, first, *_ = self._copies(outs, scr)
        for t, (_, dt) in enumerate(self.shards):
            scr[t][...] = ins[t][...].astype(dt)
        for cp in mine + first:
            cp.start()

    def finish(self, ins, outs, scr):
        copy, mine, first, me, sibling, chips, c = self._copies(outs, scr)
        nt = len(self.shards)
        passed = []
        for j, chip in enumerate(chips):
            for t in range(nt):
                copy(t, 1 + j, (*chip, c), me).wait_recv()
                cp = copy(t, 4 + j, (*chip, c), sibling)
                cp.start()
                passed.append(cp)
        for t in range(nt):
            copy(t, 0, sibling, me).wait_recv()
            for j, chip in enumerate(chips):
                copy(t, 4 + j, (*chip, 1 - c), me).wait_recv()
        for cp in first + passed:
            cp.wait_send()
        for cp in mine:
            cp.wait()


def _run_carry(carry, name):
    def body(*refs):
        n_in, n_out = len(carry.in_arrays), len(carry.out_shape)
        ins, outs, scr = refs[:n_in], refs[n_in:n_in + n_out], refs[n_in + n_out:]
        carry.start(ins, outs, scr)
        carry.finish(ins, outs, scr)

    return pl.pallas_call(
        body, in_specs=carry.in_specs, out_specs=carry.out_specs, out_shape=carry.out_shape,
        scratch_shapes=carry.scratch, compiler_params=pltpu.CompilerParams(vmem_limit_bytes=VMEM_LIMIT),
        name=name)(*carry.in_arrays)


def _sibling_reduce(grads, name):
    nt = len(grads)
    g4 = [g.reshape(4, 2, g.shape[0] // NDEV, g.shape[1]) for g in grads]

    def body(*refs):
        ins, outs = refs[:nt], refs[nt:2 * nt]
        recv, own = refs[2 * nt:3 * nt], refs[3 * nt:4 * nt]
        send_sems, recv_sems, load_sems, store_sems = refs[4 * nt:]
        x, y, c = _place()
        sends = [pltpu.make_async_remote_copy(
            src_ref=ins[t].at[:, 1 - c], dst_ref=recv[t], send_sem=send_sems.at[t], recv_sem=recv_sems.at[t],
            device_id=(x, y, 1 - c), device_id_type=MESH) for t in range(nt)]
        loads = [pltpu.make_async_copy(ins[t].at[:, c], own[t], load_sems.at[t]) for t in range(nt)]
        stores = [pltpu.make_async_copy(own[t], outs[t], store_sems.at[t]) for t in range(nt)]
        for cp in sends + loads:
            cp.start()
        for t in range(nt):
            loads[t].wait()
            sends[t].wait_recv()
            for q in range(4):
                own[t][q] = (own[t][q].astype(F32) + recv[t][q].astype(F32)).astype(BF16)
            stores[t].start()
        for t in range(nt):
            sends[t].wait_send()
            stores[t].wait()

    blocks = [(4,) + g.shape[2:] for g in g4]
    return pl.pallas_call(
        body, in_specs=[HBM] * nt, out_specs=[HBM] * nt,
        out_shape=[jax.ShapeDtypeStruct(b, BF16) for b in blocks],
        scratch_shapes=[pltpu.VMEM(b, BF16) for b in blocks] * 2 + [pltpu.SemaphoreType.DMA((nt,))] * 4,
        compiler_params=pltpu.CompilerParams(vmem_limit_bytes=VMEM_LIMIT), name=name)(*g4)


class _ExchangeCarry:
    def __init__(self, names, parts):
        nt = len(parts)
        self.names = names
        self.in_arrays = list(parts)
        self.in_specs = [HBM] * nt
        self.out_shape = [jax.ShapeDtypeStruct(p.shape, BF16) for p in parts]
        self.out_specs = [HBM] * nt
        self.scratch = [pltpu.SemaphoreType.DMA((nt, 3)), pltpu.SemaphoreType.DMA((nt, 3)),
                        pltpu.SemaphoreType.DMA((nt,))]

    def _copies(self, ins, outs, scr):
        send_sems, recv_sems, local_sems = scr
        x, y, c = _place()
        qme = 2 * x + y
        chips = [(1 - x, y), (x, 1 - y), (1 - x, 1 - y)]
        cps = []
        for t in range(len(ins)):
            cps.append(pltpu.make_async_copy(ins[t].at[qme], outs[t].at[qme], local_sems.at[t]))
            for k, (cx, cy) in enumerate(chips):
                cps.append(pltpu.make_async_remote_copy(
                    src_ref=ins[t].at[2 * cx + cy], dst_ref=outs[t].at[qme],
                    send_sem=send_sems.at[t, k], recv_sem=recv_sems.at[t, k],
                    device_id=(cx, cy, c), device_id_type=MESH))
        return cps

    def start(self, ins, outs, scr):
        for cp in self._copies(ins, outs, scr):
            cp.start()

    def finish(self, ins, outs, scr):
        for cp in self._copies(ins, outs, scr):
            cp.wait()


class _Comm:
    def __init__(self, groups):
        self.names = {tag: list(g) for tag, g in groups.items()}
        self.gathers = {tag: _GatherCarry(list(g.values())) for tag, g in groups.items()}
        self.reduced = {}

    def gathered(self, tag, outs):
        return dict(zip(self.names[tag], outs))

    def reduce_start(self, tag, grads):
        names = list(grads)
        return _ExchangeCarry(names, _sibling_reduce([grads[n] for n in names], "sibling_reduce_" + tag))

    def reduce_done(self, carry, outs):
        self.reduced.update(zip(carry.names, outs))

    def reduce_now(self, tag, grads):
        carry = self.reduce_start(tag, grads)
        self.reduce_done(carry, _run_carry(carry, "chip_exchange_" + tag))


def _adamw_math(w, g, m, v):
    m = B1 * m + (1.0 - B1) * g
    v = B2 * v + (1.0 - B2) * (g * g)
    m_hat = m / (1.0 - B1 ** STEP)
    v_hat = v / (1.0 - B2 ** STEP)
    delta = -LR * (m_hat / (jnp.sqrt(v_hat) + AEPS) + WD * w)
    return delta, m, v


def _adamw_big(recv, w, m, v, name):
    def body(r_ref, w_ref, m_ref, v_ref, g_ref, d_ref, mo_ref, vo_ref):
        g = r_ref[0].astype(F32)
        for q in range(1, 4):
            g = g + r_ref[q].astype(F32)
        d, mn, vn = _adamw_math(w_ref[...], g, m_ref[...], v_ref[...])
        g_ref[...] = g
        d_ref[...] = d
        mo_ref[...] = mn
        vo_ref[...] = vn

    return pl.pallas_call(
        body, out_shape=[jax.ShapeDtypeStruct(w.shape, F32)] * 4,
        compiler_params=pltpu.CompilerParams(vmem_limit_bytes=VMEM_LIMIT), name=name)(recv, w, m, v)


SMALL_ROWS = 40
CW_ROWS = 16
SMALL_TOTAL = SMALL_ROWS + NDEV * CW_ROWS


def _small_step(gvec, wvec, mvec, vvec):
    nr = SMALL_ROWS + CW_ROWS

    def body(g_ref, w_ref, m_ref, v_ref, go_ref, d_ref, mo_ref, vo_ref, slots, send_sems, recv_sems):
        x, y, c = _place()
        me = 4 * x + 2 * y + c
        slots[me] = g_ref[...]
        cps = []
        for k in range(1, NDEV):
            fx, fy, fc = (k >> 2) & 1, (k >> 1) & 1, k & 1
            peer = (x ^ fx, y ^ fy, c ^ fc)
            cp = pltpu.make_async_remote_copy(
                src_ref=g_ref, dst_ref=slots.at[me],
                send_sem=send_sems.at[k - 1], recv_sem=recv_sems.at[k - 1],
                device_id=peer, device_id_type=MESH)
            cp.start()
            cps.append(cp)
        for cp in cps:
            cp.wait()
        tot = slots[0]
        for j in range(1, NDEV):
            tot = tot + slots[j]
        slots[0] = tot
        g = jnp.concatenate(
            [tot[0:SMALL_ROWS], slots[0, pl.ds(pl.multiple_of(SMALL_ROWS + me * CW_ROWS, 8), CW_ROWS), :]], axis=0)
        d, mn, vn = _adamw_math(w_ref[...], g, m_ref[...], v_ref[...])
        go_ref[...] = g
        d_ref[...] = d
        mo_ref[...] = mn
        vo_ref[...] = vn

    return pl.pallas_call(
        body, in_specs=[VMEM] * 4, out_specs=[VMEM] * 4,
        out_shape=[jax.ShapeDtypeStruct((nr, 128), F32)] * 4,
        scratch_shapes=[pltpu.VMEM((NDEV, SMALL_TOTAL, 128), F32),
                        pltpu.SemaphoreType.DMA((NDEV - 1,)), pltpu.SemaphoreType.DMA((NDEV - 1,))],
        name="small_step")(gvec, wvec, mvec, vvec)


SMALL_NAMES = ("ffn1_norm", "mix_norm", "ffn2_norm", "conv_b", "conv_ln_g", "conv_ln_b", "q_norm", "k_norm")


LOSS_ROW = 38


def _pack_small(vals, loss=None):
    parts = []
    for n in SMALL_NAMES:
        a = vals[n].reshape(-1)
        if a.shape[0] < 128:
            a = jnp.concatenate([a, jnp.zeros((128 - a.shape[0],), F32)])
        parts.append(a.reshape(-1, 128))
    used = sum(p.shape[0] for p in parts)
    assert used == LOSS_ROW
    tail = jnp.zeros((SMALL_ROWS - used, 128), F32)
    if loss is not None:
        tail = tail.at[0, 0].set(loss)
    parts.append(tail)
    return jnp.concatenate(parts, axis=0)


def _unpack_small(vec, shapes):
    out, r = {}, 0
    for n in SMALL_NAMES:
        size = shapes[n][1]
        nr = max(size // 128, 1)
        out[n] = vec[r:r + nr].reshape(-1)[:size].reshape(1, size)
        r += nr
    return out


def _pack_cw(a):
    flat = a.reshape(-1)
    return jnp.concatenate([flat, jnp.zeros((CW_ROWS * 128 - flat.shape[0],), F32)]).reshape(CW_ROWS, 128)


def _unpack_cw(v):
    return v.reshape(-1)[:CK * HD].reshape(1, CK, HD)


def kernel(x, ffn1_norm, ffn1_w_gate, ffn1_w_up, ffn1_w_down, mix_norm, w_in, q_norm, k_norm, conv_w, conv_b, conv_ln_g, conv_ln_b, w_out, ffn2_norm, ffn2_w_gate, ffn2_w_up, ffn2_w_down, loss_target, m_ffn1_norm, m_ffn1_w_gate, m_ffn1_w_up, m_ffn1_w_down, m_mix_norm, m_w_in, m_q_norm, m_k_norm, m_conv_w, m_conv_b, m_conv_ln_g, m_conv_ln_b, m_w_out, m_ffn2_norm, m_ffn2_w_gate, m_ffn2_w_up, m_ffn2_w_down, v_ffn1_norm, v_ffn1_w_gate, v_ffn1_w_up, v_ffn1_w_down, v_mix_norm, v_w_in, v_q_norm, v_k_norm, v_conv_w, v_conv_b, v_conv_ln_g, v_conv_ln_b, v_w_out, v_ffn2_norm, v_ffn2_w_gate, v_ffn2_w_up, v_ffn2_w_down):
    P = dict(ffn1_norm=ffn1_norm, ffn1_w_gate=ffn1_w_gate, ffn1_w_up=ffn1_w_up, ffn1_w_down=ffn1_w_down,
             mix_norm=mix_norm, w_in=w_in, q_norm=q_norm, k_norm=k_norm, conv_w=conv_w, conv_b=conv_b,
             conv_ln_g=conv_ln_g, conv_ln_b=conv_ln_b, w_out=w_out, ffn2_norm=ffn2_norm,
             ffn2_w_gate=ffn2_w_gate, ffn2_w_up=ffn2_w_up, ffn2_w_down=ffn2_w_down)
    M = dict(ffn1_norm=m_ffn1_norm, ffn1_w_gate=m_ffn1_w_gate, ffn1_w_up=m_ffn1_w_up, ffn1_w_down=m_ffn1_w_down,
             mix_norm=m_mix_norm, w_in=m_w_in, q_norm=m_q_norm, k_norm=m_k_norm, conv_w=m_conv_w, conv_b=m_conv_b,
             conv_ln_g=m_conv_ln_g, conv_ln_b=m_conv_ln_b, w_out=m_w_out, ffn2_norm=m_ffn2_norm,
             ffn2_w_gate=m_ffn2_w_gate, ffn2_w_up=m_ffn2_w_up, ffn2_w_down=m_ffn2_w_down)
    V = dict(ffn1_norm=v_ffn1_norm, ffn1_w_gate=v_ffn1_w_gate, ffn1_w_up=v_ffn1_w_up, ffn1_w_down=v_ffn1_w_down,
             mix_norm=v_mix_norm, w_in=v_w_in, q_norm=v_q_norm, k_norm=v_k_norm, conv_w=v_conv_w, conv_b=v_conv_b,
             conv_ln_g=v_conv_ln_g, conv_ln_b=v_conv_ln_b, w_out=v_w_out, ffn2_norm=v_ffn2_norm,
             ffn2_w_gate=v_ffn2_w_gate, ffn2_w_up=v_ffn2_w_up, ffn2_w_down=v_ffn2_w_down)
    order = ["ffn1_norm", "ffn1_w_gate", "ffn1_w_up", "ffn1_w_down", "mix_norm", "w_in", "q_norm", "k_norm",
             "conv_w", "conv_b", "conv_ln_g", "conv_ln_b", "w_out", "ffn2_norm", "ffn2_w_gate", "ffn2_w_up",
             "ffn2_w_down"]
    B, S, _ = x.shape
    T = B * S

    bigs = [("wg1", "ffn1_w_gate", True), ("wu1", "ffn1_w_up", True), ("wd1", "ffn1_w_down", False),
            ("win", "w_in", True), ("wout", "w_out", False),
            ("wg2", "ffn2_w_gate", True), ("wu2", "ffn2_w_up", True), ("wd2", "ffn2_w_down", False)]
    hm = lambda a, tr: jnp.transpose(a[0]) if tr else a[0]
    cw_pad = jnp.zeros((32, 128), F32).at[0:CK, 0:HD].set(conv_w[0])
    shard = {ln: (hm(P[pn], tr), BF16) for ln, pn, tr in bigs}
    gathered = _run_carry(_GatherCarry([shard["wg1"], (cw_pad, F32)]), "gather_first")
    W = {"wg1": gathered[0]}
    cwg = gathered[1].reshape(NDEV, 32, 128)[:, 0:CK, 0:HD]
    W["conv_w"] = jnp.transpose(cwg, (1, 0, 2)).reshape(CK, DC)
    norms = {n: P[n] for n in SMALL_NAMES}
    comm = _Comm({"wu1": {"wu1": shard["wu1"]}, "wd1": {"wd1": shard["wd1"]},
                  "mix": {n: shard[n] for n in ("win", "wout")},
                  "ffn2": {n: shard[n] for n in ("wg2", "wu2", "wd2")}})

    loss_part, gx, _, small = _local_step(x.reshape(T, D), loss_target.reshape(T, D), norms, W, B, S, comm)

    G, Dl, Mn, Vn = {}, {}, {}, {}
    for ln, pn, tr in bigs:
        outs = _adamw_big(comm.reduced[ln], hm(P[pn], tr), hm(M[pn], tr), hm(V[pn], tr), "adamw_" + ln)
        G[pn], Dl[pn], Mn[pn], Vn[pn] = [(jnp.transpose(o) if tr else o)[None] for o in outs]

    dcw = small["conv_w"].reshape(CK, NDEV, HD).transpose(1, 0, 2)
    gvec = jnp.concatenate([_pack_small(small, loss_part)] + [_pack_cw(dcw[j]) for j in range(NDEV)], axis=0)
    pack = lambda dct: jnp.concatenate([_pack_small({n: dct[n] for n in SMALL_NAMES}), _pack_cw(dct["conv_w"][0])], axis=0)
    go, do, mo, vo = _small_step(gvec, pack(P), pack(M), pack(V))
    loss = go[LOSS_ROW, 0]
    shapes = {n: P[n].shape for n in SMALL_NAMES}
    for dst, vec in ((G, go), (Dl, do), (Mn, mo), (Vn, vo)):
        dst.update(_unpack_small(vec[0:SMALL_ROWS], shapes))
        dst["conv_w"] = _unpack_cw(vec[SMALL_ROWS:])

    return (loss, gx.reshape(B, S, D), *[G[n] for n in order], *[Dl[n] for n in order],
            *[Mn[n] for n in order], *[Vn[n] for n in order])
```

```python
import functools

import jax
import jax.numpy as jnp
from jax import lax
from jax.experimental import pallas as pl
from jax.experimental.pallas import tpu as pltpu

F32 = jnp.float32
BF16 = jnp.bfloat16

D = 1024
FF = 2816
HD = 64
DA = 512
DC = 512
DIN = 2560
CK = 31
BLK = 128
DILS = (1, 4, 16)
EPS = 1e-6
NDEV = 8
MESH = pl.DeviceIdType.MESH

LR, B1, B2, AEPS, WD, STEP = 0.001, 0.9, 0.999, 1e-08, 0.01, 10

NT = (((1,), (1,)), ((), ()))
TN = (((0,), (0,)), ((), ()))

VMEM_LIMIT = 56 * 1024 * 1024


def _cp(sem=None):
    return pltpu.CompilerParams(dimension_semantics=sem, vmem_limit_bytes=VMEM_LIMIT)


def _sigmoid(x):
    return 0.5 * (jnp.tanh(0.5 * x) + 1.0)


def _pallas(body, args, *, grid, in_specs, out_specs, out_shape, scratch_shapes, sem, name, carry=None):
    if carry is None:
        outs = pl.pallas_call(body, grid=grid, in_specs=in_specs, out_specs=out_specs, out_shape=out_shape,
                              scratch_shapes=scratch_shapes, compiler_params=_cp(sem), name=name)(*args)
        return outs, None
    n_in, n_out, n_scr = len(in_specs), len(out_shape), len(scratch_shapes)
    c_in, c_out = len(carry.in_arrays), len(carry.out_shape)

    def wrapped(*refs):
        ins, refs = refs[:n_in], refs[n_in:]
        cins, refs = refs[:c_in], refs[c_in:]
        outs, refs = refs[:n_out], refs[n_out:]
        couts, refs = refs[:c_out], refs[c_out:]
        scr, cscr = refs[:n_scr], refs[n_scr:]
        ids = [pl.program_id(a) for a in range(len(grid))]
        is_first = functools.reduce(jnp.logical_and, [i == 0 for i in ids])
        is_last = functools.reduce(jnp.logical_and, [i == n - 1 for i, n in zip(ids, grid)])

        @pl.when(is_first)
        def _():
            carry.start(cins, couts, cscr)

        body(*ins, *outs, *scr)

        @pl.when(is_last)
        def _():
            carry.finish(cins, couts, cscr)

    outs = pl.pallas_call(
        wrapped, grid=grid, in_specs=list(in_specs) + carry.in_specs, out_specs=list(out_specs) + carry.out_specs,
        out_shape=list(out_shape) + carry.out_shape, scratch_shapes=list(scratch_shapes) + carry.scratch,
        compiler_params=_cp(("arbitrary",) * len(grid)), name=name)(*args, *carry.in_arrays)
    return outs[:n_out], outs[n_out:]


def _ffn_fwd(x, gain, wg, wu, wd, target, name, carry=None):
    T = x.shape[0]
    tm, tf = 1024, 256
    nt, nf = T // tm, FF // tf
    with_loss = target is not None

    def body(*refs):
        if with_loss:
            (x_ref, gain_ref, wg_ref, wu_ref, wd_ref, t_ref,
             h_ref, n_ref, g_ref, u_ref, dout_ref, dyb_ref, sq_ref, nb_scr, acc_scr) = refs
        else:
            (x_ref, gain_ref, wg_ref, wu_ref, wd_ref,
             h_ref, n_ref, g_ref, u_ref, nb_scr, acc_scr) = refs
        f = pl.program_id(1)

        @pl.when(f == 0)
        def _():
            xv = x_ref[...]
            r = lax.rsqrt(jnp.mean(xv * xv, axis=-1, keepdims=True) + EPS)
            nb = (xv * r * gain_ref[...]).astype(BF16)
            nb_scr[...] = nb
            n_ref[...] = nb
            acc_scr[...] = jnp.zeros_like(acc_scr)

        nb = nb_scr[...]
        g = lax.dot_general(nb, wg_ref[...], NT, preferred_element_type=F32)
        u = lax.dot_general(nb, wu_ref[...], NT, preferred_element_type=F32)
        a = g * _sigmoid(g) * u
        g_ref[...] = g.astype(BF16)
        u_ref[...] = u.astype(BF16)
        acc_scr[...] += jnp.dot(a.astype(BF16), wd_ref[...], preferred_element_type=F32)

        @pl.when(f == nf - 1)
        def _():
            h = x_ref[...] + 0.5 * acc_scr[...]
            h_ref[...] = h
            if with_loss:
                e = h - t_ref[...]
                dout = e * (1.0 / D)
                dout_ref[...] = dout
                dyb_ref[...] = (0.5 * dout).astype(BF16)
                sq_ref[...] = jnp.sum(e * e, axis=0, keepdims=True)[None]

    row = pl.BlockSpec((tm, D), lambda t, f: (t, 0))
    wspec = pl.BlockSpec((tf, D), lambda t, f: (f, 0))
    gspec = pl.BlockSpec((tm, tf), lambda t, f: (t, f))
    in_specs = [row, pl.BlockSpec((1, D), lambda t, f: (0, 0)), wspec, wspec, wspec]
    out_shape = [jax.ShapeDtypeStruct((T, D), F32), jax.ShapeDtypeStruct((T, D), BF16),
                 jax.ShapeDtypeStruct((T, FF), BF16), jax.ShapeDtypeStruct((T, FF), BF16)]
    out_specs = [row, row, gspec, gspec]
    args = [x, gain, wg, wu, wd]
    if with_loss:
        in_specs.append(row)
        args.append(target)
        out_shape += [jax.ShapeDtypeStruct((T, D), F32), jax.ShapeDtypeStruct((T, D), BF16),
                      jax.ShapeDtypeStruct((nt, 1, D), F32)]
        out_specs += [row, row, pl.BlockSpec((1, 1, D), lambda t, f: (t, 0, 0))]
    return _pallas(
        body, args, grid=(nt, nf), in_specs=in_specs, out_specs=out_specs, out_shape=out_shape,
        scratch_shapes=[pltpu.VMEM((tm, D), BF16), pltpu.VMEM((tm, D), F32)],
        sem=("parallel", "arbitrary"), name=name, carry=carry)


def _ffn_bwd(dyb, nb, g, u, wg, wu, wd, name, carry=None):
    T = dyb.shape[0]
    tm, tf = 512, 256
    nt, nf = T // tm, FF // tf

    def body(dy_ref, n_ref, g_ref, u_ref, wg_ref, wu_ref, wd_ref,
             dwg_ref, dwu_ref, dwd_ref, dn_ref, ag_scr, au_scr, ad_scr):
        f, t = pl.program_id(0), pl.program_id(1)

        @pl.when(t == 0)
        def _():
            ag_scr[...] = jnp.zeros_like(ag_scr)
            au_scr[...] = jnp.zeros_like(au_scr)
            ad_scr[...] = jnp.zeros_like(ad_scr)

        dy = dy_ref[...]
        n = n_ref[...]
        gv = g_ref[...].astype(F32)
        uv = u_ref[...].astype(F32)
        da = lax.dot_general(dy, wd_ref[...], NT, preferred_element_type=F32)
        sg = _sigmoid(gv)
        silu = gv * sg
        ab = (silu * uv).astype(BF16)
        dgb = (da * uv * (sg * (1.0 + gv * (1.0 - sg)))).astype(BF16)
        dub = (da * silu).astype(BF16)
        ad_scr[...] += lax.dot_general(ab, dy, TN, preferred_element_type=F32)
        ag_scr[...] += lax.dot_general(dgb, n, TN, preferred_element_type=F32)
        au_scr[...] += lax.dot_general(dub, n, TN, preferred_element_type=F32)
        dn = (jnp.dot(dgb, wg_ref[...], preferred_element_type=F32)
              + jnp.dot(dub, wu_ref[...], preferred_element_type=F32))
        rows = pl.ds(pl.multiple_of(t * tm, tm), tm)

        @pl.when(f == 0)
        def _():
            dn_ref[rows, :] = dn

        @pl.when(f > 0)
        def _():
            dn_ref[rows, :] += dn

        @pl.when(t == nt - 1)
        def _():
            dwg_ref[...] = ag_scr[...].astype(BF16)
            dwu_ref[...] = au_scr[...].astype(BF16)
            dwd_ref[...] = ad_scr[...].astype(BF16)

    row = pl.BlockSpec((tm, D), lambda f, t: (t, 0))
    gspec = pl.BlockSpec((tm, tf), lambda f, t: (t, f))
    wspec = pl.BlockSpec((tf, D), lambda f, t: (f, 0))
    return _pallas(
        body, (dyb, nb, g, u, wg, wu, wd), grid=(nf, nt),
        in_specs=[row, row, gspec, gspec, wspec, wspec, wspec],
        out_specs=[wspec, wspec, wspec, pl.BlockSpec((T, D), lambda f, t: (0, 0))],
        out_shape=[jax.ShapeDtypeStruct((FF, D), BF16)] * 3 + [jax.ShapeDtypeStruct((T, D), F32)],
        scratch_shapes=[pltpu.VMEM((tf, D), F32)] * 3,
        sem=("arbitrary", "arbitrary"), name=name, carry=carry)


FC = 256


def _resident(shape):
    return pl.BlockSpec(shape, lambda *_: (0,) * len(shape), pipeline_mode=pl.Buffered(1))


def _ffn_forward(x, gain, wg, wu, wd, target, name, carry=None):
    T = x.shape[0]
    tm = 512
    nt = T // tm
    with_loss = target is not None

    def body(*refs):
        if with_loss:
            (x_ref, gain_ref, wg_ref, wu_ref, wd_ref, t_ref,
             h_ref, n_ref, g_ref, u_ref, dout_ref, dyb_ref, sq_ref, a_scr) = refs
        else:
            x_ref, gain_ref, wg_ref, wu_ref, wd_ref, h_ref, n_ref, g_ref, u_ref, a_scr = refs
        xv = x_ref[...]
        r = lax.rsqrt(jnp.mean(xv * xv, axis=-1, keepdims=True) + EPS)
        n_ref[...] = (xv * r * gain_ref[...]).astype(BF16)
        for c in range(FF // FC):
            cols = slice(c * FC, (c + 1) * FC)
            nb = n_ref[...]
            g = lax.dot_general(nb, wg_ref[cols, :], NT, preferred_element_type=F32)
            u = lax.dot_general(nb, wu_ref[cols, :], NT, preferred_element_type=F32)
            g_ref[:, cols] = g.astype(BF16)
            u_ref[:, cols] = u.astype(BF16)
            a_scr[:, cols] = (g * _sigmoid(g) * u).astype(BF16)
        h = xv + 0.5 * jnp.dot(a_scr[...], wd_ref[...], preferred_element_type=F32)
        h_ref[...] = h
        if with_loss:
            e = h - t_ref[...]
            dout = e * (1.0 / D)
            dout_ref[...] = dout
            dyb_ref[...] = (0.5 * dout).astype(BF16)
            sq_ref[...] = jnp.sum(e * e, axis=0, keepdims=True)[None]

    row = pl.BlockSpec((tm, D), lambda t: (t, 0))
    wide = pl.BlockSpec((tm, FF), lambda t: (t, 0))
    in_specs = [row, _resident((1, D)), _resident((FF, D)), _resident((FF, D)), _resident((FF, D))]
    out_shape = [jax.ShapeDtypeStruct((T, D), F32), jax.ShapeDtypeStruct((T, D), BF16),
                 jax.ShapeDtypeStruct((T, FF), BF16), jax.ShapeDtypeStruct((T, FF), BF16)]
    out_specs = [row, row, wide, wide]
    args = [x, gain, wg, wu, wd]
    if with_loss:
        in_specs.append(row)
        args.append(target)
        out_shape += [jax.ShapeDtypeStruct((T, D), F32), jax.ShapeDtypeStruct((T, D), BF16),
                      jax.ShapeDtypeStruct((nt, 1, D), F32)]
        out_specs += [row, row, pl.BlockSpec((1, 1, D), lambda t: (t, 0, 0))]
    return _pallas(
        body, args, grid=(nt,), in_specs=in_specs, out_specs=out_specs, out_shape=out_shape,
        scratch_shapes=[pltpu.VMEM((tm, FF), BF16)], sem=("parallel",), name=name, carry=carry)


def _ffn_gate(x, gain, wg, name, carry=None):
    T = x.shape[0]
    tm = 512

    def body(x_ref, gain_ref, wg_ref, n_ref, g_ref):
        xv = x_ref[...]
        r = lax.rsqrt(jnp.mean(xv * xv, axis=-1, keepdims=True) + EPS)
        n_ref[...] = (xv * r * gain_ref[...]).astype(BF16)
        for c in range(FF // FC):
            cols = slice(c * FC, (c + 1) * FC)
            g_ref[:, cols] = lax.dot_general(n_ref[...], wg_ref[cols, :], NT,
                                             preferred_element_type=F32).astype(BF16)

    row = pl.BlockSpec((tm, D), lambda t: (t, 0))
    wide = pl.BlockSpec((tm, FF), lambda t: (t, 0))
    return _pallas(
        body, (x, gain, wg), grid=(T // tm,), in_specs=[row, _resident((1, D)), _resident((FF, D))],
        out_specs=[row, wide], out_shape=[jax.ShapeDtypeStruct((T, D), BF16), jax.ShapeDtypeStruct((T, FF), BF16)],
        scratch_shapes=[], sem=("parallel",), name=name, carry=carry)


def _ffn_up(nb, g, wu, name, carry=None):
    T = nb.shape[0]
    tm = 512

    def body(n_ref, g_ref, wu_ref, u_ref, a_ref):
        for c in range(FF // FC):
            cols = slice(c * FC, (c + 1) * FC)
            u = lax.dot_general(n_ref[...], wu_ref[cols, :], NT, preferred_element_type=F32)
            gv = g_ref[:, cols].astype(F32)
            u_ref[:, cols] = u.astype(BF16)
            a_ref[:, cols] = (gv * _sigmoid(gv) * u).astype(BF16)

    row = pl.BlockSpec((tm, D), lambda t: (t, 0))
    wide = pl.BlockSpec((tm, FF), lambda t: (t, 0))
    return _pallas(
        body, (nb, g, wu), grid=(T // tm,), in_specs=[row, wide, _resident((FF, D))],
        out_specs=[wide, wide], out_shape=[jax.ShapeDtypeStruct((T, FF), BF16)] * 2,
        scratch_shapes=[], sem=("parallel",), name=name, carry=carry)


def _ffn_down(x, a, wd, name, carry=None):
    T = x.shape[0]
    tm = 512

    def body(x_ref, a_ref, wd_ref, h_ref):
        h_ref[...] = x_ref[...] + 0.5 * jnp.dot(a_ref[...], wd_ref[...], preferred_element_type=F32)

    row = pl.BlockSpec((tm, D), lambda t: (t, 0))
    wide = pl.BlockSpec((tm, FF), lambda t: (t, 0))
    return _pallas(
        body, (x, a, wd), grid=(T // tm,), in_specs=[row, wide, _resident((FF, D))],
        out_specs=[row], out_shape=[jax.ShapeDtypeStruct((T, D), F32)],
        scratch_shapes=[], sem=("parallel",), name=name, carry=carry)


def _ffn_bwd_act(dyb, g, u, x, dout, gain, wg, wu, wd, name, carry=None):
    T = x.shape[0]
    tm = 256
    nt = T // tm

    def body(dy_ref, g_ref, u_ref, x_ref, dout_ref, gain_ref, wg_ref, wu_ref, wd_ref,
             a_ref, dg_ref, du_ref, dx_ref, dgn_ref):
        for c in range(FF // FC):
            cols = slice(c * FC, (c + 1) * FC)
            da = lax.dot_general(dy_ref[...], wd_ref[cols, :], NT, preferred_element_type=F32)
            gv = g_ref[:, cols].astype(F32)
            uv = u_ref[:, cols].astype(F32)
            sg = _sigmoid(gv)
            silu = gv * sg
            a_ref[:, cols] = (silu * uv).astype(BF16)
            dg_ref[:, cols] = (da * uv * (sg * (1.0 + gv * (1.0 - sg)))).astype(BF16)
            du_ref[:, cols] = (da * silu).astype(BF16)
        dn = (jnp.dot(dg_ref[...], wg_ref[...], preferred_element_type=F32)
              + jnp.dot(du_ref[...], wu_ref[...], preferred_element_type=F32))
        dx, dgain = _rms_bwd_rows(dn, x_ref[...], gain_ref[...])
        dx_ref[...] = dout_ref[...] + dx
        dgn_ref[...] = dgain[None]

    row = pl.BlockSpec((tm, D), lambda t: (t, 0))
    wide = pl.BlockSpec((tm, FF), lambda t: (t, 0))
    return _pallas(
        body, (dyb, g, u, x, dout, gain, wg, wu, wd), grid=(nt,),
        in_specs=[row, wide, wide, row, row, _resident((1, D)), _resident((FF, D)), _resident((FF, D)),
                  _resident((FF, D))],
        out_specs=[wide, wide, wide, row, pl.BlockSpec((1, 1, D), lambda t: (t, 0, 0))],
        out_shape=[jax.ShapeDtypeStruct((T, FF), BF16)] * 3
                  + [jax.ShapeDtypeStruct((T, D), F32), jax.ShapeDtypeStruct((nt, 1, D), F32)],
        scratch_shapes=[], sem=("parallel",), name=name, carry=carry)


def _ffn_bwd_w(lhs, rhs, name, carry=None):
    T = rhs.shape[0]
    tf = 256

    def body(l_ref, r_ref, dw_ref):
        dw_ref[...] = lax.dot_general(l_ref[...], r_ref[...], TN, preferred_element_type=F32).astype(BF16)

    (dw,), got = _pallas(
        body, (lhs, rhs), grid=(FF // tf,),
        in_specs=[pl.BlockSpec((T, tf), lambda f: (0, f)), _resident((T, D))],
        out_specs=[pl.BlockSpec((tf, D), lambda f: (f, 0))], out_shape=[jax.ShapeDtypeStruct((FF, D), BF16)],
        scratch_shapes=[], sem=("parallel",), name=name, carry=carry)
    return dw, got


def _rms_bwd_rows(dn, xv, gain):
    r = lax.rsqrt(jnp.mean(xv * xv, axis=-1, keepdims=True) + EPS)
    xhat = xv * r
    dxhat = dn * gain
    dx = r * (dxhat - xhat * jnp.mean(dxhat * xhat, axis=-1, keepdims=True))
    return dx, jnp.sum(dn * xhat, axis=0, keepdims=True)


def _norm_bwd(dn, x, dout, gain, name):
    T = x.shape[0]
    tm = 512
    nt = T // tm

    def body(dn_ref, x_ref, dout_ref, gain_ref, dx_ref, dg_ref):
        dx, dgain = _rms_bwd_rows(dn_ref[...], x_ref[...], gain_ref[...])
        dx_ref[...] = dout_ref[...] + dx
        dg_ref[...] = dgain[None]

    row = pl.BlockSpec((tm, D), lambda t: (t, 0))
    return pl.pallas_call(
        body, grid=(nt,), in_specs=[row, row, row, pl.BlockSpec((1, D), lambda t: (0, 0))],
        out_specs=[row, pl.BlockSpec((1, 1, D), lambda t: (t, 0, 0))],
        out_shape=[jax.ShapeDtypeStruct((T, D), F32), jax.ShapeDtypeStruct((nt, 1, D), F32)],
        compiler_params=_cp(("parallel",)), name=name)(dn, x, dout, gain)


def _mix_in(h, gain, win, carry=None):
    T = h.shape[0]
    tm = 512

    def body(h_ref, gain_ref, w_ref, u_ref, n_ref):
        xv = h_ref[...]
        r = lax.rsqrt(jnp.mean(xv * xv, axis=-1, keepdims=True) + EPS)
        nb = (xv * r * gain_ref[...]).astype(BF16)
        n_ref[...] = nb
        u_ref[...] = lax.dot_general(nb, w_ref[...], NT, preferred_element_type=F32).astype(BF16)

    row = pl.BlockSpec((tm, D), lambda t: (t, 0))
    return _pallas(
        body, (h, gain, win), grid=(T // tm,),
        in_specs=[row, _resident((1, D)), _resident((DIN, D))],
        out_specs=[pl.BlockSpec((tm, DIN), lambda t: (t, 0)), row],
        out_shape=[jax.ShapeDtypeStruct((T, DIN), BF16), jax.ShapeDtypeStruct((T, D), BF16)],
        scratch_shapes=[], sem=("parallel",), name="mix_in", carry=carry)


def _mix_out(h, attn, conv, wout):
    T = h.shape[0]
    tm = 512

    def body(h_ref, a_ref, c_ref, w_ref, o_ref):
        o_ref[...] = (h_ref[...]
                      + jnp.dot(a_ref[...], w_ref[0:DA, :], preferred_element_type=F32)
                      + jnp.dot(c_ref[...], w_ref[DA:D, :], preferred_element_type=F32))

    row = pl.BlockSpec((tm, D), lambda t: (t, 0))
    half = pl.BlockSpec((tm, DA), lambda t: (t, 0))
    return pl.pallas_call(
        body, grid=(T // tm,),
        in_specs=[row, half, half, pl.BlockSpec((D, D), lambda t: (0, 0))],
        out_specs=row, out_shape=jax.ShapeDtypeStruct((T, D), F32),
        compiler_params=_cp(("parallel",)), name="mix_out")(h, attn, conv, wout)


def _mix_out_bwd(dh, attn, conv, wout):
    T = dh.shape[0]
    tm = 512
    nt = T // tm

    def body(dh_ref, a_ref, c_ref, w_ref, da_ref, dc_ref, dw_ref, acc_scr):
        t = pl.program_id(0)

        @pl.when(t == 0)
        def _():
            acc_scr[...] = jnp.zeros_like(acc_scr)

        dhb = dh_ref[...].astype(BF16)
        dmix = lax.dot_general(dhb, w_ref[...], NT, preferred_element_type=F32)
        da_ref[...] = dmix[:, 0:DA].astype(BF16)
        dc_ref[...] = dmix[:, DA:D].astype(BF16)
        acc_scr[0:DA, :] += lax.dot_general(a_ref[...], dhb, TN, preferred_element_type=F32)
        acc_scr[DA:D, :] += lax.dot_general(c_ref[...], dhb, TN, preferred_element_type=F32)

        @pl.when(t == nt - 1)
        def _():
            dw_ref[...] = acc_scr[...].astype(BF16)

    row = pl.BlockSpec((tm, D), lambda t: (t, 0))
    half = pl.BlockSpec((tm, DA), lambda t: (t, 0))
    full = pl.BlockSpec((D, D), lambda t: (0, 0))
    return pl.pallas_call(
        body, grid=(nt,), in_specs=[row, half, half, full], out_specs=[half, half, full],
        out_shape=[jax.ShapeDtypeStruct((T, DA), BF16)] * 2 + [jax.ShapeDtypeStruct((D, D), BF16)],
        scratch_shapes=[pltpu.VMEM((D, D), F32)],
        compiler_params=_cp(("arbitrary",)), name="mix_out_bwd")(dh, attn, conv, wout)


def _mix_in_bwd(dparts, win, nb, h, dh, gain):
    T = h.shape[0]
    tm = 512
    nt = T // tm

    def body(d0, d1, d2, d3, d4, w_ref, n_ref, h_ref, dh_ref, gain_ref,
             dw_ref, dx_ref, dyb_ref, dg_ref, acc_scr):
        t = pl.program_id(0)

        @pl.when(t == 0)
        def _():
            acc_scr[...] = jnp.zeros_like(acc_scr)

        n = n_ref[...]
        dn = jnp.zeros((tm, D), F32)
        for i, d_ref in enumerate((d0, d1, d2, d3, d4)):
            dv = d_ref[...]
            dn = dn + jnp.dot(dv, w_ref[i * DA:(i + 1) * DA, :], preferred_element_type=F32)
            acc_scr[i * DA:(i + 1) * DA, :] += lax.dot_general(dv, n, TN, preferred_element_type=F32)
        dx, dgain = _rms_bwd_rows(dn, h_ref[...], gain_ref[...])
        tot = dh_ref[...] + dx
        dx_ref[...] = tot
        dyb_ref[...] = (0.5 * tot).astype(BF16)
        dg_ref[...] = dgain[None]

        @pl.when(t == nt - 1)
        def _():
            dw_ref[...] = acc_scr[...].astype(BF16)

    row = pl.BlockSpec((tm, D), lambda t: (t, 0))
    half = pl.BlockSpec((tm, DA), lambda t: (t, 0))
    full = pl.BlockSpec((DIN, D), lambda t: (0, 0))
    return pl.pallas_call(
        body, grid=(nt,),
        in_specs=[half] * 5 + [full, row, row, row, pl.BlockSpec((1, D), lambda t: (0, 0))],
        out_specs=[full, row, row, pl.BlockSpec((1, 1, D), lambda t: (t, 0, 0))],
        out_shape=[jax.ShapeDtypeStruct((DIN, D), BF16), jax.ShapeDtypeStruct((T, D), F32),
                   jax.ShapeDtypeStruct((T, D), BF16), jax.ShapeDtypeStruct((nt, 1, D), F32)],
        scratch_shapes=[pltpu.VMEM((DIN, D), F32)],
        compiler_params=_cp(("arbitrary",)), name="mix_in_bwd")(*dparts, win, nb, h, dh, gain)


def _head_masks():
    lane = lax.broadcasted_iota(jnp.int32, (1, 2 * HD), 1)
    m0 = lane < HD
    return m0, jnp.logical_not(m0)


def _stack_heads(v, m0, m1):
    z = jnp.zeros_like(v)
    return jnp.concatenate([jnp.where(m0, v, z), jnp.where(m1, v, z)], axis=0)


def _unstack_heads(v2, m0):
    return jnp.where(m0, v2[0:BLK], v2[BLK:2 * BLK])


def _head_sums(xv):
    ri = lax.broadcasted_iota(jnp.int32, (2 * HD, 2 * HD), 0)
    ci = lax.broadcasted_iota(jnp.int32, (2 * HD, 2 * HD), 1)
    ones = jnp.where((ri < HD) == (ci < HD), 1.0, 0.0).astype(BF16)
    hi = xv.astype(BF16)
    lo = (xv - hi.astype(F32)).astype(BF16)
    return (jnp.dot(hi, ones, preferred_element_type=F32) + jnp.dot(lo, ones, preferred_element_type=F32))


def _head_rms(xv):
    return lax.rsqrt(_head_sums(xv * xv) * (1.0 / HD) + EPS)


def _band_mask(first):
    qi = lax.broadcasted_iota(jnp.int32, (BLK, 2 * BLK), 0)
    ci = lax.broadcasted_iota(jnp.int32, (BLK, 2 * BLK), 1)
    band = (ci >= qi) & (ci <= qi + BLK)
    return band & ((ci >= BLK) | jnp.logical_not(first))


def _block_rows(j, d, seg):
    r, n = j // seg, j % seg
    start = r + (d * BLK) * n
    first = n == 0
    prev = jnp.where(first, start, start - d * BLK)
    return pl.ds(start, BLK, stride=d), pl.ds(prev, BLK, stride=d), first


def _block_keys(refs, cur, prev, first, single):
    if single:
        qi = lax.broadcasted_iota(jnp.int32, (BLK, BLK), 0)
        ci = lax.broadcasted_iota(jnp.int32, (BLK, BLK), 1)
        return [r[cur, :].astype(BF16) for r in refs], ci <= qi
    return ([jnp.concatenate([r[prev, :], r[cur, :]], axis=0).astype(BF16) for r in refs], _band_mask(first))


def _attn_fwd(u, qg2, kg2, B, S, carry=None):
    T = B * S
    NB = S // BLK
    scale = HD ** -0.5

    def body(q_ref, k_ref, v_ref, qg_ref, kg_ref, o_ref, lse_ref, qn, kn, vn, os_, ls_):
        m0, m1 = _head_masks()
        qv = q_ref[...].astype(F32)
        qn[...] = qv * _head_rms(qv) * (qg_ref[...] * scale)
        kv = k_ref[...].astype(F32)
        kn[...] = kv * _head_rms(kv) * kg_ref[...]
        vn[...] = v_ref[...].astype(F32)

        for i, d in enumerate(DILS):
            seg = NB // d

            def blk(j, c, i=i, d=d, seg=seg):
                cur, prev, first = _block_rows(j, d, seg)
                q2 = _stack_heads(qn[cur, :].astype(BF16), m0, m1)
                (kk, vv), mask = _block_keys((kn, vn), cur, prev, first, False)
                s = lax.dot_general(q2, kk, NT, preferred_element_type=F32)
                s = jnp.where(jnp.concatenate([mask, mask], axis=0), s, -1e30)
                mx = jnp.max(s, axis=-1, keepdims=True)
                p = jnp.exp(s - mx)
                l = jnp.sum(p, axis=-1, keepdims=True)
                o2 = jnp.dot((p * (1.0 / l)).astype(BF16), vv, preferred_element_type=F32)
                os_[i, cur, :] = _unstack_heads(o2, m0)
                ls_[i, cur, :] = _unstack_heads(mx + jnp.log(l), m0)
                return c

            lax.fori_loop(0, NB, blk, 0, unroll=8)

        def comb(c, carry):
            rows = pl.ds(pl.multiple_of(c * 256, 256), 256)
            l0, l1, l2 = ls_[0, rows, :], ls_[1, rows, :], ls_[2, rows, :]
            mx = jnp.maximum(jnp.maximum(l0, l1), l2)
            e0, e1, e2 = jnp.exp(l0 - mx), jnp.exp(l1 - mx), jnp.exp(l2 - mx)
            tot = e0 + e1 + e2
            inv = 1.0 / tot
            o = (e0 * os_[0, rows, :] + e1 * os_[1, rows, :] + e2 * os_[2, rows, :]) * inv
            o_ref[rows, :] = o.astype(BF16)
            lse_ref[rows, :] = mx + jnp.log(tot)
            return carry

        lax.fori_loop(0, S // 256, comb, 0)

    pair = 2 * HD
    blk_spec = lambda off: pl.BlockSpec((S, pair), lambda b, p, off=off: (b, off + p))
    gspec = pl.BlockSpec((1, pair), lambda b, p: (0, 0))
    return _pallas(
        body, (u, u, u, qg2, kg2), grid=(B, DA // pair),
        in_specs=[blk_spec(0), blk_spec(DA // pair), blk_spec(2 * DA // pair), gspec, gspec],
        out_specs=[blk_spec(0), blk_spec(0)],
        out_shape=[jax.ShapeDtypeStruct((T, DA), BF16), jax.ShapeDtypeStruct((T, DA), F32)],
        scratch_shapes=[pltpu.VMEM((S, pair), F32)] * 3 + [pltpu.VMEM((3, S, pair), F32)] * 2,
        sem=("parallel", "parallel"), name="attn_fwd", carry=carry)


def _attn_bwd(u, attn, dattn, lse, qg2, kg2, B, S, carry=None):
    T = B * S
    NB = S // BLK
    scale = HD ** -0.5
    pair = 2 * HD

    def body(q_ref, k_ref, v_ref, o_ref, do_ref, lse_ref, qg_ref, kg_ref,
             dq_ref, dk_ref, dv_ref, dgn_ref,
             qn, kn, vn, don, ldl, accq, acck, accv):
        m0, m1 = _head_masks()
        lane = lax.broadcasted_iota(jnp.int32, (1, pair), 1)
        qv = q_ref[...].astype(F32)
        qn[...] = qv * _head_rms(qv) * (qg_ref[...] * scale)
        kv = k_ref[...].astype(F32)
        kn[...] = kv * _head_rms(kv) * kg_ref[...]
        vn[...] = v_ref[...].astype(F32)
        dov = do_ref[...].astype(F32)
        don[...] = dov
        ldl[...] = jnp.where((lane % HD) < HD // 2, lse_ref[...], _head_sums(dov * o_ref[...].astype(F32)))
        accq[...] = jnp.zeros_like(accq)
        acck[...] = jnp.zeros_like(acck)
        accv[...] = jnp.zeros_like(accv)

        for i, d in enumerate(DILS):
            seg = NB // d

            def blk(j, c, d=d, seg=seg):
                cur, prev, first = _block_rows(j, d, seg)
                q2 = _stack_heads(qn[cur, :].astype(BF16), m0, m1)
                do2 = _stack_heads(don[cur, :].astype(BF16), m0, m1)
                (kk, vv), mask = _block_keys((kn, vn), cur, prev, first, seg == 1)
                ldv = ldl[cur, :]
                lse2 = jnp.concatenate([ldv[:, 0:1], ldv[:, HD:HD + 1]], axis=0)
                dl2 = jnp.concatenate([ldv[:, HD // 2:HD // 2 + 1], ldv[:, HD + HD // 2:HD + HD // 2 + 1]], axis=0)
                s = lax.dot_general(q2, kk, NT, preferred_element_type=F32)
                p = jnp.where(jnp.concatenate([mask, mask], axis=0), jnp.exp(s - lse2), 0.0)
                dp = lax.dot_general(do2, vv, NT, preferred_element_type=F32)
                ds = (p * (dp - dl2)).astype(BF16)
                dq_acc = _unstack_heads(jnp.dot(ds, kk, preferred_element_type=F32), m0)
                dk_acc = lax.dot_general(ds, q2, TN, preferred_element_type=F32)
                dv_acc = lax.dot_general(p.astype(BF16), do2, TN, preferred_element_type=F32)
                accq[cur, :] += dq_acc
                if seg == 1:
                    acck[cur, :] += dk_acc
                    accv[cur, :] += dv_acc
                else:
                    acck[prev, :] += dk_acc[0:BLK]
                    acck[cur, :] += dk_acc[BLK:2 * BLK]
                    accv[prev, :] += dv_acc[0:BLK]
                    accv[cur, :] += dv_acc[BLK:2 * BLK]
                return c

            lax.fori_loop(0, NB, blk, 0, unroll=4)

        def norm_bwd(x_ref, dn, gain):
            xv = x_ref[...].astype(F32)
            r = _head_rms(xv)
            xhat = xv * r
            dxhat = dn * gain
            dx = r * (dxhat - xhat * (_head_sums(dxhat * xhat) * (1.0 / HD)))
            return dx, jnp.sum(dn * xhat, axis=0, keepdims=True)

        dq, dgq = norm_bwd(q_ref, accq[...], qg_ref[...] * scale)
        dk, dgk = norm_bwd(k_ref, acck[...], kg_ref[...])
        dq_ref[...] = dq.astype(BF16)
        dk_ref[...] = dk.astype(BF16)
        dv_ref[...] = accv[...].astype(BF16)
        dgn_ref[...] = jnp.concatenate([dgq * scale, dgk, jnp.zeros((6, pair), F32)], axis=0)[None]

    blk_spec = lambda off: pl.BlockSpec((S, pair), lambda b, p, off=off: (b, off + p))
    gspec = pl.BlockSpec((1, pair), lambda b, p: (0, 0))
    np_ = DA // pair
    return _pallas(
        body, (u, u, u, attn, dattn, lse, qg2, kg2), grid=(B, np_),
        in_specs=[blk_spec(0), blk_spec(np_), blk_spec(2 * np_), blk_spec(0), blk_spec(0), blk_spec(0),
                  gspec, gspec],
        out_specs=[blk_spec(0), blk_spec(0), blk_spec(0),
                   pl.BlockSpec((1, 8, pair), lambda b, p: (b * np_ + p, 0, 0))],
        out_shape=[jax.ShapeDtypeStruct((T, DA), BF16)] * 3 + [jax.ShapeDtypeStruct((B * np_, 8, pair), F32)],
        scratch_shapes=[pltpu.VMEM((S, pair), F32)] * 8,
        sem=("parallel", "parallel"), name="attn_bwd", carry=carry)


CT = 32
CPAD = 32


def _shifted(win, offsets):
    rolled, out = {}, {}
    n = win.shape[0]
    for o in offsets:
        sub = o % 8
        if sub not in rolled:
            rolled[sub] = win if sub == 0 else pltpu.roll(win, n - sub, 0)
        out[o] = rolled[sub][o - sub:o - sub + CT, :]
    return out


def _ln_fwd(y, g, b):
    mu = jnp.mean(y, axis=-1, keepdims=True)
    yc = y - mu
    rstd = lax.rsqrt(jnp.mean(yc * yc, axis=-1, keepdims=True) + EPS)
    xhat = yc * rstd
    return xhat, rstd, xhat * g + b


def _fill_glu(ca_ref, cg_ref, glu, S):
    glu[pl.ds(0, CPAD), :] = jnp.zeros((CPAD, DC), F32)

    def fill(i, c):
        rows = pl.ds(pl.multiple_of(i * 256, 256), 256)
        a = ca_ref[rows, :].astype(F32)
        gt = cg_ref[rows, :].astype(F32)
        glu[pl.ds(pl.multiple_of(CPAD + i * 256, CT), 256), :] = a * _sigmoid(gt)
        return c

    lax.fori_loop(0, S // 256, fill, 0)


def _conv_fwd(u, cw, cb, lg, lb, B, S):
    T = B * S

    def body(ca_ref, cg_ref, w_ref, b_ref, lg_ref, lb_ref, o_ref, y_ref, glu):
        _fill_glu(ca_ref, cg_ref, glu, S)

        def step(i, c):
            t0 = pl.multiple_of(i * CT, CT)
            win = glu[pl.ds(t0, 2 * CT), :]
            acc = jnp.zeros((CT, DC), F32) + b_ref[...]
            taps = _shifted(win, [k + 2 for k in range(CK)])
            for k in range(CK):
                acc = acc + taps[k + 2] * w_ref[k:k + 1, :]
            y_ref[pl.ds(t0, CT), :] = acc
            _, _, z = _ln_fwd(acc, lg_ref[...], lb_ref[...])
            o_ref[pl.ds(t0, CT), :] = (z * _sigmoid(z)).astype(BF16)
            return c

        lax.fori_loop(0, S // CT, step, 0, unroll=2)

    vec = pl.BlockSpec((1, DC), lambda b: (0, 0))
    return pl.pallas_call(
        body, grid=(B,),
        in_specs=[pl.BlockSpec((S, DC), lambda b: (b, 3)), pl.BlockSpec((S, DC), lambda b: (b, 4)),
                  pl.BlockSpec((CT, DC), lambda b: (0, 0)), vec, vec, vec],
        out_specs=[pl.BlockSpec((S, DC), lambda b: (b, 0))] * 2,
        out_shape=[jax.ShapeDtypeStruct((T, DC), BF16), jax.ShapeDtypeStruct((T, DC), F32)],
        scratch_shapes=[pltpu.VMEM((CPAD + S, DC), F32)],
        compiler_params=_cp(("parallel",)), name="conv_fwd")(u, u, cw, cb, lg, lb)


def _conv_bwd(u, y, dconv, cw, lg, lb, B, S):
    T = B * S

    def body(ca_ref, cg_ref, y_ref, dc_ref, w_ref, lg_ref, lb_ref,
             dca_ref, dcg_ref, dw_ref, ds_ref, glu, dyp, dwacc):
        _fill_glu(ca_ref, cg_ref, glu, S)
        dyp[pl.ds(S, CPAD), :] = jnp.zeros((CPAD, DC), F32)
        lgv, lbv = lg_ref[...], lb_ref[...]

        def sum8(v):
            return functools.reduce(jnp.add, [v[r:r + 8] for r in range(0, v.shape[0], 8)])

        P1 = 4 * CT

        def p1(i, carry):
            sb, sg, sl = carry
            t0 = pl.multiple_of(i * P1, P1)
            xhat, rstd, z = _ln_fwd(y_ref[pl.ds(t0, P1), :], lgv, lbv)
            sz = _sigmoid(z)
            dz = dc_ref[pl.ds(t0, P1), :].astype(F32) * (sz * (1.0 + z * (1.0 - sz)))
            dxhat = dz * lgv
            dy = rstd * (dxhat - jnp.mean(dxhat, axis=-1, keepdims=True)
                         - xhat * jnp.mean(dxhat * xhat, axis=-1, keepdims=True))
            dyp[pl.ds(t0, P1), :] = dy
            return sb + sum8(dy), sg + sum8(dz * xhat), sl + sum8(dz)

        z8 = jnp.zeros((8, DC), F32)
        sb, sg, sl = lax.fori_loop(0, S // P1, p1, (z8, z8, z8))
        rs = lambda v: jnp.sum(v, axis=0, keepdims=True)
        ds_ref[...] = jnp.concatenate([rs(sb), rs(sg), rs(sl), jnp.zeros((5, DC), F32)], axis=0)[None]

        def p2(i, c):
            t0 = pl.multiple_of(i * CT, CT)
            win = dyp[pl.ds(t0, 2 * CT), :]
            acc = jnp.zeros((CT, DC), F32)
            taps = _shifted(win, [30 - k for k in range(CK)])
            for k in range(CK):
                acc = acc + taps[30 - k] * w_ref[k:k + 1, :]
            a = ca_ref[pl.ds(t0, CT), :].astype(F32)
            sgt = _sigmoid(cg_ref[pl.ds(t0, CT), :].astype(F32))
            dca_ref[pl.ds(t0, CT), :] = (acc * sgt).astype(BF16)
            dcg_ref[pl.ds(t0, CT), :] = (acc * a * sgt * (1.0 - sgt)).astype(BF16)
            return c

        lax.fori_loop(0, S // CT, p2, 0)

        dwacc[...] = jnp.zeros_like(dwacc)

        def p3(i, c):
            t0 = pl.multiple_of(i * CT, CT)
            win = glu[pl.ds(t0, 2 * CT), :]
            dy = dyp[pl.ds(t0, CT), :]
            for k in range(CK):
                dwacc[k] += sum8(dy * win[k + 2:k + 2 + CT, :])
            return c

        lax.fori_loop(0, S // CT, p3, 0)
        dw_ref[...] = jnp.sum(dwacc[...], axis=1)[None]

    vec = pl.BlockSpec((1, DC), lambda b: (0, 0))
    seq = pl.BlockSpec((S, DC), lambda b: (b, 0))
    return pl.pallas_call(
        body, grid=(B,),
        in_specs=[pl.BlockSpec((S, DC), lambda b: (b, 3)), pl.BlockSpec((S, DC), lambda b: (b, 4)),
                  seq, seq, pl.BlockSpec((CT, DC), lambda b: (0, 0)), vec, vec],
        out_specs=[seq, seq, pl.BlockSpec((1, CT, DC), lambda b: (b, 0, 0)),
                   pl.BlockSpec((1, 8, DC), lambda b: (b, 0, 0))],
        out_shape=[jax.ShapeDtypeStruct((T, DC), BF16)] * 2
                  + [jax.ShapeDtypeStruct((B, CT, DC), F32), jax.ShapeDtypeStruct((B, 8, DC), F32)],
        scratch_shapes=[pltpu.VMEM((CPAD + S, DC), F32), pltpu.VMEM((S + CPAD, DC), F32),
                        pltpu.VMEM((CT, 8, DC), F32)],
        compiler_params=_cp(("parallel",)), name="conv_bwd")(u, u, y, dconv, cw, lg, lb)


def _local_step(x, target, norms, W, B, S, comm=None):
    qg2 = jnp.concatenate([norms["q_norm"], norms["q_norm"]], axis=1)
    kg2 = jnp.concatenate([norms["k_norm"], norms["k_norm"]], axis=1)
    cw = jnp.concatenate([W["conv_w"], jnp.zeros((1, DC), F32)], axis=0)

    W = dict(W)
    for kern, tag in (("gate", "wu1"), ("up", "wd1"), ("down", "win")):
        carry = comm.gathers[tag] if comm else None
        if kern == "gate":
            (n1, g1), got = _ffn_gate(x, norms["ffn1_norm"], W["wg1"], "ffn1_gate", carry=carry)
        elif kern == "up":
            (u1, act1), got = _ffn_up(n1, g1, W["wu1"], "ffn1_up", carry=carry)
        else:
            (h1,), got = _ffn_down(x, act1, W["wd1"], "ffn1_down", carry=carry)
        if comm:
            W.update(comm.gathered(tag, got))
    (u, n2), got = _mix_in(h1, norms["mix_norm"], W["win"], carry=comm.gathers["wout"] if comm else None)
    if comm:
        W.update(comm.gathered("wout", got))
    (attn, lse), got = _attn_fwd(u, qg2, kg2, B, S, carry=comm.gathers["ffn2"] if comm else None)
    if comm:
        W = dict(W, **comm.gathered("ffn2", got))
    conv, y = _conv_fwd(u, cw, norms["conv_b"], norms["conv_ln_g"], norms["conv_ln_b"], B, S)
    h2 = _mix_out(h1, attn, conv, W["wout"])
    (h3, n3, g2, u2, dout, dyb, sq), _ = _ffn_forward(h2, norms["ffn2_norm"], W["wg2"], W["wu2"], W["wd2"], target,
                                                     "ffn2_fwd")
    del h3
    loss = (0.5 / D) * jnp.sum(sq)

    (a2, dg2, du2, dh2, dgn_ffn2), _ = _ffn_bwd_act(dyb, g2, u2, h2, dout, norms["ffn2_norm"],
                                                   W["wg2"], W["wu2"], W["wd2"], "ffn2_bwd_act")
    dwd2, _ = _ffn_bwd_w(a2, dyb, "ffn2_bwd_wd")
    dwg2, _ = _ffn_bwd_w(dg2, n3, "ffn2_bwd_wg")
    dwu2, _ = _ffn_bwd_w(du2, n3, "ffn2_bwd_wu")
    dattn, dconv, dwout = _mix_out_bwd(dh2, attn, conv, W["wout"])
    carry = comm.reduce_start("ffn2", {"wg2": dwg2, "wu2": dwu2, "wd2": dwd2, "wout": dwout}) if comm else None
    (dq, dk, dv, dgn_qk), got = _attn_bwd(u, attn, dattn, lse, qg2, kg2, B, S, carry=carry)
    if comm:
        comm.reduce_done(carry, got)
    dca, dcg, dcw, dcs = _conv_bwd(u, y, dconv, cw, norms["conv_ln_g"], norms["conv_ln_b"], B, S)
    dwin, dh1, dyb1, dgn_mix = _mix_in_bwd((dq, dk, dv, dca, dcg), W["win"], n2, h1, dh2, norms["mix_norm"])
    (a1, dg1, du1, gx, dgn_ffn1), _ = _ffn_bwd_act(dyb1, g1, u1, x, dh1, norms["ffn1_norm"],
                                                  W["wg1"], W["wu1"], W["wd1"], "ffn1_bwd_act")
    carry = comm.reduce_start("win", {"win": dwin}) if comm else None
    dwd1, got = _ffn_bwd_w(a1, dyb1, "ffn1_bwd_wd", carry=carry)
    if comm:
        comm.reduce_done(carry, got)
        carry = comm.reduce_start("wd1", {"wd1": dwd1})
    dwg1, got = _ffn_bwd_w(dg1, n1, "ffn1_bwd_wg", carry=carry)
    if comm:
        comm.reduce_done(carry, got)
        carry = comm.reduce_start("wg1", {"wg1": dwg1})
    dwu1, got = _ffn_bwd_w(du1, n1, "ffn1_bwd_wu", carry=carry)
    if comm:
        comm.reduce_done(carry, got)
        comm.reduce_now("wu1", {"wu1": dwu1})

    qk = jnp.sum(dgn_qk, axis=0)
    cs = jnp.sum(dcs, axis=0)
    small = {
        "ffn1_norm": jnp.sum(dgn_ffn1, axis=0),
        "mix_norm": jnp.sum(dgn_mix, axis=0),
        "q_norm": qk[0:1, 0:HD] + qk[0:1, HD:2 * HD],
        "k_norm": qk[1:2, 0:HD] + qk[1:2, HD:2 * HD],
        "conv_w": jnp.sum(dcw, axis=0)[0:CK],
        "conv_b": cs[0:1],
        "conv_ln_g": cs[1:2],
        "conv_ln_b": cs[2:3],
        "ffn2_norm": jnp.sum(dgn_ffn2, axis=0),
    }
    big = {"wg1": dwg1, "wu1": dwu1, "wd1": dwd1, "win": dwin, "wout": dwout,
           "wg2": dwg2, "wu2": dwu2, "wd2": dwd2}
    return loss, gx, big, small


HBM = pl.BlockSpec(memory_space=pltpu.HBM)
VMEM = pl.BlockSpec(memory_space=pltpu.VMEM)


def _place():
    return lax.axis_index("x"), lax.axis_index("y"), lax.axis_index("c")


class _GatherCarry:
    def __init__(self, shards):
        nt = len(shards)
        self.shards = shards
        self.in_arrays = [s for s, _ in shards]
        self.in_specs = [VMEM] * nt
        self.out_shape = [jax.ShapeDtypeStruct((NDEV * s.shape[0], s.shape[1]), dt) for s, dt in shards]
        self.out_specs = [HBM] * nt
        self.scratch = ([pltpu.VMEM(s.shape, dt) for s, dt in shards]
                        + [pltpu.SemaphoreType.DMA((nt, 7)), pltpu.SemaphoreType.DMA((nt, 7)),
                           pltpu.SemaphoreType.DMA((nt,))])

    def _copies(self, outs, scr):
        nt = len(self.shards)
        stages = scr[:nt]
        send_sems, recv_sems, local_sems = scr[nt:]
        x, y, c = _place()
        me, sibling = (x, y, c), (x, y, 1 - c)
        chips = [(1 - x, y), (x, 1 - y), (1 - x, 1 - y)]

        def rows(t, px, py, pc):
            r = self.shards[t][0].shape[0]
            return outs[t].at[pl.ds((4 * px + 2 * py + pc) * r, r), :]

        def copy(t, k, block, to, src=None):
            return pltpu.make_async_remote_copy(
                src_ref=rows(t, *block) if src is None else src, dst_ref=rows(t, *block),
                send_sem=send_sems.at[t, k], recv_sem=recv_sems.at[t, k],
                device_id=to, device_id_type=MESH)

        mine = [pltpu.make_async_copy(stages[t], rows(t, *me), local_sems.at[t]) for t in range(nt)]
        first = []
        for t in range(nt):
            first.append(copy(t, 0, me, sibling, src=stages[t]))
            first += [copy(t, 1 + j, me, (*chip, c), src=stages[t]) for j, chip in enumerate(chips)]
        return copy, mine, first, me, sibling, chips, c

    def start(self, ins, outs, scr):
        _, mine, first, *_ = self._copies(outs, scr)
        for t, (_, dt) in enumerate(self.shards):
            scr[t][...] = ins[t][...].astype(dt)
        for cp in mine + first:
            cp.start()

    def finish(self, ins, outs, scr):
        copy, mine, first, me, sibling, chips, c = self._copies(outs, scr)
        nt = len(self.shards)
        passed = []
        for j, chip in enumerate(chips):
            for t in range(nt):
                copy(t, 1 + j, (*chip, c), me).wait_recv()
                cp = copy(t, 4 + j, (*chip, c), sibling)
                cp.start()
                passed.append(cp)
        for t in range(nt):
            copy(t, 0, sibling, me).wait_recv()
            for j, chip in enumerate(chips):
                copy(t, 4 + j, (*chip, 1 - c), me).wait_recv()
        for cp in first + passed:
            cp.wait_send()
        for cp in mine:
            cp.wait()


def _run_carry(carry, name):
    def body(*refs):
        n_in, n_out = len(carry.in_arrays), len(carry.out_shape)
        ins, outs, scr = refs[:n_in], refs[n_in:n_in + n_out], refs[n_in + n_out:]
        carry.start(ins, outs, scr)
        carry.finish(ins, outs, scr)

    return pl.pallas_call(
        body, in_specs=carry.in_specs, out_specs=carry.out_specs, out_shape=carry.out_shape,
        scratch_shapes=carry.scratch, compiler_params=pltpu.CompilerParams(vmem_limit_bytes=VMEM_LIMIT),
        name=name)(*carry.in_arrays)


def _sibling_reduce(grads, name):
    nt = len(grads)
    g4 = [g.reshape(4, 2, g.shape[0] // NDEV, g.shape[1]) for g in grads]

    def body(*refs):
        ins, outs = refs[:nt], refs[nt:2 * nt]
        recv, own = refs[2 * nt:3 * nt], refs[3 * nt:4 * nt]
        send_sems, recv_sems, load_sems, store_sems = refs[4 * nt:]
        x, y, c = _place()
        sends = [pltpu.make_async_remote_copy(
            src_ref=ins[t].at[:, 1 - c], dst_ref=recv[t], send_sem=send_sems.at[t], recv_sem=recv_sems.at[t],
            device_id=(x, y, 1 - c), device_id_type=MESH) for t in range(nt)]
        loads = [pltpu.make_async_copy(ins[t].at[:, c], own[t], load_sems.at[t]) for t in range(nt)]
        stores = [pltpu.make_async_copy(own[t], outs[t], store_sems.at[t]) for t in range(nt)]
        for cp in sends + loads:
            cp.start()
        for t in range(nt):
            loads[t].wait()
            sends[t].wait_recv()
            for q in range(4):
                own[t][q] = (own[t][q].astype(F32) + recv[t][q].astype(F32)).astype(BF16)
            stores[t].start()
        for t in range(nt):
            sends[t].wait_send()
            stores[t].wait()

    blocks = [(4,) + g.shape[2:] for g in g4]
    return pl.pallas_call(
        body, in_specs=[HBM] * nt, out_specs=[HBM] * nt,
        out_shape=[jax.ShapeDtypeStruct(b, BF16) for b in blocks],
        scratch_shapes=[pltpu.VMEM(b, BF16) for b in blocks] * 2 + [pltpu.SemaphoreType.DMA((nt,))] * 4,
        compiler_params=pltpu.CompilerParams(vmem_limit_bytes=VMEM_LIMIT), name=name)(*g4)


class _ExchangeCarry:
    def __init__(self, names, parts):
        nt = len(parts)
        self.names = names
        self.in_arrays = list(parts)
        self.in_specs = [HBM] * nt
        self.out_shape = [jax.ShapeDtypeStruct(p.shape, BF16) for p in parts]
        self.out_specs = [HBM] * nt
        self.scratch = [pltpu.SemaphoreType.DMA((nt, 3)), pltpu.SemaphoreType.DMA((nt, 3)),
                        pltpu.SemaphoreType.DMA((nt,))]

    def _copies(self, ins, outs, scr):
        send_sems, recv_sems, local_sems = scr
        x, y, c = _place()
        qme = 2 * x + y
        chips = [(1 - x, y), (x, 1 - y), (1 - x, 1 - y)]
        cps = []
        for t in range(len(ins)):
            cps.append(pltpu.make_async_copy(ins[t].at[qme], outs[t].at[qme], local_sems.at[t]))
            for k, (cx, cy) in enumerate(chips):
                cps.append(pltpu.make_async_remote_copy(
                    src_ref=ins[t].at[2 * cx + cy], dst_ref=outs[t].at[qme],
                    send_sem=send_sems.at[t, k], recv_sem=recv_sems.at[t, k],
                    device_id=(cx, cy, c), device_id_type=MESH))
        return cps

    def start(self, ins, outs, scr):
        for cp in self._copies(ins, outs, scr):
            cp.start()

    def finish(self, ins, outs, scr):
        for cp in self._copies(ins, outs, scr):
            cp.wait()


class _Comm:
    def __init__(self, groups):
        self.names = {tag: list(g) for tag, g in groups.items()}
        self.gathers = {tag: _GatherCarry(list(g.values())) for tag, g in groups.items()}
        self.reduced = {}

    def gathered(self, tag, outs):
        return dict(zip(self.names[tag], outs))

    def reduce_start(self, tag, grads):
        names = list(grads)
        return _ExchangeCarry(names, _sibling_reduce([grads[n] for n in names], "sibling_reduce_" + tag))

    def reduce_done(self, carry, outs):
        self.reduced.update(zip(carry.names, outs))

    def reduce_now(self, tag, grads):
        carry = self.reduce_start(tag, grads)
        self.reduce_done(carry, _run_carry(carry, "chip_exchange_" + tag))


def _adamw_math(w, g, m, v):
    m = B1 * m + (1.0 - B1) * g
    v = B2 * v + (1.0 - B2) * (g * g)
    m_hat = m / (1.0 - B1 ** STEP)
    v_hat = v / (1.0 - B2 ** STEP)
    delta = -LR * (m_hat / (jnp.sqrt(v_hat) + AEPS) + WD * w)
    return delta, m, v


def _adamw_big(recv, w, m, v, name):
    def body(r_ref, w_ref, m_ref, v_ref, g_ref, d_ref, mo_ref, vo_ref):
        g = r_ref[0].astype(F32)
        for q in range(1, 4):
            g = g + r_ref[q].astype(F32)
        d, mn, vn = _adamw_math(w_ref[...], g, m_ref[...], v_ref[...])
        g_ref[...] = g
        d_ref[...] = d
        mo_ref[...] = mn
        vo_ref[...] = vn

    return pl.pallas_call(
        body, out_shape=[jax.ShapeDtypeStruct(w.shape, F32)] * 4,
        compiler_params=pltpu.CompilerParams(vmem_limit_bytes=VMEM_LIMIT), name=name)(recv, w, m, v)


SMALL_NAMES = ("ffn1_norm", "mix_norm", "ffn2_norm", "conv_b", "conv_ln_g", "conv_ln_b", "q_norm", "k_norm")
SROWS = 16
LOSS_ROW = len(SMALL_NAMES)
CWF = 2


def _small_step(gs, loss_row, gcw, ws, ms, vs, wcw, mcw, vcw):
    ns = len(SMALL_NAMES)
    widths = [g.shape[1] for g in gs]

    def body(*refs):
        it = iter(refs)
        take = lambda n: [next(it) for _ in range(n)]
        g_refs, (loss_ref, gcw_ref) = take(ns), take(2)
        w_refs, m_refs, v_refs = take(ns), take(ns), take(ns)
        wcw_ref, mcw_ref, vcw_ref = take(3)
        outs = [take(4) for _ in range(ns)]
        cw_outs, (loss_out,) = take(4), take(1)
        send, slots, cslots, send_sems, recv_sems, csend_sems, crecv_sems = take(7)
        x, y, c = _place()
        me = 4 * x + 2 * y + c
        send[...] = jnp.zeros_like(send)
        for k in range(ns):
            send[k:k + 1, 0:widths[k]] = g_refs[k][...]
        send[LOSS_ROW:LOSS_ROW + 1, 0:128] = loss_ref[...]
        slots[me] = send[...]
        cslots[me] = gcw_ref[...]
        cps = []
        for k in range(1, NDEV):
            peer = (x ^ ((k >> 2) & 1), y ^ ((k >> 1) & 1), c ^ (k & 1))
            cps.append(pltpu.make_async_remote_copy(
                src_ref=send, dst_ref=slots.at[me], send_sem=send_sems.at[k - 1], recv_sem=recv_sems.at[k - 1],
                device_id=peer, device_id_type=MESH))
            cps.append(pltpu.make_async_remote_copy(
                src_ref=gcw_ref, dst_ref=cslots.at[me], send_sem=csend_sems.at[k - 1],
                recv_sem=crecv_sems.at[k - 1], device_id=peer, device_id_type=MESH))
        for cp in cps:
            cp.start()
        for cp in cps:
            cp.wait()
        tot = slots[0]
        ctot = cslots[0, me]
        for j in range(1, NDEV):
            tot = tot + slots[j]
            ctot = ctot + cslots[j, me]

        def step(g, w_ref, m_ref, v_ref, o):
            d, mn, vn = _adamw_math(w_ref[...], g, m_ref[...], v_ref[...])
            o[0][...], o[1][...], o[2][...], o[3][...] = g, d, mn, vn

        for k in range(ns):
            step(tot[k:k + 1, 0:widths[k]], w_refs[k], m_refs[k], v_refs[k], outs[k])
        step(ctot, wcw_ref, mcw_ref, vcw_ref, cw_outs)
        loss_out[...] = tot[LOSS_ROW:LOSS_ROW + 1, 0:128]

    args = [*gs, loss_row, gcw, *ws, *ms, *vs, wcw, mcw, vcw]
    out_shape = ([jax.ShapeDtypeStruct((1, n), F32) for n in widths for _ in range(4)]
                 + [jax.ShapeDtypeStruct((CWF, D), F32)] * 4 + [jax.ShapeDtypeStruct((1, 128), F32)])
    res = pl.pallas_call(
        body, in_specs=[VMEM] * len(args), out_specs=[VMEM] * len(out_shape), out_shape=out_shape,
        scratch_shapes=[pltpu.VMEM((SROWS, D), F32), pltpu.VMEM((NDEV, SROWS, D), F32),
                        pltpu.VMEM((NDEV, NDEV, CWF, D), F32)]
                       + [pltpu.SemaphoreType.DMA((NDEV - 1,))] * 4,
        name="small_step")(*args)
    per = [res[4 * k:4 * k + 4] for k in range(ns)]
    return per, res[4 * ns:4 * ns + 4], res[-1]


def _pack_cw(a):
    flat = a.reshape(a.shape[:-2] + (CK * HD,))
    pad = [(0, 0)] * (flat.ndim - 1) + [(0, CWF * D - CK * HD)]
    return jnp.pad(flat, pad).reshape(a.shape[:-2] + (CWF, D))


def _unpack_cw(v):
    return v.reshape(-1)[:CK * HD].reshape(1, CK, HD)


def kernel(x, ffn1_norm, ffn1_w_gate, ffn1_w_up, ffn1_w_down, mix_norm, w_in, q_norm, k_norm, conv_w, conv_b, conv_ln_g, conv_ln_b, w_out, ffn2_norm, ffn2_w_gate, ffn2_w_up, ffn2_w_down, loss_target, m_ffn1_norm, m_ffn1_w_gate, m_ffn1_w_up, m_ffn1_w_down, m_mix_norm, m_w_in, m_q_norm, m_k_norm, m_conv_w, m_conv_b, m_conv_ln_g, m_conv_ln_b, m_w_out, m_ffn2_norm, m_ffn2_w_gate, m_ffn2_w_up, m_ffn2_w_down, v_ffn1_norm, v_ffn1_w_gate, v_ffn1_w_up, v_ffn1_w_down, v_mix_norm, v_w_in, v_q_norm, v_k_norm, v_conv_w, v_conv_b, v_conv_ln_g, v_conv_ln_b, v_w_out, v_ffn2_norm, v_ffn2_w_gate, v_ffn2_w_up, v_ffn2_w_down):
    P = dict(ffn1_norm=ffn1_norm, ffn1_w_gate=ffn1_w_gate, ffn1_w_up=ffn1_w_up, ffn1_w_down=ffn1_w_down,
             mix_norm=mix_norm, w_in=w_in, q_norm=q_norm, k_norm=k_norm, conv_w=conv_w, conv_b=conv_b,
             conv_ln_g=conv_ln_g, conv_ln_b=conv_ln_b, w_out=w_out, ffn2_norm=ffn2_norm,
             ffn2_w_gate=ffn2_w_gate, ffn2_w_up=ffn2_w_up, ffn2_w_down=ffn2_w_down)
    M = dict(ffn1_norm=m_ffn1_norm, ffn1_w_gate=m_ffn1_w_gate, ffn1_w_up=m_ffn1_w_up, ffn1_w_down=m_ffn1_w_down,
             mix_norm=m_mix_norm, w_in=m_w_in, q_norm=m_q_norm, k_norm=m_k_norm, conv_w=m_conv_w, conv_b=m_conv_b,
             conv_ln_g=m_conv_ln_g, conv_ln_b=m_conv_ln_b, w_out=m_w_out, ffn2_norm=m_ffn2_norm,
             ffn2_w_gate=m_ffn2_w_gate, ffn2_w_up=m_ffn2_w_up, ffn2_w_down=m_ffn2_w_down)
    V = dict(ffn1_norm=v_ffn1_norm, ffn1_w_gate=v_ffn1_w_gate, ffn1_w_up=v_ffn1_w_up, ffn1_w_down=v_ffn1_w_down,
             mix_norm=v_mix_norm, w_in=v_w_in, q_norm=v_q_norm, k_norm=v_k_norm, conv_w=v_conv_w, conv_b=v_conv_b,
             conv_ln_g=v_conv_ln_g, conv_ln_b=v_conv_ln_b, w_out=v_w_out, ffn2_norm=v_ffn2_norm,
             ffn2_w_gate=v_ffn2_w_gate, ffn2_w_up=v_ffn2_w_up, ffn2_w_down=v_ffn2_w_down)
    order = ["ffn1_norm", "ffn1_w_gate", "ffn1_w_up", "ffn1_w_down", "mix_norm", "w_in", "q_norm", "k_norm",
             "conv_w", "conv_b", "conv_ln_g", "conv_ln_b", "w_out", "ffn2_norm", "ffn2_w_gate", "ffn2_w_up",
             "ffn2_w_down"]
    B, S, _ = x.shape
    T = B * S

    bigs = [("wg1", "ffn1_w_gate", True), ("wu1", "ffn1_w_up", True), ("wd1", "ffn1_w_down", False),
            ("win", "w_in", True), ("wout", "w_out", False),
            ("wg2", "ffn2_w_gate", True), ("wu2", "ffn2_w_up", True), ("wd2", "ffn2_w_down", False)]
    hm = lambda a, tr: jnp.transpose(a[0]) if tr else a[0]
    cw_pad = jnp.zeros((32, 128), F32).at[0:CK, 0:HD].set(conv_w[0])
    shard = {ln: (hm(P[pn], tr), BF16) for ln, pn, tr in bigs}
    gathered = _run_carry(_GatherCarry([shard["wg1"], (cw_pad, F32)]), "gather_first")
    W = {"wg1": gathered[0]}
    cwg = gathered[1].reshape(NDEV, 32, 128)[:, 0:CK, 0:HD]
    W["conv_w"] = jnp.transpose(cwg, (1, 0, 2)).reshape(CK, DC)
    norms = {n: P[n] for n in SMALL_NAMES}
    comm = _Comm({"wu1": {"wu1": shard["wu1"]}, "wd1": {"wd1": shard["wd1"]},
                  "win": {"win": shard["win"]}, "wout": {"wout": shard["wout"]},
                  "ffn2": {n: shard[n] for n in ("wg2", "wu2", "wd2")}})

    loss_part, gx, _, small = _local_step(x.reshape(T, D), loss_target.reshape(T, D), norms, W, B, S, comm)

    G, Dl, Mn, Vn = {}, {}, {}, {}
    for ln, pn, tr in bigs:
        outs = _adamw_big(comm.reduced[ln], hm(P[pn], tr), hm(M[pn], tr), hm(V[pn], tr), "adamw_" + ln)
        G[pn], Dl[pn], Mn[pn], Vn[pn] = [(jnp.transpose(o) if tr else o)[None] for o in outs]

    dcw = small["conv_w"].reshape(CK, NDEV, HD).transpose(1, 0, 2)
    loss_row = jnp.zeros((1, 128), F32).at[0, 0].set(loss_part)
    per, cw_outs, loss_out = _small_step(
        [small[n] for n in SMALL_NAMES], loss_row, _pack_cw(dcw),
        [P[n] for n in SMALL_NAMES], [M[n] for n in SMALL_NAMES], [V[n] for n in SMALL_NAMES],
        _pack_cw(P["conv_w"][0]), _pack_cw(M["conv_w"][0]), _pack_cw(V["conv_w"][0]))
    loss = loss_out[0, 0]
    for n, outs in zip(SMALL_NAMES, per):
        G[n], Dl[n], Mn[n], Vn[n] = outs
    G["conv_w"], Dl["conv_w"], Mn["conv_w"], Vn["conv_w"] = [_unpack_cw(o) for o in cw_outs]

    return (loss, gx.reshape(B, S, D), *[G[n] for n in order], *[Dl[n] for n in order],
            *[Mn[n] for n in order], *[Vn[n] for n in order])
```

```python
import functools

import jax
import jax.numpy as jnp
from jax import lax
from jax.experimental import pallas as pl
from jax.experimental.pallas import tpu as pltpu

F32 = jnp.float32
BF16 = jnp.bfloat16

D = 1024
FF = 2816
HD = 64
DA = 512
DC = 512
DIN = 2560
CK = 31
BLK = 128
DILS = (1, 4, 16)
EPS = 1e-6
NDEV = 8
MESH = pl.DeviceIdType.MESH

LR, B1, B2, AEPS, WD, STEP = 0.001, 0.9, 0.999, 1e-08, 0.01, 10

NT = (((1,), (1,)), ((), ()))
TN = (((0,), (0,)), ((), ()))

VMEM_LIMIT = 56 * 1024 * 1024


def _cp(sem=None):
    return pltpu.CompilerParams(dimension_semantics=sem, vmem_limit_bytes=VMEM_LIMIT)


def _sigmoid(x):
    return 0.5 * (jnp.tanh(0.5 * x) + 1.0)


def _pallas(body, args, *, grid, in_specs, out_specs, out_shape, scratch_shapes, sem, name, carry=None):
    if carry is None:
        outs = pl.pallas_call(body, grid=grid, in_specs=in_specs, out_specs=out_specs, out_shape=out_shape,
                              scratch_shapes=scratch_shapes, compiler_params=_cp(sem), name=name)(*args)
        return outs, None
    n_in, n_out, n_scr = len(in_specs), len(out_shape), len(scratch_shapes)
    c_in, c_out = len(carry.in_arrays), len(carry.out_shape)

    def wrapped(*refs):
        ins, refs = refs[:n_in], refs[n_in:]
        cins, refs = refs[:c_in], refs[c_in:]
        outs, refs = refs[:n_out], refs[n_out:]
        couts, refs = refs[:c_out], refs[c_out:]
        scr, cscr = refs[:n_scr], refs[n_scr:]
        ids = [pl.program_id(a) for a in range(len(grid))]
        is_first = functools.reduce(jnp.logical_and, [i == 0 for i in ids])
        is_last = functools.reduce(jnp.logical_and, [i == n - 1 for i, n in zip(ids, grid)])

        @pl.when(is_first)
        def _():
            carry.start(cins, couts, cscr)

        body(*ins, *outs, *scr)

        @pl.when(is_last)
        def _():
            carry.finish(cins, couts, cscr)

    outs = pl.pallas_call(
        wrapped, grid=grid, in_specs=list(in_specs) + carry.in_specs, out_specs=list(out_specs) + carry.out_specs,
        out_shape=list(out_shape) + carry.out_shape, scratch_shapes=list(scratch_shapes) + carry.scratch,
        compiler_params=_cp(("arbitrary",) * len(grid)), name=name)(*args, *carry.in_arrays)
    return outs[:n_out], outs[n_out:]


def _ffn_fwd(x, gain, wg, wu, wd, target, name, carry=None):
    T = x.shape[0]
    tm, tf = 1024, 256
    nt, nf = T // tm, FF // tf
    with_loss = target is not None

    def body(*refs):
        if with_loss:
            (x_ref, gain_ref, wg_ref, wu_ref, wd_ref, t_ref,
             h_ref, n_ref, g_ref, u_ref, dout_ref, dyb_ref, sq_ref, nb_scr, acc_scr) = refs
        else:
            (x_ref, gain_ref, wg_ref, wu_ref, wd_ref,
             h_ref, n_ref, g_ref, u_ref, nb_scr, acc_scr) = refs
        f = pl.program_id(1)

        @pl.when(f == 0)
        def _():
            xv = x_ref[...]
            r = lax.rsqrt(jnp.mean(xv * xv, axis=-1, keepdims=True) + EPS)
            nb = (xv * r * gain_ref[...]).astype(BF16)
            nb_scr[...] = nb
            n_ref[...] = nb
            acc_scr[...] = jnp.zeros_like(acc_scr)

        nb = nb_scr[...]
        g = lax.dot_general(nb, wg_ref[...], NT, preferred_element_type=F32)
        u = lax.dot_general(nb, wu_ref[...], NT, preferred_element_type=F32)
        a = g * _sigmoid(g) * u
        g_ref[...] = g.astype(BF16)
        u_ref[...] = u.astype(BF16)
        acc_scr[...] += jnp.dot(a.astype(BF16), wd_ref[...], preferred_element_type=F32)

        @pl.when(f == nf - 1)
        def _():
            h = x_ref[...] + 0.5 * acc_scr[...]
            h_ref[...] = h
            if with_loss:
                e = h - t_ref[...]
                dout = e * (1.0 / D)
                dout_ref[...] = dout
                dyb_ref[...] = (0.5 * dout).astype(BF16)
                sq_ref[...] = jnp.sum(e * e, axis=0, keepdims=True)[None]

    row = pl.BlockSpec((tm, D), lambda t, f: (t, 0))
    wspec = pl.BlockSpec((tf, D), lambda t, f: (f, 0))
    gspec = pl.BlockSpec((tm, tf), lambda t, f: (t, f))
    in_specs = [row, pl.BlockSpec((1, D), lambda t, f: (0, 0)), wspec, wspec, wspec]
    out_shape = [jax.ShapeDtypeStruct((T, D), F32), jax.ShapeDtypeStruct((T, D), BF16),
                 jax.ShapeDtypeStruct((T, FF), BF16), jax.ShapeDtypeStruct((T, FF), BF16)]
    out_specs = [row, row, gspec, gspec]
    args = [x, gain, wg, wu, wd]
    if with_loss:
        in_specs.append(row)
        args.append(target)
        out_shape += [jax.ShapeDtypeStruct((T, D), F32), jax.ShapeDtypeStruct((T, D), BF16),
                      jax.ShapeDtypeStruct((nt, 1, D), F32)]
        out_specs += [row, row, pl.BlockSpec((1, 1, D), lambda t, f: (t, 0, 0))]
    return _pallas(
        body, args, grid=(nt, nf), in_specs=in_specs, out_specs=out_specs, out_shape=out_shape,
        scratch_shapes=[pltpu.VMEM((tm, D), BF16), pltpu.VMEM((tm, D), F32)],
        sem=("parallel", "arbitrary"), name=name, carry=carry)


def _ffn_bwd(dyb, nb, g, u, wg, wu, wd, name, carry=None):
    T = dyb.shape[0]
    tm, tf = 512, 256
    nt, nf = T // tm, FF // tf

    def body(dy_ref, n_ref, g_ref, u_ref, wg_ref, wu_ref, wd_ref,
             dwg_ref, dwu_ref, dwd_ref, dn_ref, ag_scr, au_scr, ad_scr):
        f, t = pl.program_id(0), pl.program_id(1)

        @pl.when(t == 0)
        def _():
            ag_scr[...] = jnp.zeros_like(ag_scr)
            au_scr[...] = jnp.zeros_like(au_scr)
            ad_scr[...] = jnp.zeros_like(ad_scr)

        dy = dy_ref[...]
        n = n_ref[...]
        gv = g_ref[...].astype(F32)
        uv = u_ref[...].astype(F32)
        da = lax.dot_general(dy, wd_ref[...], NT, preferred_element_type=F32)
        sg = _sigmoid(gv)
        silu = gv * sg
        ab = (silu * uv).astype(BF16)
        dgb = (da * uv * (sg * (1.0 + gv * (1.0 - sg)))).astype(BF16)
        dub = (da * silu).astype(BF16)
        ad_scr[...] += lax.dot_general(ab, dy, TN, preferred_element_type=F32)
        ag_scr[...] += lax.dot_general(dgb, n, TN, preferred_element_type=F32)
        au_scr[...] += lax.dot_general(dub, n, TN, preferred_element_type=F32)
        dn = (jnp.dot(dgb, wg_ref[...], preferred_element_type=F32)
              + jnp.dot(dub, wu_ref[...], preferred_element_type=F32))
        rows = pl.ds(pl.multiple_of(t * tm, tm), tm)

        @pl.when(f == 0)
        def _():
            dn_ref[rows, :] = dn

        @pl.when(f > 0)
        def _():
            dn_ref[rows, :] += dn

        @pl.when(t == nt - 1)
        def _():
            dwg_ref[...] = ag_scr[...].astype(BF16)
            dwu_ref[...] = au_scr[...].astype(BF16)
            dwd_ref[...] = ad_scr[...].astype(BF16)

    row = pl.BlockSpec((tm, D), lambda f, t: (t, 0))
    gspec = pl.BlockSpec((tm, tf), lambda f, t: (t, f))
    wspec = pl.BlockSpec((tf, D), lambda f, t: (f, 0))
    return _pallas(
        body, (dyb, nb, g, u, wg, wu, wd), grid=(nf, nt),
        in_specs=[row, row, gspec, gspec, wspec, wspec, wspec],
        out_specs=[wspec, wspec, wspec, pl.BlockSpec((T, D), lambda f, t: (0, 0))],
        out_shape=[jax.ShapeDtypeStruct((FF, D), BF16)] * 3 + [jax.ShapeDtypeStruct((T, D), F32)],
        scratch_shapes=[pltpu.VMEM((tf, D), F32)] * 3,
        sem=("arbitrary", "arbitrary"), name=name, carry=carry)


FC = 256


def _resident(shape):
    return pl.BlockSpec(shape, lambda *_: (0,) * len(shape), pipeline_mode=pl.Buffered(1))


def _mix_out_ffn_loss(h1, attn, conv, wout, gain, wg, wu, wd, target, name):
    T = h1.shape[0]
    tm = 512
    nt = T // tm

    def body(h1_ref, at_ref, cv_ref, wo_ref, gain_ref, wg_ref, wu_ref, wd_ref, t_ref,
             h2_ref, n_ref, g_ref, u_ref, dout_ref, dyb_ref, sq_ref, a_scr):
        xv = (h1_ref[...]
              + jnp.dot(at_ref[...], wo_ref[0:DA, :], preferred_element_type=F32)
              + jnp.dot(cv_ref[...], wo_ref[DA:D, :], preferred_element_type=F32))
        h2_ref[...] = xv
        r = lax.rsqrt(jnp.mean(xv * xv, axis=-1, keepdims=True) + EPS)
        n_ref[...] = (xv * r * gain_ref[...]).astype(BF16)
        for c in range(FF // FC):
            cols = slice(c * FC, (c + 1) * FC)
            nb = n_ref[...]
            g = lax.dot_general(nb, wg_ref[cols, :], NT, preferred_element_type=F32)
            u = lax.dot_general(nb, wu_ref[cols, :], NT, preferred_element_type=F32)
            g_ref[:, cols] = g.astype(BF16)
            u_ref[:, cols] = u.astype(BF16)
            a_scr[:, cols] = (g * _sigmoid(g) * u).astype(BF16)
        e = h2_ref[...] + 0.5 * jnp.dot(a_scr[...], wd_ref[...], preferred_element_type=F32) - t_ref[...]
        dout = e * (1.0 / D)
        dout_ref[...] = dout
        dyb_ref[...] = (0.5 * dout).astype(BF16)
        sq_ref[...] = jnp.sum(e * e, axis=0, keepdims=True)[None]

    row = pl.BlockSpec((tm, D), lambda t: (t, 0))
    half = pl.BlockSpec((tm, DA), lambda t: (t, 0))
    wide = pl.BlockSpec((tm, FF), lambda t: (t, 0))
    outs, _ = _pallas(
        body, (h1, attn, conv, wout, gain, wg, wu, wd, target), grid=(nt,),
        in_specs=[row, half, half, _resident((D, D)), _resident((1, D)), _resident((FF, D)), _resident((FF, D)),
                  _resident((FF, D)), row],
        out_specs=[row, row, wide, wide, row, row, pl.BlockSpec((1, 1, D), lambda t: (t, 0, 0))],
        out_shape=[jax.ShapeDtypeStruct((T, D), F32), jax.ShapeDtypeStruct((T, D), BF16)]
                  + [jax.ShapeDtypeStruct((T, FF), BF16)] * 2
                  + [jax.ShapeDtypeStruct((T, D), F32), jax.ShapeDtypeStruct((T, D), BF16),
                     jax.ShapeDtypeStruct((nt, 1, D), F32)],
        scratch_shapes=[pltpu.VMEM((tm, FF), BF16)], sem=("parallel",), name=name)
    return outs


def _ffn_gate(x, gain, wg, name, carry=None):
    T = x.shape[0]
    tm = 512

    def body(x_ref, gain_ref, wg_ref, n_ref, g_ref):
        xv = x_ref[...]
        r = lax.rsqrt(jnp.mean(xv * xv, axis=-1, keepdims=True) + EPS)
        n_ref[...] = (xv * r * gain_ref[...]).astype(BF16)
        for c in range(FF // FC):
            cols = slice(c * FC, (c + 1) * FC)
            g_ref[:, cols] = lax.dot_general(n_ref[...], wg_ref[cols, :], NT,
                                             preferred_element_type=F32).astype(BF16)

    row = pl.BlockSpec((tm, D), lambda t: (t, 0))
    wide = pl.BlockSpec((tm, FF), lambda t: (t, 0))
    return _pallas(
        body, (x, gain, wg), grid=(T // tm,), in_specs=[row, _resident((1, D)), _resident((FF, D))],
        out_specs=[row, wide], out_shape=[jax.ShapeDtypeStruct((T, D), BF16), jax.ShapeDtypeStruct((T, FF), BF16)],
        scratch_shapes=[], sem=("parallel",), name=name, carry=carry)


def _ffn_up(nb, g, wu, name, carry=None):
    T = nb.shape[0]
    tm = 512

    def body(n_ref, g_ref, wu_ref, u_ref, a_ref):
        for c in range(FF // FC):
            cols = slice(c * FC, (c + 1) * FC)
            u = lax.dot_general(n_ref[...], wu_ref[cols, :], NT, preferred_element_type=F32)
            gv = g_ref[:, cols].astype(F32)
            u_ref[:, cols] = u.astype(BF16)
            a_ref[:, cols] = (gv * _sigmoid(gv) * u).astype(BF16)

    row = pl.BlockSpec((tm, D), lambda t: (t, 0))
    wide = pl.BlockSpec((tm, FF), lambda t: (t, 0))
    return _pallas(
        body, (nb, g, wu), grid=(T // tm,), in_specs=[row, wide, _resident((FF, D))],
        out_specs=[wide, wide], out_shape=[jax.ShapeDtypeStruct((T, FF), BF16)] * 2,
        scratch_shapes=[], sem=("parallel",), name=name, carry=carry)


def _ffn_down(x, a, wd, name, carry=None):
    T = x.shape[0]
    tm = 512

    def body(x_ref, a_ref, wd_ref, h_ref):
        h_ref[...] = x_ref[...] + 0.5 * jnp.dot(a_ref[...], wd_ref[...], preferred_element_type=F32)

    row = pl.BlockSpec((tm, D), lambda t: (t, 0))
    wide = pl.BlockSpec((tm, FF), lambda t: (t, 0))
    return _pallas(
        body, (x, a, wd), grid=(T // tm,), in_specs=[row, wide, _resident((FF, D))],
        out_specs=[row], out_shape=[jax.ShapeDtypeStruct((T, D), F32)],
        scratch_shapes=[], sem=("parallel",), name=name, carry=carry)


def _ffn_bwd_act(dyb, g, u, x, dout, gain, wg, wu, wd, name, carry=None):
    T = x.shape[0]
    tm = 256
    nt = T // tm

    def body(dy_ref, g_ref, u_ref, x_ref, dout_ref, gain_ref, wg_ref, wu_ref, wd_ref,
             dg_ref, du_ref, dx_ref, dgn_ref):
        for c in range(FF // FC):
            cols = slice(c * FC, (c + 1) * FC)
            da = lax.dot_general(dy_ref[...], wd_ref[cols, :], NT, preferred_element_type=F32)
            gv = g_ref[:, cols].astype(F32)
            uv = u_ref[:, cols].astype(F32)
            sg = _sigmoid(gv)
            dg_ref[:, cols] = (da * uv * (sg * (1.0 + gv * (1.0 - sg)))).astype(BF16)
            du_ref[:, cols] = (da * (gv * sg)).astype(BF16)
        dn = (jnp.dot(dg_ref[...], wg_ref[...], preferred_element_type=F32)
              + jnp.dot(du_ref[...], wu_ref[...], preferred_element_type=F32))
        dx, dgain = _rms_bwd_rows(dn, x_ref[...], gain_ref[...])
        dx_ref[...] = dout_ref[...] + dx
        dgn_ref[...] = dgain[None]

    row = pl.BlockSpec((tm, D), lambda t: (t, 0))
    wide = pl.BlockSpec((tm, FF), lambda t: (t, 0))
    return _pallas(
        body, (dyb, g, u, x, dout, gain, wg, wu, wd), grid=(nt,),
        in_specs=[row, wide, wide, row, row, _resident((1, D)), _resident((FF, D)), _resident((FF, D)),
                  _resident((FF, D))],
        out_specs=[wide, wide, row, pl.BlockSpec((1, 1, D), lambda t: (t, 0, 0))],
        out_shape=[jax.ShapeDtypeStruct((T, FF), BF16)] * 2
                  + [jax.ShapeDtypeStruct((T, D), F32), jax.ShapeDtypeStruct((nt, 1, D), F32)],
        scratch_shapes=[], sem=("parallel",), name=name, carry=carry)


def _ffn_bwd_w(lhs, rhs, name, carry=None):
    T = rhs.shape[0]
    tf = 256
    lhs = lhs if isinstance(lhs, tuple) else (lhs,)

    def body(*refs):
        r_ref, dw_ref = refs[-2:]
        if len(lhs) == 2:
            gv = refs[0][...].astype(F32)
            lv = (gv * _sigmoid(gv) * refs[1][...].astype(F32)).astype(BF16)
        else:
            lv = refs[0][...]
        dw_ref[...] = lax.dot_general(lv, r_ref[...], TN, preferred_element_type=F32).astype(BF16)

    (dw,), got = _pallas(
        body, (*lhs, rhs), grid=(FF // tf,),
        in_specs=[pl.BlockSpec((T, tf), lambda f: (0, f))] * len(lhs) + [_resident((T, D))],
        out_specs=[pl.BlockSpec((tf, D), lambda f: (f, 0))], out_shape=[jax.ShapeDtypeStruct((FF, D), BF16)],
        scratch_shapes=[], sem=("parallel",), name=name, carry=carry)
    return dw, got


def _rms_bwd_rows(dn, xv, gain):
    r = lax.rsqrt(jnp.mean(xv * xv, axis=-1, keepdims=True) + EPS)
    xhat = xv * r
    dxhat = dn * gain
    dx = r * (dxhat - xhat * jnp.mean(dxhat * xhat, axis=-1, keepdims=True))
    return dx, jnp.sum(dn * xhat, axis=0, keepdims=True)


def _norm_bwd(dn, x, dout, gain, name):
    T = x.shape[0]
    tm = 512
    nt = T // tm

    def body(dn_ref, x_ref, dout_ref, gain_ref, dx_ref, dg_ref):
        dx, dgain = _rms_bwd_rows(dn_ref[...], x_ref[...], gain_ref[...])
        dx_ref[...] = dout_ref[...] + dx
        dg_ref[...] = dgain[None]

    row = pl.BlockSpec((tm, D), lambda t: (t, 0))
    return pl.pallas_call(
        body, grid=(nt,), in_specs=[row, row, row, pl.BlockSpec((1, D), lambda t: (0, 0))],
        out_specs=[row, pl.BlockSpec((1, 1, D), lambda t: (t, 0, 0))],
        out_shape=[jax.ShapeDtypeStruct((T, D), F32), jax.ShapeDtypeStruct((nt, 1, D), F32)],
        compiler_params=_cp(("parallel",)), name=name)(dn, x, dout, gain)


def _mix_in(h, gain, win, carry=None):
    T = h.shape[0]
    tm = 512

    def body(h_ref, gain_ref, w_ref, u_ref, n_ref):
        xv = h_ref[...]
        r = lax.rsqrt(jnp.mean(xv * xv, axis=-1, keepdims=True) + EPS)
        nb = (xv * r * gain_ref[...]).astype(BF16)
        n_ref[...] = nb
        u_ref[...] = lax.dot_general(nb, w_ref[...], NT, preferred_element_type=F32).astype(BF16)

    row = pl.BlockSpec((tm, D), lambda t: (t, 0))
    return _pallas(
        body, (h, gain, win), grid=(T // tm,),
        in_specs=[row, _resident((1, D)), _resident((DIN, D))],
        out_specs=[pl.BlockSpec((tm, DIN), lambda t: (t, 0)), row],
        out_shape=[jax.ShapeDtypeStruct((T, DIN), BF16), jax.ShapeDtypeStruct((T, D), BF16)],
        scratch_shapes=[], sem=("parallel",), name="mix_in", carry=carry)


def _mix_out(h, attn, conv, wout):
    T = h.shape[0]
    tm = 512

    def body(h_ref, a_ref, c_ref, w_ref, o_ref):
        o_ref[...] = (h_ref[...]
                      + jnp.dot(a_ref[...], w_ref[0:DA, :], preferred_element_type=F32)
                      + jnp.dot(c_ref[...], w_ref[DA:D, :], preferred_element_type=F32))

    row = pl.BlockSpec((tm, D), lambda t: (t, 0))
    half = pl.BlockSpec((tm, DA), lambda t: (t, 0))
    return pl.pallas_call(
        body, grid=(T // tm,),
        in_specs=[row, half, half, pl.BlockSpec((D, D), lambda t: (0, 0))],
        out_specs=row, out_shape=jax.ShapeDtypeStruct((T, D), F32),
        compiler_params=_cp(("parallel",)), name="mix_out")(h, attn, conv, wout)


def _mix_out_bwd(dh, attn, conv, wout):
    T = dh.shape[0]
    tm = 512
    nt = T // tm

    def body(dh_ref, a_ref, c_ref, w_ref, da_ref, dc_ref, dw_ref, acc_scr):
        t = pl.program_id(0)

        @pl.when(t == 0)
        def _():
            acc_scr[...] = jnp.zeros_like(acc_scr)

        dhb = dh_ref[...].astype(BF16)
        dmix = lax.dot_general(dhb, w_ref[...], NT, preferred_element_type=F32)
        da_ref[...] = dmix[:, 0:DA].astype(BF16)
        dc_ref[...] = dmix[:, DA:D].astype(BF16)
        acc_scr[0:DA, :] += lax.dot_general(a_ref[...], dhb, TN, preferred_element_type=F32)
        acc_scr[DA:D, :] += lax.dot_general(c_ref[...], dhb, TN, preferred_element_type=F32)

        @pl.when(t == nt - 1)
        def _():
            dw_ref[...] = acc_scr[...].astype(BF16)

    row = pl.BlockSpec((tm, D), lambda t: (t, 0))
    half = pl.BlockSpec((tm, DA), lambda t: (t, 0))
    full = pl.BlockSpec((D, D), lambda t: (0, 0))
    return pl.pallas_call(
        body, grid=(nt,), in_specs=[row, half, half, full], out_specs=[half, half, full],
        out_shape=[jax.ShapeDtypeStruct((T, DA), BF16)] * 2 + [jax.ShapeDtypeStruct((D, D), BF16)],
        scratch_shapes=[pltpu.VMEM((D, D), F32)],
        compiler_params=_cp(("arbitrary",)), name="mix_out_bwd")(dh, attn, conv, wout)


def _mix_in_bwd(dparts, win, nb, h, dh, gain):
    T = h.shape[0]
    tm = 512
    nt = T // tm

    def body(d0, d1, d2, d3, d4, w_ref, n_ref, h_ref, dh_ref, gain_ref,
             dw_ref, dx_ref, dyb_ref, dg_ref, acc_scr):
        t = pl.program_id(0)

        @pl.when(t == 0)
        def _():
            acc_scr[...] = jnp.zeros_like(acc_scr)

        n = n_ref[...]
        dn = jnp.zeros((tm, D), F32)
        for i, d_ref in enumerate((d0, d1, d2, d3, d4)):
            dv = d_ref[...]
            dn = dn + jnp.dot(dv, w_ref[i * DA:(i + 1) * DA, :], preferred_element_type=F32)
            acc_scr[i * DA:(i + 1) * DA, :] += lax.dot_general(dv, n, TN, preferred_element_type=F32)
        dx, dgain = _rms_bwd_rows(dn, h_ref[...], gain_ref[...])
        tot = dh_ref[...] + dx
        dx_ref[...] = tot
        dyb_ref[...] = (0.5 * tot).astype(BF16)
        dg_ref[...] = dgain[None]

        @pl.when(t == nt - 1)
        def _():
            dw_ref[...] = acc_scr[...].astype(BF16)

    row = pl.BlockSpec((tm, D), lambda t: (t, 0))
    half = pl.BlockSpec((tm, DA), lambda t: (t, 0))
    full = pl.BlockSpec((DIN, D), lambda t: (0, 0))
    return pl.pallas_call(
        body, grid=(nt,),
        in_specs=[half] * 5 + [full, row, row, row, pl.BlockSpec((1, D), lambda t: (0, 0))],
        out_specs=[full, row, row, pl.BlockSpec((1, 1, D), lambda t: (t, 0, 0))],
        out_shape=[jax.ShapeDtypeStruct((DIN, D), BF16), jax.ShapeDtypeStruct((T, D), F32),
                   jax.ShapeDtypeStruct((T, D), BF16), jax.ShapeDtypeStruct((nt, 1, D), F32)],
        scratch_shapes=[pltpu.VMEM((DIN, D), F32)],
        compiler_params=_cp(("arbitrary",)), name="mix_in_bwd")(*dparts, win, nb, h, dh, gain)


def _head_masks():
    lane = lax.broadcasted_iota(jnp.int32, (1, 2 * HD), 1)
    m0 = lane < HD
    return m0, jnp.logical_not(m0)


def _stack_heads(v, m0, m1):
    z = jnp.zeros_like(v)
    return jnp.concatenate([jnp.where(m0, v, z), jnp.where(m1, v, z)], axis=0)


def _unstack_heads(v2, m0):
    return jnp.where(m0, v2[0:BLK], v2[BLK:2 * BLK])


def _head_sums(xv):
    ri = lax.broadcasted_iota(jnp.int32, (2 * HD, 2 * HD), 0)
    ci = lax.broadcasted_iota(jnp.int32, (2 * HD, 2 * HD), 1)
    ones = jnp.where((ri < HD) == (ci < HD), 1.0, 0.0).astype(BF16)
    hi = xv.astype(BF16)
    lo = (xv - hi.astype(F32)).astype(BF16)
    return (jnp.dot(hi, ones, preferred_element_type=F32) + jnp.dot(lo, ones, preferred_element_type=F32))


def _head_rms(xv):
    return lax.rsqrt(_head_sums(xv * xv) * (1.0 / HD) + EPS)


def _band_mask(first):
    qi = lax.broadcasted_iota(jnp.int32, (BLK, 2 * BLK), 0)
    ci = lax.broadcasted_iota(jnp.int32, (BLK, 2 * BLK), 1)
    band = (ci >= qi) & (ci <= qi + BLK)
    return band & ((ci >= BLK) | jnp.logical_not(first))


def _block_rows(j, d, seg):
    r, n = j // seg, j % seg
    start = r + (d * BLK) * n
    first = n == 0
    prev = jnp.where(first, start, start - d * BLK)
    return pl.ds(start, BLK, stride=d), pl.ds(prev, BLK, stride=d), first


def _block_keys(refs, cur, prev, first, single):
    if single:
        qi = lax.broadcasted_iota(jnp.int32, (BLK, BLK), 0)
        ci = lax.broadcasted_iota(jnp.int32, (BLK, BLK), 1)
        return [r[cur, :].astype(BF16) for r in refs], ci <= qi
    return ([jnp.concatenate([r[prev, :], r[cur, :]], axis=0).astype(BF16) for r in refs], _band_mask(first))


def _attn_fwd(u, qg2, kg2, B, S, carry=None):
    T = B * S
    NB = S // BLK
    scale = HD ** -0.5

    def body(q_ref, k_ref, v_ref, qg_ref, kg_ref, o_ref, lse_ref, qn, kn, vn, os_, ls_):
        m0, m1 = _head_masks()
        qv = q_ref[...].astype(F32)
        qn[...] = qv * _head_rms(qv) * (qg_ref[...] * scale)
        kv = k_ref[...].astype(F32)
        kn[...] = kv * _head_rms(kv) * kg_ref[...]
        vn[...] = v_ref[...].astype(F32)

        for i, d in enumerate(DILS):
            seg = NB // d

            def blk(j, c, i=i, d=d, seg=seg):
                cur, prev, first = _block_rows(j, d, seg)
                q2 = _stack_heads(qn[cur, :].astype(BF16), m0, m1)
                (kk, vv), mask = _block_keys((kn, vn), cur, prev, first, False)
                s = lax.dot_general(q2, kk, NT, preferred_element_type=F32)
                s = jnp.where(jnp.concatenate([mask, mask], axis=0), s, -1e30)
                mx = jnp.max(s, axis=-1, keepdims=True)
                p = jnp.exp(s - mx)
                l = jnp.sum(p, axis=-1, keepdims=True)
                o2 = jnp.dot((p * (1.0 / l)).astype(BF16), vv, preferred_element_type=F32)
                os_[i, cur, :] = _unstack_heads(o2, m0)
                ls_[i, cur, :] = _unstack_heads(mx + jnp.log(l), m0)
                return c

            lax.fori_loop(0, NB, blk, 0, unroll=8)

        def comb(c, carry):
            rows = pl.ds(pl.multiple_of(c * 256, 256), 256)
            l0, l1, l2 = ls_[0, rows, :], ls_[1, rows, :], ls_[2, rows, :]
            mx = jnp.maximum(jnp.maximum(l0, l1), l2)
            e0, e1, e2 = jnp.exp(l0 - mx), jnp.exp(l1 - mx), jnp.exp(l2 - mx)
            tot = e0 + e1 + e2
            inv = 1.0 / tot
            o = (e0 * os_[0, rows, :] + e1 * os_[1, rows, :] + e2 * os_[2, rows, :]) * inv
            o_ref[rows, :] = o.astype(BF16)
            lse_ref[rows, :] = mx + jnp.log(tot)
            return carry

        lax.fori_loop(0, S // 256, comb, 0)

    pair = 2 * HD
    blk_spec = lambda off: pl.BlockSpec((S, pair), lambda b, p, off=off: (b, off + p))
    gspec = pl.BlockSpec((1, pair), lambda b, p: (0, 0))
    return _pallas(
        body, (u, u, u, qg2, kg2), grid=(B, DA // pair),
        in_specs=[blk_spec(0), blk_spec(DA // pair), blk_spec(2 * DA // pair), gspec, gspec],
        out_specs=[blk_spec(0), blk_spec(0)],
        out_shape=[jax.ShapeDtypeStruct((T, DA), BF16), jax.ShapeDtypeStruct((T, DA), F32)],
        scratch_shapes=[pltpu.VMEM((S, pair), F32)] * 3 + [pltpu.VMEM((3, S, pair), F32)] * 2,
        sem=("parallel", "parallel"), name="attn_fwd", carry=carry)


def _attn_bwd(u, attn, dattn, lse, qg2, kg2, B, S, carry=None):
    T = B * S
    NB = S // BLK
    scale = HD ** -0.5
    pair = 2 * HD

    def body(q_ref, k_ref, v_ref, o_ref, do_ref, lse_ref, qg_ref, kg_ref,
             dq_ref, dk_ref, dv_ref, dgn_ref,
             qn, kn, vn, don, ldl, accq, acck, accv):
        m0, m1 = _head_masks()
        lane = lax.broadcasted_iota(jnp.int32, (1, pair), 1)
        qv = q_ref[...].astype(F32)
        qn[...] = qv * _head_rms(qv) * (qg_ref[...] * scale)
        kv = k_ref[...].astype(F32)
        kn[...] = kv * _head_rms(kv) * kg_ref[...]
        vn[...] = v_ref[...].astype(F32)
        dov = do_ref[...].astype(F32)
        don[...] = dov
        ldl[...] = jnp.where((lane % HD) < HD // 2, lse_ref[...], _head_sums(dov * o_ref[...].astype(F32)))
        accq[...] = jnp.zeros_like(accq)
        acck[...] = jnp.zeros_like(acck)
        accv[...] = jnp.zeros_like(accv)

        for i, d in enumerate(DILS):
            seg = NB // d

            def blk(j, c, d=d, seg=seg):
                cur, prev, first = _block_rows(j, d, seg)
                q2 = _stack_heads(qn[cur, :].astype(BF16), m0, m1)
                do2 = _stack_heads(don[cur, :].astype(BF16), m0, m1)
                (kk, vv), mask = _block_keys((kn, vn), cur, prev, first, seg == 1)
                ldv = ldl[cur, :]
                lse2 = jnp.concatenate([ldv[:, 0:1], ldv[:, HD:HD + 1]], axis=0)
                dl2 = jnp.concatenate([ldv[:, HD // 2:HD // 2 + 1], ldv[:, HD + HD // 2:HD + HD // 2 + 1]], axis=0)
                s = lax.dot_general(q2, kk, NT, preferred_element_type=F32)
                p = jnp.where(jnp.concatenate([mask, mask], axis=0), jnp.exp(s - lse2), 0.0)
                dp = lax.dot_general(do2, vv, NT, preferred_element_type=F32)
                ds = (p * (dp - dl2)).astype(BF16)
                dq_acc = _unstack_heads(jnp.dot(ds, kk, preferred_element_type=F32), m0)
                dk_acc = lax.dot_general(ds, q2, TN, preferred_element_type=F32)
                dv_acc = lax.dot_general(p.astype(BF16), do2, TN, preferred_element_type=F32)
                accq[cur, :] += dq_acc
                if seg == 1:
                    acck[cur, :] += dk_acc
                    accv[cur, :] += dv_acc
                else:
                    acck[prev, :] += dk_acc[0:BLK]
                    acck[cur, :] += dk_acc[BLK:2 * BLK]
                    accv[prev, :] += dv_acc[0:BLK]
                    accv[cur, :] += dv_acc[BLK:2 * BLK]
                return c

            lax.fori_loop(0, NB, blk, 0, unroll=4)

        def norm_bwd(x_ref, dn, gain):
            xv = x_ref[...].astype(F32)
            r = _head_rms(xv)
            xhat = xv * r
            dxhat = dn * gain
            dx = r * (dxhat - xhat * (_head_sums(dxhat * xhat) * (1.0 / HD)))
            return dx, jnp.sum(dn * xhat, axis=0, keepdims=True)

        dq, dgq = norm_bwd(q_ref, accq[...], qg_ref[...] * scale)
        dk, dgk = norm_bwd(k_ref, acck[...], kg_ref[...])
        dq_ref[...] = dq.astype(BF16)
        dk_ref[...] = dk.astype(BF16)
        dv_ref[...] = accv[...].astype(BF16)
        dgn_ref[...] = jnp.concatenate([dgq * scale, dgk, jnp.zeros((6, pair), F32)], axis=0)[None]

    blk_spec = lambda off: pl.BlockSpec((S, pair), lambda b, p, off=off: (b, off + p))
    gspec = pl.BlockSpec((1, pair), lambda b, p: (0, 0))
    np_ = DA // pair
    return _pallas(
        body, (u, u, u, attn, dattn, lse, qg2, kg2), grid=(B, np_),
        in_specs=[blk_spec(0), blk_spec(np_), blk_spec(2 * np_), blk_spec(0), blk_spec(0), blk_spec(0),
                  gspec, gspec],
        out_specs=[blk_spec(0), blk_spec(0), blk_spec(0),
                   pl.BlockSpec((1, 8, pair), lambda b, p: (b * np_ + p, 0, 0))],
        out_shape=[jax.ShapeDtypeStruct((T, DA), BF16)] * 3 + [jax.ShapeDtypeStruct((B * np_, 8, pair), F32)],
        scratch_shapes=[pltpu.VMEM((S, pair), F32)] * 8,
        sem=("parallel", "parallel"), name="attn_bwd", carry=carry)


CT = 32
CPAD = 32


def _shifted(win, offsets):
    rolled, out = {}, {}
    n = win.shape[0]
    for o in offsets:
        sub = o % 8
        if sub not in rolled:
            rolled[sub] = win if sub == 0 else pltpu.roll(win, n - sub, 0)
        out[o] = rolled[sub][o - sub:o - sub + CT, :]
    return out


def _ln_fwd(y, g, b):
    mu = jnp.mean(y, axis=-1, keepdims=True)
    yc = y - mu
    rstd = lax.rsqrt(jnp.mean(yc * yc, axis=-1, keepdims=True) + EPS)
    xhat = yc * rstd
    return xhat, rstd, xhat * g + b


def _fill_glu(ca_ref, cg_ref, glu, S):
    glu[pl.ds(0, CPAD), :] = jnp.zeros((CPAD, DC), F32)

    def fill(i, c):
        rows = pl.ds(pl.multiple_of(i * 256, 256), 256)
        a = ca_ref[rows, :].astype(F32)
        gt = cg_ref[rows, :].astype(F32)
        glu[pl.ds(pl.multiple_of(CPAD + i * 256, CT), 256), :] = a * _sigmoid(gt)
        return c

    lax.fori_loop(0, S // 256, fill, 0)


def _conv_fwd(u, cw, cb, lg, lb, B, S):
    T = B * S

    def body(ca_ref, cg_ref, w_ref, b_ref, lg_ref, lb_ref, o_ref, y_ref, glu):
        _fill_glu(ca_ref, cg_ref, glu, S)

        def step(i, c):
            t0 = pl.multiple_of(i * CT, CT)
            win = glu[pl.ds(t0, 2 * CT), :]
            acc = jnp.zeros((CT, DC), F32) + b_ref[...]
            taps = _shifted(win, [k + 2 for k in range(CK)])
            for k in range(CK):
                acc = acc + taps[k + 2] * w_ref[k:k + 1, :]
            y_ref[pl.ds(t0, CT), :] = acc
            _, _, z = _ln_fwd(acc, lg_ref[...], lb_ref[...])
            o_ref[pl.ds(t0, CT), :] = (z * _sigmoid(z)).astype(BF16)
            return c

        lax.fori_loop(0, S // CT, step, 0, unroll=2)

    vec = pl.BlockSpec((1, DC), lambda b: (0, 0))
    return pl.pallas_call(
        body, grid=(B,),
        in_specs=[pl.BlockSpec((S, DC), lambda b: (b, 3)), pl.BlockSpec((S, DC), lambda b: (b, 4)),
                  pl.BlockSpec((CT, DC), lambda b: (0, 0)), vec, vec, vec],
        out_specs=[pl.BlockSpec((S, DC), lambda b: (b, 0))] * 2,
        out_shape=[jax.ShapeDtypeStruct((T, DC), BF16), jax.ShapeDtypeStruct((T, DC), F32)],
        scratch_shapes=[pltpu.VMEM((CPAD + S, DC), F32)],
        compiler_params=_cp(("parallel",)), name="conv_fwd")(u, u, cw, cb, lg, lb)


def _conv_bwd(u, y, dconv, cw, lg, lb, B, S):
    T = B * S

    def body(ca_ref, cg_ref, y_ref, dc_ref, w_ref, lg_ref, lb_ref,
             dca_ref, dcg_ref, dw_ref, ds_ref, glu, dyp, dwacc):
        _fill_glu(ca_ref, cg_ref, glu, S)
        dyp[pl.ds(S, CPAD), :] = jnp.zeros((CPAD, DC), F32)
        lgv, lbv = lg_ref[...], lb_ref[...]

        def sum8(v):
            return functools.reduce(jnp.add, [v[r:r + 8] for r in range(0, v.shape[0], 8)])

        P1 = 4 * CT

        def p1(i, carry):
            sb, sg, sl = carry
            t0 = pl.multiple_of(i * P1, P1)
            xhat, rstd, z = _ln_fwd(y_ref[pl.ds(t0, P1), :], lgv, lbv)
            sz = _sigmoid(z)
            dz = dc_ref[pl.ds(t0, P1), :].astype(F32) * (sz * (1.0 + z * (1.0 - sz)))
            dxhat = dz * lgv
            dy = rstd * (dxhat - jnp.mean(dxhat, axis=-1, keepdims=True)
                         - xhat * jnp.mean(dxhat * xhat, axis=-1, keepdims=True))
            dyp[pl.ds(t0, P1), :] = dy
            return sb + sum8(dy), sg + sum8(dz * xhat), sl + sum8(dz)

        z8 = jnp.zeros((8, DC), F32)
        sb, sg, sl = lax.fori_loop(0, S // P1, p1, (z8, z8, z8))
        rs = lambda v: jnp.sum(v, axis=0, keepdims=True)
        ds_ref[...] = jnp.concatenate([rs(sb), rs(sg), rs(sl), jnp.zeros((5, DC), F32)], axis=0)[None]

        def p2(i, c):
            t0 = pl.multiple_of(i * CT, CT)
            win = dyp[pl.ds(t0, 2 * CT), :]
            acc = jnp.zeros((CT, DC), F32)
            taps = _shifted(win, [30 - k for k in range(CK)])
            for k in range(CK):
                acc = acc + taps[30 - k] * w_ref[k:k + 1, :]
            a = ca_ref[pl.ds(t0, CT), :].astype(F32)
            sgt = _sigmoid(cg_ref[pl.ds(t0, CT), :].astype(F32))
            dca_ref[pl.ds(t0, CT), :] = (acc * sgt).astype(BF16)
            dcg_ref[pl.ds(t0, CT), :] = (acc * a * sgt * (1.0 - sgt)).astype(BF16)
            return c

        lax.fori_loop(0, S // CT, p2, 0)

        dwacc[...] = jnp.zeros_like(dwacc)

        def p3(i, c):
            t0 = pl.multiple_of(i * CT, CT)
            win = glu[pl.ds(t0, 2 * CT), :]
            dy = dyp[pl.ds(t0, CT), :]
            for k in range(CK):
                dwacc[k] += sum8(dy * win[k + 2:k + 2 + CT, :])
            return c

        lax.fori_loop(0, S // CT, p3, 0)
        dw_ref[...] = jnp.sum(dwacc[...], axis=1)[None]

    vec = pl.BlockSpec((1, DC), lambda b: (0, 0))
    seq = pl.BlockSpec((S, DC), lambda b: (b, 0))
    return pl.pallas_call(
        body, grid=(B,),
        in_specs=[pl.BlockSpec((S, DC), lambda b: (b, 3)), pl.BlockSpec((S, DC), lambda b: (b, 4)),
                  seq, seq, pl.BlockSpec((CT, DC), lambda b: (0, 0)), vec, vec],
        out_specs=[seq, seq, pl.BlockSpec((1, CT, DC), lambda b: (b, 0, 0)),
                   pl.BlockSpec((1, 8, DC), lambda b: (b, 0, 0))],
        out_shape=[jax.ShapeDtypeStruct((T, DC), BF16)] * 2
                  + [jax.ShapeDtypeStruct((B, CT, DC), F32), jax.ShapeDtypeStruct((B, 8, DC), F32)],
        scratch_shapes=[pltpu.VMEM((CPAD + S, DC), F32), pltpu.VMEM((S + CPAD, DC), F32),
                        pltpu.VMEM((CT, 8, DC), F32)],
        compiler_params=_cp(("parallel",)), name="conv_bwd")(u, u, y, dconv, cw, lg, lb)


def _local_step(x, target, norms, W, B, S, comm=None):
    qg2 = jnp.concatenate([norms["q_norm"], norms["q_norm"]], axis=1)
    kg2 = jnp.concatenate([norms["k_norm"], norms["k_norm"]], axis=1)
    cw = jnp.concatenate([W["conv_w"], jnp.zeros((1, DC), F32)], axis=0)

    W = dict(W)
    for kern, tag in (("gate", "wu1"), ("up", "wd1"), ("down", "win")):
        carry = comm.gathers[tag] if comm else None
        if kern == "gate":
            (n1, g1), got = _ffn_gate(x, norms["ffn1_norm"], W["wg1"], "ffn1_gate", carry=carry)
        elif kern == "up":
            (u1, act1), got = _ffn_up(n1, g1, W["wu1"], "ffn1_up", carry=carry)
        else:
            (h1,), got = _ffn_down(x, act1, W["wd1"], "ffn1_down", carry=carry)
        if comm:
            W.update(comm.gathered(tag, got))
    (u, n2), got = _mix_in(h1, norms["mix_norm"], W["win"], carry=comm.gathers["wout"] if comm else None)
    if comm:
        W.update(comm.gathered("wout", got))
    (attn, lse), got = _attn_fwd(u, qg2, kg2, B, S, carry=comm.gathers["ffn2"] if comm else None)
    if comm:
        W = dict(W, **comm.gathered("ffn2", got))
    conv, y = _conv_fwd(u, cw, norms["conv_b"], norms["conv_ln_g"], norms["conv_ln_b"], B, S)
    h2, n3, g2, u2, dout, dyb, sq = _mix_out_ffn_loss(h1, attn, conv, W["wout"], norms["ffn2_norm"],
                                                      W["wg2"], W["wu2"], W["wd2"], target, "ffn2_fwd")
    loss = (0.5 / D) * jnp.sum(sq)

    (dg2, du2, dh2, dgn_ffn2), _ = _ffn_bwd_act(dyb, g2, u2, h2, dout, norms["ffn2_norm"],
                                               W["wg2"], W["wu2"], W["wd2"], "ffn2_bwd_act")
    dwd2, _ = _ffn_bwd_w((g2, u2), dyb, "ffn2_bwd_wd")
    dwg2, _ = _ffn_bwd_w(dg2, n3, "ffn2_bwd_wg")
    dwu2, _ = _ffn_bwd_w(du2, n3, "ffn2_bwd_wu")
    dattn, dconv, dwout = _mix_out_bwd(dh2, attn, conv, W["wout"])
    carry = comm.reduce_start("ffn2", {"wg2": dwg2, "wu2": dwu2, "wd2": dwd2, "wout": dwout}) if comm else None
    (dq, dk, dv, dgn_qk), got = _attn_bwd(u, attn, dattn, lse, qg2, kg2, B, S, carry=carry)
    if comm:
        comm.reduce_done(carry, got)
    dca, dcg, dcw, dcs = _conv_bwd(u, y, dconv, cw, norms["conv_ln_g"], norms["conv_ln_b"], B, S)
    dwin, dh1, dyb1, dgn_mix = _mix_in_bwd((dq, dk, dv, dca, dcg), W["win"], n2, h1, dh2, norms["mix_norm"])
    (dg1, du1, gx, dgn_ffn1), _ = _ffn_bwd_act(dyb1, g1, u1, x, dh1, norms["ffn1_norm"],
                                              W["wg1"], W["wu1"], W["wd1"], "ffn1_bwd_act")
    carry = comm.reduce_start("win", {"win": dwin}) if comm else None
    dwd1, got = _ffn_bwd_w(act1, dyb1, "ffn1_bwd_wd", carry=carry)
    if comm:
        comm.reduce_done(carry, got)
        carry = comm.reduce_start("wd1", {"wd1": dwd1})
    dwg1, got = _ffn_bwd_w(dg1, n1, "ffn1_bwd_wg", carry=carry)
    if comm:
        comm.reduce_done(carry, got)
        carry = comm.reduce_start("wg1", {"wg1": dwg1})
    dwu1, got = _ffn_bwd_w(du1, n1, "ffn1_bwd_wu", carry=carry)
    if comm:
        comm.reduce_done(carry, got)
        comm.reduce_now("wu1", {"wu1": dwu1})

    qk = jnp.sum(dgn_qk, axis=0)
    cs = jnp.sum(dcs, axis=0)
    small = {
        "ffn1_norm": jnp.sum(dgn_ffn1, axis=0),
        "mix_norm": jnp.sum(dgn_mix, axis=0),
        "q_norm": qk[0:1, 0:HD] + qk[0:1, HD:2 * HD],
        "k_norm": qk[1:2, 0:HD] + qk[1:2, HD:2 * HD],
        "conv_w": jnp.sum(dcw, axis=0)[0:CK],
        "conv_b": cs[0:1],
        "conv_ln_g": cs[1:2],
        "conv_ln_b": cs[2:3],
        "ffn2_norm": jnp.sum(dgn_ffn2, axis=0),
    }
    big = {"wg1": dwg1, "wu1": dwu1, "wd1": dwd1, "win": dwin, "wout": dwout,
           "wg2": dwg2, "wu2": dwu2, "wd2": dwd2}
    return loss, gx, big, small


HBM = pl.BlockSpec(memory_space=pltpu.HBM)
VMEM = pl.BlockSpec(memory_space=pltpu.VMEM)


def _place():
    return lax.axis_index("x"), lax.axis_index("y"), lax.axis_index("c")


class _GatherCarry:
    def __init__(self, shards):
        nt = len(shards)
        self.shards = shards
        self.in_arrays = [s for s, _ in shards]
        self.in_specs = [VMEM] * nt
        self.out_shape = [jax.ShapeDtypeStruct((NDEV * s.shape[0], s.shape[1]), dt) for s, dt in shards]
        self.out_specs = [HBM] * nt
        self.scratch = ([pltpu.VMEM(s.shape, dt) for s, dt in shards]
                        + [pltpu.SemaphoreType.DMA((nt, 7)), pltpu.SemaphoreType.DMA((nt, 7)),
                           pltpu.SemaphoreType.DMA((nt,))])

    def _copies(self, outs, scr):
        nt = len(self.shards)
        stages = scr[:nt]
        send_sems, recv_sems, local_sems = scr[nt:]
        x, y, c = _place()
        me, sibling = (x, y, c), (x, y, 1 - c)
        chips = [(1 - x, y), (x, 1 - y), (1 - x, 1 - y)]

        def rows(t, px, py, pc):
            r = self.shards[t][0].shape[0]
            return outs[t].at[pl.ds((4 * px + 2 * py + pc) * r, r), :]

        def copy(t, k, block, to, src=None):
            return pltpu.make_async_remote_copy(
                src_ref=rows(t, *block) if src is None else src, dst_ref=rows(t, *block),
                send_sem=send_sems.at[t, k], recv_sem=recv_sems.at[t, k],
                device_id=to, device_id_type=MESH)

        mine = [pltpu.make_async_copy(stages[t], rows(t, *me), local_sems.at[t]) for t in range(nt)]
        first = []
        for t in range(nt):
            first.append(copy(t, 0, me, sibling, src=stages[t]))
            first += [copy(t, 1 + j, me, (*chip, c), src=stages[t]) for j, chip in enumerate(chips)]
        return copy, mine, first, me, sibling, chips, c

    def start(self, ins, outs, scr):
        _, mine, first, *_ = self._copies(outs, scr)
        for t, (_, dt) in enumerate(self.shards):
            scr[t][...] = ins[t][...].astype(dt)
        for cp in mine + first:
            cp.start()

    def finish(self, ins, outs, scr):
        copy, mine, first, me, sibling, chips, c = self._copies(outs, scr)
        nt = len(self.shards)
        passed = []
        for j, chip in enumerate(chips):
            for t in range(nt):
                copy(t, 1 + j, (*chip, c), me).wait_recv()
                cp = copy(t, 4 + j, (*chip, c), sibling)
                cp.start()
                passed.append(cp)
        for t in range(nt):
            copy(t, 0, sibling, me).wait_recv()
            for j, chip in enumerate(chips):
                copy(t, 4 + j, (*chip, 1 - c), me).wait_recv()
        for cp in first + passed:
            cp.wait_send()
        for cp in mine:
            cp.wait()


def _run_carry(carry, name):
    def body(*refs):
        n_in, n_out = len(carry.in_arrays), len(carry.out_shape)
        ins, outs, scr = refs[:n_in], refs[n_in:n_in + n_out], refs[n_in + n_out:]
        carry.start(ins, outs, scr)
        carry.finish(ins, outs, scr)

    return pl.pallas_call(
        body, in_specs=carry.in_specs, out_specs=carry.out_specs, out_shape=carry.out_shape,
        scratch_shapes=carry.scratch, compiler_params=pltpu.CompilerParams(vmem_limit_bytes=VMEM_LIMIT),
        name=name)(*carry.in_arrays)


def _sibling_reduce(grads, name):
    nt = len(grads)
    g4 = [g.reshape(4, 2, g.shape[0] // NDEV, g.shape[1]) for g in grads]

    def body(*refs):
        ins, outs = refs[:nt], refs[nt:2 * nt]
        recv, own = refs[2 * nt:3 * nt], refs[3 * nt:4 * nt]
        send_sems, recv_sems, load_sems, store_sems = refs[4 * nt:]
        x, y, c = _place()
        sends = [pltpu.make_async_remote_copy(
            src_ref=ins[t].at[:, 1 - c], dst_ref=recv[t], send_sem=send_sems.at[t], recv_sem=recv_sems.at[t],
            device_id=(x, y, 1 - c), device_id_type=MESH) for t in range(nt)]
        loads = [pltpu.make_async_copy(ins[t].at[:, c], own[t], load_sems.at[t]) for t in range(nt)]
        stores = [pltpu.make_async_copy(own[t], outs[t], store_sems.at[t]) for t in range(nt)]
        for cp in sends + loads:
            cp.start()
        for t in range(nt):
            loads[t].wait()
            sends[t].wait_recv()
            for q in range(4):
                own[t][q] = (own[t][q].astype(F32) + recv[t][q].astype(F32)).astype(BF16)
            stores[t].start()
        for t in range(nt):
            sends[t].wait_send()
            stores[t].wait()

    blocks = [(4,) + g.shape[2:] for g in g4]
    return pl.pallas_call(
        body, in_specs=[HBM] * nt, out_specs=[HBM] * nt,
        out_shape=[jax.ShapeDtypeStruct(b, BF16) for b in blocks],
        scratch_shapes=[pltpu.VMEM(b, BF16) for b in blocks] * 2 + [pltpu.SemaphoreType.DMA((nt,))] * 4,
        compiler_params=pltpu.CompilerParams(vmem_limit_bytes=VMEM_LIMIT), name=name)(*g4)


class _ExchangeCarry:
    def __init__(self, names, parts):
        nt = len(parts)
        self.names = names
        self.in_arrays = list(parts)
        self.in_specs = [HBM] * nt
        self.out_shape = [jax.ShapeDtypeStruct(p.shape, BF16) for p in parts]
        self.out_specs = [HBM] * nt
        self.scratch = [pltpu.SemaphoreType.DMA((nt, 3)), pltpu.SemaphoreType.DMA((nt, 3)),
                        pltpu.SemaphoreType.DMA((nt,))]

    def _copies(self, ins, outs, scr):
        send_sems, recv_sems, local_sems = scr
        x, y, c = _place()
        qme = 2 * x + y
        chips = [(1 - x, y), (x, 1 - y), (1 - x, 1 - y)]
        cps = []
        for t in range(len(ins)):
            cps.append(pltpu.make_async_copy(ins[t].at[qme], outs[t].at[qme], local_sems.at[t]))
            for k, (cx, cy) in enumerate(chips):
                cps.append(pltpu.make_async_remote_copy(
                    src_ref=ins[t].at[2 * cx + cy], dst_ref=outs[t].at[qme],
                    send_sem=send_sems.at[t, k], recv_sem=recv_sems.at[t, k],
                    device_id=(cx, cy, c), device_id_type=MESH))
        return cps

    def start(self, ins, outs, scr):
        for cp in self._copies(ins, outs, scr):
            cp.start()

    def finish(self, ins, outs, scr):
        for cp in self._copies(ins, outs, scr):
            cp.wait()


class _Comm:
    def __init__(self, groups):
        self.names = {tag: list(g) for tag, g in groups.items()}
        self.gathers = {tag: _GatherCarry(list(g.values())) for tag, g in groups.items()}
        self.reduced = {}

    def gathered(self, tag, outs):
        return dict(zip(self.names[tag], outs))

    def reduce_start(self, tag, grads):
        names = list(grads)
        return _ExchangeCarry(names, _sibling_reduce([grads[n] for n in names], "sibling_reduce_" + tag))

    def reduce_done(self, carry, outs):
        self.reduced.update(zip(carry.names, outs))

    def reduce_now(self, tag, grads):
        carry = self.reduce_start(tag, grads)
        self.reduce_done(carry, _run_carry(carry, "chip_exchange_" + tag))


def _adamw_math(w, g, m, v):
    m = B1 * m + (1.0 - B1) * g
    v = B2 * v + (1.0 - B2) * (g * g)
    m_hat = m / (1.0 - B1 ** STEP)
    v_hat = v / (1.0 - B2 ** STEP)
    delta = -LR * (m_hat / (jnp.sqrt(v_hat) + AEPS) + WD * w)
    return delta, m, v


def _adamw_big(recv, w, m, v, name):
    def body(r_ref, w_ref, m_ref, v_ref, g_ref, d_ref, mo_ref, vo_ref):
        g = r_ref[0].astype(F32)
        for q in range(1, 4):
            g = g + r_ref[q].astype(F32)
        d, mn, vn = _adamw_math(w_ref[...], g, m_ref[...], v_ref[...])
        g_ref[...] = g
        d_ref[...] = d
        mo_ref[...] = mn
        vo_ref[...] = vn

    return pl.pallas_call(
        body, out_shape=[jax.ShapeDtypeStruct(w.shape, F32)] * 4,
        compiler_params=pltpu.CompilerParams(vmem_limit_bytes=VMEM_LIMIT), name=name)(recv, w, m, v)


SMALL_NAMES = ("ffn1_norm", "mix_norm", "ffn2_norm", "conv_b", "conv_ln_g", "conv_ln_b", "q_norm", "k_norm")
SROWS = 16
LOSS_ROW = len(SMALL_NAMES)
CWF = 2


def _small_step(gs, loss_row, gcw, ws, ms, vs, wcw, mcw, vcw):
    ns = len(SMALL_NAMES)
    widths = [g.shape[1] for g in gs]

    def body(*refs):
        it = iter(refs)
        take = lambda n: [next(it) for _ in range(n)]
        g_refs, (loss_ref, gcw_ref) = take(ns), take(2)
        w_refs, m_refs, v_refs = take(ns), take(ns), take(ns)
        wcw_ref, mcw_ref, vcw_ref = take(3)
        outs = [take(4) for _ in range(ns)]
        cw_outs, (loss_out,) = take(4), take(1)
        send, slots, cslots, send_sems, recv_sems, csend_sems, crecv_sems = take(7)
        x, y, c = _place()
        me = 4 * x + 2 * y + c
        send[...] = jnp.zeros_like(send)
        for k in range(ns):
            send[k:k + 1, 0:widths[k]] = g_refs[k][...]
        send[LOSS_ROW:LOSS_ROW + 1, 0:128] = loss_ref[...]
        slots[me] = send[...]
        cslots[me] = gcw_ref[...]
        cps = []
        for k in range(1, NDEV):
            peer = (x ^ ((k >> 2) & 1), y ^ ((k >> 1) & 1), c ^ (k & 1))
            cps.append(pltpu.make_async_remote_copy(
                src_ref=send, dst_ref=slots.at[me], send_sem=send_sems.at[k - 1], recv_sem=recv_sems.at[k - 1],
                device_id=peer, device_id_type=MESH))
            cps.append(pltpu.make_async_remote_copy(
                src_ref=gcw_ref, dst_ref=cslots.at[me], send_sem=csend_sems.at[k - 1],
                recv_sem=crecv_sems.at[k - 1], device_id=peer, device_id_type=MESH))
        for cp in cps:
            cp.start()
        for cp in cps:
            cp.wait()
        tot = slots[0]
        ctot = cslots[0, me]
        for j in range(1, NDEV):
            tot = tot + slots[j]
            ctot = ctot + cslots[j, me]

        def step(g, w_ref, m_ref, v_ref, o):
            d, mn, vn = _adamw_math(w_ref[...], g, m_ref[...], v_ref[...])
            o[0][...], o[1][...], o[2][...], o[3][...] = g, d, mn, vn

        for k in range(ns):
            step(tot[k:k + 1, 0:widths[k]], w_refs[k], m_refs[k], v_refs[k], outs[k])
        step(ctot, wcw_ref, mcw_ref, vcw_ref, cw_outs)
        loss_out[...] = tot[LOSS_ROW:LOSS_ROW + 1, 0:128]

    args = [*gs, loss_row, gcw, *ws, *ms, *vs, wcw, mcw, vcw]
    out_shape = ([jax.ShapeDtypeStruct((1, n), F32) for n in widths for _ in range(4)]
                 + [jax.ShapeDtypeStruct((CWF, D), F32)] * 4 + [jax.ShapeDtypeStruct((1, 128), F32)])
    res = pl.pallas_call(
        body, in_specs=[VMEM] * len(args), out_specs=[VMEM] * len(out_shape), out_shape=out_shape,
        scratch_shapes=[pltpu.VMEM((SROWS, D), F32), pltpu.VMEM((NDEV, SROWS, D), F32),
                        pltpu.VMEM((NDEV, NDEV, CWF, D), F32)]
                       + [pltpu.SemaphoreType.DMA((NDEV - 1,))] * 4,
        name="small_step")(*args)
    per = [res[4 * k:4 * k + 4] for k in range(ns)]
    return per, res[4 * ns:4 * ns + 4], res[-1]


def _pack_cw(a):
    flat = a.reshape(a.shape[:-2] + (CK * HD,))
    pad = [(0, 0)] * (flat.ndim - 1) + [(0, CWF * D - CK * HD)]
    return jnp.pad(flat, pad).reshape(a.shape[:-2] + (CWF, D))


def _unpack_cw(v):
    return v.reshape(-1)[:CK * HD].reshape(1, CK, HD)


def kernel(x, ffn1_norm, ffn1_w_gate, ffn1_w_up, ffn1_w_down, mix_norm, w_in, q_norm, k_norm, conv_w, conv_b, conv_ln_g, conv_ln_b, w_out, ffn2_norm, ffn2_w_gate, ffn2_w_up, ffn2_w_down, loss_target, m_ffn1_norm, m_ffn1_w_gate, m_ffn1_w_up, m_ffn1_w_down, m_mix_norm, m_w_in, m_q_norm, m_k_norm, m_conv_w, m_conv_b, m_conv_ln_g, m_conv_ln_b, m_w_out, m_ffn2_norm, m_ffn2_w_gate, m_ffn2_w_up, m_ffn2_w_down, v_ffn1_norm, v_ffn1_w_gate, v_ffn1_w_up, v_ffn1_w_down, v_mix_norm, v_w_in, v_q_norm, v_k_norm, v_conv_w, v_conv_b, v_conv_ln_g, v_conv_ln_b, v_w_out, v_ffn2_norm, v_ffn2_w_gate, v_ffn2_w_up, v_ffn2_w_down):
    P = dict(ffn1_norm=ffn1_norm, ffn1_w_gate=ffn1_w_gate, ffn1_w_up=ffn1_w_up, ffn1_w_down=ffn1_w_down,
             mix_norm=mix_norm, w_in=w_in, q_norm=q_norm, k_norm=k_norm, conv_w=conv_w, conv_b=conv_b,
             conv_ln_g=conv_ln_g, conv_ln_b=conv_ln_b, w_out=w_out, ffn2_norm=ffn2_norm,
             ffn2_w_gate=ffn2_w_gate, ffn2_w_up=ffn2_w_up, ffn2_w_down=ffn2_w_down)
    M = dict(ffn1_norm=m_ffn1_norm, ffn1_w_gate=m_ffn1_w_gate, ffn1_w_up=m_ffn1_w_up, ffn1_w_down=m_ffn1_w_down,
             mix_norm=m_mix_norm, w_in=m_w_in, q_norm=m_q_norm, k_norm=m_k_norm, conv_w=m_conv_w, conv_b=m_conv_b,
             conv_ln_g=m_conv_ln_g, conv_ln_b=m_conv_ln_b, w_out=m_w_out, ffn2_norm=m_ffn2_norm,
             ffn2_w_gate=m_ffn2_w_gate, ffn2_w_up=m_ffn2_w_up, ffn2_w_down=m_ffn2_w_down)
    V = dict(ffn1_norm=v_ffn1_norm, ffn1_w_gate=v_ffn1_w_gate, ffn1_w_up=v_ffn1_w_up, ffn1_w_down=v_ffn1_w_down,
             mix_norm=v_mix_norm, w_in=v_w_in, q_norm=v_q_norm, k_norm=v_k_norm, conv_w=v_conv_w, conv_b=v_conv_b,
             conv_ln_g=v_conv_ln_g, conv_ln_b=v_conv_ln_b, w_out=v_w_out, ffn2_norm=v_ffn2_norm,
             ffn2_w_gate=v_ffn2_w_gate, ffn2_w_up=v_ffn2_w_up, ffn2_w_down=v_ffn2_w_down)
    order = ["ffn1_norm", "ffn1_w_gate", "ffn1_w_up", "ffn1_w_down", "mix_norm", "w_in", "q_norm", "k_norm",
             "conv_w", "conv_b", "conv_ln_g", "conv_ln_b", "w_out", "ffn2_norm", "ffn2_w_gate", "ffn2_w_up",
             "ffn2_w_down"]
    B, S, _ = x.shape
    T = B * S

    bigs = [("wg1", "ffn1_w_gate", True), ("wu1", "ffn1_w_up", True), ("wd1", "ffn1_w_down", False),
            ("win", "w_in", True), ("wout", "w_out", False),
            ("wg2", "ffn2_w_gate", True), ("wu2", "ffn2_w_up", True), ("wd2", "ffn2_w_down", False)]
    hm = lambda a, tr: jnp.transpose(a[0]) if tr else a[0]
    cw_pad = jnp.zeros((32, 128), F32).at[0:CK, 0:HD].set(conv_w[0])
    shard = {ln: (hm(P[pn], tr), BF16) for ln, pn, tr in bigs}
    gathered = _run_carry(_GatherCarry([shard["wg1"], (cw_pad, F32)]), "gather_first")
    W = {"wg1": gathered[0]}
    cwg = gathered[1].reshape(NDEV, 32, 128)[:, 0:CK, 0:HD]
    W["conv_w"] = jnp.transpose(cwg, (1, 0, 2)).reshape(CK, DC)
    norms = {n: P[n] for n in SMALL_NAMES}
    comm = _Comm({"wu1": {"wu1": shard["wu1"]}, "wd1": {"wd1": shard["wd1"]},
                  "win": {"win": shard["win"]}, "wout": {"wout": shard["wout"]},
                  "ffn2": {n: shard[n] for n in ("wg2", "wu2", "wd2")}})

    loss_part, gx, _, small = _local_step(x.reshape(T, D), loss_target.reshape(T, D), norms, W, B, S, comm)

    G, Dl, Mn, Vn = {}, {}, {}, {}
    for ln, pn, tr in bigs:
        outs = _adamw_big(comm.reduced[ln], hm(P[pn], tr), hm(M[pn], tr), hm(V[pn], tr), "adamw_" + ln)
        G[pn], Dl[pn], Mn[pn], Vn[pn] = [(jnp.transpose(o) if tr else o)[None] for o in outs]

    dcw = small["conv_w"].reshape(CK, NDEV, HD).transpose(1, 0, 2)
    loss_row = jnp.zeros((1, 128), F32).at[0, 0].set(loss_part)
    per, cw_outs, loss_out = _small_step(
        [small[n] for n in SMALL_NAMES], loss_row, _pack_cw(dcw),
        [P[n] for n in SMALL_NAMES], [M[n] for n in SMALL_NAMES], [V[n] for n in SMALL_NAMES],
        _pack_cw(P["conv_w"][0]), _pack_cw(M["conv_w"][0]), _pack_cw(V["conv_w"][0]))
    loss = loss_out[0, 0]
    for n, outs in zip(SMALL_NAMES, per):
        G[n], Dl[n], Mn[n], Vn[n] = outs
    G["conv_w"], Dl["conv_w"], Mn["conv_w"], Vn["conv_w"] = [_unpack_cw(o) for o in cw_outs]

    return (loss, gx.reshape(B, S, D), *[G[n] for n in order], *[Dl[n] for n in order],
            *[Mn[n] for n in order], *[Vn[n] for n in order])
```

```python
import functools

import jax
import jax.numpy as jnp
from jax import lax
from jax.experimental import pallas as pl
from jax.experimental.pallas import tpu as pltpu

F32 = jnp.float32
BF16 = jnp.bfloat16

D = 1024
FF = 2816
HD = 64
DA = 512
DC = 512
DIN = 2560
CK = 31
BLK = 128
DILS = (1, 4, 16)
EPS = 1e-6
NDEV = 8
MESH = pl.DeviceIdType.MESH

LR, B1, B2, AEPS, WD, STEP = 0.001, 0.9, 0.999, 1e-08, 0.01, 10

NT = (((1,), (1,)), ((), ()))
TN = (((0,), (0,)), ((), ()))

VMEM_LIMIT = 56 * 1024 * 1024


def _cp(sem=None):
    return pltpu.CompilerParams(dimension_semantics=sem, vmem_limit_bytes=VMEM_LIMIT)


def _sigmoid(x):
    return 0.5 * (jnp.tanh(0.5 * x) + 1.0)


def _pallas(body, args, *, grid, in_specs, out_specs, out_shape, scratch_shapes, sem, name, carry=None):
    if carry is None:
        outs = pl.pallas_call(body, grid=grid, in_specs=in_specs, out_specs=out_specs, out_shape=out_shape,
                              scratch_shapes=scratch_shapes, compiler_params=_cp(sem), name=name)(*args)
        return outs, None
    n_in, n_out, n_scr = len(in_specs), len(out_shape), len(scratch_shapes)
    c_in, c_out = len(carry.in_arrays), len(carry.out_shape)

    def wrapped(*refs):
        ins, refs = refs[:n_in], refs[n_in:]
        cins, refs = refs[:c_in], refs[c_in:]
        outs, refs = refs[:n_out], refs[n_out:]
        couts, refs = refs[:c_out], refs[c_out:]
        scr, cscr = refs[:n_scr], refs[n_scr:]
        ids = [pl.program_id(a) for a in range(len(grid))]
        step = ids[0]
        for i, n in zip(ids[1:], grid[1:]):
            step = step * n + i
        steps = functools.reduce(lambda a, b: a * b, grid)

        @pl.when(step == 0)
        def _():
            carry.start(cins, couts, cscr)

        body(*ins, *outs, *scr)

        if hasattr(carry, "mid"):
            @pl.when(step == steps // 2)
            def _():
                carry.mid(cins, couts, cscr)

        @pl.when(step == steps - 1)
        def _():
            carry.finish(cins, couts, cscr)

    outs = pl.pallas_call(
        wrapped, grid=grid, in_specs=list(in_specs) + carry.in_specs, out_specs=list(out_specs) + carry.out_specs,
        out_shape=list(out_shape) + carry.out_shape, scratch_shapes=list(scratch_shapes) + carry.scratch,
        compiler_params=_cp(("arbitrary",) * len(grid)), name=name)(*args, *carry.in_arrays)
    return outs[:n_out], outs[n_out:]


def _ffn_fwd(x, gain, wg, wu, wd, target, name, carry=None):
    T = x.shape[0]
    tm, tf = 1024, 256
    nt, nf = T // tm, FF // tf
    with_loss = target is not None

    def body(*refs):
        if with_loss:
            (x_ref, gain_ref, wg_ref, wu_ref, wd_ref, t_ref,
             h_ref, n_ref, g_ref, u_ref, dout_ref, dyb_ref, sq_ref, nb_scr, acc_scr) = refs
        else:
            (x_ref, gain_ref, wg_ref, wu_ref, wd_ref,
             h_ref, n_ref, g_ref, u_ref, nb_scr, acc_scr) = refs
        f = pl.program_id(1)

        @pl.when(f == 0)
        def _():
            xv = x_ref[...]
            r = lax.rsqrt(jnp.mean(xv * xv, axis=-1, keepdims=True) + EPS)
            nb = (xv * r * gain_ref[...]).astype(BF16)
            nb_scr[...] = nb
            n_ref[...] = nb
            acc_scr[...] = jnp.zeros_like(acc_scr)

        nb = nb_scr[...]
        g = lax.dot_general(nb, wg_ref[...], NT, preferred_element_type=F32)
        u = lax.dot_general(nb, wu_ref[...], NT, preferred_element_type=F32)
        a = g * _sigmoid(g) * u
        g_ref[...] = g.astype(BF16)
        u_ref[...] = u.astype(BF16)
        acc_scr[...] += jnp.dot(a.astype(BF16), wd_ref[...], preferred_element_type=F32)

        @pl.when(f == nf - 1)
        def _():
            h = x_ref[...] + 0.5 * acc_scr[...]
            h_ref[...] = h
            if with_loss:
                e = h - t_ref[...]
                dout = e * (1.0 / D)
                dout_ref[...] = dout
                dyb_ref[...] = (0.5 * dout).astype(BF16)
                sq_ref[...] = jnp.sum(e * e, axis=0, keepdims=True)[None]

    row = pl.BlockSpec((tm, D), lambda t, f: (t, 0))
    wspec = pl.BlockSpec((tf, D), lambda t, f: (f, 0))
    gspec = pl.BlockSpec((tm, tf), lambda t, f: (t, f))
    in_specs = [row, pl.BlockSpec((1, D), lambda t, f: (0, 0)), wspec, wspec, wspec]
    out_shape = [jax.ShapeDtypeStruct((T, D), F32), jax.ShapeDtypeStruct((T, D), BF16),
                 jax.ShapeDtypeStruct((T, FF), BF16), jax.ShapeDtypeStruct((T, FF), BF16)]
    out_specs = [row, row, gspec, gspec]
    args = [x, gain, wg, wu, wd]
    if with_loss:
        in_specs.append(row)
        args.append(target)
        out_shape += [jax.ShapeDtypeStruct((T, D), F32), jax.ShapeDtypeStruct((T, D), BF16),
                      jax.ShapeDtypeStruct((nt, 1, D), F32)]
        out_specs += [row, row, pl.BlockSpec((1, 1, D), lambda t, f: (t, 0, 0))]
    return _pallas(
        body, args, grid=(nt, nf), in_specs=in_specs, out_specs=out_specs, out_shape=out_shape,
        scratch_shapes=[pltpu.VMEM((tm, D), BF16), pltpu.VMEM((tm, D), F32)],
        sem=("parallel", "arbitrary"), name=name, carry=carry)


def _ffn_bwd(dyb, nb, g, u, wg, wu, wd, name, carry=None):
    T = dyb.shape[0]
    tm, tf = 512, 256
    nt, nf = T // tm, FF // tf

    def body(dy_ref, n_ref, g_ref, u_ref, wg_ref, wu_ref, wd_ref,
             dwg_ref, dwu_ref, dwd_ref, dn_ref, ag_scr, au_scr, ad_scr):
        f, t = pl.program_id(0), pl.program_id(1)

        @pl.when(t == 0)
        def _():
            ag_scr[...] = jnp.zeros_like(ag_scr)
            au_scr[...] = jnp.zeros_like(au_scr)
            ad_scr[...] = jnp.zeros_like(ad_scr)

        dy = dy_ref[...]
        n = n_ref[...]
        gv = g_ref[...].astype(F32)
        uv = u_ref[...].astype(F32)
        da = lax.dot_general(dy, wd_ref[...], NT, preferred_element_type=F32)
        sg = _sigmoid(gv)
        silu = gv * sg
        ab = (silu * uv).astype(BF16)
        dgb = (da * uv * (sg * (1.0 + gv * (1.0 - sg)))).astype(BF16)
        dub = (da * silu).astype(BF16)
        ad_scr[...] += lax.dot_general(ab, dy, TN, preferred_element_type=F32)
        ag_scr[...] += lax.dot_general(dgb, n, TN, preferred_element_type=F32)
        au_scr[...] += lax.dot_general(dub, n, TN, preferred_element_type=F32)
        dn = (jnp.dot(dgb, wg_ref[...], preferred_element_type=F32)
              + jnp.dot(dub, wu_ref[...], preferred_element_type=F32))
        rows = pl.ds(pl.multiple_of(t * tm, tm), tm)

        @pl.when(f == 0)
        def _():
            dn_ref[rows, :] = dn

        @pl.when(f > 0)
        def _():
            dn_ref[rows, :] += dn

        @pl.when(t == nt - 1)
        def _():
            dwg_ref[...] = ag_scr[...].astype(BF16)
            dwu_ref[...] = au_scr[...].astype(BF16)
            dwd_ref[...] = ad_scr[...].astype(BF16)

    row = pl.BlockSpec((tm, D), lambda f, t: (t, 0))
    gspec = pl.BlockSpec((tm, tf), lambda f, t: (t, f))
    wspec = pl.BlockSpec((tf, D), lambda f, t: (f, 0))
    return _pallas(
        body, (dyb, nb, g, u, wg, wu, wd), grid=(nf, nt),
        in_specs=[row, row, gspec, gspec, wspec, wspec, wspec],
        out_specs=[wspec, wspec, wspec, pl.BlockSpec((T, D), lambda f, t: (0, 0))],
        out_shape=[jax.ShapeDtypeStruct((FF, D), BF16)] * 3 + [jax.ShapeDtypeStruct((T, D), F32)],
        scratch_shapes=[pltpu.VMEM((tf, D), F32)] * 3,
        sem=("arbitrary", "arbitrary"), name=name, carry=carry)


FC = 256


def _resident(shape):
    return pl.BlockSpec(shape, lambda *_: (0,) * len(shape), pipeline_mode=pl.Buffered(1))


def _mix_out_ffn_loss(h1, attn, conv, wout, gain, wg, wu, wd, target, name):
    T = h1.shape[0]
    tm = 512
    nt = T // tm

    def body(h1_ref, at_ref, cv_ref, wo_ref, gain_ref, wg_ref, wu_ref, wd_ref, t_ref,
             h2_ref, n_ref, g_ref, u_ref, dout_ref, dyb_ref, sq_ref, a_scr):
        xv = (h1_ref[...]
              + jnp.dot(at_ref[...], wo_ref[0:DA, :], preferred_element_type=F32)
              + jnp.dot(cv_ref[...], wo_ref[DA:D, :], preferred_element_type=F32))
        h2_ref[...] = xv
        r = lax.rsqrt(jnp.mean(xv * xv, axis=-1, keepdims=True) + EPS)
        n_ref[...] = (xv * r * gain_ref[...]).astype(BF16)
        for c in range(FF // FC):
            cols = slice(c * FC, (c + 1) * FC)
            nb = n_ref[...]
            g = lax.dot_general(nb, wg_ref[cols, :], NT, preferred_element_type=F32)
            u = lax.dot_general(nb, wu_ref[cols, :], NT, preferred_element_type=F32)
            g_ref[:, cols] = g.astype(BF16)
            u_ref[:, cols] = u.astype(BF16)
            a_scr[:, cols] = (g * _sigmoid(g) * u).astype(BF16)
        e = h2_ref[...] + 0.5 * jnp.dot(a_scr[...], wd_ref[...], preferred_element_type=F32) - t_ref[...]
        dout = e * (1.0 / D)
        dout_ref[...] = dout
        dyb_ref[...] = (0.5 * dout).astype(BF16)
        sq_ref[...] = jnp.sum(e * e, axis=0, keepdims=True)[None]

    row = pl.BlockSpec((tm, D), lambda t: (t, 0))
    half = pl.BlockSpec((tm, DA), lambda t: (t, 0))
    wide = pl.BlockSpec((tm, FF), lambda t: (t, 0))
    outs, _ = _pallas(
        body, (h1, attn, conv, wout, gain, wg, wu, wd, target), grid=(nt,),
        in_specs=[row, half, half, _resident((D, D)), _resident((1, D)), _resident((FF, D)), _resident((FF, D)),
                  _resident((FF, D)), row],
        out_specs=[row, row, wide, wide, row, row, pl.BlockSpec((1, 1, D), lambda t: (t, 0, 0))],
        out_shape=[jax.ShapeDtypeStruct((T, D), F32), jax.ShapeDtypeStruct((T, D), BF16)]
                  + [jax.ShapeDtypeStruct((T, FF), BF16)] * 2
                  + [jax.ShapeDtypeStruct((T, D), F32), jax.ShapeDtypeStruct((T, D), BF16),
                     jax.ShapeDtypeStruct((nt, 1, D), F32)],
        scratch_shapes=[pltpu.VMEM((tm, FF), BF16)], sem=("parallel",), name=name)
    return outs


def _ffn_gate(x, gain, wg, name, carry=None):
    T = x.shape[0]
    tm = 512

    def body(x_ref, gain_ref, wg_ref, n_ref, g_ref):
        xv = x_ref[...]
        r = lax.rsqrt(jnp.mean(xv * xv, axis=-1, keepdims=True) + EPS)
        n_ref[...] = (xv * r * gain_ref[...]).astype(BF16)
        for c in range(FF // FC):
            cols = slice(c * FC, (c + 1) * FC)
            g_ref[:, cols] = lax.dot_general(n_ref[...], wg_ref[cols, :], NT,
                                             preferred_element_type=F32).astype(BF16)

    row = pl.BlockSpec((tm, D), lambda t: (t, 0))
    wide = pl.BlockSpec((tm, FF), lambda t: (t, 0))
    return _pallas(
        body, (x, gain, wg), grid=(T // tm,), in_specs=[row, _resident((1, D)), _resident((FF, D))],
        out_specs=[row, wide], out_shape=[jax.ShapeDtypeStruct((T, D), BF16), jax.ShapeDtypeStruct((T, FF), BF16)],
        scratch_shapes=[], sem=("parallel",), name=name, carry=carry)


def _ffn_up(nb, g, wu, name, carry=None):
    T = nb.shape[0]
    tm = 512

    def body(n_ref, g_ref, wu_ref, u_ref, a_ref):
        for c in range(FF // FC):
            cols = slice(c * FC, (c + 1) * FC)
            u = lax.dot_general(n_ref[...], wu_ref[cols, :], NT, preferred_element_type=F32)
            gv = g_ref[:, cols].astype(F32)
            u_ref[:, cols] = u.astype(BF16)
            a_ref[:, cols] = (gv * _sigmoid(gv) * u).astype(BF16)

    row = pl.BlockSpec((tm, D), lambda t: (t, 0))
    wide = pl.BlockSpec((tm, FF), lambda t: (t, 0))
    return _pallas(
        body, (nb, g, wu), grid=(T // tm,), in_specs=[row, wide, _resident((FF, D))],
        out_specs=[wide, wide], out_shape=[jax.ShapeDtypeStruct((T, FF), BF16)] * 2,
        scratch_shapes=[], sem=("parallel",), name=name, carry=carry)


def _ffn_down(x, a, wd, name, carry=None):
    T = x.shape[0]
    tm = 512

    def body(x_ref, a_ref, wd_ref, h_ref):
        h_ref[...] = x_ref[...] + 0.5 * jnp.dot(a_ref[...], wd_ref[...], preferred_element_type=F32)

    row = pl.BlockSpec((tm, D), lambda t: (t, 0))
    wide = pl.BlockSpec((tm, FF), lambda t: (t, 0))
    return _pallas(
        body, (x, a, wd), grid=(T // tm,), in_specs=[row, wide, _resident((FF, D))],
        out_specs=[row], out_shape=[jax.ShapeDtypeStruct((T, D), F32)],
        scratch_shapes=[], sem=("parallel",), name=name, carry=carry)


def _ffn_bwd_act(dyb, g, u, x, dout, gain, wg, wu, wd, name, carry=None):
    T = x.shape[0]
    tm = 256
    nt = T // tm

    def body(dy_ref, g_ref, u_ref, x_ref, dout_ref, gain_ref, wg_ref, wu_ref, wd_ref,
             dg_ref, du_ref, dx_ref, dgn_ref):
        for c in range(FF // FC):
            cols = slice(c * FC, (c + 1) * FC)
            da = lax.dot_general(dy_ref[...], wd_ref[cols, :], NT, preferred_element_type=F32)
            gv = g_ref[:, cols].astype(F32)
            uv = u_ref[:, cols].astype(F32)
            sg = _sigmoid(gv)
            dg_ref[:, cols] = (da * uv * (sg * (1.0 + gv * (1.0 - sg)))).astype(BF16)
            du_ref[:, cols] = (da * (gv * sg)).astype(BF16)
        dn = (jnp.dot(dg_ref[...], wg_ref[...], preferred_element_type=F32)
              + jnp.dot(du_ref[...], wu_ref[...], preferred_element_type=F32))
        dx, dgain = _rms_bwd_rows(dn, x_ref[...], gain_ref[...])
        dx_ref[...] = dout_ref[...] + dx
        dgn_ref[...] = dgain[None]

    row = pl.BlockSpec((tm, D), lambda t: (t, 0))
    wide = pl.BlockSpec((tm, FF), lambda t: (t, 0))
    return _pallas(
        body, (dyb, g, u, x, dout, gain, wg, wu, wd), grid=(nt,),
        in_specs=[row, wide, wide, row, row, _resident((1, D)), _resident((FF, D)), _resident((FF, D)),
                  _resident((FF, D))],
        out_specs=[wide, wide, row, pl.BlockSpec((1, 1, D), lambda t: (t, 0, 0))],
        out_shape=[jax.ShapeDtypeStruct((T, FF), BF16)] * 2
                  + [jax.ShapeDtypeStruct((T, D), F32), jax.ShapeDtypeStruct((nt, 1, D), F32)],
        scratch_shapes=[], sem=("parallel",), name=name, carry=carry)


def _ffn_bwd_w(lhs, rhs, name, carry=None):
    T = rhs.shape[0]
    tf = 256
    lhs = lhs if isinstance(lhs, tuple) else (lhs,)

    def body(*refs):
        r_ref, dw_ref = refs[-2:]
        if len(lhs) == 2:
            gv = refs[0][...].astype(F32)
            lv = (gv * _sigmoid(gv) * refs[1][...].astype(F32)).astype(BF16)
        else:
            lv = refs[0][...]
        dw_ref[...] = lax.dot_general(lv, r_ref[...], TN, preferred_element_type=F32).astype(BF16)

    (dw,), got = _pallas(
        body, (*lhs, rhs), grid=(FF // tf,),
        in_specs=[pl.BlockSpec((T, tf), lambda f: (0, f))] * len(lhs) + [_resident((T, D))],
        out_specs=[pl.BlockSpec((tf, D), lambda f: (f, 0))], out_shape=[jax.ShapeDtypeStruct((FF, D), BF16)],
        scratch_shapes=[], sem=("parallel",), name=name, carry=carry)
    return dw, got


def _rms_bwd_rows(dn, xv, gain):
    r = lax.rsqrt(jnp.mean(xv * xv, axis=-1, keepdims=True) + EPS)
    xhat = xv * r
    dxhat = dn * gain
    dx = r * (dxhat - xhat * jnp.mean(dxhat * xhat, axis=-1, keepdims=True))
    return dx, jnp.sum(dn * xhat, axis=0, keepdims=True)


def _norm_bwd(dn, x, dout, gain, name):
    T = x.shape[0]
    tm = 512
    nt = T // tm

    def body(dn_ref, x_ref, dout_ref, gain_ref, dx_ref, dg_ref):
        dx, dgain = _rms_bwd_rows(dn_ref[...], x_ref[...], gain_ref[...])
        dx_ref[...] = dout_ref[...] + dx
        dg_ref[...] = dgain[None]

    row = pl.BlockSpec((tm, D), lambda t: (t, 0))
    return pl.pallas_call(
        body, grid=(nt,), in_specs=[row, row, row, pl.BlockSpec((1, D), lambda t: (0, 0))],
        out_specs=[row, pl.BlockSpec((1, 1, D), lambda t: (t, 0, 0))],
        out_shape=[jax.ShapeDtypeStruct((T, D), F32), jax.ShapeDtypeStruct((nt, 1, D), F32)],
        compiler_params=_cp(("parallel",)), name=name)(dn, x, dout, gain)


def _mix_in(h, gain, win, carry=None):
    T = h.shape[0]
    tm = 512

    def body(h_ref, gain_ref, w_ref, u_ref, n_ref):
        xv = h_ref[...]
        r = lax.rsqrt(jnp.mean(xv * xv, axis=-1, keepdims=True) + EPS)
        nb = (xv * r * gain_ref[...]).astype(BF16)
        n_ref[...] = nb
        u_ref[...] = lax.dot_general(nb, w_ref[...], NT, preferred_element_type=F32).astype(BF16)

    row = pl.BlockSpec((tm, D), lambda t: (t, 0))
    return _pallas(
        body, (h, gain, win), grid=(T // tm,),
        in_specs=[row, _resident((1, D)), _resident((DIN, D))],
        out_specs=[pl.BlockSpec((tm, DIN), lambda t: (t, 0)), row],
        out_shape=[jax.ShapeDtypeStruct((T, DIN), BF16), jax.ShapeDtypeStruct((T, D), BF16)],
        scratch_shapes=[], sem=("parallel",), name="mix_in", carry=carry)


def _mix_out(h, attn, conv, wout):
    T = h.shape[0]
    tm = 512

    def body(h_ref, a_ref, c_ref, w_ref, o_ref):
        o_ref[...] = (h_ref[...]
                      + jnp.dot(a_ref[...], w_ref[0:DA, :], preferred_element_type=F32)
                      + jnp.dot(c_ref[...], w_ref[DA:D, :], preferred_element_type=F32))

    row = pl.BlockSpec((tm, D), lambda t: (t, 0))
    half = pl.BlockSpec((tm, DA), lambda t: (t, 0))
    return pl.pallas_call(
        body, grid=(T // tm,),
        in_specs=[row, half, half, pl.BlockSpec((D, D), lambda t: (0, 0))],
        out_specs=row, out_shape=jax.ShapeDtypeStruct((T, D), F32),
        compiler_params=_cp(("parallel",)), name="mix_out")(h, attn, conv, wout)


def _mix_out_bwd(dh, attn, conv, wout):
    T = dh.shape[0]
    tm = 512
    nt = T // tm

    def body(dh_ref, a_ref, c_ref, w_ref, da_ref, dc_ref, dw_ref, acc_scr):
        t = pl.program_id(0)

        @pl.when(t == 0)
        def _():
            acc_scr[...] = jnp.zeros_like(acc_scr)

        dhb = dh_ref[...].astype(BF16)
        dmix = lax.dot_general(dhb, w_ref[...], NT, preferred_element_type=F32)
        da_ref[...] = dmix[:, 0:DA].astype(BF16)
        dc_ref[...] = dmix[:, DA:D].astype(BF16)
        acc_scr[0:DA, :] += lax.dot_general(a_ref[...], dhb, TN, preferred_element_type=F32)
        acc_scr[DA:D, :] += lax.dot_general(c_ref[...], dhb, TN, preferred_element_type=F32)

        @pl.when(t == nt - 1)
        def _():
            dw_ref[...] = acc_scr[...].astype(BF16)

    row = pl.BlockSpec((tm, D), lambda t: (t, 0))
    half = pl.BlockSpec((tm, DA), lambda t: (t, 0))
    full = pl.BlockSpec((D, D), lambda t: (0, 0))
    return pl.pallas_call(
        body, grid=(nt,), in_specs=[row, half, half, full], out_specs=[half, half, full],
        out_shape=[jax.ShapeDtypeStruct((T, DA), BF16)] * 2 + [jax.ShapeDtypeStruct((D, D), BF16)],
        scratch_shapes=[pltpu.VMEM((D, D), F32)],
        compiler_params=_cp(("arbitrary",)), name="mix_out_bwd")(dh, attn, conv, wout)


def _mix_in_bwd(dparts, win, nb, h, dh, gain):
    T = h.shape[0]
    tm = 512
    nt = T // tm

    def body(d0, d1, d2, d3, d4, w_ref, n_ref, h_ref, dh_ref, gain_ref,
             dw_ref, dx_ref, dyb_ref, dg_ref, acc_scr):
        t = pl.program_id(0)

        @pl.when(t == 0)
        def _():
            acc_scr[...] = jnp.zeros_like(acc_scr)

        n = n_ref[...]
        dn = jnp.zeros((tm, D), F32)
        for i, d_ref in enumerate((d0, d1, d2, d3, d4)):
            dv = d_ref[...]
            dn = dn + jnp.dot(dv, w_ref[i * DA:(i + 1) * DA, :], preferred_element_type=F32)
            acc_scr[i * DA:(i + 1) * DA, :] += lax.dot_general(dv, n, TN, preferred_element_type=F32)
        dx, dgain = _rms_bwd_rows(dn, h_ref[...], gain_ref[...])
        tot = dh_ref[...] + dx
        dx_ref[...] = tot
        dyb_ref[...] = (0.5 * tot).astype(BF16)
        dg_ref[...] = dgain[None]

        @pl.when(t == nt - 1)
        def _():
            dw_ref[...] = acc_scr[...].astype(BF16)

    row = pl.BlockSpec((tm, D), lambda t: (t, 0))
    half = pl.BlockSpec((tm, DA), lambda t: (t, 0))
    full = pl.BlockSpec((DIN, D), lambda t: (0, 0))
    return pl.pallas_call(
        body, grid=(nt,),
        in_specs=[half] * 5 + [full, row, row, row, pl.BlockSpec((1, D), lambda t: (0, 0))],
        out_specs=[full, row, row, pl.BlockSpec((1, 1, D), lambda t: (t, 0, 0))],
        out_shape=[jax.ShapeDtypeStruct((DIN, D), BF16), jax.ShapeDtypeStruct((T, D), F32),
                   jax.ShapeDtypeStruct((T, D), BF16), jax.ShapeDtypeStruct((nt, 1, D), F32)],
        scratch_shapes=[pltpu.VMEM((DIN, D), F32)],
        compiler_params=_cp(("arbitrary",)), name="mix_in_bwd")(*dparts, win, nb, h, dh, gain)


def _head_masks():
    lane = lax.broadcasted_iota(jnp.int32, (1, 2 * HD), 1)
    m0 = lane < HD
    return m0, jnp.logical_not(m0)


def _stack_heads(v, m0, m1):
    z = jnp.zeros_like(v)
    return jnp.concatenate([jnp.where(m0, v, z), jnp.where(m1, v, z)], axis=0)


def _unstack_heads(v2, m0):
    return jnp.where(m0, v2[0:BLK], v2[BLK:2 * BLK])


def _head_sums(xv):
    ri = lax.broadcasted_iota(jnp.int32, (2 * HD, 2 * HD), 0)
    ci = lax.broadcasted_iota(jnp.int32, (2 * HD, 2 * HD), 1)
    ones = jnp.where((ri < HD) == (ci < HD), 1.0, 0.0).astype(BF16)
    hi = xv.astype(BF16)
    lo = (xv - hi.astype(F32)).astype(BF16)
    return (jnp.dot(hi, ones, preferred_element_type=F32) + jnp.dot(lo, ones, preferred_element_type=F32))


def _head_rms(xv):
    return lax.rsqrt(_head_sums(xv * xv) * (1.0 / HD) + EPS)


def _band_mask(first):
    qi = lax.broadcasted_iota(jnp.int32, (BLK, 2 * BLK), 0)
    ci = lax.broadcasted_iota(jnp.int32, (BLK, 2 * BLK), 1)
    band = (ci >= qi) & (ci <= qi + BLK)
    return band & ((ci >= BLK) | jnp.logical_not(first))


def _block_rows(j, d, seg):
    r, n = j // seg, j % seg
    start = r + (d * BLK) * n
    first = n == 0
    prev = jnp.where(first, start, start - d * BLK)
    return pl.ds(start, BLK, stride=d), pl.ds(prev, BLK, stride=d), first


def _block_keys(refs, cur, prev, first, single):
    if single:
        qi = lax.broadcasted_iota(jnp.int32, (BLK, BLK), 0)
        ci = lax.broadcasted_iota(jnp.int32, (BLK, BLK), 1)
        return [r[cur, :].astype(BF16) for r in refs], ci <= qi
    return ([jnp.concatenate([r[prev, :], r[cur, :]], axis=0).astype(BF16) for r in refs], _band_mask(first))


def _attn_fwd(u, qg2, kg2, B, S, carry=None):
    T = B * S
    NB = S // BLK
    scale = HD ** -0.5

    def body(q_ref, k_ref, v_ref, qg_ref, kg_ref, o_ref, lse_ref, qn, kn, vn, os_, ls_):
        m0, m1 = _head_masks()
        qv = q_ref[...].astype(F32)
        qn[...] = qv * _head_rms(qv) * (qg_ref[...] * scale)
        kv = k_ref[...].astype(F32)
        kn[...] = kv * _head_rms(kv) * kg_ref[...]
        vn[...] = v_ref[...].astype(F32)

        for i, d in enumerate(DILS):
            seg = NB // d

            def blk(j, c, i=i, d=d, seg=seg):
                cur, prev, first = _block_rows(j, d, seg)
                q2 = _stack_heads(qn[cur, :].astype(BF16), m0, m1)
                (kk, vv), mask = _block_keys((kn, vn), cur, prev, first, False)
                s = lax.dot_general(q2, kk, NT, preferred_element_type=F32)
                s = jnp.where(jnp.concatenate([mask, mask], axis=0), s, -1e30)
                mx = jnp.max(s, axis=-1, keepdims=True)
                p = jnp.exp(s - mx)
                l = jnp.sum(p, axis=-1, keepdims=True)
                o2 = jnp.dot((p * (1.0 / l)).astype(BF16), vv, preferred_element_type=F32)
                os_[i, cur, :] = _unstack_heads(o2, m0)
                ls_[i, cur, :] = _unstack_heads(mx + jnp.log(l), m0)
                return c

            lax.fori_loop(0, NB, blk, 0, unroll=8)

        def comb(c, carry):
            rows = pl.ds(pl.multiple_of(c * 256, 256), 256)
            l0, l1, l2 = ls_[0, rows, :], ls_[1, rows, :], ls_[2, rows, :]
            mx = jnp.maximum(jnp.maximum(l0, l1), l2)
            e0, e1, e2 = jnp.exp(l0 - mx), jnp.exp(l1 - mx), jnp.exp(l2 - mx)
            tot = e0 + e1 + e2
            inv = 1.0 / tot
            o = (e0 * os_[0, rows, :] + e1 * os_[1, rows, :] + e2 * os_[2, rows, :]) * inv
            o_ref[rows, :] = o.astype(BF16)
            lse_ref[rows, :] = mx + jnp.log(tot)
            return carry

        lax.fori_loop(0, S // 256, comb, 0)

    pair = 2 * HD
    blk_spec = lambda off: pl.BlockSpec((S, pair), lambda b, p, off=off: (b, off + p))
    gspec = pl.BlockSpec((1, pair), lambda b, p: (0, 0))
    return _pallas(
        body, (u, u, u, qg2, kg2), grid=(B, DA // pair),
        in_specs=[blk_spec(0), blk_spec(DA // pair), blk_spec(2 * DA // pair), gspec, gspec],
        out_specs=[blk_spec(0), blk_spec(0)],
        out_shape=[jax.ShapeDtypeStruct((T, DA), BF16), jax.ShapeDtypeStruct((T, DA), F32)],
        scratch_shapes=[pltpu.VMEM((S, pair), F32)] * 3 + [pltpu.VMEM((3, S, pair), F32)] * 2,
        sem=("parallel", "parallel"), name="attn_fwd", carry=carry)


def _attn_bwd(u, attn, dattn, lse, qg2, kg2, B, S, carry=None):
    T = B * S
    NB = S // BLK
    scale = HD ** -0.5
    pair = 2 * HD

    def body(q_ref, k_ref, v_ref, o_ref, do_ref, lse_ref, qg_ref, kg_ref,
             dq_ref, dk_ref, dv_ref, dgn_ref,
             qn, kn, vn, don, ldl, accq, acck, accv):
        m0, m1 = _head_masks()
        lane = lax.broadcasted_iota(jnp.int32, (1, pair), 1)
        qv = q_ref[...].astype(F32)
        qn[...] = qv * _head_rms(qv) * (qg_ref[...] * scale)
        kv = k_ref[...].astype(F32)
        kn[...] = kv * _head_rms(kv) * kg_ref[...]
        vn[...] = v_ref[...].astype(F32)
        dov = do_ref[...].astype(F32)
        don[...] = dov
        ldl[...] = jnp.where((lane % HD) < HD // 2, lse_ref[...], _head_sums(dov * o_ref[...].astype(F32)))
        accq[...] = jnp.zeros_like(accq)
        acck[...] = jnp.zeros_like(acck)
        accv[...] = jnp.zeros_like(accv)

        for i, d in enumerate(DILS):
            seg = NB // d

            def blk(j, c, d=d, seg=seg):
                cur, prev, first = _block_rows(j, d, seg)
                q2 = _stack_heads(qn[cur, :].astype(BF16), m0, m1)
                do2 = _stack_heads(don[cur, :].astype(BF16), m0, m1)
                (kk, vv), mask = _block_keys((kn, vn), cur, prev, first, seg == 1)
                ldv = ldl[cur, :]
                lse2 = jnp.concatenate([ldv[:, 0:1], ldv[:, HD:HD + 1]], axis=0)
                dl2 = jnp.concatenate([ldv[:, HD // 2:HD // 2 + 1], ldv[:, HD + HD // 2:HD + HD // 2 + 1]], axis=0)
                s = lax.dot_general(q2, kk, NT, preferred_element_type=F32)
                p = jnp.where(jnp.concatenate([mask, mask], axis=0), jnp.exp(s - lse2), 0.0)
                dp = lax.dot_general(do2, vv, NT, preferred_element_type=F32)
                ds = (p * (dp - dl2)).astype(BF16)
                dq_acc = _unstack_heads(jnp.dot(ds, kk, preferred_element_type=F32), m0)
                dk_acc = lax.dot_general(ds, q2, TN, preferred_element_type=F32)
                dv_acc = lax.dot_general(p.astype(BF16), do2, TN, preferred_element_type=F32)
                accq[cur, :] += dq_acc
                if seg == 1:
                    acck[cur, :] += dk_acc
                    accv[cur, :] += dv_acc
                else:
                    acck[prev, :] += dk_acc[0:BLK]
                    acck[cur, :] += dk_acc[BLK:2 * BLK]
                    accv[prev, :] += dv_acc[0:BLK]
                    accv[cur, :] += dv_acc[BLK:2 * BLK]
                return c

            lax.fori_loop(0, NB, blk, 0, unroll=4)

        def norm_bwd(x_ref, dn, gain):
            xv = x_ref[...].astype(F32)
            r = _head_rms(xv)
            xhat = xv * r
            dxhat = dn * gain
            dx = r * (dxhat - xhat * (_head_sums(dxhat * xhat) * (1.0 / HD)))
            return dx, jnp.sum(dn * xhat, axis=0, keepdims=True)

        dq, dgq = norm_bwd(q_ref, accq[...], qg_ref[...] * scale)
        dk, dgk = norm_bwd(k_ref, acck[...], kg_ref[...])
        dq_ref[...] = dq.astype(BF16)
        dk_ref[...] = dk.astype(BF16)
        dv_ref[...] = accv[...].astype(BF16)
        dgn_ref[...] = jnp.concatenate([dgq * scale, dgk, jnp.zeros((6, pair), F32)], axis=0)[None]

    blk_spec = lambda off: pl.BlockSpec((S, pair), lambda b, p, off=off: (b, off + p))
    gspec = pl.BlockSpec((1, pair), lambda b, p: (0, 0))
    np_ = DA // pair
    return _pallas(
        body, (u, u, u, attn, dattn, lse, qg2, kg2), grid=(B, np_),
        in_specs=[blk_spec(0), blk_spec(np_), blk_spec(2 * np_), blk_spec(0), blk_spec(0), blk_spec(0),
                  gspec, gspec],
        out_specs=[blk_spec(0), blk_spec(0), blk_spec(0),
                   pl.BlockSpec((1, 8, pair), lambda b, p: (b * np_ + p, 0, 0))],
        out_shape=[jax.ShapeDtypeStruct((T, DA), BF16)] * 3 + [jax.ShapeDtypeStruct((B * np_, 8, pair), F32)],
        scratch_shapes=[pltpu.VMEM((S, pair), F32)] * 8,
        sem=("parallel", "parallel"), name="attn_bwd", carry=carry)


CT = 32
CPAD = 32


def _shifted(win, offsets):
    rolled, out = {}, {}
    n = win.shape[0]
    for o in offsets:
        sub = o % 8
        if sub not in rolled:
            rolled[sub] = win if sub == 0 else pltpu.roll(win, n - sub, 0)
        out[o] = rolled[sub][o - sub:o - sub + CT, :]
    return out


def _ln_fwd(y, g, b):
    mu = jnp.mean(y, axis=-1, keepdims=True)
    yc = y - mu
    rstd = lax.rsqrt(jnp.mean(yc * yc, axis=-1, keepdims=True) + EPS)
    xhat = yc * rstd
    return xhat, rstd, xhat * g + b


def _fill_glu(ca_ref, cg_ref, glu, S):
    glu[pl.ds(0, CPAD), :] = jnp.zeros((CPAD, DC), F32)

    def fill(i, c):
        rows = pl.ds(pl.multiple_of(i * 256, 256), 256)
        a = ca_ref[rows, :].astype(F32)
        gt = cg_ref[rows, :].astype(F32)
        glu[pl.ds(pl.multiple_of(CPAD + i * 256, CT), 256), :] = a * _sigmoid(gt)
        return c

    lax.fori_loop(0, S // 256, fill, 0)


def _conv_fwd(u, cw, cb, lg, lb, B, S):
    T = B * S

    def body(ca_ref, cg_ref, w_ref, b_ref, lg_ref, lb_ref, o_ref, y_ref, glu):
        _fill_glu(ca_ref, cg_ref, glu, S)

        def step(i, c):
            t0 = pl.multiple_of(i * CT, CT)
            win = glu[pl.ds(t0, 2 * CT), :]
            acc = jnp.zeros((CT, DC), F32) + b_ref[...]
            taps = _shifted(win, [k + 2 for k in range(CK)])
            for k in range(CK):
                acc = acc + taps[k + 2] * w_ref[k:k + 1, :]
            y_ref[pl.ds(t0, CT), :] = acc
            _, _, z = _ln_fwd(acc, lg_ref[...], lb_ref[...])
            o_ref[pl.ds(t0, CT), :] = (z * _sigmoid(z)).astype(BF16)
            return c

        lax.fori_loop(0, S // CT, step, 0, unroll=2)

    vec = pl.BlockSpec((1, DC), lambda b: (0, 0))
    return pl.pallas_call(
        body, grid=(B,),
        in_specs=[pl.BlockSpec((S, DC), lambda b: (b, 3)), pl.BlockSpec((S, DC), lambda b: (b, 4)),
                  pl.BlockSpec((CT, DC), lambda b: (0, 0)), vec, vec, vec],
        out_specs=[pl.BlockSpec((S, DC), lambda b: (b, 0))] * 2,
        out_shape=[jax.ShapeDtypeStruct((T, DC), BF16), jax.ShapeDtypeStruct((T, DC), F32)],
        scratch_shapes=[pltpu.VMEM((CPAD + S, DC), F32)],
        compiler_params=_cp(("parallel",)), name="conv_fwd")(u, u, cw, cb, lg, lb)


def _conv_bwd(u, y, dconv, cw, lg, lb, B, S):
    T = B * S

    def body(ca_ref, cg_ref, y_ref, dc_ref, w_ref, lg_ref, lb_ref,
             dca_ref, dcg_ref, dw_ref, ds_ref, glu, dyp, dwacc):
        _fill_glu(ca_ref, cg_ref, glu, S)
        dyp[pl.ds(S, CPAD), :] = jnp.zeros((CPAD, DC), F32)
        lgv, lbv = lg_ref[...], lb_ref[...]

        def sum8(v):
            return functools.reduce(jnp.add, [v[r:r + 8] for r in range(0, v.shape[0], 8)])

        P1 = 4 * CT

        def p1(i, carry):
            sb, sg, sl = carry
            t0 = pl.multiple_of(i * P1, P1)
            xhat, rstd, z = _ln_fwd(y_ref[pl.ds(t0, P1), :], lgv, lbv)
            sz = _sigmoid(z)
            dz = dc_ref[pl.ds(t0, P1), :].astype(F32) * (sz * (1.0 + z * (1.0 - sz)))
            dxhat = dz * lgv
            dy = rstd * (dxhat - jnp.mean(dxhat, axis=-1, keepdims=True)
                         - xhat * jnp.mean(dxhat * xhat, axis=-1, keepdims=True))
            dyp[pl.ds(t0, P1), :] = dy
            return sb + sum8(dy), sg + sum8(dz * xhat), sl + sum8(dz)

        z8 = jnp.zeros((8, DC), F32)
        sb, sg, sl = lax.fori_loop(0, S // P1, p1, (z8, z8, z8))
        rs = lambda v: jnp.sum(v, axis=0, keepdims=True)
        ds_ref[...] = jnp.concatenate([rs(sb), rs(sg), rs(sl), jnp.zeros((5, DC), F32)], axis=0)[None]

        def p2(i, c):
            t0 = pl.multiple_of(i * CT, CT)
            win = dyp[pl.ds(t0, 2 * CT), :]
            acc = jnp.zeros((CT, DC), F32)
            taps = _shifted(win, [30 - k for k in range(CK)])
            for k in range(CK):
                acc = acc + taps[30 - k] * w_ref[k:k + 1, :]
            a = ca_ref[pl.ds(t0, CT), :].astype(F32)
            sgt = _sigmoid(cg_ref[pl.ds(t0, CT), :].astype(F32))
            dca_ref[pl.ds(t0, CT), :] = (acc * sgt).astype(BF16)
            dcg_ref[pl.ds(t0, CT), :] = (acc * a * sgt * (1.0 - sgt)).astype(BF16)
            return c

        lax.fori_loop(0, S // CT, p2, 0)

        dwacc[...] = jnp.zeros_like(dwacc)

        def p3(i, c):
            t0 = pl.multiple_of(i * CT, CT)
            win = glu[pl.ds(t0, 2 * CT), :]
            dy = dyp[pl.ds(t0, CT), :]
            for k in range(CK):
                dwacc[k] += sum8(dy * win[k + 2:k + 2 + CT, :])
            return c

        lax.fori_loop(0, S // CT, p3, 0)
        dw_ref[...] = jnp.sum(dwacc[...], axis=1)[None]

    vec = pl.BlockSpec((1, DC), lambda b: (0, 0))
    seq = pl.BlockSpec((S, DC), lambda b: (b, 0))
    return pl.pallas_call(
        body, grid=(B,),
        in_specs=[pl.BlockSpec((S, DC), lambda b: (b, 3)), pl.BlockSpec((S, DC), lambda b: (b, 4)),
                  seq, seq, pl.BlockSpec((CT, DC), lambda b: (0, 0)), vec, vec],
        out_specs=[seq, seq, pl.BlockSpec((1, CT, DC), lambda b: (b, 0, 0)),
                   pl.BlockSpec((1, 8, DC), lambda b: (b, 0, 0))],
        out_shape=[jax.ShapeDtypeStruct((T, DC), BF16)] * 2
                  + [jax.ShapeDtypeStruct((B, CT, DC), F32), jax.ShapeDtypeStruct((B, 8, DC), F32)],
        scratch_shapes=[pltpu.VMEM((CPAD + S, DC), F32), pltpu.VMEM((S + CPAD, DC), F32),
                        pltpu.VMEM((CT, 8, DC), F32)],
        compiler_params=_cp(("parallel",)), name="conv_bwd")(u, u, y, dconv, cw, lg, lb)


def _local_step(x, target, norms, W, B, S, comm=None):
    qg2 = jnp.concatenate([norms["q_norm"], norms["q_norm"]], axis=1)
    kg2 = jnp.concatenate([norms["k_norm"], norms["k_norm"]], axis=1)
    cw = jnp.concatenate([W["conv_w"], jnp.zeros((1, DC), F32)], axis=0)

    W = dict(W)
    for kern, tag in (("gate", "wu1"), ("up", "wd1"), ("down", "win")):
        carry = comm.gathers[tag] if comm else None
        if kern == "gate":
            (n1, g1), got = _ffn_gate(x, norms["ffn1_norm"], W["wg1"], "ffn1_gate", carry=carry)
        elif kern == "up":
            (u1, act1), got = _ffn_up(n1, g1, W["wu1"], "ffn1_up", carry=carry)
        else:
            (h1,), got = _ffn_down(x, act1, W["wd1"], "ffn1_down", carry=carry)
        if comm:
            W.update(comm.gathered(tag, got))
    (u, n2), got = _mix_in(h1, norms["mix_norm"], W["win"], carry=comm.gathers["wout"] if comm else None)
    if comm:
        W.update(comm.gathered("wout", got))
    (attn, lse), got = _attn_fwd(u, qg2, kg2, B, S, carry=comm.gathers["ffn2"] if comm else None)
    if comm:
        W = dict(W, **comm.gathered("ffn2", got))
    conv, y = _conv_fwd(u, cw, norms["conv_b"], norms["conv_ln_g"], norms["conv_ln_b"], B, S)
    h2, n3, g2, u2, dout, dyb, sq = _mix_out_ffn_loss(h1, attn, conv, W["wout"], norms["ffn2_norm"],
                                                      W["wg2"], W["wu2"], W["wd2"], target, "ffn2_fwd")
    loss = (0.5 / D) * jnp.sum(sq)

    (dg2, du2, dh2, dgn_ffn2), _ = _ffn_bwd_act(dyb, g2, u2, h2, dout, norms["ffn2_norm"],
                                               W["wg2"], W["wu2"], W["wd2"], "ffn2_bwd_act")
    dwd2, _ = _ffn_bwd_w((g2, u2), dyb, "ffn2_bwd_wd")
    dwg2, _ = _ffn_bwd_w(dg2, n3, "ffn2_bwd_wg")
    dwu2, _ = _ffn_bwd_w(du2, n3, "ffn2_bwd_wu")
    dattn, dconv, dwout = _mix_out_bwd(dh2, attn, conv, W["wout"])
    carry = comm.reduce_start("ffn2", {"wg2": dwg2, "wu2": dwu2, "wd2": dwd2, "wout": dwout}) if comm else None
    (dq, dk, dv, dgn_qk), got = _attn_bwd(u, attn, dattn, lse, qg2, kg2, B, S, carry=carry)
    if comm:
        comm.reduce_done(carry, got)
    dca, dcg, dcw, dcs = _conv_bwd(u, y, dconv, cw, norms["conv_ln_g"], norms["conv_ln_b"], B, S)
    dwin, dh1, dyb1, dgn_mix = _mix_in_bwd((dq, dk, dv, dca, dcg), W["win"], n2, h1, dh2, norms["mix_norm"])
    (dg1, du1, gx, dgn_ffn1), _ = _ffn_bwd_act(dyb1, g1, u1, x, dh1, norms["ffn1_norm"],
                                              W["wg1"], W["wu1"], W["wd1"], "ffn1_bwd_act")
    carry = comm.reduce_start("win", {"win": dwin}) if comm else None
    dwd1, got = _ffn_bwd_w(act1, dyb1, "ffn1_bwd_wd", carry=carry)
    if comm:
        comm.reduce_done(carry, got)
        carry = comm.reduce_start("wd1", {"wd1": dwd1})
    dwg1, got = _ffn_bwd_w(dg1, n1, "ffn1_bwd_wg", carry=carry)
    if comm:
        comm.reduce_done(carry, got)
        carry = comm.reduce_start("wg1", {"wg1": dwg1})
    dwu1, got = _ffn_bwd_w(du1, n1, "ffn1_bwd_wu", carry=carry)
    if comm:
        comm.reduce_done(carry, got)
        comm.reduce_now("wu1", {"wu1": dwu1})

    qk = jnp.sum(dgn_qk, axis=0)
    cs = jnp.sum(dcs, axis=0)
    small = {
        "ffn1_norm": jnp.sum(dgn_ffn1, axis=0),
        "mix_norm": jnp.sum(dgn_mix, axis=0),
        "q_norm": qk[0:1, 0:HD] + qk[0:1, HD:2 * HD],
        "k_norm": qk[1:2, 0:HD] + qk[1:2, HD:2 * HD],
        "conv_w": jnp.sum(dcw, axis=0)[0:CK],
        "conv_b": cs[0:1],
        "conv_ln_g": cs[1:2],
        "conv_ln_b": cs[2:3],
        "ffn2_norm": jnp.sum(dgn_ffn2, axis=0),
    }
    big = {"wg1": dwg1, "wu1": dwu1, "wd1": dwd1, "win": dwin, "wout": dwout,
           "wg2": dwg2, "wu2": dwu2, "wd2": dwd2}
    return loss, gx, big, small


HBM = pl.BlockSpec(memory_space=pltpu.HBM)
VMEM = pl.BlockSpec(memory_space=pltpu.VMEM)


def _place():
    return lax.axis_index("x"), lax.axis_index("y"), lax.axis_index("c")


class _GatherCarry:
    def __init__(self, shards):
        nt = len(shards)
        self.shards = shards
        self.in_arrays = [s for s, _ in shards]
        self.in_specs = [VMEM] * nt
        self.out_shape = [jax.ShapeDtypeStruct((NDEV * s.shape[0], s.shape[1]), dt) for s, dt in shards]
        self.out_specs = [HBM] * nt
        self.scratch = ([pltpu.VMEM(s.shape, dt) for s, dt in shards]
                        + [pltpu.SemaphoreType.DMA((nt, 7)), pltpu.SemaphoreType.DMA((nt, 7)),
                           pltpu.SemaphoreType.DMA((nt,))])

    def _copies(self, outs, scr):
        nt = len(self.shards)
        stages = scr[:nt]
        send_sems, recv_sems, local_sems = scr[nt:]
        x, y, c = _place()
        me, sibling = (x, y, c), (x, y, 1 - c)
        xn, yn, diag = (1 - x, y, c), (x, 1 - y, c), (1 - x, 1 - y, c)
        via = (x ^ c, y ^ (1 - c), c)
        onto = (x ^ (1 - c), y ^ c, c)

        def rows(t, px, py, pc):
            r = self.shards[t][0].shape[0]
            return outs[t].at[pl.ds((4 * px + 2 * py + pc) * r, r), :]

        def copy(t, k, block, to, src=None):
            return pltpu.make_async_remote_copy(
                src_ref=rows(t, *block) if src is None else src, dst_ref=rows(t, *block),
                send_sem=send_sems.at[t, k], recv_sem=recv_sems.at[t, k],
                device_id=to, device_id_type=MESH)

        sib = lambda b: (b[0], b[1], 1 - c)
        return dict(
            local=[pltpu.make_async_copy(stages[t], rows(t, *me), local_sems.at[t]) for t in range(nt)],
            own=[[copy(t, 0, me, sibling, src=stages[t]), copy(t, 1, me, xn, src=stages[t]),
                  copy(t, 2, me, yn, src=stages[t])] for t in range(nt)],
            relay=[copy(t, 3, via, onto) for t in range(nt)],
            down=[[copy(t, 4, xn, sibling), copy(t, 5, yn, sibling)] for t in range(nt)],
            down_diag=[copy(t, 6, diag, sibling) for t in range(nt)],
            got_xy=[[copy(t, 1, xn, me), copy(t, 2, yn, me)] for t in range(nt)],
            got_diag=[copy(t, 3, diag, me) for t in range(nt)],
            got_sib=[[copy(t, 0, sibling, me), copy(t, 4, sib(xn), me), copy(t, 5, sib(yn), me),
                      copy(t, 6, sib(diag), me)] for t in range(nt)])

    def start(self, ins, outs, scr):
        cps = self._copies(outs, scr)
        for t, (_, dt) in enumerate(self.shards):
            scr[t][...] = ins[t][...].astype(dt)
            for cp in [cps["local"][t]] + cps["own"][t]:
                cp.start()

    def mid(self, ins, outs, scr):
        cps = self._copies(outs, scr)
        for t in range(len(self.shards)):
            for cp in cps["got_xy"][t]:
                cp.wait_recv()
            for cp in [cps["relay"][t]] + cps["down"][t]:
                cp.start()

    def finish(self, ins, outs, scr):
        cps = self._copies(outs, scr)
        for t in range(len(self.shards)):
            cps["got_diag"][t].wait_recv()
            cps["down_diag"][t].start()
        for t in range(len(self.shards)):
            for cp in cps["got_sib"][t]:
                cp.wait_recv()
            for cp in cps["own"][t] + [cps["relay"][t]] + cps["down"][t] + [cps["down_diag"][t]]:
                cp.wait_send()
            cps["local"][t].wait()


def _run_carry(carry, name):
    def body(*refs):
        n_in, n_out = len(carry.in_arrays), len(carry.out_shape)
        ins, outs, scr = refs[:n_in], refs[n_in:n_in + n_out], refs[n_in + n_out:]
        carry.start(ins, outs, scr)
        if hasattr(carry, "mid"):
            carry.mid(ins, outs, scr)
        carry.finish(ins, outs, scr)

    return pl.pallas_call(
        body, in_specs=carry.in_specs, out_specs=carry.out_specs, out_shape=carry.out_shape,
        scratch_shapes=carry.scratch, compiler_params=pltpu.CompilerParams(vmem_limit_bytes=VMEM_LIMIT),
        name=name)(*carry.in_arrays)


def _sibling_reduce(grads, name):
    nt = len(grads)
    g4 = [g.reshape(4, 2, g.shape[0] // NDEV, g.shape[1]) for g in grads]

    def body(*refs):
        ins, outs = refs[:nt], refs[nt:2 * nt]
        recv, own = refs[2 * nt:3 * nt], refs[3 * nt:4 * nt]
        send_sems, recv_sems, load_sems, store_sems = refs[4 * nt:]
        x, y, c = _place()
        sends = [pltpu.make_async_remote_copy(
            src_ref=ins[t].at[:, 1 - c], dst_ref=recv[t], send_sem=send_sems.at[t], recv_sem=recv_sems.at[t],
            device_id=(x, y, 1 - c), device_id_type=MESH) for t in range(nt)]
        loads = [pltpu.make_async_copy(ins[t].at[:, c], own[t], load_sems.at[t]) for t in range(nt)]
        stores = [pltpu.make_async_copy(own[t], outs[t], store_sems.at[t]) for t in range(nt)]
        for cp in sends + loads:
            cp.start()
        for t in range(nt):
            loads[t].wait()
            sends[t].wait_recv()
            for q in range(4):
                own[t][q] = (own[t][q].astype(F32) + recv[t][q].astype(F32)).astype(BF16)
            stores[t].start()
        for t in range(nt):
            sends[t].wait_send()
            stores[t].wait()

    blocks = [(4,) + g.shape[2:] for g in g4]
    return pl.pallas_call(
        body, in_specs=[HBM] * nt, out_specs=[HBM] * nt,
        out_shape=[jax.ShapeDtypeStruct(b, BF16) for b in blocks],
        scratch_shapes=[pltpu.VMEM(b, BF16) for b in blocks] * 2 + [pltpu.SemaphoreType.DMA((nt,))] * 4,
        compiler_params=pltpu.CompilerParams(vmem_limit_bytes=VMEM_LIMIT), name=name)(*g4)


class _ExchangeCarry:
    def __init__(self, names, parts):
        nt = len(parts)
        self.names = names
        self.in_arrays = list(parts)
        self.in_specs = [HBM] * nt
        self.out_shape = [jax.ShapeDtypeStruct(p.shape, BF16) for p in parts]
        self.out_specs = [HBM] * nt
        self.scratch = [pltpu.SemaphoreType.DMA((nt, 3)), pltpu.SemaphoreType.DMA((nt, 3)),
                        pltpu.SemaphoreType.DMA((nt,))]

    def _copies(self, ins, outs, scr):
        send_sems, recv_sems, local_sems = scr
        x, y, c = _place()
        qme = 2 * x + y
        chips = [(1 - x, y), (x, 1 - y), (1 - x, 1 - y)]
        cps = []
        for t in range(len(ins)):
            cps.append(pltpu.make_async_copy(ins[t].at[qme], outs[t].at[qme], local_sems.at[t]))
            for k, (cx, cy) in enumerate(chips):
                cps.append(pltpu.make_async_remote_copy(
                    src_ref=ins[t].at[2 * cx + cy], dst_ref=outs[t].at[qme],
                    send_sem=send_sems.at[t, k], recv_sem=recv_sems.at[t, k],
                    device_id=(cx, cy, c), device_id_type=MESH))
        return cps

    def start(self, ins, outs, scr):
        for cp in self._copies(ins, outs, scr):
            cp.start()

    def finish(self, ins, outs, scr):
        for cp in self._copies(ins, outs, scr):
            cp.wait()


class _Comm:
    def __init__(self, groups):
        self.names = {tag: list(g) for tag, g in groups.items()}
        self.gathers = {tag: _GatherCarry(list(g.values())) for tag, g in groups.items()}
        self.reduced = {}

    def gathered(self, tag, outs):
        return dict(zip(self.names[tag], outs))

    def reduce_start(self, tag, grads):
        names = list(grads)
        return _ExchangeCarry(names, _sibling_reduce([grads[n] for n in names], "sibling_reduce_" + tag))

    def reduce_done(self, carry, outs):
        self.reduced.update(zip(carry.names, outs))

    def reduce_now(self, tag, grads):
        carry = self.reduce_start(tag, grads)
        self.reduce_done(carry, _run_carry(carry, "chip_exchange_" + tag))


def _adamw_math(w, g, m, v):
    m = B1 * m + (1.0 - B1) * g
    v = B2 * v + (1.0 - B2) * (g * g)
    m_hat = m / (1.0 - B1 ** STEP)
    v_hat = v / (1.0 - B2 ** STEP)
    delta = -LR * (m_hat / (jnp.sqrt(v_hat) + AEPS) + WD * w)
    return delta, m, v


def _adamw_big(recv, w, m, v, name):
    def body(r_ref, w_ref, m_ref, v_ref, g_ref, d_ref, mo_ref, vo_ref):
        g = r_ref[0].astype(F32)
        for q in range(1, 4):
            g = g + r_ref[q].astype(F32)
        d, mn, vn = _adamw_math(w_ref[...], g, m_ref[...], v_ref[...])
        g_ref[...] = g
        d_ref[...] = d
        mo_ref[...] = mn
        vo_ref[...] = vn

    rows, n = w.shape
    tr = rows // 2
    row = pl.BlockSpec((tr, n), lambda t: (t, 0))
    return pl.pallas_call(
        body, grid=(2,), in_specs=[pl.BlockSpec((4, tr, n), lambda t: (0, t, 0)), row, row, row],
        out_specs=[row] * 4, out_shape=[jax.ShapeDtypeStruct(w.shape, F32)] * 4,
        compiler_params=_cp(("parallel",)), name=name)(recv, w, m, v)


SMALL_NAMES = ("ffn1_norm", "mix_norm", "ffn2_norm", "conv_b", "conv_ln_g", "conv_ln_b", "q_norm", "k_norm")
SROWS = 16
LOSS_ROW = len(SMALL_NAMES)
CWF = 2


def _small_step(gs, loss_row, gcw, ws, ms, vs, wcw, mcw, vcw):
    ns = len(SMALL_NAMES)
    widths = [g.shape[1] for g in gs]

    def body(*refs):
        it = iter(refs)
        take = lambda n: [next(it) for _ in range(n)]
        g_refs, (loss_ref, gcw_ref) = take(ns), take(2)
        w_refs, m_refs, v_refs = take(ns), take(ns), take(ns)
        wcw_ref, mcw_ref, vcw_ref = take(3)
        outs = [take(4) for _ in range(ns)]
        cw_outs, (loss_out,) = take(4), take(1)
        send, slots, cslots, send_sems, recv_sems, csend_sems, crecv_sems = take(7)
        x, y, c = _place()
        me = 4 * x + 2 * y + c
        send[...] = jnp.zeros_like(send)
        for k in range(ns):
            send[k:k + 1, 0:widths[k]] = g_refs[k][...]
        send[LOSS_ROW:LOSS_ROW + 1, 0:128] = loss_ref[...]
        slots[me] = send[...]
        cslots[me] = gcw_ref[...]
        cps = []
        for k in range(1, NDEV):
            peer = (x ^ ((k >> 2) & 1), y ^ ((k >> 1) & 1), c ^ (k & 1))
            cps.append(pltpu.make_async_remote_copy(
                src_ref=send, dst_ref=slots.at[me], send_sem=send_sems.at[k - 1], recv_sem=recv_sems.at[k - 1],
                device_id=peer, device_id_type=MESH))
            cps.append(pltpu.make_async_remote_copy(
                src_ref=gcw_ref, dst_ref=cslots.at[me], send_sem=csend_sems.at[k - 1],
                recv_sem=crecv_sems.at[k - 1], device_id=peer, device_id_type=MESH))
        for cp in cps:
            cp.start()
        for cp in cps:
            cp.wait()
        tot = slots[0]
        ctot = cslots[0, me]
        for j in range(1, NDEV):
            tot = tot + slots[j]
            ctot = ctot + cslots[j, me]

        def step(g, w_ref, m_ref, v_ref, o):
            d, mn, vn = _adamw_math(w_ref[...], g, m_ref[...], v_ref[...])
            o[0][...], o[1][...], o[2][...], o[3][...] = g, d, mn, vn

        for k in range(ns):
            step(tot[k:k + 1, 0:widths[k]], w_refs[k], m_refs[k], v_refs[k], outs[k])
        step(ctot, wcw_ref, mcw_ref, vcw_ref, cw_outs)
        loss_out[...] = tot[LOSS_ROW:LOSS_ROW + 1, 0:128]

    args = [*gs, loss_row, gcw, *ws, *ms, *vs, wcw, mcw, vcw]
    out_shape = ([jax.ShapeDtypeStruct((1, n), F32) for n in widths for _ in range(4)]
                 + [jax.ShapeDtypeStruct((CWF, D), F32)] * 4 + [jax.ShapeDtypeStruct((1, 128), F32)])
    res = pl.pallas_call(
        body, in_specs=[VMEM] * len(args), out_specs=[VMEM] * len(out_shape), out_shape=out_shape,
        scratch_shapes=[pltpu.VMEM((SROWS, D), F32), pltpu.VMEM((NDEV, SROWS, D), F32),
                        pltpu.VMEM((NDEV, NDEV, CWF, D), F32)]
                       + [pltpu.SemaphoreType.DMA((NDEV - 1,))] * 4,
        name="small_step")(*args)
    per = [res[4 * k:4 * k + 4] for k in range(ns)]
    return per, res[4 * ns:4 * ns + 4], res[-1]


def _pack_cw(a):
    flat = a.reshape(a.shape[:-2] + (CK * HD,))
    pad = [(0, 0)] * (flat.ndim - 1) + [(0, CWF * D - CK * HD)]
    return jnp.pad(flat, pad).reshape(a.shape[:-2] + (CWF, D))


def _unpack_cw(v):
    return v.reshape(-1)[:CK * HD].reshape(1, CK, HD)


def kernel(x, ffn1_norm, ffn1_w_gate, ffn1_w_up, ffn1_w_down, mix_norm, w_in, q_norm, k_norm, conv_w, conv_b, conv_ln_g, conv_ln_b, w_out, ffn2_norm, ffn2_w_gate, ffn2_w_up, ffn2_w_down, loss_target, m_ffn1_norm, m_ffn1_w_gate, m_ffn1_w_up, m_ffn1_w_down, m_mix_norm, m_w_in, m_q_norm, m_k_norm, m_conv_w, m_conv_b, m_conv_ln_g, m_conv_ln_b, m_w_out, m_ffn2_norm, m_ffn2_w_gate, m_ffn2_w_up, m_ffn2_w_down, v_ffn1_norm, v_ffn1_w_gate, v_ffn1_w_up, v_ffn1_w_down, v_mix_norm, v_w_in, v_q_norm, v_k_norm, v_conv_w, v_conv_b, v_conv_ln_g, v_conv_ln_b, v_w_out, v_ffn2_norm, v_ffn2_w_gate, v_ffn2_w_up, v_ffn2_w_down):
    P = dict(ffn1_norm=ffn1_norm, ffn1_w_gate=ffn1_w_gate, ffn1_w_up=ffn1_w_up, ffn1_w_down=ffn1_w_down,
             mix_norm=mix_norm, w_in=w_in, q_norm=q_norm, k_norm=k_norm, conv_w=conv_w, conv_b=conv_b,
             conv_ln_g=conv_ln_g, conv_ln_b=conv_ln_b, w_out=w_out, ffn2_norm=ffn2_norm,
             ffn2_w_gate=ffn2_w_gate, ffn2_w_up=ffn2_w_up, ffn2_w_down=ffn2_w_down)
    M = dict(ffn1_norm=m_ffn1_norm, ffn1_w_gate=m_ffn1_w_gate, ffn1_w_up=m_ffn1_w_up, ffn1_w_down=m_ffn1_w_down,
             mix_norm=m_mix_norm, w_in=m_w_in, q_norm=m_q_norm, k_norm=m_k_norm, conv_w=m_conv_w, conv_b=m_conv_b,
             conv_ln_g=m_conv_ln_g, conv_ln_b=m_conv_ln_b, w_out=m_w_out, ffn2_norm=m_ffn2_norm,
             ffn2_w_gate=m_ffn2_w_gate, ffn2_w_up=m_ffn2_w_up, ffn2_w_down=m_ffn2_w_down)
    V = dict(ffn1_norm=v_ffn1_norm, ffn1_w_gate=v_ffn1_w_gate, ffn1_w_up=v_ffn1_w_up, ffn1_w_down=v_ffn1_w_down,
             mix_norm=v_mix_norm, w_in=v_w_in, q_norm=v_q_norm, k_norm=v_k_norm, conv_w=v_conv_w, conv_b=v_conv_b,
             conv_ln_g=v_conv_ln_g, conv_ln_b=v_conv_ln_b, w_out=v_w_out, ffn2_norm=v_ffn2_norm,
             ffn2_w_gate=v_ffn2_w_gate, ffn2_w_up=v_ffn2_w_up, ffn2_w_down=v_ffn2_w_down)
    order = ["ffn1_norm", "ffn1_w_gate", "ffn1_w_up", "ffn1_w_down", "mix_norm", "w_in", "q_norm", "k_norm",
             "conv_w", "conv_b", "conv_ln_g", "conv_ln_b", "w_out", "ffn2_norm", "ffn2_w_gate", "ffn2_w_up",
             "ffn2_w_down"]
    B, S, _ = x.shape
    T = B * S

    bigs = [("wg1", "ffn1_w_gate", True), ("wu1", "ffn1_w_up", True), ("wd1", "ffn1_w_down", False),
            ("win", "w_in", True), ("wout", "w_out", False),
            ("wg2", "ffn2_w_gate", True), ("wu2", "ffn2_w_up", True), ("wd2", "ffn2_w_down", False)]
    hm = lambda a, tr: jnp.transpose(a[0]) if tr else a[0]
    cw_pad = jnp.zeros((32, 128), F32).at[0:CK, 0:HD].set(conv_w[0])
    shard = {ln: (hm(P[pn], tr), BF16) for ln, pn, tr in bigs}
    gathered = _run_carry(_GatherCarry([shard["wg1"], (cw_pad, F32)]), "gather_first")
    W = {"wg1": gathered[0]}
    cwg = gathered[1].reshape(NDEV, 32, 128)[:, 0:CK, 0:HD]
    W["conv_w"] = jnp.transpose(cwg, (1, 0, 2)).reshape(CK, DC)
    norms = {n: P[n] for n in SMALL_NAMES}
    comm = _Comm({"wu1": {"wu1": shard["wu1"]}, "wd1": {"wd1": shard["wd1"]},
                  "win": {"win": shard["win"]}, "wout": {"wout": shard["wout"]},
                  "ffn2": {n: shard[n] for n in ("wg2", "wu2", "wd2")}})

    loss_part, gx, _, small = _local_step(x.reshape(T, D), loss_target.reshape(T, D), norms, W, B, S, comm)

    G, Dl, Mn, Vn = {}, {}, {}, {}
    for ln, pn, tr in bigs:
        outs = _adamw_big(comm.reduced[ln], hm(P[pn], tr), hm(M[pn], tr), hm(V[pn], tr), "adamw_" + ln)
        G[pn], Dl[pn], Mn[pn], Vn[pn] = [(jnp.transpose(o) if tr else o)[None] for o in outs]

    dcw = small["conv_w"].reshape(CK, NDEV, HD).transpose(1, 0, 2)
    loss_row = jnp.zeros((1, 128), F32).at[0, 0].set(loss_part)
    per, cw_outs, loss_out = _small_step(
        [small[n] for n in SMALL_NAMES], loss_row, _pack_cw(dcw),
        [P[n] for n in SMALL_NAMES], [M[n] for n in SMALL_NAMES], [V[n] for n in SMALL_NAMES],
        _pack_cw(P["conv_w"][0]), _pack_cw(M["conv_w"][0]), _pack_cw(V["conv_w"][0]))
    loss = loss_out[0, 0]
    for n, outs in zip(SMALL_NAMES, per):
        G[n], Dl[n], Mn[n], Vn[n] = outs
    G["conv_w"], Dl["conv_w"], Mn["conv_w"], Vn["conv_w"] = [_unpack_cw(o) for o in cw_outs]

    return (loss, gx.reshape(B, S, D), *[G[n] for n in order], *[Dl[n] for n in order],
            *[Mn[n] for n in order], *[Vn[n] for n in order])
```

```python
import functools

import jax
import jax.numpy as jnp
from jax import lax
from jax.experimental import pallas as pl
from jax.experimental.pallas import tpu as pltpu

F32 = jnp.float32
BF16 = jnp.bfloat16

D = 1024
FF = 2816
HD = 64
DA = 512
DC = 512
DIN = 2560
CK = 31
BLK = 128
DILS = (1, 4, 16)
EPS = 1e-6
NDEV = 8
MESH = pl.DeviceIdType.MESH

LR, B1, B2, AEPS, WD, STEP = 0.001, 0.9, 0.999, 1e-08, 0.01, 10

NT = (((1,), (1,)), ((), ()))
TN = (((0,), (0,)), ((), ()))

VMEM_LIMIT = 56 * 1024 * 1024


def _cp(sem=None):
    return pltpu.CompilerParams(dimension_semantics=sem, vmem_limit_bytes=VMEM_LIMIT)


def _sigmoid(x):
    return 0.5 * (jnp.tanh(0.5 * x) + 1.0)


def _pallas(body, args, *, grid, in_specs, out_specs, out_shape, scratch_shapes, sem, name, carry=None):
    if carry is None:
        outs = pl.pallas_call(body, grid=grid, in_specs=in_specs, out_specs=out_specs, out_shape=out_shape,
                              scratch_shapes=scratch_shapes, compiler_params=_cp(sem), name=name)(*args)
        return outs, None
    n_in, n_out, n_scr = len(in_specs), len(out_shape), len(scratch_shapes)
    c_in, c_out = len(carry.in_arrays), len(carry.out_shape)

    def wrapped(*refs):
        ins, refs = refs[:n_in], refs[n_in:]
        cins, refs = refs[:c_in], refs[c_in:]
        outs, refs = refs[:n_out], refs[n_out:]
        couts, refs = refs[:c_out], refs[c_out:]
        scr, cscr = refs[:n_scr], refs[n_scr:]
        ids = [pl.program_id(a) for a in range(len(grid))]
        step = ids[0]
        for i, n in zip(ids[1:], grid[1:]):
            step = step * n + i
        steps = functools.reduce(lambda a, b: a * b, grid)

        @pl.when(step == 0)
        def _():
            carry.start(cins, couts, cscr)

        body(*ins, *outs, *scr)

        if hasattr(carry, "mid"):
            @pl.when(step == (5 * steps) // 8)
            def _():
                carry.mid(cins, couts, cscr)

        @pl.when(step == steps - 1)
        def _():
            carry.finish(cins, couts, cscr)

    outs = pl.pallas_call(
        wrapped, grid=grid, in_specs=list(in_specs) + carry.in_specs, out_specs=list(out_specs) + carry.out_specs,
        out_shape=list(out_shape) + carry.out_shape, scratch_shapes=list(scratch_shapes) + carry.scratch,
        compiler_params=_cp(("arbitrary",) * len(grid)), name=name)(*args, *carry.in_arrays)
    return outs[:n_out], outs[n_out:]


def _ffn_fwd(x, gain, wg, wu, wd, target, name, carry=None):
    T = x.shape[0]
    tm, tf = 1024, 256
    nt, nf = T // tm, FF // tf
    with_loss = target is not None

    def body(*refs):
        if with_loss:
            (x_ref, gain_ref, wg_ref, wu_ref, wd_ref, t_ref,
             h_ref, n_ref, g_ref, u_ref, dout_ref, dyb_ref, sq_ref, nb_scr, acc_scr) = refs
        else:
            (x_ref, gain_ref, wg_ref, wu_ref, wd_ref,
             h_ref, n_ref, g_ref, u_ref, nb_scr, acc_scr) = refs
        f = pl.program_id(1)

        @pl.when(f == 0)
        def _():
            xv = x_ref[...]
            r = lax.rsqrt(jnp.mean(xv * xv, axis=-1, keepdims=True) + EPS)
            nb = (xv * r * gain_ref[...]).astype(BF16)
            nb_scr[...] = nb
            n_ref[...] = nb
            acc_scr[...] = jnp.zeros_like(acc_scr)

        nb = nb_scr[...]
        g = lax.dot_general(nb, wg_ref[...], NT, preferred_element_type=F32)
        u = lax.dot_general(nb, wu_ref[...], NT, preferred_element_type=F32)
        a = g * _sigmoid(g) * u
        g_ref[...] = g.astype(BF16)
        u_ref[...] = u.astype(BF16)
        acc_scr[...] += jnp.dot(a.astype(BF16), wd_ref[...], preferred_element_type=F32)

        @pl.when(f == nf - 1)
        def _():
            h = x_ref[...] + 0.5 * acc_scr[...]
            h_ref[...] = h
            if with_loss:
                e = h - t_ref[...]
                dout = e * (1.0 / D)
                dout_ref[...] = dout
                dyb_ref[...] = (0.5 * dout).astype(BF16)
                sq_ref[...] = jnp.sum(e * e, axis=0, keepdims=True)[None]

    row = pl.BlockSpec((tm, D), lambda t, f: (t, 0))
    wspec = pl.BlockSpec((tf, D), lambda t, f: (f, 0))
    gspec = pl.BlockSpec((tm, tf), lambda t, f: (t, f))
    in_specs = [row, pl.BlockSpec((1, D), lambda t, f: (0, 0)), wspec, wspec, wspec]
    out_shape = [jax.ShapeDtypeStruct((T, D), F32), jax.ShapeDtypeStruct((T, D), BF16),
                 jax.ShapeDtypeStruct((T, FF), BF16), jax.ShapeDtypeStruct((T, FF), BF16)]
    out_specs = [row, row, gspec, gspec]
    args = [x, gain, wg, wu, wd]
    if with_loss:
        in_specs.append(row)
        args.append(target)
        out_shape += [jax.ShapeDtypeStruct((T, D), F32), jax.ShapeDtypeStruct((T, D), BF16),
                      jax.ShapeDtypeStruct((nt, 1, D), F32)]
        out_specs += [row, row, pl.BlockSpec((1, 1, D), lambda t, f: (t, 0, 0))]
    return _pallas(
        body, args, grid=(nt, nf), in_specs=in_specs, out_specs=out_specs, out_shape=out_shape,
        scratch_shapes=[pltpu.VMEM((tm, D), BF16), pltpu.VMEM((tm, D), F32)],
        sem=("parallel", "arbitrary"), name=name, carry=carry)


def _ffn_bwd(dyb, nb, g, u, wg, wu, wd, name, carry=None):
    T = dyb.shape[0]
    tm, tf = 512, 256
    nt, nf = T // tm, FF // tf

    def body(dy_ref, n_ref, g_ref, u_ref, wg_ref, wu_ref, wd_ref,
             dwg_ref, dwu_ref, dwd_ref, dn_ref, ag_scr, au_scr, ad_scr):
        f, t = pl.program_id(0), pl.program_id(1)

        @pl.when(t == 0)
        def _():
            ag_scr[...] = jnp.zeros_like(ag_scr)
            au_scr[...] = jnp.zeros_like(au_scr)
            ad_scr[...] = jnp.zeros_like(ad_scr)

        dy = dy_ref[...]
        n = n_ref[...]
        gv = g_ref[...].astype(F32)
        uv = u_ref[...].astype(F32)
        da = lax.dot_general(dy, wd_ref[...], NT, preferred_element_type=F32)
        sg = _sigmoid(gv)
        silu = gv * sg
        ab = (silu * uv).astype(BF16)
        dgb = (da * uv * (sg * (1.0 + gv * (1.0 - sg)))).astype(BF16)
        dub = (da * silu).astype(BF16)
        ad_scr[...] += lax.dot_general(ab, dy, TN, preferred_element_type=F32)
        ag_scr[...] += lax.dot_general(dgb, n, TN, preferred_element_type=F32)
        au_scr[...] += lax.dot_general(dub, n, TN, preferred_element_type=F32)
        dn = (jnp.dot(dgb, wg_ref[...], preferred_element_type=F32)
              + jnp.dot(dub, wu_ref[...], preferred_element_type=F32))
        rows = pl.ds(pl.multiple_of(t * tm, tm), tm)

        @pl.when(f == 0)
        def _():
            dn_ref[rows, :] = dn

        @pl.when(f > 0)
        def _():
            dn_ref[rows, :] += dn

        @pl.when(t == nt - 1)
        def _():
            dwg_ref[...] = ag_scr[...].astype(BF16)
            dwu_ref[...] = au_scr[...].astype(BF16)
            dwd_ref[...] = ad_scr[...].astype(BF16)

    row = pl.BlockSpec((tm, D), lambda f, t: (t, 0))
    gspec = pl.BlockSpec((tm, tf), lambda f, t: (t, f))
    wspec = pl.BlockSpec((tf, D), lambda f, t: (f, 0))
    return _pallas(
        body, (dyb, nb, g, u, wg, wu, wd), grid=(nf, nt),
        in_specs=[row, row, gspec, gspec, wspec, wspec, wspec],
        out_specs=[wspec, wspec, wspec, pl.BlockSpec((T, D), lambda f, t: (0, 0))],
        out_shape=[jax.ShapeDtypeStruct((FF, D), BF16)] * 3 + [jax.ShapeDtypeStruct((T, D), F32)],
        scratch_shapes=[pltpu.VMEM((tf, D), F32)] * 3,
        sem=("arbitrary", "arbitrary"), name=name, carry=carry)


FC = 256


def _resident(shape):
    return pl.BlockSpec(shape, lambda *_: (0,) * len(shape), pipeline_mode=pl.Buffered(1))


def _mix_out_ffn_loss(h1, attn, conv, wout, gain, wg, wu, wd, target, name):
    T = h1.shape[0]
    tm = 512
    nt = T // tm

    def body(h1_ref, at_ref, cv_ref, wo_ref, gain_ref, wg_ref, wu_ref, wd_ref, t_ref,
             h2_ref, n_ref, g_ref, u_ref, dout_ref, dyb_ref, sq_ref, a_scr):
        xv = (h1_ref[...]
              + jnp.dot(at_ref[...], wo_ref[0:DA, :], preferred_element_type=F32)
              + jnp.dot(cv_ref[...], wo_ref[DA:D, :], preferred_element_type=F32))
        h2_ref[...] = xv
        r = lax.rsqrt(jnp.mean(xv * xv, axis=-1, keepdims=True) + EPS)
        n_ref[...] = (xv * r * gain_ref[...]).astype(BF16)
        for c in range(FF // FC):
            cols = slice(c * FC, (c + 1) * FC)
            nb = n_ref[...]
            g = lax.dot_general(nb, wg_ref[cols, :], NT, preferred_element_type=F32)
            u = lax.dot_general(nb, wu_ref[cols, :], NT, preferred_element_type=F32)
            g_ref[:, cols] = g.astype(BF16)
            u_ref[:, cols] = u.astype(BF16)
            a_scr[:, cols] = (g * _sigmoid(g) * u).astype(BF16)
        e = h2_ref[...] + 0.5 * jnp.dot(a_scr[...], wd_ref[...], preferred_element_type=F32) - t_ref[...]
        dout = e * (1.0 / D)
        dout_ref[...] = dout
        dyb_ref[...] = (0.5 * dout).astype(BF16)
        sq_ref[...] = jnp.sum(e * e, axis=0, keepdims=True)[None]

    row = pl.BlockSpec((tm, D), lambda t: (t, 0))
    half = pl.BlockSpec((tm, DA), lambda t: (t, 0))
    wide = pl.BlockSpec((tm, FF), lambda t: (t, 0))
    outs, _ = _pallas(
        body, (h1, attn, conv, wout, gain, wg, wu, wd, target), grid=(nt,),
        in_specs=[row, half, half, _resident((D, D)), _resident((1, D)), _resident((FF, D)), _resident((FF, D)),
                  _resident((FF, D)), row],
        out_specs=[row, row, wide, wide, row, row, pl.BlockSpec((1, 1, D), lambda t: (t, 0, 0))],
        out_shape=[jax.ShapeDtypeStruct((T, D), F32), jax.ShapeDtypeStruct((T, D), BF16)]
                  + [jax.ShapeDtypeStruct((T, FF), BF16)] * 2
                  + [jax.ShapeDtypeStruct((T, D), F32), jax.ShapeDtypeStruct((T, D), BF16),
                     jax.ShapeDtypeStruct((nt, 1, D), F32)],
        scratch_shapes=[pltpu.VMEM((tm, FF), BF16)], sem=("parallel",), name=name)
    return outs


def _ffn_gate(x, gain, wg, name, carry=None):
    T = x.shape[0]
    tm = 512

    def body(x_ref, gain_ref, wg_ref, n_ref, g_ref):
        xv = x_ref[...]
        r = lax.rsqrt(jnp.mean(xv * xv, axis=-1, keepdims=True) + EPS)
        n_ref[...] = (xv * r * gain_ref[...]).astype(BF16)
        for c in range(FF // FC):
            cols = slice(c * FC, (c + 1) * FC)
            g_ref[:, cols] = lax.dot_general(n_ref[...], wg_ref[cols, :], NT,
                                             preferred_element_type=F32).astype(BF16)

    row = pl.BlockSpec((tm, D), lambda t: (t, 0))
    wide = pl.BlockSpec((tm, FF), lambda t: (t, 0))
    return _pallas(
        body, (x, gain, wg), grid=(T // tm,), in_specs=[row, _resident((1, D)), _resident((FF, D))],
        out_specs=[row, wide], out_shape=[jax.ShapeDtypeStruct((T, D), BF16), jax.ShapeDtypeStruct((T, FF), BF16)],
        scratch_shapes=[], sem=("parallel",), name=name, carry=carry)


def _ffn_up(nb, g, wu, name, carry=None):
    T = nb.shape[0]
    tm = 512

    def body(n_ref, g_ref, wu_ref, u_ref, a_ref):
        for c in range(FF // FC):
            cols = slice(c * FC, (c + 1) * FC)
            u = lax.dot_general(n_ref[...], wu_ref[cols, :], NT, preferred_element_type=F32)
            gv = g_ref[:, cols].astype(F32)
            u_ref[:, cols] = u.astype(BF16)
            a_ref[:, cols] = (gv * _sigmoid(gv) * u).astype(BF16)

    row = pl.BlockSpec((tm, D), lambda t: (t, 0))
    wide = pl.BlockSpec((tm, FF), lambda t: (t, 0))
    return _pallas(
        body, (nb, g, wu), grid=(T // tm,), in_specs=[row, wide, _resident((FF, D))],
        out_specs=[wide, wide], out_shape=[jax.ShapeDtypeStruct((T, FF), BF16)] * 2,
        scratch_shapes=[], sem=("parallel",), name=name, carry=carry)


def _ffn_down(x, a, wd, name, carry=None):
    T = x.shape[0]
    tm = 512

    def body(x_ref, a_ref, wd_ref, h_ref):
        h_ref[...] = x_ref[...] + 0.5 * jnp.dot(a_ref[...], wd_ref[...], preferred_element_type=F32)

    row = pl.BlockSpec((tm, D), lambda t: (t, 0))
    wide = pl.BlockSpec((tm, FF), lambda t: (t, 0))
    return _pallas(
        body, (x, a, wd), grid=(T // tm,), in_specs=[row, wide, _resident((FF, D))],
        out_specs=[row], out_shape=[jax.ShapeDtypeStruct((T, D), F32)],
        scratch_shapes=[], sem=("parallel",), name=name, carry=carry)


def _ffn_bwd_act(dyb, g, u, x, dout, gain, wg, wu, wd, name, carry=None):
    T = x.shape[0]
    tm = 256
    nt = T // tm

    def body(dy_ref, g_ref, u_ref, x_ref, dout_ref, gain_ref, wg_ref, wu_ref, wd_ref,
             dg_ref, du_ref, dx_ref, dgn_ref):
        for c in range(FF // FC):
            cols = slice(c * FC, (c + 1) * FC)
            da = lax.dot_general(dy_ref[...], wd_ref[cols, :], NT, preferred_element_type=F32)
            gv = g_ref[:, cols].astype(F32)
            uv = u_ref[:, cols].astype(F32)
            sg = _sigmoid(gv)
            dg_ref[:, cols] = (da * uv * (sg * (1.0 + gv * (1.0 - sg)))).astype(BF16)
            du_ref[:, cols] = (da * (gv * sg)).astype(BF16)
        dn = (jnp.dot(dg_ref[...], wg_ref[...], preferred_element_type=F32)
              + jnp.dot(du_ref[...], wu_ref[...], preferred_element_type=F32))
        dx, dgain = _rms_bwd_rows(dn, x_ref[...], gain_ref[...])
        dx_ref[...] = dout_ref[...] + dx
        dgn_ref[...] = dgain[None]

    row = pl.BlockSpec((tm, D), lambda t: (t, 0))
    wide = pl.BlockSpec((tm, FF), lambda t: (t, 0))
    return _pallas(
        body, (dyb, g, u, x, dout, gain, wg, wu, wd), grid=(nt,),
        in_specs=[row, wide, wide, row, row, _resident((1, D)), _resident((FF, D)), _resident((FF, D)),
                  _resident((FF, D))],
        out_specs=[wide, wide, row, pl.BlockSpec((1, 1, D), lambda t: (t, 0, 0))],
        out_shape=[jax.ShapeDtypeStruct((T, FF), BF16)] * 2
                  + [jax.ShapeDtypeStruct((T, D), F32), jax.ShapeDtypeStruct((nt, 1, D), F32)],
        scratch_shapes=[], sem=("parallel",), name=name, carry=carry)


def _ffn_bwd_w(lhs, rhs, name, carry=None):
    T = rhs.shape[0]
    tf = 256
    lhs = lhs if isinstance(lhs, tuple) else (lhs,)

    def body(*refs):
        r_ref, dw_ref = refs[-2:]
        if len(lhs) == 2:
            gv = refs[0][...].astype(F32)
            lv = (gv * _sigmoid(gv) * refs[1][...].astype(F32)).astype(BF16)
        else:
            lv = refs[0][...]
        dw_ref[...] = lax.dot_general(lv, r_ref[...], TN, preferred_element_type=F32).astype(BF16)

    (dw,), got = _pallas(
        body, (*lhs, rhs), grid=(FF // tf,),
        in_specs=[pl.BlockSpec((T, tf), lambda f: (0, f))] * len(lhs) + [_resident((T, D))],
        out_specs=[pl.BlockSpec((tf, D), lambda f: (f, 0))], out_shape=[jax.ShapeDtypeStruct((FF, D), BF16)],
        scratch_shapes=[], sem=("parallel",), name=name, carry=carry)
    return dw, got


def _rms_bwd_rows(dn, xv, gain):
    r = lax.rsqrt(jnp.mean(xv * xv, axis=-1, keepdims=True) + EPS)
    xhat = xv * r
    dxhat = dn * gain
    dx = r * (dxhat - xhat * jnp.mean(dxhat * xhat, axis=-1, keepdims=True))
    return dx, jnp.sum(dn * xhat, axis=0, keepdims=True)


def _norm_bwd(dn, x, dout, gain, name):
    T = x.shape[0]
    tm = 512
    nt = T // tm

    def body(dn_ref, x_ref, dout_ref, gain_ref, dx_ref, dg_ref):
        dx, dgain = _rms_bwd_rows(dn_ref[...], x_ref[...], gain_ref[...])
        dx_ref[...] = dout_ref[...] + dx
        dg_ref[...] = dgain[None]

    row = pl.BlockSpec((tm, D), lambda t: (t, 0))
    return pl.pallas_call(
        body, grid=(nt,), in_specs=[row, row, row, pl.BlockSpec((1, D), lambda t: (0, 0))],
        out_specs=[row, pl.BlockSpec((1, 1, D), lambda t: (t, 0, 0))],
        out_shape=[jax.ShapeDtypeStruct((T, D), F32), jax.ShapeDtypeStruct((nt, 1, D), F32)],
        compiler_params=_cp(("parallel",)), name=name)(dn, x, dout, gain)


def _mix_in(h, gain, win, carry=None):
    T = h.shape[0]
    tm = 512

    def body(h_ref, gain_ref, w_ref, u_ref, n_ref):
        xv = h_ref[...]
        r = lax.rsqrt(jnp.mean(xv * xv, axis=-1, keepdims=True) + EPS)
        nb = (xv * r * gain_ref[...]).astype(BF16)
        n_ref[...] = nb
        u_ref[...] = lax.dot_general(nb, w_ref[...], NT, preferred_element_type=F32).astype(BF16)

    row = pl.BlockSpec((tm, D), lambda t: (t, 0))
    return _pallas(
        body, (h, gain, win), grid=(T // tm,),
        in_specs=[row, _resident((1, D)), _resident((DIN, D))],
        out_specs=[pl.BlockSpec((tm, DIN), lambda t: (t, 0)), row],
        out_shape=[jax.ShapeDtypeStruct((T, DIN), BF16), jax.ShapeDtypeStruct((T, D), BF16)],
        scratch_shapes=[], sem=("parallel",), name="mix_in", carry=carry)


def _mix_out(h, attn, conv, wout):
    T = h.shape[0]
    tm = 512

    def body(h_ref, a_ref, c_ref, w_ref, o_ref):
        o_ref[...] = (h_ref[...]
                      + jnp.dot(a_ref[...], w_ref[0:DA, :], preferred_element_type=F32)
                      + jnp.dot(c_ref[...], w_ref[DA:D, :], preferred_element_type=F32))

    row = pl.BlockSpec((tm, D), lambda t: (t, 0))
    half = pl.BlockSpec((tm, DA), lambda t: (t, 0))
    return pl.pallas_call(
        body, grid=(T // tm,),
        in_specs=[row, half, half, pl.BlockSpec((D, D), lambda t: (0, 0))],
        out_specs=row, out_shape=jax.ShapeDtypeStruct((T, D), F32),
        compiler_params=_cp(("parallel",)), name="mix_out")(h, attn, conv, wout)


def _mix_out_bwd(dh, attn, conv, wout):
    T = dh.shape[0]
    tm = 512
    nt = T // tm

    def body(dh_ref, a_ref, c_ref, w_ref, da_ref, dc_ref, dw_ref, acc_scr):
        t = pl.program_id(0)

        @pl.when(t == 0)
        def _():
            acc_scr[...] = jnp.zeros_like(acc_scr)

        dhb = dh_ref[...].astype(BF16)
        dmix = lax.dot_general(dhb, w_ref[...], NT, preferred_element_type=F32)
        da_ref[...] = dmix[:, 0:DA].astype(BF16)
        dc_ref[...] = dmix[:, DA:D].astype(BF16)
        acc_scr[0:DA, :] += lax.dot_general(a_ref[...], dhb, TN, preferred_element_type=F32)
        acc_scr[DA:D, :] += lax.dot_general(c_ref[...], dhb, TN, preferred_element_type=F32)

        @pl.when(t == nt - 1)
        def _():
            dw_ref[...] = acc_scr[...].astype(BF16)

    row = pl.BlockSpec((tm, D), lambda t: (t, 0))
    half = pl.BlockSpec((tm, DA), lambda t: (t, 0))
    full = pl.BlockSpec((D, D), lambda t: (0, 0))
    return pl.pallas_call(
        body, grid=(nt,), in_specs=[row, half, half, full], out_specs=[half, half, full],
        out_shape=[jax.ShapeDtypeStruct((T, DA), BF16)] * 2 + [jax.ShapeDtypeStruct((D, D), BF16)],
        scratch_shapes=[pltpu.VMEM((D, D), F32)],
        compiler_params=_cp(("arbitrary",)), name="mix_out_bwd")(dh, attn, conv, wout)


def _mix_in_bwd(dparts, win, nb, h, dh, gain):
    T = h.shape[0]
    tm = 512
    nt = T // tm

    def body(d0, d1, d2, d3, d4, w_ref, n_ref, h_ref, dh_ref, gain_ref,
             dw_ref, dx_ref, dyb_ref, dg_ref, acc_scr):
        t = pl.program_id(0)

        @pl.when(t == 0)
        def _():
            acc_scr[...] = jnp.zeros_like(acc_scr)

        n = n_ref[...]
        dn = jnp.zeros((tm, D), F32)
        for i, d_ref in enumerate((d0, d1, d2, d3, d4)):
            dv = d_ref[...]
            dn = dn + jnp.dot(dv, w_ref[i * DA:(i + 1) * DA, :], preferred_element_type=F32)
            acc_scr[i * DA:(i + 1) * DA, :] += lax.dot_general(dv, n, TN, preferred_element_type=F32)
        dx, dgain = _rms_bwd_rows(dn, h_ref[...], gain_ref[...])
        tot = dh_ref[...] + dx
        dx_ref[...] = tot
        dyb_ref[...] = (0.5 * tot).astype(BF16)
        dg_ref[...] = dgain[None]

        @pl.when(t == nt - 1)
        def _():
            dw_ref[...] = acc_scr[...].astype(BF16)

    row = pl.BlockSpec((tm, D), lambda t: (t, 0))
    half = pl.BlockSpec((tm, DA), lambda t: (t, 0))
    full = pl.BlockSpec((DIN, D), lambda t: (0, 0))
    return pl.pallas_call(
        body, grid=(nt,),
        in_specs=[half] * 5 + [full, row, row, row, pl.BlockSpec((1, D), lambda t: (0, 0))],
        out_specs=[full, row, row, pl.BlockSpec((1, 1, D), lambda t: (t, 0, 0))],
        out_shape=[jax.ShapeDtypeStruct((DIN, D), BF16), jax.ShapeDtypeStruct((T, D), F32),
                   jax.ShapeDtypeStruct((T, D), BF16), jax.ShapeDtypeStruct((nt, 1, D), F32)],
        scratch_shapes=[pltpu.VMEM((DIN, D), F32)],
        compiler_params=_cp(("arbitrary",)), name="mix_in_bwd")(*dparts, win, nb, h, dh, gain)


def _head_masks():
    lane = lax.broadcasted_iota(jnp.int32, (1, 2 * HD), 1)
    m0 = lane < HD
    return m0, jnp.logical_not(m0)


def _stack_heads(v, m0, m1):
    z = jnp.zeros_like(v)
    return jnp.concatenate([jnp.where(m0, v, z), jnp.where(m1, v, z)], axis=0)


def _unstack_heads(v2, m0):
    return jnp.where(m0, v2[0:BLK], v2[BLK:2 * BLK])


def _head_sums(xv):
    ri = lax.broadcasted_iota(jnp.int32, (2 * HD, 2 * HD), 0)
    ci = lax.broadcasted_iota(jnp.int32, (2 * HD, 2 * HD), 1)
    ones = jnp.where((ri < HD) == (ci < HD), 1.0, 0.0).astype(BF16)
    hi = xv.astype(BF16)
    lo = (xv - hi.astype(F32)).astype(BF16)
    return (jnp.dot(hi, ones, preferred_element_type=F32) + jnp.dot(lo, ones, preferred_element_type=F32))


def _head_rms(xv):
    return lax.rsqrt(_head_sums(xv * xv) * (1.0 / HD) + EPS)


def _band_mask(first):
    qi = lax.broadcasted_iota(jnp.int32, (BLK, 2 * BLK), 0)
    ci = lax.broadcasted_iota(jnp.int32, (BLK, 2 * BLK), 1)
    band = (ci >= qi) & (ci <= qi + BLK)
    return band & ((ci >= BLK) | jnp.logical_not(first))


def _block_rows(j, d, seg):
    r, n = j // seg, j % seg
    start = r + (d * BLK) * n
    first = n == 0
    prev = jnp.where(first, start, start - d * BLK)
    return pl.ds(start, BLK, stride=d), pl.ds(prev, BLK, stride=d), first


def _block_keys(refs, cur, prev, first, single):
    if single:
        qi = lax.broadcasted_iota(jnp.int32, (BLK, BLK), 0)
        ci = lax.broadcasted_iota(jnp.int32, (BLK, BLK), 1)
        return [r[cur, :].astype(BF16) for r in refs], ci <= qi
    return ([jnp.concatenate([r[prev, :], r[cur, :]], axis=0).astype(BF16) for r in refs], _band_mask(first))


def _attn_fwd(u, qg2, kg2, B, S, carry=None):
    T = B * S
    NB = S // BLK
    scale = HD ** -0.5

    def body(q_ref, k_ref, v_ref, qg_ref, kg_ref, o_ref, lse_ref, qn, kn, vn, os_, ls_):
        m0, m1 = _head_masks()
        qv = q_ref[...].astype(F32)
        qn[...] = qv * _head_rms(qv) * (qg_ref[...] * scale)
        kv = k_ref[...].astype(F32)
        kn[...] = kv * _head_rms(kv) * kg_ref[...]
        vn[...] = v_ref[...].astype(F32)

        for i, d in enumerate(DILS):
            seg = NB // d

            def blk(j, c, i=i, d=d, seg=seg):
                cur, prev, first = _block_rows(j, d, seg)
                q2 = _stack_heads(qn[cur, :].astype(BF16), m0, m1)
                (kk, vv), mask = _block_keys((kn, vn), cur, prev, first, False)
                s = lax.dot_general(q2, kk, NT, preferred_element_type=F32)
                s = jnp.where(jnp.concatenate([mask, mask], axis=0), s, -1e30)
                mx = jnp.max(s, axis=-1, keepdims=True)
                p = jnp.exp(s - mx)
                l = jnp.sum(p, axis=-1, keepdims=True)
                o2 = jnp.dot((p * (1.0 / l)).astype(BF16), vv, preferred_element_type=F32)
                os_[i, cur, :] = _unstack_heads(o2, m0)
                ls_[i, cur, :] = _unstack_heads(mx + jnp.log(l), m0)
                return c

            lax.fori_loop(0, NB, blk, 0, unroll=8)

        def comb(c, carry):
            rows = pl.ds(pl.multiple_of(c * 256, 256), 256)
            l0, l1, l2 = ls_[0, rows, :], ls_[1, rows, :], ls_[2, rows, :]
            mx = jnp.maximum(jnp.maximum(l0, l1), l2)
            e0, e1, e2 = jnp.exp(l0 - mx), jnp.exp(l1 - mx), jnp.exp(l2 - mx)
            tot = e0 + e1 + e2
            inv = 1.0 / tot
            o = (e0 * os_[0, rows, :] + e1 * os_[1, rows, :] + e2 * os_[2, rows, :]) * inv
            o_ref[rows, :] = o.astype(BF16)
            lse_ref[rows, :] = mx + jnp.log(tot)
            return carry

        lax.fori_loop(0, S // 256, comb, 0)

    pair = 2 * HD
    blk_spec = lambda off: pl.BlockSpec((S, pair), lambda b, p, off=off: (b, off + p))
    gspec = pl.BlockSpec((1, pair), lambda b, p: (0, 0))
    return _pallas(
        body, (u, u, u, qg2, kg2), grid=(B, DA // pair),
        in_specs=[blk_spec(0), blk_spec(DA // pair), blk_spec(2 * DA // pair), gspec, gspec],
        out_specs=[blk_spec(0), blk_spec(0)],
        out_shape=[jax.ShapeDtypeStruct((T, DA), BF16), jax.ShapeDtypeStruct((T, DA), F32)],
        scratch_shapes=[pltpu.VMEM((S, pair), F32)] * 3 + [pltpu.VMEM((3, S, pair), F32)] * 2,
        sem=("parallel", "parallel"), name="attn_fwd", carry=carry)


def _attn_bwd(u, attn, dattn, lse, qg2, kg2, B, S, carry=None):
    T = B * S
    NB = S // BLK
    scale = HD ** -0.5
    pair = 2 * HD

    def body(q_ref, k_ref, v_ref, o_ref, do_ref, lse_ref, qg_ref, kg_ref,
             dq_ref, dk_ref, dv_ref, dgn_ref,
             qn, kn, vn, don, ldl, accq, acck, accv):
        m0, m1 = _head_masks()
        lane = lax.broadcasted_iota(jnp.int32, (1, pair), 1)
        qv = q_ref[...].astype(F32)
        qn[...] = qv * _head_rms(qv) * (qg_ref[...] * scale)
        kv = k_ref[...].astype(F32)
        kn[...] = kv * _head_rms(kv) * kg_ref[...]
        vn[...] = v_ref[...].astype(F32)
        dov = do_ref[...].astype(F32)
        don[...] = dov
        ldl[...] = jnp.where((lane % HD) < HD // 2, lse_ref[...], _head_sums(dov * o_ref[...].astype(F32)))
        accq[...] = jnp.zeros_like(accq)
        acck[...] = jnp.zeros_like(acck)
        accv[...] = jnp.zeros_like(accv)

        for i, d in enumerate(DILS):
            seg = NB // d

            def blk(j, c, d=d, seg=seg):
                cur, prev, first = _block_rows(j, d, seg)
                q2 = _stack_heads(qn[cur, :].astype(BF16), m0, m1)
                do2 = _stack_heads(don[cur, :].astype(BF16), m0, m1)
                (kk, vv), mask = _block_keys((kn, vn), cur, prev, first, seg == 1)
                ldv = ldl[cur, :]
                lse2 = jnp.concatenate([ldv[:, 0:1], ldv[:, HD:HD + 1]], axis=0)
                dl2 = jnp.concatenate([ldv[:, HD // 2:HD // 2 + 1], ldv[:, HD + HD // 2:HD + HD // 2 + 1]], axis=0)
                s = lax.dot_general(q2, kk, NT, preferred_element_type=F32)
                p = jnp.where(jnp.concatenate([mask, mask], axis=0), jnp.exp(s - lse2), 0.0)
                dp = lax.dot_general(do2, vv, NT, preferred_element_type=F32)
                ds = (p * (dp - dl2)).astype(BF16)
                dq_acc = _unstack_heads(jnp.dot(ds, kk, preferred_element_type=F32), m0)
                dk_acc = lax.dot_general(ds, q2, TN, preferred_element_type=F32)
                dv_acc = lax.dot_general(p.astype(BF16), do2, TN, preferred_element_type=F32)
                accq[cur, :] += dq_acc
                if seg == 1:
                    acck[cur, :] += dk_acc
                    accv[cur, :] += dv_acc
                else:
                    acck[prev, :] += dk_acc[0:BLK]
                    acck[cur, :] += dk_acc[BLK:2 * BLK]
                    accv[prev, :] += dv_acc[0:BLK]
                    accv[cur, :] += dv_acc[BLK:2 * BLK]
                return c

            lax.fori_loop(0, NB, blk, 0, unroll=4)

        def norm_bwd(x_ref, dn, gain):
            xv = x_ref[...].astype(F32)
            r = _head_rms(xv)
            xhat = xv * r
            dxhat = dn * gain
            dx = r * (dxhat - xhat * (_head_sums(dxhat * xhat) * (1.0 / HD)))
            return dx, jnp.sum(dn * xhat, axis=0, keepdims=True)

        dq, dgq = norm_bwd(q_ref, accq[...], qg_ref[...] * scale)
        dk, dgk = norm_bwd(k_ref, acck[...], kg_ref[...])
        dq_ref[...] = dq.astype(BF16)
        dk_ref[...] = dk.astype(BF16)
        dv_ref[...] = accv[...].astype(BF16)
        dgn_ref[...] = jnp.concatenate([dgq * scale, dgk, jnp.zeros((6, pair), F32)], axis=0)[None]

    blk_spec = lambda off: pl.BlockSpec((S, pair), lambda b, p, off=off: (b, off + p))
    gspec = pl.BlockSpec((1, pair), lambda b, p: (0, 0))
    np_ = DA // pair
    return _pallas(
        body, (u, u, u, attn, dattn, lse, qg2, kg2), grid=(B, np_),
        in_specs=[blk_spec(0), blk_spec(np_), blk_spec(2 * np_), blk_spec(0), blk_spec(0), blk_spec(0),
                  gspec, gspec],
        out_specs=[blk_spec(0), blk_spec(0), blk_spec(0),
                   pl.BlockSpec((1, 8, pair), lambda b, p: (b * np_ + p, 0, 0))],
        out_shape=[jax.ShapeDtypeStruct((T, DA), BF16)] * 3 + [jax.ShapeDtypeStruct((B * np_, 8, pair), F32)],
        scratch_shapes=[pltpu.VMEM((S, pair), F32)] * 8,
        sem=("parallel", "parallel"), name="attn_bwd", carry=carry)


CT = 32
CPAD = 32


def _shifted(win, offsets):
    rolled, out = {}, {}
    n = win.shape[0]
    for o in offsets:
        sub = o % 8
        if sub not in rolled:
            rolled[sub] = win if sub == 0 else pltpu.roll(win, n - sub, 0)
        out[o] = rolled[sub][o - sub:o - sub + CT, :]
    return out


def _ln_fwd(y, g, b):
    mu = jnp.mean(y, axis=-1, keepdims=True)
    yc = y - mu
    rstd = lax.rsqrt(jnp.mean(yc * yc, axis=-1, keepdims=True) + EPS)
    xhat = yc * rstd
    return xhat, rstd, xhat * g + b


def _fill_glu(ca_ref, cg_ref, glu, S):
    glu[pl.ds(0, CPAD), :] = jnp.zeros((CPAD, DC), F32)

    def fill(i, c):
        rows = pl.ds(pl.multiple_of(i * 256, 256), 256)
        a = ca_ref[rows, :].astype(F32)
        gt = cg_ref[rows, :].astype(F32)
        glu[pl.ds(pl.multiple_of(CPAD + i * 256, CT), 256), :] = a * _sigmoid(gt)
        return c

    lax.fori_loop(0, S // 256, fill, 0)


def _conv_fwd(u, cw, cb, lg, lb, B, S):
    T = B * S

    def body(ca_ref, cg_ref, w_ref, b_ref, lg_ref, lb_ref, o_ref, y_ref, glu):
        _fill_glu(ca_ref, cg_ref, glu, S)

        def step(i, c):
            t0 = pl.multiple_of(i * CT, CT)
            win = glu[pl.ds(t0, 2 * CT), :]
            acc = jnp.zeros((CT, DC), F32) + b_ref[...]
            taps = _shifted(win, [k + 2 for k in range(CK)])
            for k in range(CK):
                acc = acc + taps[k + 2] * w_ref[k:k + 1, :]
            y_ref[pl.ds(t0, CT), :] = acc
            _, _, z = _ln_fwd(acc, lg_ref[...], lb_ref[...])
            o_ref[pl.ds(t0, CT), :] = (z * _sigmoid(z)).astype(BF16)
            return c

        lax.fori_loop(0, S // CT, step, 0, unroll=2)

    vec = pl.BlockSpec((1, DC), lambda b: (0, 0))
    return pl.pallas_call(
        body, grid=(B,),
        in_specs=[pl.BlockSpec((S, DC), lambda b: (b, 3)), pl.BlockSpec((S, DC), lambda b: (b, 4)),
                  pl.BlockSpec((CT, DC), lambda b: (0, 0)), vec, vec, vec],
        out_specs=[pl.BlockSpec((S, DC), lambda b: (b, 0))] * 2,
        out_shape=[jax.ShapeDtypeStruct((T, DC), BF16), jax.ShapeDtypeStruct((T, DC), F32)],
        scratch_shapes=[pltpu.VMEM((CPAD + S, DC), F32)],
        compiler_params=_cp(("parallel",)), name="conv_fwd")(u, u, cw, cb, lg, lb)


def _conv_bwd(u, y, dconv, cw, lg, lb, B, S):
    T = B * S

    def body(ca_ref, cg_ref, y_ref, dc_ref, w_ref, lg_ref, lb_ref,
             dca_ref, dcg_ref, dw_ref, ds_ref, glu, dyp, dwacc):
        _fill_glu(ca_ref, cg_ref, glu, S)
        dyp[pl.ds(S, CPAD), :] = jnp.zeros((CPAD, DC), F32)
        lgv, lbv = lg_ref[...], lb_ref[...]

        def sum8(v):
            return functools.reduce(jnp.add, [v[r:r + 8] for r in range(0, v.shape[0], 8)])

        P1 = 4 * CT

        def p1(i, carry):
            sb, sg, sl = carry
            t0 = pl.multiple_of(i * P1, P1)
            xhat, rstd, z = _ln_fwd(y_ref[pl.ds(t0, P1), :], lgv, lbv)
            sz = _sigmoid(z)
            dz = dc_ref[pl.ds(t0, P1), :].astype(F32) * (sz * (1.0 + z * (1.0 - sz)))
            dxhat = dz * lgv
            dy = rstd * (dxhat - jnp.mean(dxhat, axis=-1, keepdims=True)
                         - xhat * jnp.mean(dxhat * xhat, axis=-1, keepdims=True))
            dyp[pl.ds(t0, P1), :] = dy
            return sb + sum8(dy), sg + sum8(dz * xhat), sl + sum8(dz)

        z8 = jnp.zeros((8, DC), F32)
        sb, sg, sl = lax.fori_loop(0, S // P1, p1, (z8, z8, z8))
        rs = lambda v: jnp.sum(v, axis=0, keepdims=True)
        ds_ref[...] = jnp.concatenate([rs(sb), rs(sg), rs(sl), jnp.zeros((5, DC), F32)], axis=0)[None]

        def p2(i, c):
            t0 = pl.multiple_of(i * CT, CT)
            win = dyp[pl.ds(t0, 2 * CT), :]
            acc = jnp.zeros((CT, DC), F32)
            taps = _shifted(win, [30 - k for k in range(CK)])
            for k in range(CK):
                acc = acc + taps[30 - k] * w_ref[k:k + 1, :]
            a = ca_ref[pl.ds(t0, CT), :].astype(F32)
            sgt = _sigmoid(cg_ref[pl.ds(t0, CT), :].astype(F32))
            dca_ref[pl.ds(t0, CT), :] = (acc * sgt).astype(BF16)
            dcg_ref[pl.ds(t0, CT), :] = (acc * a * sgt * (1.0 - sgt)).astype(BF16)
            return c

        lax.fori_loop(0, S // CT, p2, 0)

        dwacc[...] = jnp.zeros_like(dwacc)

        def p3(i, c):
            t0 = pl.multiple_of(i * CT, CT)
            win = glu[pl.ds(t0, 2 * CT), :]
            dy = dyp[pl.ds(t0, CT), :]
            for k in range(CK):
                dwacc[k] += sum8(dy * win[k + 2:k + 2 + CT, :])
            return c

        lax.fori_loop(0, S // CT, p3, 0)
        dw_ref[...] = jnp.sum(dwacc[...], axis=1)[None]

    vec = pl.BlockSpec((1, DC), lambda b: (0, 0))
    seq = pl.BlockSpec((S, DC), lambda b: (b, 0))
    return pl.pallas_call(
        body, grid=(B,),
        in_specs=[pl.BlockSpec((S, DC), lambda b: (b, 3)), pl.BlockSpec((S, DC), lambda b: (b, 4)),
                  seq, seq, pl.BlockSpec((CT, DC), lambda b: (0, 0)), vec, vec],
        out_specs=[seq, seq, pl.BlockSpec((1, CT, DC), lambda b: (b, 0, 0)),
                   pl.BlockSpec((1, 8, DC), lambda b: (b, 0, 0))],
        out_shape=[jax.ShapeDtypeStruct((T, DC), BF16)] * 2
                  + [jax.ShapeDtypeStruct((B, CT, DC), F32), jax.ShapeDtypeStruct((B, 8, DC), F32)],
        scratch_shapes=[pltpu.VMEM((CPAD + S, DC), F32), pltpu.VMEM((S + CPAD, DC), F32),
                        pltpu.VMEM((CT, 8, DC), F32)],
        compiler_params=_cp(("parallel",)), name="conv_bwd")(u, u, y, dconv, cw, lg, lb)


def _local_step(x, target, norms, W, B, S, comm=None):
    qg2 = jnp.concatenate([norms["q_norm"], norms["q_norm"]], axis=1)
    kg2 = jnp.concatenate([norms["k_norm"], norms["k_norm"]], axis=1)
    cw = jnp.concatenate([W["conv_w"], jnp.zeros((1, DC), F32)], axis=0)

    W = dict(W)
    for kern, tag in (("gate", "wu1"), ("up", "wd1"), ("down", "win")):
        carry = comm.gathers[tag] if comm else None
        if kern == "gate":
            (n1, g1), got = _ffn_gate(x, norms["ffn1_norm"], W["wg1"], "ffn1_gate", carry=carry)
        elif kern == "up":
            (u1, act1), got = _ffn_up(n1, g1, W["wu1"], "ffn1_up", carry=carry)
        else:
            (h1,), got = _ffn_down(x, act1, W["wd1"], "ffn1_down", carry=carry)
        if comm:
            W.update(comm.gathered(tag, got))
    (u, n2), got = _mix_in(h1, norms["mix_norm"], W["win"], carry=comm.gathers["wout"] if comm else None)
    if comm:
        W.update(comm.gathered("wout", got))
    (attn, lse), got = _attn_fwd(u, qg2, kg2, B, S, carry=comm.gathers["ffn2"] if comm else None)
    if comm:
        W = dict(W, **comm.gathered("ffn2", got))
    conv, y = _conv_fwd(u, cw, norms["conv_b"], norms["conv_ln_g"], norms["conv_ln_b"], B, S)
    h2, n3, g2, u2, dout, dyb, sq = _mix_out_ffn_loss(h1, attn, conv, W["wout"], norms["ffn2_norm"],
                                                      W["wg2"], W["wu2"], W["wd2"], target, "ffn2_fwd")
    loss = (0.5 / D) * jnp.sum(sq)

    (dg2, du2, dh2, dgn_ffn2), _ = _ffn_bwd_act(dyb, g2, u2, h2, dout, norms["ffn2_norm"],
                                               W["wg2"], W["wu2"], W["wd2"], "ffn2_bwd_act")
    dwd2, _ = _ffn_bwd_w((g2, u2), dyb, "ffn2_bwd_wd")
    dwg2, _ = _ffn_bwd_w(dg2, n3, "ffn2_bwd_wg")
    dwu2, _ = _ffn_bwd_w(du2, n3, "ffn2_bwd_wu")
    dattn, dconv, dwout = _mix_out_bwd(dh2, attn, conv, W["wout"])
    carry = comm.reduce_start("ffn2", {"wg2": dwg2, "wu2": dwu2, "wd2": dwd2, "wout": dwout}) if comm else None
    (dq, dk, dv, dgn_qk), got = _attn_bwd(u, attn, dattn, lse, qg2, kg2, B, S, carry=carry)
    if comm:
        comm.reduce_done(carry, got)
    dca, dcg, dcw, dcs = _conv_bwd(u, y, dconv, cw, norms["conv_ln_g"], norms["conv_ln_b"], B, S)
    dwin, dh1, dyb1, dgn_mix = _mix_in_bwd((dq, dk, dv, dca, dcg), W["win"], n2, h1, dh2, norms["mix_norm"])
    (dg1, du1, gx, dgn_ffn1), _ = _ffn_bwd_act(dyb1, g1, u1, x, dh1, norms["ffn1_norm"],
                                              W["wg1"], W["wu1"], W["wd1"], "ffn1_bwd_act")
    carry = comm.reduce_start("win", {"win": dwin}) if comm else None
    dwd1, got = _ffn_bwd_w(act1, dyb1, "ffn1_bwd_wd", carry=carry)
    if comm:
        comm.reduce_done(carry, got)
        carry = comm.reduce_start("wd1", {"wd1": dwd1})
    dwg1, got = _ffn_bwd_w(dg1, n1, "ffn1_bwd_wg", carry=carry)
    if comm:
        comm.reduce_done(carry, got)
        carry = comm.reduce_start("wg1", {"wg1": dwg1})
    dwu1, got = _ffn_bwd_w(du1, n1, "ffn1_bwd_wu", carry=carry)
    if comm:
        comm.reduce_done(carry, got)
        comm.reduce_now("wu1", {"wu1": dwu1})

    qk = jnp.sum(dgn_qk, axis=0)
    cs = jnp.sum(dcs, axis=0)
    small = {
        "ffn1_norm": jnp.sum(dgn_ffn1, axis=0),
        "mix_norm": jnp.sum(dgn_mix, axis=0),
        "q_norm": qk[0:1, 0:HD] + qk[0:1, HD:2 * HD],
        "k_norm": qk[1:2, 0:HD] + qk[1:2, HD:2 * HD],
        "conv_w": jnp.sum(dcw, axis=0)[0:CK],
        "conv_b": cs[0:1],
        "conv_ln_g": cs[1:2],
        "conv_ln_b": cs[2:3],
        "ffn2_norm": jnp.sum(dgn_ffn2, axis=0),
    }
    big = {"wg1": dwg1, "wu1": dwu1, "wd1": dwd1, "win": dwin, "wout": dwout,
           "wg2": dwg2, "wu2": dwu2, "wd2": dwd2}
    return loss, gx, big, small


HBM = pl.BlockSpec(memory_space=pltpu.HBM)
VMEM = pl.BlockSpec(memory_space=pltpu.VMEM)


def _place():
    return lax.axis_index("x"), lax.axis_index("y"), lax.axis_index("c")


class _GatherCarry:
    def __init__(self, shards):
        nt = len(shards)
        self.shards = shards
        self.in_arrays = [s for s, _ in shards]
        self.in_specs = [VMEM] * nt
        self.out_shape = [jax.ShapeDtypeStruct((NDEV * s.shape[0], s.shape[1]), dt) for s, dt in shards]
        self.out_specs = [HBM] * nt
        self.scratch = ([pltpu.VMEM(s.shape, dt) for s, dt in shards]
                        + [pltpu.SemaphoreType.DMA((nt, 7)), pltpu.SemaphoreType.DMA((nt, 7)),
                           pltpu.SemaphoreType.DMA((nt,))])

    def _copies(self, outs, scr):
        nt = len(self.shards)
        stages = scr[:nt]
        send_sems, recv_sems, local_sems = scr[nt:]
        x, y, c = _place()
        me, sibling = (x, y, c), (x, y, 1 - c)
        xn, yn, diag = (1 - x, y, c), (x, 1 - y, c), (1 - x, 1 - y, c)
        via = (x ^ c, y ^ (1 - c), c)
        onto = (x ^ (1 - c), y ^ c, c)

        def rows(t, px, py, pc):
            r = self.shards[t][0].shape[0]
            return outs[t].at[pl.ds((4 * px + 2 * py + pc) * r, r), :]

        def copy(t, k, block, to, src=None):
            return pltpu.make_async_remote_copy(
                src_ref=rows(t, *block) if src is None else src, dst_ref=rows(t, *block),
                send_sem=send_sems.at[t, k], recv_sem=recv_sems.at[t, k],
                device_id=to, device_id_type=MESH)

        sib = lambda b: (b[0], b[1], 1 - c)
        return dict(
            local=[pltpu.make_async_copy(stages[t], rows(t, *me), local_sems.at[t]) for t in range(nt)],
            own=[[copy(t, 0, me, sibling, src=stages[t]), copy(t, 1, me, xn, src=stages[t]),
                  copy(t, 2, me, yn, src=stages[t])] for t in range(nt)],
            relay=[copy(t, 3, via, onto) for t in range(nt)],
            down=[[copy(t, 4, xn, sibling), copy(t, 5, yn, sibling)] for t in range(nt)],
            down_diag=[copy(t, 6, diag, sibling) for t in range(nt)],
            got_xy=[[copy(t, 1, xn, me), copy(t, 2, yn, me)] for t in range(nt)],
            got_diag=[copy(t, 3, diag, me) for t in range(nt)],
            got_sib=[[copy(t, 0, sibling, me), copy(t, 4, sib(xn), me), copy(t, 5, sib(yn), me),
                      copy(t, 6, sib(diag), me)] for t in range(nt)])

    def start(self, ins, outs, scr):
        cps = self._copies(outs, scr)
        for t, (_, dt) in enumerate(self.shards):
            scr[t][...] = ins[t][...].astype(dt)
            for cp in [cps["local"][t]] + cps["own"][t]:
                cp.start()

    def mid(self, ins, outs, scr):
        cps = self._copies(outs, scr)
        for t in range(len(self.shards)):
            for cp in cps["got_xy"][t]:
                cp.wait_recv()
            for cp in [cps["relay"][t]] + cps["down"][t]:
                cp.start()

    def finish(self, ins, outs, scr):
        cps = self._copies(outs, scr)
        for t in range(len(self.shards)):
            cps["got_diag"][t].wait_recv()
            cps["down_diag"][t].start()
        for t in range(len(self.shards)):
            for cp in cps["got_sib"][t]:
                cp.wait_recv()
            for cp in cps["own"][t] + [cps["relay"][t]] + cps["down"][t] + [cps["down_diag"][t]]:
                cp.wait_send()
            cps["local"][t].wait()


def _run_carry(carry, name):
    def body(*refs):
        n_in, n_out = len(carry.in_arrays), len(carry.out_shape)
        ins, outs, scr = refs[:n_in], refs[n_in:n_in + n_out], refs[n_in + n_out:]
        carry.start(ins, outs, scr)
        if hasattr(carry, "mid"):
            carry.mid(ins, outs, scr)
        carry.finish(ins, outs, scr)

    return pl.pallas_call(
        body, in_specs=carry.in_specs, out_specs=carry.out_specs, out_shape=carry.out_shape,
        scratch_shapes=carry.scratch, compiler_params=pltpu.CompilerParams(vmem_limit_bytes=VMEM_LIMIT),
        name=name)(*carry.in_arrays)


def _sibling_reduce(grads, name):
    nt = len(grads)
    g4 = [g.reshape(4, 2, g.shape[0] // NDEV, g.shape[1]) for g in grads]

    def body(*refs):
        ins, outs = refs[:nt], refs[nt:2 * nt]
        recv, own = refs[2 * nt:3 * nt], refs[3 * nt:4 * nt]
        send_sems, recv_sems, load_sems, store_sems = refs[4 * nt:]
        x, y, c = _place()
        sends = [pltpu.make_async_remote_copy(
            src_ref=ins[t].at[:, 1 - c], dst_ref=recv[t], send_sem=send_sems.at[t], recv_sem=recv_sems.at[t],
            device_id=(x, y, 1 - c), device_id_type=MESH) for t in range(nt)]
        loads = [pltpu.make_async_copy(ins[t].at[:, c], own[t], load_sems.at[t]) for t in range(nt)]
        stores = [pltpu.make_async_copy(own[t], outs[t], store_sems.at[t]) for t in range(nt)]
        for cp in sends + loads:
            cp.start()
        for t in range(nt):
            loads[t].wait()
            sends[t].wait_recv()
            for q in range(4):
                own[t][q] = (own[t][q].astype(F32) + recv[t][q].astype(F32)).astype(BF16)
            stores[t].start()
        for t in range(nt):
            sends[t].wait_send()
            stores[t].wait()

    blocks = [(4,) + g.shape[2:] for g in g4]
    return pl.pallas_call(
        body, in_specs=[HBM] * nt, out_specs=[HBM] * nt,
        out_shape=[jax.ShapeDtypeStruct(b, BF16) for b in blocks],
        scratch_shapes=[pltpu.VMEM(b, BF16) for b in blocks] * 2 + [pltpu.SemaphoreType.DMA((nt,))] * 4,
        compiler_params=pltpu.CompilerParams(vmem_limit_bytes=VMEM_LIMIT), name=name)(*g4)


class _ExchangeCarry:
    def __init__(self, names, parts):
        nt = len(parts)
        self.names = names
        self.in_arrays = list(parts)
        self.in_specs = [HBM] * nt
        self.out_shape = [jax.ShapeDtypeStruct((3,) + p.shape[1:], BF16) for p in parts]
        self.out_specs = [HBM] * nt
        self.scratch = ([pltpu.VMEM(p.shape[1:], BF16) for p in parts] * 2
                        + [pltpu.SemaphoreType.DMA((nt, 3)), pltpu.SemaphoreType.DMA((nt, 3)),
                           pltpu.SemaphoreType.DMA((nt,)), pltpu.SemaphoreType.DMA((nt,))])

    def _copies(self, ins, outs, scr):
        nt = len(ins)
        relayed, mine = scr[:nt], scr[nt:2 * nt]
        send_sems, recv_sems, local_sems, load_sems = scr[2 * nt:]
        x, y, c = _place()
        q = lambda cx, cy: 2 * cx + cy
        near, far = (x ^ c, y ^ (1 - c)), (x ^ (1 - c), y ^ c)

        def remote(t, k, src, dst, chip):
            return pltpu.make_async_remote_copy(
                src_ref=src, dst_ref=dst, send_sem=send_sems.at[t, k], recv_sem=recv_sems.at[t, k],
                device_id=(*chip, c), device_id_type=MESH)

        return dict(
            keep=[pltpu.make_async_copy(ins[t].at[q(x, y)], outs[t].at[0], local_sems.at[t]) for t in range(nt)],
            load=[pltpu.make_async_copy(ins[t].at[q(*far)], mine[t], load_sems.at[t]) for t in range(nt)],
            direct=[remote(t, 0, ins[t].at[q(*near)], outs[t].at[1], near) for t in range(nt)],
            relay=[remote(t, 1, ins[t].at[q(1 - x, 1 - y)], relayed[t], near) for t in range(nt)],
            merged=[remote(t, 2, mine[t], outs[t].at[2], far) for t in range(nt)],
            relayed=relayed, mine=mine)

    def start(self, ins, outs, scr):
        cps = self._copies(ins, outs, scr)
        for t in range(len(ins)):
            for kind in ("keep", "load", "direct", "relay"):
                cps[kind][t].start()

    def mid(self, ins, outs, scr):
        cps = self._copies(ins, outs, scr)
        for t in range(len(ins)):
            cps["load"][t].wait()
            cps["relay"][t].wait_recv()
            cps["mine"][t][...] = (cps["mine"][t][...].astype(F32) + cps["relayed"][t][...].astype(F32)).astype(BF16)
            cps["merged"][t].start()

    def finish(self, ins, outs, scr):
        cps = self._copies(ins, outs, scr)
        for t in range(len(ins)):
            cps["direct"][t].wait()
            cps["relay"][t].wait_send()
            cps["merged"][t].wait()
            cps["keep"][t].wait()


class _Comm:
    def __init__(self, groups):
        self.names = {tag: list(g) for tag, g in groups.items()}
        self.gathers = {tag: _GatherCarry(list(g.values())) for tag, g in groups.items()}
        self.reduced = {}

    def gathered(self, tag, outs):
        return dict(zip(self.names[tag], outs))

    def reduce_start(self, tag, grads):
        names = list(grads)
        return _ExchangeCarry(names, _sibling_reduce([grads[n] for n in names], "sibling_reduce_" + tag))

    def reduce_done(self, carry, outs):
        self.reduced.update(zip(carry.names, outs))

    def reduce_now(self, tag, grads):
        carry = self.reduce_start(tag, grads)
        self.reduce_done(carry, _run_carry(carry, "chip_exchange_" + tag))


def _adamw_math(w, g, m, v):
    m = B1 * m + (1.0 - B1) * g
    v = B2 * v + (1.0 - B2) * (g * g)
    m_hat = m / (1.0 - B1 ** STEP)
    v_hat = v / (1.0 - B2 ** STEP)
    delta = -LR * (m_hat / (jnp.sqrt(v_hat) + AEPS) + WD * w)
    return delta, m, v


def _adamw_big(recv, w, m, v, name):
    def body(r_ref, w_ref, m_ref, v_ref, g_ref, d_ref, mo_ref, vo_ref):
        g = r_ref[0].astype(F32)
        for q in range(1, 3):
            g = g + r_ref[q].astype(F32)
        d, mn, vn = _adamw_math(w_ref[...], g, m_ref[...], v_ref[...])
        g_ref[...] = g
        d_ref[...] = d
        mo_ref[...] = mn
        vo_ref[...] = vn

    rows, n = w.shape
    tr = rows // 2
    row = pl.BlockSpec((tr, n), lambda t: (t, 0))
    return pl.pallas_call(
        body, grid=(2,), in_specs=[pl.BlockSpec((3, tr, n), lambda t: (0, t, 0)), row, row, row],
        out_specs=[row] * 4, out_shape=[jax.ShapeDtypeStruct(w.shape, F32)] * 4,
        compiler_params=_cp(("parallel",)), name=name)(recv, w, m, v)


SMALL_NAMES = ("ffn1_norm", "mix_norm", "ffn2_norm", "conv_b", "conv_ln_g", "conv_ln_b", "q_norm", "k_norm")
SROWS = 16
LOSS_ROW = len(SMALL_NAMES)
CWF = 2


def _small_step(gs, loss_row, gcw, ws, ms, vs, wcw, mcw, vcw):
    ns = len(SMALL_NAMES)
    widths = [g.shape[1] for g in gs]

    def body(*refs):
        it = iter(refs)
        take = lambda n: [next(it) for _ in range(n)]
        g_refs, (loss_ref, gcw_ref) = take(ns), take(2)
        w_refs, m_refs, v_refs = take(ns), take(ns), take(ns)
        wcw_ref, mcw_ref, vcw_ref = take(3)
        outs = [take(4) for _ in range(ns)]
        cw_outs, (loss_out,) = take(4), take(1)
        send, slots, cslots, send_sems, recv_sems, csend_sems, crecv_sems = take(7)
        x, y, c = _place()
        me = 4 * x + 2 * y + c
        send[...] = jnp.zeros_like(send)
        for k in range(ns):
            send[k:k + 1, 0:widths[k]] = g_refs[k][...]
        send[LOSS_ROW:LOSS_ROW + 1, 0:128] = loss_ref[...]
        slots[me] = send[...]
        cslots[me] = gcw_ref[...]
        cps = []
        for k in range(1, NDEV):
            peer = (x ^ ((k >> 2) & 1), y ^ ((k >> 1) & 1), c ^ (k & 1))
            cps.append(pltpu.make_async_remote_copy(
                src_ref=send, dst_ref=slots.at[me], send_sem=send_sems.at[k - 1], recv_sem=recv_sems.at[k - 1],
                device_id=peer, device_id_type=MESH))
            cps.append(pltpu.make_async_remote_copy(
                src_ref=gcw_ref, dst_ref=cslots.at[me], send_sem=csend_sems.at[k - 1],
                recv_sem=crecv_sems.at[k - 1], device_id=peer, device_id_type=MESH))
        for cp in cps:
            cp.start()
        for cp in cps:
            cp.wait()
        tot = slots[0]
        ctot = cslots[0, me]
        for j in range(1, NDEV):
            tot = tot + slots[j]
            ctot = ctot + cslots[j, me]

        def step(g, w_ref, m_ref, v_ref, o):
            d, mn, vn = _adamw_math(w_ref[...], g, m_ref[...], v_ref[...])
            o[0][...], o[1][...], o[2][...], o[3][...] = g, d, mn, vn

        for k in range(ns):
            step(tot[k:k + 1, 0:widths[k]], w_refs[k], m_refs[k], v_refs[k], outs[k])
        step(ctot, wcw_ref, mcw_ref, vcw_ref, cw_outs)
        loss_out[...] = tot[LOSS_ROW:LOSS_ROW + 1, 0:128]

    args = [*gs, loss_row, gcw, *ws, *ms, *vs, wcw, mcw, vcw]
    out_shape = ([jax.ShapeDtypeStruct((1, n), F32) for n in widths for _ in range(4)]
                 + [jax.ShapeDtypeStruct((CWF, D), F32)] * 4 + [jax.ShapeDtypeStruct((1, 128), F32)])
    res = pl.pallas_call(
        body, in_specs=[VMEM] * len(args), out_specs=[VMEM] * len(out_shape), out_shape=out_shape,
        scratch_shapes=[pltpu.VMEM((SROWS, D), F32), pltpu.VMEM((NDEV, SROWS, D), F32),
                        pltpu.VMEM((NDEV, NDEV, CWF, D), F32)]
                       + [pltpu.SemaphoreType.DMA((NDEV - 1,))] * 4,
        name="small_step")(*args)
    per = [res[4 * k:4 * k + 4] for k in range(ns)]
    return per, res[4 * ns:4 * ns + 4], res[-1]


def _pack_cw(a):
    flat = a.reshape(a.shape[:-2] + (CK * HD,))
    pad = [(0, 0)] * (flat.ndim - 1) + [(0, CWF * D - CK * HD)]
    return jnp.pad(flat, pad).reshape(a.shape[:-2] + (CWF, D))


def _unpack_cw(v):
    return v.reshape(-1)[:CK * HD].reshape(1, CK, HD)


def kernel(x, ffn1_norm, ffn1_w_gate, ffn1_w_up, ffn1_w_down, mix_norm, w_in, q_norm, k_norm, conv_w, conv_b, conv_ln_g, conv_ln_b, w_out, ffn2_norm, ffn2_w_gate, ffn2_w_up, ffn2_w_down, loss_target, m_ffn1_norm, m_ffn1_w_gate, m_ffn1_w_up, m_ffn1_w_down, m_mix_norm, m_w_in, m_q_norm, m_k_norm, m_conv_w, m_conv_b, m_conv_ln_g, m_conv_ln_b, m_w_out, m_ffn2_norm, m_ffn2_w_gate, m_ffn2_w_up, m_ffn2_w_down, v_ffn1_norm, v_ffn1_w_gate, v_ffn1_w_up, v_ffn1_w_down, v_mix_norm, v_w_in, v_q_norm, v_k_norm, v_conv_w, v_conv_b, v_conv_ln_g, v_conv_ln_b, v_w_out, v_ffn2_norm, v_ffn2_w_gate, v_ffn2_w_up, v_ffn2_w_down):
    P = dict(ffn1_norm=ffn1_norm, ffn1_w_gate=ffn1_w_gate, ffn1_w_up=ffn1_w_up, ffn1_w_down=ffn1_w_down,
             mix_norm=mix_norm, w_in=w_in, q_norm=q_norm, k_norm=k_norm, conv_w=conv_w, conv_b=conv_b,
             conv_ln_g=conv_ln_g, conv_ln_b=conv_ln_b, w_out=w_out, ffn2_norm=ffn2_norm,
             ffn2_w_gate=ffn2_w_gate, ffn2_w_up=ffn2_w_up, ffn2_w_down=ffn2_w_down)
    M = dict(ffn1_norm=m_ffn1_norm, ffn1_w_gate=m_ffn1_w_gate, ffn1_w_up=m_ffn1_w_up, ffn1_w_down=m_ffn1_w_down,
             mix_norm=m_mix_norm, w_in=m_w_in, q_norm=m_q_norm, k_norm=m_k_norm, conv_w=m_conv_w, conv_b=m_conv_b,
             conv_ln_g=m_conv_ln_g, conv_ln_b=m_conv_ln_b, w_out=m_w_out, ffn2_norm=m_ffn2_norm,
             ffn2_w_gate=m_ffn2_w_gate, ffn2_w_up=m_ffn2_w_up, ffn2_w_down=m_ffn2_w_down)
    V = dict(ffn1_norm=v_ffn1_norm, ffn1_w_gate=v_ffn1_w_gate, ffn1_w_up=v_ffn1_w_up, ffn1_w_down=v_ffn1_w_down,
             mix_norm=v_mix_norm, w_in=v_w_in, q_norm=v_q_norm, k_norm=v_k_norm, conv_w=v_conv_w, conv_b=v_conv_b,
             conv_ln_g=v_conv_ln_g, conv_ln_b=v_conv_ln_b, w_out=v_w_out, ffn2_norm=v_ffn2_norm,
             ffn2_w_gate=v_ffn2_w_gate, ffn2_w_up=v_ffn2_w_up, ffn2_w_down=v_ffn2_w_down)
    order = ["ffn1_norm", "ffn1_w_gate", "ffn1_w_up", "ffn1_w_down", "mix_norm", "w_in", "q_norm", "k_norm",
             "conv_w", "conv_b", "conv_ln_g", "conv_ln_b", "w_out", "ffn2_norm", "ffn2_w_gate", "ffn2_w_up",
             "ffn2_w_down"]
    B, S, _ = x.shape
    T = B * S

    bigs = [("wg1", "ffn1_w_gate", True), ("wu1", "ffn1_w_up", True), ("wd1", "ffn1_w_down", False),
            ("win", "w_in", True), ("wout", "w_out", False),
            ("wg2", "ffn2_w_gate", True), ("wu2", "ffn2_w_up", True), ("wd2", "ffn2_w_down", False)]
    hm = lambda a, tr: jnp.transpose(a[0]) if tr else a[0]
    cw_pad = jnp.zeros((32, 128), F32).at[0:CK, 0:HD].set(conv_w[0])
    shard = {ln: (hm(P[pn], tr), BF16) for ln, pn, tr in bigs}
    gathered = _run_carry(_GatherCarry([shard["wg1"], (cw_pad, F32)]), "gather_first")
    W = {"wg1": gathered[0]}
    cwg = gathered[1].reshape(NDEV, 32, 128)[:, 0:CK, 0:HD]
    W["conv_w"] = jnp.transpose(cwg, (1, 0, 2)).reshape(CK, DC)
    norms = {n: P[n] for n in SMALL_NAMES}
    comm = _Comm({"wu1": {"wu1": shard["wu1"]}, "wd1": {"wd1": shard["wd1"]},
                  "win": {"win": shard["win"]}, "wout": {"wout": shard["wout"]},
                  "ffn2": {n: shard[n] for n in ("wg2", "wu2", "wd2")}})

    loss_part, gx, _, small = _local_step(x.reshape(T, D), loss_target.reshape(T, D), norms, W, B, S, comm)

    G, Dl, Mn, Vn = {}, {}, {}, {}
    for ln, pn, tr in bigs:
        outs = _adamw_big(comm.reduced[ln], hm(P[pn], tr), hm(M[pn], tr), hm(V[pn], tr), "adamw_" + ln)
        G[pn], Dl[pn], Mn[pn], Vn[pn] = [(jnp.transpose(o) if tr else o)[None] for o in outs]

    dcw = small["conv_w"].reshape(CK, NDEV, HD).transpose(1, 0, 2)
    loss_row = jnp.zeros((1, 128), F32).at[0, 0].set(loss_part)
    per, cw_outs, loss_out = _small_step(
        [small[n] for n in SMALL_NAMES], loss_row, _pack_cw(dcw),
        [P[n] for n in SMALL_NAMES], [M[n] for n in SMALL_NAMES], [V[n] for n in SMALL_NAMES],
        _pack_cw(P["conv_w"][0]), _pack_cw(M["conv_w"][0]), _pack_cw(V["conv_w"][0]))
    loss = loss_out[0, 0]
    for n, outs in zip(SMALL_NAMES, per):
        G[n], Dl[n], Mn[n], Vn[n] = outs
    G["conv_w"], Dl["conv_w"], Mn["conv_w"], Vn["conv_w"] = [_unpack_cw(o) for o in cw_outs]

    return (loss, gx.reshape(B, S, D), *[G[n] for n in order], *[Dl[n] for n in order],
            *[Mn[n] for n in order], *[Vn[n] for n in order])
```

```python
import functools

import jax
import jax.numpy as jnp
from jax import lax
from jax.experimental import pallas as pl
from jax.experimental.pallas import tpu as pltpu

F32 = jnp.float32
BF16 = jnp.bfloat16

D = 1024
FF = 2816
HD = 64
DA = 512
DC = 512
DIN = 2560
CK = 31
BLK = 128
DILS = (1, 4, 16)
EPS = 1e-6
NDEV = 8
MESH = pl.DeviceIdType.MESH

LR, B1, B2, AEPS, WD, STEP = 0.001, 0.9, 0.999, 1e-08, 0.01, 10

NT = (((1,), (1,)), ((), ()))
TN = (((0,), (0,)), ((), ()))

VMEM_LIMIT = 56 * 1024 * 1024


def _cp(sem=None):
    return pltpu.CompilerParams(dimension_semantics=sem, vmem_limit_bytes=VMEM_LIMIT)


def _sigmoid(x):
    return 0.5 * (jnp.tanh(0.5 * x) + 1.0)


def _pallas(body, args, *, grid, in_specs, out_specs, out_shape, scratch_shapes, sem, name, carry=None):
    if carry is None:
        outs = pl.pallas_call(body, grid=grid, in_specs=in_specs, out_specs=out_specs, out_shape=out_shape,
                              scratch_shapes=scratch_shapes, compiler_params=_cp(sem), name=name)(*args)
        return outs, None
    n_in, n_out, n_scr = len(in_specs), len(out_shape), len(scratch_shapes)
    c_in, c_out = len(carry.in_arrays), len(carry.out_shape)

    def wrapped(*refs):
        ins, refs = refs[:n_in], refs[n_in:]
        cins, refs = refs[:c_in], refs[c_in:]
        outs, refs = refs[:n_out], refs[n_out:]
        couts, refs = refs[:c_out], refs[c_out:]
        scr, cscr = refs[:n_scr], refs[n_scr:]
        ids = [pl.program_id(a) for a in range(len(grid))]
        step = ids[0]
        for i, n in zip(ids[1:], grid[1:]):
            step = step * n + i
        steps = functools.reduce(lambda a, b: a * b, grid)

        @pl.when(step == 0)
        def _():
            carry.start(cins, couts, cscr)

        body(*ins, *outs, *scr)

        if hasattr(carry, "mid"):
            @pl.when(step == steps // 2)
            def _():
                carry.mid(cins, couts, cscr)

        @pl.when(step == steps - 1)
        def _():
            carry.finish(cins, couts, cscr)

    outs = pl.pallas_call(
        wrapped, grid=grid, in_specs=list(in_specs) + carry.in_specs, out_specs=list(out_specs) + carry.out_specs,
        out_shape=list(out_shape) + carry.out_shape, scratch_shapes=list(scratch_shapes) + carry.scratch,
        compiler_params=_cp(("arbitrary",) * len(grid)), name=name)(*args, *carry.in_arrays)
    return outs[:n_out], outs[n_out:]


def _ffn_fwd(x, gain, wg, wu, wd, target, name, carry=None):
    T = x.shape[0]
    tm, tf = 1024, 256
    nt, nf = T // tm, FF // tf
    with_loss = target is not None

    def body(*refs):
        if with_loss:
            (x_ref, gain_ref, wg_ref, wu_ref, wd_ref, t_ref,
             h_ref, n_ref, g_ref, u_ref, dout_ref, dyb_ref, sq_ref, nb_scr, acc_scr) = refs
        else:
            (x_ref, gain_ref, wg_ref, wu_ref, wd_ref,
             h_ref, n_ref, g_ref, u_ref, nb_scr, acc_scr) = refs
        f = pl.program_id(1)

        @pl.when(f == 0)
        def _():
            xv = x_ref[...]
            r = lax.rsqrt(jnp.mean(xv * xv, axis=-1, keepdims=True) + EPS)
            nb = (xv * r * gain_ref[...]).astype(BF16)
            nb_scr[...] = nb
            n_ref[...] = nb
            acc_scr[...] = jnp.zeros_like(acc_scr)

        nb = nb_scr[...]
        g = lax.dot_general(nb, wg_ref[...], NT, preferred_element_type=F32)
        u = lax.dot_general(nb, wu_ref[...], NT, preferred_element_type=F32)
        a = g * _sigmoid(g) * u
        g_ref[...] = g.astype(BF16)
        u_ref[...] = u.astype(BF16)
        acc_scr[...] += jnp.dot(a.astype(BF16), wd_ref[...], preferred_element_type=F32)

        @pl.when(f == nf - 1)
        def _():
            h = x_ref[...] + 0.5 * acc_scr[...]
            h_ref[...] = h
            if with_loss:
                e = h - t_ref[...]
                dout = e * (1.0 / D)
                dout_ref[...] = dout
                dyb_ref[...] = (0.5 * dout).astype(BF16)
                sq_ref[...] = jnp.sum(e * e, axis=0, keepdims=True)[None]

    row = pl.BlockSpec((tm, D), lambda t, f: (t, 0))
    wspec = pl.BlockSpec((tf, D), lambda t, f: (f, 0))
    gspec = pl.BlockSpec((tm, tf), lambda t, f: (t, f))
    in_specs = [row, pl.BlockSpec((1, D), lambda t, f: (0, 0)), wspec, wspec, wspec]
    out_shape = [jax.ShapeDtypeStruct((T, D), F32), jax.ShapeDtypeStruct((T, D), BF16),
                 jax.ShapeDtypeStruct((T, FF), BF16), jax.ShapeDtypeStruct((T, FF), BF16)]
    out_specs = [row, row, gspec, gspec]
    args = [x, gain, wg, wu, wd]
    if with_loss:
        in_specs.append(row)
        args.append(target)
        out_shape += [jax.ShapeDtypeStruct((T, D), F32), jax.ShapeDtypeStruct((T, D), BF16),
                      jax.ShapeDtypeStruct((nt, 1, D), F32)]
        out_specs += [row, row, pl.BlockSpec((1, 1, D), lambda t, f: (t, 0, 0))]
    return _pallas(
        body, args, grid=(nt, nf), in_specs=in_specs, out_specs=out_specs, out_shape=out_shape,
        scratch_shapes=[pltpu.VMEM((tm, D), BF16), pltpu.VMEM((tm, D), F32)],
        sem=("parallel", "arbitrary"), name=name, carry=carry)


def _ffn_bwd(dyb, nb, g, u, wg, wu, wd, name, carry=None):
    T = dyb.shape[0]
    tm, tf = 512, 256
    nt, nf = T // tm, FF // tf

    def body(dy_ref, n_ref, g_ref, u_ref, wg_ref, wu_ref, wd_ref,
             dwg_ref, dwu_ref, dwd_ref, dn_ref, ag_scr, au_scr, ad_scr):
        f, t = pl.program_id(0), pl.program_id(1)

        @pl.when(t == 0)
        def _():
            ag_scr[...] = jnp.zeros_like(ag_scr)
            au_scr[...] = jnp.zeros_like(au_scr)
            ad_scr[...] = jnp.zeros_like(ad_scr)

        dy = dy_ref[...]
        n = n_ref[...]
        gv = g_ref[...].astype(F32)
        uv = u_ref[...].astype(F32)
        da = lax.dot_general(dy, wd_ref[...], NT, preferred_element_type=F32)
        sg = _sigmoid(gv)
        silu = gv * sg
        ab = (silu * uv).astype(BF16)
        dgb = (da * uv * (sg * (1.0 + gv * (1.0 - sg)))).astype(BF16)
        dub = (da * silu).astype(BF16)
        ad_scr[...] += lax.dot_general(ab, dy, TN, preferred_element_type=F32)
        ag_scr[...] += lax.dot_general(dgb, n, TN, preferred_element_type=F32)
        au_scr[...] += lax.dot_general(dub, n, TN, preferred_element_type=F32)
        dn = (jnp.dot(dgb, wg_ref[...], preferred_element_type=F32)
              + jnp.dot(dub, wu_ref[...], preferred_element_type=F32))
        rows = pl.ds(pl.multiple_of(t * tm, tm), tm)

        @pl.when(f == 0)
        def _():
            dn_ref[rows, :] = dn

        @pl.when(f > 0)
        def _():
            dn_ref[rows, :] += dn

        @pl.when(t == nt - 1)
        def _():
            dwg_ref[...] = ag_scr[...].astype(BF16)
            dwu_ref[...] = au_scr[...].astype(BF16)
            dwd_ref[...] = ad_scr[...].astype(BF16)

    row = pl.BlockSpec((tm, D), lambda f, t: (t, 0))
    gspec = pl.BlockSpec((tm, tf), lambda f, t: (t, f))
    wspec = pl.BlockSpec((tf, D), lambda f, t: (f, 0))
    return _pallas(
        body, (dyb, nb, g, u, wg, wu, wd), grid=(nf, nt),
        in_specs=[row, row, gspec, gspec, wspec, wspec, wspec],
        out_specs=[wspec, wspec, wspec, pl.BlockSpec((T, D), lambda f, t: (0, 0))],
        out_shape=[jax.ShapeDtypeStruct((FF, D), BF16)] * 3 + [jax.ShapeDtypeStruct((T, D), F32)],
        scratch_shapes=[pltpu.VMEM((tf, D), F32)] * 3,
        sem=("arbitrary", "arbitrary"), name=name, carry=carry)


FC = 256


def _resident(shape):
    return pl.BlockSpec(shape, lambda *_: (0,) * len(shape), pipeline_mode=pl.Buffered(1))


def _mix_out_ffn_loss(h1, attn, conv, wout, gain, wg, wu, wd, target, name):
    T = h1.shape[0]
    tm = 512
    nt = T // tm

    def body(h1_ref, at_ref, cv_ref, wo_ref, gain_ref, wg_ref, wu_ref, wd_ref, t_ref,
             h2_ref, n_ref, g_ref, u_ref, dout_ref, dyb_ref, sq_ref, a_scr):
        xv = (h1_ref[...]
              + jnp.dot(at_ref[...], wo_ref[0:DA, :], preferred_element_type=F32)
              + jnp.dot(cv_ref[...], wo_ref[DA:D, :], preferred_element_type=F32))
        h2_ref[...] = xv
        r = lax.rsqrt(jnp.mean(xv * xv, axis=-1, keepdims=True) + EPS)
        n_ref[...] = (xv * r * gain_ref[...]).astype(BF16)
        for c in range(FF // FC):
            cols = slice(c * FC, (c + 1) * FC)
            nb = n_ref[...]
            g = lax.dot_general(nb, wg_ref[cols, :], NT, preferred_element_type=F32)
            u = lax.dot_general(nb, wu_ref[cols, :], NT, preferred_element_type=F32)
            g_ref[:, cols] = g.astype(BF16)
            u_ref[:, cols] = u.astype(BF16)
            a_scr[:, cols] = (g * _sigmoid(g) * u).astype(BF16)
        e = h2_ref[...] + 0.5 * jnp.dot(a_scr[...], wd_ref[...], preferred_element_type=F32) - t_ref[...]
        dout = e * (1.0 / D)
        dout_ref[...] = dout
        dyb_ref[...] = (0.5 * dout).astype(BF16)
        sq_ref[...] = jnp.sum(e * e, axis=0, keepdims=True)[None]

    row = pl.BlockSpec((tm, D), lambda t: (t, 0))
    half = pl.BlockSpec((tm, DA), lambda t: (t, 0))
    wide = pl.BlockSpec((tm, FF), lambda t: (t, 0))
    outs, _ = _pallas(
        body, (h1, attn, conv, wout, gain, wg, wu, wd, target), grid=(nt,),
        in_specs=[row, half, half, _resident((D, D)), _resident((1, D)), _resident((FF, D)), _resident((FF, D)),
                  _resident((FF, D)), row],
        out_specs=[row, row, wide, wide, row, row, pl.BlockSpec((1, 1, D), lambda t: (t, 0, 0))],
        out_shape=[jax.ShapeDtypeStruct((T, D), F32), jax.ShapeDtypeStruct((T, D), BF16)]
                  + [jax.ShapeDtypeStruct((T, FF), BF16)] * 2
                  + [jax.ShapeDtypeStruct((T, D), F32), jax.ShapeDtypeStruct((T, D), BF16),
                     jax.ShapeDtypeStruct((nt, 1, D), F32)],
        scratch_shapes=[pltpu.VMEM((tm, FF), BF16)], sem=("parallel",), name=name)
    return outs


def _ffn_gate_up(x, gain, wg, wu, name, carry=None):
    T = x.shape[0]
    tm = 512

    def body(x_ref, gain_ref, wg_ref, wu_ref, n_ref, g_ref, u_ref, a_ref):
        xv = x_ref[...]
        r = lax.rsqrt(jnp.mean(xv * xv, axis=-1, keepdims=True) + EPS)
        n_ref[...] = (xv * r * gain_ref[...]).astype(BF16)
        for c in range(FF // FC):
            cols = slice(c * FC, (c + 1) * FC)
            nb = n_ref[...]
            g = lax.dot_general(nb, wg_ref[cols, :], NT, preferred_element_type=F32)
            u = lax.dot_general(nb, wu_ref[cols, :], NT, preferred_element_type=F32)
            g_ref[:, cols] = g.astype(BF16)
            u_ref[:, cols] = u.astype(BF16)
            a_ref[:, cols] = (g * _sigmoid(g) * u).astype(BF16)

    row = pl.BlockSpec((tm, D), lambda t: (t, 0))
    wide = pl.BlockSpec((tm, FF), lambda t: (t, 0))
    return _pallas(
        body, (x, gain, wg, wu), grid=(T // tm,),
        in_specs=[row, _resident((1, D)), _resident((FF, D)), _resident((FF, D))],
        out_specs=[row, wide, wide, wide],
        out_shape=[jax.ShapeDtypeStruct((T, D), BF16)] + [jax.ShapeDtypeStruct((T, FF), BF16)] * 3,
        scratch_shapes=[], sem=("parallel",), name=name, carry=carry)


def _ffn_down(x, a, wd, name, carry=None):
    T = x.shape[0]
    tm = 512

    def body(x_ref, a_ref, wd_ref, h_ref):
        h_ref[...] = x_ref[...] + 0.5 * jnp.dot(a_ref[...], wd_ref[...], preferred_element_type=F32)

    row = pl.BlockSpec((tm, D), lambda t: (t, 0))
    wide = pl.BlockSpec((tm, FF), lambda t: (t, 0))
    return _pallas(
        body, (x, a, wd), grid=(T // tm,), in_specs=[row, wide, _resident((FF, D))],
        out_specs=[row], out_shape=[jax.ShapeDtypeStruct((T, D), F32)],
        scratch_shapes=[], sem=("parallel",), name=name, carry=carry)


def _ffn_bwd_act(dyb, g, u, x, dout, gain, wg, wu, wd, name, carry=None):
    T = x.shape[0]
    tm = 256
    nt = T // tm

    def body(dy_ref, g_ref, u_ref, x_ref, dout_ref, gain_ref, wg_ref, wu_ref, wd_ref,
             dg_ref, du_ref, dx_ref, dgn_ref):
        for c in range(FF // FC):
            cols = slice(c * FC, (c + 1) * FC)
            da = lax.dot_general(dy_ref[...], wd_ref[cols, :], NT, preferred_element_type=F32)
            gv = g_ref[:, cols].astype(F32)
            uv = u_ref[:, cols].astype(F32)
            sg = _sigmoid(gv)
            dg_ref[:, cols] = (da * uv * (sg * (1.0 + gv * (1.0 - sg)))).astype(BF16)
            du_ref[:, cols] = (da * (gv * sg)).astype(BF16)
        dn = (jnp.dot(dg_ref[...], wg_ref[...], preferred_element_type=F32)
              + jnp.dot(du_ref[...], wu_ref[...], preferred_element_type=F32))
        dx, dgain = _rms_bwd_rows(dn, x_ref[...], gain_ref[...])
        dx_ref[...] = dout_ref[...] + dx
        dgn_ref[...] = dgain[None]

    row = pl.BlockSpec((tm, D), lambda t: (t, 0))
    wide = pl.BlockSpec((tm, FF), lambda t: (t, 0))
    return _pallas(
        body, (dyb, g, u, x, dout, gain, wg, wu, wd), grid=(nt,),
        in_specs=[row, wide, wide, row, row, _resident((1, D)), _resident((FF, D)), _resident((FF, D)),
                  _resident((FF, D))],
        out_specs=[wide, wide, row, pl.BlockSpec((1, 1, D), lambda t: (t, 0, 0))],
        out_shape=[jax.ShapeDtypeStruct((T, FF), BF16)] * 2
                  + [jax.ShapeDtypeStruct((T, D), F32), jax.ShapeDtypeStruct((nt, 1, D), F32)],
        scratch_shapes=[], sem=("parallel",), name=name, carry=carry)


def _ffn_bwd_w(lhs, rhs, name, carry=None):
    T = rhs.shape[0]
    tf = 256
    lhs = lhs if isinstance(lhs, tuple) else (lhs,)

    def body(*refs):
        r_ref, dw_ref = refs[-2:]
        if len(lhs) == 2:
            gv = refs[0][...].astype(F32)
            lv = (gv * _sigmoid(gv) * refs[1][...].astype(F32)).astype(BF16)
        else:
            lv = refs[0][...]
        dw_ref[...] = lax.dot_general(lv, r_ref[...], TN, preferred_element_type=F32).astype(BF16)

    (dw,), got = _pallas(
        body, (*lhs, rhs), grid=(FF // tf,),
        in_specs=[pl.BlockSpec((T, tf), lambda f: (0, f))] * len(lhs) + [_resident((T, D))],
        out_specs=[pl.BlockSpec((tf, D), lambda f: (f, 0))], out_shape=[jax.ShapeDtypeStruct((FF, D), BF16)],
        scratch_shapes=[], sem=("parallel",), name=name, carry=carry)
    return dw, got


def _rms_bwd_rows(dn, xv, gain):
    r = lax.rsqrt(jnp.mean(xv * xv, axis=-1, keepdims=True) + EPS)
    xhat = xv * r
    dxhat = dn * gain
    dx = r * (dxhat - xhat * jnp.mean(dxhat * xhat, axis=-1, keepdims=True))
    return dx, jnp.sum(dn * xhat, axis=0, keepdims=True)


def _norm_bwd(dn, x, dout, gain, name):
    T = x.shape[0]
    tm = 512
    nt = T // tm

    def body(dn_ref, x_ref, dout_ref, gain_ref, dx_ref, dg_ref):
        dx, dgain = _rms_bwd_rows(dn_ref[...], x_ref[...], gain_ref[...])
        dx_ref[...] = dout_ref[...] + dx
        dg_ref[...] = dgain[None]

    row = pl.BlockSpec((tm, D), lambda t: (t, 0))
    return pl.pallas_call(
        body, grid=(nt,), in_specs=[row, row, row, pl.BlockSpec((1, D), lambda t: (0, 0))],
        out_specs=[row, pl.BlockSpec((1, 1, D), lambda t: (t, 0, 0))],
        out_shape=[jax.ShapeDtypeStruct((T, D), F32), jax.ShapeDtypeStruct((nt, 1, D), F32)],
        compiler_params=_cp(("parallel",)), name=name)(dn, x, dout, gain)


def _mix_in(h, gain, win, carry=None):
    T = h.shape[0]
    tm = 512

    def body(h_ref, gain_ref, w_ref, u_ref, n_ref):
        xv = h_ref[...]
        r = lax.rsqrt(jnp.mean(xv * xv, axis=-1, keepdims=True) + EPS)
        nb = (xv * r * gain_ref[...]).astype(BF16)
        n_ref[...] = nb
        u_ref[...] = lax.dot_general(nb, w_ref[...], NT, preferred_element_type=F32).astype(BF16)

    row = pl.BlockSpec((tm, D), lambda t: (t, 0))
    return _pallas(
        body, (h, gain, win), grid=(T // tm,),
        in_specs=[row, _resident((1, D)), _resident((DIN, D))],
        out_specs=[pl.BlockSpec((tm, DIN), lambda t: (t, 0)), row],
        out_shape=[jax.ShapeDtypeStruct((T, DIN), BF16), jax.ShapeDtypeStruct((T, D), BF16)],
        scratch_shapes=[], sem=("parallel",), name="mix_in", carry=carry)


def _mix_out(h, attn, conv, wout):
    T = h.shape[0]
    tm = 512

    def body(h_ref, a_ref, c_ref, w_ref, o_ref):
        o_ref[...] = (h_ref[...]
                      + jnp.dot(a_ref[...], w_ref[0:DA, :], preferred_element_type=F32)
                      + jnp.dot(c_ref[...], w_ref[DA:D, :], preferred_element_type=F32))

    row = pl.BlockSpec((tm, D), lambda t: (t, 0))
    half = pl.BlockSpec((tm, DA), lambda t: (t, 0))
    return pl.pallas_call(
        body, grid=(T // tm,),
        in_specs=[row, half, half, pl.BlockSpec((D, D), lambda t: (0, 0))],
        out_specs=row, out_shape=jax.ShapeDtypeStruct((T, D), F32),
        compiler_params=_cp(("parallel",)), name="mix_out")(h, attn, conv, wout)


def _mix_out_bwd(dh, attn, conv, wout):
    T = dh.shape[0]
    tm = 512
    nt = T // tm

    def body(dh_ref, a_ref, c_ref, w_ref, da_ref, dc_ref, dw_ref, acc_scr):
        t = pl.program_id(0)

        @pl.when(t == 0)
        def _():
            acc_scr[...] = jnp.zeros_like(acc_scr)

        dhb = dh_ref[...].astype(BF16)
        dmix = lax.dot_general(dhb, w_ref[...], NT, preferred_element_type=F32)
        da_ref[...] = dmix[:, 0:DA].astype(BF16)
        dc_ref[...] = dmix[:, DA:D].astype(BF16)
        acc_scr[0:DA, :] += lax.dot_general(a_ref[...], dhb, TN, preferred_element_type=F32)
        acc_scr[DA:D, :] += lax.dot_general(c_ref[...], dhb, TN, preferred_element_type=F32)

        @pl.when(t == nt - 1)
        def _():
            dw_ref[...] = acc_scr[...].astype(BF16)

    row = pl.BlockSpec((tm, D), lambda t: (t, 0))
    half = pl.BlockSpec((tm, DA), lambda t: (t, 0))
    full = pl.BlockSpec((D, D), lambda t: (0, 0))
    return pl.pallas_call(
        body, grid=(nt,), in_specs=[row, half, half, full], out_specs=[half, half, full],
        out_shape=[jax.ShapeDtypeStruct((T, DA), BF16)] * 2 + [jax.ShapeDtypeStruct((D, D), BF16)],
        scratch_shapes=[pltpu.VMEM((D, D), F32)],
        compiler_params=_cp(("arbitrary",)), name="mix_out_bwd")(dh, attn, conv, wout)


def _mix_in_bwd(dparts, win, nb, h, dh, gain):
    T = h.shape[0]
    tm = 512
    nt = T // tm

    def body(d0, d1, d2, d3, d4, w_ref, n_ref, h_ref, dh_ref, gain_ref,
             dw_ref, dx_ref, dyb_ref, dg_ref, acc_scr):
        t = pl.program_id(0)

        @pl.when(t == 0)
        def _():
            acc_scr[...] = jnp.zeros_like(acc_scr)

        n = n_ref[...]
        dn = jnp.zeros((tm, D), F32)
        for i, d_ref in enumerate((d0, d1, d2, d3, d4)):
            dv = d_ref[...]
            dn = dn + jnp.dot(dv, w_ref[i * DA:(i + 1) * DA, :], preferred_element_type=F32)
            acc_scr[i * DA:(i + 1) * DA, :] += lax.dot_general(dv, n, TN, preferred_element_type=F32)
        dx, dgain = _rms_bwd_rows(dn, h_ref[...], gain_ref[...])
        tot = dh_ref[...] + dx
        dx_ref[...] = tot
        dyb_ref[...] = (0.5 * tot).astype(BF16)
        dg_ref[...] = dgain[None]

        @pl.when(t == nt - 1)
        def _():
            dw_ref[...] = acc_scr[...].astype(BF16)

    row = pl.BlockSpec((tm, D), lambda t: (t, 0))
    half = pl.BlockSpec((tm, DA), lambda t: (t, 0))
    full = pl.BlockSpec((DIN, D), lambda t: (0, 0))
    return pl.pallas_call(
        body, grid=(nt,),
        in_specs=[half] * 5 + [full, row, row, row, pl.BlockSpec((1, D), lambda t: (0, 0))],
        out_specs=[full, row, row, pl.BlockSpec((1, 1, D), lambda t: (t, 0, 0))],
        out_shape=[jax.ShapeDtypeStruct((DIN, D), BF16), jax.ShapeDtypeStruct((T, D), F32),
                   jax.ShapeDtypeStruct((T, D), BF16), jax.ShapeDtypeStruct((nt, 1, D), F32)],
        scratch_shapes=[pltpu.VMEM((DIN, D), F32)],
        compiler_params=_cp(("arbitrary",)), name="mix_in_bwd")(*dparts, win, nb, h, dh, gain)


def _head_masks():
    lane = lax.broadcasted_iota(jnp.int32, (1, 2 * HD), 1)
    m0 = lane < HD
    return m0, jnp.logical_not(m0)


def _stack_heads(v, m0, m1):
    z = jnp.zeros_like(v)
    return jnp.concatenate([jnp.where(m0, v, z), jnp.where(m1, v, z)], axis=0)


def _unstack_heads(v2, m0):
    return jnp.where(m0, v2[0:BLK], v2[BLK:2 * BLK])


def _head_sums(xv):
    ri = lax.broadcasted_iota(jnp.int32, (2 * HD, 2 * HD), 0)
    ci = lax.broadcasted_iota(jnp.int32, (2 * HD, 2 * HD), 1)
    ones = jnp.where((ri < HD) == (ci < HD), 1.0, 0.0).astype(BF16)
    hi = xv.astype(BF16)
    lo = (xv - hi.astype(F32)).astype(BF16)
    return (jnp.dot(hi, ones, preferred_element_type=F32) + jnp.dot(lo, ones, preferred_element_type=F32))


def _head_rms(xv):
    return lax.rsqrt(_head_sums(xv * xv) * (1.0 / HD) + EPS)


def _band_mask(first):
    qi = lax.broadcasted_iota(jnp.int32, (BLK, 2 * BLK), 0)
    ci = lax.broadcasted_iota(jnp.int32, (BLK, 2 * BLK), 1)
    band = (ci >= qi) & (ci <= qi + BLK)
    return band & ((ci >= BLK) | jnp.logical_not(first))


def _block_rows(j, d, seg):
    r, n = j // seg, j % seg
    start = r + (d * BLK) * n
    first = n == 0
    prev = jnp.where(first, start, start - d * BLK)
    return pl.ds(start, BLK, stride=d), pl.ds(prev, BLK, stride=d), first


def _block_keys(refs, cur, prev, first, single):
    if single:
        qi = lax.broadcasted_iota(jnp.int32, (BLK, BLK), 0)
        ci = lax.broadcasted_iota(jnp.int32, (BLK, BLK), 1)
        return [r[cur, :].astype(BF16) for r in refs], ci <= qi
    return ([jnp.concatenate([r[prev, :], r[cur, :]], axis=0).astype(BF16) for r in refs], _band_mask(first))


def _attn_fwd(u, qg2, kg2, B, S, carry=None):
    T = B * S
    NB = S // BLK
    scale = HD ** -0.5

    def body(q_ref, k_ref, v_ref, qg_ref, kg_ref, o_ref, lse_ref, qn, kn, vn, os_, ls_):
        m0, m1 = _head_masks()
        qv = q_ref[...].astype(F32)
        qn[...] = qv * _head_rms(qv) * (qg_ref[...] * scale)
        kv = k_ref[...].astype(F32)
        kn[...] = kv * _head_rms(kv) * kg_ref[...]
        vn[...] = v_ref[...].astype(F32)

        for i, d in enumerate(DILS):
            seg = NB // d

            def blk(j, c, i=i, d=d, seg=seg):
                cur, prev, first = _block_rows(j, d, seg)
                q2 = _stack_heads(qn[cur, :].astype(BF16), m0, m1)
                (kk, vv), mask = _block_keys((kn, vn), cur, prev, first, False)
                s = lax.dot_general(q2, kk, NT, preferred_element_type=F32)
                s = jnp.where(jnp.concatenate([mask, mask], axis=0), s, -1e30)
                mx = jnp.max(s, axis=-1, keepdims=True)
                p = jnp.exp(s - mx)
                l = jnp.sum(p, axis=-1, keepdims=True)
                o2 = jnp.dot((p * (1.0 / l)).astype(BF16), vv, preferred_element_type=F32)
                os_[i, cur, :] = _unstack_heads(o2, m0)
                ls_[i, cur, :] = _unstack_heads(mx + jnp.log(l), m0)
                return c

            lax.fori_loop(0, NB, blk, 0, unroll=8)

        def comb(c, carry):
            rows = pl.ds(pl.multiple_of(c * 256, 256), 256)
            l0, l1, l2 = ls_[0, rows, :], ls_[1, rows, :], ls_[2, rows, :]
            mx = jnp.maximum(jnp.maximum(l0, l1), l2)
            e0, e1, e2 = jnp.exp(l0 - mx), jnp.exp(l1 - mx), jnp.exp(l2 - mx)
            tot = e0 + e1 + e2
            inv = 1.0 / tot
            o = (e0 * os_[0, rows, :] + e1 * os_[1, rows, :] + e2 * os_[2, rows, :]) * inv
            o_ref[rows, :] = o.astype(BF16)
            lse_ref[rows, :] = mx + jnp.log(tot)
            return carry

        lax.fori_loop(0, S // 256, comb, 0)

    pair = 2 * HD
    blk_spec = lambda off: pl.BlockSpec((S, pair), lambda b, p, off=off: (b, off + p))
    gspec = pl.BlockSpec((1, pair), lambda b, p: (0, 0))
    return _pallas(
        body, (u, u, u, qg2, kg2), grid=(B, DA // pair),
        in_specs=[blk_spec(0), blk_spec(DA // pair), blk_spec(2 * DA // pair), gspec, gspec],
        out_specs=[blk_spec(0), blk_spec(0)],
        out_shape=[jax.ShapeDtypeStruct((T, DA), BF16), jax.ShapeDtypeStruct((T, DA), F32)],
        scratch_shapes=[pltpu.VMEM((S, pair), F32)] * 3 + [pltpu.VMEM((3, S, pair), F32)] * 2,
        sem=("parallel", "parallel"), name="attn_fwd", carry=carry)


def _attn_bwd(u, attn, dattn, lse, qg2, kg2, B, S, carry=None):
    T = B * S
    NB = S // BLK
    scale = HD ** -0.5
    pair = 2 * HD

    def body(q_ref, k_ref, v_ref, o_ref, do_ref, lse_ref, qg_ref, kg_ref,
             dq_ref, dk_ref, dv_ref, dgn_ref,
             qn, kn, vn, don, ldl, accq, acck, accv):
        m0, m1 = _head_masks()
        lane = lax.broadcasted_iota(jnp.int32, (1, pair), 1)
        qv = q_ref[...].astype(F32)
        qn[...] = qv * _head_rms(qv) * (qg_ref[...] * scale)
        kv = k_ref[...].astype(F32)
        kn[...] = kv * _head_rms(kv) * kg_ref[...]
        vn[...] = v_ref[...].astype(F32)
        dov = do_ref[...].astype(F32)
        don[...] = dov
        ldl[...] = jnp.where((lane % HD) < HD // 2, lse_ref[...], _head_sums(dov * o_ref[...].astype(F32)))
        accq[...] = jnp.zeros_like(accq)
        acck[...] = jnp.zeros_like(acck)
        accv[...] = jnp.zeros_like(accv)

        for i, d in enumerate(DILS):
            seg = NB // d

            def blk(j, c, d=d, seg=seg):
                cur, prev, first = _block_rows(j, d, seg)
                q2 = _stack_heads(qn[cur, :].astype(BF16), m0, m1)
                do2 = _stack_heads(don[cur, :].astype(BF16), m0, m1)
                (kk, vv), mask = _block_keys((kn, vn), cur, prev, first, seg == 1)
                ldv = ldl[cur, :]
                lse2 = jnp.concatenate([ldv[:, 0:1], ldv[:, HD:HD + 1]], axis=0)
                dl2 = jnp.concatenate([ldv[:, HD // 2:HD // 2 + 1], ldv[:, HD + HD // 2:HD + HD // 2 + 1]], axis=0)
                s = lax.dot_general(q2, kk, NT, preferred_element_type=F32)
                p = jnp.where(jnp.concatenate([mask, mask], axis=0), jnp.exp(s - lse2), 0.0)
                dp = lax.dot_general(do2, vv, NT, preferred_element_type=F32)
                ds = (p * (dp - dl2)).astype(BF16)
                dq_acc = _unstack_heads(jnp.dot(ds, kk, preferred_element_type=F32), m0)
                dk_acc = lax.dot_general(ds, q2, TN, preferred_element_type=F32)
                dv_acc = lax.dot_general(p.astype(BF16), do2, TN, preferred_element_type=F32)
                accq[cur, :] += dq_acc
                if seg == 1:
                    acck[cur, :] += dk_acc
                    accv[cur, :] += dv_acc
                else:
                    acck[prev, :] += dk_acc[0:BLK]
                    acck[cur, :] += dk_acc[BLK:2 * BLK]
                    accv[prev, :] += dv_acc[0:BLK]
                    accv[cur, :] += dv_acc[BLK:2 * BLK]
                return c

            lax.fori_loop(0, NB, blk, 0, unroll=8)

        def norm_bwd(x_ref, dn, gain):
            xv = x_ref[...].astype(F32)
            r = _head_rms(xv)
            xhat = xv * r
            dxhat = dn * gain
            dx = r * (dxhat - xhat * (_head_sums(dxhat * xhat) * (1.0 / HD)))
            return dx, jnp.sum(dn * xhat, axis=0, keepdims=True)

        dq, dgq = norm_bwd(q_ref, accq[...], qg_ref[...] * scale)
        dk, dgk = norm_bwd(k_ref, acck[...], kg_ref[...])
        dq_ref[...] = dq.astype(BF16)
        dk_ref[...] = dk.astype(BF16)
        dv_ref[...] = accv[...].astype(BF16)
        dgn_ref[...] = jnp.concatenate([dgq * scale, dgk, jnp.zeros((6, pair), F32)], axis=0)[None]

    blk_spec = lambda off: pl.BlockSpec((S, pair), lambda b, p, off=off: (b, off + p))
    gspec = pl.BlockSpec((1, pair), lambda b, p: (0, 0))
    np_ = DA // pair
    return _pallas(
        body, (u, u, u, attn, dattn, lse, qg2, kg2), grid=(B, np_),
        in_specs=[blk_spec(0), blk_spec(np_), blk_spec(2 * np_), blk_spec(0), blk_spec(0), blk_spec(0),
                  gspec, gspec],
        out_specs=[blk_spec(0), blk_spec(0), blk_spec(0),
                   pl.BlockSpec((1, 8, pair), lambda b, p: (b * np_ + p, 0, 0))],
        out_shape=[jax.ShapeDtypeStruct((T, DA), BF16)] * 3 + [jax.ShapeDtypeStruct((B * np_, 8, pair), F32)],
        scratch_shapes=[pltpu.VMEM((S, pair), F32)] * 8,
        sem=("parallel", "parallel"), name="attn_bwd", carry=carry)


CT = 32
CPAD = 32


def _shifted(win, offsets):
    rolled, out = {}, {}
    n = win.shape[0]
    for o in offsets:
        sub = o % 8
        if sub not in rolled:
            rolled[sub] = win if sub == 0 else pltpu.roll(win, n - sub, 0)
        out[o] = rolled[sub][o - sub:o - sub + CT, :]
    return out


def _ln_fwd(y, g, b):
    mu = jnp.mean(y, axis=-1, keepdims=True)
    yc = y - mu
    rstd = lax.rsqrt(jnp.mean(yc * yc, axis=-1, keepdims=True) + EPS)
    xhat = yc * rstd
    return xhat, rstd, xhat * g + b


def _fill_glu(ca_ref, cg_ref, glu, S):
    glu[pl.ds(0, CPAD), :] = jnp.zeros((CPAD, DC), F32)

    def fill(i, c):
        rows = pl.ds(pl.multiple_of(i * 256, 256), 256)
        a = ca_ref[rows, :].astype(F32)
        gt = cg_ref[rows, :].astype(F32)
        glu[pl.ds(pl.multiple_of(CPAD + i * 256, CT), 256), :] = a * _sigmoid(gt)
        return c

    lax.fori_loop(0, S // 256, fill, 0)


def _conv_fwd(u, cw, cb, lg, lb, B, S):
    T = B * S

    def body(ca_ref, cg_ref, w_ref, b_ref, lg_ref, lb_ref, o_ref, y_ref, glu):
        _fill_glu(ca_ref, cg_ref, glu, S)

        def step(i, c):
            t0 = pl.multiple_of(i * CT, CT)
            win = glu[pl.ds(t0, 2 * CT), :]
            acc = jnp.zeros((CT, DC), F32) + b_ref[...]
            taps = _shifted(win, [k + 2 for k in range(CK)])
            for k in range(CK):
                acc = acc + taps[k + 2] * w_ref[k:k + 1, :]
            y_ref[pl.ds(t0, CT), :] = acc
            _, _, z = _ln_fwd(acc, lg_ref[...], lb_ref[...])
            o_ref[pl.ds(t0, CT), :] = (z * _sigmoid(z)).astype(BF16)
            return c

        lax.fori_loop(0, S // CT, step, 0, unroll=2)

    vec = pl.BlockSpec((1, DC), lambda b: (0, 0))
    return pl.pallas_call(
        body, grid=(B,),
        in_specs=[pl.BlockSpec((S, DC), lambda b: (b, 3)), pl.BlockSpec((S, DC), lambda b: (b, 4)),
                  pl.BlockSpec((CT, DC), lambda b: (0, 0)), vec, vec, vec],
        out_specs=[pl.BlockSpec((S, DC), lambda b: (b, 0))] * 2,
        out_shape=[jax.ShapeDtypeStruct((T, DC), BF16), jax.ShapeDtypeStruct((T, DC), F32)],
        scratch_shapes=[pltpu.VMEM((CPAD + S, DC), F32)],
        compiler_params=_cp(("parallel",)), name="conv_fwd")(u, u, cw, cb, lg, lb)


def _conv_bwd(u, y, dconv, cw, lg, lb, B, S):
    T = B * S

    def body(ca_ref, cg_ref, y_ref, dc_ref, w_ref, lg_ref, lb_ref,
             dca_ref, dcg_ref, dw_ref, ds_ref, glu, dyp, dwacc):
        _fill_glu(ca_ref, cg_ref, glu, S)
        dyp[pl.ds(S, CPAD), :] = jnp.zeros((CPAD, DC), F32)
        lgv, lbv = lg_ref[...], lb_ref[...]

        def sum8(v):
            return functools.reduce(jnp.add, [v[r:r + 8] for r in range(0, v.shape[0], 8)])

        P1 = 4 * CT

        def p1(i, carry):
            sb, sg, sl = carry
            t0 = pl.multiple_of(i * P1, P1)
            xhat, rstd, z = _ln_fwd(y_ref[pl.ds(t0, P1), :], lgv, lbv)
            sz = _sigmoid(z)
            dz = dc_ref[pl.ds(t0, P1), :].astype(F32) * (sz * (1.0 + z * (1.0 - sz)))
            dxhat = dz * lgv
            dy = rstd * (dxhat - jnp.mean(dxhat, axis=-1, keepdims=True)
                         - xhat * jnp.mean(dxhat * xhat, axis=-1, keepdims=True))
            dyp[pl.ds(t0, P1), :] = dy
            return sb + sum8(dy), sg + sum8(dz * xhat), sl + sum8(dz)

        z8 = jnp.zeros((8, DC), F32)
        sb, sg, sl = lax.fori_loop(0, S // P1, p1, (z8, z8, z8))
        rs = lambda v: jnp.sum(v, axis=0, keepdims=True)
        ds_ref[...] = jnp.concatenate([rs(sb), rs(sg), rs(sl), jnp.zeros((5, DC), F32)], axis=0)[None]

        def p2(i, c):
            t0 = pl.multiple_of(i * CT, CT)
            win = dyp[pl.ds(t0, 2 * CT), :]
            acc = jnp.zeros((CT, DC), F32)
            taps = _shifted(win, [30 - k for k in range(CK)])
            for k in range(CK):
                acc = acc + taps[30 - k] * w_ref[k:k + 1, :]
            a = ca_ref[pl.ds(t0, CT), :].astype(F32)
            sgt = _sigmoid(cg_ref[pl.ds(t0, CT), :].astype(F32))
            dca_ref[pl.ds(t0, CT), :] = (acc * sgt).astype(BF16)
            dcg_ref[pl.ds(t0, CT), :] = (acc * a * sgt * (1.0 - sgt)).astype(BF16)
            return c

        lax.fori_loop(0, S // CT, p2, 0)

        dwacc[...] = jnp.zeros_like(dwacc)

        def p3(i, c):
            t0 = pl.multiple_of(i * CT, CT)
            win = glu[pl.ds(t0, 2 * CT), :]
            dy = dyp[pl.ds(t0, CT), :]
            for k in range(CK):
                dwacc[k] += sum8(dy * win[k + 2:k + 2 + CT, :])
            return c

        lax.fori_loop(0, S // CT, p3, 0)
        dw_ref[...] = jnp.sum(dwacc[...], axis=1)[None]

    vec = pl.BlockSpec((1, DC), lambda b: (0, 0))
    seq = pl.BlockSpec((S, DC), lambda b: (b, 0))
    return pl.pallas_call(
        body, grid=(B,),
        in_specs=[pl.BlockSpec((S, DC), lambda b: (b, 3)), pl.BlockSpec((S, DC), lambda b: (b, 4)),
                  seq, seq, pl.BlockSpec((CT, DC), lambda b: (0, 0)), vec, vec],
        out_specs=[seq, seq, pl.BlockSpec((1, CT, DC), lambda b: (b, 0, 0)),
                   pl.BlockSpec((1, 8, DC), lambda b: (b, 0, 0))],
        out_shape=[jax.ShapeDtypeStruct((T, DC), BF16)] * 2
                  + [jax.ShapeDtypeStruct((B, CT, DC), F32), jax.ShapeDtypeStruct((B, 8, DC), F32)],
        scratch_shapes=[pltpu.VMEM((CPAD + S, DC), F32), pltpu.VMEM((S + CPAD, DC), F32),
                        pltpu.VMEM((CT, 8, DC), F32)],
        compiler_params=_cp(("parallel",)), name="conv_bwd")(u, u, y, dconv, cw, lg, lb)


def _local_step(x, target, norms, W, B, S, comm=None):
    qg2 = jnp.concatenate([norms["q_norm"], norms["q_norm"]], axis=1)
    kg2 = jnp.concatenate([norms["k_norm"], norms["k_norm"]], axis=1)
    cw = jnp.concatenate([W["conv_w"], jnp.zeros((1, DC), F32)], axis=0)

    W = dict(W)
    (n1, g1, u1, act1), got = _ffn_gate_up(x, norms["ffn1_norm"], W["wg1"], W["wu1"], "ffn1_gate_up",
                                           carry=comm.gathers["down_in"] if comm else None)
    if comm:
        W.update(comm.gathered("down_in", got))
    (h1,), got = _ffn_down(x, act1, W["wd1"], "ffn1_down", carry=comm.gathers["wout"] if comm else None)
    if comm:
        W.update(comm.gathered("wout", got))
    (u, n2), _ = _mix_in(h1, norms["mix_norm"], W["win"])
    (attn, lse), got = _attn_fwd(u, qg2, kg2, B, S, carry=comm.gathers["ffn2"] if comm else None)
    if comm:
        W = dict(W, **comm.gathered("ffn2", got))
    conv, y = _conv_fwd(u, cw, norms["conv_b"], norms["conv_ln_g"], norms["conv_ln_b"], B, S)
    h2, n3, g2, u2, dout, dyb, sq = _mix_out_ffn_loss(h1, attn, conv, W["wout"], norms["ffn2_norm"],
                                                      W["wg2"], W["wu2"], W["wd2"], target, "ffn2_fwd")
    loss = (0.5 / D) * jnp.sum(sq)

    (dg2, du2, dh2, dgn_ffn2), _ = _ffn_bwd_act(dyb, g2, u2, h2, dout, norms["ffn2_norm"],
                                               W["wg2"], W["wu2"], W["wd2"], "ffn2_bwd_act")
    dwd2, _ = _ffn_bwd_w((g2, u2), dyb, "ffn2_bwd_wd")
    dwg2, _ = _ffn_bwd_w(dg2, n3, "ffn2_bwd_wg")
    dwu2, _ = _ffn_bwd_w(du2, n3, "ffn2_bwd_wu")
    dattn, dconv, dwout = _mix_out_bwd(dh2, attn, conv, W["wout"])
    carry = comm.reduce_start("ffn2", {"wg2": dwg2, "wu2": dwu2, "wd2": dwd2, "wout": dwout}) if comm else None
    (dq, dk, dv, dgn_qk), got = _attn_bwd(u, attn, dattn, lse, qg2, kg2, B, S, carry=carry)
    if comm:
        comm.reduce_done(carry, got)
    dca, dcg, dcw, dcs = _conv_bwd(u, y, dconv, cw, norms["conv_ln_g"], norms["conv_ln_b"], B, S)
    dwin, dh1, dyb1, dgn_mix = _mix_in_bwd((dq, dk, dv, dca, dcg), W["win"], n2, h1, dh2, norms["mix_norm"])
    (dg1, du1, gx, dgn_ffn1), _ = _ffn_bwd_act(dyb1, g1, u1, x, dh1, norms["ffn1_norm"],
                                              W["wg1"], W["wu1"], W["wd1"], "ffn1_bwd_act")
    carry = comm.reduce_start("win", {"win": dwin}) if comm else None
    dwd1, got = _ffn_bwd_w(act1, dyb1, "ffn1_bwd_wd", carry=carry)
    if comm:
        comm.reduce_done(carry, got)
        carry = comm.reduce_start("wd1", {"wd1": dwd1})
    dwg1, got = _ffn_bwd_w(dg1, n1, "ffn1_bwd_wg", carry=carry)
    if comm:
        comm.reduce_done(carry, got)
        carry = comm.reduce_start("wg1", {"wg1": dwg1})
    dwu1, got = _ffn_bwd_w(du1, n1, "ffn1_bwd_wu", carry=carry)
    if comm:
        comm.reduce_done(carry, got)
        comm.reduce_now("wu1", {"wu1": dwu1})

    qk = jnp.sum(dgn_qk, axis=0)
    cs = jnp.sum(dcs, axis=0)
    small = {
        "ffn1_norm": jnp.sum(dgn_ffn1, axis=0),
        "mix_norm": jnp.sum(dgn_mix, axis=0),
        "q_norm": qk[0:1, 0:HD] + qk[0:1, HD:2 * HD],
        "k_norm": qk[1:2, 0:HD] + qk[1:2, HD:2 * HD],
        "conv_w": jnp.sum(dcw, axis=0)[0:CK],
        "conv_b": cs[0:1],
        "conv_ln_g": cs[1:2],
        "conv_ln_b": cs[2:3],
        "ffn2_norm": jnp.sum(dgn_ffn2, axis=0),
    }
    big = {"wg1": dwg1, "wu1": dwu1, "wd1": dwd1, "win": dwin, "wout": dwout,
           "wg2": dwg2, "wu2": dwu2, "wd2": dwd2}
    return loss, gx, big, small


HBM = pl.BlockSpec(memory_space=pltpu.HBM)
VMEM = pl.BlockSpec(memory_space=pltpu.VMEM)


def _place():
    return lax.axis_index("x"), lax.axis_index("y"), lax.axis_index("c")


class _GatherCarry:
    def __init__(self, shards):
        nt = len(shards)
        self.shards = shards
        self.in_arrays = [s for s, _ in shards]
        self.in_specs = [VMEM] * nt
        self.out_shape = [jax.ShapeDtypeStruct((NDEV * s.shape[0], s.shape[1]), dt) for s, dt in shards]
        self.out_specs = [HBM] * nt
        self.scratch = ([pltpu.VMEM(s.shape, dt) for s, dt in shards]
                        + [pltpu.SemaphoreType.DMA((nt, 7)), pltpu.SemaphoreType.DMA((nt, 7)),
                           pltpu.SemaphoreType.DMA((nt,))])

    def _copies(self, outs, scr):
        nt = len(self.shards)
        stages = scr[:nt]
        send_sems, recv_sems, local_sems = scr[nt:]
        x, y, c = _place()
        me, sibling = (x, y, c), (x, y, 1 - c)
        xn, yn, diag = (1 - x, y, c), (x, 1 - y, c), (1 - x, 1 - y, c)
        via = (x ^ c, y ^ (1 - c), c)
        onto = (x ^ (1 - c), y ^ c, c)

        def rows(t, px, py, pc):
            r = self.shards[t][0].shape[0]
            return outs[t].at[pl.ds((4 * px + 2 * py + pc) * r, r), :]

        def copy(t, k, block, to, src=None):
            return pltpu.make_async_remote_copy(
                src_ref=rows(t, *block) if src is None else src, dst_ref=rows(t, *block),
                send_sem=send_sems.at[t, k], recv_sem=recv_sems.at[t, k],
                device_id=to, device_id_type=MESH)

        sib = lambda b: (b[0], b[1], 1 - c)
        return dict(
            local=[pltpu.make_async_copy(stages[t], rows(t, *me), local_sems.at[t]) for t in range(nt)],
            own=[[copy(t, 0, me, sibling, src=stages[t]), copy(t, 1, me, xn, src=stages[t]),
                  copy(t, 2, me, yn, src=stages[t])] for t in range(nt)],
            relay=[copy(t, 3, via, onto) for t in range(nt)],
            down=[[copy(t, 4, xn, sibling), copy(t, 5, yn, sibling)] for t in range(nt)],
            down_diag=[copy(t, 6, diag, sibling) for t in range(nt)],
            got_xy=[[copy(t, 1, xn, me), copy(t, 2, yn, me)] for t in range(nt)],
            got_diag=[copy(t, 3, diag, me) for t in range(nt)],
            got_sib=[[copy(t, 0, sibling, me), copy(t, 4, sib(xn), me), copy(t, 5, sib(yn), me),
                      copy(t, 6, sib(diag), me)] for t in range(nt)])

    def start(self, ins, outs, scr):
        cps = self._copies(outs, scr)
        for t, (_, dt) in enumerate(self.shards):
            scr[t][...] = ins[t][...].astype(dt)
            for cp in [cps["local"][t]] + cps["own"][t]:
                cp.start()

    def mid(self, ins, outs, scr):
        cps = self._copies(outs, scr)
        for t in range(len(self.shards)):
            for cp in cps["got_xy"][t]:
                cp.wait_recv()
            for cp in [cps["relay"][t]] + cps["down"][t]:
                cp.start()

    def finish(self, ins, outs, scr):
        cps = self._copies(outs, scr)
        for t in range(len(self.shards)):
            cps["got_diag"][t].wait_recv()
            cps["down_diag"][t].start()
        for t in range(len(self.shards)):
            for cp in cps["got_sib"][t]:
                cp.wait_recv()
            for cp in cps["own"][t] + [cps["relay"][t]] + cps["down"][t] + [cps["down_diag"][t]]:
                cp.wait_send()
            cps["local"][t].wait()


def _run_carry(carry, name):
    def body(*refs):
        n_in, n_out = len(carry.in_arrays), len(carry.out_shape)
        ins, outs, scr = refs[:n_in], refs[n_in:n_in + n_out], refs[n_in + n_out:]
        carry.start(ins, outs, scr)
        if hasattr(carry, "mid"):
            carry.mid(ins, outs, scr)
        carry.finish(ins, outs, scr)

    return pl.pallas_call(
        body, in_specs=carry.in_specs, out_specs=carry.out_specs, out_shape=carry.out_shape,
        scratch_shapes=carry.scratch, compiler_params=pltpu.CompilerParams(vmem_limit_bytes=VMEM_LIMIT),
        name=name)(*carry.in_arrays)


def _sibling_reduce(grads, name):
    nt = len(grads)
    g4 = [g.reshape(4, 2, g.shape[0] // NDEV, g.shape[1]) for g in grads]

    def body(*refs):
        ins, outs = refs[:nt], refs[nt:2 * nt]
        recv, own = refs[2 * nt:3 * nt], refs[3 * nt:4 * nt]
        send_sems, recv_sems, load_sems, store_sems = refs[4 * nt:]
        x, y, c = _place()
        sends = [pltpu.make_async_remote_copy(
            src_ref=ins[t].at[:, 1 - c], dst_ref=recv[t], send_sem=send_sems.at[t], recv_sem=recv_sems.at[t],
            device_id=(x, y, 1 - c), device_id_type=MESH) for t in range(nt)]
        loads = [pltpu.make_async_copy(ins[t].at[:, c], own[t], load_sems.at[t]) for t in range(nt)]
        stores = [pltpu.make_async_copy(own[t], outs[t], store_sems.at[t]) for t in range(nt)]
        for cp in sends + loads:
            cp.start()
        for t in range(nt):
            loads[t].wait()
            sends[t].wait_recv()
            for q in range(4):
                own[t][q] = (own[t][q].astype(F32) + recv[t][q].astype(F32)).astype(BF16)
            stores[t].start()
        for t in range(nt):
            sends[t].wait_send()
            stores[t].wait()

    blocks = [(4,) + g.shape[2:] for g in g4]
    return pl.pallas_call(
        body, in_specs=[HBM] * nt, out_specs=[HBM] * nt,
        out_shape=[jax.ShapeDtypeStruct(b, BF16) for b in blocks],
        scratch_shapes=[pltpu.VMEM(b, BF16) for b in blocks] * 2 + [pltpu.SemaphoreType.DMA((nt,))] * 4,
        compiler_params=pltpu.CompilerParams(vmem_limit_bytes=VMEM_LIMIT), name=name)(*g4)


class _ExchangeCarry:
    def __init__(self, names, parts):
        nt = len(parts)
        self.names = names
        self.in_arrays = list(parts)
        self.in_specs = [HBM] * nt
        self.out_shape = [jax.ShapeDtypeStruct((3,) + p.shape[1:], BF16) for p in parts]
        self.out_specs = [HBM] * nt
        self.scratch = ([pltpu.VMEM(p.shape[1:], BF16) for p in parts] * 2
                        + [pltpu.SemaphoreType.DMA((nt, 3)), pltpu.SemaphoreType.DMA((nt, 3)),
                           pltpu.SemaphoreType.DMA((nt,)), pltpu.SemaphoreType.DMA((nt,))])

    def _copies(self, ins, outs, scr):
        nt = len(ins)
        relayed, mine = scr[:nt], scr[nt:2 * nt]
        send_sems, recv_sems, local_sems, load_sems = scr[2 * nt:]
        x, y, c = _place()
        q = lambda cx, cy: 2 * cx + cy
        near, far = (x ^ c, y ^ (1 - c)), (x ^ (1 - c), y ^ c)

        def remote(t, k, src, dst, chip):
            return pltpu.make_async_remote_copy(
                src_ref=src, dst_ref=dst, send_sem=send_sems.at[t, k], recv_sem=recv_sems.at[t, k],
                device_id=(*chip, c), device_id_type=MESH)

        return dict(
            keep=[pltpu.make_async_copy(ins[t].at[q(x, y)], outs[t].at[0], local_sems.at[t]) for t in range(nt)],
            load=[pltpu.make_async_copy(ins[t].at[q(*far)], mine[t], load_sems.at[t]) for t in range(nt)],
            direct=[remote(t, 0, ins[t].at[q(*near)], outs[t].at[1], near) for t in range(nt)],
            relay=[remote(t, 1, ins[t].at[q(1 - x, 1 - y)], relayed[t], near) for t in range(nt)],
            merged=[remote(t, 2, mine[t], outs[t].at[2], far) for t in range(nt)],
            relayed=relayed, mine=mine)

    def start(self, ins, outs, scr):
        cps = self._copies(ins, outs, scr)
        for t in range(len(ins)):
            for kind in ("keep", "load", "direct", "relay"):
                cps[kind][t].start()

    def mid(self, ins, outs, scr):
        cps = self._copies(ins, outs, scr)
        for t in range(len(ins)):
            cps["load"][t].wait()
            cps["relay"][t].wait_recv()
            cps["mine"][t][...] = (cps["mine"][t][...].astype(F32) + cps["relayed"][t][...].astype(F32)).astype(BF16)
            cps["merged"][t].start()

    def finish(self, ins, outs, scr):
        cps = self._copies(ins, outs, scr)
        for t in range(len(ins)):
            cps["direct"][t].wait()
            cps["relay"][t].wait_send()
            cps["merged"][t].wait()
            cps["keep"][t].wait()


class _Comm:
    def __init__(self, groups):
        self.names = {tag: list(g) for tag, g in groups.items()}
        self.gathers = {tag: _GatherCarry(list(g.values())) for tag, g in groups.items()}
        self.reduced = {}

    def gathered(self, tag, outs):
        return dict(zip(self.names[tag], outs))

    def reduce_start(self, tag, grads):
        names = list(grads)
        return _ExchangeCarry(names, _sibling_reduce([grads[n] for n in names], "sibling_reduce_" + tag))

    def reduce_done(self, carry, outs):
        self.reduced.update(zip(carry.names, outs))

    def reduce_now(self, tag, grads):
        carry = self.reduce_start(tag, grads)
        self.reduce_done(carry, _run_carry(carry, "chip_exchange_" + tag))


def _adamw_math(w, g, m, v):
    m = B1 * m + (1.0 - B1) * g
    v = B2 * v + (1.0 - B2) * (g * g)
    m_hat = m / (1.0 - B1 ** STEP)
    v_hat = v / (1.0 - B2 ** STEP)
    delta = -LR * (m_hat / (jnp.sqrt(v_hat) + AEPS) + WD * w)
    return delta, m, v


def _adamw_big(recv, w, m, v, name):
    def body(r_ref, w_ref, m_ref, v_ref, g_ref, d_ref, mo_ref, vo_ref):
        g = r_ref[0].astype(F32)
        for q in range(1, 3):
            g = g + r_ref[q].astype(F32)
        d, mn, vn = _adamw_math(w_ref[...], g, m_ref[...], v_ref[...])
        g_ref[...] = g
        d_ref[...] = d
        mo_ref[...] = mn
        vo_ref[...] = vn

    rows, n = w.shape
    tr = rows // 2
    row = pl.BlockSpec((tr, n), lambda t: (t, 0))
    return pl.pallas_call(
        body, grid=(2,), in_specs=[pl.BlockSpec((3, tr, n), lambda t: (0, t, 0)), row, row, row],
        out_specs=[row] * 4, out_shape=[jax.ShapeDtypeStruct(w.shape, F32)] * 4,
        compiler_params=_cp(("parallel",)), name=name)(recv, w, m, v)


SMALL_NAMES = ("ffn1_norm", "mix_norm", "ffn2_norm", "conv_b", "conv_ln_g", "conv_ln_b", "q_norm", "k_norm")
SROWS = 16
LOSS_ROW = len(SMALL_NAMES)
CWF = 2


def _small_step(gs, loss_row, gcw, ws, ms, vs, wcw, mcw, vcw):
    ns = len(SMALL_NAMES)
    widths = [g.shape[1] for g in gs]

    def body(*refs):
        it = iter(refs)
        take = lambda n: [next(it) for _ in range(n)]
        g_refs, (loss_ref, gcw_ref) = take(ns), take(2)
        w_refs, m_refs, v_refs = take(ns), take(ns), take(ns)
        wcw_ref, mcw_ref, vcw_ref = take(3)
        outs = [take(4) for _ in range(ns)]
        cw_outs, (loss_out,) = take(4), take(1)
        send, slots, cslots, send_sems, recv_sems, csend_sems, crecv_sems = take(7)
        x, y, c = _place()
        me = 4 * x + 2 * y + c
        send[...] = jnp.zeros_like(send)
        for k in range(ns):
            send[k:k + 1, 0:widths[k]] = g_refs[k][...]
        send[LOSS_ROW:LOSS_ROW + 1, 0:128] = loss_ref[...]
        slots[me] = send[...]
        cslots[me] = gcw_ref[...]
        cps = []
        for k in range(1, NDEV):
            peer = (x ^ ((k >> 2) & 1), y ^ ((k >> 1) & 1), c ^ (k & 1))
            cps.append(pltpu.make_async_remote_copy(
                src_ref=send, dst_ref=slots.at[me], send_sem=send_sems.at[k - 1], recv_sem=recv_sems.at[k - 1],
                device_id=peer, device_id_type=MESH))
            cps.append(pltpu.make_async_remote_copy(
                src_ref=gcw_ref, dst_ref=cslots.at[me], send_sem=csend_sems.at[k - 1],
                recv_sem=crecv_sems.at[k - 1], device_id=peer, device_id_type=MESH))
        for cp in cps:
            cp.start()
        for cp in cps:
            cp.wait()
        tot = slots[0]
        ctot = cslots[0, me]
        for j in range(1, NDEV):
            tot = tot + slots[j]
            ctot = ctot + cslots[j, me]

        def step(g, w_ref, m_ref, v_ref, o):
            d, mn, vn = _adamw_math(w_ref[...], g, m_ref[...], v_ref[...])
            o[0][...], o[1][...], o[2][...], o[3][...] = g, d, mn, vn

        for k in range(ns):
            step(tot[k:k + 1, 0:widths[k]], w_refs[k], m_refs[k], v_refs[k], outs[k])
        step(ctot, wcw_ref, mcw_ref, vcw_ref, cw_outs)
        loss_out[...] = tot[LOSS_ROW:LOSS_ROW + 1, 0:128]

    args = [*gs, loss_row, gcw, *ws, *ms, *vs, wcw, mcw, vcw]
    out_shape = ([jax.ShapeDtypeStruct((1, n), F32) for n in widths for _ in range(4)]
                 + [jax.ShapeDtypeStruct((CWF, D), F32)] * 4 + [jax.ShapeDtypeStruct((1, 128), F32)])
    res = pl.pallas_call(
        body, in_specs=[VMEM] * len(args), out_specs=[VMEM] * len(out_shape), out_shape=out_shape,
        scratch_shapes=[pltpu.VMEM((SROWS, D), F32), pltpu.VMEM((NDEV, SROWS, D), F32),
                        pltpu.VMEM((NDEV, NDEV, CWF, D), F32)]
                       + [pltpu.SemaphoreType.DMA((NDEV - 1,))] * 4,
        name="small_step")(*args)
    per = [res[4 * k:4 * k + 4] for k in range(ns)]
    return per, res[4 * ns:4 * ns + 4], res[-1]


def _pack_cw(a):
    flat = a.reshape(a.shape[:-2] + (CK * HD,))
    pad = [(0, 0)] * (flat.ndim - 1) + [(0, CWF * D - CK * HD)]
    return jnp.pad(flat, pad).reshape(a.shape[:-2] + (CWF, D))


def _unpack_cw(v):
    return v.reshape(-1)[:CK * HD].reshape(1, CK, HD)


def kernel(x, ffn1_norm, ffn1_w_gate, ffn1_w_up, ffn1_w_down, mix_norm, w_in, q_norm, k_norm, conv_w, conv_b, conv_ln_g, conv_ln_b, w_out, ffn2_norm, ffn2_w_gate, ffn2_w_up, ffn2_w_down, loss_target, m_ffn1_norm, m_ffn1_w_gate, m_ffn1_w_up, m_ffn1_w_down, m_mix_norm, m_w_in, m_q_norm, m_k_norm, m_conv_w, m_conv_b, m_conv_ln_g, m_conv_ln_b, m_w_out, m_ffn2_norm, m_ffn2_w_gate, m_ffn2_w_up, m_ffn2_w_down, v_ffn1_norm, v_ffn1_w_gate, v_ffn1_w_up, v_ffn1_w_down, v_mix_norm, v_w_in, v_q_norm, v_k_norm, v_conv_w, v_conv_b, v_conv_ln_g, v_conv_ln_b, v_w_out, v_ffn2_norm, v_ffn2_w_gate, v_ffn2_w_up, v_ffn2_w_down):
    P = dict(ffn1_norm=ffn1_norm, ffn1_w_gate=ffn1_w_gate, ffn1_w_up=ffn1_w_up, ffn1_w_down=ffn1_w_down,
             mix_norm=mix_norm, w_in=w_in, q_norm=q_norm, k_norm=k_norm, conv_w=conv_w, conv_b=conv_b,
             conv_ln_g=conv_ln_g, conv_ln_b=conv_ln_b, w_out=w_out, ffn2_norm=ffn2_norm,
             ffn2_w_gate=ffn2_w_gate, ffn2_w_up=ffn2_w_up, ffn2_w_down=ffn2_w_down)
    M = dict(ffn1_norm=m_ffn1_norm, ffn1_w_gate=m_ffn1_w_gate, ffn1_w_up=m_ffn1_w_up, ffn1_w_down=m_ffn1_w_down,
             mix_norm=m_mix_norm, w_in=m_w_in, q_norm=m_q_norm, k_norm=m_k_norm, conv_w=m_conv_w, conv_b=m_conv_b,
             conv_ln_g=m_conv_ln_g, conv_ln_b=m_conv_ln_b, w_out=m_w_out, ffn2_norm=m_ffn2_norm,
             ffn2_w_gate=m_ffn2_w_gate, ffn2_w_up=m_ffn2_w_up, ffn2_w_down=m_ffn2_w_down)
    V = dict(ffn1_norm=v_ffn1_norm, ffn1_w_gate=v_ffn1_w_gate, ffn1_w_up=v_ffn1_w_up, ffn1_w_down=v_ffn1_w_down,
             mix_norm=v_mix_norm, w_in=v_w_in, q_norm=v_q_norm, k_norm=v_k_norm, conv_w=v_conv_w, conv_b=v_conv_b,
             conv_ln_g=v_conv_ln_g, conv_ln_b=v_conv_ln_b, w_out=v_w_out, ffn2_norm=v_ffn2_norm,
             ffn2_w_gate=v_ffn2_w_gate, ffn2_w_up=v_ffn2_w_up, ffn2_w_down=v_ffn2_w_down)
    order = ["ffn1_norm", "ffn1_w_gate", "ffn1_w_up", "ffn1_w_down", "mix_norm", "w_in", "q_norm", "k_norm",
             "conv_w", "conv_b", "conv_ln_g", "conv_ln_b", "w_out", "ffn2_norm", "ffn2_w_gate", "ffn2_w_up",
             "ffn2_w_down"]
    B, S, _ = x.shape
    T = B * S

    bigs = [("wg1", "ffn1_w_gate", True), ("wu1", "ffn1_w_up", True), ("wd1", "ffn1_w_down", False),
            ("win", "w_in", True), ("wout", "w_out", False),
            ("wg2", "ffn2_w_gate", True), ("wu2", "ffn2_w_up", True), ("wd2", "ffn2_w_down", False)]
    hm = lambda a, tr: jnp.transpose(a[0]) if tr else a[0]
    cw_pad = jnp.zeros((32, 128), F32).at[0:CK, 0:HD].set(conv_w[0])
    shard = {ln: (hm(P[pn], tr), BF16) for ln, pn, tr in bigs}
    gathered = _run_carry(_GatherCarry([shard["wg1"], shard["wu1"], (cw_pad, F32)]), "gather_first")
    W = {"wg1": gathered[0], "wu1": gathered[1]}
    cwg = gathered[2].reshape(NDEV, 32, 128)[:, 0:CK, 0:HD]
    W["conv_w"] = jnp.transpose(cwg, (1, 0, 2)).reshape(CK, DC)
    norms = {n: P[n] for n in SMALL_NAMES}
    comm = _Comm({"down_in": {n: shard[n] for n in ("wd1", "win")}, "wout": {"wout": shard["wout"]},
                  "ffn2": {n: shard[n] for n in ("wg2", "wu2", "wd2")}})

    loss_part, gx, _, small = _local_step(x.reshape(T, D), loss_target.reshape(T, D), norms, W, B, S, comm)

    G, Dl, Mn, Vn = {}, {}, {}, {}
    for ln, pn, tr in bigs:
        outs = _adamw_big(comm.reduced[ln], hm(P[pn], tr), hm(M[pn], tr), hm(V[pn], tr), "adamw_" + ln)
        G[pn], Dl[pn], Mn[pn], Vn[pn] = [(jnp.transpose(o) if tr else o)[None] for o in outs]

    dcw = small["conv_w"].reshape(CK, NDEV, HD).transpose(1, 0, 2)
    loss_row = jnp.zeros((1, 128), F32).at[0, 0].set(loss_part)
    per, cw_outs, loss_out = _small_step(
        [small[n] for n in SMALL_NAMES], loss_row, _pack_cw(dcw),
        [P[n] for n in SMALL_NAMES], [M[n] for n in SMALL_NAMES], [V[n] for n in SMALL_NAMES],
        _pack_cw(P["conv_w"][0]), _pack_cw(M["conv_w"][0]), _pack_cw(V["conv_w"][0]))
    loss = loss_out[0, 0]
    for n, outs in zip(SMALL_NAMES, per):
        G[n], Dl[n], Mn[n], Vn[n] = outs
    G["conv_w"], Dl["conv_w"], Mn["conv_w"], Vn["conv_w"] = [_unpack_cw(o) for o in cw_outs]

    return (loss, gx.reshape(B, S, D), *[G[n] for n in order], *[Dl[n] for n in order],
            *[Mn[n] for n in order], *[Vn[n] for n in order])
```

```python
import functools

import jax
import jax.numpy as jnp
from jax import lax
from jax.experimental import pallas as pl
from jax.experimental.pallas import tpu as pltpu

F32 = jnp.float32
BF16 = jnp.bfloat16

D = 1024
FF = 2816
HD = 64
DA = 512
DC = 512
DIN = 2560
CK = 31
BLK = 128
DILS = (1, 4, 16)
EPS = 1e-6
NDEV = 8
MESH = pl.DeviceIdType.MESH

LR, B1, B2, AEPS, WD, STEP = 0.001, 0.9, 0.999, 1e-08, 0.01, 10

NT = (((1,), (1,)), ((), ()))
TN = (((0,), (0,)), ((), ()))

VMEM_LIMIT = 56 * 1024 * 1024


def _cp(sem=None):
    return pltpu.CompilerParams(dimension_semantics=sem, vmem_limit_bytes=VMEM_LIMIT)


def _sigmoid(x):
    return 0.5 * (jnp.tanh(0.5 * x) + 1.0)


def _pallas(body, args, *, grid, in_specs, out_specs, out_shape, scratch_shapes, sem, name, carry=None):
    if carry is None:
        outs = pl.pallas_call(body, grid=grid, in_specs=in_specs, out_specs=out_specs, out_shape=out_shape,
                              scratch_shapes=scratch_shapes, compiler_params=_cp(sem), name=name)(*args)
        return outs, None
    n_in, n_out, n_scr = len(in_specs), len(out_shape), len(scratch_shapes)
    c_in, c_out = len(carry.in_arrays), len(carry.out_shape)

    def wrapped(*refs):
        ins, refs = refs[:n_in], refs[n_in:]
        cins, refs = refs[:c_in], refs[c_in:]
        outs, refs = refs[:n_out], refs[n_out:]
        couts, refs = refs[:c_out], refs[c_out:]
        scr, cscr = refs[:n_scr], refs[n_scr:]
        ids = [pl.program_id(a) for a in range(len(grid))]
        step = ids[0]
        for i, n in zip(ids[1:], grid[1:]):
            step = step * n + i
        steps = functools.reduce(lambda a, b: a * b, grid)

        @pl.when(step == 0)
        def _():
            carry.start(cins, couts, cscr)

        body(*ins, *outs, *scr)

        if hasattr(carry, "mid"):
            @pl.when(step == steps // 2)
            def _():
                carry.mid(cins, couts, cscr)

        @pl.when(step == steps - 1)
        def _():
            carry.finish(cins, couts, cscr)

    outs = pl.pallas_call(
        wrapped, grid=grid, in_specs=list(in_specs) + carry.in_specs, out_specs=list(out_specs) + carry.out_specs,
        out_shape=list(out_shape) + carry.out_shape, scratch_shapes=list(scratch_shapes) + carry.scratch,
        compiler_params=_cp(("arbitrary",) * len(grid)), name=name)(*args, *carry.in_arrays)
    return outs[:n_out], outs[n_out:]


FC = 256


def _resident(shape):
    return pl.BlockSpec(shape, lambda *_: (0,) * len(shape), pipeline_mode=pl.Buffered(1))


def _mix_out_ffn_loss(h1, attn, conv, wout, gain, wg, wu, wd, target, name):
    T = h1.shape[0]
    tm = 512
    nt = T // tm

    def body(h1_ref, at_ref, cv_ref, wo_ref, gain_ref, wg_ref, wu_ref, wd_ref, t_ref,
             h2_ref, n_ref, g_ref, u_ref, dout_ref, dyb_ref, sq_ref, a_scr):
        xv = (h1_ref[...]
              + jnp.dot(at_ref[...], wo_ref[0:DA, :], preferred_element_type=F32)
              + jnp.dot(cv_ref[...], wo_ref[DA:D, :], preferred_element_type=F32))
        h2_ref[...] = xv
        r = lax.rsqrt(jnp.mean(xv * xv, axis=-1, keepdims=True) + EPS)
        n_ref[...] = (xv * r * gain_ref[...]).astype(BF16)
        for c in range(FF // FC):
            cols = slice(c * FC, (c + 1) * FC)
            nb = n_ref[...]
            g = lax.dot_general(nb, wg_ref[cols, :], NT, preferred_element_type=F32)
            u = lax.dot_general(nb, wu_ref[cols, :], NT, preferred_element_type=F32)
            g_ref[:, cols] = g.astype(BF16)
            u_ref[:, cols] = u.astype(BF16)
            a_scr[:, cols] = (g * _sigmoid(g) * u).astype(BF16)
        e = h2_ref[...] + 0.5 * jnp.dot(a_scr[...], wd_ref[...], preferred_element_type=F32) - t_ref[...]
        dout = e * (1.0 / D)
        dout_ref[...] = dout
        dyb_ref[...] = (0.5 * dout).astype(BF16)
        sq_ref[...] = jnp.sum(e * e, axis=0, keepdims=True)[None]

    row = pl.BlockSpec((tm, D), lambda t: (t, 0))
    half = pl.BlockSpec((tm, DA), lambda t: (t, 0))
    wide = pl.BlockSpec((tm, FF), lambda t: (t, 0))
    outs, _ = _pallas(
        body, (h1, attn, conv, wout, gain, wg, wu, wd, target), grid=(nt,),
        in_specs=[row, half, half, _resident((D, D)), _resident((1, D)), _resident((FF, D)), _resident((FF, D)),
                  _resident((FF, D)), row],
        out_specs=[row, row, wide, wide, row, row, pl.BlockSpec((1, 1, D), lambda t: (t, 0, 0))],
        out_shape=[jax.ShapeDtypeStruct((T, D), F32), jax.ShapeDtypeStruct((T, D), BF16)]
                  + [jax.ShapeDtypeStruct((T, FF), BF16)] * 2
                  + [jax.ShapeDtypeStruct((T, D), F32), jax.ShapeDtypeStruct((T, D), BF16),
                     jax.ShapeDtypeStruct((nt, 1, D), F32)],
        scratch_shapes=[pltpu.VMEM((tm, FF), BF16)], sem=("parallel",), name=name)
    return outs


def _ffn_gate_up(x, gain, wg, wu, name, carry=None):
    T = x.shape[0]
    tm = 512

    def body(x_ref, gain_ref, wg_ref, wu_ref, n_ref, g_ref, u_ref, a_ref):
        xv = x_ref[...]
        r = lax.rsqrt(jnp.mean(xv * xv, axis=-1, keepdims=True) + EPS)
        n_ref[...] = (xv * r * gain_ref[...]).astype(BF16)
        for c in range(FF // FC):
            cols = slice(c * FC, (c + 1) * FC)
            nb = n_ref[...]
            g = lax.dot_general(nb, wg_ref[cols, :], NT, preferred_element_type=F32)
            u = lax.dot_general(nb, wu_ref[cols, :], NT, preferred_element_type=F32)
            g_ref[:, cols] = g.astype(BF16)
            u_ref[:, cols] = u.astype(BF16)
            a_ref[:, cols] = (g * _sigmoid(g) * u).astype(BF16)

    row = pl.BlockSpec((tm, D), lambda t: (t, 0))
    wide = pl.BlockSpec((tm, FF), lambda t: (t, 0))
    return _pallas(
        body, (x, gain, wg, wu), grid=(T // tm,),
        in_specs=[row, _resident((1, D)), _resident((FF, D)), _resident((FF, D))],
        out_specs=[row, wide, wide, wide],
        out_shape=[jax.ShapeDtypeStruct((T, D), BF16)] + [jax.ShapeDtypeStruct((T, FF), BF16)] * 3,
        scratch_shapes=[], sem=("parallel",), name=name, carry=carry)


def _ffn_down(x, a, wd, name, carry=None):
    T = x.shape[0]
    tm = 512

    def body(x_ref, a_ref, wd_ref, h_ref):
        h_ref[...] = x_ref[...] + 0.5 * jnp.dot(a_ref[...], wd_ref[...], preferred_element_type=F32)

    row = pl.BlockSpec((tm, D), lambda t: (t, 0))
    wide = pl.BlockSpec((tm, FF), lambda t: (t, 0))
    return _pallas(
        body, (x, a, wd), grid=(T // tm,), in_specs=[row, wide, _resident((FF, D))],
        out_specs=[row], out_shape=[jax.ShapeDtypeStruct((T, D), F32)],
        scratch_shapes=[], sem=("parallel",), name=name, carry=carry)


def _ffn_bwd_act(dyb, g, u, x, dout, gain, wg, wu, wd, name, carry=None):
    T = x.shape[0]
    tm = 256
    nt = T // tm

    def body(dy_ref, g_ref, u_ref, x_ref, dout_ref, gain_ref, wg_ref, wu_ref, wd_ref,
             dg_ref, du_ref, dx_ref, dgn_ref):
        for c in range(FF // FC):
            cols = slice(c * FC, (c + 1) * FC)
            da = lax.dot_general(dy_ref[...], wd_ref[cols, :], NT, preferred_element_type=F32)
            gv = g_ref[:, cols].astype(F32)
            uv = u_ref[:, cols].astype(F32)
            sg = _sigmoid(gv)
            dg_ref[:, cols] = (da * uv * (sg * (1.0 + gv * (1.0 - sg)))).astype(BF16)
            du_ref[:, cols] = (da * (gv * sg)).astype(BF16)
        dn = (jnp.dot(dg_ref[...], wg_ref[...], preferred_element_type=F32)
              + jnp.dot(du_ref[...], wu_ref[...], preferred_element_type=F32))
        dx, dgain = _rms_bwd_rows(dn, x_ref[...], gain_ref[...])
        dx_ref[...] = dout_ref[...] + dx
        dgn_ref[...] = dgain[None]

    row = pl.BlockSpec((tm, D), lambda t: (t, 0))
    wide = pl.BlockSpec((tm, FF), lambda t: (t, 0))
    return _pallas(
        body, (dyb, g, u, x, dout, gain, wg, wu, wd), grid=(nt,),
        in_specs=[row, wide, wide, row, row, _resident((1, D)), _resident((FF, D)), _resident((FF, D)),
                  _resident((FF, D))],
        out_specs=[wide, wide, row, pl.BlockSpec((1, 1, D), lambda t: (t, 0, 0))],
        out_shape=[jax.ShapeDtypeStruct((T, FF), BF16)] * 2
                  + [jax.ShapeDtypeStruct((T, D), F32), jax.ShapeDtypeStruct((nt, 1, D), F32)],
        scratch_shapes=[], sem=("parallel",), name=name, carry=carry)


def _ffn_bwd_w(lhs, rhs, name, carry=None):
    T = rhs.shape[0]
    tf = 256
    lhs = lhs if isinstance(lhs, tuple) else (lhs,)

    def body(*refs):
        r_ref, dw_ref = refs[-2:]
        if len(lhs) == 2:
            gv = refs[0][...].astype(F32)
            lv = (gv * _sigmoid(gv) * refs[1][...].astype(F32)).astype(BF16)
        else:
            lv = refs[0][...]
        dw_ref[...] = lax.dot_general(lv, r_ref[...], TN, preferred_element_type=F32).astype(BF16)

    (dw,), got = _pallas(
        body, (*lhs, rhs), grid=(FF // tf,),
        in_specs=[pl.BlockSpec((T, tf), lambda f: (0, f))] * len(lhs) + [_resident((T, D))],
        out_specs=[pl.BlockSpec((tf, D), lambda f: (f, 0))], out_shape=[jax.ShapeDtypeStruct((FF, D), BF16)],
        scratch_shapes=[], sem=("parallel",), name=name, carry=carry)
    return dw, got


def _rms_bwd_rows(dn, xv, gain):
    r = lax.rsqrt(jnp.mean(xv * xv, axis=-1, keepdims=True) + EPS)
    xhat = xv * r
    dxhat = dn * gain
    dx = r * (dxhat - xhat * jnp.mean(dxhat * xhat, axis=-1, keepdims=True))
    return dx, jnp.sum(dn * xhat, axis=0, keepdims=True)


def _mix_in(h, gain, win, carry=None):
    T = h.shape[0]
    tm = 512

    def body(h_ref, gain_ref, w_ref, u_ref, n_ref):
        xv = h_ref[...]
        r = lax.rsqrt(jnp.mean(xv * xv, axis=-1, keepdims=True) + EPS)
        nb = (xv * r * gain_ref[...]).astype(BF16)
        n_ref[...] = nb
        u_ref[...] = lax.dot_general(nb, w_ref[...], NT, preferred_element_type=F32).astype(BF16)

    row = pl.BlockSpec((tm, D), lambda t: (t, 0))
    return _pallas(
        body, (h, gain, win), grid=(T // tm,),
        in_specs=[row, _resident((1, D)), _resident((DIN, D))],
        out_specs=[pl.BlockSpec((tm, DIN), lambda t: (t, 0)), row],
        out_shape=[jax.ShapeDtypeStruct((T, DIN), BF16), jax.ShapeDtypeStruct((T, D), BF16)],
        scratch_shapes=[], sem=("parallel",), name="mix_in", carry=carry)


def _mix_out_bwd(dh, attn, conv, wout):
    T = dh.shape[0]
    tm = 512
    nt = T // tm

    def body(dh_ref, a_ref, c_ref, w_ref, da_ref, dc_ref, dw_ref, acc_scr):
        t = pl.program_id(0)

        @pl.when(t == 0)
        def _():
            acc_scr[...] = jnp.zeros_like(acc_scr)

        dhb = dh_ref[...].astype(BF16)
        dmix = lax.dot_general(dhb, w_ref[...], NT, preferred_element_type=F32)
        da_ref[...] = dmix[:, 0:DA].astype(BF16)
        dc_ref[...] = dmix[:, DA:D].astype(BF16)
        acc_scr[0:DA, :] += lax.dot_general(a_ref[...], dhb, TN, preferred_element_type=F32)
        acc_scr[DA:D, :] += lax.dot_general(c_ref[...], dhb, TN, preferred_element_type=F32)

        @pl.when(t == nt - 1)
        def _():
            dw_ref[...] = acc_scr[...].astype(BF16)

    row = pl.BlockSpec((tm, D), lambda t: (t, 0))
    half = pl.BlockSpec((tm, DA), lambda t: (t, 0))
    full = pl.BlockSpec((D, D), lambda t: (0, 0))
    return pl.pallas_call(
        body, grid=(nt,), in_specs=[row, half, half, full], out_specs=[half, half, full],
        out_shape=[jax.ShapeDtypeStruct((T, DA), BF16)] * 2 + [jax.ShapeDtypeStruct((D, D), BF16)],
        scratch_shapes=[pltpu.VMEM((D, D), F32)],
        compiler_params=_cp(("arbitrary",)), name="mix_out_bwd")(dh, attn, conv, wout)


def _mix_in_bwd(dparts, win, nb, h, dh, gain):
    T = h.shape[0]
    tm = 512
    nt = T // tm

    def body(d0, d1, d2, d3, d4, w_ref, n_ref, h_ref, dh_ref, gain_ref,
             dw_ref, dx_ref, dyb_ref, dg_ref, acc_scr):
        t = pl.program_id(0)

        @pl.when(t == 0)
        def _():
            acc_scr[...] = jnp.zeros_like(acc_scr)

        n = n_ref[...]
        dn = jnp.zeros((tm, D), F32)
        for i, d_ref in enumerate((d0, d1, d2, d3, d4)):
            dv = d_ref[...]
            dn = dn + jnp.dot(dv, w_ref[i * DA:(i + 1) * DA, :], preferred_element_type=F32)
            acc_scr[i * DA:(i + 1) * DA, :] += lax.dot_general(dv, n, TN, preferred_element_type=F32)
        dx, dgain = _rms_bwd_rows(dn, h_ref[...], gain_ref[...])
        tot = dh_ref[...] + dx
        dx_ref[...] = tot
        dyb_ref[...] = (0.5 * tot).astype(BF16)
        dg_ref[...] = dgain[None]

        @pl.when(t == nt - 1)
        def _():
            dw_ref[...] = acc_scr[...].astype(BF16)

    row = pl.BlockSpec((tm, D), lambda t: (t, 0))
    half = pl.BlockSpec((tm, DA), lambda t: (t, 0))
    full = pl.BlockSpec((DIN, D), lambda t: (0, 0))
    return pl.pallas_call(
        body, grid=(nt,),
        in_specs=[half] * 5 + [full, row, row, row, pl.BlockSpec((1, D), lambda t: (0, 0))],
        out_specs=[full, row, row, pl.BlockSpec((1, 1, D), lambda t: (t, 0, 0))],
        out_shape=[jax.ShapeDtypeStruct((DIN, D), BF16), jax.ShapeDtypeStruct((T, D), F32),
                   jax.ShapeDtypeStruct((T, D), BF16), jax.ShapeDtypeStruct((nt, 1, D), F32)],
        scratch_shapes=[pltpu.VMEM((DIN, D), F32)],
        compiler_params=_cp(("arbitrary",)), name="mix_in_bwd")(*dparts, win, nb, h, dh, gain)


def _head_masks():
    lane = lax.broadcasted_iota(jnp.int32, (1, 2 * HD), 1)
    m0 = lane < HD
    return m0, jnp.logical_not(m0)


def _stack_heads(v, m0, m1):
    z = jnp.zeros_like(v)
    return jnp.concatenate([jnp.where(m0, v, z), jnp.where(m1, v, z)], axis=0)


def _unstack_heads(v2, m0):
    return jnp.where(m0, v2[0:BLK], v2[BLK:2 * BLK])


def _head_sums(xv):
    ri = lax.broadcasted_iota(jnp.int32, (2 * HD, 2 * HD), 0)
    ci = lax.broadcasted_iota(jnp.int32, (2 * HD, 2 * HD), 1)
    ones = jnp.where((ri < HD) == (ci < HD), 1.0, 0.0).astype(BF16)
    hi = xv.astype(BF16)
    lo = (xv - hi.astype(F32)).astype(BF16)
    return (jnp.dot(hi, ones, preferred_element_type=F32) + jnp.dot(lo, ones, preferred_element_type=F32))


def _head_rms(xv):
    return lax.rsqrt(_head_sums(xv * xv) * (1.0 / HD) + EPS)


def _band_mask(first):
    qi = lax.broadcasted_iota(jnp.int32, (BLK, 2 * BLK), 0)
    ci = lax.broadcasted_iota(jnp.int32, (BLK, 2 * BLK), 1)
    band = (ci >= qi) & (ci <= qi + BLK)
    return band & ((ci >= BLK) | jnp.logical_not(first))


def _block_rows(j, d, seg):
    r, n = j // seg, j % seg
    start = r + (d * BLK) * n
    first = n == 0
    prev = jnp.where(first, start, start - d * BLK)
    return pl.ds(start, BLK, stride=d), pl.ds(prev, BLK, stride=d), first


def _block_keys(refs, cur, prev, first, single):
    if single:
        qi = lax.broadcasted_iota(jnp.int32, (BLK, BLK), 0)
        ci = lax.broadcasted_iota(jnp.int32, (BLK, BLK), 1)
        return [r[cur, :].astype(BF16) for r in refs], ci <= qi
    return ([jnp.concatenate([r[prev, :], r[cur, :]], axis=0).astype(BF16) for r in refs], _band_mask(first))


def _attn_fwd(u, qg2, kg2, B, S, carry=None):
    T = B * S
    NB = S // BLK
    scale = HD ** -0.5

    def body(q_ref, k_ref, v_ref, qg_ref, kg_ref, o_ref, lse_ref, qn, kn, vn, os_, ls_):
        m0, m1 = _head_masks()
        qv = q_ref[...].astype(F32)
        qn[...] = qv * _head_rms(qv) * (qg_ref[...] * scale)
        kv = k_ref[...].astype(F32)
        kn[...] = kv * _head_rms(kv) * kg_ref[...]
        vn[...] = v_ref[...].astype(F32)

        for i, d in enumerate(DILS):
            seg = NB // d

            def blk(j, c, i=i, d=d, seg=seg):
                cur, prev, first = _block_rows(j, d, seg)
                q2 = _stack_heads(qn[cur, :].astype(BF16), m0, m1)
                (kk, vv), mask = _block_keys((kn, vn), cur, prev, first, False)
                s = lax.dot_general(q2, kk, NT, preferred_element_type=F32)
                s = jnp.where(jnp.concatenate([mask, mask], axis=0), s, -1e30)
                mx = jnp.max(s, axis=-1, keepdims=True)
                p = jnp.exp(s - mx)
                l = jnp.sum(p, axis=-1, keepdims=True)
                o2 = jnp.dot((p * (1.0 / l)).astype(BF16), vv, preferred_element_type=F32)
                os_[i, cur, :] = _unstack_heads(o2, m0)
                ls_[i, cur, :] = _unstack_heads(mx + jnp.log(l), m0)
                return c

            lax.fori_loop(0, NB, blk, 0, unroll=8)

        def comb(c, carry):
            rows = pl.ds(pl.multiple_of(c * 256, 256), 256)
            l0, l1, l2 = ls_[0, rows, :], ls_[1, rows, :], ls_[2, rows, :]
            mx = jnp.maximum(jnp.maximum(l0, l1), l2)
            e0, e1, e2 = jnp.exp(l0 - mx), jnp.exp(l1 - mx), jnp.exp(l2 - mx)
            tot = e0 + e1 + e2
            inv = 1.0 / tot
            o = (e0 * os_[0, rows, :] + e1 * os_[1, rows, :] + e2 * os_[2, rows, :]) * inv
            o_ref[rows, :] = o.astype(BF16)
            lse_ref[rows, :] = mx + jnp.log(tot)
            return carry

        lax.fori_loop(0, S // 256, comb, 0)

    pair = 2 * HD
    blk_spec = lambda off: pl.BlockSpec((S, pair), lambda b, p, off=off: (b, off + p))
    gspec = pl.BlockSpec((1, pair), lambda b, p: (0, 0))
    return _pallas(
        body, (u, u, u, qg2, kg2), grid=(B, DA // pair),
        in_specs=[blk_spec(0), blk_spec(DA // pair), blk_spec(2 * DA // pair), gspec, gspec],
        out_specs=[blk_spec(0), blk_spec(0)],
        out_shape=[jax.ShapeDtypeStruct((T, DA), BF16), jax.ShapeDtypeStruct((T, DA), F32)],
        scratch_shapes=[pltpu.VMEM((S, pair), F32)] * 3 + [pltpu.VMEM((3, S, pair), F32)] * 2,
        sem=("parallel", "parallel"), name="attn_fwd", carry=carry)


def _attn_bwd(u, attn, dattn, lse, qg2, kg2, B, S, carry=None):
    T = B * S
    NB = S // BLK
    scale = HD ** -0.5
    pair = 2 * HD

    def body(q_ref, k_ref, v_ref, o_ref, do_ref, lse_ref, qg_ref, kg_ref,
             dq_ref, dk_ref, dv_ref, dgn_ref,
             qn, kn, vn, don, ldl, accq, acck, accv):
        m0, m1 = _head_masks()
        lane = lax.broadcasted_iota(jnp.int32, (1, pair), 1)
        qv = q_ref[...].astype(F32)
        qn[...] = qv * _head_rms(qv) * (qg_ref[...] * scale)
        kv = k_ref[...].astype(F32)
        kn[...] = kv * _head_rms(kv) * kg_ref[...]
        vn[...] = v_ref[...].astype(F32)
        dov = do_ref[...].astype(F32)
        don[...] = dov
        ldl[...] = jnp.where((lane % HD) < HD // 2, lse_ref[...], _head_sums(dov * o_ref[...].astype(F32)))
        accq[...] = jnp.zeros_like(accq)
        acck[...] = jnp.zeros_like(acck)
        accv[...] = jnp.zeros_like(accv)

        for i, d in enumerate(DILS):
            seg = NB // d

            def blk(j, c, d=d, seg=seg):
                cur, prev, first = _block_rows(j, d, seg)
                q2 = _stack_heads(qn[cur, :].astype(BF16), m0, m1)
                do2 = _stack_heads(don[cur, :].astype(BF16), m0, m1)
                (kk, vv), mask = _block_keys((kn, vn), cur, prev, first, seg == 1)
                ldv = ldl[cur, :]
                lse2 = jnp.concatenate([ldv[:, 0:1], ldv[:, HD:HD + 1]], axis=0)
                dl2 = jnp.concatenate([ldv[:, HD // 2:HD // 2 + 1], ldv[:, HD + HD // 2:HD + HD // 2 + 1]], axis=0)
                s = lax.dot_general(q2, kk, NT, preferred_element_type=F32)
                p = jnp.where(jnp.concatenate([mask, mask], axis=0), jnp.exp(s - lse2), 0.0)
                dp = lax.dot_general(do2, vv, NT, preferred_element_type=F32)
                ds = (p * (dp - dl2)).astype(BF16)
                dq_acc = _unstack_heads(jnp.dot(ds, kk, preferred_element_type=F32), m0)
                dk_acc = lax.dot_general(ds, q2, TN, preferred_element_type=F32)
                dv_acc = lax.dot_general(p.astype(BF16), do2, TN, preferred_element_type=F32)
                accq[cur, :] += dq_acc
                if seg == 1:
                    acck[cur, :] += dk_acc
                    accv[cur, :] += dv_acc
                else:
                    acck[prev, :] += dk_acc[0:BLK]
                    acck[cur, :] += dk_acc[BLK:2 * BLK]
                    accv[prev, :] += dv_acc[0:BLK]
                    accv[cur, :] += dv_acc[BLK:2 * BLK]
                return c

            lax.fori_loop(0, NB, blk, 0, unroll=8)

        def norm_bwd(x_ref, dn, gain):
            xv = x_ref[...].astype(F32)
            r = _head_rms(xv)
            xhat = xv * r
            dxhat = dn * gain
            dx = r * (dxhat - xhat * (_head_sums(dxhat * xhat) * (1.0 / HD)))
            return dx, jnp.sum(dn * xhat, axis=0, keepdims=True)

        dq, dgq = norm_bwd(q_ref, accq[...], qg_ref[...] * scale)
        dk, dgk = norm_bwd(k_ref, acck[...], kg_ref[...])
        dq_ref[...] = dq.astype(BF16)
        dk_ref[...] = dk.astype(BF16)
        dv_ref[...] = accv[...].astype(BF16)
        dgn_ref[...] = jnp.concatenate([dgq * scale, dgk, jnp.zeros((6, pair), F32)], axis=0)[None]

    blk_spec = lambda off: pl.BlockSpec((S, pair), lambda b, p, off=off: (b, off + p))
    gspec = pl.BlockSpec((1, pair), lambda b, p: (0, 0))
    np_ = DA // pair
    return _pallas(
        body, (u, u, u, attn, dattn, lse, qg2, kg2), grid=(B, np_),
        in_specs=[blk_spec(0), blk_spec(np_), blk_spec(2 * np_), blk_spec(0), blk_spec(0), blk_spec(0),
                  gspec, gspec],
        out_specs=[blk_spec(0), blk_spec(0), blk_spec(0),
                   pl.BlockSpec((1, 8, pair), lambda b, p: (b * np_ + p, 0, 0))],
        out_shape=[jax.ShapeDtypeStruct((T, DA), BF16)] * 3 + [jax.ShapeDtypeStruct((B * np_, 8, pair), F32)],
        scratch_shapes=[pltpu.VMEM((S, pair), F32)] * 8,
        sem=("parallel", "parallel"), name="attn_bwd", carry=carry)


CT = 32
CPAD = 32


def _shifted(win, offsets):
    rolled, out = {}, {}
    n = win.shape[0]
    for o in offsets:
        sub = o % 8
        if sub not in rolled:
            rolled[sub] = win if sub == 0 else pltpu.roll(win, n - sub, 0)
        out[o] = rolled[sub][o - sub:o - sub + CT, :]
    return out


def _ln_fwd(y, g, b):
    mu = jnp.mean(y, axis=-1, keepdims=True)
    yc = y - mu
    rstd = lax.rsqrt(jnp.mean(yc * yc, axis=-1, keepdims=True) + EPS)
    xhat = yc * rstd
    return xhat, rstd, xhat * g + b


def _fill_glu(ca_ref, cg_ref, glu, S):
    glu[pl.ds(0, CPAD), :] = jnp.zeros((CPAD, DC), F32)

    def fill(i, c):
        rows = pl.ds(pl.multiple_of(i * 256, 256), 256)
        a = ca_ref[rows, :].astype(F32)
        gt = cg_ref[rows, :].astype(F32)
        glu[pl.ds(pl.multiple_of(CPAD + i * 256, CT), 256), :] = a * _sigmoid(gt)
        return c

    lax.fori_loop(0, S // 256, fill, 0)


def _conv_fwd(u, cw, cb, lg, lb, B, S):
    T = B * S

    def body(ca_ref, cg_ref, w_ref, b_ref, lg_ref, lb_ref, o_ref, y_ref, glu):
        _fill_glu(ca_ref, cg_ref, glu, S)

        def step(i, c):
            t0 = pl.multiple_of(i * CT, CT)
            win = glu[pl.ds(t0, 2 * CT), :]
            acc = jnp.zeros((CT, DC), F32) + b_ref[...]
            taps = _shifted(win, [k + 2 for k in range(CK)])
            for k in range(CK):
                acc = acc + taps[k + 2] * w_ref[k:k + 1, :]
            y_ref[pl.ds(t0, CT), :] = acc
            _, _, z = _ln_fwd(acc, lg_ref[...], lb_ref[...])
            o_ref[pl.ds(t0, CT), :] = (z * _sigmoid(z)).astype(BF16)
            return c

        lax.fori_loop(0, S // CT, step, 0, unroll=2)

    vec = pl.BlockSpec((1, DC), lambda b: (0, 0))
    return pl.pallas_call(
        body, grid=(B,),
        in_specs=[pl.BlockSpec((S, DC), lambda b: (b, 3)), pl.BlockSpec((S, DC), lambda b: (b, 4)),
                  pl.BlockSpec((CT, DC), lambda b: (0, 0)), vec, vec, vec],
        out_specs=[pl.BlockSpec((S, DC), lambda b: (b, 0))] * 2,
        out_shape=[jax.ShapeDtypeStruct((T, DC), BF16), jax.ShapeDtypeStruct((T, DC), F32)],
        scratch_shapes=[pltpu.VMEM((CPAD + S, DC), F32)],
        compiler_params=_cp(("parallel",)), name="conv_fwd")(u, u, cw, cb, lg, lb)


def _conv_bwd(u, y, dconv, cw, lg, lb, B, S):
    T = B * S

    def body(ca_ref, cg_ref, y_ref, dc_ref, w_ref, lg_ref, lb_ref,
             dca_ref, dcg_ref, dw_ref, ds_ref, glu, dyp, dwacc):
        _fill_glu(ca_ref, cg_ref, glu, S)
        dyp[pl.ds(S, CPAD), :] = jnp.zeros((CPAD, DC), F32)
        lgv, lbv = lg_ref[...], lb_ref[...]

        def sum8(v):
            return functools.reduce(jnp.add, [v[r:r + 8] for r in range(0, v.shape[0], 8)])

        P1 = 4 * CT

        def p1(i, carry):
            sb, sg, sl = carry
            t0 = pl.multiple_of(i * P1, P1)
            xhat, rstd, z = _ln_fwd(y_ref[pl.ds(t0, P1), :], lgv, lbv)
            sz = _sigmoid(z)
            dz = dc_ref[pl.ds(t0, P1), :].astype(F32) * (sz * (1.0 + z * (1.0 - sz)))
            dxhat = dz * lgv
            dy = rstd * (dxhat - jnp.mean(dxhat, axis=-1, keepdims=True)
                         - xhat * jnp.mean(dxhat * xhat, axis=-1, keepdims=True))
            dyp[pl.ds(t0, P1), :] = dy
            return sb + sum8(dy), sg + sum8(dz * xhat), sl + sum8(dz)

        z8 = jnp.zeros((8, DC), F32)
        sb, sg, sl = lax.fori_loop(0, S // P1, p1, (z8, z8, z8))
        rs = lambda v: jnp.sum(v, axis=0, keepdims=True)
        ds_ref[...] = jnp.concatenate([rs(sb), rs(sg), rs(sl), jnp.zeros((5, DC), F32)], axis=0)[None]

        def p2(i, c):
            t0 = pl.multiple_of(i * CT, CT)
            win = dyp[pl.ds(t0, 2 * CT), :]
            acc = jnp.zeros((CT, DC), F32)
            taps = _shifted(win, [30 - k for k in range(CK)])
            for k in range(CK):
                acc = acc + taps[30 - k] * w_ref[k:k + 1, :]
            a = ca_ref[pl.ds(t0, CT), :].astype(F32)
            sgt = _sigmoid(cg_ref[pl.ds(t0, CT), :].astype(F32))
            dca_ref[pl.ds(t0, CT), :] = (acc * sgt).astype(BF16)
            dcg_ref[pl.ds(t0, CT), :] = (acc * a * sgt * (1.0 - sgt)).astype(BF16)
            return c

        lax.fori_loop(0, S // CT, p2, 0)

        dwacc[...] = jnp.zeros_like(dwacc)

        def p3(i, c):
            t0 = pl.multiple_of(i * CT, CT)
            win = glu[pl.ds(t0, 2 * CT), :]
            dy = dyp[pl.ds(t0, CT), :]
            for k in range(CK):
                dwacc[k] += sum8(dy * win[k + 2:k + 2 + CT, :])
            return c

        lax.fori_loop(0, S // CT, p3, 0)
        dw_ref[...] = jnp.sum(dwacc[...], axis=1)[None]

    vec = pl.BlockSpec((1, DC), lambda b: (0, 0))
    seq = pl.BlockSpec((S, DC), lambda b: (b, 0))
    return pl.pallas_call(
        body, grid=(B,),
        in_specs=[pl.BlockSpec((S, DC), lambda b: (b, 3)), pl.BlockSpec((S, DC), lambda b: (b, 4)),
                  seq, seq, pl.BlockSpec((CT, DC), lambda b: (0, 0)), vec, vec],
        out_specs=[seq, seq, pl.BlockSpec((1, CT, DC), lambda b: (b, 0, 0)),
                   pl.BlockSpec((1, 8, DC), lambda b: (b, 0, 0))],
        out_shape=[jax.ShapeDtypeStruct((T, DC), BF16)] * 2
                  + [jax.ShapeDtypeStruct((B, CT, DC), F32), jax.ShapeDtypeStruct((B, 8, DC), F32)],
        scratch_shapes=[pltpu.VMEM((CPAD + S, DC), F32), pltpu.VMEM((S + CPAD, DC), F32),
                        pltpu.VMEM((CT, 8, DC), F32)],
        compiler_params=_cp(("parallel",)), name="conv_bwd")(u, u, y, dconv, cw, lg, lb)


def _local_step(x, target, norms, W, B, S, comm=None):
    qg2 = jnp.concatenate([norms["q_norm"], norms["q_norm"]], axis=1)
    kg2 = jnp.concatenate([norms["k_norm"], norms["k_norm"]], axis=1)
    cw = jnp.concatenate([W["conv_w"], jnp.zeros((1, DC), F32)], axis=0)

    W = dict(W)
    (n1, g1, u1, act1), got = _ffn_gate_up(x, norms["ffn1_norm"], W["wg1"], W["wu1"], "ffn1_gate_up",
                                           carry=comm.gathers["down_in"] if comm else None)
    if comm:
        W.update(comm.gathered("down_in", got))
    (h1,), _ = _ffn_down(x, act1, W["wd1"], "ffn1_down")
    (u, n2), _ = _mix_in(h1, norms["mix_norm"], W["win"])
    (attn, lse), got = _attn_fwd(u, qg2, kg2, B, S, carry=comm.gathers["ffn2"] if comm else None)
    if comm:
        W = dict(W, **comm.gathered("ffn2", got))
    conv, y = _conv_fwd(u, cw, norms["conv_b"], norms["conv_ln_g"], norms["conv_ln_b"], B, S)
    h2, n3, g2, u2, dout, dyb, sq = _mix_out_ffn_loss(h1, attn, conv, W["wout"], norms["ffn2_norm"],
                                                      W["wg2"], W["wu2"], W["wd2"], target, "ffn2_fwd")
    loss = (0.5 / D) * jnp.sum(sq)

    (dg2, du2, dh2, dgn_ffn2), _ = _ffn_bwd_act(dyb, g2, u2, h2, dout, norms["ffn2_norm"],
                                               W["wg2"], W["wu2"], W["wd2"], "ffn2_bwd_act")
    dwd2, _ = _ffn_bwd_w((g2, u2), dyb, "ffn2_bwd_wd")
    dwg2, _ = _ffn_bwd_w(dg2, n3, "ffn2_bwd_wg")
    dwu2, _ = _ffn_bwd_w(du2, n3, "ffn2_bwd_wu")
    dattn, dconv, dwout = _mix_out_bwd(dh2, attn, conv, W["wout"])
    carry = comm.reduce_start("ffn2", {"wg2": dwg2, "wu2": dwu2, "wd2": dwd2, "wout": dwout}) if comm else None
    (dq, dk, dv, dgn_qk), got = _attn_bwd(u, attn, dattn, lse, qg2, kg2, B, S, carry=carry)
    if comm:
        comm.reduce_done(carry, got)
    dca, dcg, dcw, dcs = _conv_bwd(u, y, dconv, cw, norms["conv_ln_g"], norms["conv_ln_b"], B, S)
    dwin, dh1, dyb1, dgn_mix = _mix_in_bwd((dq, dk, dv, dca, dcg), W["win"], n2, h1, dh2, norms["mix_norm"])
    (dg1, du1, gx, dgn_ffn1), _ = _ffn_bwd_act(dyb1, g1, u1, x, dh1, norms["ffn1_norm"],
                                              W["wg1"], W["wu1"], W["wd1"], "ffn1_bwd_act")
    carry = comm.reduce_start("win", {"win": dwin}) if comm else None
    dwd1, got = _ffn_bwd_w(act1, dyb1, "ffn1_bwd_wd", carry=carry)
    if comm:
        comm.reduce_done(carry, got)
        carry = comm.reduce_start("wd1", {"wd1": dwd1})
    dwg1, got = _ffn_bwd_w(dg1, n1, "ffn1_bwd_wg", carry=carry)
    if comm:
        comm.reduce_done(carry, got)
        carry = comm.reduce_start("wg1", {"wg1": dwg1})
    dwu1, got = _ffn_bwd_w(du1, n1, "ffn1_bwd_wu", carry=carry)
    if comm:
        comm.reduce_done(carry, got)
        comm.reduce_now("wu1", {"wu1": dwu1})

    qk = jnp.sum(dgn_qk, axis=0)
    cs = jnp.sum(dcs, axis=0)
    small = {
        "ffn1_norm": jnp.sum(dgn_ffn1, axis=0),
        "mix_norm": jnp.sum(dgn_mix, axis=0),
        "q_norm": qk[0:1, 0:HD] + qk[0:1, HD:2 * HD],
        "k_norm": qk[1:2, 0:HD] + qk[1:2, HD:2 * HD],
        "conv_w": jnp.sum(dcw, axis=0)[0:CK],
        "conv_b": cs[0:1],
        "conv_ln_g": cs[1:2],
        "conv_ln_b": cs[2:3],
        "ffn2_norm": jnp.sum(dgn_ffn2, axis=0),
    }
    big = {"wg1": dwg1, "wu1": dwu1, "wd1": dwd1, "win": dwin, "wout": dwout,
           "wg2": dwg2, "wu2": dwu2, "wd2": dwd2}
    return loss, gx, big, small


HBM = pl.BlockSpec(memory_space=pltpu.HBM)
VMEM = pl.BlockSpec(memory_space=pltpu.VMEM)


def _place():
    return lax.axis_index("x"), lax.axis_index("y"), lax.axis_index("c")


class _GatherCarry:
    def __init__(self, shards):
        nt = len(shards)
        self.shards = shards
        self.in_arrays = [s for s, _ in shards]
        self.in_specs = [VMEM] * nt
        self.out_shape = [jax.ShapeDtypeStruct((NDEV * s.shape[0], s.shape[1]), dt) for s, dt in shards]
        self.out_specs = [HBM] * nt
        self.scratch = ([pltpu.VMEM(s.shape, dt) for s, dt in shards]
                        + [pltpu.SemaphoreType.DMA((nt, 7)), pltpu.SemaphoreType.DMA((nt, 7)),
                           pltpu.SemaphoreType.DMA((nt,))])

    def _copies(self, outs, scr):
        nt = len(self.shards)
        stages = scr[:nt]
        send_sems, recv_sems, local_sems = scr[nt:]
        x, y, c = _place()
        me, sibling = (x, y, c), (x, y, 1 - c)
        xn, yn, diag = (1 - x, y, c), (x, 1 - y, c), (1 - x, 1 - y, c)
        via = (x ^ c, y ^ (1 - c), c)
        onto = (x ^ (1 - c), y ^ c, c)

        def rows(t, px, py, pc):
            r = self.shards[t][0].shape[0]
            return outs[t].at[pl.ds((4 * px + 2 * py + pc) * r, r), :]

        def copy(t, k, block, to, src=None):
            return pltpu.make_async_remote_copy(
                src_ref=rows(t, *block) if src is None else src, dst_ref=rows(t, *block),
                send_sem=send_sems.at[t, k], recv_sem=recv_sems.at[t, k],
                device_id=to, device_id_type=MESH)

        sib = lambda b: (b[0], b[1], 1 - c)
        return dict(
            local=[pltpu.make_async_copy(stages[t], rows(t, *me), local_sems.at[t]) for t in range(nt)],
            own=[[copy(t, 0, me, sibling, src=stages[t]), copy(t, 1, me, xn, src=stages[t]),
                  copy(t, 2, me, yn, src=stages[t])] for t in range(nt)],
            relay=[copy(t, 3, via, onto) for t in range(nt)],
            down=[[copy(t, 4, xn, sibling), copy(t, 5, yn, sibling)] for t in range(nt)],
            down_diag=[copy(t, 6, diag, sibling) for t in range(nt)],
            got_xy=[[copy(t, 1, xn, me), copy(t, 2, yn, me)] for t in range(nt)],
            got_diag=[copy(t, 3, diag, me) for t in range(nt)],
            got_sib=[[copy(t, 0, sibling, me), copy(t, 4, sib(xn), me), copy(t, 5, sib(yn), me),
                      copy(t, 6, sib(diag), me)] for t in range(nt)])

    def start(self, ins, outs, scr):
        cps = self._copies(outs, scr)
        for t, (_, dt) in enumerate(self.shards):
            scr[t][...] = ins[t][...].astype(dt)
            for cp in [cps["local"][t]] + cps["own"][t]:
                cp.start()

    def mid(self, ins, outs, scr):
        cps = self._copies(outs, scr)
        for t in range(len(self.shards)):
            for cp in cps["got_xy"][t]:
                cp.wait_recv()
            for cp in [cps["relay"][t]] + cps["down"][t]:
                cp.start()

    def finish(self, ins, outs, scr):
        cps = self._copies(outs, scr)
        for t in range(len(self.shards)):
            cps["got_diag"][t].wait_recv()
            cps["down_diag"][t].start()
        for t in range(len(self.shards)):
            for cp in cps["got_sib"][t]:
                cp.wait_recv()
            for cp in cps["own"][t] + [cps["relay"][t]] + cps["down"][t] + [cps["down_diag"][t]]:
                cp.wait_send()
            cps["local"][t].wait()


def _run_carry(carry, name):
    def body(*refs):
        n_in, n_out = len(carry.in_arrays), len(carry.out_shape)
        ins, outs, scr = refs[:n_in], refs[n_in:n_in + n_out], refs[n_in + n_out:]
        carry.start(ins, outs, scr)
        if hasattr(carry, "mid"):
            carry.mid(ins, outs, scr)
        carry.finish(ins, outs, scr)

    return pl.pallas_call(
        body, in_specs=carry.in_specs, out_specs=carry.out_specs, out_shape=carry.out_shape,
        scratch_shapes=carry.scratch, compiler_params=pltpu.CompilerParams(vmem_limit_bytes=VMEM_LIMIT),
        name=name)(*carry.in_arrays)


def _sibling_reduce(grads, name):
    nt = len(grads)
    g4 = [g.reshape(4, 2, g.shape[0] // NDEV, g.shape[1]) for g in grads]

    def body(*refs):
        ins, outs = refs[:nt], refs[nt:2 * nt]
        recv, own = refs[2 * nt:3 * nt], refs[3 * nt:4 * nt]
        send_sems, recv_sems, load_sems, store_sems = refs[4 * nt:]
        x, y, c = _place()
        sends = [pltpu.make_async_remote_copy(
            src_ref=ins[t].at[:, 1 - c], dst_ref=recv[t], send_sem=send_sems.at[t], recv_sem=recv_sems.at[t],
            device_id=(x, y, 1 - c), device_id_type=MESH) for t in range(nt)]
        loads = [pltpu.make_async_copy(ins[t].at[:, c], own[t], load_sems.at[t]) for t in range(nt)]
        stores = [pltpu.make_async_copy(own[t], outs[t], store_sems.at[t]) for t in range(nt)]
        for cp in sends + loads:
            cp.start()
        for t in range(nt):
            loads[t].wait()
            sends[t].wait_recv()
            for q in range(4):
                own[t][q] = (own[t][q].astype(F32) + recv[t][q].astype(F32)).astype(BF16)
            stores[t].start()
        for t in range(nt):
            sends[t].wait_send()
            stores[t].wait()

    blocks = [(4,) + g.shape[2:] for g in g4]
    return pl.pallas_call(
        body, in_specs=[HBM] * nt, out_specs=[HBM] * nt,
        out_shape=[jax.ShapeDtypeStruct(b, BF16) for b in blocks],
        scratch_shapes=[pltpu.VMEM(b, BF16) for b in blocks] * 2 + [pltpu.SemaphoreType.DMA((nt,))] * 4,
        compiler_params=pltpu.CompilerParams(vmem_limit_bytes=VMEM_LIMIT), name=name)(*g4)


class _ExchangeCarry:
    def __init__(self, names, parts):
        nt = len(parts)
        self.names = names
        self.in_arrays = list(parts)
        self.in_specs = [HBM] * nt
        self.out_shape = [jax.ShapeDtypeStruct((3,) + p.shape[1:], BF16) for p in parts]
        self.out_specs = [HBM] * nt
        self.scratch = ([pltpu.VMEM(p.shape[1:], BF16) for p in parts] * 2
                        + [pltpu.SemaphoreType.DMA((nt, 3)), pltpu.SemaphoreType.DMA((nt, 3)),
                           pltpu.SemaphoreType.DMA((nt,)), pltpu.SemaphoreType.DMA((nt,))])

    def _copies(self, ins, outs, scr):
        nt = len(ins)
        relayed, mine = scr[:nt], scr[nt:2 * nt]
        send_sems, recv_sems, local_sems, load_sems = scr[2 * nt:]
        x, y, c = _place()
        q = lambda cx, cy: 2 * cx + cy
        near, far = (x ^ c, y ^ (1 - c)), (x ^ (1 - c), y ^ c)

        def remote(t, k, src, dst, chip):
            return pltpu.make_async_remote_copy(
                src_ref=src, dst_ref=dst, send_sem=send_sems.at[t, k], recv_sem=recv_sems.at[t, k],
                device_id=(*chip, c), device_id_type=MESH)

        return dict(
            keep=[pltpu.make_async_copy(ins[t].at[q(x, y)], outs[t].at[0], local_sems.at[t]) for t in range(nt)],
            load=[pltpu.make_async_copy(ins[t].at[q(*far)], mine[t], load_sems.at[t]) for t in range(nt)],
            direct=[remote(t, 0, ins[t].at[q(*near)], outs[t].at[1], near) for t in range(nt)],
            relay=[remote(t, 1, ins[t].at[q(1 - x, 1 - y)], relayed[t], near) for t in range(nt)],
            merged=[remote(t, 2, mine[t], outs[t].at[2], far) for t in range(nt)],
            relayed=relayed, mine=mine)

    def start(self, ins, outs, scr):
        cps = self._copies(ins, outs, scr)
        for t in range(len(ins)):
            for kind in ("keep", "load", "direct", "relay"):
                cps[kind][t].start()

    def mid(self, ins, outs, scr):
        cps = self._copies(ins, outs, scr)
        for t in range(len(ins)):
            cps["load"][t].wait()
            cps["relay"][t].wait_recv()
            cps["mine"][t][...] = (cps["mine"][t][...].astype(F32) + cps["relayed"][t][...].astype(F32)).astype(BF16)
            cps["merged"][t].start()

    def finish(self, ins, outs, scr):
        cps = self._copies(ins, outs, scr)
        for t in range(len(ins)):
            cps["direct"][t].wait()
            cps["relay"][t].wait_send()
            cps["merged"][t].wait()
            cps["keep"][t].wait()


class _Comm:
    def __init__(self, groups):
        self.names = {tag: list(g) for tag, g in groups.items()}
        self.gathers = {tag: _GatherCarry(list(g.values())) for tag, g in groups.items()}
        self.reduced = {}

    def gathered(self, tag, outs):
        return dict(zip(self.names[tag], outs))

    def reduce_start(self, tag, grads):
        names = list(grads)
        return _ExchangeCarry(names, _sibling_reduce([grads[n] for n in names], "sibling_reduce_" + tag))

    def reduce_done(self, carry, outs):
        self.reduced.update(zip(carry.names, outs))

    def reduce_now(self, tag, grads):
        carry = self.reduce_start(tag, grads)
        self.reduce_done(carry, _run_carry(carry, "chip_exchange_" + tag))


def _adamw_math(w, g, m, v):
    m = B1 * m + (1.0 - B1) * g
    v = B2 * v + (1.0 - B2) * (g * g)
    m_hat = m / (1.0 - B1 ** STEP)
    v_hat = v / (1.0 - B2 ** STEP)
    delta = -LR * (m_hat / (jnp.sqrt(v_hat) + AEPS) + WD * w)
    return delta, m, v


def _adamw_big(recv, w, m, v, name):
    def body(r_ref, w_ref, m_ref, v_ref, g_ref, d_ref, mo_ref, vo_ref):
        g = r_ref[0].astype(F32)
        for q in range(1, 3):
            g = g + r_ref[q].astype(F32)
        d, mn, vn = _adamw_math(w_ref[...], g, m_ref[...], v_ref[...])
        g_ref[...] = g
        d_ref[...] = d
        mo_ref[...] = mn
        vo_ref[...] = vn

    rows, n = w.shape
    tr = rows // 2
    row = pl.BlockSpec((tr, n), lambda t: (t, 0))
    return pl.pallas_call(
        body, grid=(2,), in_specs=[pl.BlockSpec((3, tr, n), lambda t: (0, t, 0)), row, row, row],
        out_specs=[row] * 4, out_shape=[jax.ShapeDtypeStruct(w.shape, F32)] * 4,
        compiler_params=_cp(("parallel",)), name=name)(recv, w, m, v)


SMALL_NAMES = ("ffn1_norm", "mix_norm", "ffn2_norm", "conv_b", "conv_ln_g", "conv_ln_b", "q_norm", "k_norm")
SROWS = 16
LOSS_ROW = len(SMALL_NAMES)
CWF = 2


def _small_step(gs, loss_row, gcw, ws, ms, vs, wcw, mcw, vcw):
    ns = len(SMALL_NAMES)
    widths = [g.shape[1] for g in gs]

    def body(*refs):
        it = iter(refs)
        take = lambda n: [next(it) for _ in range(n)]
        g_refs, (loss_ref, gcw_ref) = take(ns), take(2)
        w_refs, m_refs, v_refs = take(ns), take(ns), take(ns)
        wcw_ref, mcw_ref, vcw_ref = take(3)
        outs = [take(4) for _ in range(ns)]
        cw_outs, (loss_out,) = take(4), take(1)
        send, slots, cslots, send_sems, recv_sems, csend_sems, crecv_sems = take(7)
        x, y, c = _place()
        me = 4 * x + 2 * y + c
        send[...] = jnp.zeros_like(send)
        for k in range(ns):
            send[k:k + 1, 0:widths[k]] = g_refs[k][...]
        send[LOSS_ROW:LOSS_ROW + 1, 0:128] = loss_ref[...]
        slots[me] = send[...]
        cslots[me] = gcw_ref[...]
        cps = []
        for k in range(1, NDEV):
            peer = (x ^ ((k >> 2) & 1), y ^ ((k >> 1) & 1), c ^ (k & 1))
            cps.append(pltpu.make_async_remote_copy(
                src_ref=send, dst_ref=slots.at[me], send_sem=send_sems.at[k - 1], recv_sem=recv_sems.at[k - 1],
                device_id=peer, device_id_type=MESH))
            cps.append(pltpu.make_async_remote_copy(
                src_ref=gcw_ref, dst_ref=cslots.at[me], send_sem=csend_sems.at[k - 1],
                recv_sem=crecv_sems.at[k - 1], device_id=peer, device_id_type=MESH))
        for cp in cps:
            cp.start()
        for cp in cps:
            cp.wait()
        tot = slots[0]
        ctot = cslots[0, me]
        for j in range(1, NDEV):
            tot = tot + slots[j]
            ctot = ctot + cslots[j, me]

        def step(g, w_ref, m_ref, v_ref, o):
            d, mn, vn = _adamw_math(w_ref[...], g, m_ref[...], v_ref[...])
            o[0][...], o[1][...], o[2][...], o[3][...] = g, d, mn, vn

        for k in range(ns):
            step(tot[k:k + 1, 0:widths[k]], w_refs[k], m_refs[k], v_refs[k], outs[k])
        step(ctot, wcw_ref, mcw_ref, vcw_ref, cw_outs)
        loss_out[...] = tot[LOSS_ROW:LOSS_ROW + 1, 0:128]

    args = [*gs, loss_row, gcw, *ws, *ms, *vs, wcw, mcw, vcw]
    out_shape = ([jax.ShapeDtypeStruct((1, n), F32) for n in widths for _ in range(4)]
                 + [jax.ShapeDtypeStruct((CWF, D), F32)] * 4 + [jax.ShapeDtypeStruct((1, 128), F32)])
    res = pl.pallas_call(
        body, in_specs=[VMEM] * len(args), out_specs=[VMEM] * len(out_shape), out_shape=out_shape,
        scratch_shapes=[pltpu.VMEM((SROWS, D), F32), pltpu.VMEM((NDEV, SROWS, D), F32),
                        pltpu.VMEM((NDEV, NDEV, CWF, D), F32)]
                       + [pltpu.SemaphoreType.DMA((NDEV - 1,))] * 4,
        name="small_step")(*args)
    per = [res[4 * k:4 * k + 4] for k in range(ns)]
    return per, res[4 * ns:4 * ns + 4], res[-1]


def _pack_cw(a):
    flat = a.reshape(a.shape[:-2] + (CK * HD,))
    pad = [(0, 0)] * (flat.ndim - 1) + [(0, CWF * D - CK * HD)]
    return jnp.pad(flat, pad).reshape(a.shape[:-2] + (CWF, D))


def _unpack_cw(v):
    return v.reshape(-1)[:CK * HD].reshape(1, CK, HD)


def kernel(x, ffn1_norm, ffn1_w_gate, ffn1_w_up, ffn1_w_down, mix_norm, w_in, q_norm, k_norm, conv_w, conv_b, conv_ln_g, conv_ln_b, w_out, ffn2_norm, ffn2_w_gate, ffn2_w_up, ffn2_w_down, loss_target, m_ffn1_norm, m_ffn1_w_gate, m_ffn1_w_up, m_ffn1_w_down, m_mix_norm, m_w_in, m_q_norm, m_k_norm, m_conv_w, m_conv_b, m_conv_ln_g, m_conv_ln_b, m_w_out, m_ffn2_norm, m_ffn2_w_gate, m_ffn2_w_up, m_ffn2_w_down, v_ffn1_norm, v_ffn1_w_gate, v_ffn1_w_up, v_ffn1_w_down, v_mix_norm, v_w_in, v_q_norm, v_k_norm, v_conv_w, v_conv_b, v_conv_ln_g, v_conv_ln_b, v_w_out, v_ffn2_norm, v_ffn2_w_gate, v_ffn2_w_up, v_ffn2_w_down):
    P = dict(ffn1_norm=ffn1_norm, ffn1_w_gate=ffn1_w_gate, ffn1_w_up=ffn1_w_up, ffn1_w_down=ffn1_w_down,
             mix_norm=mix_norm, w_in=w_in, q_norm=q_norm, k_norm=k_norm, conv_w=conv_w, conv_b=conv_b,
             conv_ln_g=conv_ln_g, conv_ln_b=conv_ln_b, w_out=w_out, ffn2_norm=ffn2_norm,
             ffn2_w_gate=ffn2_w_gate, ffn2_w_up=ffn2_w_up, ffn2_w_down=ffn2_w_down)
    M = dict(ffn1_norm=m_ffn1_norm, ffn1_w_gate=m_ffn1_w_gate, ffn1_w_up=m_ffn1_w_up, ffn1_w_down=m_ffn1_w_down,
             mix_norm=m_mix_norm, w_in=m_w_in, q_norm=m_q_norm, k_norm=m_k_norm, conv_w=m_conv_w, conv_b=m_conv_b,
             conv_ln_g=m_conv_ln_g, conv_ln_b=m_conv_ln_b, w_out=m_w_out, ffn2_norm=m_ffn2_norm,
             ffn2_w_gate=m_ffn2_w_gate, ffn2_w_up=m_ffn2_w_up, ffn2_w_down=m_ffn2_w_down)
    V = dict(ffn1_norm=v_ffn1_norm, ffn1_w_gate=v_ffn1_w_gate, ffn1_w_up=v_ffn1_w_up, ffn1_w_down=v_ffn1_w_down,
             mix_norm=v_mix_norm, w_in=v_w_in, q_norm=v_q_norm, k_norm=v_k_norm, conv_w=v_conv_w, conv_b=v_conv_b,
             conv_ln_g=v_conv_ln_g, conv_ln_b=v_conv_ln_b, w_out=v_w_out, ffn2_norm=v_ffn2_norm,
             ffn2_w_gate=v_ffn2_w_gate, ffn2_w_up=v_ffn2_w_up, ffn2_w_down=v_ffn2_w_down)
    order = ["ffn1_norm", "ffn1_w_gate", "ffn1_w_up", "ffn1_w_down", "mix_norm", "w_in", "q_norm", "k_norm",
             "conv_w", "conv_b", "conv_ln_g", "conv_ln_b", "w_out", "ffn2_norm", "ffn2_w_gate", "ffn2_w_up",
             "ffn2_w_down"]
    B, S, _ = x.shape
    T = B * S

    bigs = [("wg1", "ffn1_w_gate", True), ("wu1", "ffn1_w_up", True), ("wd1", "ffn1_w_down", False),
            ("win", "w_in", True), ("wout", "w_out", False),
            ("wg2", "ffn2_w_gate", True), ("wu2", "ffn2_w_up", True), ("wd2", "ffn2_w_down", False)]
    hm = lambda a, tr: jnp.transpose(a[0]) if tr else a[0]
    cw_pad = jnp.zeros((32, 128), F32).at[0:CK, 0:HD].set(conv_w[0])
    shard = {ln: (hm(P[pn], tr), BF16) for ln, pn, tr in bigs}
    gathered = _run_carry(_GatherCarry([shard["wg1"], shard["wu1"], (cw_pad, F32)]), "gather_first")
    W = {"wg1": gathered[0], "wu1": gathered[1]}
    cwg = gathered[2].reshape(NDEV, 32, 128)[:, 0:CK, 0:HD]
    W["conv_w"] = jnp.transpose(cwg, (1, 0, 2)).reshape(CK, DC)
    norms = {n: P[n] for n in SMALL_NAMES}
    comm = _Comm({"down_in": {n: shard[n] for n in ("wd1", "win", "wout")},
                  "ffn2": {n: shard[n] for n in ("wg2", "wu2", "wd2")}})

    loss_part, gx, _, small = _local_step(x.reshape(T, D), loss_target.reshape(T, D), norms, W, B, S, comm)

    G, Dl, Mn, Vn = {}, {}, {}, {}
    for ln, pn, tr in bigs:
        outs = _adamw_big(comm.reduced[ln], hm(P[pn], tr), hm(M[pn], tr), hm(V[pn], tr), "adamw_" + ln)
        G[pn], Dl[pn], Mn[pn], Vn[pn] = [(jnp.transpose(o) if tr else o)[None] for o in outs]

    dcw = small["conv_w"].reshape(CK, NDEV, HD).transpose(1, 0, 2)
    loss_row = jnp.zeros((1, 128), F32).at[0, 0].set(loss_part)
    per, cw_outs, loss_out = _small_step(
        [small[n] for n in SMALL_NAMES], loss_row, _pack_cw(dcw),
        [P[n] for n in SMALL_NAMES], [M[n] for n in SMALL_NAMES], [V[n] for n in SMALL_NAMES],
        _pack_cw(P["conv_w"][0]), _pack_cw(M["conv_w"][0]), _pack_cw(V["conv_w"][0]))
    loss = loss_out[0, 0]
    for n, outs in zip(SMALL_NAMES, per):
        G[n], Dl[n], Mn[n], Vn[n] = outs
    G["conv_w"], Dl["conv_w"], Mn["conv_w"], Vn["conv_w"] = [_unpack_cw(o) for o in cw_outs]

    return (loss, gx.reshape(B, S, D), *[G[n] for n in order], *[Dl[n] for n in order],
            *[Mn[n] for n in order], *[Vn[n] for n in order])
```

```python
import functools

import jax
import jax.numpy as jnp
from jax import lax
from jax.experimental import pallas as pl
from jax.experimental.pallas import tpu as pltpu

F32 = jnp.float32
BF16 = jnp.bfloat16

D = 1024
FF = 2816
HD = 64
DA = 512
DC = 512
DIN = 2560
CK = 31
BLK = 128
DILS = (1, 4, 16)
EPS = 1e-6
NDEV = 8
MESH = pl.DeviceIdType.MESH

LR, B1, B2, AEPS, WD, STEP = 0.001, 0.9, 0.999, 1e-08, 0.01, 10

NT = (((1,), (1,)), ((), ()))
TN = (((0,), (0,)), ((), ()))

VMEM_LIMIT = 56 * 1024 * 1024


def _cp(sem=None):
    return pltpu.CompilerParams(dimension_semantics=sem, vmem_limit_bytes=VMEM_LIMIT)


def _sigmoid(x):
    return 0.5 * (jnp.tanh(0.5 * x) + 1.0)


def _pallas(body, args, *, grid, in_specs, out_specs, out_shape, scratch_shapes, sem, name, carry=None):
    if carry is None:
        outs = pl.pallas_call(body, grid=grid, in_specs=in_specs, out_specs=out_specs, out_shape=out_shape,
                              scratch_shapes=scratch_shapes, compiler_params=_cp(sem), name=name)(*args)
        return outs, None
    n_in, n_out, n_scr = len(in_specs), len(out_shape), len(scratch_shapes)
    c_in, c_out = len(carry.in_arrays), len(carry.out_shape)

    def wrapped(*refs):
        ins, refs = refs[:n_in], refs[n_in:]
        cins, refs = refs[:c_in], refs[c_in:]
        outs, refs = refs[:n_out], refs[n_out:]
        couts, refs = refs[:c_out], refs[c_out:]
        scr, cscr = refs[:n_scr], refs[n_scr:]
        ids = [pl.program_id(a) for a in range(len(grid))]
        step = ids[0]
        for i, n in zip(ids[1:], grid[1:]):
            step = step * n + i
        steps = functools.reduce(lambda a, b: a * b, grid)

        @pl.when(step == 0)
        def _():
            carry.start(cins, couts, cscr)

        body(*ins, *outs, *scr)

        if hasattr(carry, "mid"):
            @pl.when(step == int(steps * carry.mid_at))
            def _():
                carry.mid(cins, couts, cscr)

        @pl.when(step == steps - 1)
        def _():
            carry.finish(cins, couts, cscr)

    outs = pl.pallas_call(
        wrapped, grid=grid, in_specs=list(in_specs) + carry.in_specs, out_specs=list(out_specs) + carry.out_specs,
        out_shape=list(out_shape) + carry.out_shape, scratch_shapes=list(scratch_shapes) + carry.scratch,
        compiler_params=_cp(("arbitrary",) * len(grid)), name=name)(*args, *carry.in_arrays)
    return outs[:n_out], outs[n_out:]


FC = 256


def _resident(shape):
    return pl.BlockSpec(shape, lambda *_: (0,) * len(shape), pipeline_mode=pl.Buffered(1))


def _mix_out_ffn_loss(h1, attn, conv, wout, gain, wg, wu, wd, target, name):
    T = h1.shape[0]
    tm = 512
    nt = T // tm

    def body(h1_ref, at_ref, cv_ref, wo_ref, gain_ref, wg_ref, wu_ref, wd_ref, t_ref,
             h2_ref, n_ref, g_ref, u_ref, dout_ref, dyb_ref, sq_ref, a_scr):
        xv = (h1_ref[...]
              + jnp.dot(at_ref[...], wo_ref[0:DA, :], preferred_element_type=F32)
              + jnp.dot(cv_ref[...], wo_ref[DA:D, :], preferred_element_type=F32))
        h2_ref[...] = xv
        r = lax.rsqrt(jnp.mean(xv * xv, axis=-1, keepdims=True) + EPS)
        n_ref[...] = (xv * r * gain_ref[...]).astype(BF16)
        for c in range(FF // FC):
            cols = slice(c * FC, (c + 1) * FC)
            nb = n_ref[...]
            g = lax.dot_general(nb, wg_ref[cols, :], NT, preferred_element_type=F32)
            u = lax.dot_general(nb, wu_ref[cols, :], NT, preferred_element_type=F32)
            g_ref[:, cols] = g.astype(BF16)
            u_ref[:, cols] = u.astype(BF16)
            a_scr[:, cols] = (g * _sigmoid(g) * u).astype(BF16)
        e = h2_ref[...] + 0.5 * jnp.dot(a_scr[...], wd_ref[...], preferred_element_type=F32) - t_ref[...]
        dout = e * (1.0 / D)
        dout_ref[...] = dout
        dyb_ref[...] = (0.5 * dout).astype(BF16)
        sq_ref[...] = jnp.sum(e * e, axis=0, keepdims=True)[None]

    row = pl.BlockSpec((tm, D), lambda t: (t, 0))
    half = pl.BlockSpec((tm, DA), lambda t: (t, 0))
    wide = pl.BlockSpec((tm, FF), lambda t: (t, 0))
    outs, _ = _pallas(
        body, (h1, attn, conv, wout, gain, wg, wu, wd, target), grid=(nt,),
        in_specs=[row, half, half, _resident((D, D)), _resident((1, D)), _resident((FF, D)), _resident((FF, D)),
                  _resident((FF, D)), row],
        out_specs=[row, row, wide, wide, row, row, pl.BlockSpec((1, 1, D), lambda t: (t, 0, 0))],
        out_shape=[jax.ShapeDtypeStruct((T, D), F32), jax.ShapeDtypeStruct((T, D), BF16)]
                  + [jax.ShapeDtypeStruct((T, FF), BF16)] * 2
                  + [jax.ShapeDtypeStruct((T, D), F32), jax.ShapeDtypeStruct((T, D), BF16),
                     jax.ShapeDtypeStruct((nt, 1, D), F32)],
        scratch_shapes=[pltpu.VMEM((tm, FF), BF16)], sem=("parallel",), name=name)
    return outs


def _ffn_gate_up(x, gain, wg, wu, name, carry=None):
    T = x.shape[0]
    tm = 512

    def body(x_ref, gain_ref, wg_ref, wu_ref, n_ref, g_ref, u_ref, a_ref):
        xv = x_ref[...]
        r = lax.rsqrt(jnp.mean(xv * xv, axis=-1, keepdims=True) + EPS)
        n_ref[...] = (xv * r * gain_ref[...]).astype(BF16)
        for c in range(FF // FC):
            cols = slice(c * FC, (c + 1) * FC)
            nb = n_ref[...]
            g = lax.dot_general(nb, wg_ref[cols, :], NT, preferred_element_type=F32)
            u = lax.dot_general(nb, wu_ref[cols, :], NT, preferred_element_type=F32)
            g_ref[:, cols] = g.astype(BF16)
            u_ref[:, cols] = u.astype(BF16)
            a_ref[:, cols] = (g * _sigmoid(g) * u).astype(BF16)

    row = pl.BlockSpec((tm, D), lambda t: (t, 0))
    wide = pl.BlockSpec((tm, FF), lambda t: (t, 0))
    return _pallas(
        body, (x, gain, wg, wu), grid=(T // tm,),
        in_specs=[row, _resident((1, D)), _resident((FF, D)), _resident((FF, D))],
        out_specs=[row, wide, wide, wide],
        out_shape=[jax.ShapeDtypeStruct((T, D), BF16)] + [jax.ShapeDtypeStruct((T, FF), BF16)] * 3,
        scratch_shapes=[], sem=("parallel",), name=name, carry=carry)


def _ffn_down(x, a, wd, name, carry=None):
    T = x.shape[0]
    tm = 512

    def body(x_ref, a_ref, wd_ref, h_ref):
        h_ref[...] = x_ref[...] + 0.5 * jnp.dot(a_ref[...], wd_ref[...], preferred_element_type=F32)

    row = pl.BlockSpec((tm, D), lambda t: (t, 0))
    wide = pl.BlockSpec((tm, FF), lambda t: (t, 0))
    return _pallas(
        body, (x, a, wd), grid=(T // tm,), in_specs=[row, wide, _resident((FF, D))],
        out_specs=[row], out_shape=[jax.ShapeDtypeStruct((T, D), F32)],
        scratch_shapes=[], sem=("parallel",), name=name, carry=carry)


def _ffn_bwd_act(dyb, g, u, x, dout, gain, wg, wu, wd, name, carry=None):
    T = x.shape[0]
    tm = 256
    nt = T // tm

    def body(dy_ref, g_ref, u_ref, x_ref, dout_ref, gain_ref, wg_ref, wu_ref, wd_ref,
             dg_ref, du_ref, dx_ref, dgn_ref):
        for c in range(FF // FC):
            cols = slice(c * FC, (c + 1) * FC)
            da = lax.dot_general(dy_ref[...], wd_ref[cols, :], NT, preferred_element_type=F32)
            gv = g_ref[:, cols].astype(F32)
            uv = u_ref[:, cols].astype(F32)
            sg = _sigmoid(gv)
            dg_ref[:, cols] = (da * uv * (sg * (1.0 + gv * (1.0 - sg)))).astype(BF16)
            du_ref[:, cols] = (da * (gv * sg)).astype(BF16)
        dn = (jnp.dot(dg_ref[...], wg_ref[...], preferred_element_type=F32)
              + jnp.dot(du_ref[...], wu_ref[...], preferred_element_type=F32))
        dx, dgain = _rms_bwd_rows(dn, x_ref[...], gain_ref[...])
        dx_ref[...] = dout_ref[...] + dx
        dgn_ref[...] = dgain[None]

    row = pl.BlockSpec((tm, D), lambda t: (t, 0))
    wide = pl.BlockSpec((tm, FF), lambda t: (t, 0))
    return _pallas(
        body, (dyb, g, u, x, dout, gain, wg, wu, wd), grid=(nt,),
        in_specs=[row, wide, wide, row, row, _resident((1, D)), _resident((FF, D)), _resident((FF, D)),
                  _resident((FF, D))],
        out_specs=[wide, wide, row, pl.BlockSpec((1, 1, D), lambda t: (t, 0, 0))],
        out_shape=[jax.ShapeDtypeStruct((T, FF), BF16)] * 2
                  + [jax.ShapeDtypeStruct((T, D), F32), jax.ShapeDtypeStruct((nt, 1, D), F32)],
        scratch_shapes=[], sem=("parallel",), name=name, carry=carry)


def _ffn_bwd_w(lhs, rhs, name, carry=None):
    T = rhs.shape[0]
    tf = 256
    lhs = lhs if isinstance(lhs, tuple) else (lhs,)

    def body(*refs):
        r_ref, dw_ref = refs[-2:]
        if len(lhs) == 2:
            gv = refs[0][...].astype(F32)
            lv = (gv * _sigmoid(gv) * refs[1][...].astype(F32)).astype(BF16)
        else:
            lv = refs[0][...]
        dw_ref[...] = lax.dot_general(lv, r_ref[...], TN, preferred_element_type=F32).astype(BF16)

    (dw,), got = _pallas(
        body, (*lhs, rhs), grid=(FF // tf,),
        in_specs=[pl.BlockSpec((T, tf), lambda f: (0, f))] * len(lhs) + [_resident((T, D))],
        out_specs=[pl.BlockSpec((tf, D), lambda f: (f, 0))], out_shape=[jax.ShapeDtypeStruct((FF, D), BF16)],
        scratch_shapes=[], sem=("parallel",), name=name, carry=carry)
    return dw, got


def _rms_bwd_rows(dn, xv, gain):
    r = lax.rsqrt(jnp.mean(xv * xv, axis=-1, keepdims=True) + EPS)
    xhat = xv * r
    dxhat = dn * gain
    dx = r * (dxhat - xhat * jnp.mean(dxhat * xhat, axis=-1, keepdims=True))
    return dx, jnp.sum(dn * xhat, axis=0, keepdims=True)


def _mix_in(h, gain, win, carry=None):
    T = h.shape[0]
    tm = 512

    def body(h_ref, gain_ref, w_ref, u_ref, n_ref):
        xv = h_ref[...]
        r = lax.rsqrt(jnp.mean(xv * xv, axis=-1, keepdims=True) + EPS)
        nb = (xv * r * gain_ref[...]).astype(BF16)
        n_ref[...] = nb
        u_ref[...] = lax.dot_general(nb, w_ref[...], NT, preferred_element_type=F32).astype(BF16)

    row = pl.BlockSpec((tm, D), lambda t: (t, 0))
    return _pallas(
        body, (h, gain, win), grid=(T // tm,),
        in_specs=[row, _resident((1, D)), _resident((DIN, D))],
        out_specs=[pl.BlockSpec((tm, DIN), lambda t: (t, 0)), row],
        out_shape=[jax.ShapeDtypeStruct((T, DIN), BF16), jax.ShapeDtypeStruct((T, D), BF16)],
        scratch_shapes=[], sem=("parallel",), name="mix_in", carry=carry)


def _mix_out_bwd(dh, attn, conv, wout):
    T = dh.shape[0]
    tm = 512
    nt = T // tm

    def body(dh_ref, a_ref, c_ref, w_ref, da_ref, dc_ref, dw_ref, acc_scr):
        t = pl.program_id(0)

        @pl.when(t == 0)
        def _():
            acc_scr[...] = jnp.zeros_like(acc_scr)

        dhb = dh_ref[...].astype(BF16)
        dmix = lax.dot_general(dhb, w_ref[...], NT, preferred_element_type=F32)
        da_ref[...] = dmix[:, 0:DA].astype(BF16)
        dc_ref[...] = dmix[:, DA:D].astype(BF16)
        acc_scr[0:DA, :] += lax.dot_general(a_ref[...], dhb, TN, preferred_element_type=F32)
        acc_scr[DA:D, :] += lax.dot_general(c_ref[...], dhb, TN, preferred_element_type=F32)

        @pl.when(t == nt - 1)
        def _():
            dw_ref[...] = acc_scr[...].astype(BF16)

    row = pl.BlockSpec((tm, D), lambda t: (t, 0))
    half = pl.BlockSpec((tm, DA), lambda t: (t, 0))
    full = pl.BlockSpec((D, D), lambda t: (0, 0))
    return pl.pallas_call(
        body, grid=(nt,), in_specs=[row, half, half, full], out_specs=[half, half, full],
        out_shape=[jax.ShapeDtypeStruct((T, DA), BF16)] * 2 + [jax.ShapeDtypeStruct((D, D), BF16)],
        scratch_shapes=[pltpu.VMEM((D, D), F32)],
        compiler_params=_cp(("arbitrary",)), name="mix_out_bwd")(dh, attn, conv, wout)


def _mix_in_bwd(dparts, win, nb, h, dh, gain):
    T = h.shape[0]
    tm = 512
    nt = T // tm

    def body(d0, d1, d2, d3, d4, w_ref, n_ref, h_ref, dh_ref, gain_ref,
             dw_ref, dx_ref, dyb_ref, dg_ref, acc_scr):
        t = pl.program_id(0)

        @pl.when(t == 0)
        def _():
            acc_scr[...] = jnp.zeros_like(acc_scr)

        n = n_ref[...]
        dn = jnp.zeros((tm, D), F32)
        for i, d_ref in enumerate((d0, d1, d2, d3, d4)):
            dv = d_ref[...]
            dn = dn + jnp.dot(dv, w_ref[i * DA:(i + 1) * DA, :], preferred_element_type=F32)
            acc_scr[i * DA:(i + 1) * DA, :] += lax.dot_general(dv, n, TN, preferred_element_type=F32)
        dx, dgain = _rms_bwd_rows(dn, h_ref[...], gain_ref[...])
        tot = dh_ref[...] + dx
        dx_ref[...] = tot
        dyb_ref[...] = (0.5 * tot).astype(BF16)
        dg_ref[...] = dgain[None]

        @pl.when(t == nt - 1)
        def _():
            dw_ref[...] = acc_scr[...].astype(BF16)

    row = pl.BlockSpec((tm, D), lambda t: (t, 0))
    half = pl.BlockSpec((tm, DA), lambda t: (t, 0))
    full = pl.BlockSpec((DIN, D), lambda t: (0, 0))
    return pl.pallas_call(
        body, grid=(nt,),
        in_specs=[half] * 5 + [full, row, row, row, pl.BlockSpec((1, D), lambda t: (0, 0))],
        out_specs=[full, row, row, pl.BlockSpec((1, 1, D), lambda t: (t, 0, 0))],
        out_shape=[jax.ShapeDtypeStruct((DIN, D), BF16), jax.ShapeDtypeStruct((T, D), F32),
                   jax.ShapeDtypeStruct((T, D), BF16), jax.ShapeDtypeStruct((nt, 1, D), F32)],
        scratch_shapes=[pltpu.VMEM((DIN, D), F32)],
        compiler_params=_cp(("arbitrary",)), name="mix_in_bwd")(*dparts, win, nb, h, dh, gain)


def _head_masks():
    lane = lax.broadcasted_iota(jnp.int32, (1, 2 * HD), 1)
    m0 = lane < HD
    return m0, jnp.logical_not(m0)


def _stack_heads(v, m0, m1):
    z = jnp.zeros_like(v)
    return jnp.concatenate([jnp.where(m0, v, z), jnp.where(m1, v, z)], axis=0)


def _unstack_heads(v2, m0):
    return jnp.where(m0, v2[0:BLK], v2[BLK:2 * BLK])


def _head_sums(xv):
    ri = lax.broadcasted_iota(jnp.int32, (2 * HD, 2 * HD), 0)
    ci = lax.broadcasted_iota(jnp.int32, (2 * HD, 2 * HD), 1)
    ones = jnp.where((ri < HD) == (ci < HD), 1.0, 0.0).astype(BF16)
    hi = xv.astype(BF16)
    lo = (xv - hi.astype(F32)).astype(BF16)
    return (jnp.dot(hi, ones, preferred_element_type=F32) + jnp.dot(lo, ones, preferred_element_type=F32))


def _head_rms(xv):
    return lax.rsqrt(_head_sums(xv * xv) * (1.0 / HD) + EPS)


def _band_mask(first):
    qi = lax.broadcasted_iota(jnp.int32, (BLK, 2 * BLK), 0)
    ci = lax.broadcasted_iota(jnp.int32, (BLK, 2 * BLK), 1)
    band = (ci >= qi) & (ci <= qi + BLK)
    return band & ((ci >= BLK) | jnp.logical_not(first))


def _block_rows(j, d, seg):
    r, n = j // seg, j % seg
    start = r + (d * BLK) * n
    first = n == 0
    prev = jnp.where(first, start, start - d * BLK)
    return pl.ds(start, BLK, stride=d), pl.ds(prev, BLK, stride=d), first


def _block_keys(refs, cur, prev, first, single):
    if single:
        qi = lax.broadcasted_iota(jnp.int32, (BLK, BLK), 0)
        ci = lax.broadcasted_iota(jnp.int32, (BLK, BLK), 1)
        return [r[cur, :].astype(BF16) for r in refs], ci <= qi
    return ([jnp.concatenate([r[prev, :], r[cur, :]], axis=0).astype(BF16) for r in refs], _band_mask(first))


def _attn_fwd(u, qg2, kg2, B, S, carry=None):
    T = B * S
    NB = S // BLK
    scale = HD ** -0.5

    def body(q_ref, k_ref, v_ref, qg_ref, kg_ref, o_ref, lse_ref, qn, kn, vn, os_, ls_):
        m0, m1 = _head_masks()
        qv = q_ref[...].astype(F32)
        qn[...] = qv * _head_rms(qv) * (qg_ref[...] * scale)
        kv = k_ref[...].astype(F32)
        kn[...] = kv * _head_rms(kv) * kg_ref[...]
        vn[...] = v_ref[...].astype(F32)

        for i, d in enumerate(DILS):
            seg = NB // d

            def blk(j, c, i=i, d=d, seg=seg):
                cur, prev, first = _block_rows(j, d, seg)
                q2 = _stack_heads(qn[cur, :].astype(BF16), m0, m1)
                (kk, vv), mask = _block_keys((kn, vn), cur, prev, first, False)
                s = lax.dot_general(q2, kk, NT, preferred_element_type=F32)
                s = jnp.where(jnp.concatenate([mask, mask], axis=0), s, -1e30)
                mx = jnp.max(s, axis=-1, keepdims=True)
                p = jnp.exp(s - mx)
                l = jnp.sum(p, axis=-1, keepdims=True)
                o2 = jnp.dot((p * (1.0 / l)).astype(BF16), vv, preferred_element_type=F32)
                os_[i, cur, :] = _unstack_heads(o2, m0)
                ls_[i, cur, :] = _unstack_heads(mx + jnp.log(l), m0)
                return c

            lax.fori_loop(0, NB, blk, 0, unroll=8)

        def comb(c, carry):
            rows = pl.ds(pl.multiple_of(c * 256, 256), 256)
            l0, l1, l2 = ls_[0, rows, :], ls_[1, rows, :], ls_[2, rows, :]
            mx = jnp.maximum(jnp.maximum(l0, l1), l2)
            e0, e1, e2 = jnp.exp(l0 - mx), jnp.exp(l1 - mx), jnp.exp(l2 - mx)
            tot = e0 + e1 + e2
            inv = 1.0 / tot
            o = (e0 * os_[0, rows, :] + e1 * os_[1, rows, :] + e2 * os_[2, rows, :]) * inv
            o_ref[rows, :] = o.astype(BF16)
            lse_ref[rows, :] = mx + jnp.log(tot)
            return carry

        lax.fori_loop(0, S // 256, comb, 0)

    pair = 2 * HD
    blk_spec = lambda off: pl.BlockSpec((S, pair), lambda b, p, off=off: (b, off + p))
    gspec = pl.BlockSpec((1, pair), lambda b, p: (0, 0))
    return _pallas(
        body, (u, u, u, qg2, kg2), grid=(B, DA // pair),
        in_specs=[blk_spec(0), blk_spec(DA // pair), blk_spec(2 * DA // pair), gspec, gspec],
        out_specs=[blk_spec(0), blk_spec(0)],
        out_shape=[jax.ShapeDtypeStruct((T, DA), BF16), jax.ShapeDtypeStruct((T, DA), F32)],
        scratch_shapes=[pltpu.VMEM((S, pair), F32)] * 3 + [pltpu.VMEM((3, S, pair), F32)] * 2,
        sem=("parallel", "parallel"), name="attn_fwd", carry=carry)


def _attn_bwd(u, attn, dattn, lse, qg2, kg2, B, S, carry=None):
    T = B * S
    NB = S // BLK
    scale = HD ** -0.5
    pair = 2 * HD

    def body(q_ref, k_ref, v_ref, o_ref, do_ref, lse_ref, qg_ref, kg_ref,
             dq_ref, dk_ref, dv_ref, dgn_ref,
             qn, kn, vn, don, ldl, accq, acck, accv, rq, rk):
        m0, m1 = _head_masks()
        lane = lax.broadcasted_iota(jnp.int32, (1, pair), 1)
        qv = q_ref[...].astype(F32)
        rq[...] = _head_rms(qv)
        qn[...] = qv * rq[...] * (qg_ref[...] * scale)
        kv = k_ref[...].astype(F32)
        rk[...] = _head_rms(kv)
        kn[...] = kv * rk[...] * kg_ref[...]
        vn[...] = v_ref[...].astype(F32)
        dov = do_ref[...].astype(F32)
        don[...] = dov
        ldl[...] = jnp.where((lane % HD) < HD // 2, lse_ref[...], _head_sums(dov * o_ref[...].astype(F32)))
        accq[...] = jnp.zeros_like(accq)
        acck[...] = jnp.zeros_like(acck)
        accv[...] = jnp.zeros_like(accv)

        for i, d in enumerate(DILS):
            seg = NB // d

            def blk(j, c, d=d, seg=seg):
                cur, prev, first = _block_rows(j, d, seg)
                q2 = _stack_heads(qn[cur, :].astype(BF16), m0, m1)
                do2 = _stack_heads(don[cur, :].astype(BF16), m0, m1)
                (kk, vv), mask = _block_keys((kn, vn), cur, prev, first, seg == 1)
                ldv = ldl[cur, :]
                lse2 = jnp.concatenate([ldv[:, 0:1], ldv[:, HD:HD + 1]], axis=0)
                dl2 = jnp.concatenate([ldv[:, HD // 2:HD // 2 + 1], ldv[:, HD + HD // 2:HD + HD // 2 + 1]], axis=0)
                s = lax.dot_general(q2, kk, NT, preferred_element_type=F32)
                p = jnp.where(jnp.concatenate([mask, mask], axis=0), jnp.exp(s - lse2), 0.0)
                dp = lax.dot_general(do2, vv, NT, preferred_element_type=F32)
                ds = (p * (dp - dl2)).astype(BF16)
                dq_acc = _unstack_heads(jnp.dot(ds, kk, preferred_element_type=F32), m0)
                dk_acc = lax.dot_general(ds, q2, TN, preferred_element_type=F32)
                dv_acc = lax.dot_general(p.astype(BF16), do2, TN, preferred_element_type=F32)
                accq[cur, :] += dq_acc
                if seg == 1:
                    acck[cur, :] += dk_acc
                    accv[cur, :] += dv_acc
                else:
                    acck[prev, :] += dk_acc[0:BLK]
                    acck[cur, :] += dk_acc[BLK:2 * BLK]
                    accv[prev, :] += dv_acc[0:BLK]
                    accv[cur, :] += dv_acc[BLK:2 * BLK]
                return c

            lax.fori_loop(0, NB, blk, 0, unroll=8)

        def norm_bwd(x_ref, r_ref, dn, gain):
            r = r_ref[...]
            xhat = x_ref[...].astype(F32) * r
            dxhat = dn * gain
            dx = r * (dxhat - xhat * (_head_sums(dxhat * xhat) * (1.0 / HD)))
            return dx, jnp.sum(dn * xhat, axis=0, keepdims=True)

        dq, dgq = norm_bwd(q_ref, rq, accq[...], qg_ref[...] * scale)
        dk, dgk = norm_bwd(k_ref, rk, acck[...], kg_ref[...])
        dq_ref[...] = dq.astype(BF16)
        dk_ref[...] = dk.astype(BF16)
        dv_ref[...] = accv[...].astype(BF16)
        dgn_ref[...] = jnp.concatenate([dgq * scale, dgk, jnp.zeros((6, pair), F32)], axis=0)[None]

    blk_spec = lambda off: pl.BlockSpec((S, pair), lambda b, p, off=off: (b, off + p))
    gspec = pl.BlockSpec((1, pair), lambda b, p: (0, 0))
    np_ = DA // pair
    return _pallas(
        body, (u, u, u, attn, dattn, lse, qg2, kg2), grid=(B, np_),
        in_specs=[blk_spec(0), blk_spec(np_), blk_spec(2 * np_), blk_spec(0), blk_spec(0), blk_spec(0),
                  gspec, gspec],
        out_specs=[blk_spec(0), blk_spec(0), blk_spec(0),
                   pl.BlockSpec((1, 8, pair), lambda b, p: (b * np_ + p, 0, 0))],
        out_shape=[jax.ShapeDtypeStruct((T, DA), BF16)] * 3 + [jax.ShapeDtypeStruct((B * np_, 8, pair), F32)],
        scratch_shapes=[pltpu.VMEM((S, pair), F32)] * 10,
        sem=("parallel", "parallel"), name="attn_bwd", carry=carry)


CT = 32
CPAD = 32


def _shifted(win, offsets):
    rolled, out = {}, {}
    n = win.shape[0]
    for o in offsets:
        sub = o % 8
        if sub not in rolled:
            rolled[sub] = win if sub == 0 else pltpu.roll(win, n - sub, 0)
        out[o] = rolled[sub][o - sub:o - sub + CT, :]
    return out


def _ln_fwd(y, g, b):
    mu = jnp.mean(y, axis=-1, keepdims=True)
    yc = y - mu
    rstd = lax.rsqrt(jnp.mean(yc * yc, axis=-1, keepdims=True) + EPS)
    xhat = yc * rstd
    return xhat, rstd, xhat * g + b


def _fill_glu(ca_ref, cg_ref, glu, S):
    glu[pl.ds(0, CPAD), :] = jnp.zeros((CPAD, DC), F32)

    def fill(i, c):
        rows = pl.ds(pl.multiple_of(i * 256, 256), 256)
        a = ca_ref[rows, :].astype(F32)
        gt = cg_ref[rows, :].astype(F32)
        glu[pl.ds(pl.multiple_of(CPAD + i * 256, CT), 256), :] = a * _sigmoid(gt)
        return c

    lax.fori_loop(0, S // 256, fill, 0)


def _conv_fwd(u, cw, cb, lg, lb, B, S):
    T = B * S

    def body(ca_ref, cg_ref, w_ref, b_ref, lg_ref, lb_ref, o_ref, y_ref, glu):
        _fill_glu(ca_ref, cg_ref, glu, S)

        def step(i, c):
            t0 = pl.multiple_of(i * CT, CT)
            win = glu[pl.ds(t0, 2 * CT), :]
            acc = jnp.zeros((CT, DC), F32) + b_ref[...]
            taps = _shifted(win, [k + 2 for k in range(CK)])
            for k in range(CK):
                acc = acc + taps[k + 2] * w_ref[k:k + 1, :]
            y_ref[pl.ds(t0, CT), :] = acc
            _, _, z = _ln_fwd(acc, lg_ref[...], lb_ref[...])
            o_ref[pl.ds(t0, CT), :] = (z * _sigmoid(z)).astype(BF16)
            return c

        lax.fori_loop(0, S // CT, step, 0, unroll=2)

    vec = pl.BlockSpec((1, DC), lambda b: (0, 0))
    return pl.pallas_call(
        body, grid=(B,),
        in_specs=[pl.BlockSpec((S, DC), lambda b: (b, 3)), pl.BlockSpec((S, DC), lambda b: (b, 4)),
                  pl.BlockSpec((CT, DC), lambda b: (0, 0)), vec, vec, vec],
        out_specs=[pl.BlockSpec((S, DC), lambda b: (b, 0))] * 2,
        out_shape=[jax.ShapeDtypeStruct((T, DC), BF16), jax.ShapeDtypeStruct((T, DC), F32)],
        scratch_shapes=[pltpu.VMEM((CPAD + S, DC), F32)],
        compiler_params=_cp(("parallel",)), name="conv_fwd")(u, u, cw, cb, lg, lb)


def _conv_bwd(u, y, dconv, cw, lg, lb, B, S):
    T = B * S

    def body(ca_ref, cg_ref, y_ref, dc_ref, w_ref, lg_ref, lb_ref,
             dca_ref, dcg_ref, dw_ref, ds_ref, glu, dyp, dwacc):
        _fill_glu(ca_ref, cg_ref, glu, S)
        dyp[pl.ds(S, CPAD), :] = jnp.zeros((CPAD, DC), F32)
        lgv, lbv = lg_ref[...], lb_ref[...]

        def sum8(v):
            return functools.reduce(jnp.add, [v[r:r + 8] for r in range(0, v.shape[0], 8)])

        P1 = 4 * CT

        def p1(i, carry):
            sb, sg, sl = carry
            t0 = pl.multiple_of(i * P1, P1)
            xhat, rstd, z = _ln_fwd(y_ref[pl.ds(t0, P1), :], lgv, lbv)
            sz = _sigmoid(z)
            dz = dc_ref[pl.ds(t0, P1), :].astype(F32) * (sz * (1.0 + z * (1.0 - sz)))
            dxhat = dz * lgv
            dy = rstd * (dxhat - jnp.mean(dxhat, axis=-1, keepdims=True)
                         - xhat * jnp.mean(dxhat * xhat, axis=-1, keepdims=True))
            dyp[pl.ds(t0, P1), :] = dy
            return sb + sum8(dy), sg + sum8(dz * xhat), sl + sum8(dz)

        z8 = jnp.zeros((8, DC), F32)
        sb, sg, sl = lax.fori_loop(0, S // P1, p1, (z8, z8, z8))
        rs = lambda v: jnp.sum(v, axis=0, keepdims=True)
        ds_ref[...] = jnp.concatenate([rs(sb), rs(sg), rs(sl), jnp.zeros((5, DC), F32)], axis=0)[None]

        def p2(i, c):
            t0 = pl.multiple_of(i * CT, CT)
            win = dyp[pl.ds(t0, 2 * CT), :]
            acc = jnp.zeros((CT, DC), F32)
            taps = _shifted(win, [30 - k for k in range(CK)])
            for k in range(CK):
                acc = acc + taps[30 - k] * w_ref[k:k + 1, :]
            a = ca_ref[pl.ds(t0, CT), :].astype(F32)
            sgt = _sigmoid(cg_ref[pl.ds(t0, CT), :].astype(F32))
            dca_ref[pl.ds(t0, CT), :] = (acc * sgt).astype(BF16)
            dcg_ref[pl.ds(t0, CT), :] = (acc * a * sgt * (1.0 - sgt)).astype(BF16)
            return c

        lax.fori_loop(0, S // CT, p2, 0)

        dwacc[...] = jnp.zeros_like(dwacc)

        def p3(i, c):
            t0 = pl.multiple_of(i * CT, CT)
            win = glu[pl.ds(t0, 2 * CT), :]
            dy = dyp[pl.ds(t0, CT), :]
            for k in range(CK):
                dwacc[k] += sum8(dy * win[k + 2:k + 2 + CT, :])
            return c

        lax.fori_loop(0, S // CT, p3, 0)
        dw_ref[...] = jnp.sum(dwacc[...], axis=1)[None]

    vec = pl.BlockSpec((1, DC), lambda b: (0, 0))
    seq = pl.BlockSpec((S, DC), lambda b: (b, 0))
    return pl.pallas_call(
        body, grid=(B,),
        in_specs=[pl.BlockSpec((S, DC), lambda b: (b, 3)), pl.BlockSpec((S, DC), lambda b: (b, 4)),
                  seq, seq, pl.BlockSpec((CT, DC), lambda b: (0, 0)), vec, vec],
        out_specs=[seq, seq, pl.BlockSpec((1, CT, DC), lambda b: (b, 0, 0)),
                   pl.BlockSpec((1, 8, DC), lambda b: (b, 0, 0))],
        out_shape=[jax.ShapeDtypeStruct((T, DC), BF16)] * 2
                  + [jax.ShapeDtypeStruct((B, CT, DC), F32), jax.ShapeDtypeStruct((B, 8, DC), F32)],
        scratch_shapes=[pltpu.VMEM((CPAD + S, DC), F32), pltpu.VMEM((S + CPAD, DC), F32),
                        pltpu.VMEM((CT, 8, DC), F32)],
        compiler_params=_cp(("parallel",)), name="conv_bwd")(u, u, y, dconv, cw, lg, lb)


def _local_step(x, target, norms, W, B, S, comm=None):
    qg2 = jnp.concatenate([norms["q_norm"], norms["q_norm"]], axis=1)
    kg2 = jnp.concatenate([norms["k_norm"], norms["k_norm"]], axis=1)
    cw = jnp.concatenate([W["conv_w"], jnp.zeros((1, DC), F32)], axis=0)

    W = dict(W)
    (n1, g1, u1, act1), got = _ffn_gate_up(x, norms["ffn1_norm"], W["wg1"], W["wu1"], "ffn1_gate_up",
                                           carry=comm.gathers["down_in"] if comm else None)
    if comm:
        W.update(comm.gathered("down_in", got))
    (h1,), _ = _ffn_down(x, act1, W["wd1"], "ffn1_down")
    (u, n2), _ = _mix_in(h1, norms["mix_norm"], W["win"])
    (attn, lse), got = _attn_fwd(u, qg2, kg2, B, S, carry=comm.gathers["ffn2"] if comm else None)
    if comm:
        W = dict(W, **comm.gathered("ffn2", got))
    conv, y = _conv_fwd(u, cw, norms["conv_b"], norms["conv_ln_g"], norms["conv_ln_b"], B, S)
    h2, n3, g2, u2, dout, dyb, sq = _mix_out_ffn_loss(h1, attn, conv, W["wout"], norms["ffn2_norm"],
                                                      W["wg2"], W["wu2"], W["wd2"], target, "ffn2_fwd")
    loss = (0.5 / D) * jnp.sum(sq)

    (dg2, du2, dh2, dgn_ffn2), _ = _ffn_bwd_act(dyb, g2, u2, h2, dout, norms["ffn2_norm"],
                                               W["wg2"], W["wu2"], W["wd2"], "ffn2_bwd_act")
    dwd2, _ = _ffn_bwd_w((g2, u2), dyb, "ffn2_bwd_wd")
    dwg2, _ = _ffn_bwd_w(dg2, n3, "ffn2_bwd_wg")
    dwu2, _ = _ffn_bwd_w(du2, n3, "ffn2_bwd_wu")
    dattn, dconv, dwout = _mix_out_bwd(dh2, attn, conv, W["wout"])
    carry = comm.reduce_start("ffn2", {"wg2": dwg2, "wu2": dwu2, "wd2": dwd2, "wout": dwout}) if comm else None
    (dq, dk, dv, dgn_qk), got = _attn_bwd(u, attn, dattn, lse, qg2, kg2, B, S, carry=carry)
    if comm:
        comm.reduce_done(carry, got)
    dca, dcg, dcw, dcs = _conv_bwd(u, y, dconv, cw, norms["conv_ln_g"], norms["conv_ln_b"], B, S)
    dwin, dh1, dyb1, dgn_mix = _mix_in_bwd((dq, dk, dv, dca, dcg), W["win"], n2, h1, dh2, norms["mix_norm"])
    (dg1, du1, gx, dgn_ffn1), _ = _ffn_bwd_act(dyb1, g1, u1, x, dh1, norms["ffn1_norm"],
                                              W["wg1"], W["wu1"], W["wd1"], "ffn1_bwd_act")
    carry = comm.reduce_start("win", {"win": dwin}, 0.7) if comm else None
    dwd1, got = _ffn_bwd_w(act1, dyb1, "ffn1_bwd_wd", carry=carry)
    if comm:
        comm.reduce_done(carry, got)
        carry = comm.reduce_start("wd1", {"wd1": dwd1}, 0.7)
    dwg1, got = _ffn_bwd_w(dg1, n1, "ffn1_bwd_wg", carry=carry)
    if comm:
        comm.reduce_done(carry, got)
        carry = comm.reduce_start("wg1", {"wg1": dwg1}, 0.7)
    dwu1, got = _ffn_bwd_w(du1, n1, "ffn1_bwd_wu", carry=carry)
    if comm:
        comm.reduce_done(carry, got)
        comm.reduce_now("wu1", {"wu1": dwu1})

    qk = jnp.sum(dgn_qk, axis=0)
    cs = jnp.sum(dcs, axis=0)
    small = {
        "ffn1_norm": jnp.sum(dgn_ffn1, axis=0),
        "mix_norm": jnp.sum(dgn_mix, axis=0),
        "q_norm": qk[0:1, 0:HD] + qk[0:1, HD:2 * HD],
        "k_norm": qk[1:2, 0:HD] + qk[1:2, HD:2 * HD],
        "conv_w": jnp.sum(dcw, axis=0)[0:CK],
        "conv_b": cs[0:1],
        "conv_ln_g": cs[1:2],
        "conv_ln_b": cs[2:3],
        "ffn2_norm": jnp.sum(dgn_ffn2, axis=0),
    }
    big = {"wg1": dwg1, "wu1": dwu1, "wd1": dwd1, "win": dwin, "wout": dwout,
           "wg2": dwg2, "wu2": dwu2, "wd2": dwd2}
    return loss, gx, big, small


HBM = pl.BlockSpec(memory_space=pltpu.HBM)
VMEM = pl.BlockSpec(memory_space=pltpu.VMEM)


def _place():
    return lax.axis_index("x"), lax.axis_index("y"), lax.axis_index("c")


class _GatherCarry:
    def __init__(self, shards, mid_at=0.5):
        nt = len(shards)
        self.mid_at = mid_at
        self.shards = shards
        self.in_arrays = [s for s, _ in shards]
        self.in_specs = [VMEM] * nt
        self.out_shape = [jax.ShapeDtypeStruct((NDEV * s.shape[0], s.shape[1]), dt) for s, dt in shards]
        self.out_specs = [HBM] * nt
        self.scratch = ([pltpu.VMEM(s.shape, dt) for s, dt in shards]
                        + [pltpu.SemaphoreType.DMA((nt, 7)), pltpu.SemaphoreType.DMA((nt, 7)),
                           pltpu.SemaphoreType.DMA((nt,))])

    def _copies(self, outs, scr):
        nt = len(self.shards)
        stages = scr[:nt]
        send_sems, recv_sems, local_sems = scr[nt:]
        x, y, c = _place()
        me, sibling = (x, y, c), (x, y, 1 - c)
        xn, yn, diag = (1 - x, y, c), (x, 1 - y, c), (1 - x, 1 - y, c)
        via = (x ^ c, y ^ (1 - c), c)
        onto = (x ^ (1 - c), y ^ c, c)

        def rows(t, px, py, pc):
            r = self.shards[t][0].shape[0]
            return outs[t].at[pl.ds((4 * px + 2 * py + pc) * r, r), :]

        def copy(t, k, block, to, src=None):
            return pltpu.make_async_remote_copy(
                src_ref=rows(t, *block) if src is None else src, dst_ref=rows(t, *block),
                send_sem=send_sems.at[t, k], recv_sem=recv_sems.at[t, k],
                device_id=to, device_id_type=MESH)

        sib = lambda b: (b[0], b[1], 1 - c)
        return dict(
            local=[pltpu.make_async_copy(stages[t], rows(t, *me), local_sems.at[t]) for t in range(nt)],
            own=[[copy(t, 0, me, sibling, src=stages[t]), copy(t, 1, me, xn, src=stages[t]),
                  copy(t, 2, me, yn, src=stages[t])] for t in range(nt)],
            relay=[copy(t, 3, via, onto) for t in range(nt)],
            down=[[copy(t, 4, xn, sibling), copy(t, 5, yn, sibling)] for t in range(nt)],
            down_diag=[copy(t, 6, diag, sibling) for t in range(nt)],
            got_xy=[[copy(t, 1, xn, me), copy(t, 2, yn, me)] for t in range(nt)],
            got_diag=[copy(t, 3, diag, me) for t in range(nt)],
            got_sib=[[copy(t, 0, sibling, me), copy(t, 4, sib(xn), me), copy(t, 5, sib(yn), me),
                      copy(t, 6, sib(diag), me)] for t in range(nt)])

    def start(self, ins, outs, scr):
        cps = self._copies(outs, scr)
        for t, (_, dt) in enumerate(self.shards):
            scr[t][...] = ins[t][...].astype(dt)
            for cp in [cps["local"][t]] + cps["own"][t]:
                cp.start()

    def mid(self, ins, outs, scr):
        cps = self._copies(outs, scr)
        for t in range(len(self.shards)):
            for cp in cps["got_xy"][t]:
                cp.wait_recv()
            for cp in [cps["relay"][t]] + cps["down"][t]:
                cp.start()

    def finish(self, ins, outs, scr):
        cps = self._copies(outs, scr)
        for t in range(len(self.shards)):
            cps["got_diag"][t].wait_recv()
            cps["down_diag"][t].start()
        for t in range(len(self.shards)):
            for cp in cps["got_sib"][t]:
                cp.wait_recv()
            for cp in cps["own"][t] + [cps["relay"][t]] + cps["down"][t] + [cps["down_diag"][t]]:
                cp.wait_send()
            cps["local"][t].wait()


def _run_carry(carry, name):
    def body(*refs):
        n_in, n_out = len(carry.in_arrays), len(carry.out_shape)
        ins, outs, scr = refs[:n_in], refs[n_in:n_in + n_out], refs[n_in + n_out:]
        carry.start(ins, outs, scr)
        if hasattr(carry, "mid"):
            carry.mid(ins, outs, scr)
        carry.finish(ins, outs, scr)

    return pl.pallas_call(
        body, in_specs=carry.in_specs, out_specs=carry.out_specs, out_shape=carry.out_shape,
        scratch_shapes=carry.scratch, compiler_params=pltpu.CompilerParams(vmem_limit_bytes=VMEM_LIMIT),
        name=name)(*carry.in_arrays)


def _sibling_reduce(grads, name):
    nt = len(grads)
    g4 = [g.reshape(4, 2, g.shape[0] // NDEV, g.shape[1]) for g in grads]

    def body(*refs):
        ins, outs = refs[:nt], refs[nt:2 * nt]
        recv, own = refs[2 * nt:3 * nt], refs[3 * nt:4 * nt]
        send_sems, recv_sems, load_sems, store_sems = refs[4 * nt:]
        x, y, c = _place()
        sends = [pltpu.make_async_remote_copy(
            src_ref=ins[t].at[:, 1 - c], dst_ref=recv[t], send_sem=send_sems.at[t], recv_sem=recv_sems.at[t],
            device_id=(x, y, 1 - c), device_id_type=MESH) for t in range(nt)]
        loads = [pltpu.make_async_copy(ins[t].at[:, c], own[t], load_sems.at[t]) for t in range(nt)]
        stores = [pltpu.make_async_copy(own[t], outs[t], store_sems.at[t]) for t in range(nt)]
        for cp in sends + loads:
            cp.start()
        for t in range(nt):
            loads[t].wait()
            sends[t].wait_recv()
            for q in range(4):
                own[t][q] = (own[t][q].astype(F32) + recv[t][q].astype(F32)).astype(BF16)
            stores[t].start()
        for t in range(nt):
            sends[t].wait_send()
            stores[t].wait()

    blocks = [(4,) + g.shape[2:] for g in g4]
    return pl.pallas_call(
        body, in_specs=[HBM] * nt, out_specs=[HBM] * nt,
        out_shape=[jax.ShapeDtypeStruct(b, BF16) for b in blocks],
        scratch_shapes=[pltpu.VMEM(b, BF16) for b in blocks] * 2 + [pltpu.SemaphoreType.DMA((nt,))] * 4,
        compiler_params=pltpu.CompilerParams(vmem_limit_bytes=VMEM_LIMIT), name=name)(*g4)


class _ExchangeCarry:
    def __init__(self, names, parts, mid_at=0.5):
        nt = len(parts)
        self.mid_at = mid_at
        self.names = names
        self.in_arrays = list(parts)
        self.in_specs = [HBM] * nt
        self.out_shape = [jax.ShapeDtypeStruct((3,) + p.shape[1:], BF16) for p in parts]
        self.out_specs = [HBM] * nt
        self.scratch = ([pltpu.VMEM(p.shape[1:], BF16) for p in parts] * 2
                        + [pltpu.SemaphoreType.DMA((nt, 3)), pltpu.SemaphoreType.DMA((nt, 3)),
                           pltpu.SemaphoreType.DMA((nt,)), pltpu.SemaphoreType.DMA((nt,))])

    def _copies(self, ins, outs, scr):
        nt = len(ins)
        relayed, mine = scr[:nt], scr[nt:2 * nt]
        send_sems, recv_sems, local_sems, load_sems = scr[2 * nt:]
        x, y, c = _place()
        q = lambda cx, cy: 2 * cx + cy
        near, far = (x ^ c, y ^ (1 - c)), (x ^ (1 - c), y ^ c)

        def remote(t, k, src, dst, chip):
            return pltpu.make_async_remote_copy(
                src_ref=src, dst_ref=dst, send_sem=send_sems.at[t, k], recv_sem=recv_sems.at[t, k],
                device_id=(*chip, c), device_id_type=MESH)

        return dict(
            keep=[pltpu.make_async_copy(ins[t].at[q(x, y)], outs[t].at[0], local_sems.at[t]) for t in range(nt)],
            load=[pltpu.make_async_copy(ins[t].at[q(*far)], mine[t], load_sems.at[t]) for t in range(nt)],
            direct=[remote(t, 0, ins[t].at[q(*near)], outs[t].at[1], near) for t in range(nt)],
            relay=[remote(t, 1, ins[t].at[q(1 - x, 1 - y)], relayed[t], near) for t in range(nt)],
            merged=[remote(t, 2, mine[t], outs[t].at[2], far) for t in range(nt)],
            relayed=relayed, mine=mine)

    def start(self, ins, outs, scr):
        cps = self._copies(ins, outs, scr)
        for t in range(len(ins)):
            for kind in ("keep", "load", "direct", "relay"):
                cps[kind][t].start()

    def mid(self, ins, outs, scr):
        cps = self._copies(ins, outs, scr)
        for t in range(len(ins)):
            cps["load"][t].wait()
            cps["relay"][t].wait_recv()
            cps["mine"][t][...] = (cps["mine"][t][...].astype(F32) + cps["relayed"][t][...].astype(F32)).astype(BF16)
            cps["merged"][t].start()

    def finish(self, ins, outs, scr):
        cps = self._copies(ins, outs, scr)
        for t in range(len(ins)):
            cps["direct"][t].wait()
            cps["relay"][t].wait_send()
            cps["merged"][t].wait()
            cps["keep"][t].wait()


class _Comm:
    def __init__(self, groups):
        self.names = {tag: list(g) for tag, (g, _) in groups.items()}
        self.gathers = {tag: _GatherCarry(list(g.values()), mid_at) for tag, (g, mid_at) in groups.items()}
        self.reduced = {}

    def gathered(self, tag, outs):
        return dict(zip(self.names[tag], outs))

    def reduce_start(self, tag, grads, mid_at=0.5):
        names = list(grads)
        parts = _sibling_reduce([grads[n] for n in names], "sibling_reduce_" + tag)
        return _ExchangeCarry(names, parts, mid_at)

    def reduce_done(self, carry, outs):
        self.reduced.update(zip(carry.names, outs))

    def reduce_now(self, tag, grads):
        carry = self.reduce_start(tag, grads)
        self.reduce_done(carry, _run_carry(carry, "chip_exchange_" + tag))


def _adamw_math(w, g, m, v):
    m = B1 * m + (1.0 - B1) * g
    v = B2 * v + (1.0 - B2) * (g * g)
    m_hat = m / (1.0 - B1 ** STEP)
    v_hat = v / (1.0 - B2 ** STEP)
    delta = -LR * (m_hat / (jnp.sqrt(v_hat) + AEPS) + WD * w)
    return delta, m, v


def _adamw_big(recv, w, m, v, name):
    def body(r_ref, w_ref, m_ref, v_ref, g_ref, d_ref, mo_ref, vo_ref):
        g = r_ref[0].astype(F32)
        for q in range(1, 3):
            g = g + r_ref[q].astype(F32)
        d, mn, vn = _adamw_math(w_ref[...], g, m_ref[...], v_ref[...])
        g_ref[...] = g
        d_ref[...] = d
        mo_ref[...] = mn
        vo_ref[...] = vn

    rows, n = w.shape
    tr = rows // 2
    row = pl.BlockSpec((tr, n), lambda t: (t, 0))
    return pl.pallas_call(
        body, grid=(2,), in_specs=[pl.BlockSpec((3, tr, n), lambda t: (0, t, 0)), row, row, row],
        out_specs=[row] * 4, out_shape=[jax.ShapeDtypeStruct(w.shape, F32)] * 4,
        compiler_params=_cp(("parallel",)), name=name)(recv, w, m, v)


SMALL_NAMES = ("ffn1_norm", "mix_norm", "ffn2_norm", "conv_b", "conv_ln_g", "conv_ln_b", "q_norm", "k_norm")
SROWS = 16
LOSS_ROW = len(SMALL_NAMES)
CWF = 2


def _small_step(gs, loss_row, gcw, ws, ms, vs, wcw, mcw, vcw):
    ns = len(SMALL_NAMES)
    widths = [g.shape[1] for g in gs]

    def body(*refs):
        it = iter(refs)
        take = lambda n: [next(it) for _ in range(n)]
        g_refs, (loss_ref, gcw_ref) = take(ns), take(2)
        w_refs, m_refs, v_refs = take(ns), take(ns), take(ns)
        wcw_ref, mcw_ref, vcw_ref = take(3)
        outs = [take(4) for _ in range(ns)]
        cw_outs, (loss_out,) = take(4), take(1)
        send, slots, cslots, send_sems, recv_sems, csend_sems, crecv_sems = take(7)
        x, y, c = _place()
        me = 4 * x + 2 * y + c
        send[...] = jnp.zeros_like(send)
        for k in range(ns):
            send[k:k + 1, 0:widths[k]] = g_refs[k][...]
        send[LOSS_ROW:LOSS_ROW + 1, 0:128] = loss_ref[...]
        slots[me] = send[...]
        cslots[me] = gcw_ref[...]
        cps = []
        for k in range(1, NDEV):
            peer = (x ^ ((k >> 2) & 1), y ^ ((k >> 1) & 1), c ^ (k & 1))
            cps.append(pltpu.make_async_remote_copy(
                src_ref=send, dst_ref=slots.at[me], send_sem=send_sems.at[k - 1], recv_sem=recv_sems.at[k - 1],
                device_id=peer, device_id_type=MESH))
            cps.append(pltpu.make_async_remote_copy(
                src_ref=gcw_ref, dst_ref=cslots.at[me], send_sem=csend_sems.at[k - 1],
                recv_sem=crecv_sems.at[k - 1], device_id=peer, device_id_type=MESH))
        for cp in cps:
            cp.start()
        for cp in cps:
            cp.wait()
        tot = slots[0]
        ctot = cslots[0, me]
        for j in range(1, NDEV):
            tot = tot + slots[j]
            ctot = ctot + cslots[j, me]

        def step(g, w_ref, m_ref, v_ref, o):
            d, mn, vn = _adamw_math(w_ref[...], g, m_ref[...], v_ref[...])
            o[0][...], o[1][...], o[2][...], o[3][...] = g, d, mn, vn

        for k in range(ns):
            step(tot[k:k + 1, 0:widths[k]], w_refs[k], m_refs[k], v_refs[k], outs[k])
        step(ctot, wcw_ref, mcw_ref, vcw_ref, cw_outs)
        loss_out[...] = tot[LOSS_ROW:LOSS_ROW + 1, 0:128]

    args = [*gs, loss_row, gcw, *ws, *ms, *vs, wcw, mcw, vcw]
    out_shape = ([jax.ShapeDtypeStruct((1, n), F32) for n in widths for _ in range(4)]
                 + [jax.ShapeDtypeStruct((CWF, D), F32)] * 4 + [jax.ShapeDtypeStruct((1, 128), F32)])
    res = pl.pallas_call(
        body, in_specs=[VMEM] * len(args), out_specs=[VMEM] * len(out_shape), out_shape=out_shape,
        scratch_shapes=[pltpu.VMEM((SROWS, D), F32), pltpu.VMEM((NDEV, SROWS, D), F32),
                        pltpu.VMEM((NDEV, NDEV, CWF, D), F32)]
                       + [pltpu.SemaphoreType.DMA((NDEV - 1,))] * 4,
        name="small_step")(*args)
    per = [res[4 * k:4 * k + 4] for k in range(ns)]
    return per, res[4 * ns:4 * ns + 4], res[-1]


def _pack_cw(a):
    flat = a.reshape(a.shape[:-2] + (CK * HD,))
    pad = [(0, 0)] * (flat.ndim - 1) + [(0, CWF * D - CK * HD)]
    return jnp.pad(flat, pad).reshape(a.shape[:-2] + (CWF, D))


def _unpack_cw(v):
    return v.reshape(-1)[:CK * HD].reshape(1, CK, HD)


def kernel(x, ffn1_norm, ffn1_w_gate, ffn1_w_up, ffn1_w_down, mix_norm, w_in, q_norm, k_norm, conv_w, conv_b, conv_ln_g, conv_ln_b, w_out, ffn2_norm, ffn2_w_gate, ffn2_w_up, ffn2_w_down, loss_target, m_ffn1_norm, m_ffn1_w_gate, m_ffn1_w_up, m_ffn1_w_down, m_mix_norm, m_w_in, m_q_norm, m_k_norm, m_conv_w, m_conv_b, m_conv_ln_g, m_conv_ln_b, m_w_out, m_ffn2_norm, m_ffn2_w_gate, m_ffn2_w_up, m_ffn2_w_down, v_ffn1_norm, v_ffn1_w_gate, v_ffn1_w_up, v_ffn1_w_down, v_mix_norm, v_w_in, v_q_norm, v_k_norm, v_conv_w, v_conv_b, v_conv_ln_g, v_conv_ln_b, v_w_out, v_ffn2_norm, v_ffn2_w_gate, v_ffn2_w_up, v_ffn2_w_down):
    P = dict(ffn1_norm=ffn1_norm, ffn1_w_gate=ffn1_w_gate, ffn1_w_up=ffn1_w_up, ffn1_w_down=ffn1_w_down,
             mix_norm=mix_norm, w_in=w_in, q_norm=q_norm, k_norm=k_norm, conv_w=conv_w, conv_b=conv_b,
             conv_ln_g=conv_ln_g, conv_ln_b=conv_ln_b, w_out=w_out, ffn2_norm=ffn2_norm,
             ffn2_w_gate=ffn2_w_gate, ffn2_w_up=ffn2_w_up, ffn2_w_down=ffn2_w_down)
    M = dict(ffn1_norm=m_ffn1_norm, ffn1_w_gate=m_ffn1_w_gate, ffn1_w_up=m_ffn1_w_up, ffn1_w_down=m_ffn1_w_down,
             mix_norm=m_mix_norm, w_in=m_w_in, q_norm=m_q_norm, k_norm=m_k_norm, conv_w=m_conv_w, conv_b=m_conv_b,
             conv_ln_g=m_conv_ln_g, conv_ln_b=m_conv_ln_b, w_out=m_w_out, ffn2_norm=m_ffn2_norm,
             ffn2_w_gate=m_ffn2_w_gate, ffn2_w_up=m_ffn2_w_up, ffn2_w_down=m_ffn2_w_down)
    V = dict(ffn1_norm=v_ffn1_norm, ffn1_w_gate=v_ffn1_w_gate, ffn1_w_up=v_ffn1_w_up, ffn1_w_down=v_ffn1_w_down,
             mix_norm=v_mix_norm, w_in=v_w_in, q_norm=v_q_norm, k_norm=v_k_norm, conv_w=v_conv_w, conv_b=v_conv_b,
             conv_ln_g=v_conv_ln_g, conv_ln_b=v_conv_ln_b, w_out=v_w_out, ffn2_norm=v_ffn2_norm,
             ffn2_w_gate=v_ffn2_w_gate, ffn2_w_up=v_ffn2_w_up, ffn2_w_down=v_ffn2_w_down)
    order = ["ffn1_norm", "ffn1_w_gate", "ffn1_w_up", "ffn1_w_down", "mix_norm", "w_in", "q_norm", "k_norm",
             "conv_w", "conv_b", "conv_ln_g", "conv_ln_b", "w_out", "ffn2_norm", "ffn2_w_gate", "ffn2_w_up",
             "ffn2_w_down"]
    B, S, _ = x.shape
    T = B * S

    bigs = [("wg1", "ffn1_w_gate", True), ("wu1", "ffn1_w_up", True), ("wd1", "ffn1_w_down", False),
            ("win", "w_in", True), ("wout", "w_out", False),
            ("wg2", "ffn2_w_gate", True), ("wu2", "ffn2_w_up", True), ("wd2", "ffn2_w_down", False)]
    hm = lambda a, tr: jnp.transpose(a[0]) if tr else a[0]
    cw_pad = jnp.zeros((32, 128), F32).at[0:CK, 0:HD].set(conv_w[0])
    shard = {ln: (hm(P[pn], tr), BF16) for ln, pn, tr in bigs}
    gathered = _run_carry(_GatherCarry([shard["wg1"], shard["wu1"], (cw_pad, F32)]), "gather_first")
    W = {"wg1": gathered[0], "wu1": gathered[1]}
    cwg = gathered[2].reshape(NDEV, 32, 128)[:, 0:CK, 0:HD]
    W["conv_w"] = jnp.transpose(cwg, (1, 0, 2)).reshape(CK, DC)
    norms = {n: P[n] for n in SMALL_NAMES}
    comm = _Comm({"down_in": ({n: shard[n] for n in ("wd1", "win", "wout")}, 0.75),
                  "ffn2": ({n: shard[n] for n in ("wg2", "wu2", "wd2")}, 0.5)})

    loss_part, gx, _, small = _local_step(x.reshape(T, D), loss_target.reshape(T, D), norms, W, B, S, comm)

    G, Dl, Mn, Vn = {}, {}, {}, {}
    for ln, pn, tr in bigs:
        outs = _adamw_big(comm.reduced[ln], hm(P[pn], tr), hm(M[pn], tr), hm(V[pn], tr), "adamw_" + ln)
        G[pn], Dl[pn], Mn[pn], Vn[pn] = [(jnp.transpose(o) if tr else o)[None] for o in outs]

    dcw = small["conv_w"].reshape(CK, NDEV, HD).transpose(1, 0, 2)
    loss_row = jnp.zeros((1, 128), F32).at[0, 0].set(loss_part)
    per, cw_outs, loss_out = _small_step(
        [small[n] for n in SMALL_NAMES], loss_row, _pack_cw(dcw),
        [P[n] for n in SMALL_NAMES], [M[n] for n in SMALL_NAMES], [V[n] for n in SMALL_NAMES],
        _pack_cw(P["conv_w"][0]), _pack_cw(M["conv_w"][0]), _pack_cw(V["conv_w"][0]))
    loss = loss_out[0, 0]
    for n, outs in zip(SMALL_NAMES, per):
        G[n], Dl[n], Mn[n], Vn[n] = outs
    G["conv_w"], Dl["conv_w"], Mn["conv_w"], Vn["conv_w"] = [_unpack_cw(o) for o in cw_outs]

    return (loss, gx.reshape(B, S, D), *[G[n] for n in order], *[Dl[n] for n in order],
            *[Mn[n] for n in order], *[Vn[n] for n in order])
```

```python
import functools

import jax
import jax.numpy as jnp
from jax import lax
from jax.experimental import pallas as pl
from jax.experimental.pallas import tpu as pltpu

F32 = jnp.float32
BF16 = jnp.bfloat16

D = 1024
FF = 2816
HD = 64
DA = 512
DC = 512
DIN = 2560
CK = 31
BLK = 128
DILS = (1, 4, 16)
EPS = 1e-6
NDEV = 8
MESH = pl.DeviceIdType.MESH

LR, B1, B2, AEPS, WD, STEP = 0.001, 0.9, 0.999, 1e-08, 0.01, 10

NT = (((1,), (1,)), ((), ()))
TN = (((0,), (0,)), ((), ()))

VMEM_LIMIT = 56 * 1024 * 1024


def _cp(sem=None):
    return pltpu.CompilerParams(dimension_semantics=sem, vmem_limit_bytes=VMEM_LIMIT)


def _sigmoid(x):
    return 0.5 * (jnp.tanh(0.5 * x) + 1.0)


def _pallas(body, args, *, grid, in_specs, out_specs, out_shape, scratch_shapes, sem, name, carry=None):
    if carry is None:
        outs = pl.pallas_call(body, grid=grid, in_specs=in_specs, out_specs=out_specs, out_shape=out_shape,
                              scratch_shapes=scratch_shapes, compiler_params=_cp(sem), name=name)(*args)
        return outs, None
    n_in, n_out, n_scr = len(in_specs), len(out_shape), len(scratch_shapes)
    c_in, c_out = len(carry.in_arrays), len(carry.out_shape)

    def wrapped(*refs):
        ins, refs = refs[:n_in], refs[n_in:]
        cins, refs = refs[:c_in], refs[c_in:]
        outs, refs = refs[:n_out], refs[n_out:]
        couts, refs = refs[:c_out], refs[c_out:]
        scr, cscr = refs[:n_scr], refs[n_scr:]
        ids = [pl.program_id(a) for a in range(len(grid))]
        step = ids[0]
        for i, n in zip(ids[1:], grid[1:]):
            step = step * n + i
        steps = functools.reduce(lambda a, b: a * b, grid)

        @pl.when(step == 0)
        def _():
            carry.start(cins, couts, cscr)

        body(*ins, *outs, *scr)

        if hasattr(carry, "mid"):
            @pl.when(step == int(steps * carry.mid_at))
            def _():
                carry.mid(cins, couts, cscr)

        @pl.when(step == steps - 1)
        def _():
            carry.finish(cins, couts, cscr)

    outs = pl.pallas_call(
        wrapped, grid=grid, in_specs=list(in_specs) + carry.in_specs, out_specs=list(out_specs) + carry.out_specs,
        out_shape=list(out_shape) + carry.out_shape, scratch_shapes=list(scratch_shapes) + carry.scratch,
        compiler_params=_cp(("arbitrary",) * len(grid)), name=name)(*args, *carry.in_arrays)
    return outs[:n_out], outs[n_out:]


FC = 256


def _resident(shape):
    return pl.BlockSpec(shape, lambda *_: (0,) * len(shape), pipeline_mode=pl.Buffered(1))


def _mix_out_ffn_loss(h1, attn, conv, wout, gain, wg, wu, wd, target, name):
    T = h1.shape[0]
    tm = 512
    nt = T // tm

    def body(h1_ref, at_ref, cv_ref, wo_ref, gain_ref, wg_ref, wu_ref, wd_ref, t_ref,
             h2_ref, n_ref, g_ref, u_ref, dout_ref, dyb_ref, sq_ref, a_scr):
        xv = (h1_ref[...]
              + jnp.dot(at_ref[...], wo_ref[0:DA, :], preferred_element_type=F32)
              + jnp.dot(cv_ref[...], wo_ref[DA:D, :], preferred_element_type=F32))
        h2_ref[...] = xv
        r = lax.rsqrt(jnp.mean(xv * xv, axis=-1, keepdims=True) + EPS)
        n_ref[...] = (xv * r * gain_ref[...]).astype(BF16)
        for c in range(FF // FC):
            cols = slice(c * FC, (c + 1) * FC)
            nb = n_ref[...]
            g = lax.dot_general(nb, wg_ref[cols, :], NT, preferred_element_type=F32)
            u = lax.dot_general(nb, wu_ref[cols, :], NT, preferred_element_type=F32)
            g_ref[:, cols] = g.astype(BF16)
            u_ref[:, cols] = u.astype(BF16)
            a_scr[:, cols] = (g * _sigmoid(g) * u).astype(BF16)
        e = h2_ref[...] + 0.5 * jnp.dot(a_scr[...], wd_ref[...], preferred_element_type=F32) - t_ref[...]
        dout = e * (1.0 / D)
        dout_ref[...] = dout
        dyb_ref[...] = (0.5 * dout).astype(BF16)
        sq_ref[...] = jnp.sum(e * e, axis=0, keepdims=True)[None]

    row = pl.BlockSpec((tm, D), lambda t: (t, 0))
    half = pl.BlockSpec((tm, DA), lambda t: (t, 0))
    wide = pl.BlockSpec((tm, FF), lambda t: (t, 0))
    outs, _ = _pallas(
        body, (h1, attn, conv, wout, gain, wg, wu, wd, target), grid=(nt,),
        in_specs=[row, half, half, _resident((D, D)), _resident((1, D)), _resident((FF, D)), _resident((FF, D)),
                  _resident((FF, D)), row],
        out_specs=[row, row, wide, wide, row, row, pl.BlockSpec((1, 1, D), lambda t: (t, 0, 0))],
        out_shape=[jax.ShapeDtypeStruct((T, D), F32), jax.ShapeDtypeStruct((T, D), BF16)]
                  + [jax.ShapeDtypeStruct((T, FF), BF16)] * 2
                  + [jax.ShapeDtypeStruct((T, D), F32), jax.ShapeDtypeStruct((T, D), BF16),
                     jax.ShapeDtypeStruct((nt, 1, D), F32)],
        scratch_shapes=[pltpu.VMEM((tm, FF), BF16)], sem=("parallel",), name=name)
    return outs


def _ffn_gate_up(x, gain, wg, wu, name, carry=None):
    T = x.shape[0]
    tm = 512

    def body(x_ref, gain_ref, wg_ref, wu_ref, n_ref, g_ref, u_ref, a_ref):
        xv = x_ref[...]
        r = lax.rsqrt(jnp.mean(xv * xv, axis=-1, keepdims=True) + EPS)
        n_ref[...] = (xv * r * gain_ref[...]).astype(BF16)
        for c in range(FF // FC):
            cols = slice(c * FC, (c + 1) * FC)
            nb = n_ref[...]
            g = lax.dot_general(nb, wg_ref[cols, :], NT, preferred_element_type=F32)
            u = lax.dot_general(nb, wu_ref[cols, :], NT, preferred_element_type=F32)
            g_ref[:, cols] = g.astype(BF16)
            u_ref[:, cols] = u.astype(BF16)
            a_ref[:, cols] = (g * _sigmoid(g) * u).astype(BF16)

    row = pl.BlockSpec((tm, D), lambda t: (t, 0))
    wide = pl.BlockSpec((tm, FF), lambda t: (t, 0))
    return _pallas(
        body, (x, gain, wg, wu), grid=(T // tm,),
        in_specs=[row, _resident((1, D)), _resident((FF, D)), _resident((FF, D))],
        out_specs=[row, wide, wide, wide],
        out_shape=[jax.ShapeDtypeStruct((T, D), BF16)] + [jax.ShapeDtypeStruct((T, FF), BF16)] * 3,
        scratch_shapes=[], sem=("parallel",), name=name, carry=carry)


def _ffn_down(x, a, wd, name, carry=None):
    T = x.shape[0]
    tm = 512

    def body(x_ref, a_ref, wd_ref, h_ref):
        h_ref[...] = x_ref[...] + 0.5 * jnp.dot(a_ref[...], wd_ref[...], preferred_element_type=F32)

    row = pl.BlockSpec((tm, D), lambda t: (t, 0))
    wide = pl.BlockSpec((tm, FF), lambda t: (t, 0))
    return _pallas(
        body, (x, a, wd), grid=(T // tm,), in_specs=[row, wide, _resident((FF, D))],
        out_specs=[row], out_shape=[jax.ShapeDtypeStruct((T, D), F32)],
        scratch_shapes=[], sem=("parallel",), name=name, carry=carry)


def _ffn_bwd_act(dyb, g, u, x, dout, gain, wg, wu, wd, name, carry=None):
    T = x.shape[0]
    tm = 256
    nt = T // tm

    def body(dy_ref, g_ref, u_ref, x_ref, dout_ref, gain_ref, wg_ref, wu_ref, wd_ref,
             dg_ref, du_ref, dx_ref, dgn_ref):
        for c in range(FF // FC):
            cols = slice(c * FC, (c + 1) * FC)
            da = lax.dot_general(dy_ref[...], wd_ref[cols, :], NT, preferred_element_type=F32)
            gv = g_ref[:, cols].astype(F32)
            uv = u_ref[:, cols].astype(F32)
            sg = _sigmoid(gv)
            dg_ref[:, cols] = (da * uv * (sg * (1.0 + gv * (1.0 - sg)))).astype(BF16)
            du_ref[:, cols] = (da * (gv * sg)).astype(BF16)
        dn = (jnp.dot(dg_ref[...], wg_ref[...], preferred_element_type=F32)
              + jnp.dot(du_ref[...], wu_ref[...], preferred_element_type=F32))
        dx, dgain = _rms_bwd_rows(dn, x_ref[...], gain_ref[...])
        dx_ref[...] = dout_ref[...] + dx
        dgn_ref[...] = dgain[None]

    row = pl.BlockSpec((tm, D), lambda t: (t, 0))
    wide = pl.BlockSpec((tm, FF), lambda t: (t, 0))
    return _pallas(
        body, (dyb, g, u, x, dout, gain, wg, wu, wd), grid=(nt,),
        in_specs=[row, wide, wide, row, row, _resident((1, D)), _resident((FF, D)), _resident((FF, D)),
                  _resident((FF, D))],
        out_specs=[wide, wide, row, pl.BlockSpec((1, 1, D), lambda t: (t, 0, 0))],
        out_shape=[jax.ShapeDtypeStruct((T, FF), BF16)] * 2
                  + [jax.ShapeDtypeStruct((T, D), F32), jax.ShapeDtypeStruct((nt, 1, D), F32)],
        scratch_shapes=[], sem=("parallel",), name=name, carry=carry)


def _ffn_bwd_w(lhs, rhs, name, carry=None):
    T = rhs.shape[0]
    tf = 256
    lhs = lhs if isinstance(lhs, tuple) else (lhs,)

    def body(*refs):
        r_ref, dw_ref = refs[-2:]
        if len(lhs) == 2:
            gv = refs[0][...].astype(F32)
            lv = (gv * _sigmoid(gv) * refs[1][...].astype(F32)).astype(BF16)
        else:
            lv = refs[0][...]
        dw_ref[...] = lax.dot_general(lv, r_ref[...], TN, preferred_element_type=F32).astype(BF16)

    (dw,), got = _pallas(
        body, (*lhs, rhs), grid=(FF // tf,),
        in_specs=[pl.BlockSpec((T, tf), lambda f: (0, f))] * len(lhs) + [_resident((T, D))],
        out_specs=[pl.BlockSpec((tf, D), lambda f: (f, 0))], out_shape=[jax.ShapeDtypeStruct((FF, D), BF16)],
        scratch_shapes=[], sem=("parallel",), name=name, carry=carry)
    return dw, got


def _rms_bwd_rows(dn, xv, gain):
    r = lax.rsqrt(jnp.mean(xv * xv, axis=-1, keepdims=True) + EPS)
    xhat = xv * r
    dxhat = dn * gain
    dx = r * (dxhat - xhat * jnp.mean(dxhat * xhat, axis=-1, keepdims=True))
    return dx, jnp.sum(dn * xhat, axis=0, keepdims=True)


def _mix_in(h, gain, win, carry=None):
    T = h.shape[0]
    tm = 512

    def body(h_ref, gain_ref, w_ref, u_ref, n_ref):
        xv = h_ref[...]
        r = lax.rsqrt(jnp.mean(xv * xv, axis=-1, keepdims=True) + EPS)
        nb = (xv * r * gain_ref[...]).astype(BF16)
        n_ref[...] = nb
        u_ref[...] = lax.dot_general(nb, w_ref[...], NT, preferred_element_type=F32).astype(BF16)

    row = pl.BlockSpec((tm, D), lambda t: (t, 0))
    return _pallas(
        body, (h, gain, win), grid=(T // tm,),
        in_specs=[row, _resident((1, D)), _resident((DIN, D))],
        out_specs=[pl.BlockSpec((tm, DIN), lambda t: (t, 0)), row],
        out_shape=[jax.ShapeDtypeStruct((T, DIN), BF16), jax.ShapeDtypeStruct((T, D), BF16)],
        scratch_shapes=[], sem=("parallel",), name="mix_in", carry=carry)


def _mix_out_bwd(dh, attn, conv, wout):
    T = dh.shape[0]
    tm = 512
    nt = T // tm

    def body(dh_ref, a_ref, c_ref, w_ref, da_ref, dc_ref, dw_ref, acc_scr):
        t = pl.program_id(0)

        @pl.when(t == 0)
        def _():
            acc_scr[...] = jnp.zeros_like(acc_scr)

        dhb = dh_ref[...].astype(BF16)
        dmix = lax.dot_general(dhb, w_ref[...], NT, preferred_element_type=F32)
        da_ref[...] = dmix[:, 0:DA].astype(BF16)
        dc_ref[...] = dmix[:, DA:D].astype(BF16)
        acc_scr[0:DA, :] += lax.dot_general(a_ref[...], dhb, TN, preferred_element_type=F32)
        acc_scr[DA:D, :] += lax.dot_general(c_ref[...], dhb, TN, preferred_element_type=F32)

        @pl.when(t == nt - 1)
        def _():
            dw_ref[...] = acc_scr[...].astype(BF16)

    row = pl.BlockSpec((tm, D), lambda t: (t, 0))
    half = pl.BlockSpec((tm, DA), lambda t: (t, 0))
    full = pl.BlockSpec((D, D), lambda t: (0, 0))
    return pl.pallas_call(
        body, grid=(nt,), in_specs=[row, half, half, full], out_specs=[half, half, full],
        out_shape=[jax.ShapeDtypeStruct((T, DA), BF16)] * 2 + [jax.ShapeDtypeStruct((D, D), BF16)],
        scratch_shapes=[pltpu.VMEM((D, D), F32)],
        compiler_params=_cp(("arbitrary",)), name="mix_out_bwd")(dh, attn, conv, wout)


def _mix_in_bwd(dparts, win, nb, h, dh, gain):
    T = h.shape[0]
    tm = 512
    nt = T // tm

    def body(d0, d1, d2, d3, d4, w_ref, n_ref, h_ref, dh_ref, gain_ref,
             dw_ref, dx_ref, dyb_ref, dg_ref, acc_scr):
        t = pl.program_id(0)

        @pl.when(t == 0)
        def _():
            acc_scr[...] = jnp.zeros_like(acc_scr)

        n = n_ref[...]
        dn = jnp.zeros((tm, D), F32)
        for i, d_ref in enumerate((d0, d1, d2, d3, d4)):
            dv = d_ref[...]
            dn = dn + jnp.dot(dv, w_ref[i * DA:(i + 1) * DA, :], preferred_element_type=F32)
            acc_scr[i * DA:(i + 1) * DA, :] += lax.dot_general(dv, n, TN, preferred_element_type=F32)
        dx, dgain = _rms_bwd_rows(dn, h_ref[...], gain_ref[...])
        tot = dh_ref[...] + dx
        dx_ref[...] = tot
        dyb_ref[...] = (0.5 * tot).astype(BF16)
        dg_ref[...] = dgain[None]

        @pl.when(t == nt - 1)
        def _():
            dw_ref[...] = acc_scr[...].astype(BF16)

    row = pl.BlockSpec((tm, D), lambda t: (t, 0))
    half = pl.BlockSpec((tm, DA), lambda t: (t, 0))
    full = pl.BlockSpec((DIN, D), lambda t: (0, 0))
    return pl.pallas_call(
        body, grid=(nt,),
        in_specs=[half] * 5 + [full, row, row, row, pl.BlockSpec((1, D), lambda t: (0, 0))],
        out_specs=[full, row, row, pl.BlockSpec((1, 1, D), lambda t: (t, 0, 0))],
        out_shape=[jax.ShapeDtypeStruct((DIN, D), BF16), jax.ShapeDtypeStruct((T, D), F32),
                   jax.ShapeDtypeStruct((T, D), BF16), jax.ShapeDtypeStruct((nt, 1, D), F32)],
        scratch_shapes=[pltpu.VMEM((DIN, D), F32)],
        compiler_params=_cp(("arbitrary",)), name="mix_in_bwd")(*dparts, win, nb, h, dh, gain)


def _head_masks():
    lane = lax.broadcasted_iota(jnp.int32, (1, 2 * HD), 1)
    m0 = lane < HD
    return m0, jnp.logical_not(m0)


def _stack_heads(v, m0, m1):
    z = jnp.zeros_like(v)
    return jnp.concatenate([jnp.where(m0, v, z), jnp.where(m1, v, z)], axis=0)


def _unstack_heads(v2, m0):
    return jnp.where(m0, v2[0:BLK], v2[BLK:2 * BLK])


def _head_sums(xv):
    ri = lax.broadcasted_iota(jnp.int32, (2 * HD, 2 * HD), 0)
    ci = lax.broadcasted_iota(jnp.int32, (2 * HD, 2 * HD), 1)
    ones = jnp.where((ri < HD) == (ci < HD), 1.0, 0.0).astype(BF16)
    hi = xv.astype(BF16)
    lo = (xv - hi.astype(F32)).astype(BF16)
    return (jnp.dot(hi, ones, preferred_element_type=F32) + jnp.dot(lo, ones, preferred_element_type=F32))


def _head_rms(xv):
    return lax.rsqrt(_head_sums(xv * xv) * (1.0 / HD) + EPS)


def _band_mask(first):
    qi = lax.broadcasted_iota(jnp.int32, (BLK, 2 * BLK), 0)
    ci = lax.broadcasted_iota(jnp.int32, (BLK, 2 * BLK), 1)
    band = (ci >= qi) & (ci <= qi + BLK)
    return band & ((ci >= BLK) | jnp.logical_not(first))


def _block_rows(j, d, seg):
    r, n = j // seg, j % seg
    start = r + (d * BLK) * n
    first = n == 0
    prev = jnp.where(first, start, start - d * BLK)
    return pl.ds(start, BLK, stride=d), pl.ds(prev, BLK, stride=d), first


def _block_keys(refs, cur, prev, first, single):
    if single:
        qi = lax.broadcasted_iota(jnp.int32, (BLK, BLK), 0)
        ci = lax.broadcasted_iota(jnp.int32, (BLK, BLK), 1)
        return [r[cur, :].astype(BF16) for r in refs], ci <= qi
    return ([jnp.concatenate([r[prev, :], r[cur, :]], axis=0).astype(BF16) for r in refs], _band_mask(first))


def _attn_fwd(u, qg2, kg2, B, S, carry=None):
    T = B * S
    NB = S // BLK
    scale = HD ** -0.5

    def body(q_ref, k_ref, v_ref, qg_ref, kg_ref, o_ref, lse_ref, qn, kn, vn, os_, ls_):
        m0, m1 = _head_masks()
        qv = q_ref[...].astype(F32)
        qn[...] = qv * _head_rms(qv) * (qg_ref[...] * scale)
        kv = k_ref[...].astype(F32)
        kn[...] = kv * _head_rms(kv) * kg_ref[...]
        vn[...] = v_ref[...].astype(F32)

        for i, d in enumerate(DILS):
            seg = NB // d

            def blk(j, c, i=i, d=d, seg=seg):
                cur, prev, first = _block_rows(j, d, seg)
                q2 = _stack_heads(qn[cur, :].astype(BF16), m0, m1)
                (kk, vv), mask = _block_keys((kn, vn), cur, prev, first, False)
                s = lax.dot_general(q2, kk, NT, preferred_element_type=F32)
                s = jnp.where(jnp.concatenate([mask, mask], axis=0), s, -1e30)
                mx = jnp.max(s, axis=-1, keepdims=True)
                p = jnp.exp(s - mx)
                l = jnp.sum(p, axis=-1, keepdims=True)
                o2 = jnp.dot((p * (1.0 / l)).astype(BF16), vv, preferred_element_type=F32)
                os_[i, cur, :] = _unstack_heads(o2, m0)
                ls_[i, cur, :] = _unstack_heads(mx + jnp.log(l), m0)
                return c

            lax.fori_loop(0, NB, blk, 0, unroll=8)

        def comb(c, carry):
            rows = pl.ds(pl.multiple_of(c * 256, 256), 256)
            l0, l1, l2 = ls_[0, rows, :], ls_[1, rows, :], ls_[2, rows, :]
            mx = jnp.maximum(jnp.maximum(l0, l1), l2)
            e0, e1, e2 = jnp.exp(l0 - mx), jnp.exp(l1 - mx), jnp.exp(l2 - mx)
            tot = e0 + e1 + e2
            inv = 1.0 / tot
            o = (e0 * os_[0, rows, :] + e1 * os_[1, rows, :] + e2 * os_[2, rows, :]) * inv
            o_ref[rows, :] = o.astype(BF16)
            lse_ref[rows, :] = mx + jnp.log(tot)
            return carry

        lax.fori_loop(0, S // 256, comb, 0)

    pair = 2 * HD
    blk_spec = lambda off: pl.BlockSpec((S, pair), lambda b, p, off=off: (b, off + p))
    gspec = pl.BlockSpec((1, pair), lambda b, p: (0, 0))
    return _pallas(
        body, (u, u, u, qg2, kg2), grid=(B, DA // pair),
        in_specs=[blk_spec(0), blk_spec(DA // pair), blk_spec(2 * DA // pair), gspec, gspec],
        out_specs=[blk_spec(0), blk_spec(0)],
        out_shape=[jax.ShapeDtypeStruct((T, DA), BF16), jax.ShapeDtypeStruct((T, DA), F32)],
        scratch_shapes=[pltpu.VMEM((S, pair), F32)] * 3 + [pltpu.VMEM((3, S, pair), F32)] * 2,
        sem=("parallel", "parallel"), name="attn_fwd", carry=carry)


def _attn_bwd(u, attn, dattn, lse, qg2, kg2, B, S, carry=None):
    T = B * S
    NB = S // BLK
    scale = HD ** -0.5
    pair = 2 * HD

    def body(q_ref, k_ref, v_ref, o_ref, do_ref, lse_ref, qg_ref, kg_ref,
             dq_ref, dk_ref, dv_ref, dgn_ref,
             qn, kn, vn, don, ldl, accq, acck, accv, rq, rk):
        m0, m1 = _head_masks()
        lane = lax.broadcasted_iota(jnp.int32, (1, pair), 1)
        qv = q_ref[...].astype(F32)
        rq[...] = _head_rms(qv)
        qn[...] = qv * rq[...] * (qg_ref[...] * scale)
        kv = k_ref[...].astype(F32)
        rk[...] = _head_rms(kv)
        kn[...] = kv * rk[...] * kg_ref[...]
        vn[...] = v_ref[...].astype(F32)
        dov = do_ref[...].astype(F32)
        don[...] = dov
        ldl[...] = jnp.where((lane % HD) < HD // 2, lse_ref[...], _head_sums(dov * o_ref[...].astype(F32)))
        accq[...] = jnp.zeros_like(accq)
        acck[...] = jnp.zeros_like(acck)
        accv[...] = jnp.zeros_like(accv)

        for i, d in enumerate(DILS):
            seg = NB // d

            def blk(j, c, d=d, seg=seg):
                cur, prev, first = _block_rows(j, d, seg)
                q2 = _stack_heads(qn[cur, :].astype(BF16), m0, m1)
                do2 = _stack_heads(don[cur, :].astype(BF16), m0, m1)
                (kk, vv), mask = _block_keys((kn, vn), cur, prev, first, seg == 1)
                ldv = ldl[cur, :]
                lse2 = jnp.concatenate([ldv[:, 0:1], ldv[:, HD:HD + 1]], axis=0)
                dl2 = jnp.concatenate([ldv[:, HD // 2:HD // 2 + 1], ldv[:, HD + HD // 2:HD + HD // 2 + 1]], axis=0)
                s = lax.dot_general(q2, kk, NT, preferred_element_type=F32)
                p = jnp.where(jnp.concatenate([mask, mask], axis=0), jnp.exp(s - lse2), 0.0)
                dp = lax.dot_general(do2, vv, NT, preferred_element_type=F32)
                ds = (p * (dp - dl2)).astype(BF16)
                dq_acc = _unstack_heads(jnp.dot(ds, kk, preferred_element_type=F32), m0)
                dk_acc = lax.dot_general(ds, q2, TN, preferred_element_type=F32)
                dv_acc = lax.dot_general(p.astype(BF16), do2, TN, preferred_element_type=F32)
                accq[cur, :] += dq_acc
                if seg == 1:
                    acck[cur, :] += dk_acc
                    accv[cur, :] += dv_acc
                else:
                    acck[prev, :] += dk_acc[0:BLK]
                    acck[cur, :] += dk_acc[BLK:2 * BLK]
                    accv[prev, :] += dv_acc[0:BLK]
                    accv[cur, :] += dv_acc[BLK:2 * BLK]
                return c

            lax.fori_loop(0, NB, blk, 0, unroll=8)

        def norm_bwd(x_ref, r_ref, dn, gain):
            r = r_ref[...]
            xhat = x_ref[...].astype(F32) * r
            dxhat = dn * gain
            dx = r * (dxhat - xhat * (_head_sums(dxhat * xhat) * (1.0 / HD)))
            return dx, jnp.sum(dn * xhat, axis=0, keepdims=True)

        dq, dgq = norm_bwd(q_ref, rq, accq[...], qg_ref[...] * scale)
        dk, dgk = norm_bwd(k_ref, rk, acck[...], kg_ref[...])
        dq_ref[...] = dq.astype(BF16)
        dk_ref[...] = dk.astype(BF16)
        dv_ref[...] = accv[...].astype(BF16)
        dgn_ref[...] = jnp.concatenate([dgq * scale, dgk, jnp.zeros((6, pair), F32)], axis=0)[None]

    blk_spec = lambda off: pl.BlockSpec((S, pair), lambda b, p, off=off: (b, off + p))
    gspec = pl.BlockSpec((1, pair), lambda b, p: (0, 0))
    np_ = DA // pair
    return _pallas(
        body, (u, u, u, attn, dattn, lse, qg2, kg2), grid=(B, np_),
        in_specs=[blk_spec(0), blk_spec(np_), blk_spec(2 * np_), blk_spec(0), blk_spec(0), blk_spec(0),
                  gspec, gspec],
        out_specs=[blk_spec(0), blk_spec(0), blk_spec(0),
                   pl.BlockSpec((1, 8, pair), lambda b, p: (b * np_ + p, 0, 0))],
        out_shape=[jax.ShapeDtypeStruct((T, DA), BF16)] * 3 + [jax.ShapeDtypeStruct((B * np_, 8, pair), F32)],
        scratch_shapes=[pltpu.VMEM((S, pair), F32)] * 10,
        sem=("parallel", "parallel"), name="attn_bwd", carry=carry)


CT = 32
CPAD = 32


def _shifted(win, offsets):
    rolled, out = {}, {}
    n = win.shape[0]
    for o in offsets:
        sub = o % 8
        if sub not in rolled:
            rolled[sub] = win if sub == 0 else pltpu.roll(win, n - sub, 0)
        out[o] = rolled[sub][o - sub:o - sub + CT, :]
    return out


def _ln_fwd(y, g, b):
    mu = jnp.mean(y, axis=-1, keepdims=True)
    yc = y - mu
    rstd = lax.rsqrt(jnp.mean(yc * yc, axis=-1, keepdims=True) + EPS)
    xhat = yc * rstd
    return xhat, rstd, xhat * g + b


def _fill_glu(ca_ref, cg_ref, glu, S):
    glu[pl.ds(0, CPAD), :] = jnp.zeros((CPAD, DC), F32)

    def fill(i, c):
        rows = pl.ds(pl.multiple_of(i * 256, 256), 256)
        a = ca_ref[rows, :].astype(F32)
        gt = cg_ref[rows, :].astype(F32)
        glu[pl.ds(pl.multiple_of(CPAD + i * 256, CT), 256), :] = a * _sigmoid(gt)
        return c

    lax.fori_loop(0, S // 256, fill, 0)


def _conv_fwd(u, cw, cb, lg, lb, B, S):
    T = B * S

    def body(ca_ref, cg_ref, w_ref, b_ref, lg_ref, lb_ref, o_ref, y_ref, glu):
        _fill_glu(ca_ref, cg_ref, glu, S)

        def step(i, c):
            t0 = pl.multiple_of(i * CT, CT)
            win = glu[pl.ds(t0, 2 * CT), :]
            acc = jnp.zeros((CT, DC), F32) + b_ref[...]
            taps = _shifted(win, [k + 2 for k in range(CK)])
            for k in range(CK):
                acc = acc + taps[k + 2] * w_ref[k:k + 1, :]
            y_ref[pl.ds(t0, CT), :] = acc
            _, _, z = _ln_fwd(acc, lg_ref[...], lb_ref[...])
            o_ref[pl.ds(t0, CT), :] = (z * _sigmoid(z)).astype(BF16)
            return c

        lax.fori_loop(0, S // CT, step, 0, unroll=2)

    vec = pl.BlockSpec((1, DC), lambda b: (0, 0))
    return pl.pallas_call(
        body, grid=(B,),
        in_specs=[pl.BlockSpec((S, DC), lambda b: (b, 3)), pl.BlockSpec((S, DC), lambda b: (b, 4)),
                  pl.BlockSpec((CT, DC), lambda b: (0, 0)), vec, vec, vec],
        out_specs=[pl.BlockSpec((S, DC), lambda b: (b, 0))] * 2,
        out_shape=[jax.ShapeDtypeStruct((T, DC), BF16), jax.ShapeDtypeStruct((T, DC), F32)],
        scratch_shapes=[pltpu.VMEM((CPAD + S, DC), F32)],
        compiler_params=_cp(("parallel",)), name="conv_fwd")(u, u, cw, cb, lg, lb)


def _conv_bwd(u, y, dconv, cw, lg, lb, B, S):
    T = B * S

    def body(ca_ref, cg_ref, y_ref, dc_ref, w_ref, lg_ref, lb_ref,
             dca_ref, dcg_ref, dw_ref, ds_ref, glu, dyp, dwacc):
        _fill_glu(ca_ref, cg_ref, glu, S)
        dyp[pl.ds(S, CPAD), :] = jnp.zeros((CPAD, DC), F32)
        lgv, lbv = lg_ref[...], lb_ref[...]

        def sum8(v):
            return functools.reduce(jnp.add, [v[r:r + 8] for r in range(0, v.shape[0], 8)])

        P1 = 4 * CT

        def p1(i, carry):
            sb, sg, sl = carry
            t0 = pl.multiple_of(i * P1, P1)
            xhat, rstd, z = _ln_fwd(y_ref[pl.ds(t0, P1), :], lgv, lbv)
            sz = _sigmoid(z)
            dz = dc_ref[pl.ds(t0, P1), :].astype(F32) * (sz * (1.0 + z * (1.0 - sz)))
            dxhat = dz * lgv
            dy = rstd * (dxhat - jnp.mean(dxhat, axis=-1, keepdims=True)
                         - xhat * jnp.mean(dxhat * xhat, axis=-1, keepdims=True))
            dyp[pl.ds(t0, P1), :] = dy
            return sb + sum8(dy), sg + sum8(dz * xhat), sl + sum8(dz)

        z8 = jnp.zeros((8, DC), F32)
        sb, sg, sl = lax.fori_loop(0, S // P1, p1, (z8, z8, z8))
        rs = lambda v: jnp.sum(v, axis=0, keepdims=True)
        ds_ref[...] = jnp.concatenate([rs(sb), rs(sg), rs(sl), jnp.zeros((5, DC), F32)], axis=0)[None]

        def p2(i, c):
            t0 = pl.multiple_of(i * CT, CT)
            win = dyp[pl.ds(t0, 2 * CT), :]
            acc = jnp.zeros((CT, DC), F32)
            taps = _shifted(win, [30 - k for k in range(CK)])
            for k in range(CK):
                acc = acc + taps[30 - k] * w_ref[k:k + 1, :]
            a = ca_ref[pl.ds(t0, CT), :].astype(F32)
            sgt = _sigmoid(cg_ref[pl.ds(t0, CT), :].astype(F32))
            dca_ref[pl.ds(t0, CT), :] = (acc * sgt).astype(BF16)
            dcg_ref[pl.ds(t0, CT), :] = (acc * a * sgt * (1.0 - sgt)).astype(BF16)
            return c

        lax.fori_loop(0, S // CT, p2, 0)

        dwacc[...] = jnp.zeros_like(dwacc)

        def p3(i, c):
            t0 = pl.multiple_of(i * CT, CT)
            win = glu[pl.ds(t0, 2 * CT), :]
            dy = dyp[pl.ds(t0, CT), :]
            for k in range(CK):
                dwacc[k] += sum8(dy * win[k + 2:k + 2 + CT, :])
            return c

        lax.fori_loop(0, S // CT, p3, 0)
        dw_ref[...] = jnp.sum(dwacc[...], axis=1)[None]

    vec = pl.BlockSpec((1, DC), lambda b: (0, 0))
    seq = pl.BlockSpec((S, DC), lambda b: (b, 0))
    return pl.pallas_call(
        body, grid=(B,),
        in_specs=[pl.BlockSpec((S, DC), lambda b: (b, 3)), pl.BlockSpec((S, DC), lambda b: (b, 4)),
                  seq, seq, pl.BlockSpec((CT, DC), lambda b: (0, 0)), vec, vec],
        out_specs=[seq, seq, pl.BlockSpec((1, CT, DC), lambda b: (b, 0, 0)),
                   pl.BlockSpec((1, 8, DC), lambda b: (b, 0, 0))],
        out_shape=[jax.ShapeDtypeStruct((T, DC), BF16)] * 2
                  + [jax.ShapeDtypeStruct((B, CT, DC), F32), jax.ShapeDtypeStruct((B, 8, DC), F32)],
        scratch_shapes=[pltpu.VMEM((CPAD + S, DC), F32), pltpu.VMEM((S + CPAD, DC), F32),
                        pltpu.VMEM((CT, 8, DC), F32)],
        compiler_params=_cp(("parallel",)), name="conv_bwd")(u, u, y, dconv, cw, lg, lb)


def _local_step(x, target, norms, W, B, S, comm=None):
    qg2 = jnp.concatenate([norms["q_norm"], norms["q_norm"]], axis=1)
    kg2 = jnp.concatenate([norms["k_norm"], norms["k_norm"]], axis=1)
    cw = jnp.concatenate([W["conv_w"], jnp.zeros((1, DC), F32)], axis=0)

    W = dict(W)
    (n1, g1, u1, act1), got = _ffn_gate_up(x, norms["ffn1_norm"], W["wg1"], W["wu1"], "ffn1_gate_up",
                                           carry=comm.gathers["down_in"] if comm else None)
    if comm:
        W.update(comm.gathered("down_in", got))
    (h1,), _ = _ffn_down(x, act1, W["wd1"], "ffn1_down")
    (u, n2), _ = _mix_in(h1, norms["mix_norm"], W["win"])
    (attn, lse), got = _attn_fwd(u, qg2, kg2, B, S, carry=comm.gathers["ffn2"] if comm else None)
    if comm:
        W = dict(W, **comm.gathered("ffn2", got))
    conv, y = _conv_fwd(u, cw, norms["conv_b"], norms["conv_ln_g"], norms["conv_ln_b"], B, S)
    h2, n3, g2, u2, dout, dyb, sq = _mix_out_ffn_loss(h1, attn, conv, W["wout"], norms["ffn2_norm"],
                                                      W["wg2"], W["wu2"], W["wd2"], target, "ffn2_fwd")
    loss = (0.5 / D) * jnp.sum(sq)

    (dg2, du2, dh2, dgn_ffn2), _ = _ffn_bwd_act(dyb, g2, u2, h2, dout, norms["ffn2_norm"],
                                               W["wg2"], W["wu2"], W["wd2"], "ffn2_bwd_act")
    dwd2, _ = _ffn_bwd_w((g2, u2), dyb, "ffn2_bwd_wd")
    dwg2, _ = _ffn_bwd_w(dg2, n3, "ffn2_bwd_wg")
    dwu2, _ = _ffn_bwd_w(du2, n3, "ffn2_bwd_wu")
    dattn, dconv, dwout = _mix_out_bwd(dh2, attn, conv, W["wout"])
    carry = comm.reduce_start("ffn2", {"wg2": dwg2, "wu2": dwu2, "wd2": dwd2, "wout": dwout}) if comm else None
    (dq, dk, dv, dgn_qk), got = _attn_bwd(u, attn, dattn, lse, qg2, kg2, B, S, carry=carry)
    if comm:
        comm.reduce_done(carry, got)
    dca, dcg, dcw, dcs = _conv_bwd(u, y, dconv, cw, norms["conv_ln_g"], norms["conv_ln_b"], B, S)
    dwin, dh1, dyb1, dgn_mix = _mix_in_bwd((dq, dk, dv, dca, dcg), W["win"], n2, h1, dh2, norms["mix_norm"])
    (dg1, du1, gx, dgn_ffn1), _ = _ffn_bwd_act(dyb1, g1, u1, x, dh1, norms["ffn1_norm"],
                                              W["wg1"], W["wu1"], W["wd1"], "ffn1_bwd_act")
    carry = comm.reduce_start("win", {"win": dwin}) if comm else None
    dwd1, got = _ffn_bwd_w(act1, dyb1, "ffn1_bwd_wd", carry=carry)
    if comm:
        comm.reduce_done(carry, got)
        carry = comm.reduce_start("wd1", {"wd1": dwd1})
    dwg1, got = _ffn_bwd_w(dg1, n1, "ffn1_bwd_wg", carry=carry)
    if comm:
        comm.reduce_done(carry, got)
        carry = comm.reduce_start("wg1", {"wg1": dwg1})
    dwu1, got = _ffn_bwd_w(du1, n1, "ffn1_bwd_wu", carry=carry)
    if comm:
        comm.reduce_done(carry, got)
        comm.reduce_now("wu1", {"wu1": dwu1})

    qk = jnp.sum(dgn_qk, axis=0)
    cs = jnp.sum(dcs, axis=0)
    small = {
        "ffn1_norm": jnp.sum(dgn_ffn1, axis=0),
        "mix_norm": jnp.sum(dgn_mix, axis=0),
        "q_norm": qk[0:1, 0:HD] + qk[0:1, HD:2 * HD],
        "k_norm": qk[1:2, 0:HD] + qk[1:2, HD:2 * HD],
        "conv_w": jnp.sum(dcw, axis=0)[0:CK],
        "conv_b": cs[0:1],
        "conv_ln_g": cs[1:2],
        "conv_ln_b": cs[2:3],
        "ffn2_norm": jnp.sum(dgn_ffn2, axis=0),
    }
    big = {"wg1": dwg1, "wu1": dwu1, "wd1": dwd1, "win": dwin, "wout": dwout,
           "wg2": dwg2, "wu2": dwu2, "wd2": dwd2}
    return loss, gx, big, small


HBM = pl.BlockSpec(memory_space=pltpu.HBM)
VMEM = pl.BlockSpec(memory_space=pltpu.VMEM)


def _place():
    return lax.axis_index("x"), lax.axis_index("y"), lax.axis_index("c")


class _GatherCarry:
    def __init__(self, shards, mid_at=0.5):
        nt = len(shards)
        self.mid_at = mid_at
        self.shards = shards
        self.in_arrays = [s for s, _ in shards]
        self.in_specs = [VMEM] * nt
        self.out_shape = [jax.ShapeDtypeStruct((NDEV * s.shape[0], s.shape[1]), dt) for s, dt in shards]
        self.out_specs = [HBM] * nt
        self.scratch = ([pltpu.VMEM(s.shape, dt) for s, dt in shards]
                        + [pltpu.SemaphoreType.DMA((nt, 7)), pltpu.SemaphoreType.DMA((nt, 7)),
                           pltpu.SemaphoreType.DMA((nt,))])

    def _copies(self, outs, scr):
        nt = len(self.shards)
        stages = scr[:nt]
        send_sems, recv_sems, local_sems = scr[nt:]
        x, y, c = _place()
        me, sibling = (x, y, c), (x, y, 1 - c)
        xn, yn, diag = (1 - x, y, c), (x, 1 - y, c), (1 - x, 1 - y, c)
        via = (x ^ c, y ^ (1 - c), c)
        onto = (x ^ (1 - c), y ^ c, c)

        def rows(t, px, py, pc):
            r = self.shards[t][0].shape[0]
            return outs[t].at[pl.ds((4 * px + 2 * py + pc) * r, r), :]

        def copy(t, k, block, to, src=None):
            return pltpu.make_async_remote_copy(
                src_ref=rows(t, *block) if src is None else src, dst_ref=rows(t, *block),
                send_sem=send_sems.at[t, k], recv_sem=recv_sems.at[t, k],
                device_id=to, device_id_type=MESH)

        sib = lambda b: (b[0], b[1], 1 - c)
        return dict(
            local=[pltpu.make_async_copy(stages[t], rows(t, *me), local_sems.at[t]) for t in range(nt)],
            own=[[copy(t, 0, me, sibling, src=stages[t]), copy(t, 1, me, xn, src=stages[t]),
                  copy(t, 2, me, yn, src=stages[t])] for t in range(nt)],
            relay=[copy(t, 3, via, onto) for t in range(nt)],
            down=[[copy(t, 4, xn, sibling), copy(t, 5, yn, sibling)] for t in range(nt)],
            down_diag=[copy(t, 6, diag, sibling) for t in range(nt)],
            got_xy=[[copy(t, 1, xn, me), copy(t, 2, yn, me)] for t in range(nt)],
            got_diag=[copy(t, 3, diag, me) for t in range(nt)],
            got_sib=[[copy(t, 0, sibling, me), copy(t, 4, sib(xn), me), copy(t, 5, sib(yn), me),
                      copy(t, 6, sib(diag), me)] for t in range(nt)])

    def start(self, ins, outs, scr):
        cps = self._copies(outs, scr)
        for t, (_, dt) in enumerate(self.shards):
            scr[t][...] = ins[t][...].astype(dt)
            for cp in [cps["local"][t]] + cps["own"][t]:
                cp.start()

    def mid(self, ins, outs, scr):
        cps = self._copies(outs, scr)
        for t in range(len(self.shards)):
            for cp in cps["got_xy"][t]:
                cp.wait_recv()
            for cp in [cps["relay"][t]] + cps["down"][t]:
                cp.start()

    def finish(self, ins, outs, scr):
        cps = self._copies(outs, scr)
        for t in range(len(self.shards)):
            cps["got_diag"][t].wait_recv()
            cps["down_diag"][t].start()
        for t in range(len(self.shards)):
            for cp in cps["got_sib"][t]:
                cp.wait_recv()
            for cp in cps["own"][t] + [cps["relay"][t]] + cps["down"][t] + [cps["down_diag"][t]]:
                cp.wait_send()
            cps["local"][t].wait()


def _run_carry(carry, name):
    def body(*refs):
        n_in, n_out = len(carry.in_arrays), len(carry.out_shape)
        ins, outs, scr = refs[:n_in], refs[n_in:n_in + n_out], refs[n_in + n_out:]
        carry.start(ins, outs, scr)
        if hasattr(carry, "mid"):
            carry.mid(ins, outs, scr)
        carry.finish(ins, outs, scr)

    return pl.pallas_call(
        body, in_specs=carry.in_specs, out_specs=carry.out_specs, out_shape=carry.out_shape,
        scratch_shapes=carry.scratch, compiler_params=pltpu.CompilerParams(vmem_limit_bytes=VMEM_LIMIT),
        name=name)(*carry.in_arrays)


def _sibling_reduce(grads, name):
    nt = len(grads)
    g4 = [g.reshape(4, 2, g.shape[0] // NDEV, g.shape[1]) for g in grads]

    def body(*refs):
        ins, outs = refs[:nt], refs[nt:2 * nt]
        recv, own = refs[2 * nt:3 * nt], refs[3 * nt:4 * nt]
        send_sems, recv_sems, load_sems, store_sems = refs[4 * nt:]
        x, y, c = _place()
        sends = [pltpu.make_async_remote_copy(
            src_ref=ins[t].at[:, 1 - c], dst_ref=recv[t], send_sem=send_sems.at[t], recv_sem=recv_sems.at[t],
            device_id=(x, y, 1 - c), device_id_type=MESH) for t in range(nt)]
        loads = [pltpu.make_async_copy(ins[t].at[:, c], own[t], load_sems.at[t]) for t in range(nt)]
        stores = [pltpu.make_async_copy(own[t], outs[t], store_sems.at[t]) for t in range(nt)]
        for cp in sends + loads:
            cp.start()
        for t in range(nt):
            loads[t].wait()
            sends[t].wait_recv()
            for q in range(4):
                own[t][q] = (own[t][q].astype(F32) + recv[t][q].astype(F32)).astype(BF16)
            stores[t].start()
        for t in range(nt):
            sends[t].wait_send()
            stores[t].wait()

    blocks = [(4,) + g.shape[2:] for g in g4]
    return pl.pallas_call(
        body, in_specs=[HBM] * nt, out_specs=[HBM] * nt,
        out_shape=[jax.ShapeDtypeStruct(b, BF16) for b in blocks],
        scratch_shapes=[pltpu.VMEM(b, BF16) for b in blocks] * 2 + [pltpu.SemaphoreType.DMA((nt,))] * 4,
        compiler_params=pltpu.CompilerParams(vmem_limit_bytes=VMEM_LIMIT), name=name)(*g4)


class _ExchangeCarry:
    def __init__(self, names, parts, mid_at=0.5):
        nt = len(parts)
        self.mid_at = mid_at
        self.names = names
        self.in_arrays = list(parts)
        self.in_specs = [HBM] * nt
        self.out_shape = [jax.ShapeDtypeStruct((3,) + p.shape[1:], BF16) for p in parts]
        self.out_specs = [HBM] * nt
        self.scratch = ([pltpu.VMEM(p.shape[1:], BF16) for p in parts] * 2
                        + [pltpu.SemaphoreType.DMA((nt, 3)), pltpu.SemaphoreType.DMA((nt, 3)),
                           pltpu.SemaphoreType.DMA((nt,)), pltpu.SemaphoreType.DMA((nt,))])

    def _copies(self, ins, outs, scr):
        nt = len(ins)
        relayed, mine = scr[:nt], scr[nt:2 * nt]
        send_sems, recv_sems, local_sems, load_sems = scr[2 * nt:]
        x, y, c = _place()
        q = lambda cx, cy: 2 * cx + cy
        near, far = (x ^ c, y ^ (1 - c)), (x ^ (1 - c), y ^ c)

        def remote(t, k, src, dst, chip):
            return pltpu.make_async_remote_copy(
                src_ref=src, dst_ref=dst, send_sem=send_sems.at[t, k], recv_sem=recv_sems.at[t, k],
                device_id=(*chip, c), device_id_type=MESH)

        return dict(
            keep=[pltpu.make_async_copy(ins[t].at[q(x, y)], outs[t].at[0], local_sems.at[t]) for t in range(nt)],
            load=[pltpu.make_async_copy(ins[t].at[q(*far)], mine[t], load_sems.at[t]) for t in range(nt)],
            direct=[remote(t, 0, ins[t].at[q(*near)], outs[t].at[1], near) for t in range(nt)],
            relay=[remote(t, 1, ins[t].at[q(1 - x, 1 - y)], relayed[t], near) for t in range(nt)],
            merged=[remote(t, 2, mine[t], outs[t].at[2], far) for t in range(nt)],
            relayed=relayed, mine=mine)

    def start(self, ins, outs, scr):
        cps = self._copies(ins, outs, scr)
        for t in range(len(ins)):
            for kind in ("keep", "load", "direct", "relay"):
                cps[kind][t].start()

    def mid(self, ins, outs, scr):
        cps = self._copies(ins, outs, scr)
        for t in range(len(ins)):
            cps["load"][t].wait()
            cps["relay"][t].wait_recv()
            cps["mine"][t][...] = (cps["mine"][t][...].astype(F32) + cps["relayed"][t][...].astype(F32)).astype(BF16)
            cps["merged"][t].start()

    def finish(self, ins, outs, scr):
        cps = self._copies(ins, outs, scr)
        for t in range(len(ins)):
            cps["direct"][t].wait()
            cps["relay"][t].wait_send()
            cps["merged"][t].wait()
            cps["keep"][t].wait()


class _Comm:
    def __init__(self, groups):
        self.names = {tag: list(g) for tag, (g, _) in groups.items()}
        self.gathers = {tag: _GatherCarry(list(g.values()), mid_at) for tag, (g, mid_at) in groups.items()}
        self.reduced = {}

    def gathered(self, tag, outs):
        return dict(zip(self.names[tag], outs))

    def reduce_start(self, tag, grads, mid_at=0.5):
        names = list(grads)
        parts = _sibling_reduce([grads[n] for n in names], "sibling_reduce_" + tag)
        return _ExchangeCarry(names, parts, mid_at)

    def reduce_done(self, carry, outs):
        self.reduced.update(zip(carry.names, outs))

    def reduce_now(self, tag, grads):
        carry = self.reduce_start(tag, grads)
        self.reduce_done(carry, _run_carry(carry, "chip_exchange_" + tag))


def _adamw_math(w, g, m, v):
    m = B1 * m + (1.0 - B1) * g
    v = B2 * v + (1.0 - B2) * (g * g)
    m_hat = m / (1.0 - B1 ** STEP)
    v_hat = v / (1.0 - B2 ** STEP)
    delta = -LR * (m_hat / (jnp.sqrt(v_hat) + AEPS) + WD * w)
    return delta, m, v


def _adamw_big(recv, w, m, v, name):
    def body(r_ref, w_ref, m_ref, v_ref, g_ref, d_ref, mo_ref, vo_ref):
        g = r_ref[0].astype(F32)
        for q in range(1, 3):
            g = g + r_ref[q].astype(F32)
        d, mn, vn = _adamw_math(w_ref[...], g, m_ref[...], v_ref[...])
        g_ref[...] = g
        d_ref[...] = d
        mo_ref[...] = mn
        vo_ref[...] = vn

    rows, n = w.shape
    tr = rows // 2
    row = pl.BlockSpec((tr, n), lambda t: (t, 0))
    return pl.pallas_call(
        body, grid=(2,), in_specs=[pl.BlockSpec((3, tr, n), lambda t: (0, t, 0)), row, row, row],
        out_specs=[row] * 4, out_shape=[jax.ShapeDtypeStruct(w.shape, F32)] * 4,
        compiler_params=_cp(("parallel",)), name=name)(recv, w, m, v)


SMALL_NAMES = ("ffn1_norm", "mix_norm", "ffn2_norm", "conv_b", "conv_ln_g", "conv_ln_b", "q_norm", "k_norm")
SROWS = 16
LOSS_ROW = len(SMALL_NAMES)
CWF = 2


def _small_step(gs, loss_row, gcw, ws, ms, vs, wcw, mcw, vcw):
    ns = len(SMALL_NAMES)
    widths = [g.shape[1] for g in gs]

    def body(*refs):
        it = iter(refs)
        take = lambda n: [next(it) for _ in range(n)]
        g_refs, (loss_ref, gcw_ref) = take(ns), take(2)
        w_refs, m_refs, v_refs = take(ns), take(ns), take(ns)
        wcw_ref, mcw_ref, vcw_ref = take(3)
        outs = [take(4) for _ in range(ns)]
        cw_outs, (loss_out,) = take(4), take(1)
        send, slots, cslots, send_sems, recv_sems, csend_sems, crecv_sems = take(7)
        x, y, c = _place()
        me = 4 * x + 2 * y + c
        send[...] = jnp.zeros_like(send)
        for k in range(ns):
            send[k:k + 1, 0:widths[k]] = g_refs[k][...]
        send[LOSS_ROW:LOSS_ROW + 1, 0:128] = loss_ref[...]
        slots[me] = send[...]
        cslots[me] = gcw_ref[...]
        cps = []
        for k in range(1, NDEV):
            peer = (x ^ ((k >> 2) & 1), y ^ ((k >> 1) & 1), c ^ (k & 1))
            cps.append(pltpu.make_async_remote_copy(
                src_ref=send, dst_ref=slots.at[me], send_sem=send_sems.at[k - 1], recv_sem=recv_sems.at[k - 1],
                device_id=peer, device_id_type=MESH))
            cps.append(pltpu.make_async_remote_copy(
                src_ref=gcw_ref, dst_ref=cslots.at[me], send_sem=csend_sems.at[k - 1],
                recv_sem=crecv_sems.at[k - 1], device_id=peer, device_id_type=MESH))
        for cp in cps:
            cp.start()
        for cp in cps:
            cp.wait()
        tot = slots[0]
        ctot = cslots[0, me]
        for j in range(1, NDEV):
            tot = tot + slots[j]
            ctot = ctot + cslots[j, me]

        def step(g, w_ref, m_ref, v_ref, o):
            d, mn, vn = _adamw_math(w_ref[...], g, m_ref[...], v_ref[...])
            o[0][...], o[1][...], o[2][...], o[3][...] = g, d, mn, vn

        for k in range(ns):
            step(tot[k:k + 1, 0:widths[k]], w_refs[k], m_refs[k], v_refs[k], outs[k])
        step(ctot, wcw_ref, mcw_ref, vcw_ref, cw_outs)
        loss_out[...] = tot[LOSS_ROW:LOSS_ROW + 1, 0:128]

    args = [*gs, loss_row, gcw, *ws, *ms, *vs, wcw, mcw, vcw]
    out_shape = ([jax.ShapeDtypeStruct((1, n), F32) for n in widths for _ in range(4)]
                 + [jax.ShapeDtypeStruct((CWF, D), F32)] * 4 + [jax.ShapeDtypeStruct((1, 128), F32)])
    res = pl.pallas_call(
        body, in_specs=[VMEM] * len(args), out_specs=[VMEM] * len(out_shape), out_shape=out_shape,
        scratch_shapes=[pltpu.VMEM((SROWS, D), F32), pltpu.VMEM((NDEV, SROWS, D), F32),
                        pltpu.VMEM((NDEV, NDEV, CWF, D), F32)]
                       + [pltpu.SemaphoreType.DMA((NDEV - 1,))] * 4,
        name="small_step")(*args)
    per = [res[4 * k:4 * k + 4] for k in range(ns)]
    return per, res[4 * ns:4 * ns + 4], res[-1]


def _pack_cw(a):
    flat = a.reshape(a.shape[:-2] + (CK * HD,))
    pad = [(0, 0)] * (flat.ndim - 1) + [(0, CWF * D - CK * HD)]
    return jnp.pad(flat, pad).reshape(a.shape[:-2] + (CWF, D))


def _unpack_cw(v):
    return v.reshape(-1)[:CK * HD].reshape(1, CK, HD)


def kernel(x, ffn1_norm, ffn1_w_gate, ffn1_w_up, ffn1_w_down, mix_norm, w_in, q_norm, k_norm, conv_w, conv_b, conv_ln_g, conv_ln_b, w_out, ffn2_norm, ffn2_w_gate, ffn2_w_up, ffn2_w_down, loss_target, m_ffn1_norm, m_ffn1_w_gate, m_ffn1_w_up, m_ffn1_w_down, m_mix_norm, m_w_in, m_q_norm, m_k_norm, m_conv_w, m_conv_b, m_conv_ln_g, m_conv_ln_b, m_w_out, m_ffn2_norm, m_ffn2_w_gate, m_ffn2_w_up, m_ffn2_w_down, v_ffn1_norm, v_ffn1_w_gate, v_ffn1_w_up, v_ffn1_w_down, v_mix_norm, v_w_in, v_q_norm, v_k_norm, v_conv_w, v_conv_b, v_conv_ln_g, v_conv_ln_b, v_w_out, v_ffn2_norm, v_ffn2_w_gate, v_ffn2_w_up, v_ffn2_w_down):
    P = dict(ffn1_norm=ffn1_norm, ffn1_w_gate=ffn1_w_gate, ffn1_w_up=ffn1_w_up, ffn1_w_down=ffn1_w_down,
             mix_norm=mix_norm, w_in=w_in, q_norm=q_norm, k_norm=k_norm, conv_w=conv_w, conv_b=conv_b,
             conv_ln_g=conv_ln_g, conv_ln_b=conv_ln_b, w_out=w_out, ffn2_norm=ffn2_norm,
             ffn2_w_gate=ffn2_w_gate, ffn2_w_up=ffn2_w_up, ffn2_w_down=ffn2_w_down)
    M = dict(ffn1_norm=m_ffn1_norm, ffn1_w_gate=m_ffn1_w_gate, ffn1_w_up=m_ffn1_w_up, ffn1_w_down=m_ffn1_w_down,
             mix_norm=m_mix_norm, w_in=m_w_in, q_norm=m_q_norm, k_norm=m_k_norm, conv_w=m_conv_w, conv_b=m_conv_b,
             conv_ln_g=m_conv_ln_g, conv_ln_b=m_conv_ln_b, w_out=m_w_out, ffn2_norm=m_ffn2_norm,
             ffn2_w_gate=m_ffn2_w_gate, ffn2_w_up=m_ffn2_w_up, ffn2_w_down=m_ffn2_w_down)
    V = dict(ffn1_norm=v_ffn1_norm, ffn1_w_gate=v_ffn1_w_gate, ffn1_w_up=v_ffn1_w_up, ffn1_w_down=v_ffn1_w_down,
             mix_norm=v_mix_norm, w_in=v_w_in, q_norm=v_q_norm, k_norm=v_k_norm, conv_w=v_conv_w, conv_b=v_conv_b,
             conv_ln_g=v_conv_ln_g, conv_ln_b=v_conv_ln_b, w_out=v_w_out, ffn2_norm=v_ffn2_norm,
             ffn2_w_gate=v_ffn2_w_gate, ffn2_w_up=v_ffn2_w_up, ffn2_w_down=v_ffn2_w_down)
    order = ["ffn1_norm", "ffn1_w_gate", "ffn1_w_up", "ffn1_w_down", "mix_norm", "w_in", "q_norm", "k_norm",
             "conv_w", "conv_b", "conv_ln_g", "conv_ln_b", "w_out", "ffn2_norm", "ffn2_w_gate", "ffn2_w_up",
             "ffn2_w_down"]
    B, S, _ = x.shape
    T = B * S

    bigs = [("wg1", "ffn1_w_gate", True), ("wu1", "ffn1_w_up", True), ("wd1", "ffn1_w_down", False),
            ("win", "w_in", True), ("wout", "w_out", False),
            ("wg2", "ffn2_w_gate", True), ("wu2", "ffn2_w_up", True), ("wd2", "ffn2_w_down", False)]
    hm = lambda a, tr: jnp.transpose(a[0]) if tr else a[0]
    cw_pad = jnp.zeros((32, 128), F32).at[0:CK, 0:HD].set(conv_w[0])
    shard = {ln: (hm(P[pn], tr), BF16) for ln, pn, tr in bigs}
    gathered = _run_carry(_GatherCarry([shard["wg1"], shard["wu1"], (cw_pad, F32)]), "gather_first")
    W = {"wg1": gathered[0], "wu1": gathered[1]}
    cwg = gathered[2].reshape(NDEV, 32, 128)[:, 0:CK, 0:HD]
    W["conv_w"] = jnp.transpose(cwg, (1, 0, 2)).reshape(CK, DC)
    norms = {n: P[n] for n in SMALL_NAMES}
    comm = _Comm({"down_in": ({n: shard[n] for n in ("wd1", "win", "wout")}, 0.375),
                  "ffn2": ({n: shard[n] for n in ("wg2", "wu2", "wd2")}, 0.5)})

    loss_part, gx, _, small = _local_step(x.reshape(T, D), loss_target.reshape(T, D), norms, W, B, S, comm)

    G, Dl, Mn, Vn = {}, {}, {}, {}
    for ln, pn, tr in bigs:
        outs = _adamw_big(comm.reduced[ln], hm(P[pn], tr), hm(M[pn], tr), hm(V[pn], tr), "adamw_" + ln)
        G[pn], Dl[pn], Mn[pn], Vn[pn] = [(jnp.transpose(o) if tr else o)[None] for o in outs]

    dcw = small["conv_w"].reshape(CK, NDEV, HD).transpose(1, 0, 2)
    loss_row = jnp.zeros((1, 128), F32).at[0, 0].set(loss_part)
    per, cw_outs, loss_out = _small_step(
        [small[n] for n in SMALL_NAMES], loss_row, _pack_cw(dcw),
        [P[n] for n in SMALL_NAMES], [M[n] for n in SMALL_NAMES], [V[n] for n in SMALL_NAMES],
        _pack_cw(P["conv_w"][0]), _pack_cw(M["conv_w"][0]), _pack_cw(V["conv_w"][0]))
    loss = loss_out[0, 0]
    for n, outs in zip(SMALL_NAMES, per):
        G[n], Dl[n], Mn[n], Vn[n] = outs
    G["conv_w"], Dl["conv_w"], Mn["conv_w"], Vn["conv_w"] = [_unpack_cw(o) for o in cw_outs]

    return (loss, gx.reshape(B, S, D), *[G[n] for n in order], *[Dl[n] for n in order],
            *[Mn[n] for n in order], *[Vn[n] for n in order])
```

```python
import functools

import jax
import jax.numpy as jnp
from jax import lax
from jax.experimental import pallas as pl
from jax.experimental.pallas import tpu as pltpu

F32 = jnp.float32
BF16 = jnp.bfloat16

D = 1024
FF = 2816
HD = 64
DA = 512
DC = 512
DIN = 2560
CK = 31
BLK = 128
DILS = (1, 4, 16)
EPS = 1e-6
NDEV = 8
MESH = pl.DeviceIdType.MESH

LR, B1, B2, AEPS, WD, STEP = 0.001, 0.9, 0.999, 1e-08, 0.01, 10

NT = (((1,), (1,)), ((), ()))
TN = (((0,), (0,)), ((), ()))

VMEM_LIMIT = 56 * 1024 * 1024


def _cp(sem=None):
    return pltpu.CompilerParams(dimension_semantics=sem, vmem_limit_bytes=VMEM_LIMIT)


def _sigmoid(x):
    return 0.5 * (jnp.tanh(0.5 * x) + 1.0)


def _pallas(body, args, *, grid, in_specs, out_specs, out_shape, scratch_shapes, sem, name, carry=None):
    if carry is None:
        outs = pl.pallas_call(body, grid=grid, in_specs=in_specs, out_specs=out_specs, out_shape=out_shape,
                              scratch_shapes=scratch_shapes, compiler_params=_cp(sem), name=name)(*args)
        return outs, None
    n_in, n_out, n_scr = len(in_specs), len(out_shape), len(scratch_shapes)
    c_in, c_out = len(carry.in_arrays), len(carry.out_shape)

    def wrapped(*refs):
        ins, refs = refs[:n_in], refs[n_in:]
        cins, refs = refs[:c_in], refs[c_in:]
        outs, refs = refs[:n_out], refs[n_out:]
        couts, refs = refs[:c_out], refs[c_out:]
        scr, cscr = refs[:n_scr], refs[n_scr:]
        ids = [pl.program_id(a) for a in range(len(grid))]
        step = ids[0]
        for i, n in zip(ids[1:], grid[1:]):
            step = step * n + i
        steps = functools.reduce(lambda a, b: a * b, grid)

        @pl.when(step == 0)
        def _():
            carry.start(cins, couts, cscr)

        body(*ins, *outs, *scr)

        if hasattr(carry, "mid"):
            @pl.when(step == int(steps * carry.mid_at))
            def _():
                carry.mid(cins, couts, cscr)

        @pl.when(step == steps - 1)
        def _():
            carry.finish(cins, couts, cscr)

    outs = pl.pallas_call(
        wrapped, grid=grid, in_specs=list(in_specs) + carry.in_specs, out_specs=list(out_specs) + carry.out_specs,
        out_shape=list(out_shape) + carry.out_shape, scratch_shapes=list(scratch_shapes) + carry.scratch,
        compiler_params=_cp(("arbitrary",) * len(grid)), name=name)(*args, *carry.in_arrays)
    return outs[:n_out], outs[n_out:]


FC = 256


def _resident(shape):
    return pl.BlockSpec(shape, lambda *_: (0,) * len(shape), pipeline_mode=pl.Buffered(1))


def _mix_out_ffn_loss(h1, attn, conv, wout, gain, wg, wu, wd, target, name):
    T = h1.shape[0]
    tm = 512
    nt = T // tm

    def body(h1_ref, at_ref, cv_ref, wo_ref, gain_ref, wg_ref, wu_ref, wd_ref, t_ref,
             h2_ref, n_ref, g_ref, u_ref, dout_ref, dyb_ref, sq_ref, a_scr):
        xv = (h1_ref[...]
              + jnp.dot(at_ref[...], wo_ref[0:DA, :], preferred_element_type=F32)
              + jnp.dot(cv_ref[...], wo_ref[DA:D, :], preferred_element_type=F32))
        h2_ref[...] = xv
        r = lax.rsqrt(jnp.mean(xv * xv, axis=-1, keepdims=True) + EPS)
        n_ref[...] = (xv * r * gain_ref[...]).astype(BF16)
        for c in range(FF // FC):
            cols = slice(c * FC, (c + 1) * FC)
            nb = n_ref[...]
            g = lax.dot_general(nb, wg_ref[cols, :], NT, preferred_element_type=F32)
            u = lax.dot_general(nb, wu_ref[cols, :], NT, preferred_element_type=F32)
            g_ref[:, cols] = g.astype(BF16)
            u_ref[:, cols] = u.astype(BF16)
            a_scr[:, cols] = (g * _sigmoid(g) * u).astype(BF16)
        e = h2_ref[...] + 0.5 * jnp.dot(a_scr[...], wd_ref[...], preferred_element_type=F32) - t_ref[...]
        dout = e * (1.0 / D)
        dout_ref[...] = dout
        dyb_ref[...] = (0.5 * dout).astype(BF16)
        sq_ref[...] = jnp.sum(e * e, axis=0, keepdims=True)[None]

    row = pl.BlockSpec((tm, D), lambda t: (t, 0))
    half = pl.BlockSpec((tm, DA), lambda t: (t, 0))
    wide = pl.BlockSpec((tm, FF), lambda t: (t, 0))
    outs, _ = _pallas(
        body, (h1, attn, conv, wout, gain, wg, wu, wd, target), grid=(nt,),
        in_specs=[row, half, half, _resident((D, D)), _resident((1, D)), _resident((FF, D)), _resident((FF, D)),
                  _resident((FF, D)), row],
        out_specs=[row, row, wide, wide, row, row, pl.BlockSpec((1, 1, D), lambda t: (t, 0, 0))],
        out_shape=[jax.ShapeDtypeStruct((T, D), F32), jax.ShapeDtypeStruct((T, D), BF16)]
                  + [jax.ShapeDtypeStruct((T, FF), BF16)] * 2
                  + [jax.ShapeDtypeStruct((T, D), F32), jax.ShapeDtypeStruct((T, D), BF16),
                     jax.ShapeDtypeStruct((nt, 1, D), F32)],
        scratch_shapes=[pltpu.VMEM((tm, FF), BF16)], sem=("parallel",), name=name)
    return outs


def _ffn_gate_up(x, gain, wg, wu, name, carry=None):
    T = x.shape[0]
    tm = 512

    def body(x_ref, gain_ref, wg_ref, wu_ref, n_ref, g_ref, u_ref, a_ref):
        xv = x_ref[...]
        r = lax.rsqrt(jnp.mean(xv * xv, axis=-1, keepdims=True) + EPS)
        n_ref[...] = (xv * r * gain_ref[...]).astype(BF16)
        for c in range(FF // FC):
            cols = slice(c * FC, (c + 1) * FC)
            nb = n_ref[...]
            g = lax.dot_general(nb, wg_ref[cols, :], NT, preferred_element_type=F32)
            u = lax.dot_general(nb, wu_ref[cols, :], NT, preferred_element_type=F32)
            g_ref[:, cols] = g.astype(BF16)
            u_ref[:, cols] = u.astype(BF16)
            a_ref[:, cols] = (g * _sigmoid(g) * u).astype(BF16)

    row = pl.BlockSpec((tm, D), lambda t: (t, 0))
    wide = pl.BlockSpec((tm, FF), lambda t: (t, 0))
    return _pallas(
        body, (x, gain, wg, wu), grid=(T // tm,),
        in_specs=[row, _resident((1, D)), _resident((FF, D)), _resident((FF, D))],
        out_specs=[row, wide, wide, wide],
        out_shape=[jax.ShapeDtypeStruct((T, D), BF16)] + [jax.ShapeDtypeStruct((T, FF), BF16)] * 3,
        scratch_shapes=[], sem=("parallel",), name=name, carry=carry)


def _ffn_down(x, a, wd, name, carry=None):
    T = x.shape[0]
    tm = 512

    def body(x_ref, a_ref, wd_ref, h_ref):
        h_ref[...] = x_ref[...] + 0.5 * jnp.dot(a_ref[...], wd_ref[...], preferred_element_type=F32)

    row = pl.BlockSpec((tm, D), lambda t: (t, 0))
    wide = pl.BlockSpec((tm, FF), lambda t: (t, 0))
    return _pallas(
        body, (x, a, wd), grid=(T // tm,), in_specs=[row, wide, _resident((FF, D))],
        out_specs=[row], out_shape=[jax.ShapeDtypeStruct((T, D), F32)],
        scratch_shapes=[], sem=("parallel",), name=name, carry=carry)


def _ffn_bwd_act(dyb, g, u, x, dout, gain, wg, wu, wd, name, carry=None):
    T = x.shape[0]
    tm = 256
    nt = T // tm

    def body(dy_ref, g_ref, u_ref, x_ref, dout_ref, gain_ref, wg_ref, wu_ref, wd_ref,
             dg_ref, du_ref, dx_ref, dgn_ref):
        for c in range(FF // FC):
            cols = slice(c * FC, (c + 1) * FC)
            da = lax.dot_general(dy_ref[...], wd_ref[cols, :], NT, preferred_element_type=F32)
            gv = g_ref[:, cols].astype(F32)
            uv = u_ref[:, cols].astype(F32)
            sg = _sigmoid(gv)
            dg_ref[:, cols] = (da * uv * (sg * (1.0 + gv * (1.0 - sg)))).astype(BF16)
            du_ref[:, cols] = (da * (gv * sg)).astype(BF16)
        dn = (jnp.dot(dg_ref[...], wg_ref[...], preferred_element_type=F32)
              + jnp.dot(du_ref[...], wu_ref[...], preferred_element_type=F32))
        dx, dgain = _rms_bwd_rows(dn, x_ref[...], gain_ref[...])
        dx_ref[...] = dout_ref[...] + dx
        dgn_ref[...] = dgain[None]

    row = pl.BlockSpec((tm, D), lambda t: (t, 0))
    wide = pl.BlockSpec((tm, FF), lambda t: (t, 0))
    return _pallas(
        body, (dyb, g, u, x, dout, gain, wg, wu, wd), grid=(nt,),
        in_specs=[row, wide, wide, row, row, _resident((1, D)), _resident((FF, D)), _resident((FF, D)),
                  _resident((FF, D))],
        out_specs=[wide, wide, row, pl.BlockSpec((1, 1, D), lambda t: (t, 0, 0))],
        out_shape=[jax.ShapeDtypeStruct((T, FF), BF16)] * 2
                  + [jax.ShapeDtypeStruct((T, D), F32), jax.ShapeDtypeStruct((nt, 1, D), F32)],
        scratch_shapes=[], sem=("parallel",), name=name, carry=carry)


def _ffn_bwd_w(lhs, rhs, name, carry=None):
    T = rhs.shape[0]
    tf = 256
    lhs = lhs if isinstance(lhs, tuple) else (lhs,)

    def body(*refs):
        r_ref, dw_ref = refs[-2:]
        if len(lhs) == 2:
            gv = refs[0][...].astype(F32)
            lv = (gv * _sigmoid(gv) * refs[1][...].astype(F32)).astype(BF16)
        else:
            lv = refs[0][...]
        dw_ref[...] = lax.dot_general(lv, r_ref[...], TN, preferred_element_type=F32).astype(BF16)

    (dw,), got = _pallas(
        body, (*lhs, rhs), grid=(FF // tf,),
        in_specs=[pl.BlockSpec((T, tf), lambda f: (0, f))] * len(lhs) + [_resident((T, D))],
        out_specs=[pl.BlockSpec((tf, D), lambda f: (f, 0))], out_shape=[jax.ShapeDtypeStruct((FF, D), BF16)],
        scratch_shapes=[], sem=("parallel",), name=name, carry=carry)
    return dw, got


def _rms_bwd_rows(dn, xv, gain):
    r = lax.rsqrt(jnp.mean(xv * xv, axis=-1, keepdims=True) + EPS)
    xhat = xv * r
    dxhat = dn * gain
    dx = r * (dxhat - xhat * jnp.mean(dxhat * xhat, axis=-1, keepdims=True))
    return dx, jnp.sum(dn * xhat, axis=0, keepdims=True)


def _mix_in(h, gain, win, carry=None):
    T = h.shape[0]
    tm = 512

    def body(h_ref, gain_ref, w_ref, u_ref, n_ref):
        xv = h_ref[...]
        r = lax.rsqrt(jnp.mean(xv * xv, axis=-1, keepdims=True) + EPS)
        nb = (xv * r * gain_ref[...]).astype(BF16)
        n_ref[...] = nb
        u_ref[...] = lax.dot_general(nb, w_ref[...], NT, preferred_element_type=F32).astype(BF16)

    row = pl.BlockSpec((tm, D), lambda t: (t, 0))
    return _pallas(
        body, (h, gain, win), grid=(T // tm,),
        in_specs=[row, _resident((1, D)), _resident((DIN, D))],
        out_specs=[pl.BlockSpec((tm, DIN), lambda t: (t, 0)), row],
        out_shape=[jax.ShapeDtypeStruct((T, DIN), BF16), jax.ShapeDtypeStruct((T, D), BF16)],
        scratch_shapes=[], sem=("parallel",), name="mix_in", carry=carry)


def _mix_out_bwd(dh, attn, conv, wout):
    T = dh.shape[0]
    tm = 512
    nt = T // tm

    def body(dh_ref, a_ref, c_ref, w_ref, da_ref, dc_ref, dw_ref, acc_scr):
        t = pl.program_id(0)

        @pl.when(t == 0)
        def _():
            acc_scr[...] = jnp.zeros_like(acc_scr)

        dhb = dh_ref[...].astype(BF16)
        dmix = lax.dot_general(dhb, w_ref[...], NT, preferred_element_type=F32)
        da_ref[...] = dmix[:, 0:DA].astype(BF16)
        dc_ref[...] = dmix[:, DA:D].astype(BF16)
        acc_scr[0:DA, :] += lax.dot_general(a_ref[...], dhb, TN, preferred_element_type=F32)
        acc_scr[DA:D, :] += lax.dot_general(c_ref[...], dhb, TN, preferred_element_type=F32)

        @pl.when(t == nt - 1)
        def _():
            dw_ref[...] = acc_scr[...].astype(BF16)

    row = pl.BlockSpec((tm, D), lambda t: (t, 0))
    half = pl.BlockSpec((tm, DA), lambda t: (t, 0))
    full = pl.BlockSpec((D, D), lambda t: (0, 0))
    return pl.pallas_call(
        body, grid=(nt,), in_specs=[row, half, half, full], out_specs=[half, half, full],
        out_shape=[jax.ShapeDtypeStruct((T, DA), BF16)] * 2 + [jax.ShapeDtypeStruct((D, D), BF16)],
        scratch_shapes=[pltpu.VMEM((D, D), F32)],
        compiler_params=_cp(("arbitrary",)), name="mix_out_bwd")(dh, attn, conv, wout)


def _mix_in_bwd(dparts, win, nb, h, dh, gain):
    T = h.shape[0]
    tm = 512
    nt = T // tm

    def body(d0, d1, d2, d3, d4, w_ref, n_ref, h_ref, dh_ref, gain_ref,
             dw_ref, dx_ref, dyb_ref, dg_ref, acc_scr):
        t = pl.program_id(0)

        @pl.when(t == 0)
        def _():
            acc_scr[...] = jnp.zeros_like(acc_scr)

        n = n_ref[...]
        dn = jnp.zeros((tm, D), F32)
        for i, d_ref in enumerate((d0, d1, d2, d3, d4)):
            dv = d_ref[...]
            dn = dn + jnp.dot(dv, w_ref[i * DA:(i + 1) * DA, :], preferred_element_type=F32)
            acc_scr[i * DA:(i + 1) * DA, :] += lax.dot_general(dv, n, TN, preferred_element_type=F32)
        dx, dgain = _rms_bwd_rows(dn, h_ref[...], gain_ref[...])
        tot = dh_ref[...] + dx
        dx_ref[...] = tot
        dyb_ref[...] = (0.5 * tot).astype(BF16)
        dg_ref[...] = dgain[None]

        @pl.when(t == nt - 1)
        def _():
            dw_ref[...] = acc_scr[...].astype(BF16)

    row = pl.BlockSpec((tm, D), lambda t: (t, 0))
    half = pl.BlockSpec((tm, DA), lambda t: (t, 0))
    full = pl.BlockSpec((DIN, D), lambda t: (0, 0))
    return pl.pallas_call(
        body, grid=(nt,),
        in_specs=[half] * 5 + [full, row, row, row, pl.BlockSpec((1, D), lambda t: (0, 0))],
        out_specs=[full, row, row, pl.BlockSpec((1, 1, D), lambda t: (t, 0, 0))],
        out_shape=[jax.ShapeDtypeStruct((DIN, D), BF16), jax.ShapeDtypeStruct((T, D), F32),
                   jax.ShapeDtypeStruct((T, D), BF16), jax.ShapeDtypeStruct((nt, 1, D), F32)],
        scratch_shapes=[pltpu.VMEM((DIN, D), F32)],
        compiler_params=_cp(("arbitrary",)), name="mix_in_bwd")(*dparts, win, nb, h, dh, gain)


def _head_masks():
    lane = lax.broadcasted_iota(jnp.int32, (1, 2 * HD), 1)
    m0 = lane < HD
    return m0, jnp.logical_not(m0)


def _stack_heads(v, m0, m1):
    z = jnp.zeros_like(v)
    return jnp.concatenate([jnp.where(m0, v, z), jnp.where(m1, v, z)], axis=0)


def _unstack_heads(v2, m0):
    return jnp.where(m0, v2[0:BLK], v2[BLK:2 * BLK])


def _head_sums(xv):
    ri = lax.broadcasted_iota(jnp.int32, (2 * HD, 2 * HD), 0)
    ci = lax.broadcasted_iota(jnp.int32, (2 * HD, 2 * HD), 1)
    ones = jnp.where((ri < HD) == (ci < HD), 1.0, 0.0).astype(BF16)
    hi = xv.astype(BF16)
    lo = (xv - hi.astype(F32)).astype(BF16)
    return (jnp.dot(hi, ones, preferred_element_type=F32) + jnp.dot(lo, ones, preferred_element_type=F32))


def _head_rms(xv):
    return lax.rsqrt(_head_sums(xv * xv) * (1.0 / HD) + EPS)


def _band_mask(first):
    qi = lax.broadcasted_iota(jnp.int32, (BLK, 2 * BLK), 0)
    ci = lax.broadcasted_iota(jnp.int32, (BLK, 2 * BLK), 1)
    band = (ci >= qi) & (ci <= qi + BLK)
    return band & ((ci >= BLK) | jnp.logical_not(first))


def _block_rows(j, d, seg):
    r, n = j // seg, j % seg
    start = r + (d * BLK) * n
    first = n == 0
    prev = jnp.where(first, start, start - d * BLK)
    return pl.ds(start, BLK, stride=d), pl.ds(prev, BLK, stride=d), first


def _block_keys(refs, cur, prev, first, single):
    if single:
        qi = lax.broadcasted_iota(jnp.int32, (BLK, BLK), 0)
        ci = lax.broadcasted_iota(jnp.int32, (BLK, BLK), 1)
        return [r[cur, :].astype(BF16) for r in refs], ci <= qi
    return ([jnp.concatenate([r[prev, :], r[cur, :]], axis=0).astype(BF16) for r in refs], _band_mask(first))


def _attn_fwd(u, qg2, kg2, B, S, carry=None):
    T = B * S
    NB = S // BLK
    scale = HD ** -0.5

    def body(q_ref, k_ref, v_ref, qg_ref, kg_ref, o_ref, lse_ref, qn, kn, vn, os_, ls_):
        m0, m1 = _head_masks()
        qv = q_ref[...].astype(F32)
        qn[...] = qv * _head_rms(qv) * (qg_ref[...] * scale)
        kv = k_ref[...].astype(F32)
        kn[...] = kv * _head_rms(kv) * kg_ref[...]
        vn[...] = v_ref[...].astype(F32)

        for i, d in enumerate(DILS):
            seg = NB // d

            def blk(j, c, i=i, d=d, seg=seg):
                cur, prev, first = _block_rows(j, d, seg)
                q2 = _stack_heads(qn[cur, :].astype(BF16), m0, m1)
                (kk, vv), mask = _block_keys((kn, vn), cur, prev, first, False)
                s = lax.dot_general(q2, kk, NT, preferred_element_type=F32)
                s = jnp.where(jnp.concatenate([mask, mask], axis=0), s, -1e30)
                mx = jnp.max(s, axis=-1, keepdims=True)
                p = jnp.exp(s - mx)
                l = jnp.sum(p, axis=-1, keepdims=True)
                o2 = jnp.dot((p * (1.0 / l)).astype(BF16), vv, preferred_element_type=F32)
                os_[i, cur, :] = _unstack_heads(o2, m0)
                ls_[i, cur, :] = _unstack_heads(mx + jnp.log(l), m0)
                return c

            lax.fori_loop(0, NB, blk, 0, unroll=8)

        def comb(c, carry):
            rows = pl.ds(pl.multiple_of(c * 256, 256), 256)
            l0, l1, l2 = ls_[0, rows, :], ls_[1, rows, :], ls_[2, rows, :]
            mx = jnp.maximum(jnp.maximum(l0, l1), l2)
            e0, e1, e2 = jnp.exp(l0 - mx), jnp.exp(l1 - mx), jnp.exp(l2 - mx)
            tot = e0 + e1 + e2
            inv = 1.0 / tot
            o = (e0 * os_[0, rows, :] + e1 * os_[1, rows, :] + e2 * os_[2, rows, :]) * inv
            o_ref[rows, :] = o.astype(BF16)
            lse_ref[rows, :] = mx + jnp.log(tot)
            return carry

        lax.fori_loop(0, S // 256, comb, 0)

    pair = 2 * HD
    blk_spec = lambda off: pl.BlockSpec((S, pair), lambda b, p, off=off: (b, off + p))
    gspec = pl.BlockSpec((1, pair), lambda b, p: (0, 0))
    return _pallas(
        body, (u, u, u, qg2, kg2), grid=(B, DA // pair),
        in_specs=[blk_spec(0), blk_spec(DA // pair), blk_spec(2 * DA // pair), gspec, gspec],
        out_specs=[blk_spec(0), blk_spec(0)],
        out_shape=[jax.ShapeDtypeStruct((T, DA), BF16), jax.ShapeDtypeStruct((T, DA), F32)],
        scratch_shapes=[pltpu.VMEM((S, pair), F32)] * 3 + [pltpu.VMEM((3, S, pair), F32)] * 2,
        sem=("parallel", "parallel"), name="attn_fwd", carry=carry)


def _attn_bwd(u, attn, dattn, lse, qg2, kg2, B, S, carry=None):
    T = B * S
    NB = S // BLK
    scale = HD ** -0.5
    pair = 2 * HD

    def body(q_ref, k_ref, v_ref, o_ref, do_ref, lse_ref, qg_ref, kg_ref,
             dq_ref, dk_ref, dv_ref, dgn_ref,
             qn, kn, vn, don, ldl, accq, acck, accv, rq, rk):
        m0, m1 = _head_masks()
        lane = lax.broadcasted_iota(jnp.int32, (1, pair), 1)
        qv = q_ref[...].astype(F32)
        rq[...] = _head_rms(qv)
        qn[...] = qv * rq[...] * (qg_ref[...] * scale)
        kv = k_ref[...].astype(F32)
        rk[...] = _head_rms(kv)
        kn[...] = kv * rk[...] * kg_ref[...]
        vn[...] = v_ref[...].astype(F32)
        dov = do_ref[...].astype(F32)
        don[...] = dov
        ldl[...] = jnp.where((lane % HD) < HD // 2, lse_ref[...], _head_sums(dov * o_ref[...].astype(F32)))
        accq[...] = jnp.zeros_like(accq)
        acck[...] = jnp.zeros_like(acck)
        accv[...] = jnp.zeros_like(accv)

        for i, d in enumerate(DILS):
            seg = NB // d

            def blk(j, c, d=d, seg=seg):
                cur, prev, first = _block_rows(j, d, seg)
                q2 = _stack_heads(qn[cur, :].astype(BF16), m0, m1)
                do2 = _stack_heads(don[cur, :].astype(BF16), m0, m1)
                (kk, vv), mask = _block_keys((kn, vn), cur, prev, first, seg == 1)
                ldv = ldl[cur, :]
                lse2 = jnp.concatenate([ldv[:, 0:1], ldv[:, HD:HD + 1]], axis=0)
                dl2 = jnp.concatenate([ldv[:, HD // 2:HD // 2 + 1], ldv[:, HD + HD // 2:HD + HD // 2 + 1]], axis=0)
                s = lax.dot_general(q2, kk, NT, preferred_element_type=F32)
                p = jnp.where(jnp.concatenate([mask, mask], axis=0), jnp.exp(s - lse2), 0.0)
                dp = lax.dot_general(do2, vv, NT, preferred_element_type=F32)
                ds = (p * (dp - dl2)).astype(BF16)
                dq_acc = _unstack_heads(jnp.dot(ds, kk, preferred_element_type=F32), m0)
                dk_acc = lax.dot_general(ds, q2, TN, preferred_element_type=F32)
                dv_acc = lax.dot_general(p.astype(BF16), do2, TN, preferred_element_type=F32)
                accq[cur, :] += dq_acc
                if seg == 1:
                    acck[cur, :] += dk_acc
                    accv[cur, :] += dv_acc
                else:
                    acck[prev, :] += dk_acc[0:BLK]
                    acck[cur, :] += dk_acc[BLK:2 * BLK]
                    accv[prev, :] += dv_acc[0:BLK]
                    accv[cur, :] += dv_acc[BLK:2 * BLK]
                return c

            lax.fori_loop(0, NB, blk, 0, unroll=8)

        def norm_bwd(x_ref, r_ref, dn, gain):
            r = r_ref[...]
            xhat = x_ref[...].astype(F32) * r
            dxhat = dn * gain
            dx = r * (dxhat - xhat * (_head_sums(dxhat * xhat) * (1.0 / HD)))
            return dx, jnp.sum(dn * xhat, axis=0, keepdims=True)

        dq, dgq = norm_bwd(q_ref, rq, accq[...], qg_ref[...] * scale)
        dk, dgk = norm_bwd(k_ref, rk, acck[...], kg_ref[...])
        dq_ref[...] = dq.astype(BF16)
        dk_ref[...] = dk.astype(BF16)
        dv_ref[...] = accv[...].astype(BF16)
        dgn_ref[...] = jnp.concatenate([dgq * scale, dgk, jnp.zeros((6, pair), F32)], axis=0)[None]

    blk_spec = lambda off: pl.BlockSpec((S, pair), lambda b, p, off=off: (b, off + p))
    gspec = pl.BlockSpec((1, pair), lambda b, p: (0, 0))
    np_ = DA // pair
    return _pallas(
        body, (u, u, u, attn, dattn, lse, qg2, kg2), grid=(B, np_),
        in_specs=[blk_spec(0), blk_spec(np_), blk_spec(2 * np_), blk_spec(0), blk_spec(0), blk_spec(0),
                  gspec, gspec],
        out_specs=[blk_spec(0), blk_spec(0), blk_spec(0),
                   pl.BlockSpec((1, 8, pair), lambda b, p: (b * np_ + p, 0, 0))],
        out_shape=[jax.ShapeDtypeStruct((T, DA), BF16)] * 3 + [jax.ShapeDtypeStruct((B * np_, 8, pair), F32)],
        scratch_shapes=[pltpu.VMEM((S, pair), F32)] * 10,
        sem=("parallel", "parallel"), name="attn_bwd", carry=carry)


CT = 32
CPAD = 32


def _shifted(win, offsets):
    rolled, out = {}, {}
    n = win.shape[0]
    for o in offsets:
        sub = o % 8
        if sub not in rolled:
            rolled[sub] = win if sub == 0 else pltpu.roll(win, n - sub, 0)
        out[o] = rolled[sub][o - sub:o - sub + CT, :]
    return out


def _ln_fwd(y, g, b):
    mu = jnp.mean(y, axis=-1, keepdims=True)
    yc = y - mu
    rstd = lax.rsqrt(jnp.mean(yc * yc, axis=-1, keepdims=True) + EPS)
    xhat = yc * rstd
    return xhat, rstd, xhat * g + b


def _fill_glu(ca_ref, cg_ref, glu, S):
    glu[pl.ds(0, CPAD), :] = jnp.zeros((CPAD, DC), F32)

    def fill(i, c):
        rows = pl.ds(pl.multiple_of(i * 256, 256), 256)
        a = ca_ref[rows, :].astype(F32)
        gt = cg_ref[rows, :].astype(F32)
        glu[pl.ds(pl.multiple_of(CPAD + i * 256, CT), 256), :] = a * _sigmoid(gt)
        return c

    lax.fori_loop(0, S // 256, fill, 0)


def _conv_fwd(u, cw, cb, lg, lb, B, S):
    T = B * S

    def body(ca_ref, cg_ref, w_ref, b_ref, lg_ref, lb_ref, o_ref, y_ref, glu):
        _fill_glu(ca_ref, cg_ref, glu, S)

        def step(i, c):
            t0 = pl.multiple_of(i * CT, CT)
            win = glu[pl.ds(t0, 2 * CT), :]
            acc = jnp.zeros((CT, DC), F32) + b_ref[...]
            taps = _shifted(win, [k + 2 for k in range(CK)])
            for k in range(CK):
                acc = acc + taps[k + 2] * w_ref[k:k + 1, :]
            y_ref[pl.ds(t0, CT), :] = acc
            _, _, z = _ln_fwd(acc, lg_ref[...], lb_ref[...])
            o_ref[pl.ds(t0, CT), :] = (z * _sigmoid(z)).astype(BF16)
            return c

        lax.fori_loop(0, S // CT, step, 0, unroll=2)

    vec = pl.BlockSpec((1, DC), lambda b: (0, 0))
    return pl.pallas_call(
        body, grid=(B,),
        in_specs=[pl.BlockSpec((S, DC), lambda b: (b, 3)), pl.BlockSpec((S, DC), lambda b: (b, 4)),
                  pl.BlockSpec((CT, DC), lambda b: (0, 0)), vec, vec, vec],
        out_specs=[pl.BlockSpec((S, DC), lambda b: (b, 0))] * 2,
        out_shape=[jax.ShapeDtypeStruct((T, DC), BF16), jax.ShapeDtypeStruct((T, DC), F32)],
        scratch_shapes=[pltpu.VMEM((CPAD + S, DC), F32)],
        compiler_params=_cp(("parallel",)), name="conv_fwd")(u, u, cw, cb, lg, lb)


def _conv_bwd(u, y, dconv, cw, lg, lb, B, S):
    T = B * S

    def body(ca_ref, cg_ref, y_ref, dc_ref, w_ref, lg_ref, lb_ref,
             dca_ref, dcg_ref, dw_ref, ds_ref, glu, dyp, dwacc):
        _fill_glu(ca_ref, cg_ref, glu, S)
        dyp[pl.ds(S, CPAD), :] = jnp.zeros((CPAD, DC), F32)
        lgv, lbv = lg_ref[...], lb_ref[...]

        def sum8(v):
            return functools.reduce(jnp.add, [v[r:r + 8] for r in range(0, v.shape[0], 8)])

        P1 = 4 * CT

        def p1(i, carry):
            sb, sg, sl = carry
            t0 = pl.multiple_of(i * P1, P1)
            xhat, rstd, z = _ln_fwd(y_ref[pl.ds(t0, P1), :], lgv, lbv)
            sz = _sigmoid(z)
            dz = dc_ref[pl.ds(t0, P1), :].astype(F32) * (sz * (1.0 + z * (1.0 - sz)))
            dxhat = dz * lgv
            dy = rstd * (dxhat - jnp.mean(dxhat, axis=-1, keepdims=True)
                         - xhat * jnp.mean(dxhat * xhat, axis=-1, keepdims=True))
            dyp[pl.ds(t0, P1), :] = dy
            return sb + sum8(dy), sg + sum8(dz * xhat), sl + sum8(dz)

        z8 = jnp.zeros((8, DC), F32)
        sb, sg, sl = lax.fori_loop(0, S // P1, p1, (z8, z8, z8))
        rs = lambda v: jnp.sum(v, axis=0, keepdims=True)
        ds_ref[...] = jnp.concatenate([rs(sb), rs(sg), rs(sl), jnp.zeros((5, DC), F32)], axis=0)[None]

        def p2(i, c):
            t0 = pl.multiple_of(i * CT, CT)
            win = dyp[pl.ds(t0, 2 * CT), :]
            acc = jnp.zeros((CT, DC), F32)
            taps = _shifted(win, [30 - k for k in range(CK)])
            for k in range(CK):
                acc = acc + taps[30 - k] * w_ref[k:k + 1, :]
            a = ca_ref[pl.ds(t0, CT), :].astype(F32)
            sgt = _sigmoid(cg_ref[pl.ds(t0, CT), :].astype(F32))
            dca_ref[pl.ds(t0, CT), :] = (acc * sgt).astype(BF16)
            dcg_ref[pl.ds(t0, CT), :] = (acc * a * sgt * (1.0 - sgt)).astype(BF16)
            return c

        lax.fori_loop(0, S // CT, p2, 0)

        dwacc[...] = jnp.zeros_like(dwacc)

        def p3(i, c):
            t0 = pl.multiple_of(i * CT, CT)
            win = glu[pl.ds(t0, 2 * CT), :]
            dy = dyp[pl.ds(t0, CT), :]
            for k in range(CK):
                dwacc[k] += sum8(dy * win[k + 2:k + 2 + CT, :])
            return c

        lax.fori_loop(0, S // CT, p3, 0)
        dw_ref[...] = jnp.sum(dwacc[...], axis=1)[None]

    vec = pl.BlockSpec((1, DC), lambda b: (0, 0))
    seq = pl.BlockSpec((S, DC), lambda b: (b, 0))
    return pl.pallas_call(
        body, grid=(B,),
        in_specs=[pl.BlockSpec((S, DC), lambda b: (b, 3)), pl.BlockSpec((S, DC), lambda b: (b, 4)),
                  seq, seq, pl.BlockSpec((CT, DC), lambda b: (0, 0)), vec, vec],
        out_specs=[seq, seq, pl.BlockSpec((1, CT, DC), lambda b: (b, 0, 0)),
                   pl.BlockSpec((1, 8, DC), lambda b: (b, 0, 0))],
        out_shape=[jax.ShapeDtypeStruct((T, DC), BF16)] * 2
                  + [jax.ShapeDtypeStruct((B, CT, DC), F32), jax.ShapeDtypeStruct((B, 8, DC), F32)],
        scratch_shapes=[pltpu.VMEM((CPAD + S, DC), F32), pltpu.VMEM((S + CPAD, DC), F32),
                        pltpu.VMEM((CT, 8, DC), F32)],
        compiler_params=_cp(("parallel",)), name="conv_bwd")(u, u, y, dconv, cw, lg, lb)


def _local_step(x, target, norms, W, B, S, comm=None):
    qg2 = jnp.concatenate([norms["q_norm"], norms["q_norm"]], axis=1)
    kg2 = jnp.concatenate([norms["k_norm"], norms["k_norm"]], axis=1)
    cw = jnp.concatenate([W["conv_w"], jnp.zeros((1, DC), F32)], axis=0)

    W = dict(W)
    (n1, g1, u1, act1), got = _ffn_gate_up(x, norms["ffn1_norm"], W["wg1"], W["wu1"], "ffn1_gate_up",
                                           carry=comm.gathers["down_in"] if comm else None)
    if comm:
        W.update(comm.gathered("down_in", got))
    (h1,), _ = _ffn_down(x, act1, W["wd1"], "ffn1_down")
    (u, n2), _ = _mix_in(h1, norms["mix_norm"], W["win"])
    (attn, lse), got = _attn_fwd(u, qg2, kg2, B, S, carry=comm.gathers["ffn2"] if comm else None)
    if comm:
        W = dict(W, **comm.gathered("ffn2", got))
    conv, y = _conv_fwd(u, cw, norms["conv_b"], norms["conv_ln_g"], norms["conv_ln_b"], B, S)
    h2, n3, g2, u2, dout, dyb, sq = _mix_out_ffn_loss(h1, attn, conv, W["wout"], norms["ffn2_norm"],
                                                      W["wg2"], W["wu2"], W["wd2"], target, "ffn2_fwd")
    loss = (0.5 / D) * jnp.sum(sq)

    (dg2, du2, dh2, dgn_ffn2), _ = _ffn_bwd_act(dyb, g2, u2, h2, dout, norms["ffn2_norm"],
                                               W["wg2"], W["wu2"], W["wd2"], "ffn2_bwd_act")
    dwd2, _ = _ffn_bwd_w((g2, u2), dyb, "ffn2_bwd_wd")
    dwg2, _ = _ffn_bwd_w(dg2, n3, "ffn2_bwd_wg")
    dwu2, _ = _ffn_bwd_w(du2, n3, "ffn2_bwd_wu")
    dattn, dconv, dwout = _mix_out_bwd(dh2, attn, conv, W["wout"])
    carry = comm.reduce_start("ffn2", {"wg2": dwg2, "wu2": dwu2, "wd2": dwd2, "wout": dwout}) if comm else None
    (dq, dk, dv, dgn_qk), got = _attn_bwd(u, attn, dattn, lse, qg2, kg2, B, S, carry=carry)
    if comm:
        comm.reduce_done(carry, got)
    dca, dcg, dcw, dcs = _conv_bwd(u, y, dconv, cw, norms["conv_ln_g"], norms["conv_ln_b"], B, S)
    dwin, dh1, dyb1, dgn_mix = _mix_in_bwd((dq, dk, dv, dca, dcg), W["win"], n2, h1, dh2, norms["mix_norm"])
    (dg1, du1, gx, dgn_ffn1), _ = _ffn_bwd_act(dyb1, g1, u1, x, dh1, norms["ffn1_norm"],
                                              W["wg1"], W["wu1"], W["wd1"], "ffn1_bwd_act")
    carry = comm.reduce_start("win", {"win": dwin}) if comm else None
    dwd1, got = _ffn_bwd_w(act1, dyb1, "ffn1_bwd_wd", carry=carry)
    if comm:
        comm.reduce_done(carry, got)
        carry = comm.reduce_start("wd1", {"wd1": dwd1})
    dwg1, got = _ffn_bwd_w(dg1, n1, "ffn1_bwd_wg", carry=carry)
    if comm:
        comm.reduce_done(carry, got)
        carry = comm.reduce_start("wg1", {"wg1": dwg1})
    dwu1, got = _ffn_bwd_w(du1, n1, "ffn1_bwd_wu", carry=carry)
    if comm:
        comm.reduce_done(carry, got)
        comm.last = comm.reduce_start("wu1", {"wu1": dwu1})

    qk = jnp.sum(dgn_qk, axis=0)
    cs = jnp.sum(dcs, axis=0)
    small = {
        "ffn1_norm": jnp.sum(dgn_ffn1, axis=0),
        "mix_norm": jnp.sum(dgn_mix, axis=0),
        "q_norm": qk[0:1, 0:HD] + qk[0:1, HD:2 * HD],
        "k_norm": qk[1:2, 0:HD] + qk[1:2, HD:2 * HD],
        "conv_w": jnp.sum(dcw, axis=0)[0:CK],
        "conv_b": cs[0:1],
        "conv_ln_g": cs[1:2],
        "conv_ln_b": cs[2:3],
        "ffn2_norm": jnp.sum(dgn_ffn2, axis=0),
    }
    big = {"wg1": dwg1, "wu1": dwu1, "wd1": dwd1, "win": dwin, "wout": dwout,
           "wg2": dwg2, "wu2": dwu2, "wd2": dwd2}
    return loss, gx, big, small


HBM = pl.BlockSpec(memory_space=pltpu.HBM)
VMEM = pl.BlockSpec(memory_space=pltpu.VMEM)


def _place():
    return lax.axis_index("x"), lax.axis_index("y"), lax.axis_index("c")


class _GatherCarry:
    def __init__(self, shards, mid_at=0.5):
        nt = len(shards)
        self.mid_at = mid_at
        self.shards = shards
        self.in_arrays = [s for s, _ in shards]
        self.in_specs = [VMEM] * nt
        self.out_shape = [jax.ShapeDtypeStruct((NDEV * s.shape[0], s.shape[1]), dt) for s, dt in shards]
        self.out_specs = [HBM] * nt
        self.scratch = ([pltpu.VMEM(s.shape, dt) for s, dt in shards]
                        + [pltpu.SemaphoreType.DMA((nt, 7)), pltpu.SemaphoreType.DMA((nt, 7)),
                           pltpu.SemaphoreType.DMA((nt,))])

    def _copies(self, outs, scr):
        nt = len(self.shards)
        stages = scr[:nt]
        send_sems, recv_sems, local_sems = scr[nt:]
        x, y, c = _place()
        me, sibling = (x, y, c), (x, y, 1 - c)
        xn, yn, diag = (1 - x, y, c), (x, 1 - y, c), (1 - x, 1 - y, c)
        via = (x ^ c, y ^ (1 - c), c)
        onto = (x ^ (1 - c), y ^ c, c)

        def rows(t, px, py, pc):
            r = self.shards[t][0].shape[0]
            return outs[t].at[pl.ds((4 * px + 2 * py + pc) * r, r), :]

        def copy(t, k, block, to, src=None):
            return pltpu.make_async_remote_copy(
                src_ref=rows(t, *block) if src is None else src, dst_ref=rows(t, *block),
                send_sem=send_sems.at[t, k], recv_sem=recv_sems.at[t, k],
                device_id=to, device_id_type=MESH)

        sib = lambda b: (b[0], b[1], 1 - c)
        return dict(
            local=[pltpu.make_async_copy(stages[t], rows(t, *me), local_sems.at[t]) for t in range(nt)],
            own=[[copy(t, 0, me, sibling, src=stages[t]), copy(t, 1, me, xn, src=stages[t]),
                  copy(t, 2, me, yn, src=stages[t])] for t in range(nt)],
            relay=[copy(t, 3, via, onto) for t in range(nt)],
            down=[[copy(t, 4, xn, sibling), copy(t, 5, yn, sibling)] for t in range(nt)],
            down_diag=[copy(t, 6, diag, sibling) for t in range(nt)],
            got_xy=[[copy(t, 1, xn, me), copy(t, 2, yn, me)] for t in range(nt)],
            got_diag=[copy(t, 3, diag, me) for t in range(nt)],
            got_sib=[[copy(t, 0, sibling, me), copy(t, 4, sib(xn), me), copy(t, 5, sib(yn), me),
                      copy(t, 6, sib(diag), me)] for t in range(nt)])

    def start(self, ins, outs, scr):
        cps = self._copies(outs, scr)
        for t, (_, dt) in enumerate(self.shards):
            scr[t][...] = ins[t][...].astype(dt)
            for cp in [cps["local"][t]] + cps["own"][t]:
                cp.start()

    def mid(self, ins, outs, scr):
        cps = self._copies(outs, scr)
        for t in range(len(self.shards)):
            for cp in cps["got_xy"][t]:
                cp.wait_recv()
            for cp in [cps["relay"][t]] + cps["down"][t]:
                cp.start()

    def finish(self, ins, outs, scr):
        cps = self._copies(outs, scr)
        for t in range(len(self.shards)):
            cps["got_diag"][t].wait_recv()
            cps["down_diag"][t].start()
        for t in range(len(self.shards)):
            for cp in cps["got_sib"][t]:
                cp.wait_recv()
            for cp in cps["own"][t] + [cps["relay"][t]] + cps["down"][t] + [cps["down_diag"][t]]:
                cp.wait_send()
            cps["local"][t].wait()


def _run_carry(carry, name):
    def body(*refs):
        n_in, n_out = len(carry.in_arrays), len(carry.out_shape)
        ins, outs, scr = refs[:n_in], refs[n_in:n_in + n_out], refs[n_in + n_out:]
        carry.start(ins, outs, scr)
        if hasattr(carry, "mid"):
            carry.mid(ins, outs, scr)
        carry.finish(ins, outs, scr)

    return pl.pallas_call(
        body, in_specs=carry.in_specs, out_specs=carry.out_specs, out_shape=carry.out_shape,
        scratch_shapes=carry.scratch, compiler_params=pltpu.CompilerParams(vmem_limit_bytes=VMEM_LIMIT),
        name=name)(*carry.in_arrays)


def _sibling_reduce(grads, name):
    nt = len(grads)
    g4 = [g.reshape(4, 2, g.shape[0] // NDEV, g.shape[1]) for g in grads]

    def body(*refs):
        ins, outs = refs[:nt], refs[nt:2 * nt]
        recv, own = refs[2 * nt:3 * nt], refs[3 * nt:4 * nt]
        send_sems, recv_sems, load_sems, store_sems = refs[4 * nt:]
        x, y, c = _place()
        sends = [pltpu.make_async_remote_copy(
            src_ref=ins[t].at[:, 1 - c], dst_ref=recv[t], send_sem=send_sems.at[t], recv_sem=recv_sems.at[t],
            device_id=(x, y, 1 - c), device_id_type=MESH) for t in range(nt)]
        loads = [pltpu.make_async_copy(ins[t].at[:, c], own[t], load_sems.at[t]) for t in range(nt)]
        stores = [pltpu.make_async_copy(own[t], outs[t], store_sems.at[t]) for t in range(nt)]
        for cp in sends + loads:
            cp.start()
        for t in range(nt):
            loads[t].wait()
            sends[t].wait_recv()
            for q in range(4):
                own[t][q] = (own[t][q].astype(F32) + recv[t][q].astype(F32)).astype(BF16)
            stores[t].start()
        for t in range(nt):
            sends[t].wait_send()
            stores[t].wait()

    blocks = [(4,) + g.shape[2:] for g in g4]
    return pl.pallas_call(
        body, in_specs=[HBM] * nt, out_specs=[HBM] * nt,
        out_shape=[jax.ShapeDtypeStruct(b, BF16) for b in blocks],
        scratch_shapes=[pltpu.VMEM(b, BF16) for b in blocks] * 2 + [pltpu.SemaphoreType.DMA((nt,))] * 4,
        compiler_params=pltpu.CompilerParams(vmem_limit_bytes=VMEM_LIMIT), name=name)(*g4)


class _ExchangeCarry:
    def __init__(self, names, parts, mid_at=0.5):
        nt = len(parts)
        self.mid_at = mid_at
        self.names = names
        self.in_arrays = list(parts)
        self.in_specs = [HBM] * nt
        self.out_shape = [jax.ShapeDtypeStruct((3,) + p.shape[1:], BF16) for p in parts]
        self.out_specs = [HBM] * nt
        self.scratch = ([pltpu.VMEM(p.shape[1:], BF16) for p in parts] * 2
                        + [pltpu.SemaphoreType.DMA((nt, 3)), pltpu.SemaphoreType.DMA((nt, 3)),
                           pltpu.SemaphoreType.DMA((nt,)), pltpu.SemaphoreType.DMA((nt,))])

    def _copies(self, ins, outs, scr):
        nt = len(ins)
        relayed, mine = scr[:nt], scr[nt:2 * nt]
        send_sems, recv_sems, local_sems, load_sems = scr[2 * nt:]
        x, y, c = _place()
        q = lambda cx, cy: 2 * cx + cy
        near, far = (x ^ c, y ^ (1 - c)), (x ^ (1 - c), y ^ c)

        def remote(t, k, src, dst, chip):
            return pltpu.make_async_remote_copy(
                src_ref=src, dst_ref=dst, send_sem=send_sems.at[t, k], recv_sem=recv_sems.at[t, k],
                device_id=(*chip, c), device_id_type=MESH)

        return dict(
            keep=[pltpu.make_async_copy(ins[t].at[q(x, y)], outs[t].at[0], local_sems.at[t]) for t in range(nt)],
            load=[pltpu.make_async_copy(ins[t].at[q(*far)], mine[t], load_sems.at[t]) for t in range(nt)],
            direct=[remote(t, 0, ins[t].at[q(*near)], outs[t].at[1], near) for t in range(nt)],
            relay=[remote(t, 1, ins[t].at[q(1 - x, 1 - y)], relayed[t], near) for t in range(nt)],
            merged=[remote(t, 2, mine[t], outs[t].at[2], far) for t in range(nt)],
            relayed=relayed, mine=mine)

    def start(self, ins, outs, scr):
        cps = self._copies(ins, outs, scr)
        for t in range(len(ins)):
            for kind in ("keep", "load", "direct", "relay"):
                cps[kind][t].start()

    def mid(self, ins, outs, scr):
        cps = self._copies(ins, outs, scr)
        for t in range(len(ins)):
            cps["load"][t].wait()
            cps["relay"][t].wait_recv()
            cps["mine"][t][...] = (cps["mine"][t][...].astype(F32) + cps["relayed"][t][...].astype(F32)).astype(BF16)
            cps["merged"][t].start()

    def finish(self, ins, outs, scr):
        cps = self._copies(ins, outs, scr)
        for t in range(len(ins)):
            cps["direct"][t].wait()
            cps["relay"][t].wait_send()
            cps["merged"][t].wait()
            cps["keep"][t].wait()


class _Comm:
    def __init__(self, groups):
        self.names = {tag: list(g) for tag, (g, _) in groups.items()}
        self.gathers = {tag: _GatherCarry(list(g.values()), mid_at) for tag, (g, mid_at) in groups.items()}
        self.reduced = {}

    def gathered(self, tag, outs):
        return dict(zip(self.names[tag], outs))

    def reduce_start(self, tag, grads, mid_at=0.5):
        names = list(grads)
        parts = _sibling_reduce([grads[n] for n in names], "sibling_reduce_" + tag)
        return _ExchangeCarry(names, parts, mid_at)

    def reduce_done(self, carry, outs):
        self.reduced.update(zip(carry.names, outs))


def _adamw_math(w, g, m, v):
    m = B1 * m + (1.0 - B1) * g
    v = B2 * v + (1.0 - B2) * (g * g)
    m_hat = m / (1.0 - B1 ** STEP)
    v_hat = v / (1.0 - B2 ** STEP)
    delta = -LR * (m_hat / (jnp.sqrt(v_hat) + AEPS) + WD * w)
    return delta, m, v


def _adamw_big(recv, w, m, v, name):
    def body(r_ref, w_ref, m_ref, v_ref, g_ref, d_ref, mo_ref, vo_ref):
        g = r_ref[0].astype(F32)
        for q in range(1, 3):
            g = g + r_ref[q].astype(F32)
        d, mn, vn = _adamw_math(w_ref[...], g, m_ref[...], v_ref[...])
        g_ref[...] = g
        d_ref[...] = d
        mo_ref[...] = mn
        vo_ref[...] = vn

    rows, n = w.shape
    tr = rows // 2
    row = pl.BlockSpec((tr, n), lambda t: (t, 0))
    return pl.pallas_call(
        body, grid=(2,), in_specs=[pl.BlockSpec((3, tr, n), lambda t: (0, t, 0)), row, row, row],
        out_specs=[row] * 4, out_shape=[jax.ShapeDtypeStruct(w.shape, F32)] * 4,
        compiler_params=_cp(("parallel",)), name=name)(recv, w, m, v)


SMALL_NAMES = ("ffn1_norm", "mix_norm", "ffn2_norm", "conv_b", "conv_ln_g", "conv_ln_b", "q_norm", "k_norm")
SROWS = 16
LOSS_ROW = len(SMALL_NAMES)
CWF = 2


def _small_step(gs, loss_row, gcw, ws, ms, vs, wcw, mcw, vcw, carry=None):
    ns = len(SMALL_NAMES)
    widths = [g.shape[1] for g in gs]

    def body(*refs):
        it = iter(refs)
        take = lambda n: [next(it) for _ in range(n)]
        g_refs, (loss_ref, gcw_ref) = take(ns), take(2)
        w_refs, m_refs, v_refs = take(ns), take(ns), take(ns)
        wcw_ref, mcw_ref, vcw_ref = take(3)
        cins = take(len(carry.in_arrays)) if carry else []
        outs = [take(4) for _ in range(ns)]
        cw_outs, (loss_out,) = take(4), take(1)
        couts = take(len(carry.out_shape)) if carry else []
        send, slots, cslots, send_sems, recv_sems, csend_sems, crecv_sems = take(7)
        cscr = list(it)
        x, y, c = _place()
        me = 4 * x + 2 * y + c
        send[...] = jnp.zeros_like(send)
        for k in range(ns):
            send[k:k + 1, 0:widths[k]] = g_refs[k][...]
        send[LOSS_ROW:LOSS_ROW + 1, 0:128] = loss_ref[...]
        slots[me] = send[...]
        cslots[me] = gcw_ref[...]
        cps = []
        for k in range(1, NDEV):
            peer = (x ^ ((k >> 2) & 1), y ^ ((k >> 1) & 1), c ^ (k & 1))
            cps.append(pltpu.make_async_remote_copy(
                src_ref=send, dst_ref=slots.at[me], send_sem=send_sems.at[k - 1], recv_sem=recv_sems.at[k - 1],
                device_id=peer, device_id_type=MESH))
            cps.append(pltpu.make_async_remote_copy(
                src_ref=gcw_ref, dst_ref=cslots.at[me], send_sem=csend_sems.at[k - 1],
                recv_sem=crecv_sems.at[k - 1], device_id=peer, device_id_type=MESH))
        for cp in cps:
            cp.start()
        if carry:
            carry.start(cins, couts, cscr)
        for cp in cps:
            cp.wait()
        if carry:
            carry.mid(cins, couts, cscr)
        tot = slots[0]
        ctot = cslots[0, me]
        for j in range(1, NDEV):
            tot = tot + slots[j]
            ctot = ctot + cslots[j, me]

        def step(g, w_ref, m_ref, v_ref, o):
            d, mn, vn = _adamw_math(w_ref[...], g, m_ref[...], v_ref[...])
            o[0][...], o[1][...], o[2][...], o[3][...] = g, d, mn, vn

        for k in range(ns):
            step(tot[k:k + 1, 0:widths[k]], w_refs[k], m_refs[k], v_refs[k], outs[k])
        step(ctot, wcw_ref, mcw_ref, vcw_ref, cw_outs)
        loss_out[...] = tot[LOSS_ROW:LOSS_ROW + 1, 0:128]
        if carry:
            carry.finish(cins, couts, cscr)

    args = [*gs, loss_row, gcw, *ws, *ms, *vs, wcw, mcw, vcw]
    out_shape = ([jax.ShapeDtypeStruct((1, n), F32) for n in widths for _ in range(4)]
                 + [jax.ShapeDtypeStruct((CWF, D), F32)] * 4 + [jax.ShapeDtypeStruct((1, 128), F32)])
    n_own = len(out_shape)
    res = pl.pallas_call(
        body, in_specs=[VMEM] * len(args) + (carry.in_specs if carry else []),
        out_specs=[VMEM] * n_own + (carry.out_specs if carry else []),
        out_shape=out_shape + (carry.out_shape if carry else []),
        scratch_shapes=[pltpu.VMEM((SROWS, D), F32), pltpu.VMEM((NDEV, SROWS, D), F32),
                        pltpu.VMEM((NDEV, NDEV, CWF, D), F32)]
                       + [pltpu.SemaphoreType.DMA((NDEV - 1,))] * 4 + (carry.scratch if carry else []),
        name="small_step")(*args, *(carry.in_arrays if carry else []))
    per = [res[4 * k:4 * k + 4] for k in range(ns)]
    return per, res[4 * ns:4 * ns + 4], res[n_own - 1], res[n_own:]


def _pack_cw(a):
    flat = a.reshape(a.shape[:-2] + (CK * HD,))
    pad = [(0, 0)] * (flat.ndim - 1) + [(0, CWF * D - CK * HD)]
    return jnp.pad(flat, pad).reshape(a.shape[:-2] + (CWF, D))


def _unpack_cw(v):
    return v.reshape(-1)[:CK * HD].reshape(1, CK, HD)


def kernel(x, ffn1_norm, ffn1_w_gate, ffn1_w_up, ffn1_w_down, mix_norm, w_in, q_norm, k_norm, conv_w, conv_b, conv_ln_g, conv_ln_b, w_out, ffn2_norm, ffn2_w_gate, ffn2_w_up, ffn2_w_down, loss_target, m_ffn1_norm, m_ffn1_w_gate, m_ffn1_w_up, m_ffn1_w_down, m_mix_norm, m_w_in, m_q_norm, m_k_norm, m_conv_w, m_conv_b, m_conv_ln_g, m_conv_ln_b, m_w_out, m_ffn2_norm, m_ffn2_w_gate, m_ffn2_w_up, m_ffn2_w_down, v_ffn1_norm, v_ffn1_w_gate, v_ffn1_w_up, v_ffn1_w_down, v_mix_norm, v_w_in, v_q_norm, v_k_norm, v_conv_w, v_conv_b, v_conv_ln_g, v_conv_ln_b, v_w_out, v_ffn2_norm, v_ffn2_w_gate, v_ffn2_w_up, v_ffn2_w_down):
    P = dict(ffn1_norm=ffn1_norm, ffn1_w_gate=ffn1_w_gate, ffn1_w_up=ffn1_w_up, ffn1_w_down=ffn1_w_down,
             mix_norm=mix_norm, w_in=w_in, q_norm=q_norm, k_norm=k_norm, conv_w=conv_w, conv_b=conv_b,
             conv_ln_g=conv_ln_g, conv_ln_b=conv_ln_b, w_out=w_out, ffn2_norm=ffn2_norm,
             ffn2_w_gate=ffn2_w_gate, ffn2_w_up=ffn2_w_up, ffn2_w_down=ffn2_w_down)
    M = dict(ffn1_norm=m_ffn1_norm, ffn1_w_gate=m_ffn1_w_gate, ffn1_w_up=m_ffn1_w_up, ffn1_w_down=m_ffn1_w_down,
             mix_norm=m_mix_norm, w_in=m_w_in, q_norm=m_q_norm, k_norm=m_k_norm, conv_w=m_conv_w, conv_b=m_conv_b,
             conv_ln_g=m_conv_ln_g, conv_ln_b=m_conv_ln_b, w_out=m_w_out, ffn2_norm=m_ffn2_norm,
             ffn2_w_gate=m_ffn2_w_gate, ffn2_w_up=m_ffn2_w_up, ffn2_w_down=m_ffn2_w_down)
    V = dict(ffn1_norm=v_ffn1_norm, ffn1_w_gate=v_ffn1_w_gate, ffn1_w_up=v_ffn1_w_up, ffn1_w_down=v_ffn1_w_down,
             mix_norm=v_mix_norm, w_in=v_w_in, q_norm=v_q_norm, k_norm=v_k_norm, conv_w=v_conv_w, conv_b=v_conv_b,
             conv_ln_g=v_conv_ln_g, conv_ln_b=v_conv_ln_b, w_out=v_w_out, ffn2_norm=v_ffn2_norm,
             ffn2_w_gate=v_ffn2_w_gate, ffn2_w_up=v_ffn2_w_up, ffn2_w_down=v_ffn2_w_down)
    order = ["ffn1_norm", "ffn1_w_gate", "ffn1_w_up", "ffn1_w_down", "mix_norm", "w_in", "q_norm", "k_norm",
             "conv_w", "conv_b", "conv_ln_g", "conv_ln_b", "w_out", "ffn2_norm", "ffn2_w_gate", "ffn2_w_up",
             "ffn2_w_down"]
    B, S, _ = x.shape
    T = B * S

    bigs = [("wg1", "ffn1_w_gate", True), ("wu1", "ffn1_w_up", True), ("wd1", "ffn1_w_down", False),
            ("win", "w_in", True), ("wout", "w_out", False),
            ("wg2", "ffn2_w_gate", True), ("wu2", "ffn2_w_up", True), ("wd2", "ffn2_w_down", False)]
    hm = lambda a, tr: jnp.transpose(a[0]) if tr else a[0]
    cw_pad = jnp.zeros((32, 128), F32).at[0:CK, 0:HD].set(conv_w[0])
    shard = {ln: (hm(P[pn], tr), BF16) for ln, pn, tr in bigs}
    gathered = _run_carry(_GatherCarry([shard["wg1"], shard["wu1"], (cw_pad, F32)]), "gather_first")
    W = {"wg1": gathered[0], "wu1": gathered[1]}
    cwg = gathered[2].reshape(NDEV, 32, 128)[:, 0:CK, 0:HD]
    W["conv_w"] = jnp.transpose(cwg, (1, 0, 2)).reshape(CK, DC)
    norms = {n: P[n] for n in SMALL_NAMES}
    comm = _Comm({"down_in": ({n: shard[n] for n in ("wd1", "win", "wout")}, 0.5),
                  "ffn2": ({n: shard[n] for n in ("wg2", "wu2", "wd2")}, 0.5)})

    loss_part, gx, _, small = _local_step(x.reshape(T, D), loss_target.reshape(T, D), norms, W, B, S, comm)

    G, Dl, Mn, Vn = {}, {}, {}, {}
    dcw = small["conv_w"].reshape(CK, NDEV, HD).transpose(1, 0, 2)
    loss_row = jnp.zeros((1, 128), F32).at[0, 0].set(loss_part)
    per, cw_outs, loss_out, got = _small_step(
        [small[n] for n in SMALL_NAMES], loss_row, _pack_cw(dcw),
        [P[n] for n in SMALL_NAMES], [M[n] for n in SMALL_NAMES], [V[n] for n in SMALL_NAMES],
        _pack_cw(P["conv_w"][0]), _pack_cw(M["conv_w"][0]), _pack_cw(V["conv_w"][0]), carry=comm.last)
    comm.reduce_done(comm.last, got)
    loss = loss_out[0, 0]
    for n, outs in zip(SMALL_NAMES, per):
        G[n], Dl[n], Mn[n], Vn[n] = outs
    G["conv_w"], Dl["conv_w"], Mn["conv_w"], Vn["conv_w"] = [_unpack_cw(o) for o in cw_outs]

    for ln, pn, tr in bigs:
        outs = _adamw_big(comm.reduced[ln], hm(P[pn], tr), hm(M[pn], tr), hm(V[pn], tr), "adamw_" + ln)
        G[pn], Dl[pn], Mn[pn], Vn[pn] = [(jnp.transpose(o) if tr else o)[None] for o in outs]

    return (loss, gx.reshape(B, S, D), *[G[n] for n in order], *[Dl[n] for n in order],
            *[Mn[n] for n in order], *[Vn[n] for n in order])
```

```python
import functools

import jax
import jax.numpy as jnp
from jax import lax
from jax.experimental import pallas as pl
from jax.experimental.pallas import tpu as pltpu

F32 = jnp.float32
BF16 = jnp.bfloat16

D = 1024
FF = 2816
HD = 64
DA = 512
DC = 512
DIN = 2560
CK = 31
BLK = 128
DILS = (1, 4, 16)
EPS = 1e-6
NDEV = 8
MESH = pl.DeviceIdType.MESH

LR, B1, B2, AEPS, WD, STEP = 0.001, 0.9, 0.999, 1e-08, 0.01, 10

NT = (((1,), (1,)), ((), ()))
TN = (((0,), (0,)), ((), ()))

VMEM_LIMIT = 56 * 1024 * 1024


def _cp(sem=None):
    return pltpu.CompilerParams(dimension_semantics=sem, vmem_limit_bytes=VMEM_LIMIT)


def _sigmoid(x):
    return 0.5 * (jnp.tanh(0.5 * x) + 1.0)


def _pallas(body, args, *, grid, in_specs, out_specs, out_shape, scratch_shapes, sem, name, carry=None):
    if carry is None:
        outs = pl.pallas_call(body, grid=grid, in_specs=in_specs, out_specs=out_specs, out_shape=out_shape,
                              scratch_shapes=scratch_shapes, compiler_params=_cp(sem), name=name)(*args)
        return outs, None
    n_in, n_out, n_scr = len(in_specs), len(out_shape), len(scratch_shapes)
    c_in, c_out = len(carry.in_arrays), len(carry.out_shape)

    def wrapped(*refs):
        ins, refs = refs[:n_in], refs[n_in:]
        cins, refs = refs[:c_in], refs[c_in:]
        outs, refs = refs[:n_out], refs[n_out:]
        couts, refs = refs[:c_out], refs[c_out:]
        scr, cscr = refs[:n_scr], refs[n_scr:]
        ids = [pl.program_id(a) for a in range(len(grid))]
        step = ids[0]
        for i, n in zip(ids[1:], grid[1:]):
            step = step * n + i
        steps = functools.reduce(lambda a, b: a * b, grid)

        @pl.when(step == 0)
        def _():
            carry.start(cins, couts, cscr)

        body(*ins, *outs, *scr)

        if hasattr(carry, "mid"):
            @pl.when(step == int(steps * carry.mid_at))
            def _():
                carry.mid(cins, couts, cscr)

        @pl.when(step == steps - 1)
        def _():
            carry.finish(cins, couts, cscr)

    outs = pl.pallas_call(
        wrapped, grid=grid, in_specs=list(in_specs) + carry.in_specs, out_specs=list(out_specs) + carry.out_specs,
        out_shape=list(out_shape) + carry.out_shape, scratch_shapes=list(scratch_shapes) + carry.scratch,
        compiler_params=_cp(("arbitrary",) * len(grid)), name=name)(*args, *carry.in_arrays)
    return outs[:n_out], outs[n_out:]


FC = 256


def _swiglu_parts(g, u):
    sg = _sigmoid(g)
    f2 = g * sg
    return ((f2 * u).astype(BF16), (u * (sg * (1.0 + g * (1.0 - sg)))).astype(BF16), f2.astype(BF16))


def _resident(shape):
    return pl.BlockSpec(shape, lambda *_: (0,) * len(shape), pipeline_mode=pl.Buffered(1))


def _mix_out_ffn_loss(h1, attn, conv, wout, gain, wg, wu, wd, target, name):
    T = h1.shape[0]
    tm = 256
    nt = T // tm

    def body(h1_ref, at_ref, cv_ref, wo_ref, gain_ref, wg_ref, wu_ref, wd_ref, t_ref,
             h2_ref, n_ref, a_ref, f1_ref, f2_ref, dout_ref, dyb_ref, sq_ref):
        xv = (h1_ref[...]
              + jnp.dot(at_ref[...], wo_ref[0:DA, :], preferred_element_type=F32)
              + jnp.dot(cv_ref[...], wo_ref[DA:D, :], preferred_element_type=F32))
        h2_ref[...] = xv
        r = lax.rsqrt(jnp.mean(xv * xv, axis=-1, keepdims=True) + EPS)
        n_ref[...] = (xv * r * gain_ref[...]).astype(BF16)
        for c in range(FF // FC):
            cols = slice(c * FC, (c + 1) * FC)
            nb = n_ref[...]
            g = lax.dot_general(nb, wg_ref[cols, :], NT, preferred_element_type=F32)
            u = lax.dot_general(nb, wu_ref[cols, :], NT, preferred_element_type=F32)
            a_ref[:, cols], f1_ref[:, cols], f2_ref[:, cols] = _swiglu_parts(g, u)
        e = h2_ref[...] + 0.5 * jnp.dot(a_ref[...], wd_ref[...], preferred_element_type=F32) - t_ref[...]
        dout = e * (1.0 / D)
        dout_ref[...] = dout
        dyb_ref[...] = (0.5 * dout).astype(BF16)
        sq_ref[...] = jnp.sum(e * e, axis=0, keepdims=True)[None]

    row = pl.BlockSpec((tm, D), lambda t: (t, 0))
    half = pl.BlockSpec((tm, DA), lambda t: (t, 0))
    wide = pl.BlockSpec((tm, FF), lambda t: (t, 0))
    outs, _ = _pallas(
        body, (h1, attn, conv, wout, gain, wg, wu, wd, target), grid=(nt,),
        in_specs=[row, half, half, _resident((D, D)), _resident((1, D)), _resident((FF, D)), _resident((FF, D)),
                  _resident((FF, D)), row],
        out_specs=[row, row, wide, wide, wide, row, row, pl.BlockSpec((1, 1, D), lambda t: (t, 0, 0))],
        out_shape=[jax.ShapeDtypeStruct((T, D), F32), jax.ShapeDtypeStruct((T, D), BF16)]
                  + [jax.ShapeDtypeStruct((T, FF), BF16)] * 3
                  + [jax.ShapeDtypeStruct((T, D), F32), jax.ShapeDtypeStruct((T, D), BF16),
                     jax.ShapeDtypeStruct((nt, 1, D), F32)],
        scratch_shapes=[], sem=("parallel",), name=name)
    return outs


def _ffn_gate_up(x, gain, wg, wu, name, carry=None):
    T = x.shape[0]
    tm = 512

    def body(x_ref, gain_ref, wg_ref, wu_ref, n_ref, a_ref, f1_ref, f2_ref):
        xv = x_ref[...]
        r = lax.rsqrt(jnp.mean(xv * xv, axis=-1, keepdims=True) + EPS)
        n_ref[...] = (xv * r * gain_ref[...]).astype(BF16)
        for c in range(FF // FC):
            cols = slice(c * FC, (c + 1) * FC)
            nb = n_ref[...]
            g = lax.dot_general(nb, wg_ref[cols, :], NT, preferred_element_type=F32)
            u = lax.dot_general(nb, wu_ref[cols, :], NT, preferred_element_type=F32)
            a_ref[:, cols], f1_ref[:, cols], f2_ref[:, cols] = _swiglu_parts(g, u)

    row = pl.BlockSpec((tm, D), lambda t: (t, 0))
    wide = pl.BlockSpec((tm, FF), lambda t: (t, 0))
    return _pallas(
        body, (x, gain, wg, wu), grid=(T // tm,),
        in_specs=[row, _resident((1, D)), _resident((FF, D)), _resident((FF, D))],
        out_specs=[row, wide, wide, wide],
        out_shape=[jax.ShapeDtypeStruct((T, D), BF16)] + [jax.ShapeDtypeStruct((T, FF), BF16)] * 3,
        scratch_shapes=[], sem=("parallel",), name=name, carry=carry)


def _ffn_down(x, a, wd, name, carry=None):
    T = x.shape[0]
    tm = 512

    def body(x_ref, a_ref, wd_ref, h_ref):
        h_ref[...] = x_ref[...] + 0.5 * jnp.dot(a_ref[...], wd_ref[...], preferred_element_type=F32)

    row = pl.BlockSpec((tm, D), lambda t: (t, 0))
    wide = pl.BlockSpec((tm, FF), lambda t: (t, 0))
    return _pallas(
        body, (x, a, wd), grid=(T // tm,), in_specs=[row, wide, _resident((FF, D))],
        out_specs=[row], out_shape=[jax.ShapeDtypeStruct((T, D), F32)],
        scratch_shapes=[], sem=("parallel",), name=name, carry=carry)


def _ffn_bwd_act(dyb, f1, f2, x, dout, gain, wg, wu, wd, name, carry=None):
    T = x.shape[0]
    tm = 256
    nt = T // tm

    def body(dy_ref, f1_ref, f2_ref, x_ref, dout_ref, gain_ref, wg_ref, wu_ref, wd_ref,
             dg_ref, du_ref, dx_ref, dgn_ref):
        for c in range(FF // FC):
            cols = slice(c * FC, (c + 1) * FC)
            da = lax.dot_general(dy_ref[...], wd_ref[cols, :], NT, preferred_element_type=F32)
            dg_ref[:, cols] = (da * f1_ref[:, cols].astype(F32)).astype(BF16)
            du_ref[:, cols] = (da * f2_ref[:, cols].astype(F32)).astype(BF16)
        dn = (jnp.dot(dg_ref[...], wg_ref[...], preferred_element_type=F32)
              + jnp.dot(du_ref[...], wu_ref[...], preferred_element_type=F32))
        dx, dgain = _rms_bwd_rows(dn, x_ref[...], gain_ref[...])
        dx_ref[...] = dout_ref[...] + dx
        dgn_ref[...] = dgain[None]

    row = pl.BlockSpec((tm, D), lambda t: (t, 0))
    wide = pl.BlockSpec((tm, FF), lambda t: (t, 0))
    return _pallas(
        body, (dyb, f1, f2, x, dout, gain, wg, wu, wd), grid=(nt,),
        in_specs=[row, wide, wide, row, row, _resident((1, D)), _resident((FF, D)), _resident((FF, D)),
                  _resident((FF, D))],
        out_specs=[wide, wide, row, pl.BlockSpec((1, 1, D), lambda t: (t, 0, 0))],
        out_shape=[jax.ShapeDtypeStruct((T, FF), BF16)] * 2
                  + [jax.ShapeDtypeStruct((T, D), F32), jax.ShapeDtypeStruct((nt, 1, D), F32)],
        scratch_shapes=[], sem=("parallel",), name=name, carry=carry)


def _ffn_bwd_w(lhs, rhs, name, carry=None):
    T = rhs.shape[0]
    tf = 256

    def body(l_ref, r_ref, dw_ref):
        dw_ref[...] = lax.dot_general(l_ref[...], r_ref[...], TN, preferred_element_type=F32).astype(BF16)

    (dw,), got = _pallas(
        body, (lhs, rhs), grid=(FF // tf,),
        in_specs=[pl.BlockSpec((T, tf), lambda f: (0, f)), _resident((T, D))],
        out_specs=[pl.BlockSpec((tf, D), lambda f: (f, 0))], out_shape=[jax.ShapeDtypeStruct((FF, D), BF16)],
        scratch_shapes=[], sem=("parallel",), name=name, carry=carry)
    return dw, got


def _rms_bwd_rows(dn, xv, gain):
    r = lax.rsqrt(jnp.mean(xv * xv, axis=-1, keepdims=True) + EPS)
    xhat = xv * r
    dxhat = dn * gain
    dx = r * (dxhat - xhat * jnp.mean(dxhat * xhat, axis=-1, keepdims=True))
    return dx, jnp.sum(dn * xhat, axis=0, keepdims=True)


def _mix_in(h, gain, win, carry=None):
    T = h.shape[0]
    tm = 512

    def body(h_ref, gain_ref, w_ref, u_ref, n_ref):
        xv = h_ref[...]
        r = lax.rsqrt(jnp.mean(xv * xv, axis=-1, keepdims=True) + EPS)
        nb = (xv * r * gain_ref[...]).astype(BF16)
        n_ref[...] = nb
        u_ref[...] = lax.dot_general(nb, w_ref[...], NT, preferred_element_type=F32).astype(BF16)

    row = pl.BlockSpec((tm, D), lambda t: (t, 0))
    return _pallas(
        body, (h, gain, win), grid=(T // tm,),
        in_specs=[row, _resident((1, D)), _resident((DIN, D))],
        out_specs=[pl.BlockSpec((tm, DIN), lambda t: (t, 0)), row],
        out_shape=[jax.ShapeDtypeStruct((T, DIN), BF16), jax.ShapeDtypeStruct((T, D), BF16)],
        scratch_shapes=[], sem=("parallel",), name="mix_in", carry=carry)


def _mix_out_bwd(dh, attn, conv, wout):
    T = dh.shape[0]
    tm = 512
    nt = T // tm

    def body(dh_ref, a_ref, c_ref, w_ref, da_ref, dc_ref, dw_ref, acc_scr):
        t = pl.program_id(0)

        @pl.when(t == 0)
        def _():
            acc_scr[...] = jnp.zeros_like(acc_scr)

        dhb = dh_ref[...].astype(BF16)
        dmix = lax.dot_general(dhb, w_ref[...], NT, preferred_element_type=F32)
        da_ref[...] = dmix[:, 0:DA].astype(BF16)
        dc_ref[...] = dmix[:, DA:D].astype(BF16)
        acc_scr[0:DA, :] += lax.dot_general(a_ref[...], dhb, TN, preferred_element_type=F32)
        acc_scr[DA:D, :] += lax.dot_general(c_ref[...], dhb, TN, preferred_element_type=F32)

        @pl.when(t == nt - 1)
        def _():
            dw_ref[...] = acc_scr[...].astype(BF16)

    row = pl.BlockSpec((tm, D), lambda t: (t, 0))
    half = pl.BlockSpec((tm, DA), lambda t: (t, 0))
    full = pl.BlockSpec((D, D), lambda t: (0, 0))
    return pl.pallas_call(
        body, grid=(nt,), in_specs=[row, half, half, full], out_specs=[half, half, full],
        out_shape=[jax.ShapeDtypeStruct((T, DA), BF16)] * 2 + [jax.ShapeDtypeStruct((D, D), BF16)],
        scratch_shapes=[pltpu.VMEM((D, D), F32)],
        compiler_params=_cp(("arbitrary",)), name="mix_out_bwd")(dh, attn, conv, wout)


def _mix_in_bwd(dparts, win, nb, h, dh, gain):
    T = h.shape[0]
    tm = 512
    nt = T // tm

    def body(d0, d1, d2, d3, d4, w_ref, n_ref, h_ref, dh_ref, gain_ref,
             dw_ref, dx_ref, dyb_ref, dg_ref, acc_scr):
        t = pl.program_id(0)

        @pl.when(t == 0)
        def _():
            acc_scr[...] = jnp.zeros_like(acc_scr)

        n = n_ref[...]
        dn = jnp.zeros((tm, D), F32)
        for i, d_ref in enumerate((d0, d1, d2, d3, d4)):
            dv = d_ref[...]
            dn = dn + jnp.dot(dv, w_ref[i * DA:(i + 1) * DA, :], preferred_element_type=F32)
            acc_scr[i * DA:(i + 1) * DA, :] += lax.dot_general(dv, n, TN, preferred_element_type=F32)
        dx, dgain = _rms_bwd_rows(dn, h_ref[...], gain_ref[...])
        tot = dh_ref[...] + dx
        dx_ref[...] = tot
        dyb_ref[...] = (0.5 * tot).astype(BF16)
        dg_ref[...] = dgain[None]

        @pl.when(t == nt - 1)
        def _():
            dw_ref[...] = acc_scr[...].astype(BF16)

    row = pl.BlockSpec((tm, D), lambda t: (t, 0))
    half = pl.BlockSpec((tm, DA), lambda t: (t, 0))
    full = pl.BlockSpec((DIN, D), lambda t: (0, 0))
    return pl.pallas_call(
        body, grid=(nt,),
        in_specs=[half] * 5 + [full, row, row, row, pl.BlockSpec((1, D), lambda t: (0, 0))],
        out_specs=[full, row, row, pl.BlockSpec((1, 1, D), lambda t: (t, 0, 0))],
        out_shape=[jax.ShapeDtypeStruct((DIN, D), BF16), jax.ShapeDtypeStruct((T, D), F32),
                   jax.ShapeDtypeStruct((T, D), BF16), jax.ShapeDtypeStruct((nt, 1, D), F32)],
        scratch_shapes=[pltpu.VMEM((DIN, D), F32)],
        compiler_params=_cp(("arbitrary",)), name="mix_in_bwd")(*dparts, win, nb, h, dh, gain)


def _head_masks():
    lane = lax.broadcasted_iota(jnp.int32, (1, 2 * HD), 1)
    m0 = lane < HD
    return m0, jnp.logical_not(m0)


def _stack_heads(v, m0, m1):
    z = jnp.zeros_like(v)
    return jnp.concatenate([jnp.where(m0, v, z), jnp.where(m1, v, z)], axis=0)


def _unstack_heads(v2, m0):
    return jnp.where(m0, v2[0:BLK], v2[BLK:2 * BLK])


def _head_sums(xv):
    ri = lax.broadcasted_iota(jnp.int32, (2 * HD, 2 * HD), 0)
    ci = lax.broadcasted_iota(jnp.int32, (2 * HD, 2 * HD), 1)
    ones = jnp.where((ri < HD) == (ci < HD), 1.0, 0.0).astype(BF16)
    hi = xv.astype(BF16)
    lo = (xv - hi.astype(F32)).astype(BF16)
    return (jnp.dot(hi, ones, preferred_element_type=F32) + jnp.dot(lo, ones, preferred_element_type=F32))


def _head_rms(xv):
    return lax.rsqrt(_head_sums(xv * xv) * (1.0 / HD) + EPS)


def _band_mask(first):
    qi = lax.broadcasted_iota(jnp.int32, (BLK, 2 * BLK), 0)
    ci = lax.broadcasted_iota(jnp.int32, (BLK, 2 * BLK), 1)
    band = (ci >= qi) & (ci <= qi + BLK)
    return band & ((ci >= BLK) | jnp.logical_not(first))


def _block_rows(j, d, seg):
    r, n = j // seg, j % seg
    start = r + (d * BLK) * n
    first = n == 0
    prev = jnp.where(first, start, start - d * BLK)
    return pl.ds(start, BLK, stride=d), pl.ds(prev, BLK, stride=d), first


def _block_keys(refs, cur, prev, first, single):
    if single:
        qi = lax.broadcasted_iota(jnp.int32, (BLK, BLK), 0)
        ci = lax.broadcasted_iota(jnp.int32, (BLK, BLK), 1)
        return [r[cur, :].astype(BF16) for r in refs], ci <= qi
    return ([jnp.concatenate([r[prev, :], r[cur, :]], axis=0).astype(BF16) for r in refs], _band_mask(first))


def _attn_fwd(u, qg2, kg2, B, S, carry=None):
    T = B * S
    NB = S // BLK
    scale = HD ** -0.5

    def body(q_ref, k_ref, v_ref, qg_ref, kg_ref, o_ref, lse_ref, qn, kn, vn, os_, ls_):
        m0, m1 = _head_masks()
        qv = q_ref[...].astype(F32)
        qn[...] = qv * _head_rms(qv) * (qg_ref[...] * scale)
        kv = k_ref[...].astype(F32)
        kn[...] = kv * _head_rms(kv) * kg_ref[...]
        vn[...] = v_ref[...].astype(F32)

        for i, d in enumerate(DILS):
            seg = NB // d

            def blk(j, c, i=i, d=d, seg=seg):
                cur, prev, first = _block_rows(j, d, seg)
                q2 = _stack_heads(qn[cur, :].astype(BF16), m0, m1)
                (kk, vv), mask = _block_keys((kn, vn), cur, prev, first, False)
                s = lax.dot_general(q2, kk, NT, preferred_element_type=F32)
                s = jnp.where(jnp.concatenate([mask, mask], axis=0), s, -1e30)
                mx = jnp.max(s, axis=-1, keepdims=True)
                p = jnp.exp(s - mx)
                l = jnp.sum(p, axis=-1, keepdims=True)
                o2 = jnp.dot((p * (1.0 / l)).astype(BF16), vv, preferred_element_type=F32)
                os_[i, cur, :] = _unstack_heads(o2, m0)
                ls_[i, cur, :] = _unstack_heads(mx + jnp.log(l), m0)
                return c

            lax.fori_loop(0, NB, blk, 0, unroll=8)

        def comb(c, carry):
            rows = pl.ds(pl.multiple_of(c * 256, 256), 256)
            l0, l1, l2 = ls_[0, rows, :], ls_[1, rows, :], ls_[2, rows, :]
            mx = jnp.maximum(jnp.maximum(l0, l1), l2)
            e0, e1, e2 = jnp.exp(l0 - mx), jnp.exp(l1 - mx), jnp.exp(l2 - mx)
            tot = e0 + e1 + e2
            inv = 1.0 / tot
            o = (e0 * os_[0, rows, :] + e1 * os_[1, rows, :] + e2 * os_[2, rows, :]) * inv
            o_ref[rows, :] = o.astype(BF16)
            lse_ref[rows, :] = mx + jnp.log(tot)
            return carry

        lax.fori_loop(0, S // 256, comb, 0)

    pair = 2 * HD
    blk_spec = lambda off: pl.BlockSpec((S, pair), lambda b, p, off=off: (b, off + p))
    gspec = pl.BlockSpec((1, pair), lambda b, p: (0, 0))
    return _pallas(
        body, (u, u, u, qg2, kg2), grid=(B, DA // pair),
        in_specs=[blk_spec(0), blk_spec(DA // pair), blk_spec(2 * DA // pair), gspec, gspec],
        out_specs=[blk_spec(0), blk_spec(0)],
        out_shape=[jax.ShapeDtypeStruct((T, DA), BF16), jax.ShapeDtypeStruct((T, DA), F32)],
        scratch_shapes=[pltpu.VMEM((S, pair), F32)] * 3 + [pltpu.VMEM((3, S, pair), F32)] * 2,
        sem=("parallel", "parallel"), name="attn_fwd", carry=carry)


def _attn_bwd(u, attn, dattn, lse, qg2, kg2, B, S, carry=None):
    T = B * S
    NB = S // BLK
    scale = HD ** -0.5
    pair = 2 * HD

    def body(q_ref, k_ref, v_ref, o_ref, do_ref, lse_ref, qg_ref, kg_ref,
             dq_ref, dk_ref, dv_ref, dgn_ref,
             qn, kn, vn, don, ldl, accq, acck, accv, rq, rk):
        m0, m1 = _head_masks()
        lane = lax.broadcasted_iota(jnp.int32, (1, pair), 1)
        qv = q_ref[...].astype(F32)
        rq[...] = _head_rms(qv)
        qn[...] = qv * rq[...] * (qg_ref[...] * scale)
        kv = k_ref[...].astype(F32)
        rk[...] = _head_rms(kv)
        kn[...] = kv * rk[...] * kg_ref[...]
        vn[...] = v_ref[...].astype(F32)
        dov = do_ref[...].astype(F32)
        don[...] = dov
        ldl[...] = jnp.where((lane % HD) < HD // 2, lse_ref[...], _head_sums(dov * o_ref[...].astype(F32)))
        accq[...] = jnp.zeros_like(accq)
        acck[...] = jnp.zeros_like(acck)
        accv[...] = jnp.zeros_like(accv)

        for i, d in enumerate(DILS):
            seg = NB // d

            def blk(j, c, d=d, seg=seg):
                cur, prev, first = _block_rows(j, d, seg)
                q2 = _stack_heads(qn[cur, :].astype(BF16), m0, m1)
                do2 = _stack_heads(don[cur, :].astype(BF16), m0, m1)
                (kk, vv), mask = _block_keys((kn, vn), cur, prev, first, seg == 1)
                ldv = ldl[cur, :]
                lse2 = jnp.concatenate([ldv[:, 0:1], ldv[:, HD:HD + 1]], axis=0)
                dl2 = jnp.concatenate([ldv[:, HD // 2:HD // 2 + 1], ldv[:, HD + HD // 2:HD + HD // 2 + 1]], axis=0)
                s = lax.dot_general(q2, kk, NT, preferred_element_type=F32)
                p = jnp.where(jnp.concatenate([mask, mask], axis=0), jnp.exp(s - lse2), 0.0)
                dp = lax.dot_general(do2, vv, NT, preferred_element_type=F32)
                ds = (p * (dp - dl2)).astype(BF16)
                dq_acc = _unstack_heads(jnp.dot(ds, kk, preferred_element_type=F32), m0)
                dk_acc = lax.dot_general(ds, q2, TN, preferred_element_type=F32)
                dv_acc = lax.dot_general(p.astype(BF16), do2, TN, preferred_element_type=F32)
                accq[cur, :] += dq_acc
                if seg == 1:
                    acck[cur, :] += dk_acc
                    accv[cur, :] += dv_acc
                else:
                    acck[prev, :] += dk_acc[0:BLK]
                    acck[cur, :] += dk_acc[BLK:2 * BLK]
                    accv[prev, :] += dv_acc[0:BLK]
                    accv[cur, :] += dv_acc[BLK:2 * BLK]
                return c

            lax.fori_loop(0, NB, blk, 0, unroll=8)

        def norm_bwd(x_ref, r_ref, dn, gain):
            r = r_ref[...]
            xhat = x_ref[...].astype(F32) * r
            dxhat = dn * gain
            dx = r * (dxhat - xhat * (_head_sums(dxhat * xhat) * (1.0 / HD)))
            return dx, jnp.sum(dn * xhat, axis=0, keepdims=True)

        dq, dgq = norm_bwd(q_ref, rq, accq[...], qg_ref[...] * scale)
        dk, dgk = norm_bwd(k_ref, rk, acck[...], kg_ref[...])
        dq_ref[...] = dq.astype(BF16)
        dk_ref[...] = dk.astype(BF16)
        dv_ref[...] = accv[...].astype(BF16)
        dgn_ref[...] = jnp.concatenate([dgq * scale, dgk, jnp.zeros((6, pair), F32)], axis=0)[None]

    blk_spec = lambda off: pl.BlockSpec((S, pair), lambda b, p, off=off: (b, off + p))
    gspec = pl.BlockSpec((1, pair), lambda b, p: (0, 0))
    np_ = DA // pair
    return _pallas(
        body, (u, u, u, attn, dattn, lse, qg2, kg2), grid=(B, np_),
        in_specs=[blk_spec(0), blk_spec(np_), blk_spec(2 * np_), blk_spec(0), blk_spec(0), blk_spec(0),
                  gspec, gspec],
        out_specs=[blk_spec(0), blk_spec(0), blk_spec(0),
                   pl.BlockSpec((1, 8, pair), lambda b, p: (b * np_ + p, 0, 0))],
        out_shape=[jax.ShapeDtypeStruct((T, DA), BF16)] * 3 + [jax.ShapeDtypeStruct((B * np_, 8, pair), F32)],
        scratch_shapes=[pltpu.VMEM((S, pair), F32)] * 10,
        sem=("parallel", "parallel"), name="attn_bwd", carry=carry)


CT = 32
CPAD = 32


def _shifted(win, offsets):
    rolled, out = {}, {}
    n = win.shape[0]
    for o in offsets:
        sub = o % 8
        if sub not in rolled:
            rolled[sub] = win if sub == 0 else pltpu.roll(win, n - sub, 0)
        out[o] = rolled[sub][o - sub:o - sub + CT, :]
    return out


def _ln_fwd(y, g, b):
    mu = jnp.mean(y, axis=-1, keepdims=True)
    yc = y - mu
    rstd = lax.rsqrt(jnp.mean(yc * yc, axis=-1, keepdims=True) + EPS)
    xhat = yc * rstd
    return xhat, rstd, xhat * g + b


def _fill_glu(ca_ref, cg_ref, glu, S):
    glu[pl.ds(0, CPAD), :] = jnp.zeros((CPAD, DC), F32)

    def fill(i, c):
        rows = pl.ds(pl.multiple_of(i * 256, 256), 256)
        a = ca_ref[rows, :].astype(F32)
        gt = cg_ref[rows, :].astype(F32)
        glu[pl.ds(pl.multiple_of(CPAD + i * 256, CT), 256), :] = a * _sigmoid(gt)
        return c

    lax.fori_loop(0, S // 256, fill, 0)


def _conv_fwd(u, cw, cb, lg, lb, B, S):
    T = B * S

    def body(ca_ref, cg_ref, w_ref, b_ref, lg_ref, lb_ref, o_ref, y_ref, glu):
        _fill_glu(ca_ref, cg_ref, glu, S)

        def step(i, c):
            t0 = pl.multiple_of(i * CT, CT)
            win = glu[pl.ds(t0, 2 * CT), :]
            acc = jnp.zeros((CT, DC), F32) + b_ref[...]
            taps = _shifted(win, [k + 2 for k in range(CK)])
            for k in range(CK):
                acc = acc + taps[k + 2] * w_ref[k:k + 1, :]
            y_ref[pl.ds(t0, CT), :] = acc
            _, _, z = _ln_fwd(acc, lg_ref[...], lb_ref[...])
            o_ref[pl.ds(t0, CT), :] = (z * _sigmoid(z)).astype(BF16)
            return c

        lax.fori_loop(0, S // CT, step, 0, unroll=2)

    vec = pl.BlockSpec((1, DC), lambda b: (0, 0))
    return pl.pallas_call(
        body, grid=(B,),
        in_specs=[pl.BlockSpec((S, DC), lambda b: (b, 3)), pl.BlockSpec((S, DC), lambda b: (b, 4)),
                  pl.BlockSpec((CT, DC), lambda b: (0, 0)), vec, vec, vec],
        out_specs=[pl.BlockSpec((S, DC), lambda b: (b, 0))] * 2,
        out_shape=[jax.ShapeDtypeStruct((T, DC), BF16), jax.ShapeDtypeStruct((T, DC), F32)],
        scratch_shapes=[pltpu.VMEM((CPAD + S, DC), F32)],
        compiler_params=_cp(("parallel",)), name="conv_fwd")(u, u, cw, cb, lg, lb)


def _conv_bwd(u, y, dconv, cw, lg, lb, B, S):
    T = B * S

    def body(ca_ref, cg_ref, y_ref, dc_ref, w_ref, lg_ref, lb_ref,
             dca_ref, dcg_ref, dw_ref, ds_ref, glu, dyp, dwacc):
        _fill_glu(ca_ref, cg_ref, glu, S)
        dyp[pl.ds(S, CPAD), :] = jnp.zeros((CPAD, DC), F32)
        lgv, lbv = lg_ref[...], lb_ref[...]

        def sum8(v):
            return functools.reduce(jnp.add, [v[r:r + 8] for r in range(0, v.shape[0], 8)])

        P1 = 4 * CT

        def p1(i, carry):
            sb, sg, sl = carry
            t0 = pl.multiple_of(i * P1, P1)
            xhat, rstd, z = _ln_fwd(y_ref[pl.ds(t0, P1), :], lgv, lbv)
            sz = _sigmoid(z)
            dz = dc_ref[pl.ds(t0, P1), :].astype(F32) * (sz * (1.0 + z * (1.0 - sz)))
            dxhat = dz * lgv
            dy = rstd * (dxhat - jnp.mean(dxhat, axis=-1, keepdims=True)
                         - xhat * jnp.mean(dxhat * xhat, axis=-1, keepdims=True))
            dyp[pl.ds(t0, P1), :] = dy
            return sb + sum8(dy), sg + sum8(dz * xhat), sl + sum8(dz)

        z8 = jnp.zeros((8, DC), F32)
        sb, sg, sl = lax.fori_loop(0, S // P1, p1, (z8, z8, z8))
        rs = lambda v: jnp.sum(v, axis=0, keepdims=True)
        ds_ref[...] = jnp.concatenate([rs(sb), rs(sg), rs(sl), jnp.zeros((5, DC), F32)], axis=0)[None]

        def p2(i, c):
            t0 = pl.multiple_of(i * CT, CT)
            win = dyp[pl.ds(t0, 2 * CT), :]
            acc = jnp.zeros((CT, DC), F32)
            taps = _shifted(win, [30 - k for k in range(CK)])
            for k in range(CK):
                acc = acc + taps[30 - k] * w_ref[k:k + 1, :]
            a = ca_ref[pl.ds(t0, CT), :].astype(F32)
            sgt = _sigmoid(cg_ref[pl.ds(t0, CT), :].astype(F32))
            dca_ref[pl.ds(t0, CT), :] = (acc * sgt).astype(BF16)
            dcg_ref[pl.ds(t0, CT), :] = (acc * a * sgt * (1.0 - sgt)).astype(BF16)
            return c

        lax.fori_loop(0, S // CT, p2, 0)

        dwacc[...] = jnp.zeros_like(dwacc)

        def p3(i, c):
            t0 = pl.multiple_of(i * CT, CT)
            win = glu[pl.ds(t0, 2 * CT), :]
            dy = dyp[pl.ds(t0, CT), :]
            for k in range(CK):
                dwacc[k] += sum8(dy * win[k + 2:k + 2 + CT, :])
            return c

        lax.fori_loop(0, S // CT, p3, 0)
        dw_ref[...] = jnp.sum(dwacc[...], axis=1)[None]

    vec = pl.BlockSpec((1, DC), lambda b: (0, 0))
    seq = pl.BlockSpec((S, DC), lambda b: (b, 0))
    return pl.pallas_call(
        body, grid=(B,),
        in_specs=[pl.BlockSpec((S, DC), lambda b: (b, 3)), pl.BlockSpec((S, DC), lambda b: (b, 4)),
                  seq, seq, pl.BlockSpec((CT, DC), lambda b: (0, 0)), vec, vec],
        out_specs=[seq, seq, pl.BlockSpec((1, CT, DC), lambda b: (b, 0, 0)),
                   pl.BlockSpec((1, 8, DC), lambda b: (b, 0, 0))],
        out_shape=[jax.ShapeDtypeStruct((T, DC), BF16)] * 2
                  + [jax.ShapeDtypeStruct((B, CT, DC), F32), jax.ShapeDtypeStruct((B, 8, DC), F32)],
        scratch_shapes=[pltpu.VMEM((CPAD + S, DC), F32), pltpu.VMEM((S + CPAD, DC), F32),
                        pltpu.VMEM((CT, 8, DC), F32)],
        compiler_params=_cp(("parallel",)), name="conv_bwd")(u, u, y, dconv, cw, lg, lb)


def _local_step(x, target, norms, W, B, S, comm=None):
    qg2 = jnp.concatenate([norms["q_norm"], norms["q_norm"]], axis=1)
    kg2 = jnp.concatenate([norms["k_norm"], norms["k_norm"]], axis=1)
    cw = jnp.concatenate([W["conv_w"], jnp.zeros((1, DC), F32)], axis=0)

    W = dict(W)
    (n1, act1, f1a, f2a), got = _ffn_gate_up(x, norms["ffn1_norm"], W["wg1"], W["wu1"], "ffn1_gate_up",
                                           carry=comm.gathers["down_in"] if comm else None)
    if comm:
        W.update(comm.gathered("down_in", got))
    (h1,), _ = _ffn_down(x, act1, W["wd1"], "ffn1_down")
    (u, n2), _ = _mix_in(h1, norms["mix_norm"], W["win"])
    (attn, lse), got = _attn_fwd(u, qg2, kg2, B, S, carry=comm.gathers["ffn2"] if comm else None)
    if comm:
        W = dict(W, **comm.gathered("ffn2", got))
    conv, y = _conv_fwd(u, cw, norms["conv_b"], norms["conv_ln_g"], norms["conv_ln_b"], B, S)
    h2, n3, act2, f1b, f2b, dout, dyb, sq = _mix_out_ffn_loss(h1, attn, conv, W["wout"], norms["ffn2_norm"],
                                                              W["wg2"], W["wu2"], W["wd2"], target, "ffn2_fwd")
    loss = (0.5 / D) * jnp.sum(sq)

    (dg2, du2, dh2, dgn_ffn2), _ = _ffn_bwd_act(dyb, f1b, f2b, h2, dout, norms["ffn2_norm"],
                                               W["wg2"], W["wu2"], W["wd2"], "ffn2_bwd_act")
    dwd2, _ = _ffn_bwd_w(act2, dyb, "ffn2_bwd_wd")
    dwg2, _ = _ffn_bwd_w(dg2, n3, "ffn2_bwd_wg")
    dwu2, _ = _ffn_bwd_w(du2, n3, "ffn2_bwd_wu")
    dattn, dconv, dwout = _mix_out_bwd(dh2, attn, conv, W["wout"])
    carry = comm.reduce_start("ffn2", {"wg2": dwg2, "wu2": dwu2, "wd2": dwd2, "wout": dwout}) if comm else None
    (dq, dk, dv, dgn_qk), got = _attn_bwd(u, attn, dattn, lse, qg2, kg2, B, S, carry=carry)
    if comm:
        comm.reduce_done(carry, got)
    dca, dcg, dcw, dcs = _conv_bwd(u, y, dconv, cw, norms["conv_ln_g"], norms["conv_ln_b"], B, S)
    dwin, dh1, dyb1, dgn_mix = _mix_in_bwd((dq, dk, dv, dca, dcg), W["win"], n2, h1, dh2, norms["mix_norm"])
    (dg1, du1, gx, dgn_ffn1), _ = _ffn_bwd_act(dyb1, f1a, f2a, x, dh1, norms["ffn1_norm"],
                                              W["wg1"], W["wu1"], W["wd1"], "ffn1_bwd_act")
    carry = comm.reduce_start("win", {"win": dwin}) if comm else None
    dwd1, got = _ffn_bwd_w(act1, dyb1, "ffn1_bwd_wd", carry=carry)
    if comm:
        comm.reduce_done(carry, got)
        carry = comm.reduce_start("wd1", {"wd1": dwd1})
    dwg1, got = _ffn_bwd_w(dg1, n1, "ffn1_bwd_wg", carry=carry)
    if comm:
        comm.reduce_done(carry, got)
        carry = comm.reduce_start("wg1", {"wg1": dwg1})
    dwu1, got = _ffn_bwd_w(du1, n1, "ffn1_bwd_wu", carry=carry)
    if comm:
        comm.reduce_done(carry, got)
        comm.last = comm.reduce_start("wu1", {"wu1": dwu1})

    qk = jnp.sum(dgn_qk, axis=0)
    cs = jnp.sum(dcs, axis=0)
    small = {
        "ffn1_norm": jnp.sum(dgn_ffn1, axis=0),
        "mix_norm": jnp.sum(dgn_mix, axis=0),
        "q_norm": qk[0:1, 0:HD] + qk[0:1, HD:2 * HD],
        "k_norm": qk[1:2, 0:HD] + qk[1:2, HD:2 * HD],
        "conv_w": jnp.sum(dcw, axis=0)[0:CK],
        "conv_b": cs[0:1],
        "conv_ln_g": cs[1:2],
        "conv_ln_b": cs[2:3],
        "ffn2_norm": jnp.sum(dgn_ffn2, axis=0),
    }
    big = {"wg1": dwg1, "wu1": dwu1, "wd1": dwd1, "win": dwin, "wout": dwout,
           "wg2": dwg2, "wu2": dwu2, "wd2": dwd2}
    return loss, gx, big, small


HBM = pl.BlockSpec(memory_space=pltpu.HBM)
VMEM = pl.BlockSpec(memory_space=pltpu.VMEM)


def _place():
    return lax.axis_index("x"), lax.axis_index("y"), lax.axis_index("c")


class _GatherCarry:
    def __init__(self, shards, mid_at=0.5):
        nt = len(shards)
        self.mid_at = mid_at
        self.shards = shards
        self.in_arrays = [s for s, _ in shards]
        self.in_specs = [VMEM] * nt
        self.out_shape = [jax.ShapeDtypeStruct((NDEV * s.shape[0], s.shape[1]), dt) for s, dt in shards]
        self.out_specs = [HBM] * nt
        self.scratch = ([pltpu.VMEM(s.shape, dt) for s, dt in shards]
                        + [pltpu.SemaphoreType.DMA((nt, 7)), pltpu.SemaphoreType.DMA((nt, 7)),
                           pltpu.SemaphoreType.DMA((nt,))])

    def _copies(self, outs, scr):
        nt = len(self.shards)
        stages = scr[:nt]
        send_sems, recv_sems, local_sems = scr[nt:]
        x, y, c = _place()
        me, sibling = (x, y, c), (x, y, 1 - c)
        xn, yn, diag = (1 - x, y, c), (x, 1 - y, c), (1 - x, 1 - y, c)
        via = (x ^ c, y ^ (1 - c), c)
        onto = (x ^ (1 - c), y ^ c, c)

        def rows(t, px, py, pc):
            r = self.shards[t][0].shape[0]
            return outs[t].at[pl.ds((4 * px + 2 * py + pc) * r, r), :]

        def copy(t, k, block, to, src=None):
            return pltpu.make_async_remote_copy(
                src_ref=rows(t, *block) if src is None else src, dst_ref=rows(t, *block),
                send_sem=send_sems.at[t, k], recv_sem=recv_sems.at[t, k],
                device_id=to, device_id_type=MESH)

        sib = lambda b: (b[0], b[1], 1 - c)
        return dict(
            local=[pltpu.make_async_copy(stages[t], rows(t, *me), local_sems.at[t]) for t in range(nt)],
            own=[[copy(t, 0, me, sibling, src=stages[t]), copy(t, 1, me, xn, src=stages[t]),
                  copy(t, 2, me, yn, src=stages[t])] for t in range(nt)],
            relay=[copy(t, 3, via, onto) for t in range(nt)],
            down=[[copy(t, 4, xn, sibling), copy(t, 5, yn, sibling)] for t in range(nt)],
            down_diag=[copy(t, 6, diag, sibling) for t in range(nt)],
            got_xy=[[copy(t, 1, xn, me), copy(t, 2, yn, me)] for t in range(nt)],
            got_diag=[copy(t, 3, diag, me) for t in range(nt)],
            got_sib=[[copy(t, 0, sibling, me), copy(t, 4, sib(xn), me), copy(t, 5, sib(yn), me),
                      copy(t, 6, sib(diag), me)] for t in range(nt)])

    def start(self, ins, outs, scr):
        cps = self._copies(outs, scr)
        for t, (_, dt) in enumerate(self.shards):
            scr[t][...] = ins[t][...].astype(dt)
            for cp in [cps["local"][t]] + cps["own"][t]:
                cp.start()

    def mid(self, ins, outs, scr):
        cps = self._copies(outs, scr)
        for t in range(len(self.shards)):
            for cp in cps["got_xy"][t]:
                cp.wait_recv()
            for cp in [cps["relay"][t]] + cps["down"][t]:
                cp.start()

    def finish(self, ins, outs, scr):
        cps = self._copies(outs, scr)
        for t in range(len(self.shards)):
            cps["got_diag"][t].wait_recv()
            cps["down_diag"][t].start()
        for t in range(len(self.shards)):
            for cp in cps["got_sib"][t]:
                cp.wait_recv()
            for cp in cps["own"][t] + [cps["relay"][t]] + cps["down"][t] + [cps["down_diag"][t]]:
                cp.wait_send()
            cps["local"][t].wait()


def _run_carry(carry, name):
    def body(*refs):
        n_in, n_out = len(carry.in_arrays), len(carry.out_shape)
        ins, outs, scr = refs[:n_in], refs[n_in:n_in + n_out], refs[n_in + n_out:]
        carry.start(ins, outs, scr)
        if hasattr(carry, "mid"):
            carry.mid(ins, outs, scr)
        carry.finish(ins, outs, scr)

    return pl.pallas_call(
        body, in_specs=carry.in_specs, out_specs=carry.out_specs, out_shape=carry.out_shape,
        scratch_shapes=carry.scratch, compiler_params=pltpu.CompilerParams(vmem_limit_bytes=VMEM_LIMIT),
        name=name)(*carry.in_arrays)


def _sibling_reduce(grads, name):
    nt = len(grads)
    g4 = [g.reshape(4, 2, g.shape[0] // NDEV, g.shape[1]) for g in grads]

    def body(*refs):
        ins, outs = refs[:nt], refs[nt:2 * nt]
        recv, own = refs[2 * nt:3 * nt], refs[3 * nt:4 * nt]
        send_sems, recv_sems, load_sems, store_sems = refs[4 * nt:]
        x, y, c = _place()
        sends = [pltpu.make_async_remote_copy(
            src_ref=ins[t].at[:, 1 - c], dst_ref=recv[t], send_sem=send_sems.at[t], recv_sem=recv_sems.at[t],
            device_id=(x, y, 1 - c), device_id_type=MESH) for t in range(nt)]
        loads = [pltpu.make_async_copy(ins[t].at[:, c], own[t], load_sems.at[t]) for t in range(nt)]
        stores = [pltpu.make_async_copy(own[t], outs[t], store_sems.at[t]) for t in range(nt)]
        for cp in sends + loads:
            cp.start()
        for t in range(nt):
            loads[t].wait()
            sends[t].wait_recv()
            for q in range(4):
                own[t][q] = (own[t][q].astype(F32) + recv[t][q].astype(F32)).astype(BF16)
            stores[t].start()
        for t in range(nt):
            sends[t].wait_send()
            stores[t].wait()

    blocks = [(4,) + g.shape[2:] for g in g4]
    return pl.pallas_call(
        body, in_specs=[HBM] * nt, out_specs=[HBM] * nt,
        out_shape=[jax.ShapeDtypeStruct(b, BF16) for b in blocks],
        scratch_shapes=[pltpu.VMEM(b, BF16) for b in blocks] * 2 + [pltpu.SemaphoreType.DMA((nt,))] * 4,
        compiler_params=pltpu.CompilerParams(vmem_limit_bytes=VMEM_LIMIT), name=name)(*g4)


class _ExchangeCarry:
    def __init__(self, names, parts, mid_at=0.5):
        nt = len(parts)
        self.mid_at = mid_at
        self.names = names
        self.in_arrays = list(parts)
        self.in_specs = [HBM] * nt
        self.out_shape = [jax.ShapeDtypeStruct((3,) + p.shape[1:], BF16) for p in parts]
        self.out_specs = [HBM] * nt
        self.scratch = ([pltpu.VMEM(p.shape[1:], BF16) for p in parts] * 2
                        + [pltpu.SemaphoreType.DMA((nt, 3)), pltpu.SemaphoreType.DMA((nt, 3)),
                           pltpu.SemaphoreType.DMA((nt,)), pltpu.SemaphoreType.DMA((nt,))])

    def _copies(self, ins, outs, scr):
        nt = len(ins)
        relayed, mine = scr[:nt], scr[nt:2 * nt]
        send_sems, recv_sems, local_sems, load_sems = scr[2 * nt:]
        x, y, c = _place()
        q = lambda cx, cy: 2 * cx + cy
        near, far = (x ^ c, y ^ (1 - c)), (x ^ (1 - c), y ^ c)

        def remote(t, k, src, dst, chip):
            return pltpu.make_async_remote_copy(
                src_ref=src, dst_ref=dst, send_sem=send_sems.at[t, k], recv_sem=recv_sems.at[t, k],
                device_id=(*chip, c), device_id_type=MESH)

        return dict(
            keep=[pltpu.make_async_copy(ins[t].at[q(x, y)], outs[t].at[0], local_sems.at[t]) for t in range(nt)],
            load=[pltpu.make_async_copy(ins[t].at[q(*far)], mine[t], load_sems.at[t]) for t in range(nt)],
            direct=[remote(t, 0, ins[t].at[q(*near)], outs[t].at[1], near) for t in range(nt)],
            relay=[remote(t, 1, ins[t].at[q(1 - x, 1 - y)], relayed[t], near) for t in range(nt)],
            merged=[remote(t, 2, mine[t], outs[t].at[2], far) for t in range(nt)],
            relayed=relayed, mine=mine)

    def start(self, ins, outs, scr):
        cps = self._copies(ins, outs, scr)
        for t in range(len(ins)):
            for kind in ("keep", "load", "direct", "relay"):
                cps[kind][t].start()

    def mid(self, ins, outs, scr):
        cps = self._copies(ins, outs, scr)
        for t in range(len(ins)):
            cps["load"][t].wait()
            cps["relay"][t].wait_recv()
            cps["mine"][t][...] = (cps["mine"][t][...].astype(F32) + cps["relayed"][t][...].astype(F32)).astype(BF16)
            cps["merged"][t].start()

    def finish(self, ins, outs, scr):
        cps = self._copies(ins, outs, scr)
        for t in range(len(ins)):
            cps["direct"][t].wait()
            cps["relay"][t].wait_send()
            cps["merged"][t].wait()
            cps["keep"][t].wait()


class _Comm:
    def __init__(self, groups):
        self.names = {tag: list(g) for tag, (g, _) in groups.items()}
        self.gathers = {tag: _GatherCarry(list(g.values()), mid_at) for tag, (g, mid_at) in groups.items()}
        self.reduced = {}

    def gathered(self, tag, outs):
        return dict(zip(self.names[tag], outs))

    def reduce_start(self, tag, grads, mid_at=0.5):
        names = list(grads)
        parts = _sibling_reduce([grads[n] for n in names], "sibling_reduce_" + tag)
        return _ExchangeCarry(names, parts, mid_at)

    def reduce_done(self, carry, outs):
        self.reduced.update(zip(carry.names, outs))


def _adamw_math(w, g, m, v):
    m = B1 * m + (1.0 - B1) * g
    v = B2 * v + (1.0 - B2) * (g * g)
    m_hat = m / (1.0 - B1 ** STEP)
    v_hat = v / (1.0 - B2 ** STEP)
    delta = -LR * (m_hat / (jnp.sqrt(v_hat) + AEPS) + WD * w)
    return delta, m, v


def _adamw_big(recv, w, m, v, name):
    def body(r_ref, w_ref, m_ref, v_ref, g_ref, d_ref, mo_ref, vo_ref):
        g = r_ref[0].astype(F32)
        for q in range(1, 3):
            g = g + r_ref[q].astype(F32)
        d, mn, vn = _adamw_math(w_ref[...], g, m_ref[...], v_ref[...])
        g_ref[...] = g
        d_ref[...] = d
        mo_ref[...] = mn
        vo_ref[...] = vn

    rows, n = w.shape
    tr = rows // 2
    row = pl.BlockSpec((tr, n), lambda t: (t, 0))
    return pl.pallas_call(
        body, grid=(2,), in_specs=[pl.BlockSpec((3, tr, n), lambda t: (0, t, 0)), row, row, row],
        out_specs=[row] * 4, out_shape=[jax.ShapeDtypeStruct(w.shape, F32)] * 4,
        compiler_params=_cp(("parallel",)), name=name)(recv, w, m, v)


SMALL_NAMES = ("ffn1_norm", "mix_norm", "ffn2_norm", "conv_b", "conv_ln_g", "conv_ln_b", "q_norm", "k_norm")
SROWS = 16
LOSS_ROW = len(SMALL_NAMES)
CWF = 2


def _small_step(gs, loss_row, gcw, ws, ms, vs, wcw, mcw, vcw, carry=None):
    ns = len(SMALL_NAMES)
    widths = [g.shape[1] for g in gs]

    def body(*refs):
        it = iter(refs)
        take = lambda n: [next(it) for _ in range(n)]
        g_refs, (loss_ref, gcw_ref) = take(ns), take(2)
        w_refs, m_refs, v_refs = take(ns), take(ns), take(ns)
        wcw_ref, mcw_ref, vcw_ref = take(3)
        cins = take(len(carry.in_arrays)) if carry else []
        outs = [take(4) for _ in range(ns)]
        cw_outs, (loss_out,) = take(4), take(1)
        couts = take(len(carry.out_shape)) if carry else []
        send, slots, cslots, send_sems, recv_sems, csend_sems, crecv_sems = take(7)
        cscr = list(it)
        x, y, c = _place()
        me = 4 * x + 2 * y + c
        send[...] = jnp.zeros_like(send)
        for k in range(ns):
            send[k:k + 1, 0:widths[k]] = g_refs[k][...]
        send[LOSS_ROW:LOSS_ROW + 1, 0:128] = loss_ref[...]
        slots[me] = send[...]
        cslots[me] = gcw_ref[...]
        cps = []
        for k in range(1, NDEV):
            peer = (x ^ ((k >> 2) & 1), y ^ ((k >> 1) & 1), c ^ (k & 1))
            cps.append(pltpu.make_async_remote_copy(
                src_ref=send, dst_ref=slots.at[me], send_sem=send_sems.at[k - 1], recv_sem=recv_sems.at[k - 1],
                device_id=peer, device_id_type=MESH))
            cps.append(pltpu.make_async_remote_copy(
                src_ref=gcw_ref, dst_ref=cslots.at[me], send_sem=csend_sems.at[k - 1],
                recv_sem=crecv_sems.at[k - 1], device_id=peer, device_id_type=MESH))
        for cp in cps:
            cp.start()
        if carry:
            carry.start(cins, couts, cscr)
        for cp in cps:
            cp.wait()
        if carry:
            carry.mid(cins, couts, cscr)
        tot = slots[0]
        ctot = cslots[0, me]
        for j in range(1, NDEV):
            tot = tot + slots[j]
            ctot = ctot + cslots[j, me]

        def step(g, w_ref, m_ref, v_ref, o):
            d, mn, vn = _adamw_math(w_ref[...], g, m_ref[...], v_ref[...])
            o[0][...], o[1][...], o[2][...], o[3][...] = g, d, mn, vn

        for k in range(ns):
            step(tot[k:k + 1, 0:widths[k]], w_refs[k], m_refs[k], v_refs[k], outs[k])
        step(ctot, wcw_ref, mcw_ref, vcw_ref, cw_outs)
        loss_out[...] = tot[LOSS_ROW:LOSS_ROW + 1, 0:128]
        if carry:
            carry.finish(cins, couts, cscr)

    args = [*gs, loss_row, gcw, *ws, *ms, *vs, wcw, mcw, vcw]
    out_shape = ([jax.ShapeDtypeStruct((1, n), F32) for n in widths for _ in range(4)]
                 + [jax.ShapeDtypeStruct((CWF, D), F32)] * 4 + [jax.ShapeDtypeStruct((1, 128), F32)])
    n_own = len(out_shape)
    res = pl.pallas_call(
        body, in_specs=[VMEM] * len(args) + (carry.in_specs if carry else []),
        out_specs=[VMEM] * n_own + (carry.out_specs if carry else []),
        out_shape=out_shape + (carry.out_shape if carry else []),
        scratch_shapes=[pltpu.VMEM((SROWS, D), F32), pltpu.VMEM((NDEV, SROWS, D), F32),
                        pltpu.VMEM((NDEV, NDEV, CWF, D), F32)]
                       + [pltpu.SemaphoreType.DMA((NDEV - 1,))] * 4 + (carry.scratch if carry else []),
        name="small_step")(*args, *(carry.in_arrays if carry else []))
    per = [res[4 * k:4 * k + 4] for k in range(ns)]
    return per, res[4 * ns:4 * ns + 4], res[n_own - 1], res[n_own:]


def _pack_cw(a):
    flat = a.reshape(a.shape[:-2] + (CK * HD,))
    pad = [(0, 0)] * (flat.ndim - 1) + [(0, CWF * D - CK * HD)]
    return jnp.pad(flat, pad).reshape(a.shape[:-2] + (CWF, D))


def _unpack_cw(v):
    return v.reshape(-1)[:CK * HD].reshape(1, CK, HD)


def kernel(x, ffn1_norm, ffn1_w_gate, ffn1_w_up, ffn1_w_down, mix_norm, w_in, q_norm, k_norm, conv_w, conv_b, conv_ln_g, conv_ln_b, w_out, ffn2_norm, ffn2_w_gate, ffn2_w_up, ffn2_w_down, loss_target, m_ffn1_norm, m_ffn1_w_gate, m_ffn1_w_up, m_ffn1_w_down, m_mix_norm, m_w_in, m_q_norm, m_k_norm, m_conv_w, m_conv_b, m_conv_ln_g, m_conv_ln_b, m_w_out, m_ffn2_norm, m_ffn2_w_gate, m_ffn2_w_up, m_ffn2_w_down, v_ffn1_norm, v_ffn1_w_gate, v_ffn1_w_up, v_ffn1_w_down, v_mix_norm, v_w_in, v_q_norm, v_k_norm, v_conv_w, v_conv_b, v_conv_ln_g, v_conv_ln_b, v_w_out, v_ffn2_norm, v_ffn2_w_gate, v_ffn2_w_up, v_ffn2_w_down):
    P = dict(ffn1_norm=ffn1_norm, ffn1_w_gate=ffn1_w_gate, ffn1_w_up=ffn1_w_up, ffn1_w_down=ffn1_w_down,
             mix_norm=mix_norm, w_in=w_in, q_norm=q_norm, k_norm=k_norm, conv_w=conv_w, conv_b=conv_b,
             conv_ln_g=conv_ln_g, conv_ln_b=conv_ln_b, w_out=w_out, ffn2_norm=ffn2_norm,
             ffn2_w_gate=ffn2_w_gate, ffn2_w_up=ffn2_w_up, ffn2_w_down=ffn2_w_down)
    M = dict(ffn1_norm=m_ffn1_norm, ffn1_w_gate=m_ffn1_w_gate, ffn1_w_up=m_ffn1_w_up, ffn1_w_down=m_ffn1_w_down,
             mix_norm=m_mix_norm, w_in=m_w_in, q_norm=m_q_norm, k_norm=m_k_norm, conv_w=m_conv_w, conv_b=m_conv_b,
             conv_ln_g=m_conv_ln_g, conv_ln_b=m_conv_ln_b, w_out=m_w_out, ffn2_norm=m_ffn2_norm,
             ffn2_w_gate=m_ffn2_w_gate, ffn2_w_up=m_ffn2_w_up, ffn2_w_down=m_ffn2_w_down)
    V = dict(ffn1_norm=v_ffn1_norm, ffn1_w_gate=v_ffn1_w_gate, ffn1_w_up=v_ffn1_w_up, ffn1_w_down=v_ffn1_w_down,
             mix_norm=v_mix_norm, w_in=v_w_in, q_norm=v_q_norm, k_norm=v_k_norm, conv_w=v_conv_w, conv_b=v_conv_b,
             conv_ln_g=v_conv_ln_g, conv_ln_b=v_conv_ln_b, w_out=v_w_out, ffn2_norm=v_ffn2_norm,
             ffn2_w_gate=v_ffn2_w_gate, ffn2_w_up=v_ffn2_w_up, ffn2_w_down=v_ffn2_w_down)
    order = ["ffn1_norm", "ffn1_w_gate", "ffn1_w_up", "ffn1_w_down", "mix_norm", "w_in", "q_norm", "k_norm",
             "conv_w", "conv_b", "conv_ln_g", "conv_ln_b", "w_out", "ffn2_norm", "ffn2_w_gate", "ffn2_w_up",
             "ffn2_w_down"]
    B, S, _ = x.shape
    T = B * S

    bigs = [("wg1", "ffn1_w_gate", True), ("wu1", "ffn1_w_up", True), ("wd1", "ffn1_w_down", False),
            ("win", "w_in", True), ("wout", "w_out", False),
            ("wg2", "ffn2_w_gate", True), ("wu2", "ffn2_w_up", True), ("wd2", "ffn2_w_down", False)]
    hm = lambda a, tr: jnp.transpose(a[0]) if tr else a[0]
    cw_pad = jnp.zeros((32, 128), F32).at[0:CK, 0:HD].set(conv_w[0])
    shard = {ln: (hm(P[pn], tr), BF16) for ln, pn, tr in bigs}
    gathered = _run_carry(_GatherCarry([shard["wg1"], shard["wu1"], (cw_pad, F32)]), "gather_first")
    W = {"wg1": gathered[0], "wu1": gathered[1]}
    cwg = gathered[2].reshape(NDEV, 32, 128)[:, 0:CK, 0:HD]
    W["conv_w"] = jnp.transpose(cwg, (1, 0, 2)).reshape(CK, DC)
    norms = {n: P[n] for n in SMALL_NAMES}
    comm = _Comm({"down_in": ({n: shard[n] for n in ("wd1", "win", "wout")}, 0.5),
                  "ffn2": ({n: shard[n] for n in ("wg2", "wu2", "wd2")}, 0.5)})

    loss_part, gx, _, small = _local_step(x.reshape(T, D), loss_target.reshape(T, D), norms, W, B, S, comm)

    G, Dl, Mn, Vn = {}, {}, {}, {}
    dcw = small["conv_w"].reshape(CK, NDEV, HD).transpose(1, 0, 2)
    loss_row = jnp.zeros((1, 128), F32).at[0, 0].set(loss_part)
    per, cw_outs, loss_out, got = _small_step(
        [small[n] for n in SMALL_NAMES], loss_row, _pack_cw(dcw),
        [P[n] for n in SMALL_NAMES], [M[n] for n in SMALL_NAMES], [V[n] for n in SMALL_NAMES],
        _pack_cw(P["conv_w"][0]), _pack_cw(M["conv_w"][0]), _pack_cw(V["conv_w"][0]), carry=comm.last)
    comm.reduce_done(comm.last, got)
    loss = loss_out[0, 0]
    for n, outs in zip(SMALL_NAMES, per):
        G[n], Dl[n], Mn[n], Vn[n] = outs
    G["conv_w"], Dl["conv_w"], Mn["conv_w"], Vn["conv_w"] = [_unpack_cw(o) for o in cw_outs]

    for ln, pn, tr in bigs:
        outs = _adamw_big(comm.reduced[ln], hm(P[pn], tr), hm(M[pn], tr), hm(V[pn], tr), "adamw_" + ln)
        G[pn], Dl[pn], Mn[pn], Vn[pn] = [(jnp.transpose(o) if tr else o)[None] for o in outs]

    return (loss, gx.reshape(B, S, D), *[G[n] for n in order], *[Dl[n] for n in order],
            *[Mn[n] for n in order], *[Vn[n] for n in order])
```

```python
import functools

import jax
import jax.numpy as jnp
from jax import lax
from jax.experimental import pallas as pl
from jax.experimental.pallas import tpu as pltpu

F32 = jnp.float32
BF16 = jnp.bfloat16

D = 1024
FF = 2816
HD = 64
DA = 512
DC = 512
DIN = 2560
CK = 31
BLK = 128
DILS = (1, 4, 16)
EPS = 1e-6
NDEV = 8
MESH = pl.DeviceIdType.MESH

LR, B1, B2, AEPS, WD, STEP = 0.001, 0.9, 0.999, 1e-08, 0.01, 10

NT = (((1,), (1,)), ((), ()))
TN = (((0,), (0,)), ((), ()))

VMEM_LIMIT = 60 * 1024 * 1024


def _cp(sem=None):
    return pltpu.CompilerParams(dimension_semantics=sem, vmem_limit_bytes=VMEM_LIMIT)


def _sigmoid(x):
    return 0.5 * (jnp.tanh(0.5 * x) + 1.0)


def _pallas(body, args, *, grid, in_specs, out_specs, out_shape, scratch_shapes, sem, name, carry=None):
    if carry is None:
        outs = pl.pallas_call(body, grid=grid, in_specs=in_specs, out_specs=out_specs, out_shape=out_shape,
                              scratch_shapes=scratch_shapes, compiler_params=_cp(sem), name=name)(*args)
        return outs, None
    n_in, n_out, n_scr = len(in_specs), len(out_shape), len(scratch_shapes)
    c_in, c_out = len(carry.in_arrays), len(carry.out_shape)

    def wrapped(*refs):
        ins, refs = refs[:n_in], refs[n_in:]
        cins, refs = refs[:c_in], refs[c_in:]
        outs, refs = refs[:n_out], refs[n_out:]
        couts, refs = refs[:c_out], refs[c_out:]
        scr, cscr = refs[:n_scr], refs[n_scr:]
        ids = [pl.program_id(a) for a in range(len(grid))]
        step = ids[0]
        for i, n in zip(ids[1:], grid[1:]):
            step = step * n + i
        steps = functools.reduce(lambda a, b: a * b, grid)

        @pl.when(step == 0)
        def _():
            carry.start(cins, couts, cscr)

        body(*ins, *outs, *scr)

        if hasattr(carry, "mid"):
            @pl.when(step == int(steps * carry.mid_at))
            def _():
                carry.mid(cins, couts, cscr)

        @pl.when(step == steps - 1)
        def _():
            carry.finish(cins, couts, cscr)

    outs = pl.pallas_call(
        wrapped, grid=grid, in_specs=list(in_specs) + carry.in_specs, out_specs=list(out_specs) + carry.out_specs,
        out_shape=list(out_shape) + carry.out_shape, scratch_shapes=list(scratch_shapes) + carry.scratch,
        compiler_params=_cp(("arbitrary",) * len(grid)), name=name)(*args, *carry.in_arrays)
    return outs[:n_out], outs[n_out:]


FC = 256


def _resident(shape):
    return pl.BlockSpec(shape, lambda *_: (0,) * len(shape), pipeline_mode=pl.Buffered(1))


def _mix_out_ffn_loss(h1, attn, conv, wout, gain, wg, wu, wd, target, name):
    T = h1.shape[0]
    tm = 512
    nt = T // tm

    def body(h1_ref, at_ref, cv_ref, wo_ref, gain_ref, wg_ref, wu_ref, wd_ref, t_ref,
             h2_ref, n_ref, g_ref, u_ref, dout_ref, dyb_ref, sq_ref, a_hbm, a_scr, a_sem):
        t = pl.program_id(0)
        a_out = lambda i: pltpu.make_async_copy(a_scr, a_hbm.at[pl.ds(pl.multiple_of(i * tm, tm), tm), :], a_sem)

        @pl.when(t > 0)
        def _():
            a_out(t - 1).wait()

        xv = (h1_ref[...]
              + jnp.dot(at_ref[...], wo_ref[0:DA, :], preferred_element_type=F32)
              + jnp.dot(cv_ref[...], wo_ref[DA:D, :], preferred_element_type=F32))
        h2_ref[...] = xv
        r = lax.rsqrt(jnp.mean(xv * xv, axis=-1, keepdims=True) + EPS)
        n_ref[...] = (xv * r * gain_ref[...]).astype(BF16)
        for c in range(FF // FC):
            cols = slice(c * FC, (c + 1) * FC)
            nb = n_ref[...]
            g = lax.dot_general(nb, wg_ref[cols, :], NT, preferred_element_type=F32)
            u = lax.dot_general(nb, wu_ref[cols, :], NT, preferred_element_type=F32)
            g_ref[:, cols] = g.astype(BF16)
            u_ref[:, cols] = u.astype(BF16)
            a_scr[:, cols] = (g * _sigmoid(g) * u).astype(BF16)
        a_out(t).start()
        e = h2_ref[...] + 0.5 * jnp.dot(a_scr[...], wd_ref[...], preferred_element_type=F32) - t_ref[...]
        dout = e * (1.0 / D)
        dout_ref[...] = dout
        dyb_ref[...] = (0.5 * dout).astype(BF16)
        sq_ref[...] = jnp.sum(e * e, axis=0, keepdims=True)[None]

        @pl.when(t == nt - 1)
        def _():
            a_out(t).wait()

    row = pl.BlockSpec((tm, D), lambda t: (t, 0))
    half = pl.BlockSpec((tm, DA), lambda t: (t, 0))
    wide = pl.BlockSpec((tm, FF), lambda t: (t, 0))
    outs, _ = _pallas(
        body, (h1, attn, conv, wout, gain, wg, wu, wd, target), grid=(nt,),
        in_specs=[row, half, half, _resident((D, D)), _resident((1, D)), _resident((FF, D)), _resident((FF, D)),
                  _resident((FF, D)), row],
        out_specs=[row, row, wide, wide, row, row, pl.BlockSpec((1, 1, D), lambda t: (t, 0, 0)), HBM],
        out_shape=[jax.ShapeDtypeStruct((T, D), F32), jax.ShapeDtypeStruct((T, D), BF16)]
                  + [jax.ShapeDtypeStruct((T, FF), BF16)] * 2
                  + [jax.ShapeDtypeStruct((T, D), F32), jax.ShapeDtypeStruct((T, D), BF16),
                     jax.ShapeDtypeStruct((nt, 1, D), F32), jax.ShapeDtypeStruct((T, FF), BF16)],
        scratch_shapes=[pltpu.VMEM((tm, FF), BF16), pltpu.SemaphoreType.DMA(())],
        sem=("arbitrary",), name=name)
    return outs


def _ffn_gate_up(x, gain, wg, wu, name, carry=None):
    T = x.shape[0]
    tm = 512

    def body(x_ref, gain_ref, wg_ref, wu_ref, n_ref, g_ref, u_ref, a_ref):
        xv = x_ref[...]
        r = lax.rsqrt(jnp.mean(xv * xv, axis=-1, keepdims=True) + EPS)
        n_ref[...] = (xv * r * gain_ref[...]).astype(BF16)
        for c in range(FF // FC):
            cols = slice(c * FC, (c + 1) * FC)
            nb = n_ref[...]
            g = lax.dot_general(nb, wg_ref[cols, :], NT, preferred_element_type=F32)
            u = lax.dot_general(nb, wu_ref[cols, :], NT, preferred_element_type=F32)
            g_ref[:, cols] = g.astype(BF16)
            u_ref[:, cols] = u.astype(BF16)
            a_ref[:, cols] = (g * _sigmoid(g) * u).astype(BF16)

    row = pl.BlockSpec((tm, D), lambda t: (t, 0))
    wide = pl.BlockSpec((tm, FF), lambda t: (t, 0))
    return _pallas(
        body, (x, gain, wg, wu), grid=(T // tm,),
        in_specs=[row, _resident((1, D)), _resident((FF, D)), _resident((FF, D))],
        out_specs=[row, wide, wide, wide],
        out_shape=[jax.ShapeDtypeStruct((T, D), BF16)] + [jax.ShapeDtypeStruct((T, FF), BF16)] * 3,
        scratch_shapes=[], sem=("parallel",), name=name, carry=carry)


def _ffn_down_mix_in(x, a, wd, gain, win, name):
    T = x.shape[0]
    tm = 512

    def body(x_ref, a_ref, wd_ref, gain_ref, win_ref, h_ref, u_ref, n_ref):
        hv = x_ref[...] + 0.5 * jnp.dot(a_ref[...], wd_ref[...], preferred_element_type=F32)
        h_ref[...] = hv
        r = lax.rsqrt(jnp.mean(hv * hv, axis=-1, keepdims=True) + EPS)
        n_ref[...] = (hv * r * gain_ref[...]).astype(BF16)
        u_ref[...] = lax.dot_general(n_ref[...], win_ref[...], NT, preferred_element_type=F32).astype(BF16)

    row = pl.BlockSpec((tm, D), lambda t: (t, 0))
    wide = pl.BlockSpec((tm, FF), lambda t: (t, 0))
    outs, _ = _pallas(
        body, (x, a, wd, gain, win), grid=(T // tm,),
        in_specs=[row, wide, _resident((FF, D)), _resident((1, D)), _resident((DIN, D))],
        out_specs=[row, pl.BlockSpec((tm, DIN), lambda t: (t, 0)), row],
        out_shape=[jax.ShapeDtypeStruct((T, D), F32), jax.ShapeDtypeStruct((T, DIN), BF16),
                   jax.ShapeDtypeStruct((T, D), BF16)],
        scratch_shapes=[], sem=("parallel",), name=name)
    return outs


def _ffn_bwd_act(dyb, g, u, x, dout, gain, wg, wu, wd, name, carry=None):
    T = x.shape[0]
    tm = 256
    nt = T // tm

    def body(dy_ref, g_ref, u_ref, x_ref, dout_ref, gain_ref, wg_ref, wu_ref, wd_ref,
             dg_ref, du_ref, dx_ref, dgn_ref):
        for c in range(FF // FC):
            cols = slice(c * FC, (c + 1) * FC)
            da = lax.dot_general(dy_ref[...], wd_ref[cols, :], NT, preferred_element_type=F32)
            gv = g_ref[:, cols].astype(F32)
            uv = u_ref[:, cols].astype(F32)
            sg = _sigmoid(gv)
            dg_ref[:, cols] = (da * uv * (sg * (1.0 + gv * (1.0 - sg)))).astype(BF16)
            du_ref[:, cols] = (da * (gv * sg)).astype(BF16)
        dn = (jnp.dot(dg_ref[...], wg_ref[...], preferred_element_type=F32)
              + jnp.dot(du_ref[...], wu_ref[...], preferred_element_type=F32))
        dx, dgain = _rms_bwd_rows(dn, x_ref[...], gain_ref[...])
        dx_ref[...] = dout_ref[...] + dx
        dgn_ref[...] = dgain[None]

    row = pl.BlockSpec((tm, D), lambda t: (t, 0))
    wide = pl.BlockSpec((tm, FF), lambda t: (t, 0))
    return _pallas(
        body, (dyb, g, u, x, dout, gain, wg, wu, wd), grid=(nt,),
        in_specs=[row, wide, wide, row, row, _resident((1, D)), _resident((FF, D)), _resident((FF, D)),
                  _resident((FF, D))],
        out_specs=[wide, wide, row, pl.BlockSpec((1, 1, D), lambda t: (t, 0, 0))],
        out_shape=[jax.ShapeDtypeStruct((T, FF), BF16)] * 2
                  + [jax.ShapeDtypeStruct((T, D), F32), jax.ShapeDtypeStruct((nt, 1, D), F32)],
        scratch_shapes=[], sem=("parallel",), name=name, carry=carry)


def _ffn_bwd_w(lhs, rhs, name, carry=None):
    T = rhs.shape[0]
    tf = 256

    def body(l_ref, r_ref, dw_ref):
        dw_ref[...] = lax.dot_general(l_ref[...], r_ref[...], TN, preferred_element_type=F32).astype(BF16)

    (dw,), got = _pallas(
        body, (lhs, rhs), grid=(FF // tf,),
        in_specs=[pl.BlockSpec((T, tf), lambda f: (0, f)), _resident((T, D))],
        out_specs=[pl.BlockSpec((tf, D), lambda f: (f, 0))], out_shape=[jax.ShapeDtypeStruct((FF, D), BF16)],
        scratch_shapes=[], sem=("parallel",), name=name, carry=carry)
    return dw, got


def _rms_bwd_rows(dn, xv, gain):
    r = lax.rsqrt(jnp.mean(xv * xv, axis=-1, keepdims=True) + EPS)
    xhat = xv * r
    dxhat = dn * gain
    dx = r * (dxhat - xhat * jnp.mean(dxhat * xhat, axis=-1, keepdims=True))
    return dx, jnp.sum(dn * xhat, axis=0, keepdims=True)


def _mix_out_bwd(dh, attn, conv, wout):
    T = dh.shape[0]
    tm = 512
    nt = T // tm

    def body(dh_ref, a_ref, c_ref, w_ref, da_ref, dc_ref, dw_ref, acc_scr):
        t = pl.program_id(0)

        @pl.when(t == 0)
        def _():
            acc_scr[...] = jnp.zeros_like(acc_scr)

        dhb = dh_ref[...].astype(BF16)
        dmix = lax.dot_general(dhb, w_ref[...], NT, preferred_element_type=F32)
        da_ref[...] = dmix[:, 0:DA].astype(BF16)
        dc_ref[...] = dmix[:, DA:D].astype(BF16)
        acc_scr[0:DA, :] += lax.dot_general(a_ref[...], dhb, TN, preferred_element_type=F32)
        acc_scr[DA:D, :] += lax.dot_general(c_ref[...], dhb, TN, preferred_element_type=F32)

        @pl.when(t == nt - 1)
        def _():
            dw_ref[...] = acc_scr[...].astype(BF16)

    row = pl.BlockSpec((tm, D), lambda t: (t, 0))
    half = pl.BlockSpec((tm, DA), lambda t: (t, 0))
    full = pl.BlockSpec((D, D), lambda t: (0, 0))
    return pl.pallas_call(
        body, grid=(nt,), in_specs=[row, half, half, full], out_specs=[half, half, full],
        out_shape=[jax.ShapeDtypeStruct((T, DA), BF16)] * 2 + [jax.ShapeDtypeStruct((D, D), BF16)],
        scratch_shapes=[pltpu.VMEM((D, D), F32)],
        compiler_params=_cp(("arbitrary",)), name="mix_out_bwd")(dh, attn, conv, wout)


def _mix_in_bwd(dparts, win, nb, h, dh, gain):
    T = h.shape[0]
    tm = 512
    nt = T // tm

    def body(d0, d1, d2, d3, d4, w_ref, n_ref, h_ref, dh_ref, gain_ref,
             dw_ref, dx_ref, dyb_ref, dg_ref, acc_scr):
        t = pl.program_id(0)

        @pl.when(t == 0)
        def _():
            acc_scr[...] = jnp.zeros_like(acc_scr)

        n = n_ref[...]
        dn = jnp.zeros((tm, D), F32)
        for i, d_ref in enumerate((d0, d1, d2, d3, d4)):
            dv = d_ref[...]
            dn = dn + jnp.dot(dv, w_ref[i * DA:(i + 1) * DA, :], preferred_element_type=F32)
            acc_scr[i * DA:(i + 1) * DA, :] += lax.dot_general(dv, n, TN, preferred_element_type=F32)
        dx, dgain = _rms_bwd_rows(dn, h_ref[...], gain_ref[...])
        tot = dh_ref[...] + dx
        dx_ref[...] = tot
        dyb_ref[...] = (0.5 * tot).astype(BF16)
        dg_ref[...] = dgain[None]

        @pl.when(t == nt - 1)
        def _():
            dw_ref[...] = acc_scr[...].astype(BF16)

    row = pl.BlockSpec((tm, D), lambda t: (t, 0))
    half = pl.BlockSpec((tm, DA), lambda t: (t, 0))
    full = pl.BlockSpec((DIN, D), lambda t: (0, 0))
    return pl.pallas_call(
        body, grid=(nt,),
        in_specs=[half] * 5 + [full, row, row, row, pl.BlockSpec((1, D), lambda t: (0, 0))],
        out_specs=[full, row, row, pl.BlockSpec((1, 1, D), lambda t: (t, 0, 0))],
        out_shape=[jax.ShapeDtypeStruct((DIN, D), BF16), jax.ShapeDtypeStruct((T, D), F32),
                   jax.ShapeDtypeStruct((T, D), BF16), jax.ShapeDtypeStruct((nt, 1, D), F32)],
        scratch_shapes=[pltpu.VMEM((DIN, D), F32)],
        compiler_params=_cp(("arbitrary",)), name="mix_in_bwd")(*dparts, win, nb, h, dh, gain)


def _head_masks():
    lane = lax.broadcasted_iota(jnp.int32, (1, 2 * HD), 1)
    m0 = lane < HD
    return m0, jnp.logical_not(m0)


def _stack_heads(v, m0, m1):
    z = jnp.zeros_like(v)
    return jnp.concatenate([jnp.where(m0, v, z), jnp.where(m1, v, z)], axis=0)


def _unstack_heads(v2, m0):
    return jnp.where(m0, v2[0:BLK], v2[BLK:2 * BLK])


def _head_sums(xv):
    ri = lax.broadcasted_iota(jnp.int32, (2 * HD, 2 * HD), 0)
    ci = lax.broadcasted_iota(jnp.int32, (2 * HD, 2 * HD), 1)
    ones = jnp.where((ri < HD) == (ci < HD), 1.0, 0.0).astype(BF16)
    hi = xv.astype(BF16)
    lo = (xv - hi.astype(F32)).astype(BF16)
    return (jnp.dot(hi, ones, preferred_element_type=F32) + jnp.dot(lo, ones, preferred_element_type=F32))


def _head_rms(xv):
    return lax.rsqrt(_head_sums(xv * xv) * (1.0 / HD) + EPS)


def _band_mask(first):
    qi = lax.broadcasted_iota(jnp.int32, (BLK, 2 * BLK), 0)
    ci = lax.broadcasted_iota(jnp.int32, (BLK, 2 * BLK), 1)
    band = (ci >= qi) & (ci <= qi + BLK)
    return band & ((ci >= BLK) | jnp.logical_not(first))


def _block_rows(j, d, seg):
    r, n = j // seg, j % seg
    start = r + (d * BLK) * n
    first = n == 0
    prev = jnp.where(first, start, start - d * BLK)
    return pl.ds(start, BLK, stride=d), pl.ds(prev, BLK, stride=d), first


def _block_keys(refs, cur, prev, first, single):
    if single:
        qi = lax.broadcasted_iota(jnp.int32, (BLK, BLK), 0)
        ci = lax.broadcasted_iota(jnp.int32, (BLK, BLK), 1)
        return [r[cur, :].astype(BF16) for r in refs], ci <= qi
    return ([jnp.concatenate([r[prev, :], r[cur, :]], axis=0).astype(BF16) for r in refs], _band_mask(first))


def _attn_fwd(u, qg2, kg2, B, S, carry=None):
    T = B * S
    NB = S // BLK
    scale = HD ** -0.5

    def body(q_ref, k_ref, v_ref, qg_ref, kg_ref, o_ref, lse_ref, qn, kn, vn, os_, ls_):
        m0, m1 = _head_masks()
        qv = q_ref[...].astype(F32)
        qn[...] = qv * _head_rms(qv) * (qg_ref[...] * scale)
        kv = k_ref[...].astype(F32)
        kn[...] = kv * _head_rms(kv) * kg_ref[...]
        vn[...] = v_ref[...].astype(F32)

        for i, d in enumerate(DILS):
            seg = NB // d

            def blk(j, c, i=i, d=d, seg=seg):
                cur, prev, first = _block_rows(j, d, seg)
                q2 = _stack_heads(qn[cur, :].astype(BF16), m0, m1)
                (kk, vv), mask = _block_keys((kn, vn), cur, prev, first, False)
                s = lax.dot_general(q2, kk, NT, preferred_element_type=F32)
                s = jnp.where(jnp.concatenate([mask, mask], axis=0), s, -1e30)
                mx = jnp.max(s, axis=-1, keepdims=True)
                p = jnp.exp(s - mx)
                l = jnp.sum(p, axis=-1, keepdims=True)
                o2 = jnp.dot((p * (1.0 / l)).astype(BF16), vv, preferred_element_type=F32)
                os_[i, cur, :] = _unstack_heads(o2, m0)
                ls_[i, cur, :] = _unstack_heads(mx + jnp.log(l), m0)
                return c

            lax.fori_loop(0, NB, blk, 0, unroll=8)

        def comb(c, carry):
            rows = pl.ds(pl.multiple_of(c * 256, 256), 256)
            l0, l1, l2 = ls_[0, rows, :], ls_[1, rows, :], ls_[2, rows, :]
            mx = jnp.maximum(jnp.maximum(l0, l1), l2)
            e0, e1, e2 = jnp.exp(l0 - mx), jnp.exp(l1 - mx), jnp.exp(l2 - mx)
            tot = e0 + e1 + e2
            inv = 1.0 / tot
            o = (e0 * os_[0, rows, :] + e1 * os_[1, rows, :] + e2 * os_[2, rows, :]) * inv
            o_ref[rows, :] = o.astype(BF16)
            lse_ref[rows, :] = mx + jnp.log(tot)
            return carry

        lax.fori_loop(0, S // 256, comb, 0)

    pair = 2 * HD
    blk_spec = lambda off: pl.BlockSpec((S, pair), lambda b, p, off=off: (b, off + p))
    gspec = pl.BlockSpec((1, pair), lambda b, p: (0, 0))
    return _pallas(
        body, (u, u, u, qg2, kg2), grid=(B, DA // pair),
        in_specs=[blk_spec(0), blk_spec(DA // pair), blk_spec(2 * DA // pair), gspec, gspec],
        out_specs=[blk_spec(0), blk_spec(0)],
        out_shape=[jax.ShapeDtypeStruct((T, DA), BF16), jax.ShapeDtypeStruct((T, DA), F32)],
        scratch_shapes=[pltpu.VMEM((S, pair), F32)] * 3 + [pltpu.VMEM((3, S, pair), F32)] * 2,
        sem=("parallel", "parallel"), name="attn_fwd", carry=carry)


def _attn_bwd(u, attn, dattn, lse, qg2, kg2, B, S, carry=None):
    T = B * S
    NB = S // BLK
    scale = HD ** -0.5
    pair = 2 * HD

    def body(q_ref, k_ref, v_ref, o_ref, do_ref, lse_ref, qg_ref, kg_ref,
             dq_ref, dk_ref, dv_ref, dgn_ref,
             qn, kn, vn, don, ldl, accq, acck, accv, rq, rk):
        m0, m1 = _head_masks()
        lane = lax.broadcasted_iota(jnp.int32, (1, pair), 1)
        qv = q_ref[...].astype(F32)
        rq[...] = _head_rms(qv)
        qn[...] = qv * rq[...] * (qg_ref[...] * scale)
        kv = k_ref[...].astype(F32)
        rk[...] = _head_rms(kv)
        kn[...] = kv * rk[...] * kg_ref[...]
        vn[...] = v_ref[...].astype(F32)
        dov = do_ref[...].astype(F32)
        don[...] = dov
        ldl[...] = jnp.where((lane % HD) < HD // 2, lse_ref[...], _head_sums(dov * o_ref[...].astype(F32)))
        accq[...] = jnp.zeros_like(accq)
        acck[...] = jnp.zeros_like(acck)
        accv[...] = jnp.zeros_like(accv)

        for i, d in enumerate(DILS):
            seg = NB // d

            def blk(j, c, d=d, seg=seg):
                cur, prev, first = _block_rows(j, d, seg)
                q2 = _stack_heads(qn[cur, :].astype(BF16), m0, m1)
                do2 = _stack_heads(don[cur, :].astype(BF16), m0, m1)
                (kk, vv), mask = _block_keys((kn, vn), cur, prev, first, seg == 1)
                ldv = ldl[cur, :]
                lse2 = jnp.concatenate([ldv[:, 0:1], ldv[:, HD:HD + 1]], axis=0)
                dl2 = jnp.concatenate([ldv[:, HD // 2:HD // 2 + 1], ldv[:, HD + HD // 2:HD + HD // 2 + 1]], axis=0)
                s = lax.dot_general(q2, kk, NT, preferred_element_type=F32)
                p = jnp.where(jnp.concatenate([mask, mask], axis=0), jnp.exp(s - lse2), 0.0)
                dp = lax.dot_general(do2, vv, NT, preferred_element_type=F32)
                ds = (p * (dp - dl2)).astype(BF16)
                dq_acc = _unstack_heads(jnp.dot(ds, kk, preferred_element_type=F32), m0)
                dk_acc = lax.dot_general(ds, q2, TN, preferred_element_type=F32)
                dv_acc = lax.dot_general(p.astype(BF16), do2, TN, preferred_element_type=F32)
                accq[cur, :] += dq_acc
                if seg == 1:
                    acck[cur, :] += dk_acc
                    accv[cur, :] += dv_acc
                else:
                    acck[prev, :] += dk_acc[0:BLK]
                    acck[cur, :] += dk_acc[BLK:2 * BLK]
                    accv[prev, :] += dv_acc[0:BLK]
                    accv[cur, :] += dv_acc[BLK:2 * BLK]
                return c

            lax.fori_loop(0, NB, blk, 0, unroll=8)

        def norm_bwd(x_ref, r_ref, dn, gain):
            r = r_ref[...]
            xhat = x_ref[...].astype(F32) * r
            dxhat = dn * gain
            dx = r * (dxhat - xhat * (_head_sums(dxhat * xhat) * (1.0 / HD)))
            return dx, jnp.sum(dn * xhat, axis=0, keepdims=True)

        dq, dgq = norm_bwd(q_ref, rq, accq[...], qg_ref[...] * scale)
        dk, dgk = norm_bwd(k_ref, rk, acck[...], kg_ref[...])
        dq_ref[...] = dq.astype(BF16)
        dk_ref[...] = dk.astype(BF16)
        dv_ref[...] = accv[...].astype(BF16)
        dgn_ref[...] = jnp.concatenate([dgq * scale, dgk, jnp.zeros((6, pair), F32)], axis=0)[None]

    blk_spec = lambda off: pl.BlockSpec((S, pair), lambda b, p, off=off: (b, off + p))
    gspec = pl.BlockSpec((1, pair), lambda b, p: (0, 0))
    np_ = DA // pair
    return _pallas(
        body, (u, u, u, attn, dattn, lse, qg2, kg2), grid=(B, np_),
        in_specs=[blk_spec(0), blk_spec(np_), blk_spec(2 * np_), blk_spec(0), blk_spec(0), blk_spec(0),
                  gspec, gspec],
        out_specs=[blk_spec(0), blk_spec(0), blk_spec(0),
                   pl.BlockSpec((1, 8, pair), lambda b, p: (b * np_ + p, 0, 0))],
        out_shape=[jax.ShapeDtypeStruct((T, DA), BF16)] * 3 + [jax.ShapeDtypeStruct((B * np_, 8, pair), F32)],
        scratch_shapes=[pltpu.VMEM((S, pair), F32)] * 10,
        sem=("parallel", "parallel"), name="attn_bwd", carry=carry)


CT = 32
CPAD = 32


def _shifted(win, offsets):
    rolled, out = {}, {}
    n = win.shape[0]
    for o in offsets:
        sub = o % 8
        if sub not in rolled:
            rolled[sub] = win if sub == 0 else pltpu.roll(win, n - sub, 0)
        out[o] = rolled[sub][o - sub:o - sub + CT, :]
    return out


def _ln_fwd(y, g, b):
    mu = jnp.mean(y, axis=-1, keepdims=True)
    yc = y - mu
    rstd = lax.rsqrt(jnp.mean(yc * yc, axis=-1, keepdims=True) + EPS)
    xhat = yc * rstd
    return xhat, rstd, xhat * g + b


def _fill_glu(ca_ref, cg_ref, glu, S):
    glu[pl.ds(0, CPAD), :] = jnp.zeros((CPAD, DC), F32)

    def fill(i, c):
        rows = pl.ds(pl.multiple_of(i * 256, 256), 256)
        a = ca_ref[rows, :].astype(F32)
        gt = cg_ref[rows, :].astype(F32)
        glu[pl.ds(pl.multiple_of(CPAD + i * 256, CT), 256), :] = a * _sigmoid(gt)
        return c

    lax.fori_loop(0, S // 256, fill, 0)


def _conv_fwd(u, cw, cb, lg, lb, B, S):
    T = B * S

    def body(ca_ref, cg_ref, w_ref, b_ref, lg_ref, lb_ref, o_ref, y_ref, glu):
        _fill_glu(ca_ref, cg_ref, glu, S)

        def step(i, c):
            t0 = pl.multiple_of(i * CT, CT)
            win = glu[pl.ds(t0, 2 * CT), :]
            acc = jnp.zeros((CT, DC), F32) + b_ref[...]
            taps = _shifted(win, [k + 2 for k in range(CK)])
            for k in range(CK):
                acc = acc + taps[k + 2] * w_ref[k:k + 1, :]
            y_ref[pl.ds(t0, CT), :] = acc
            _, _, z = _ln_fwd(acc, lg_ref[...], lb_ref[...])
            o_ref[pl.ds(t0, CT), :] = (z * _sigmoid(z)).astype(BF16)
            return c

        lax.fori_loop(0, S // CT, step, 0, unroll=2)

    vec = pl.BlockSpec((1, DC), lambda b: (0, 0))
    return pl.pallas_call(
        body, grid=(B,),
        in_specs=[pl.BlockSpec((S, DC), lambda b: (b, 3)), pl.BlockSpec((S, DC), lambda b: (b, 4)),
                  pl.BlockSpec((CT, DC), lambda b: (0, 0)), vec, vec, vec],
        out_specs=[pl.BlockSpec((S, DC), lambda b: (b, 0))] * 2,
        out_shape=[jax.ShapeDtypeStruct((T, DC), BF16), jax.ShapeDtypeStruct((T, DC), F32)],
        scratch_shapes=[pltpu.VMEM((CPAD + S, DC), F32)],
        compiler_params=_cp(("parallel",)), name="conv_fwd")(u, u, cw, cb, lg, lb)


def _conv_bwd(u, y, dconv, cw, lg, lb, B, S):
    T = B * S

    def body(ca_ref, cg_ref, y_ref, dc_ref, w_ref, lg_ref, lb_ref,
             dca_ref, dcg_ref, dw_ref, ds_ref, glu, dyp, dwacc):
        _fill_glu(ca_ref, cg_ref, glu, S)
        dyp[pl.ds(S, CPAD), :] = jnp.zeros((CPAD, DC), F32)
        lgv, lbv = lg_ref[...], lb_ref[...]

        def sum8(v):
            return functools.reduce(jnp.add, [v[r:r + 8] for r in range(0, v.shape[0], 8)])

        P1 = 4 * CT

        def p1(i, carry):
            sb, sg, sl = carry
            t0 = pl.multiple_of(i * P1, P1)
            xhat, rstd, z = _ln_fwd(y_ref[pl.ds(t0, P1), :], lgv, lbv)
            sz = _sigmoid(z)
            dz = dc_ref[pl.ds(t0, P1), :].astype(F32) * (sz * (1.0 + z * (1.0 - sz)))
            dxhat = dz * lgv
            dy = rstd * (dxhat - jnp.mean(dxhat, axis=-1, keepdims=True)
                         - xhat * jnp.mean(dxhat * xhat, axis=-1, keepdims=True))
            dyp[pl.ds(t0, P1), :] = dy
            return sb + sum8(dy), sg + sum8(dz * xhat), sl + sum8(dz)

        z8 = jnp.zeros((8, DC), F32)
        sb, sg, sl = lax.fori_loop(0, S // P1, p1, (z8, z8, z8))
        rs = lambda v: jnp.sum(v, axis=0, keepdims=True)
        ds_ref[...] = jnp.concatenate([rs(sb), rs(sg), rs(sl), jnp.zeros((5, DC), F32)], axis=0)[None]

        def p2(i, c):
            t0 = pl.multiple_of(i * CT, CT)
            win = dyp[pl.ds(t0, 2 * CT), :]
            acc = jnp.zeros((CT, DC), F32)
            taps = _shifted(win, [30 - k for k in range(CK)])
            for k in range(CK):
                acc = acc + taps[30 - k] * w_ref[k:k + 1, :]
            a = ca_ref[pl.ds(t0, CT), :].astype(F32)
            sgt = _sigmoid(cg_ref[pl.ds(t0, CT), :].astype(F32))
            dca_ref[pl.ds(t0, CT), :] = (acc * sgt).astype(BF16)
            dcg_ref[pl.ds(t0, CT), :] = (acc * a * sgt * (1.0 - sgt)).astype(BF16)
            return c

        lax.fori_loop(0, S // CT, p2, 0)

        dwacc[...] = jnp.zeros_like(dwacc)

        def p3(i, c):
            t0 = pl.multiple_of(i * CT, CT)
            win = glu[pl.ds(t0, 2 * CT), :]
            dy = dyp[pl.ds(t0, CT), :]
            for k in range(CK):
                dwacc[k] += sum8(dy * win[k + 2:k + 2 + CT, :])
            return c

        lax.fori_loop(0, S // CT, p3, 0)
        dw_ref[...] = jnp.sum(dwacc[...], axis=1)[None]

    vec = pl.BlockSpec((1, DC), lambda b: (0, 0))
    seq = pl.BlockSpec((S, DC), lambda b: (b, 0))
    return pl.pallas_call(
        body, grid=(B,),
        in_specs=[pl.BlockSpec((S, DC), lambda b: (b, 3)), pl.BlockSpec((S, DC), lambda b: (b, 4)),
                  seq, seq, pl.BlockSpec((CT, DC), lambda b: (0, 0)), vec, vec],
        out_specs=[seq, seq, pl.BlockSpec((1, CT, DC), lambda b: (b, 0, 0)),
                   pl.BlockSpec((1, 8, DC), lambda b: (b, 0, 0))],
        out_shape=[jax.ShapeDtypeStruct((T, DC), BF16)] * 2
                  + [jax.ShapeDtypeStruct((B, CT, DC), F32), jax.ShapeDtypeStruct((B, 8, DC), F32)],
        scratch_shapes=[pltpu.VMEM((CPAD + S, DC), F32), pltpu.VMEM((S + CPAD, DC), F32),
                        pltpu.VMEM((CT, 8, DC), F32)],
        compiler_params=_cp(("parallel",)), name="conv_bwd")(u, u, y, dconv, cw, lg, lb)


def _local_step(x, target, norms, W, B, S, comm=None):
    qg2 = jnp.concatenate([norms["q_norm"], norms["q_norm"]], axis=1)
    kg2 = jnp.concatenate([norms["k_norm"], norms["k_norm"]], axis=1)
    cw = jnp.concatenate([W["conv_w"], jnp.zeros((1, DC), F32)], axis=0)

    W = dict(W)
    (n1, g1, u1, act1), got = _ffn_gate_up(x, norms["ffn1_norm"], W["wg1"], W["wu1"], "ffn1_gate_up",
                                           carry=comm.gathers["down_in"] if comm else None)
    if comm:
        W.update(comm.gathered("down_in", got))
    h1, u, n2 = _ffn_down_mix_in(x, act1, W["wd1"], norms["mix_norm"], W["win"], "ffn1_down_mix_in")
    (attn, lse), got = _attn_fwd(u, qg2, kg2, B, S, carry=comm.gathers["ffn2"] if comm else None)
    if comm:
        W = dict(W, **comm.gathered("ffn2", got))
    conv, y = _conv_fwd(u, cw, norms["conv_b"], norms["conv_ln_g"], norms["conv_ln_b"], B, S)
    h2, n3, g2, u2, dout, dyb, sq, act2 = _mix_out_ffn_loss(h1, attn, conv, W["wout"], norms["ffn2_norm"],
                                                            W["wg2"], W["wu2"], W["wd2"], target, "ffn2_fwd")
    loss = (0.5 / D) * jnp.sum(sq)

    (dg2, du2, dh2, dgn_ffn2), _ = _ffn_bwd_act(dyb, g2, u2, h2, dout, norms["ffn2_norm"],
                                               W["wg2"], W["wu2"], W["wd2"], "ffn2_bwd_act")
    dwd2, _ = _ffn_bwd_w(act2, dyb, "ffn2_bwd_wd")
    dwg2, _ = _ffn_bwd_w(dg2, n3, "ffn2_bwd_wg")
    dwu2, _ = _ffn_bwd_w(du2, n3, "ffn2_bwd_wu")
    dattn, dconv, dwout = _mix_out_bwd(dh2, attn, conv, W["wout"])
    carry = comm.reduce_start("ffn2", {"wg2": dwg2, "wu2": dwu2, "wd2": dwd2, "wout": dwout}) if comm else None
    (dq, dk, dv, dgn_qk), got = _attn_bwd(u, attn, dattn, lse, qg2, kg2, B, S, carry=carry)
    if comm:
        comm.reduce_done(carry, got)
    dca, dcg, dcw, dcs = _conv_bwd(u, y, dconv, cw, norms["conv_ln_g"], norms["conv_ln_b"], B, S)
    dwin, dh1, dyb1, dgn_mix = _mix_in_bwd((dq, dk, dv, dca, dcg), W["win"], n2, h1, dh2, norms["mix_norm"])
    (dg1, du1, gx, dgn_ffn1), _ = _ffn_bwd_act(dyb1, g1, u1, x, dh1, norms["ffn1_norm"],
                                              W["wg1"], W["wu1"], W["wd1"], "ffn1_bwd_act")
    carry = comm.reduce_start("win", {"win": dwin}) if comm else None
    dwd1, got = _ffn_bwd_w(act1, dyb1, "ffn1_bwd_wd", carry=carry)
    if comm:
        comm.reduce_done(carry, got)
        carry = comm.reduce_start("wd1", {"wd1": dwd1})
    dwg1, got = _ffn_bwd_w(dg1, n1, "ffn1_bwd_wg", carry=carry)
    if comm:
        comm.reduce_done(carry, got)
        carry = comm.reduce_start("wg1", {"wg1": dwg1})
    dwu1, got = _ffn_bwd_w(du1, n1, "ffn1_bwd_wu", carry=carry)
    if comm:
        comm.reduce_done(carry, got)
        comm.last = comm.reduce_start("wu1", {"wu1": dwu1})

    qk = jnp.sum(dgn_qk, axis=0)
    cs = jnp.sum(dcs, axis=0)
    small = {
        "ffn1_norm": jnp.sum(dgn_ffn1, axis=0),
        "mix_norm": jnp.sum(dgn_mix, axis=0),
        "q_norm": qk[0:1, 0:HD] + qk[0:1, HD:2 * HD],
        "k_norm": qk[1:2, 0:HD] + qk[1:2, HD:2 * HD],
        "conv_w": jnp.sum(dcw, axis=0)[0:CK],
        "conv_b": cs[0:1],
        "conv_ln_g": cs[1:2],
        "conv_ln_b": cs[2:3],
        "ffn2_norm": jnp.sum(dgn_ffn2, axis=0),
    }
    big = {"wg1": dwg1, "wu1": dwu1, "wd1": dwd1, "win": dwin, "wout": dwout,
           "wg2": dwg2, "wu2": dwu2, "wd2": dwd2}
    return loss, gx, big, small


HBM = pl.BlockSpec(memory_space=pltpu.HBM)
VMEM = pl.BlockSpec(memory_space=pltpu.VMEM)


def _place():
    return lax.axis_index("x"), lax.axis_index("y"), lax.axis_index("c")


class _GatherCarry:
    def __init__(self, shards, mid_at=0.5):
        nt = len(shards)
        self.mid_at = mid_at
        self.shards = shards
        self.in_arrays = [s for s, _ in shards]
        self.in_specs = [VMEM] * nt
        self.out_shape = [jax.ShapeDtypeStruct((NDEV * s.shape[0], s.shape[1]), dt) for s, dt in shards]
        self.out_specs = [HBM] * nt
        self.scratch = ([pltpu.VMEM(s.shape, dt) for s, dt in shards]
                        + [pltpu.SemaphoreType.DMA((nt, 7)), pltpu.SemaphoreType.DMA((nt, 7)),
                           pltpu.SemaphoreType.DMA((nt,))])

    def _copies(self, outs, scr):
        nt = len(self.shards)
        stages = scr[:nt]
        send_sems, recv_sems, local_sems = scr[nt:]
        x, y, c = _place()
        me, sibling = (x, y, c), (x, y, 1 - c)
        xn, yn, diag = (1 - x, y, c), (x, 1 - y, c), (1 - x, 1 - y, c)
        via = (x ^ c, y ^ (1 - c), c)
        onto = (x ^ (1 - c), y ^ c, c)

        def rows(t, px, py, pc):
            r = self.shards[t][0].shape[0]
            return outs[t].at[pl.ds((4 * px + 2 * py + pc) * r, r), :]

        def copy(t, k, block, to, src=None):
            return pltpu.make_async_remote_copy(
                src_ref=rows(t, *block) if src is None else src, dst_ref=rows(t, *block),
                send_sem=send_sems.at[t, k], recv_sem=recv_sems.at[t, k],
                device_id=to, device_id_type=MESH)

        sib = lambda b: (b[0], b[1], 1 - c)
        return dict(
            local=[pltpu.make_async_copy(stages[t], rows(t, *me), local_sems.at[t]) for t in range(nt)],
            own=[[copy(t, 0, me, sibling, src=stages[t]), copy(t, 1, me, xn, src=stages[t]),
                  copy(t, 2, me, yn, src=stages[t])] for t in range(nt)],
            relay=[copy(t, 3, via, onto) for t in range(nt)],
            down=[[copy(t, 4, xn, sibling), copy(t, 5, yn, sibling)] for t in range(nt)],
            down_diag=[copy(t, 6, diag, sibling) for t in range(nt)],
            got_xy=[[copy(t, 1, xn, me), copy(t, 2, yn, me)] for t in range(nt)],
            got_diag=[copy(t, 3, diag, me) for t in range(nt)],
            got_sib=[[copy(t, 0, sibling, me), copy(t, 4, sib(xn), me), copy(t, 5, sib(yn), me),
                      copy(t, 6, sib(diag), me)] for t in range(nt)])

    def start(self, ins, outs, scr):
        cps = self._copies(outs, scr)
        for t, (_, dt) in enumerate(self.shards):
            scr[t][...] = ins[t][...].astype(dt)
            for cp in [cps["local"][t]] + cps["own"][t]:
                cp.start()

    def mid(self, ins, outs, scr):
        cps = self._copies(outs, scr)
        for t in range(len(self.shards)):
            for cp in cps["got_xy"][t]:
                cp.wait_recv()
            for cp in [cps["relay"][t]] + cps["down"][t]:
                cp.start()

    def finish(self, ins, outs, scr):
        cps = self._copies(outs, scr)
        for t in range(len(self.shards)):
            cps["got_diag"][t].wait_recv()
            cps["down_diag"][t].start()
        for t in range(len(self.shards)):
            for cp in cps["got_sib"][t]:
                cp.wait_recv()
            for cp in cps["own"][t] + [cps["relay"][t]] + cps["down"][t] + [cps["down_diag"][t]]:
                cp.wait_send()
            cps["local"][t].wait()


def _run_carry(carry, name):
    def body(*refs):
        n_in, n_out = len(carry.in_arrays), len(carry.out_shape)
        ins, outs, scr = refs[:n_in], refs[n_in:n_in + n_out], refs[n_in + n_out:]
        carry.start(ins, outs, scr)
        if hasattr(carry, "mid"):
            carry.mid(ins, outs, scr)
        carry.finish(ins, outs, scr)

    return pl.pallas_call(
        body, in_specs=carry.in_specs, out_specs=carry.out_specs, out_shape=carry.out_shape,
        scratch_shapes=carry.scratch, compiler_params=pltpu.CompilerParams(vmem_limit_bytes=VMEM_LIMIT),
        name=name)(*carry.in_arrays)


def _sibling_reduce(grads, name):
    nt = len(grads)
    g4 = [g.reshape(4, 2, g.shape[0] // NDEV, g.shape[1]) for g in grads]

    def body(*refs):
        ins, outs = refs[:nt], refs[nt:2 * nt]
        recv, own = refs[2 * nt:3 * nt], refs[3 * nt:4 * nt]
        send_sems, recv_sems, load_sems, store_sems = refs[4 * nt:]
        x, y, c = _place()
        sends = [pltpu.make_async_remote_copy(
            src_ref=ins[t].at[:, 1 - c], dst_ref=recv[t], send_sem=send_sems.at[t], recv_sem=recv_sems.at[t],
            device_id=(x, y, 1 - c), device_id_type=MESH) for t in range(nt)]
        loads = [pltpu.make_async_copy(ins[t].at[:, c], own[t], load_sems.at[t]) for t in range(nt)]
        stores = [pltpu.make_async_copy(own[t], outs[t], store_sems.at[t]) for t in range(nt)]
        for cp in sends + loads:
            cp.start()
        for t in range(nt):
            loads[t].wait()
            sends[t].wait_recv()
            for q in range(4):
                own[t][q] = (own[t][q].astype(F32) + recv[t][q].astype(F32)).astype(BF16)
            stores[t].start()
        for t in range(nt):
            sends[t].wait_send()
            stores[t].wait()

    blocks = [(4,) + g.shape[2:] for g in g4]
    return pl.pallas_call(
        body, in_specs=[HBM] * nt, out_specs=[HBM] * nt,
        out_shape=[jax.ShapeDtypeStruct(b, BF16) for b in blocks],
        scratch_shapes=[pltpu.VMEM(b, BF16) for b in blocks] * 2 + [pltpu.SemaphoreType.DMA((nt,))] * 4,
        compiler_params=pltpu.CompilerParams(vmem_limit_bytes=VMEM_LIMIT), name=name)(*g4)


class _ExchangeCarry:
    def __init__(self, names, parts, mid_at=0.5):
        nt = len(parts)
        self.mid_at = mid_at
        self.names = names
        self.in_arrays = list(parts)
        self.in_specs = [HBM] * nt
        self.out_shape = [jax.ShapeDtypeStruct((3,) + p.shape[1:], BF16) for p in parts]
        self.out_specs = [HBM] * nt
        self.scratch = ([pltpu.VMEM(p.shape[1:], BF16) for p in parts] * 2
                        + [pltpu.SemaphoreType.DMA((nt, 3)), pltpu.SemaphoreType.DMA((nt, 3)),
                           pltpu.SemaphoreType.DMA((nt,)), pltpu.SemaphoreType.DMA((nt,))])

    def _copies(self, ins, outs, scr):
        nt = len(ins)
        relayed, mine = scr[:nt], scr[nt:2 * nt]
        send_sems, recv_sems, local_sems, load_sems = scr[2 * nt:]
        x, y, c = _place()
        q = lambda cx, cy: 2 * cx + cy
        near, far = (x ^ c, y ^ (1 - c)), (x ^ (1 - c), y ^ c)

        def remote(t, k, src, dst, chip):
            return pltpu.make_async_remote_copy(
                src_ref=src, dst_ref=dst, send_sem=send_sems.at[t, k], recv_sem=recv_sems.at[t, k],
                device_id=(*chip, c), device_id_type=MESH)

        return dict(
            keep=[pltpu.make_async_copy(ins[t].at[q(x, y)], outs[t].at[0], local_sems.at[t]) for t in range(nt)],
            load=[pltpu.make_async_copy(ins[t].at[q(*far)], mine[t], load_sems.at[t]) for t in range(nt)],
            direct=[remote(t, 0, ins[t].at[q(*near)], outs[t].at[1], near) for t in range(nt)],
            relay=[remote(t, 1, ins[t].at[q(1 - x, 1 - y)], relayed[t], near) for t in range(nt)],
            merged=[remote(t, 2, mine[t], outs[t].at[2], far) for t in range(nt)],
            relayed=relayed, mine=mine)

    def start(self, ins, outs, scr):
        cps = self._copies(ins, outs, scr)
        for t in range(len(ins)):
            for kind in ("keep", "load", "direct", "relay"):
                cps[kind][t].start()

    def mid(self, ins, outs, scr):
        cps = self._copies(ins, outs, scr)
        for t in range(len(ins)):
            cps["load"][t].wait()
            cps["relay"][t].wait_recv()
            cps["mine"][t][...] = (cps["mine"][t][...].astype(F32) + cps["relayed"][t][...].astype(F32)).astype(BF16)
            cps["merged"][t].start()

    def finish(self, ins, outs, scr):
        cps = self._copies(ins, outs, scr)
        for t in range(len(ins)):
            cps["direct"][t].wait()
            cps["relay"][t].wait_send()
            cps["merged"][t].wait()
            cps["keep"][t].wait()


class _Comm:
    def __init__(self, groups):
        self.names = {tag: list(g) for tag, (g, _) in groups.items()}
        self.gathers = {tag: _GatherCarry(list(g.values()), mid_at) for tag, (g, mid_at) in groups.items()}
        self.reduced = {}

    def gathered(self, tag, outs):
        return dict(zip(self.names[tag], outs))

    def reduce_start(self, tag, grads, mid_at=0.5):
        names = list(grads)
        parts = _sibling_reduce([grads[n] for n in names], "sibling_reduce_" + tag)
        return _ExchangeCarry(names, parts, mid_at)

    def reduce_done(self, carry, outs):
        self.reduced.update(zip(carry.names, outs))


def _adamw_math(w, g, m, v):
    m = B1 * m + (1.0 - B1) * g
    v = B2 * v + (1.0 - B2) * (g * g)
    m_hat = m / (1.0 - B1 ** STEP)
    v_hat = v / (1.0 - B2 ** STEP)
    delta = -LR * (m_hat / (jnp.sqrt(v_hat) + AEPS) + WD * w)
    return delta, m, v


def _adamw_big(recv, w, m, v, name):
    def body(r_ref, w_ref, m_ref, v_ref, g_ref, d_ref, mo_ref, vo_ref):
        g = r_ref[0].astype(F32)
        for q in range(1, 3):
            g = g + r_ref[q].astype(F32)
        d, mn, vn = _adamw_math(w_ref[...], g, m_ref[...], v_ref[...])
        g_ref[...] = g
        d_ref[...] = d
        mo_ref[...] = mn
        vo_ref[...] = vn

    rows, n = w.shape
    tr = rows // 2
    row = pl.BlockSpec((tr, n), lambda t: (t, 0))
    return pl.pallas_call(
        body, grid=(2,), in_specs=[pl.BlockSpec((3, tr, n), lambda t: (0, t, 0)), row, row, row],
        out_specs=[row] * 4, out_shape=[jax.ShapeDtypeStruct(w.shape, F32)] * 4,
        compiler_params=_cp(("parallel",)), name=name)(recv, w, m, v)


SMALL_NAMES = ("ffn1_norm", "mix_norm", "ffn2_norm", "conv_b", "conv_ln_g", "conv_ln_b", "q_norm", "k_norm")
SROWS = 16
LOSS_ROW = len(SMALL_NAMES)
CWF = 2


def _small_step(gs, loss_row, gcw, ws, ms, vs, wcw, mcw, vcw, carry=None):
    ns = len(SMALL_NAMES)
    widths = [g.shape[1] for g in gs]

    def body(*refs):
        it = iter(refs)
        take = lambda n: [next(it) for _ in range(n)]
        g_refs, (loss_ref, gcw_ref) = take(ns), take(2)
        w_refs, m_refs, v_refs = take(ns), take(ns), take(ns)
        wcw_ref, mcw_ref, vcw_ref = take(3)
        cins = take(len(carry.in_arrays)) if carry else []
        outs = [take(4) for _ in range(ns)]
        cw_outs, (loss_out,) = take(4), take(1)
        couts = take(len(carry.out_shape)) if carry else []
        send, slots, cslots, send_sems, recv_sems, csend_sems, crecv_sems = take(7)
        cscr = list(it)
        x, y, c = _place()
        me = 4 * x + 2 * y + c
        send[...] = jnp.zeros_like(send)
        for k in range(ns):
            send[k:k + 1, 0:widths[k]] = g_refs[k][...]
        send[LOSS_ROW:LOSS_ROW + 1, 0:128] = loss_ref[...]
        slots[me] = send[...]
        cslots[me] = gcw_ref[...]
        cps = []
        for k in range(1, NDEV):
            peer = (x ^ ((k >> 2) & 1), y ^ ((k >> 1) & 1), c ^ (k & 1))
            cps.append(pltpu.make_async_remote_copy(
                src_ref=send, dst_ref=slots.at[me], send_sem=send_sems.at[k - 1], recv_sem=recv_sems.at[k - 1],
                device_id=peer, device_id_type=MESH))
            cps.append(pltpu.make_async_remote_copy(
                src_ref=gcw_ref, dst_ref=cslots.at[me], send_sem=csend_sems.at[k - 1],
                recv_sem=crecv_sems.at[k - 1], device_id=peer, device_id_type=MESH))
        for cp in cps:
            cp.start()
        if carry:
            carry.start(cins, couts, cscr)
        for cp in cps:
            cp.wait()
        if carry:
            carry.mid(cins, couts, cscr)
        tot = slots[0]
        ctot = cslots[0, me]
        for j in range(1, NDEV):
            tot = tot + slots[j]
            ctot = ctot + cslots[j, me]

        def step(g, w_ref, m_ref, v_ref, o):
            d, mn, vn = _adamw_math(w_ref[...], g, m_ref[...], v_ref[...])
            o[0][...], o[1][...], o[2][...], o[3][...] = g, d, mn, vn

        for k in range(ns):
            step(tot[k:k + 1, 0:widths[k]], w_refs[k], m_refs[k], v_refs[k], outs[k])
        step(ctot, wcw_ref, mcw_ref, vcw_ref, cw_outs)
        loss_out[...] = tot[LOSS_ROW:LOSS_ROW + 1, 0:128]
        if carry:
            carry.finish(cins, couts, cscr)

    args = [*gs, loss_row, gcw, *ws, *ms, *vs, wcw, mcw, vcw]
    out_shape = ([jax.ShapeDtypeStruct((1, n), F32) for n in widths for _ in range(4)]
                 + [jax.ShapeDtypeStruct((CWF, D), F32)] * 4 + [jax.ShapeDtypeStruct((1, 128), F32)])
    n_own = len(out_shape)
    res = pl.pallas_call(
        body, in_specs=[VMEM] * len(args) + (carry.in_specs if carry else []),
        out_specs=[VMEM] * n_own + (carry.out_specs if carry else []),
        out_shape=out_shape + (carry.out_shape if carry else []),
        scratch_shapes=[pltpu.VMEM((SROWS, D), F32), pltpu.VMEM((NDEV, SROWS, D), F32),
                        pltpu.VMEM((NDEV, NDEV, CWF, D), F32)]
                       + [pltpu.SemaphoreType.DMA((NDEV - 1,))] * 4 + (carry.scratch if carry else []),
        name="small_step")(*args, *(carry.in_arrays if carry else []))
    per = [res[4 * k:4 * k + 4] for k in range(ns)]
    return per, res[4 * ns:4 * ns + 4], res[n_own - 1], res[n_own:]


def _pack_cw(a):
    flat = a.reshape(a.shape[:-2] + (CK * HD,))
    pad = [(0, 0)] * (flat.ndim - 1) + [(0, CWF * D - CK * HD)]
    return jnp.pad(flat, pad).reshape(a.shape[:-2] + (CWF, D))


def _unpack_cw(v):
    return v.reshape(-1)[:CK * HD].reshape(1, CK, HD)


def kernel(x, ffn1_norm, ffn1_w_gate, ffn1_w_up, ffn1_w_down, mix_norm, w_in, q_norm, k_norm, conv_w, conv_b, conv_ln_g, conv_ln_b, w_out, ffn2_norm, ffn2_w_gate, ffn2_w_up, ffn2_w_down, loss_target, m_ffn1_norm, m_ffn1_w_gate, m_ffn1_w_up, m_ffn1_w_down, m_mix_norm, m_w_in, m_q_norm, m_k_norm, m_conv_w, m_conv_b, m_conv_ln_g, m_conv_ln_b, m_w_out, m_ffn2_norm, m_ffn2_w_gate, m_ffn2_w_up, m_ffn2_w_down, v_ffn1_norm, v_ffn1_w_gate, v_ffn1_w_up, v_ffn1_w_down, v_mix_norm, v_w_in, v_q_norm, v_k_norm, v_conv_w, v_conv_b, v_conv_ln_g, v_conv_ln_b, v_w_out, v_ffn2_norm, v_ffn2_w_gate, v_ffn2_w_up, v_ffn2_w_down):
    P = dict(ffn1_norm=ffn1_norm, ffn1_w_gate=ffn1_w_gate, ffn1_w_up=ffn1_w_up, ffn1_w_down=ffn1_w_down,
             mix_norm=mix_norm, w_in=w_in, q_norm=q_norm, k_norm=k_norm, conv_w=conv_w, conv_b=conv_b,
             conv_ln_g=conv_ln_g, conv_ln_b=conv_ln_b, w_out=w_out, ffn2_norm=ffn2_norm,
             ffn2_w_gate=ffn2_w_gate, ffn2_w_up=ffn2_w_up, ffn2_w_down=ffn2_w_down)
    M = dict(ffn1_norm=m_ffn1_norm, ffn1_w_gate=m_ffn1_w_gate, ffn1_w_up=m_ffn1_w_up, ffn1_w_down=m_ffn1_w_down,
             mix_norm=m_mix_norm, w_in=m_w_in, q_norm=m_q_norm, k_norm=m_k_norm, conv_w=m_conv_w, conv_b=m_conv_b,
             conv_ln_g=m_conv_ln_g, conv_ln_b=m_conv_ln_b, w_out=m_w_out, ffn2_norm=m_ffn2_norm,
             ffn2_w_gate=m_ffn2_w_gate, ffn2_w_up=m_ffn2_w_up, ffn2_w_down=m_ffn2_w_down)
    V = dict(ffn1_norm=v_ffn1_norm, ffn1_w_gate=v_ffn1_w_gate, ffn1_w_up=v_ffn1_w_up, ffn1_w_down=v_ffn1_w_down,
             mix_norm=v_mix_norm, w_in=v_w_in, q_norm=v_q_norm, k_norm=v_k_norm, conv_w=v_conv_w, conv_b=v_conv_b,
             conv_ln_g=v_conv_ln_g, conv_ln_b=v_conv_ln_b, w_out=v_w_out, ffn2_norm=v_ffn2_norm,
             ffn2_w_gate=v_ffn2_w_gate, ffn2_w_up=v_ffn2_w_up, ffn2_w_down=v_ffn2_w_down)
    order = ["ffn1_norm", "ffn1_w_gate", "ffn1_w_up", "ffn1_w_down", "mix_norm", "w_in", "q_norm", "k_norm",
             "conv_w", "conv_b", "conv_ln_g", "conv_ln_b", "w_out", "ffn2_norm", "ffn2_w_gate", "ffn2_w_up",
             "ffn2_w_down"]
    B, S, _ = x.shape
    T = B * S

    bigs = [("wg1", "ffn1_w_gate", True), ("wu1", "ffn1_w_up", True), ("wd1", "ffn1_w_down", False),
            ("win", "w_in", True), ("wout", "w_out", False),
            ("wg2", "ffn2_w_gate", True), ("wu2", "ffn2_w_up", True), ("wd2", "ffn2_w_down", False)]
    hm = lambda a, tr: jnp.transpose(a[0]) if tr else a[0]
    cw_pad = jnp.zeros((32, 128), F32).at[0:CK, 0:HD].set(conv_w[0])
    shard = {ln: (hm(P[pn], tr), BF16) for ln, pn, tr in bigs}
    gathered = _run_carry(_GatherCarry([shard["wg1"], shard["wu1"], (cw_pad, F32)]), "gather_first")
    W = {"wg1": gathered[0], "wu1": gathered[1]}
    cwg = gathered[2].reshape(NDEV, 32, 128)[:, 0:CK, 0:HD]
    W["conv_w"] = jnp.transpose(cwg, (1, 0, 2)).reshape(CK, DC)
    norms = {n: P[n] for n in SMALL_NAMES}
    comm = _Comm({"down_in": ({n: shard[n] for n in ("wd1", "win", "wout")}, 0.5),
                  "ffn2": ({n: shard[n] for n in ("wg2", "wu2", "wd2")}, 0.5)})

    loss_part, gx, _, small = _local_step(x.reshape(T, D), loss_target.reshape(T, D), norms, W, B, S, comm)

    G, Dl, Mn, Vn = {}, {}, {}, {}
    dcw = small["conv_w"].reshape(CK, NDEV, HD).transpose(1, 0, 2)
    loss_row = jnp.zeros((1, 128), F32).at[0, 0].set(loss_part)
    per, cw_outs, loss_out, got = _small_step(
        [small[n] for n in SMALL_NAMES], loss_row, _pack_cw(dcw),
        [P[n] for n in SMALL_NAMES], [M[n] for n in SMALL_NAMES], [V[n] for n in SMALL_NAMES],
        _pack_cw(P["conv_w"][0]), _pack_cw(M["conv_w"][0]), _pack_cw(V["conv_w"][0]), carry=comm.last)
    comm.reduce_done(comm.last, got)
    loss = loss_out[0, 0]
    for n, outs in zip(SMALL_NAMES, per):
        G[n], Dl[n], Mn[n], Vn[n] = outs
    G["conv_w"], Dl["conv_w"], Mn["conv_w"], Vn["conv_w"] = [_unpack_cw(o) for o in cw_outs]

    for ln, pn, tr in bigs:
        outs = _adamw_big(comm.reduced[ln], hm(P[pn], tr), hm(M[pn], tr), hm(V[pn], tr), "adamw_" + ln)
        G[pn], Dl[pn], Mn[pn], Vn[pn] = [(jnp.transpose(o) if tr else o)[None] for o in outs]

    return (loss, gx.reshape(B, S, D), *[G[n] for n in order], *[Dl[n] for n in order],
            *[Mn[n] for n in order], *[Vn[n] for n in order])
```

```python
import functools

import jax
import jax.numpy as jnp
from jax import lax
from jax.experimental import pallas as pl
from jax.experimental.pallas import tpu as pltpu

F32 = jnp.float32
BF16 = jnp.bfloat16

D = 1024
FF = 2816
HD = 64
DA = 512
DC = 512
DIN = 2560
CK = 31
BLK = 128
DILS = (1, 4, 16)
EPS = 1e-6
NDEV = 8
MESH = pl.DeviceIdType.MESH

LR, B1, B2, AEPS, WD, STEP = 0.001, 0.9, 0.999, 1e-08, 0.01, 10

NT = (((1,), (1,)), ((), ()))
TN = (((0,), (0,)), ((), ()))

VMEM_LIMIT = 60 * 1024 * 1024


def _cp(sem=None):
    return pltpu.CompilerParams(dimension_semantics=sem, vmem_limit_bytes=VMEM_LIMIT)


def _sigmoid(x):
    return 0.5 * (jnp.tanh(0.5 * x) + 1.0)


def _pallas(body, args, *, grid, in_specs, out_specs, out_shape, scratch_shapes, sem, name, carry=None):
    if carry is None:
        outs = pl.pallas_call(body, grid=grid, in_specs=in_specs, out_specs=out_specs, out_shape=out_shape,
                              scratch_shapes=scratch_shapes, compiler_params=_cp(sem), name=name)(*args)
        return outs, None
    n_in, n_out, n_scr = len(in_specs), len(out_shape), len(scratch_shapes)
    c_in, c_out = len(carry.in_arrays), len(carry.out_shape)

    def wrapped(*refs):
        ins, refs = refs[:n_in], refs[n_in:]
        cins, refs = refs[:c_in], refs[c_in:]
        outs, refs = refs[:n_out], refs[n_out:]
        couts, refs = refs[:c_out], refs[c_out:]
        scr, cscr = refs[:n_scr], refs[n_scr:]
        ids = [pl.program_id(a) for a in range(len(grid))]
        step = ids[0]
        for i, n in zip(ids[1:], grid[1:]):
            step = step * n + i
        steps = functools.reduce(lambda a, b: a * b, grid)

        @pl.when(step == 0)
        def _():
            carry.start(cins, couts, cscr)

        body(*ins, *outs, *scr)

        if hasattr(carry, "mid"):
            @pl.when(step == int(steps * carry.mid_at))
            def _():
                carry.mid(cins, couts, cscr)

        @pl.when(step == steps - 1)
        def _():
            carry.finish(cins, couts, cscr)

    outs = pl.pallas_call(
        wrapped, grid=grid, in_specs=list(in_specs) + carry.in_specs, out_specs=list(out_specs) + carry.out_specs,
        out_shape=list(out_shape) + carry.out_shape, scratch_shapes=list(scratch_shapes) + carry.scratch,
        compiler_params=_cp(("arbitrary",) * len(grid)), name=name)(*args, *carry.in_arrays)
    return outs[:n_out], outs[n_out:]


FC = 256


def _resident(shape):
    return pl.BlockSpec(shape, lambda *_: (0,) * len(shape), pipeline_mode=pl.Buffered(1))


def _mix_out_ffn_loss(h1, attn, conv, wout, gain, wg, wu, wd, target, name):
    T = h1.shape[0]
    tm = 512
    nt = T // tm

    def body(h1_ref, at_ref, cv_ref, wo_ref, gain_ref, wg_ref, wu_ref, wd_ref, t_ref,
             h2_ref, n_ref, g_ref, u_ref, dout_ref, dyb_ref, sq_ref, a_hbm, a_scr, a_sem):
        t = pl.program_id(0)
        a_out = lambda i: pltpu.make_async_copy(a_scr, a_hbm.at[pl.ds(pl.multiple_of(i * tm, tm), tm), :], a_sem)

        @pl.when(t > 0)
        def _():
            a_out(t - 1).wait()

        xv = (h1_ref[...]
              + jnp.dot(at_ref[...], wo_ref[0:DA, :], preferred_element_type=F32)
              + jnp.dot(cv_ref[...], wo_ref[DA:D, :], preferred_element_type=F32))
        h2_ref[...] = xv
        r = lax.rsqrt(jnp.mean(xv * xv, axis=-1, keepdims=True) + EPS)
        n_ref[...] = (xv * r * gain_ref[...]).astype(BF16)
        for c in range(FF // FC):
            cols = slice(c * FC, (c + 1) * FC)
            nb = n_ref[...]
            g = lax.dot_general(nb, wg_ref[cols, :], NT, preferred_element_type=F32)
            u = lax.dot_general(nb, wu_ref[cols, :], NT, preferred_element_type=F32)
            g_ref[:, cols] = g.astype(BF16)
            u_ref[:, cols] = u.astype(BF16)
            a_scr[:, cols] = (g * _sigmoid(g) * u).astype(BF16)
        a_out(t).start()
        e = h2_ref[...] + 0.5 * jnp.dot(a_scr[...], wd_ref[...], preferred_element_type=F32) - t_ref[...]
        dout = e * (1.0 / D)
        dout_ref[...] = dout
        dyb_ref[...] = (0.5 * dout).astype(BF16)
        sq_ref[...] = jnp.sum(e * e, axis=0, keepdims=True)[None]

        @pl.when(t == nt - 1)
        def _():
            a_out(t).wait()

    row = pl.BlockSpec((tm, D), lambda t: (t, 0))
    half = pl.BlockSpec((tm, DA), lambda t: (t, 0))
    wide = pl.BlockSpec((tm, FF), lambda t: (t, 0))
    outs, _ = _pallas(
        body, (h1, attn, conv, wout, gain, wg, wu, wd, target), grid=(nt,),
        in_specs=[row, half, half, _resident((D, D)), _resident((1, D)), _resident((FF, D)), _resident((FF, D)),
                  _resident((FF, D)), row],
        out_specs=[row, row, wide, wide, row, row, pl.BlockSpec((1, 1, D), lambda t: (t, 0, 0)), HBM],
        out_shape=[jax.ShapeDtypeStruct((T, D), F32), jax.ShapeDtypeStruct((T, D), BF16)]
                  + [jax.ShapeDtypeStruct((T, FF), BF16)] * 2
                  + [jax.ShapeDtypeStruct((T, D), F32), jax.ShapeDtypeStruct((T, D), BF16),
                     jax.ShapeDtypeStruct((nt, 1, D), F32), jax.ShapeDtypeStruct((T, FF), BF16)],
        scratch_shapes=[pltpu.VMEM((tm, FF), BF16), pltpu.SemaphoreType.DMA(())],
        sem=("arbitrary",), name=name)
    return outs


def _ffn_gate_up(x, gain, wg, wu, name, carry=None):
    T = x.shape[0]
    tm = 512

    def body(x_ref, gain_ref, wg_ref, wu_ref, n_ref, g_ref, u_ref, a_ref):
        xv = x_ref[...]
        r = lax.rsqrt(jnp.mean(xv * xv, axis=-1, keepdims=True) + EPS)
        n_ref[...] = (xv * r * gain_ref[...]).astype(BF16)
        for c in range(FF // FC):
            cols = slice(c * FC, (c + 1) * FC)
            nb = n_ref[...]
            g = lax.dot_general(nb, wg_ref[cols, :], NT, preferred_element_type=F32)
            u = lax.dot_general(nb, wu_ref[cols, :], NT, preferred_element_type=F32)
            g_ref[:, cols] = g.astype(BF16)
            u_ref[:, cols] = u.astype(BF16)
            a_ref[:, cols] = (g * _sigmoid(g) * u).astype(BF16)

    row = pl.BlockSpec((tm, D), lambda t: (t, 0))
    wide = pl.BlockSpec((tm, FF), lambda t: (t, 0))
    return _pallas(
        body, (x, gain, wg, wu), grid=(T // tm,),
        in_specs=[row, _resident((1, D)), _resident((FF, D)), _resident((FF, D))],
        out_specs=[row, wide, wide, wide],
        out_shape=[jax.ShapeDtypeStruct((T, D), BF16)] + [jax.ShapeDtypeStruct((T, FF), BF16)] * 3,
        scratch_shapes=[], sem=("parallel",), name=name, carry=carry)


def _ffn_down_mix_in(x, a, wd, gain, win, name):
    T = x.shape[0]
    tm = 512

    def body(x_ref, a_ref, wd_ref, gain_ref, win_ref, h_ref, u_ref, n_ref):
        hv = x_ref[...] + 0.5 * jnp.dot(a_ref[...], wd_ref[...], preferred_element_type=F32)
        h_ref[...] = hv
        r = lax.rsqrt(jnp.mean(hv * hv, axis=-1, keepdims=True) + EPS)
        n_ref[...] = (hv * r * gain_ref[...]).astype(BF16)
        u_ref[...] = lax.dot_general(n_ref[...], win_ref[...], NT, preferred_element_type=F32).astype(BF16)

    row = pl.BlockSpec((tm, D), lambda t: (t, 0))
    wide = pl.BlockSpec((tm, FF), lambda t: (t, 0))
    outs, _ = _pallas(
        body, (x, a, wd, gain, win), grid=(T // tm,),
        in_specs=[row, wide, _resident((FF, D)), _resident((1, D)), _resident((DIN, D))],
        out_specs=[row, pl.BlockSpec((tm, DIN), lambda t: (t, 0)), row],
        out_shape=[jax.ShapeDtypeStruct((T, D), F32), jax.ShapeDtypeStruct((T, DIN), BF16),
                   jax.ShapeDtypeStruct((T, D), BF16)],
        scratch_shapes=[], sem=("parallel",), name=name)
    return outs


def _ffn_bwd_act(dyb, g, u, x, dout, gain, wg, wu, wd, name, carry=None):
    T = x.shape[0]
    tm = 256
    nt = T // tm

    def body(dy_ref, g_ref, u_ref, x_ref, dout_ref, gain_ref, wg_ref, wu_ref, wd_ref,
             dg_ref, du_ref, dx_ref, dgn_ref):
        for c in range(FF // FC):
            cols = slice(c * FC, (c + 1) * FC)
            da = lax.dot_general(dy_ref[...], wd_ref[cols, :], NT, preferred_element_type=F32)
            gv = g_ref[:, cols].astype(F32)
            uv = u_ref[:, cols].astype(F32)
            sg = _sigmoid(gv)
            dg_ref[:, cols] = (da * uv * (sg * (1.0 + gv * (1.0 - sg)))).astype(BF16)
            du_ref[:, cols] = (da * (gv * sg)).astype(BF16)
        dn = (jnp.dot(dg_ref[...], wg_ref[...], preferred_element_type=F32)
              + jnp.dot(du_ref[...], wu_ref[...], preferred_element_type=F32))
        dx, dgain = _rms_bwd_rows(dn, x_ref[...], gain_ref[...])
        dx_ref[...] = dout_ref[...] + dx
        dgn_ref[...] = dgain[None]

    row = pl.BlockSpec((tm, D), lambda t: (t, 0))
    wide = pl.BlockSpec((tm, FF), lambda t: (t, 0))
    return _pallas(
        body, (dyb, g, u, x, dout, gain, wg, wu, wd), grid=(nt,),
        in_specs=[row, wide, wide, row, row, _resident((1, D)), _resident((FF, D)), _resident((FF, D)),
                  _resident((FF, D))],
        out_specs=[wide, wide, row, pl.BlockSpec((1, 1, D), lambda t: (t, 0, 0))],
        out_shape=[jax.ShapeDtypeStruct((T, FF), BF16)] * 2
                  + [jax.ShapeDtypeStruct((T, D), F32), jax.ShapeDtypeStruct((nt, 1, D), F32)],
        scratch_shapes=[], sem=("parallel",), name=name, carry=carry)


def _ffn_bwd_w(lhs, rhs, name, carry=None):
    T = rhs.shape[0]
    tf = 256

    def body(l_ref, r_ref, dw_ref):
        dw_ref[...] = lax.dot_general(l_ref[...], r_ref[...], TN, preferred_element_type=F32).astype(BF16)

    (dw,), got = _pallas(
        body, (lhs, rhs), grid=(FF // tf,),
        in_specs=[pl.BlockSpec((T, tf), lambda f: (0, f)), _resident((T, D))],
        out_specs=[pl.BlockSpec((tf, D), lambda f: (f, 0))], out_shape=[jax.ShapeDtypeStruct((FF, D), BF16)],
        scratch_shapes=[], sem=("parallel",), name=name, carry=carry)
    return dw, got


def _rms_bwd_rows(dn, xv, gain):
    r = lax.rsqrt(jnp.mean(xv * xv, axis=-1, keepdims=True) + EPS)
    xhat = xv * r
    dxhat = dn * gain
    dx = r * (dxhat - xhat * jnp.mean(dxhat * xhat, axis=-1, keepdims=True))
    return dx, jnp.sum(dn * xhat, axis=0, keepdims=True)


def _mix_out_bwd(dh, attn, conv, wout):
    T = dh.shape[0]
    tm = 512
    nt = T // tm

    def body(dh_ref, a_ref, c_ref, w_ref, da_ref, dc_ref, dw_ref, acc_scr):
        t = pl.program_id(0)

        @pl.when(t == 0)
        def _():
            acc_scr[...] = jnp.zeros_like(acc_scr)

        dhb = dh_ref[...].astype(BF16)
        dmix = lax.dot_general(dhb, w_ref[...], NT, preferred_element_type=F32)
        da_ref[...] = dmix[:, 0:DA].astype(BF16)
        dc_ref[...] = dmix[:, DA:D].astype(BF16)
        acc_scr[0:DA, :] += lax.dot_general(a_ref[...], dhb, TN, preferred_element_type=F32)
        acc_scr[DA:D, :] += lax.dot_general(c_ref[...], dhb, TN, preferred_element_type=F32)

        @pl.when(t == nt - 1)
        def _():
            dw_ref[...] = acc_scr[...].astype(BF16)

    row = pl.BlockSpec((tm, D), lambda t: (t, 0))
    half = pl.BlockSpec((tm, DA), lambda t: (t, 0))
    full = pl.BlockSpec((D, D), lambda t: (0, 0))
    return pl.pallas_call(
        body, grid=(nt,), in_specs=[row, half, half, full], out_specs=[half, half, full],
        out_shape=[jax.ShapeDtypeStruct((T, DA), BF16)] * 2 + [jax.ShapeDtypeStruct((D, D), BF16)],
        scratch_shapes=[pltpu.VMEM((D, D), F32)],
        compiler_params=_cp(("arbitrary",)), name="mix_out_bwd")(dh, attn, conv, wout)


def _mix_in_bwd(dparts, win, nb, h, dh, gain):
    T = h.shape[0]
    tm = 512
    nt = T // tm

    def body(d0, d1, d2, d3, d4, w_ref, n_ref, h_ref, dh_ref, gain_ref,
             dw_ref, dx_ref, dyb_ref, dg_ref, acc_scr):
        t = pl.program_id(0)

        @pl.when(t == 0)
        def _():
            acc_scr[...] = jnp.zeros_like(acc_scr)

        n = n_ref[...]
        dn = jnp.zeros((tm, D), F32)
        for i, d_ref in enumerate((d0, d1, d2, d3, d4)):
            dv = d_ref[...]
            dn = dn + jnp.dot(dv, w_ref[i * DA:(i + 1) * DA, :], preferred_element_type=F32)
            acc_scr[i * DA:(i + 1) * DA, :] += lax.dot_general(dv, n, TN, preferred_element_type=F32)
        dx, dgain = _rms_bwd_rows(dn, h_ref[...], gain_ref[...])
        tot = dh_ref[...] + dx
        dx_ref[...] = tot
        dyb_ref[...] = (0.5 * tot).astype(BF16)
        dg_ref[...] = dgain[None]

        @pl.when(t == nt - 1)
        def _():
            dw_ref[...] = acc_scr[...].astype(BF16)

    row = pl.BlockSpec((tm, D), lambda t: (t, 0))
    half = pl.BlockSpec((tm, DA), lambda t: (t, 0))
    full = pl.BlockSpec((DIN, D), lambda t: (0, 0))
    return pl.pallas_call(
        body, grid=(nt,),
        in_specs=[half] * 5 + [full, row, row, row, pl.BlockSpec((1, D), lambda t: (0, 0))],
        out_specs=[full, row, row, pl.BlockSpec((1, 1, D), lambda t: (t, 0, 0))],
        out_shape=[jax.ShapeDtypeStruct((DIN, D), BF16), jax.ShapeDtypeStruct((T, D), F32),
                   jax.ShapeDtypeStruct((T, D), BF16), jax.ShapeDtypeStruct((nt, 1, D), F32)],
        scratch_shapes=[pltpu.VMEM((DIN, D), F32)],
        compiler_params=_cp(("arbitrary",)), name="mix_in_bwd")(*dparts, win, nb, h, dh, gain)


def _head_masks():
    lane = lax.broadcasted_iota(jnp.int32, (1, 2 * HD), 1)
    m0 = lane < HD
    return m0, jnp.logical_not(m0)


def _stack_heads(v, m0, m1):
    z = jnp.zeros_like(v)
    return jnp.concatenate([jnp.where(m0, v, z), jnp.where(m1, v, z)], axis=0)


def _unstack_heads(v2, m0):
    return jnp.where(m0, v2[0:BLK], v2[BLK:2 * BLK])


def _head_sums(xv):
    ri = lax.broadcasted_iota(jnp.int32, (2 * HD, 2 * HD), 0)
    ci = lax.broadcasted_iota(jnp.int32, (2 * HD, 2 * HD), 1)
    ones = jnp.where((ri < HD) == (ci < HD), 1.0, 0.0).astype(BF16)
    hi = xv.astype(BF16)
    lo = (xv - hi.astype(F32)).astype(BF16)
    return (jnp.dot(hi, ones, preferred_element_type=F32) + jnp.dot(lo, ones, preferred_element_type=F32))


def _head_rms(xv):
    return lax.rsqrt(_head_sums(xv * xv) * (1.0 / HD) + EPS)


def _band_mask(first):
    qi = lax.broadcasted_iota(jnp.int32, (BLK, 2 * BLK), 0)
    ci = lax.broadcasted_iota(jnp.int32, (BLK, 2 * BLK), 1)
    band = (ci >= qi) & (ci <= qi + BLK)
    return band & ((ci >= BLK) | jnp.logical_not(first))


def _block_rows(j, d, seg):
    r, n = j // seg, j % seg
    start = r + (d * BLK) * n
    first = n == 0
    prev = jnp.where(first, start, start - d * BLK)
    return pl.ds(start, BLK, stride=d), pl.ds(prev, BLK, stride=d), first


def _block_keys(refs, cur, prev, first, single):
    if single:
        qi = lax.broadcasted_iota(jnp.int32, (BLK, BLK), 0)
        ci = lax.broadcasted_iota(jnp.int32, (BLK, BLK), 1)
        return [r[cur, :].astype(BF16) for r in refs], ci <= qi
    return ([jnp.concatenate([r[prev, :], r[cur, :]], axis=0).astype(BF16) for r in refs], _band_mask(first))


def _attn_fwd(u, qg2, kg2, B, S, carry=None):
    T = B * S
    NB = S // BLK
    scale = HD ** -0.5

    def body(q_ref, k_ref, v_ref, qg_ref, kg_ref, o_ref, lse_ref, qn, kn, vn, os_, ls_):
        m0, m1 = _head_masks()
        qv = q_ref[...].astype(F32)
        qn[...] = qv * _head_rms(qv) * (qg_ref[...] * scale)
        kv = k_ref[...].astype(F32)
        kn[...] = kv * _head_rms(kv) * kg_ref[...]
        vn[...] = v_ref[...].astype(F32)

        for i, d in enumerate(DILS):
            seg = NB // d

            def blk(j, c, i=i, d=d, seg=seg):
                cur, prev, first = _block_rows(j, d, seg)
                q2 = _stack_heads(qn[cur, :].astype(BF16), m0, m1)
                (kk, vv), mask = _block_keys((kn, vn), cur, prev, first, False)
                s = lax.dot_general(q2, kk, NT, preferred_element_type=F32)
                s = jnp.where(jnp.concatenate([mask, mask], axis=0), s, -1e30)
                mx = jnp.max(s, axis=-1, keepdims=True)
                p = jnp.exp(s - mx)
                l = jnp.sum(p, axis=-1, keepdims=True)
                o2 = jnp.dot((p * (1.0 / l)).astype(BF16), vv, preferred_element_type=F32)
                os_[i, cur, :] = _unstack_heads(o2, m0)
                ls_[i, cur, :] = _unstack_heads(mx + jnp.log(l), m0)
                return c

            lax.fori_loop(0, NB, blk, 0, unroll=8)

        def comb(c, carry):
            rows = pl.ds(pl.multiple_of(c * 256, 256), 256)
            l0, l1, l2 = ls_[0, rows, :], ls_[1, rows, :], ls_[2, rows, :]
            mx = jnp.maximum(jnp.maximum(l0, l1), l2)
            e0, e1, e2 = jnp.exp(l0 - mx), jnp.exp(l1 - mx), jnp.exp(l2 - mx)
            tot = e0 + e1 + e2
            inv = 1.0 / tot
            o = (e0 * os_[0, rows, :] + e1 * os_[1, rows, :] + e2 * os_[2, rows, :]) * inv
            o_ref[rows, :] = o.astype(BF16)
            lse_ref[rows, :] = mx + jnp.log(tot)
            return carry

        lax.fori_loop(0, S // 256, comb, 0)

    pair = 2 * HD
    blk_spec = lambda off: pl.BlockSpec((S, pair), lambda b, p, off=off: (b, off + p))
    gspec = pl.BlockSpec((1, pair), lambda b, p: (0, 0))
    return _pallas(
        body, (u, u, u, qg2, kg2), grid=(B, DA // pair),
        in_specs=[blk_spec(0), blk_spec(DA // pair), blk_spec(2 * DA // pair), gspec, gspec],
        out_specs=[blk_spec(0), blk_spec(0)],
        out_shape=[jax.ShapeDtypeStruct((T, DA), BF16), jax.ShapeDtypeStruct((T, DA), F32)],
        scratch_shapes=[pltpu.VMEM((S, pair), F32)] * 3 + [pltpu.VMEM((3, S, pair), F32)] * 2,
        sem=("parallel", "parallel"), name="attn_fwd", carry=carry)


def _attn_bwd(u, attn, dattn, lse, qg2, kg2, B, S, carry=None):
    T = B * S
    NB = S // BLK
    scale = HD ** -0.5
    pair = 2 * HD

    def body(q_ref, k_ref, v_ref, o_ref, do_ref, lse_ref, qg_ref, kg_ref,
             dq_ref, dk_ref, dv_ref, dgn_ref,
             qn, kn, vn, don, ldl, accq, acck, accv, rq, rk):
        m0, m1 = _head_masks()
        lane = lax.broadcasted_iota(jnp.int32, (1, pair), 1)
        qv = q_ref[...].astype(F32)
        rq[...] = _head_rms(qv)
        qn[...] = qv * rq[...] * (qg_ref[...] * scale)
        kv = k_ref[...].astype(F32)
        rk[...] = _head_rms(kv)
        kn[...] = kv * rk[...] * kg_ref[...]
        vn[...] = v_ref[...].astype(F32)
        dov = do_ref[...].astype(F32)
        don[...] = dov
        ldl[...] = jnp.where((lane % HD) < HD // 2, lse_ref[...], _head_sums(dov * o_ref[...].astype(F32)))

        for i, d in enumerate(DILS):
            seg = NB // d

            def blk(j, c, i=i, d=d, seg=seg):
                cur, prev, first = _block_rows(j, d, seg)
                q2 = _stack_heads(qn[cur, :].astype(BF16), m0, m1)
                do2 = _stack_heads(don[cur, :].astype(BF16), m0, m1)
                (kk, vv), mask = _block_keys((kn, vn), cur, prev, first, seg == 1)
                ldv = ldl[cur, :]
                lse2 = jnp.concatenate([ldv[:, 0:1], ldv[:, HD:HD + 1]], axis=0)
                dl2 = jnp.concatenate([ldv[:, HD // 2:HD // 2 + 1], ldv[:, HD + HD // 2:HD + HD // 2 + 1]], axis=0)
                s = lax.dot_general(q2, kk, NT, preferred_element_type=F32)
                p = jnp.where(jnp.concatenate([mask, mask], axis=0), jnp.exp(s - lse2), 0.0)
                dp = lax.dot_general(do2, vv, NT, preferred_element_type=F32)
                ds = (p * (dp - dl2)).astype(BF16)
                dq_acc = _unstack_heads(jnp.dot(ds, kk, preferred_element_type=F32), m0)
                dk_acc = lax.dot_general(ds, q2, TN, preferred_element_type=F32)
                dv_acc = lax.dot_general(p.astype(BF16), do2, TN, preferred_element_type=F32)
                if i == 0:
                    accq[cur, :] = dq_acc
                    acck[cur, :] = dk_acc[BLK:2 * BLK]
                    accv[cur, :] = dv_acc[BLK:2 * BLK]
                    acck[prev, :] += dk_acc[0:BLK]
                    accv[prev, :] += dv_acc[0:BLK]
                elif seg == 1:
                    accq[cur, :] += dq_acc
                    acck[cur, :] += dk_acc
                    accv[cur, :] += dv_acc
                else:
                    accq[cur, :] += dq_acc
                    acck[prev, :] += dk_acc[0:BLK]
                    acck[cur, :] += dk_acc[BLK:2 * BLK]
                    accv[prev, :] += dv_acc[0:BLK]
                    accv[cur, :] += dv_acc[BLK:2 * BLK]
                return c

            lax.fori_loop(0, NB, blk, 0, unroll=8)

        def norm_bwd(x_ref, r_ref, dn, gain):
            r = r_ref[...]
            xhat = x_ref[...].astype(F32) * r
            dxhat = dn * gain
            dx = r * (dxhat - xhat * (_head_sums(dxhat * xhat) * (1.0 / HD)))
            return dx, jnp.sum(dn * xhat, axis=0, keepdims=True)

        dq, dgq = norm_bwd(q_ref, rq, accq[...], qg_ref[...] * scale)
        dk, dgk = norm_bwd(k_ref, rk, acck[...], kg_ref[...])
        dq_ref[...] = dq.astype(BF16)
        dk_ref[...] = dk.astype(BF16)
        dv_ref[...] = accv[...].astype(BF16)
        dgn_ref[...] = jnp.concatenate([dgq * scale, dgk, jnp.zeros((6, pair), F32)], axis=0)[None]

    blk_spec = lambda off: pl.BlockSpec((S, pair), lambda b, p, off=off: (b, off + p))
    gspec = pl.BlockSpec((1, pair), lambda b, p: (0, 0))
    np_ = DA // pair
    return _pallas(
        body, (u, u, u, attn, dattn, lse, qg2, kg2), grid=(B, np_),
        in_specs=[blk_spec(0), blk_spec(np_), blk_spec(2 * np_), blk_spec(0), blk_spec(0), blk_spec(0),
                  gspec, gspec],
        out_specs=[blk_spec(0), blk_spec(0), blk_spec(0),
                   pl.BlockSpec((1, 8, pair), lambda b, p: (b * np_ + p, 0, 0))],
        out_shape=[jax.ShapeDtypeStruct((T, DA), BF16)] * 3 + [jax.ShapeDtypeStruct((B * np_, 8, pair), F32)],
        scratch_shapes=[pltpu.VMEM((S, pair), F32)] * 10,
        sem=("parallel", "parallel"), name="attn_bwd", carry=carry)


CT = 32
CPAD = 32


def _shifted(win, offsets):
    rolled, out = {}, {}
    n = win.shape[0]
    for o in offsets:
        sub = o % 8
        if sub not in rolled:
            rolled[sub] = win if sub == 0 else pltpu.roll(win, n - sub, 0)
        out[o] = rolled[sub][o - sub:o - sub + CT, :]
    return out


def _ln_fwd(y, g, b):
    mu = jnp.mean(y, axis=-1, keepdims=True)
    yc = y - mu
    rstd = lax.rsqrt(jnp.mean(yc * yc, axis=-1, keepdims=True) + EPS)
    xhat = yc * rstd
    return xhat, rstd, xhat * g + b


def _fill_glu(ca_ref, cg_ref, glu, S):
    glu[pl.ds(0, CPAD), :] = jnp.zeros((CPAD, DC), F32)

    def fill(i, c):
        rows = pl.ds(pl.multiple_of(i * 256, 256), 256)
        a = ca_ref[rows, :].astype(F32)
        gt = cg_ref[rows, :].astype(F32)
        glu[pl.ds(pl.multiple_of(CPAD + i * 256, CT), 256), :] = a * _sigmoid(gt)
        return c

    lax.fori_loop(0, S // 256, fill, 0)


def _conv_fwd(u, cw, cb, lg, lb, B, S):
    T = B * S

    def body(ca_ref, cg_ref, w_ref, b_ref, lg_ref, lb_ref, o_ref, y_ref, glu):
        _fill_glu(ca_ref, cg_ref, glu, S)

        def step(i, c):
            t0 = pl.multiple_of(i * CT, CT)
            win = glu[pl.ds(t0, 2 * CT), :]
            acc = jnp.zeros((CT, DC), F32) + b_ref[...]
            taps = _shifted(win, [k + 2 for k in range(CK)])
            for k in range(CK):
                acc = acc + taps[k + 2] * w_ref[k:k + 1, :]
            y_ref[pl.ds(t0, CT), :] = acc
            _, _, z = _ln_fwd(acc, lg_ref[...], lb_ref[...])
            o_ref[pl.ds(t0, CT), :] = (z * _sigmoid(z)).astype(BF16)
            return c

        lax.fori_loop(0, S // CT, step, 0, unroll=2)

    vec = pl.BlockSpec((1, DC), lambda b: (0, 0))
    return pl.pallas_call(
        body, grid=(B,),
        in_specs=[pl.BlockSpec((S, DC), lambda b: (b, 3)), pl.BlockSpec((S, DC), lambda b: (b, 4)),
                  pl.BlockSpec((CT, DC), lambda b: (0, 0)), vec, vec, vec],
        out_specs=[pl.BlockSpec((S, DC), lambda b: (b, 0))] * 2,
        out_shape=[jax.ShapeDtypeStruct((T, DC), BF16), jax.ShapeDtypeStruct((T, DC), F32)],
        scratch_shapes=[pltpu.VMEM((CPAD + S, DC), F32)],
        compiler_params=_cp(("parallel",)), name="conv_fwd")(u, u, cw, cb, lg, lb)


def _conv_bwd(u, y, dconv, cw, lg, lb, B, S):
    T = B * S

    def body(ca_ref, cg_ref, y_ref, dc_ref, w_ref, lg_ref, lb_ref,
             dca_ref, dcg_ref, dw_ref, ds_ref, glu, dyp, dwacc):
        _fill_glu(ca_ref, cg_ref, glu, S)
        dyp[pl.ds(S, CPAD), :] = jnp.zeros((CPAD, DC), F32)
        lgv, lbv = lg_ref[...], lb_ref[...]

        def sum8(v):
            return functools.reduce(jnp.add, [v[r:r + 8] for r in range(0, v.shape[0], 8)])

        P1 = 4 * CT

        def p1(i, carry):
            sb, sg, sl = carry
            t0 = pl.multiple_of(i * P1, P1)
            xhat, rstd, z = _ln_fwd(y_ref[pl.ds(t0, P1), :], lgv, lbv)
            sz = _sigmoid(z)
            dz = dc_ref[pl.ds(t0, P1), :].astype(F32) * (sz * (1.0 + z * (1.0 - sz)))
            dxhat = dz * lgv
            dy = rstd * (dxhat - jnp.mean(dxhat, axis=-1, keepdims=True)
                         - xhat * jnp.mean(dxhat * xhat, axis=-1, keepdims=True))
            dyp[pl.ds(t0, P1), :] = dy
            return sb + sum8(dy), sg + sum8(dz * xhat), sl + sum8(dz)

        z8 = jnp.zeros((8, DC), F32)
        sb, sg, sl = lax.fori_loop(0, S // P1, p1, (z8, z8, z8))
        rs = lambda v: jnp.sum(v, axis=0, keepdims=True)
        ds_ref[...] = jnp.concatenate([rs(sb), rs(sg), rs(sl), jnp.zeros((5, DC), F32)], axis=0)[None]

        def p2(i, c):
            t0 = pl.multiple_of(i * CT, CT)
            win = dyp[pl.ds(t0, 2 * CT), :]
            acc = jnp.zeros((CT, DC), F32)
            taps = _shifted(win, [30 - k for k in range(CK)])
            for k in range(CK):
                acc = acc + taps[30 - k] * w_ref[k:k + 1, :]
            a = ca_ref[pl.ds(t0, CT), :].astype(F32)
            sgt = _sigmoid(cg_ref[pl.ds(t0, CT), :].astype(F32))
            dca_ref[pl.ds(t0, CT), :] = (acc * sgt).astype(BF16)
            dcg_ref[pl.ds(t0, CT), :] = (acc * a * sgt * (1.0 - sgt)).astype(BF16)
            return c

        lax.fori_loop(0, S // CT, p2, 0)

        dwacc[...] = jnp.zeros_like(dwacc)

        def p3(i, c):
            t0 = pl.multiple_of(i * CT, CT)
            win = glu[pl.ds(t0, 2 * CT), :]
            dy = dyp[pl.ds(t0, CT), :]
            for k in range(CK):
                dwacc[k] += sum8(dy * win[k + 2:k + 2 + CT, :])
            return c

        lax.fori_loop(0, S // CT, p3, 0)
        dw_ref[...] = jnp.sum(dwacc[...], axis=1)[None]

    vec = pl.BlockSpec((1, DC), lambda b: (0, 0))
    seq = pl.BlockSpec((S, DC), lambda b: (b, 0))
    return pl.pallas_call(
        body, grid=(B,),
        in_specs=[pl.BlockSpec((S, DC), lambda b: (b, 3)), pl.BlockSpec((S, DC), lambda b: (b, 4)),
                  seq, seq, pl.BlockSpec((CT, DC), lambda b: (0, 0)), vec, vec],
        out_specs=[seq, seq, pl.BlockSpec((1, CT, DC), lambda b: (b, 0, 0)),
                   pl.BlockSpec((1, 8, DC), lambda b: (b, 0, 0))],
        out_shape=[jax.ShapeDtypeStruct((T, DC), BF16)] * 2
                  + [jax.ShapeDtypeStruct((B, CT, DC), F32), jax.ShapeDtypeStruct((B, 8, DC), F32)],
        scratch_shapes=[pltpu.VMEM((CPAD + S, DC), F32), pltpu.VMEM((S + CPAD, DC), F32),
                        pltpu.VMEM((CT, 8, DC), F32)],
        compiler_params=_cp(("parallel",)), name="conv_bwd")(u, u, y, dconv, cw, lg, lb)


def _local_step(x, target, norms, W, B, S, comm=None):
    qg2 = jnp.concatenate([norms["q_norm"], norms["q_norm"]], axis=1)
    kg2 = jnp.concatenate([norms["k_norm"], norms["k_norm"]], axis=1)
    cw = jnp.concatenate([W["conv_w"], jnp.zeros((1, DC), F32)], axis=0)

    W = dict(W)
    (n1, g1, u1, act1), got = _ffn_gate_up(x, norms["ffn1_norm"], W["wg1"], W["wu1"], "ffn1_gate_up",
                                           carry=comm.gathers["down_in"] if comm else None)
    if comm:
        W.update(comm.gathered("down_in", got))
    h1, u, n2 = _ffn_down_mix_in(x, act1, W["wd1"], norms["mix_norm"], W["win"], "ffn1_down_mix_in")
    (attn, lse), got = _attn_fwd(u, qg2, kg2, B, S, carry=comm.gathers["ffn2"] if comm else None)
    if comm:
        W = dict(W, **comm.gathered("ffn2", got))
    conv, y = _conv_fwd(u, cw, norms["conv_b"], norms["conv_ln_g"], norms["conv_ln_b"], B, S)
    h2, n3, g2, u2, dout, dyb, sq, act2 = _mix_out_ffn_loss(h1, attn, conv, W["wout"], norms["ffn2_norm"],
                                                            W["wg2"], W["wu2"], W["wd2"], target, "ffn2_fwd")
    loss = (0.5 / D) * jnp.sum(sq)

    (dg2, du2, dh2, dgn_ffn2), _ = _ffn_bwd_act(dyb, g2, u2, h2, dout, norms["ffn2_norm"],
                                               W["wg2"], W["wu2"], W["wd2"], "ffn2_bwd_act")
    dwd2, _ = _ffn_bwd_w(act2, dyb, "ffn2_bwd_wd")
    dwg2, _ = _ffn_bwd_w(dg2, n3, "ffn2_bwd_wg")
    dwu2, _ = _ffn_bwd_w(du2, n3, "ffn2_bwd_wu")
    dattn, dconv, dwout = _mix_out_bwd(dh2, attn, conv, W["wout"])
    carry = comm.reduce_start("ffn2", {"wg2": dwg2, "wu2": dwu2, "wd2": dwd2, "wout": dwout}) if comm else None
    (dq, dk, dv, dgn_qk), got = _attn_bwd(u, attn, dattn, lse, qg2, kg2, B, S, carry=carry)
    if comm:
        comm.reduce_done(carry, got)
    dca, dcg, dcw, dcs = _conv_bwd(u, y, dconv, cw, norms["conv_ln_g"], norms["conv_ln_b"], B, S)
    dwin, dh1, dyb1, dgn_mix = _mix_in_bwd((dq, dk, dv, dca, dcg), W["win"], n2, h1, dh2, norms["mix_norm"])
    (dg1, du1, gx, dgn_ffn1), _ = _ffn_bwd_act(dyb1, g1, u1, x, dh1, norms["ffn1_norm"],
                                              W["wg1"], W["wu1"], W["wd1"], "ffn1_bwd_act")
    carry = comm.reduce_start("win", {"win": dwin}) if comm else None
    dwd1, got = _ffn_bwd_w(act1, dyb1, "ffn1_bwd_wd", carry=carry)
    if comm:
        comm.reduce_done(carry, got)
        carry = comm.reduce_start("wd1", {"wd1": dwd1})
    dwg1, got = _ffn_bwd_w(dg1, n1, "ffn1_bwd_wg", carry=carry)
    if comm:
        comm.reduce_done(carry, got)
        carry = comm.reduce_start("wg1", {"wg1": dwg1})
    dwu1, got = _ffn_bwd_w(du1, n1, "ffn1_bwd_wu", carry=carry)
    if comm:
        comm.reduce_done(carry, got)
        comm.last = comm.reduce_start("wu1", {"wu1": dwu1})

    qk = jnp.sum(dgn_qk, axis=0)
    cs = jnp.sum(dcs, axis=0)
    small = {
        "ffn1_norm": jnp.sum(dgn_ffn1, axis=0),
        "mix_norm": jnp.sum(dgn_mix, axis=0),
        "q_norm": qk[0:1, 0:HD] + qk[0:1, HD:2 * HD],
        "k_norm": qk[1:2, 0:HD] + qk[1:2, HD:2 * HD],
        "conv_w": jnp.sum(dcw, axis=0)[0:CK],
        "conv_b": cs[0:1],
        "conv_ln_g": cs[1:2],
        "conv_ln_b": cs[2:3],
        "ffn2_norm": jnp.sum(dgn_ffn2, axis=0),
    }
    big = {"wg1": dwg1, "wu1": dwu1, "wd1": dwd1, "win": dwin, "wout": dwout,
           "wg2": dwg2, "wu2": dwu2, "wd2": dwd2}
    return loss, gx, big, small


HBM = pl.BlockSpec(memory_space=pltpu.HBM)
VMEM = pl.BlockSpec(memory_space=pltpu.VMEM)


def _place():
    return lax.axis_index("x"), lax.axis_index("y"), lax.axis_index("c")


class _GatherCarry:
    def __init__(self, shards, mid_at=0.5):
        nt = len(shards)
        self.mid_at = mid_at
        self.shards = shards
        self.in_arrays = [s for s, _ in shards]
        self.in_specs = [VMEM] * nt
        self.out_shape = [jax.ShapeDtypeStruct((NDEV * s.shape[0], s.shape[1]), dt) for s, dt in shards]
        self.out_specs = [HBM] * nt
        self.scratch = ([pltpu.VMEM(s.shape, dt) for s, dt in shards]
                        + [pltpu.SemaphoreType.DMA((nt, 7)), pltpu.SemaphoreType.DMA((nt, 7)),
                           pltpu.SemaphoreType.DMA((nt,))])

    def _copies(self, outs, scr):
        nt = len(self.shards)
        stages = scr[:nt]
        send_sems, recv_sems, local_sems = scr[nt:]
        x, y, c = _place()
        me, sibling = (x, y, c), (x, y, 1 - c)
        xn, yn, diag = (1 - x, y, c), (x, 1 - y, c), (1 - x, 1 - y, c)
        via = (x ^ c, y ^ (1 - c), c)
        onto = (x ^ (1 - c), y ^ c, c)

        def rows(t, px, py, pc):
            r = self.shards[t][0].shape[0]
            return outs[t].at[pl.ds((4 * px + 2 * py + pc) * r, r), :]

        def copy(t, k, block, to, src=None):
            return pltpu.make_async_remote_copy(
                src_ref=rows(t, *block) if src is None else src, dst_ref=rows(t, *block),
                send_sem=send_sems.at[t, k], recv_sem=recv_sems.at[t, k],
                device_id=to, device_id_type=MESH)

        sib = lambda b: (b[0], b[1], 1 - c)
        return dict(
            local=[pltpu.make_async_copy(stages[t], rows(t, *me), local_sems.at[t]) for t in range(nt)],
            own=[[copy(t, 0, me, sibling, src=stages[t]), copy(t, 1, me, xn, src=stages[t]),
                  copy(t, 2, me, yn, src=stages[t])] for t in range(nt)],
            relay=[copy(t, 3, via, onto) for t in range(nt)],
            down=[[copy(t, 4, xn, sibling), copy(t, 5, yn, sibling)] for t in range(nt)],
            down_diag=[copy(t, 6, diag, sibling) for t in range(nt)],
            got_xy=[[copy(t, 1, xn, me), copy(t, 2, yn, me)] for t in range(nt)],
            got_diag=[copy(t, 3, diag, me) for t in range(nt)],
            got_sib=[[copy(t, 0, sibling, me), copy(t, 4, sib(xn), me), copy(t, 5, sib(yn), me),
                      copy(t, 6, sib(diag), me)] for t in range(nt)])

    def start(self, ins, outs, scr):
        cps = self._copies(outs, scr)
        for t, (_, dt) in enumerate(self.shards):
            scr[t][...] = ins[t][...].astype(dt)
            for cp in [cps["local"][t]] + cps["own"][t]:
                cp.start()

    def mid(self, ins, outs, scr):
        cps = self._copies(outs, scr)
        for t in range(len(self.shards)):
            for cp in cps["got_xy"][t]:
                cp.wait_recv()
            for cp in [cps["relay"][t]] + cps["down"][t]:
                cp.start()

    def finish(self, ins, outs, scr):
        cps = self._copies(outs, scr)
        for t in range(len(self.shards)):
            cps["got_diag"][t].wait_recv()
            cps["down_diag"][t].start()
        for t in range(len(self.shards)):
            for cp in cps["got_sib"][t]:
                cp.wait_recv()
            for cp in cps["own"][t] + [cps["relay"][t]] + cps["down"][t] + [cps["down_diag"][t]]:
                cp.wait_send()
            cps["local"][t].wait()


def _run_carry(carry, name):
    def body(*refs):
        n_in, n_out = len(carry.in_arrays), len(carry.out_shape)
        ins, outs, scr = refs[:n_in], refs[n_in:n_in + n_out], refs[n_in + n_out:]
        carry.start(ins, outs, scr)
        if hasattr(carry, "mid"):
            carry.mid(ins, outs, scr)
        carry.finish(ins, outs, scr)

    return pl.pallas_call(
        body, in_specs=carry.in_specs, out_specs=carry.out_specs, out_shape=carry.out_shape,
        scratch_shapes=carry.scratch, compiler_params=pltpu.CompilerParams(vmem_limit_bytes=VMEM_LIMIT),
        name=name)(*carry.in_arrays)


def _sibling_reduce(grads, name):
    nt = len(grads)
    g4 = [g.reshape(4, 2, g.shape[0] // NDEV, g.shape[1]) for g in grads]

    def body(*refs):
        ins, outs = refs[:nt], refs[nt:2 * nt]
        recv, own = refs[2 * nt:3 * nt], refs[3 * nt:4 * nt]
        send_sems, recv_sems, load_sems, store_sems = refs[4 * nt:]
        x, y, c = _place()
        sends = [pltpu.make_async_remote_copy(
            src_ref=ins[t].at[:, 1 - c], dst_ref=recv[t], send_sem=send_sems.at[t], recv_sem=recv_sems.at[t],
            device_id=(x, y, 1 - c), device_id_type=MESH) for t in range(nt)]
        loads = [pltpu.make_async_copy(ins[t].at[:, c], own[t], load_sems.at[t]) for t in range(nt)]
        stores = [pltpu.make_async_copy(own[t], outs[t], store_sems.at[t]) for t in range(nt)]
        for cp in sends + loads:
            cp.start()
        for t in range(nt):
            loads[t].wait()
            sends[t].wait_recv()
            for q in range(4):
                own[t][q] = (own[t][q].astype(F32) + recv[t][q].astype(F32)).astype(BF16)
            stores[t].start()
        for t in range(nt):
            sends[t].wait_send()
            stores[t].wait()

    blocks = [(4,) + g.shape[2:] for g in g4]
    return pl.pallas_call(
        body, in_specs=[HBM] * nt, out_specs=[HBM] * nt,
        out_shape=[jax.ShapeDtypeStruct(b, BF16) for b in blocks],
        scratch_shapes=[pltpu.VMEM(b, BF16) for b in blocks] * 2 + [pltpu.SemaphoreType.DMA((nt,))] * 4,
        compiler_params=pltpu.CompilerParams(vmem_limit_bytes=VMEM_LIMIT), name=name)(*g4)


class _ExchangeCarry:
    def __init__(self, names, parts, mid_at=0.5):
        nt = len(parts)
        self.mid_at = mid_at
        self.names = names
        self.in_arrays = list(parts)
        self.in_specs = [HBM] * nt
        self.out_shape = [jax.ShapeDtypeStruct((3,) + p.shape[1:], BF16) for p in parts]
        self.out_specs = [HBM] * nt
        self.scratch = ([pltpu.VMEM(p.shape[1:], BF16) for p in parts] * 2
                        + [pltpu.SemaphoreType.DMA((nt, 3)), pltpu.SemaphoreType.DMA((nt, 3)),
                           pltpu.SemaphoreType.DMA((nt,)), pltpu.SemaphoreType.DMA((nt,))])

    def _copies(self, ins, outs, scr):
        nt = len(ins)
        relayed, mine = scr[:nt], scr[nt:2 * nt]
        send_sems, recv_sems, local_sems, load_sems = scr[2 * nt:]
        x, y, c = _place()
        q = lambda cx, cy: 2 * cx + cy
        near, far = (x ^ c, y ^ (1 - c)), (x ^ (1 - c), y ^ c)

        def remote(t, k, src, dst, chip):
            return pltpu.make_async_remote_copy(
                src_ref=src, dst_ref=dst, send_sem=send_sems.at[t, k], recv_sem=recv_sems.at[t, k],
                device_id=(*chip, c), device_id_type=MESH)

        return dict(
            keep=[pltpu.make_async_copy(ins[t].at[q(x, y)], outs[t].at[0], local_sems.at[t]) for t in range(nt)],
            load=[pltpu.make_async_copy(ins[t].at[q(*far)], mine[t], load_sems.at[t]) for t in range(nt)],
            direct=[remote(t, 0, ins[t].at[q(*near)], outs[t].at[1], near) for t in range(nt)],
            relay=[remote(t, 1, ins[t].at[q(1 - x, 1 - y)], relayed[t], near) for t in range(nt)],
            merged=[remote(t, 2, mine[t], outs[t].at[2], far) for t in range(nt)],
            relayed=relayed, mine=mine)

    def start(self, ins, outs, scr):
        cps = self._copies(ins, outs, scr)
        for t in range(len(ins)):
            for kind in ("keep", "load", "direct", "relay"):
                cps[kind][t].start()

    def mid(self, ins, outs, scr):
        cps = self._copies(ins, outs, scr)
        for t in range(len(ins)):
            cps["load"][t].wait()
            cps["relay"][t].wait_recv()
            cps["mine"][t][...] = (cps["mine"][t][...].astype(F32) + cps["relayed"][t][...].astype(F32)).astype(BF16)
            cps["merged"][t].start()

    def finish(self, ins, outs, scr):
        cps = self._copies(ins, outs, scr)
        for t in range(len(ins)):
            cps["direct"][t].wait()
            cps["relay"][t].wait_send()
            cps["merged"][t].wait()
            cps["keep"][t].wait()


class _Comm:
    def __init__(self, groups):
        self.names = {tag: list(g) for tag, (g, _) in groups.items()}
        self.gathers = {tag: _GatherCarry(list(g.values()), mid_at) for tag, (g, mid_at) in groups.items()}
        self.reduced = {}

    def gathered(self, tag, outs):
        return dict(zip(self.names[tag], outs))

    def reduce_start(self, tag, grads, mid_at=0.5):
        names = list(grads)
        parts = _sibling_reduce([grads[n] for n in names], "sibling_reduce_" + tag)
        return _ExchangeCarry(names, parts, mid_at)

    def reduce_done(self, carry, outs):
        self.reduced.update(zip(carry.names, outs))


def _adamw_math(w, g, m, v):
    m = B1 * m + (1.0 - B1) * g
    v = B2 * v + (1.0 - B2) * (g * g)
    m_hat = m / (1.0 - B1 ** STEP)
    v_hat = v / (1.0 - B2 ** STEP)
    delta = -LR * (m_hat / (jnp.sqrt(v_hat) + AEPS) + WD * w)
    return delta, m, v


def _adamw_big(recv, w, m, v, name):
    def body(r_ref, w_ref, m_ref, v_ref, g_ref, d_ref, mo_ref, vo_ref):
        g = r_ref[0].astype(F32)
        for q in range(1, 3):
            g = g + r_ref[q].astype(F32)
        d, mn, vn = _adamw_math(w_ref[...], g, m_ref[...], v_ref[...])
        g_ref[...] = g
        d_ref[...] = d
        mo_ref[...] = mn
        vo_ref[...] = vn

    rows, n = w.shape
    tr = rows // 2
    row = pl.BlockSpec((tr, n), lambda t: (t, 0))
    return pl.pallas_call(
        body, grid=(2,), in_specs=[pl.BlockSpec((3, tr, n), lambda t: (0, t, 0)), row, row, row],
        out_specs=[row] * 4, out_shape=[jax.ShapeDtypeStruct(w.shape, F32)] * 4,
        compiler_params=_cp(("parallel",)), name=name)(recv, w, m, v)


SMALL_NAMES = ("ffn1_norm", "mix_norm", "ffn2_norm", "conv_b", "conv_ln_g", "conv_ln_b", "q_norm", "k_norm")
SROWS = 16
LOSS_ROW = len(SMALL_NAMES)
CWF = 2


def _small_step(gs, loss_row, gcw, ws, ms, vs, wcw, mcw, vcw, carry=None):
    ns = len(SMALL_NAMES)
    widths = [g.shape[1] for g in gs]

    def body(*refs):
        it = iter(refs)
        take = lambda n: [next(it) for _ in range(n)]
        g_refs, (loss_ref, gcw_ref) = take(ns), take(2)
        w_refs, m_refs, v_refs = take(ns), take(ns), take(ns)
        wcw_ref, mcw_ref, vcw_ref = take(3)
        cins = take(len(carry.in_arrays)) if carry else []
        outs = [take(4) for _ in range(ns)]
        cw_outs, (loss_out,) = take(4), take(1)
        couts = take(len(carry.out_shape)) if carry else []
        send, slots, cslots, send_sems, recv_sems, csend_sems, crecv_sems = take(7)
        cscr = list(it)
        x, y, c = _place()
        me = 4 * x + 2 * y + c
        send[...] = jnp.zeros_like(send)
        for k in range(ns):
            send[k:k + 1, 0:widths[k]] = g_refs[k][...]
        send[LOSS_ROW:LOSS_ROW + 1, 0:128] = loss_ref[...]
        slots[me] = send[...]
        cslots[me] = gcw_ref[...]
        cps = []
        for k in range(1, NDEV):
            peer = (x ^ ((k >> 2) & 1), y ^ ((k >> 1) & 1), c ^ (k & 1))
            cps.append(pltpu.make_async_remote_copy(
                src_ref=send, dst_ref=slots.at[me], send_sem=send_sems.at[k - 1], recv_sem=recv_sems.at[k - 1],
                device_id=peer, device_id_type=MESH))
            cps.append(pltpu.make_async_remote_copy(
                src_ref=gcw_ref, dst_ref=cslots.at[me], send_sem=csend_sems.at[k - 1],
                recv_sem=crecv_sems.at[k - 1], device_id=peer, device_id_type=MESH))
        for cp in cps:
            cp.start()
        if carry:
            carry.start(cins, couts, cscr)
        for cp in cps:
            cp.wait()
        if carry:
            carry.mid(cins, couts, cscr)
        tot = slots[0]
        ctot = cslots[0, me]
        for j in range(1, NDEV):
            tot = tot + slots[j]
            ctot = ctot + cslots[j, me]

        def step(g, w_ref, m_ref, v_ref, o):
            d, mn, vn = _adamw_math(w_ref[...], g, m_ref[...], v_ref[...])
            o[0][...], o[1][...], o[2][...], o[3][...] = g, d, mn, vn

        for k in range(ns):
            step(tot[k:k + 1, 0:widths[k]], w_refs[k], m_refs[k], v_refs[k], outs[k])
        step(ctot, wcw_ref, mcw_ref, vcw_ref, cw_outs)
        loss_out[...] = tot[LOSS_ROW:LOSS_ROW + 1, 0:128]
        if carry:
            carry.finish(cins, couts, cscr)

    args = [*gs, loss_row, gcw, *ws, *ms, *vs, wcw, mcw, vcw]
    out_shape = ([jax.ShapeDtypeStruct((1, n), F32) for n in widths for _ in range(4)]
                 + [jax.ShapeDtypeStruct((CWF, D), F32)] * 4 + [jax.ShapeDtypeStruct((1, 128), F32)])
    n_own = len(out_shape)
    res = pl.pallas_call(
        body, in_specs=[VMEM] * len(args) + (carry.in_specs if carry else []),
        out_specs=[VMEM] * n_own + (carry.out_specs if carry else []),
        out_shape=out_shape + (carry.out_shape if carry else []),
        scratch_shapes=[pltpu.VMEM((SROWS, D), F32), pltpu.VMEM((NDEV, SROWS, D), F32),
                        pltpu.VMEM((NDEV, NDEV, CWF, D), F32)]
                       + [pltpu.SemaphoreType.DMA((NDEV - 1,))] * 4 + (carry.scratch if carry else []),
        name="small_step")(*args, *(carry.in_arrays if carry else []))
    per = [res[4 * k:4 * k + 4] for k in range(ns)]
    return per, res[4 * ns:4 * ns + 4], res[n_own - 1], res[n_own:]


def _pack_cw(a):
    flat = a.reshape(a.shape[:-2] + (CK * HD,))
    pad = [(0, 0)] * (flat.ndim - 1) + [(0, CWF * D - CK * HD)]
    return jnp.pad(flat, pad).reshape(a.shape[:-2] + (CWF, D))


def _unpack_cw(v):
    return v.reshape(-1)[:CK * HD].reshape(1, CK, HD)


def kernel(x, ffn1_norm, ffn1_w_gate, ffn1_w_up, ffn1_w_down, mix_norm, w_in, q_norm, k_norm, conv_w, conv_b, conv_ln_g, conv_ln_b, w_out, ffn2_norm, ffn2_w_gate, ffn2_w_up, ffn2_w_down, loss_target, m_ffn1_norm, m_ffn1_w_gate, m_ffn1_w_up, m_ffn1_w_down, m_mix_norm, m_w_in, m_q_norm, m_k_norm, m_conv_w, m_conv_b, m_conv_ln_g, m_conv_ln_b, m_w_out, m_ffn2_norm, m_ffn2_w_gate, m_ffn2_w_up, m_ffn2_w_down, v_ffn1_norm, v_ffn1_w_gate, v_ffn1_w_up, v_ffn1_w_down, v_mix_norm, v_w_in, v_q_norm, v_k_norm, v_conv_w, v_conv_b, v_conv_ln_g, v_conv_ln_b, v_w_out, v_ffn2_norm, v_ffn2_w_gate, v_ffn2_w_up, v_ffn2_w_down):
    P = dict(ffn1_norm=ffn1_norm, ffn1_w_gate=ffn1_w_gate, ffn1_w_up=ffn1_w_up, ffn1_w_down=ffn1_w_down,
             mix_norm=mix_norm, w_in=w_in, q_norm=q_norm, k_norm=k_norm, conv_w=conv_w, conv_b=conv_b,
             conv_ln_g=conv_ln_g, conv_ln_b=conv_ln_b, w_out=w_out, ffn2_norm=ffn2_norm,
             ffn2_w_gate=ffn2_w_gate, ffn2_w_up=ffn2_w_up, ffn2_w_down=ffn2_w_down)
    M = dict(ffn1_norm=m_ffn1_norm, ffn1_w_gate=m_ffn1_w_gate, ffn1_w_up=m_ffn1_w_up, ffn1_w_down=m_ffn1_w_down,
             mix_norm=m_mix_norm, w_in=m_w_in, q_norm=m_q_norm, k_norm=m_k_norm, conv_w=m_conv_w, conv_b=m_conv_b,
             conv_ln_g=m_conv_ln_g, conv_ln_b=m_conv_ln_b, w_out=m_w_out, ffn2_norm=m_ffn2_norm,
             ffn2_w_gate=m_ffn2_w_gate, ffn2_w_up=m_ffn2_w_up, ffn2_w_down=m_ffn2_w_down)
    V = dict(ffn1_norm=v_ffn1_norm, ffn1_w_gate=v_ffn1_w_gate, ffn1_w_up=v_ffn1_w_up, ffn1_w_down=v_ffn1_w_down,
             mix_norm=v_mix_norm, w_in=v_w_in, q_norm=v_q_norm, k_norm=v_k_norm, conv_w=v_conv_w, conv_b=v_conv_b,
             conv_ln_g=v_conv_ln_g, conv_ln_b=v_conv_ln_b, w_out=v_w_out, ffn2_norm=v_ffn2_norm,
             ffn2_w_gate=v_ffn2_w_gate, ffn2_w_up=v_ffn2_w_up, ffn2_w_down=v_ffn2_w_down)
    order = ["ffn1_norm", "ffn1_w_gate", "ffn1_w_up", "ffn1_w_down", "mix_norm", "w_in", "q_norm", "k_norm",
             "conv_w", "conv_b", "conv_ln_g", "conv_ln_b", "w_out", "ffn2_norm", "ffn2_w_gate", "ffn2_w_up",
             "ffn2_w_down"]
    B, S, _ = x.shape
    T = B * S

    bigs = [("wg1", "ffn1_w_gate", True), ("wu1", "ffn1_w_up", True), ("wd1", "ffn1_w_down", False),
            ("win", "w_in", True), ("wout", "w_out", False),
            ("wg2", "ffn2_w_gate", True), ("wu2", "ffn2_w_up", True), ("wd2", "ffn2_w_down", False)]
    hm = lambda a, tr: jnp.transpose(a[0]) if tr else a[0]
    cw_pad = jnp.zeros((32, 128), F32).at[0:CK, 0:HD].set(conv_w[0])
    shard = {ln: (hm(P[pn], tr), BF16) for ln, pn, tr in bigs}
    gathered = _run_carry(_GatherCarry([shard["wg1"], shard["wu1"], (cw_pad, F32)]), "gather_first")
    W = {"wg1": gathered[0], "wu1": gathered[1]}
    cwg = gathered[2].reshape(NDEV, 32, 128)[:, 0:CK, 0:HD]
    W["conv_w"] = jnp.transpose(cwg, (1, 0, 2)).reshape(CK, DC)
    norms = {n: P[n] for n in SMALL_NAMES}
    comm = _Comm({"down_in": ({n: shard[n] for n in ("wd1", "win", "wout")}, 0.5),
                  "ffn2": ({n: shard[n] for n in ("wg2", "wu2", "wd2")}, 0.5)})

    loss_part, gx, _, small = _local_step(x.reshape(T, D), loss_target.reshape(T, D), norms, W, B, S, comm)

    G, Dl, Mn, Vn = {}, {}, {}, {}
    dcw = small["conv_w"].reshape(CK, NDEV, HD).transpose(1, 0, 2)
    loss_row = jnp.zeros((1, 128), F32).at[0, 0].set(loss_part)
    per, cw_outs, loss_out, got = _small_step(
        [small[n] for n in SMALL_NAMES], loss_row, _pack_cw(dcw),
        [P[n] for n in SMALL_NAMES], [M[n] for n in SMALL_NAMES], [V[n] for n in SMALL_NAMES],
        _pack_cw(P["conv_w"][0]), _pack_cw(M["conv_w"][0]), _pack_cw(V["conv_w"][0]), carry=comm.last)
    comm.reduce_done(comm.last, got)
    loss = loss_out[0, 0]
    for n, outs in zip(SMALL_NAMES, per):
        G[n], Dl[n], Mn[n], Vn[n] = outs
    G["conv_w"], Dl["conv_w"], Mn["conv_w"], Vn["conv_w"] = [_unpack_cw(o) for o in cw_outs]

    for ln, pn, tr in bigs:
        outs = _adamw_big(comm.reduced[ln], hm(P[pn], tr), hm(M[pn], tr), hm(V[pn], tr), "adamw_" + ln)
        G[pn], Dl[pn], Mn[pn], Vn[pn] = [(jnp.transpose(o) if tr else o)[None] for o in outs]

    return (loss, gx.reshape(B, S, D), *[G[n] for n in order], *[Dl[n] for n in order],
            *[Mn[n] for n in order], *[Vn[n] for n in order])
```

```python
import functools

import jax
import jax.numpy as jnp
from jax import lax
from jax.experimental import pallas as pl
from jax.experimental.pallas import tpu as pltpu

F32 = jnp.float32
BF16 = jnp.bfloat16

D = 1024
FF = 2816
HD = 64
DA = 512
DC = 512
DIN = 2560
CK = 31
BLK = 128
DILS = (1, 4, 16)
EPS = 1e-6
NDEV = 8
MESH = pl.DeviceIdType.MESH

LR, B1, B2, AEPS, WD, STEP = 0.001, 0.9, 0.999, 1e-08, 0.01, 10

NT = (((1,), (1,)), ((), ()))
TN = (((0,), (0,)), ((), ()))

VMEM_LIMIT = 60 * 1024 * 1024


def _cp(sem=None):
    return pltpu.CompilerParams(dimension_semantics=sem, vmem_limit_bytes=VMEM_LIMIT)


def _sigmoid(x):
    return 0.5 * (jnp.tanh(0.5 * x) + 1.0)


def _pallas(body, args, *, grid, in_specs, out_specs, out_shape, scratch_shapes, sem, name, carry=None):
    if carry is None:
        outs = pl.pallas_call(body, grid=grid, in_specs=in_specs, out_specs=out_specs, out_shape=out_shape,
                              scratch_shapes=scratch_shapes, compiler_params=_cp(sem), name=name)(*args)
        return outs, None
    n_in, n_out, n_scr = len(in_specs), len(out_shape), len(scratch_shapes)
    c_in, c_out = len(carry.in_arrays), len(carry.out_shape)

    def wrapped(*refs):
        ins, refs = refs[:n_in], refs[n_in:]
        cins, refs = refs[:c_in], refs[c_in:]
        outs, refs = refs[:n_out], refs[n_out:]
        couts, refs = refs[:c_out], refs[c_out:]
        scr, cscr = refs[:n_scr], refs[n_scr:]
        ids = [pl.program_id(a) for a in range(len(grid))]
        step = ids[0]
        for i, n in zip(ids[1:], grid[1:]):
            step = step * n + i
        steps = functools.reduce(lambda a, b: a * b, grid)

        @pl.when(step == 0)
        def _():
            carry.start(cins, couts, cscr)

        body(*ins, *outs, *scr)

        if hasattr(carry, "mid"):
            @pl.when(step == int(steps * carry.mid_at))
            def _():
                carry.mid(cins, couts, cscr)

        @pl.when(step == steps - 1)
        def _():
            carry.finish(cins, couts, cscr)

    outs = pl.pallas_call(
        wrapped, grid=grid, in_specs=list(in_specs) + carry.in_specs, out_specs=list(out_specs) + carry.out_specs,
        out_shape=list(out_shape) + carry.out_shape, scratch_shapes=list(scratch_shapes) + carry.scratch,
        compiler_params=_cp(("arbitrary",) * len(grid)), name=name)(*args, *carry.in_arrays)
    return outs[:n_out], outs[n_out:]


FC = 256


def _resident(shape):
    return pl.BlockSpec(shape, lambda *_: (0,) * len(shape), pipeline_mode=pl.Buffered(1))


def _mix_out_ffn_loss(h1, attn, conv, wout, gain, wg, wu, wd, target, name):
    T = h1.shape[0]
    tm = 512
    nt = T // tm

    def body(h1_ref, at_ref, cv_ref, wo_ref, gain_ref, wg_ref, wu_ref, wd_ref, t_ref,
             h2_ref, n_ref, g_ref, u_ref, dout_ref, dyb_ref, sq_ref, a_hbm, a_scr, a_sem):
        t = pl.program_id(0)
        a_out = lambda i: pltpu.make_async_copy(a_scr, a_hbm.at[pl.ds(pl.multiple_of(i * tm, tm), tm), :], a_sem)

        @pl.when(t > 0)
        def _():
            a_out(t - 1).wait()

        xv = (h1_ref[...]
              + jnp.dot(at_ref[...], wo_ref[0:DA, :], preferred_element_type=F32)
              + jnp.dot(cv_ref[...], wo_ref[DA:D, :], preferred_element_type=F32))
        h2_ref[...] = xv
        r = lax.rsqrt(jnp.mean(xv * xv, axis=-1, keepdims=True) + EPS)
        n_ref[...] = (xv * r * gain_ref[...]).astype(BF16)
        for c in range(FF // FC):
            cols = slice(c * FC, (c + 1) * FC)
            nb = n_ref[...]
            g = lax.dot_general(nb, wg_ref[cols, :], NT, preferred_element_type=F32)
            u = lax.dot_general(nb, wu_ref[cols, :], NT, preferred_element_type=F32)
            g_ref[:, cols] = g.astype(BF16)
            u_ref[:, cols] = u.astype(BF16)
            a_scr[:, cols] = (g * _sigmoid(g) * u).astype(BF16)
        a_out(t).start()
        e = h2_ref[...] + 0.5 * jnp.dot(a_scr[...], wd_ref[...], preferred_element_type=F32) - t_ref[...]
        dout = e * (1.0 / D)
        dout_ref[...] = dout
        dyb_ref[...] = (0.5 * dout).astype(BF16)
        sq_ref[...] = jnp.sum(e * e, axis=0, keepdims=True)[None]

        @pl.when(t == nt - 1)
        def _():
            a_out(t).wait()

    row = pl.BlockSpec((tm, D), lambda t: (t, 0))
    half = pl.BlockSpec((tm, DA), lambda t: (t, 0))
    wide = pl.BlockSpec((tm, FF), lambda t: (t, 0))
    outs, _ = _pallas(
        body, (h1, attn, conv, wout, gain, wg, wu, wd, target), grid=(nt,),
        in_specs=[row, half, half, _resident((D, D)), _resident((1, D)), _resident((FF, D)), _resident((FF, D)),
                  _resident((FF, D)), row],
        out_specs=[row, row, wide, wide, row, row, pl.BlockSpec((1, 1, D), lambda t: (t, 0, 0)), HBM],
        out_shape=[jax.ShapeDtypeStruct((T, D), F32), jax.ShapeDtypeStruct((T, D), BF16)]
                  + [jax.ShapeDtypeStruct((T, FF), BF16)] * 2
                  + [jax.ShapeDtypeStruct((T, D), F32), jax.ShapeDtypeStruct((T, D), BF16),
                     jax.ShapeDtypeStruct((nt, 1, D), F32), jax.ShapeDtypeStruct((T, FF), BF16)],
        scratch_shapes=[pltpu.VMEM((tm, FF), BF16), pltpu.SemaphoreType.DMA(())],
        sem=("arbitrary",), name=name)
    return outs


def _ffn_gate_up(x, gain, wg, wu, name, carry=None):
    T = x.shape[0]
    tm = 512

    def body(x_ref, gain_ref, wg_ref, wu_ref, n_ref, g_ref, u_ref, a_ref):
        xv = x_ref[...]
        r = lax.rsqrt(jnp.mean(xv * xv, axis=-1, keepdims=True) + EPS)
        n_ref[...] = (xv * r * gain_ref[...]).astype(BF16)
        for c in range(FF // FC):
            cols = slice(c * FC, (c + 1) * FC)
            nb = n_ref[...]
            g = lax.dot_general(nb, wg_ref[cols, :], NT, preferred_element_type=F32)
            u = lax.dot_general(nb, wu_ref[cols, :], NT, preferred_element_type=F32)
            g_ref[:, cols] = g.astype(BF16)
            u_ref[:, cols] = u.astype(BF16)
            a_ref[:, cols] = (g * _sigmoid(g) * u).astype(BF16)

    row = pl.BlockSpec((tm, D), lambda t: (t, 0))
    wide = pl.BlockSpec((tm, FF), lambda t: (t, 0))
    return _pallas(
        body, (x, gain, wg, wu), grid=(T // tm,),
        in_specs=[row, _resident((1, D)), _resident((FF, D)), _resident((FF, D))],
        out_specs=[row, wide, wide, wide],
        out_shape=[jax.ShapeDtypeStruct((T, D), BF16)] + [jax.ShapeDtypeStruct((T, FF), BF16)] * 3,
        scratch_shapes=[], sem=("parallel",), name=name, carry=carry)


def _ffn_down_mix_in(x, a, wd, gain, win, name):
    T = x.shape[0]
    tm = 512

    def body(x_ref, a_ref, wd_ref, gain_ref, win_ref, h_ref, u_ref, n_ref):
        hv = x_ref[...] + 0.5 * jnp.dot(a_ref[...], wd_ref[...], preferred_element_type=F32)
        h_ref[...] = hv
        r = lax.rsqrt(jnp.mean(hv * hv, axis=-1, keepdims=True) + EPS)
        n_ref[...] = (hv * r * gain_ref[...]).astype(BF16)
        u_ref[...] = lax.dot_general(n_ref[...], win_ref[...], NT, preferred_element_type=F32).astype(BF16)

    row = pl.BlockSpec((tm, D), lambda t: (t, 0))
    wide = pl.BlockSpec((tm, FF), lambda t: (t, 0))
    outs, _ = _pallas(
        body, (x, a, wd, gain, win), grid=(T // tm,),
        in_specs=[row, wide, _resident((FF, D)), _resident((1, D)), _resident((DIN, D))],
        out_specs=[row, pl.BlockSpec((tm, DIN), lambda t: (t, 0)), row],
        out_shape=[jax.ShapeDtypeStruct((T, D), F32), jax.ShapeDtypeStruct((T, DIN), BF16),
                   jax.ShapeDtypeStruct((T, D), BF16)],
        scratch_shapes=[], sem=("parallel",), name=name)
    return outs


def _ffn_bwd_act(dyb, g, u, x, dout, gain, wg, wu, wd, name, carry=None):
    T = x.shape[0]
    tm = 256
    nt = T // tm

    def body(dy_ref, g_ref, u_ref, x_ref, dout_ref, gain_ref, wg_ref, wu_ref, wd_ref,
             dg_ref, du_ref, dx_ref, dgn_ref):
        for c in range(FF // FC):
            cols = slice(c * FC, (c + 1) * FC)
            da = lax.dot_general(dy_ref[...], wd_ref[cols, :], NT, preferred_element_type=F32)
            gv = g_ref[:, cols].astype(F32)
            uv = u_ref[:, cols].astype(F32)
            sg = _sigmoid(gv)
            dg_ref[:, cols] = (da * uv * (sg * (1.0 + gv * (1.0 - sg)))).astype(BF16)
            du_ref[:, cols] = (da * (gv * sg)).astype(BF16)
        dn = (jnp.dot(dg_ref[...], wg_ref[...], preferred_element_type=F32)
              + jnp.dot(du_ref[...], wu_ref[...], preferred_element_type=F32))
        dx, dgain = _rms_bwd_rows(dn, x_ref[...], gain_ref[...])
        dx_ref[...] = dout_ref[...] + dx
        dgn_ref[...] = dgain[None]

    row = pl.BlockSpec((tm, D), lambda t: (t, 0))
    wide = pl.BlockSpec((tm, FF), lambda t: (t, 0))
    return _pallas(
        body, (dyb, g, u, x, dout, gain, wg, wu, wd), grid=(nt,),
        in_specs=[row, wide, wide, row, row, _resident((1, D)), _resident((FF, D)), _resident((FF, D)),
                  _resident((FF, D))],
        out_specs=[wide, wide, row, pl.BlockSpec((1, 1, D), lambda t: (t, 0, 0))],
        out_shape=[jax.ShapeDtypeStruct((T, FF), BF16)] * 2
                  + [jax.ShapeDtypeStruct((T, D), F32), jax.ShapeDtypeStruct((nt, 1, D), F32)],
        scratch_shapes=[], sem=("parallel",), name=name, carry=carry)


def _ffn_bwd_w(lhs, rhs, name, carry=None):
    T = rhs.shape[0]
    tf = 256

    def body(l_ref, r_ref, dw_ref):
        dw_ref[...] = lax.dot_general(l_ref[...], r_ref[...], TN, preferred_element_type=F32).astype(BF16)

    (dw,), got = _pallas(
        body, (lhs, rhs), grid=(FF // tf,),
        in_specs=[pl.BlockSpec((T, tf), lambda f: (0, f)), _resident((T, D))],
        out_specs=[pl.BlockSpec((tf, D), lambda f: (f, 0))], out_shape=[jax.ShapeDtypeStruct((FF, D), BF16)],
        scratch_shapes=[], sem=("parallel",), name=name, carry=carry)
    return dw, got


def _rms_bwd_rows(dn, xv, gain):
    r = lax.rsqrt(jnp.mean(xv * xv, axis=-1, keepdims=True) + EPS)
    xhat = xv * r
    dxhat = dn * gain
    dx = r * (dxhat - xhat * jnp.mean(dxhat * xhat, axis=-1, keepdims=True))
    return dx, jnp.sum(dn * xhat, axis=0, keepdims=True)


def _mix_out_bwd(dh, attn, conv, wout):
    T = dh.shape[0]
    tm = 512
    nt = T // tm

    def body(dh_ref, a_ref, c_ref, w_ref, da_ref, dc_ref, dw_ref, acc_scr):
        t = pl.program_id(0)

        @pl.when(t == 0)
        def _():
            acc_scr[...] = jnp.zeros_like(acc_scr)

        dhb = dh_ref[...].astype(BF16)
        dmix = lax.dot_general(dhb, w_ref[...], NT, preferred_element_type=F32)
        da_ref[...] = dmix[:, 0:DA].astype(BF16)
        dc_ref[...] = dmix[:, DA:D].astype(BF16)
        acc_scr[0:DA, :] += lax.dot_general(a_ref[...], dhb, TN, preferred_element_type=F32)
        acc_scr[DA:D, :] += lax.dot_general(c_ref[...], dhb, TN, preferred_element_type=F32)

        @pl.when(t == nt - 1)
        def _():
            dw_ref[...] = acc_scr[...].astype(BF16)

    row = pl.BlockSpec((tm, D), lambda t: (t, 0))
    half = pl.BlockSpec((tm, DA), lambda t: (t, 0))
    full = pl.BlockSpec((D, D), lambda t: (0, 0))
    return pl.pallas_call(
        body, grid=(nt,), in_specs=[row, half, half, full], out_specs=[half, half, full],
        out_shape=[jax.ShapeDtypeStruct((T, DA), BF16)] * 2 + [jax.ShapeDtypeStruct((D, D), BF16)],
        scratch_shapes=[pltpu.VMEM((D, D), F32)],
        compiler_params=_cp(("arbitrary",)), name="mix_out_bwd")(dh, attn, conv, wout)


def _mix_in_bwd(dparts, win, nb, h, dh, gain):
    T = h.shape[0]
    tm = 512
    nt = T // tm

    def body(d0, d1, d2, d3, d4, w_ref, n_ref, h_ref, dh_ref, gain_ref,
             dw_ref, dx_ref, dyb_ref, dg_ref, acc_scr):
        t = pl.program_id(0)

        @pl.when(t == 0)
        def _():
            acc_scr[...] = jnp.zeros_like(acc_scr)

        n = n_ref[...]
        dn = jnp.zeros((tm, D), F32)
        for i, d_ref in enumerate((d0, d1, d2, d3, d4)):
            dv = d_ref[...]
            dn = dn + jnp.dot(dv, w_ref[i * DA:(i + 1) * DA, :], preferred_element_type=F32)
            acc_scr[i * DA:(i + 1) * DA, :] += lax.dot_general(dv, n, TN, preferred_element_type=F32)
        dx, dgain = _rms_bwd_rows(dn, h_ref[...], gain_ref[...])
        tot = dh_ref[...] + dx
        dx_ref[...] = tot
        dyb_ref[...] = (0.5 * tot).astype(BF16)
        dg_ref[...] = dgain[None]

        @pl.when(t == nt - 1)
        def _():
            dw_ref[...] = acc_scr[...].astype(BF16)

    row = pl.BlockSpec((tm, D), lambda t: (t, 0))
    half = pl.BlockSpec((tm, DA), lambda t: (t, 0))
    full = pl.BlockSpec((DIN, D), lambda t: (0, 0))
    return pl.pallas_call(
        body, grid=(nt,),
        in_specs=[half] * 5 + [full, row, row, row, pl.BlockSpec((1, D), lambda t: (0, 0))],
        out_specs=[full, row, row, pl.BlockSpec((1, 1, D), lambda t: (t, 0, 0))],
        out_shape=[jax.ShapeDtypeStruct((DIN, D), BF16), jax.ShapeDtypeStruct((T, D), F32),
                   jax.ShapeDtypeStruct((T, D), BF16), jax.ShapeDtypeStruct((nt, 1, D), F32)],
        scratch_shapes=[pltpu.VMEM((DIN, D), F32)],
        compiler_params=_cp(("arbitrary",)), name="mix_in_bwd")(*dparts, win, nb, h, dh, gain)


def _head_masks():
    lane = lax.broadcasted_iota(jnp.int32, (1, 2 * HD), 1)
    m0 = lane < HD
    return m0, jnp.logical_not(m0)


def _stack_heads(v, m0, m1):
    z = jnp.zeros_like(v)
    return jnp.concatenate([jnp.where(m0, v, z), jnp.where(m1, v, z)], axis=0)


def _unstack_heads(v2, m0):
    return jnp.where(m0, v2[0:BLK], v2[BLK:2 * BLK])


def _head_sums(xv):
    ri = lax.broadcasted_iota(jnp.int32, (2 * HD, 2 * HD), 0)
    ci = lax.broadcasted_iota(jnp.int32, (2 * HD, 2 * HD), 1)
    ones = jnp.where((ri < HD) == (ci < HD), 1.0, 0.0).astype(BF16)
    hi = xv.astype(BF16)
    lo = (xv - hi.astype(F32)).astype(BF16)
    return (jnp.dot(hi, ones, preferred_element_type=F32) + jnp.dot(lo, ones, preferred_element_type=F32))


def _head_rms(xv):
    return lax.rsqrt(_head_sums(xv * xv) * (1.0 / HD) + EPS)


def _band_mask(first):
    qi = lax.broadcasted_iota(jnp.int32, (BLK, 2 * BLK), 0)
    ci = lax.broadcasted_iota(jnp.int32, (BLK, 2 * BLK), 1)
    band = (ci >= qi) & (ci <= qi + BLK)
    return band & ((ci >= BLK) | jnp.logical_not(first))


def _block_rows(j, d, seg):
    r, n = j // seg, j % seg
    start = r + (d * BLK) * n
    first = n == 0
    prev = jnp.where(first, start, start - d * BLK)
    return pl.ds(start, BLK, stride=d), pl.ds(prev, BLK, stride=d), first


def _block_keys(refs, cur, prev, first, single):
    if single:
        qi = lax.broadcasted_iota(jnp.int32, (BLK, BLK), 0)
        ci = lax.broadcasted_iota(jnp.int32, (BLK, BLK), 1)
        return [r[cur, :].astype(BF16) for r in refs], ci <= qi
    return ([jnp.concatenate([r[prev, :], r[cur, :]], axis=0).astype(BF16) for r in refs], _band_mask(first))


def _attn_fwd(u, qg2, kg2, B, S, carry=None):
    T = B * S
    NB = S // BLK
    scale = HD ** -0.5

    def body(q_ref, k_ref, v_ref, qg_ref, kg_ref, o_ref, lse_ref, qn, kn, vn, os_, ls_):
        m0, m1 = _head_masks()
        qv = q_ref[...].astype(F32)
        qn[...] = qv * _head_rms(qv) * (qg_ref[...] * scale)
        kv = k_ref[...].astype(F32)
        kn[...] = kv * _head_rms(kv) * kg_ref[...]
        vn[...] = v_ref[...].astype(F32)

        for i, d in enumerate(DILS):
            seg = NB // d

            def blk(j, c, i=i, d=d, seg=seg):
                cur, prev, first = _block_rows(j, d, seg)
                q2 = _stack_heads(qn[cur, :].astype(BF16), m0, m1)
                (kk, vv), mask = _block_keys((kn, vn), cur, prev, first, False)
                s = lax.dot_general(q2, kk, NT, preferred_element_type=F32)
                s = jnp.where(jnp.concatenate([mask, mask], axis=0), s, -1e30)
                mx = jnp.max(s, axis=-1, keepdims=True)
                p = jnp.exp(s - mx)
                l = jnp.sum(p, axis=-1, keepdims=True)
                o2 = jnp.dot((p * (1.0 / l)).astype(BF16), vv, preferred_element_type=F32)
                os_[i, cur, :] = _unstack_heads(o2, m0)
                ls_[i, cur, :] = _unstack_heads(mx + jnp.log(l), m0)
                return c

            lax.fori_loop(0, NB, blk, 0, unroll=8)

        def comb(c, carry):
            rows = pl.ds(pl.multiple_of(c * 256, 256), 256)
            l0, l1, l2 = ls_[0, rows, :], ls_[1, rows, :], ls_[2, rows, :]
            mx = jnp.maximum(jnp.maximum(l0, l1), l2)
            e0, e1, e2 = jnp.exp(l0 - mx), jnp.exp(l1 - mx), jnp.exp(l2 - mx)
            tot = e0 + e1 + e2
            inv = 1.0 / tot
            o = (e0 * os_[0, rows, :] + e1 * os_[1, rows, :] + e2 * os_[2, rows, :]) * inv
            o_ref[rows, :] = o.astype(BF16)
            lse_ref[rows, :] = mx + jnp.log(tot)
            return carry

        lax.fori_loop(0, S // 256, comb, 0)

    pair = 2 * HD
    blk_spec = lambda off: pl.BlockSpec((S, pair), lambda b, p, off=off: (b, off + p))
    gspec = pl.BlockSpec((1, pair), lambda b, p: (0, 0))
    return _pallas(
        body, (u, u, u, qg2, kg2), grid=(B, DA // pair),
        in_specs=[blk_spec(0), blk_spec(DA // pair), blk_spec(2 * DA // pair), gspec, gspec],
        out_specs=[blk_spec(0), blk_spec(0)],
        out_shape=[jax.ShapeDtypeStruct((T, DA), BF16), jax.ShapeDtypeStruct((T, DA), F32)],
        scratch_shapes=[pltpu.VMEM((S, pair), F32)] * 3 + [pltpu.VMEM((3, S, pair), F32)] * 2,
        sem=("parallel", "parallel"), name="attn_fwd", carry=carry)


def _attn_bwd(u, attn, dattn, lse, qg2, kg2, B, S, carry=None):
    T = B * S
    NB = S // BLK
    scale = HD ** -0.5
    pair = 2 * HD

    def body(q_ref, k_ref, v_ref, o_ref, do_ref, lse_ref, qg_ref, kg_ref,
             dq_ref, dk_ref, dv_ref, dgn_ref,
             qn, kn, vn, don, ldl, accq, acck, accv, rq, rk):
        m0, m1 = _head_masks()
        lane = lax.broadcasted_iota(jnp.int32, (1, pair), 1)
        qv = q_ref[...].astype(F32)
        rq[...] = _head_rms(qv)
        qn[...] = qv * rq[...] * (qg_ref[...] * scale)
        kv = k_ref[...].astype(F32)
        rk[...] = _head_rms(kv)
        kn[...] = kv * rk[...] * kg_ref[...]
        vn[...] = v_ref[...].astype(F32)
        dov = do_ref[...].astype(F32)
        don[...] = dov
        ldl[...] = jnp.where((lane % HD) < HD // 2, lse_ref[...], _head_sums(dov * o_ref[...].astype(F32)))

        for i, d in enumerate(DILS):
            seg = NB // d

            def blk(j, c, i=i, d=d, seg=seg):
                cur, prev, first = _block_rows(j, d, seg)
                q2 = _stack_heads(qn[cur, :].astype(BF16), m0, m1)
                do2 = _stack_heads(don[cur, :].astype(BF16), m0, m1)
                (kk, vv), mask = _block_keys((kn, vn), cur, prev, first, seg == 1)
                ldv = ldl[cur, :]
                lse2 = jnp.concatenate([ldv[:, 0:1], ldv[:, HD:HD + 1]], axis=0)
                dl2 = jnp.concatenate([ldv[:, HD // 2:HD // 2 + 1], ldv[:, HD + HD // 2:HD + HD // 2 + 1]], axis=0)
                s = lax.dot_general(q2, kk, NT, preferred_element_type=F32)
                p = jnp.where(jnp.concatenate([mask, mask], axis=0), jnp.exp(s - lse2), 0.0)
                dp = lax.dot_general(do2, vv, NT, preferred_element_type=F32)
                ds = (p * (dp - dl2)).astype(BF16)
                dq_acc = _unstack_heads(jnp.dot(ds, kk, preferred_element_type=F32), m0)
                dk_acc = lax.dot_general(ds, q2, TN, preferred_element_type=F32)
                dv_acc = lax.dot_general(p.astype(BF16), do2, TN, preferred_element_type=F32)
                if i == 0:
                    accq[cur, :] = dq_acc
                    acck[cur, :] = dk_acc[BLK:2 * BLK]
                    accv[cur, :] = dv_acc[BLK:2 * BLK]
                    acck[prev, :] += dk_acc[0:BLK]
                    accv[prev, :] += dv_acc[0:BLK]
                elif seg == 1:
                    accq[cur, :] += dq_acc
                    acck[cur, :] += dk_acc
                    accv[cur, :] += dv_acc
                else:
                    accq[cur, :] += dq_acc
                    acck[prev, :] += dk_acc[0:BLK]
                    acck[cur, :] += dk_acc[BLK:2 * BLK]
                    accv[prev, :] += dv_acc[0:BLK]
                    accv[cur, :] += dv_acc[BLK:2 * BLK]
                return c

            lax.fori_loop(0, NB, blk, 0, unroll=8)

        def norm_bwd(x_ref, r_ref, dn, gain):
            r = r_ref[...]
            xhat = x_ref[...].astype(F32) * r
            dxhat = dn * gain
            dx = r * (dxhat - xhat * (_head_sums(dxhat * xhat) * (1.0 / HD)))
            return dx, jnp.sum(dn * xhat, axis=0, keepdims=True)

        dq, dgq = norm_bwd(q_ref, rq, accq[...], qg_ref[...] * scale)
        dk, dgk = norm_bwd(k_ref, rk, acck[...], kg_ref[...])
        dq_ref[...] = dq.astype(BF16)
        dk_ref[...] = dk.astype(BF16)
        dv_ref[...] = accv[...].astype(BF16)
        dgn_ref[...] = jnp.concatenate([dgq * scale, dgk, jnp.zeros((6, pair), F32)], axis=0)[None]

    blk_spec = lambda off: pl.BlockSpec((S, pair), lambda b, p, off=off: (b, off + p))
    gspec = pl.BlockSpec((1, pair), lambda b, p: (0, 0))
    np_ = DA // pair
    return _pallas(
        body, (u, u, u, attn, dattn, lse, qg2, kg2), grid=(B, np_),
        in_specs=[blk_spec(0), blk_spec(np_), blk_spec(2 * np_), blk_spec(0), blk_spec(0), blk_spec(0),
                  gspec, gspec],
        out_specs=[blk_spec(0), blk_spec(0), blk_spec(0),
                   pl.BlockSpec((1, 8, pair), lambda b, p: (b * np_ + p, 0, 0))],
        out_shape=[jax.ShapeDtypeStruct((T, DA), BF16)] * 3 + [jax.ShapeDtypeStruct((B * np_, 8, pair), F32)],
        scratch_shapes=[pltpu.VMEM((S, pair), F32)] * 10,
        sem=("parallel", "parallel"), name="attn_bwd", carry=carry)


CT = 32
CPAD = 32


def _shifted(win, offsets):
    rolled, out = {}, {}
    n = win.shape[0]
    for o in offsets:
        sub = o % 8
        if sub not in rolled:
            rolled[sub] = win if sub == 0 else pltpu.roll(win, n - sub, 0)
        out[o] = rolled[sub][o - sub:o - sub + CT, :]
    return out


def _ln_fwd(y, g, b):
    mu = jnp.mean(y, axis=-1, keepdims=True)
    yc = y - mu
    rstd = lax.rsqrt(jnp.mean(yc * yc, axis=-1, keepdims=True) + EPS)
    xhat = yc * rstd
    return xhat, rstd, xhat * g + b


def _fill_glu(ca_ref, cg_ref, glu, S):
    glu[pl.ds(0, CPAD), :] = jnp.zeros((CPAD, DC), F32)

    def fill(i, c):
        rows = pl.ds(pl.multiple_of(i * 256, 256), 256)
        a = ca_ref[rows, :].astype(F32)
        gt = cg_ref[rows, :].astype(F32)
        glu[pl.ds(pl.multiple_of(CPAD + i * 256, CT), 256), :] = a * _sigmoid(gt)
        return c

    lax.fori_loop(0, S // 256, fill, 0)


def _conv_fwd(u, cw, cb, lg, lb, B, S):
    T = B * S

    def body(ca_ref, cg_ref, w_ref, b_ref, lg_ref, lb_ref, o_ref, y_ref, glu):
        _fill_glu(ca_ref, cg_ref, glu, S)

        def step(i, c):
            t0 = pl.multiple_of(i * CT, CT)
            win = glu[pl.ds(t0, 2 * CT), :]
            acc = jnp.zeros((CT, DC), F32) + b_ref[...]
            taps = _shifted(win, [k + 2 for k in range(CK)])
            for k in range(CK):
                acc = acc + taps[k + 2] * w_ref[k:k + 1, :]
            y_ref[pl.ds(t0, CT), :] = acc
            _, _, z = _ln_fwd(acc, lg_ref[...], lb_ref[...])
            o_ref[pl.ds(t0, CT), :] = (z * _sigmoid(z)).astype(BF16)
            return c

        lax.fori_loop(0, S // CT, step, 0, unroll=2)

    vec = pl.BlockSpec((1, DC), lambda b: (0, 0))
    return pl.pallas_call(
        body, grid=(B,),
        in_specs=[pl.BlockSpec((S, DC), lambda b: (b, 3)), pl.BlockSpec((S, DC), lambda b: (b, 4)),
                  pl.BlockSpec((CT, DC), lambda b: (0, 0)), vec, vec, vec],
        out_specs=[pl.BlockSpec((S, DC), lambda b: (b, 0))] * 2,
        out_shape=[jax.ShapeDtypeStruct((T, DC), BF16), jax.ShapeDtypeStruct((T, DC), F32)],
        scratch_shapes=[pltpu.VMEM((CPAD + S, DC), F32)],
        compiler_params=_cp(("parallel",)), name="conv_fwd")(u, u, cw, cb, lg, lb)


def _conv_bwd(u, y, dconv, cw, lg, lb, B, S):
    T = B * S

    def body(ca_ref, cg_ref, y_ref, dc_ref, w_ref, lg_ref, lb_ref,
             dca_ref, dcg_ref, dw_ref, ds_ref, glu, dyp, dwacc):
        _fill_glu(ca_ref, cg_ref, glu, S)
        dyp[pl.ds(S, CPAD), :] = jnp.zeros((CPAD, DC), F32)
        lgv, lbv = lg_ref[...], lb_ref[...]

        def sum8(v):
            return functools.reduce(jnp.add, [v[r:r + 8] for r in range(0, v.shape[0], 8)])

        P1 = 4 * CT

        def p1(i, carry):
            sb, sg, sl = carry
            t0 = pl.multiple_of(i * P1, P1)
            xhat, rstd, z = _ln_fwd(y_ref[pl.ds(t0, P1), :], lgv, lbv)
            sz = _sigmoid(z)
            dz = dc_ref[pl.ds(t0, P1), :].astype(F32) * (sz * (1.0 + z * (1.0 - sz)))
            dxhat = dz * lgv
            dy = rstd * (dxhat - jnp.mean(dxhat, axis=-1, keepdims=True)
                         - xhat * jnp.mean(dxhat * xhat, axis=-1, keepdims=True))
            dyp[pl.ds(t0, P1), :] = dy
            return sb + sum8(dy), sg + sum8(dz * xhat), sl + sum8(dz)

        z8 = jnp.zeros((8, DC), F32)
        sb, sg, sl = lax.fori_loop(0, S // P1, p1, (z8, z8, z8))
        rs = lambda v: jnp.sum(v, axis=0, keepdims=True)
        ds_ref[...] = jnp.concatenate([rs(sb), rs(sg), rs(sl), jnp.zeros((5, DC), F32)], axis=0)[None]

        def p2(i, c):
            t0 = pl.multiple_of(i * CT, CT)
            win = dyp[pl.ds(t0, 2 * CT), :]
            acc = jnp.zeros((CT, DC), F32)
            taps = _shifted(win, [30 - k for k in range(CK)])
            for k in range(CK):
                acc = acc + taps[30 - k] * w_ref[k:k + 1, :]
            a = ca_ref[pl.ds(t0, CT), :].astype(F32)
            sgt = _sigmoid(cg_ref[pl.ds(t0, CT), :].astype(F32))
            dca_ref[pl.ds(t0, CT), :] = (acc * sgt).astype(BF16)
            dcg_ref[pl.ds(t0, CT), :] = (acc * a * sgt * (1.0 - sgt)).astype(BF16)
            return c

        lax.fori_loop(0, S // CT, p2, 0)

        dwacc[...] = jnp.zeros_like(dwacc)

        def p3(i, c):
            t0 = pl.multiple_of(i * CT, CT)
            win = glu[pl.ds(t0, 2 * CT), :]
            dy = dyp[pl.ds(t0, CT), :]
            for k in range(CK):
                dwacc[k] += sum8(dy * win[k + 2:k + 2 + CT, :])
            return c

        lax.fori_loop(0, S // CT, p3, 0)
        dw_ref[...] = jnp.sum(dwacc[...], axis=1)[None]

    vec = pl.BlockSpec((1, DC), lambda b: (0, 0))
    seq = pl.BlockSpec((S, DC), lambda b: (b, 0))
    return pl.pallas_call(
        body, grid=(B,),
        in_specs=[pl.BlockSpec((S, DC), lambda b: (b, 3)), pl.BlockSpec((S, DC), lambda b: (b, 4)),
                  seq, seq, pl.BlockSpec((CT, DC), lambda b: (0, 0)), vec, vec],
        out_specs=[seq, seq, pl.BlockSpec((1, CT, DC), lambda b: (b, 0, 0)),
                   pl.BlockSpec((1, 8, DC), lambda b: (b, 0, 0))],
        out_shape=[jax.ShapeDtypeStruct((T, DC), BF16)] * 2
                  + [jax.ShapeDtypeStruct((B, CT, DC), F32), jax.ShapeDtypeStruct((B, 8, DC), F32)],
        scratch_shapes=[pltpu.VMEM((CPAD + S, DC), F32), pltpu.VMEM((S + CPAD, DC), F32),
                        pltpu.VMEM((CT, 8, DC), F32)],
        compiler_params=_cp(("parallel",)), name="conv_bwd")(u, u, y, dconv, cw, lg, lb)


def _local_step(x, target, norms, W, B, S, comm=None):
    qg2 = jnp.concatenate([norms["q_norm"], norms["q_norm"]], axis=1)
    kg2 = jnp.concatenate([norms["k_norm"], norms["k_norm"]], axis=1)
    cw = jnp.concatenate([W["conv_w"], jnp.zeros((1, DC), F32)], axis=0)

    W = dict(W)
    (n1, g1, u1, act1), got = _ffn_gate_up(x, norms["ffn1_norm"], W["wg1"], W["wu1"], "ffn1_gate_up",
                                           carry=comm.gathers["down_in"] if comm else None)
    if comm:
        W.update(comm.gathered("down_in", got))
    h1, u, n2 = _ffn_down_mix_in(x, act1, W["wd1"], norms["mix_norm"], W["win"], "ffn1_down_mix_in")
    (attn, lse), got = _attn_fwd(u, qg2, kg2, B, S, carry=comm.gathers["ffn2"] if comm else None)
    if comm:
        W = dict(W, **comm.gathered("ffn2", got))
    conv, y = _conv_fwd(u, cw, norms["conv_b"], norms["conv_ln_g"], norms["conv_ln_b"], B, S)
    h2, n3, g2, u2, dout, dyb, sq, act2 = _mix_out_ffn_loss(h1, attn, conv, W["wout"], norms["ffn2_norm"],
                                                            W["wg2"], W["wu2"], W["wd2"], target, "ffn2_fwd")
    loss = (0.5 / D) * jnp.sum(sq)

    (dg2, du2, dh2, dgn_ffn2), _ = _ffn_bwd_act(dyb, g2, u2, h2, dout, norms["ffn2_norm"],
                                               W["wg2"], W["wu2"], W["wd2"], "ffn2_bwd_act")
    dwd2, _ = _ffn_bwd_w(act2, dyb, "ffn2_bwd_wd")
    dwg2, _ = _ffn_bwd_w(dg2, n3, "ffn2_bwd_wg")
    dwu2, _ = _ffn_bwd_w(du2, n3, "ffn2_bwd_wu")
    dattn, dconv, dwout = _mix_out_bwd(dh2, attn, conv, W["wout"])
    carry = comm.reduce_start("ffn2", {"wg2": dwg2, "wu2": dwu2, "wd2": dwd2, "wout": dwout}) if comm else None
    (dq, dk, dv, dgn_qk), got = _attn_bwd(u, attn, dattn, lse, qg2, kg2, B, S, carry=carry)
    if comm:
        comm.reduce_done(carry, got)
    dca, dcg, dcw, dcs = _conv_bwd(u, y, dconv, cw, norms["conv_ln_g"], norms["conv_ln_b"], B, S)
    dwin, dh1, dyb1, dgn_mix = _mix_in_bwd((dq, dk, dv, dca, dcg), W["win"], n2, h1, dh2, norms["mix_norm"])
    (dg1, du1, gx, dgn_ffn1), _ = _ffn_bwd_act(dyb1, g1, u1, x, dh1, norms["ffn1_norm"],
                                              W["wg1"], W["wu1"], W["wd1"], "ffn1_bwd_act")
    carry = comm.reduce_start("win", {"win": dwin}) if comm else None
    dwd1, got = _ffn_bwd_w(act1, dyb1, "ffn1_bwd_wd", carry=carry)
    if comm:
        comm.reduce_done(carry, got)
        carry = comm.reduce_start("wd1", {"wd1": dwd1})
    dwg1, got = _ffn_bwd_w(dg1, n1, "ffn1_bwd_wg", carry=carry)
    if comm:
        comm.reduce_done(carry, got)
        carry = comm.reduce_start("wg1", {"wg1": dwg1})
    dwu1, got = _ffn_bwd_w(du1, n1, "ffn1_bwd_wu", carry=carry)
    if comm:
        comm.reduce_done(carry, got)
        comm.last = comm.reduce_start("wu1", {"wu1": dwu1})

    qk = jnp.sum(dgn_qk, axis=0)
    cs = jnp.sum(dcs, axis=0)
    small = {
        "ffn1_norm": jnp.sum(dgn_ffn1, axis=0),
        "mix_norm": jnp.sum(dgn_mix, axis=0),
        "q_norm": qk[0:1, 0:HD] + qk[0:1, HD:2 * HD],
        "k_norm": qk[1:2, 0:HD] + qk[1:2, HD:2 * HD],
        "conv_w": jnp.sum(dcw, axis=0)[0:CK],
        "conv_b": cs[0:1],
        "conv_ln_g": cs[1:2],
        "conv_ln_b": cs[2:3],
        "ffn2_norm": jnp.sum(dgn_ffn2, axis=0),
    }
    big = {"wg1": dwg1, "wu1": dwu1, "wd1": dwd1, "win": dwin, "wout": dwout,
           "wg2": dwg2, "wu2": dwu2, "wd2": dwd2}
    return loss, gx, big, small


HBM = pl.BlockSpec(memory_space=pltpu.HBM)
VMEM = pl.BlockSpec(memory_space=pltpu.VMEM)


def _place():
    return lax.axis_index("x"), lax.axis_index("y"), lax.axis_index("c")


class _GatherCarry:
    def __init__(self, shards, mid_at=0.5):
        nt = len(shards)
        self.mid_at = mid_at
        self.shards = shards
        self.in_arrays = [s for s, _ in shards]
        self.in_specs = [VMEM] * nt
        self.out_shape = [jax.ShapeDtypeStruct((NDEV * s.shape[0], s.shape[1]), dt) for s, dt in shards]
        self.out_specs = [HBM] * nt
        self.scratch = ([pltpu.VMEM(s.shape, dt) for s, dt in shards]
                        + [pltpu.SemaphoreType.DMA((nt, 7)), pltpu.SemaphoreType.DMA((nt, 7)),
                           pltpu.SemaphoreType.DMA((nt,))])

    def _copies(self, outs, scr):
        nt = len(self.shards)
        stages = scr[:nt]
        send_sems, recv_sems, local_sems = scr[nt:]
        x, y, c = _place()
        me, sibling = (x, y, c), (x, y, 1 - c)
        xn, yn, diag = (1 - x, y, c), (x, 1 - y, c), (1 - x, 1 - y, c)
        via = (x ^ c, y ^ (1 - c), c)
        onto = (x ^ (1 - c), y ^ c, c)

        def rows(t, px, py, pc):
            r = self.shards[t][0].shape[0]
            return outs[t].at[pl.ds((4 * px + 2 * py + pc) * r, r), :]

        def copy(t, k, block, to, src=None):
            return pltpu.make_async_remote_copy(
                src_ref=rows(t, *block) if src is None else src, dst_ref=rows(t, *block),
                send_sem=send_sems.at[t, k], recv_sem=recv_sems.at[t, k],
                device_id=to, device_id_type=MESH)

        sib = lambda b: (b[0], b[1], 1 - c)
        return dict(
            local=[pltpu.make_async_copy(stages[t], rows(t, *me), local_sems.at[t]) for t in range(nt)],
            own=[[copy(t, 0, me, sibling, src=stages[t]), copy(t, 1, me, xn, src=stages[t]),
                  copy(t, 2, me, yn, src=stages[t])] for t in range(nt)],
            relay=[copy(t, 3, via, onto) for t in range(nt)],
            down=[[copy(t, 4, xn, sibling), copy(t, 5, yn, sibling)] for t in range(nt)],
            down_diag=[copy(t, 6, diag, sibling) for t in range(nt)],
            got_xy=[[copy(t, 1, xn, me), copy(t, 2, yn, me)] for t in range(nt)],
            got_diag=[copy(t, 3, diag, me) for t in range(nt)],
            got_sib=[[copy(t, 0, sibling, me), copy(t, 4, sib(xn), me), copy(t, 5, sib(yn), me),
                      copy(t, 6, sib(diag), me)] for t in range(nt)])

    def start(self, ins, outs, scr):
        cps = self._copies(outs, scr)
        for t, (_, dt) in enumerate(self.shards):
            scr[t][...] = ins[t][...].astype(dt)
            for cp in [cps["local"][t]] + cps["own"][t]:
                cp.start()

    def mid(self, ins, outs, scr):
        cps = self._copies(outs, scr)
        for t in range(len(self.shards)):
            for cp in cps["got_xy"][t]:
                cp.wait_recv()
            for cp in [cps["relay"][t]] + cps["down"][t]:
                cp.start()

    def finish(self, ins, outs, scr):
        cps = self._copies(outs, scr)
        for t in range(len(self.shards)):
            cps["got_diag"][t].wait_recv()
            cps["down_diag"][t].start()
        for t in range(len(self.shards)):
            for cp in cps["got_sib"][t]:
                cp.wait_recv()
            for cp in cps["own"][t] + [cps["relay"][t]] + cps["down"][t] + [cps["down_diag"][t]]:
                cp.wait_send()
            cps["local"][t].wait()


def _run_carry(carry, name):
    def body(*refs):
        n_in, n_out = len(carry.in_arrays), len(carry.out_shape)
        ins, outs, scr = refs[:n_in], refs[n_in:n_in + n_out], refs[n_in + n_out:]
        carry.start(ins, outs, scr)
        if hasattr(carry, "mid"):
            carry.mid(ins, outs, scr)
        carry.finish(ins, outs, scr)

    return pl.pallas_call(
        body, in_specs=carry.in_specs, out_specs=carry.out_specs, out_shape=carry.out_shape,
        scratch_shapes=carry.scratch, compiler_params=pltpu.CompilerParams(vmem_limit_bytes=VMEM_LIMIT),
        name=name)(*carry.in_arrays)


def _sibling_reduce(grads, name):
    nt = len(grads)
    g4 = [g.reshape(4, 2, g.shape[0] // NDEV, g.shape[1]) for g in grads]

    def body(*refs):
        ins, outs = refs[:nt], refs[nt:2 * nt]
        recv, own = refs[2 * nt:3 * nt], refs[3 * nt:4 * nt]
        send_sems, recv_sems, load_sems, store_sems = refs[4 * nt:]
        x, y, c = _place()
        sends = [pltpu.make_async_remote_copy(
            src_ref=ins[t].at[:, 1 - c], dst_ref=recv[t], send_sem=send_sems.at[t], recv_sem=recv_sems.at[t],
            device_id=(x, y, 1 - c), device_id_type=MESH) for t in range(nt)]
        loads = [pltpu.make_async_copy(ins[t].at[:, c], own[t], load_sems.at[t]) for t in range(nt)]
        stores = [pltpu.make_async_copy(own[t], outs[t], store_sems.at[t]) for t in range(nt)]
        for cp in sends + loads:
            cp.start()
        for t in range(nt):
            loads[t].wait()
            sends[t].wait_recv()
            for q in range(4):
                own[t][q] = (own[t][q].astype(F32) + recv[t][q].astype(F32)).astype(BF16)
            stores[t].start()
        for t in range(nt):
            sends[t].wait_send()
            stores[t].wait()

    blocks = [(4,) + g.shape[2:] for g in g4]
    return pl.pallas_call(
        body, in_specs=[HBM] * nt, out_specs=[HBM] * nt,
        out_shape=[jax.ShapeDtypeStruct(b, BF16) for b in blocks],
        scratch_shapes=[pltpu.VMEM(b, BF16) for b in blocks] * 2 + [pltpu.SemaphoreType.DMA((nt,))] * 4,
        compiler_params=pltpu.CompilerParams(vmem_limit_bytes=VMEM_LIMIT), name=name)(*g4)


class _ExchangeCarry:
    def __init__(self, names, parts, mid_at=0.5):
        nt = len(parts)
        self.mid_at = mid_at
        self.names = names
        self.in_arrays = list(parts)
        self.in_specs = [HBM] * nt
        self.out_shape = [jax.ShapeDtypeStruct((3,) + p.shape[1:], BF16) for p in parts]
        self.out_specs = [HBM] * nt
        self.scratch = ([pltpu.VMEM(p.shape[1:], BF16) for p in parts] * 2
                        + [pltpu.SemaphoreType.DMA((nt, 3)), pltpu.SemaphoreType.DMA((nt, 3)),
                           pltpu.SemaphoreType.DMA((nt,)), pltpu.SemaphoreType.DMA((nt,))])

    def _copies(self, ins, outs, scr):
        nt = len(ins)
        relayed, mine = scr[:nt], scr[nt:2 * nt]
        send_sems, recv_sems, local_sems, load_sems = scr[2 * nt:]
        x, y, c = _place()
        q = lambda cx, cy: 2 * cx + cy
        near, far = (x ^ c, y ^ (1 - c)), (x ^ (1 - c), y ^ c)

        def remote(t, k, src, dst, chip):
            return pltpu.make_async_remote_copy(
                src_ref=src, dst_ref=dst, send_sem=send_sems.at[t, k], recv_sem=recv_sems.at[t, k],
                device_id=(*chip, c), device_id_type=MESH)

        return dict(
            keep=[pltpu.make_async_copy(ins[t].at[q(x, y)], outs[t].at[0], local_sems.at[t]) for t in range(nt)],
            load=[pltpu.make_async_copy(ins[t].at[q(*far)], mine[t], load_sems.at[t]) for t in range(nt)],
            direct=[remote(t, 0, ins[t].at[q(*near)], outs[t].at[1], near) for t in range(nt)],
            relay=[remote(t, 1, ins[t].at[q(1 - x, 1 - y)], relayed[t], near) for t in range(nt)],
            merged=[remote(t, 2, mine[t], outs[t].at[2], far) for t in range(nt)],
            relayed=relayed, mine=mine)

    def start(self, ins, outs, scr):
        cps = self._copies(ins, outs, scr)
        for t in range(len(ins)):
            for kind in ("keep", "load", "direct", "relay"):
                cps[kind][t].start()

    def mid(self, ins, outs, scr):
        cps = self._copies(ins, outs, scr)
        for t in range(len(ins)):
            cps["load"][t].wait()
            cps["relay"][t].wait_recv()
            cps["mine"][t][...] = (cps["mine"][t][...].astype(F32) + cps["relayed"][t][...].astype(F32)).astype(BF16)
            cps["merged"][t].start()

    def finish(self, ins, outs, scr):
        cps = self._copies(ins, outs, scr)
        for t in range(len(ins)):
            cps["direct"][t].wait()
            cps["relay"][t].wait_send()
            cps["merged"][t].wait()
            cps["keep"][t].wait()


class _Comm:
    def __init__(self, groups):
        self.names = {tag: list(g) for tag, (g, _) in groups.items()}
        self.gathers = {tag: _GatherCarry(list(g.values()), mid_at) for tag, (g, mid_at) in groups.items()}
        self.reduced = {}

    def gathered(self, tag, outs):
        return dict(zip(self.names[tag], outs))

    def reduce_start(self, tag, grads, mid_at=0.5):
        names = list(grads)
        parts = _sibling_reduce([grads[n] for n in names], "sibling_reduce_" + tag)
        return _ExchangeCarry(names, parts, mid_at)

    def reduce_done(self, carry, outs):
        self.reduced.update(zip(carry.names, outs))


def _adamw_math(w, g, m, v):
    m = B1 * m + (1.0 - B1) * g
    v = B2 * v + (1.0 - B2) * (g * g)
    m_hat = m / (1.0 - B1 ** STEP)
    v_hat = v / (1.0 - B2 ** STEP)
    delta = -LR * (m_hat / (jnp.sqrt(v_hat) + AEPS) + WD * w)
    return delta, m, v


def _adamw_big(recv, w, m, v, name):
    def body(r_ref, w_ref, m_ref, v_ref, g_ref, d_ref, mo_ref, vo_ref):
        g = r_ref[0].astype(F32)
        for q in range(1, 3):
            g = g + r_ref[q].astype(F32)
        d, mn, vn = _adamw_math(w_ref[...], g, m_ref[...], v_ref[...])
        g_ref[...] = g
        d_ref[...] = d
        mo_ref[...] = mn
        vo_ref[...] = vn

    rows, n = w.shape
    tr = rows // 2
    row = pl.BlockSpec((tr, n), lambda t: (t, 0))
    return pl.pallas_call(
        body, grid=(2,), in_specs=[pl.BlockSpec((3, tr, n), lambda t: (0, t, 0)), row, row, row],
        out_specs=[row] * 4, out_shape=[jax.ShapeDtypeStruct(w.shape, F32)] * 4,
        compiler_params=_cp(("parallel",)), name=name)(recv, w, m, v)


SMALL_NAMES = ("ffn1_norm", "mix_norm", "ffn2_norm", "conv_b", "conv_ln_g", "conv_ln_b", "q_norm", "k_norm")
SROWS = 16
LOSS_ROW = len(SMALL_NAMES)
CWF = 2


def _small_step(gs, loss_row, gcw, ws, ms, vs, wcw, mcw, vcw, carry=None):
    ns = len(SMALL_NAMES)
    widths = [g.shape[1] for g in gs]

    def body(*refs):
        it = iter(refs)
        take = lambda n: [next(it) for _ in range(n)]
        g_refs, (loss_ref, gcw_ref) = take(ns), take(2)
        w_refs, m_refs, v_refs = take(ns), take(ns), take(ns)
        wcw_ref, mcw_ref, vcw_ref = take(3)
        cins = take(len(carry.in_arrays)) if carry else []
        outs = [take(4) for _ in range(ns)]
        cw_outs, (loss_out,) = take(4), take(1)
        couts = take(len(carry.out_shape)) if carry else []
        send, slots, cslots, send_sems, recv_sems, csend_sems, crecv_sems = take(7)
        cscr = list(it)
        x, y, c = _place()
        me = 4 * x + 2 * y + c
        send[...] = jnp.zeros_like(send)
        for k in range(ns):
            send[k:k + 1, 0:widths[k]] = g_refs[k][...]
        send[LOSS_ROW:LOSS_ROW + 1, 0:128] = loss_ref[...]
        slots[me] = send[...]
        cslots[me] = gcw_ref[...]
        cps = []
        for k in range(1, NDEV):
            peer = (x ^ ((k >> 2) & 1), y ^ ((k >> 1) & 1), c ^ (k & 1))
            cps.append(pltpu.make_async_remote_copy(
                src_ref=send, dst_ref=slots.at[me], send_sem=send_sems.at[k - 1], recv_sem=recv_sems.at[k - 1],
                device_id=peer, device_id_type=MESH))
            cps.append(pltpu.make_async_remote_copy(
                src_ref=gcw_ref, dst_ref=cslots.at[me], send_sem=csend_sems.at[k - 1],
                recv_sem=crecv_sems.at[k - 1], device_id=peer, device_id_type=MESH))
        for cp in cps:
            cp.start()
        if carry:
            carry.start(cins, couts, cscr)
        for cp in cps:
            cp.wait()
        if carry:
            carry.mid(cins, couts, cscr)
        tot = slots[0]
        ctot = cslots[0, me]
        for j in range(1, NDEV):
            tot = tot + slots[j]
            ctot = ctot + cslots[j, me]

        def step(g, w_ref, m_ref, v_ref, o):
            d, mn, vn = _adamw_math(w_ref[...], g, m_ref[...], v_ref[...])
            o[0][...], o[1][...], o[2][...], o[3][...] = g, d, mn, vn

        for k in range(ns):
            step(tot[k:k + 1, 0:widths[k]], w_refs[k], m_refs[k], v_refs[k], outs[k])
        step(ctot, wcw_ref, mcw_ref, vcw_ref, cw_outs)
        loss_out[...] = tot[LOSS_ROW:LOSS_ROW + 1, 0:128]
        if carry:
            carry.finish(cins, couts, cscr)

    args = [*gs, loss_row, gcw, *ws, *ms, *vs, wcw, mcw, vcw]
    out_shape = ([jax.ShapeDtypeStruct((1, n), F32) for n in widths for _ in range(4)]
                 + [jax.ShapeDtypeStruct((CWF, D), F32)] * 4 + [jax.ShapeDtypeStruct((1, 128), F32)])
    n_own = len(out_shape)
    res = pl.pallas_call(
        body, in_specs=[VMEM] * len(args) + (carry.in_specs if carry else []),
        out_specs=[VMEM] * n_own + (carry.out_specs if carry else []),
        out_shape=out_shape + (carry.out_shape if carry else []),
        scratch_shapes=[pltpu.VMEM((SROWS, D), F32), pltpu.VMEM((NDEV, SROWS, D), F32),
                        pltpu.VMEM((NDEV, NDEV, CWF, D), F32)]
                       + [pltpu.SemaphoreType.DMA((NDEV - 1,))] * 4 + (carry.scratch if carry else []),
        name="small_step")(*args, *(carry.in_arrays if carry else []))
    per = [res[4 * k:4 * k + 4] for k in range(ns)]
    return per, res[4 * ns:4 * ns + 4], res[n_own - 1], res[n_own:]


def _pack_cw(a):
    flat = a.reshape(a.shape[:-2] + (CK * HD,))
    pad = [(0, 0)] * (flat.ndim - 1) + [(0, CWF * D - CK * HD)]
    return jnp.pad(flat, pad).reshape(a.shape[:-2] + (CWF, D))


def _unpack_cw(v):
    return v.reshape(-1)[:CK * HD].reshape(1, CK, HD)


def kernel(x, ffn1_norm, ffn1_w_gate, ffn1_w_up, ffn1_w_down, mix_norm, w_in, q_norm, k_norm, conv_w, conv_b, conv_ln_g, conv_ln_b, w_out, ffn2_norm, ffn2_w_gate, ffn2_w_up, ffn2_w_down, loss_target, m_ffn1_norm, m_ffn1_w_gate, m_ffn1_w_up, m_ffn1_w_down, m_mix_norm, m_w_in, m_q_norm, m_k_norm, m_conv_w, m_conv_b, m_conv_ln_g, m_conv_ln_b, m_w_out, m_ffn2_norm, m_ffn2_w_gate, m_ffn2_w_up, m_ffn2_w_down, v_ffn1_norm, v_ffn1_w_gate, v_ffn1_w_up, v_ffn1_w_down, v_mix_norm, v_w_in, v_q_norm, v_k_norm, v_conv_w, v_conv_b, v_conv_ln_g, v_conv_ln_b, v_w_out, v_ffn2_norm, v_ffn2_w_gate, v_ffn2_w_up, v_ffn2_w_down):
    P = dict(ffn1_norm=ffn1_norm, ffn1_w_gate=ffn1_w_gate, ffn1_w_up=ffn1_w_up, ffn1_w_down=ffn1_w_down,
             mix_norm=mix_norm, w_in=w_in, q_norm=q_norm, k_norm=k_norm, conv_w=conv_w, conv_b=conv_b,
             conv_ln_g=conv_ln_g, conv_ln_b=conv_ln_b, w_out=w_out, ffn2_norm=ffn2_norm,
             ffn2_w_gate=ffn2_w_gate, ffn2_w_up=ffn2_w_up, ffn2_w_down=ffn2_w_down)
    M = dict(ffn1_norm=m_ffn1_norm, ffn1_w_gate=m_ffn1_w_gate, ffn1_w_up=m_ffn1_w_up, ffn1_w_down=m_ffn1_w_down,
             mix_norm=m_mix_norm, w_in=m_w_in, q_norm=m_q_norm, k_norm=m_k_norm, conv_w=m_conv_w, conv_b=m_conv_b,
             conv_ln_g=m_conv_ln_g, conv_ln_b=m_conv_ln_b, w_out=m_w_out, ffn2_norm=m_ffn2_norm,
             ffn2_w_gate=m_ffn2_w_gate, ffn2_w_up=m_ffn2_w_up, ffn2_w_down=m_ffn2_w_down)
    V = dict(ffn1_norm=v_ffn1_norm, ffn1_w_gate=v_ffn1_w_gate, ffn1_w_up=v_ffn1_w_up, ffn1_w_down=v_ffn1_w_down,
             mix_norm=v_mix_norm, w_in=v_w_in, q_norm=v_q_norm, k_norm=v_k_norm, conv_w=v_conv_w, conv_b=v_conv_b,
             conv_ln_g=v_conv_ln_g, conv_ln_b=v_conv_ln_b, w_out=v_w_out, ffn2_norm=v_ffn2_norm,
             ffn2_w_gate=v_ffn2_w_gate, ffn2_w_up=v_ffn2_w_up, ffn2_w_down=v_ffn2_w_down)
    order = ["ffn1_norm", "ffn1_w_gate", "ffn1_w_up", "ffn1_w_down", "mix_norm", "w_in", "q_norm", "k_norm",
             "conv_w", "conv_b", "conv_ln_g", "conv_ln_b", "w_out", "ffn2_norm", "ffn2_w_gate", "ffn2_w_up",
             "ffn2_w_down"]
    B, S, _ = x.shape
    T = B * S

    bigs = [("wg1", "ffn1_w_gate", True), ("wu1", "ffn1_w_up", True), ("wd1", "ffn1_w_down", False),
            ("win", "w_in", True), ("wout", "w_out", False),
            ("wg2", "ffn2_w_gate", True), ("wu2", "ffn2_w_up", True), ("wd2", "ffn2_w_down", False)]
    hm = lambda a, tr: jnp.transpose(a[0]) if tr else a[0]
    cw_pad = jnp.zeros((32, 128), F32).at[0:CK, 0:HD].set(conv_w[0])
    shard = {ln: (hm(P[pn], tr), BF16) for ln, pn, tr in bigs}
    gathered = _run_carry(_GatherCarry([shard["wg1"], shard["wu1"], (cw_pad, F32)]), "gather_first")
    W = {"wg1": gathered[0], "wu1": gathered[1]}
    cwg = gathered[2].reshape(NDEV, 32, 128)[:, 0:CK, 0:HD]
    W["conv_w"] = jnp.transpose(cwg, (1, 0, 2)).reshape(CK, DC)
    norms = {n: P[n] for n in SMALL_NAMES}
    comm = _Comm({"down_in": ({n: shard[n] for n in ("wd1", "win")}, 0.5),
                  "ffn2": ({n: shard[n] for n in ("wg2", "wu2", "wd2", "wout")}, 0.5)})

    loss_part, gx, _, small = _local_step(x.reshape(T, D), loss_target.reshape(T, D), norms, W, B, S, comm)

    G, Dl, Mn, Vn = {}, {}, {}, {}
    dcw = small["conv_w"].reshape(CK, NDEV, HD).transpose(1, 0, 2)
    loss_row = jnp.zeros((1, 128), F32).at[0, 0].set(loss_part)
    per, cw_outs, loss_out, got = _small_step(
        [small[n] for n in SMALL_NAMES], loss_row, _pack_cw(dcw),
        [P[n] for n in SMALL_NAMES], [M[n] for n in SMALL_NAMES], [V[n] for n in SMALL_NAMES],
        _pack_cw(P["conv_w"][0]), _pack_cw(M["conv_w"][0]), _pack_cw(V["conv_w"][0]), carry=comm.last)
    comm.reduce_done(comm.last, got)
    loss = loss_out[0, 0]
    for n, outs in zip(SMALL_NAMES, per):
        G[n], Dl[n], Mn[n], Vn[n] = outs
    G["conv_w"], Dl["conv_w"], Mn["conv_w"], Vn["conv_w"] = [_unpack_cw(o) for o in cw_outs]

    for ln, pn, tr in bigs:
        outs = _adamw_big(comm.reduced[ln], hm(P[pn], tr), hm(M[pn], tr), hm(V[pn], tr), "adamw_" + ln)
        G[pn], Dl[pn], Mn[pn], Vn[pn] = [(jnp.transpose(o) if tr else o)[None] for o in outs]

    return (loss, gx.reshape(B, S, D), *[G[n] for n in order], *[Dl[n] for n in order],
            *[Mn[n] for n in order], *[Vn[n] for n in order])
```

```python
import functools

import jax
import jax.numpy as jnp
from jax import lax
from jax.experimental import pallas as pl
from jax.experimental.pallas import tpu as pltpu

F32 = jnp.float32
BF16 = jnp.bfloat16

D = 1024
FF = 2816
HD = 64
DA = 512
DC = 512
DIN = 2560
CK = 31
BLK = 128
DILS = (1, 4, 16)
EPS = 1e-6
NDEV = 8
MESH = pl.DeviceIdType.MESH

LR, B1, B2, AEPS, WD, STEP = 0.001, 0.9, 0.999, 1e-08, 0.01, 10

NT = (((1,), (1,)), ((), ()))
TN = (((0,), (0,)), ((), ()))

VMEM_LIMIT = 60 * 1024 * 1024


def _cp(sem=None):
    return pltpu.CompilerParams(dimension_semantics=sem, vmem_limit_bytes=VMEM_LIMIT)


def _sigmoid(x):
    return 0.5 * (jnp.tanh(0.5 * x) + 1.0)


def _pallas(body, args, *, grid, in_specs, out_specs, out_shape, scratch_shapes, sem, name, carry=None):
    if carry is None:
        outs = pl.pallas_call(body, grid=grid, in_specs=in_specs, out_specs=out_specs, out_shape=out_shape,
                              scratch_shapes=scratch_shapes, compiler_params=_cp(sem), name=name)(*args)
        return outs, None
    n_in, n_out, n_scr = len(in_specs), len(out_shape), len(scratch_shapes)
    c_in, c_out = len(carry.in_arrays), len(carry.out_shape)

    def wrapped(*refs):
        ins, refs = refs[:n_in], refs[n_in:]
        cins, refs = refs[:c_in], refs[c_in:]
        outs, refs = refs[:n_out], refs[n_out:]
        couts, refs = refs[:c_out], refs[c_out:]
        scr, cscr = refs[:n_scr], refs[n_scr:]
        ids = [pl.program_id(a) for a in range(len(grid))]
        step = ids[0]
        for i, n in zip(ids[1:], grid[1:]):
            step = step * n + i
        steps = functools.reduce(lambda a, b: a * b, grid)

        @pl.when(step == 0)
        def _():
            carry.start(cins, couts, cscr)

        body(*ins, *outs, *scr)

        if hasattr(carry, "mid"):
            @pl.when(step == int(steps * carry.mid_at))
            def _():
                carry.mid(cins, couts, cscr)

        @pl.when(step == steps - 1)
        def _():
            carry.finish(cins, couts, cscr)

    outs = pl.pallas_call(
        wrapped, grid=grid, in_specs=list(in_specs) + carry.in_specs, out_specs=list(out_specs) + carry.out_specs,
        out_shape=list(out_shape) + carry.out_shape, scratch_shapes=list(scratch_shapes) + carry.scratch,
        compiler_params=_cp(("arbitrary",) * len(grid)), name=name)(*args, *carry.in_arrays)
    return outs[:n_out], outs[n_out:]


FC = 256


def _resident(shape):
    return pl.BlockSpec(shape, lambda *_: (0,) * len(shape), pipeline_mode=pl.Buffered(1))


def _mix_out_ffn_loss(h1, attn, conv, wout, gain, wg, wu, wd, target, name):
    T = h1.shape[0]
    tm = 512
    nt = T // tm

    def body(h1_ref, at_ref, cv_ref, wo_ref, gain_ref, wg_ref, wu_ref, wd_ref, t_ref,
             h2_ref, n_ref, g_ref, u_ref, dout_ref, dyb_ref, sq_ref, a_hbm, a_scr, a_sem):
        t = pl.program_id(0)
        a_out = lambda i: pltpu.make_async_copy(a_scr, a_hbm.at[pl.ds(pl.multiple_of(i * tm, tm), tm), :], a_sem)

        @pl.when(t > 0)
        def _():
            a_out(t - 1).wait()

        xv = (h1_ref[...]
              + jnp.dot(at_ref[...], wo_ref[0:DA, :], preferred_element_type=F32)
              + jnp.dot(cv_ref[...], wo_ref[DA:D, :], preferred_element_type=F32))
        h2_ref[...] = xv
        r = lax.rsqrt(jnp.mean(xv * xv, axis=-1, keepdims=True) + EPS)
        n_ref[...] = (xv * r * gain_ref[...]).astype(BF16)
        for c in range(FF // FC):
            cols = slice(c * FC, (c + 1) * FC)
            nb = n_ref[...]
            g = lax.dot_general(nb, wg_ref[cols, :], NT, preferred_element_type=F32)
            u = lax.dot_general(nb, wu_ref[cols, :], NT, preferred_element_type=F32)
            g_ref[:, cols] = g.astype(BF16)
            u_ref[:, cols] = u.astype(BF16)
            a_scr[:, cols] = (g * _sigmoid(g) * u).astype(BF16)
        a_out(t).start()
        e = h2_ref[...] + 0.5 * jnp.dot(a_scr[...], wd_ref[...], preferred_element_type=F32) - t_ref[...]
        dout = e * (1.0 / D)
        dout_ref[...] = dout
        dyb_ref[...] = (0.5 * dout).astype(BF16)
        sq_ref[...] = jnp.sum(e * e, axis=0, keepdims=True)[None]

        @pl.when(t == nt - 1)
        def _():
            a_out(t).wait()

    row = pl.BlockSpec((tm, D), lambda t: (t, 0))
    half = pl.BlockSpec((tm, DA), lambda t: (t, 0))
    wide = pl.BlockSpec((tm, FF), lambda t: (t, 0))
    outs, _ = _pallas(
        body, (h1, attn, conv, wout, gain, wg, wu, wd, target), grid=(nt,),
        in_specs=[row, half, half, _resident((D, D)), _resident((1, D)), _resident((FF, D)), _resident((FF, D)),
                  _resident((FF, D)), row],
        out_specs=[row, row, wide, wide, row, row, pl.BlockSpec((1, 1, D), lambda t: (t, 0, 0)), HBM],
        out_shape=[jax.ShapeDtypeStruct((T, D), F32), jax.ShapeDtypeStruct((T, D), BF16)]
                  + [jax.ShapeDtypeStruct((T, FF), BF16)] * 2
                  + [jax.ShapeDtypeStruct((T, D), F32), jax.ShapeDtypeStruct((T, D), BF16),
                     jax.ShapeDtypeStruct((nt, 1, D), F32), jax.ShapeDtypeStruct((T, FF), BF16)],
        scratch_shapes=[pltpu.VMEM((tm, FF), BF16), pltpu.SemaphoreType.DMA(())],
        sem=("arbitrary",), name=name)
    return outs


def _ffn_gate_up(x, gain, wg, wu, name, carry=None):
    T = x.shape[0]
    tm = 512

    def body(x_ref, gain_ref, wg_ref, wu_ref, n_ref, g_ref, u_ref, a_ref):
        xv = x_ref[...]
        r = lax.rsqrt(jnp.mean(xv * xv, axis=-1, keepdims=True) + EPS)
        n_ref[...] = (xv * r * gain_ref[...]).astype(BF16)
        for c in range(FF // FC):
            cols = slice(c * FC, (c + 1) * FC)
            nb = n_ref[...]
            g = lax.dot_general(nb, wg_ref[cols, :], NT, preferred_element_type=F32)
            u = lax.dot_general(nb, wu_ref[cols, :], NT, preferred_element_type=F32)
            g_ref[:, cols] = g.astype(BF16)
            u_ref[:, cols] = u.astype(BF16)
            a_ref[:, cols] = (g * _sigmoid(g) * u).astype(BF16)

    row = pl.BlockSpec((tm, D), lambda t: (t, 0))
    wide = pl.BlockSpec((tm, FF), lambda t: (t, 0))
    return _pallas(
        body, (x, gain, wg, wu), grid=(T // tm,),
        in_specs=[row, _resident((1, D)), _resident((FF, D)), _resident((FF, D))],
        out_specs=[row, wide, wide, wide],
        out_shape=[jax.ShapeDtypeStruct((T, D), BF16)] + [jax.ShapeDtypeStruct((T, FF), BF16)] * 3,
        scratch_shapes=[], sem=("parallel",), name=name, carry=carry)


def _ffn_down_mix_in(x, a, wd, gain, win, name):
    T = x.shape[0]
    tm = 512

    def body(x_ref, a_ref, wd_ref, gain_ref, win_ref, h_ref, u_ref, n_ref):
        hv = x_ref[...] + 0.5 * jnp.dot(a_ref[...], wd_ref[...], preferred_element_type=F32)
        h_ref[...] = hv
        r = lax.rsqrt(jnp.mean(hv * hv, axis=-1, keepdims=True) + EPS)
        n_ref[...] = (hv * r * gain_ref[...]).astype(BF16)
        u_ref[...] = lax.dot_general(n_ref[...], win_ref[...], NT, preferred_element_type=F32).astype(BF16)

    row = pl.BlockSpec((tm, D), lambda t: (t, 0))
    wide = pl.BlockSpec((tm, FF), lambda t: (t, 0))
    outs, _ = _pallas(
        body, (x, a, wd, gain, win), grid=(T // tm,),
        in_specs=[row, wide, _resident((FF, D)), _resident((1, D)), _resident((DIN, D))],
        out_specs=[row, pl.BlockSpec((tm, DIN), lambda t: (t, 0)), row],
        out_shape=[jax.ShapeDtypeStruct((T, D), F32), jax.ShapeDtypeStruct((T, DIN), BF16),
                   jax.ShapeDtypeStruct((T, D), BF16)],
        scratch_shapes=[], sem=("parallel",), name=name)
    return outs


def _ffn_bwd_act(dyb, g, u, x, dout, gain, wg, wu, wd, name, carry=None):
    T = x.shape[0]
    tm = 256
    nt = T // tm

    def body(dy_ref, g_ref, u_ref, x_ref, dout_ref, gain_ref, wg_ref, wu_ref, wd_ref,
             dg_ref, du_ref, dx_ref, dgn_ref):
        for c in range(FF // FC):
            cols = slice(c * FC, (c + 1) * FC)
            da = lax.dot_general(dy_ref[...], wd_ref[cols, :], NT, preferred_element_type=F32)
            gv = g_ref[:, cols].astype(F32)
            uv = u_ref[:, cols].astype(F32)
            sg = _sigmoid(gv)
            dg_ref[:, cols] = (da * uv * (sg * (1.0 + gv * (1.0 - sg)))).astype(BF16)
            du_ref[:, cols] = (da * (gv * sg)).astype(BF16)
        dn = (jnp.dot(dg_ref[...], wg_ref[...], preferred_element_type=F32)
              + jnp.dot(du_ref[...], wu_ref[...], preferred_element_type=F32))
        dx, dgain = _rms_bwd_rows(dn, x_ref[...], gain_ref[...])
        dx_ref[...] = dout_ref[...] + dx
        dgn_ref[...] = dgain[None]

    row = pl.BlockSpec((tm, D), lambda t: (t, 0))
    wide = pl.BlockSpec((tm, FF), lambda t: (t, 0))
    return _pallas(
        body, (dyb, g, u, x, dout, gain, wg, wu, wd), grid=(nt,),
        in_specs=[row, wide, wide, row, row, _resident((1, D)), _resident((FF, D)), _resident((FF, D)),
                  _resident((FF, D))],
        out_specs=[wide, wide, row, pl.BlockSpec((1, 1, D), lambda t: (t, 0, 0))],
        out_shape=[jax.ShapeDtypeStruct((T, FF), BF16)] * 2
                  + [jax.ShapeDtypeStruct((T, D), F32), jax.ShapeDtypeStruct((nt, 1, D), F32)],
        scratch_shapes=[], sem=("parallel",), name=name, carry=carry)


def _ffn_bwd_w(lhs, rhs, name, carry=None):
    T = rhs.shape[0]
    tf = 256

    def body(l_ref, r_ref, dw_ref):
        dw_ref[...] = lax.dot_general(l_ref[...], r_ref[...], TN, preferred_element_type=F32).astype(BF16)

    (dw,), got = _pallas(
        body, (lhs, rhs), grid=(FF // tf,),
        in_specs=[pl.BlockSpec((T, tf), lambda f: (0, f)), _resident((T, D))],
        out_specs=[pl.BlockSpec((tf, D), lambda f: (f, 0))], out_shape=[jax.ShapeDtypeStruct((FF, D), BF16)],
        scratch_shapes=[], sem=("parallel",), name=name, carry=carry)
    return dw, got


def _rms_bwd_rows(dn, xv, gain):
    r = lax.rsqrt(jnp.mean(xv * xv, axis=-1, keepdims=True) + EPS)
    xhat = xv * r
    dxhat = dn * gain
    dx = r * (dxhat - xhat * jnp.mean(dxhat * xhat, axis=-1, keepdims=True))
    return dx, jnp.sum(dn * xhat, axis=0, keepdims=True)


def _mix_out_bwd(dh, attn, conv, wout):
    T = dh.shape[0]
    tm = 512
    nt = T // tm

    def body(dh_ref, a_ref, c_ref, w_ref, da_ref, dc_ref, dw_ref, acc_scr):
        t = pl.program_id(0)

        @pl.when(t == 0)
        def _():
            acc_scr[...] = jnp.zeros_like(acc_scr)

        dhb = dh_ref[...].astype(BF16)
        dmix = lax.dot_general(dhb, w_ref[...], NT, preferred_element_type=F32)
        da_ref[...] = dmix[:, 0:DA].astype(BF16)
        dc_ref[...] = dmix[:, DA:D].astype(BF16)
        acc_scr[0:DA, :] += lax.dot_general(a_ref[...], dhb, TN, preferred_element_type=F32)
        acc_scr[DA:D, :] += lax.dot_general(c_ref[...], dhb, TN, preferred_element_type=F32)

        @pl.when(t == nt - 1)
        def _():
            dw_ref[...] = acc_scr[...].astype(BF16)

    row = pl.BlockSpec((tm, D), lambda t: (t, 0))
    half = pl.BlockSpec((tm, DA), lambda t: (t, 0))
    full = pl.BlockSpec((D, D), lambda t: (0, 0))
    return pl.pallas_call(
        body, grid=(nt,), in_specs=[row, half, half, full], out_specs=[half, half, full],
        out_shape=[jax.ShapeDtypeStruct((T, DA), BF16)] * 2 + [jax.ShapeDtypeStruct((D, D), BF16)],
        scratch_shapes=[pltpu.VMEM((D, D), F32)],
        compiler_params=_cp(("arbitrary",)), name="mix_out_bwd")(dh, attn, conv, wout)


def _mix_in_bwd(dparts, win, nb, h, dh, gain):
    T = h.shape[0]
    tm = 512
    nt = T // tm

    def body(d0, d1, d2, d3, d4, w_ref, n_ref, h_ref, dh_ref, gain_ref,
             dw_ref, dx_ref, dyb_ref, dg_ref, acc_scr):
        t = pl.program_id(0)

        @pl.when(t == 0)
        def _():
            acc_scr[...] = jnp.zeros_like(acc_scr)

        n = n_ref[...]
        dn = jnp.zeros((tm, D), F32)
        for i, d_ref in enumerate((d0, d1, d2, d3, d4)):
            dv = d_ref[...]
            dn = dn + jnp.dot(dv, w_ref[i * DA:(i + 1) * DA, :], preferred_element_type=F32)
            acc_scr[i * DA:(i + 1) * DA, :] += lax.dot_general(dv, n, TN, preferred_element_type=F32)
        dx, dgain = _rms_bwd_rows(dn, h_ref[...], gain_ref[...])
        tot = dh_ref[...] + dx
        dx_ref[...] = tot
        dyb_ref[...] = (0.5 * tot).astype(BF16)
        dg_ref[...] = dgain[None]

        @pl.when(t == nt - 1)
        def _():
            dw_ref[...] = acc_scr[...].astype(BF16)

    row = pl.BlockSpec((tm, D), lambda t: (t, 0))
    half = pl.BlockSpec((tm, DA), lambda t: (t, 0))
    full = pl.BlockSpec((DIN, D), lambda t: (0, 0))
    return pl.pallas_call(
        body, grid=(nt,),
        in_specs=[half] * 5 + [full, row, row, row, pl.BlockSpec((1, D), lambda t: (0, 0))],
        out_specs=[full, row, row, pl.BlockSpec((1, 1, D), lambda t: (t, 0, 0))],
        out_shape=[jax.ShapeDtypeStruct((DIN, D), BF16), jax.ShapeDtypeStruct((T, D), F32),
                   jax.ShapeDtypeStruct((T, D), BF16), jax.ShapeDtypeStruct((nt, 1, D), F32)],
        scratch_shapes=[pltpu.VMEM((DIN, D), F32)],
        compiler_params=_cp(("arbitrary",)), name="mix_in_bwd")(*dparts, win, nb, h, dh, gain)


def _head_masks():
    lane = lax.broadcasted_iota(jnp.int32, (1, 2 * HD), 1)
    m0 = lane < HD
    return m0, jnp.logical_not(m0)


def _stack_heads(v, m0, m1):
    z = jnp.zeros_like(v)
    return jnp.concatenate([jnp.where(m0, v, z), jnp.where(m1, v, z)], axis=0)


def _unstack_heads(v2, m0):
    return jnp.where(m0, v2[0:BLK], v2[BLK:2 * BLK])


def _head_sums(xv):
    ri = lax.broadcasted_iota(jnp.int32, (2 * HD, 2 * HD), 0)
    ci = lax.broadcasted_iota(jnp.int32, (2 * HD, 2 * HD), 1)
    ones = jnp.where((ri < HD) == (ci < HD), 1.0, 0.0).astype(BF16)
    hi = xv.astype(BF16)
    lo = (xv - hi.astype(F32)).astype(BF16)
    return (jnp.dot(hi, ones, preferred_element_type=F32) + jnp.dot(lo, ones, preferred_element_type=F32))


def _head_rms(xv):
    return lax.rsqrt(_head_sums(xv * xv) * (1.0 / HD) + EPS)


def _band_mask(first):
    qi = lax.broadcasted_iota(jnp.int32, (BLK, 2 * BLK), 0)
    ci = lax.broadcasted_iota(jnp.int32, (BLK, 2 * BLK), 1)
    band = (ci >= qi) & (ci <= qi + BLK)
    return band & ((ci >= BLK) | jnp.logical_not(first))


def _block_rows(j, d, seg):
    r, n = j // seg, j % seg
    start = r + (d * BLK) * n
    first = n == 0
    prev = jnp.where(first, start, start - d * BLK)
    return pl.ds(start, BLK, stride=d), pl.ds(prev, BLK, stride=d), first


def _block_keys(refs, cur, prev, first, single):
    if single:
        qi = lax.broadcasted_iota(jnp.int32, (BLK, BLK), 0)
        ci = lax.broadcasted_iota(jnp.int32, (BLK, BLK), 1)
        return [r[cur, :].astype(BF16) for r in refs], ci <= qi
    return ([jnp.concatenate([r[prev, :], r[cur, :]], axis=0).astype(BF16) for r in refs], _band_mask(first))


def _attn_fwd(u, qg2, kg2, B, S, carry=None):
    T = B * S
    NB = S // BLK
    scale = HD ** -0.5

    def body(q_ref, k_ref, v_ref, qg_ref, kg_ref, o_ref, lse_ref, qn, kn, vn, os_, ls_):
        m0, m1 = _head_masks()
        qv = q_ref[...].astype(F32)
        qn[...] = qv * _head_rms(qv) * (qg_ref[...] * scale)
        kv = k_ref[...].astype(F32)
        kn[...] = kv * _head_rms(kv) * kg_ref[...]
        vn[...] = v_ref[...].astype(F32)

        for i, d in enumerate(DILS):
            seg = NB // d

            def blk(j, c, i=i, d=d, seg=seg):
                cur, prev, first = _block_rows(j, d, seg)
                q2 = _stack_heads(qn[cur, :].astype(BF16), m0, m1)
                (kk, vv), mask = _block_keys((kn, vn), cur, prev, first, False)
                s = lax.dot_general(q2, kk, NT, preferred_element_type=F32)
                s = jnp.where(jnp.concatenate([mask, mask], axis=0), s, -1e30)
                mx = jnp.max(s, axis=-1, keepdims=True)
                p = jnp.exp(s - mx)
                l = jnp.sum(p, axis=-1, keepdims=True)
                o2 = jnp.dot((p * (1.0 / l)).astype(BF16), vv, preferred_element_type=F32)
                os_[i, cur, :] = _unstack_heads(o2, m0)
                ls_[i, cur, :] = _unstack_heads(mx + jnp.log(l), m0)
                return c

            lax.fori_loop(0, NB, blk, 0, unroll=8)

        def comb(c, carry):
            rows = pl.ds(pl.multiple_of(c * 256, 256), 256)
            l0, l1, l2 = ls_[0, rows, :], ls_[1, rows, :], ls_[2, rows, :]
            mx = jnp.maximum(jnp.maximum(l0, l1), l2)
            e0, e1, e2 = jnp.exp(l0 - mx), jnp.exp(l1 - mx), jnp.exp(l2 - mx)
            tot = e0 + e1 + e2
            inv = 1.0 / tot
            o = (e0 * os_[0, rows, :] + e1 * os_[1, rows, :] + e2 * os_[2, rows, :]) * inv
            o_ref[rows, :] = o.astype(BF16)
            lse_ref[rows, :] = mx + jnp.log(tot)
            return carry

        lax.fori_loop(0, S // 256, comb, 0)

    pair = 2 * HD
    blk_spec = lambda off: pl.BlockSpec((S, pair), lambda b, p, off=off: (b, off + p))
    gspec = pl.BlockSpec((1, pair), lambda b, p: (0, 0))
    return _pallas(
        body, (u, u, u, qg2, kg2), grid=(B, DA // pair),
        in_specs=[blk_spec(0), blk_spec(DA // pair), blk_spec(2 * DA // pair), gspec, gspec],
        out_specs=[blk_spec(0), blk_spec(0)],
        out_shape=[jax.ShapeDtypeStruct((T, DA), BF16), jax.ShapeDtypeStruct((T, DA), F32)],
        scratch_shapes=[pltpu.VMEM((S, pair), F32)] * 3 + [pltpu.VMEM((3, S, pair), F32)] * 2,
        sem=("parallel", "parallel"), name="attn_fwd", carry=carry)


def _attn_bwd(u, attn, dattn, lse, qg2, kg2, B, S, carry=None):
    T = B * S
    NB = S // BLK
    scale = HD ** -0.5
    pair = 2 * HD

    def body(q_ref, k_ref, v_ref, o_ref, do_ref, lse_ref, qg_ref, kg_ref,
             dq_ref, dk_ref, dv_ref, dgn_ref,
             qn, kn, vn, don, ldl, accq, acck, accv, rq, rk):
        m0, m1 = _head_masks()
        lane = lax.broadcasted_iota(jnp.int32, (1, pair), 1)
        qv = q_ref[...].astype(F32)
        rq[...] = _head_rms(qv)
        qn[...] = qv * rq[...] * (qg_ref[...] * scale)
        kv = k_ref[...].astype(F32)
        rk[...] = _head_rms(kv)
        kn[...] = kv * rk[...] * kg_ref[...]
        vn[...] = v_ref[...].astype(F32)
        dov = do_ref[...].astype(F32)
        don[...] = dov
        ldl[...] = jnp.where((lane % HD) < HD // 2, lse_ref[...], _head_sums(dov * o_ref[...].astype(F32)))

        for i, d in enumerate(DILS):
            seg = NB // d

            def blk(j, c, i=i, d=d, seg=seg):
                cur, prev, first = _block_rows(j, d, seg)
                q2 = _stack_heads(qn[cur, :].astype(BF16), m0, m1)
                do2 = _stack_heads(don[cur, :].astype(BF16), m0, m1)
                (kk, vv), mask = _block_keys((kn, vn), cur, prev, first, seg == 1)
                ldv = ldl[cur, :]
                lse2 = jnp.concatenate([ldv[:, 0:1], ldv[:, HD:HD + 1]], axis=0)
                dl2 = jnp.concatenate([ldv[:, HD // 2:HD // 2 + 1], ldv[:, HD + HD // 2:HD + HD // 2 + 1]], axis=0)
                s = lax.dot_general(q2, kk, NT, preferred_element_type=F32)
                p = jnp.where(jnp.concatenate([mask, mask], axis=0), jnp.exp(s - lse2), 0.0)
                dp = lax.dot_general(do2, vv, NT, preferred_element_type=F32)
                ds = (p * (dp - dl2)).astype(BF16)
                dq_acc = _unstack_heads(jnp.dot(ds, kk, preferred_element_type=F32), m0)
                dk_acc = lax.dot_general(ds, q2, TN, preferred_element_type=F32)
                dv_acc = lax.dot_general(p.astype(BF16), do2, TN, preferred_element_type=F32)
                if i == 0:
                    accq[cur, :] = dq_acc
                    acck[cur, :] = dk_acc[BLK:2 * BLK]
                    accv[cur, :] = dv_acc[BLK:2 * BLK]
                    acck[prev, :] += dk_acc[0:BLK]
                    accv[prev, :] += dv_acc[0:BLK]
                elif seg == 1:
                    accq[cur, :] += dq_acc
                    acck[cur, :] += dk_acc
                    accv[cur, :] += dv_acc
                else:
                    accq[cur, :] += dq_acc
                    acck[prev, :] += dk_acc[0:BLK]
                    acck[cur, :] += dk_acc[BLK:2 * BLK]
                    accv[prev, :] += dv_acc[0:BLK]
                    accv[cur, :] += dv_acc[BLK:2 * BLK]
                return c

            lax.fori_loop(0, NB, blk, 0, unroll=8)

        def norm_bwd(x_ref, r_ref, dn, gain):
            r = r_ref[...]
            xhat = x_ref[...].astype(F32) * r
            dxhat = dn * gain
            dx = r * (dxhat - xhat * (_head_sums(dxhat * xhat) * (1.0 / HD)))
            return dx, jnp.sum(dn * xhat, axis=0, keepdims=True)

        dq, dgq = norm_bwd(q_ref, rq, accq[...], qg_ref[...] * scale)
        dk, dgk = norm_bwd(k_ref, rk, acck[...], kg_ref[...])
        dq_ref[...] = dq.astype(BF16)
        dk_ref[...] = dk.astype(BF16)
        dv_ref[...] = accv[...].astype(BF16)
        dgn_ref[...] = jnp.concatenate([dgq * scale, dgk, jnp.zeros((6, pair), F32)], axis=0)[None]

    blk_spec = lambda off: pl.BlockSpec((S, pair), lambda b, p, off=off: (b, off + p))
    gspec = pl.BlockSpec((1, pair), lambda b, p: (0, 0))
    np_ = DA // pair
    return _pallas(
        body, (u, u, u, attn, dattn, lse, qg2, kg2), grid=(B, np_),
        in_specs=[blk_spec(0), blk_spec(np_), blk_spec(2 * np_), blk_spec(0), blk_spec(0), blk_spec(0),
                  gspec, gspec],
        out_specs=[blk_spec(0), blk_spec(0), blk_spec(0),
                   pl.BlockSpec((1, 8, pair), lambda b, p: (b * np_ + p, 0, 0))],
        out_shape=[jax.ShapeDtypeStruct((T, DA), BF16)] * 3 + [jax.ShapeDtypeStruct((B * np_, 8, pair), F32)],
        scratch_shapes=[pltpu.VMEM((S, pair), F32)] * 10,
        sem=("parallel", "parallel"), name="attn_bwd", carry=carry)


CT = 32
CPAD = 32


def _shifted(win, offsets):
    rolled, out = {}, {}
    n = win.shape[0]
    for o in offsets:
        sub = o % 8
        if sub not in rolled:
            rolled[sub] = win if sub == 0 else pltpu.roll(win, n - sub, 0)
        out[o] = rolled[sub][o - sub:o - sub + CT, :]
    return out


def _ln_fwd(y, g, b):
    mu = jnp.mean(y, axis=-1, keepdims=True)
    yc = y - mu
    rstd = lax.rsqrt(jnp.mean(yc * yc, axis=-1, keepdims=True) + EPS)
    xhat = yc * rstd
    return xhat, rstd, xhat * g + b


def _fill_glu(ca_ref, cg_ref, glu, S):
    glu[pl.ds(0, CPAD), :] = jnp.zeros((CPAD, DC), F32)

    def fill(i, c):
        rows = pl.ds(pl.multiple_of(i * 256, 256), 256)
        a = ca_ref[rows, :].astype(F32)
        gt = cg_ref[rows, :].astype(F32)
        glu[pl.ds(pl.multiple_of(CPAD + i * 256, CT), 256), :] = a * _sigmoid(gt)
        return c

    lax.fori_loop(0, S // 256, fill, 0)


def _conv_fwd(u, cw, cb, lg, lb, B, S):
    T = B * S

    def body(ca_ref, cg_ref, w_ref, b_ref, lg_ref, lb_ref, o_ref, y_ref, glu):
        _fill_glu(ca_ref, cg_ref, glu, S)

        def step(i, c):
            t0 = pl.multiple_of(i * CT, CT)
            win = glu[pl.ds(t0, 2 * CT), :]
            acc = jnp.zeros((CT, DC), F32) + b_ref[...]
            taps = _shifted(win, [k + 2 for k in range(CK)])
            for k in range(CK):
                acc = acc + taps[k + 2] * w_ref[k:k + 1, :]
            y_ref[pl.ds(t0, CT), :] = acc
            _, _, z = _ln_fwd(acc, lg_ref[...], lb_ref[...])
            o_ref[pl.ds(t0, CT), :] = (z * _sigmoid(z)).astype(BF16)
            return c

        lax.fori_loop(0, S // CT, step, 0, unroll=4)

    vec = pl.BlockSpec((1, DC), lambda b: (0, 0))
    return pl.pallas_call(
        body, grid=(B,),
        in_specs=[pl.BlockSpec((S, DC), lambda b: (b, 3)), pl.BlockSpec((S, DC), lambda b: (b, 4)),
                  pl.BlockSpec((CT, DC), lambda b: (0, 0)), vec, vec, vec],
        out_specs=[pl.BlockSpec((S, DC), lambda b: (b, 0))] * 2,
        out_shape=[jax.ShapeDtypeStruct((T, DC), BF16), jax.ShapeDtypeStruct((T, DC), F32)],
        scratch_shapes=[pltpu.VMEM((CPAD + S, DC), F32)],
        compiler_params=_cp(("parallel",)), name="conv_fwd")(u, u, cw, cb, lg, lb)


def _conv_bwd(u, y, dconv, cw, lg, lb, B, S):
    T = B * S

    def body(ca_ref, cg_ref, y_ref, dc_ref, w_ref, lg_ref, lb_ref,
             dca_ref, dcg_ref, dw_ref, ds_ref, glu, dyp, dwacc):
        _fill_glu(ca_ref, cg_ref, glu, S)
        dyp[pl.ds(S, CPAD), :] = jnp.zeros((CPAD, DC), F32)
        lgv, lbv = lg_ref[...], lb_ref[...]

        def sum8(v):
            return functools.reduce(jnp.add, [v[r:r + 8] for r in range(0, v.shape[0], 8)])

        P1 = 4 * CT

        def p1(i, carry):
            sb, sg, sl = carry
            t0 = pl.multiple_of(i * P1, P1)
            xhat, rstd, z = _ln_fwd(y_ref[pl.ds(t0, P1), :], lgv, lbv)
            sz = _sigmoid(z)
            dz = dc_ref[pl.ds(t0, P1), :].astype(F32) * (sz * (1.0 + z * (1.0 - sz)))
            dxhat = dz * lgv
            dy = rstd * (dxhat - jnp.mean(dxhat, axis=-1, keepdims=True)
                         - xhat * jnp.mean(dxhat * xhat, axis=-1, keepdims=True))
            dyp[pl.ds(t0, P1), :] = dy
            return sb + sum8(dy), sg + sum8(dz * xhat), sl + sum8(dz)

        z8 = jnp.zeros((8, DC), F32)
        sb, sg, sl = lax.fori_loop(0, S // P1, p1, (z8, z8, z8))
        rs = lambda v: jnp.sum(v, axis=0, keepdims=True)
        ds_ref[...] = jnp.concatenate([rs(sb), rs(sg), rs(sl), jnp.zeros((5, DC), F32)], axis=0)[None]

        def p2(i, c):
            t0 = pl.multiple_of(i * CT, CT)
            win = dyp[pl.ds(t0, 2 * CT), :]
            acc = jnp.zeros((CT, DC), F32)
            taps = _shifted(win, [30 - k for k in range(CK)])
            for k in range(CK):
                acc = acc + taps[30 - k] * w_ref[k:k + 1, :]
            a = ca_ref[pl.ds(t0, CT), :].astype(F32)
            sgt = _sigmoid(cg_ref[pl.ds(t0, CT), :].astype(F32))
            dca_ref[pl.ds(t0, CT), :] = (acc * sgt).astype(BF16)
            dcg_ref[pl.ds(t0, CT), :] = (acc * a * sgt * (1.0 - sgt)).astype(BF16)
            return c

        lax.fori_loop(0, S // CT, p2, 0)

        dwacc[...] = jnp.zeros_like(dwacc)

        def p3(i, c):
            t0 = pl.multiple_of(i * CT, CT)
            win = glu[pl.ds(t0, 2 * CT), :]
            dy = dyp[pl.ds(t0, CT), :]
            for k in range(CK):
                dwacc[k] += sum8(dy * win[k + 2:k + 2 + CT, :])
            return c

        lax.fori_loop(0, S // CT, p3, 0)
        dw_ref[...] = jnp.sum(dwacc[...], axis=1)[None]

    vec = pl.BlockSpec((1, DC), lambda b: (0, 0))
    seq = pl.BlockSpec((S, DC), lambda b: (b, 0))
    return pl.pallas_call(
        body, grid=(B,),
        in_specs=[pl.BlockSpec((S, DC), lambda b: (b, 3)), pl.BlockSpec((S, DC), lambda b: (b, 4)),
                  seq, seq, pl.BlockSpec((CT, DC), lambda b: (0, 0)), vec, vec],
        out_specs=[seq, seq, pl.BlockSpec((1, CT, DC), lambda b: (b, 0, 0)),
                   pl.BlockSpec((1, 8, DC), lambda b: (b, 0, 0))],
        out_shape=[jax.ShapeDtypeStruct((T, DC), BF16)] * 2
                  + [jax.ShapeDtypeStruct((B, CT, DC), F32), jax.ShapeDtypeStruct((B, 8, DC), F32)],
        scratch_shapes=[pltpu.VMEM((CPAD + S, DC), F32), pltpu.VMEM((S + CPAD, DC), F32),
                        pltpu.VMEM((CT, 8, DC), F32)],
        compiler_params=_cp(("parallel",)), name="conv_bwd")(u, u, y, dconv, cw, lg, lb)


def _local_step(x, target, norms, W, B, S, comm=None):
    qg2 = jnp.concatenate([norms["q_norm"], norms["q_norm"]], axis=1)
    kg2 = jnp.concatenate([norms["k_norm"], norms["k_norm"]], axis=1)
    cw = jnp.concatenate([W["conv_w"], jnp.zeros((1, DC), F32)], axis=0)

    W = dict(W)
    (n1, g1, u1, act1), got = _ffn_gate_up(x, norms["ffn1_norm"], W["wg1"], W["wu1"], "ffn1_gate_up",
                                           carry=comm.gathers["down_in"] if comm else None)
    if comm:
        W.update(comm.gathered("down_in", got))
    h1, u, n2 = _ffn_down_mix_in(x, act1, W["wd1"], norms["mix_norm"], W["win"], "ffn1_down_mix_in")
    (attn, lse), got = _attn_fwd(u, qg2, kg2, B, S, carry=comm.gathers["ffn2"] if comm else None)
    if comm:
        W = dict(W, **comm.gathered("ffn2", got))
    conv, y = _conv_fwd(u, cw, norms["conv_b"], norms["conv_ln_g"], norms["conv_ln_b"], B, S)
    h2, n3, g2, u2, dout, dyb, sq, act2 = _mix_out_ffn_loss(h1, attn, conv, W["wout"], norms["ffn2_norm"],
                                                            W["wg2"], W["wu2"], W["wd2"], target, "ffn2_fwd")
    loss = (0.5 / D) * jnp.sum(sq)

    (dg2, du2, dh2, dgn_ffn2), _ = _ffn_bwd_act(dyb, g2, u2, h2, dout, norms["ffn2_norm"],
                                               W["wg2"], W["wu2"], W["wd2"], "ffn2_bwd_act")
    dwd2, _ = _ffn_bwd_w(act2, dyb, "ffn2_bwd_wd")
    dwg2, _ = _ffn_bwd_w(dg2, n3, "ffn2_bwd_wg")
    dwu2, _ = _ffn_bwd_w(du2, n3, "ffn2_bwd_wu")
    dattn, dconv, dwout = _mix_out_bwd(dh2, attn, conv, W["wout"])
    carry = comm.reduce_start("ffn2", {"wg2": dwg2, "wu2": dwu2, "wd2": dwd2, "wout": dwout}) if comm else None
    (dq, dk, dv, dgn_qk), got = _attn_bwd(u, attn, dattn, lse, qg2, kg2, B, S, carry=carry)
    if comm:
        comm.reduce_done(carry, got)
    dca, dcg, dcw, dcs = _conv_bwd(u, y, dconv, cw, norms["conv_ln_g"], norms["conv_ln_b"], B, S)
    dwin, dh1, dyb1, dgn_mix = _mix_in_bwd((dq, dk, dv, dca, dcg), W["win"], n2, h1, dh2, norms["mix_norm"])
    (dg1, du1, gx, dgn_ffn1), _ = _ffn_bwd_act(dyb1, g1, u1, x, dh1, norms["ffn1_norm"],
                                              W["wg1"], W["wu1"], W["wd1"], "ffn1_bwd_act")
    carry = comm.reduce_start("win", {"win": dwin}) if comm else None
    dwd1, got = _ffn_bwd_w(act1, dyb1, "ffn1_bwd_wd", carry=carry)
    if comm:
        comm.reduce_done(carry, got)
        carry = comm.reduce_start("wd1", {"wd1": dwd1})
    dwg1, got = _ffn_bwd_w(dg1, n1, "ffn1_bwd_wg", carry=carry)
    if comm:
        comm.reduce_done(carry, got)
        carry = comm.reduce_start("wg1", {"wg1": dwg1})
    dwu1, got = _ffn_bwd_w(du1, n1, "ffn1_bwd_wu", carry=carry)
    if comm:
        comm.reduce_done(carry, got)
        comm.last = comm.reduce_start("wu1", {"wu1": dwu1})

    qk = jnp.sum(dgn_qk, axis=0)
    cs = jnp.sum(dcs, axis=0)
    small = {
        "ffn1_norm": jnp.sum(dgn_ffn1, axis=0),
        "mix_norm": jnp.sum(dgn_mix, axis=0),
        "q_norm": qk[0:1, 0:HD] + qk[0:1, HD:2 * HD],
        "k_norm": qk[1:2, 0:HD] + qk[1:2, HD:2 * HD],
        "conv_w": jnp.sum(dcw, axis=0)[0:CK],
        "conv_b": cs[0:1],
        "conv_ln_g": cs[1:2],
        "conv_ln_b": cs[2:3],
        "ffn2_norm": jnp.sum(dgn_ffn2, axis=0),
    }
    big = {"wg1": dwg1, "wu1": dwu1, "wd1": dwd1, "win": dwin, "wout": dwout,
           "wg2": dwg2, "wu2": dwu2, "wd2": dwd2}
    return loss, gx, big, small


HBM = pl.BlockSpec(memory_space=pltpu.HBM)
VMEM = pl.BlockSpec(memory_space=pltpu.VMEM)


def _place():
    return lax.axis_index("x"), lax.axis_index("y"), lax.axis_index("c")


class _GatherCarry:
    def __init__(self, shards, mid_at=0.5):
        nt = len(shards)
        self.mid_at = mid_at
        self.shards = shards
        self.in_arrays = [s for s, _ in shards]
        self.in_specs = [VMEM] * nt
        self.out_shape = [jax.ShapeDtypeStruct((NDEV * s.shape[0], s.shape[1]), dt) for s, dt in shards]
        self.out_specs = [HBM] * nt
        self.scratch = ([pltpu.VMEM(s.shape, dt) for s, dt in shards]
                        + [pltpu.SemaphoreType.DMA((nt, 7)), pltpu.SemaphoreType.DMA((nt, 7)),
                           pltpu.SemaphoreType.DMA((nt,))])

    def _copies(self, outs, scr):
        nt = len(self.shards)
        stages = scr[:nt]
        send_sems, recv_sems, local_sems = scr[nt:]
        x, y, c = _place()
        me, sibling = (x, y, c), (x, y, 1 - c)
        xn, yn, diag = (1 - x, y, c), (x, 1 - y, c), (1 - x, 1 - y, c)
        via = (x ^ c, y ^ (1 - c), c)
        onto = (x ^ (1 - c), y ^ c, c)

        def rows(t, px, py, pc):
            r = self.shards[t][0].shape[0]
            return outs[t].at[pl.ds((4 * px + 2 * py + pc) * r, r), :]

        def copy(t, k, block, to, src=None):
            return pltpu.make_async_remote_copy(
                src_ref=rows(t, *block) if src is None else src, dst_ref=rows(t, *block),
                send_sem=send_sems.at[t, k], recv_sem=recv_sems.at[t, k],
                device_id=to, device_id_type=MESH)

        sib = lambda b: (b[0], b[1], 1 - c)
        return dict(
            local=[pltpu.make_async_copy(stages[t], rows(t, *me), local_sems.at[t]) for t in range(nt)],
            own=[[copy(t, 0, me, sibling, src=stages[t]), copy(t, 1, me, xn, src=stages[t]),
                  copy(t, 2, me, yn, src=stages[t])] for t in range(nt)],
            relay=[copy(t, 3, via, onto) for t in range(nt)],
            down=[[copy(t, 4, xn, sibling), copy(t, 5, yn, sibling)] for t in range(nt)],
            down_diag=[copy(t, 6, diag, sibling) for t in range(nt)],
            got_xy=[[copy(t, 1, xn, me), copy(t, 2, yn, me)] for t in range(nt)],
            got_diag=[copy(t, 3, diag, me) for t in range(nt)],
            got_sib=[[copy(t, 0, sibling, me), copy(t, 4, sib(xn), me), copy(t, 5, sib(yn), me),
                      copy(t, 6, sib(diag), me)] for t in range(nt)])

    def start(self, ins, outs, scr):
        cps = self._copies(outs, scr)
        for t, (_, dt) in enumerate(self.shards):
            scr[t][...] = ins[t][...].astype(dt)
            for cp in [cps["local"][t]] + cps["own"][t]:
                cp.start()

    def mid(self, ins, outs, scr):
        cps = self._copies(outs, scr)
        for t in range(len(self.shards)):
            for cp in cps["got_xy"][t]:
                cp.wait_recv()
            for cp in [cps["relay"][t]] + cps["down"][t]:
                cp.start()

    def finish(self, ins, outs, scr):
        cps = self._copies(outs, scr)
        for t in range(len(self.shards)):
            cps["got_diag"][t].wait_recv()
            cps["down_diag"][t].start()
        for t in range(len(self.shards)):
            for cp in cps["got_sib"][t]:
                cp.wait_recv()
            for cp in cps["own"][t] + [cps["relay"][t]] + cps["down"][t] + [cps["down_diag"][t]]:
                cp.wait_send()
            cps["local"][t].wait()


def _run_carry(carry, name):
    def body(*refs):
        n_in, n_out = len(carry.in_arrays), len(carry.out_shape)
        ins, outs, scr = refs[:n_in], refs[n_in:n_in + n_out], refs[n_in + n_out:]
        carry.start(ins, outs, scr)
        if hasattr(carry, "mid"):
            carry.mid(ins, outs, scr)
        carry.finish(ins, outs, scr)

    return pl.pallas_call(
        body, in_specs=carry.in_specs, out_specs=carry.out_specs, out_shape=carry.out_shape,
        scratch_shapes=carry.scratch, compiler_params=pltpu.CompilerParams(vmem_limit_bytes=VMEM_LIMIT),
        name=name)(*carry.in_arrays)


def _sibling_reduce(grads, name):
    nt = len(grads)
    g4 = [g.reshape(4, 2, g.shape[0] // NDEV, g.shape[1]) for g in grads]

    def body(*refs):
        ins, outs = refs[:nt], refs[nt:2 * nt]
        recv, own = refs[2 * nt:3 * nt], refs[3 * nt:4 * nt]
        send_sems, recv_sems, load_sems, store_sems = refs[4 * nt:]
        x, y, c = _place()
        sends = [pltpu.make_async_remote_copy(
            src_ref=ins[t].at[:, 1 - c], dst_ref=recv[t], send_sem=send_sems.at[t], recv_sem=recv_sems.at[t],
            device_id=(x, y, 1 - c), device_id_type=MESH) for t in range(nt)]
        loads = [pltpu.make_async_copy(ins[t].at[:, c], own[t], load_sems.at[t]) for t in range(nt)]
        stores = [pltpu.make_async_copy(own[t], outs[t], store_sems.at[t]) for t in range(nt)]
        for cp in sends + loads:
            cp.start()
        for t in range(nt):
            loads[t].wait()
            sends[t].wait_recv()
            for q in range(4):
                own[t][q] = (own[t][q].astype(F32) + recv[t][q].astype(F32)).astype(BF16)
            stores[t].start()
        for t in range(nt):
            sends[t].wait_send()
            stores[t].wait()

    blocks = [(4,) + g.shape[2:] for g in g4]
    return pl.pallas_call(
        body, in_specs=[HBM] * nt, out_specs=[HBM] * nt,
        out_shape=[jax.ShapeDtypeStruct(b, BF16) for b in blocks],
        scratch_shapes=[pltpu.VMEM(b, BF16) for b in blocks] * 2 + [pltpu.SemaphoreType.DMA((nt,))] * 4,
        compiler_params=pltpu.CompilerParams(vmem_limit_bytes=VMEM_LIMIT), name=name)(*g4)


class _ExchangeCarry:
    def __init__(self, names, parts, mid_at=0.5):
        nt = len(parts)
        self.mid_at = mid_at
        self.names = names
        self.in_arrays = list(parts)
        self.in_specs = [HBM] * nt
        self.out_shape = [jax.ShapeDtypeStruct((3,) + p.shape[1:], BF16) for p in parts]
        self.out_specs = [HBM] * nt
        self.scratch = ([pltpu.VMEM(p.shape[1:], BF16) for p in parts] * 2
                        + [pltpu.SemaphoreType.DMA((nt, 3)), pltpu.SemaphoreType.DMA((nt, 3)),
                           pltpu.SemaphoreType.DMA((nt,)), pltpu.SemaphoreType.DMA((nt,))])

    def _copies(self, ins, outs, scr):
        nt = len(ins)
        relayed, mine = scr[:nt], scr[nt:2 * nt]
        send_sems, recv_sems, local_sems, load_sems = scr[2 * nt:]
        x, y, c = _place()
        q = lambda cx, cy: 2 * cx + cy
        near, far = (x ^ c, y ^ (1 - c)), (x ^ (1 - c), y ^ c)

        def remote(t, k, src, dst, chip):
            return pltpu.make_async_remote_copy(
                src_ref=src, dst_ref=dst, send_sem=send_sems.at[t, k], recv_sem=recv_sems.at[t, k],
                device_id=(*chip, c), device_id_type=MESH)

        return dict(
            keep=[pltpu.make_async_copy(ins[t].at[q(x, y)], outs[t].at[0], local_sems.at[t]) for t in range(nt)],
            load=[pltpu.make_async_copy(ins[t].at[q(*far)], mine[t], load_sems.at[t]) for t in range(nt)],
            direct=[remote(t, 0, ins[t].at[q(*near)], outs[t].at[1], near) for t in range(nt)],
            relay=[remote(t, 1, ins[t].at[q(1 - x, 1 - y)], relayed[t], near) for t in range(nt)],
            merged=[remote(t, 2, mine[t], outs[t].at[2], far) for t in range(nt)],
            relayed=relayed, mine=mine)

    def start(self, ins, outs, scr):
        cps = self._copies(ins, outs, scr)
        for t in range(len(ins)):
            for kind in ("keep", "load", "direct", "relay"):
                cps[kind][t].start()

    def mid(self, ins, outs, scr):
        cps = self._copies(ins, outs, scr)
        for t in range(len(ins)):
            cps["load"][t].wait()
            cps["relay"][t].wait_recv()
            cps["mine"][t][...] = (cps["mine"][t][...].astype(F32) + cps["relayed"][t][...].astype(F32)).astype(BF16)
            cps["merged"][t].start()

    def finish(self, ins, outs, scr):
        cps = self._copies(ins, outs, scr)
        for t in range(len(ins)):
            cps["direct"][t].wait()
            cps["relay"][t].wait_send()
            cps["merged"][t].wait()
            cps["keep"][t].wait()


class _Comm:
    def __init__(self, groups):
        self.names = {tag: list(g) for tag, (g, _) in groups.items()}
        self.gathers = {tag: _GatherCarry(list(g.values()), mid_at) for tag, (g, mid_at) in groups.items()}
        self.reduced = {}

    def gathered(self, tag, outs):
        return dict(zip(self.names[tag], outs))

    def reduce_start(self, tag, grads, mid_at=0.5):
        names = list(grads)
        parts = _sibling_reduce([grads[n] for n in names], "sibling_reduce_" + tag)
        return _ExchangeCarry(names, parts, mid_at)

    def reduce_done(self, carry, outs):
        self.reduced.update(zip(carry.names, outs))


def _adamw_math(w, g, m, v):
    m = B1 * m + (1.0 - B1) * g
    v = B2 * v + (1.0 - B2) * (g * g)
    m_hat = m / (1.0 - B1 ** STEP)
    v_hat = v / (1.0 - B2 ** STEP)
    delta = -LR * (m_hat / (jnp.sqrt(v_hat) + AEPS) + WD * w)
    return delta, m, v


def _adamw_big(recv, w, m, v, name):
    def body(r_ref, w_ref, m_ref, v_ref, g_ref, d_ref, mo_ref, vo_ref):
        g = r_ref[0].astype(F32)
        for q in range(1, 3):
            g = g + r_ref[q].astype(F32)
        d, mn, vn = _adamw_math(w_ref[...], g, m_ref[...], v_ref[...])
        g_ref[...] = g
        d_ref[...] = d
        mo_ref[...] = mn
        vo_ref[...] = vn

    rows, n = w.shape
    tr = rows // 2
    row = pl.BlockSpec((tr, n), lambda t: (t, 0))
    return pl.pallas_call(
        body, grid=(2,), in_specs=[pl.BlockSpec((3, tr, n), lambda t: (0, t, 0)), row, row, row],
        out_specs=[row] * 4, out_shape=[jax.ShapeDtypeStruct(w.shape, F32)] * 4,
        compiler_params=_cp(("parallel",)), name=name)(recv, w, m, v)


SMALL_NAMES = ("ffn1_norm", "mix_norm", "ffn2_norm", "conv_b", "conv_ln_g", "conv_ln_b", "q_norm", "k_norm")
SROWS = 16
LOSS_ROW = len(SMALL_NAMES)
CWF = 2


def _small_step(gs, loss_row, gcw, ws, ms, vs, wcw, mcw, vcw, carry=None):
    ns = len(SMALL_NAMES)
    widths = [g.shape[1] for g in gs]

    def body(*refs):
        it = iter(refs)
        take = lambda n: [next(it) for _ in range(n)]
        g_refs, (loss_ref, gcw_ref) = take(ns), take(2)
        w_refs, m_refs, v_refs = take(ns), take(ns), take(ns)
        wcw_ref, mcw_ref, vcw_ref = take(3)
        cins = take(len(carry.in_arrays)) if carry else []
        outs = [take(4) for _ in range(ns)]
        cw_outs, (loss_out,) = take(4), take(1)
        couts = take(len(carry.out_shape)) if carry else []
        send, slots, cslots, send_sems, recv_sems, csend_sems, crecv_sems = take(7)
        cscr = list(it)
        x, y, c = _place()
        me = 4 * x + 2 * y + c
        send[...] = jnp.zeros_like(send)
        for k in range(ns):
            send[k:k + 1, 0:widths[k]] = g_refs[k][...]
        send[LOSS_ROW:LOSS_ROW + 1, 0:128] = loss_ref[...]
        slots[me] = send[...]
        cslots[me] = gcw_ref[...]
        cps = []
        for k in range(1, NDEV):
            peer = (x ^ ((k >> 2) & 1), y ^ ((k >> 1) & 1), c ^ (k & 1))
            cps.append(pltpu.make_async_remote_copy(
                src_ref=send, dst_ref=slots.at[me], send_sem=send_sems.at[k - 1], recv_sem=recv_sems.at[k - 1],
                device_id=peer, device_id_type=MESH))
            cps.append(pltpu.make_async_remote_copy(
                src_ref=gcw_ref, dst_ref=cslots.at[me], send_sem=csend_sems.at[k - 1],
                recv_sem=crecv_sems.at[k - 1], device_id=peer, device_id_type=MESH))
        for cp in cps:
            cp.start()
        if carry:
            carry.start(cins, couts, cscr)
        for cp in cps:
            cp.wait()
        if carry:
            carry.mid(cins, couts, cscr)
        tot = slots[0]
        ctot = cslots[0, me]
        for j in range(1, NDEV):
            tot = tot + slots[j]
            ctot = ctot + cslots[j, me]

        def step(g, w_ref, m_ref, v_ref, o):
            d, mn, vn = _adamw_math(w_ref[...], g, m_ref[...], v_ref[...])
            o[0][...], o[1][...], o[2][...], o[3][...] = g, d, mn, vn

        for k in range(ns):
            step(tot[k:k + 1, 0:widths[k]], w_refs[k], m_refs[k], v_refs[k], outs[k])
        step(ctot, wcw_ref, mcw_ref, vcw_ref, cw_outs)
        loss_out[...] = tot[LOSS_ROW:LOSS_ROW + 1, 0:128]
        if carry:
            carry.finish(cins, couts, cscr)

    args = [*gs, loss_row, gcw, *ws, *ms, *vs, wcw, mcw, vcw]
    out_shape = ([jax.ShapeDtypeStruct((1, n), F32) for n in widths for _ in range(4)]
                 + [jax.ShapeDtypeStruct((CWF, D), F32)] * 4 + [jax.ShapeDtypeStruct((1, 128), F32)])
    n_own = len(out_shape)
    res = pl.pallas_call(
        body, in_specs=[VMEM] * len(args) + (carry.in_specs if carry else []),
        out_specs=[VMEM] * n_own + (carry.out_specs if carry else []),
        out_shape=out_shape + (carry.out_shape if carry else []),
        scratch_shapes=[pltpu.VMEM((SROWS, D), F32), pltpu.VMEM((NDEV, SROWS, D), F32),
                        pltpu.VMEM((NDEV, NDEV, CWF, D), F32)]
                       + [pltpu.SemaphoreType.DMA((NDEV - 1,))] * 4 + (carry.scratch if carry else []),
        name="small_step")(*args, *(carry.in_arrays if carry else []))
    per = [res[4 * k:4 * k + 4] for k in range(ns)]
    return per, res[4 * ns:4 * ns + 4], res[n_own - 1], res[n_own:]


def _pack_cw(a):
    flat = a.reshape(a.shape[:-2] + (CK * HD,))
    pad = [(0, 0)] * (flat.ndim - 1) + [(0, CWF * D - CK * HD)]
    return jnp.pad(flat, pad).reshape(a.shape[:-2] + (CWF, D))


def _unpack_cw(v):
    return v.reshape(-1)[:CK * HD].reshape(1, CK, HD)


def kernel(x, ffn1_norm, ffn1_w_gate, ffn1_w_up, ffn1_w_down, mix_norm, w_in, q_norm, k_norm, conv_w, conv_b, conv_ln_g, conv_ln_b, w_out, ffn2_norm, ffn2_w_gate, ffn2_w_up, ffn2_w_down, loss_target, m_ffn1_norm, m_ffn1_w_gate, m_ffn1_w_up, m_ffn1_w_down, m_mix_norm, m_w_in, m_q_norm, m_k_norm, m_conv_w, m_conv_b, m_conv_ln_g, m_conv_ln_b, m_w_out, m_ffn2_norm, m_ffn2_w_gate, m_ffn2_w_up, m_ffn2_w_down, v_ffn1_norm, v_ffn1_w_gate, v_ffn1_w_up, v_ffn1_w_down, v_mix_norm, v_w_in, v_q_norm, v_k_norm, v_conv_w, v_conv_b, v_conv_ln_g, v_conv_ln_b, v_w_out, v_ffn2_norm, v_ffn2_w_gate, v_ffn2_w_up, v_ffn2_w_down):
    P = dict(ffn1_norm=ffn1_norm, ffn1_w_gate=ffn1_w_gate, ffn1_w_up=ffn1_w_up, ffn1_w_down=ffn1_w_down,
             mix_norm=mix_norm, w_in=w_in, q_norm=q_norm, k_norm=k_norm, conv_w=conv_w, conv_b=conv_b,
             conv_ln_g=conv_ln_g, conv_ln_b=conv_ln_b, w_out=w_out, ffn2_norm=ffn2_norm,
             ffn2_w_gate=ffn2_w_gate, ffn2_w_up=ffn2_w_up, ffn2_w_down=ffn2_w_down)
    M = dict(ffn1_norm=m_ffn1_norm, ffn1_w_gate=m_ffn1_w_gate, ffn1_w_up=m_ffn1_w_up, ffn1_w_down=m_ffn1_w_down,
             mix_norm=m_mix_norm, w_in=m_w_in, q_norm=m_q_norm, k_norm=m_k_norm, conv_w=m_conv_w, conv_b=m_conv_b,
             conv_ln_g=m_conv_ln_g, conv_ln_b=m_conv_ln_b, w_out=m_w_out, ffn2_norm=m_ffn2_norm,
             ffn2_w_gate=m_ffn2_w_gate, ffn2_w_up=m_ffn2_w_up, ffn2_w_down=m_ffn2_w_down)
    V = dict(ffn1_norm=v_ffn1_norm, ffn1_w_gate=v_ffn1_w_gate, ffn1_w_up=v_ffn1_w_up, ffn1_w_down=v_ffn1_w_down,
             mix_norm=v_mix_norm, w_in=v_w_in, q_norm=v_q_norm, k_norm=v_k_norm, conv_w=v_conv_w, conv_b=v_conv_b,
             conv_ln_g=v_conv_ln_g, conv_ln_b=v_conv_ln_b, w_out=v_w_out, ffn2_norm=v_ffn2_norm,
             ffn2_w_gate=v_ffn2_w_gate, ffn2_w_up=v_ffn2_w_up, ffn2_w_down=v_ffn2_w_down)
    order = ["ffn1_norm", "ffn1_w_gate", "ffn1_w_up", "ffn1_w_down", "mix_norm", "w_in", "q_norm", "k_norm",
             "conv_w", "conv_b", "conv_ln_g", "conv_ln_b", "w_out", "ffn2_norm", "ffn2_w_gate", "ffn2_w_up",
             "ffn2_w_down"]
    B, S, _ = x.shape
    T = B * S

    bigs = [("wg1", "ffn1_w_gate", True), ("wu1", "ffn1_w_up", True), ("wd1", "ffn1_w_down", False),
            ("win", "w_in", True), ("wout", "w_out", False),
            ("wg2", "ffn2_w_gate", True), ("wu2", "ffn2_w_up", True), ("wd2", "ffn2_w_down", False)]
    hm = lambda a, tr: jnp.transpose(a[0]) if tr else a[0]
    cw_pad = jnp.zeros((32, 128), F32).at[0:CK, 0:HD].set(conv_w[0])
    shard = {ln: (hm(P[pn], tr), BF16) for ln, pn, tr in bigs}
    gathered = _run_carry(_GatherCarry([shard["wg1"], shard["wu1"], (cw_pad, F32)]), "gather_first")
    W = {"wg1": gathered[0], "wu1": gathered[1]}
    cwg = gathered[2].reshape(NDEV, 32, 128)[:, 0:CK, 0:HD]
    W["conv_w"] = jnp.transpose(cwg, (1, 0, 2)).reshape(CK, DC)
    norms = {n: P[n] for n in SMALL_NAMES}
    comm = _Comm({"down_in": ({n: shard[n] for n in ("wd1", "win")}, 0.5),
                  "ffn2": ({n: shard[n] for n in ("wg2", "wu2", "wd2", "wout")}, 0.5)})

    loss_part, gx, _, small = _local_step(x.reshape(T, D), loss_target.reshape(T, D), norms, W, B, S, comm)

    G, Dl, Mn, Vn = {}, {}, {}, {}
    dcw = small["conv_w"].reshape(CK, NDEV, HD).transpose(1, 0, 2)
    loss_row = jnp.zeros((1, 128), F32).at[0, 0].set(loss_part)
    per, cw_outs, loss_out, got = _small_step(
        [small[n] for n in SMALL_NAMES], loss_row, _pack_cw(dcw),
        [P[n] for n in SMALL_NAMES], [M[n] for n in SMALL_NAMES], [V[n] for n in SMALL_NAMES],
        _pack_cw(P["conv_w"][0]), _pack_cw(M["conv_w"][0]), _pack_cw(V["conv_w"][0]), carry=comm.last)
    comm.reduce_done(comm.last, got)
    loss = loss_out[0, 0]
    for n, outs in zip(SMALL_NAMES, per):
        G[n], Dl[n], Mn[n], Vn[n] = outs
    G["conv_w"], Dl["conv_w"], Mn["conv_w"], Vn["conv_w"] = [_unpack_cw(o) for o in cw_outs]

    for ln, pn, tr in bigs:
        outs = _adamw_big(comm.reduced[ln], hm(P[pn], tr), hm(M[pn], tr), hm(V[pn], tr), "adamw_" + ln)
        G[pn], Dl[pn], Mn[pn], Vn[pn] = [(jnp.transpose(o) if tr else o)[None] for o in outs]

    return (loss, gx.reshape(B, S, D), *[G[n] for n in order], *[Dl[n] for n in order],
            *[Mn[n] for n in order], *[Vn[n] for n in order])
```

```python
import functools

import jax
import jax.numpy as jnp
from jax import lax
from jax.experimental import pallas as pl
from jax.experimental.pallas import tpu as pltpu

F32 = jnp.float32
BF16 = jnp.bfloat16

D = 1024
FF = 2816
HD = 64
DA = 512
DC = 512
DIN = 2560
CK = 31
BLK = 128
DILS = (1, 4, 16)
EPS = 1e-6
NDEV = 8
MESH = pl.DeviceIdType.MESH

LR, B1, B2, AEPS, WD, STEP = 0.001, 0.9, 0.999, 1e-08, 0.01, 10

NT = (((1,), (1,)), ((), ()))
TN = (((0,), (0,)), ((), ()))

VMEM_LIMIT = 60 * 1024 * 1024


def _cp(sem=None):
    return pltpu.CompilerParams(dimension_semantics=sem, vmem_limit_bytes=VMEM_LIMIT)


def _sigmoid(x):
    return 0.5 * (jnp.tanh(0.5 * x) + 1.0)


def _pallas(body, args, *, grid, in_specs, out_specs, out_shape, scratch_shapes, sem, name, carry=None):
    if carry is None:
        outs = pl.pallas_call(body, grid=grid, in_specs=in_specs, out_specs=out_specs, out_shape=out_shape,
                              scratch_shapes=scratch_shapes, compiler_params=_cp(sem), name=name)(*args)
        return outs, None
    n_in, n_out, n_scr = len(in_specs), len(out_shape), len(scratch_shapes)
    c_in, c_out = len(carry.in_arrays), len(carry.out_shape)

    def wrapped(*refs):
        ins, refs = refs[:n_in], refs[n_in:]
        cins, refs = refs[:c_in], refs[c_in:]
        outs, refs = refs[:n_out], refs[n_out:]
        couts, refs = refs[:c_out], refs[c_out:]
        scr, cscr = refs[:n_scr], refs[n_scr:]
        ids = [pl.program_id(a) for a in range(len(grid))]
        step = ids[0]
        for i, n in zip(ids[1:], grid[1:]):
            step = step * n + i
        steps = functools.reduce(lambda a, b: a * b, grid)

        @pl.when(step == 0)
        def _():
            carry.start(cins, couts, cscr)

        body(*ins, *outs, *scr)

        if hasattr(carry, "mid"):
            @pl.when(step == int(steps * carry.mid_at))
            def _():
                carry.mid(cins, couts, cscr)

        @pl.when(step == steps - 1)
        def _():
            carry.finish(cins, couts, cscr)

    outs = pl.pallas_call(
        wrapped, grid=grid, in_specs=list(in_specs) + carry.in_specs, out_specs=list(out_specs) + carry.out_specs,
        out_shape=list(out_shape) + carry.out_shape, scratch_shapes=list(scratch_shapes) + carry.scratch,
        compiler_params=_cp(("arbitrary",) * len(grid)), name=name)(*args, *carry.in_arrays)
    return outs[:n_out], outs[n_out:]


FC = 256


def _resident(shape):
    return pl.BlockSpec(shape, lambda *_: (0,) * len(shape), pipeline_mode=pl.Buffered(1))


def _mix_out_ffn_loss(h1, attn, conv, wout, gain, wg, wu, wd, target, name):
    T = h1.shape[0]
    tm = 512
    nt = T // tm

    def body(h1_ref, at_ref, cv_ref, wo_ref, gain_ref, wg_ref, wu_ref, wd_ref, t_ref,
             h2_ref, n_ref, g_ref, u_ref, dout_ref, dyb_ref, sq_ref, a_hbm, a_scr, a_sem):
        t = pl.program_id(0)
        a_out = lambda i: pltpu.make_async_copy(a_scr, a_hbm.at[pl.ds(pl.multiple_of(i * tm, tm), tm), :], a_sem)

        @pl.when(t > 0)
        def _():
            a_out(t - 1).wait()

        xv = (h1_ref[...]
              + jnp.dot(at_ref[...], wo_ref[0:DA, :], preferred_element_type=F32)
              + jnp.dot(cv_ref[...], wo_ref[DA:D, :], preferred_element_type=F32))
        h2_ref[...] = xv
        r = lax.rsqrt(jnp.mean(xv * xv, axis=-1, keepdims=True) + EPS)
        n_ref[...] = (xv * r * gain_ref[...]).astype(BF16)
        for c in range(FF // FC):
            cols = slice(c * FC, (c + 1) * FC)
            nb = n_ref[...]
            g = lax.dot_general(nb, wg_ref[cols, :], NT, preferred_element_type=F32)
            u = lax.dot_general(nb, wu_ref[cols, :], NT, preferred_element_type=F32)
            g_ref[:, cols] = g.astype(BF16)
            u_ref[:, cols] = u.astype(BF16)
            a_scr[:, cols] = (g * _sigmoid(g) * u).astype(BF16)
        a_out(t).start()
        e = h2_ref[...] + 0.5 * jnp.dot(a_scr[...], wd_ref[...], preferred_element_type=F32) - t_ref[...]
        dout = e * (1.0 / D)
        dout_ref[...] = dout
        dyb_ref[...] = (0.5 * dout).astype(BF16)
        sq_ref[...] = jnp.sum(e * e, axis=0, keepdims=True)[None]

        @pl.when(t == nt - 1)
        def _():
            a_out(t).wait()

    row = pl.BlockSpec((tm, D), lambda t: (t, 0))
    half = pl.BlockSpec((tm, DA), lambda t: (t, 0))
    wide = pl.BlockSpec((tm, FF), lambda t: (t, 0))
    outs, _ = _pallas(
        body, (h1, attn, conv, wout, gain, wg, wu, wd, target), grid=(nt,),
        in_specs=[row, half, half, _resident((D, D)), _resident((1, D)), _resident((FF, D)), _resident((FF, D)),
                  _resident((FF, D)), row],
        out_specs=[row, row, wide, wide, row, row, pl.BlockSpec((1, 1, D), lambda t: (t, 0, 0)), HBM],
        out_shape=[jax.ShapeDtypeStruct((T, D), F32), jax.ShapeDtypeStruct((T, D), BF16)]
                  + [jax.ShapeDtypeStruct((T, FF), BF16)] * 2
                  + [jax.ShapeDtypeStruct((T, D), F32), jax.ShapeDtypeStruct((T, D), BF16),
                     jax.ShapeDtypeStruct((nt, 1, D), F32), jax.ShapeDtypeStruct((T, FF), BF16)],
        scratch_shapes=[pltpu.VMEM((tm, FF), BF16), pltpu.SemaphoreType.DMA(())],
        sem=("arbitrary",), name=name)
    return outs


def _ffn_gate_up(x, gain, wg, wu, name, carry=None):
    T = x.shape[0]
    tm = 512

    def body(x_ref, gain_ref, wg_ref, wu_ref, n_ref, g_ref, u_ref, a_ref):
        xv = x_ref[...]
        r = lax.rsqrt(jnp.mean(xv * xv, axis=-1, keepdims=True) + EPS)
        n_ref[...] = (xv * r * gain_ref[...]).astype(BF16)
        for c in range(FF // FC):
            cols = slice(c * FC, (c + 1) * FC)
            nb = n_ref[...]
            g = lax.dot_general(nb, wg_ref[cols, :], NT, preferred_element_type=F32)
            u = lax.dot_general(nb, wu_ref[cols, :], NT, preferred_element_type=F32)
            g_ref[:, cols] = g.astype(BF16)
            u_ref[:, cols] = u.astype(BF16)
            a_ref[:, cols] = (g * _sigmoid(g) * u).astype(BF16)

    row = pl.BlockSpec((tm, D), lambda t: (t, 0))
    wide = pl.BlockSpec((tm, FF), lambda t: (t, 0))
    return _pallas(
        body, (x, gain, wg, wu), grid=(T // tm,),
        in_specs=[row, _resident((1, D)), _resident((FF, D)), _resident((FF, D))],
        out_specs=[row, wide, wide, wide],
        out_shape=[jax.ShapeDtypeStruct((T, D), BF16)] + [jax.ShapeDtypeStruct((T, FF), BF16)] * 3,
        scratch_shapes=[], sem=("parallel",), name=name, carry=carry)


def _ffn_down_mix_in(x, a, wd, gain, win, name):
    T = x.shape[0]
    tm = 512

    def body(x_ref, a_ref, wd_ref, gain_ref, win_ref, h_ref, u_ref, n_ref):
        hv = x_ref[...] + 0.5 * jnp.dot(a_ref[...], wd_ref[...], preferred_element_type=F32)
        h_ref[...] = hv
        r = lax.rsqrt(jnp.mean(hv * hv, axis=-1, keepdims=True) + EPS)
        n_ref[...] = (hv * r * gain_ref[...]).astype(BF16)
        u_ref[...] = lax.dot_general(n_ref[...], win_ref[...], NT, preferred_element_type=F32).astype(BF16)

    row = pl.BlockSpec((tm, D), lambda t: (t, 0))
    wide = pl.BlockSpec((tm, FF), lambda t: (t, 0))
    outs, _ = _pallas(
        body, (x, a, wd, gain, win), grid=(T // tm,),
        in_specs=[row, wide, _resident((FF, D)), _resident((1, D)), _resident((DIN, D))],
        out_specs=[row, pl.BlockSpec((tm, DIN), lambda t: (t, 0)), row],
        out_shape=[jax.ShapeDtypeStruct((T, D), F32), jax.ShapeDtypeStruct((T, DIN), BF16),
                   jax.ShapeDtypeStruct((T, D), BF16)],
        scratch_shapes=[], sem=("parallel",), name=name)
    return outs


def _ffn_bwd_act(dyb, g, u, x, dout, gain, wg, wu, wd, name, carry=None):
    T = x.shape[0]
    tm = 256
    nt = T // tm

    def body(dy_ref, g_ref, u_ref, x_ref, dout_ref, gain_ref, wg_ref, wu_ref, wd_ref,
             dg_ref, du_ref, dx_ref, dgn_ref):
        for c in range(FF // FC):
            cols = slice(c * FC, (c + 1) * FC)
            da = lax.dot_general(dy_ref[...], wd_ref[cols, :], NT, preferred_element_type=F32)
            gv = g_ref[:, cols].astype(F32)
            uv = u_ref[:, cols].astype(F32)
            sg = _sigmoid(gv)
            dg_ref[:, cols] = (da * uv * (sg * (1.0 + gv * (1.0 - sg)))).astype(BF16)
            du_ref[:, cols] = (da * (gv * sg)).astype(BF16)
        dn = (jnp.dot(dg_ref[...], wg_ref[...], preferred_element_type=F32)
              + jnp.dot(du_ref[...], wu_ref[...], preferred_element_type=F32))
        dx, dgain = _rms_bwd_rows(dn, x_ref[...], gain_ref[...])
        dx_ref[...] = dout_ref[...] + dx
        dgn_ref[...] = dgain[None]

    row = pl.BlockSpec((tm, D), lambda t: (t, 0))
    wide = pl.BlockSpec((tm, FF), lambda t: (t, 0))
    return _pallas(
        body, (dyb, g, u, x, dout, gain, wg, wu, wd), grid=(nt,),
        in_specs=[row, wide, wide, row, row, _resident((1, D)), _resident((FF, D)), _resident((FF, D)),
                  _resident((FF, D))],
        out_specs=[wide, wide, row, pl.BlockSpec((1, 1, D), lambda t: (t, 0, 0))],
        out_shape=[jax.ShapeDtypeStruct((T, FF), BF16)] * 2
                  + [jax.ShapeDtypeStruct((T, D), F32), jax.ShapeDtypeStruct((nt, 1, D), F32)],
        scratch_shapes=[], sem=("parallel",), name=name, carry=carry)


def _ffn_bwd_w(lhs, rhs, name, carry=None):
    T = rhs.shape[0]
    tf = 256

    def body(l_ref, r_ref, dw_ref):
        dw_ref[...] = lax.dot_general(l_ref[...], r_ref[...], TN, preferred_element_type=F32).astype(BF16)

    (dw,), got = _pallas(
        body, (lhs, rhs), grid=(FF // tf,),
        in_specs=[pl.BlockSpec((T, tf), lambda f: (0, f)), _resident((T, D))],
        out_specs=[pl.BlockSpec((tf, D), lambda f: (f, 0))], out_shape=[jax.ShapeDtypeStruct((FF, D), BF16)],
        scratch_shapes=[], sem=("parallel",), name=name, carry=carry)
    return dw, got


def _rms_bwd_rows(dn, xv, gain):
    r = lax.rsqrt(jnp.mean(xv * xv, axis=-1, keepdims=True) + EPS)
    xhat = xv * r
    dxhat = dn * gain
    dx = r * (dxhat - xhat * jnp.mean(dxhat * xhat, axis=-1, keepdims=True))
    return dx, jnp.sum(dn * xhat, axis=0, keepdims=True)


def _mix_out_bwd(dh, attn, conv, wout):
    T = dh.shape[0]
    tm = 512
    nt = T // tm

    def body(dh_ref, a_ref, c_ref, w_ref, da_ref, dc_ref, dw_ref, acc_scr):
        t = pl.program_id(0)

        @pl.when(t == 0)
        def _():
            acc_scr[...] = jnp.zeros_like(acc_scr)

        dhb = dh_ref[...].astype(BF16)
        dmix = lax.dot_general(dhb, w_ref[...], NT, preferred_element_type=F32)
        da_ref[...] = dmix[:, 0:DA].astype(BF16)
        dc_ref[...] = dmix[:, DA:D].astype(BF16)
        acc_scr[0:DA, :] += lax.dot_general(a_ref[...], dhb, TN, preferred_element_type=F32)
        acc_scr[DA:D, :] += lax.dot_general(c_ref[...], dhb, TN, preferred_element_type=F32)

        @pl.when(t == nt - 1)
        def _():
            dw_ref[...] = acc_scr[...].astype(BF16)

    row = pl.BlockSpec((tm, D), lambda t: (t, 0))
    half = pl.BlockSpec((tm, DA), lambda t: (t, 0))
    full = pl.BlockSpec((D, D), lambda t: (0, 0))
    return pl.pallas_call(
        body, grid=(nt,), in_specs=[row, half, half, full], out_specs=[half, half, full],
        out_shape=[jax.ShapeDtypeStruct((T, DA), BF16)] * 2 + [jax.ShapeDtypeStruct((D, D), BF16)],
        scratch_shapes=[pltpu.VMEM((D, D), F32)],
        compiler_params=_cp(("arbitrary",)), name="mix_out_bwd")(dh, attn, conv, wout)


def _mix_in_bwd(dparts, win, nb, h, dh, gain):
    T = h.shape[0]
    tm = 512
    nt = T // tm

    def body(d0, d1, d2, d3, d4, w_ref, n_ref, h_ref, dh_ref, gain_ref,
             dw_ref, dx_ref, dyb_ref, dg_ref, acc_scr):
        t = pl.program_id(0)

        @pl.when(t == 0)
        def _():
            acc_scr[...] = jnp.zeros_like(acc_scr)

        n = n_ref[...]
        dn = jnp.zeros((tm, D), F32)
        for i, d_ref in enumerate((d0, d1, d2, d3, d4)):
            dv = d_ref[...]
            dn = dn + jnp.dot(dv, w_ref[i * DA:(i + 1) * DA, :], preferred_element_type=F32)
            acc_scr[i * DA:(i + 1) * DA, :] += lax.dot_general(dv, n, TN, preferred_element_type=F32)
        dx, dgain = _rms_bwd_rows(dn, h_ref[...], gain_ref[...])
        tot = dh_ref[...] + dx
        dx_ref[...] = tot
        dyb_ref[...] = (0.5 * tot).astype(BF16)
        dg_ref[...] = dgain[None]

        @pl.when(t == nt - 1)
        def _():
            dw_ref[...] = acc_scr[...].astype(BF16)

    row = pl.BlockSpec((tm, D), lambda t: (t, 0))
    half = pl.BlockSpec((tm, DA), lambda t: (t, 0))
    full = pl.BlockSpec((DIN, D), lambda t: (0, 0))
    return pl.pallas_call(
        body, grid=(nt,),
        in_specs=[half] * 5 + [full, row, row, row, pl.BlockSpec((1, D), lambda t: (0, 0))],
        out_specs=[full, row, row, pl.BlockSpec((1, 1, D), lambda t: (t, 0, 0))],
        out_shape=[jax.ShapeDtypeStruct((DIN, D), BF16), jax.ShapeDtypeStruct((T, D), F32),
                   jax.ShapeDtypeStruct((T, D), BF16), jax.ShapeDtypeStruct((nt, 1, D), F32)],
        scratch_shapes=[pltpu.VMEM((DIN, D), F32)],
        compiler_params=_cp(("arbitrary",)), name="mix_in_bwd")(*dparts, win, nb, h, dh, gain)


def _head_masks():
    lane = lax.broadcasted_iota(jnp.int32, (1, 2 * HD), 1)
    m0 = lane < HD
    return m0, jnp.logical_not(m0)


def _stack_heads(v, m0, m1):
    z = jnp.zeros_like(v)
    return jnp.concatenate([jnp.where(m0, v, z), jnp.where(m1, v, z)], axis=0)


def _unstack_heads(v2, m0):
    return jnp.where(m0, v2[0:BLK], v2[BLK:2 * BLK])


def _head_sums(xv):
    ri = lax.broadcasted_iota(jnp.int32, (2 * HD, 2 * HD), 0)
    ci = lax.broadcasted_iota(jnp.int32, (2 * HD, 2 * HD), 1)
    ones = jnp.where((ri < HD) == (ci < HD), 1.0, 0.0).astype(BF16)
    hi = xv.astype(BF16)
    lo = (xv - hi.astype(F32)).astype(BF16)
    return (jnp.dot(hi, ones, preferred_element_type=F32) + jnp.dot(lo, ones, preferred_element_type=F32))


def _head_rms(xv):
    return lax.rsqrt(_head_sums(xv * xv) * (1.0 / HD) + EPS)


def _band_mask(first):
    qi = lax.broadcasted_iota(jnp.int32, (BLK, 2 * BLK), 0)
    ci = lax.broadcasted_iota(jnp.int32, (BLK, 2 * BLK), 1)
    band = (ci >= qi) & (ci <= qi + BLK)
    return band & ((ci >= BLK) | jnp.logical_not(first))


def _block_rows(j, d, seg):
    r, n = j // seg, j % seg
    start = r + (d * BLK) * n
    first = n == 0
    prev = jnp.where(first, start, start - d * BLK)
    return pl.ds(start, BLK, stride=d), pl.ds(prev, BLK, stride=d), first


def _block_keys(refs, cur, prev, first, single):
    if single:
        qi = lax.broadcasted_iota(jnp.int32, (BLK, BLK), 0)
        ci = lax.broadcasted_iota(jnp.int32, (BLK, BLK), 1)
        return [r[cur, :].astype(BF16) for r in refs], ci <= qi
    return ([jnp.concatenate([r[prev, :], r[cur, :]], axis=0).astype(BF16) for r in refs], _band_mask(first))


def _attn_fwd(u, qg2, kg2, B, S, carry=None):
    T = B * S
    NB = S // BLK
    scale = HD ** -0.5

    def body(q_ref, k_ref, v_ref, qg_ref, kg_ref, o_ref, lse_ref, qn, kn, vn, os_, ls_):
        m0, m1 = _head_masks()
        qv = q_ref[...].astype(F32)
        qn[...] = qv * _head_rms(qv) * (qg_ref[...] * scale)
        kv = k_ref[...].astype(F32)
        kn[...] = kv * _head_rms(kv) * kg_ref[...]
        vn[...] = v_ref[...].astype(F32)

        for i, d in enumerate(DILS):
            seg = NB // d

            def blk(j, c, i=i, d=d, seg=seg):
                cur, prev, first = _block_rows(j, d, seg)
                q2 = _stack_heads(qn[cur, :].astype(BF16), m0, m1)
                (kk, vv), mask = _block_keys((kn, vn), cur, prev, first, False)
                s = lax.dot_general(q2, kk, NT, preferred_element_type=F32)
                s = jnp.where(jnp.concatenate([mask, mask], axis=0), s, -1e30)
                mx = jnp.max(s, axis=-1, keepdims=True)
                p = jnp.exp(s - mx)
                l = jnp.sum(p, axis=-1, keepdims=True)
                o2 = jnp.dot((p * (1.0 / l)).astype(BF16), vv, preferred_element_type=F32)
                os_[i, cur, :] = _unstack_heads(o2, m0)
                ls_[i, cur, :] = _unstack_heads(mx + jnp.log(l), m0)
                return c

            lax.fori_loop(0, NB, blk, 0, unroll=8)

        def comb(c, carry):
            rows = pl.ds(pl.multiple_of(c * 256, 256), 256)
            l0, l1, l2 = ls_[0, rows, :], ls_[1, rows, :], ls_[2, rows, :]
            mx = jnp.maximum(jnp.maximum(l0, l1), l2)
            e0, e1, e2 = jnp.exp(l0 - mx), jnp.exp(l1 - mx), jnp.exp(l2 - mx)
            tot = e0 + e1 + e2
            inv = 1.0 / tot
            o = (e0 * os_[0, rows, :] + e1 * os_[1, rows, :] + e2 * os_[2, rows, :]) * inv
            o_ref[rows, :] = o.astype(BF16)
            lse_ref[rows, :] = mx + jnp.log(tot)
            return carry

        lax.fori_loop(0, S // 256, comb, 0)

    pair = 2 * HD
    blk_spec = lambda off: pl.BlockSpec((S, pair), lambda b, p, off=off: (b, off + p))
    gspec = pl.BlockSpec((1, pair), lambda b, p: (0, 0))
    return _pallas(
        body, (u, u, u, qg2, kg2), grid=(B, DA // pair),
        in_specs=[blk_spec(0), blk_spec(DA // pair), blk_spec(2 * DA // pair), gspec, gspec],
        out_specs=[blk_spec(0), blk_spec(0)],
        out_shape=[jax.ShapeDtypeStruct((T, DA), BF16), jax.ShapeDtypeStruct((T, DA), F32)],
        scratch_shapes=[pltpu.VMEM((S, pair), F32)] * 3 + [pltpu.VMEM((3, S, pair), F32)] * 2,
        sem=("parallel", "parallel"), name="attn_fwd", carry=carry)


def _attn_bwd(u, attn, dattn, lse, qg2, kg2, B, S, carry=None):
    T = B * S
    NB = S // BLK
    scale = HD ** -0.5
    pair = 2 * HD

    def body(q_ref, k_ref, v_ref, o_ref, do_ref, lse_ref, qg_ref, kg_ref,
             dq_ref, dk_ref, dv_ref, dgn_ref,
             qn, kn, vn, don, ldl, accq, acck, accv, rq, rk):
        m0, m1 = _head_masks()
        lane = lax.broadcasted_iota(jnp.int32, (1, pair), 1)
        qv = q_ref[...].astype(F32)
        rq[...] = _head_rms(qv)
        qn[...] = qv * rq[...] * (qg_ref[...] * scale)
        kv = k_ref[...].astype(F32)
        rk[...] = _head_rms(kv)
        kn[...] = kv * rk[...] * kg_ref[...]
        vn[...] = v_ref[...].astype(F32)
        dov = do_ref[...].astype(F32)
        don[...] = dov
        ldl[...] = jnp.where((lane % HD) < HD // 2, lse_ref[...], _head_sums(dov * o_ref[...].astype(F32)))

        for i, d in enumerate(DILS):
            seg = NB // d

            def blk(j, c, i=i, d=d, seg=seg):
                cur, prev, first = _block_rows(j, d, seg)
                q2 = _stack_heads(qn[cur, :].astype(BF16), m0, m1)
                do2 = _stack_heads(don[cur, :].astype(BF16), m0, m1)
                (kk, vv), mask = _block_keys((kn, vn), cur, prev, first, seg == 1)
                ldv = ldl[cur, :]
                lse2 = jnp.concatenate([ldv[:, 0:1], ldv[:, HD:HD + 1]], axis=0)
                dl2 = jnp.concatenate([ldv[:, HD // 2:HD // 2 + 1], ldv[:, HD + HD // 2:HD + HD // 2 + 1]], axis=0)
                s = lax.dot_general(q2, kk, NT, preferred_element_type=F32)
                p = jnp.where(jnp.concatenate([mask, mask], axis=0), jnp.exp(s - lse2), 0.0)
                dp = lax.dot_general(do2, vv, NT, preferred_element_type=F32)
                ds = (p * (dp - dl2)).astype(BF16)
                dq_acc = _unstack_heads(jnp.dot(ds, kk, preferred_element_type=F32), m0)
                dk_acc = lax.dot_general(ds, q2, TN, preferred_element_type=F32)
                dv_acc = lax.dot_general(p.astype(BF16), do2, TN, preferred_element_type=F32)
                if i == 0:
                    accq[cur, :] = dq_acc
                    acck[cur, :] = dk_acc[BLK:2 * BLK]
                    accv[cur, :] = dv_acc[BLK:2 * BLK]
                    acck[prev, :] += dk_acc[0:BLK]
                    accv[prev, :] += dv_acc[0:BLK]
                elif seg == 1:
                    accq[cur, :] += dq_acc
                    acck[cur, :] += dk_acc
                    accv[cur, :] += dv_acc
                else:
                    accq[cur, :] += dq_acc
                    acck[prev, :] += dk_acc[0:BLK]
                    acck[cur, :] += dk_acc[BLK:2 * BLK]
                    accv[prev, :] += dv_acc[0:BLK]
                    accv[cur, :] += dv_acc[BLK:2 * BLK]
                return c

            lax.fori_loop(0, NB, blk, 0, unroll=8)

        def norm_bwd(x_ref, r_ref, dn, gain):
            r = r_ref[...]
            xhat = x_ref[...].astype(F32) * r
            dxhat = dn * gain
            dx = r * (dxhat - xhat * (_head_sums(dxhat * xhat) * (1.0 / HD)))
            return dx, jnp.sum(dn * xhat, axis=0, keepdims=True)

        dq, dgq = norm_bwd(q_ref, rq, accq[...], qg_ref[...] * scale)
        dk, dgk = norm_bwd(k_ref, rk, acck[...], kg_ref[...])
        dq_ref[...] = dq.astype(BF16)
        dk_ref[...] = dk.astype(BF16)
        dv_ref[...] = accv[...].astype(BF16)
        dgn_ref[...] = jnp.concatenate([dgq * scale, dgk, jnp.zeros((6, pair), F32)], axis=0)[None]

    blk_spec = lambda off: pl.BlockSpec((S, pair), lambda b, p, off=off: (b, off + p))
    gspec = pl.BlockSpec((1, pair), lambda b, p: (0, 0))
    np_ = DA // pair
    return _pallas(
        body, (u, u, u, attn, dattn, lse, qg2, kg2), grid=(B, np_),
        in_specs=[blk_spec(0), blk_spec(np_), blk_spec(2 * np_), blk_spec(0), blk_spec(0), blk_spec(0),
                  gspec, gspec],
        out_specs=[blk_spec(0), blk_spec(0), blk_spec(0),
                   pl.BlockSpec((1, 8, pair), lambda b, p: (b * np_ + p, 0, 0))],
        out_shape=[jax.ShapeDtypeStruct((T, DA), BF16)] * 3 + [jax.ShapeDtypeStruct((B * np_, 8, pair), F32)],
        scratch_shapes=[pltpu.VMEM((S, pair), F32)] * 10,
        sem=("parallel", "parallel"), name="attn_bwd", carry=carry)


CT = 32
CPAD = 32


def _shifted(win, offsets):
    rolled, out = {}, {}
    n = win.shape[0]
    for o in offsets:
        sub = o % 8
        if sub not in rolled:
            rolled[sub] = win if sub == 0 else pltpu.roll(win, n - sub, 0)
        out[o] = rolled[sub][o - sub:o - sub + CT, :]
    return out


def _ln_fwd(y, g, b):
    mu = jnp.mean(y, axis=-1, keepdims=True)
    yc = y - mu
    rstd = lax.rsqrt(jnp.mean(yc * yc, axis=-1, keepdims=True) + EPS)
    xhat = yc * rstd
    return xhat, rstd, xhat * g + b


def _fill_glu(ca_ref, cg_ref, glu, S):
    glu[pl.ds(0, CPAD), :] = jnp.zeros((CPAD, DC), F32)

    def fill(i, c):
        rows = pl.ds(pl.multiple_of(i * 256, 256), 256)
        a = ca_ref[rows, :].astype(F32)
        gt = cg_ref[rows, :].astype(F32)
        glu[pl.ds(pl.multiple_of(CPAD + i * 256, CT), 256), :] = a * _sigmoid(gt)
        return c

    lax.fori_loop(0, S // 256, fill, 0)


def _conv_fwd(u, cw, cb, lg, lb, B, S):
    T = B * S

    def body(ca_ref, cg_ref, w_ref, b_ref, lg_ref, lb_ref, o_ref, y_ref, glu):
        _fill_glu(ca_ref, cg_ref, glu, S)

        def step(i, c):
            t0 = pl.multiple_of(i * CT, CT)
            win = glu[pl.ds(t0, 2 * CT), :]
            acc = jnp.zeros((CT, DC), F32) + b_ref[...]
            taps = _shifted(win, [k + 2 for k in range(CK)])
            for k in range(CK):
                acc = acc + taps[k + 2] * w_ref[k:k + 1, :]
            y_ref[pl.ds(t0, CT), :] = acc
            _, _, z = _ln_fwd(acc, lg_ref[...], lb_ref[...])
            o_ref[pl.ds(t0, CT), :] = (z * _sigmoid(z)).astype(BF16)
            return c

        lax.fori_loop(0, S // CT, step, 0, unroll=4)

    vec = pl.BlockSpec((1, DC), lambda b: (0, 0))
    return pl.pallas_call(
        body, grid=(B,),
        in_specs=[pl.BlockSpec((S, DC), lambda b: (b, 3)), pl.BlockSpec((S, DC), lambda b: (b, 4)),
                  pl.BlockSpec((CT, DC), lambda b: (0, 0)), vec, vec, vec],
        out_specs=[pl.BlockSpec((S, DC), lambda b: (b, 0))] * 2,
        out_shape=[jax.ShapeDtypeStruct((T, DC), BF16), jax.ShapeDtypeStruct((T, DC), F32)],
        scratch_shapes=[pltpu.VMEM((CPAD + S, DC), F32)],
        compiler_params=_cp(("parallel",)), name="conv_fwd")(u, u, cw, cb, lg, lb)


def _conv_bwd(u, y, dconv, cw, lg, lb, B, S):
    T = B * S

    def body(ca_ref, cg_ref, y_ref, dc_ref, w_ref, lg_ref, lb_ref,
             dca_ref, dcg_ref, dw_ref, ds_ref, glu, dyp, dwacc):
        _fill_glu(ca_ref, cg_ref, glu, S)
        dyp[pl.ds(S, CPAD), :] = jnp.zeros((CPAD, DC), F32)
        lgv, lbv = lg_ref[...], lb_ref[...]

        def sum8(v):
            return functools.reduce(jnp.add, [v[r:r + 8] for r in range(0, v.shape[0], 8)])

        P1 = 4 * CT

        def p1(i, carry):
            sb, sg, sl = carry
            t0 = pl.multiple_of(i * P1, P1)
            xhat, rstd, z = _ln_fwd(y_ref[pl.ds(t0, P1), :], lgv, lbv)
            sz = _sigmoid(z)
            dz = dc_ref[pl.ds(t0, P1), :].astype(F32) * (sz * (1.0 + z * (1.0 - sz)))
            dxhat = dz * lgv
            dy = rstd * (dxhat - jnp.mean(dxhat, axis=-1, keepdims=True)
                         - xhat * jnp.mean(dxhat * xhat, axis=-1, keepdims=True))
            dyp[pl.ds(t0, P1), :] = dy
            return sb + sum8(dy), sg + sum8(dz * xhat), sl + sum8(dz)

        z8 = jnp.zeros((8, DC), F32)
        sb, sg, sl = lax.fori_loop(0, S // P1, p1, (z8, z8, z8))
        rs = lambda v: jnp.sum(v, axis=0, keepdims=True)
        ds_ref[...] = jnp.concatenate([rs(sb), rs(sg), rs(sl), jnp.zeros((5, DC), F32)], axis=0)[None]

        def p2(i, c):
            t0 = pl.multiple_of(i * CT, CT)
            win = dyp[pl.ds(t0, 2 * CT), :]
            acc = jnp.zeros((CT, DC), F32)
            taps = _shifted(win, [30 - k for k in range(CK)])
            for k in range(CK):
                acc = acc + taps[30 - k] * w_ref[k:k + 1, :]
            a = ca_ref[pl.ds(t0, CT), :].astype(F32)
            sgt = _sigmoid(cg_ref[pl.ds(t0, CT), :].astype(F32))
            dca_ref[pl.ds(t0, CT), :] = (acc * sgt).astype(BF16)
            dcg_ref[pl.ds(t0, CT), :] = (acc * a * sgt * (1.0 - sgt)).astype(BF16)
            return c

        lax.fori_loop(0, S // CT, p2, 0)

        dwacc[...] = jnp.zeros_like(dwacc)

        def p3(i, c):
            t0 = pl.multiple_of(i * CT, CT)
            win = glu[pl.ds(t0, 2 * CT), :]
            dy = dyp[pl.ds(t0, CT), :]
            for k in range(CK):
                dwacc[k] += sum8(dy * win[k + 2:k + 2 + CT, :])
            return c

        lax.fori_loop(0, S // CT, p3, 0)
        dw_ref[...] = jnp.sum(dwacc[...], axis=1)[None]

    vec = pl.BlockSpec((1, DC), lambda b: (0, 0))
    seq = pl.BlockSpec((S, DC), lambda b: (b, 0))
    return pl.pallas_call(
        body, grid=(B,),
        in_specs=[pl.BlockSpec((S, DC), lambda b: (b, 3)), pl.BlockSpec((S, DC), lambda b: (b, 4)),
                  seq, seq, pl.BlockSpec((CT, DC), lambda b: (0, 0)), vec, vec],
        out_specs=[seq, seq, pl.BlockSpec((1, CT, DC), lambda b: (b, 0, 0)),
                   pl.BlockSpec((1, 8, DC), lambda b: (b, 0, 0))],
        out_shape=[jax.ShapeDtypeStruct((T, DC), BF16)] * 2
                  + [jax.ShapeDtypeStruct((B, CT, DC), F32), jax.ShapeDtypeStruct((B, 8, DC), F32)],
        scratch_shapes=[pltpu.VMEM((CPAD + S, DC), F32), pltpu.VMEM((S + CPAD, DC), F32),
                        pltpu.VMEM((CT, 8, DC), F32)],
        compiler_params=_cp(("parallel",)), name="conv_bwd")(u, u, y, dconv, cw, lg, lb)


def _local_step(x, target, norms, W, B, S, comm=None):
    qg2 = jnp.concatenate([norms["q_norm"], norms["q_norm"]], axis=1)
    kg2 = jnp.concatenate([norms["k_norm"], norms["k_norm"]], axis=1)
    cw = jnp.concatenate([W["conv_w"], jnp.zeros((1, DC), F32)], axis=0)

    W = dict(W)
    (n1, g1, u1, act1), got = _ffn_gate_up(x, norms["ffn1_norm"], W["wg1"], W["wu1"], "ffn1_gate_up",
                                           carry=comm.gathers["down_in"] if comm else None)
    if comm:
        W.update(comm.gathered("down_in", got))
    h1, u, n2 = _ffn_down_mix_in(x, act1, W["wd1"], norms["mix_norm"], W["win"], "ffn1_down_mix_in")
    (attn, lse), got = _attn_fwd(u, qg2, kg2, B, S, carry=comm.gathers["ffn2"] if comm else None)
    if comm:
        W = dict(W, **comm.gathered("ffn2", got))
    conv, y = _conv_fwd(u, cw, norms["conv_b"], norms["conv_ln_g"], norms["conv_ln_b"], B, S)
    h2, n3, g2, u2, dout, dyb, sq, act2 = _mix_out_ffn_loss(h1, attn, conv, W["wout"], norms["ffn2_norm"],
                                                            W["wg2"], W["wu2"], W["wd2"], target, "ffn2_fwd")
    loss = (0.5 / D) * jnp.sum(sq)

    (dg2, du2, dh2, dgn_ffn2), _ = _ffn_bwd_act(dyb, g2, u2, h2, dout, norms["ffn2_norm"],
                                               W["wg2"], W["wu2"], W["wd2"], "ffn2_bwd_act")
    dwd2, _ = _ffn_bwd_w(act2, dyb, "ffn2_bwd_wd")
    dwg2, _ = _ffn_bwd_w(dg2, n3, "ffn2_bwd_wg")
    dwu2, _ = _ffn_bwd_w(du2, n3, "ffn2_bwd_wu")
    dattn, dconv, dwout = _mix_out_bwd(dh2, attn, conv, W["wout"])
    carry = comm.reduce_start("ffn2", {"wg2": dwg2, "wu2": dwu2, "wd2": dwd2, "wout": dwout}) if comm else None
    (dq, dk, dv, dgn_qk), got = _attn_bwd(u, attn, dattn, lse, qg2, kg2, B, S, carry=carry)
    if comm:
        comm.reduce_done(carry, got)
    dca, dcg, dcw, dcs = _conv_bwd(u, y, dconv, cw, norms["conv_ln_g"], norms["conv_ln_b"], B, S)
    dwin, dh1, dyb1, dgn_mix = _mix_in_bwd((dq, dk, dv, dca, dcg), W["win"], n2, h1, dh2, norms["mix_norm"])
    (dg1, du1, gx, dgn_ffn1), _ = _ffn_bwd_act(dyb1, g1, u1, x, dh1, norms["ffn1_norm"],
                                              W["wg1"], W["wu1"], W["wd1"], "ffn1_bwd_act")
    carry = comm.reduce_start("win", {"win": dwin}) if comm else None
    dwd1, got = _ffn_bwd_w(act1, dyb1, "ffn1_bwd_wd", carry=carry)
    if comm:
        comm.reduce_done(carry, got)
        carry = comm.reduce_start("wd1", {"wd1": dwd1})
    dwg1, got = _ffn_bwd_w(dg1, n1, "ffn1_bwd_wg", carry=carry)
    if comm:
        comm.reduce_done(carry, got)
        carry = comm.reduce_start("wg1", {"wg1": dwg1})
    dwu1, got = _ffn_bwd_w(du1, n1, "ffn1_bwd_wu", carry=carry)
    if comm:
        comm.reduce_done(carry, got)
        comm.last = comm.reduce_start("wu1", {"wu1": dwu1})

    qk = jnp.sum(dgn_qk, axis=0)
    cs = jnp.sum(dcs, axis=0)
    small = {
        "ffn1_norm": jnp.sum(dgn_ffn1, axis=0),
        "mix_norm": jnp.sum(dgn_mix, axis=0),
        "q_norm": qk[0:1, 0:HD] + qk[0:1, HD:2 * HD],
        "k_norm": qk[1:2, 0:HD] + qk[1:2, HD:2 * HD],
        "conv_w": jnp.sum(dcw, axis=0)[0:CK],
        "conv_b": cs[0:1],
        "conv_ln_g": cs[1:2],
        "conv_ln_b": cs[2:3],
        "ffn2_norm": jnp.sum(dgn_ffn2, axis=0),
    }
    big = {"wg1": dwg1, "wu1": dwu1, "wd1": dwd1, "win": dwin, "wout": dwout,
           "wg2": dwg2, "wu2": dwu2, "wd2": dwd2}
    return loss, gx, big, small


HBM = pl.BlockSpec(memory_space=pltpu.HBM)
VMEM = pl.BlockSpec(memory_space=pltpu.VMEM)


def _place():
    return lax.axis_index("x"), lax.axis_index("y"), lax.axis_index("c")


class _GatherCarry:
    def __init__(self, shards, mid_at=0.5):
        nt = len(shards)
        self.mid_at = mid_at
        self.shards = shards
        self.in_arrays = [s for s, _ in shards]
        self.in_specs = [VMEM] * nt
        self.out_shape = [jax.ShapeDtypeStruct((NDEV * s.shape[0], s.shape[1]), dt) for s, dt in shards]
        self.out_specs = [HBM] * nt
        self.scratch = ([pltpu.VMEM(s.shape, dt) for s, dt in shards]
                        + [pltpu.SemaphoreType.DMA((nt, 7)), pltpu.SemaphoreType.DMA((nt, 7)),
                           pltpu.SemaphoreType.DMA((nt,))])

    def _copies(self, outs, scr):
        nt = len(self.shards)
        stages = scr[:nt]
        send_sems, recv_sems, local_sems = scr[nt:]
        x, y, c = _place()
        me, sibling = (x, y, c), (x, y, 1 - c)
        xn, yn, diag = (1 - x, y, c), (x, 1 - y, c), (1 - x, 1 - y, c)
        via = (x ^ c, y ^ (1 - c), c)
        onto = (x ^ (1 - c), y ^ c, c)

        def rows(t, px, py, pc):
            r = self.shards[t][0].shape[0]
            return outs[t].at[pl.ds((4 * px + 2 * py + pc) * r, r), :]

        def copy(t, k, block, to, src=None):
            return pltpu.make_async_remote_copy(
                src_ref=rows(t, *block) if src is None else src, dst_ref=rows(t, *block),
                send_sem=send_sems.at[t, k], recv_sem=recv_sems.at[t, k],
                device_id=to, device_id_type=MESH)

        sib = lambda b: (b[0], b[1], 1 - c)
        return dict(
            local=[pltpu.make_async_copy(stages[t], rows(t, *me), local_sems.at[t]) for t in range(nt)],
            own=[[copy(t, 0, me, sibling, src=stages[t]), copy(t, 1, me, xn, src=stages[t]),
                  copy(t, 2, me, yn, src=stages[t])] for t in range(nt)],
            relay=[copy(t, 3, via, onto) for t in range(nt)],
            down=[[copy(t, 4, xn, sibling), copy(t, 5, yn, sibling)] for t in range(nt)],
            down_diag=[copy(t, 6, diag, sibling) for t in range(nt)],
            got_xy=[[copy(t, 1, xn, me), copy(t, 2, yn, me)] for t in range(nt)],
            got_diag=[copy(t, 3, diag, me) for t in range(nt)],
            got_sib=[[copy(t, 0, sibling, me), copy(t, 4, sib(xn), me), copy(t, 5, sib(yn), me),
                      copy(t, 6, sib(diag), me)] for t in range(nt)])

    def start(self, ins, outs, scr):
        cps = self._copies(outs, scr)
        for t, (_, dt) in enumerate(self.shards):
            scr[t][...] = ins[t][...].astype(dt)
            for cp in [cps["local"][t]] + cps["own"][t]:
                cp.start()

    def mid(self, ins, outs, scr):
        cps = self._copies(outs, scr)
        for t in range(len(self.shards)):
            for cp in cps["got_xy"][t]:
                cp.wait_recv()
            for cp in [cps["relay"][t]] + cps["down"][t]:
                cp.start()

    def finish(self, ins, outs, scr):
        cps = self._copies(outs, scr)
        for t in range(len(self.shards)):
            cps["got_diag"][t].wait_recv()
            cps["down_diag"][t].start()
        for t in range(len(self.shards)):
            for cp in cps["got_sib"][t]:
                cp.wait_recv()
            for cp in cps["own"][t] + [cps["relay"][t]] + cps["down"][t] + [cps["down_diag"][t]]:
                cp.wait_send()
            cps["local"][t].wait()


def _run_carry(carry, name):
    def body(*refs):
        n_in, n_out = len(carry.in_arrays), len(carry.out_shape)
        ins, outs, scr = refs[:n_in], refs[n_in:n_in + n_out], refs[n_in + n_out:]
        carry.start(ins, outs, scr)
        if hasattr(carry, "mid"):
            carry.mid(ins, outs, scr)
        carry.finish(ins, outs, scr)

    return pl.pallas_call(
        body, in_specs=carry.in_specs, out_specs=carry.out_specs, out_shape=carry.out_shape,
        scratch_shapes=carry.scratch, compiler_params=pltpu.CompilerParams(vmem_limit_bytes=VMEM_LIMIT),
        name=name)(*carry.in_arrays)


class _ExchangeCarry:
    def __init__(self, names, grads, mid_at=0.5):
        nt = len(grads)
        self.mid_at = mid_at
        self.names = names
        self.in_arrays = [g.reshape(4, 2, g.shape[0] // NDEV, g.shape[1]) for g in grads]
        self.in_specs = [HBM] * nt
        blocks = [g.shape[2:] for g in self.in_arrays]
        self.out_shape = [jax.ShapeDtypeStruct((3,) + b, BF16) for b in blocks]
        self.out_specs = [HBM] * nt
        self.scratch = ([pltpu.VMEM((4,) + b, BF16) for b in blocks] * 2 + [pltpu.VMEM(b, BF16) for b in blocks]
                        + [pltpu.SemaphoreType.DMA((nt, 3)), pltpu.SemaphoreType.DMA((nt, 3))]
                        + [pltpu.SemaphoreType.DMA((nt,))] * 4)

    def _copies(self, ins, outs, scr):
        nt = len(ins)
        theirs, own, relayed = scr[:nt], scr[nt:2 * nt], scr[2 * nt:3 * nt]
        send_sems, recv_sems, keep_sems, load_sems, swap_send, swap_recv = scr[3 * nt:]
        x, y, c = _place()
        q = lambda cx, cy: 2 * cx + cy
        near, far = (x ^ c, y ^ (1 - c)), (x ^ (1 - c), y ^ c)

        def remote(t, k, src, dst, chip):
            return pltpu.make_async_remote_copy(
                src_ref=src, dst_ref=dst, send_sem=send_sems.at[t, k], recv_sem=recv_sems.at[t, k],
                device_id=(*chip, c), device_id_type=MESH)

        return dict(
            swap=[pltpu.make_async_remote_copy(
                src_ref=ins[t].at[:, 1 - c], dst_ref=theirs[t], send_sem=swap_send.at[t], recv_sem=swap_recv.at[t],
                device_id=(x, y, 1 - c), device_id_type=MESH) for t in range(nt)],
            load=[pltpu.make_async_copy(ins[t].at[:, c], own[t], load_sems.at[t]) for t in range(nt)],
            keep=[pltpu.make_async_copy(own[t].at[q(x, y)], outs[t].at[0], keep_sems.at[t]) for t in range(nt)],
            direct=[remote(t, 0, own[t].at[q(*near)], outs[t].at[1], near) for t in range(nt)],
            relay=[remote(t, 1, own[t].at[q(1 - x, 1 - y)], relayed[t], near) for t in range(nt)],
            merged=[remote(t, 2, own[t].at[q(*far)], outs[t].at[2], far) for t in range(nt)],
            theirs=theirs, own=own, relayed=relayed, far=q(*far))

    def start(self, ins, outs, scr):
        cps = self._copies(ins, outs, scr)
        for t in range(len(ins)):
            cps["swap"][t].start()
            cps["load"][t].start()
        for t in range(len(ins)):
            cps["load"][t].wait()
            cps["swap"][t].wait_recv()
            own, theirs = cps["own"][t], cps["theirs"][t]
            for j in range(4):
                own[j] = (own[j].astype(F32) + theirs[j].astype(F32)).astype(BF16)
            for kind in ("keep", "direct", "relay"):
                cps[kind][t].start()

    def mid(self, ins, outs, scr):
        cps = self._copies(ins, outs, scr)
        for t in range(len(ins)):
            cps["relay"][t].wait_recv()
            own, far = cps["own"][t], cps["far"]
            own[far] = (own[far].astype(F32) + cps["relayed"][t][...].astype(F32)).astype(BF16)
            cps["merged"][t].start()

    def finish(self, ins, outs, scr):
        cps = self._copies(ins, outs, scr)
        for t in range(len(ins)):
            cps["swap"][t].wait_send()
            cps["direct"][t].wait()
            cps["relay"][t].wait_send()
            cps["merged"][t].wait()
            cps["keep"][t].wait()


class _Comm:
    def __init__(self, groups):
        self.names = {tag: list(g) for tag, (g, _) in groups.items()}
        self.gathers = {tag: _GatherCarry(list(g.values()), mid_at) for tag, (g, mid_at) in groups.items()}
        self.reduced = {}

    def gathered(self, tag, outs):
        return dict(zip(self.names[tag], outs))

    def reduce_start(self, tag, grads, mid_at=0.5):
        names = list(grads)
        return _ExchangeCarry(names, [grads[n] for n in names], mid_at)

    def reduce_done(self, carry, outs):
        self.reduced.update(zip(carry.names, outs))


def _adamw_math(w, g, m, v):
    m = B1 * m + (1.0 - B1) * g
    v = B2 * v + (1.0 - B2) * (g * g)
    m_hat = m / (1.0 - B1 ** STEP)
    v_hat = v / (1.0 - B2 ** STEP)
    delta = -LR * (m_hat / (jnp.sqrt(v_hat) + AEPS) + WD * w)
    return delta, m, v


def _adamw_big(recv, w, m, v, name):
    def body(r_ref, w_ref, m_ref, v_ref, g_ref, d_ref, mo_ref, vo_ref):
        g = r_ref[0].astype(F32)
        for q in range(1, 3):
            g = g + r_ref[q].astype(F32)
        d, mn, vn = _adamw_math(w_ref[...], g, m_ref[...], v_ref[...])
        g_ref[...] = g
        d_ref[...] = d
        mo_ref[...] = mn
        vo_ref[...] = vn

    rows, n = w.shape
    tr = rows // 2
    row = pl.BlockSpec((tr, n), lambda t: (t, 0))
    return pl.pallas_call(
        body, grid=(2,), in_specs=[pl.BlockSpec((3, tr, n), lambda t: (0, t, 0)), row, row, row],
        out_specs=[row] * 4, out_shape=[jax.ShapeDtypeStruct(w.shape, F32)] * 4,
        compiler_params=_cp(("parallel",)), name=name)(recv, w, m, v)


SMALL_NAMES = ("ffn1_norm", "mix_norm", "ffn2_norm", "conv_b", "conv_ln_g", "conv_ln_b", "q_norm", "k_norm")
SROWS = 16
LOSS_ROW = len(SMALL_NAMES)
CWF = 2


def _small_step(gs, loss_row, gcw, ws, ms, vs, wcw, mcw, vcw, carry=None):
    ns = len(SMALL_NAMES)
    widths = [g.shape[1] for g in gs]

    def body(*refs):
        it = iter(refs)
        take = lambda n: [next(it) for _ in range(n)]
        g_refs, (loss_ref, gcw_ref) = take(ns), take(2)
        w_refs, m_refs, v_refs = take(ns), take(ns), take(ns)
        wcw_ref, mcw_ref, vcw_ref = take(3)
        cins = take(len(carry.in_arrays)) if carry else []
        outs = [take(4) for _ in range(ns)]
        cw_outs, (loss_out,) = take(4), take(1)
        couts = take(len(carry.out_shape)) if carry else []
        send, slots, cslots, send_sems, recv_sems, csend_sems, crecv_sems = take(7)
        cscr = list(it)
        x, y, c = _place()
        me = 4 * x + 2 * y + c
        send[...] = jnp.zeros_like(send)
        for k in range(ns):
            send[k:k + 1, 0:widths[k]] = g_refs[k][...]
        send[LOSS_ROW:LOSS_ROW + 1, 0:128] = loss_ref[...]
        slots[me] = send[...]
        cslots[me] = gcw_ref[...]
        cps = []
        for k in range(1, NDEV):
            peer = (x ^ ((k >> 2) & 1), y ^ ((k >> 1) & 1), c ^ (k & 1))
            cps.append(pltpu.make_async_remote_copy(
                src_ref=send, dst_ref=slots.at[me], send_sem=send_sems.at[k - 1], recv_sem=recv_sems.at[k - 1],
                device_id=peer, device_id_type=MESH))
            cps.append(pltpu.make_async_remote_copy(
                src_ref=gcw_ref, dst_ref=cslots.at[me], send_sem=csend_sems.at[k - 1],
                recv_sem=crecv_sems.at[k - 1], device_id=peer, device_id_type=MESH))
        for cp in cps:
            cp.start()
        if carry:
            carry.start(cins, couts, cscr)
        for cp in cps:
            cp.wait()
        if carry:
            carry.mid(cins, couts, cscr)
        tot = slots[0]
        ctot = cslots[0, me]
        for j in range(1, NDEV):
            tot = tot + slots[j]
            ctot = ctot + cslots[j, me]

        def step(g, w_ref, m_ref, v_ref, o):
            d, mn, vn = _adamw_math(w_ref[...], g, m_ref[...], v_ref[...])
            o[0][...], o[1][...], o[2][...], o[3][...] = g, d, mn, vn

        for k in range(ns):
            step(tot[k:k + 1, 0:widths[k]], w_refs[k], m_refs[k], v_refs[k], outs[k])
        step(ctot, wcw_ref, mcw_ref, vcw_ref, cw_outs)
        loss_out[...] = tot[LOSS_ROW:LOSS_ROW + 1, 0:128]
        if carry:
            carry.finish(cins, couts, cscr)

    args = [*gs, loss_row, gcw, *ws, *ms, *vs, wcw, mcw, vcw]
    out_shape = ([jax.ShapeDtypeStruct((1, n), F32) for n in widths for _ in range(4)]
                 + [jax.ShapeDtypeStruct((CWF, D), F32)] * 4 + [jax.ShapeDtypeStruct((1, 128), F32)])
    n_own = len(out_shape)
    res = pl.pallas_call(
        body, in_specs=[VMEM] * len(args) + (carry.in_specs if carry else []),
        out_specs=[VMEM] * n_own + (carry.out_specs if carry else []),
        out_shape=out_shape + (carry.out_shape if carry else []),
        scratch_shapes=[pltpu.VMEM((SROWS, D), F32), pltpu.VMEM((NDEV, SROWS, D), F32),
                        pltpu.VMEM((NDEV, NDEV, CWF, D), F32)]
                       + [pltpu.SemaphoreType.DMA((NDEV - 1,))] * 4 + (carry.scratch if carry else []),
        name="small_step")(*args, *(carry.in_arrays if carry else []))
    per = [res[4 * k:4 * k + 4] for k in range(ns)]
    return per, res[4 * ns:4 * ns + 4], res[n_own - 1], res[n_own:]


def _pack_cw(a):
    flat = a.reshape(a.shape[:-2] + (CK * HD,))
    pad = [(0, 0)] * (flat.ndim - 1) + [(0, CWF * D - CK * HD)]
    return jnp.pad(flat, pad).reshape(a.shape[:-2] + (CWF, D))


def _unpack_cw(v):
    return v.reshape(-1)[:CK * HD].reshape(1, CK, HD)


def kernel(x, ffn1_norm, ffn1_w_gate, ffn1_w_up, ffn1_w_down, mix_norm, w_in, q_norm, k_norm, conv_w, conv_b, conv_ln_g, conv_ln_b, w_out, ffn2_norm, ffn2_w_gate, ffn2_w_up, ffn2_w_down, loss_target, m_ffn1_norm, m_ffn1_w_gate, m_ffn1_w_up, m_ffn1_w_down, m_mix_norm, m_w_in, m_q_norm, m_k_norm, m_conv_w, m_conv_b, m_conv_ln_g, m_conv_ln_b, m_w_out, m_ffn2_norm, m_ffn2_w_gate, m_ffn2_w_up, m_ffn2_w_down, v_ffn1_norm, v_ffn1_w_gate, v_ffn1_w_up, v_ffn1_w_down, v_mix_norm, v_w_in, v_q_norm, v_k_norm, v_conv_w, v_conv_b, v_conv_ln_g, v_conv_ln_b, v_w_out, v_ffn2_norm, v_ffn2_w_gate, v_ffn2_w_up, v_ffn2_w_down):
    P = dict(ffn1_norm=ffn1_norm, ffn1_w_gate=ffn1_w_gate, ffn1_w_up=ffn1_w_up, ffn1_w_down=ffn1_w_down,
             mix_norm=mix_norm, w_in=w_in, q_norm=q_norm, k_norm=k_norm, conv_w=conv_w, conv_b=conv_b,
             conv_ln_g=conv_ln_g, conv_ln_b=conv_ln_b, w_out=w_out, ffn2_norm=ffn2_norm,
             ffn2_w_gate=ffn2_w_gate, ffn2_w_up=ffn2_w_up, ffn2_w_down=ffn2_w_down)
    M = dict(ffn1_norm=m_ffn1_norm, ffn1_w_gate=m_ffn1_w_gate, ffn1_w_up=m_ffn1_w_up, ffn1_w_down=m_ffn1_w_down,
             mix_norm=m_mix_norm, w_in=m_w_in, q_norm=m_q_norm, k_norm=m_k_norm, conv_w=m_conv_w, conv_b=m_conv_b,
             conv_ln_g=m_conv_ln_g, conv_ln_b=m_conv_ln_b, w_out=m_w_out, ffn2_norm=m_ffn2_norm,
             ffn2_w_gate=m_ffn2_w_gate, ffn2_w_up=m_ffn2_w_up, ffn2_w_down=m_ffn2_w_down)
    V = dict(ffn1_norm=v_ffn1_norm, ffn1_w_gate=v_ffn1_w_gate, ffn1_w_up=v_ffn1_w_up, ffn1_w_down=v_ffn1_w_down,
             mix_norm=v_mix_norm, w_in=v_w_in, q_norm=v_q_norm, k_norm=v_k_norm, conv_w=v_conv_w, conv_b=v_conv_b,
             conv_ln_g=v_conv_ln_g, conv_ln_b=v_conv_ln_b, w_out=v_w_out, ffn2_norm=v_ffn2_norm,
             ffn2_w_gate=v_ffn2_w_gate, ffn2_w_up=v_ffn2_w_up, ffn2_w_down=v_ffn2_w_down)
    order = ["ffn1_norm", "ffn1_w_gate", "ffn1_w_up", "ffn1_w_down", "mix_norm", "w_in", "q_norm", "k_norm",
             "conv_w", "conv_b", "conv_ln_g", "conv_ln_b", "w_out", "ffn2_norm", "ffn2_w_gate", "ffn2_w_up",
             "ffn2_w_down"]
    B, S, _ = x.shape
    T = B * S

    bigs = [("wg1", "ffn1_w_gate", True), ("wu1", "ffn1_w_up", True), ("wd1", "ffn1_w_down", False),
            ("win", "w_in", True), ("wout", "w_out", False),
            ("wg2", "ffn2_w_gate", True), ("wu2", "ffn2_w_up", True), ("wd2", "ffn2_w_down", False)]
    hm = lambda a, tr: jnp.transpose(a[0]) if tr else a[0]
    cw_pad = jnp.zeros((32, 128), F32).at[0:CK, 0:HD].set(conv_w[0])
    shard = {ln: (hm(P[pn], tr), BF16) for ln, pn, tr in bigs}
    gathered = _run_carry(_GatherCarry([shard["wg1"], shard["wu1"], (cw_pad, F32)]), "gather_first")
    W = {"wg1": gathered[0], "wu1": gathered[1]}
    cwg = gathered[2].reshape(NDEV, 32, 128)[:, 0:CK, 0:HD]
    W["conv_w"] = jnp.transpose(cwg, (1, 0, 2)).reshape(CK, DC)
    norms = {n: P[n] for n in SMALL_NAMES}
    comm = _Comm({"down_in": ({n: shard[n] for n in ("wd1", "win")}, 0.5),
                  "ffn2": ({n: shard[n] for n in ("wg2", "wu2", "wd2", "wout")}, 0.5)})

    loss_part, gx, _, small = _local_step(x.reshape(T, D), loss_target.reshape(T, D), norms, W, B, S, comm)

    G, Dl, Mn, Vn = {}, {}, {}, {}
    dcw = small["conv_w"].reshape(CK, NDEV, HD).transpose(1, 0, 2)
    loss_row = jnp.zeros((1, 128), F32).at[0, 0].set(loss_part)
    per, cw_outs, loss_out, got = _small_step(
        [small[n] for n in SMALL_NAMES], loss_row, _pack_cw(dcw),
        [P[n] for n in SMALL_NAMES], [M[n] for n in SMALL_NAMES], [V[n] for n in SMALL_NAMES],
        _pack_cw(P["conv_w"][0]), _pack_cw(M["conv_w"][0]), _pack_cw(V["conv_w"][0]), carry=comm.last)
    comm.reduce_done(comm.last, got)
    loss = loss_out[0, 0]
    for n, outs in zip(SMALL_NAMES, per):
        G[n], Dl[n], Mn[n], Vn[n] = outs
    G["conv_w"], Dl["conv_w"], Mn["conv_w"], Vn["conv_w"] = [_unpack_cw(o) for o in cw_outs]

    for ln, pn, tr in bigs:
        outs = _adamw_big(comm.reduced[ln], hm(P[pn], tr), hm(M[pn], tr), hm(V[pn], tr), "adamw_" + ln)
        G[pn], Dl[pn], Mn[pn], Vn[pn] = [(jnp.transpose(o) if tr else o)[None] for o in outs]

    return (loss, gx.reshape(B, S, D), *[G[n] for n in order], *[Dl[n] for n in order],
            *[Mn[n] for n in order], *[Vn[n] for n in order])
```

```python
import functools

import jax
import jax.numpy as jnp
from jax import lax
from jax.experimental import pallas as pl
from jax.experimental.pallas import tpu as pltpu

F32 = jnp.float32
BF16 = jnp.bfloat16

D = 1024
FF = 2816
HD = 64
DA = 512
DC = 512
DIN = 2560
CK = 31
BLK = 128
DILS = (1, 4, 16)
EPS = 1e-6
NDEV = 8
MESH = pl.DeviceIdType.MESH

LR, B1, B2, AEPS, WD, STEP = 0.001, 0.9, 0.999, 1e-08, 0.01, 10

NT = (((1,), (1,)), ((), ()))
TN = (((0,), (0,)), ((), ()))

VMEM_LIMIT = 60 * 1024 * 1024


def _cp(sem=None):
    return pltpu.CompilerParams(dimension_semantics=sem, vmem_limit_bytes=VMEM_LIMIT)


def _sigmoid(x):
    return 0.5 * (jnp.tanh(0.5 * x) + 1.0)


def _pallas(body, args, *, grid, in_specs, out_specs, out_shape, scratch_shapes, sem, name, carry=None):
    if carry is None:
        outs = pl.pallas_call(body, grid=grid, in_specs=in_specs, out_specs=out_specs, out_shape=out_shape,
                              scratch_shapes=scratch_shapes, compiler_params=_cp(sem), name=name)(*args)
        return outs, None
    n_in, n_out, n_scr = len(in_specs), len(out_shape), len(scratch_shapes)
    c_in, c_out = len(carry.in_arrays), len(carry.out_shape)

    def wrapped(*refs):
        ins, refs = refs[:n_in], refs[n_in:]
        cins, refs = refs[:c_in], refs[c_in:]
        outs, refs = refs[:n_out], refs[n_out:]
        couts, refs = refs[:c_out], refs[c_out:]
        scr, cscr = refs[:n_scr], refs[n_scr:]
        ids = [pl.program_id(a) for a in range(len(grid))]
        step = ids[0]
        for i, n in zip(ids[1:], grid[1:]):
            step = step * n + i
        steps = functools.reduce(lambda a, b: a * b, grid)

        @pl.when(step == 0)
        def _():
            carry.start(cins, couts, cscr)

        body(*ins, *outs, *scr)

        for frac, stage in carry.stages():
            @pl.when(step == int(steps * frac))
            def _(stage=stage):
                stage(cins, couts, cscr)

        @pl.when(step == steps - 1)
        def _():
            carry.finish(cins, couts, cscr)

    outs = pl.pallas_call(
        wrapped, grid=grid, in_specs=list(in_specs) + carry.in_specs, out_specs=list(out_specs) + carry.out_specs,
        out_shape=list(out_shape) + carry.out_shape, scratch_shapes=list(scratch_shapes) + carry.scratch,
        compiler_params=_cp(("arbitrary",) * len(grid)), name=name)(*args, *carry.in_arrays)
    return outs[:n_out], outs[n_out:]


FC = 256


def _resident(shape):
    return pl.BlockSpec(shape, lambda *_: (0,) * len(shape), pipeline_mode=pl.Buffered(1))


def _mix_out_ffn_loss(h1, attn, conv, wout, gain, wg, wu, wd, target, name):
    T = h1.shape[0]
    tm = 512
    nt = T // tm

    def body(h1_ref, at_ref, cv_ref, wo_ref, gain_ref, wg_ref, wu_ref, wd_ref, t_ref,
             h2_ref, n_ref, g_ref, u_ref, dout_ref, dyb_ref, sq_ref, a_hbm, a_scr, a_sem):
        t = pl.program_id(0)
        a_out = lambda i: pltpu.make_async_copy(a_scr, a_hbm.at[pl.ds(pl.multiple_of(i * tm, tm), tm), :], a_sem)

        @pl.when(t > 0)
        def _():
            a_out(t - 1).wait()

        xv = (h1_ref[...]
              + jnp.dot(at_ref[...], wo_ref[0:DA, :], preferred_element_type=F32)
              + jnp.dot(cv_ref[...], wo_ref[DA:D, :], preferred_element_type=F32))
        h2_ref[...] = xv
        r = lax.rsqrt(jnp.mean(xv * xv, axis=-1, keepdims=True) + EPS)
        n_ref[...] = (xv * r * gain_ref[...]).astype(BF16)
        for c in range(FF // FC):
            cols = slice(c * FC, (c + 1) * FC)
            nb = n_ref[...]
            g = lax.dot_general(nb, wg_ref[cols, :], NT, preferred_element_type=F32)
            u = lax.dot_general(nb, wu_ref[cols, :], NT, preferred_element_type=F32)
            g_ref[:, cols] = g.astype(BF16)
            u_ref[:, cols] = u.astype(BF16)
            a_scr[:, cols] = (g * _sigmoid(g) * u).astype(BF16)
        a_out(t).start()
        e = h2_ref[...] + 0.5 * jnp.dot(a_scr[...], wd_ref[...], preferred_element_type=F32) - t_ref[...]
        dout = e * (1.0 / D)
        dout_ref[...] = dout
        dyb_ref[...] = (0.5 * dout).astype(BF16)
        sq_ref[...] = jnp.sum(e * e, axis=0, keepdims=True)[None]

        @pl.when(t == nt - 1)
        def _():
            a_out(t).wait()

    row = pl.BlockSpec((tm, D), lambda t: (t, 0))
    half = pl.BlockSpec((tm, DA), lambda t: (t, 0))
    wide = pl.BlockSpec((tm, FF), lambda t: (t, 0))
    outs, _ = _pallas(
        body, (h1, attn, conv, wout, gain, wg, wu, wd, target), grid=(nt,),
        in_specs=[row, half, half, _resident((D, D)), _resident((1, D)), _resident((FF, D)), _resident((FF, D)),
                  _resident((FF, D)), row],
        out_specs=[row, row, wide, wide, row, row, pl.BlockSpec((1, 1, D), lambda t: (t, 0, 0)), HBM],
        out_shape=[jax.ShapeDtypeStruct((T, D), F32), jax.ShapeDtypeStruct((T, D), BF16)]
                  + [jax.ShapeDtypeStruct((T, FF), BF16)] * 2
                  + [jax.ShapeDtypeStruct((T, D), F32), jax.ShapeDtypeStruct((T, D), BF16),
                     jax.ShapeDtypeStruct((nt, 1, D), F32), jax.ShapeDtypeStruct((T, FF), BF16)],
        scratch_shapes=[pltpu.VMEM((tm, FF), BF16), pltpu.SemaphoreType.DMA(())],
        sem=("arbitrary",), name=name)
    return outs


def _ffn_gate_up(x, gain, wg, wu, name, carry=None):
    T = x.shape[0]
    tm = 512

    def body(x_ref, gain_ref, wg_ref, wu_ref, n_ref, g_ref, u_ref, a_ref):
        xv = x_ref[...]
        r = lax.rsqrt(jnp.mean(xv * xv, axis=-1, keepdims=True) + EPS)
        n_ref[...] = (xv * r * gain_ref[...]).astype(BF16)
        for c in range(FF // FC):
            cols = slice(c * FC, (c + 1) * FC)
            nb = n_ref[...]
            g = lax.dot_general(nb, wg_ref[cols, :], NT, preferred_element_type=F32)
            u = lax.dot_general(nb, wu_ref[cols, :], NT, preferred_element_type=F32)
            g_ref[:, cols] = g.astype(BF16)
            u_ref[:, cols] = u.astype(BF16)
            a_ref[:, cols] = (g * _sigmoid(g) * u).astype(BF16)

    row = pl.BlockSpec((tm, D), lambda t: (t, 0))
    wide = pl.BlockSpec((tm, FF), lambda t: (t, 0))
    return _pallas(
        body, (x, gain, wg, wu), grid=(T // tm,),
        in_specs=[row, _resident((1, D)), _resident((FF, D)), _resident((FF, D))],
        out_specs=[row, wide, wide, wide],
        out_shape=[jax.ShapeDtypeStruct((T, D), BF16)] + [jax.ShapeDtypeStruct((T, FF), BF16)] * 3,
        scratch_shapes=[], sem=("parallel",), name=name, carry=carry)


def _ffn_down_mix_in(x, a, wd, gain, win, name):
    T = x.shape[0]
    tm = 512

    def body(x_ref, a_ref, wd_ref, gain_ref, win_ref, h_ref, u_ref, n_ref):
        hv = x_ref[...] + 0.5 * jnp.dot(a_ref[...], wd_ref[...], preferred_element_type=F32)
        h_ref[...] = hv
        r = lax.rsqrt(jnp.mean(hv * hv, axis=-1, keepdims=True) + EPS)
        n_ref[...] = (hv * r * gain_ref[...]).astype(BF16)
        u_ref[...] = lax.dot_general(n_ref[...], win_ref[...], NT, preferred_element_type=F32).astype(BF16)

    row = pl.BlockSpec((tm, D), lambda t: (t, 0))
    wide = pl.BlockSpec((tm, FF), lambda t: (t, 0))
    outs, _ = _pallas(
        body, (x, a, wd, gain, win), grid=(T // tm,),
        in_specs=[row, wide, _resident((FF, D)), _resident((1, D)), _resident((DIN, D))],
        out_specs=[row, pl.BlockSpec((tm, DIN), lambda t: (t, 0)), row],
        out_shape=[jax.ShapeDtypeStruct((T, D), F32), jax.ShapeDtypeStruct((T, DIN), BF16),
                   jax.ShapeDtypeStruct((T, D), BF16)],
        scratch_shapes=[], sem=("parallel",), name=name)
    return outs


def _ffn_bwd_act(dyb, g, u, x, dout, gain, wg, wu, wd, name, carry=None):
    T = x.shape[0]
    tm = 256
    nt = T // tm

    def body(dy_ref, g_ref, u_ref, x_ref, dout_ref, gain_ref, wg_ref, wu_ref, wd_ref,
             dg_ref, du_ref, dx_ref, dgn_ref):
        for c in range(FF // FC):
            cols = slice(c * FC, (c + 1) * FC)
            da = lax.dot_general(dy_ref[...], wd_ref[cols, :], NT, preferred_element_type=F32)
            gv = g_ref[:, cols].astype(F32)
            uv = u_ref[:, cols].astype(F32)
            sg = _sigmoid(gv)
            dg_ref[:, cols] = (da * uv * (sg * (1.0 + gv * (1.0 - sg)))).astype(BF16)
            du_ref[:, cols] = (da * (gv * sg)).astype(BF16)
        dn = (jnp.dot(dg_ref[...], wg_ref[...], preferred_element_type=F32)
              + jnp.dot(du_ref[...], wu_ref[...], preferred_element_type=F32))
        dx, dgain = _rms_bwd_rows(dn, x_ref[...], gain_ref[...])
        dx_ref[...] = dout_ref[...] + dx
        dgn_ref[...] = dgain[None]

    row = pl.BlockSpec((tm, D), lambda t: (t, 0))
    wide = pl.BlockSpec((tm, FF), lambda t: (t, 0))
    return _pallas(
        body, (dyb, g, u, x, dout, gain, wg, wu, wd), grid=(nt,),
        in_specs=[row, wide, wide, row, row, _resident((1, D)), _resident((FF, D)), _resident((FF, D)),
                  _resident((FF, D))],
        out_specs=[wide, wide, row, pl.BlockSpec((1, 1, D), lambda t: (t, 0, 0))],
        out_shape=[jax.ShapeDtypeStruct((T, FF), BF16)] * 2
                  + [jax.ShapeDtypeStruct((T, D), F32), jax.ShapeDtypeStruct((nt, 1, D), F32)],
        scratch_shapes=[], sem=("parallel",), name=name, carry=carry)


def _ffn_bwd_w(lhs, rhs, name, carry=None):
    T = rhs.shape[0]
    tf = 256

    def body(l_ref, r_ref, dw_ref):
        dw_ref[...] = lax.dot_general(l_ref[...], r_ref[...], TN, preferred_element_type=F32).astype(BF16)

    (dw,), got = _pallas(
        body, (lhs, rhs), grid=(FF // tf,),
        in_specs=[pl.BlockSpec((T, tf), lambda f: (0, f)), _resident((T, D))],
        out_specs=[pl.BlockSpec((tf, D), lambda f: (f, 0))], out_shape=[jax.ShapeDtypeStruct((FF, D), BF16)],
        scratch_shapes=[], sem=("parallel",), name=name, carry=carry)
    return dw, got


def _rms_bwd_rows(dn, xv, gain):
    r = lax.rsqrt(jnp.mean(xv * xv, axis=-1, keepdims=True) + EPS)
    xhat = xv * r
    dxhat = dn * gain
    dx = r * (dxhat - xhat * jnp.mean(dxhat * xhat, axis=-1, keepdims=True))
    return dx, jnp.sum(dn * xhat, axis=0, keepdims=True)


def _mix_out_bwd(dh, attn, conv, wout):
    T = dh.shape[0]
    tm = 512
    nt = T // tm

    def body(dh_ref, a_ref, c_ref, w_ref, da_ref, dc_ref, dw_ref, acc_scr):
        t = pl.program_id(0)

        @pl.when(t == 0)
        def _():
            acc_scr[...] = jnp.zeros_like(acc_scr)

        dhb = dh_ref[...].astype(BF16)
        dmix = lax.dot_general(dhb, w_ref[...], NT, preferred_element_type=F32)
        da_ref[...] = dmix[:, 0:DA].astype(BF16)
        dc_ref[...] = dmix[:, DA:D].astype(BF16)
        acc_scr[0:DA, :] += lax.dot_general(a_ref[...], dhb, TN, preferred_element_type=F32)
        acc_scr[DA:D, :] += lax.dot_general(c_ref[...], dhb, TN, preferred_element_type=F32)

        @pl.when(t == nt - 1)
        def _():
            dw_ref[...] = acc_scr[...].astype(BF16)

    row = pl.BlockSpec((tm, D), lambda t: (t, 0))
    half = pl.BlockSpec((tm, DA), lambda t: (t, 0))
    full = pl.BlockSpec((D, D), lambda t: (0, 0))
    return pl.pallas_call(
        body, grid=(nt,), in_specs=[row, half, half, full], out_specs=[half, half, full],
        out_shape=[jax.ShapeDtypeStruct((T, DA), BF16)] * 2 + [jax.ShapeDtypeStruct((D, D), BF16)],
        scratch_shapes=[pltpu.VMEM((D, D), F32)],
        compiler_params=_cp(("arbitrary",)), name="mix_out_bwd")(dh, attn, conv, wout)


def _mix_in_bwd(dparts, win, nb, h, dh, gain):
    T = h.shape[0]
    tm = 512
    nt = T // tm

    def body(d0, d1, d2, d3, d4, w_ref, n_ref, h_ref, dh_ref, gain_ref,
             dw_ref, dx_ref, dyb_ref, dg_ref, acc_scr):
        t = pl.program_id(0)

        @pl.when(t == 0)
        def _():
            acc_scr[...] = jnp.zeros_like(acc_scr)

        n = n_ref[...]
        dn = jnp.zeros((tm, D), F32)
        for i, d_ref in enumerate((d0, d1, d2, d3, d4)):
            dv = d_ref[...]
            dn = dn + jnp.dot(dv, w_ref[i * DA:(i + 1) * DA, :], preferred_element_type=F32)
            acc_scr[i * DA:(i + 1) * DA, :] += lax.dot_general(dv, n, TN, preferred_element_type=F32)
        dx, dgain = _rms_bwd_rows(dn, h_ref[...], gain_ref[...])
        tot = dh_ref[...] + dx
        dx_ref[...] = tot
        dyb_ref[...] = (0.5 * tot).astype(BF16)
        dg_ref[...] = dgain[None]

        @pl.when(t == nt - 1)
        def _():
            dw_ref[...] = acc_scr[...].astype(BF16)

    row = pl.BlockSpec((tm, D), lambda t: (t, 0))
    half = pl.BlockSpec((tm, DA), lambda t: (t, 0))
    full = pl.BlockSpec((DIN, D), lambda t: (0, 0))
    return pl.pallas_call(
        body, grid=(nt,),
        in_specs=[half] * 5 + [full, row, row, row, pl.BlockSpec((1, D), lambda t: (0, 0))],
        out_specs=[full, row, row, pl.BlockSpec((1, 1, D), lambda t: (t, 0, 0))],
        out_shape=[jax.ShapeDtypeStruct((DIN, D), BF16), jax.ShapeDtypeStruct((T, D), F32),
                   jax.ShapeDtypeStruct((T, D), BF16), jax.ShapeDtypeStruct((nt, 1, D), F32)],
        scratch_shapes=[pltpu.VMEM((DIN, D), F32)],
        compiler_params=_cp(("arbitrary",)), name="mix_in_bwd")(*dparts, win, nb, h, dh, gain)


def _head_masks():
    lane = lax.broadcasted_iota(jnp.int32, (1, 2 * HD), 1)
    m0 = lane < HD
    return m0, jnp.logical_not(m0)


def _stack_heads(v, m0, m1):
    z = jnp.zeros_like(v)
    return jnp.concatenate([jnp.where(m0, v, z), jnp.where(m1, v, z)], axis=0)


def _unstack_heads(v2, m0):
    return jnp.where(m0, v2[0:BLK], v2[BLK:2 * BLK])


def _head_sums(xv):
    ri = lax.broadcasted_iota(jnp.int32, (2 * HD, 2 * HD), 0)
    ci = lax.broadcasted_iota(jnp.int32, (2 * HD, 2 * HD), 1)
    ones = jnp.where((ri < HD) == (ci < HD), 1.0, 0.0).astype(BF16)
    hi = xv.astype(BF16)
    lo = (xv - hi.astype(F32)).astype(BF16)
    return (jnp.dot(hi, ones, preferred_element_type=F32) + jnp.dot(lo, ones, preferred_element_type=F32))


def _head_rms(xv):
    return lax.rsqrt(_head_sums(xv * xv) * (1.0 / HD) + EPS)


def _band_mask(first):
    qi = lax.broadcasted_iota(jnp.int32, (BLK, 2 * BLK), 0)
    ci = lax.broadcasted_iota(jnp.int32, (BLK, 2 * BLK), 1)
    band = (ci >= qi) & (ci <= qi + BLK)
    return band & ((ci >= BLK) | jnp.logical_not(first))


def _block_rows(j, d, seg):
    r, n = j // seg, j % seg
    start = r + (d * BLK) * n
    first = n == 0
    prev = jnp.where(first, start, start - d * BLK)
    return pl.ds(start, BLK, stride=d), pl.ds(prev, BLK, stride=d), first


def _block_keys(refs, cur, prev, first, single):
    if single:
        qi = lax.broadcasted_iota(jnp.int32, (BLK, BLK), 0)
        ci = lax.broadcasted_iota(jnp.int32, (BLK, BLK), 1)
        return [r[cur, :].astype(BF16) for r in refs], ci <= qi
    return ([jnp.concatenate([r[prev, :], r[cur, :]], axis=0).astype(BF16) for r in refs], _band_mask(first))


def _attn_fwd(u, qg2, kg2, B, S, carry=None):
    T = B * S
    NB = S // BLK
    scale = HD ** -0.5

    def body(q_ref, k_ref, v_ref, qg_ref, kg_ref, o_ref, lse_ref, qn, kn, vn, os_, ls_):
        m0, m1 = _head_masks()
        qv = q_ref[...].astype(F32)
        qn[...] = qv * _head_rms(qv) * (qg_ref[...] * scale)
        kv = k_ref[...].astype(F32)
        kn[...] = kv * _head_rms(kv) * kg_ref[...]
        vn[...] = v_ref[...].astype(F32)

        for i, d in enumerate(DILS):
            seg = NB // d

            def blk(j, c, i=i, d=d, seg=seg):
                cur, prev, first = _block_rows(j, d, seg)
                q2 = _stack_heads(qn[cur, :].astype(BF16), m0, m1)
                (kk, vv), mask = _block_keys((kn, vn), cur, prev, first, False)
                s = lax.dot_general(q2, kk, NT, preferred_element_type=F32)
                s = jnp.where(jnp.concatenate([mask, mask], axis=0), s, -1e30)
                mx = jnp.max(s, axis=-1, keepdims=True)
                p = jnp.exp(s - mx)
                l = jnp.sum(p, axis=-1, keepdims=True)
                o2 = jnp.dot((p * (1.0 / l)).astype(BF16), vv, preferred_element_type=F32)
                os_[i, cur, :] = _unstack_heads(o2, m0)
                ls_[i, cur, :] = _unstack_heads(mx + jnp.log(l), m0)
                return c

            lax.fori_loop(0, NB, blk, 0, unroll=8)

        def comb(c, carry):
            rows = pl.ds(pl.multiple_of(c * 256, 256), 256)
            l0, l1, l2 = ls_[0, rows, :], ls_[1, rows, :], ls_[2, rows, :]
            mx = jnp.maximum(jnp.maximum(l0, l1), l2)
            e0, e1, e2 = jnp.exp(l0 - mx), jnp.exp(l1 - mx), jnp.exp(l2 - mx)
            tot = e0 + e1 + e2
            inv = 1.0 / tot
            o = (e0 * os_[0, rows, :] + e1 * os_[1, rows, :] + e2 * os_[2, rows, :]) * inv
            o_ref[rows, :] = o.astype(BF16)
            lse_ref[rows, :] = mx + jnp.log(tot)
            return carry

        lax.fori_loop(0, S // 256, comb, 0)

    pair = 2 * HD
    blk_spec = lambda off: pl.BlockSpec((S, pair), lambda b, p, off=off: (b, off + p))
    gspec = pl.BlockSpec((1, pair), lambda b, p: (0, 0))
    return _pallas(
        body, (u, u, u, qg2, kg2), grid=(B, DA // pair),
        in_specs=[blk_spec(0), blk_spec(DA // pair), blk_spec(2 * DA // pair), gspec, gspec],
        out_specs=[blk_spec(0), blk_spec(0)],
        out_shape=[jax.ShapeDtypeStruct((T, DA), BF16), jax.ShapeDtypeStruct((T, DA), F32)],
        scratch_shapes=[pltpu.VMEM((S, pair), F32)] * 3 + [pltpu.VMEM((3, S, pair), F32)] * 2,
        sem=("parallel", "parallel"), name="attn_fwd", carry=carry)


def _attn_bwd(u, attn, dattn, lse, qg2, kg2, B, S, carry=None):
    T = B * S
    NB = S // BLK
    scale = HD ** -0.5
    pair = 2 * HD

    def body(q_ref, k_ref, v_ref, o_ref, do_ref, lse_ref, qg_ref, kg_ref,
             dq_ref, dk_ref, dv_ref, dgn_ref,
             qn, kn, vn, don, ldl, accq, acck, accv, rq, rk):
        m0, m1 = _head_masks()
        lane = lax.broadcasted_iota(jnp.int32, (1, pair), 1)
        qv = q_ref[...].astype(F32)
        rq[...] = _head_rms(qv)
        qn[...] = qv * rq[...] * (qg_ref[...] * scale)
        kv = k_ref[...].astype(F32)
        rk[...] = _head_rms(kv)
        kn[...] = kv * rk[...] * kg_ref[...]
        vn[...] = v_ref[...].astype(F32)
        dov = do_ref[...].astype(F32)
        don[...] = dov
        ldl[...] = jnp.where((lane % HD) < HD // 2, lse_ref[...], _head_sums(dov * o_ref[...].astype(F32)))

        for i, d in enumerate(DILS):
            seg = NB // d

            def blk(j, c, i=i, d=d, seg=seg):
                cur, prev, first = _block_rows(j, d, seg)
                q2 = _stack_heads(qn[cur, :].astype(BF16), m0, m1)
                do2 = _stack_heads(don[cur, :].astype(BF16), m0, m1)
                (kk, vv), mask = _block_keys((kn, vn), cur, prev, first, seg == 1)
                ldv = ldl[cur, :]
                lse2 = jnp.concatenate([ldv[:, 0:1], ldv[:, HD:HD + 1]], axis=0)
                dl2 = jnp.concatenate([ldv[:, HD // 2:HD // 2 + 1], ldv[:, HD + HD // 2:HD + HD // 2 + 1]], axis=0)
                s = lax.dot_general(q2, kk, NT, preferred_element_type=F32)
                p = jnp.where(jnp.concatenate([mask, mask], axis=0), jnp.exp(s - lse2), 0.0)
                dp = lax.dot_general(do2, vv, NT, preferred_element_type=F32)
                ds = (p * (dp - dl2)).astype(BF16)
                dq_acc = _unstack_heads(jnp.dot(ds, kk, preferred_element_type=F32), m0)
                dk_acc = lax.dot_general(ds, q2, TN, preferred_element_type=F32)
                dv_acc = lax.dot_general(p.astype(BF16), do2, TN, preferred_element_type=F32)
                if i == 0:
                    accq[cur, :] = dq_acc
                    acck[cur, :] = dk_acc[BLK:2 * BLK]
                    accv[cur, :] = dv_acc[BLK:2 * BLK]
                    acck[prev, :] += dk_acc[0:BLK]
                    accv[prev, :] += dv_acc[0:BLK]
                elif seg == 1:
                    accq[cur, :] += dq_acc
                    acck[cur, :] += dk_acc
                    accv[cur, :] += dv_acc
                else:
                    accq[cur, :] += dq_acc
                    acck[prev, :] += dk_acc[0:BLK]
                    acck[cur, :] += dk_acc[BLK:2 * BLK]
                    accv[prev, :] += dv_acc[0:BLK]
                    accv[cur, :] += dv_acc[BLK:2 * BLK]
                return c

            lax.fori_loop(0, NB, blk, 0, unroll=8)

        def norm_bwd(x_ref, r_ref, dn, gain):
            r = r_ref[...]
            xhat = x_ref[...].astype(F32) * r
            dxhat = dn * gain
            dx = r * (dxhat - xhat * (_head_sums(dxhat * xhat) * (1.0 / HD)))
            return dx, jnp.sum(dn * xhat, axis=0, keepdims=True)

        dq, dgq = norm_bwd(q_ref, rq, accq[...], qg_ref[...] * scale)
        dk, dgk = norm_bwd(k_ref, rk, acck[...], kg_ref[...])
        dq_ref[...] = dq.astype(BF16)
        dk_ref[...] = dk.astype(BF16)
        dv_ref[...] = accv[...].astype(BF16)
        dgn_ref[...] = jnp.concatenate([dgq * scale, dgk, jnp.zeros((6, pair), F32)], axis=0)[None]

    blk_spec = lambda off: pl.BlockSpec((S, pair), lambda b, p, off=off: (b, off + p))
    gspec = pl.BlockSpec((1, pair), lambda b, p: (0, 0))
    np_ = DA // pair
    return _pallas(
        body, (u, u, u, attn, dattn, lse, qg2, kg2), grid=(B, np_),
        in_specs=[blk_spec(0), blk_spec(np_), blk_spec(2 * np_), blk_spec(0), blk_spec(0), blk_spec(0),
                  gspec, gspec],
        out_specs=[blk_spec(0), blk_spec(0), blk_spec(0),
                   pl.BlockSpec((1, 8, pair), lambda b, p: (b * np_ + p, 0, 0))],
        out_shape=[jax.ShapeDtypeStruct((T, DA), BF16)] * 3 + [jax.ShapeDtypeStruct((B * np_, 8, pair), F32)],
        scratch_shapes=[pltpu.VMEM((S, pair), F32)] * 10,
        sem=("parallel", "parallel"), name="attn_bwd", carry=carry)


CT = 32
CPAD = 32


def _shifted(win, offsets):
    rolled, out = {}, {}
    n = win.shape[0]
    for o in offsets:
        sub = o % 8
        if sub not in rolled:
            rolled[sub] = win if sub == 0 else pltpu.roll(win, n - sub, 0)
        out[o] = rolled[sub][o - sub:o - sub + CT, :]
    return out


def _ln_fwd(y, g, b):
    mu = jnp.mean(y, axis=-1, keepdims=True)
    yc = y - mu
    rstd = lax.rsqrt(jnp.mean(yc * yc, axis=-1, keepdims=True) + EPS)
    xhat = yc * rstd
    return xhat, rstd, xhat * g + b


def _fill_glu(ca_ref, cg_ref, glu, S):
    glu[pl.ds(0, CPAD), :] = jnp.zeros((CPAD, DC), F32)

    def fill(i, c):
        rows = pl.ds(pl.multiple_of(i * 256, 256), 256)
        a = ca_ref[rows, :].astype(F32)
        gt = cg_ref[rows, :].astype(F32)
        glu[pl.ds(pl.multiple_of(CPAD + i * 256, CT), 256), :] = a * _sigmoid(gt)
        return c

    lax.fori_loop(0, S // 256, fill, 0)


def _conv_fwd(u, cw, cb, lg, lb, B, S):
    T = B * S

    def body(ca_ref, cg_ref, w_ref, b_ref, lg_ref, lb_ref, o_ref, y_ref, glu):
        _fill_glu(ca_ref, cg_ref, glu, S)

        def step(i, c):
            t0 = pl.multiple_of(i * CT, CT)
            win = glu[pl.ds(t0, 2 * CT), :]
            acc = jnp.zeros((CT, DC), F32) + b_ref[...]
            taps = _shifted(win, [k + 2 for k in range(CK)])
            for k in range(CK):
                acc = acc + taps[k + 2] * w_ref[k:k + 1, :]
            y_ref[pl.ds(t0, CT), :] = acc
            _, _, z = _ln_fwd(acc, lg_ref[...], lb_ref[...])
            o_ref[pl.ds(t0, CT), :] = (z * _sigmoid(z)).astype(BF16)
            return c

        lax.fori_loop(0, S // CT, step, 0, unroll=4)

    vec = pl.BlockSpec((1, DC), lambda b: (0, 0))
    return pl.pallas_call(
        body, grid=(B,),
        in_specs=[pl.BlockSpec((S, DC), lambda b: (b, 3)), pl.BlockSpec((S, DC), lambda b: (b, 4)),
                  pl.BlockSpec((CT, DC), lambda b: (0, 0)), vec, vec, vec],
        out_specs=[pl.BlockSpec((S, DC), lambda b: (b, 0))] * 2,
        out_shape=[jax.ShapeDtypeStruct((T, DC), BF16), jax.ShapeDtypeStruct((T, DC), F32)],
        scratch_shapes=[pltpu.VMEM((CPAD + S, DC), F32)],
        compiler_params=_cp(("parallel",)), name="conv_fwd")(u, u, cw, cb, lg, lb)


def _conv_bwd(u, y, dconv, cw, lg, lb, B, S):
    T = B * S

    def body(ca_ref, cg_ref, y_ref, dc_ref, w_ref, lg_ref, lb_ref,
             dca_ref, dcg_ref, dw_ref, ds_ref, glu, dyp, dwacc):
        _fill_glu(ca_ref, cg_ref, glu, S)
        dyp[pl.ds(S, CPAD), :] = jnp.zeros((CPAD, DC), F32)
        lgv, lbv = lg_ref[...], lb_ref[...]

        def sum8(v):
            return functools.reduce(jnp.add, [v[r:r + 8] for r in range(0, v.shape[0], 8)])

        P1 = 4 * CT

        def p1(i, carry):
            sb, sg, sl = carry
            t0 = pl.multiple_of(i * P1, P1)
            xhat, rstd, z = _ln_fwd(y_ref[pl.ds(t0, P1), :], lgv, lbv)
            sz = _sigmoid(z)
            dz = dc_ref[pl.ds(t0, P1), :].astype(F32) * (sz * (1.0 + z * (1.0 - sz)))
            dxhat = dz * lgv
            dy = rstd * (dxhat - jnp.mean(dxhat, axis=-1, keepdims=True)
                         - xhat * jnp.mean(dxhat * xhat, axis=-1, keepdims=True))
            dyp[pl.ds(t0, P1), :] = dy
            return sb + sum8(dy), sg + sum8(dz * xhat), sl + sum8(dz)

        z8 = jnp.zeros((8, DC), F32)
        sb, sg, sl = lax.fori_loop(0, S // P1, p1, (z8, z8, z8))
        rs = lambda v: jnp.sum(v, axis=0, keepdims=True)
        ds_ref[...] = jnp.concatenate([rs(sb), rs(sg), rs(sl), jnp.zeros((5, DC), F32)], axis=0)[None]

        def p2(i, c):
            t0 = pl.multiple_of(i * CT, CT)
            win = dyp[pl.ds(t0, 2 * CT), :]
            acc = jnp.zeros((CT, DC), F32)
            taps = _shifted(win, [30 - k for k in range(CK)])
            for k in range(CK):
                acc = acc + taps[30 - k] * w_ref[k:k + 1, :]
            a = ca_ref[pl.ds(t0, CT), :].astype(F32)
            sgt = _sigmoid(cg_ref[pl.ds(t0, CT), :].astype(F32))
            dca_ref[pl.ds(t0, CT), :] = (acc * sgt).astype(BF16)
            dcg_ref[pl.ds(t0, CT), :] = (acc * a * sgt * (1.0 - sgt)).astype(BF16)
            return c

        lax.fori_loop(0, S // CT, p2, 0)

        dwacc[...] = jnp.zeros_like(dwacc)

        def p3(i, c):
            t0 = pl.multiple_of(i * CT, CT)
            win = glu[pl.ds(t0, 2 * CT), :]
            dy = dyp[pl.ds(t0, CT), :]
            for k in range(CK):
                dwacc[k] += sum8(dy * win[k + 2:k + 2 + CT, :])
            return c

        lax.fori_loop(0, S // CT, p3, 0)
        dw_ref[...] = jnp.sum(dwacc[...], axis=1)[None]

    vec = pl.BlockSpec((1, DC), lambda b: (0, 0))
    seq = pl.BlockSpec((S, DC), lambda b: (b, 0))
    return pl.pallas_call(
        body, grid=(B,),
        in_specs=[pl.BlockSpec((S, DC), lambda b: (b, 3)), pl.BlockSpec((S, DC), lambda b: (b, 4)),
                  seq, seq, pl.BlockSpec((CT, DC), lambda b: (0, 0)), vec, vec],
        out_specs=[seq, seq, pl.BlockSpec((1, CT, DC), lambda b: (b, 0, 0)),
                   pl.BlockSpec((1, 8, DC), lambda b: (b, 0, 0))],
        out_shape=[jax.ShapeDtypeStruct((T, DC), BF16)] * 2
                  + [jax.ShapeDtypeStruct((B, CT, DC), F32), jax.ShapeDtypeStruct((B, 8, DC), F32)],
        scratch_shapes=[pltpu.VMEM((CPAD + S, DC), F32), pltpu.VMEM((S + CPAD, DC), F32),
                        pltpu.VMEM((CT, 8, DC), F32)],
        compiler_params=_cp(("parallel",)), name="conv_bwd")(u, u, y, dconv, cw, lg, lb)


def _local_step(x, target, norms, W, B, S, comm=None):
    qg2 = jnp.concatenate([norms["q_norm"], norms["q_norm"]], axis=1)
    kg2 = jnp.concatenate([norms["k_norm"], norms["k_norm"]], axis=1)
    cw = jnp.concatenate([W["conv_w"], jnp.zeros((1, DC), F32)], axis=0)

    W = dict(W)
    (n1, g1, u1, act1), got = _ffn_gate_up(x, norms["ffn1_norm"], W["wg1"], W["wu1"], "ffn1_gate_up",
                                           carry=comm.gathers["down_in"] if comm else None)
    if comm:
        W.update(comm.gathered("down_in", got))
    h1, u, n2 = _ffn_down_mix_in(x, act1, W["wd1"], norms["mix_norm"], W["win"], "ffn1_down_mix_in")
    (attn, lse), got = _attn_fwd(u, qg2, kg2, B, S, carry=comm.gathers["ffn2"] if comm else None)
    if comm:
        W = dict(W, **comm.gathered("ffn2", got))
    conv, y = _conv_fwd(u, cw, norms["conv_b"], norms["conv_ln_g"], norms["conv_ln_b"], B, S)
    h2, n3, g2, u2, dout, dyb, sq, act2 = _mix_out_ffn_loss(h1, attn, conv, W["wout"], norms["ffn2_norm"],
                                                            W["wg2"], W["wu2"], W["wd2"], target, "ffn2_fwd")
    loss = (0.5 / D) * jnp.sum(sq)

    (dg2, du2, dh2, dgn_ffn2), _ = _ffn_bwd_act(dyb, g2, u2, h2, dout, norms["ffn2_norm"],
                                               W["wg2"], W["wu2"], W["wd2"], "ffn2_bwd_act")
    dwd2, _ = _ffn_bwd_w(act2, dyb, "ffn2_bwd_wd")
    dwg2, _ = _ffn_bwd_w(dg2, n3, "ffn2_bwd_wg")
    dwu2, _ = _ffn_bwd_w(du2, n3, "ffn2_bwd_wu")
    dattn, dconv, dwout = _mix_out_bwd(dh2, attn, conv, W["wout"])
    carry = comm.reduce_start("ffn2", {"wg2": dwg2, "wu2": dwu2, "wd2": dwd2, "wout": dwout}) if comm else None
    (dq, dk, dv, dgn_qk), got = _attn_bwd(u, attn, dattn, lse, qg2, kg2, B, S, carry=carry)
    if comm:
        comm.reduce_done(carry, got)
    dca, dcg, dcw, dcs = _conv_bwd(u, y, dconv, cw, norms["conv_ln_g"], norms["conv_ln_b"], B, S)
    dwin, dh1, dyb1, dgn_mix = _mix_in_bwd((dq, dk, dv, dca, dcg), W["win"], n2, h1, dh2, norms["mix_norm"])
    (dg1, du1, gx, dgn_ffn1), _ = _ffn_bwd_act(dyb1, g1, u1, x, dh1, norms["ffn1_norm"],
                                              W["wg1"], W["wu1"], W["wd1"], "ffn1_bwd_act")
    carry = comm.reduce_start("win", {"win": dwin}) if comm else None
    dwd1, got = _ffn_bwd_w(act1, dyb1, "ffn1_bwd_wd", carry=carry)
    if comm:
        comm.reduce_done(carry, got)
        carry = comm.reduce_start("wd1", {"wd1": dwd1})
    dwg1, got = _ffn_bwd_w(dg1, n1, "ffn1_bwd_wg", carry=carry)
    if comm:
        comm.reduce_done(carry, got)
        carry = comm.reduce_start("wg1", {"wg1": dwg1})
    dwu1, got = _ffn_bwd_w(du1, n1, "ffn1_bwd_wu", carry=carry)
    if comm:
        comm.reduce_done(carry, got)
        comm.last = comm.reduce_start("wu1", {"wu1": dwu1})

    qk = jnp.sum(dgn_qk, axis=0)
    cs = jnp.sum(dcs, axis=0)
    small = {
        "ffn1_norm": jnp.sum(dgn_ffn1, axis=0),
        "mix_norm": jnp.sum(dgn_mix, axis=0),
        "q_norm": qk[0:1, 0:HD] + qk[0:1, HD:2 * HD],
        "k_norm": qk[1:2, 0:HD] + qk[1:2, HD:2 * HD],
        "conv_w": jnp.sum(dcw, axis=0)[0:CK],
        "conv_b": cs[0:1],
        "conv_ln_g": cs[1:2],
        "conv_ln_b": cs[2:3],
        "ffn2_norm": jnp.sum(dgn_ffn2, axis=0),
    }
    big = {"wg1": dwg1, "wu1": dwu1, "wd1": dwd1, "win": dwin, "wout": dwout,
           "wg2": dwg2, "wu2": dwu2, "wd2": dwd2}
    return loss, gx, big, small


HBM = pl.BlockSpec(memory_space=pltpu.HBM)
VMEM = pl.BlockSpec(memory_space=pltpu.VMEM)


def _place():
    return lax.axis_index("x"), lax.axis_index("y"), lax.axis_index("c")


class _GatherCarry:
    def __init__(self, shards, mid_at=0.5):
        nt = len(shards)
        self.mid_at = mid_at
        self.shards = shards
        self.in_arrays = [s for s, _ in shards]
        self.in_specs = [VMEM] * nt
        self.out_shape = [jax.ShapeDtypeStruct((NDEV * s.shape[0], s.shape[1]), dt) for s, dt in shards]
        self.out_specs = [HBM] * nt
        self.scratch = ([pltpu.VMEM(s.shape, dt) for s, dt in shards]
                        + [pltpu.SemaphoreType.DMA((nt, 7)), pltpu.SemaphoreType.DMA((nt, 7)),
                           pltpu.SemaphoreType.DMA((nt,))])

    def _copies(self, outs, scr):
        nt = len(self.shards)
        stages = scr[:nt]
        send_sems, recv_sems, local_sems = scr[nt:]
        x, y, c = _place()
        me, sibling = (x, y, c), (x, y, 1 - c)
        xn, yn, diag = (1 - x, y, c), (x, 1 - y, c), (1 - x, 1 - y, c)
        via = (x ^ c, y ^ (1 - c), c)
        onto = (x ^ (1 - c), y ^ c, c)

        def rows(t, px, py, pc):
            r = self.shards[t][0].shape[0]
            return outs[t].at[pl.ds((4 * px + 2 * py + pc) * r, r), :]

        def copy(t, k, block, to, src=None):
            return pltpu.make_async_remote_copy(
                src_ref=rows(t, *block) if src is None else src, dst_ref=rows(t, *block),
                send_sem=send_sems.at[t, k], recv_sem=recv_sems.at[t, k],
                device_id=to, device_id_type=MESH)

        sib = lambda b: (b[0], b[1], 1 - c)
        return dict(
            local=[pltpu.make_async_copy(stages[t], rows(t, *me), local_sems.at[t]) for t in range(nt)],
            own=[[copy(t, 0, me, sibling, src=stages[t]), copy(t, 1, me, xn, src=stages[t]),
                  copy(t, 2, me, yn, src=stages[t])] for t in range(nt)],
            relay=[copy(t, 3, via, onto) for t in range(nt)],
            down=[[copy(t, 4, xn, sibling), copy(t, 5, yn, sibling)] for t in range(nt)],
            down_diag=[copy(t, 6, diag, sibling) for t in range(nt)],
            got_xy=[[copy(t, 1, xn, me), copy(t, 2, yn, me)] for t in range(nt)],
            got_diag=[copy(t, 3, diag, me) for t in range(nt)],
            got_sib=[[copy(t, 0, sibling, me), copy(t, 4, sib(xn), me), copy(t, 5, sib(yn), me),
                      copy(t, 6, sib(diag), me)] for t in range(nt)])

    def start(self, ins, outs, scr):
        cps = self._copies(outs, scr)
        for t, (_, dt) in enumerate(self.shards):
            scr[t][...] = ins[t][...].astype(dt)
            for cp in [cps["local"][t]] + cps["own"][t]:
                cp.start()

    def stages(self):
        return [(self.mid_at, self.mid)]

    def mid(self, ins, outs, scr):
        cps = self._copies(outs, scr)
        for t in range(len(self.shards)):
            for cp in cps["got_xy"][t]:
                cp.wait_recv()
            for cp in [cps["relay"][t]] + cps["down"][t]:
                cp.start()

    def finish(self, ins, outs, scr):
        cps = self._copies(outs, scr)
        for t in range(len(self.shards)):
            cps["got_diag"][t].wait_recv()
            cps["down_diag"][t].start()
        for t in range(len(self.shards)):
            for cp in cps["got_sib"][t]:
                cp.wait_recv()
            for cp in cps["own"][t] + [cps["relay"][t]] + cps["down"][t] + [cps["down_diag"][t]]:
                cp.wait_send()
            cps["local"][t].wait()


def _run_carry(carry, name):
    def body(*refs):
        n_in, n_out = len(carry.in_arrays), len(carry.out_shape)
        ins, outs, scr = refs[:n_in], refs[n_in:n_in + n_out], refs[n_in + n_out:]
        carry.start(ins, outs, scr)
        for _, stage in carry.stages():
            stage(ins, outs, scr)
        carry.finish(ins, outs, scr)

    return pl.pallas_call(
        body, in_specs=carry.in_specs, out_specs=carry.out_specs, out_shape=carry.out_shape,
        scratch_shapes=carry.scratch, compiler_params=pltpu.CompilerParams(vmem_limit_bytes=VMEM_LIMIT),
        name=name)(*carry.in_arrays)


class _ExchangeCarry:
    def __init__(self, names, grads, mid_at=0.5):
        nt = len(grads)
        self.mid_at = mid_at
        self.names = names
        self.in_arrays = [g.reshape(4, 2, g.shape[0] // NDEV, g.shape[1]) for g in grads]
        self.in_specs = [HBM] * nt
        blocks = [g.shape[2:] for g in self.in_arrays]
        self.out_shape = [jax.ShapeDtypeStruct((3,) + b, BF16) for b in blocks]
        self.out_specs = [HBM] * nt
        self.scratch = ([pltpu.VMEM((4,) + b, BF16) for b in blocks] * 2 + [pltpu.VMEM(b, BF16) for b in blocks]
                        + [pltpu.SemaphoreType.DMA((nt, 3)), pltpu.SemaphoreType.DMA((nt, 3))]
                        + [pltpu.SemaphoreType.DMA((nt,))] * 4)

    def _copies(self, ins, outs, scr):
        nt = len(ins)
        theirs, own, relayed = scr[:nt], scr[nt:2 * nt], scr[2 * nt:3 * nt]
        send_sems, recv_sems, keep_sems, load_sems, swap_send, swap_recv = scr[3 * nt:]
        x, y, c = _place()
        q = lambda cx, cy: 2 * cx + cy
        near, far = (x ^ c, y ^ (1 - c)), (x ^ (1 - c), y ^ c)

        def remote(t, k, src, dst, chip):
            return pltpu.make_async_remote_copy(
                src_ref=src, dst_ref=dst, send_sem=send_sems.at[t, k], recv_sem=recv_sems.at[t, k],
                device_id=(*chip, c), device_id_type=MESH)

        return dict(
            swap=[pltpu.make_async_remote_copy(
                src_ref=ins[t].at[:, 1 - c], dst_ref=theirs[t], send_sem=swap_send.at[t], recv_sem=swap_recv.at[t],
                device_id=(x, y, 1 - c), device_id_type=MESH) for t in range(nt)],
            load=[pltpu.make_async_copy(ins[t].at[:, c], own[t], load_sems.at[t]) for t in range(nt)],
            keep=[pltpu.make_async_copy(own[t].at[q(x, y)], outs[t].at[0], keep_sems.at[t]) for t in range(nt)],
            direct=[remote(t, 0, own[t].at[q(*near)], outs[t].at[1], near) for t in range(nt)],
            relay=[remote(t, 1, own[t].at[q(1 - x, 1 - y)], relayed[t], near) for t in range(nt)],
            merged=[remote(t, 2, own[t].at[q(*far)], outs[t].at[2], far) for t in range(nt)],
            theirs=theirs, own=own, relayed=relayed, far=q(*far))

    EARLY_AT = 0.1

    def stages(self):
        return [(self.EARLY_AT, self.early), (self.mid_at, self.mid)]

    def start(self, ins, outs, scr):
        cps = self._copies(ins, outs, scr)
        for t in range(len(ins)):
            cps["swap"][t].start()
            cps["load"][t].start()

    def early(self, ins, outs, scr):
        cps = self._copies(ins, outs, scr)
        for t in range(len(ins)):
            cps["load"][t].wait()
            cps["swap"][t].wait_recv()
            own, theirs = cps["own"][t], cps["theirs"][t]
            for j in range(4):
                own[j] = (own[j].astype(F32) + theirs[j].astype(F32)).astype(BF16)
            for kind in ("keep", "direct", "relay"):
                cps[kind][t].start()

    def mid(self, ins, outs, scr):
        cps = self._copies(ins, outs, scr)
        for t in range(len(ins)):
            cps["relay"][t].wait_recv()
            own, far = cps["own"][t], cps["far"]
            own[far] = (own[far].astype(F32) + cps["relayed"][t][...].astype(F32)).astype(BF16)
            cps["merged"][t].start()

    def finish(self, ins, outs, scr):
        cps = self._copies(ins, outs, scr)
        for t in range(len(ins)):
            cps["swap"][t].wait_send()
            cps["direct"][t].wait()
            cps["relay"][t].wait_send()
            cps["merged"][t].wait()
            cps["keep"][t].wait()


class _Comm:
    def __init__(self, groups):
        self.names = {tag: list(g) for tag, (g, _) in groups.items()}
        self.gathers = {tag: _GatherCarry(list(g.values()), mid_at) for tag, (g, mid_at) in groups.items()}
        self.reduced = {}

    def gathered(self, tag, outs):
        return dict(zip(self.names[tag], outs))

    def reduce_start(self, tag, grads, mid_at=0.5):
        names = list(grads)
        return _ExchangeCarry(names, [grads[n] for n in names], mid_at)

    def reduce_done(self, carry, outs):
        self.reduced.update(zip(carry.names, outs))


def _adamw_math(w, g, m, v):
    m = B1 * m + (1.0 - B1) * g
    v = B2 * v + (1.0 - B2) * (g * g)
    m_hat = m / (1.0 - B1 ** STEP)
    v_hat = v / (1.0 - B2 ** STEP)
    delta = -LR * (m_hat / (jnp.sqrt(v_hat) + AEPS) + WD * w)
    return delta, m, v


def _adamw_big(recv, w, m, v, name):
    def body(r_ref, w_ref, m_ref, v_ref, g_ref, d_ref, mo_ref, vo_ref):
        g = r_ref[0].astype(F32)
        for q in range(1, 3):
            g = g + r_ref[q].astype(F32)
        d, mn, vn = _adamw_math(w_ref[...], g, m_ref[...], v_ref[...])
        g_ref[...] = g
        d_ref[...] = d
        mo_ref[...] = mn
        vo_ref[...] = vn

    rows, n = w.shape
    tr = rows // 2
    row = pl.BlockSpec((tr, n), lambda t: (t, 0))
    return pl.pallas_call(
        body, grid=(2,), in_specs=[pl.BlockSpec((3, tr, n), lambda t: (0, t, 0)), row, row, row],
        out_specs=[row] * 4, out_shape=[jax.ShapeDtypeStruct(w.shape, F32)] * 4,
        compiler_params=_cp(("parallel",)), name=name)(recv, w, m, v)


SMALL_NAMES = ("ffn1_norm", "mix_norm", "ffn2_norm", "conv_b", "conv_ln_g", "conv_ln_b", "q_norm", "k_norm")
SROWS = 16
LOSS_ROW = len(SMALL_NAMES)
CWF = 2


def _small_step(gs, loss_row, gcw, ws, ms, vs, wcw, mcw, vcw, carry=None):
    ns = len(SMALL_NAMES)
    widths = [g.shape[1] for g in gs]

    def body(*refs):
        it = iter(refs)
        take = lambda n: [next(it) for _ in range(n)]
        g_refs, (loss_ref, gcw_ref) = take(ns), take(2)
        w_refs, m_refs, v_refs = take(ns), take(ns), take(ns)
        wcw_ref, mcw_ref, vcw_ref = take(3)
        cins = take(len(carry.in_arrays)) if carry else []
        outs = [take(4) for _ in range(ns)]
        cw_outs, (loss_out,) = take(4), take(1)
        couts = take(len(carry.out_shape)) if carry else []
        send, slots, cslots, send_sems, recv_sems, csend_sems, crecv_sems = take(7)
        cscr = list(it)
        x, y, c = _place()
        me = 4 * x + 2 * y + c
        send[...] = jnp.zeros_like(send)
        for k in range(ns):
            send[k:k + 1, 0:widths[k]] = g_refs[k][...]
        send[LOSS_ROW:LOSS_ROW + 1, 0:128] = loss_ref[...]
        slots[me] = send[...]
        cslots[me] = gcw_ref[...]
        cps = []
        for k in range(1, NDEV):
            peer = (x ^ ((k >> 2) & 1), y ^ ((k >> 1) & 1), c ^ (k & 1))
            cps.append(pltpu.make_async_remote_copy(
                src_ref=send, dst_ref=slots.at[me], send_sem=send_sems.at[k - 1], recv_sem=recv_sems.at[k - 1],
                device_id=peer, device_id_type=MESH))
            cps.append(pltpu.make_async_remote_copy(
                src_ref=gcw_ref, dst_ref=cslots.at[me], send_sem=csend_sems.at[k - 1],
                recv_sem=crecv_sems.at[k - 1], device_id=peer, device_id_type=MESH))
        for cp in cps:
            cp.start()
        if carry:
            carry.start(cins, couts, cscr)
        for cp in cps:
            cp.wait()
        if carry:
            for _, stage in carry.stages():
                stage(cins, couts, cscr)
        tot = slots[0]
        ctot = cslots[0, me]
        for j in range(1, NDEV):
            tot = tot + slots[j]
            ctot = ctot + cslots[j, me]

        def step(g, w_ref, m_ref, v_ref, o):
            d, mn, vn = _adamw_math(w_ref[...], g, m_ref[...], v_ref[...])
            o[0][...], o[1][...], o[2][...], o[3][...] = g, d, mn, vn

        for k in range(ns):
            step(tot[k:k + 1, 0:widths[k]], w_refs[k], m_refs[k], v_refs[k], outs[k])
        step(ctot, wcw_ref, mcw_ref, vcw_ref, cw_outs)
        loss_out[...] = tot[LOSS_ROW:LOSS_ROW + 1, 0:128]
        if carry:
            carry.finish(cins, couts, cscr)

    args = [*gs, loss_row, gcw, *ws, *ms, *vs, wcw, mcw, vcw]
    out_shape = ([jax.ShapeDtypeStruct((1, n), F32) for n in widths for _ in range(4)]
                 + [jax.ShapeDtypeStruct((CWF, D), F32)] * 4 + [jax.ShapeDtypeStruct((1, 128), F32)])
    n_own = len(out_shape)
    res = pl.pallas_call(
        body, in_specs=[VMEM] * len(args) + (carry.in_specs if carry else []),
        out_specs=[VMEM] * n_own + (carry.out_specs if carry else []),
        out_shape=out_shape + (carry.out_shape if carry else []),
        scratch_shapes=[pltpu.VMEM((SROWS, D), F32), pltpu.VMEM((NDEV, SROWS, D), F32),
                        pltpu.VMEM((NDEV, NDEV, CWF, D), F32)]
                       + [pltpu.SemaphoreType.DMA((NDEV - 1,))] * 4 + (carry.scratch if carry else []),
        name="small_step")(*args, *(carry.in_arrays if carry else []))
    per = [res[4 * k:4 * k + 4] for k in range(ns)]
    return per, res[4 * ns:4 * ns + 4], res[n_own - 1], res[n_own:]


def _pack_cw(a):
    flat = a.reshape(a.shape[:-2] + (CK * HD,))
    pad = [(0, 0)] * (flat.ndim - 1) + [(0, CWF * D - CK * HD)]
    return jnp.pad(flat, pad).reshape(a.shape[:-2] + (CWF, D))


def _unpack_cw(v):
    return v.reshape(-1)[:CK * HD].reshape(1, CK, HD)


def kernel(x, ffn1_norm, ffn1_w_gate, ffn1_w_up, ffn1_w_down, mix_norm, w_in, q_norm, k_norm, conv_w, conv_b, conv_ln_g, conv_ln_b, w_out, ffn2_norm, ffn2_w_gate, ffn2_w_up, ffn2_w_down, loss_target, m_ffn1_norm, m_ffn1_w_gate, m_ffn1_w_up, m_ffn1_w_down, m_mix_norm, m_w_in, m_q_norm, m_k_norm, m_conv_w, m_conv_b, m_conv_ln_g, m_conv_ln_b, m_w_out, m_ffn2_norm, m_ffn2_w_gate, m_ffn2_w_up, m_ffn2_w_down, v_ffn1_norm, v_ffn1_w_gate, v_ffn1_w_up, v_ffn1_w_down, v_mix_norm, v_w_in, v_q_norm, v_k_norm, v_conv_w, v_conv_b, v_conv_ln_g, v_conv_ln_b, v_w_out, v_ffn2_norm, v_ffn2_w_gate, v_ffn2_w_up, v_ffn2_w_down):
    P = dict(ffn1_norm=ffn1_norm, ffn1_w_gate=ffn1_w_gate, ffn1_w_up=ffn1_w_up, ffn1_w_down=ffn1_w_down,
             mix_norm=mix_norm, w_in=w_in, q_norm=q_norm, k_norm=k_norm, conv_w=conv_w, conv_b=conv_b,
             conv_ln_g=conv_ln_g, conv_ln_b=conv_ln_b, w_out=w_out, ffn2_norm=ffn2_norm,
             ffn2_w_gate=ffn2_w_gate, ffn2_w_up=ffn2_w_up, ffn2_w_down=ffn2_w_down)
    M = dict(ffn1_norm=m_ffn1_norm, ffn1_w_gate=m_ffn1_w_gate, ffn1_w_up=m_ffn1_w_up, ffn1_w_down=m_ffn1_w_down,
             mix_norm=m_mix_norm, w_in=m_w_in, q_norm=m_q_norm, k_norm=m_k_norm, conv_w=m_conv_w, conv_b=m_conv_b,
             conv_ln_g=m_conv_ln_g, conv_ln_b=m_conv_ln_b, w_out=m_w_out, ffn2_norm=m_ffn2_norm,
             ffn2_w_gate=m_ffn2_w_gate, ffn2_w_up=m_ffn2_w_up, ffn2_w_down=m_ffn2_w_down)
    V = dict(ffn1_norm=v_ffn1_norm, ffn1_w_gate=v_ffn1_w_gate, ffn1_w_up=v_ffn1_w_up, ffn1_w_down=v_ffn1_w_down,
             mix_norm=v_mix_norm, w_in=v_w_in, q_norm=v_q_norm, k_norm=v_k_norm, conv_w=v_conv_w, conv_b=v_conv_b,
             conv_ln_g=v_conv_ln_g, conv_ln_b=v_conv_ln_b, w_out=v_w_out, ffn2_norm=v_ffn2_norm,
             ffn2_w_gate=v_ffn2_w_gate, ffn2_w_up=v_ffn2_w_up, ffn2_w_down=v_ffn2_w_down)
    order = ["ffn1_norm", "ffn1_w_gate", "ffn1_w_up", "ffn1_w_down", "mix_norm", "w_in", "q_norm", "k_norm",
             "conv_w", "conv_b", "conv_ln_g", "conv_ln_b", "w_out", "ffn2_norm", "ffn2_w_gate", "ffn2_w_up",
             "ffn2_w_down"]
    B, S, _ = x.shape
    T = B * S

    bigs = [("wg1", "ffn1_w_gate", True), ("wu1", "ffn1_w_up", True), ("wd1", "ffn1_w_down", False),
            ("win", "w_in", True), ("wout", "w_out", False),
            ("wg2", "ffn2_w_gate", True), ("wu2", "ffn2_w_up", True), ("wd2", "ffn2_w_down", False)]
    hm = lambda a, tr: jnp.transpose(a[0]) if tr else a[0]
    cw_pad = jnp.zeros((32, 128), F32).at[0:CK, 0:HD].set(conv_w[0])
    shard = {ln: (hm(P[pn], tr), BF16) for ln, pn, tr in bigs}
    gathered = _run_carry(_GatherCarry([shard["wg1"], shard["wu1"], (cw_pad, F32)]), "gather_first")
    W = {"wg1": gathered[0], "wu1": gathered[1]}
    cwg = gathered[2].reshape(NDEV, 32, 128)[:, 0:CK, 0:HD]
    W["conv_w"] = jnp.transpose(cwg, (1, 0, 2)).reshape(CK, DC)
    norms = {n: P[n] for n in SMALL_NAMES}
    comm = _Comm({"down_in": ({n: shard[n] for n in ("wd1", "win")}, 0.5),
                  "ffn2": ({n: shard[n] for n in ("wg2", "wu2", "wd2", "wout")}, 0.5)})

    loss_part, gx, _, small = _local_step(x.reshape(T, D), loss_target.reshape(T, D), norms, W, B, S, comm)

    G, Dl, Mn, Vn = {}, {}, {}, {}
    dcw = small["conv_w"].reshape(CK, NDEV, HD).transpose(1, 0, 2)
    loss_row = jnp.zeros((1, 128), F32).at[0, 0].set(loss_part)
    per, cw_outs, loss_out, got = _small_step(
        [small[n] for n in SMALL_NAMES], loss_row, _pack_cw(dcw),
        [P[n] for n in SMALL_NAMES], [M[n] for n in SMALL_NAMES], [V[n] for n in SMALL_NAMES],
        _pack_cw(P["conv_w"][0]), _pack_cw(M["conv_w"][0]), _pack_cw(V["conv_w"][0]), carry=comm.last)
    comm.reduce_done(comm.last, got)
    loss = loss_out[0, 0]
    for n, outs in zip(SMALL_NAMES, per):
        G[n], Dl[n], Mn[n], Vn[n] = outs
    G["conv_w"], Dl["conv_w"], Mn["conv_w"], Vn["conv_w"] = [_unpack_cw(o) for o in cw_outs]

    for ln, pn, tr in bigs:
        outs = _adamw_big(comm.reduced[ln], hm(P[pn], tr), hm(M[pn], tr), hm(V[pn], tr), "adamw_" + ln)
        G[pn], Dl[pn], Mn[pn], Vn[pn] = [(jnp.transpose(o) if tr else o)[None] for o in outs]

    return (loss, gx.reshape(B, S, D), *[G[n] for n in order], *[Dl[n] for n in order],
            *[Mn[n] for n in order], *[Vn[n] for n in order])
```

```python
import functools

import jax
import jax.numpy as jnp
from jax import lax
from jax.experimental import pallas as pl
from jax.experimental.pallas import tpu as pltpu

F32 = jnp.float32
BF16 = jnp.bfloat16

D = 1024
FF = 2816
HD = 64
DA = 512
DC = 512
DIN = 2560
CK = 31
BLK = 128
DILS = (1, 4, 16)
EPS = 1e-6
NDEV = 8
MESH = pl.DeviceIdType.MESH

LR, B1, B2, AEPS, WD, STEP = 0.001, 0.9, 0.999, 1e-08, 0.01, 10

NT = (((1,), (1,)), ((), ()))
TN = (((0,), (0,)), ((), ()))

VMEM_LIMIT = 60 * 1024 * 1024


def _cp(sem=None):
    return pltpu.CompilerParams(dimension_semantics=sem, vmem_limit_bytes=VMEM_LIMIT)


def _sigmoid(x):
    return 0.5 * (jnp.tanh(0.5 * x) + 1.0)


def _pallas(body, args, *, grid, in_specs, out_specs, out_shape, scratch_shapes, sem, name, carry=None):
    if carry is None:
        outs = pl.pallas_call(body, grid=grid, in_specs=in_specs, out_specs=out_specs, out_shape=out_shape,
                              scratch_shapes=scratch_shapes, compiler_params=_cp(sem), name=name)(*args)
        return outs, None
    n_in, n_out, n_scr = len(in_specs), len(out_shape), len(scratch_shapes)
    c_in, c_out = len(carry.in_arrays), len(carry.out_shape)

    def wrapped(*refs):
        ins, refs = refs[:n_in], refs[n_in:]
        cins, refs = refs[:c_in], refs[c_in:]
        outs, refs = refs[:n_out], refs[n_out:]
        couts, refs = refs[:c_out], refs[c_out:]
        scr, cscr = refs[:n_scr], refs[n_scr:]
        ids = [pl.program_id(a) for a in range(len(grid))]
        step = ids[0]
        for i, n in zip(ids[1:], grid[1:]):
            step = step * n + i
        steps = functools.reduce(lambda a, b: a * b, grid)

        @pl.when(step == 0)
        def _():
            carry.start(cins, couts, cscr)

        body(*ins, *outs, *scr)

        for frac, stage in carry.stages():
            @pl.when(step == int(steps * frac))
            def _(stage=stage):
                stage(cins, couts, cscr)

        @pl.when(step == steps - 1)
        def _():
            carry.finish(cins, couts, cscr)

    outs = pl.pallas_call(
        wrapped, grid=grid, in_specs=list(in_specs) + carry.in_specs, out_specs=list(out_specs) + carry.out_specs,
        out_shape=list(out_shape) + carry.out_shape, scratch_shapes=list(scratch_shapes) + carry.scratch,
        compiler_params=_cp(("arbitrary",) * len(grid)), name=name)(*args, *carry.in_arrays)
    return outs[:n_out], outs[n_out:]


FC = 256


def _resident(shape):
    return pl.BlockSpec(shape, lambda *_: (0,) * len(shape), pipeline_mode=pl.Buffered(1))


def _mix_out_ffn_loss(h1, attn, conv, wout, gain, wg, wu, wd, target, name):
    T = h1.shape[0]
    tm = 512
    nt = T // tm

    def body(h1_ref, at_ref, cv_ref, wo_ref, gain_ref, wg_ref, wu_ref, wd_ref, t_ref,
             h2_ref, n_ref, g_ref, u_ref, dout_ref, dyb_ref, sq_ref, a_hbm, a_scr, a_sem):
        t = pl.program_id(0)
        a_out = lambda i: pltpu.make_async_copy(a_scr, a_hbm.at[pl.ds(pl.multiple_of(i * tm, tm), tm), :], a_sem)

        @pl.when(t > 0)
        def _():
            a_out(t - 1).wait()

        xv = (h1_ref[...]
              + jnp.dot(at_ref[...], wo_ref[0:DA, :], preferred_element_type=F32)
              + jnp.dot(cv_ref[...], wo_ref[DA:D, :], preferred_element_type=F32))
        h2_ref[...] = xv
        r = lax.rsqrt(jnp.mean(xv * xv, axis=-1, keepdims=True) + EPS)
        n_ref[...] = (xv * r * gain_ref[...]).astype(BF16)
        for c in range(FF // FC):
            cols = slice(c * FC, (c + 1) * FC)
            nb = n_ref[...]
            g = lax.dot_general(nb, wg_ref[cols, :], NT, preferred_element_type=F32)
            u = lax.dot_general(nb, wu_ref[cols, :], NT, preferred_element_type=F32)
            g_ref[:, cols] = g.astype(BF16)
            u_ref[:, cols] = u.astype(BF16)
            a_scr[:, cols] = (g * _sigmoid(g) * u).astype(BF16)
        a_out(t).start()
        e = h2_ref[...] + 0.5 * jnp.dot(a_scr[...], wd_ref[...], preferred_element_type=F32) - t_ref[...]
        dout = e * (1.0 / D)
        dout_ref[...] = dout
        dyb_ref[...] = (0.5 * dout).astype(BF16)
        sq_ref[...] = jnp.sum(e * e, axis=0, keepdims=True)[None]

        @pl.when(t == nt - 1)
        def _():
            a_out(t).wait()

    row = pl.BlockSpec((tm, D), lambda t: (t, 0))
    half = pl.BlockSpec((tm, DA), lambda t: (t, 0))
    wide = pl.BlockSpec((tm, FF), lambda t: (t, 0))
    outs, _ = _pallas(
        body, (h1, attn, conv, wout, gain, wg, wu, wd, target), grid=(nt,),
        in_specs=[row, half, half, _resident((D, D)), _resident((1, D)), _resident((FF, D)), _resident((FF, D)),
                  _resident((FF, D)), row],
        out_specs=[row, row, wide, wide, row, row, pl.BlockSpec((1, 1, D), lambda t: (t, 0, 0)), HBM],
        out_shape=[jax.ShapeDtypeStruct((T, D), F32), jax.ShapeDtypeStruct((T, D), BF16)]
                  + [jax.ShapeDtypeStruct((T, FF), BF16)] * 2
                  + [jax.ShapeDtypeStruct((T, D), F32), jax.ShapeDtypeStruct((T, D), BF16),
                     jax.ShapeDtypeStruct((nt, 1, D), F32), jax.ShapeDtypeStruct((T, FF), BF16)],
        scratch_shapes=[pltpu.VMEM((tm, FF), BF16), pltpu.SemaphoreType.DMA(())],
        sem=("arbitrary",), name=name)
    return outs


def _ffn_gate_up(x, gain, wg, wu, name, carry=None):
    T = x.shape[0]
    tm = 512

    def body(x_ref, gain_ref, wg_ref, wu_ref, n_ref, g_ref, u_ref, a_ref):
        xv = x_ref[...]
        r = lax.rsqrt(jnp.mean(xv * xv, axis=-1, keepdims=True) + EPS)
        n_ref[...] = (xv * r * gain_ref[...]).astype(BF16)
        for c in range(FF // FC):
            cols = slice(c * FC, (c + 1) * FC)
            nb = n_ref[...]
            g = lax.dot_general(nb, wg_ref[cols, :], NT, preferred_element_type=F32)
            u = lax.dot_general(nb, wu_ref[cols, :], NT, preferred_element_type=F32)
            g_ref[:, cols] = g.astype(BF16)
            u_ref[:, cols] = u.astype(BF16)
            a_ref[:, cols] = (g * _sigmoid(g) * u).astype(BF16)

    row = pl.BlockSpec((tm, D), lambda t: (t, 0))
    wide = pl.BlockSpec((tm, FF), lambda t: (t, 0))
    return _pallas(
        body, (x, gain, wg, wu), grid=(T // tm,),
        in_specs=[row, _resident((1, D)), _resident((FF, D)), _resident((FF, D))],
        out_specs=[row, wide, wide, wide],
        out_shape=[jax.ShapeDtypeStruct((T, D), BF16)] + [jax.ShapeDtypeStruct((T, FF), BF16)] * 3,
        scratch_shapes=[], sem=("parallel",), name=name, carry=carry)


def _ffn_down_mix_in(x, a, wd, gain, win, name):
    T = x.shape[0]
    tm = 512

    def body(x_ref, a_ref, wd_ref, gain_ref, win_ref, h_ref, u_ref, n_ref):
        hv = x_ref[...] + 0.5 * jnp.dot(a_ref[...], wd_ref[...], preferred_element_type=F32)
        h_ref[...] = hv
        r = lax.rsqrt(jnp.mean(hv * hv, axis=-1, keepdims=True) + EPS)
        n_ref[...] = (hv * r * gain_ref[...]).astype(BF16)
        u_ref[...] = lax.dot_general(n_ref[...], win_ref[...], NT, preferred_element_type=F32).astype(BF16)

    row = pl.BlockSpec((tm, D), lambda t: (t, 0))
    wide = pl.BlockSpec((tm, FF), lambda t: (t, 0))
    outs, _ = _pallas(
        body, (x, a, wd, gain, win), grid=(T // tm,),
        in_specs=[row, wide, _resident((FF, D)), _resident((1, D)), _resident((DIN, D))],
        out_specs=[row, pl.BlockSpec((tm, DIN), lambda t: (t, 0)), row],
        out_shape=[jax.ShapeDtypeStruct((T, D), F32), jax.ShapeDtypeStruct((T, DIN), BF16),
                   jax.ShapeDtypeStruct((T, D), BF16)],
        scratch_shapes=[], sem=("parallel",), name=name)
    return outs


def _ffn_bwd_act(dyb, g, u, x, dout, gain, wg, wu, wd, name, carry=None):
    T = x.shape[0]
    tm = 256
    nt = T // tm

    def body(dy_ref, g_ref, u_ref, x_ref, dout_ref, gain_ref, wg_ref, wu_ref, wd_ref,
             dg_ref, du_ref, dx_ref, dgn_ref):
        for c in range(FF // FC):
            cols = slice(c * FC, (c + 1) * FC)
            da = lax.dot_general(dy_ref[...], wd_ref[cols, :], NT, preferred_element_type=F32)
            gv = g_ref[:, cols].astype(F32)
            uv = u_ref[:, cols].astype(F32)
            sg = _sigmoid(gv)
            dg_ref[:, cols] = (da * uv * (sg * (1.0 + gv * (1.0 - sg)))).astype(BF16)
            du_ref[:, cols] = (da * (gv * sg)).astype(BF16)
        dn = (jnp.dot(dg_ref[...], wg_ref[...], preferred_element_type=F32)
              + jnp.dot(du_ref[...], wu_ref[...], preferred_element_type=F32))
        dx, dgain = _rms_bwd_rows(dn, x_ref[...], gain_ref[...])
        dx_ref[...] = dout_ref[...] + dx
        dgn_ref[...] = dgain[None]

    row = pl.BlockSpec((tm, D), lambda t: (t, 0))
    wide = pl.BlockSpec((tm, FF), lambda t: (t, 0))
    return _pallas(
        body, (dyb, g, u, x, dout, gain, wg, wu, wd), grid=(nt,),
        in_specs=[row, wide, wide, row, row, _resident((1, D)), _resident((FF, D)), _resident((FF, D)),
                  _resident((FF, D))],
        out_specs=[wide, wide, row, pl.BlockSpec((1, 1, D), lambda t: (t, 0, 0))],
        out_shape=[jax.ShapeDtypeStruct((T, FF), BF16)] * 2
                  + [jax.ShapeDtypeStruct((T, D), F32), jax.ShapeDtypeStruct((nt, 1, D), F32)],
        scratch_shapes=[], sem=("parallel",), name=name, carry=carry)


def _ffn_bwd_w(lhs, rhs, name, carry=None):
    T = rhs.shape[0]
    tf = 256

    def body(l_ref, r_ref, dw_ref):
        dw_ref[...] = lax.dot_general(l_ref[...], r_ref[...], TN, preferred_element_type=F32).astype(BF16)

    (dw,), got = _pallas(
        body, (lhs, rhs), grid=(FF // tf,),
        in_specs=[pl.BlockSpec((T, tf), lambda f: (0, f)), _resident((T, D))],
        out_specs=[pl.BlockSpec((tf, D), lambda f: (f, 0))], out_shape=[jax.ShapeDtypeStruct((FF, D), BF16)],
        scratch_shapes=[], sem=("parallel",), name=name, carry=carry)
    return dw, got


def _rms_bwd_rows(dn, xv, gain):
    r = lax.rsqrt(jnp.mean(xv * xv, axis=-1, keepdims=True) + EPS)
    xhat = xv * r
    dxhat = dn * gain
    dx = r * (dxhat - xhat * jnp.mean(dxhat * xhat, axis=-1, keepdims=True))
    return dx, jnp.sum(dn * xhat, axis=0, keepdims=True)


def _mix_out_bwd(dh, attn, conv, wout):
    T = dh.shape[0]
    tm = 512
    nt = T // tm

    def body(dh_ref, a_ref, c_ref, w_ref, da_ref, dc_ref, dw_ref, acc_scr):
        t = pl.program_id(0)

        @pl.when(t == 0)
        def _():
            acc_scr[...] = jnp.zeros_like(acc_scr)

        dhb = dh_ref[...].astype(BF16)
        dmix = lax.dot_general(dhb, w_ref[...], NT, preferred_element_type=F32)
        da_ref[...] = dmix[:, 0:DA].astype(BF16)
        dc_ref[...] = dmix[:, DA:D].astype(BF16)
        acc_scr[0:DA, :] += lax.dot_general(a_ref[...], dhb, TN, preferred_element_type=F32)
        acc_scr[DA:D, :] += lax.dot_general(c_ref[...], dhb, TN, preferred_element_type=F32)

        @pl.when(t == nt - 1)
        def _():
            dw_ref[...] = acc_scr[...].astype(BF16)

    row = pl.BlockSpec((tm, D), lambda t: (t, 0))
    half = pl.BlockSpec((tm, DA), lambda t: (t, 0))
    full = pl.BlockSpec((D, D), lambda t: (0, 0))
    return pl.pallas_call(
        body, grid=(nt,), in_specs=[row, half, half, full], out_specs=[half, half, full],
        out_shape=[jax.ShapeDtypeStruct((T, DA), BF16)] * 2 + [jax.ShapeDtypeStruct((D, D), BF16)],
        scratch_shapes=[pltpu.VMEM((D, D), F32)],
        compiler_params=_cp(("arbitrary",)), name="mix_out_bwd")(dh, attn, conv, wout)


def _mix_in_bwd(dparts, win, nb, h, dh, gain):
    T = h.shape[0]
    tm = 512
    nt = T // tm

    def body(d0, d1, d2, d3, d4, w_ref, n_ref, h_ref, dh_ref, gain_ref,
             dw_ref, dx_ref, dyb_ref, dg_ref, acc_scr):
        t = pl.program_id(0)

        @pl.when(t == 0)
        def _():
            acc_scr[...] = jnp.zeros_like(acc_scr)

        n = n_ref[...]
        dn = jnp.zeros((tm, D), F32)
        for i, d_ref in enumerate((d0, d1, d2, d3, d4)):
            dv = d_ref[...]
            dn = dn + jnp.dot(dv, w_ref[i * DA:(i + 1) * DA, :], preferred_element_type=F32)
            acc_scr[i * DA:(i + 1) * DA, :] += lax.dot_general(dv, n, TN, preferred_element_type=F32)
        dx, dgain = _rms_bwd_rows(dn, h_ref[...], gain_ref[...])
        tot = dh_ref[...] + dx
        dx_ref[...] = tot
        dyb_ref[...] = (0.5 * tot).astype(BF16)
        dg_ref[...] = dgain[None]

        @pl.when(t == nt - 1)
        def _():
            dw_ref[...] = acc_scr[...].astype(BF16)

    row = pl.BlockSpec((tm, D), lambda t: (t, 0))
    half = pl.BlockSpec((tm, DA), lambda t: (t, 0))
    full = pl.BlockSpec((DIN, D), lambda t: (0, 0))
    return pl.pallas_call(
        body, grid=(nt,),
        in_specs=[half] * 5 + [full, row, row, row, pl.BlockSpec((1, D), lambda t: (0, 0))],
        out_specs=[full, row, row, pl.BlockSpec((1, 1, D), lambda t: (t, 0, 0))],
        out_shape=[jax.ShapeDtypeStruct((DIN, D), BF16), jax.ShapeDtypeStruct((T, D), F32),
                   jax.ShapeDtypeStruct((T, D), BF16), jax.ShapeDtypeStruct((nt, 1, D), F32)],
        scratch_shapes=[pltpu.VMEM((DIN, D), F32)],
        compiler_params=_cp(("arbitrary",)), name="mix_in_bwd")(*dparts, win, nb, h, dh, gain)


def _head_masks():
    lane = lax.broadcasted_iota(jnp.int32, (1, 2 * HD), 1)
    m0 = lane < HD
    return m0, jnp.logical_not(m0)


def _stack_heads(v, m0, m1):
    z = jnp.zeros_like(v)
    return jnp.concatenate([jnp.where(m0, v, z), jnp.where(m1, v, z)], axis=0)


def _unstack_heads(v2, m0):
    return jnp.where(m0, v2[0:BLK], v2[BLK:2 * BLK])


def _head_sums(xv):
    ri = lax.broadcasted_iota(jnp.int32, (2 * HD, 2 * HD), 0)
    ci = lax.broadcasted_iota(jnp.int32, (2 * HD, 2 * HD), 1)
    ones = jnp.where((ri < HD) == (ci < HD), 1.0, 0.0).astype(BF16)
    hi = xv.astype(BF16)
    lo = (xv - hi.astype(F32)).astype(BF16)
    return (jnp.dot(hi, ones, preferred_element_type=F32) + jnp.dot(lo, ones, preferred_element_type=F32))


def _head_rms(xv):
    return lax.rsqrt(_head_sums(xv * xv) * (1.0 / HD) + EPS)


def _band_mask(first):
    qi = lax.broadcasted_iota(jnp.int32, (BLK, 2 * BLK), 0)
    ci = lax.broadcasted_iota(jnp.int32, (BLK, 2 * BLK), 1)
    band = (ci >= qi) & (ci <= qi + BLK)
    return band & ((ci >= BLK) | jnp.logical_not(first))


def _block_rows(j, d, seg):
    r, n = j // seg, j % seg
    start = r + (d * BLK) * n
    first = n == 0
    prev = jnp.where(first, start, start - d * BLK)
    return pl.ds(start, BLK, stride=d), pl.ds(prev, BLK, stride=d), first


def _block_keys(refs, cur, prev, first, single):
    if single:
        qi = lax.broadcasted_iota(jnp.int32, (BLK, BLK), 0)
        ci = lax.broadcasted_iota(jnp.int32, (BLK, BLK), 1)
        return [r[cur, :].astype(BF16) for r in refs], ci <= qi
    return ([jnp.concatenate([r[prev, :], r[cur, :]], axis=0).astype(BF16) for r in refs], _band_mask(first))


def _attn_fwd(u, qg2, kg2, B, S, carry=None):
    T = B * S
    NB = S // BLK
    scale = HD ** -0.5

    def body(q_ref, k_ref, v_ref, qg_ref, kg_ref, o_ref, lse_ref, qn, kn, vn, os_, ls_):
        m0, m1 = _head_masks()
        qv = q_ref[...].astype(F32)
        qn[...] = qv * _head_rms(qv) * (qg_ref[...] * scale)
        kv = k_ref[...].astype(F32)
        kn[...] = kv * _head_rms(kv) * kg_ref[...]
        vn[...] = v_ref[...].astype(F32)

        for i, d in enumerate(DILS):
            seg = NB // d

            def blk(j, c, i=i, d=d, seg=seg):
                cur, prev, first = _block_rows(j, d, seg)
                q2 = _stack_heads(qn[cur, :].astype(BF16), m0, m1)
                (kk, vv), mask = _block_keys((kn, vn), cur, prev, first, False)
                s = lax.dot_general(q2, kk, NT, preferred_element_type=F32)
                s = jnp.where(jnp.concatenate([mask, mask], axis=0), s, -1e30)
                mx = jnp.max(s, axis=-1, keepdims=True)
                p = jnp.exp(s - mx)
                l = jnp.sum(p, axis=-1, keepdims=True)
                o2 = jnp.dot((p * (1.0 / l)).astype(BF16), vv, preferred_element_type=F32)
                os_[i, cur, :] = _unstack_heads(o2, m0)
                ls_[i, cur, :] = _unstack_heads(mx + jnp.log(l), m0)
                return c

            lax.fori_loop(0, NB, blk, 0, unroll=8)

        def comb(c, carry):
            rows = pl.ds(pl.multiple_of(c * 256, 256), 256)
            l0, l1, l2 = ls_[0, rows, :], ls_[1, rows, :], ls_[2, rows, :]
            mx = jnp.maximum(jnp.maximum(l0, l1), l2)
            e0, e1, e2 = jnp.exp(l0 - mx), jnp.exp(l1 - mx), jnp.exp(l2 - mx)
            tot = e0 + e1 + e2
            inv = 1.0 / tot
            o = (e0 * os_[0, rows, :] + e1 * os_[1, rows, :] + e2 * os_[2, rows, :]) * inv
            o_ref[rows, :] = o.astype(BF16)
            lse_ref[rows, :] = mx + jnp.log(tot)
            return carry

        lax.fori_loop(0, S // 256, comb, 0)

    pair = 2 * HD
    blk_spec = lambda off: pl.BlockSpec((S, pair), lambda b, p, off=off: (b, off + p))
    gspec = pl.BlockSpec((1, pair), lambda b, p: (0, 0))
    return _pallas(
        body, (u, u, u, qg2, kg2), grid=(B, DA // pair),
        in_specs=[blk_spec(0), blk_spec(DA // pair), blk_spec(2 * DA // pair), gspec, gspec],
        out_specs=[blk_spec(0), blk_spec(0)],
        out_shape=[jax.ShapeDtypeStruct((T, DA), BF16), jax.ShapeDtypeStruct((T, DA), F32)],
        scratch_shapes=[pltpu.VMEM((S, pair), F32)] * 3 + [pltpu.VMEM((3, S, pair), F32)] * 2,
        sem=("parallel", "parallel"), name="attn_fwd", carry=carry)


def _attn_bwd(u, attn, dattn, lse, qg2, kg2, B, S, carry=None):
    T = B * S
    NB = S // BLK
    scale = HD ** -0.5
    pair = 2 * HD

    def body(q_ref, k_ref, v_ref, o_ref, do_ref, lse_ref, qg_ref, kg_ref,
             dq_ref, dk_ref, dv_ref, dgn_ref,
             qn, kn, vn, don, ldl, accq, acck, accv, rq, rk):
        m0, m1 = _head_masks()
        lane = lax.broadcasted_iota(jnp.int32, (1, pair), 1)
        qv = q_ref[...].astype(F32)
        rq[...] = _head_rms(qv)
        qn[...] = qv * rq[...] * (qg_ref[...] * scale)
        kv = k_ref[...].astype(F32)
        rk[...] = _head_rms(kv)
        kn[...] = kv * rk[...] * kg_ref[...]
        vn[...] = v_ref[...].astype(F32)
        dov = do_ref[...].astype(F32)
        don[...] = dov
        ldl[...] = jnp.where((lane % HD) < HD // 2, lse_ref[...], _head_sums(dov * o_ref[...].astype(F32)))

        for i, d in enumerate(DILS):
            seg = NB // d

            def blk(j, c, i=i, d=d, seg=seg):
                cur, prev, first = _block_rows(j, d, seg)
                q2 = _stack_heads(qn[cur, :].astype(BF16), m0, m1)
                do2 = _stack_heads(don[cur, :].astype(BF16), m0, m1)
                (kk, vv), mask = _block_keys((kn, vn), cur, prev, first, seg == 1)
                ldv = ldl[cur, :]
                lse2 = jnp.concatenate([ldv[:, 0:1], ldv[:, HD:HD + 1]], axis=0)
                dl2 = jnp.concatenate([ldv[:, HD // 2:HD // 2 + 1], ldv[:, HD + HD // 2:HD + HD // 2 + 1]], axis=0)
                s = lax.dot_general(q2, kk, NT, preferred_element_type=F32)
                p = jnp.where(jnp.concatenate([mask, mask], axis=0), jnp.exp(s - lse2), 0.0)
                dp = lax.dot_general(do2, vv, NT, preferred_element_type=F32)
                ds = (p * (dp - dl2)).astype(BF16)
                dq_acc = _unstack_heads(jnp.dot(ds, kk, preferred_element_type=F32), m0)
                dk_acc = lax.dot_general(ds, q2, TN, preferred_element_type=F32)
                dv_acc = lax.dot_general(p.astype(BF16), do2, TN, preferred_element_type=F32)
                if i == 0:
                    accq[cur, :] = dq_acc
                    acck[cur, :] = dk_acc[BLK:2 * BLK]
                    accv[cur, :] = dv_acc[BLK:2 * BLK]
                    acck[prev, :] += dk_acc[0:BLK]
                    accv[prev, :] += dv_acc[0:BLK]
                elif seg == 1:
                    accq[cur, :] += dq_acc
                    acck[cur, :] += dk_acc
                    accv[cur, :] += dv_acc
                else:
                    accq[cur, :] += dq_acc
                    acck[prev, :] += dk_acc[0:BLK]
                    acck[cur, :] += dk_acc[BLK:2 * BLK]
                    accv[prev, :] += dv_acc[0:BLK]
                    accv[cur, :] += dv_acc[BLK:2 * BLK]
                return c

            lax.fori_loop(0, NB, blk, 0, unroll=8)

        def norm_bwd(x_ref, r_ref, dn, gain):
            r = r_ref[...]
            xhat = x_ref[...].astype(F32) * r
            dxhat = dn * gain
            dx = r * (dxhat - xhat * (_head_sums(dxhat * xhat) * (1.0 / HD)))
            return dx, jnp.sum(dn * xhat, axis=0, keepdims=True)

        dq, dgq = norm_bwd(q_ref, rq, accq[...], qg_ref[...] * scale)
        dk, dgk = norm_bwd(k_ref, rk, acck[...], kg_ref[...])
        dq_ref[...] = dq.astype(BF16)
        dk_ref[...] = dk.astype(BF16)
        dv_ref[...] = accv[...].astype(BF16)
        dgn_ref[...] = jnp.concatenate([dgq * scale, dgk, jnp.zeros((6, pair), F32)], axis=0)[None]

    blk_spec = lambda off: pl.BlockSpec((S, pair), lambda b, p, off=off: (b, off + p))
    gspec = pl.BlockSpec((1, pair), lambda b, p: (0, 0))
    np_ = DA // pair
    return _pallas(
        body, (u, u, u, attn, dattn, lse, qg2, kg2), grid=(B, np_),
        in_specs=[blk_spec(0), blk_spec(np_), blk_spec(2 * np_), blk_spec(0), blk_spec(0), blk_spec(0),
                  gspec, gspec],
        out_specs=[blk_spec(0), blk_spec(0), blk_spec(0),
                   pl.BlockSpec((1, 8, pair), lambda b, p: (b * np_ + p, 0, 0))],
        out_shape=[jax.ShapeDtypeStruct((T, DA), BF16)] * 3 + [jax.ShapeDtypeStruct((B * np_, 8, pair), F32)],
        scratch_shapes=[pltpu.VMEM((S, pair), F32)] * 10,
        sem=("parallel", "parallel"), name="attn_bwd", carry=carry)


CT = 32
CPAD = 32


def _shifted(win, offsets):
    rolled, out = {}, {}
    n = win.shape[0]
    for o in offsets:
        sub = o % 8
        if sub not in rolled:
            rolled[sub] = win if sub == 0 else pltpu.roll(win, n - sub, 0)
        out[o] = rolled[sub][o - sub:o - sub + CT, :]
    return out


def _ln_fwd(y, g, b):
    mu = jnp.mean(y, axis=-1, keepdims=True)
    yc = y - mu
    rstd = lax.rsqrt(jnp.mean(yc * yc, axis=-1, keepdims=True) + EPS)
    xhat = yc * rstd
    return xhat, rstd, xhat * g + b


def _fill_glu(ca_ref, cg_ref, glu, S):
    glu[pl.ds(0, CPAD), :] = jnp.zeros((CPAD, DC), F32)

    def fill(i, c):
        rows = pl.ds(pl.multiple_of(i * 256, 256), 256)
        a = ca_ref[rows, :].astype(F32)
        gt = cg_ref[rows, :].astype(F32)
        glu[pl.ds(pl.multiple_of(CPAD + i * 256, CT), 256), :] = a * _sigmoid(gt)
        return c

    lax.fori_loop(0, S // 256, fill, 0)


def _conv_fwd(u, cw, cb, lg, lb, B, S):
    T = B * S

    def body(ca_ref, cg_ref, w_ref, b_ref, lg_ref, lb_ref, o_ref, y_ref, glu):
        _fill_glu(ca_ref, cg_ref, glu, S)

        def step(i, c):
            t0 = pl.multiple_of(i * CT, CT)
            win = glu[pl.ds(t0, 2 * CT), :]
            acc = jnp.zeros((CT, DC), F32) + b_ref[...]
            taps = _shifted(win, [k + 2 for k in range(CK)])
            for k in range(CK):
                acc = acc + taps[k + 2] * w_ref[k:k + 1, :]
            y_ref[pl.ds(t0, CT), :] = acc
            _, _, z = _ln_fwd(acc, lg_ref[...], lb_ref[...])
            o_ref[pl.ds(t0, CT), :] = (z * _sigmoid(z)).astype(BF16)
            return c

        lax.fori_loop(0, S // CT, step, 0, unroll=4)

    vec = pl.BlockSpec((1, DC), lambda b: (0, 0))
    return pl.pallas_call(
        body, grid=(B,),
        in_specs=[pl.BlockSpec((S, DC), lambda b: (b, 3)), pl.BlockSpec((S, DC), lambda b: (b, 4)),
                  pl.BlockSpec((CT, DC), lambda b: (0, 0)), vec, vec, vec],
        out_specs=[pl.BlockSpec((S, DC), lambda b: (b, 0))] * 2,
        out_shape=[jax.ShapeDtypeStruct((T, DC), BF16), jax.ShapeDtypeStruct((T, DC), F32)],
        scratch_shapes=[pltpu.VMEM((CPAD + S, DC), F32)],
        compiler_params=_cp(("parallel",)), name="conv_fwd")(u, u, cw, cb, lg, lb)


def _conv_bwd(u, y, dconv, cw, lg, lb, B, S):
    T = B * S

    def body(ca_ref, cg_ref, y_ref, dc_ref, w_ref, lg_ref, lb_ref,
             dca_ref, dcg_ref, dw_ref, ds_ref, glu, dyp, dwacc):
        _fill_glu(ca_ref, cg_ref, glu, S)
        dyp[pl.ds(S, CPAD), :] = jnp.zeros((CPAD, DC), F32)
        lgv, lbv = lg_ref[...], lb_ref[...]

        def sum8(v):
            return functools.reduce(jnp.add, [v[r:r + 8] for r in range(0, v.shape[0], 8)])

        P1 = 4 * CT

        def p1(i, carry):
            sb, sg, sl = carry
            t0 = pl.multiple_of(i * P1, P1)
            xhat, rstd, z = _ln_fwd(y_ref[pl.ds(t0, P1), :], lgv, lbv)
            sz = _sigmoid(z)
            dz = dc_ref[pl.ds(t0, P1), :].astype(F32) * (sz * (1.0 + z * (1.0 - sz)))
            dxhat = dz * lgv
            dy = rstd * (dxhat - jnp.mean(dxhat, axis=-1, keepdims=True)
                         - xhat * jnp.mean(dxhat * xhat, axis=-1, keepdims=True))
            dyp[pl.ds(t0, P1), :] = dy
            return sb + sum8(dy), sg + sum8(dz * xhat), sl + sum8(dz)

        z8 = jnp.zeros((8, DC), F32)
        sb, sg, sl = lax.fori_loop(0, S // P1, p1, (z8, z8, z8))
        rs = lambda v: jnp.sum(v, axis=0, keepdims=True)
        ds_ref[...] = jnp.concatenate([rs(sb), rs(sg), rs(sl), jnp.zeros((5, DC), F32)], axis=0)[None]

        def p2(i, c):
            t0 = pl.multiple_of(i * CT, CT)
            win = dyp[pl.ds(t0, 2 * CT), :]
            acc = jnp.zeros((CT, DC), F32)
            taps = _shifted(win, [30 - k for k in range(CK)])
            for k in range(CK):
                acc = acc + taps[30 - k] * w_ref[k:k + 1, :]
            a = ca_ref[pl.ds(t0, CT), :].astype(F32)
            sgt = _sigmoid(cg_ref[pl.ds(t0, CT), :].astype(F32))
            dca_ref[pl.ds(t0, CT), :] = (acc * sgt).astype(BF16)
            dcg_ref[pl.ds(t0, CT), :] = (acc * a * sgt * (1.0 - sgt)).astype(BF16)
            return c

        lax.fori_loop(0, S // CT, p2, 0)

        dwacc[...] = jnp.zeros_like(dwacc)

        def p3(i, c):
            t0 = pl.multiple_of(i * CT, CT)
            win = glu[pl.ds(t0, 2 * CT), :]
            dy = dyp[pl.ds(t0, CT), :]
            for k in range(CK):
                dwacc[k] += sum8(dy * win[k + 2:k + 2 + CT, :])
            return c

        lax.fori_loop(0, S // CT, p3, 0)
        dw_ref[...] = jnp.sum(dwacc[...], axis=1)[None]

    vec = pl.BlockSpec((1, DC), lambda b: (0, 0))
    seq = pl.BlockSpec((S, DC), lambda b: (b, 0))
    return pl.pallas_call(
        body, grid=(B,),
        in_specs=[pl.BlockSpec((S, DC), lambda b: (b, 3)), pl.BlockSpec((S, DC), lambda b: (b, 4)),
                  seq, seq, pl.BlockSpec((CT, DC), lambda b: (0, 0)), vec, vec],
        out_specs=[seq, seq, pl.BlockSpec((1, CT, DC), lambda b: (b, 0, 0)),
                   pl.BlockSpec((1, 8, DC), lambda b: (b, 0, 0))],
        out_shape=[jax.ShapeDtypeStruct((T, DC), BF16)] * 2
                  + [jax.ShapeDtypeStruct((B, CT, DC), F32), jax.ShapeDtypeStruct((B, 8, DC), F32)],
        scratch_shapes=[pltpu.VMEM((CPAD + S, DC), F32), pltpu.VMEM((S + CPAD, DC), F32),
                        pltpu.VMEM((CT, 8, DC), F32)],
        compiler_params=_cp(("parallel",)), name="conv_bwd")(u, u, y, dconv, cw, lg, lb)


def _local_step(x, target, norms, W, B, S, comm=None):
    qg2 = jnp.concatenate([norms["q_norm"], norms["q_norm"]], axis=1)
    kg2 = jnp.concatenate([norms["k_norm"], norms["k_norm"]], axis=1)
    cw = jnp.concatenate([W["conv_w"], jnp.zeros((1, DC), F32)], axis=0)

    W = dict(W)
    (n1, g1, u1, act1), got = _ffn_gate_up(x, norms["ffn1_norm"], W["wg1"], W["wu1"], "ffn1_gate_up",
                                           carry=comm.gathers["down_in"] if comm else None)
    if comm:
        W.update(comm.gathered("down_in", got))
    h1, u, n2 = _ffn_down_mix_in(x, act1, W["wd1"], norms["mix_norm"], W["win"], "ffn1_down_mix_in")
    (attn, lse), got = _attn_fwd(u, qg2, kg2, B, S, carry=comm.gathers["ffn2"] if comm else None)
    if comm:
        W = dict(W, **comm.gathered("ffn2", got))
    conv, y = _conv_fwd(u, cw, norms["conv_b"], norms["conv_ln_g"], norms["conv_ln_b"], B, S)
    h2, n3, g2, u2, dout, dyb, sq, act2 = _mix_out_ffn_loss(h1, attn, conv, W["wout"], norms["ffn2_norm"],
                                                            W["wg2"], W["wu2"], W["wd2"], target, "ffn2_fwd")
    loss = (0.5 / D) * jnp.sum(sq)

    (dg2, du2, dh2, dgn_ffn2), _ = _ffn_bwd_act(dyb, g2, u2, h2, dout, norms["ffn2_norm"],
                                               W["wg2"], W["wu2"], W["wd2"], "ffn2_bwd_act")
    dwd2, _ = _ffn_bwd_w(act2, dyb, "ffn2_bwd_wd")
    dwg2, _ = _ffn_bwd_w(dg2, n3, "ffn2_bwd_wg")
    dwu2, _ = _ffn_bwd_w(du2, n3, "ffn2_bwd_wu")
    dattn, dconv, dwout = _mix_out_bwd(dh2, attn, conv, W["wout"])
    carry = comm.reduce_start("ffn2", {"wg2": dwg2, "wu2": dwu2, "wd2": dwd2, "wout": dwout}) if comm else None
    (dq, dk, dv, dgn_qk), got = _attn_bwd(u, attn, dattn, lse, qg2, kg2, B, S, carry=carry)
    if comm:
        comm.reduce_done(carry, got)
    dca, dcg, dcw, dcs = _conv_bwd(u, y, dconv, cw, norms["conv_ln_g"], norms["conv_ln_b"], B, S)
    dwin, dh1, dyb1, dgn_mix = _mix_in_bwd((dq, dk, dv, dca, dcg), W["win"], n2, h1, dh2, norms["mix_norm"])
    (dg1, du1, gx, dgn_ffn1), _ = _ffn_bwd_act(dyb1, g1, u1, x, dh1, norms["ffn1_norm"],
                                              W["wg1"], W["wu1"], W["wd1"], "ffn1_bwd_act")
    carry = comm.reduce_start("win", {"win": dwin}) if comm else None
    dwd1, got = _ffn_bwd_w(act1, dyb1, "ffn1_bwd_wd", carry=carry)
    if comm:
        comm.reduce_done(carry, got)
        carry = comm.reduce_start("wd1", {"wd1": dwd1})
    dwg1, got = _ffn_bwd_w(dg1, n1, "ffn1_bwd_wg", carry=carry)
    if comm:
        comm.reduce_done(carry, got)
        carry = comm.reduce_start("wg1", {"wg1": dwg1})
    dwu1, got = _ffn_bwd_w(du1, n1, "ffn1_bwd_wu", carry=carry)
    if comm:
        comm.reduce_done(carry, got)
        comm.last = comm.reduce_start("wu1", {"wu1": dwu1})

    qk = jnp.sum(dgn_qk, axis=0)
    cs = jnp.sum(dcs, axis=0)
    small = {
        "ffn1_norm": jnp.sum(dgn_ffn1, axis=0),
        "mix_norm": jnp.sum(dgn_mix, axis=0),
        "q_norm": qk[0:1, 0:HD] + qk[0:1, HD:2 * HD],
        "k_norm": qk[1:2, 0:HD] + qk[1:2, HD:2 * HD],
        "conv_w": jnp.sum(dcw, axis=0)[0:CK],
        "conv_b": cs[0:1],
        "conv_ln_g": cs[1:2],
        "conv_ln_b": cs[2:3],
        "ffn2_norm": jnp.sum(dgn_ffn2, axis=0),
    }
    big = {"wg1": dwg1, "wu1": dwu1, "wd1": dwd1, "win": dwin, "wout": dwout,
           "wg2": dwg2, "wu2": dwu2, "wd2": dwd2}
    return loss, gx, big, small


HBM = pl.BlockSpec(memory_space=pltpu.HBM)
VMEM = pl.BlockSpec(memory_space=pltpu.VMEM)


def _place():
    return lax.axis_index("x"), lax.axis_index("y"), lax.axis_index("c")


class _GatherCarry:
    def __init__(self, shards, mid_at=0.5):
        nt = len(shards)
        self.mid_at = mid_at
        self.shards = shards
        self.in_arrays = [s for s, _ in shards]
        self.in_specs = [VMEM] * nt
        self.out_shape = [jax.ShapeDtypeStruct((NDEV * s.shape[0], s.shape[1]), dt) for s, dt in shards]
        self.out_specs = [HBM] * nt
        self.scratch = ([pltpu.VMEM(s.shape, dt) for s, dt in shards]
                        + [pltpu.SemaphoreType.DMA((nt, 7)), pltpu.SemaphoreType.DMA((nt, 7)),
                           pltpu.SemaphoreType.DMA((nt,))])

    def _copies(self, outs, scr):
        nt = len(self.shards)
        stages = scr[:nt]
        send_sems, recv_sems, local_sems = scr[nt:]
        x, y, c = _place()
        me, sibling = (x, y, c), (x, y, 1 - c)
        xn, yn, diag = (1 - x, y, c), (x, 1 - y, c), (1 - x, 1 - y, c)
        via = (x ^ c, y ^ (1 - c), c)
        onto = (x ^ (1 - c), y ^ c, c)

        def rows(t, px, py, pc):
            r = self.shards[t][0].shape[0]
            return outs[t].at[pl.ds((4 * px + 2 * py + pc) * r, r), :]

        def copy(t, k, block, to, src=None):
            return pltpu.make_async_remote_copy(
                src_ref=rows(t, *block) if src is None else src, dst_ref=rows(t, *block),
                send_sem=send_sems.at[t, k], recv_sem=recv_sems.at[t, k],
                device_id=to, device_id_type=MESH)

        sib = lambda b: (b[0], b[1], 1 - c)
        return dict(
            local=[pltpu.make_async_copy(stages[t], rows(t, *me), local_sems.at[t]) for t in range(nt)],
            own=[[copy(t, 0, me, sibling, src=stages[t]), copy(t, 1, me, xn, src=stages[t]),
                  copy(t, 2, me, yn, src=stages[t])] for t in range(nt)],
            relay=[copy(t, 3, via, onto) for t in range(nt)],
            down=[[copy(t, 4, xn, sibling), copy(t, 5, yn, sibling)] for t in range(nt)],
            down_diag=[copy(t, 6, diag, sibling) for t in range(nt)],
            got_xy=[[copy(t, 1, xn, me), copy(t, 2, yn, me)] for t in range(nt)],
            got_diag=[copy(t, 3, diag, me) for t in range(nt)],
            got_sib=[[copy(t, 0, sibling, me), copy(t, 4, sib(xn), me), copy(t, 5, sib(yn), me),
                      copy(t, 6, sib(diag), me)] for t in range(nt)])

    def start(self, ins, outs, scr):
        cps = self._copies(outs, scr)
        for t, (_, dt) in enumerate(self.shards):
            scr[t][...] = ins[t][...].astype(dt)
            for cp in [cps["local"][t]] + cps["own"][t]:
                cp.start()

    def stages(self):
        return [(self.mid_at, self.mid)]

    def mid(self, ins, outs, scr):
        cps = self._copies(outs, scr)
        for t in range(len(self.shards)):
            for cp in cps["got_xy"][t]:
                cp.wait_recv()
            for cp in [cps["relay"][t]] + cps["down"][t]:
                cp.start()

    def finish(self, ins, outs, scr):
        cps = self._copies(outs, scr)
        for t in range(len(self.shards)):
            cps["got_diag"][t].wait_recv()
            cps["down_diag"][t].start()
        for t in range(len(self.shards)):
            for cp in cps["got_sib"][t]:
                cp.wait_recv()
            for cp in cps["own"][t] + [cps["relay"][t]] + cps["down"][t] + [cps["down_diag"][t]]:
                cp.wait_send()
            cps["local"][t].wait()


def _run_carry(carry, name):
    def body(*refs):
        n_in, n_out = len(carry.in_arrays), len(carry.out_shape)
        ins, outs, scr = refs[:n_in], refs[n_in:n_in + n_out], refs[n_in + n_out:]
        carry.start(ins, outs, scr)
        for _, stage in carry.stages():
            stage(ins, outs, scr)
        carry.finish(ins, outs, scr)

    return pl.pallas_call(
        body, in_specs=carry.in_specs, out_specs=carry.out_specs, out_shape=carry.out_shape,
        scratch_shapes=carry.scratch, compiler_params=pltpu.CompilerParams(vmem_limit_bytes=VMEM_LIMIT),
        name=name)(*carry.in_arrays)


class _ExchangeCarry:
    def __init__(self, names, grads, mid_at=0.5):
        nt = len(grads)
        self.mid_at = mid_at
        self.names = names
        self.in_arrays = [g.reshape(4, 2, g.shape[0] // NDEV, g.shape[1]) for g in grads]
        self.in_specs = [HBM] * nt
        blocks = [g.shape[2:] for g in self.in_arrays]
        self.out_shape = [jax.ShapeDtypeStruct((3,) + b, BF16) for b in blocks]
        self.out_specs = [HBM] * nt
        self.scratch = ([pltpu.VMEM((4,) + b, BF16) for b in blocks] * 2 + [pltpu.VMEM(b, BF16) for b in blocks]
                        + [pltpu.SemaphoreType.DMA((nt, 3)), pltpu.SemaphoreType.DMA((nt, 3))]
                        + [pltpu.SemaphoreType.DMA((nt,))] * 4)

    def _copies(self, ins, outs, scr):
        nt = len(ins)
        theirs, own, relayed = scr[:nt], scr[nt:2 * nt], scr[2 * nt:3 * nt]
        send_sems, recv_sems, keep_sems, load_sems, swap_send, swap_recv = scr[3 * nt:]
        x, y, c = _place()
        q = lambda cx, cy: 2 * cx + cy
        near, far = (x ^ c, y ^ (1 - c)), (x ^ (1 - c), y ^ c)

        def remote(t, k, src, dst, chip):
            return pltpu.make_async_remote_copy(
                src_ref=src, dst_ref=dst, send_sem=send_sems.at[t, k], recv_sem=recv_sems.at[t, k],
                device_id=(*chip, c), device_id_type=MESH)

        return dict(
            swap=[pltpu.make_async_remote_copy(
                src_ref=ins[t].at[:, 1 - c], dst_ref=theirs[t], send_sem=swap_send.at[t], recv_sem=swap_recv.at[t],
                device_id=(x, y, 1 - c), device_id_type=MESH) for t in range(nt)],
            load=[pltpu.make_async_copy(ins[t].at[:, c], own[t], load_sems.at[t]) for t in range(nt)],
            keep=[pltpu.make_async_copy(own[t].at[q(x, y)], outs[t].at[0], keep_sems.at[t]) for t in range(nt)],
            direct=[remote(t, 0, own[t].at[q(*near)], outs[t].at[1], near) for t in range(nt)],
            relay=[remote(t, 1, own[t].at[q(1 - x, 1 - y)], relayed[t], near) for t in range(nt)],
            merged=[remote(t, 2, own[t].at[q(*far)], outs[t].at[2], far) for t in range(nt)],
            theirs=theirs, own=own, relayed=relayed, far=q(*far))

    EARLY_AT = 0.1

    def stages(self):
        return [(self.EARLY_AT, self.early), (self.mid_at, self.mid)]

    def start(self, ins, outs, scr):
        cps = self._copies(ins, outs, scr)
        for t in range(len(ins)):
            cps["swap"][t].start()
            cps["load"][t].start()

    def early(self, ins, outs, scr):
        cps = self._copies(ins, outs, scr)
        for t in range(len(ins)):
            cps["load"][t].wait()
            cps["swap"][t].wait_recv()
            own, theirs = cps["own"][t], cps["theirs"][t]
            for j in range(4):
                own[j] = (own[j].astype(F32) + theirs[j].astype(F32)).astype(BF16)
            for kind in ("relay", "direct", "keep"):
                cps[kind][t].start()

    def mid(self, ins, outs, scr):
        cps = self._copies(ins, outs, scr)
        for t in range(len(ins)):
            cps["relay"][t].wait_recv()
            own, far = cps["own"][t], cps["far"]
            own[far] = (own[far].astype(F32) + cps["relayed"][t][...].astype(F32)).astype(BF16)
            cps["merged"][t].start()

    def finish(self, ins, outs, scr):
        cps = self._copies(ins, outs, scr)
        for t in range(len(ins)):
            cps["swap"][t].wait_send()
            cps["direct"][t].wait()
            cps["relay"][t].wait_send()
            cps["merged"][t].wait()
            cps["keep"][t].wait()


class _Comm:
    def __init__(self, groups):
        self.names = {tag: list(g) for tag, (g, _) in groups.items()}
        self.gathers = {tag: _GatherCarry(list(g.values()), mid_at) for tag, (g, mid_at) in groups.items()}
        self.reduced = {}

    def gathered(self, tag, outs):
        return dict(zip(self.names[tag], outs))

    def reduce_start(self, tag, grads, mid_at=0.5):
        names = list(grads)
        return _ExchangeCarry(names, [grads[n] for n in names], mid_at)

    def reduce_done(self, carry, outs):
        self.reduced.update(zip(carry.names, outs))


def _adamw_math(w, g, m, v):
    m = B1 * m + (1.0 - B1) * g
    v = B2 * v + (1.0 - B2) * (g * g)
    m_hat = m / (1.0 - B1 ** STEP)
    v_hat = v / (1.0 - B2 ** STEP)
    delta = -LR * (m_hat / (jnp.sqrt(v_hat) + AEPS) + WD * w)
    return delta, m, v


def _adamw_big(recv, w, m, v, name):
    def body(r_ref, w_ref, m_ref, v_ref, g_ref, d_ref, mo_ref, vo_ref):
        g = r_ref[0].astype(F32)
        for q in range(1, 3):
            g = g + r_ref[q].astype(F32)
        d, mn, vn = _adamw_math(w_ref[...], g, m_ref[...], v_ref[...])
        g_ref[...] = g
        d_ref[...] = d
        mo_ref[...] = mn
        vo_ref[...] = vn

    rows, n = w.shape
    tr = rows // 2
    row = pl.BlockSpec((tr, n), lambda t: (t, 0))
    return pl.pallas_call(
        body, grid=(2,), in_specs=[pl.BlockSpec((3, tr, n), lambda t: (0, t, 0)), row, row, row],
        out_specs=[row] * 4, out_shape=[jax.ShapeDtypeStruct(w.shape, F32)] * 4,
        compiler_params=_cp(("parallel",)), name=name)(recv, w, m, v)


SMALL_NAMES = ("ffn1_norm", "mix_norm", "ffn2_norm", "conv_b", "conv_ln_g", "conv_ln_b", "q_norm", "k_norm")
SROWS = 16
LOSS_ROW = len(SMALL_NAMES)
CWF = 2


def _small_step(gs, loss_row, gcw, ws, ms, vs, wcw, mcw, vcw, carry=None):
    ns = len(SMALL_NAMES)
    widths = [g.shape[1] for g in gs]

    def body(*refs):
        it = iter(refs)
        take = lambda n: [next(it) for _ in range(n)]
        g_refs, (loss_ref, gcw_ref) = take(ns), take(2)
        w_refs, m_refs, v_refs = take(ns), take(ns), take(ns)
        wcw_ref, mcw_ref, vcw_ref = take(3)
        cins = take(len(carry.in_arrays)) if carry else []
        outs = [take(4) for _ in range(ns)]
        cw_outs, (loss_out,) = take(4), take(1)
        couts = take(len(carry.out_shape)) if carry else []
        send, slots, cslots, send_sems, recv_sems, csend_sems, crecv_sems = take(7)
        cscr = list(it)
        x, y, c = _place()
        me = 4 * x + 2 * y + c
        send[...] = jnp.zeros_like(send)
        for k in range(ns):
            send[k:k + 1, 0:widths[k]] = g_refs[k][...]
        send[LOSS_ROW:LOSS_ROW + 1, 0:128] = loss_ref[...]
        slots[me] = send[...]
        cslots[me] = gcw_ref[...]
        cps = []
        for k in range(1, NDEV):
            peer = (x ^ ((k >> 2) & 1), y ^ ((k >> 1) & 1), c ^ (k & 1))
            cps.append(pltpu.make_async_remote_copy(
                src_ref=send, dst_ref=slots.at[me], send_sem=send_sems.at[k - 1], recv_sem=recv_sems.at[k - 1],
                device_id=peer, device_id_type=MESH))
            cps.append(pltpu.make_async_remote_copy(
                src_ref=gcw_ref, dst_ref=cslots.at[me], send_sem=csend_sems.at[k - 1],
                recv_sem=crecv_sems.at[k - 1], device_id=peer, device_id_type=MESH))
        for cp in cps:
            cp.start()
        if carry:
            carry.start(cins, couts, cscr)
        for cp in cps:
            cp.wait()
        if carry:
            for _, stage in carry.stages():
                stage(cins, couts, cscr)
        tot = slots[0]
        ctot = cslots[0, me]
        for j in range(1, NDEV):
            tot = tot + slots[j]
            ctot = ctot + cslots[j, me]

        def step(g, w_ref, m_ref, v_ref, o):
            d, mn, vn = _adamw_math(w_ref[...], g, m_ref[...], v_ref[...])
            o[0][...], o[1][...], o[2][...], o[3][...] = g, d, mn, vn

        for k in range(ns):
            step(tot[k:k + 1, 0:widths[k]], w_refs[k], m_refs[k], v_refs[k], outs[k])
        step(ctot, wcw_ref, mcw_ref, vcw_ref, cw_outs)
        loss_out[...] = tot[LOSS_ROW:LOSS_ROW + 1, 0:128]
        if carry:
            carry.finish(cins, couts, cscr)

    args = [*gs, loss_row, gcw, *ws, *ms, *vs, wcw, mcw, vcw]
    out_shape = ([jax.ShapeDtypeStruct((1, n), F32) for n in widths for _ in range(4)]
                 + [jax.ShapeDtypeStruct((CWF, D), F32)] * 4 + [jax.ShapeDtypeStruct((1, 128), F32)])
    n_own = len(out_shape)
    res = pl.pallas_call(
        body, in_specs=[VMEM] * len(args) + (carry.in_specs if carry else []),
        out_specs=[VMEM] * n_own + (carry.out_specs if carry else []),
        out_shape=out_shape + (carry.out_shape if carry else []),
        scratch_shapes=[pltpu.VMEM((SROWS, D), F32), pltpu.VMEM((NDEV, SROWS, D), F32),
                        pltpu.VMEM((NDEV, NDEV, CWF, D), F32)]
                       + [pltpu.SemaphoreType.DMA((NDEV - 1,))] * 4 + (carry.scratch if carry else []),
        name="small_step")(*args, *(carry.in_arrays if carry else []))
    per = [res[4 * k:4 * k + 4] for k in range(ns)]
    return per, res[4 * ns:4 * ns + 4], res[n_own - 1], res[n_own:]


def _pack_cw(a):
    flat = a.reshape(a.shape[:-2] + (CK * HD,))
    pad = [(0, 0)] * (flat.ndim - 1) + [(0, CWF * D - CK * HD)]
    return jnp.pad(flat, pad).reshape(a.shape[:-2] + (CWF, D))


def _unpack_cw(v):
    return v.reshape(-1)[:CK * HD].reshape(1, CK, HD)


def kernel(x, ffn1_norm, ffn1_w_gate, ffn1_w_up, ffn1_w_down, mix_norm, w_in, q_norm, k_norm, conv_w, conv_b, conv_ln_g, conv_ln_b, w_out, ffn2_norm, ffn2_w_gate, ffn2_w_up, ffn2_w_down, loss_target, m_ffn1_norm, m_ffn1_w_gate, m_ffn1_w_up, m_ffn1_w_down, m_mix_norm, m_w_in, m_q_norm, m_k_norm, m_conv_w, m_conv_b, m_conv_ln_g, m_conv_ln_b, m_w_out, m_ffn2_norm, m_ffn2_w_gate, m_ffn2_w_up, m_ffn2_w_down, v_ffn1_norm, v_ffn1_w_gate, v_ffn1_w_up, v_ffn1_w_down, v_mix_norm, v_w_in, v_q_norm, v_k_norm, v_conv_w, v_conv_b, v_conv_ln_g, v_conv_ln_b, v_w_out, v_ffn2_norm, v_ffn2_w_gate, v_ffn2_w_up, v_ffn2_w_down):
    P = dict(ffn1_norm=ffn1_norm, ffn1_w_gate=ffn1_w_gate, ffn1_w_up=ffn1_w_up, ffn1_w_down=ffn1_w_down,
             mix_norm=mix_norm, w_in=w_in, q_norm=q_norm, k_norm=k_norm, conv_w=conv_w, conv_b=conv_b,
             conv_ln_g=conv_ln_g, conv_ln_b=conv_ln_b, w_out=w_out, ffn2_norm=ffn2_norm,
             ffn2_w_gate=ffn2_w_gate, ffn2_w_up=ffn2_w_up, ffn2_w_down=ffn2_w_down)
    M = dict(ffn1_norm=m_ffn1_norm, ffn1_w_gate=m_ffn1_w_gate, ffn1_w_up=m_ffn1_w_up, ffn1_w_down=m_ffn1_w_down,
             mix_norm=m_mix_norm, w_in=m_w_in, q_norm=m_q_norm, k_norm=m_k_norm, conv_w=m_conv_w, conv_b=m_conv_b,
             conv_ln_g=m_conv_ln_g, conv_ln_b=m_conv_ln_b, w_out=m_w_out, ffn2_norm=m_ffn2_norm,
             ffn2_w_gate=m_ffn2_w_gate, ffn2_w_up=m_ffn2_w_up, ffn2_w_down=m_ffn2_w_down)
    V = dict(ffn1_norm=v_ffn1_norm, ffn1_w_gate=v_ffn1_w_gate, ffn1_w_up=v_ffn1_w_up, ffn1_w_down=v_ffn1_w_down,
             mix_norm=v_mix_norm, w_in=v_w_in, q_norm=v_q_norm, k_norm=v_k_norm, conv_w=v_conv_w, conv_b=v_conv_b,
             conv_ln_g=v_conv_ln_g, conv_ln_b=v_conv_ln_b, w_out=v_w_out, ffn2_norm=v_ffn2_norm,
             ffn2_w_gate=v_ffn2_w_gate, ffn2_w_up=v_ffn2_w_up, ffn2_w_down=v_ffn2_w_down)
    order = ["ffn1_norm", "ffn1_w_gate", "ffn1_w_up", "ffn1_w_down", "mix_norm", "w_in", "q_norm", "k_norm",
             "conv_w", "conv_b", "conv_ln_g", "conv_ln_b", "w_out", "ffn2_norm", "ffn2_w_gate", "ffn2_w_up",
             "ffn2_w_down"]
    B, S, _ = x.shape
    T = B * S

    bigs = [("wg1", "ffn1_w_gate", True), ("wu1", "ffn1_w_up", True), ("wd1", "ffn1_w_down", False),
            ("win", "w_in", True), ("wout", "w_out", False),
            ("wg2", "ffn2_w_gate", True), ("wu2", "ffn2_w_up", True), ("wd2", "ffn2_w_down", False)]
    hm = lambda a, tr: jnp.transpose(a[0]) if tr else a[0]
    cw_pad = jnp.zeros((32, 128), F32).at[0:CK, 0:HD].set(conv_w[0])
    shard = {ln: (hm(P[pn], tr), BF16) for ln, pn, tr in bigs}
    gathered = _run_carry(_GatherCarry([shard["wg1"], shard["wu1"], (cw_pad, F32)]), "gather_first")
    W = {"wg1": gathered[0], "wu1": gathered[1]}
    cwg = gathered[2].reshape(NDEV, 32, 128)[:, 0:CK, 0:HD]
    W["conv_w"] = jnp.transpose(cwg, (1, 0, 2)).reshape(CK, DC)
    norms = {n: P[n] for n in SMALL_NAMES}
    comm = _Comm({"down_in": ({n: shard[n] for n in ("wd1", "win")}, 0.5),
                  "ffn2": ({n: shard[n] for n in ("wg2", "wu2", "wd2", "wout")}, 0.5)})

    loss_part, gx, _, small = _local_step(x.reshape(T, D), loss_target.reshape(T, D), norms, W, B, S, comm)

    G, Dl, Mn, Vn = {}, {}, {}, {}
    dcw = small["conv_w"].reshape(CK, NDEV, HD).transpose(1, 0, 2)
    loss_row = jnp.zeros((1, 128), F32).at[0, 0].set(loss_part)
    per, cw_outs, loss_out, got = _small_step(
        [small[n] for n in SMALL_NAMES], loss_row, _pack_cw(dcw),
        [P[n] for n in SMALL_NAMES], [M[n] for n in SMALL_NAMES], [V[n] for n in SMALL_NAMES],
        _pack_cw(P["conv_w"][0]), _pack_cw(M["conv_w"][0]), _pack_cw(V["conv_w"][0]), carry=comm.last)
    comm.reduce_done(comm.last, got)
    loss = loss_out[0, 0]
    for n, outs in zip(SMALL_NAMES, per):
        G[n], Dl[n], Mn[n], Vn[n] = outs
    G["conv_w"], Dl["conv_w"], Mn["conv_w"], Vn["conv_w"] = [_unpack_cw(o) for o in cw_outs]

    for ln, pn, tr in bigs:
        outs = _adamw_big(comm.reduced[ln], hm(P[pn], tr), hm(M[pn], tr), hm(V[pn], tr), "adamw_" + ln)
        G[pn], Dl[pn], Mn[pn], Vn[pn] = [(jnp.transpose(o) if tr else o)[None] for o in outs]

    return (loss, gx.reshape(B, S, D), *[G[n] for n in order], *[Dl[n] for n in order],
            *[Mn[n] for n in order], *[Vn[n] for n in order])
```

```python
import functools

import jax
import jax.numpy as jnp
from jax import lax
from jax.experimental import pallas as pl
from jax.experimental.pallas import tpu as pltpu

F32 = jnp.float32
BF16 = jnp.bfloat16

D = 1024
FF = 2816
HD = 64
DA = 512
DC = 512
DIN = 2560
CK = 31
BLK = 128
DILS = (1, 4, 16)
EPS = 1e-6
NDEV = 8
MESH = pl.DeviceIdType.MESH

LR, B1, B2, AEPS, WD, STEP = 0.001, 0.9, 0.999, 1e-08, 0.01, 10

NT = (((1,), (1,)), ((), ()))
TN = (((0,), (0,)), ((), ()))

VMEM_LIMIT = 60 * 1024 * 1024


def _cp(sem=None):
    return pltpu.CompilerParams(dimension_semantics=sem, vmem_limit_bytes=VMEM_LIMIT)


def _sigmoid(x):
    return 0.5 * (jnp.tanh(0.5 * x) + 1.0)


def _pallas(body, args, *, grid, in_specs, out_specs, out_shape, scratch_shapes, sem, name, carry=None):
    if carry is None:
        outs = pl.pallas_call(body, grid=grid, in_specs=in_specs, out_specs=out_specs, out_shape=out_shape,
                              scratch_shapes=scratch_shapes, compiler_params=_cp(sem), name=name)(*args)
        return outs, None
    n_in, n_out, n_scr = len(in_specs), len(out_shape), len(scratch_shapes)
    c_in, c_out = len(carry.in_arrays), len(carry.out_shape)

    def wrapped(*refs):
        ins, refs = refs[:n_in], refs[n_in:]
        cins, refs = refs[:c_in], refs[c_in:]
        outs, refs = refs[:n_out], refs[n_out:]
        couts, refs = refs[:c_out], refs[c_out:]
        scr, cscr = refs[:n_scr], refs[n_scr:]
        ids = [pl.program_id(a) for a in range(len(grid))]
        step = ids[0]
        for i, n in zip(ids[1:], grid[1:]):
            step = step * n + i
        steps = functools.reduce(lambda a, b: a * b, grid)

        @pl.when(step == 0)
        def _():
            carry.start(cins, couts, cscr)

        body(*ins, *outs, *scr)

        for frac, stage in carry.stages():
            @pl.when(step == int(steps * frac))
            def _(stage=stage):
                stage(cins, couts, cscr)

        @pl.when(step == steps - 1)
        def _():
            carry.finish(cins, couts, cscr)

    outs = pl.pallas_call(
        wrapped, grid=grid, in_specs=list(in_specs) + carry.in_specs, out_specs=list(out_specs) + carry.out_specs,
        out_shape=list(out_shape) + carry.out_shape, scratch_shapes=list(scratch_shapes) + carry.scratch,
        compiler_params=_cp(("arbitrary",) * len(grid)), name=name)(*args, *carry.in_arrays)
    return outs[:n_out], outs[n_out:]


FC = 256


def _resident(shape):
    return pl.BlockSpec(shape, lambda *_: (0,) * len(shape), pipeline_mode=pl.Buffered(1))


def _mix_out_ffn_loss(h1, attn, conv, wout, gain, wg, wu, wd, target, name):
    T = h1.shape[0]
    tm = 512
    nt = T // tm

    def body(h1_ref, at_ref, cv_ref, wo_ref, gain_ref, wg_ref, wu_ref, wd_ref, t_ref,
             h2_ref, n_ref, g_ref, u_ref, dout_ref, dyb_ref, sq_ref, a_hbm, a_scr, a_sem):
        t = pl.program_id(0)
        a_out = lambda i: pltpu.make_async_copy(a_scr, a_hbm.at[pl.ds(pl.multiple_of(i * tm, tm), tm), :], a_sem)

        @pl.when(t > 0)
        def _():
            a_out(t - 1).wait()

        xv = (h1_ref[...]
              + jnp.dot(at_ref[...], wo_ref[0:DA, :], preferred_element_type=F32)
              + jnp.dot(cv_ref[...], wo_ref[DA:D, :], preferred_element_type=F32))
        h2_ref[...] = xv
        r = lax.rsqrt(jnp.mean(xv * xv, axis=-1, keepdims=True) + EPS)
        n_ref[...] = (xv * r * gain_ref[...]).astype(BF16)
        for c in range(FF // FC):
            cols = slice(c * FC, (c + 1) * FC)
            nb = n_ref[...]
            g = lax.dot_general(nb, wg_ref[cols, :], NT, preferred_element_type=F32)
            u = lax.dot_general(nb, wu_ref[cols, :], NT, preferred_element_type=F32)
            g_ref[:, cols] = g.astype(BF16)
            u_ref[:, cols] = u.astype(BF16)
            a_scr[:, cols] = (g * _sigmoid(g) * u).astype(BF16)
        a_out(t).start()
        e = h2_ref[...] + 0.5 * jnp.dot(a_scr[...], wd_ref[...], preferred_element_type=F32) - t_ref[...]
        dout = e * (1.0 / D)
        dout_ref[...] = dout
        dyb_ref[...] = (0.5 * dout).astype(BF16)
        sq_ref[...] = jnp.sum(e * e, axis=0, keepdims=True)[None]

        @pl.when(t == nt - 1)
        def _():
            a_out(t).wait()

    row = pl.BlockSpec((tm, D), lambda t: (t, 0))
    half = pl.BlockSpec((tm, DA), lambda t: (t, 0))
    wide = pl.BlockSpec((tm, FF), lambda t: (t, 0))
    outs, _ = _pallas(
        body, (h1, attn, conv, wout, gain, wg, wu, wd, target), grid=(nt,),
        in_specs=[row, half, half, _resident((D, D)), _resident((1, D)), _resident((FF, D)), _resident((FF, D)),
                  _resident((FF, D)), row],
        out_specs=[row, row, wide, wide, row, row, pl.BlockSpec((1, 1, D), lambda t: (t, 0, 0)), HBM],
        out_shape=[jax.ShapeDtypeStruct((T, D), F32), jax.ShapeDtypeStruct((T, D), BF16)]
                  + [jax.ShapeDtypeStruct((T, FF), BF16)] * 2
                  + [jax.ShapeDtypeStruct((T, D), F32), jax.ShapeDtypeStruct((T, D), BF16),
                     jax.ShapeDtypeStruct((nt, 1, D), F32), jax.ShapeDtypeStruct((T, FF), BF16)],
        scratch_shapes=[pltpu.VMEM((tm, FF), BF16), pltpu.SemaphoreType.DMA(())],
        sem=("arbitrary",), name=name)
    return outs


def _ffn_gate_up(x, gain, wg, wu, name, carry=None):
    T = x.shape[0]
    tm = 512

    def body(x_ref, gain_ref, wg_ref, wu_ref, n_ref, g_ref, u_ref, a_ref):
        xv = x_ref[...]
        r = lax.rsqrt(jnp.mean(xv * xv, axis=-1, keepdims=True) + EPS)
        n_ref[...] = (xv * r * gain_ref[...]).astype(BF16)
        for c in range(FF // FC):
            cols = slice(c * FC, (c + 1) * FC)
            nb = n_ref[...]
            g = lax.dot_general(nb, wg_ref[cols, :], NT, preferred_element_type=F32)
            u = lax.dot_general(nb, wu_ref[cols, :], NT, preferred_element_type=F32)
            g_ref[:, cols] = g.astype(BF16)
            u_ref[:, cols] = u.astype(BF16)
            a_ref[:, cols] = (g * _sigmoid(g) * u).astype(BF16)

    row = pl.BlockSpec((tm, D), lambda t: (t, 0))
    wide = pl.BlockSpec((tm, FF), lambda t: (t, 0))
    return _pallas(
        body, (x, gain, wg, wu), grid=(T // tm,),
        in_specs=[row, _resident((1, D)), _resident((FF, D)), _resident((FF, D))],
        out_specs=[row, wide, wide, wide],
        out_shape=[jax.ShapeDtypeStruct((T, D), BF16)] + [jax.ShapeDtypeStruct((T, FF), BF16)] * 3,
        scratch_shapes=[], sem=("parallel",), name=name, carry=carry)


def _ffn_down_mix_in(x, a, wd, gain, win, name):
    T = x.shape[0]
    tm = 512

    def body(x_ref, a_ref, wd_ref, gain_ref, win_ref, h_ref, u_ref, n_ref):
        hv = x_ref[...] + 0.5 * jnp.dot(a_ref[...], wd_ref[...], preferred_element_type=F32)
        h_ref[...] = hv
        r = lax.rsqrt(jnp.mean(hv * hv, axis=-1, keepdims=True) + EPS)
        n_ref[...] = (hv * r * gain_ref[...]).astype(BF16)
        u_ref[...] = lax.dot_general(n_ref[...], win_ref[...], NT, preferred_element_type=F32).astype(BF16)

    row = pl.BlockSpec((tm, D), lambda t: (t, 0))
    wide = pl.BlockSpec((tm, FF), lambda t: (t, 0))
    outs, _ = _pallas(
        body, (x, a, wd, gain, win), grid=(T // tm,),
        in_specs=[row, wide, _resident((FF, D)), _resident((1, D)), _resident((DIN, D))],
        out_specs=[row, pl.BlockSpec((tm, DIN), lambda t: (t, 0)), row],
        out_shape=[jax.ShapeDtypeStruct((T, D), F32), jax.ShapeDtypeStruct((T, DIN), BF16),
                   jax.ShapeDtypeStruct((T, D), BF16)],
        scratch_shapes=[], sem=("parallel",), name=name)
    return outs


def _ffn_bwd_act(dyb, g, u, x, dout, gain, wg, wu, wd, name, carry=None):
    T = x.shape[0]
    tm = 256
    nt = T // tm

    def body(dy_ref, g_ref, u_ref, x_ref, dout_ref, gain_ref, wg_ref, wu_ref, wd_ref,
             dg_ref, du_ref, dx_ref, dgn_ref):
        for c in range(FF // FC):
            cols = slice(c * FC, (c + 1) * FC)
            da = lax.dot_general(dy_ref[...], wd_ref[cols, :], NT, preferred_element_type=F32)
            gv = g_ref[:, cols].astype(F32)
            uv = u_ref[:, cols].astype(F32)
            sg = _sigmoid(gv)
            dg_ref[:, cols] = (da * uv * (sg * (1.0 + gv * (1.0 - sg)))).astype(BF16)
            du_ref[:, cols] = (da * (gv * sg)).astype(BF16)
        dn = (jnp.dot(dg_ref[...], wg_ref[...], preferred_element_type=F32)
              + jnp.dot(du_ref[...], wu_ref[...], preferred_element_type=F32))
        dx, dgain = _rms_bwd_rows(dn, x_ref[...], gain_ref[...])
        dx_ref[...] = dout_ref[...] + dx
        dgn_ref[...] = dgain[None]

    row = pl.BlockSpec((tm, D), lambda t: (t, 0))
    wide = pl.BlockSpec((tm, FF), lambda t: (t, 0))
    return _pallas(
        body, (dyb, g, u, x, dout, gain, wg, wu, wd), grid=(nt,),
        in_specs=[row, wide, wide, row, row, _resident((1, D)), _resident((FF, D)), _resident((FF, D)),
                  _resident((FF, D))],
        out_specs=[wide, wide, row, pl.BlockSpec((1, 1, D), lambda t: (t, 0, 0))],
        out_shape=[jax.ShapeDtypeStruct((T, FF), BF16)] * 2
                  + [jax.ShapeDtypeStruct((T, D), F32), jax.ShapeDtypeStruct((nt, 1, D), F32)],
        scratch_shapes=[], sem=("parallel",), name=name, carry=carry)


def _ffn_bwd_w(lhs, rhs, name, carry=None):
    T = rhs.shape[0]
    tf = 256

    def body(l_ref, r_ref, dw_ref):
        dw_ref[...] = lax.dot_general(l_ref[...], r_ref[...], TN, preferred_element_type=F32).astype(BF16)

    (dw,), got = _pallas(
        body, (lhs, rhs), grid=(FF // tf,),
        in_specs=[pl.BlockSpec((T, tf), lambda f: (0, f)), _resident((T, D))],
        out_specs=[pl.BlockSpec((tf, D), lambda f: (f, 0))], out_shape=[jax.ShapeDtypeStruct((FF, D), BF16)],
        scratch_shapes=[], sem=("parallel",), name=name, carry=carry)
    return dw, got


def _rms_bwd_rows(dn, xv, gain):
    r = lax.rsqrt(jnp.mean(xv * xv, axis=-1, keepdims=True) + EPS)
    xhat = xv * r
    dxhat = dn * gain
    dx = r * (dxhat - xhat * jnp.mean(dxhat * xhat, axis=-1, keepdims=True))
    return dx, jnp.sum(dn * xhat, axis=0, keepdims=True)


def _mix_out_bwd(dh, attn, conv, wout):
    T = dh.shape[0]
    tm = 512
    nt = T // tm

    def body(dh_ref, a_ref, c_ref, w_ref, da_ref, dc_ref, dw_ref, acc_scr):
        t = pl.program_id(0)

        @pl.when(t == 0)
        def _():
            acc_scr[...] = jnp.zeros_like(acc_scr)

        dhb = dh_ref[...].astype(BF16)
        dmix = lax.dot_general(dhb, w_ref[...], NT, preferred_element_type=F32)
        da_ref[...] = dmix[:, 0:DA].astype(BF16)
        dc_ref[...] = dmix[:, DA:D].astype(BF16)
        acc_scr[0:DA, :] += lax.dot_general(a_ref[...], dhb, TN, preferred_element_type=F32)
        acc_scr[DA:D, :] += lax.dot_general(c_ref[...], dhb, TN, preferred_element_type=F32)

        @pl.when(t == nt - 1)
        def _():
            dw_ref[...] = acc_scr[...].astype(BF16)

    row = pl.BlockSpec((tm, D), lambda t: (t, 0))
    half = pl.BlockSpec((tm, DA), lambda t: (t, 0))
    full = pl.BlockSpec((D, D), lambda t: (0, 0))
    return pl.pallas_call(
        body, grid=(nt,), in_specs=[row, half, half, full], out_specs=[half, half, full],
        out_shape=[jax.ShapeDtypeStruct((T, DA), BF16)] * 2 + [jax.ShapeDtypeStruct((D, D), BF16)],
        scratch_shapes=[pltpu.VMEM((D, D), F32)],
        compiler_params=_cp(("arbitrary",)), name="mix_out_bwd")(dh, attn, conv, wout)


def _mix_in_bwd(dparts, win, nb, h, dh, gain):
    T = h.shape[0]
    tm = 512
    nt = T // tm

    def body(d0, d1, d2, d3, d4, w_ref, n_ref, h_ref, dh_ref, gain_ref,
             dw_ref, dx_ref, dyb_ref, dg_ref, acc_scr):
        t = pl.program_id(0)

        @pl.when(t == 0)
        def _():
            acc_scr[...] = jnp.zeros_like(acc_scr)

        n = n_ref[...]
        dn = jnp.zeros((tm, D), F32)
        for i, d_ref in enumerate((d0, d1, d2, d3, d4)):
            dv = d_ref[...]
            dn = dn + jnp.dot(dv, w_ref[i * DA:(i + 1) * DA, :], preferred_element_type=F32)
            acc_scr[i * DA:(i + 1) * DA, :] += lax.dot_general(dv, n, TN, preferred_element_type=F32)
        dx, dgain = _rms_bwd_rows(dn, h_ref[...], gain_ref[...])
        tot = dh_ref[...] + dx
        dx_ref[...] = tot
        dyb_ref[...] = (0.5 * tot).astype(BF16)
        dg_ref[...] = dgain[None]

        @pl.when(t == nt - 1)
        def _():
            dw_ref[...] = acc_scr[...].astype(BF16)

    row = pl.BlockSpec((tm, D), lambda t: (t, 0))
    half = pl.BlockSpec((tm, DA), lambda t: (t, 0))
    full = pl.BlockSpec((DIN, D), lambda t: (0, 0))
    return pl.pallas_call(
        body, grid=(nt,),
        in_specs=[half] * 5 + [full, row, row, row, pl.BlockSpec((1, D), lambda t: (0, 0))],
        out_specs=[full, row, row, pl.BlockSpec((1, 1, D), lambda t: (t, 0, 0))],
        out_shape=[jax.ShapeDtypeStruct((DIN, D), BF16), jax.ShapeDtypeStruct((T, D), F32),
                   jax.ShapeDtypeStruct((T, D), BF16), jax.ShapeDtypeStruct((nt, 1, D), F32)],
        scratch_shapes=[pltpu.VMEM((DIN, D), F32)],
        compiler_params=_cp(("arbitrary",)), name="mix_in_bwd")(*dparts, win, nb, h, dh, gain)


def _head_masks():
    lane = lax.broadcasted_iota(jnp.int32, (1, 2 * HD), 1)
    m0 = lane < HD
    return m0, jnp.logical_not(m0)


def _stack_heads(v, m0, m1):
    z = jnp.zeros_like(v)
    return jnp.concatenate([jnp.where(m0, v, z), jnp.where(m1, v, z)], axis=0)


def _unstack_heads(v2, m0):
    return jnp.where(m0, v2[0:BLK], v2[BLK:2 * BLK])


def _head_sums(xv):
    ri = lax.broadcasted_iota(jnp.int32, (2 * HD, 2 * HD), 0)
    ci = lax.broadcasted_iota(jnp.int32, (2 * HD, 2 * HD), 1)
    ones = jnp.where((ri < HD) == (ci < HD), 1.0, 0.0).astype(BF16)
    hi = xv.astype(BF16)
    lo = (xv - hi.astype(F32)).astype(BF16)
    return (jnp.dot(hi, ones, preferred_element_type=F32) + jnp.dot(lo, ones, preferred_element_type=F32))


def _head_rms(xv):
    return lax.rsqrt(_head_sums(xv * xv) * (1.0 / HD) + EPS)


def _band_mask(first):
    qi = lax.broadcasted_iota(jnp.int32, (BLK, 2 * BLK), 0)
    ci = lax.broadcasted_iota(jnp.int32, (BLK, 2 * BLK), 1)
    band = (ci >= qi) & (ci <= qi + BLK)
    return band & ((ci >= BLK) | jnp.logical_not(first))


def _block_rows(j, d, seg):
    r, n = j // seg, j % seg
    start = r + (d * BLK) * n
    first = n == 0
    prev = jnp.where(first, start, start - d * BLK)
    return pl.ds(start, BLK, stride=d), pl.ds(prev, BLK, stride=d), first


def _block_keys(refs, cur, prev, first, single):
    if single:
        qi = lax.broadcasted_iota(jnp.int32, (BLK, BLK), 0)
        ci = lax.broadcasted_iota(jnp.int32, (BLK, BLK), 1)
        return [r[cur, :].astype(BF16) for r in refs], ci <= qi
    return ([jnp.concatenate([r[prev, :], r[cur, :]], axis=0).astype(BF16) for r in refs], _band_mask(first))


def _attn_fwd(u, qg2, kg2, B, S, carry=None):
    T = B * S
    NB = S // BLK
    scale = HD ** -0.5

    def body(q_ref, k_ref, v_ref, qg_ref, kg_ref, o_ref, lse_ref, qn, kn, vn, os_, ls_):
        m0, m1 = _head_masks()
        qv = q_ref[...].astype(F32)
        qn[...] = qv * _head_rms(qv) * (qg_ref[...] * scale)
        kv = k_ref[...].astype(F32)
        kn[...] = kv * _head_rms(kv) * kg_ref[...]
        vn[...] = v_ref[...].astype(F32)

        for i, d in enumerate(DILS):
            seg = NB // d

            def blk(j, c, i=i, d=d, seg=seg):
                cur, prev, first = _block_rows(j, d, seg)
                q2 = _stack_heads(qn[cur, :].astype(BF16), m0, m1)
                (kk, vv), mask = _block_keys((kn, vn), cur, prev, first, False)
                s = lax.dot_general(q2, kk, NT, preferred_element_type=F32)
                s = jnp.where(jnp.concatenate([mask, mask], axis=0), s, -1e30)
                mx = jnp.max(s, axis=-1, keepdims=True)
                p = jnp.exp(s - mx)
                l = jnp.sum(p, axis=-1, keepdims=True)
                o2 = jnp.dot((p * (1.0 / l)).astype(BF16), vv, preferred_element_type=F32)
                os_[i, cur, :] = _unstack_heads(o2, m0)
                ls_[i, cur, :] = _unstack_heads(mx + jnp.log(l), m0)
                return c

            lax.fori_loop(0, NB, blk, 0, unroll=8)

        def comb(c, carry):
            rows = pl.ds(pl.multiple_of(c * 256, 256), 256)
            l0, l1, l2 = ls_[0, rows, :], ls_[1, rows, :], ls_[2, rows, :]
            mx = jnp.maximum(jnp.maximum(l0, l1), l2)
            e0, e1, e2 = jnp.exp(l0 - mx), jnp.exp(l1 - mx), jnp.exp(l2 - mx)
            tot = e0 + e1 + e2
            inv = 1.0 / tot
            o = (e0 * os_[0, rows, :] + e1 * os_[1, rows, :] + e2 * os_[2, rows, :]) * inv
            o_ref[rows, :] = o.astype(BF16)
            lse_ref[rows, :] = mx + jnp.log(tot)
            return carry

        lax.fori_loop(0, S // 256, comb, 0)

    pair = 2 * HD
    blk_spec = lambda off: pl.BlockSpec((S, pair), lambda b, p, off=off: (b, off + p))
    gspec = pl.BlockSpec((1, pair), lambda b, p: (0, 0))
    return _pallas(
        body, (u, u, u, qg2, kg2), grid=(B, DA // pair),
        in_specs=[blk_spec(0), blk_spec(DA // pair), blk_spec(2 * DA // pair), gspec, gspec],
        out_specs=[blk_spec(0), blk_spec(0)],
        out_shape=[jax.ShapeDtypeStruct((T, DA), BF16), jax.ShapeDtypeStruct((T, DA), F32)],
        scratch_shapes=[pltpu.VMEM((S, pair), F32)] * 3 + [pltpu.VMEM((3, S, pair), F32)] * 2,
        sem=("parallel", "parallel"), name="attn_fwd", carry=carry)


def _attn_bwd(u, attn, dattn, lse, qg2, kg2, B, S, carry=None):
    T = B * S
    NB = S // BLK
    scale = HD ** -0.5
    pair = 2 * HD

    def body(q_ref, k_ref, v_ref, o_ref, do_ref, lse_ref, qg_ref, kg_ref,
             dq_ref, dk_ref, dv_ref, dgn_ref,
             qn, kn, vn, don, ldl, accq, acck, accv, rq, rk):
        m0, m1 = _head_masks()
        lane = lax.broadcasted_iota(jnp.int32, (1, pair), 1)
        qv = q_ref[...].astype(F32)
        rq[...] = _head_rms(qv)
        qn[...] = qv * rq[...] * (qg_ref[...] * scale)
        kv = k_ref[...].astype(F32)
        rk[...] = _head_rms(kv)
        kn[...] = kv * rk[...] * kg_ref[...]
        vn[...] = v_ref[...].astype(F32)
        dov = do_ref[...].astype(F32)
        don[...] = dov
        ldl[...] = jnp.where((lane % HD) < HD // 2, lse_ref[...], _head_sums(dov * o_ref[...].astype(F32)))

        for i, d in enumerate(DILS):
            seg = NB // d

            def blk(j, c, i=i, d=d, seg=seg):
                cur, prev, first = _block_rows(j, d, seg)
                q2 = _stack_heads(qn[cur, :].astype(BF16), m0, m1)
                do2 = _stack_heads(don[cur, :].astype(BF16), m0, m1)
                (kk, vv), mask = _block_keys((kn, vn), cur, prev, first, seg == 1)
                ldv = ldl[cur, :]
                lse2 = jnp.concatenate([ldv[:, 0:1], ldv[:, HD:HD + 1]], axis=0)
                dl2 = jnp.concatenate([ldv[:, HD // 2:HD // 2 + 1], ldv[:, HD + HD // 2:HD + HD // 2 + 1]], axis=0)
                s = lax.dot_general(q2, kk, NT, preferred_element_type=F32)
                p = jnp.where(jnp.concatenate([mask, mask], axis=0), jnp.exp(s - lse2), 0.0)
                dp = lax.dot_general(do2, vv, NT, preferred_element_type=F32)
                ds = (p * (dp - dl2)).astype(BF16)
                dq_acc = _unstack_heads(jnp.dot(ds, kk, preferred_element_type=F32), m0)
                dk_acc = lax.dot_general(ds, q2, TN, preferred_element_type=F32)
                dv_acc = lax.dot_general(p.astype(BF16), do2, TN, preferred_element_type=F32)
                if i == 0:
                    accq[cur, :] = dq_acc
                    acck[cur, :] = dk_acc[BLK:2 * BLK]
                    accv[cur, :] = dv_acc[BLK:2 * BLK]
                    acck[prev, :] += dk_acc[0:BLK]
                    accv[prev, :] += dv_acc[0:BLK]
                elif seg == 1:
                    accq[cur, :] += dq_acc
                    acck[cur, :] += dk_acc
                    accv[cur, :] += dv_acc
                else:
                    accq[cur, :] += dq_acc
                    acck[prev, :] += dk_acc[0:BLK]
                    acck[cur, :] += dk_acc[BLK:2 * BLK]
                    accv[prev, :] += dv_acc[0:BLK]
                    accv[cur, :] += dv_acc[BLK:2 * BLK]
                return c

            lax.fori_loop(0, NB, blk, 0, unroll=8)

        def norm_bwd(x_ref, r_ref, dn, gain):
            r = r_ref[...]
            xhat = x_ref[...].astype(F32) * r
            dxhat = dn * gain
            dx = r * (dxhat - xhat * (_head_sums(dxhat * xhat) * (1.0 / HD)))
            return dx, jnp.sum(dn * xhat, axis=0, keepdims=True)

        dq, dgq = norm_bwd(q_ref, rq, accq[...], qg_ref[...] * scale)
        dk, dgk = norm_bwd(k_ref, rk, acck[...], kg_ref[...])
        dq_ref[...] = dq.astype(BF16)
        dk_ref[...] = dk.astype(BF16)
        dv_ref[...] = accv[...].astype(BF16)
        dgn_ref[...] = jnp.concatenate([dgq * scale, dgk, jnp.zeros((6, pair), F32)], axis=0)[None]

    blk_spec = lambda off: pl.BlockSpec((S, pair), lambda b, p, off=off: (b, off + p))
    gspec = pl.BlockSpec((1, pair), lambda b, p: (0, 0))
    np_ = DA // pair
    return _pallas(
        body, (u, u, u, attn, dattn, lse, qg2, kg2), grid=(B, np_),
        in_specs=[blk_spec(0), blk_spec(np_), blk_spec(2 * np_), blk_spec(0), blk_spec(0), blk_spec(0),
                  gspec, gspec],
        out_specs=[blk_spec(0), blk_spec(0), blk_spec(0),
                   pl.BlockSpec((1, 8, pair), lambda b, p: (b * np_ + p, 0, 0))],
        out_shape=[jax.ShapeDtypeStruct((T, DA), BF16)] * 3 + [jax.ShapeDtypeStruct((B * np_, 8, pair), F32)],
        scratch_shapes=[pltpu.VMEM((S, pair), F32)] * 10,
        sem=("parallel", "parallel"), name="attn_bwd", carry=carry)


CT = 32
CPAD = 32


def _shifted(win, offsets):
    rolled, out = {}, {}
    n = win.shape[0]
    for o in offsets:
        sub = o % 8
        if sub not in rolled:
            rolled[sub] = win if sub == 0 else pltpu.roll(win, n - sub, 0)
        out[o] = rolled[sub][o - sub:o - sub + CT, :]
    return out


def _ln_fwd(y, g, b):
    mu = jnp.mean(y, axis=-1, keepdims=True)
    yc = y - mu
    rstd = lax.rsqrt(jnp.mean(yc * yc, axis=-1, keepdims=True) + EPS)
    xhat = yc * rstd
    return xhat, rstd, xhat * g + b


def _fill_glu(ca_ref, cg_ref, glu, S):
    glu[pl.ds(0, CPAD), :] = jnp.zeros((CPAD, DC), F32)

    def fill(i, c):
        rows = pl.ds(pl.multiple_of(i * 256, 256), 256)
        a = ca_ref[rows, :].astype(F32)
        gt = cg_ref[rows, :].astype(F32)
        glu[pl.ds(pl.multiple_of(CPAD + i * 256, CT), 256), :] = a * _sigmoid(gt)
        return c

    lax.fori_loop(0, S // 256, fill, 0)


def _conv_fwd(u, cw, cb, lg, lb, B, S):
    T = B * S

    def body(ca_ref, cg_ref, w_ref, b_ref, lg_ref, lb_ref, o_ref, y_ref, glu):
        _fill_glu(ca_ref, cg_ref, glu, S)

        def step(i, c):
            t0 = pl.multiple_of(i * CT, CT)
            win = glu[pl.ds(t0, 2 * CT), :]
            acc = jnp.zeros((CT, DC), F32) + b_ref[...]
            taps = _shifted(win, [k + 2 for k in range(CK)])
            for k in range(CK):
                acc = acc + taps[k + 2] * w_ref[k:k + 1, :]
            y_ref[pl.ds(t0, CT), :] = acc
            _, _, z = _ln_fwd(acc, lg_ref[...], lb_ref[...])
            o_ref[pl.ds(t0, CT), :] = (z * _sigmoid(z)).astype(BF16)
            return c

        lax.fori_loop(0, S // CT, step, 0, unroll=4)

    vec = pl.BlockSpec((1, DC), lambda b: (0, 0))
    return pl.pallas_call(
        body, grid=(B,),
        in_specs=[pl.BlockSpec((S, DC), lambda b: (b, 3)), pl.BlockSpec((S, DC), lambda b: (b, 4)),
                  pl.BlockSpec((CT, DC), lambda b: (0, 0)), vec, vec, vec],
        out_specs=[pl.BlockSpec((S, DC), lambda b: (b, 0))] * 2,
        out_shape=[jax.ShapeDtypeStruct((T, DC), BF16), jax.ShapeDtypeStruct((T, DC), F32)],
        scratch_shapes=[pltpu.VMEM((CPAD + S, DC), F32)],
        compiler_params=_cp(("parallel",)), name="conv_fwd")(u, u, cw, cb, lg, lb)


def _conv_bwd(u, y, dconv, cw, lg, lb, B, S):
    T = B * S

    def body(ca_ref, cg_ref, y_ref, dc_ref, w_ref, lg_ref, lb_ref,
             dca_ref, dcg_ref, dw_ref, ds_ref, glu, dyp, dwacc):
        _fill_glu(ca_ref, cg_ref, glu, S)
        dyp[pl.ds(S, CPAD), :] = jnp.zeros((CPAD, DC), F32)
        lgv, lbv = lg_ref[...], lb_ref[...]

        def sum8(v):
            return functools.reduce(jnp.add, [v[r:r + 8] for r in range(0, v.shape[0], 8)])

        P1 = 4 * CT

        def p1(i, carry):
            sb, sg, sl = carry
            t0 = pl.multiple_of(i * P1, P1)
            xhat, rstd, z = _ln_fwd(y_ref[pl.ds(t0, P1), :], lgv, lbv)
            sz = _sigmoid(z)
            dz = dc_ref[pl.ds(t0, P1), :].astype(F32) * (sz * (1.0 + z * (1.0 - sz)))
            dxhat = dz * lgv
            dy = rstd * (dxhat - jnp.mean(dxhat, axis=-1, keepdims=True)
                         - xhat * jnp.mean(dxhat * xhat, axis=-1, keepdims=True))
            dyp[pl.ds(t0, P1), :] = dy
            return sb + sum8(dy), sg + sum8(dz * xhat), sl + sum8(dz)

        z8 = jnp.zeros((8, DC), F32)
        sb, sg, sl = lax.fori_loop(0, S // P1, p1, (z8, z8, z8))
        rs = lambda v: jnp.sum(v, axis=0, keepdims=True)
        ds_ref[...] = jnp.concatenate([rs(sb), rs(sg), rs(sl), jnp.zeros((5, DC), F32)], axis=0)[None]

        def p2(i, c):
            t0 = pl.multiple_of(i * CT, CT)
            win = dyp[pl.ds(t0, 2 * CT), :]
            acc = jnp.zeros((CT, DC), F32)
            taps = _shifted(win, [30 - k for k in range(CK)])
            for k in range(CK):
                acc = acc + taps[30 - k] * w_ref[k:k + 1, :]
            a = ca_ref[pl.ds(t0, CT), :].astype(F32)
            sgt = _sigmoid(cg_ref[pl.ds(t0, CT), :].astype(F32))
            dca_ref[pl.ds(t0, CT), :] = (acc * sgt).astype(BF16)
            dcg_ref[pl.ds(t0, CT), :] = (acc * a * sgt * (1.0 - sgt)).astype(BF16)
            return c

        lax.fori_loop(0, S // CT, p2, 0)

        dwacc[...] = jnp.zeros_like(dwacc)

        def p3(i, c):
            t0 = pl.multiple_of(i * CT, CT)
            win = glu[pl.ds(t0, 2 * CT), :]
            dy = dyp[pl.ds(t0, CT), :]
            for k in range(CK):
                dwacc[k] += sum8(dy * win[k + 2:k + 2 + CT, :])
            return c

        lax.fori_loop(0, S // CT, p3, 0)
        dw_ref[...] = jnp.sum(dwacc[...], axis=1)[None]

    vec = pl.BlockSpec((1, DC), lambda b: (0, 0))
    seq = pl.BlockSpec((S, DC), lambda b: (b, 0))
    return pl.pallas_call(
        body, grid=(B,),
        in_specs=[pl.BlockSpec((S, DC), lambda b: (b, 3)), pl.BlockSpec((S, DC), lambda b: (b, 4)),
                  seq, seq, pl.BlockSpec((CT, DC), lambda b: (0, 0)), vec, vec],
        out_specs=[seq, seq, pl.BlockSpec((1, CT, DC), lambda b: (b, 0, 0)),
                   pl.BlockSpec((1, 8, DC), lambda b: (b, 0, 0))],
        out_shape=[jax.ShapeDtypeStruct((T, DC), BF16)] * 2
                  + [jax.ShapeDtypeStruct((B, CT, DC), F32), jax.ShapeDtypeStruct((B, 8, DC), F32)],
        scratch_shapes=[pltpu.VMEM((CPAD + S, DC), F32), pltpu.VMEM((S + CPAD, DC), F32),
                        pltpu.VMEM((CT, 8, DC), F32)],
        compiler_params=_cp(("parallel",)), name="conv_bwd")(u, u, y, dconv, cw, lg, lb)


def _local_step(x, target, norms, W, B, S, comm=None):
    qg2 = jnp.concatenate([norms["q_norm"], norms["q_norm"]], axis=1)
    kg2 = jnp.concatenate([norms["k_norm"], norms["k_norm"]], axis=1)
    cw = jnp.concatenate([W["conv_w"], jnp.zeros((1, DC), F32)], axis=0)

    W = dict(W)
    (n1, g1, u1, act1), got = _ffn_gate_up(x, norms["ffn1_norm"], W["wg1"], W["wu1"], "ffn1_gate_up",
                                           carry=comm.gathers["down_in"] if comm else None)
    if comm:
        W.update(comm.gathered("down_in", got))
    h1, u, n2 = _ffn_down_mix_in(x, act1, W["wd1"], norms["mix_norm"], W["win"], "ffn1_down_mix_in")
    (attn, lse), got = _attn_fwd(u, qg2, kg2, B, S, carry=comm.gathers["ffn2"] if comm else None)
    if comm:
        W = dict(W, **comm.gathered("ffn2", got))
    conv, y = _conv_fwd(u, cw, norms["conv_b"], norms["conv_ln_g"], norms["conv_ln_b"], B, S)
    h2, n3, g2, u2, dout, dyb, sq, act2 = _mix_out_ffn_loss(h1, attn, conv, W["wout"], norms["ffn2_norm"],
                                                            W["wg2"], W["wu2"], W["wd2"], target, "ffn2_fwd")
    loss = (0.5 / D) * jnp.sum(sq)

    (dg2, du2, dh2, dgn_ffn2), _ = _ffn_bwd_act(dyb, g2, u2, h2, dout, norms["ffn2_norm"],
                                               W["wg2"], W["wu2"], W["wd2"], "ffn2_bwd_act")
    dwd2, _ = _ffn_bwd_w(act2, dyb, "ffn2_bwd_wd")
    dwg2, _ = _ffn_bwd_w(dg2, n3, "ffn2_bwd_wg")
    dwu2, _ = _ffn_bwd_w(du2, n3, "ffn2_bwd_wu")
    dattn, dconv, dwout = _mix_out_bwd(dh2, attn, conv, W["wout"])
    carry = comm.reduce_start("ffn2", {"wg2": dwg2, "wu2": dwu2, "wd2": dwd2, "wout": dwout}) if comm else None
    (dq, dk, dv, dgn_qk), got = _attn_bwd(u, attn, dattn, lse, qg2, kg2, B, S, carry=carry)
    if comm:
        comm.reduce_done(carry, got)
    dca, dcg, dcw, dcs = _conv_bwd(u, y, dconv, cw, norms["conv_ln_g"], norms["conv_ln_b"], B, S)
    dwin, dh1, dyb1, dgn_mix = _mix_in_bwd((dq, dk, dv, dca, dcg), W["win"], n2, h1, dh2, norms["mix_norm"])
    (dg1, du1, gx, dgn_ffn1), _ = _ffn_bwd_act(dyb1, g1, u1, x, dh1, norms["ffn1_norm"],
                                              W["wg1"], W["wu1"], W["wd1"], "ffn1_bwd_act")
    carry = comm.reduce_start("win", {"win": dwin}) if comm else None
    dwd1, got = _ffn_bwd_w(act1, dyb1, "ffn1_bwd_wd", carry=carry)
    if comm:
        comm.reduce_done(carry, got)
        carry = comm.reduce_start("wd1", {"wd1": dwd1})
    dwg1, got = _ffn_bwd_w(dg1, n1, "ffn1_bwd_wg", carry=carry)
    if comm:
        comm.reduce_done(carry, got)
        carry = comm.reduce_start("wg1", {"wg1": dwg1})
    dwu1, got = _ffn_bwd_w(du1, n1, "ffn1_bwd_wu", carry=carry)
    if comm:
        comm.reduce_done(carry, got)
        comm.last = comm.reduce_start("wu1", {"wu1": dwu1})

    qk = jnp.sum(dgn_qk, axis=0)
    cs = jnp.sum(dcs, axis=0)
    small = {
        "ffn1_norm": jnp.sum(dgn_ffn1, axis=0),
        "mix_norm": jnp.sum(dgn_mix, axis=0),
        "q_norm": qk[0:1, 0:HD] + qk[0:1, HD:2 * HD],
        "k_norm": qk[1:2, 0:HD] + qk[1:2, HD:2 * HD],
        "conv_w": jnp.sum(dcw, axis=0)[0:CK],
        "conv_b": cs[0:1],
        "conv_ln_g": cs[1:2],
        "conv_ln_b": cs[2:3],
        "ffn2_norm": jnp.sum(dgn_ffn2, axis=0),
    }
    big = {"wg1": dwg1, "wu1": dwu1, "wd1": dwd1, "win": dwin, "wout": dwout,
           "wg2": dwg2, "wu2": dwu2, "wd2": dwd2}
    return loss, gx, big, small


HBM = pl.BlockSpec(memory_space=pltpu.HBM)
VMEM = pl.BlockSpec(memory_space=pltpu.VMEM)


def _place():
    return lax.axis_index("x"), lax.axis_index("y"), lax.axis_index("c")


class _GatherCarry:
    def __init__(self, shards, mid_at=0.5):
        nt = len(shards)
        self.mid_at = mid_at
        self.shards = shards
        self.in_arrays = [s for s, _ in shards]
        self.in_specs = [VMEM] * nt
        self.out_shape = [jax.ShapeDtypeStruct((NDEV * s.shape[0], s.shape[1]), dt) for s, dt in shards]
        self.out_specs = [HBM] * nt
        self.scratch = ([pltpu.VMEM(s.shape, dt) for s, dt in shards]
                        + [pltpu.SemaphoreType.DMA((nt, 7)), pltpu.SemaphoreType.DMA((nt, 7)),
                           pltpu.SemaphoreType.DMA((nt,))])

    def _copies(self, outs, scr):
        nt = len(self.shards)
        stages = scr[:nt]
        send_sems, recv_sems, local_sems = scr[nt:]
        x, y, c = _place()
        me, sibling = (x, y, c), (x, y, 1 - c)
        xn, yn, diag = (1 - x, y, c), (x, 1 - y, c), (1 - x, 1 - y, c)
        via = (x ^ c, y ^ (1 - c), c)
        onto = (x ^ (1 - c), y ^ c, c)

        def rows(t, px, py, pc):
            r = self.shards[t][0].shape[0]
            return outs[t].at[pl.ds((4 * px + 2 * py + pc) * r, r), :]

        def copy(t, k, block, to, src=None):
            return pltpu.make_async_remote_copy(
                src_ref=rows(t, *block) if src is None else src, dst_ref=rows(t, *block),
                send_sem=send_sems.at[t, k], recv_sem=recv_sems.at[t, k],
                device_id=to, device_id_type=MESH)

        sib = lambda b: (b[0], b[1], 1 - c)
        return dict(
            local=[pltpu.make_async_copy(stages[t], rows(t, *me), local_sems.at[t]) for t in range(nt)],
            own=[[copy(t, 0, me, sibling, src=stages[t]), copy(t, 1, me, xn, src=stages[t]),
                  copy(t, 2, me, yn, src=stages[t])] for t in range(nt)],
            relay=[copy(t, 3, via, onto) for t in range(nt)],
            down=[[copy(t, 4, xn, sibling), copy(t, 5, yn, sibling)] for t in range(nt)],
            down_diag=[copy(t, 6, diag, sibling) for t in range(nt)],
            got_xy=[[copy(t, 1, xn, me), copy(t, 2, yn, me)] for t in range(nt)],
            got_diag=[copy(t, 3, diag, me) for t in range(nt)],
            got_sib=[[copy(t, 0, sibling, me), copy(t, 4, sib(xn), me), copy(t, 5, sib(yn), me),
                      copy(t, 6, sib(diag), me)] for t in range(nt)])

    def start(self, ins, outs, scr):
        cps = self._copies(outs, scr)
        for t, (_, dt) in enumerate(self.shards):
            scr[t][...] = ins[t][...].astype(dt)
            for cp in [cps["local"][t]] + cps["own"][t]:
                cp.start()

    def stages(self):
        sizes = [s.size * jnp.dtype(dt).itemsize for s, dt in self.shards]
        done = [sum(sizes[:t + 1]) / sum(sizes) for t in range(len(sizes))]
        return [(self.mid_at * f, functools.partial(self.mid, t)) for t, f in enumerate(done)]

    def mid(self, t, ins, outs, scr):
        cps = self._copies(outs, scr)
        for cp in cps["got_xy"][t]:
            cp.wait_recv()
        for cp in [cps["relay"][t]] + cps["down"][t]:
            cp.start()

    def finish(self, ins, outs, scr):
        cps = self._copies(outs, scr)
        for t in range(len(self.shards)):
            cps["got_diag"][t].wait_recv()
            cps["down_diag"][t].start()
        for t in range(len(self.shards)):
            for cp in cps["got_sib"][t]:
                cp.wait_recv()
            for cp in cps["own"][t] + [cps["relay"][t]] + cps["down"][t] + [cps["down_diag"][t]]:
                cp.wait_send()
            cps["local"][t].wait()


def _run_carry(carry, name):
    def body(*refs):
        n_in, n_out = len(carry.in_arrays), len(carry.out_shape)
        ins, outs, scr = refs[:n_in], refs[n_in:n_in + n_out], refs[n_in + n_out:]
        carry.start(ins, outs, scr)
        for _, stage in carry.stages():
            stage(ins, outs, scr)
        carry.finish(ins, outs, scr)

    return pl.pallas_call(
        body, in_specs=carry.in_specs, out_specs=carry.out_specs, out_shape=carry.out_shape,
        scratch_shapes=carry.scratch, compiler_params=pltpu.CompilerParams(vmem_limit_bytes=VMEM_LIMIT),
        name=name)(*carry.in_arrays)


class _ExchangeCarry:
    def __init__(self, names, grads, mid_at=0.5):
        nt = len(grads)
        self.mid_at = mid_at
        self.names = names
        self.in_arrays = [g.reshape(4, 2, g.shape[0] // NDEV, g.shape[1]) for g in grads]
        self.in_specs = [HBM] * nt
        blocks = [g.shape[2:] for g in self.in_arrays]
        self.out_shape = [jax.ShapeDtypeStruct((3,) + b, BF16) for b in blocks]
        self.out_specs = [HBM] * nt
        self.scratch = ([pltpu.VMEM((4,) + b, BF16) for b in blocks] * 2 + [pltpu.VMEM(b, BF16) for b in blocks]
                        + [pltpu.SemaphoreType.DMA((nt, 3)), pltpu.SemaphoreType.DMA((nt, 3))]
                        + [pltpu.SemaphoreType.DMA((nt,))] * 4)

    def _copies(self, ins, outs, scr):
        nt = len(ins)
        theirs, own, relayed = scr[:nt], scr[nt:2 * nt], scr[2 * nt:3 * nt]
        send_sems, recv_sems, keep_sems, load_sems, swap_send, swap_recv = scr[3 * nt:]
        x, y, c = _place()
        q = lambda cx, cy: 2 * cx + cy
        near, far = (x ^ c, y ^ (1 - c)), (x ^ (1 - c), y ^ c)

        def remote(t, k, src, dst, chip):
            return pltpu.make_async_remote_copy(
                src_ref=src, dst_ref=dst, send_sem=send_sems.at[t, k], recv_sem=recv_sems.at[t, k],
                device_id=(*chip, c), device_id_type=MESH)

        return dict(
            swap=[pltpu.make_async_remote_copy(
                src_ref=ins[t].at[:, 1 - c], dst_ref=theirs[t], send_sem=swap_send.at[t], recv_sem=swap_recv.at[t],
                device_id=(x, y, 1 - c), device_id_type=MESH) for t in range(nt)],
            load=[pltpu.make_async_copy(ins[t].at[:, c], own[t], load_sems.at[t]) for t in range(nt)],
            keep=[pltpu.make_async_copy(own[t].at[q(x, y)], outs[t].at[0], keep_sems.at[t]) for t in range(nt)],
            direct=[remote(t, 0, own[t].at[q(*near)], outs[t].at[1], near) for t in range(nt)],
            relay=[remote(t, 1, own[t].at[q(1 - x, 1 - y)], relayed[t], near) for t in range(nt)],
            merged=[remote(t, 2, own[t].at[q(*far)], outs[t].at[2], far) for t in range(nt)],
            theirs=theirs, own=own, relayed=relayed, far=q(*far))

    EARLY_AT = 0.1

    def stages(self):
        return [(self.EARLY_AT, self.early), (self.mid_at, self.mid)]

    def start(self, ins, outs, scr):
        cps = self._copies(ins, outs, scr)
        for t in range(len(ins)):
            cps["swap"][t].start()
            cps["load"][t].start()

    def early(self, ins, outs, scr):
        cps = self._copies(ins, outs, scr)
        for t in range(len(ins)):
            cps["load"][t].wait()
            cps["swap"][t].wait_recv()
            own, theirs = cps["own"][t], cps["theirs"][t]
            for j in range(4):
                own[j] = (own[j].astype(F32) + theirs[j].astype(F32)).astype(BF16)
            for kind in ("relay", "direct", "keep"):
                cps[kind][t].start()

    def mid(self, ins, outs, scr):
        cps = self._copies(ins, outs, scr)
        for t in range(len(ins)):
            cps["relay"][t].wait_recv()
            own, far = cps["own"][t], cps["far"]
            own[far] = (own[far].astype(F32) + cps["relayed"][t][...].astype(F32)).astype(BF16)
            cps["merged"][t].start()

    def finish(self, ins, outs, scr):
        cps = self._copies(ins, outs, scr)
        for t in range(len(ins)):
            cps["swap"][t].wait_send()
            cps["direct"][t].wait()
            cps["relay"][t].wait_send()
            cps["merged"][t].wait()
            cps["keep"][t].wait()


class _Comm:
    def __init__(self, groups):
        self.names = {tag: list(g) for tag, (g, _) in groups.items()}
        self.gathers = {tag: _GatherCarry(list(g.values()), mid_at) for tag, (g, mid_at) in groups.items()}
        self.reduced = {}

    def gathered(self, tag, outs):
        return dict(zip(self.names[tag], outs))

    def reduce_start(self, tag, grads, mid_at=0.5):
        names = list(grads)
        return _ExchangeCarry(names, [grads[n] for n in names], mid_at)

    def reduce_done(self, carry, outs):
        self.reduced.update(zip(carry.names, outs))


def _adamw_math(w, g, m, v):
    m = B1 * m + (1.0 - B1) * g
    v = B2 * v + (1.0 - B2) * (g * g)
    m_hat = m / (1.0 - B1 ** STEP)
    v_hat = v / (1.0 - B2 ** STEP)
    delta = -LR * (m_hat / (jnp.sqrt(v_hat) + AEPS) + WD * w)
    return delta, m, v


def _adamw_big(recv, w, m, v, name):
    def body(r_ref, w_ref, m_ref, v_ref, g_ref, d_ref, mo_ref, vo_ref):
        g = r_ref[0].astype(F32)
        for q in range(1, 3):
            g = g + r_ref[q].astype(F32)
        d, mn, vn = _adamw_math(w_ref[...], g, m_ref[...], v_ref[...])
        g_ref[...] = g
        d_ref[...] = d
        mo_ref[...] = mn
        vo_ref[...] = vn

    rows, n = w.shape
    tr = rows // 2
    row = pl.BlockSpec((tr, n), lambda t: (t, 0))
    return pl.pallas_call(
        body, grid=(2,), in_specs=[pl.BlockSpec((3, tr, n), lambda t: (0, t, 0)), row, row, row],
        out_specs=[row] * 4, out_shape=[jax.ShapeDtypeStruct(w.shape, F32)] * 4,
        compiler_params=_cp(("parallel",)), name=name)(recv, w, m, v)


SMALL_NAMES = ("ffn1_norm", "mix_norm", "ffn2_norm", "conv_b", "conv_ln_g", "conv_ln_b", "q_norm", "k_norm")
SROWS = 16
LOSS_ROW = len(SMALL_NAMES)
CWF = 2


def _small_step(gs, loss_row, gcw, ws, ms, vs, wcw, mcw, vcw, carry=None):
    ns = len(SMALL_NAMES)
    widths = [g.shape[1] for g in gs]

    def body(*refs):
        it = iter(refs)
        take = lambda n: [next(it) for _ in range(n)]
        g_refs, (loss_ref, gcw_ref) = take(ns), take(2)
        w_refs, m_refs, v_refs = take(ns), take(ns), take(ns)
        wcw_ref, mcw_ref, vcw_ref = take(3)
        cins = take(len(carry.in_arrays)) if carry else []
        outs = [take(4) for _ in range(ns)]
        cw_outs, (loss_out,) = take(4), take(1)
        couts = take(len(carry.out_shape)) if carry else []
        send, slots, cslots, send_sems, recv_sems, csend_sems, crecv_sems = take(7)
        cscr = list(it)
        x, y, c = _place()
        me = 4 * x + 2 * y + c
        send[...] = jnp.zeros_like(send)
        for k in range(ns):
            send[k:k + 1, 0:widths[k]] = g_refs[k][...]
        send[LOSS_ROW:LOSS_ROW + 1, 0:128] = loss_ref[...]
        slots[me] = send[...]
        cslots[me] = gcw_ref[...]
        cps = []
        for k in range(1, NDEV):
            peer = (x ^ ((k >> 2) & 1), y ^ ((k >> 1) & 1), c ^ (k & 1))
            cps.append(pltpu.make_async_remote_copy(
                src_ref=send, dst_ref=slots.at[me], send_sem=send_sems.at[k - 1], recv_sem=recv_sems.at[k - 1],
                device_id=peer, device_id_type=MESH))
            cps.append(pltpu.make_async_remote_copy(
                src_ref=gcw_ref, dst_ref=cslots.at[me], send_sem=csend_sems.at[k - 1],
                recv_sem=crecv_sems.at[k - 1], device_id=peer, device_id_type=MESH))
        for cp in cps:
            cp.start()
        stages = [stage for _, stage in carry.stages()] if carry else []
        if carry:
            carry.start(cins, couts, cscr)
        for stage in stages[:1]:
            stage(cins, couts, cscr)
        for cp in cps:
            cp.wait()
        tot = slots[0]
        ctot = cslots[0, me]
        for j in range(1, NDEV):
            tot = tot + slots[j]
            ctot = ctot + cslots[j, me]

        def step(g, w_ref, m_ref, v_ref, o):
            d, mn, vn = _adamw_math(w_ref[...], g, m_ref[...], v_ref[...])
            o[0][...], o[1][...], o[2][...], o[3][...] = g, d, mn, vn

        for k in range(ns):
            step(tot[k:k + 1, 0:widths[k]], w_refs[k], m_refs[k], v_refs[k], outs[k])
        step(ctot, wcw_ref, mcw_ref, vcw_ref, cw_outs)
        loss_out[...] = tot[LOSS_ROW:LOSS_ROW + 1, 0:128]
        for stage in stages[1:]:
            stage(cins, couts, cscr)
        if carry:
            carry.finish(cins, couts, cscr)

    args = [*gs, loss_row, gcw, *ws, *ms, *vs, wcw, mcw, vcw]
    out_shape = ([jax.ShapeDtypeStruct((1, n), F32) for n in widths for _ in range(4)]
                 + [jax.ShapeDtypeStruct((CWF, D), F32)] * 4 + [jax.ShapeDtypeStruct((1, 128), F32)])
    n_own = len(out_shape)
    res = pl.pallas_call(
        body, in_specs=[VMEM] * len(args) + (carry.in_specs if carry else []),
        out_specs=[VMEM] * n_own + (carry.out_specs if carry else []),
        out_shape=out_shape + (carry.out_shape if carry else []),
        scratch_shapes=[pltpu.VMEM((SROWS, D), F32), pltpu.VMEM((NDEV, SROWS, D), F32),
                        pltpu.VMEM((NDEV, NDEV, CWF, D), F32)]
                       + [pltpu.SemaphoreType.DMA((NDEV - 1,))] * 4 + (carry.scratch if carry else []),
        name="small_step")(*args, *(carry.in_arrays if carry else []))
    per = [res[4 * k:4 * k + 4] for k in range(ns)]
    return per, res[4 * ns:4 * ns + 4], res[n_own - 1], res[n_own:]


def _pack_cw(a):
    flat = a.reshape(a.shape[:-2] + (CK * HD,))
    pad = [(0, 0)] * (flat.ndim - 1) + [(0, CWF * D - CK * HD)]
    return jnp.pad(flat, pad).reshape(a.shape[:-2] + (CWF, D))


def _unpack_cw(v):
    return v.reshape(-1)[:CK * HD].reshape(1, CK, HD)


def kernel(x, ffn1_norm, ffn1_w_gate, ffn1_w_up, ffn1_w_down, mix_norm, w_in, q_norm, k_norm, conv_w, conv_b, conv_ln_g, conv_ln_b, w_out, ffn2_norm, ffn2_w_gate, ffn2_w_up, ffn2_w_down, loss_target, m_ffn1_norm, m_ffn1_w_gate, m_ffn1_w_up, m_ffn1_w_down, m_mix_norm, m_w_in, m_q_norm, m_k_norm, m_conv_w, m_conv_b, m_conv_ln_g, m_conv_ln_b, m_w_out, m_ffn2_norm, m_ffn2_w_gate, m_ffn2_w_up, m_ffn2_w_down, v_ffn1_norm, v_ffn1_w_gate, v_ffn1_w_up, v_ffn1_w_down, v_mix_norm, v_w_in, v_q_norm, v_k_norm, v_conv_w, v_conv_b, v_conv_ln_g, v_conv_ln_b, v_w_out, v_ffn2_norm, v_ffn2_w_gate, v_ffn2_w_up, v_ffn2_w_down):
    P = dict(ffn1_norm=ffn1_norm, ffn1_w_gate=ffn1_w_gate, ffn1_w_up=ffn1_w_up, ffn1_w_down=ffn1_w_down,
             mix_norm=mix_norm, w_in=w_in, q_norm=q_norm, k_norm=k_norm, conv_w=conv_w, conv_b=conv_b,
             conv_ln_g=conv_ln_g, conv_ln_b=conv_ln_b, w_out=w_out, ffn2_norm=ffn2_norm,
             ffn2_w_gate=ffn2_w_gate, ffn2_w_up=ffn2_w_up, ffn2_w_down=ffn2_w_down)
    M = dict(ffn1_norm=m_ffn1_norm, ffn1_w_gate=m_ffn1_w_gate, ffn1_w_up=m_ffn1_w_up, ffn1_w_down=m_ffn1_w_down,
             mix_norm=m_mix_norm, w_in=m_w_in, q_norm=m_q_norm, k_norm=m_k_norm, conv_w=m_conv_w, conv_b=m_conv_b,
             conv_ln_g=m_conv_ln_g, conv_ln_b=m_conv_ln_b, w_out=m_w_out, ffn2_norm=m_ffn2_norm,
             ffn2_w_gate=m_ffn2_w_gate, ffn2_w_up=m_ffn2_w_up, ffn2_w_down=m_ffn2_w_down)
    V = dict(ffn1_norm=v_ffn1_norm, ffn1_w_gate=v_ffn1_w_gate, ffn1_w_up=v_ffn1_w_up, ffn1_w_down=v_ffn1_w_down,
             mix_norm=v_mix_norm, w_in=v_w_in, q_norm=v_q_norm, k_norm=v_k_norm, conv_w=v_conv_w, conv_b=v_conv_b,
             conv_ln_g=v_conv_ln_g, conv_ln_b=v_conv_ln_b, w_out=v_w_out, ffn2_norm=v_ffn2_norm,
             ffn2_w_gate=v_ffn2_w_gate, ffn2_w_up=v_ffn2_w_up, ffn2_w_down=v_ffn2_w_down)
    order = ["ffn1_norm", "ffn1_w_gate", "ffn1_w_up", "ffn1_w_down", "mix_norm", "w_in", "q_norm", "k_norm",
             "conv_w", "conv_b", "conv_ln_g", "conv_ln_b", "w_out", "ffn2_norm", "ffn2_w_gate", "ffn2_w_up",
             "ffn2_w_down"]
    B, S, _ = x.shape
    T = B * S

    bigs = [("wg1", "ffn1_w_gate", True), ("wu1", "ffn1_w_up", True), ("wd1", "ffn1_w_down", False),
            ("win", "w_in", True), ("wout", "w_out", False),
            ("wg2", "ffn2_w_gate", True), ("wu2", "ffn2_w_up", True), ("wd2", "ffn2_w_down", False)]
    hm = lambda a, tr: jnp.transpose(a[0]) if tr else a[0]
    cw_pad = jnp.zeros((32, 128), F32).at[0:CK, 0:HD].set(conv_w[0])
    shard = {ln: (hm(P[pn], tr), BF16) for ln, pn, tr in bigs}
    gathered = _run_carry(_GatherCarry([shard["wg1"], shard["wu1"], (cw_pad, F32)]), "gather_first")
    W = {"wg1": gathered[0], "wu1": gathered[1]}
    cwg = gathered[2].reshape(NDEV, 32, 128)[:, 0:CK, 0:HD]
    W["conv_w"] = jnp.transpose(cwg, (1, 0, 2)).reshape(CK, DC)
    norms = {n: P[n] for n in SMALL_NAMES}
    comm = _Comm({"down_in": ({n: shard[n] for n in ("wd1", "win")}, 0.5),
                  "ffn2": ({n: shard[n] for n in ("wg2", "wu2", "wd2", "wout")}, 0.5)})

    loss_part, gx, _, small = _local_step(x.reshape(T, D), loss_target.reshape(T, D), norms, W, B, S, comm)

    G, Dl, Mn, Vn = {}, {}, {}, {}
    dcw = small["conv_w"].reshape(CK, NDEV, HD).transpose(1, 0, 2)
    loss_row = jnp.zeros((1, 128), F32).at[0, 0].set(loss_part)
    per, cw_outs, loss_out, got = _small_step(
        [small[n] for n in SMALL_NAMES], loss_row, _pack_cw(dcw),
        [P[n] for n in SMALL_NAMES], [M[n] for n in SMALL_NAMES], [V[n] for n in SMALL_NAMES],
        _pack_cw(P["conv_w"][0]), _pack_cw(M["conv_w"][0]), _pack_cw(V["conv_w"][0]), carry=comm.last)
    comm.reduce_done(comm.last, got)
    loss = loss_out[0, 0]
    for n, outs in zip(SMALL_NAMES, per):
        G[n], Dl[n], Mn[n], Vn[n] = outs
    G["conv_w"], Dl["conv_w"], Mn["conv_w"], Vn["conv_w"] = [_unpack_cw(o) for o in cw_outs]

    for ln, pn, tr in bigs:
        outs = _adamw_big(comm.reduced[ln], hm(P[pn], tr), hm(M[pn], tr), hm(V[pn], tr), "adamw_" + ln)
        G[pn], Dl[pn], Mn[pn], Vn[pn] = [(jnp.transpose(o) if tr else o)[None] for o in outs]

    return (loss, gx.reshape(B, S, D), *[G[n] for n in order], *[Dl[n] for n in order],
            *[Mn[n] for n in order], *[Vn[n] for n in order])
```

```python
import functools

import jax
import jax.numpy as jnp
from jax import lax
from jax.experimental import pallas as pl
from jax.experimental.pallas import tpu as pltpu

F32 = jnp.float32
BF16 = jnp.bfloat16

D = 1024
FF = 2816
HD = 64
DA = 512
DC = 512
DIN = 2560
CK = 31
BLK = 128
DILS = (1, 4, 16)
EPS = 1e-6
NDEV = 8
MESH = pl.DeviceIdType.MESH

LR, B1, B2, AEPS, WD, STEP = 0.001, 0.9, 0.999, 1e-08, 0.01, 10

NT = (((1,), (1,)), ((), ()))
TN = (((0,), (0,)), ((), ()))

VMEM_LIMIT = 60 * 1024 * 1024


def _cp(sem=None):
    return pltpu.CompilerParams(dimension_semantics=sem, vmem_limit_bytes=VMEM_LIMIT)


def _sigmoid(x):
    return 0.5 * (jnp.tanh(0.5 * x) + 1.0)


def _pallas(body, args, *, grid, in_specs, out_specs, out_shape, scratch_shapes, sem, name, carry=None):
    if carry is None:
        outs = pl.pallas_call(body, grid=grid, in_specs=in_specs, out_specs=out_specs, out_shape=out_shape,
                              scratch_shapes=scratch_shapes, compiler_params=_cp(sem), name=name)(*args)
        return outs, None
    n_in, n_out, n_scr = len(in_specs), len(out_shape), len(scratch_shapes)
    c_in, c_out = len(carry.in_arrays), len(carry.out_shape)

    def wrapped(*refs):
        ins, refs = refs[:n_in], refs[n_in:]
        cins, refs = refs[:c_in], refs[c_in:]
        outs, refs = refs[:n_out], refs[n_out:]
        couts, refs = refs[:c_out], refs[c_out:]
        scr, cscr = refs[:n_scr], refs[n_scr:]
        ids = [pl.program_id(a) for a in range(len(grid))]
        step = ids[0]
        for i, n in zip(ids[1:], grid[1:]):
            step = step * n + i
        steps = functools.reduce(lambda a, b: a * b, grid)

        @pl.when(step == 0)
        def _():
            carry.start(cins, couts, cscr)

        body(*ins, *outs, *scr)

        for frac, stage in carry.stages():
            @pl.when(step == int(steps * frac))
            def _(stage=stage):
                stage(cins, couts, cscr)

        @pl.when(step == steps - 1)
        def _():
            carry.finish(cins, couts, cscr)

    outs = pl.pallas_call(
        wrapped, grid=grid, in_specs=list(in_specs) + carry.in_specs, out_specs=list(out_specs) + carry.out_specs,
        out_shape=list(out_shape) + carry.out_shape, scratch_shapes=list(scratch_shapes) + carry.scratch,
        compiler_params=_cp(("arbitrary",) * len(grid)), name=name)(*args, *carry.in_arrays)
    return outs[:n_out], outs[n_out:]


FC = 256


def _resident(shape):
    return pl.BlockSpec(shape, lambda *_: (0,) * len(shape), pipeline_mode=pl.Buffered(1))


def _mix_out_ffn_loss(h1, attn, conv, wout, gain, wg, wu, wd, target, name):
    T = h1.shape[0]
    tm = 512
    nt = T // tm

    def body(h1_ref, at_ref, cv_ref, wo_ref, gain_ref, wg_ref, wu_ref, wd_ref, t_ref,
             h2_ref, n_ref, g_ref, u_ref, dout_ref, dyb_ref, sq_ref, a_hbm, a_scr, a_sem):
        t = pl.program_id(0)
        a_out = lambda i: pltpu.make_async_copy(a_scr, a_hbm.at[pl.ds(pl.multiple_of(i * tm, tm), tm), :], a_sem)

        @pl.when(t > 0)
        def _():
            a_out(t - 1).wait()

        xv = (h1_ref[...]
              + jnp.dot(at_ref[...], wo_ref[0:DA, :], preferred_element_type=F32)
              + jnp.dot(cv_ref[...], wo_ref[DA:D, :], preferred_element_type=F32))
        h2_ref[...] = xv
        r = lax.rsqrt(jnp.mean(xv * xv, axis=-1, keepdims=True) + EPS)
        n_ref[...] = (xv * r * gain_ref[...]).astype(BF16)
        for c in range(FF // FC):
            cols = slice(c * FC, (c + 1) * FC)
            nb = n_ref[...]
            g = lax.dot_general(nb, wg_ref[cols, :], NT, preferred_element_type=F32)
            u = lax.dot_general(nb, wu_ref[cols, :], NT, preferred_element_type=F32)
            g_ref[:, cols] = g.astype(BF16)
            u_ref[:, cols] = u.astype(BF16)
            a_scr[:, cols] = (g * _sigmoid(g) * u).astype(BF16)
        a_out(t).start()
        e = h2_ref[...] + 0.5 * jnp.dot(a_scr[...], wd_ref[...], preferred_element_type=F32) - t_ref[...]
        dout = e * (1.0 / D)
        dout_ref[...] = dout
        dyb_ref[...] = (0.5 * dout).astype(BF16)
        sq_ref[...] = jnp.sum(e * e, axis=0, keepdims=True)[None]

        @pl.when(t == nt - 1)
        def _():
            a_out(t).wait()

    row = pl.BlockSpec((tm, D), lambda t: (t, 0))
    half = pl.BlockSpec((tm, DA), lambda t: (t, 0))
    wide = pl.BlockSpec((tm, FF), lambda t: (t, 0))
    outs, _ = _pallas(
        body, (h1, attn, conv, wout, gain, wg, wu, wd, target), grid=(nt,),
        in_specs=[row, half, half, _resident((D, D)), _resident((1, D)), _resident((FF, D)), _resident((FF, D)),
                  _resident((FF, D)), row],
        out_specs=[row, row, wide, wide, row, row, pl.BlockSpec((1, 1, D), lambda t: (t, 0, 0)), HBM],
        out_shape=[jax.ShapeDtypeStruct((T, D), F32), jax.ShapeDtypeStruct((T, D), BF16)]
                  + [jax.ShapeDtypeStruct((T, FF), BF16)] * 2
                  + [jax.ShapeDtypeStruct((T, D), F32), jax.ShapeDtypeStruct((T, D), BF16),
                     jax.ShapeDtypeStruct((nt, 1, D), F32), jax.ShapeDtypeStruct((T, FF), BF16)],
        scratch_shapes=[pltpu.VMEM((tm, FF), BF16), pltpu.SemaphoreType.DMA(())],
        sem=("arbitrary",), name=name)
    return outs


def _ffn_gate(x, gain, wg, name, carry=None):
    T = x.shape[0]
    tm = 512

    def body(x_ref, gain_ref, wg_ref, n_ref, g_ref):
        xv = x_ref[...]
        r = lax.rsqrt(jnp.mean(xv * xv, axis=-1, keepdims=True) + EPS)
        n_ref[...] = (xv * r * gain_ref[...]).astype(BF16)
        for c in range(FF // FC):
            cols = slice(c * FC, (c + 1) * FC)
            g_ref[:, cols] = lax.dot_general(n_ref[...], wg_ref[cols, :], NT,
                                             preferred_element_type=F32).astype(BF16)

    row = pl.BlockSpec((tm, D), lambda t: (t, 0))
    wide = pl.BlockSpec((tm, FF), lambda t: (t, 0))
    return _pallas(
        body, (x, gain, wg), grid=(T // tm,),
        in_specs=[row, _resident((1, D)), _resident((FF, D))], out_specs=[row, wide],
        out_shape=[jax.ShapeDtypeStruct((T, D), BF16), jax.ShapeDtypeStruct((T, FF), BF16)],
        scratch_shapes=[], sem=("parallel",), name=name, carry=carry)


def _ffn_up(n, g, wu, name, carry=None):
    T = n.shape[0]
    tm = 512

    def body(n_ref, g_ref, wu_ref, u_ref, a_ref):
        for c in range(FF // FC):
            cols = slice(c * FC, (c + 1) * FC)
            u = lax.dot_general(n_ref[...], wu_ref[cols, :], NT, preferred_element_type=F32)
            g = g_ref[:, cols].astype(F32)
            u_ref[:, cols] = u.astype(BF16)
            a_ref[:, cols] = (g * _sigmoid(g) * u).astype(BF16)

    row = pl.BlockSpec((tm, D), lambda t: (t, 0))
    wide = pl.BlockSpec((tm, FF), lambda t: (t, 0))
    return _pallas(
        body, (n, g, wu), grid=(T // tm,),
        in_specs=[row, wide, _resident((FF, D))], out_specs=[wide, wide],
        out_shape=[jax.ShapeDtypeStruct((T, FF), BF16)] * 2,
        scratch_shapes=[], sem=("parallel",), name=name, carry=carry)


def _ffn_down(x, a, wd, name, carry=None):
    T = x.shape[0]
    tm = 512

    def body(x_ref, a_ref, wd_ref, h_ref):
        h_ref[...] = x_ref[...] + 0.5 * jnp.dot(a_ref[...], wd_ref[...], preferred_element_type=F32)

    row = pl.BlockSpec((tm, D), lambda t: (t, 0))
    return _pallas(
        body, (x, a, wd), grid=(T // tm,),
        in_specs=[row, pl.BlockSpec((tm, FF), lambda t: (t, 0)), _resident((FF, D))], out_specs=[row],
        out_shape=[jax.ShapeDtypeStruct((T, D), F32)],
        scratch_shapes=[], sem=("parallel",), name=name, carry=carry)


def _mix_in(h, gain, win, name):
    T = h.shape[0]
    tm = 512

    def body(h_ref, gain_ref, win_ref, n_ref, u_ref):
        hv = h_ref[...]
        r = lax.rsqrt(jnp.mean(hv * hv, axis=-1, keepdims=True) + EPS)
        n_ref[...] = (hv * r * gain_ref[...]).astype(BF16)
        u_ref[...] = lax.dot_general(n_ref[...], win_ref[...], NT, preferred_element_type=F32).astype(BF16)

    row = pl.BlockSpec((tm, D), lambda t: (t, 0))
    outs, _ = _pallas(
        body, (h, gain, win), grid=(T // tm,),
        in_specs=[row, _resident((1, D)), _resident((DIN, D))],
        out_specs=[row, pl.BlockSpec((tm, DIN), lambda t: (t, 0))],
        out_shape=[jax.ShapeDtypeStruct((T, D), BF16), jax.ShapeDtypeStruct((T, DIN), BF16)],
        scratch_shapes=[], sem=("parallel",), name=name)
    return outs


def _ffn_bwd_act(dyb, g, u, x, dout, gain, wg, wu, wd, name, carry=None):
    T = x.shape[0]
    tm = 256
    nt = T // tm

    def body(dy_ref, g_ref, u_ref, x_ref, dout_ref, gain_ref, wg_ref, wu_ref, wd_ref,
             dg_ref, du_ref, dx_ref, dgn_ref):
        for c in range(FF // FC):
            cols = slice(c * FC, (c + 1) * FC)
            da = lax.dot_general(dy_ref[...], wd_ref[cols, :], NT, preferred_element_type=F32)
            gv = g_ref[:, cols].astype(F32)
            uv = u_ref[:, cols].astype(F32)
            sg = _sigmoid(gv)
            dg_ref[:, cols] = (da * uv * (sg * (1.0 + gv * (1.0 - sg)))).astype(BF16)
            du_ref[:, cols] = (da * (gv * sg)).astype(BF16)
        dn = (jnp.dot(dg_ref[...], wg_ref[...], preferred_element_type=F32)
              + jnp.dot(du_ref[...], wu_ref[...], preferred_element_type=F32))
        dx, dgain = _rms_bwd_rows(dn, x_ref[...], gain_ref[...])
        dx_ref[...] = dout_ref[...] + dx
        dgn_ref[...] = dgain[None]

    row = pl.BlockSpec((tm, D), lambda t: (t, 0))
    wide = pl.BlockSpec((tm, FF), lambda t: (t, 0))
    return _pallas(
        body, (dyb, g, u, x, dout, gain, wg, wu, wd), grid=(nt,),
        in_specs=[row, wide, wide, row, row, _resident((1, D)), _resident((FF, D)), _resident((FF, D)),
                  _resident((FF, D))],
        out_specs=[wide, wide, row, pl.BlockSpec((1, 1, D), lambda t: (t, 0, 0))],
        out_shape=[jax.ShapeDtypeStruct((T, FF), BF16)] * 2
                  + [jax.ShapeDtypeStruct((T, D), F32), jax.ShapeDtypeStruct((nt, 1, D), F32)],
        scratch_shapes=[], sem=("parallel",), name=name, carry=carry)


def _ffn_bwd_w(lhs, rhs, name, carry=None):
    T = rhs.shape[0]
    tf = 256

    def body(l_ref, r_ref, dw_ref):
        dw_ref[...] = lax.dot_general(l_ref[...], r_ref[...], TN, preferred_element_type=F32).astype(BF16)

    (dw,), got = _pallas(
        body, (lhs, rhs), grid=(FF // tf,),
        in_specs=[pl.BlockSpec((T, tf), lambda f: (0, f)), _resident((T, D))],
        out_specs=[pl.BlockSpec((tf, D), lambda f: (f, 0))], out_shape=[jax.ShapeDtypeStruct((FF, D), BF16)],
        scratch_shapes=[], sem=("parallel",), name=name, carry=carry)
    return dw, got


def _rms_bwd_rows(dn, xv, gain):
    r = lax.rsqrt(jnp.mean(xv * xv, axis=-1, keepdims=True) + EPS)
    xhat = xv * r
    dxhat = dn * gain
    dx = r * (dxhat - xhat * jnp.mean(dxhat * xhat, axis=-1, keepdims=True))
    return dx, jnp.sum(dn * xhat, axis=0, keepdims=True)


def _mix_out_bwd(dh, attn, conv, wout):
    T = dh.shape[0]
    tm = 512
    nt = T // tm

    def body(dh_ref, a_ref, c_ref, w_ref, da_ref, dc_ref, dw_ref, acc_scr):
        t = pl.program_id(0)

        @pl.when(t == 0)
        def _():
            acc_scr[...] = jnp.zeros_like(acc_scr)

        dhb = dh_ref[...].astype(BF16)
        dmix = lax.dot_general(dhb, w_ref[...], NT, preferred_element_type=F32)
        da_ref[...] = dmix[:, 0:DA].astype(BF16)
        dc_ref[...] = dmix[:, DA:D].astype(BF16)
        acc_scr[0:DA, :] += lax.dot_general(a_ref[...], dhb, TN, preferred_element_type=F32)
        acc_scr[DA:D, :] += lax.dot_general(c_ref[...], dhb, TN, preferred_element_type=F32)

        @pl.when(t == nt - 1)
        def _():
            dw_ref[...] = acc_scr[...].astype(BF16)

    row = pl.BlockSpec((tm, D), lambda t: (t, 0))
    half = pl.BlockSpec((tm, DA), lambda t: (t, 0))
    full = pl.BlockSpec((D, D), lambda t: (0, 0))
    return pl.pallas_call(
        body, grid=(nt,), in_specs=[row, half, half, full], out_specs=[half, half, full],
        out_shape=[jax.ShapeDtypeStruct((T, DA), BF16)] * 2 + [jax.ShapeDtypeStruct((D, D), BF16)],
        scratch_shapes=[pltpu.VMEM((D, D), F32)],
        compiler_params=_cp(("arbitrary",)), name="mix_out_bwd")(dh, attn, conv, wout)


def _mix_in_bwd(dparts, win, nb, h, dh, gain):
    T = h.shape[0]
    tm = 512
    nt = T // tm

    def body(d0, d1, d2, d3, d4, w_ref, n_ref, h_ref, dh_ref, gain_ref,
             dw_ref, dx_ref, dyb_ref, dg_ref, acc_scr):
        t = pl.program_id(0)

        @pl.when(t == 0)
        def _():
            acc_scr[...] = jnp.zeros_like(acc_scr)

        n = n_ref[...]
        dn = jnp.zeros((tm, D), F32)
        for i, d_ref in enumerate((d0, d1, d2, d3, d4)):
            dv = d_ref[...]
            dn = dn + jnp.dot(dv, w_ref[i * DA:(i + 1) * DA, :], preferred_element_type=F32)
            acc_scr[i * DA:(i + 1) * DA, :] += lax.dot_general(dv, n, TN, preferred_element_type=F32)
        dx, dgain = _rms_bwd_rows(dn, h_ref[...], gain_ref[...])
        tot = dh_ref[...] + dx
        dx_ref[...] = tot
        dyb_ref[...] = (0.5 * tot).astype(BF16)
        dg_ref[...] = dgain[None]

        @pl.when(t == nt - 1)
        def _():
            dw_ref[...] = acc_scr[...].astype(BF16)

    row = pl.BlockSpec((tm, D), lambda t: (t, 0))
    half = pl.BlockSpec((tm, DA), lambda t: (t, 0))
    full = pl.BlockSpec((DIN, D), lambda t: (0, 0))
    return pl.pallas_call(
        body, grid=(nt,),
        in_specs=[half] * 5 + [full, row, row, row, pl.BlockSpec((1, D), lambda t: (0, 0))],
        out_specs=[full, row, row, pl.BlockSpec((1, 1, D), lambda t: (t, 0, 0))],
        out_shape=[jax.ShapeDtypeStruct((DIN, D), BF16), jax.ShapeDtypeStruct((T, D), F32),
                   jax.ShapeDtypeStruct((T, D), BF16), jax.ShapeDtypeStruct((nt, 1, D), F32)],
        scratch_shapes=[pltpu.VMEM((DIN, D), F32)],
        compiler_params=_cp(("arbitrary",)), name="mix_in_bwd")(*dparts, win, nb, h, dh, gain)


def _head_masks():
    lane = lax.broadcasted_iota(jnp.int32, (1, 2 * HD), 1)
    m0 = lane < HD
    return m0, jnp.logical_not(m0)


def _stack_heads(v, m0, m1):
    z = jnp.zeros_like(v)
    return jnp.concatenate([jnp.where(m0, v, z), jnp.where(m1, v, z)], axis=0)


def _unstack_heads(v2, m0):
    return jnp.where(m0, v2[0:BLK], v2[BLK:2 * BLK])


def _head_sums(xv):
    ri = lax.broadcasted_iota(jnp.int32, (2 * HD, 2 * HD), 0)
    ci = lax.broadcasted_iota(jnp.int32, (2 * HD, 2 * HD), 1)
    ones = jnp.where((ri < HD) == (ci < HD), 1.0, 0.0).astype(BF16)
    hi = xv.astype(BF16)
    lo = (xv - hi.astype(F32)).astype(BF16)
    return (jnp.dot(hi, ones, preferred_element_type=F32) + jnp.dot(lo, ones, preferred_element_type=F32))


def _head_rms(xv):
    return lax.rsqrt(_head_sums(xv * xv) * (1.0 / HD) + EPS)


def _band_mask(first):
    qi = lax.broadcasted_iota(jnp.int32, (BLK, 2 * BLK), 0)
    ci = lax.broadcasted_iota(jnp.int32, (BLK, 2 * BLK), 1)
    band = (ci >= qi) & (ci <= qi + BLK)
    return band & ((ci >= BLK) | jnp.logical_not(first))


def _block_rows(j, d, seg):
    r, n = j // seg, j % seg
    start = r + (d * BLK) * n
    first = n == 0
    prev = jnp.where(first, start, start - d * BLK)
    return pl.ds(start, BLK, stride=d), pl.ds(prev, BLK, stride=d), first


def _block_keys(refs, cur, prev, first, single):
    if single:
        qi = lax.broadcasted_iota(jnp.int32, (BLK, BLK), 0)
        ci = lax.broadcasted_iota(jnp.int32, (BLK, BLK), 1)
        return [r[cur, :].astype(BF16) for r in refs], ci <= qi
    return ([jnp.concatenate([r[prev, :], r[cur, :]], axis=0).astype(BF16) for r in refs], _band_mask(first))


def _attn_fwd(u, qg2, kg2, B, S, carry=None):
    T = B * S
    NB = S // BLK
    scale = HD ** -0.5

    def body(q_ref, k_ref, v_ref, qg_ref, kg_ref, o_ref, lse_ref, qn, kn, vn, os_, ls_):
        m0, m1 = _head_masks()
        qv = q_ref[...].astype(F32)
        qn[...] = qv * _head_rms(qv) * (qg_ref[...] * scale)
        kv = k_ref[...].astype(F32)
        kn[...] = kv * _head_rms(kv) * kg_ref[...]
        vn[...] = v_ref[...].astype(F32)

        for i, d in enumerate(DILS):
            seg = NB // d

            def blk(j, c, i=i, d=d, seg=seg):
                cur, prev, first = _block_rows(j, d, seg)
                q2 = _stack_heads(qn[cur, :].astype(BF16), m0, m1)
                (kk, vv), mask = _block_keys((kn, vn), cur, prev, first, False)
                s = lax.dot_general(q2, kk, NT, preferred_element_type=F32)
                s = jnp.where(jnp.concatenate([mask, mask], axis=0), s, -1e30)
                mx = jnp.max(s, axis=-1, keepdims=True)
                p = jnp.exp(s - mx)
                l = jnp.sum(p, axis=-1, keepdims=True)
                o2 = jnp.dot((p * (1.0 / l)).astype(BF16), vv, preferred_element_type=F32)
                os_[i, cur, :] = _unstack_heads(o2, m0)
                ls_[i, cur, :] = _unstack_heads(mx + jnp.log(l), m0)
                return c

            lax.fori_loop(0, NB, blk, 0, unroll=8)

        def comb(c, carry):
            rows = pl.ds(pl.multiple_of(c * 256, 256), 256)
            l0, l1, l2 = ls_[0, rows, :], ls_[1, rows, :], ls_[2, rows, :]
            mx = jnp.maximum(jnp.maximum(l0, l1), l2)
            e0, e1, e2 = jnp.exp(l0 - mx), jnp.exp(l1 - mx), jnp.exp(l2 - mx)
            tot = e0 + e1 + e2
            inv = 1.0 / tot
            o = (e0 * os_[0, rows, :] + e1 * os_[1, rows, :] + e2 * os_[2, rows, :]) * inv
            o_ref[rows, :] = o.astype(BF16)
            lse_ref[rows, :] = mx + jnp.log(tot)
            return carry

        lax.fori_loop(0, S // 256, comb, 0)

    pair = 2 * HD
    blk_spec = lambda off: pl.BlockSpec((S, pair), lambda b, p, off=off: (b, off + p))
    gspec = pl.BlockSpec((1, pair), lambda b, p: (0, 0))
    return _pallas(
        body, (u, u, u, qg2, kg2), grid=(B, DA // pair),
        in_specs=[blk_spec(0), blk_spec(DA // pair), blk_spec(2 * DA // pair), gspec, gspec],
        out_specs=[blk_spec(0), blk_spec(0)],
        out_shape=[jax.ShapeDtypeStruct((T, DA), BF16), jax.ShapeDtypeStruct((T, DA), F32)],
        scratch_shapes=[pltpu.VMEM((S, pair), F32)] * 3 + [pltpu.VMEM((3, S, pair), F32)] * 2,
        sem=("parallel", "parallel"), name="attn_fwd", carry=carry)


def _attn_bwd(u, attn, dattn, lse, qg2, kg2, B, S, carry=None):
    T = B * S
    NB = S // BLK
    scale = HD ** -0.5
    pair = 2 * HD

    def body(q_ref, k_ref, v_ref, o_ref, do_ref, lse_ref, qg_ref, kg_ref,
             dq_ref, dk_ref, dv_ref, dgn_ref,
             qn, kn, vn, don, ldl, accq, acck, accv, rq, rk):
        m0, m1 = _head_masks()
        lane = lax.broadcasted_iota(jnp.int32, (1, pair), 1)
        qv = q_ref[...].astype(F32)
        rq[...] = _head_rms(qv)
        qn[...] = qv * rq[...] * (qg_ref[...] * scale)
        kv = k_ref[...].astype(F32)
        rk[...] = _head_rms(kv)
        kn[...] = kv * rk[...] * kg_ref[...]
        vn[...] = v_ref[...].astype(F32)
        dov = do_ref[...].astype(F32)
        don[...] = dov
        ldl[...] = jnp.where((lane % HD) < HD // 2, lse_ref[...], _head_sums(dov * o_ref[...].astype(F32)))

        for i, d in enumerate(DILS):
            seg = NB // d

            def blk(j, c, i=i, d=d, seg=seg):
                cur, prev, first = _block_rows(j, d, seg)
                q2 = _stack_heads(qn[cur, :].astype(BF16), m0, m1)
                do2 = _stack_heads(don[cur, :].astype(BF16), m0, m1)
                (kk, vv), mask = _block_keys((kn, vn), cur, prev, first, seg == 1)
                ldv = ldl[cur, :]
                lse2 = jnp.concatenate([ldv[:, 0:1], ldv[:, HD:HD + 1]], axis=0)
                dl2 = jnp.concatenate([ldv[:, HD // 2:HD // 2 + 1], ldv[:, HD + HD // 2:HD + HD // 2 + 1]], axis=0)
                s = lax.dot_general(q2, kk, NT, preferred_element_type=F32)
                p = jnp.where(jnp.concatenate([mask, mask], axis=0), jnp.exp(s - lse2), 0.0)
                dp = lax.dot_general(do2, vv, NT, preferred_element_type=F32)
                ds = (p * (dp - dl2)).astype(BF16)
                dq_acc = _unstack_heads(jnp.dot(ds, kk, preferred_element_type=F32), m0)
                dk_acc = lax.dot_general(ds, q2, TN, preferred_element_type=F32)
                dv_acc = lax.dot_general(p.astype(BF16), do2, TN, preferred_element_type=F32)
                if i == 0:
                    accq[cur, :] = dq_acc
                    acck[cur, :] = dk_acc[BLK:2 * BLK]
                    accv[cur, :] = dv_acc[BLK:2 * BLK]
                    acck[prev, :] += dk_acc[0:BLK]
                    accv[prev, :] += dv_acc[0:BLK]
                elif seg == 1:
                    accq[cur, :] += dq_acc
                    acck[cur, :] += dk_acc
                    accv[cur, :] += dv_acc
                else:
                    accq[cur, :] += dq_acc
                    acck[prev, :] += dk_acc[0:BLK]
                    acck[cur, :] += dk_acc[BLK:2 * BLK]
                    accv[prev, :] += dv_acc[0:BLK]
                    accv[cur, :] += dv_acc[BLK:2 * BLK]
                return c

            lax.fori_loop(0, NB, blk, 0, unroll=8)

        def norm_bwd(x_ref, r_ref, dn, gain):
            r = r_ref[...]
            xhat = x_ref[...].astype(F32) * r
            dxhat = dn * gain
            dx = r * (dxhat - xhat * (_head_sums(dxhat * xhat) * (1.0 / HD)))
            return dx, jnp.sum(dn * xhat, axis=0, keepdims=True)

        dq, dgq = norm_bwd(q_ref, rq, accq[...], qg_ref[...] * scale)
        dk, dgk = norm_bwd(k_ref, rk, acck[...], kg_ref[...])
        dq_ref[...] = dq.astype(BF16)
        dk_ref[...] = dk.astype(BF16)
        dv_ref[...] = accv[...].astype(BF16)
        dgn_ref[...] = jnp.concatenate([dgq * scale, dgk, jnp.zeros((6, pair), F32)], axis=0)[None]

    blk_spec = lambda off: pl.BlockSpec((S, pair), lambda b, p, off=off: (b, off + p))
    gspec = pl.BlockSpec((1, pair), lambda b, p: (0, 0))
    np_ = DA // pair
    return _pallas(
        body, (u, u, u, attn, dattn, lse, qg2, kg2), grid=(B, np_),
        in_specs=[blk_spec(0), blk_spec(np_), blk_spec(2 * np_), blk_spec(0), blk_spec(0), blk_spec(0),
                  gspec, gspec],
        out_specs=[blk_spec(0), blk_spec(0), blk_spec(0),
                   pl.BlockSpec((1, 8, pair), lambda b, p: (b * np_ + p, 0, 0))],
        out_shape=[jax.ShapeDtypeStruct((T, DA), BF16)] * 3 + [jax.ShapeDtypeStruct((B * np_, 8, pair), F32)],
        scratch_shapes=[pltpu.VMEM((S, pair), F32)] * 10,
        sem=("parallel", "parallel"), name="attn_bwd", carry=carry)


CT = 32
CPAD = 32


def _shifted(win, offsets):
    rolled, out = {}, {}
    n = win.shape[0]
    for o in offsets:
        sub = o % 8
        if sub not in rolled:
            rolled[sub] = win if sub == 0 else pltpu.roll(win, n - sub, 0)
        out[o] = rolled[sub][o - sub:o - sub + CT, :]
    return out


def _ln_fwd(y, g, b):
    mu = jnp.mean(y, axis=-1, keepdims=True)
    yc = y - mu
    rstd = lax.rsqrt(jnp.mean(yc * yc, axis=-1, keepdims=True) + EPS)
    xhat = yc * rstd
    return xhat, rstd, xhat * g + b


def _fill_glu(ca_ref, cg_ref, glu, S):
    glu[pl.ds(0, CPAD), :] = jnp.zeros((CPAD, DC), F32)

    def fill(i, c):
        rows = pl.ds(pl.multiple_of(i * 256, 256), 256)
        a = ca_ref[rows, :].astype(F32)
        gt = cg_ref[rows, :].astype(F32)
        glu[pl.ds(pl.multiple_of(CPAD + i * 256, CT), 256), :] = a * _sigmoid(gt)
        return c

    lax.fori_loop(0, S // 256, fill, 0)


def _conv_fwd(u, cw, cb, lg, lb, B, S):
    T = B * S

    def body(ca_ref, cg_ref, w_ref, b_ref, lg_ref, lb_ref, o_ref, y_ref, glu):
        _fill_glu(ca_ref, cg_ref, glu, S)

        def step(i, c):
            t0 = pl.multiple_of(i * CT, CT)
            win = glu[pl.ds(t0, 2 * CT), :]
            acc = jnp.zeros((CT, DC), F32) + b_ref[...]
            taps = _shifted(win, [k + 2 for k in range(CK)])
            for k in range(CK):
                acc = acc + taps[k + 2] * w_ref[k:k + 1, :]
            y_ref[pl.ds(t0, CT), :] = acc
            _, _, z = _ln_fwd(acc, lg_ref[...], lb_ref[...])
            o_ref[pl.ds(t0, CT), :] = (z * _sigmoid(z)).astype(BF16)
            return c

        lax.fori_loop(0, S // CT, step, 0, unroll=4)

    vec = pl.BlockSpec((1, DC), lambda b: (0, 0))
    return pl.pallas_call(
        body, grid=(B,),
        in_specs=[pl.BlockSpec((S, DC), lambda b: (b, 3)), pl.BlockSpec((S, DC), lambda b: (b, 4)),
                  pl.BlockSpec((CT, DC), lambda b: (0, 0)), vec, vec, vec],
        out_specs=[pl.BlockSpec((S, DC), lambda b: (b, 0))] * 2,
        out_shape=[jax.ShapeDtypeStruct((T, DC), BF16), jax.ShapeDtypeStruct((T, DC), F32)],
        scratch_shapes=[pltpu.VMEM((CPAD + S, DC), F32)],
        compiler_params=_cp(("parallel",)), name="conv_fwd")(u, u, cw, cb, lg, lb)


def _conv_bwd(u, y, dconv, cw, lg, lb, B, S):
    T = B * S

    def body(ca_ref, cg_ref, y_ref, dc_ref, w_ref, lg_ref, lb_ref,
             dca_ref, dcg_ref, dw_ref, ds_ref, glu, dyp, dwacc):
        _fill_glu(ca_ref, cg_ref, glu, S)
        dyp[pl.ds(S, CPAD), :] = jnp.zeros((CPAD, DC), F32)
        lgv, lbv = lg_ref[...], lb_ref[...]

        def sum8(v):
            return functools.reduce(jnp.add, [v[r:r + 8] for r in range(0, v.shape[0], 8)])

        P1 = 4 * CT

        def p1(i, carry):
            sb, sg, sl = carry
            t0 = pl.multiple_of(i * P1, P1)
            xhat, rstd, z = _ln_fwd(y_ref[pl.ds(t0, P1), :], lgv, lbv)
            sz = _sigmoid(z)
            dz = dc_ref[pl.ds(t0, P1), :].astype(F32) * (sz * (1.0 + z * (1.0 - sz)))
            dxhat = dz * lgv
            dy = rstd * (dxhat - jnp.mean(dxhat, axis=-1, keepdims=True)
                         - xhat * jnp.mean(dxhat * xhat, axis=-1, keepdims=True))
            dyp[pl.ds(t0, P1), :] = dy
            return sb + sum8(dy), sg + sum8(dz * xhat), sl + sum8(dz)

        z8 = jnp.zeros((8, DC), F32)
        sb, sg, sl = lax.fori_loop(0, S // P1, p1, (z8, z8, z8))
        rs = lambda v: jnp.sum(v, axis=0, keepdims=True)
        ds_ref[...] = jnp.concatenate([rs(sb), rs(sg), rs(sl), jnp.zeros((5, DC), F32)], axis=0)[None]

        def p2(i, c):
            t0 = pl.multiple_of(i * CT, CT)
            win = dyp[pl.ds(t0, 2 * CT), :]
            acc = jnp.zeros((CT, DC), F32)
            taps = _shifted(win, [30 - k for k in range(CK)])
            for k in range(CK):
                acc = acc + taps[30 - k] * w_ref[k:k + 1, :]
            a = ca_ref[pl.ds(t0, CT), :].astype(F32)
            sgt = _sigmoid(cg_ref[pl.ds(t0, CT), :].astype(F32))
            dca_ref[pl.ds(t0, CT), :] = (acc * sgt).astype(BF16)
            dcg_ref[pl.ds(t0, CT), :] = (acc * a * sgt * (1.0 - sgt)).astype(BF16)
            return c

        lax.fori_loop(0, S // CT, p2, 0)

        dwacc[...] = jnp.zeros_like(dwacc)

        def p3(i, c):
            t0 = pl.multiple_of(i * CT, CT)
            win = glu[pl.ds(t0, 2 * CT), :]
            dy = dyp[pl.ds(t0, CT), :]
            for k in range(CK):
                dwacc[k] += sum8(dy * win[k + 2:k + 2 + CT, :])
            return c

        lax.fori_loop(0, S // CT, p3, 0)
        dw_ref[...] = jnp.sum(dwacc[...], axis=1)[None]

    vec = pl.BlockSpec((1, DC), lambda b: (0, 0))
    seq = pl.BlockSpec((S, DC), lambda b: (b, 0))
    return pl.pallas_call(
        body, grid=(B,),
        in_specs=[pl.BlockSpec((S, DC), lambda b: (b, 3)), pl.BlockSpec((S, DC), lambda b: (b, 4)),
                  seq, seq, pl.BlockSpec((CT, DC), lambda b: (0, 0)), vec, vec],
        out_specs=[seq, seq, pl.BlockSpec((1, CT, DC), lambda b: (b, 0, 0)),
                   pl.BlockSpec((1, 8, DC), lambda b: (b, 0, 0))],
        out_shape=[jax.ShapeDtypeStruct((T, DC), BF16)] * 2
                  + [jax.ShapeDtypeStruct((B, CT, DC), F32), jax.ShapeDtypeStruct((B, 8, DC), F32)],
        scratch_shapes=[pltpu.VMEM((CPAD + S, DC), F32), pltpu.VMEM((S + CPAD, DC), F32),
                        pltpu.VMEM((CT, 8, DC), F32)],
        compiler_params=_cp(("parallel",)), name="conv_bwd")(u, u, y, dconv, cw, lg, lb)


def _local_step(x, target, norms, W, B, S, comm=None):
    qg2 = jnp.concatenate([norms["q_norm"], norms["q_norm"]], axis=1)
    kg2 = jnp.concatenate([norms["k_norm"], norms["k_norm"]], axis=1)
    cw = jnp.concatenate([W["conv_w"], jnp.zeros((1, DC), F32)], axis=0)

    W = dict(W)
    (n1, g1), got = _ffn_gate(x, norms["ffn1_norm"], W["wg1"], "ffn1_gate",
                              carry=comm.gathers["up"] if comm else None)
    if comm:
        W.update(comm.gathered("up", got))
    (u1, act1), got = _ffn_up(n1, g1, W["wu1"], "ffn1_up", carry=comm.gathers["down"] if comm else None)
    if comm:
        W.update(comm.gathered("down", got))
    (h1,), got = _ffn_down(x, act1, W["wd1"], "ffn1_down", carry=comm.gathers["mix_in"] if comm else None)
    if comm:
        W.update(comm.gathered("mix_in", got))
    n2, u = _mix_in(h1, norms["mix_norm"], W["win"], "mix_in")
    (attn, lse), got = _attn_fwd(u, qg2, kg2, B, S, carry=comm.gathers["ffn2"] if comm else None)
    if comm:
        W = dict(W, **comm.gathered("ffn2", got))
    conv, y = _conv_fwd(u, cw, norms["conv_b"], norms["conv_ln_g"], norms["conv_ln_b"], B, S)
    h2, n3, g2, u2, dout, dyb, sq, act2 = _mix_out_ffn_loss(h1, attn, conv, W["wout"], norms["ffn2_norm"],
                                                            W["wg2"], W["wu2"], W["wd2"], target, "ffn2_fwd")
    loss = (0.5 / D) * jnp.sum(sq)

    (dg2, du2, dh2, dgn_ffn2), _ = _ffn_bwd_act(dyb, g2, u2, h2, dout, norms["ffn2_norm"],
                                               W["wg2"], W["wu2"], W["wd2"], "ffn2_bwd_act")
    dwd2, _ = _ffn_bwd_w(act2, dyb, "ffn2_bwd_wd")
    dwg2, _ = _ffn_bwd_w(dg2, n3, "ffn2_bwd_wg")
    dwu2, _ = _ffn_bwd_w(du2, n3, "ffn2_bwd_wu")
    dattn, dconv, dwout = _mix_out_bwd(dh2, attn, conv, W["wout"])
    carry = comm.reduce_start({"wg2": dwg2, "wu2": dwu2, "wd2": dwd2, "wout": dwout}) if comm else None
    (dq, dk, dv, dgn_qk), got = _attn_bwd(u, attn, dattn, lse, qg2, kg2, B, S, carry=carry)
    if comm:
        comm.reduce_done(carry, got)
    dca, dcg, dcw, dcs = _conv_bwd(u, y, dconv, cw, norms["conv_ln_g"], norms["conv_ln_b"], B, S)
    dwin, dh1, dyb1, dgn_mix = _mix_in_bwd((dq, dk, dv, dca, dcg), W["win"], n2, h1, dh2, norms["mix_norm"])
    (dg1, du1, gx, dgn_ffn1), _ = _ffn_bwd_act(dyb1, g1, u1, x, dh1, norms["ffn1_norm"],
                                              W["wg1"], W["wu1"], W["wd1"], "ffn1_bwd_act")
    carry = comm.reduce_start({"win": dwin}) if comm else None
    dwd1, got = _ffn_bwd_w(act1, dyb1, "ffn1_bwd_wd", carry=carry)
    if comm:
        comm.reduce_done(carry, got)
        carry = comm.reduce_start({"wd1": dwd1})
    dwg1, got = _ffn_bwd_w(dg1, n1, "ffn1_bwd_wg", carry=carry)
    if comm:
        comm.reduce_done(carry, got)
        carry = comm.reduce_start({"wg1": dwg1})
    dwu1, got = _ffn_bwd_w(du1, n1, "ffn1_bwd_wu", carry=carry)
    if comm:
        comm.reduce_done(carry, got)
        comm.last = comm.reduce_start({"wu1": dwu1})

    qk = jnp.sum(dgn_qk, axis=0)
    cs = jnp.sum(dcs, axis=0)
    small = {
        "ffn1_norm": jnp.sum(dgn_ffn1, axis=0),
        "mix_norm": jnp.sum(dgn_mix, axis=0),
        "q_norm": qk[0:1, 0:HD] + qk[0:1, HD:2 * HD],
        "k_norm": qk[1:2, 0:HD] + qk[1:2, HD:2 * HD],
        "conv_w": jnp.sum(dcw, axis=0)[0:CK],
        "conv_b": cs[0:1],
        "conv_ln_g": cs[1:2],
        "conv_ln_b": cs[2:3],
        "ffn2_norm": jnp.sum(dgn_ffn2, axis=0),
    }
    big = {"wg1": dwg1, "wu1": dwu1, "wd1": dwd1, "win": dwin, "wout": dwout,
           "wg2": dwg2, "wu2": dwu2, "wd2": dwd2}
    return loss, gx, big, small


HBM = pl.BlockSpec(memory_space=pltpu.HBM)
VMEM = pl.BlockSpec(memory_space=pltpu.VMEM)


def _place():
    return lax.axis_index("x"), lax.axis_index("y"), lax.axis_index("c")


class _GatherCarry:
    def __init__(self, shards, mid_at=0.5):
        nt = len(shards)
        self.mid_at = mid_at
        self.shards = shards
        self.in_arrays = [s for s, _ in shards]
        self.in_specs = [VMEM] * nt
        self.out_shape = [jax.ShapeDtypeStruct((NDEV * s.shape[0], s.shape[1]), dt) for s, dt in shards]
        self.out_specs = [HBM] * nt
        self.scratch = ([pltpu.VMEM(s.shape, dt) for s, dt in shards]
                        + [pltpu.SemaphoreType.DMA((nt, 7)), pltpu.SemaphoreType.DMA((nt, 7)),
                           pltpu.SemaphoreType.DMA((nt,))])

    def _copies(self, outs, scr):
        nt = len(self.shards)
        stages = scr[:nt]
        send_sems, recv_sems, local_sems = scr[nt:]
        x, y, c = _place()
        me, sibling = (x, y, c), (x, y, 1 - c)
        xn, yn, diag = (1 - x, y, c), (x, 1 - y, c), (1 - x, 1 - y, c)
        via = (x ^ c, y ^ (1 - c), c)
        onto = (x ^ (1 - c), y ^ c, c)

        def rows(t, px, py, pc):
            r = self.shards[t][0].shape[0]
            return outs[t].at[pl.ds((4 * px + 2 * py + pc) * r, r), :]

        def copy(t, k, block, to, src=None):
            return pltpu.make_async_remote_copy(
                src_ref=rows(t, *block) if src is None else src, dst_ref=rows(t, *block),
                send_sem=send_sems.at[t, k], recv_sem=recv_sems.at[t, k],
                device_id=to, device_id_type=MESH)

        sib = lambda b: (b[0], b[1], 1 - c)
        return dict(
            local=[pltpu.make_async_copy(stages[t], rows(t, *me), local_sems.at[t]) for t in range(nt)],
            own=[[copy(t, 0, me, sibling, src=stages[t]), copy(t, 1, me, xn, src=stages[t]),
                  copy(t, 2, me, yn, src=stages[t])] for t in range(nt)],
            relay=[copy(t, 3, via, onto) for t in range(nt)],
            down=[[copy(t, 4, xn, sibling), copy(t, 5, yn, sibling)] for t in range(nt)],
            down_diag=[copy(t, 6, diag, sibling) for t in range(nt)],
            got_xy=[[copy(t, 1, xn, me), copy(t, 2, yn, me)] for t in range(nt)],
            got_diag=[copy(t, 3, diag, me) for t in range(nt)],
            got_sib=[[copy(t, 0, sibling, me), copy(t, 4, sib(xn), me), copy(t, 5, sib(yn), me),
                      copy(t, 6, sib(diag), me)] for t in range(nt)])

    def start(self, ins, outs, scr):
        cps = self._copies(outs, scr)
        for t, (_, dt) in enumerate(self.shards):
            scr[t][...] = ins[t][...].astype(dt)
            for cp in [cps["local"][t]] + cps["own"][t]:
                cp.start()

    def stages(self):
        sizes = [s.size * jnp.dtype(dt).itemsize for s, dt in self.shards]
        done = [sum(sizes[:t + 1]) / sum(sizes) for t in range(len(sizes))]
        return [(self.mid_at * f, functools.partial(self.mid, t)) for t, f in enumerate(done)]

    def mid(self, t, ins, outs, scr):
        cps = self._copies(outs, scr)
        for cp in cps["got_xy"][t]:
            cp.wait_recv()
        for cp in [cps["relay"][t]] + cps["down"][t]:
            cp.start()

    def finish(self, ins, outs, scr):
        cps = self._copies(outs, scr)
        for t in range(len(self.shards)):
            cps["got_diag"][t].wait_recv()
            cps["down_diag"][t].start()
        for t in range(len(self.shards)):
            for cp in cps["got_sib"][t]:
                cp.wait_recv()
            for cp in cps["own"][t] + [cps["relay"][t]] + cps["down"][t] + [cps["down_diag"][t]]:
                cp.wait_send()
            cps["local"][t].wait()


def _run_carry(carry, name):
    def body(*refs):
        n_in, n_out = len(carry.in_arrays), len(carry.out_shape)
        ins, outs, scr = refs[:n_in], refs[n_in:n_in + n_out], refs[n_in + n_out:]
        carry.start(ins, outs, scr)
        for _, stage in carry.stages():
            stage(ins, outs, scr)
        carry.finish(ins, outs, scr)

    return pl.pallas_call(
        body, in_specs=carry.in_specs, out_specs=carry.out_specs, out_shape=carry.out_shape,
        scratch_shapes=carry.scratch, compiler_params=pltpu.CompilerParams(vmem_limit_bytes=VMEM_LIMIT),
        name=name)(*carry.in_arrays)


class _ExchangeCarry:
    def __init__(self, names, grads, mid_at=0.5):
        nt = len(grads)
        self.mid_at = mid_at
        self.names = names
        self.in_arrays = [g.reshape(4, 2, g.shape[0] // NDEV, g.shape[1]) for g in grads]
        self.in_specs = [HBM] * nt
        blocks = [g.shape[2:] for g in self.in_arrays]
        self.out_shape = [jax.ShapeDtypeStruct((3,) + b, BF16) for b in blocks]
        self.out_specs = [HBM] * nt
        self.scratch = ([pltpu.VMEM((4,) + b, BF16) for b in blocks] * 2 + [pltpu.VMEM(b, BF16) for b in blocks]
                        + [pltpu.SemaphoreType.DMA((nt, 3)), pltpu.SemaphoreType.DMA((nt, 3))]
                        + [pltpu.SemaphoreType.DMA((nt,))] * 4)

    def _copies(self, ins, outs, scr):
        nt = len(ins)
        theirs, own, relayed = scr[:nt], scr[nt:2 * nt], scr[2 * nt:3 * nt]
        send_sems, recv_sems, keep_sems, load_sems, swap_send, swap_recv = scr[3 * nt:]
        x, y, c = _place()
        q = lambda cx, cy: 2 * cx + cy
        near, far = (x ^ c, y ^ (1 - c)), (x ^ (1 - c), y ^ c)

        def remote(t, k, src, dst, chip):
            return pltpu.make_async_remote_copy(
                src_ref=src, dst_ref=dst, send_sem=send_sems.at[t, k], recv_sem=recv_sems.at[t, k],
                device_id=(*chip, c), device_id_type=MESH)

        return dict(
            swap=[pltpu.make_async_remote_copy(
                src_ref=ins[t].at[:, 1 - c], dst_ref=theirs[t], send_sem=swap_send.at[t], recv_sem=swap_recv.at[t],
                device_id=(x, y, 1 - c), device_id_type=MESH) for t in range(nt)],
            load=[pltpu.make_async_copy(ins[t].at[:, c], own[t], load_sems.at[t]) for t in range(nt)],
            keep=[pltpu.make_async_copy(own[t].at[q(x, y)], outs[t].at[0], keep_sems.at[t]) for t in range(nt)],
            direct=[remote(t, 0, own[t].at[q(*near)], outs[t].at[1], near) for t in range(nt)],
            relay=[remote(t, 1, own[t].at[q(1 - x, 1 - y)], relayed[t], near) for t in range(nt)],
            merged=[remote(t, 2, own[t].at[q(*far)], outs[t].at[2], far) for t in range(nt)],
            theirs=theirs, own=own, relayed=relayed, far=q(*far))

    EARLY_AT = 0.1

    def stages(self):
        return [(self.EARLY_AT, self.early), (self.mid_at, self.mid)]

    def start(self, ins, outs, scr):
        cps = self._copies(ins, outs, scr)
        for t in range(len(ins)):
            cps["swap"][t].start()
            cps["load"][t].start()

    def early(self, ins, outs, scr):
        cps = self._copies(ins, outs, scr)
        for t in range(len(ins)):
            cps["load"][t].wait()
            cps["swap"][t].wait_recv()
            own, theirs = cps["own"][t], cps["theirs"][t]
            for j in range(4):
                own[j] = (own[j].astype(F32) + theirs[j].astype(F32)).astype(BF16)
            for kind in ("relay", "direct", "keep"):
                cps[kind][t].start()

    def mid(self, ins, outs, scr):
        cps = self._copies(ins, outs, scr)
        for t in range(len(ins)):
            cps["relay"][t].wait_recv()
            own, far = cps["own"][t], cps["far"]
            own[far] = (own[far].astype(F32) + cps["relayed"][t][...].astype(F32)).astype(BF16)
            cps["merged"][t].start()

    def finish(self, ins, outs, scr):
        cps = self._copies(ins, outs, scr)
        for t in range(len(ins)):
            cps["swap"][t].wait_send()
            cps["direct"][t].wait()
            cps["relay"][t].wait_send()
            cps["merged"][t].wait()
            cps["keep"][t].wait()


class _Comm:
    def __init__(self, groups):
        self.names = {tag: list(g) for tag, (g, _) in groups.items()}
        self.gathers = {tag: _GatherCarry(list(g.values()), mid_at) for tag, (g, mid_at) in groups.items()}
        self.reduced = {}
        self.last = None

    def gathered(self, tag, outs):
        return dict(zip(self.names[tag], outs))

    def reduce_start(self, grads):
        names = list(grads)
        return _ExchangeCarry(names, [grads[n] for n in names])

    def reduce_done(self, carry, outs):
        self.reduced.update(zip(carry.names, outs))


def _adamw_math(w, g, m, v):
    m = B1 * m + (1.0 - B1) * g
    v = B2 * v + (1.0 - B2) * (g * g)
    m_hat = m / (1.0 - B1 ** STEP)
    v_hat = v / (1.0 - B2 ** STEP)
    delta = -LR * (m_hat / (jnp.sqrt(v_hat) + AEPS) + WD * w)
    return delta, m, v


def _adamw_big(recv, w, m, v, name):
    def body(r_ref, w_ref, m_ref, v_ref, g_ref, d_ref, mo_ref, vo_ref):
        g = r_ref[0].astype(F32)
        for q in range(1, 3):
            g = g + r_ref[q].astype(F32)
        d, mn, vn = _adamw_math(w_ref[...], g, m_ref[...], v_ref[...])
        g_ref[...] = g
        d_ref[...] = d
        mo_ref[...] = mn
        vo_ref[...] = vn

    rows, n = w.shape
    tr = rows // 2
    row = pl.BlockSpec((tr, n), lambda t: (t, 0))
    return pl.pallas_call(
        body, grid=(2,), in_specs=[pl.BlockSpec((3, tr, n), lambda t: (0, t, 0)), row, row, row],
        out_specs=[row] * 4, out_shape=[jax.ShapeDtypeStruct(w.shape, F32)] * 4,
        compiler_params=_cp(("parallel",)), name=name)(recv, w, m, v)


SMALL_NAMES = ("ffn1_norm", "mix_norm", "ffn2_norm", "conv_b", "conv_ln_g", "conv_ln_b", "q_norm", "k_norm")
SROWS = 16
LOSS_ROW = len(SMALL_NAMES)
CWF = 2


def _small_step(gs, loss_row, gcw, ws, ms, vs, wcw, mcw, vcw, carry=None):
    ns = len(SMALL_NAMES)
    widths = [g.shape[1] for g in gs]

    def body(*refs):
        it = iter(refs)
        take = lambda n: [next(it) for _ in range(n)]
        g_refs, (loss_ref, gcw_ref) = take(ns), take(2)
        w_refs, m_refs, v_refs = take(ns), take(ns), take(ns)
        wcw_ref, mcw_ref, vcw_ref = take(3)
        cins = take(len(carry.in_arrays)) if carry else []
        outs = [take(4) for _ in range(ns)]
        cw_outs, (loss_out,) = take(4), take(1)
        couts = take(len(carry.out_shape)) if carry else []
        send, slots, cslots, send_sems, recv_sems, csend_sems, crecv_sems = take(7)
        cscr = list(it)
        x, y, c = _place()
        me = 4 * x + 2 * y + c
        send[...] = jnp.zeros_like(send)
        for k in range(ns):
            send[k:k + 1, 0:widths[k]] = g_refs[k][...]
        send[LOSS_ROW:LOSS_ROW + 1, 0:128] = loss_ref[...]
        slots[me] = send[...]
        cslots[me] = gcw_ref[...]
        cps = []
        for k in range(1, NDEV):
            peer = (x ^ ((k >> 2) & 1), y ^ ((k >> 1) & 1), c ^ (k & 1))
            cps.append(pltpu.make_async_remote_copy(
                src_ref=send, dst_ref=slots.at[me], send_sem=send_sems.at[k - 1], recv_sem=recv_sems.at[k - 1],
                device_id=peer, device_id_type=MESH))
            cps.append(pltpu.make_async_remote_copy(
                src_ref=gcw_ref, dst_ref=cslots.at[me], send_sem=csend_sems.at[k - 1],
                recv_sem=crecv_sems.at[k - 1], device_id=peer, device_id_type=MESH))
        for cp in cps:
            cp.start()
        stages = [stage for _, stage in carry.stages()] if carry else []
        if carry:
            carry.start(cins, couts, cscr)
        for stage in stages[:1]:
            stage(cins, couts, cscr)
        for cp in cps:
            cp.wait()
        tot = slots[0]
        ctot = cslots[0, me]
        for j in range(1, NDEV):
            tot = tot + slots[j]
            ctot = ctot + cslots[j, me]

        def step(g, w_ref, m_ref, v_ref, o):
            d, mn, vn = _adamw_math(w_ref[...], g, m_ref[...], v_ref[...])
            o[0][...], o[1][...], o[2][...], o[3][...] = g, d, mn, vn

        for k in range(ns):
            step(tot[k:k + 1, 0:widths[k]], w_refs[k], m_refs[k], v_refs[k], outs[k])
        step(ctot, wcw_ref, mcw_ref, vcw_ref, cw_outs)
        loss_out[...] = tot[LOSS_ROW:LOSS_ROW + 1, 0:128]
        for stage in stages[1:]:
            stage(cins, couts, cscr)
        if carry:
            carry.finish(cins, couts, cscr)

    args = [*gs, loss_row, gcw, *ws, *ms, *vs, wcw, mcw, vcw]
    out_shape = ([jax.ShapeDtypeStruct((1, n), F32) for n in widths for _ in range(4)]
                 + [jax.ShapeDtypeStruct((CWF, D), F32)] * 4 + [jax.ShapeDtypeStruct((1, 128), F32)])
    n_own = len(out_shape)
    res = pl.pallas_call(
        body, in_specs=[VMEM] * len(args) + (carry.in_specs if carry else []),
        out_specs=[VMEM] * n_own + (carry.out_specs if carry else []),
        out_shape=out_shape + (carry.out_shape if carry else []),
        scratch_shapes=[pltpu.VMEM((SROWS, D), F32), pltpu.VMEM((NDEV, SROWS, D), F32),
                        pltpu.VMEM((NDEV, NDEV, CWF, D), F32)]
                       + [pltpu.SemaphoreType.DMA((NDEV - 1,))] * 4 + (carry.scratch if carry else []),
        name="small_step")(*args, *(carry.in_arrays if carry else []))
    per = [res[4 * k:4 * k + 4] for k in range(ns)]
    return per, res[4 * ns:4 * ns + 4], res[n_own - 1], res[n_own:]


def _pack_cw(a):
    flat = a.reshape(a.shape[:-2] + (CK * HD,))
    pad = [(0, 0)] * (flat.ndim - 1) + [(0, CWF * D - CK * HD)]
    return jnp.pad(flat, pad).reshape(a.shape[:-2] + (CWF, D))


def _unpack_cw(v):
    return v.reshape(-1)[:CK * HD].reshape(1, CK, HD)


def kernel(x, ffn1_norm, ffn1_w_gate, ffn1_w_up, ffn1_w_down, mix_norm, w_in, q_norm, k_norm, conv_w, conv_b, conv_ln_g, conv_ln_b, w_out, ffn2_norm, ffn2_w_gate, ffn2_w_up, ffn2_w_down, loss_target, m_ffn1_norm, m_ffn1_w_gate, m_ffn1_w_up, m_ffn1_w_down, m_mix_norm, m_w_in, m_q_norm, m_k_norm, m_conv_w, m_conv_b, m_conv_ln_g, m_conv_ln_b, m_w_out, m_ffn2_norm, m_ffn2_w_gate, m_ffn2_w_up, m_ffn2_w_down, v_ffn1_norm, v_ffn1_w_gate, v_ffn1_w_up, v_ffn1_w_down, v_mix_norm, v_w_in, v_q_norm, v_k_norm, v_conv_w, v_conv_b, v_conv_ln_g, v_conv_ln_b, v_w_out, v_ffn2_norm, v_ffn2_w_gate, v_ffn2_w_up, v_ffn2_w_down):
    P = dict(ffn1_norm=ffn1_norm, ffn1_w_gate=ffn1_w_gate, ffn1_w_up=ffn1_w_up, ffn1_w_down=ffn1_w_down,
             mix_norm=mix_norm, w_in=w_in, q_norm=q_norm, k_norm=k_norm, conv_w=conv_w, conv_b=conv_b,
             conv_ln_g=conv_ln_g, conv_ln_b=conv_ln_b, w_out=w_out, ffn2_norm=ffn2_norm,
             ffn2_w_gate=ffn2_w_gate, ffn2_w_up=ffn2_w_up, ffn2_w_down=ffn2_w_down)
    M = dict(ffn1_norm=m_ffn1_norm, ffn1_w_gate=m_ffn1_w_gate, ffn1_w_up=m_ffn1_w_up, ffn1_w_down=m_ffn1_w_down,
             mix_norm=m_mix_norm, w_in=m_w_in, q_norm=m_q_norm, k_norm=m_k_norm, conv_w=m_conv_w, conv_b=m_conv_b,
             conv_ln_g=m_conv_ln_g, conv_ln_b=m_conv_ln_b, w_out=m_w_out, ffn2_norm=m_ffn2_norm,
             ffn2_w_gate=m_ffn2_w_gate, ffn2_w_up=m_ffn2_w_up, ffn2_w_down=m_ffn2_w_down)
    V = dict(ffn1_norm=v_ffn1_norm, ffn1_w_gate=v_ffn1_w_gate, ffn1_w_up=v_ffn1_w_up, ffn1_w_down=v_ffn1_w_down,
             mix_norm=v_mix_norm, w_in=v_w_in, q_norm=v_q_norm, k_norm=v_k_norm, conv_w=v_conv_w, conv_b=v_conv_b,
             conv_ln_g=v_conv_ln_g, conv_ln_b=v_conv_ln_b, w_out=v_w_out, ffn2_norm=v_ffn2_norm,
             ffn2_w_gate=v_ffn2_w_gate, ffn2_w_up=v_ffn2_w_up, ffn2_w_down=v_ffn2_w_down)
    order = ["ffn1_norm", "ffn1_w_gate", "ffn1_w_up", "ffn1_w_down", "mix_norm", "w_in", "q_norm", "k_norm",
             "conv_w", "conv_b", "conv_ln_g", "conv_ln_b", "w_out", "ffn2_norm", "ffn2_w_gate", "ffn2_w_up",
             "ffn2_w_down"]
    B, S, _ = x.shape
    T = B * S

    bigs = [("wg1", "ffn1_w_gate", True), ("wu1", "ffn1_w_up", True), ("wd1", "ffn1_w_down", False),
            ("win", "w_in", True), ("wout", "w_out", False),
            ("wg2", "ffn2_w_gate", True), ("wu2", "ffn2_w_up", True), ("wd2", "ffn2_w_down", False)]
    hm = lambda a, tr: jnp.transpose(a[0]) if tr else a[0]
    cw_pad = jnp.zeros((32, 128), F32).at[0:CK, 0:HD].set(conv_w[0])
    shard = {ln: (hm(P[pn], tr), BF16) for ln, pn, tr in bigs}
    gathered = _run_carry(_GatherCarry([shard["wg1"], (cw_pad, F32)]), "gather_first")
    W = {"wg1": gathered[0]}
    cwg = gathered[1].reshape(NDEV, 32, 128)[:, 0:CK, 0:HD]
    W["conv_w"] = jnp.transpose(cwg, (1, 0, 2)).reshape(CK, DC)
    norms = {n: P[n] for n in SMALL_NAMES}
    comm = _Comm({"up": ({"wu1": shard["wu1"]}, 0.5), "down": ({"wd1": shard["wd1"]}, 0.5),
                  "mix_in": ({"win": shard["win"]}, 0.5),
                  "ffn2": ({n: shard[n] for n in ("wg2", "wu2", "wd2", "wout")}, 0.5)})

    loss_part, gx, _, small = _local_step(x.reshape(T, D), loss_target.reshape(T, D), norms, W, B, S, comm)

    G, Dl, Mn, Vn = {}, {}, {}, {}
    dcw = small["conv_w"].reshape(CK, NDEV, HD).transpose(1, 0, 2)
    loss_row = jnp.zeros((1, 128), F32).at[0, 0].set(loss_part)
    per, cw_outs, loss_out, got = _small_step(
        [small[n] for n in SMALL_NAMES], loss_row, _pack_cw(dcw),
        [P[n] for n in SMALL_NAMES], [M[n] for n in SMALL_NAMES], [V[n] for n in SMALL_NAMES],
        _pack_cw(P["conv_w"][0]), _pack_cw(M["conv_w"][0]), _pack_cw(V["conv_w"][0]), carry=comm.last)
    comm.reduce_done(comm.last, got)
    loss = loss_out[0, 0]
    for n, outs in zip(SMALL_NAMES, per):
        G[n], Dl[n], Mn[n], Vn[n] = outs
    G["conv_w"], Dl["conv_w"], Mn["conv_w"], Vn["conv_w"] = [_unpack_cw(o) for o in cw_outs]

    for ln, pn, tr in bigs:
        outs = _adamw_big(comm.reduced[ln], hm(P[pn], tr), hm(M[pn], tr), hm(V[pn], tr), "adamw_" + ln)
        G[pn], Dl[pn], Mn[pn], Vn[pn] = [(jnp.transpose(o) if tr else o)[None] for o in outs]

    return (loss, gx.reshape(B, S, D), *[G[n] for n in order], *[Dl[n] for n in order],
            *[Mn[n] for n in order], *[Vn[n] for n in order])
```

```python
import functools

import jax
import jax.numpy as jnp
from jax import lax
from jax.experimental import pallas as pl
from jax.experimental.pallas import tpu as pltpu

F32 = jnp.float32
BF16 = jnp.bfloat16

D = 1024
FF = 2816
HD = 64
DA = 512
DC = 512
DIN = 2560
CK = 31
BLK = 128
DILS = (1, 4, 16)
EPS = 1e-6
NDEV = 8
MESH = pl.DeviceIdType.MESH

LR, B1, B2, AEPS, WD, STEP = 0.001, 0.9, 0.999, 1e-08, 0.01, 10

NT = (((1,), (1,)), ((), ()))
TN = (((0,), (0,)), ((), ()))

VMEM_LIMIT = 60 * 1024 * 1024


def _cp(sem=None):
    return pltpu.CompilerParams(dimension_semantics=sem, vmem_limit_bytes=VMEM_LIMIT)


def _sigmoid(x):
    return 0.5 * (jnp.tanh(0.5 * x) + 1.0)


def _pallas(body, args, *, grid, in_specs, out_specs, out_shape, scratch_shapes, sem, name, carry=None):
    if carry is None:
        outs = pl.pallas_call(body, grid=grid, in_specs=in_specs, out_specs=out_specs, out_shape=out_shape,
                              scratch_shapes=scratch_shapes, compiler_params=_cp(sem), name=name)(*args)
        return outs, None
    n_in, n_out, n_scr = len(in_specs), len(out_shape), len(scratch_shapes)
    c_in, c_out = len(carry.in_arrays), len(carry.out_shape)

    def wrapped(*refs):
        ins, refs = refs[:n_in], refs[n_in:]
        cins, refs = refs[:c_in], refs[c_in:]
        outs, refs = refs[:n_out], refs[n_out:]
        couts, refs = refs[:c_out], refs[c_out:]
        scr, cscr = refs[:n_scr], refs[n_scr:]
        ids = [pl.program_id(a) for a in range(len(grid))]
        step = ids[0]
        for i, n in zip(ids[1:], grid[1:]):
            step = step * n + i
        steps = functools.reduce(lambda a, b: a * b, grid)

        @pl.when(step == 0)
        def _():
            carry.start(cins, couts, cscr)

        body(*ins, *outs, *scr)

        for frac, stage in carry.stages():
            @pl.when(step == int(steps * frac))
            def _(stage=stage):
                stage(cins, couts, cscr)

        @pl.when(step == steps - 1)
        def _():
            carry.finish(cins, couts, cscr)

    outs = pl.pallas_call(
        wrapped, grid=grid, in_specs=list(in_specs) + carry.in_specs, out_specs=list(out_specs) + carry.out_specs,
        out_shape=list(out_shape) + carry.out_shape, scratch_shapes=list(scratch_shapes) + carry.scratch,
        compiler_params=_cp(("arbitrary",) * len(grid)), name=name)(*args, *carry.in_arrays)
    return outs[:n_out], outs[n_out:]


FC = 256


def _resident(shape):
    return pl.BlockSpec(shape, lambda *_: (0,) * len(shape), pipeline_mode=pl.Buffered(1))


def _mix_out_ffn_loss(h1, attn, conv, wout, gain, wg, wu, wd, target, name):
    T = h1.shape[0]
    tm = 512
    nt = T // tm

    def body(h1_ref, at_ref, cv_ref, wo_ref, gain_ref, wg_ref, wu_ref, wd_ref, t_ref,
             h2_ref, n_ref, g_ref, u_ref, dout_ref, dyb_ref, sq_ref, a_hbm, a_scr, a_sem):
        t = pl.program_id(0)
        a_out = lambda i: pltpu.make_async_copy(a_scr, a_hbm.at[pl.ds(pl.multiple_of(i * tm, tm), tm), :], a_sem)

        @pl.when(t > 0)
        def _():
            a_out(t - 1).wait()

        xv = (h1_ref[...]
              + jnp.dot(at_ref[...], wo_ref[0:DA, :], preferred_element_type=F32)
              + jnp.dot(cv_ref[...], wo_ref[DA:D, :], preferred_element_type=F32))
        h2_ref[...] = xv
        r = lax.rsqrt(jnp.mean(xv * xv, axis=-1, keepdims=True) + EPS)
        n_ref[...] = (xv * r * gain_ref[...]).astype(BF16)
        for c in range(FF // FC):
            cols = slice(c * FC, (c + 1) * FC)
            nb = n_ref[...]
            g = lax.dot_general(nb, wg_ref[cols, :], NT, preferred_element_type=F32)
            u = lax.dot_general(nb, wu_ref[cols, :], NT, preferred_element_type=F32)
            g_ref[:, cols] = g.astype(BF16)
            u_ref[:, cols] = u.astype(BF16)
            a_scr[:, cols] = (g * _sigmoid(g) * u).astype(BF16)
        a_out(t).start()
        e = h2_ref[...] + 0.5 * jnp.dot(a_scr[...], wd_ref[...], preferred_element_type=F32) - t_ref[...]
        dout = e * (1.0 / D)
        dout_ref[...] = dout
        dyb_ref[...] = (0.5 * dout).astype(BF16)
        sq_ref[...] = jnp.sum(e * e, axis=0, keepdims=True)[None]

        @pl.when(t == nt - 1)
        def _():
            a_out(t).wait()

    row = pl.BlockSpec((tm, D), lambda t: (t, 0))
    half = pl.BlockSpec((tm, DA), lambda t: (t, 0))
    wide = pl.BlockSpec((tm, FF), lambda t: (t, 0))
    outs, _ = _pallas(
        body, (h1, attn, conv, wout, gain, wg, wu, wd, target), grid=(nt,),
        in_specs=[row, half, half, _resident((D, D)), _resident((1, D)), _resident((FF, D)), _resident((FF, D)),
                  _resident((FF, D)), row],
        out_specs=[row, row, wide, wide, row, row, pl.BlockSpec((1, 1, D), lambda t: (t, 0, 0)), HBM],
        out_shape=[jax.ShapeDtypeStruct((T, D), F32), jax.ShapeDtypeStruct((T, D), BF16)]
                  + [jax.ShapeDtypeStruct((T, FF), BF16)] * 2
                  + [jax.ShapeDtypeStruct((T, D), F32), jax.ShapeDtypeStruct((T, D), BF16),
                     jax.ShapeDtypeStruct((nt, 1, D), F32), jax.ShapeDtypeStruct((T, FF), BF16)],
        scratch_shapes=[pltpu.VMEM((tm, FF), BF16), pltpu.SemaphoreType.DMA(())],
        sem=("arbitrary",), name=name)
    return outs


def _ffn_gate_up(x, gain, wg, wu, name, carry=None):
    T = x.shape[0]
    tm = 512

    def body(x_ref, gain_ref, wg_ref, wu_ref, n_ref, g_ref, u_ref, a_ref):
        xv = x_ref[...]
        r = lax.rsqrt(jnp.mean(xv * xv, axis=-1, keepdims=True) + EPS)
        n_ref[...] = (xv * r * gain_ref[...]).astype(BF16)
        for c in range(FF // FC):
            cols = slice(c * FC, (c + 1) * FC)
            nb = n_ref[...]
            g = lax.dot_general(nb, wg_ref[cols, :], NT, preferred_element_type=F32)
            u = lax.dot_general(nb, wu_ref[cols, :], NT, preferred_element_type=F32)
            g_ref[:, cols] = g.astype(BF16)
            u_ref[:, cols] = u.astype(BF16)
            a_ref[:, cols] = (g * _sigmoid(g) * u).astype(BF16)

    row = pl.BlockSpec((tm, D), lambda t: (t, 0))
    wide = pl.BlockSpec((tm, FF), lambda t: (t, 0))
    return _pallas(
        body, (x, gain, wg, wu), grid=(T // tm,),
        in_specs=[row, _resident((1, D)), _resident((FF, D)), _resident((FF, D))],
        out_specs=[row, wide, wide, wide],
        out_shape=[jax.ShapeDtypeStruct((T, D), BF16)] + [jax.ShapeDtypeStruct((T, FF), BF16)] * 3,
        scratch_shapes=[], sem=("parallel",), name=name, carry=carry)


def _ffn_down_mix_in(x, a, wd, gain, win, name):
    T = x.shape[0]
    tm = 512

    def body(x_ref, a_ref, wd_ref, gain_ref, win_ref, h_ref, u_ref, n_ref):
        hv = x_ref[...] + 0.5 * jnp.dot(a_ref[...], wd_ref[...], preferred_element_type=F32)
        h_ref[...] = hv
        r = lax.rsqrt(jnp.mean(hv * hv, axis=-1, keepdims=True) + EPS)
        n_ref[...] = (hv * r * gain_ref[...]).astype(BF16)
        u_ref[...] = lax.dot_general(n_ref[...], win_ref[...], NT, preferred_element_type=F32).astype(BF16)

    row = pl.BlockSpec((tm, D), lambda t: (t, 0))
    wide = pl.BlockSpec((tm, FF), lambda t: (t, 0))
    outs, _ = _pallas(
        body, (x, a, wd, gain, win), grid=(T // tm,),
        in_specs=[row, wide, _resident((FF, D)), _resident((1, D)), _resident((DIN, D))],
        out_specs=[row, pl.BlockSpec((tm, DIN), lambda t: (t, 0)), row],
        out_shape=[jax.ShapeDtypeStruct((T, D), F32), jax.ShapeDtypeStruct((T, DIN), BF16),
                   jax.ShapeDtypeStruct((T, D), BF16)],
        scratch_shapes=[], sem=("parallel",), name=name)
    return outs


def _ffn_bwd_act(dyb, g, u, x, dout, gain, wg, wu, wd, name, carry=None):
    T = x.shape[0]
    tm = 256
    nt = T // tm

    def body(dy_ref, g_ref, u_ref, x_ref, dout_ref, gain_ref, wg_ref, wu_ref, wd_ref,
             dg_ref, du_ref, dx_ref, dgn_ref):
        for c in range(FF // FC):
            cols = slice(c * FC, (c + 1) * FC)
            da = lax.dot_general(dy_ref[...], wd_ref[cols, :], NT, preferred_element_type=F32)
            gv = g_ref[:, cols].astype(F32)
            uv = u_ref[:, cols].astype(F32)
            sg = _sigmoid(gv)
            dg_ref[:, cols] = (da * uv * (sg * (1.0 + gv * (1.0 - sg)))).astype(BF16)
            du_ref[:, cols] = (da * (gv * sg)).astype(BF16)
        dn = (jnp.dot(dg_ref[...], wg_ref[...], preferred_element_type=F32)
              + jnp.dot(du_ref[...], wu_ref[...], preferred_element_type=F32))
        dx, dgain = _rms_bwd_rows(dn, x_ref[...], gain_ref[...])
        dx_ref[...] = dout_ref[...] + dx
        dgn_ref[...] = dgain[None]

    row = pl.BlockSpec((tm, D), lambda t: (t, 0))
    wide = pl.BlockSpec((tm, FF), lambda t: (t, 0))
    return _pallas(
        body, (dyb, g, u, x, dout, gain, wg, wu, wd), grid=(nt,),
        in_specs=[row, wide, wide, row, row, _resident((1, D)), _resident((FF, D)), _resident((FF, D)),
                  _resident((FF, D))],
        out_specs=[wide, wide, row, pl.BlockSpec((1, 1, D), lambda t: (t, 0, 0))],
        out_shape=[jax.ShapeDtypeStruct((T, FF), BF16)] * 2
                  + [jax.ShapeDtypeStruct((T, D), F32), jax.ShapeDtypeStruct((nt, 1, D), F32)],
        scratch_shapes=[], sem=("parallel",), name=name, carry=carry)


def _ffn_bwd_w(lhs, rhs, name, carry=None):
    T = rhs.shape[0]
    tf = 256

    def body(l_ref, r_ref, dw_ref):
        dw_ref[...] = lax.dot_general(l_ref[...], r_ref[...], TN, preferred_element_type=F32).astype(BF16)

    (dw,), got = _pallas(
        body, (lhs, rhs), grid=(FF // tf,),
        in_specs=[pl.BlockSpec((T, tf), lambda f: (0, f)), _resident((T, D))],
        out_specs=[pl.BlockSpec((tf, D), lambda f: (f, 0))], out_shape=[jax.ShapeDtypeStruct((FF, D), BF16)],
        scratch_shapes=[], sem=("parallel",), name=name, carry=carry)
    return dw, got


def _rms_bwd_rows(dn, xv, gain):
    r = lax.rsqrt(jnp.mean(xv * xv, axis=-1, keepdims=True) + EPS)
    xhat = xv * r
    dxhat = dn * gain
    dx = r * (dxhat - xhat * jnp.mean(dxhat * xhat, axis=-1, keepdims=True))
    return dx, jnp.sum(dn * xhat, axis=0, keepdims=True)


def _mix_out_bwd(dh, attn, conv, wout):
    T = dh.shape[0]
    tm = 512
    nt = T // tm

    def body(dh_ref, a_ref, c_ref, w_ref, da_ref, dc_ref, dw_ref, acc_scr):
        t = pl.program_id(0)

        @pl.when(t == 0)
        def _():
            acc_scr[...] = jnp.zeros_like(acc_scr)

        dhb = dh_ref[...].astype(BF16)
        dmix = lax.dot_general(dhb, w_ref[...], NT, preferred_element_type=F32)
        da_ref[...] = dmix[:, 0:DA].astype(BF16)
        dc_ref[...] = dmix[:, DA:D].astype(BF16)
        acc_scr[0:DA, :] += lax.dot_general(a_ref[...], dhb, TN, preferred_element_type=F32)
        acc_scr[DA:D, :] += lax.dot_general(c_ref[...], dhb, TN, preferred_element_type=F32)

        @pl.when(t == nt - 1)
        def _():
            dw_ref[...] = acc_scr[...].astype(BF16)

    row = pl.BlockSpec((tm, D), lambda t: (t, 0))
    half = pl.BlockSpec((tm, DA), lambda t: (t, 0))
    full = pl.BlockSpec((D, D), lambda t: (0, 0))
    return pl.pallas_call(
        body, grid=(nt,), in_specs=[row, half, half, full], out_specs=[half, half, full],
        out_shape=[jax.ShapeDtypeStruct((T, DA), BF16)] * 2 + [jax.ShapeDtypeStruct((D, D), BF16)],
        scratch_shapes=[pltpu.VMEM((D, D), F32)],
        compiler_params=_cp(("arbitrary",)), name="mix_out_bwd")(dh, attn, conv, wout)


def _mix_in_bwd(dparts, win, nb, h, dh, gain):
    T = h.shape[0]
    tm = 512
    nt = T // tm

    def body(d0, d1, d2, d3, d4, w_ref, n_ref, h_ref, dh_ref, gain_ref,
             dw_ref, dx_ref, dyb_ref, dg_ref, acc_scr):
        t = pl.program_id(0)

        @pl.when(t == 0)
        def _():
            acc_scr[...] = jnp.zeros_like(acc_scr)

        n = n_ref[...]
        dn = jnp.zeros((tm, D), F32)
        for i, d_ref in enumerate((d0, d1, d2, d3, d4)):
            dv = d_ref[...]
            dn = dn + jnp.dot(dv, w_ref[i * DA:(i + 1) * DA, :], preferred_element_type=F32)
            acc_scr[i * DA:(i + 1) * DA, :] += lax.dot_general(dv, n, TN, preferred_element_type=F32)
        dx, dgain = _rms_bwd_rows(dn, h_ref[...], gain_ref[...])
        tot = dh_ref[...] + dx
        dx_ref[...] = tot
        dyb_ref[...] = (0.5 * tot).astype(BF16)
        dg_ref[...] = dgain[None]

        @pl.when(t == nt - 1)
        def _():
            dw_ref[...] = acc_scr[...].astype(BF16)

    row = pl.BlockSpec((tm, D), lambda t: (t, 0))
    half = pl.BlockSpec((tm, DA), lambda t: (t, 0))
    full = pl.BlockSpec((DIN, D), lambda t: (0, 0))
    return pl.pallas_call(
        body, grid=(nt,),
        in_specs=[half] * 5 + [full, row, row, row, pl.BlockSpec((1, D), lambda t: (0, 0))],
        out_specs=[full, row, row, pl.BlockSpec((1, 1, D), lambda t: (t, 0, 0))],
        out_shape=[jax.ShapeDtypeStruct((DIN, D), BF16), jax.ShapeDtypeStruct((T, D), F32),
                   jax.ShapeDtypeStruct((T, D), BF16), jax.ShapeDtypeStruct((nt, 1, D), F32)],
        scratch_shapes=[pltpu.VMEM((DIN, D), F32)],
        compiler_params=_cp(("arbitrary",)), name="mix_in_bwd")(*dparts, win, nb, h, dh, gain)


def _head_masks():
    lane = lax.broadcasted_iota(jnp.int32, (1, 2 * HD), 1)
    m0 = lane < HD
    return m0, jnp.logical_not(m0)


def _stack_heads(v, m0, m1):
    z = jnp.zeros_like(v)
    return jnp.concatenate([jnp.where(m0, v, z), jnp.where(m1, v, z)], axis=0)


def _unstack_heads(v2, m0):
    return jnp.where(m0, v2[0:BLK], v2[BLK:2 * BLK])


def _head_sums(xv):
    ri = lax.broadcasted_iota(jnp.int32, (2 * HD, 2 * HD), 0)
    ci = lax.broadcasted_iota(jnp.int32, (2 * HD, 2 * HD), 1)
    ones = jnp.where((ri < HD) == (ci < HD), 1.0, 0.0).astype(BF16)
    hi = xv.astype(BF16)
    lo = (xv - hi.astype(F32)).astype(BF16)
    return (jnp.dot(hi, ones, preferred_element_type=F32) + jnp.dot(lo, ones, preferred_element_type=F32))


def _head_rms(xv):
    return lax.rsqrt(_head_sums(xv * xv) * (1.0 / HD) + EPS)


def _band_mask(first):
    qi = lax.broadcasted_iota(jnp.int32, (BLK, 2 * BLK), 0)
    ci = lax.broadcasted_iota(jnp.int32, (BLK, 2 * BLK), 1)
    band = (ci >= qi) & (ci <= qi + BLK)
    return band & ((ci >= BLK) | jnp.logical_not(first))


def _block_rows(j, d, seg):
    r, n = j // seg, j % seg
    start = r + (d * BLK) * n
    first = n == 0
    prev = jnp.where(first, start, start - d * BLK)
    return pl.ds(start, BLK, stride=d), pl.ds(prev, BLK, stride=d), first


def _block_keys(refs, cur, prev, first, single):
    if single:
        qi = lax.broadcasted_iota(jnp.int32, (BLK, BLK), 0)
        ci = lax.broadcasted_iota(jnp.int32, (BLK, BLK), 1)
        return [r[cur, :].astype(BF16) for r in refs], ci <= qi
    return ([jnp.concatenate([r[prev, :], r[cur, :]], axis=0).astype(BF16) for r in refs], _band_mask(first))


def _attn_fwd(u, qg2, kg2, B, S, carry=None):
    T = B * S
    NB = S // BLK
    scale = HD ** -0.5

    def body(q_ref, k_ref, v_ref, qg_ref, kg_ref, o_ref, lse_ref, qn, kn, vn, os_, ls_):
        m0, m1 = _head_masks()
        qv = q_ref[...].astype(F32)
        qn[...] = qv * _head_rms(qv) * (qg_ref[...] * scale)
        kv = k_ref[...].astype(F32)
        kn[...] = kv * _head_rms(kv) * kg_ref[...]
        vn[...] = v_ref[...].astype(F32)

        for i, d in enumerate(DILS):
            seg = NB // d

            def blk(j, c, i=i, d=d, seg=seg):
                cur, prev, first = _block_rows(j, d, seg)
                q2 = _stack_heads(qn[cur, :].astype(BF16), m0, m1)
                (kk, vv), mask = _block_keys((kn, vn), cur, prev, first, False)
                s = lax.dot_general(q2, kk, NT, preferred_element_type=F32)
                s = jnp.where(jnp.concatenate([mask, mask], axis=0), s, -1e30)
                mx = jnp.max(s, axis=-1, keepdims=True)
                p = jnp.exp(s - mx)
                l = jnp.sum(p, axis=-1, keepdims=True)
                o2 = jnp.dot((p * (1.0 / l)).astype(BF16), vv, preferred_element_type=F32)
                os_[i, cur, :] = _unstack_heads(o2, m0)
                ls_[i, cur, :] = _unstack_heads(mx + jnp.log(l), m0)
                return c

            lax.fori_loop(0, NB, blk, 0, unroll=8)

        def comb(c, carry):
            rows = pl.ds(pl.multiple_of(c * 256, 256), 256)
            l0, l1, l2 = ls_[0, rows, :], ls_[1, rows, :], ls_[2, rows, :]
            mx = jnp.maximum(jnp.maximum(l0, l1), l2)
            e0, e1, e2 = jnp.exp(l0 - mx), jnp.exp(l1 - mx), jnp.exp(l2 - mx)
            tot = e0 + e1 + e2
            inv = 1.0 / tot
            o = (e0 * os_[0, rows, :] + e1 * os_[1, rows, :] + e2 * os_[2, rows, :]) * inv
            o_ref[rows, :] = o.astype(BF16)
            lse_ref[rows, :] = mx + jnp.log(tot)
            return carry

        lax.fori_loop(0, S // 256, comb, 0)

    pair = 2 * HD
    blk_spec = lambda off: pl.BlockSpec((S, pair), lambda b, p, off=off: (b, off + p))
    gspec = pl.BlockSpec((1, pair), lambda b, p: (0, 0))
    return _pallas(
        body, (u, u, u, qg2, kg2), grid=(B, DA // pair),
        in_specs=[blk_spec(0), blk_spec(DA // pair), blk_spec(2 * DA // pair), gspec, gspec],
        out_specs=[blk_spec(0), blk_spec(0)],
        out_shape=[jax.ShapeDtypeStruct((T, DA), BF16), jax.ShapeDtypeStruct((T, DA), F32)],
        scratch_shapes=[pltpu.VMEM((S, pair), F32)] * 3 + [pltpu.VMEM((3, S, pair), F32)] * 2,
        sem=("parallel", "parallel"), name="attn_fwd", carry=carry)


def _attn_bwd(u, attn, dattn, lse, qg2, kg2, B, S, carry=None):
    T = B * S
    NB = S // BLK
    scale = HD ** -0.5
    pair = 2 * HD

    def body(q_ref, k_ref, v_ref, o_ref, do_ref, lse_ref, qg_ref, kg_ref,
             dq_ref, dk_ref, dv_ref, dgn_ref,
             qn, kn, vn, don, ldl, accq, acck, accv, rq, rk):
        m0, m1 = _head_masks()
        lane = lax.broadcasted_iota(jnp.int32, (1, pair), 1)
        qv = q_ref[...].astype(F32)
        rq[...] = _head_rms(qv)
        qn[...] = qv * rq[...] * (qg_ref[...] * scale)
        kv = k_ref[...].astype(F32)
        rk[...] = _head_rms(kv)
        kn[...] = kv * rk[...] * kg_ref[...]
        vn[...] = v_ref[...].astype(F32)
        dov = do_ref[...].astype(F32)
        don[...] = dov
        ldl[...] = jnp.where((lane % HD) < HD // 2, lse_ref[...], _head_sums(dov * o_ref[...].astype(F32)))

        for i, d in enumerate(DILS):
            seg = NB // d

            def blk(j, c, i=i, d=d, seg=seg):
                cur, prev, first = _block_rows(j, d, seg)
                q2 = _stack_heads(qn[cur, :].astype(BF16), m0, m1)
                do2 = _stack_heads(don[cur, :].astype(BF16), m0, m1)
                (kk, vv), mask = _block_keys((kn, vn), cur, prev, first, seg == 1)
                ldv = ldl[cur, :]
                lse2 = jnp.concatenate([ldv[:, 0:1], ldv[:, HD:HD + 1]], axis=0)
                dl2 = jnp.concatenate([ldv[:, HD // 2:HD // 2 + 1], ldv[:, HD + HD // 2:HD + HD // 2 + 1]], axis=0)
                s = lax.dot_general(q2, kk, NT, preferred_element_type=F32)
                p = jnp.where(jnp.concatenate([mask, mask], axis=0), jnp.exp(s - lse2), 0.0)
                dp = lax.dot_general(do2, vv, NT, preferred_element_type=F32)
                ds = (p * (dp - dl2)).astype(BF16)
                dq_acc = _unstack_heads(jnp.dot(ds, kk, preferred_element_type=F32), m0)
                dk_acc = lax.dot_general(ds, q2, TN, preferred_element_type=F32)
                dv_acc = lax.dot_general(p.astype(BF16), do2, TN, preferred_element_type=F32)
                if i == 0:
                    accq[cur, :] = dq_acc
                    acck[cur, :] = dk_acc[BLK:2 * BLK]
                    accv[cur, :] = dv_acc[BLK:2 * BLK]
                    acck[prev, :] += dk_acc[0:BLK]
                    accv[prev, :] += dv_acc[0:BLK]
                elif seg == 1:
                    accq[cur, :] += dq_acc
                    acck[cur, :] += dk_acc
                    accv[cur, :] += dv_acc
                else:
                    accq[cur, :] += dq_acc
                    acck[prev, :] += dk_acc[0:BLK]
                    acck[cur, :] += dk_acc[BLK:2 * BLK]
                    accv[prev, :] += dv_acc[0:BLK]
                    accv[cur, :] += dv_acc[BLK:2 * BLK]
                return c

            lax.fori_loop(0, NB, blk, 0, unroll=8)

        def norm_bwd(x_ref, r_ref, dn, gain):
            r = r_ref[...]
            xhat = x_ref[...].astype(F32) * r
            dxhat = dn * gain
            dx = r * (dxhat - xhat * (_head_sums(dxhat * xhat) * (1.0 / HD)))
            return dx, jnp.sum(dn * xhat, axis=0, keepdims=True)

        dq, dgq = norm_bwd(q_ref, rq, accq[...], qg_ref[...] * scale)
        dk, dgk = norm_bwd(k_ref, rk, acck[...], kg_ref[...])
        dq_ref[...] = dq.astype(BF16)
        dk_ref[...] = dk.astype(BF16)
        dv_ref[...] = accv[...].astype(BF16)
        dgn_ref[...] = jnp.concatenate([dgq * scale, dgk, jnp.zeros((6, pair), F32)], axis=0)[None]

    blk_spec = lambda off: pl.BlockSpec((S, pair), lambda b, p, off=off: (b, off + p))
    gspec = pl.BlockSpec((1, pair), lambda b, p: (0, 0))
    np_ = DA // pair
    return _pallas(
        body, (u, u, u, attn, dattn, lse, qg2, kg2), grid=(B, np_),
        in_specs=[blk_spec(0), blk_spec(np_), blk_spec(2 * np_), blk_spec(0), blk_spec(0), blk_spec(0),
                  gspec, gspec],
        out_specs=[blk_spec(0), blk_spec(0), blk_spec(0),
                   pl.BlockSpec((1, 8, pair), lambda b, p: (b * np_ + p, 0, 0))],
        out_shape=[jax.ShapeDtypeStruct((T, DA), BF16)] * 3 + [jax.ShapeDtypeStruct((B * np_, 8, pair), F32)],
        scratch_shapes=[pltpu.VMEM((S, pair), F32)] * 10,
        sem=("parallel", "parallel"), name="attn_bwd", carry=carry)


CT = 32
CPAD = 32


def _shifted(win, offsets):
    rolled, out = {}, {}
    n = win.shape[0]
    for o in offsets:
        sub = o % 8
        if sub not in rolled:
            rolled[sub] = win if sub == 0 else pltpu.roll(win, n - sub, 0)
        out[o] = rolled[sub][o - sub:o - sub + CT, :]
    return out


def _ln_fwd(y, g, b):
    mu = jnp.mean(y, axis=-1, keepdims=True)
    yc = y - mu
    rstd = lax.rsqrt(jnp.mean(yc * yc, axis=-1, keepdims=True) + EPS)
    xhat = yc * rstd
    return xhat, rstd, xhat * g + b


def _fill_glu(ca_ref, cg_ref, glu, S):
    glu[pl.ds(0, CPAD), :] = jnp.zeros((CPAD, DC), F32)

    def fill(i, c):
        rows = pl.ds(pl.multiple_of(i * 256, 256), 256)
        a = ca_ref[rows, :].astype(F32)
        gt = cg_ref[rows, :].astype(F32)
        glu[pl.ds(pl.multiple_of(CPAD + i * 256, CT), 256), :] = a * _sigmoid(gt)
        return c

    lax.fori_loop(0, S // 256, fill, 0)


def _conv_fwd(u, cw, cb, lg, lb, B, S):
    T = B * S

    def body(ca_ref, cg_ref, w_ref, b_ref, lg_ref, lb_ref, o_ref, y_ref, glu):
        _fill_glu(ca_ref, cg_ref, glu, S)

        def step(i, c):
            t0 = pl.multiple_of(i * CT, CT)
            win = glu[pl.ds(t0, 2 * CT), :]
            acc = jnp.zeros((CT, DC), F32) + b_ref[...]
            taps = _shifted(win, [k + 2 for k in range(CK)])
            for k in range(CK):
                acc = acc + taps[k + 2] * w_ref[k:k + 1, :]
            y_ref[pl.ds(t0, CT), :] = acc
            _, _, z = _ln_fwd(acc, lg_ref[...], lb_ref[...])
            o_ref[pl.ds(t0, CT), :] = (z * _sigmoid(z)).astype(BF16)
            return c

        lax.fori_loop(0, S // CT, step, 0, unroll=4)

    vec = pl.BlockSpec((1, DC), lambda b: (0, 0))
    return pl.pallas_call(
        body, grid=(B,),
        in_specs=[pl.BlockSpec((S, DC), lambda b: (b, 3)), pl.BlockSpec((S, DC), lambda b: (b, 4)),
                  pl.BlockSpec((CT, DC), lambda b: (0, 0)), vec, vec, vec],
        out_specs=[pl.BlockSpec((S, DC), lambda b: (b, 0))] * 2,
        out_shape=[jax.ShapeDtypeStruct((T, DC), BF16), jax.ShapeDtypeStruct((T, DC), F32)],
        scratch_shapes=[pltpu.VMEM((CPAD + S, DC), F32)],
        compiler_params=_cp(("parallel",)), name="conv_fwd")(u, u, cw, cb, lg, lb)


def _conv_bwd(u, y, dconv, cw, lg, lb, B, S):
    T = B * S

    def body(ca_ref, cg_ref, y_ref, dc_ref, w_ref, lg_ref, lb_ref,
             dca_ref, dcg_ref, dw_ref, ds_ref, glu, dyp, dwacc):
        _fill_glu(ca_ref, cg_ref, glu, S)
        dyp[pl.ds(S, CPAD), :] = jnp.zeros((CPAD, DC), F32)
        lgv, lbv = lg_ref[...], lb_ref[...]

        def sum8(v):
            return functools.reduce(jnp.add, [v[r:r + 8] for r in range(0, v.shape[0], 8)])

        P1 = 4 * CT

        def p1(i, carry):
            sb, sg, sl = carry
            t0 = pl.multiple_of(i * P1, P1)
            xhat, rstd, z = _ln_fwd(y_ref[pl.ds(t0, P1), :], lgv, lbv)
            sz = _sigmoid(z)
            dz = dc_ref[pl.ds(t0, P1), :].astype(F32) * (sz * (1.0 + z * (1.0 - sz)))
            dxhat = dz * lgv
            dy = rstd * (dxhat - jnp.mean(dxhat, axis=-1, keepdims=True)
                         - xhat * jnp.mean(dxhat * xhat, axis=-1, keepdims=True))
            dyp[pl.ds(t0, P1), :] = dy
            return sb + sum8(dy), sg + sum8(dz * xhat), sl + sum8(dz)

        z8 = jnp.zeros((8, DC), F32)
        sb, sg, sl = lax.fori_loop(0, S // P1, p1, (z8, z8, z8))
        rs = lambda v: jnp.sum(v, axis=0, keepdims=True)
        ds_ref[...] = jnp.concatenate([rs(sb), rs(sg), rs(sl), jnp.zeros((5, DC), F32)], axis=0)[None]

        def p2(i, c):
            t0 = pl.multiple_of(i * CT, CT)
            win = dyp[pl.ds(t0, 2 * CT), :]
            acc = jnp.zeros((CT, DC), F32)
            taps = _shifted(win, [30 - k for k in range(CK)])
            for k in range(CK):
                acc = acc + taps[30 - k] * w_ref[k:k + 1, :]
            a = ca_ref[pl.ds(t0, CT), :].astype(F32)
            sgt = _sigmoid(cg_ref[pl.ds(t0, CT), :].astype(F32))
            dca_ref[pl.ds(t0, CT), :] = (acc * sgt).astype(BF16)
            dcg_ref[pl.ds(t0, CT), :] = (acc * a * sgt * (1.0 - sgt)).astype(BF16)
            return c

        lax.fori_loop(0, S // CT, p2, 0)

        dwacc[...] = jnp.zeros_like(dwacc)

        def p3(i, c):
            t0 = pl.multiple_of(i * CT, CT)
            win = glu[pl.ds(t0, 2 * CT), :]
            dy = dyp[pl.ds(t0, CT), :]
            for k in range(CK):
                dwacc[k] += sum8(dy * win[k + 2:k + 2 + CT, :])
            return c

        lax.fori_loop(0, S // CT, p3, 0)
        dw_ref[...] = jnp.sum(dwacc[...], axis=1)[None]

    vec = pl.BlockSpec((1, DC), lambda b: (0, 0))
    seq = pl.BlockSpec((S, DC), lambda b: (b, 0))
    return pl.pallas_call(
        body, grid=(B,),
        in_specs=[pl.BlockSpec((S, DC), lambda b: (b, 3)), pl.BlockSpec((S, DC), lambda b: (b, 4)),
                  seq, seq, pl.BlockSpec((CT, DC), lambda b: (0, 0)), vec, vec],
        out_specs=[seq, seq, pl.BlockSpec((1, CT, DC), lambda b: (b, 0, 0)),
                   pl.BlockSpec((1, 8, DC), lambda b: (b, 0, 0))],
        out_shape=[jax.ShapeDtypeStruct((T, DC), BF16)] * 2
                  + [jax.ShapeDtypeStruct((B, CT, DC), F32), jax.ShapeDtypeStruct((B, 8, DC), F32)],
        scratch_shapes=[pltpu.VMEM((CPAD + S, DC), F32), pltpu.VMEM((S + CPAD, DC), F32),
                        pltpu.VMEM((CT, 8, DC), F32)],
        compiler_params=_cp(("parallel",)), name="conv_bwd")(u, u, y, dconv, cw, lg, lb)


def _local_step(x, target, norms, W, B, S, comm=None):
    qg2 = jnp.concatenate([norms["q_norm"], norms["q_norm"]], axis=1)
    kg2 = jnp.concatenate([norms["k_norm"], norms["k_norm"]], axis=1)
    cw = jnp.concatenate([W["conv_w"], jnp.zeros((1, DC), F32)], axis=0)

    W = dict(W)
    (n1, g1, u1, act1), got = _ffn_gate_up(x, norms["ffn1_norm"], W["wg1"], W["wu1"], "ffn1_gate_up",
                                           carry=comm.gathers["down_in"] if comm else None)
    if comm:
        W.update(comm.gathered("down_in", got))
    h1, u, n2 = _ffn_down_mix_in(x, act1, W["wd1"], norms["mix_norm"], W["win"], "ffn1_down_mix_in")
    (attn, lse), got = _attn_fwd(u, qg2, kg2, B, S, carry=comm.gathers["ffn2"] if comm else None)
    if comm:
        W = dict(W, **comm.gathered("ffn2", got))
    conv, y = _conv_fwd(u, cw, norms["conv_b"], norms["conv_ln_g"], norms["conv_ln_b"], B, S)
    h2, n3, g2, u2, dout, dyb, sq, act2 = _mix_out_ffn_loss(h1, attn, conv, W["wout"], norms["ffn2_norm"],
                                                            W["wg2"], W["wu2"], W["wd2"], target, "ffn2_fwd")
    loss = (0.5 / D) * jnp.sum(sq)

    (dg2, du2, dh2, dgn_ffn2), _ = _ffn_bwd_act(dyb, g2, u2, h2, dout, norms["ffn2_norm"],
                                               W["wg2"], W["wu2"], W["wd2"], "ffn2_bwd_act")
    dwd2, _ = _ffn_bwd_w(act2, dyb, "ffn2_bwd_wd")
    dwg2, _ = _ffn_bwd_w(dg2, n3, "ffn2_bwd_wg")
    dwu2, _ = _ffn_bwd_w(du2, n3, "ffn2_bwd_wu")
    dattn, dconv, dwout = _mix_out_bwd(dh2, attn, conv, W["wout"])
    carry = comm.reduce_start({"wg2": dwg2, "wu2": dwu2, "wd2": dwd2, "wout": dwout}) if comm else None
    (dq, dk, dv, dgn_qk), got = _attn_bwd(u, attn, dattn, lse, qg2, kg2, B, S, carry=carry)
    if comm:
        comm.reduce_done(carry, got)
    dca, dcg, dcw, dcs = _conv_bwd(u, y, dconv, cw, norms["conv_ln_g"], norms["conv_ln_b"], B, S)
    dwin, dh1, dyb1, dgn_mix = _mix_in_bwd((dq, dk, dv, dca, dcg), W["win"], n2, h1, dh2, norms["mix_norm"])
    dwd1, _ = _ffn_bwd_w(act1, dyb1, "ffn1_bwd_wd")
    carry = comm.reduce_start({"win": dwin, "wd1": dwd1}) if comm else None
    (dg1, du1, gx, dgn_ffn1), got = _ffn_bwd_act(dyb1, g1, u1, x, dh1, norms["ffn1_norm"],
                                                W["wg1"], W["wu1"], W["wd1"], "ffn1_bwd_act", carry=carry)
    if comm:
        comm.reduce_done(carry, got)
    dwg1, _ = _ffn_bwd_w(dg1, n1, "ffn1_bwd_wg")
    carry = comm.reduce_start({"wg1": dwg1}) if comm else None
    dwu1, got = _ffn_bwd_w(du1, n1, "ffn1_bwd_wu", carry=carry)
    if comm:
        comm.reduce_done(carry, got)
        comm.last = comm.reduce_start({"wu1": dwu1})

    qk = jnp.sum(dgn_qk, axis=0)
    cs = jnp.sum(dcs, axis=0)
    small = {
        "ffn1_norm": jnp.sum(dgn_ffn1, axis=0),
        "mix_norm": jnp.sum(dgn_mix, axis=0),
        "q_norm": qk[0:1, 0:HD] + qk[0:1, HD:2 * HD],
        "k_norm": qk[1:2, 0:HD] + qk[1:2, HD:2 * HD],
        "conv_w": jnp.sum(dcw, axis=0)[0:CK],
        "conv_b": cs[0:1],
        "conv_ln_g": cs[1:2],
        "conv_ln_b": cs[2:3],
        "ffn2_norm": jnp.sum(dgn_ffn2, axis=0),
    }
    big = {"wg1": dwg1, "wu1": dwu1, "wd1": dwd1, "win": dwin, "wout": dwout,
           "wg2": dwg2, "wu2": dwu2, "wd2": dwd2}
    return loss, gx, big, small


HBM = pl.BlockSpec(memory_space=pltpu.HBM)
VMEM = pl.BlockSpec(memory_space=pltpu.VMEM)


def _place():
    return lax.axis_index("x"), lax.axis_index("y"), lax.axis_index("c")


class _GatherCarry:
    def __init__(self, shards, mid_at=0.5):
        nt = len(shards)
        self.mid_at = mid_at
        self.shards = shards
        self.in_arrays = [s for s, _ in shards]
        self.in_specs = [VMEM] * nt
        self.out_shape = [jax.ShapeDtypeStruct((NDEV * s.shape[0], s.shape[1]), dt) for s, dt in shards]
        self.out_specs = [HBM] * nt
        self.scratch = ([pltpu.VMEM(s.shape, dt) for s, dt in shards]
                        + [pltpu.SemaphoreType.DMA((nt, 7)), pltpu.SemaphoreType.DMA((nt, 7)),
                           pltpu.SemaphoreType.DMA((nt,))])

    def _copies(self, outs, scr):
        nt = len(self.shards)
        stages = scr[:nt]
        send_sems, recv_sems, local_sems = scr[nt:]
        x, y, c = _place()
        me, sibling = (x, y, c), (x, y, 1 - c)
        xn, yn, diag = (1 - x, y, c), (x, 1 - y, c), (1 - x, 1 - y, c)
        via = (x ^ c, y ^ (1 - c), c)
        onto = (x ^ (1 - c), y ^ c, c)

        def rows(t, px, py, pc):
            r = self.shards[t][0].shape[0]
            return outs[t].at[pl.ds((4 * px + 2 * py + pc) * r, r), :]

        def copy(t, k, block, to, src=None):
            return pltpu.make_async_remote_copy(
                src_ref=rows(t, *block) if src is None else src, dst_ref=rows(t, *block),
                send_sem=send_sems.at[t, k], recv_sem=recv_sems.at[t, k],
                device_id=to, device_id_type=MESH)

        sib = lambda b: (b[0], b[1], 1 - c)
        return dict(
            local=[pltpu.make_async_copy(stages[t], rows(t, *me), local_sems.at[t]) for t in range(nt)],
            own=[[copy(t, 0, me, sibling, src=stages[t]), copy(t, 1, me, xn, src=stages[t]),
                  copy(t, 2, me, yn, src=stages[t])] for t in range(nt)],
            relay=[copy(t, 3, via, onto) for t in range(nt)],
            down=[[copy(t, 4, xn, sibling), copy(t, 5, yn, sibling)] for t in range(nt)],
            down_diag=[copy(t, 6, diag, sibling) for t in range(nt)],
            got_xy=[[copy(t, 1, xn, me), copy(t, 2, yn, me)] for t in range(nt)],
            got_diag=[copy(t, 3, diag, me) for t in range(nt)],
            got_sib=[[copy(t, 0, sibling, me), copy(t, 4, sib(xn), me), copy(t, 5, sib(yn), me),
                      copy(t, 6, sib(diag), me)] for t in range(nt)])

    def start(self, ins, outs, scr):
        cps = self._copies(outs, scr)
        for t, (_, dt) in enumerate(self.shards):
            scr[t][...] = ins[t][...].astype(dt)
            for cp in [cps["local"][t]] + cps["own"][t]:
                cp.start()

    def stages(self):
        sizes = [s.size * jnp.dtype(dt).itemsize for s, dt in self.shards]
        done = [sum(sizes[:t + 1]) / sum(sizes) for t in range(len(sizes))]
        return [(self.mid_at * f, functools.partial(self.mid, t)) for t, f in enumerate(done)]

    def mid(self, t, ins, outs, scr):
        cps = self._copies(outs, scr)
        for cp in cps["got_xy"][t]:
            cp.wait_recv()
        for cp in [cps["relay"][t]] + cps["down"][t]:
            cp.start()

    def finish(self, ins, outs, scr):
        cps = self._copies(outs, scr)
        for t in range(len(self.shards)):
            cps["got_diag"][t].wait_recv()
            cps["down_diag"][t].start()
        for t in range(len(self.shards)):
            for cp in cps["got_sib"][t]:
                cp.wait_recv()
            for cp in cps["own"][t] + [cps["relay"][t]] + cps["down"][t] + [cps["down_diag"][t]]:
                cp.wait_send()
            cps["local"][t].wait()


def _run_carry(carry, name):
    def body(*refs):
        n_in, n_out = len(carry.in_arrays), len(carry.out_shape)
        ins, outs, scr = refs[:n_in], refs[n_in:n_in + n_out], refs[n_in + n_out:]
        carry.start(ins, outs, scr)
        for _, stage in carry.stages():
            stage(ins, outs, scr)
        carry.finish(ins, outs, scr)

    return pl.pallas_call(
        body, in_specs=carry.in_specs, out_specs=carry.out_specs, out_shape=carry.out_shape,
        scratch_shapes=carry.scratch, compiler_params=pltpu.CompilerParams(vmem_limit_bytes=VMEM_LIMIT),
        name=name)(*carry.in_arrays)


class _ExchangeCarry:
    def __init__(self, names, grads, mid_at=0.5):
        nt = len(grads)
        self.mid_at = mid_at
        self.names = names
        self.in_arrays = [g.reshape(4, 2, g.shape[0] // NDEV, g.shape[1]) for g in grads]
        self.in_specs = [HBM] * nt
        blocks = [g.shape[2:] for g in self.in_arrays]
        self.out_shape = [jax.ShapeDtypeStruct((3,) + b, BF16) for b in blocks]
        self.out_specs = [HBM] * nt
        self.scratch = ([pltpu.VMEM((4,) + b, BF16) for b in blocks] * 2 + [pltpu.VMEM(b, BF16) for b in blocks]
                        + [pltpu.SemaphoreType.DMA((nt, 3)), pltpu.SemaphoreType.DMA((nt, 3))]
                        + [pltpu.SemaphoreType.DMA((nt,))] * 4)

    def _copies(self, ins, outs, scr):
        nt = len(ins)
        theirs, own, relayed = scr[:nt], scr[nt:2 * nt], scr[2 * nt:3 * nt]
        send_sems, recv_sems, keep_sems, load_sems, swap_send, swap_recv = scr[3 * nt:]
        x, y, c = _place()
        q = lambda cx, cy: 2 * cx + cy
        near, far = (x ^ c, y ^ (1 - c)), (x ^ (1 - c), y ^ c)

        def remote(t, k, src, dst, chip):
            return pltpu.make_async_remote_copy(
                src_ref=src, dst_ref=dst, send_sem=send_sems.at[t, k], recv_sem=recv_sems.at[t, k],
                device_id=(*chip, c), device_id_type=MESH)

        return dict(
            swap=[pltpu.make_async_remote_copy(
                src_ref=ins[t].at[:, 1 - c], dst_ref=theirs[t], send_sem=swap_send.at[t], recv_sem=swap_recv.at[t],
                device_id=(x, y, 1 - c), device_id_type=MESH) for t in range(nt)],
            load=[pltpu.make_async_copy(ins[t].at[:, c], own[t], load_sems.at[t]) for t in range(nt)],
            keep=[pltpu.make_async_copy(own[t].at[q(x, y)], outs[t].at[0], keep_sems.at[t]) for t in range(nt)],
            direct=[remote(t, 0, own[t].at[q(*near)], outs[t].at[1], near) for t in range(nt)],
            relay=[remote(t, 1, own[t].at[q(1 - x, 1 - y)], relayed[t], near) for t in range(nt)],
            merged=[remote(t, 2, own[t].at[q(*far)], outs[t].at[2], far) for t in range(nt)],
            theirs=theirs, own=own, relayed=relayed, far=q(*far))

    EARLY_AT = 0.1

    def stages(self):
        return [(self.EARLY_AT, self.early), (self.mid_at, self.mid)]

    def start(self, ins, outs, scr):
        cps = self._copies(ins, outs, scr)
        for t in range(len(ins)):
            cps["swap"][t].start()
            cps["load"][t].start()

    def early(self, ins, outs, scr):
        cps = self._copies(ins, outs, scr)
        for t in range(len(ins)):
            cps["load"][t].wait()
            cps["swap"][t].wait_recv()
            own, theirs = cps["own"][t], cps["theirs"][t]
            for j in range(4):
                own[j] = (own[j].astype(F32) + theirs[j].astype(F32)).astype(BF16)
            for kind in ("relay", "direct", "keep"):
                cps[kind][t].start()

    def mid(self, ins, outs, scr):
        cps = self._copies(ins, outs, scr)
        for t in range(len(ins)):
            cps["relay"][t].wait_recv()
            own, far = cps["own"][t], cps["far"]
            own[far] = (own[far].astype(F32) + cps["relayed"][t][...].astype(F32)).astype(BF16)
            cps["merged"][t].start()

    def finish(self, ins, outs, scr):
        cps = self._copies(ins, outs, scr)
        for t in range(len(ins)):
            cps["swap"][t].wait_send()
            cps["direct"][t].wait()
            cps["relay"][t].wait_send()
            cps["merged"][t].wait()
            cps["keep"][t].wait()


class _Comm:
    def __init__(self, groups):
        self.names = {tag: list(g) for tag, (g, _) in groups.items()}
        self.gathers = {tag: _GatherCarry(list(g.values()), mid_at) for tag, (g, mid_at) in groups.items()}
        self.reduced = {}
        self.last = None

    def gathered(self, tag, outs):
        return dict(zip(self.names[tag], outs))

    def reduce_start(self, grads):
        names = list(grads)
        return _ExchangeCarry(names, [grads[n] for n in names])

    def reduce_done(self, carry, outs):
        self.reduced.update(zip(carry.names, outs))


def _adamw_math(w, g, m, v):
    m = B1 * m + (1.0 - B1) * g
    v = B2 * v + (1.0 - B2) * (g * g)
    m_hat = m / (1.0 - B1 ** STEP)
    v_hat = v / (1.0 - B2 ** STEP)
    delta = -LR * (m_hat / (jnp.sqrt(v_hat) + AEPS) + WD * w)
    return delta, m, v


def _adamw_big(recv, w, m, v, name):
    def body(r_ref, w_ref, m_ref, v_ref, g_ref, d_ref, mo_ref, vo_ref):
        g = r_ref[0].astype(F32)
        for q in range(1, 3):
            g = g + r_ref[q].astype(F32)
        d, mn, vn = _adamw_math(w_ref[...], g, m_ref[...], v_ref[...])
        g_ref[...] = g
        d_ref[...] = d
        mo_ref[...] = mn
        vo_ref[...] = vn

    rows, n = w.shape
    tr = rows // 2
    row = pl.BlockSpec((tr, n), lambda t: (t, 0))
    return pl.pallas_call(
        body, grid=(2,), in_specs=[pl.BlockSpec((3, tr, n), lambda t: (0, t, 0)), row, row, row],
        out_specs=[row] * 4, out_shape=[jax.ShapeDtypeStruct(w.shape, F32)] * 4,
        compiler_params=_cp(("parallel",)), name=name)(recv, w, m, v)


SMALL_NAMES = ("ffn1_norm", "mix_norm", "ffn2_norm", "conv_b", "conv_ln_g", "conv_ln_b", "q_norm", "k_norm")
SROWS = 16
LOSS_ROW = len(SMALL_NAMES)
CWF = 2


def _small_step(gs, loss_row, gcw, ws, ms, vs, wcw, mcw, vcw, carry=None):
    ns = len(SMALL_NAMES)
    widths = [g.shape[1] for g in gs]

    def body(*refs):
        it = iter(refs)
        take = lambda n: [next(it) for _ in range(n)]
        g_refs, (loss_ref, gcw_ref) = take(ns), take(2)
        w_refs, m_refs, v_refs = take(ns), take(ns), take(ns)
        wcw_ref, mcw_ref, vcw_ref = take(3)
        cins = take(len(carry.in_arrays)) if carry else []
        outs = [take(4) for _ in range(ns)]
        cw_outs, (loss_out,) = take(4), take(1)
        couts = take(len(carry.out_shape)) if carry else []
        send, slots, cslots, send_sems, recv_sems, csend_sems, crecv_sems = take(7)
        cscr = list(it)
        x, y, c = _place()
        me = 4 * x + 2 * y + c
        send[...] = jnp.zeros_like(send)
        for k in range(ns):
            send[k:k + 1, 0:widths[k]] = g_refs[k][...]
        send[LOSS_ROW:LOSS_ROW + 1, 0:128] = loss_ref[...]
        slots[me] = send[...]
        cslots[me] = gcw_ref[...]
        cps = []
        for k in range(1, NDEV):
            peer = (x ^ ((k >> 2) & 1), y ^ ((k >> 1) & 1), c ^ (k & 1))
            cps.append(pltpu.make_async_remote_copy(
                src_ref=send, dst_ref=slots.at[me], send_sem=send_sems.at[k - 1], recv_sem=recv_sems.at[k - 1],
                device_id=peer, device_id_type=MESH))
            cps.append(pltpu.make_async_remote_copy(
                src_ref=gcw_ref, dst_ref=cslots.at[me], send_sem=csend_sems.at[k - 1],
                recv_sem=crecv_sems.at[k - 1], device_id=peer, device_id_type=MESH))
        for cp in cps:
            cp.start()
        stages = [stage for _, stage in carry.stages()] if carry else []
        if carry:
            carry.start(cins, couts, cscr)
        for stage in stages[:1]:
            stage(cins, couts, cscr)
        for cp in cps:
            cp.wait()
        tot = slots[0]
        ctot = cslots[0, me]
        for j in range(1, NDEV):
            tot = tot + slots[j]
            ctot = ctot + cslots[j, me]

        def step(g, w_ref, m_ref, v_ref, o):
            d, mn, vn = _adamw_math(w_ref[...], g, m_ref[...], v_ref[...])
            o[0][...], o[1][...], o[2][...], o[3][...] = g, d, mn, vn

        for k in range(ns):
            step(tot[k:k + 1, 0:widths[k]], w_refs[k], m_refs[k], v_refs[k], outs[k])
        step(ctot, wcw_ref, mcw_ref, vcw_ref, cw_outs)
        loss_out[...] = tot[LOSS_ROW:LOSS_ROW + 1, 0:128]
        for stage in stages[1:]:
            stage(cins, couts, cscr)
        if carry:
            carry.finish(cins, couts, cscr)

    args = [*gs, loss_row, gcw, *ws, *ms, *vs, wcw, mcw, vcw]
    out_shape = ([jax.ShapeDtypeStruct((1, n), F32) for n in widths for _ in range(4)]
                 + [jax.ShapeDtypeStruct((CWF, D), F32)] * 4 + [jax.ShapeDtypeStruct((1, 128), F32)])
    n_own = len(out_shape)
    res = pl.pallas_call(
        body, in_specs=[VMEM] * len(args) + (carry.in_specs if carry else []),
        out_specs=[VMEM] * n_own + (carry.out_specs if carry else []),
        out_shape=out_shape + (carry.out_shape if carry else []),
        scratch_shapes=[pltpu.VMEM((SROWS, D), F32), pltpu.VMEM((NDEV, SROWS, D), F32),
                        pltpu.VMEM((NDEV, NDEV, CWF, D), F32)]
                       + [pltpu.SemaphoreType.DMA((NDEV - 1,))] * 4 + (carry.scratch if carry else []),
        name="small_step")(*args, *(carry.in_arrays if carry else []))
    per = [res[4 * k:4 * k + 4] for k in range(ns)]
    return per, res[4 * ns:4 * ns + 4], res[n_own - 1], res[n_own:]


def _pack_cw(a):
    flat = a.reshape(a.shape[:-2] + (CK * HD,))
    pad = [(0, 0)] * (flat.ndim - 1) + [(0, CWF * D - CK * HD)]
    return jnp.pad(flat, pad).reshape(a.shape[:-2] + (CWF, D))


def _unpack_cw(v):
    return v.reshape(-1)[:CK * HD].reshape(1, CK, HD)


def kernel(x, ffn1_norm, ffn1_w_gate, ffn1_w_up, ffn1_w_down, mix_norm, w_in, q_norm, k_norm, conv_w, conv_b, conv_ln_g, conv_ln_b, w_out, ffn2_norm, ffn2_w_gate, ffn2_w_up, ffn2_w_down, loss_target, m_ffn1_norm, m_ffn1_w_gate, m_ffn1_w_up, m_ffn1_w_down, m_mix_norm, m_w_in, m_q_norm, m_k_norm, m_conv_w, m_conv_b, m_conv_ln_g, m_conv_ln_b, m_w_out, m_ffn2_norm, m_ffn2_w_gate, m_ffn2_w_up, m_ffn2_w_down, v_ffn1_norm, v_ffn1_w_gate, v_ffn1_w_up, v_ffn1_w_down, v_mix_norm, v_w_in, v_q_norm, v_k_norm, v_conv_w, v_conv_b, v_conv_ln_g, v_conv_ln_b, v_w_out, v_ffn2_norm, v_ffn2_w_gate, v_ffn2_w_up, v_ffn2_w_down):
    P = dict(ffn1_norm=ffn1_norm, ffn1_w_gate=ffn1_w_gate, ffn1_w_up=ffn1_w_up, ffn1_w_down=ffn1_w_down,
             mix_norm=mix_norm, w_in=w_in, q_norm=q_norm, k_norm=k_norm, conv_w=conv_w, conv_b=conv_b,
             conv_ln_g=conv_ln_g, conv_ln_b=conv_ln_b, w_out=w_out, ffn2_norm=ffn2_norm,
             ffn2_w_gate=ffn2_w_gate, ffn2_w_up=ffn2_w_up, ffn2_w_down=ffn2_w_down)
    M = dict(ffn1_norm=m_ffn1_norm, ffn1_w_gate=m_ffn1_w_gate, ffn1_w_up=m_ffn1_w_up, ffn1_w_down=m_ffn1_w_down,
             mix_norm=m_mix_norm, w_in=m_w_in, q_norm=m_q_norm, k_norm=m_k_norm, conv_w=m_conv_w, conv_b=m_conv_b,
             conv_ln_g=m_conv_ln_g, conv_ln_b=m_conv_ln_b, w_out=m_w_out, ffn2_norm=m_ffn2_norm,
             ffn2_w_gate=m_ffn2_w_gate, ffn2_w_up=m_ffn2_w_up, ffn2_w_down=m_ffn2_w_down)
    V = dict(ffn1_norm=v_ffn1_norm, ffn1_w_gate=v_ffn1_w_gate, ffn1_w_up=v_ffn1_w_up, ffn1_w_down=v_ffn1_w_down,
             mix_norm=v_mix_norm, w_in=v_w_in, q_norm=v_q_norm, k_norm=v_k_norm, conv_w=v_conv_w, conv_b=v_conv_b,
             conv_ln_g=v_conv_ln_g, conv_ln_b=v_conv_ln_b, w_out=v_w_out, ffn2_norm=v_ffn2_norm,
             ffn2_w_gate=v_ffn2_w_gate, ffn2_w_up=v_ffn2_w_up, ffn2_w_down=v_ffn2_w_down)
    order = ["ffn1_norm", "ffn1_w_gate", "ffn1_w_up", "ffn1_w_down", "mix_norm", "w_in", "q_norm", "k_norm",
             "conv_w", "conv_b", "conv_ln_g", "conv_ln_b", "w_out", "ffn2_norm", "ffn2_w_gate", "ffn2_w_up",
             "ffn2_w_down"]
    B, S, _ = x.shape
    T = B * S

    bigs = [("wg1", "ffn1_w_gate", True), ("wu1", "ffn1_w_up", True), ("wd1", "ffn1_w_down", False),
            ("win", "w_in", True), ("wout", "w_out", False),
            ("wg2", "ffn2_w_gate", True), ("wu2", "ffn2_w_up", True), ("wd2", "ffn2_w_down", False)]
    hm = lambda a, tr: jnp.transpose(a[0]) if tr else a[0]
    cw_pad = jnp.zeros((32, 128), F32).at[0:CK, 0:HD].set(conv_w[0])
    shard = {ln: (hm(P[pn], tr), BF16) for ln, pn, tr in bigs}
    gathered = _run_carry(_GatherCarry([shard["wg1"], shard["wu1"], (cw_pad, F32)]), "gather_first")
    W = {"wg1": gathered[0], "wu1": gathered[1]}
    cwg = gathered[2].reshape(NDEV, 32, 128)[:, 0:CK, 0:HD]
    W["conv_w"] = jnp.transpose(cwg, (1, 0, 2)).reshape(CK, DC)
    norms = {n: P[n] for n in SMALL_NAMES}
    comm = _Comm({"down_in": ({n: shard[n] for n in ("wd1", "win")}, 0.5),
                  "ffn2": ({n: shard[n] for n in ("wg2", "wu2", "wd2", "wout")}, 0.5)})

    loss_part, gx, _, small = _local_step(x.reshape(T, D), loss_target.reshape(T, D), norms, W, B, S, comm)

    G, Dl, Mn, Vn = {}, {}, {}, {}
    dcw = small["conv_w"].reshape(CK, NDEV, HD).transpose(1, 0, 2)
    loss_row = jnp.zeros((1, 128), F32).at[0, 0].set(loss_part)
    per, cw_outs, loss_out, got = _small_step(
        [small[n] for n in SMALL_NAMES], loss_row, _pack_cw(dcw),
        [P[n] for n in SMALL_NAMES], [M[n] for n in SMALL_NAMES], [V[n] for n in SMALL_NAMES],
        _pack_cw(P["conv_w"][0]), _pack_cw(M["conv_w"][0]), _pack_cw(V["conv_w"][0]), carry=comm.last)
    comm.reduce_done(comm.last, got)
    loss = loss_out[0, 0]
    for n, outs in zip(SMALL_NAMES, per):
        G[n], Dl[n], Mn[n], Vn[n] = outs
    G["conv_w"], Dl["conv_w"], Mn["conv_w"], Vn["conv_w"] = [_unpack_cw(o) for o in cw_outs]

    for ln, pn, tr in bigs:
        outs = _adamw_big(comm.reduced[ln], hm(P[pn], tr), hm(M[pn], tr), hm(V[pn], tr), "adamw_" + ln)
        G[pn], Dl[pn], Mn[pn], Vn[pn] = [(jnp.transpose(o) if tr else o)[None] for o in outs]

    return (loss, gx.reshape(B, S, D), *[G[n] for n in order], *[Dl[n] for n in order],
            *[Mn[n] for n in order], *[Vn[n] for n in order])
```

```python
import functools

import jax
import jax.numpy as jnp
from jax import lax
from jax.experimental import pallas as pl
from jax.experimental.pallas import tpu as pltpu

F32 = jnp.float32
BF16 = jnp.bfloat16

D = 1024
FF = 2816
HD = 64
DA = 512
DC = 512
DIN = 2560
CK = 31
BLK = 128
DILS = (1, 4, 16)
EPS = 1e-6
NDEV = 8
MESH = pl.DeviceIdType.MESH

LR, B1, B2, AEPS, WD, STEP = 0.001, 0.9, 0.999, 1e-08, 0.01, 10

NT = (((1,), (1,)), ((), ()))
TN = (((0,), (0,)), ((), ()))

VMEM_LIMIT = 60 * 1024 * 1024


def _cp(sem=None):
    return pltpu.CompilerParams(dimension_semantics=sem, vmem_limit_bytes=VMEM_LIMIT)


def _sigmoid(x):
    return 0.5 * (jnp.tanh(0.5 * x) + 1.0)


def _pallas(body, args, *, grid, in_specs, out_specs, out_shape, scratch_shapes, sem, name, carry=None):
    if carry is None:
        outs = pl.pallas_call(body, grid=grid, in_specs=in_specs, out_specs=out_specs, out_shape=out_shape,
                              scratch_shapes=scratch_shapes, compiler_params=_cp(sem), name=name)(*args)
        return outs, None
    n_in, n_out, n_scr = len(in_specs), len(out_shape), len(scratch_shapes)
    c_in, c_out = len(carry.in_arrays), len(carry.out_shape)

    def wrapped(*refs):
        ins, refs = refs[:n_in], refs[n_in:]
        cins, refs = refs[:c_in], refs[c_in:]
        outs, refs = refs[:n_out], refs[n_out:]
        couts, refs = refs[:c_out], refs[c_out:]
        scr, cscr = refs[:n_scr], refs[n_scr:]
        ids = [pl.program_id(a) for a in range(len(grid))]
        step = ids[0]
        for i, n in zip(ids[1:], grid[1:]):
            step = step * n + i
        steps = functools.reduce(lambda a, b: a * b, grid)

        @pl.when(step == 0)
        def _():
            carry.start(cins, couts, cscr)

        body(*ins, *outs, *scr)

        for frac, stage in carry.stages():
            @pl.when(step == int(steps * frac))
            def _(stage=stage):
                stage(cins, couts, cscr)

        @pl.when(step == steps - 1)
        def _():
            carry.finish(cins, couts, cscr)

    outs = pl.pallas_call(
        wrapped, grid=grid, in_specs=list(in_specs) + carry.in_specs, out_specs=list(out_specs) + carry.out_specs,
        out_shape=list(out_shape) + carry.out_shape, scratch_shapes=list(scratch_shapes) + carry.scratch,
        compiler_params=_cp(("arbitrary",) * len(grid)), name=name)(*args, *carry.in_arrays)
    return outs[:n_out], outs[n_out:]


FC = 256


def _resident(shape):
    return pl.BlockSpec(shape, lambda *_: (0,) * len(shape), pipeline_mode=pl.Buffered(1))


def _mix_out_ffn_loss(h1, attn, conv, wout, gain, wg, wu, wd, target, name):
    T = h1.shape[0]
    tm = 512
    nt = T // tm

    def body(h1_ref, at_ref, cv_ref, wo_ref, gain_ref, wg_ref, wu_ref, wd_ref, t_ref,
             h2_ref, n_ref, g_ref, u_ref, dout_ref, dyb_ref, sq_ref, a_hbm, a_scr, a_sem):
        t = pl.program_id(0)
        a_out = lambda i: pltpu.make_async_copy(a_scr, a_hbm.at[pl.ds(pl.multiple_of(i * tm, tm), tm), :], a_sem)

        @pl.when(t > 0)
        def _():
            a_out(t - 1).wait()

        xv = (h1_ref[...]
              + jnp.dot(at_ref[...], wo_ref[0:DA, :], preferred_element_type=F32)
              + jnp.dot(cv_ref[...], wo_ref[DA:D, :], preferred_element_type=F32))
        h2_ref[...] = xv
        r = lax.rsqrt(jnp.mean(xv * xv, axis=-1, keepdims=True) + EPS)
        n_ref[...] = (xv * r * gain_ref[...]).astype(BF16)
        for c in range(FF // FC):
            cols = slice(c * FC, (c + 1) * FC)
            nb = n_ref[...]
            g = lax.dot_general(nb, wg_ref[cols, :], NT, preferred_element_type=F32)
            u = lax.dot_general(nb, wu_ref[cols, :], NT, preferred_element_type=F32)
            g_ref[:, cols] = g.astype(BF16)
            u_ref[:, cols] = u.astype(BF16)
            a_scr[:, cols] = (g * _sigmoid(g) * u).astype(BF16)
        a_out(t).start()
        e = h2_ref[...] + 0.5 * jnp.dot(a_scr[...], wd_ref[...], preferred_element_type=F32) - t_ref[...]
        dout = e * (1.0 / D)
        dout_ref[...] = dout
        dyb_ref[...] = (0.5 * dout).astype(BF16)
        sq_ref[...] = jnp.sum(e * e, axis=0, keepdims=True)[None]

        @pl.when(t == nt - 1)
        def _():
            a_out(t).wait()

    row = pl.BlockSpec((tm, D), lambda t: (t, 0))
    half = pl.BlockSpec((tm, DA), lambda t: (t, 0))
    wide = pl.BlockSpec((tm, FF), lambda t: (t, 0))
    outs, _ = _pallas(
        body, (h1, attn, conv, wout, gain, wg, wu, wd, target), grid=(nt,),
        in_specs=[row, half, half, _resident((D, D)), _resident((1, D)), _resident((FF, D)), _resident((FF, D)),
                  _resident((FF, D)), row],
        out_specs=[row, row, wide, wide, row, row, pl.BlockSpec((1, 1, D), lambda t: (t, 0, 0)), HBM],
        out_shape=[jax.ShapeDtypeStruct((T, D), F32), jax.ShapeDtypeStruct((T, D), BF16)]
                  + [jax.ShapeDtypeStruct((T, FF), BF16)] * 2
                  + [jax.ShapeDtypeStruct((T, D), F32), jax.ShapeDtypeStruct((T, D), BF16),
                     jax.ShapeDtypeStruct((nt, 1, D), F32), jax.ShapeDtypeStruct((T, FF), BF16)],
        scratch_shapes=[pltpu.VMEM((tm, FF), BF16), pltpu.SemaphoreType.DMA(())],
        sem=("arbitrary",), name=name)
    return outs


def _ffn_gate_up(x, gain, wg, wu, name, carry=None):
    T = x.shape[0]
    tm = 512

    def body(x_ref, gain_ref, wg_ref, wu_ref, n_ref, g_ref, u_ref, a_ref):
        xv = x_ref[...]
        r = lax.rsqrt(jnp.mean(xv * xv, axis=-1, keepdims=True) + EPS)
        n_ref[...] = (xv * r * gain_ref[...]).astype(BF16)
        for c in range(FF // FC):
            cols = slice(c * FC, (c + 1) * FC)
            nb = n_ref[...]
            g = lax.dot_general(nb, wg_ref[cols, :], NT, preferred_element_type=F32)
            u = lax.dot_general(nb, wu_ref[cols, :], NT, preferred_element_type=F32)
            g_ref[:, cols] = g.astype(BF16)
            u_ref[:, cols] = u.astype(BF16)
            a_ref[:, cols] = (g * _sigmoid(g) * u).astype(BF16)

    row = pl.BlockSpec((tm, D), lambda t: (t, 0))
    wide = pl.BlockSpec((tm, FF), lambda t: (t, 0))
    return _pallas(
        body, (x, gain, wg, wu), grid=(T // tm,),
        in_specs=[row, _resident((1, D)), _resident((FF, D)), _resident((FF, D))],
        out_specs=[row, wide, wide, wide],
        out_shape=[jax.ShapeDtypeStruct((T, D), BF16)] + [jax.ShapeDtypeStruct((T, FF), BF16)] * 3,
        scratch_shapes=[], sem=("parallel",), name=name, carry=carry)


def _ffn_down_mix_in(x, a, wd, gain, win, name):
    T = x.shape[0]
    tm = 512

    def body(x_ref, a_ref, wd_ref, gain_ref, win_ref, h_ref, u_ref, n_ref):
        hv = x_ref[...] + 0.5 * jnp.dot(a_ref[...], wd_ref[...], preferred_element_type=F32)
        h_ref[...] = hv
        r = lax.rsqrt(jnp.mean(hv * hv, axis=-1, keepdims=True) + EPS)
        n_ref[...] = (hv * r * gain_ref[...]).astype(BF16)
        u_ref[...] = lax.dot_general(n_ref[...], win_ref[...], NT, preferred_element_type=F32).astype(BF16)

    row = pl.BlockSpec((tm, D), lambda t: (t, 0))
    wide = pl.BlockSpec((tm, FF), lambda t: (t, 0))
    outs, _ = _pallas(
        body, (x, a, wd, gain, win), grid=(T // tm,),
        in_specs=[row, wide, _resident((FF, D)), _resident((1, D)), _resident((DIN, D))],
        out_specs=[row, pl.BlockSpec((tm, DIN), lambda t: (t, 0)), row],
        out_shape=[jax.ShapeDtypeStruct((T, D), F32), jax.ShapeDtypeStruct((T, DIN), BF16),
                   jax.ShapeDtypeStruct((T, D), BF16)],
        scratch_shapes=[], sem=("parallel",), name=name)
    return outs


def _ffn_bwd_act(dyb, g, u, x, dout, gain, wg, wu, wd, name, carry=None):
    T = x.shape[0]
    tm = 256
    nt = T // tm

    def body(dy_ref, g_ref, u_ref, x_ref, dout_ref, gain_ref, wg_ref, wu_ref, wd_ref,
             dg_ref, du_ref, dx_ref, dgn_ref):
        for c in range(FF // FC):
            cols = slice(c * FC, (c + 1) * FC)
            da = lax.dot_general(dy_ref[...], wd_ref[cols, :], NT, preferred_element_type=F32)
            gv = g_ref[:, cols].astype(F32)
            uv = u_ref[:, cols].astype(F32)
            sg = _sigmoid(gv)
            dg_ref[:, cols] = (da * uv * (sg * (1.0 + gv * (1.0 - sg)))).astype(BF16)
            du_ref[:, cols] = (da * (gv * sg)).astype(BF16)
        dn = (jnp.dot(dg_ref[...], wg_ref[...], preferred_element_type=F32)
              + jnp.dot(du_ref[...], wu_ref[...], preferred_element_type=F32))
        dx, dgain = _rms_bwd_rows(dn, x_ref[...], gain_ref[...])
        dx_ref[...] = dout_ref[...] + dx
        dgn_ref[...] = dgain[None]

    row = pl.BlockSpec((tm, D), lambda t: (t, 0))
    wide = pl.BlockSpec((tm, FF), lambda t: (t, 0))
    return _pallas(
        body, (dyb, g, u, x, dout, gain, wg, wu, wd), grid=(nt,),
        in_specs=[row, wide, wide, row, row, _resident((1, D)), _resident((FF, D)), _resident((FF, D)),
                  _resident((FF, D))],
        out_specs=[wide, wide, row, pl.BlockSpec((1, 1, D), lambda t: (t, 0, 0))],
        out_shape=[jax.ShapeDtypeStruct((T, FF), BF16)] * 2
                  + [jax.ShapeDtypeStruct((T, D), F32), jax.ShapeDtypeStruct((nt, 1, D), F32)],
        scratch_shapes=[], sem=("parallel",), name=name, carry=carry)


def _ffn_bwd_w(lhs, rhs, name, carry=None):
    T = rhs.shape[0]
    tf = 256

    def body(l_ref, r_ref, dw_ref):
        dw_ref[...] = lax.dot_general(l_ref[...], r_ref[...], TN, preferred_element_type=F32).astype(BF16)

    (dw,), got = _pallas(
        body, (lhs, rhs), grid=(FF // tf,),
        in_specs=[pl.BlockSpec((T, tf), lambda f: (0, f)), _resident((T, D))],
        out_specs=[pl.BlockSpec((tf, D), lambda f: (f, 0))], out_shape=[jax.ShapeDtypeStruct((FF, D), BF16)],
        scratch_shapes=[], sem=("parallel",), name=name, carry=carry)
    return dw, got


def _rms_bwd_rows(dn, xv, gain):
    r = lax.rsqrt(jnp.mean(xv * xv, axis=-1, keepdims=True) + EPS)
    xhat = xv * r
    dxhat = dn * gain
    dx = r * (dxhat - xhat * jnp.mean(dxhat * xhat, axis=-1, keepdims=True))
    return dx, jnp.sum(dn * xhat, axis=0, keepdims=True)


def _mix_out_bwd(dh, attn, conv, wout):
    T = dh.shape[0]
    tm = 512
    nt = T // tm

    def body(dh_ref, a_ref, c_ref, w_ref, da_ref, dc_ref, dw_ref, acc_scr):
        t = pl.program_id(0)

        @pl.when(t == 0)
        def _():
            acc_scr[...] = jnp.zeros_like(acc_scr)

        dhb = dh_ref[...].astype(BF16)
        dmix = lax.dot_general(dhb, w_ref[...], NT, preferred_element_type=F32)
        da_ref[...] = dmix[:, 0:DA].astype(BF16)
        dc_ref[...] = dmix[:, DA:D].astype(BF16)
        acc_scr[0:DA, :] += lax.dot_general(a_ref[...], dhb, TN, preferred_element_type=F32)
        acc_scr[DA:D, :] += lax.dot_general(c_ref[...], dhb, TN, preferred_element_type=F32)

        @pl.when(t == nt - 1)
        def _():
            dw_ref[...] = acc_scr[...].astype(BF16)

    row = pl.BlockSpec((tm, D), lambda t: (t, 0))
    half = pl.BlockSpec((tm, DA), lambda t: (t, 0))
    full = pl.BlockSpec((D, D), lambda t: (0, 0))
    return pl.pallas_call(
        body, grid=(nt,), in_specs=[row, half, half, full], out_specs=[half, half, full],
        out_shape=[jax.ShapeDtypeStruct((T, DA), BF16)] * 2 + [jax.ShapeDtypeStruct((D, D), BF16)],
        scratch_shapes=[pltpu.VMEM((D, D), F32)],
        compiler_params=_cp(("arbitrary",)), name="mix_out_bwd")(dh, attn, conv, wout)


def _mix_in_bwd(dparts, win, nb, h, dh, gain):
    T = h.shape[0]
    tm = 512
    nt = T // tm

    def body(d0, d1, d2, d3, d4, w_ref, n_ref, h_ref, dh_ref, gain_ref,
             dw_ref, dx_ref, dyb_ref, dg_ref, acc_scr):
        t = pl.program_id(0)

        @pl.when(t == 0)
        def _():
            acc_scr[...] = jnp.zeros_like(acc_scr)

        n = n_ref[...]
        dn = jnp.zeros((tm, D), F32)
        for i, d_ref in enumerate((d0, d1, d2, d3, d4)):
            dv = d_ref[...]
            dn = dn + jnp.dot(dv, w_ref[i * DA:(i + 1) * DA, :], preferred_element_type=F32)
            acc_scr[i * DA:(i + 1) * DA, :] += lax.dot_general(dv, n, TN, preferred_element_type=F32)
        dx, dgain = _rms_bwd_rows(dn, h_ref[...], gain_ref[...])
        tot = dh_ref[...] + dx
        dx_ref[...] = tot
        dyb_ref[...] = (0.5 * tot).astype(BF16)
        dg_ref[...] = dgain[None]

        @pl.when(t == nt - 1)
        def _():
            dw_ref[...] = acc_scr[...].astype(BF16)

    row = pl.BlockSpec((tm, D), lambda t: (t, 0))
    half = pl.BlockSpec((tm, DA), lambda t: (t, 0))
    full = pl.BlockSpec((DIN, D), lambda t: (0, 0))
    return pl.pallas_call(
        body, grid=(nt,),
        in_specs=[half] * 5 + [full, row, row, row, pl.BlockSpec((1, D), lambda t: (0, 0))],
        out_specs=[full, row, row, pl.BlockSpec((1, 1, D), lambda t: (t, 0, 0))],
        out_shape=[jax.ShapeDtypeStruct((DIN, D), BF16), jax.ShapeDtypeStruct((T, D), F32),
                   jax.ShapeDtypeStruct((T, D), BF16), jax.ShapeDtypeStruct((nt, 1, D), F32)],
        scratch_shapes=[pltpu.VMEM((DIN, D), F32)],
        compiler_params=_cp(("arbitrary",)), name="mix_in_bwd")(*dparts, win, nb, h, dh, gain)


def _head_masks():
    lane = lax.broadcasted_iota(jnp.int32, (1, 2 * HD), 1)
    m0 = lane < HD
    return m0, jnp.logical_not(m0)


def _stack_heads(v, m0, m1):
    z = jnp.zeros_like(v)
    return jnp.concatenate([jnp.where(m0, v, z), jnp.where(m1, v, z)], axis=0)


def _unstack_heads(v2, m0):
    return jnp.where(m0, v2[0:BLK], v2[BLK:2 * BLK])


def _head_sums(xv):
    ri = lax.broadcasted_iota(jnp.int32, (2 * HD, 2 * HD), 0)
    ci = lax.broadcasted_iota(jnp.int32, (2 * HD, 2 * HD), 1)
    ones = jnp.where((ri < HD) == (ci < HD), 1.0, 0.0).astype(BF16)
    hi = xv.astype(BF16)
    lo = (xv - hi.astype(F32)).astype(BF16)
    return (jnp.dot(hi, ones, preferred_element_type=F32) + jnp.dot(lo, ones, preferred_element_type=F32))


def _head_rms(xv):
    return lax.rsqrt(_head_sums(xv * xv) * (1.0 / HD) + EPS)


def _band_mask(first):
    qi = lax.broadcasted_iota(jnp.int32, (BLK, 2 * BLK), 0)
    ci = lax.broadcasted_iota(jnp.int32, (BLK, 2 * BLK), 1)
    band = (ci >= qi) & (ci <= qi + BLK)
    return band & ((ci >= BLK) | jnp.logical_not(first))


def _block_rows(j, d, seg):
    r, n = j // seg, j % seg
    start = r + (d * BLK) * n
    first = n == 0
    prev = jnp.where(first, start, start - d * BLK)
    return pl.ds(start, BLK, stride=d), pl.ds(prev, BLK, stride=d), first


def _block_keys(refs, cur, prev, first, single):
    if single:
        qi = lax.broadcasted_iota(jnp.int32, (BLK, BLK), 0)
        ci = lax.broadcasted_iota(jnp.int32, (BLK, BLK), 1)
        return [r[cur, :].astype(BF16) for r in refs], ci <= qi
    return ([jnp.concatenate([r[prev, :], r[cur, :]], axis=0).astype(BF16) for r in refs], _band_mask(first))


def _attn_fwd(u, qg2, kg2, B, S, carry=None):
    T = B * S
    NB = S // BLK
    scale = HD ** -0.5

    def body(q_ref, k_ref, v_ref, qg_ref, kg_ref, o_ref, lse_ref, qn, kn, vn, os_, ls_):
        m0, m1 = _head_masks()
        qv = q_ref[...].astype(F32)
        qn[...] = qv * _head_rms(qv) * (qg_ref[...] * scale)
        kv = k_ref[...].astype(F32)
        kn[...] = kv * _head_rms(kv) * kg_ref[...]
        vn[...] = v_ref[...].astype(F32)

        for i, d in enumerate(DILS):
            seg = NB // d

            def blk(j, c, i=i, d=d, seg=seg):
                cur, prev, first = _block_rows(j, d, seg)
                q2 = _stack_heads(qn[cur, :].astype(BF16), m0, m1)
                (kk, vv), mask = _block_keys((kn, vn), cur, prev, first, False)
                s = lax.dot_general(q2, kk, NT, preferred_element_type=F32)
                s = jnp.where(jnp.concatenate([mask, mask], axis=0), s, -1e30)
                mx = jnp.max(s, axis=-1, keepdims=True)
                p = jnp.exp(s - mx)
                l = jnp.sum(p, axis=-1, keepdims=True)
                o2 = jnp.dot((p * (1.0 / l)).astype(BF16), vv, preferred_element_type=F32)
                os_[i, cur, :] = _unstack_heads(o2, m0)
                ls_[i, cur, :] = _unstack_heads(mx + jnp.log(l), m0)
                return c

            lax.fori_loop(0, NB, blk, 0, unroll=8)

        def comb(c, carry):
            rows = pl.ds(pl.multiple_of(c * 256, 256), 256)
            l0, l1, l2 = ls_[0, rows, :], ls_[1, rows, :], ls_[2, rows, :]
            mx = jnp.maximum(jnp.maximum(l0, l1), l2)
            e0, e1, e2 = jnp.exp(l0 - mx), jnp.exp(l1 - mx), jnp.exp(l2 - mx)
            tot = e0 + e1 + e2
            inv = 1.0 / tot
            o = (e0 * os_[0, rows, :] + e1 * os_[1, rows, :] + e2 * os_[2, rows, :]) * inv
            o_ref[rows, :] = o.astype(BF16)
            lse_ref[rows, :] = mx + jnp.log(tot)
            return carry

        lax.fori_loop(0, S // 256, comb, 0)

    pair = 2 * HD
    blk_spec = lambda off: pl.BlockSpec((S, pair), lambda b, p, off=off: (b, off + p))
    gspec = pl.BlockSpec((1, pair), lambda b, p: (0, 0))
    return _pallas(
        body, (u, u, u, qg2, kg2), grid=(B, DA // pair),
        in_specs=[blk_spec(0), blk_spec(DA // pair), blk_spec(2 * DA // pair), gspec, gspec],
        out_specs=[blk_spec(0), blk_spec(0)],
        out_shape=[jax.ShapeDtypeStruct((T, DA), BF16), jax.ShapeDtypeStruct((T, DA), F32)],
        scratch_shapes=[pltpu.VMEM((S, pair), F32)] * 3 + [pltpu.VMEM((3, S, pair), F32)] * 2,
        sem=("parallel", "parallel"), name="attn_fwd", carry=carry)


def _attn_bwd(u, attn, dattn, lse, qg2, kg2, B, S, carry=None):
    T = B * S
    NB = S // BLK
    scale = HD ** -0.5
    pair = 2 * HD

    def body(q_ref, k_ref, v_ref, o_ref, do_ref, lse_ref, qg_ref, kg_ref,
             dq_ref, dk_ref, dv_ref, dgn_ref,
             qn, kn, vn, don, ldl, accq, acck, accv, rq, rk):
        m0, m1 = _head_masks()
        lane = lax.broadcasted_iota(jnp.int32, (1, pair), 1)
        qv = q_ref[...].astype(F32)
        rq[...] = _head_rms(qv)
        qn[...] = qv * rq[...] * (qg_ref[...] * scale)
        kv = k_ref[...].astype(F32)
        rk[...] = _head_rms(kv)
        kn[...] = kv * rk[...] * kg_ref[...]
        vn[...] = v_ref[...].astype(F32)
        dov = do_ref[...].astype(F32)
        don[...] = dov
        ldl[...] = jnp.where((lane % HD) < HD // 2, lse_ref[...], _head_sums(dov * o_ref[...].astype(F32)))

        for i, d in enumerate(DILS):
            seg = NB // d

            def blk(j, c, i=i, d=d, seg=seg):
                cur, prev, first = _block_rows(j, d, seg)
                q2 = _stack_heads(qn[cur, :].astype(BF16), m0, m1)
                do2 = _stack_heads(don[cur, :].astype(BF16), m0, m1)
                (kk, vv), mask = _block_keys((kn, vn), cur, prev, first, seg == 1)
                ldv = ldl[cur, :]
                lse2 = jnp.concatenate([ldv[:, 0:1], ldv[:, HD:HD + 1]], axis=0)
                dl2 = jnp.concatenate([ldv[:, HD // 2:HD // 2 + 1], ldv[:, HD + HD // 2:HD + HD // 2 + 1]], axis=0)
                s = lax.dot_general(q2, kk, NT, preferred_element_type=F32)
                p = jnp.where(jnp.concatenate([mask, mask], axis=0), jnp.exp(s - lse2), 0.0)
                dp = lax.dot_general(do2, vv, NT, preferred_element_type=F32)
                ds = (p * (dp - dl2)).astype(BF16)
                dq_acc = _unstack_heads(jnp.dot(ds, kk, preferred_element_type=F32), m0)
                dk_acc = lax.dot_general(ds, q2, TN, preferred_element_type=F32)
                dv_acc = lax.dot_general(p.astype(BF16), do2, TN, preferred_element_type=F32)
                if i == 0:
                    accq[cur, :] = dq_acc
                    acck[cur, :] = dk_acc[BLK:2 * BLK]
                    accv[cur, :] = dv_acc[BLK:2 * BLK]
                    acck[prev, :] += dk_acc[0:BLK]
                    accv[prev, :] += dv_acc[0:BLK]
                elif seg == 1:
                    accq[cur, :] += dq_acc
                    acck[cur, :] += dk_acc
                    accv[cur, :] += dv_acc
                else:
                    accq[cur, :] += dq_acc
                    acck[prev, :] += dk_acc[0:BLK]
                    acck[cur, :] += dk_acc[BLK:2 * BLK]
                    accv[prev, :] += dv_acc[0:BLK]
                    accv[cur, :] += dv_acc[BLK:2 * BLK]
                return c

            lax.fori_loop(0, NB, blk, 0, unroll=8)

        def norm_bwd(x_ref, r_ref, dn, gain):
            r = r_ref[...]
            xhat = x_ref[...].astype(F32) * r
            dxhat = dn * gain
            dx = r * (dxhat - xhat * (_head_sums(dxhat * xhat) * (1.0 / HD)))
            return dx, jnp.sum(dn * xhat, axis=0, keepdims=True)

        dq, dgq = norm_bwd(q_ref, rq, accq[...], qg_ref[...] * scale)
        dk, dgk = norm_bwd(k_ref, rk, acck[...], kg_ref[...])
        dq_ref[...] = dq.astype(BF16)
        dk_ref[...] = dk.astype(BF16)
        dv_ref[...] = accv[...].astype(BF16)
        dgn_ref[...] = jnp.concatenate([dgq * scale, dgk, jnp.zeros((6, pair), F32)], axis=0)[None]

    blk_spec = lambda off: pl.BlockSpec((S, pair), lambda b, p, off=off: (b, off + p))
    gspec = pl.BlockSpec((1, pair), lambda b, p: (0, 0))
    np_ = DA // pair
    return _pallas(
        body, (u, u, u, attn, dattn, lse, qg2, kg2), grid=(B, np_),
        in_specs=[blk_spec(0), blk_spec(np_), blk_spec(2 * np_), blk_spec(0), blk_spec(0), blk_spec(0),
                  gspec, gspec],
        out_specs=[blk_spec(0), blk_spec(0), blk_spec(0),
                   pl.BlockSpec((1, 8, pair), lambda b, p: (b * np_ + p, 0, 0))],
        out_shape=[jax.ShapeDtypeStruct((T, DA), BF16)] * 3 + [jax.ShapeDtypeStruct((B * np_, 8, pair), F32)],
        scratch_shapes=[pltpu.VMEM((S, pair), F32)] * 10,
        sem=("parallel", "parallel"), name="attn_bwd", carry=carry)


CT = 32
CPAD = 32


def _shifted(win, offsets):
    rolled, out = {}, {}
    n = win.shape[0]
    for o in offsets:
        sub = o % 8
        if sub not in rolled:
            rolled[sub] = win if sub == 0 else pltpu.roll(win, n - sub, 0)
        out[o] = rolled[sub][o - sub:o - sub + CT, :]
    return out


def _ln_fwd(y, g, b):
    mu = jnp.mean(y, axis=-1, keepdims=True)
    yc = y - mu
    rstd = lax.rsqrt(jnp.mean(yc * yc, axis=-1, keepdims=True) + EPS)
    xhat = yc * rstd
    return xhat, rstd, xhat * g + b


def _fill_glu(ca_ref, cg_ref, glu, S):
    glu[pl.ds(0, CPAD), :] = jnp.zeros((CPAD, DC), F32)

    def fill(i, c):
        rows = pl.ds(pl.multiple_of(i * 256, 256), 256)
        a = ca_ref[rows, :].astype(F32)
        gt = cg_ref[rows, :].astype(F32)
        glu[pl.ds(pl.multiple_of(CPAD + i * 256, CT), 256), :] = a * _sigmoid(gt)
        return c

    lax.fori_loop(0, S // 256, fill, 0)


def _conv_fwd(u, cw, cb, lg, lb, B, S):
    T = B * S

    def body(ca_ref, cg_ref, w_ref, b_ref, lg_ref, lb_ref, o_ref, y_ref, glu):
        _fill_glu(ca_ref, cg_ref, glu, S)

        def step(i, c):
            t0 = pl.multiple_of(i * CT, CT)
            win = glu[pl.ds(t0, 2 * CT), :]
            acc = jnp.zeros((CT, DC), F32) + b_ref[...]
            taps = _shifted(win, [k + 2 for k in range(CK)])
            for k in range(CK):
                acc = acc + taps[k + 2] * w_ref[k:k + 1, :]
            y_ref[pl.ds(t0, CT), :] = acc
            _, _, z = _ln_fwd(acc, lg_ref[...], lb_ref[...])
            o_ref[pl.ds(t0, CT), :] = (z * _sigmoid(z)).astype(BF16)
            return c

        lax.fori_loop(0, S // CT, step, 0, unroll=4)

    vec = pl.BlockSpec((1, DC), lambda b: (0, 0))
    return pl.pallas_call(
        body, grid=(B,),
        in_specs=[pl.BlockSpec((S, DC), lambda b: (b, 3)), pl.BlockSpec((S, DC), lambda b: (b, 4)),
                  pl.BlockSpec((CT, DC), lambda b: (0, 0)), vec, vec, vec],
        out_specs=[pl.BlockSpec((S, DC), lambda b: (b, 0))] * 2,
        out_shape=[jax.ShapeDtypeStruct((T, DC), BF16), jax.ShapeDtypeStruct((T, DC), F32)],
        scratch_shapes=[pltpu.VMEM((CPAD + S, DC), F32)],
        compiler_params=_cp(("parallel",)), name="conv_fwd")(u, u, cw, cb, lg, lb)


def _conv_bwd(u, y, dconv, cw, lg, lb, B, S):
    T = B * S

    def body(ca_ref, cg_ref, y_ref, dc_ref, w_ref, lg_ref, lb_ref,
             dca_ref, dcg_ref, dw_ref, ds_ref, glu, dyp, dwacc):
        _fill_glu(ca_ref, cg_ref, glu, S)
        dyp[pl.ds(S, CPAD), :] = jnp.zeros((CPAD, DC), F32)
        lgv, lbv = lg_ref[...], lb_ref[...]

        def sum8(v):
            return functools.reduce(jnp.add, [v[r:r + 8] for r in range(0, v.shape[0], 8)])

        P1 = 4 * CT

        def p1(i, carry):
            sb, sg, sl = carry
            t0 = pl.multiple_of(i * P1, P1)
            xhat, rstd, z = _ln_fwd(y_ref[pl.ds(t0, P1), :], lgv, lbv)
            sz = _sigmoid(z)
            dz = dc_ref[pl.ds(t0, P1), :].astype(F32) * (sz * (1.0 + z * (1.0 - sz)))
            dxhat = dz * lgv
            dy = rstd * (dxhat - jnp.mean(dxhat, axis=-1, keepdims=True)
                         - xhat * jnp.mean(dxhat * xhat, axis=-1, keepdims=True))
            dyp[pl.ds(t0, P1), :] = dy
            return sb + sum8(dy), sg + sum8(dz * xhat), sl + sum8(dz)

        z8 = jnp.zeros((8, DC), F32)
        sb, sg, sl = lax.fori_loop(0, S // P1, p1, (z8, z8, z8))
        rs = lambda v: jnp.sum(v, axis=0, keepdims=True)
        ds_ref[...] = jnp.concatenate([rs(sb), rs(sg), rs(sl), jnp.zeros((5, DC), F32)], axis=0)[None]

        def p2(i, c):
            t0 = pl.multiple_of(i * CT, CT)
            win = dyp[pl.ds(t0, 2 * CT), :]
            acc = jnp.zeros((CT, DC), F32)
            taps = _shifted(win, [30 - k for k in range(CK)])
            for k in range(CK):
                acc = acc + taps[30 - k] * w_ref[k:k + 1, :]
            a = ca_ref[pl.ds(t0, CT), :].astype(F32)
            sgt = _sigmoid(cg_ref[pl.ds(t0, CT), :].astype(F32))
            dca_ref[pl.ds(t0, CT), :] = (acc * sgt).astype(BF16)
            dcg_ref[pl.ds(t0, CT), :] = (acc * a * sgt * (1.0 - sgt)).astype(BF16)
            return c

        lax.fori_loop(0, S // CT, p2, 0)

        dwacc[...] = jnp.zeros_like(dwacc)

        def p3(i, c):
            t0 = pl.multiple_of(i * CT, CT)
            win = glu[pl.ds(t0, 2 * CT), :]
            dy = dyp[pl.ds(t0, CT), :]
            for k in range(CK):
                dwacc[k] += sum8(dy * win[k + 2:k + 2 + CT, :])
            return c

        lax.fori_loop(0, S // CT, p3, 0)
        dw_ref[...] = jnp.sum(dwacc[...], axis=1)[None]

    vec = pl.BlockSpec((1, DC), lambda b: (0, 0))
    seq = pl.BlockSpec((S, DC), lambda b: (b, 0))
    return pl.pallas_call(
        body, grid=(B,),
        in_specs=[pl.BlockSpec((S, DC), lambda b: (b, 3)), pl.BlockSpec((S, DC), lambda b: (b, 4)),
                  seq, seq, pl.BlockSpec((CT, DC), lambda b: (0, 0)), vec, vec],
        out_specs=[seq, seq, pl.BlockSpec((1, CT, DC), lambda b: (b, 0, 0)),
                   pl.BlockSpec((1, 8, DC), lambda b: (b, 0, 0))],
        out_shape=[jax.ShapeDtypeStruct((T, DC), BF16)] * 2
                  + [jax.ShapeDtypeStruct((B, CT, DC), F32), jax.ShapeDtypeStruct((B, 8, DC), F32)],
        scratch_shapes=[pltpu.VMEM((CPAD + S, DC), F32), pltpu.VMEM((S + CPAD, DC), F32),
                        pltpu.VMEM((CT, 8, DC), F32)],
        compiler_params=_cp(("parallel",)), name="conv_bwd")(u, u, y, dconv, cw, lg, lb)


def _local_step(x, target, norms, W, B, S, comm=None):
    qg2 = jnp.concatenate([norms["q_norm"], norms["q_norm"]], axis=1)
    kg2 = jnp.concatenate([norms["k_norm"], norms["k_norm"]], axis=1)
    cw = jnp.concatenate([W["conv_w"], jnp.zeros((1, DC), F32)], axis=0)

    W = dict(W)
    (n1, g1, u1, act1), got = _ffn_gate_up(x, norms["ffn1_norm"], W["wg1"], W["wu1"], "ffn1_gate_up",
                                           carry=comm.gathers["down_in"] if comm else None)
    if comm:
        W.update(comm.gathered("down_in", got))
    h1, u, n2 = _ffn_down_mix_in(x, act1, W["wd1"], norms["mix_norm"], W["win"], "ffn1_down_mix_in")
    (attn, lse), got = _attn_fwd(u, qg2, kg2, B, S, carry=comm.gathers["ffn2"] if comm else None)
    if comm:
        W = dict(W, **comm.gathered("ffn2", got))
    conv, y = _conv_fwd(u, cw, norms["conv_b"], norms["conv_ln_g"], norms["conv_ln_b"], B, S)
    h2, n3, g2, u2, dout, dyb, sq, act2 = _mix_out_ffn_loss(h1, attn, conv, W["wout"], norms["ffn2_norm"],
                                                            W["wg2"], W["wu2"], W["wd2"], target, "ffn2_fwd")
    loss = (0.5 / D) * jnp.sum(sq)

    (dg2, du2, dh2, dgn_ffn2), _ = _ffn_bwd_act(dyb, g2, u2, h2, dout, norms["ffn2_norm"],
                                               W["wg2"], W["wu2"], W["wd2"], "ffn2_bwd_act")
    dwd2, _ = _ffn_bwd_w(act2, dyb, "ffn2_bwd_wd")
    dwg2, _ = _ffn_bwd_w(dg2, n3, "ffn2_bwd_wg")
    dwu2, _ = _ffn_bwd_w(du2, n3, "ffn2_bwd_wu")
    dattn, dconv, dwout = _mix_out_bwd(dh2, attn, conv, W["wout"])
    carry = comm.reduce_start({"wg2": dwg2, "wu2": dwu2, "wd2": dwd2, "wout": dwout}) if comm else None
    (dq, dk, dv, dgn_qk), got = _attn_bwd(u, attn, dattn, lse, qg2, kg2, B, S, carry=carry)
    if comm:
        comm.reduce_done(carry, got)
    dca, dcg, dcw, dcs = _conv_bwd(u, y, dconv, cw, norms["conv_ln_g"], norms["conv_ln_b"], B, S)
    dwin, dh1, dyb1, dgn_mix = _mix_in_bwd((dq, dk, dv, dca, dcg), W["win"], n2, h1, dh2, norms["mix_norm"])
    (dg1, du1, gx, dgn_ffn1), _ = _ffn_bwd_act(dyb1, g1, u1, x, dh1, norms["ffn1_norm"],
                                              W["wg1"], W["wu1"], W["wd1"], "ffn1_bwd_act")
    carry = comm.reduce_start({"win": dwin}) if comm else None
    dwd1, got = _ffn_bwd_w(act1, dyb1, "ffn1_bwd_wd", carry=carry)
    if comm:
        comm.reduce_done(carry, got)
        carry = comm.reduce_start({"wd1": dwd1})
    dwg1, got = _ffn_bwd_w(dg1, n1, "ffn1_bwd_wg", carry=carry)
    if comm:
        comm.reduce_done(carry, got)
        carry = comm.reduce_start({"wg1": dwg1})
    dwu1, got = _ffn_bwd_w(du1, n1, "ffn1_bwd_wu", carry=carry)
    if comm:
        comm.reduce_done(carry, got)
        comm.last = comm.reduce_start({"wu1": dwu1})

    qk = jnp.sum(dgn_qk, axis=0)
    cs = jnp.sum(dcs, axis=0)
    small = {
        "ffn1_norm": jnp.sum(dgn_ffn1, axis=0),
        "mix_norm": jnp.sum(dgn_mix, axis=0),
        "q_norm": qk[0:1, 0:HD] + qk[0:1, HD:2 * HD],
        "k_norm": qk[1:2, 0:HD] + qk[1:2, HD:2 * HD],
        "conv_w": jnp.sum(dcw, axis=0)[0:CK],
        "conv_b": cs[0:1],
        "conv_ln_g": cs[1:2],
        "conv_ln_b": cs[2:3],
        "ffn2_norm": jnp.sum(dgn_ffn2, axis=0),
    }
    big = {"wg1": dwg1, "wu1": dwu1, "wd1": dwd1, "win": dwin, "wout": dwout,
           "wg2": dwg2, "wu2": dwu2, "wd2": dwd2}
    return loss, gx, big, small


HBM = pl.BlockSpec(memory_space=pltpu.HBM)
VMEM = pl.BlockSpec(memory_space=pltpu.VMEM)


def _place():
    return lax.axis_index("x"), lax.axis_index("y"), lax.axis_index("c")


class _GatherCarry:
    def __init__(self, shards, mid_at=0.5):
        nt = len(shards)
        self.mid_at = mid_at
        self.shards = shards
        self.in_arrays = [s for s, _ in shards]
        self.in_specs = [VMEM] * nt
        self.out_shape = [jax.ShapeDtypeStruct((NDEV * s.shape[0], s.shape[1]), dt) for s, dt in shards]
        self.out_specs = [HBM] * nt
        self.scratch = ([pltpu.VMEM(s.shape, dt) for s, dt in shards]
                        + [pltpu.SemaphoreType.DMA((nt, 7)), pltpu.SemaphoreType.DMA((nt, 7)),
                           pltpu.SemaphoreType.DMA((nt,))])

    def _copies(self, outs, scr):
        nt = len(self.shards)
        stages = scr[:nt]
        send_sems, recv_sems, local_sems = scr[nt:]
        x, y, c = _place()
        me, sibling = (x, y, c), (x, y, 1 - c)
        xn, yn, diag = (1 - x, y, c), (x, 1 - y, c), (1 - x, 1 - y, c)
        via = (x ^ c, y ^ (1 - c), c)
        onto = (x ^ (1 - c), y ^ c, c)

        def rows(t, px, py, pc):
            r = self.shards[t][0].shape[0]
            return outs[t].at[pl.ds((4 * px + 2 * py + pc) * r, r), :]

        def copy(t, k, block, to, src=None):
            return pltpu.make_async_remote_copy(
                src_ref=rows(t, *block) if src is None else src, dst_ref=rows(t, *block),
                send_sem=send_sems.at[t, k], recv_sem=recv_sems.at[t, k],
                device_id=to, device_id_type=MESH)

        sib = lambda b: (b[0], b[1], 1 - c)
        return dict(
            local=[pltpu.make_async_copy(stages[t], rows(t, *me), local_sems.at[t]) for t in range(nt)],
            own=[[copy(t, 0, me, sibling, src=stages[t]), copy(t, 1, me, xn, src=stages[t]),
                  copy(t, 2, me, yn, src=stages[t])] for t in range(nt)],
            relay=[copy(t, 3, via, onto) for t in range(nt)],
            down=[[copy(t, 4, xn, sibling), copy(t, 5, yn, sibling)] for t in range(nt)],
            down_diag=[copy(t, 6, diag, sibling) for t in range(nt)],
            got_xy=[[copy(t, 1, xn, me), copy(t, 2, yn, me)] for t in range(nt)],
            got_diag=[copy(t, 3, diag, me) for t in range(nt)],
            got_sib=[[copy(t, 0, sibling, me), copy(t, 4, sib(xn), me), copy(t, 5, sib(yn), me),
                      copy(t, 6, sib(diag), me)] for t in range(nt)])

    def start(self, ins, outs, scr):
        cps = self._copies(outs, scr)
        for t, (_, dt) in enumerate(self.shards):
            scr[t][...] = ins[t][...].astype(dt)
            for cp in [cps["local"][t]] + cps["own"][t]:
                cp.start()

    def stages(self):
        sizes = [s.size * jnp.dtype(dt).itemsize for s, dt in self.shards]
        done = [sum(sizes[:t + 1]) / sum(sizes) for t in range(len(sizes))]
        return [(self.mid_at * f, functools.partial(self.mid, t)) for t, f in enumerate(done)]

    def mid(self, t, ins, outs, scr):
        cps = self._copies(outs, scr)
        for cp in cps["got_xy"][t]:
            cp.wait_recv()
        for cp in [cps["relay"][t]] + cps["down"][t]:
            cp.start()

    def finish(self, ins, outs, scr):
        cps = self._copies(outs, scr)
        for t in range(len(self.shards)):
            cps["got_diag"][t].wait_recv()
            cps["down_diag"][t].start()
        for t in range(len(self.shards)):
            for cp in cps["got_sib"][t]:
                cp.wait_recv()
            for cp in cps["own"][t] + [cps["relay"][t]] + cps["down"][t] + [cps["down_diag"][t]]:
                cp.wait_send()
            cps["local"][t].wait()


def _run_carry(carry, name):
    def body(*refs):
        n_in, n_out = len(carry.in_arrays), len(carry.out_shape)
        ins, outs, scr = refs[:n_in], refs[n_in:n_in + n_out], refs[n_in + n_out:]
        carry.start(ins, outs, scr)
        for _, stage in carry.stages():
            stage(ins, outs, scr)
        carry.finish(ins, outs, scr)

    return pl.pallas_call(
        body, in_specs=carry.in_specs, out_specs=carry.out_specs, out_shape=carry.out_shape,
        scratch_shapes=carry.scratch, compiler_params=pltpu.CompilerParams(vmem_limit_bytes=VMEM_LIMIT),
        name=name)(*carry.in_arrays)


class _ExchangeCarry:
    def __init__(self, names, grads, mid_at=0.5):
        nt = len(grads)
        self.mid_at = mid_at
        self.names = names
        self.in_arrays = [g.reshape(4, 2, g.shape[0] // NDEV, g.shape[1]) for g in grads]
        self.in_specs = [HBM] * nt
        blocks = [g.shape[2:] for g in self.in_arrays]
        self.out_shape = [jax.ShapeDtypeStruct((3,) + b, BF16) for b in blocks]
        self.out_specs = [HBM] * nt
        self.scratch = ([pltpu.VMEM((4,) + b, BF16) for b in blocks] * 2 + [pltpu.VMEM(b, BF16) for b in blocks]
                        + [pltpu.SemaphoreType.DMA((nt, 3)), pltpu.SemaphoreType.DMA((nt, 3))]
                        + [pltpu.SemaphoreType.DMA((nt,))] * 4)

    def _copies(self, ins, outs, scr):
        nt = len(ins)
        theirs, own, relayed = scr[:nt], scr[nt:2 * nt], scr[2 * nt:3 * nt]
        send_sems, recv_sems, keep_sems, load_sems, swap_send, swap_recv = scr[3 * nt:]
        x, y, c = _place()
        q = lambda cx, cy: 2 * cx + cy
        near, far = (x ^ c, y ^ (1 - c)), (x ^ (1 - c), y ^ c)

        def remote(t, k, src, dst, chip):
            return pltpu.make_async_remote_copy(
                src_ref=src, dst_ref=dst, send_sem=send_sems.at[t, k], recv_sem=recv_sems.at[t, k],
                device_id=(*chip, c), device_id_type=MESH)

        return dict(
            swap=[pltpu.make_async_remote_copy(
                src_ref=ins[t].at[:, 1 - c], dst_ref=theirs[t], send_sem=swap_send.at[t], recv_sem=swap_recv.at[t],
                device_id=(x, y, 1 - c), device_id_type=MESH) for t in range(nt)],
            load=[pltpu.make_async_copy(ins[t].at[:, c], own[t], load_sems.at[t]) for t in range(nt)],
            keep=[pltpu.make_async_copy(own[t].at[q(x, y)], outs[t].at[0], keep_sems.at[t]) for t in range(nt)],
            direct=[remote(t, 0, own[t].at[q(*near)], outs[t].at[1], near) for t in range(nt)],
            relay=[remote(t, 1, own[t].at[q(1 - x, 1 - y)], relayed[t], near) for t in range(nt)],
            merged=[remote(t, 2, own[t].at[q(*far)], outs[t].at[2], far) for t in range(nt)],
            theirs=theirs, own=own, relayed=relayed, far=q(*far))

    EARLY_AT = 0.1

    def stages(self):
        return [(self.EARLY_AT, self.early), (self.mid_at, self.mid)]

    def start(self, ins, outs, scr):
        cps = self._copies(ins, outs, scr)
        for t in range(len(ins)):
            cps["swap"][t].start()
            cps["load"][t].start()

    def early(self, ins, outs, scr):
        cps = self._copies(ins, outs, scr)
        for t in range(len(ins)):
            cps["load"][t].wait()
            cps["swap"][t].wait_recv()
            own, theirs = cps["own"][t], cps["theirs"][t]
            for j in range(4):
                own[j] = (own[j].astype(F32) + theirs[j].astype(F32)).astype(BF16)
            for kind in ("relay", "direct", "keep"):
                cps[kind][t].start()

    def mid(self, ins, outs, scr):
        cps = self._copies(ins, outs, scr)
        for t in range(len(ins)):
            cps["relay"][t].wait_recv()
            own, far = cps["own"][t], cps["far"]
            own[far] = (own[far].astype(F32) + cps["relayed"][t][...].astype(F32)).astype(BF16)
            cps["merged"][t].start()

    def finish(self, ins, outs, scr):
        cps = self._copies(ins, outs, scr)
        for t in range(len(ins)):
            cps["swap"][t].wait_send()
            cps["direct"][t].wait()
            cps["relay"][t].wait_send()
            cps["merged"][t].wait()
            cps["keep"][t].wait()


class _Comm:
    def __init__(self, groups):
        self.names = {tag: list(g) for tag, (g, _) in groups.items()}
        self.gathers = {tag: _GatherCarry(list(g.values()), mid_at) for tag, (g, mid_at) in groups.items()}
        self.reduced = {}
        self.last = None

    def gathered(self, tag, outs):
        return dict(zip(self.names[tag], outs))

    def reduce_start(self, grads):
        names = list(grads)
        return _ExchangeCarry(names, [grads[n] for n in names])

    def reduce_done(self, carry, outs):
        self.reduced.update(zip(carry.names, outs))


def _adamw_math(w, g, m, v):
    m = B1 * m + (1.0 - B1) * g
    v = B2 * v + (1.0 - B2) * (g * g)
    m_hat = m / (1.0 - B1 ** STEP)
    v_hat = v / (1.0 - B2 ** STEP)
    delta = -LR * (m_hat / (jnp.sqrt(v_hat) + AEPS) + WD * w)
    return delta, m, v


def _adamw_big(recv, w, m, v, name):
    def body(r_ref, w_ref, m_ref, v_ref, g_ref, d_ref, mo_ref, vo_ref):
        g = r_ref[0].astype(F32)
        for q in range(1, 3):
            g = g + r_ref[q].astype(F32)
        d, mn, vn = _adamw_math(w_ref[...], g, m_ref[...], v_ref[...])
        g_ref[...] = g
        d_ref[...] = d
        mo_ref[...] = mn
        vo_ref[...] = vn

    rows, n = w.shape
    tr = rows // 2
    row = pl.BlockSpec((tr, n), lambda t: (t, 0))
    return pl.pallas_call(
        body, grid=(2,), in_specs=[pl.BlockSpec((3, tr, n), lambda t: (0, t, 0)), row, row, row],
        out_specs=[row] * 4, out_shape=[jax.ShapeDtypeStruct(w.shape, F32)] * 4,
        compiler_params=_cp(("parallel",)), name=name)(recv, w, m, v)


SMALL_NAMES = ("ffn1_norm", "mix_norm", "ffn2_norm", "conv_b", "conv_ln_g", "conv_ln_b", "q_norm", "k_norm")
SROWS = 16
LOSS_ROW = len(SMALL_NAMES)
CWF = 2


def _small_sums(gs, loss_row, gcw, carry=None):
    ns = len(SMALL_NAMES)
    widths = [g.shape[1] for g in gs]

    def body(*refs):
        it = iter(refs)
        take = lambda n: [next(it) for _ in range(n)]
        g_refs, (loss_ref, gcw_ref) = take(ns), take(2)
        cins = take(len(carry.in_arrays)) if carry else []
        tot_ref, ctot_ref = take(2)
        couts = take(len(carry.out_shape)) if carry else []
        send, slots, cslots, send_sems, recv_sems, csend_sems, crecv_sems = take(7)
        cscr = list(it)
        x, y, c = _place()
        me = 4 * x + 2 * y + c
        send[...] = jnp.zeros_like(send)
        for k in range(ns):
            send[k:k + 1, 0:widths[k]] = g_refs[k][...]
        send[LOSS_ROW:LOSS_ROW + 1, 0:128] = loss_ref[...]
        slots[me] = send[...]
        cslots[me] = gcw_ref[...]
        cps = []
        for k in range(1, NDEV):
            peer = (x ^ ((k >> 2) & 1), y ^ ((k >> 1) & 1), c ^ (k & 1))
            cps.append(pltpu.make_async_remote_copy(
                src_ref=send, dst_ref=slots.at[me], send_sem=send_sems.at[k - 1], recv_sem=recv_sems.at[k - 1],
                device_id=peer, device_id_type=MESH))
            cps.append(pltpu.make_async_remote_copy(
                src_ref=gcw_ref, dst_ref=cslots.at[me], send_sem=csend_sems.at[k - 1],
                recv_sem=crecv_sems.at[k - 1], device_id=peer, device_id_type=MESH))
        for cp in cps:
            cp.start()
        stages = [stage for _, stage in carry.stages()] if carry else []
        if carry:
            carry.start(cins, couts, cscr)
        for stage in stages[:1]:
            stage(cins, couts, cscr)
        for cp in cps:
            cp.wait()
        tot = slots[0]
        ctot = cslots[0, me]
        for j in range(1, NDEV):
            tot = tot + slots[j]
            ctot = ctot + cslots[j, me]
        tot_ref[...] = tot
        ctot_ref[...] = ctot
        for stage in stages[1:]:
            stage(cins, couts, cscr)
        if carry:
            carry.finish(cins, couts, cscr)

    args = [*gs, loss_row, gcw]
    out_shape = [jax.ShapeDtypeStruct((SROWS, D), F32), jax.ShapeDtypeStruct((CWF, D), F32)]
    res = pl.pallas_call(
        body, in_specs=[VMEM] * len(args) + (carry.in_specs if carry else []),
        out_specs=[VMEM] * 2 + (carry.out_specs if carry else []),
        out_shape=out_shape + (carry.out_shape if carry else []),
        scratch_shapes=[pltpu.VMEM((SROWS, D), F32), pltpu.VMEM((NDEV, SROWS, D), F32),
                        pltpu.VMEM((NDEV, NDEV, CWF, D), F32)]
                       + [pltpu.SemaphoreType.DMA((NDEV - 1,))] * 4 + (carry.scratch if carry else []),
        name="small_sums")(*args, *(carry.in_arrays if carry else []))
    return res[0], res[1], res[2:]


def _adamw_small(tot, ctot, ws, ms, vs, wcw, mcw, vcw):
    ns = len(SMALL_NAMES)
    widths = [w.shape[1] for w in ws]

    def body(*refs):
        it = iter(refs)
        take = lambda n: [next(it) for _ in range(n)]
        (tot_ref, ctot_ref), w_refs, m_refs, v_refs = take(2), take(ns), take(ns), take(ns)
        wcw_ref, mcw_ref, vcw_ref = take(3)
        outs = [take(4) for _ in range(ns)]
        cw_outs, (loss_out,) = take(4), take(1)

        def step(g, w_ref, m_ref, v_ref, o):
            d, mn, vn = _adamw_math(w_ref[...], g, m_ref[...], v_ref[...])
            o[0][...], o[1][...], o[2][...], o[3][...] = g, d, mn, vn

        for k in range(ns):
            step(tot_ref[k:k + 1, 0:widths[k]], w_refs[k], m_refs[k], v_refs[k], outs[k])
        step(ctot_ref[...], wcw_ref, mcw_ref, vcw_ref, cw_outs)
        loss_out[...] = tot_ref[LOSS_ROW:LOSS_ROW + 1, 0:128]

    args = [tot, ctot, *ws, *ms, *vs, wcw, mcw, vcw]
    out_shape = ([jax.ShapeDtypeStruct((1, n), F32) for n in widths for _ in range(4)]
                 + [jax.ShapeDtypeStruct((CWF, D), F32)] * 4 + [jax.ShapeDtypeStruct((1, 128), F32)])
    res = pl.pallas_call(
        body, in_specs=[VMEM] * len(args), out_specs=[VMEM] * len(out_shape), out_shape=out_shape,
        name="adamw_small")(*args)
    per = [res[4 * k:4 * k + 4] for k in range(ns)]
    return per, res[4 * ns:4 * ns + 4], res[-1]


def _pack_cw(a):
    flat = a.reshape(a.shape[:-2] + (CK * HD,))
    pad = [(0, 0)] * (flat.ndim - 1) + [(0, CWF * D - CK * HD)]
    return jnp.pad(flat, pad).reshape(a.shape[:-2] + (CWF, D))


def _unpack_cw(v):
    return v.reshape(-1)[:CK * HD].reshape(1, CK, HD)


def kernel(x, ffn1_norm, ffn1_w_gate, ffn1_w_up, ffn1_w_down, mix_norm, w_in, q_norm, k_norm, conv_w, conv_b, conv_ln_g, conv_ln_b, w_out, ffn2_norm, ffn2_w_gate, ffn2_w_up, ffn2_w_down, loss_target, m_ffn1_norm, m_ffn1_w_gate, m_ffn1_w_up, m_ffn1_w_down, m_mix_norm, m_w_in, m_q_norm, m_k_norm, m_conv_w, m_conv_b, m_conv_ln_g, m_conv_ln_b, m_w_out, m_ffn2_norm, m_ffn2_w_gate, m_ffn2_w_up, m_ffn2_w_down, v_ffn1_norm, v_ffn1_w_gate, v_ffn1_w_up, v_ffn1_w_down, v_mix_norm, v_w_in, v_q_norm, v_k_norm, v_conv_w, v_conv_b, v_conv_ln_g, v_conv_ln_b, v_w_out, v_ffn2_norm, v_ffn2_w_gate, v_ffn2_w_up, v_ffn2_w_down):
    P = dict(ffn1_norm=ffn1_norm, ffn1_w_gate=ffn1_w_gate, ffn1_w_up=ffn1_w_up, ffn1_w_down=ffn1_w_down,
             mix_norm=mix_norm, w_in=w_in, q_norm=q_norm, k_norm=k_norm, conv_w=conv_w, conv_b=conv_b,
             conv_ln_g=conv_ln_g, conv_ln_b=conv_ln_b, w_out=w_out, ffn2_norm=ffn2_norm,
             ffn2_w_gate=ffn2_w_gate, ffn2_w_up=ffn2_w_up, ffn2_w_down=ffn2_w_down)
    M = dict(ffn1_norm=m_ffn1_norm, ffn1_w_gate=m_ffn1_w_gate, ffn1_w_up=m_ffn1_w_up, ffn1_w_down=m_ffn1_w_down,
             mix_norm=m_mix_norm, w_in=m_w_in, q_norm=m_q_norm, k_norm=m_k_norm, conv_w=m_conv_w, conv_b=m_conv_b,
             conv_ln_g=m_conv_ln_g, conv_ln_b=m_conv_ln_b, w_out=m_w_out, ffn2_norm=m_ffn2_norm,
             ffn2_w_gate=m_ffn2_w_gate, ffn2_w_up=m_ffn2_w_up, ffn2_w_down=m_ffn2_w_down)
    V = dict(ffn1_norm=v_ffn1_norm, ffn1_w_gate=v_ffn1_w_gate, ffn1_w_up=v_ffn1_w_up, ffn1_w_down=v_ffn1_w_down,
             mix_norm=v_mix_norm, w_in=v_w_in, q_norm=v_q_norm, k_norm=v_k_norm, conv_w=v_conv_w, conv_b=v_conv_b,
             conv_ln_g=v_conv_ln_g, conv_ln_b=v_conv_ln_b, w_out=v_w_out, ffn2_norm=v_ffn2_norm,
             ffn2_w_gate=v_ffn2_w_gate, ffn2_w_up=v_ffn2_w_up, ffn2_w_down=v_ffn2_w_down)
    order = ["ffn1_norm", "ffn1_w_gate", "ffn1_w_up", "ffn1_w_down", "mix_norm", "w_in", "q_norm", "k_norm",
             "conv_w", "conv_b", "conv_ln_g", "conv_ln_b", "w_out", "ffn2_norm", "ffn2_w_gate", "ffn2_w_up",
             "ffn2_w_down"]
    B, S, _ = x.shape
    T = B * S

    bigs = [("wg1", "ffn1_w_gate", True), ("wu1", "ffn1_w_up", True), ("wd1", "ffn1_w_down", False),
            ("win", "w_in", True), ("wout", "w_out", False),
            ("wg2", "ffn2_w_gate", True), ("wu2", "ffn2_w_up", True), ("wd2", "ffn2_w_down", False)]
    hm = lambda a, tr: jnp.transpose(a[0]) if tr else a[0]
    cw_pad = jnp.zeros((32, 128), F32).at[0:CK, 0:HD].set(conv_w[0])
    shard = {ln: (hm(P[pn], tr), BF16) for ln, pn, tr in bigs}
    gathered = _run_carry(_GatherCarry([shard["wg1"], shard["wu1"], (cw_pad, F32)]), "gather_first")
    W = {"wg1": gathered[0], "wu1": gathered[1]}
    cwg = gathered[2].reshape(NDEV, 32, 128)[:, 0:CK, 0:HD]
    W["conv_w"] = jnp.transpose(cwg, (1, 0, 2)).reshape(CK, DC)
    norms = {n: P[n] for n in SMALL_NAMES}
    comm = _Comm({"down_in": ({n: shard[n] for n in ("wd1", "win")}, 0.5),
                  "ffn2": ({n: shard[n] for n in ("wg2", "wu2", "wd2", "wout")}, 0.5)})

    loss_part, gx, _, small = _local_step(x.reshape(T, D), loss_target.reshape(T, D), norms, W, B, S, comm)

    G, Dl, Mn, Vn = {}, {}, {}, {}
    dcw = small["conv_w"].reshape(CK, NDEV, HD).transpose(1, 0, 2)
    loss_row = jnp.zeros((1, 128), F32).at[0, 0].set(loss_part)
    tot, ctot, got = _small_sums([small[n] for n in SMALL_NAMES], loss_row, _pack_cw(dcw), carry=comm.last)
    comm.reduce_done(comm.last, got)
    per, cw_outs, loss_out = _adamw_small(
        tot, ctot, [P[n] for n in SMALL_NAMES], [M[n] for n in SMALL_NAMES], [V[n] for n in SMALL_NAMES],
        _pack_cw(P["conv_w"][0]), _pack_cw(M["conv_w"][0]), _pack_cw(V["conv_w"][0]))
    loss = loss_out[0, 0]
    for n, outs in zip(SMALL_NAMES, per):
        G[n], Dl[n], Mn[n], Vn[n] = outs
    G["conv_w"], Dl["conv_w"], Mn["conv_w"], Vn["conv_w"] = [_unpack_cw(o) for o in cw_outs]

    for ln, pn, tr in bigs:
        outs = _adamw_big(comm.reduced[ln], hm(P[pn], tr), hm(M[pn], tr), hm(V[pn], tr), "adamw_" + ln)
        G[pn], Dl[pn], Mn[pn], Vn[pn] = [(jnp.transpose(o) if tr else o)[None] for o in outs]

    return (loss, gx.reshape(B, S, D), *[G[n] for n in order], *[Dl[n] for n in order],
            *[Mn[n] for n in order], *[Vn[n] for n in order])
```

```python
import functools

import jax
import jax.numpy as jnp
from jax import lax
from jax.experimental import pallas as pl
from jax.experimental.pallas import tpu as pltpu

F32 = jnp.float32
BF16 = jnp.bfloat16

D = 1024
FF = 2816
HD = 64
DA = 512
DC = 512
DIN = 2560
CK = 31
BLK = 128
DILS = (1, 4, 16)
EPS = 1e-6
NDEV = 8
MESH = pl.DeviceIdType.MESH

LR, B1, B2, AEPS, WD, STEP = 0.001, 0.9, 0.999, 1e-08, 0.01, 10

NT = (((1,), (1,)), ((), ()))
TN = (((0,), (0,)), ((), ()))

VMEM_LIMIT = 60 * 1024 * 1024


def _cp(sem=None):
    return pltpu.CompilerParams(dimension_semantics=sem, vmem_limit_bytes=VMEM_LIMIT)


def _sigmoid(x):
    return 0.5 * (jnp.tanh(0.5 * x) + 1.0)


def _pallas(body, args, *, grid, in_specs, out_specs, out_shape, scratch_shapes, sem, name, carry=None):
    if carry is None:
        outs = pl.pallas_call(body, grid=grid, in_specs=in_specs, out_specs=out_specs, out_shape=out_shape,
                              scratch_shapes=scratch_shapes, compiler_params=_cp(sem), name=name)(*args)
        return outs, None
    n_in, n_out, n_scr = len(in_specs), len(out_shape), len(scratch_shapes)
    c_in, c_out = len(carry.in_arrays), len(carry.out_shape)

    def wrapped(*refs):
        ins, refs = refs[:n_in], refs[n_in:]
        cins, refs = refs[:c_in], refs[c_in:]
        outs, refs = refs[:n_out], refs[n_out:]
        couts, refs = refs[:c_out], refs[c_out:]
        scr, cscr = refs[:n_scr], refs[n_scr:]
        ids = [pl.program_id(a) for a in range(len(grid))]
        step = ids[0]
        for i, n in zip(ids[1:], grid[1:]):
            step = step * n + i
        steps = functools.reduce(lambda a, b: a * b, grid)

        @pl.when(step == 0)
        def _():
            carry.start(cins, couts, cscr)

        body(*ins, *outs, *scr)

        for frac, stage in carry.stages():
            @pl.when(step == int(steps * frac))
            def _(stage=stage):
                stage(cins, couts, cscr)

        @pl.when(step == steps - 1)
        def _():
            carry.finish(cins, couts, cscr)

    outs = pl.pallas_call(
        wrapped, grid=grid, in_specs=list(in_specs) + carry.in_specs, out_specs=list(out_specs) + carry.out_specs,
        out_shape=list(out_shape) + carry.out_shape, scratch_shapes=list(scratch_shapes) + carry.scratch,
        compiler_params=_cp(("arbitrary",) * len(grid)), name=name)(*args, *carry.in_arrays)
    return outs[:n_out], outs[n_out:]


FC = 256


def _resident(shape):
    return pl.BlockSpec(shape, lambda *_: (0,) * len(shape), pipeline_mode=pl.Buffered(1))


def _mix_out_ffn_loss(h1, attn, conv, wout, gain, wg, wu, wd, target, name):
    T = h1.shape[0]
    tm = 512
    nt = T // tm

    def body(h1_ref, at_ref, cv_ref, wo_ref, gain_ref, wg_ref, wu_ref, wd_ref, t_ref,
             h2_ref, n_ref, g_ref, u_ref, dout_ref, dyb_ref, sq_ref, a_hbm, a_scr, a_sem):
        t = pl.program_id(0)
        a_out = lambda i: pltpu.make_async_copy(a_scr, a_hbm.at[pl.ds(pl.multiple_of(i * tm, tm), tm), :], a_sem)

        @pl.when(t > 0)
        def _():
            a_out(t - 1).wait()

        xv = (h1_ref[...]
              + jnp.dot(at_ref[...], wo_ref[0:DA, :], preferred_element_type=F32)
              + jnp.dot(cv_ref[...], wo_ref[DA:D, :], preferred_element_type=F32))
        h2_ref[...] = xv
        r = lax.rsqrt(jnp.mean(xv * xv, axis=-1, keepdims=True) + EPS)
        n_ref[...] = (xv * r * gain_ref[...]).astype(BF16)
        for c in range(FF // FC):
            cols = slice(c * FC, (c + 1) * FC)
            nb = n_ref[...]
            g = lax.dot_general(nb, wg_ref[cols, :], NT, preferred_element_type=F32)
            u = lax.dot_general(nb, wu_ref[cols, :], NT, preferred_element_type=F32)
            g_ref[:, cols] = g.astype(BF16)
            u_ref[:, cols] = u.astype(BF16)
            a_scr[:, cols] = (g * _sigmoid(g) * u).astype(BF16)
        a_out(t).start()
        e = h2_ref[...] + 0.5 * jnp.dot(a_scr[...], wd_ref[...], preferred_element_type=F32) - t_ref[...]
        dout = e * (1.0 / D)
        dout_ref[...] = dout
        dyb_ref[...] = (0.5 * dout).astype(BF16)
        sq_ref[...] = jnp.sum(e * e, axis=0, keepdims=True)[None]

        @pl.when(t == nt - 1)
        def _():
            a_out(t).wait()

    row = pl.BlockSpec((tm, D), lambda t: (t, 0))
    half = pl.BlockSpec((tm, DA), lambda t: (t, 0))
    wide = pl.BlockSpec((tm, FF), lambda t: (t, 0))
    outs, _ = _pallas(
        body, (h1, attn, conv, wout, gain, wg, wu, wd, target), grid=(nt,),
        in_specs=[row, half, half, _resident((D, D)), _resident((1, D)), _resident((FF, D)), _resident((FF, D)),
                  _resident((FF, D)), row],
        out_specs=[row, row, wide, wide, row, row, pl.BlockSpec((1, 1, D), lambda t: (t, 0, 0)), HBM],
        out_shape=[jax.ShapeDtypeStruct((T, D), F32), jax.ShapeDtypeStruct((T, D), BF16)]
                  + [jax.ShapeDtypeStruct((T, FF), BF16)] * 2
                  + [jax.ShapeDtypeStruct((T, D), F32), jax.ShapeDtypeStruct((T, D), BF16),
                     jax.ShapeDtypeStruct((nt, 1, D), F32), jax.ShapeDtypeStruct((T, FF), BF16)],
        scratch_shapes=[pltpu.VMEM((tm, FF), BF16), pltpu.SemaphoreType.DMA(())],
        sem=("arbitrary",), name=name)
    return outs


def _ffn_gate_up(x, gain, wg, wu, name, carry=None):
    T = x.shape[0]
    tm = 512

    def body(x_ref, gain_ref, wg_ref, wu_ref, n_ref, g_ref, u_ref, a_ref):
        xv = x_ref[...]
        r = lax.rsqrt(jnp.mean(xv * xv, axis=-1, keepdims=True) + EPS)
        n_ref[...] = (xv * r * gain_ref[...]).astype(BF16)
        for c in range(FF // FC):
            cols = slice(c * FC, (c + 1) * FC)
            nb = n_ref[...]
            g = lax.dot_general(nb, wg_ref[cols, :], NT, preferred_element_type=F32)
            u = lax.dot_general(nb, wu_ref[cols, :], NT, preferred_element_type=F32)
            g_ref[:, cols] = g.astype(BF16)
            u_ref[:, cols] = u.astype(BF16)
            a_ref[:, cols] = (g * _sigmoid(g) * u).astype(BF16)

    row = pl.BlockSpec((tm, D), lambda t: (t, 0))
    wide = pl.BlockSpec((tm, FF), lambda t: (t, 0))
    return _pallas(
        body, (x, gain, wg, wu), grid=(T // tm,),
        in_specs=[row, _resident((1, D)), _resident((FF, D)), _resident((FF, D))],
        out_specs=[row, wide, wide, wide],
        out_shape=[jax.ShapeDtypeStruct((T, D), BF16)] + [jax.ShapeDtypeStruct((T, FF), BF16)] * 3,
        scratch_shapes=[], sem=("parallel",), name=name, carry=carry)


def _ffn_down_mix_in(x, a, wd, gain, win, name):
    T = x.shape[0]
    tm = 512

    def body(x_ref, a_ref, wd_ref, gain_ref, win_ref, h_ref, u_ref, n_ref):
        hv = x_ref[...] + 0.5 * jnp.dot(a_ref[...], wd_ref[...], preferred_element_type=F32)
        h_ref[...] = hv
        r = lax.rsqrt(jnp.mean(hv * hv, axis=-1, keepdims=True) + EPS)
        n_ref[...] = (hv * r * gain_ref[...]).astype(BF16)
        u_ref[...] = lax.dot_general(n_ref[...], win_ref[...], NT, preferred_element_type=F32).astype(BF16)

    row = pl.BlockSpec((tm, D), lambda t: (t, 0))
    wide = pl.BlockSpec((tm, FF), lambda t: (t, 0))
    outs, _ = _pallas(
        body, (x, a, wd, gain, win), grid=(T // tm,),
        in_specs=[row, wide, _resident((FF, D)), _resident((1, D)), _resident((DIN, D))],
        out_specs=[row, pl.BlockSpec((tm, DIN), lambda t: (t, 0)), row],
        out_shape=[jax.ShapeDtypeStruct((T, D), F32), jax.ShapeDtypeStruct((T, DIN), BF16),
                   jax.ShapeDtypeStruct((T, D), BF16)],
        scratch_shapes=[], sem=("parallel",), name=name)
    return outs


def _ffn_bwd_act(dyb, g, u, x, dout, gain, wg, wu, wd, name, carry=None):
    T = x.shape[0]
    tm = 256
    nt = T // tm

    def body(dy_ref, g_ref, u_ref, x_ref, dout_ref, gain_ref, wg_ref, wu_ref, wd_ref,
             dg_ref, du_ref, dx_ref, dgn_ref):
        for c in range(FF // FC):
            cols = slice(c * FC, (c + 1) * FC)
            da = lax.dot_general(dy_ref[...], wd_ref[cols, :], NT, preferred_element_type=F32)
            gv = g_ref[:, cols].astype(F32)
            uv = u_ref[:, cols].astype(F32)
            sg = _sigmoid(gv)
            dg_ref[:, cols] = (da * uv * (sg * (1.0 + gv * (1.0 - sg)))).astype(BF16)
            du_ref[:, cols] = (da * (gv * sg)).astype(BF16)
        dn = (jnp.dot(dg_ref[...], wg_ref[...], preferred_element_type=F32)
              + jnp.dot(du_ref[...], wu_ref[...], preferred_element_type=F32))
        dx, dgain = _rms_bwd_rows(dn, x_ref[...], gain_ref[...])
        dx_ref[...] = dout_ref[...] + dx
        dgn_ref[...] = dgain[None]

    row = pl.BlockSpec((tm, D), lambda t: (t, 0))
    wide = pl.BlockSpec((tm, FF), lambda t: (t, 0))
    return _pallas(
        body, (dyb, g, u, x, dout, gain, wg, wu, wd), grid=(nt,),
        in_specs=[row, wide, wide, row, row, _resident((1, D)), _resident((FF, D)), _resident((FF, D)),
                  _resident((FF, D))],
        out_specs=[wide, wide, row, pl.BlockSpec((1, 1, D), lambda t: (t, 0, 0))],
        out_shape=[jax.ShapeDtypeStruct((T, FF), BF16)] * 2
                  + [jax.ShapeDtypeStruct((T, D), F32), jax.ShapeDtypeStruct((nt, 1, D), F32)],
        scratch_shapes=[], sem=("parallel",), name=name, carry=carry)


def _ffn_bwd_w(lhs, rhs, name, carry=None):
    T = rhs.shape[0]
    tf = 256

    def body(l_ref, r_ref, dw_ref):
        dw_ref[...] = lax.dot_general(l_ref[...], r_ref[...], TN, preferred_element_type=F32).astype(BF16)

    (dw,), got = _pallas(
        body, (lhs, rhs), grid=(FF // tf,),
        in_specs=[pl.BlockSpec((T, tf), lambda f: (0, f)), _resident((T, D))],
        out_specs=[pl.BlockSpec((tf, D), lambda f: (f, 0))], out_shape=[jax.ShapeDtypeStruct((FF, D), BF16)],
        scratch_shapes=[], sem=("parallel",), name=name, carry=carry)
    return dw, got


def _rms_bwd_rows(dn, xv, gain):
    r = lax.rsqrt(jnp.mean(xv * xv, axis=-1, keepdims=True) + EPS)
    xhat = xv * r
    dxhat = dn * gain
    dx = r * (dxhat - xhat * jnp.mean(dxhat * xhat, axis=-1, keepdims=True))
    return dx, jnp.sum(dn * xhat, axis=0, keepdims=True)


def _mix_out_bwd(dh, attn, conv, wout):
    T = dh.shape[0]
    tm = 512
    nt = T // tm

    def body(dh_ref, a_ref, c_ref, w_ref, da_ref, dc_ref, dw_ref, acc_scr):
        t = pl.program_id(0)

        @pl.when(t == 0)
        def _():
            acc_scr[...] = jnp.zeros_like(acc_scr)

        dhb = dh_ref[...].astype(BF16)
        dmix = lax.dot_general(dhb, w_ref[...], NT, preferred_element_type=F32)
        da_ref[...] = dmix[:, 0:DA].astype(BF16)
        dc_ref[...] = dmix[:, DA:D].astype(BF16)
        acc_scr[0:DA, :] += lax.dot_general(a_ref[...], dhb, TN, preferred_element_type=F32)
        acc_scr[DA:D, :] += lax.dot_general(c_ref[...], dhb, TN, preferred_element_type=F32)

        @pl.when(t == nt - 1)
        def _():
            dw_ref[...] = acc_scr[...].astype(BF16)

    row = pl.BlockSpec((tm, D), lambda t: (t, 0))
    half = pl.BlockSpec((tm, DA), lambda t: (t, 0))
    full = pl.BlockSpec((D, D), lambda t: (0, 0))
    return pl.pallas_call(
        body, grid=(nt,), in_specs=[row, half, half, full], out_specs=[half, half, full],
        out_shape=[jax.ShapeDtypeStruct((T, DA), BF16)] * 2 + [jax.ShapeDtypeStruct((D, D), BF16)],
        scratch_shapes=[pltpu.VMEM((D, D), F32)],
        compiler_params=_cp(("arbitrary",)), name="mix_out_bwd")(dh, attn, conv, wout)


def _mix_in_bwd(dparts, win, nb, h, dh, gain):
    T = h.shape[0]
    tm = 512
    nt = T // tm

    def body(d0, d1, d2, d3, d4, w_ref, n_ref, h_ref, dh_ref, gain_ref,
             dw_ref, dx_ref, dyb_ref, dg_ref, acc_scr):
        t = pl.program_id(0)

        @pl.when(t == 0)
        def _():
            acc_scr[...] = jnp.zeros_like(acc_scr)

        n = n_ref[...]
        dn = jnp.zeros((tm, D), F32)
        for i, d_ref in enumerate((d0, d1, d2, d3, d4)):
            dv = d_ref[...]
            dn = dn + jnp.dot(dv, w_ref[i * DA:(i + 1) * DA, :], preferred_element_type=F32)
            acc_scr[i * DA:(i + 1) * DA, :] += lax.dot_general(dv, n, TN, preferred_element_type=F32)
        dx, dgain = _rms_bwd_rows(dn, h_ref[...], gain_ref[...])
        tot = dh_ref[...] + dx
        dx_ref[...] = tot
        dyb_ref[...] = (0.5 * tot).astype(BF16)
        dg_ref[...] = dgain[None]

        @pl.when(t == nt - 1)
        def _():
            dw_ref[...] = acc_scr[...].astype(BF16)

    row = pl.BlockSpec((tm, D), lambda t: (t, 0))
    half = pl.BlockSpec((tm, DA), lambda t: (t, 0))
    full = pl.BlockSpec((DIN, D), lambda t: (0, 0))
    return pl.pallas_call(
        body, grid=(nt,),
        in_specs=[half] * 5 + [full, row, row, row, pl.BlockSpec((1, D), lambda t: (0, 0))],
        out_specs=[full, row, row, pl.BlockSpec((1, 1, D), lambda t: (t, 0, 0))],
        out_shape=[jax.ShapeDtypeStruct((DIN, D), BF16), jax.ShapeDtypeStruct((T, D), F32),
                   jax.ShapeDtypeStruct((T, D), BF16), jax.ShapeDtypeStruct((nt, 1, D), F32)],
        scratch_shapes=[pltpu.VMEM((DIN, D), F32)],
        compiler_params=_cp(("arbitrary",)), name="mix_in_bwd")(*dparts, win, nb, h, dh, gain)


def _head_masks():
    lane = lax.broadcasted_iota(jnp.int32, (1, 2 * HD), 1)
    m0 = lane < HD
    return m0, jnp.logical_not(m0)


def _stack_heads(v, m0, m1):
    z = jnp.zeros_like(v)
    return jnp.concatenate([jnp.where(m0, v, z), jnp.where(m1, v, z)], axis=0)


def _unstack_heads(v2, m0):
    return jnp.where(m0, v2[0:BLK], v2[BLK:2 * BLK])


def _head_sums(xv):
    ri = lax.broadcasted_iota(jnp.int32, (2 * HD, 2 * HD), 0)
    ci = lax.broadcasted_iota(jnp.int32, (2 * HD, 2 * HD), 1)
    ones = jnp.where((ri < HD) == (ci < HD), 1.0, 0.0).astype(BF16)
    hi = xv.astype(BF16)
    lo = (xv - hi.astype(F32)).astype(BF16)
    return (jnp.dot(hi, ones, preferred_element_type=F32) + jnp.dot(lo, ones, preferred_element_type=F32))


def _head_rms(xv):
    return lax.rsqrt(_head_sums(xv * xv) * (1.0 / HD) + EPS)


def _band_mask(first):
    qi = lax.broadcasted_iota(jnp.int32, (BLK, 2 * BLK), 0)
    ci = lax.broadcasted_iota(jnp.int32, (BLK, 2 * BLK), 1)
    band = (ci >= qi) & (ci <= qi + BLK)
    return band & ((ci >= BLK) | jnp.logical_not(first))


def _block_rows(j, d, seg):
    r, n = j // seg, j % seg
    start = r + (d * BLK) * n
    first = n == 0
    prev = jnp.where(first, start, start - d * BLK)
    return pl.ds(start, BLK, stride=d), pl.ds(prev, BLK, stride=d), first


def _block_keys(refs, cur, prev, first, single):
    if single:
        qi = lax.broadcasted_iota(jnp.int32, (BLK, BLK), 0)
        ci = lax.broadcasted_iota(jnp.int32, (BLK, BLK), 1)
        return [r[cur, :].astype(BF16) for r in refs], ci <= qi
    return ([jnp.concatenate([r[prev, :], r[cur, :]], axis=0).astype(BF16) for r in refs], _band_mask(first))


def _attn_fwd(u, qg2, kg2, B, S, carry=None):
    T = B * S
    NB = S // BLK
    scale = HD ** -0.5

    def body(q_ref, k_ref, v_ref, qg_ref, kg_ref, o_ref, lse_ref, qn, kn, vn, os_, ls_):
        m0, m1 = _head_masks()
        qv = q_ref[...].astype(F32)
        qn[...] = qv * _head_rms(qv) * (qg_ref[...] * scale)
        kv = k_ref[...].astype(F32)
        kn[...] = kv * _head_rms(kv) * kg_ref[...]
        vn[...] = v_ref[...].astype(F32)

        for i, d in enumerate(DILS):
            seg = NB // d

            def blk(j, c, i=i, d=d, seg=seg):
                cur, prev, first = _block_rows(j, d, seg)
                q2 = _stack_heads(qn[cur, :].astype(BF16), m0, m1)
                (kk, vv), mask = _block_keys((kn, vn), cur, prev, first, False)
                s = lax.dot_general(q2, kk, NT, preferred_element_type=F32)
                s = jnp.where(jnp.concatenate([mask, mask], axis=0), s, -1e30)
                mx = jnp.max(s, axis=-1, keepdims=True)
                p = jnp.exp(s - mx)
                l = jnp.sum(p, axis=-1, keepdims=True)
                o2 = jnp.dot((p * (1.0 / l)).astype(BF16), vv, preferred_element_type=F32)
                os_[i, cur, :] = _unstack_heads(o2, m0)
                ls_[i, cur, :] = _unstack_heads(mx + jnp.log(l), m0)
                return c

            lax.fori_loop(0, NB, blk, 0, unroll=8)

        def comb(c, carry):
            rows = pl.ds(pl.multiple_of(c * 256, 256), 256)
            l0, l1, l2 = ls_[0, rows, :], ls_[1, rows, :], ls_[2, rows, :]
            mx = jnp.maximum(jnp.maximum(l0, l1), l2)
            e0, e1, e2 = jnp.exp(l0 - mx), jnp.exp(l1 - mx), jnp.exp(l2 - mx)
            tot = e0 + e1 + e2
            inv = 1.0 / tot
            o = (e0 * os_[0, rows, :] + e1 * os_[1, rows, :] + e2 * os_[2, rows, :]) * inv
            o_ref[rows, :] = o.astype(BF16)
            lse_ref[rows, :] = mx + jnp.log(tot)
            return carry

        lax.fori_loop(0, S // 256, comb, 0)

    pair = 2 * HD
    blk_spec = lambda off: pl.BlockSpec((S, pair), lambda b, p, off=off: (b, off + p))
    gspec = pl.BlockSpec((1, pair), lambda b, p: (0, 0))
    return _pallas(
        body, (u, u, u, qg2, kg2), grid=(B, DA // pair),
        in_specs=[blk_spec(0), blk_spec(DA // pair), blk_spec(2 * DA // pair), gspec, gspec],
        out_specs=[blk_spec(0), blk_spec(0)],
        out_shape=[jax.ShapeDtypeStruct((T, DA), BF16), jax.ShapeDtypeStruct((T, DA), F32)],
        scratch_shapes=[pltpu.VMEM((S, pair), F32)] * 3 + [pltpu.VMEM((3, S, pair), F32)] * 2,
        sem=("parallel", "parallel"), name="attn_fwd", carry=carry)


def _attn_bwd(u, attn, dattn, lse, qg2, kg2, B, S, carry=None):
    T = B * S
    NB = S // BLK
    scale = HD ** -0.5
    pair = 2 * HD

    def body(q_ref, k_ref, v_ref, o_ref, do_ref, lse_ref, qg_ref, kg_ref,
             dq_ref, dk_ref, dv_ref, dgn_ref,
             qn, kn, vn, don, ldl, accq, acck, accv, rq, rk):
        m0, m1 = _head_masks()
        lane = lax.broadcasted_iota(jnp.int32, (1, pair), 1)
        qv = q_ref[...].astype(F32)
        rq[...] = _head_rms(qv)
        qn[...] = qv * rq[...] * (qg_ref[...] * scale)
        kv = k_ref[...].astype(F32)
        rk[...] = _head_rms(kv)
        kn[...] = kv * rk[...] * kg_ref[...]
        vn[...] = v_ref[...].astype(F32)
        dov = do_ref[...].astype(F32)
        don[...] = dov
        ldl[...] = jnp.where((lane % HD) < HD // 2, lse_ref[...], _head_sums(dov * o_ref[...].astype(F32)))

        for i, d in enumerate(DILS):
            seg = NB // d

            def blk(j, c, i=i, d=d, seg=seg):
                cur, prev, first = _block_rows(j, d, seg)
                q2 = _stack_heads(qn[cur, :].astype(BF16), m0, m1)
                do2 = _stack_heads(don[cur, :].astype(BF16), m0, m1)
                (kk, vv), mask = _block_keys((kn, vn), cur, prev, first, seg == 1)
                ldv = ldl[cur, :]
                lse2 = jnp.concatenate([ldv[:, 0:1], ldv[:, HD:HD + 1]], axis=0)
                dl2 = jnp.concatenate([ldv[:, HD // 2:HD // 2 + 1], ldv[:, HD + HD // 2:HD + HD // 2 + 1]], axis=0)
                s = lax.dot_general(q2, kk, NT, preferred_element_type=F32)
                p = jnp.where(jnp.concatenate([mask, mask], axis=0), jnp.exp(s - lse2), 0.0)
                dp = lax.dot_general(do2, vv, NT, preferred_element_type=F32)
                ds = (p * (dp - dl2)).astype(BF16)
                dq_acc = _unstack_heads(jnp.dot(ds, kk, preferred_element_type=F32), m0)
                dk_acc = lax.dot_general(ds, q2, TN, preferred_element_type=F32)
                dv_acc = lax.dot_general(p.astype(BF16), do2, TN, preferred_element_type=F32)
                if i == 0:
                    accq[cur, :] = dq_acc
                    acck[cur, :] = dk_acc[BLK:2 * BLK]
                    accv[cur, :] = dv_acc[BLK:2 * BLK]
                    acck[prev, :] += dk_acc[0:BLK]
                    accv[prev, :] += dv_acc[0:BLK]
                elif seg == 1:
                    accq[cur, :] += dq_acc
                    acck[cur, :] += dk_acc
                    accv[cur, :] += dv_acc
                else:
                    accq[cur, :] += dq_acc
                    acck[prev, :] += dk_acc[0:BLK]
                    acck[cur, :] += dk_acc[BLK:2 * BLK]
                    accv[prev, :] += dv_acc[0:BLK]
                    accv[cur, :] += dv_acc[BLK:2 * BLK]
                return c

            lax.fori_loop(0, NB, blk, 0, unroll=8)

        def norm_bwd(x_ref, r_ref, dn, gain):
            r = r_ref[...]
            xhat = x_ref[...].astype(F32) * r
            dxhat = dn * gain
            dx = r * (dxhat - xhat * (_head_sums(dxhat * xhat) * (1.0 / HD)))
            return dx, jnp.sum(dn * xhat, axis=0, keepdims=True)

        dq, dgq = norm_bwd(q_ref, rq, accq[...], qg_ref[...] * scale)
        dk, dgk = norm_bwd(k_ref, rk, acck[...], kg_ref[...])
        dq_ref[...] = dq.astype(BF16)
        dk_ref[...] = dk.astype(BF16)
        dv_ref[...] = accv[...].astype(BF16)
        dgn_ref[...] = jnp.concatenate([dgq * scale, dgk, jnp.zeros((6, pair), F32)], axis=0)[None]

    blk_spec = lambda off: pl.BlockSpec((S, pair), lambda b, p, off=off: (b, off + p))
    gspec = pl.BlockSpec((1, pair), lambda b, p: (0, 0))
    np_ = DA // pair
    return _pallas(
        body, (u, u, u, attn, dattn, lse, qg2, kg2), grid=(B, np_),
        in_specs=[blk_spec(0), blk_spec(np_), blk_spec(2 * np_), blk_spec(0), blk_spec(0), blk_spec(0),
                  gspec, gspec],
        out_specs=[blk_spec(0), blk_spec(0), blk_spec(0),
                   pl.BlockSpec((1, 8, pair), lambda b, p: (b * np_ + p, 0, 0))],
        out_shape=[jax.ShapeDtypeStruct((T, DA), BF16)] * 3 + [jax.ShapeDtypeStruct((B * np_, 8, pair), F32)],
        scratch_shapes=[pltpu.VMEM((S, pair), F32)] * 10,
        sem=("parallel", "parallel"), name="attn_bwd", carry=carry)


CT = 32
CPAD = 32


def _shifted(win, offsets):
    rolled, out = {}, {}
    n = win.shape[0]
    for o in offsets:
        sub = o % 8
        if sub not in rolled:
            rolled[sub] = win if sub == 0 else pltpu.roll(win, n - sub, 0)
        out[o] = rolled[sub][o - sub:o - sub + CT, :]
    return out


def _ln_fwd(y, g, b):
    mu = jnp.mean(y, axis=-1, keepdims=True)
    yc = y - mu
    rstd = lax.rsqrt(jnp.mean(yc * yc, axis=-1, keepdims=True) + EPS)
    xhat = yc * rstd
    return xhat, rstd, xhat * g + b


def _fill_glu(ca_ref, cg_ref, glu, S):
    glu[pl.ds(0, CPAD), :] = jnp.zeros((CPAD, DC), F32)

    def fill(i, c):
        rows = pl.ds(pl.multiple_of(i * 256, 256), 256)
        a = ca_ref[rows, :].astype(F32)
        gt = cg_ref[rows, :].astype(F32)
        glu[pl.ds(pl.multiple_of(CPAD + i * 256, CT), 256), :] = a * _sigmoid(gt)
        return c

    lax.fori_loop(0, S // 256, fill, 0)


def _conv_fwd(u, cw, cb, lg, lb, B, S):
    T = B * S

    def body(ca_ref, cg_ref, w_ref, b_ref, lg_ref, lb_ref, o_ref, y_ref, glu):
        _fill_glu(ca_ref, cg_ref, glu, S)

        def step(i, c):
            t0 = pl.multiple_of(i * CT, CT)
            win = glu[pl.ds(t0, 2 * CT), :]
            acc = jnp.zeros((CT, DC), F32) + b_ref[...]
            taps = _shifted(win, [k + 2 for k in range(CK)])
            for k in range(CK):
                acc = acc + taps[k + 2] * w_ref[k:k + 1, :]
            y_ref[pl.ds(t0, CT), :] = acc
            _, _, z = _ln_fwd(acc, lg_ref[...], lb_ref[...])
            o_ref[pl.ds(t0, CT), :] = (z * _sigmoid(z)).astype(BF16)
            return c

        lax.fori_loop(0, S // CT, step, 0, unroll=4)

    vec = pl.BlockSpec((1, DC), lambda b: (0, 0))
    return pl.pallas_call(
        body, grid=(B,),
        in_specs=[pl.BlockSpec((S, DC), lambda b: (b, 3)), pl.BlockSpec((S, DC), lambda b: (b, 4)),
                  pl.BlockSpec((CT, DC), lambda b: (0, 0)), vec, vec, vec],
        out_specs=[pl.BlockSpec((S, DC), lambda b: (b, 0))] * 2,
        out_shape=[jax.ShapeDtypeStruct((T, DC), BF16), jax.ShapeDtypeStruct((T, DC), F32)],
        scratch_shapes=[pltpu.VMEM((CPAD + S, DC), F32)],
        compiler_params=_cp(("parallel",)), name="conv_fwd")(u, u, cw, cb, lg, lb)


def _conv_bwd(u, y, dconv, cw, lg, lb, B, S):
    T = B * S

    def body(ca_ref, cg_ref, y_ref, dc_ref, w_ref, lg_ref, lb_ref,
             dca_ref, dcg_ref, dw_ref, ds_ref, glu, dyp, dwacc):
        _fill_glu(ca_ref, cg_ref, glu, S)
        dyp[pl.ds(S, CPAD), :] = jnp.zeros((CPAD, DC), F32)
        lgv, lbv = lg_ref[...], lb_ref[...]

        def sum8(v):
            return functools.reduce(jnp.add, [v[r:r + 8] for r in range(0, v.shape[0], 8)])

        P1 = 4 * CT

        def p1(i, carry):
            sb, sg, sl = carry
            t0 = pl.multiple_of(i * P1, P1)
            xhat, rstd, z = _ln_fwd(y_ref[pl.ds(t0, P1), :], lgv, lbv)
            sz = _sigmoid(z)
            dz = dc_ref[pl.ds(t0, P1), :].astype(F32) * (sz * (1.0 + z * (1.0 - sz)))
            dxhat = dz * lgv
            dy = rstd * (dxhat - jnp.mean(dxhat, axis=-1, keepdims=True)
                         - xhat * jnp.mean(dxhat * xhat, axis=-1, keepdims=True))
            dyp[pl.ds(t0, P1), :] = dy
            return sb + sum8(dy), sg + sum8(dz * xhat), sl + sum8(dz)

        z8 = jnp.zeros((8, DC), F32)
        sb, sg, sl = lax.fori_loop(0, S // P1, p1, (z8, z8, z8))
        rs = lambda v: jnp.sum(v, axis=0, keepdims=True)
        ds_ref[...] = jnp.concatenate([rs(sb), rs(sg), rs(sl), jnp.zeros((5, DC), F32)], axis=0)[None]

        def p2(i, c):
            t0 = pl.multiple_of(i * CT, CT)
            win = dyp[pl.ds(t0, 2 * CT), :]
            acc = jnp.zeros((CT, DC), F32)
            taps = _shifted(win, [30 - k for k in range(CK)])
            for k in range(CK):
                acc = acc + taps[30 - k] * w_ref[k:k + 1, :]
            a = ca_ref[pl.ds(t0, CT), :].astype(F32)
            sgt = _sigmoid(cg_ref[pl.ds(t0, CT), :].astype(F32))
            dca_ref[pl.ds(t0, CT), :] = (acc * sgt).astype(BF16)
            dcg_ref[pl.ds(t0, CT), :] = (acc * a * sgt * (1.0 - sgt)).astype(BF16)
            return c

        lax.fori_loop(0, S // CT, p2, 0)

        dwacc[...] = jnp.zeros_like(dwacc)

        def p3(i, c):
            t0 = pl.multiple_of(i * CT, CT)
            win = glu[pl.ds(t0, 2 * CT), :]
            dy = dyp[pl.ds(t0, CT), :]
            for k in range(CK):
                dwacc[k] += sum8(dy * win[k + 2:k + 2 + CT, :])
            return c

        lax.fori_loop(0, S // CT, p3, 0)
        dw_ref[...] = jnp.sum(dwacc[...], axis=1)[None]

    vec = pl.BlockSpec((1, DC), lambda b: (0, 0))
    seq = pl.BlockSpec((S, DC), lambda b: (b, 0))
    return pl.pallas_call(
        body, grid=(B,),
        in_specs=[pl.BlockSpec((S, DC), lambda b: (b, 3)), pl.BlockSpec((S, DC), lambda b: (b, 4)),
                  seq, seq, pl.BlockSpec((CT, DC), lambda b: (0, 0)), vec, vec],
        out_specs=[seq, seq, pl.BlockSpec((1, CT, DC), lambda b: (b, 0, 0)),
                   pl.BlockSpec((1, 8, DC), lambda b: (b, 0, 0))],
        out_shape=[jax.ShapeDtypeStruct((T, DC), BF16)] * 2
                  + [jax.ShapeDtypeStruct((B, CT, DC), F32), jax.ShapeDtypeStruct((B, 8, DC), F32)],
        scratch_shapes=[pltpu.VMEM((CPAD + S, DC), F32), pltpu.VMEM((S + CPAD, DC), F32),
                        pltpu.VMEM((CT, 8, DC), F32)],
        compiler_params=_cp(("parallel",)), name="conv_bwd")(u, u, y, dconv, cw, lg, lb)


def _local_step(x, target, norms, W, B, S, comm=None):
    qg2 = jnp.concatenate([norms["q_norm"], norms["q_norm"]], axis=1)
    kg2 = jnp.concatenate([norms["k_norm"], norms["k_norm"]], axis=1)
    cw = jnp.concatenate([W["conv_w"], jnp.zeros((1, DC), F32)], axis=0)

    W = dict(W)
    (n1, g1, u1, act1), got = _ffn_gate_up(x, norms["ffn1_norm"], W["wg1"], W["wu1"], "ffn1_gate_up",
                                           carry=comm.gathers["down_in"] if comm else None)
    if comm:
        W.update(comm.gathered("down_in", got))
    h1, u, n2 = _ffn_down_mix_in(x, act1, W["wd1"], norms["mix_norm"], W["win"], "ffn1_down_mix_in")
    (attn, lse), got = _attn_fwd(u, qg2, kg2, B, S, carry=comm.gathers["ffn2"] if comm else None)
    if comm:
        W = dict(W, **comm.gathered("ffn2", got))
    conv, y = _conv_fwd(u, cw, norms["conv_b"], norms["conv_ln_g"], norms["conv_ln_b"], B, S)
    h2, n3, g2, u2, dout, dyb, sq, act2 = _mix_out_ffn_loss(h1, attn, conv, W["wout"], norms["ffn2_norm"],
                                                            W["wg2"], W["wu2"], W["wd2"], target, "ffn2_fwd")
    loss = (0.5 / D) * jnp.sum(sq)

    (dg2, du2, dh2, dgn_ffn2), _ = _ffn_bwd_act(dyb, g2, u2, h2, dout, norms["ffn2_norm"],
                                               W["wg2"], W["wu2"], W["wd2"], "ffn2_bwd_act")
    dwd2, _ = _ffn_bwd_w(act2, dyb, "ffn2_bwd_wd")
    dwg2, _ = _ffn_bwd_w(dg2, n3, "ffn2_bwd_wg")
    dwu2, _ = _ffn_bwd_w(du2, n3, "ffn2_bwd_wu")
    dattn, dconv, dwout = _mix_out_bwd(dh2, attn, conv, W["wout"])
    carry = comm.reduce_start({"wg2": dwg2, "wu2": dwu2, "wd2": dwd2, "wout": dwout}) if comm else None
    (dq, dk, dv, dgn_qk), got = _attn_bwd(u, attn, dattn, lse, qg2, kg2, B, S, carry=carry)
    if comm:
        comm.reduce_done(carry, got)
    dca, dcg, dcw, dcs = _conv_bwd(u, y, dconv, cw, norms["conv_ln_g"], norms["conv_ln_b"], B, S)
    dwin, dh1, dyb1, dgn_mix = _mix_in_bwd((dq, dk, dv, dca, dcg), W["win"], n2, h1, dh2, norms["mix_norm"])
    (dg1, du1, gx, dgn_ffn1), _ = _ffn_bwd_act(dyb1, g1, u1, x, dh1, norms["ffn1_norm"],
                                              W["wg1"], W["wu1"], W["wd1"], "ffn1_bwd_act")
    carry = comm.reduce_start({"win": dwin}) if comm else None
    dwd1, got = _ffn_bwd_w(act1, dyb1, "ffn1_bwd_wd", carry=carry)
    if comm:
        comm.reduce_done(carry, got)
        carry = comm.reduce_start({"wd1": dwd1})
    dwg1, got = _ffn_bwd_w(dg1, n1, "ffn1_bwd_wg", carry=carry)
    if comm:
        comm.reduce_done(carry, got)
        carry = comm.reduce_start({"wg1": dwg1})
    dwu1, got = _ffn_bwd_w(du1, n1, "ffn1_bwd_wu", carry=carry)
    if comm:
        comm.reduce_done(carry, got)
        comm.last = comm.reduce_start({"wu1": dwu1})

    qk = jnp.sum(dgn_qk, axis=0)
    cs = jnp.sum(dcs, axis=0)
    small = {
        "ffn1_norm": jnp.sum(dgn_ffn1, axis=0),
        "mix_norm": jnp.sum(dgn_mix, axis=0),
        "q_norm": qk[0:1, 0:HD] + qk[0:1, HD:2 * HD],
        "k_norm": qk[1:2, 0:HD] + qk[1:2, HD:2 * HD],
        "conv_w": jnp.sum(dcw, axis=0)[0:CK],
        "conv_b": cs[0:1],
        "conv_ln_g": cs[1:2],
        "conv_ln_b": cs[2:3],
        "ffn2_norm": jnp.sum(dgn_ffn2, axis=0),
    }
    big = {"wg1": dwg1, "wu1": dwu1, "wd1": dwd1, "win": dwin, "wout": dwout,
           "wg2": dwg2, "wu2": dwu2, "wd2": dwd2}
    return loss, gx, big, small


HBM = pl.BlockSpec(memory_space=pltpu.HBM)
VMEM = pl.BlockSpec(memory_space=pltpu.VMEM)


def _place():
    return lax.axis_index("x"), lax.axis_index("y"), lax.axis_index("c")


class _GatherCarry:
    def __init__(self, shards, mid_at=0.5):
        nt = len(shards)
        self.mid_at = mid_at
        self.shards = shards
        self.in_arrays = [s for s, _ in shards]
        self.in_specs = [VMEM] * nt
        self.out_shape = [jax.ShapeDtypeStruct((NDEV * s.shape[0], s.shape[1]), dt) for s, dt in shards]
        self.out_specs = [HBM] * nt
        self.scratch = ([pltpu.VMEM(s.shape, dt) for s, dt in shards]
                        + [pltpu.SemaphoreType.DMA((nt, 7)), pltpu.SemaphoreType.DMA((nt, 7)),
                           pltpu.SemaphoreType.DMA((nt,))])

    def _copies(self, outs, scr):
        nt = len(self.shards)
        stages = scr[:nt]
        send_sems, recv_sems, local_sems = scr[nt:]
        x, y, c = _place()
        me, sibling = (x, y, c), (x, y, 1 - c)
        xn, yn, diag = (1 - x, y, c), (x, 1 - y, c), (1 - x, 1 - y, c)
        via = (x ^ c, y ^ (1 - c), c)
        onto = (x ^ (1 - c), y ^ c, c)

        def rows(t, px, py, pc):
            r = self.shards[t][0].shape[0]
            return outs[t].at[pl.ds((4 * px + 2 * py + pc) * r, r), :]

        def copy(t, k, block, to, src=None):
            return pltpu.make_async_remote_copy(
                src_ref=rows(t, *block) if src is None else src, dst_ref=rows(t, *block),
                send_sem=send_sems.at[t, k], recv_sem=recv_sems.at[t, k],
                device_id=to, device_id_type=MESH)

        sib = lambda b: (b[0], b[1], 1 - c)
        return dict(
            local=[pltpu.make_async_copy(stages[t], rows(t, *me), local_sems.at[t]) for t in range(nt)],
            own=[[copy(t, 0, me, sibling, src=stages[t]), copy(t, 1, me, xn, src=stages[t]),
                  copy(t, 2, me, yn, src=stages[t])] for t in range(nt)],
            relay=[copy(t, 3, via, onto) for t in range(nt)],
            down=[[copy(t, 4, xn, sibling), copy(t, 5, yn, sibling)] for t in range(nt)],
            down_diag=[copy(t, 6, diag, sibling) for t in range(nt)],
            got_xy=[[copy(t, 1, xn, me), copy(t, 2, yn, me)] for t in range(nt)],
            got_diag=[copy(t, 3, diag, me) for t in range(nt)],
            got_sib=[[copy(t, 0, sibling, me), copy(t, 4, sib(xn), me), copy(t, 5, sib(yn), me),
                      copy(t, 6, sib(diag), me)] for t in range(nt)])

    def start(self, ins, outs, scr):
        cps = self._copies(outs, scr)
        for t, (_, dt) in enumerate(self.shards):
            scr[t][...] = ins[t][...].astype(dt)
            for cp in [cps["local"][t]] + cps["own"][t]:
                cp.start()

    def stages(self):
        sizes = [s.size * jnp.dtype(dt).itemsize for s, dt in self.shards]
        done = [sum(sizes[:t + 1]) / sum(sizes) for t in range(len(sizes))]
        return [(self.mid_at * f, functools.partial(self.mid, t)) for t, f in enumerate(done)]

    def mid(self, t, ins, outs, scr):
        cps = self._copies(outs, scr)
        for cp in cps["got_xy"][t]:
            cp.wait_recv()
        for cp in [cps["relay"][t]] + cps["down"][t]:
            cp.start()

    def finish(self, ins, outs, scr):
        cps = self._copies(outs, scr)
        for t in range(len(self.shards)):
            cps["got_diag"][t].wait_recv()
            cps["down_diag"][t].start()
        for t in range(len(self.shards)):
            for cp in cps["got_sib"][t]:
                cp.wait_recv()
            for cp in cps["own"][t] + [cps["relay"][t]] + cps["down"][t] + [cps["down_diag"][t]]:
                cp.wait_send()
            cps["local"][t].wait()


def _run_carry(carry, name):
    def body(*refs):
        n_in, n_out = len(carry.in_arrays), len(carry.out_shape)
        ins, outs, scr = refs[:n_in], refs[n_in:n_in + n_out], refs[n_in + n_out:]
        carry.start(ins, outs, scr)
        for _, stage in carry.stages():
            stage(ins, outs, scr)
        carry.finish(ins, outs, scr)

    return pl.pallas_call(
        body, in_specs=carry.in_specs, out_specs=carry.out_specs, out_shape=carry.out_shape,
        scratch_shapes=carry.scratch, compiler_params=pltpu.CompilerParams(vmem_limit_bytes=VMEM_LIMIT),
        name=name)(*carry.in_arrays)


class _ExchangeCarry:
    def __init__(self, names, grads, mid_at=0.45):
        nt = len(grads)
        self.mid_at = mid_at
        self.names = names
        self.in_arrays = [g.reshape(4, 2, g.shape[0] // NDEV, g.shape[1]) for g in grads]
        self.in_specs = [HBM] * nt
        blocks = [g.shape[2:] for g in self.in_arrays]
        self.out_shape = [jax.ShapeDtypeStruct((3,) + b, BF16) for b in blocks]
        self.out_specs = [HBM] * nt
        self.scratch = ([pltpu.VMEM((4,) + b, BF16) for b in blocks] * 2 + [pltpu.VMEM(b, BF16) for b in blocks]
                        + [pltpu.SemaphoreType.DMA((nt, 3)), pltpu.SemaphoreType.DMA((nt, 3))]
                        + [pltpu.SemaphoreType.DMA((nt,))] * 4)

    def _copies(self, ins, outs, scr):
        nt = len(ins)
        theirs, own, relayed = scr[:nt], scr[nt:2 * nt], scr[2 * nt:3 * nt]
        send_sems, recv_sems, keep_sems, load_sems, swap_send, swap_recv = scr[3 * nt:]
        x, y, c = _place()
        q = lambda cx, cy: 2 * cx + cy
        near, far = (x ^ c, y ^ (1 - c)), (x ^ (1 - c), y ^ c)

        def remote(t, k, src, dst, chip):
            return pltpu.make_async_remote_copy(
                src_ref=src, dst_ref=dst, send_sem=send_sems.at[t, k], recv_sem=recv_sems.at[t, k],
                device_id=(*chip, c), device_id_type=MESH)

        return dict(
            swap=[pltpu.make_async_remote_copy(
                src_ref=ins[t].at[:, 1 - c], dst_ref=theirs[t], send_sem=swap_send.at[t], recv_sem=swap_recv.at[t],
                device_id=(x, y, 1 - c), device_id_type=MESH) for t in range(nt)],
            load=[pltpu.make_async_copy(ins[t].at[:, c], own[t], load_sems.at[t]) for t in range(nt)],
            keep=[pltpu.make_async_copy(own[t].at[q(x, y)], outs[t].at[0], keep_sems.at[t]) for t in range(nt)],
            direct=[remote(t, 0, own[t].at[q(*near)], outs[t].at[1], near) for t in range(nt)],
            relay=[remote(t, 1, own[t].at[q(1 - x, 1 - y)], relayed[t], near) for t in range(nt)],
            merged=[remote(t, 2, own[t].at[q(*far)], outs[t].at[2], far) for t in range(nt)],
            theirs=theirs, own=own, relayed=relayed, far=q(*far))

    EARLY_AT = 0.07

    def stages(self):
        return [(self.EARLY_AT, self.early), (self.mid_at, self.mid)]

    def start(self, ins, outs, scr):
        cps = self._copies(ins, outs, scr)
        for t in range(len(ins)):
            cps["swap"][t].start()
            cps["load"][t].start()

    def early(self, ins, outs, scr):
        cps = self._copies(ins, outs, scr)
        for t in range(len(ins)):
            cps["load"][t].wait()
            cps["swap"][t].wait_recv()
            own, theirs = cps["own"][t], cps["theirs"][t]
            for j in range(4):
                own[j] = (own[j].astype(F32) + theirs[j].astype(F32)).astype(BF16)
            for kind in ("relay", "direct", "keep"):
                cps[kind][t].start()

    def mid(self, ins, outs, scr):
        cps = self._copies(ins, outs, scr)
        for t in range(len(ins)):
            cps["relay"][t].wait_recv()
            own, far = cps["own"][t], cps["far"]
            own[far] = (own[far].astype(F32) + cps["relayed"][t][...].astype(F32)).astype(BF16)
            cps["merged"][t].start()

    def finish(self, ins, outs, scr):
        cps = self._copies(ins, outs, scr)
        for t in range(len(ins)):
            cps["swap"][t].wait_send()
            cps["direct"][t].wait()
            cps["relay"][t].wait_send()
            cps["merged"][t].wait()
            cps["keep"][t].wait()


class _Comm:
    def __init__(self, groups):
        self.names = {tag: list(g) for tag, (g, _) in groups.items()}
        self.gathers = {tag: _GatherCarry(list(g.values()), mid_at) for tag, (g, mid_at) in groups.items()}
        self.reduced = {}
        self.last = None

    def gathered(self, tag, outs):
        return dict(zip(self.names[tag], outs))

    def reduce_start(self, grads):
        names = list(grads)
        return _ExchangeCarry(names, [grads[n] for n in names])

    def reduce_done(self, carry, outs):
        self.reduced.update(zip(carry.names, outs))


def _adamw_math(w, g, m, v):
    m = B1 * m + (1.0 - B1) * g
    v = B2 * v + (1.0 - B2) * (g * g)
    m_hat = m / (1.0 - B1 ** STEP)
    v_hat = v / (1.0 - B2 ** STEP)
    delta = -LR * (m_hat / (jnp.sqrt(v_hat) + AEPS) + WD * w)
    return delta, m, v


def _adamw_big(recv, w, m, v, name):
    def body(r_ref, w_ref, m_ref, v_ref, g_ref, d_ref, mo_ref, vo_ref):
        g = r_ref[0].astype(F32)
        for q in range(1, 3):
            g = g + r_ref[q].astype(F32)
        d, mn, vn = _adamw_math(w_ref[...], g, m_ref[...], v_ref[...])
        g_ref[...] = g
        d_ref[...] = d
        mo_ref[...] = mn
        vo_ref[...] = vn

    rows, n = w.shape
    tr = rows // 2
    row = pl.BlockSpec((tr, n), lambda t: (t, 0))
    return pl.pallas_call(
        body, grid=(2,), in_specs=[pl.BlockSpec((3, tr, n), lambda t: (0, t, 0)), row, row, row],
        out_specs=[row] * 4, out_shape=[jax.ShapeDtypeStruct(w.shape, F32)] * 4,
        compiler_params=_cp(("parallel",)), name=name)(recv, w, m, v)


SMALL_NAMES = ("ffn1_norm", "mix_norm", "ffn2_norm", "conv_b", "conv_ln_g", "conv_ln_b", "q_norm", "k_norm")
SROWS = 16
LOSS_ROW = len(SMALL_NAMES)
CWF = 2


def _small_sums(gs, loss_row, gcw, carry=None):
    ns = len(SMALL_NAMES)
    widths = [g.shape[1] for g in gs]

    def body(*refs):
        it = iter(refs)
        take = lambda n: [next(it) for _ in range(n)]
        g_refs, (loss_ref, gcw_ref) = take(ns), take(2)
        cins = take(len(carry.in_arrays)) if carry else []
        tot_ref, ctot_ref = take(2)
        couts = take(len(carry.out_shape)) if carry else []
        send, slots, cslots, send_sems, recv_sems, csend_sems, crecv_sems = take(7)
        cscr = list(it)
        x, y, c = _place()
        me = 4 * x + 2 * y + c
        send[...] = jnp.zeros_like(send)
        for k in range(ns):
            send[k:k + 1, 0:widths[k]] = g_refs[k][...]
        send[LOSS_ROW:LOSS_ROW + 1, 0:128] = loss_ref[...]
        slots[me] = send[...]
        cslots[me] = gcw_ref[...]
        cps = []
        for k in range(1, NDEV):
            peer = (x ^ ((k >> 2) & 1), y ^ ((k >> 1) & 1), c ^ (k & 1))
            cps.append(pltpu.make_async_remote_copy(
                src_ref=send, dst_ref=slots.at[me], send_sem=send_sems.at[k - 1], recv_sem=recv_sems.at[k - 1],
                device_id=peer, device_id_type=MESH))
            cps.append(pltpu.make_async_remote_copy(
                src_ref=gcw_ref, dst_ref=cslots.at[me], send_sem=csend_sems.at[k - 1],
                recv_sem=crecv_sems.at[k - 1], device_id=peer, device_id_type=MESH))
        for cp in cps:
            cp.start()
        stages = [stage for _, stage in carry.stages()] if carry else []
        if carry:
            carry.start(cins, couts, cscr)
        for stage in stages[:1]:
            stage(cins, couts, cscr)
        for cp in cps:
            cp.wait()
        tot = slots[0]
        ctot = cslots[0, me]
        for j in range(1, NDEV):
            tot = tot + slots[j]
            ctot = ctot + cslots[j, me]
        tot_ref[...] = tot
        ctot_ref[...] = ctot
        for stage in stages[1:]:
            stage(cins, couts, cscr)
        if carry:
            carry.finish(cins, couts, cscr)

    args = [*gs, loss_row, gcw]
    out_shape = [jax.ShapeDtypeStruct((SROWS, D), F32), jax.ShapeDtypeStruct((CWF, D), F32)]
    res = pl.pallas_call(
        body, in_specs=[VMEM] * len(args) + (carry.in_specs if carry else []),
        out_specs=[VMEM] * 2 + (carry.out_specs if carry else []),
        out_shape=out_shape + (carry.out_shape if carry else []),
        scratch_shapes=[pltpu.VMEM((SROWS, D), F32), pltpu.VMEM((NDEV, SROWS, D), F32),
                        pltpu.VMEM((NDEV, NDEV, CWF, D), F32)]
                       + [pltpu.SemaphoreType.DMA((NDEV - 1,))] * 4 + (carry.scratch if carry else []),
        name="small_sums")(*args, *(carry.in_arrays if carry else []))
    return res[0], res[1], res[2:]


def _adamw_small(tot, ctot, ws, ms, vs, wcw, mcw, vcw):
    ns = len(SMALL_NAMES)
    widths = [w.shape[1] for w in ws]

    def body(*refs):
        it = iter(refs)
        take = lambda n: [next(it) for _ in range(n)]
        (tot_ref, ctot_ref), w_refs, m_refs, v_refs = take(2), take(ns), take(ns), take(ns)
        wcw_ref, mcw_ref, vcw_ref = take(3)
        outs = [take(4) for _ in range(ns)]
        cw_outs, (loss_out,) = take(4), take(1)

        def step(g, w_ref, m_ref, v_ref, o):
            d, mn, vn = _adamw_math(w_ref[...], g, m_ref[...], v_ref[...])
            o[0][...], o[1][...], o[2][...], o[3][...] = g, d, mn, vn

        for k in range(ns):
            step(tot_ref[k:k + 1, 0:widths[k]], w_refs[k], m_refs[k], v_refs[k], outs[k])
        step(ctot_ref[...], wcw_ref, mcw_ref, vcw_ref, cw_outs)
        loss_out[...] = tot_ref[LOSS_ROW:LOSS_ROW + 1, 0:128]

    args = [tot, ctot, *ws, *ms, *vs, wcw, mcw, vcw]
    out_shape = ([jax.ShapeDtypeStruct((1, n), F32) for n in widths for _ in range(4)]
                 + [jax.ShapeDtypeStruct((CWF, D), F32)] * 4 + [jax.ShapeDtypeStruct((1, 128), F32)])
    res = pl.pallas_call(
        body, in_specs=[VMEM] * len(args), out_specs=[VMEM] * len(out_shape), out_shape=out_shape,
        name="adamw_small")(*args)
    per = [res[4 * k:4 * k + 4] for k in range(ns)]
    return per, res[4 * ns:4 * ns + 4], res[-1]


def _pack_cw(a):
    flat = a.reshape(a.shape[:-2] + (CK * HD,))
    pad = [(0, 0)] * (flat.ndim - 1) + [(0, CWF * D - CK * HD)]
    return jnp.pad(flat, pad).reshape(a.shape[:-2] + (CWF, D))


def _unpack_cw(v):
    return v.reshape(-1)[:CK * HD].reshape(1, CK, HD)


def kernel(x, ffn1_norm, ffn1_w_gate, ffn1_w_up, ffn1_w_down, mix_norm, w_in, q_norm, k_norm, conv_w, conv_b, conv_ln_g, conv_ln_b, w_out, ffn2_norm, ffn2_w_gate, ffn2_w_up, ffn2_w_down, loss_target, m_ffn1_norm, m_ffn1_w_gate, m_ffn1_w_up, m_ffn1_w_down, m_mix_norm, m_w_in, m_q_norm, m_k_norm, m_conv_w, m_conv_b, m_conv_ln_g, m_conv_ln_b, m_w_out, m_ffn2_norm, m_ffn2_w_gate, m_ffn2_w_up, m_ffn2_w_down, v_ffn1_norm, v_ffn1_w_gate, v_ffn1_w_up, v_ffn1_w_down, v_mix_norm, v_w_in, v_q_norm, v_k_norm, v_conv_w, v_conv_b, v_conv_ln_g, v_conv_ln_b, v_w_out, v_ffn2_norm, v_ffn2_w_gate, v_ffn2_w_up, v_ffn2_w_down):
    P = dict(ffn1_norm=ffn1_norm, ffn1_w_gate=ffn1_w_gate, ffn1_w_up=ffn1_w_up, ffn1_w_down=ffn1_w_down,
             mix_norm=mix_norm, w_in=w_in, q_norm=q_norm, k_norm=k_norm, conv_w=conv_w, conv_b=conv_b,
             conv_ln_g=conv_ln_g, conv_ln_b=conv_ln_b, w_out=w_out, ffn2_norm=ffn2_norm,
             ffn2_w_gate=ffn2_w_gate, ffn2_w_up=ffn2_w_up, ffn2_w_down=ffn2_w_down)
    M = dict(ffn1_norm=m_ffn1_norm, ffn1_w_gate=m_ffn1_w_gate, ffn1_w_up=m_ffn1_w_up, ffn1_w_down=m_ffn1_w_down,
             mix_norm=m_mix_norm, w_in=m_w_in, q_norm=m_q_norm, k_norm=m_k_norm, conv_w=m_conv_w, conv_b=m_conv_b,
             conv_ln_g=m_conv_ln_g, conv_ln_b=m_conv_ln_b, w_out=m_w_out, ffn2_norm=m_ffn2_norm,
             ffn2_w_gate=m_ffn2_w_gate, ffn2_w_up=m_ffn2_w_up, ffn2_w_down=m_ffn2_w_down)
    V = dict(ffn1_norm=v_ffn1_norm, ffn1_w_gate=v_ffn1_w_gate, ffn1_w_up=v_ffn1_w_up, ffn1_w_down=v_ffn1_w_down,
             mix_norm=v_mix_norm, w_in=v_w_in, q_norm=v_q_norm, k_norm=v_k_norm, conv_w=v_conv_w, conv_b=v_conv_b,
             conv_ln_g=v_conv_ln_g, conv_ln_b=v_conv_ln_b, w_out=v_w_out, ffn2_norm=v_ffn2_norm,
             ffn2_w_gate=v_ffn2_w_gate, ffn2_w_up=v_ffn2_w_up, ffn2_w_down=v_ffn2_w_down)
    order = ["ffn1_norm", "ffn1_w_gate", "ffn1_w_up", "ffn1_w_down", "mix_norm", "w_in", "q_norm", "k_norm",
             "conv_w", "conv_b", "conv_ln_g", "conv_ln_b", "w_out", "ffn2_norm", "ffn2_w_gate", "ffn2_w_up",
             "ffn2_w_down"]
    B, S, _ = x.shape
    T = B * S

    bigs = [("wg1", "ffn1_w_gate", True), ("wu1", "ffn1_w_up", True), ("wd1", "ffn1_w_down", False),
            ("win", "w_in", True), ("wout", "w_out", False),
            ("wg2", "ffn2_w_gate", True), ("wu2", "ffn2_w_up", True), ("wd2", "ffn2_w_down", False)]
    hm = lambda a, tr: jnp.transpose(a[0]) if tr else a[0]
    cw_pad = jnp.zeros((32, 128), F32).at[0:CK, 0:HD].set(conv_w[0])
    shard = {ln: (hm(P[pn], tr), BF16) for ln, pn, tr in bigs}
    gathered = _run_carry(_GatherCarry([shard["wg1"], shard["wu1"], (cw_pad, F32)]), "gather_first")
    W = {"wg1": gathered[0], "wu1": gathered[1]}
    cwg = gathered[2].reshape(NDEV, 32, 128)[:, 0:CK, 0:HD]
    W["conv_w"] = jnp.transpose(cwg, (1, 0, 2)).reshape(CK, DC)
    norms = {n: P[n] for n in SMALL_NAMES}
    comm = _Comm({"down_in": ({n: shard[n] for n in ("wd1", "win")}, 0.5),
                  "ffn2": ({n: shard[n] for n in ("wg2", "wu2", "wd2", "wout")}, 0.5)})

    loss_part, gx, _, small = _local_step(x.reshape(T, D), loss_target.reshape(T, D), norms, W, B, S, comm)

    G, Dl, Mn, Vn = {}, {}, {}, {}
    dcw = small["conv_w"].reshape(CK, NDEV, HD).transpose(1, 0, 2)
    loss_row = jnp.zeros((1, 128), F32).at[0, 0].set(loss_part)
    tot, ctot, got = _small_sums([small[n] for n in SMALL_NAMES], loss_row, _pack_cw(dcw), carry=comm.last)
    comm.reduce_done(comm.last, got)
    per, cw_outs, loss_out = _adamw_small(
        tot, ctot, [P[n] for n in SMALL_NAMES], [M[n] for n in SMALL_NAMES], [V[n] for n in SMALL_NAMES],
        _pack_cw(P["conv_w"][0]), _pack_cw(M["conv_w"][0]), _pack_cw(V["conv_w"][0]))
    loss = loss_out[0, 0]
    for n, outs in zip(SMALL_NAMES, per):
        G[n], Dl[n], Mn[n], Vn[n] = outs
    G["conv_w"], Dl["conv_w"], Mn["conv_w"], Vn["conv_w"] = [_unpack_cw(o) for o in cw_outs]

    for ln, pn, tr in bigs:
        outs = _adamw_big(comm.reduced[ln], hm(P[pn], tr), hm(M[pn], tr), hm(V[pn], tr), "adamw_" + ln)
        G[pn], Dl[pn], Mn[pn], Vn[pn] = [(jnp.transpose(o) if tr else o)[None] for o in outs]

    return (loss, gx.reshape(B, S, D), *[G[n] for n in order], *[Dl[n] for n in order],
            *[Mn[n] for n in order], *[Vn[n] for n in order])
```

```python
import functools

import jax
import jax.numpy as jnp
from jax import lax
from jax.experimental import pallas as pl
from jax.experimental.pallas import tpu as pltpu

F32 = jnp.float32
BF16 = jnp.bfloat16

D = 1024
FF = 2816
HD = 64
DA = 512
DC = 512
DIN = 2560
CK = 31
BLK = 128
DILS = (1, 4, 16)
EPS = 1e-6
NDEV = 8
MESH = pl.DeviceIdType.MESH

LR, B1, B2, AEPS, WD, STEP = 0.001, 0.9, 0.999, 1e-08, 0.01, 10

NT = (((1,), (1,)), ((), ()))
TN = (((0,), (0,)), ((), ()))

VMEM_LIMIT = 60 * 1024 * 1024


def _cp(sem=None):
    return pltpu.CompilerParams(dimension_semantics=sem, vmem_limit_bytes=VMEM_LIMIT)


def _sigmoid(x):
    return 0.5 * (jnp.tanh(0.5 * x) + 1.0)


def _pallas(body, args, *, grid, in_specs, out_specs, out_shape, scratch_shapes, sem, name, carry=None):
    if carry is None:
        outs = pl.pallas_call(body, grid=grid, in_specs=in_specs, out_specs=out_specs, out_shape=out_shape,
                              scratch_shapes=scratch_shapes, compiler_params=_cp(sem), name=name)(*args)
        return outs, None
    n_in, n_out, n_scr = len(in_specs), len(out_shape), len(scratch_shapes)
    c_in, c_out = len(carry.in_arrays), len(carry.out_shape)

    def wrapped(*refs):
        ins, refs = refs[:n_in], refs[n_in:]
        cins, refs = refs[:c_in], refs[c_in:]
        outs, refs = refs[:n_out], refs[n_out:]
        couts, refs = refs[:c_out], refs[c_out:]
        scr, cscr = refs[:n_scr], refs[n_scr:]
        ids = [pl.program_id(a) for a in range(len(grid))]
        step = ids[0]
        for i, n in zip(ids[1:], grid[1:]):
            step = step * n + i
        steps = functools.reduce(lambda a, b: a * b, grid)

        @pl.when(step == 0)
        def _():
            carry.start(cins, couts, cscr)

        body(*ins, *outs, *scr)

        for frac, stage in carry.stages():
            @pl.when(step == int(steps * frac))
            def _(stage=stage):
                stage(cins, couts, cscr)

        @pl.when(step == steps - 1)
        def _():
            carry.finish(cins, couts, cscr)

    outs = pl.pallas_call(
        wrapped, grid=grid, in_specs=list(in_specs) + carry.in_specs, out_specs=list(out_specs) + carry.out_specs,
        out_shape=list(out_shape) + carry.out_shape, scratch_shapes=list(scratch_shapes) + carry.scratch,
        compiler_params=_cp(("arbitrary",) * len(grid)), name=name)(*args, *carry.in_arrays)
    return outs[:n_out], outs[n_out:]


FC = 256


def _resident(shape):
    return pl.BlockSpec(shape, lambda *_: (0,) * len(shape), pipeline_mode=pl.Buffered(1))


def _mix_out_ffn_loss(h1, attn, conv, wout, gain, wg, wu, wd, target, name):
    T = h1.shape[0]
    tm = 512
    nt = T // tm

    def body(h1_ref, at_ref, cv_ref, wo_ref, gain_ref, wg_ref, wu_ref, wd_ref, t_ref,
             h2_ref, n_ref, g_ref, u_ref, dout_ref, dyb_ref, sq_ref, a_hbm, a_scr, a_sem):
        t = pl.program_id(0)
        a_out = lambda i: pltpu.make_async_copy(a_scr, a_hbm.at[pl.ds(pl.multiple_of(i * tm, tm), tm), :], a_sem)

        @pl.when(t > 0)
        def _():
            a_out(t - 1).wait()

        xv = (h1_ref[...]
              + jnp.dot(at_ref[...], wo_ref[0:DA, :], preferred_element_type=F32)
              + jnp.dot(cv_ref[...], wo_ref[DA:D, :], preferred_element_type=F32))
        h2_ref[...] = xv
        r = lax.rsqrt(jnp.mean(xv * xv, axis=-1, keepdims=True) + EPS)
        n_ref[...] = (xv * r * gain_ref[...]).astype(BF16)
        for c in range(FF // FC):
            cols = slice(c * FC, (c + 1) * FC)
            nb = n_ref[...]
            g = lax.dot_general(nb, wg_ref[cols, :], NT, preferred_element_type=F32)
            u = lax.dot_general(nb, wu_ref[cols, :], NT, preferred_element_type=F32)
            g_ref[:, cols] = g.astype(BF16)
            u_ref[:, cols] = u.astype(BF16)
            a_scr[:, cols] = (g * _sigmoid(g) * u).astype(BF16)
        a_out(t).start()
        e = h2_ref[...] + 0.5 * jnp.dot(a_scr[...], wd_ref[...], preferred_element_type=F32) - t_ref[...]
        dout = e * (1.0 / D)
        dout_ref[...] = dout
        dyb_ref[...] = (0.5 * dout).astype(BF16)
        sq_ref[...] = jnp.sum(e * e, axis=0, keepdims=True)[None]

        @pl.when(t == nt - 1)
        def _():
            a_out(t).wait()

    row = pl.BlockSpec((tm, D), lambda t: (t, 0))
    half = pl.BlockSpec((tm, DA), lambda t: (t, 0))
    wide = pl.BlockSpec((tm, FF), lambda t: (t, 0))
    outs, _ = _pallas(
        body, (h1, attn, conv, wout, gain, wg, wu, wd, target), grid=(nt,),
        in_specs=[row, half, half, _resident((D, D)), _resident((1, D)), _resident((FF, D)), _resident((FF, D)),
                  _resident((FF, D)), row],
        out_specs=[row, row, wide, wide, row, row, pl.BlockSpec((1, 1, D), lambda t: (t, 0, 0)), HBM],
        out_shape=[jax.ShapeDtypeStruct((T, D), F32), jax.ShapeDtypeStruct((T, D), BF16)]
                  + [jax.ShapeDtypeStruct((T, FF), BF16)] * 2
                  + [jax.ShapeDtypeStruct((T, D), F32), jax.ShapeDtypeStruct((T, D), BF16),
                     jax.ShapeDtypeStruct((nt, 1, D), F32), jax.ShapeDtypeStruct((T, FF), BF16)],
        scratch_shapes=[pltpu.VMEM((tm, FF), BF16), pltpu.SemaphoreType.DMA(())],
        sem=("arbitrary",), name=name)
    return outs


def _ffn_gate_up(x, gain, wg, wu, name, carry=None):
    T = x.shape[0]
    tm = 512

    def body(x_ref, gain_ref, wg_ref, wu_ref, n_ref, g_ref, u_ref, a_ref):
        xv = x_ref[...]
        r = lax.rsqrt(jnp.mean(xv * xv, axis=-1, keepdims=True) + EPS)
        n_ref[...] = (xv * r * gain_ref[...]).astype(BF16)
        for c in range(FF // FC):
            cols = slice(c * FC, (c + 1) * FC)
            nb = n_ref[...]
            g = lax.dot_general(nb, wg_ref[cols, :], NT, preferred_element_type=F32)
            u = lax.dot_general(nb, wu_ref[cols, :], NT, preferred_element_type=F32)
            g_ref[:, cols] = g.astype(BF16)
            u_ref[:, cols] = u.astype(BF16)
            a_ref[:, cols] = (g * _sigmoid(g) * u).astype(BF16)

    row = pl.BlockSpec((tm, D), lambda t: (t, 0))
    wide = pl.BlockSpec((tm, FF), lambda t: (t, 0))
    return _pallas(
        body, (x, gain, wg, wu), grid=(T // tm,),
        in_specs=[row, _resident((1, D)), _resident((FF, D)), _resident((FF, D))],
        out_specs=[row, wide, wide, wide],
        out_shape=[jax.ShapeDtypeStruct((T, D), BF16)] + [jax.ShapeDtypeStruct((T, FF), BF16)] * 3,
        scratch_shapes=[], sem=("parallel",), name=name, carry=carry)


def _ffn_down_mix_in(x, a, wd, gain, win, name):
    T = x.shape[0]
    tm = 512

    def body(x_ref, a_ref, wd_ref, gain_ref, win_ref, h_ref, u_ref, n_ref):
        hv = x_ref[...] + 0.5 * jnp.dot(a_ref[...], wd_ref[...], preferred_element_type=F32)
        h_ref[...] = hv
        r = lax.rsqrt(jnp.mean(hv * hv, axis=-1, keepdims=True) + EPS)
        n_ref[...] = (hv * r * gain_ref[...]).astype(BF16)
        u_ref[...] = lax.dot_general(n_ref[...], win_ref[...], NT, preferred_element_type=F32).astype(BF16)

    row = pl.BlockSpec((tm, D), lambda t: (t, 0))
    wide = pl.BlockSpec((tm, FF), lambda t: (t, 0))
    outs, _ = _pallas(
        body, (x, a, wd, gain, win), grid=(T // tm,),
        in_specs=[row, wide, _resident((FF, D)), _resident((1, D)), _resident((DIN, D))],
        out_specs=[row, pl.BlockSpec((tm, DIN), lambda t: (t, 0)), row],
        out_shape=[jax.ShapeDtypeStruct((T, D), F32), jax.ShapeDtypeStruct((T, DIN), BF16),
                   jax.ShapeDtypeStruct((T, D), BF16)],
        scratch_shapes=[], sem=("parallel",), name=name)
    return outs


def _ffn_bwd_act(dyb, g, u, x, dout, gain, wg, wu, wd, name, carry=None):
    T = x.shape[0]
    tm = 256
    nt = T // tm

    def body(dy_ref, g_ref, u_ref, x_ref, dout_ref, gain_ref, wg_ref, wu_ref, wd_ref,
             dg_ref, du_ref, dx_ref, dgn_ref):
        for c in range(FF // FC):
            cols = slice(c * FC, (c + 1) * FC)
            da = lax.dot_general(dy_ref[...], wd_ref[cols, :], NT, preferred_element_type=F32)
            gv = g_ref[:, cols].astype(F32)
            uv = u_ref[:, cols].astype(F32)
            sg = _sigmoid(gv)
            dg_ref[:, cols] = (da * uv * (sg * (1.0 + gv * (1.0 - sg)))).astype(BF16)
            du_ref[:, cols] = (da * (gv * sg)).astype(BF16)
        dn = (jnp.dot(dg_ref[...], wg_ref[...], preferred_element_type=F32)
              + jnp.dot(du_ref[...], wu_ref[...], preferred_element_type=F32))
        dx, dgain = _rms_bwd_rows(dn, x_ref[...], gain_ref[...])
        dx_ref[...] = dout_ref[...] + dx
        dgn_ref[...] = dgain[None]

    row = pl.BlockSpec((tm, D), lambda t: (t, 0))
    wide = pl.BlockSpec((tm, FF), lambda t: (t, 0))
    return _pallas(
        body, (dyb, g, u, x, dout, gain, wg, wu, wd), grid=(nt,),
        in_specs=[row, wide, wide, row, row, _resident((1, D)), _resident((FF, D)), _resident((FF, D)),
                  _resident((FF, D))],
        out_specs=[wide, wide, row, pl.BlockSpec((1, 1, D), lambda t: (t, 0, 0))],
        out_shape=[jax.ShapeDtypeStruct((T, FF), BF16)] * 2
                  + [jax.ShapeDtypeStruct((T, D), F32), jax.ShapeDtypeStruct((nt, 1, D), F32)],
        scratch_shapes=[], sem=("parallel",), name=name, carry=carry)


def _ffn_bwd_w(lhs, rhs, name, carry=None):
    T = rhs.shape[0]
    tf = 256

    def body(l_ref, r_ref, dw_ref):
        dw_ref[...] = lax.dot_general(l_ref[...], r_ref[...], TN, preferred_element_type=F32).astype(BF16)

    (dw,), got = _pallas(
        body, (lhs, rhs), grid=(FF // tf,),
        in_specs=[pl.BlockSpec((T, tf), lambda f: (0, f)), _resident((T, D))],
        out_specs=[pl.BlockSpec((tf, D), lambda f: (f, 0))], out_shape=[jax.ShapeDtypeStruct((FF, D), BF16)],
        scratch_shapes=[], sem=("parallel",), name=name, carry=carry)
    return dw, got


def _rms_bwd_rows(dn, xv, gain):
    r = lax.rsqrt(jnp.mean(xv * xv, axis=-1, keepdims=True) + EPS)
    xhat = xv * r
    dxhat = dn * gain
    dx = r * (dxhat - xhat * jnp.mean(dxhat * xhat, axis=-1, keepdims=True))
    return dx, jnp.sum(dn * xhat, axis=0, keepdims=True)


def _mix_out_bwd(dh, attn, conv, wout):
    T = dh.shape[0]
    tm = 512
    nt = T // tm

    def body(dh_ref, a_ref, c_ref, w_ref, da_ref, dc_ref, dw_ref, acc_scr):
        t = pl.program_id(0)

        @pl.when(t == 0)
        def _():
            acc_scr[...] = jnp.zeros_like(acc_scr)

        dhb = dh_ref[...].astype(BF16)
        dmix = lax.dot_general(dhb, w_ref[...], NT, preferred_element_type=F32)
        da_ref[...] = dmix[:, 0:DA].astype(BF16)
        dc_ref[...] = dmix[:, DA:D].astype(BF16)
        acc_scr[0:DA, :] += lax.dot_general(a_ref[...], dhb, TN, preferred_element_type=F32)
        acc_scr[DA:D, :] += lax.dot_general(c_ref[...], dhb, TN, preferred_element_type=F32)

        @pl.when(t == nt - 1)
        def _():
            dw_ref[...] = acc_scr[...].astype(BF16)

    row = pl.BlockSpec((tm, D), lambda t: (t, 0))
    half = pl.BlockSpec((tm, DA), lambda t: (t, 0))
    full = pl.BlockSpec((D, D), lambda t: (0, 0))
    return pl.pallas_call(
        body, grid=(nt,), in_specs=[row, half, half, full], out_specs=[half, half, full],
        out_shape=[jax.ShapeDtypeStruct((T, DA), BF16)] * 2 + [jax.ShapeDtypeStruct((D, D), BF16)],
        scratch_shapes=[pltpu.VMEM((D, D), F32)],
        compiler_params=_cp(("arbitrary",)), name="mix_out_bwd")(dh, attn, conv, wout)


def _mix_in_bwd(dparts, win, nb, h, dh, gain):
    T = h.shape[0]
    tm = 512
    nt = T // tm

    def body(d0, d1, d2, d3, d4, w_ref, n_ref, h_ref, dh_ref, gain_ref,
             dw_ref, dx_ref, dyb_ref, dg_ref, acc_scr):
        t = pl.program_id(0)

        @pl.when(t == 0)
        def _():
            acc_scr[...] = jnp.zeros_like(acc_scr)

        n = n_ref[...]
        dn = jnp.zeros((tm, D), F32)
        for i, d_ref in enumerate((d0, d1, d2, d3, d4)):
            dv = d_ref[...]
            dn = dn + jnp.dot(dv, w_ref[i * DA:(i + 1) * DA, :], preferred_element_type=F32)
            acc_scr[i * DA:(i + 1) * DA, :] += lax.dot_general(dv, n, TN, preferred_element_type=F32)
        dx, dgain = _rms_bwd_rows(dn, h_ref[...], gain_ref[...])
        tot = dh_ref[...] + dx
        dx_ref[...] = tot
        dyb_ref[...] = (0.5 * tot).astype(BF16)
        dg_ref[...] = dgain[None]

        @pl.when(t == nt - 1)
        def _():
            dw_ref[...] = acc_scr[...].astype(BF16)

    row = pl.BlockSpec((tm, D), lambda t: (t, 0))
    half = pl.BlockSpec((tm, DA), lambda t: (t, 0))
    full = pl.BlockSpec((DIN, D), lambda t: (0, 0))
    return pl.pallas_call(
        body, grid=(nt,),
        in_specs=[half] * 5 + [full, row, row, row, pl.BlockSpec((1, D), lambda t: (0, 0))],
        out_specs=[full, row, row, pl.BlockSpec((1, 1, D), lambda t: (t, 0, 0))],
        out_shape=[jax.ShapeDtypeStruct((DIN, D), BF16), jax.ShapeDtypeStruct((T, D), F32),
                   jax.ShapeDtypeStruct((T, D), BF16), jax.ShapeDtypeStruct((nt, 1, D), F32)],
        scratch_shapes=[pltpu.VMEM((DIN, D), F32)],
        compiler_params=_cp(("arbitrary",)), name="mix_in_bwd")(*dparts, win, nb, h, dh, gain)


def _head_masks():
    lane = lax.broadcasted_iota(jnp.int32, (1, 2 * HD), 1)
    m0 = lane < HD
    return m0, jnp.logical_not(m0)


def _stack_heads(v, m0, m1):
    z = jnp.zeros_like(v)
    return jnp.concatenate([jnp.where(m0, v, z), jnp.where(m1, v, z)], axis=0)


def _unstack_heads(v2, m0):
    return jnp.where(m0, v2[0:BLK], v2[BLK:2 * BLK])


def _head_sums(xv):
    ri = lax.broadcasted_iota(jnp.int32, (2 * HD, 2 * HD), 0)
    ci = lax.broadcasted_iota(jnp.int32, (2 * HD, 2 * HD), 1)
    ones = jnp.where((ri < HD) == (ci < HD), 1.0, 0.0).astype(BF16)
    hi = xv.astype(BF16)
    lo = (xv - hi.astype(F32)).astype(BF16)
    return (jnp.dot(hi, ones, preferred_element_type=F32) + jnp.dot(lo, ones, preferred_element_type=F32))


def _head_rms(xv):
    return lax.rsqrt(_head_sums(xv * xv) * (1.0 / HD) + EPS)


def _band_mask(first):
    qi = lax.broadcasted_iota(jnp.int32, (BLK, 2 * BLK), 0)
    ci = lax.broadcasted_iota(jnp.int32, (BLK, 2 * BLK), 1)
    band = (ci >= qi) & (ci <= qi + BLK)
    return band & ((ci >= BLK) | jnp.logical_not(first))


def _block_rows(j, d, seg):
    r, n = j // seg, j % seg
    start = r + (d * BLK) * n
    first = n == 0
    prev = jnp.where(first, start, start - d * BLK)
    return pl.ds(start, BLK, stride=d), pl.ds(prev, BLK, stride=d), first


def _block_keys(refs, cur, prev, first, single):
    if single:
        qi = lax.broadcasted_iota(jnp.int32, (BLK, BLK), 0)
        ci = lax.broadcasted_iota(jnp.int32, (BLK, BLK), 1)
        return [r[cur, :].astype(BF16) for r in refs], ci <= qi
    return ([jnp.concatenate([r[prev, :], r[cur, :]], axis=0).astype(BF16) for r in refs], _band_mask(first))


def _attn_fwd(u, qg2, kg2, B, S, carry=None):
    T = B * S
    NB = S // BLK
    scale = HD ** -0.5

    def body(q_ref, k_ref, v_ref, qg_ref, kg_ref, o_ref, lse_ref, qn, kn, vn, os_, ls_):
        m0, m1 = _head_masks()
        qv = q_ref[...].astype(F32)
        qn[...] = qv * _head_rms(qv) * (qg_ref[...] * scale)
        kv = k_ref[...].astype(F32)
        kn[...] = kv * _head_rms(kv) * kg_ref[...]
        vn[...] = v_ref[...].astype(F32)

        for i, d in enumerate(DILS):
            seg = NB // d

            def blk(j, c, i=i, d=d, seg=seg):
                cur, prev, first = _block_rows(j, d, seg)
                q2 = _stack_heads(qn[cur, :].astype(BF16), m0, m1)
                (kk, vv), mask = _block_keys((kn, vn), cur, prev, first, False)
                s = lax.dot_general(q2, kk, NT, preferred_element_type=F32)
                s = jnp.where(jnp.concatenate([mask, mask], axis=0), s, -1e30)
                mx = jnp.max(s, axis=-1, keepdims=True)
                p = jnp.exp(s - mx)
                l = jnp.sum(p, axis=-1, keepdims=True)
                o2 = jnp.dot((p * (1.0 / l)).astype(BF16), vv, preferred_element_type=F32)
                os_[i, cur, :] = _unstack_heads(o2, m0)
                ls_[i, cur, :] = _unstack_heads(mx + jnp.log(l), m0)
                return c

            lax.fori_loop(0, NB, blk, 0, unroll=8)

        def comb(c, carry):
            rows = pl.ds(pl.multiple_of(c * 256, 256), 256)
            l0, l1, l2 = ls_[0, rows, :], ls_[1, rows, :], ls_[2, rows, :]
            mx = jnp.maximum(jnp.maximum(l0, l1), l2)
            e0, e1, e2 = jnp.exp(l0 - mx), jnp.exp(l1 - mx), jnp.exp(l2 - mx)
            tot = e0 + e1 + e2
            inv = 1.0 / tot
            o = (e0 * os_[0, rows, :] + e1 * os_[1, rows, :] + e2 * os_[2, rows, :]) * inv
            o_ref[rows, :] = o.astype(BF16)
            lse_ref[rows, :] = mx + jnp.log(tot)
            return carry

        lax.fori_loop(0, S // 256, comb, 0)

    pair = 2 * HD
    blk_spec = lambda off: pl.BlockSpec((S, pair), lambda b, p, off=off: (b, off + p))
    gspec = pl.BlockSpec((1, pair), lambda b, p: (0, 0))
    return _pallas(
        body, (u, u, u, qg2, kg2), grid=(B, DA // pair),
        in_specs=[blk_spec(0), blk_spec(DA // pair), blk_spec(2 * DA // pair), gspec, gspec],
        out_specs=[blk_spec(0), blk_spec(0)],
        out_shape=[jax.ShapeDtypeStruct((T, DA), BF16), jax.ShapeDtypeStruct((T, DA), F32)],
        scratch_shapes=[pltpu.VMEM((S, pair), F32)] * 3 + [pltpu.VMEM((3, S, pair), F32)] * 2,
        sem=("parallel", "parallel"), name="attn_fwd", carry=carry)


def _attn_bwd(u, attn, dattn, lse, qg2, kg2, B, S, carry=None):
    T = B * S
    NB = S // BLK
    scale = HD ** -0.5
    pair = 2 * HD

    def body(q_ref, k_ref, v_ref, o_ref, do_ref, lse_ref, qg_ref, kg_ref,
             dq_ref, dk_ref, dv_ref, dgn_ref,
             qn, kn, vn, don, ldl, accq, acck, accv, rq, rk):
        m0, m1 = _head_masks()
        lane = lax.broadcasted_iota(jnp.int32, (1, pair), 1)
        qv = q_ref[...].astype(F32)
        rq[...] = _head_rms(qv)
        qn[...] = qv * rq[...] * (qg_ref[...] * scale)
        kv = k_ref[...].astype(F32)
        rk[...] = _head_rms(kv)
        kn[...] = kv * rk[...] * kg_ref[...]
        vn[...] = v_ref[...].astype(F32)
        dov = do_ref[...].astype(F32)
        don[...] = dov
        ldl[...] = jnp.where((lane % HD) < HD // 2, lse_ref[...], _head_sums(dov * o_ref[...].astype(F32)))

        for i, d in enumerate(DILS):
            seg = NB // d

            def blk(j, c, i=i, d=d, seg=seg):
                cur, prev, first = _block_rows(j, d, seg)
                q2 = _stack_heads(qn[cur, :].astype(BF16), m0, m1)
                do2 = _stack_heads(don[cur, :].astype(BF16), m0, m1)
                (kk, vv), mask = _block_keys((kn, vn), cur, prev, first, seg == 1)
                ldv = ldl[cur, :]
                lse2 = jnp.concatenate([ldv[:, 0:1], ldv[:, HD:HD + 1]], axis=0)
                dl2 = jnp.concatenate([ldv[:, HD // 2:HD // 2 + 1], ldv[:, HD + HD // 2:HD + HD // 2 + 1]], axis=0)
                s = lax.dot_general(q2, kk, NT, preferred_element_type=F32)
                p = jnp.where(jnp.concatenate([mask, mask], axis=0), jnp.exp(s - lse2), 0.0)
                dp = lax.dot_general(do2, vv, NT, preferred_element_type=F32)
                ds = (p * (dp - dl2)).astype(BF16)
                dq_acc = _unstack_heads(jnp.dot(ds, kk, preferred_element_type=F32), m0)
                dk_acc = lax.dot_general(ds, q2, TN, preferred_element_type=F32)
                dv_acc = lax.dot_general(p.astype(BF16), do2, TN, preferred_element_type=F32)
                if i == 0:
                    accq[cur, :] = dq_acc
                    acck[cur, :] = dk_acc[BLK:2 * BLK]
                    accv[cur, :] = dv_acc[BLK:2 * BLK]
                    acck[prev, :] += dk_acc[0:BLK]
                    accv[prev, :] += dv_acc[0:BLK]
                elif seg == 1:
                    accq[cur, :] += dq_acc
                    acck[cur, :] += dk_acc
                    accv[cur, :] += dv_acc
                else:
                    accq[cur, :] += dq_acc
                    acck[prev, :] += dk_acc[0:BLK]
                    acck[cur, :] += dk_acc[BLK:2 * BLK]
                    accv[prev, :] += dv_acc[0:BLK]
                    accv[cur, :] += dv_acc[BLK:2 * BLK]
                return c

            lax.fori_loop(0, NB, blk, 0, unroll=8)

        def norm_bwd(x_ref, r_ref, dn, gain):
            r = r_ref[...]
            xhat = x_ref[...].astype(F32) * r
            dxhat = dn * gain
            dx = r * (dxhat - xhat * (_head_sums(dxhat * xhat) * (1.0 / HD)))
            return dx, jnp.sum(dn * xhat, axis=0, keepdims=True)

        dq, dgq = norm_bwd(q_ref, rq, accq[...], qg_ref[...] * scale)
        dk, dgk = norm_bwd(k_ref, rk, acck[...], kg_ref[...])
        dq_ref[...] = dq.astype(BF16)
        dk_ref[...] = dk.astype(BF16)
        dv_ref[...] = accv[...].astype(BF16)
        dgn_ref[...] = jnp.concatenate([dgq * scale, dgk, jnp.zeros((6, pair), F32)], axis=0)[None]

    blk_spec = lambda off: pl.BlockSpec((S, pair), lambda b, p, off=off: (b, off + p))
    gspec = pl.BlockSpec((1, pair), lambda b, p: (0, 0))
    np_ = DA // pair
    return _pallas(
        body, (u, u, u, attn, dattn, lse, qg2, kg2), grid=(B, np_),
        in_specs=[blk_spec(0), blk_spec(np_), blk_spec(2 * np_), blk_spec(0), blk_spec(0), blk_spec(0),
                  gspec, gspec],
        out_specs=[blk_spec(0), blk_spec(0), blk_spec(0),
                   pl.BlockSpec((1, 8, pair), lambda b, p: (b * np_ + p, 0, 0))],
        out_shape=[jax.ShapeDtypeStruct((T, DA), BF16)] * 3 + [jax.ShapeDtypeStruct((B * np_, 8, pair), F32)],
        scratch_shapes=[pltpu.VMEM((S, pair), F32)] * 10,
        sem=("parallel", "parallel"), name="attn_bwd", carry=carry)


CT = 32
CPAD = 32


def _shifted(win, offsets):
    rolled, out = {}, {}
    n = win.shape[0]
    for o in offsets:
        sub = o % 8
        if sub not in rolled:
            rolled[sub] = win if sub == 0 else pltpu.roll(win, n - sub, 0)
        out[o] = rolled[sub][o - sub:o - sub + CT, :]
    return out


def _ln_fwd(y, g, b):
    mu = jnp.mean(y, axis=-1, keepdims=True)
    yc = y - mu
    rstd = lax.rsqrt(jnp.mean(yc * yc, axis=-1, keepdims=True) + EPS)
    xhat = yc * rstd
    return xhat, rstd, xhat * g + b


def _fill_glu(ca_ref, cg_ref, glu, S):
    glu[pl.ds(0, CPAD), :] = jnp.zeros((CPAD, DC), F32)

    def fill(i, c):
        rows = pl.ds(pl.multiple_of(i * 256, 256), 256)
        a = ca_ref[rows, :].astype(F32)
        gt = cg_ref[rows, :].astype(F32)
        glu[pl.ds(pl.multiple_of(CPAD + i * 256, CT), 256), :] = a * _sigmoid(gt)
        return c

    lax.fori_loop(0, S // 256, fill, 0)


def _conv_fwd(u, cw, cb, lg, lb, B, S):
    T = B * S

    def body(ca_ref, cg_ref, w_ref, b_ref, lg_ref, lb_ref, o_ref, y_ref, glu):
        _fill_glu(ca_ref, cg_ref, glu, S)

        def step(i, c):
            t0 = pl.multiple_of(i * CT, CT)
            win = glu[pl.ds(t0, 2 * CT), :]
            acc = jnp.zeros((CT, DC), F32) + b_ref[...]
            taps = _shifted(win, [k + 2 for k in range(CK)])
            for k in range(CK):
                acc = acc + taps[k + 2] * w_ref[k:k + 1, :]
            y_ref[pl.ds(t0, CT), :] = acc
            _, _, z = _ln_fwd(acc, lg_ref[...], lb_ref[...])
            o_ref[pl.ds(t0, CT), :] = (z * _sigmoid(z)).astype(BF16)
            return c

        lax.fori_loop(0, S // CT, step, 0, unroll=4)

    vec = pl.BlockSpec((1, DC), lambda b: (0, 0))
    return pl.pallas_call(
        body, grid=(B,),
        in_specs=[pl.BlockSpec((S, DC), lambda b: (b, 3)), pl.BlockSpec((S, DC), lambda b: (b, 4)),
                  pl.BlockSpec((CT, DC), lambda b: (0, 0)), vec, vec, vec],
        out_specs=[pl.BlockSpec((S, DC), lambda b: (b, 0))] * 2,
        out_shape=[jax.ShapeDtypeStruct((T, DC), BF16), jax.ShapeDtypeStruct((T, DC), F32)],
        scratch_shapes=[pltpu.VMEM((CPAD + S, DC), F32)],
        compiler_params=_cp(("parallel",)), name="conv_fwd")(u, u, cw, cb, lg, lb)


def _conv_bwd(u, y, dconv, cw, lg, lb, B, S):
    T = B * S

    def body(ca_ref, cg_ref, y_ref, dc_ref, w_ref, lg_ref, lb_ref,
             dca_ref, dcg_ref, dw_ref, ds_ref, glu, dyp, dwacc):
        _fill_glu(ca_ref, cg_ref, glu, S)
        dyp[pl.ds(S, CPAD), :] = jnp.zeros((CPAD, DC), F32)
        lgv, lbv = lg_ref[...], lb_ref[...]

        def sum8(v):
            return functools.reduce(jnp.add, [v[r:r + 8] for r in range(0, v.shape[0], 8)])

        P1 = 4 * CT

        def p1(i, carry):
            sb, sg, sl = carry
            t0 = pl.multiple_of(i * P1, P1)
            xhat, rstd, z = _ln_fwd(y_ref[pl.ds(t0, P1), :], lgv, lbv)
            sz = _sigmoid(z)
            dz = dc_ref[pl.ds(t0, P1), :].astype(F32) * (sz * (1.0 + z * (1.0 - sz)))
            dxhat = dz * lgv
            dy = rstd * (dxhat - jnp.mean(dxhat, axis=-1, keepdims=True)
                         - xhat * jnp.mean(dxhat * xhat, axis=-1, keepdims=True))
            dyp[pl.ds(t0, P1), :] = dy
            return sb + sum8(dy), sg + sum8(dz * xhat), sl + sum8(dz)

        z8 = jnp.zeros((8, DC), F32)
        sb, sg, sl = lax.fori_loop(0, S // P1, p1, (z8, z8, z8))
        rs = lambda v: jnp.sum(v, axis=0, keepdims=True)
        ds_ref[...] = jnp.concatenate([rs(sb), rs(sg), rs(sl), jnp.zeros((5, DC), F32)], axis=0)[None]

        def p2(i, c):
            t0 = pl.multiple_of(i * CT, CT)
            win = dyp[pl.ds(t0, 2 * CT), :]
            acc = jnp.zeros((CT, DC), F32)
            taps = _shifted(win, [30 - k for k in range(CK)])
            for k in range(CK):
                acc = acc + taps[30 - k] * w_ref[k:k + 1, :]
            a = ca_ref[pl.ds(t0, CT), :].astype(F32)
            sgt = _sigmoid(cg_ref[pl.ds(t0, CT), :].astype(F32))
            dca_ref[pl.ds(t0, CT), :] = (acc * sgt).astype(BF16)
            dcg_ref[pl.ds(t0, CT), :] = (acc * a * sgt * (1.0 - sgt)).astype(BF16)
            return c

        lax.fori_loop(0, S // CT, p2, 0)

        dwacc[...] = jnp.zeros_like(dwacc)

        def p3(i, c):
            t0 = pl.multiple_of(i * CT, CT)
            win = glu[pl.ds(t0, 2 * CT), :]
            dy = dyp[pl.ds(t0, CT), :]
            for k in range(CK):
                dwacc[k] += sum8(dy * win[k + 2:k + 2 + CT, :])
            return c

        lax.fori_loop(0, S // CT, p3, 0)
        dw_ref[...] = jnp.sum(dwacc[...], axis=1)[None]

    vec = pl.BlockSpec((1, DC), lambda b: (0, 0))
    seq = pl.BlockSpec((S, DC), lambda b: (b, 0))
    return pl.pallas_call(
        body, grid=(B,),
        in_specs=[pl.BlockSpec((S, DC), lambda b: (b, 3)), pl.BlockSpec((S, DC), lambda b: (b, 4)),
                  seq, seq, pl.BlockSpec((CT, DC), lambda b: (0, 0)), vec, vec],
        out_specs=[seq, seq, pl.BlockSpec((1, CT, DC), lambda b: (b, 0, 0)),
                   pl.BlockSpec((1, 8, DC), lambda b: (b, 0, 0))],
        out_shape=[jax.ShapeDtypeStruct((T, DC), BF16)] * 2
                  + [jax.ShapeDtypeStruct((B, CT, DC), F32), jax.ShapeDtypeStruct((B, 8, DC), F32)],
        scratch_shapes=[pltpu.VMEM((CPAD + S, DC), F32), pltpu.VMEM((S + CPAD, DC), F32),
                        pltpu.VMEM((CT, 8, DC), F32)],
        compiler_params=_cp(("parallel",)), name="conv_bwd")(u, u, y, dconv, cw, lg, lb)


def _local_step(x, target, norms, W, B, S, comm=None):
    qg2 = jnp.concatenate([norms["q_norm"], norms["q_norm"]], axis=1)
    kg2 = jnp.concatenate([norms["k_norm"], norms["k_norm"]], axis=1)
    cw = jnp.concatenate([W["conv_w"], jnp.zeros((1, DC), F32)], axis=0)

    W = dict(W)
    (n1, g1, u1, act1), got = _ffn_gate_up(x, norms["ffn1_norm"], W["wg1"], W["wu1"], "ffn1_gate_up",
                                           carry=comm.gathers["down_in"] if comm else None)
    if comm:
        W.update(comm.gathered("down_in", got))
    h1, u, n2 = _ffn_down_mix_in(x, act1, W["wd1"], norms["mix_norm"], W["win"], "ffn1_down_mix_in")
    (attn, lse), got = _attn_fwd(u, qg2, kg2, B, S, carry=comm.gathers["ffn2"] if comm else None)
    if comm:
        W = dict(W, **comm.gathered("ffn2", got))
    conv, y = _conv_fwd(u, cw, norms["conv_b"], norms["conv_ln_g"], norms["conv_ln_b"], B, S)
    h2, n3, g2, u2, dout, dyb, sq, act2 = _mix_out_ffn_loss(h1, attn, conv, W["wout"], norms["ffn2_norm"],
                                                            W["wg2"], W["wu2"], W["wd2"], target, "ffn2_fwd")
    loss = (0.5 / D) * jnp.sum(sq)

    (dg2, du2, dh2, dgn_ffn2), _ = _ffn_bwd_act(dyb, g2, u2, h2, dout, norms["ffn2_norm"],
                                               W["wg2"], W["wu2"], W["wd2"], "ffn2_bwd_act")
    dwd2, _ = _ffn_bwd_w(act2, dyb, "ffn2_bwd_wd")
    dwg2, _ = _ffn_bwd_w(dg2, n3, "ffn2_bwd_wg")
    dwu2, _ = _ffn_bwd_w(du2, n3, "ffn2_bwd_wu")
    dattn, dconv, dwout = _mix_out_bwd(dh2, attn, conv, W["wout"])
    carry = comm.reduce_start({"wg2": dwg2, "wu2": dwu2, "wd2": dwd2, "wout": dwout}) if comm else None
    (dq, dk, dv, dgn_qk), got = _attn_bwd(u, attn, dattn, lse, qg2, kg2, B, S, carry=carry)
    if comm:
        comm.reduce_done(carry, got)
    dca, dcg, dcw, dcs = _conv_bwd(u, y, dconv, cw, norms["conv_ln_g"], norms["conv_ln_b"], B, S)
    dwin, dh1, dyb1, dgn_mix = _mix_in_bwd((dq, dk, dv, dca, dcg), W["win"], n2, h1, dh2, norms["mix_norm"])
    (dg1, du1, gx, dgn_ffn1), _ = _ffn_bwd_act(dyb1, g1, u1, x, dh1, norms["ffn1_norm"],
                                              W["wg1"], W["wu1"], W["wd1"], "ffn1_bwd_act")
    carry = comm.reduce_start({"win": dwin}) if comm else None
    dwd1, got = _ffn_bwd_w(act1, dyb1, "ffn1_bwd_wd", carry=carry)
    if comm:
        comm.reduce_done(carry, got)
        carry = comm.reduce_start({"wd1": dwd1})
    dwg1, got = _ffn_bwd_w(dg1, n1, "ffn1_bwd_wg", carry=carry)
    if comm:
        comm.reduce_done(carry, got)
        carry = comm.reduce_start({"wg1": dwg1})
    dwu1, got = _ffn_bwd_w(du1, n1, "ffn1_bwd_wu", carry=carry)
    if comm:
        comm.reduce_done(carry, got)
        comm.last = comm.reduce_start({"wu1": dwu1})

    qk = jnp.sum(dgn_qk, axis=0)
    cs = jnp.sum(dcs, axis=0)
    small = {
        "ffn1_norm": jnp.sum(dgn_ffn1, axis=0),
        "mix_norm": jnp.sum(dgn_mix, axis=0),
        "q_norm": qk[0:1, 0:HD] + qk[0:1, HD:2 * HD],
        "k_norm": qk[1:2, 0:HD] + qk[1:2, HD:2 * HD],
        "conv_w": jnp.sum(dcw, axis=0)[0:CK],
        "conv_b": cs[0:1],
        "conv_ln_g": cs[1:2],
        "conv_ln_b": cs[2:3],
        "ffn2_norm": jnp.sum(dgn_ffn2, axis=0),
    }
    big = {"wg1": dwg1, "wu1": dwu1, "wd1": dwd1, "win": dwin, "wout": dwout,
           "wg2": dwg2, "wu2": dwu2, "wd2": dwd2}
    return loss, gx, big, small


HBM = pl.BlockSpec(memory_space=pltpu.HBM)
VMEM = pl.BlockSpec(memory_space=pltpu.VMEM)


def _place():
    return lax.axis_index("x"), lax.axis_index("y"), lax.axis_index("c")


class _GatherCarry:
    def __init__(self, shards, mid_at=0.5):
        nt = len(shards)
        self.mid_at = mid_at
        self.shards = shards
        self.in_arrays = [s for s, _ in shards]
        self.in_specs = [VMEM] * nt
        self.out_shape = [jax.ShapeDtypeStruct((NDEV * s.shape[0], s.shape[1]), dt) for s, dt in shards]
        self.out_specs = [HBM] * nt
        self.scratch = ([pltpu.VMEM(s.shape, dt) for s, dt in shards]
                        + [pltpu.SemaphoreType.DMA((nt, 7)), pltpu.SemaphoreType.DMA((nt, 7)),
                           pltpu.SemaphoreType.DMA((nt,))])

    def _copies(self, outs, scr):
        nt = len(self.shards)
        stages = scr[:nt]
        send_sems, recv_sems, local_sems = scr[nt:]
        x, y, c = _place()
        me, sibling = (x, y, c), (x, y, 1 - c)
        xn, yn, diag = (1 - x, y, c), (x, 1 - y, c), (1 - x, 1 - y, c)
        via = (x ^ c, y ^ (1 - c), c)
        onto = (x ^ (1 - c), y ^ c, c)

        def rows(t, px, py, pc):
            r = self.shards[t][0].shape[0]
            return outs[t].at[pl.ds((4 * px + 2 * py + pc) * r, r), :]

        def copy(t, k, block, to, src=None):
            return pltpu.make_async_remote_copy(
                src_ref=rows(t, *block) if src is None else src, dst_ref=rows(t, *block),
                send_sem=send_sems.at[t, k], recv_sem=recv_sems.at[t, k],
                device_id=to, device_id_type=MESH)

        sib = lambda b: (b[0], b[1], 1 - c)
        return dict(
            local=[pltpu.make_async_copy(stages[t], rows(t, *me), local_sems.at[t]) for t in range(nt)],
            own=[[copy(t, 0, me, sibling, src=stages[t]), copy(t, 1, me, xn, src=stages[t]),
                  copy(t, 2, me, yn, src=stages[t])] for t in range(nt)],
            relay=[copy(t, 3, via, onto) for t in range(nt)],
            down=[[copy(t, 4, xn, sibling), copy(t, 5, yn, sibling)] for t in range(nt)],
            down_diag=[copy(t, 6, diag, sibling) for t in range(nt)],
            got_xy=[[copy(t, 1, xn, me), copy(t, 2, yn, me)] for t in range(nt)],
            got_diag=[copy(t, 3, diag, me) for t in range(nt)],
            got_sib=[[copy(t, 0, sibling, me), copy(t, 4, sib(xn), me), copy(t, 5, sib(yn), me),
                      copy(t, 6, sib(diag), me)] for t in range(nt)])

    def start(self, ins, outs, scr):
        cps = self._copies(outs, scr)
        for t, (_, dt) in enumerate(self.shards):
            scr[t][...] = ins[t][...].astype(dt)
            for cp in [cps["local"][t]] + cps["own"][t]:
                cp.start()

    def stages(self):
        sizes = [s.size * jnp.dtype(dt).itemsize for s, dt in self.shards]
        done = [sum(sizes[:t + 1]) / sum(sizes) for t in range(len(sizes))]
        return [(self.mid_at * f, functools.partial(self.mid, t)) for t, f in enumerate(done)]

    def mid(self, t, ins, outs, scr):
        cps = self._copies(outs, scr)
        for cp in cps["got_xy"][t]:
            cp.wait_recv()
        for cp in [cps["relay"][t]] + cps["down"][t]:
            cp.start()

    def finish(self, ins, outs, scr):
        cps = self._copies(outs, scr)
        for t in range(len(self.shards)):
            cps["got_diag"][t].wait_recv()
            cps["down_diag"][t].start()
        for t in range(len(self.shards)):
            for cp in cps["got_sib"][t]:
                cp.wait_recv()
            for cp in cps["own"][t] + [cps["relay"][t]] + cps["down"][t] + [cps["down_diag"][t]]:
                cp.wait_send()
            cps["local"][t].wait()


def _run_carry(carry, name):
    def body(*refs):
        n_in, n_out = len(carry.in_arrays), len(carry.out_shape)
        ins, outs, scr = refs[:n_in], refs[n_in:n_in + n_out], refs[n_in + n_out:]
        carry.start(ins, outs, scr)
        for _, stage in carry.stages():
            stage(ins, outs, scr)
        carry.finish(ins, outs, scr)

    return pl.pallas_call(
        body, in_specs=carry.in_specs, out_specs=carry.out_specs, out_shape=carry.out_shape,
        scratch_shapes=carry.scratch, compiler_params=pltpu.CompilerParams(vmem_limit_bytes=VMEM_LIMIT),
        name=name)(*carry.in_arrays)


class _ExchangeCarry:
    def __init__(self, names, grads, mid_at=0.5):
        nt = len(grads)
        self.mid_at = mid_at
        self.names = names
        self.in_arrays = [g.reshape(4, 2, g.shape[0] // NDEV, g.shape[1]) for g in grads]
        self.in_specs = [HBM] * nt
        blocks = [g.shape[2:] for g in self.in_arrays]
        self.out_shape = [jax.ShapeDtypeStruct((3,) + b, BF16) for b in blocks]
        self.out_specs = [HBM] * nt
        self.scratch = ([pltpu.VMEM((4,) + b, BF16) for b in blocks] * 2 + [pltpu.VMEM(b, BF16) for b in blocks]
                        + [pltpu.SemaphoreType.DMA((nt, 3)), pltpu.SemaphoreType.DMA((nt, 3))]
                        + [pltpu.SemaphoreType.DMA((nt,))] * 4)

    def _copies(self, ins, outs, scr):
        nt = len(ins)
        theirs, own, relayed = scr[:nt], scr[nt:2 * nt], scr[2 * nt:3 * nt]
        send_sems, recv_sems, keep_sems, load_sems, swap_send, swap_recv = scr[3 * nt:]
        x, y, c = _place()
        q = lambda cx, cy: 2 * cx + cy
        near, far = (x ^ c, y ^ (1 - c)), (x ^ (1 - c), y ^ c)

        def remote(t, k, src, dst, chip):
            return pltpu.make_async_remote_copy(
                src_ref=src, dst_ref=dst, send_sem=send_sems.at[t, k], recv_sem=recv_sems.at[t, k],
                device_id=(*chip, c), device_id_type=MESH)

        return dict(
            swap=[pltpu.make_async_remote_copy(
                src_ref=ins[t].at[:, 1 - c], dst_ref=theirs[t], send_sem=swap_send.at[t], recv_sem=swap_recv.at[t],
                device_id=(x, y, 1 - c), device_id_type=MESH) for t in range(nt)],
            load=[pltpu.make_async_copy(ins[t].at[:, c], own[t], load_sems.at[t]) for t in range(nt)],
            keep=[pltpu.make_async_copy(own[t].at[q(x, y)], outs[t].at[0], keep_sems.at[t]) for t in range(nt)],
            direct=[remote(t, 0, own[t].at[q(*near)], outs[t].at[1], near) for t in range(nt)],
            relay=[remote(t, 1, own[t].at[q(1 - x, 1 - y)], relayed[t], near) for t in range(nt)],
            merged=[remote(t, 2, own[t].at[q(*far)], outs[t].at[2], far) for t in range(nt)],
            theirs=theirs, own=own, relayed=relayed, far=q(*far))

    EARLY_AT = 0.1

    def stages(self):
        return [(self.EARLY_AT, self.early), (self.mid_at, self.mid)]

    def start(self, ins, outs, scr):
        cps = self._copies(ins, outs, scr)
        for t in range(len(ins)):
            cps["swap"][t].start()
            cps["load"][t].start()

    def early(self, ins, outs, scr):
        cps = self._copies(ins, outs, scr)
        for t in range(len(ins)):
            cps["load"][t].wait()
            cps["swap"][t].wait_recv()
            own, theirs = cps["own"][t], cps["theirs"][t]
            for j in range(4):
                own[j] = (own[j].astype(F32) + theirs[j].astype(F32)).astype(BF16)
            for kind in ("relay", "direct", "keep"):
                cps[kind][t].start()

    def mid(self, ins, outs, scr):
        cps = self._copies(ins, outs, scr)
        for t in range(len(ins)):
            cps["relay"][t].wait_recv()
            own, far = cps["own"][t], cps["far"]
            own[far] = (own[far].astype(F32) + cps["relayed"][t][...].astype(F32)).astype(BF16)
            cps["merged"][t].start()

    def finish(self, ins, outs, scr):
        cps = self._copies(ins, outs, scr)
        for t in range(len(ins)):
            cps["swap"][t].wait_send()
            cps["direct"][t].wait()
            cps["relay"][t].wait_send()
            cps["merged"][t].wait()
            cps["keep"][t].wait()


class _Comm:
    def __init__(self, groups):
        self.names = {tag: list(g) for tag, (g, _) in groups.items()}
        self.gathers = {tag: _GatherCarry(list(g.values()), mid_at) for tag, (g, mid_at) in groups.items()}
        self.reduced = {}
        self.last = None

    def gathered(self, tag, outs):
        return dict(zip(self.names[tag], outs))

    def reduce_start(self, grads):
        names = list(grads)
        return _ExchangeCarry(names, [grads[n] for n in names])

    def reduce_done(self, carry, outs):
        self.reduced.update(zip(carry.names, outs))


def _adamw_math(w, g, m, v):
    m = B1 * m + (1.0 - B1) * g
    v = B2 * v + (1.0 - B2) * (g * g)
    m_hat = m / (1.0 - B1 ** STEP)
    v_hat = v / (1.0 - B2 ** STEP)
    delta = -LR * (m_hat / (jnp.sqrt(v_hat) + AEPS) + WD * w)
    return delta, m, v


def _adamw_big(recvs, ws, ms, vs, name):
    nw = len(ws)

    def body(*refs):
        ins, outs = refs[:4 * nw], refs[4 * nw:]
        for k in range(nw):
            @pl.when(pl.program_id(0) // 2 == k)
            def _(k=k):
                r_ref, w_ref, m_ref, v_ref = ins[4 * k:4 * k + 4]
                g_ref, d_ref, mo_ref, vo_ref = outs[4 * k:4 * k + 4]
                g = r_ref[0].astype(F32)
                for q in range(1, 3):
                    g = g + r_ref[q].astype(F32)
                d, mn, vn = _adamw_math(w_ref[...], g, m_ref[...], v_ref[...])
                g_ref[...] = g
                d_ref[...] = d
                mo_ref[...] = mn
                vo_ref[...] = vn

    in_specs, out_specs, out_shape, args = [], [], [], []
    for k, (r, w, m, v) in enumerate(zip(recvs, ws, ms, vs)):
        rows, n = w.shape
        tr = rows // 2
        tile = lambda s, k=k: jnp.clip(s - 2 * k, 0, 1)
        row = pl.BlockSpec((tr, n), lambda s, tile=tile: (tile(s), 0))
        in_specs += [pl.BlockSpec((3, tr, n), lambda s, tile=tile: (0, tile(s), 0)), row, row, row]
        out_specs += [row] * 4
        out_shape += [jax.ShapeDtypeStruct(w.shape, F32)] * 4
        args += [r, w, m, v]
    res = pl.pallas_call(
        body, grid=(2 * nw,), in_specs=in_specs, out_specs=out_specs, out_shape=out_shape,
        compiler_params=_cp(("arbitrary",)), name=name)(*args)
    return [res[4 * k:4 * k + 4] for k in range(nw)]


SMALL_NAMES = ("ffn1_norm", "mix_norm", "ffn2_norm", "conv_b", "conv_ln_g", "conv_ln_b", "q_norm", "k_norm")
SROWS = 16
LOSS_ROW = len(SMALL_NAMES)
CWF = 2


def _small_sums(gs, loss_row, gcw, carry=None):
    ns = len(SMALL_NAMES)
    widths = [g.shape[1] for g in gs]

    def body(*refs):
        it = iter(refs)
        take = lambda n: [next(it) for _ in range(n)]
        g_refs, (loss_ref, gcw_ref) = take(ns), take(2)
        cins = take(len(carry.in_arrays)) if carry else []
        tot_ref, ctot_ref = take(2)
        couts = take(len(carry.out_shape)) if carry else []
        send, slots, cslots, send_sems, recv_sems, csend_sems, crecv_sems = take(7)
        cscr = list(it)
        x, y, c = _place()
        me = 4 * x + 2 * y + c
        send[...] = jnp.zeros_like(send)
        for k in range(ns):
            send[k:k + 1, 0:widths[k]] = g_refs[k][...]
        send[LOSS_ROW:LOSS_ROW + 1, 0:128] = loss_ref[...]
        slots[me] = send[...]
        cslots[me] = gcw_ref[...]
        cps = []
        for k in range(1, NDEV):
            peer = (x ^ ((k >> 2) & 1), y ^ ((k >> 1) & 1), c ^ (k & 1))
            cps.append(pltpu.make_async_remote_copy(
                src_ref=send, dst_ref=slots.at[me], send_sem=send_sems.at[k - 1], recv_sem=recv_sems.at[k - 1],
                device_id=peer, device_id_type=MESH))
            cps.append(pltpu.make_async_remote_copy(
                src_ref=gcw_ref, dst_ref=cslots.at[me], send_sem=csend_sems.at[k - 1],
                recv_sem=crecv_sems.at[k - 1], device_id=peer, device_id_type=MESH))
        for cp in cps:
            cp.start()
        stages = [stage for _, stage in carry.stages()] if carry else []
        if carry:
            carry.start(cins, couts, cscr)
        for stage in stages[:1]:
            stage(cins, couts, cscr)
        for cp in cps:
            cp.wait()
        tot = slots[0]
        ctot = cslots[0, me]
        for j in range(1, NDEV):
            tot = tot + slots[j]
            ctot = ctot + cslots[j, me]
        tot_ref[...] = tot
        ctot_ref[...] = ctot
        for stage in stages[1:]:
            stage(cins, couts, cscr)
        if carry:
            carry.finish(cins, couts, cscr)

    args = [*gs, loss_row, gcw]
    out_shape = [jax.ShapeDtypeStruct((SROWS, D), F32), jax.ShapeDtypeStruct((CWF, D), F32)]
    res = pl.pallas_call(
        body, in_specs=[VMEM] * len(args) + (carry.in_specs if carry else []),
        out_specs=[VMEM] * 2 + (carry.out_specs if carry else []),
        out_shape=out_shape + (carry.out_shape if carry else []),
        scratch_shapes=[pltpu.VMEM((SROWS, D), F32), pltpu.VMEM((NDEV, SROWS, D), F32),
                        pltpu.VMEM((NDEV, NDEV, CWF, D), F32)]
                       + [pltpu.SemaphoreType.DMA((NDEV - 1,))] * 4 + (carry.scratch if carry else []),
        name="small_sums")(*args, *(carry.in_arrays if carry else []))
    return res[0], res[1], res[2:]


def _adamw_small(tot, ctot, ws, ms, vs, wcw, mcw, vcw):
    ns = len(SMALL_NAMES)
    widths = [w.shape[1] for w in ws]

    def body(*refs):
        it = iter(refs)
        take = lambda n: [next(it) for _ in range(n)]
        (tot_ref, ctot_ref), w_refs, m_refs, v_refs = take(2), take(ns), take(ns), take(ns)
        wcw_ref, mcw_ref, vcw_ref = take(3)
        outs = [take(4) for _ in range(ns)]
        cw_outs, (loss_out,) = take(4), take(1)

        def step(g, w_ref, m_ref, v_ref, o):
            d, mn, vn = _adamw_math(w_ref[...], g, m_ref[...], v_ref[...])
            o[0][...], o[1][...], o[2][...], o[3][...] = g, d, mn, vn

        for k in range(ns):
            step(tot_ref[k:k + 1, 0:widths[k]], w_refs[k], m_refs[k], v_refs[k], outs[k])
        step(ctot_ref[...], wcw_ref, mcw_ref, vcw_ref, cw_outs)
        loss_out[...] = tot_ref[LOSS_ROW:LOSS_ROW + 1, 0:128]

    args = [tot, ctot, *ws, *ms, *vs, wcw, mcw, vcw]
    out_shape = ([jax.ShapeDtypeStruct((1, n), F32) for n in widths for _ in range(4)]
                 + [jax.ShapeDtypeStruct((CWF, D), F32)] * 4 + [jax.ShapeDtypeStruct((1, 128), F32)])
    res = pl.pallas_call(
        body, in_specs=[VMEM] * len(args), out_specs=[VMEM] * len(out_shape), out_shape=out_shape,
        name="adamw_small")(*args)
    per = [res[4 * k:4 * k + 4] for k in range(ns)]
    return per, res[4 * ns:4 * ns + 4], res[-1]


def _pack_cw(a):
    flat = a.reshape(a.shape[:-2] + (CK * HD,))
    pad = [(0, 0)] * (flat.ndim - 1) + [(0, CWF * D - CK * HD)]
    return jnp.pad(flat, pad).reshape(a.shape[:-2] + (CWF, D))


def _unpack_cw(v):
    return v.reshape(-1)[:CK * HD].reshape(1, CK, HD)


def kernel(x, ffn1_norm, ffn1_w_gate, ffn1_w_up, ffn1_w_down, mix_norm, w_in, q_norm, k_norm, conv_w, conv_b, conv_ln_g, conv_ln_b, w_out, ffn2_norm, ffn2_w_gate, ffn2_w_up, ffn2_w_down, loss_target, m_ffn1_norm, m_ffn1_w_gate, m_ffn1_w_up, m_ffn1_w_down, m_mix_norm, m_w_in, m_q_norm, m_k_norm, m_conv_w, m_conv_b, m_conv_ln_g, m_conv_ln_b, m_w_out, m_ffn2_norm, m_ffn2_w_gate, m_ffn2_w_up, m_ffn2_w_down, v_ffn1_norm, v_ffn1_w_gate, v_ffn1_w_up, v_ffn1_w_down, v_mix_norm, v_w_in, v_q_norm, v_k_norm, v_conv_w, v_conv_b, v_conv_ln_g, v_conv_ln_b, v_w_out, v_ffn2_norm, v_ffn2_w_gate, v_ffn2_w_up, v_ffn2_w_down):
    P = dict(ffn1_norm=ffn1_norm, ffn1_w_gate=ffn1_w_gate, ffn1_w_up=ffn1_w_up, ffn1_w_down=ffn1_w_down,
             mix_norm=mix_norm, w_in=w_in, q_norm=q_norm, k_norm=k_norm, conv_w=conv_w, conv_b=conv_b,
             conv_ln_g=conv_ln_g, conv_ln_b=conv_ln_b, w_out=w_out, ffn2_norm=ffn2_norm,
             ffn2_w_gate=ffn2_w_gate, ffn2_w_up=ffn2_w_up, ffn2_w_down=ffn2_w_down)
    M = dict(ffn1_norm=m_ffn1_norm, ffn1_w_gate=m_ffn1_w_gate, ffn1_w_up=m_ffn1_w_up, ffn1_w_down=m_ffn1_w_down,
             mix_norm=m_mix_norm, w_in=m_w_in, q_norm=m_q_norm, k_norm=m_k_norm, conv_w=m_conv_w, conv_b=m_conv_b,
             conv_ln_g=m_conv_ln_g, conv_ln_b=m_conv_ln_b, w_out=m_w_out, ffn2_norm=m_ffn2_norm,
             ffn2_w_gate=m_ffn2_w_gate, ffn2_w_up=m_ffn2_w_up, ffn2_w_down=m_ffn2_w_down)
    V = dict(ffn1_norm=v_ffn1_norm, ffn1_w_gate=v_ffn1_w_gate, ffn1_w_up=v_ffn1_w_up, ffn1_w_down=v_ffn1_w_down,
             mix_norm=v_mix_norm, w_in=v_w_in, q_norm=v_q_norm, k_norm=v_k_norm, conv_w=v_conv_w, conv_b=v_conv_b,
             conv_ln_g=v_conv_ln_g, conv_ln_b=v_conv_ln_b, w_out=v_w_out, ffn2_norm=v_ffn2_norm,
             ffn2_w_gate=v_ffn2_w_gate, ffn2_w_up=v_ffn2_w_up, ffn2_w_down=v_ffn2_w_down)
    order = ["ffn1_norm", "ffn1_w_gate", "ffn1_w_up", "ffn1_w_down", "mix_norm", "w_in", "q_norm", "k_norm",
             "conv_w", "conv_b", "conv_ln_g", "conv_ln_b", "w_out", "ffn2_norm", "ffn2_w_gate", "ffn2_w_up",
             "ffn2_w_down"]
    B, S, _ = x.shape
    T = B * S

    bigs = [("wg1", "ffn1_w_gate", True), ("wu1", "ffn1_w_up", True), ("wd1", "ffn1_w_down", False),
            ("win", "w_in", True), ("wout", "w_out", False),
            ("wg2", "ffn2_w_gate", True), ("wu2", "ffn2_w_up", True), ("wd2", "ffn2_w_down", False)]
    hm = lambda a, tr: jnp.transpose(a[0]) if tr else a[0]
    cw_pad = jnp.zeros((32, 128), F32).at[0:CK, 0:HD].set(conv_w[0])
    shard = {ln: (hm(P[pn], tr), BF16) for ln, pn, tr in bigs}
    gathered = _run_carry(_GatherCarry([shard["wg1"], shard["wu1"], (cw_pad, F32)]), "gather_first")
    W = {"wg1": gathered[0], "wu1": gathered[1]}
    cwg = gathered[2].reshape(NDEV, 32, 128)[:, 0:CK, 0:HD]
    W["conv_w"] = jnp.transpose(cwg, (1, 0, 2)).reshape(CK, DC)
    norms = {n: P[n] for n in SMALL_NAMES}
    comm = _Comm({"down_in": ({n: shard[n] for n in ("wd1", "win")}, 0.5),
                  "ffn2": ({n: shard[n] for n in ("wg2", "wu2", "wd2", "wout")}, 0.5)})

    loss_part, gx, _, small = _local_step(x.reshape(T, D), loss_target.reshape(T, D), norms, W, B, S, comm)

    G, Dl, Mn, Vn = {}, {}, {}, {}
    dcw = small["conv_w"].reshape(CK, NDEV, HD).transpose(1, 0, 2)
    loss_row = jnp.zeros((1, 128), F32).at[0, 0].set(loss_part)
    tot, ctot, got = _small_sums([small[n] for n in SMALL_NAMES], loss_row, _pack_cw(dcw), carry=comm.last)
    comm.reduce_done(comm.last, got)
    per, cw_outs, loss_out = _adamw_small(
        tot, ctot, [P[n] for n in SMALL_NAMES], [M[n] for n in SMALL_NAMES], [V[n] for n in SMALL_NAMES],
        _pack_cw(P["conv_w"][0]), _pack_cw(M["conv_w"][0]), _pack_cw(V["conv_w"][0]))
    loss = loss_out[0, 0]
    for n, outs in zip(SMALL_NAMES, per):
        G[n], Dl[n], Mn[n], Vn[n] = outs
    G["conv_w"], Dl["conv_w"], Mn["conv_w"], Vn["conv_w"] = [_unpack_cw(o) for o in cw_outs]

    by_name = {ln: (pn, tr) for ln, pn, tr in bigs}
    for group in (("wg2", "wu2", "wd2"), ("wout", "win", "wd1"), ("wg1", "wu1")):
        pts = [by_name[ln] for ln in group]
        res = _adamw_big([comm.reduced[ln] for ln in group], [hm(P[pn], tr) for pn, tr in pts],
                         [hm(M[pn], tr) for pn, tr in pts], [hm(V[pn], tr) for pn, tr in pts],
                         "adamw_" + "_".join(group))
        for (pn, tr), outs in zip(pts, res):
            G[pn], Dl[pn], Mn[pn], Vn[pn] = [(jnp.transpose(o) if tr else o)[None] for o in outs]

    return (loss, gx.reshape(B, S, D), *[G[n] for n in order], *[Dl[n] for n in order],
            *[Mn[n] for n in order], *[Vn[n] for n in order])
```

```python
import functools

import jax
import jax.numpy as jnp
from jax import lax
from jax.experimental import pallas as pl
from jax.experimental.pallas import tpu as pltpu

F32 = jnp.float32
BF16 = jnp.bfloat16

D = 1024
FF = 2816
HD = 64
DA = 512
DC = 512
DIN = 2560
CK = 31
BLK = 128
DILS = (1, 4, 16)
EPS = 1e-6
NDEV = 8
MESH = pl.DeviceIdType.MESH

LR, B1, B2, AEPS, WD, STEP = 0.001, 0.9, 0.999, 1e-08, 0.01, 10

NT = (((1,), (1,)), ((), ()))
TN = (((0,), (0,)), ((), ()))

VMEM_LIMIT = 60 * 1024 * 1024


def _cp(sem=None):
    return pltpu.CompilerParams(dimension_semantics=sem, vmem_limit_bytes=VMEM_LIMIT)


def _sigmoid(x):
    return 0.5 * (jnp.tanh(0.5 * x) + 1.0)


def _pallas(body, args, *, grid, in_specs, out_specs, out_shape, scratch_shapes, sem, name, carry=None):
    if carry is None:
        outs = pl.pallas_call(body, grid=grid, in_specs=in_specs, out_specs=out_specs, out_shape=out_shape,
                              scratch_shapes=scratch_shapes, compiler_params=_cp(sem), name=name)(*args)
        return outs, None
    n_in, n_out, n_scr = len(in_specs), len(out_shape), len(scratch_shapes)
    c_in, c_out = len(carry.in_arrays), len(carry.out_shape)

    def wrapped(*refs):
        ins, refs = refs[:n_in], refs[n_in:]
        cins, refs = refs[:c_in], refs[c_in:]
        outs, refs = refs[:n_out], refs[n_out:]
        couts, refs = refs[:c_out], refs[c_out:]
        scr, cscr = refs[:n_scr], refs[n_scr:]
        ids = [pl.program_id(a) for a in range(len(grid))]
        step = ids[0]
        for i, n in zip(ids[1:], grid[1:]):
            step = step * n + i
        steps = functools.reduce(lambda a, b: a * b, grid)

        @pl.when(step == 0)
        def _():
            carry.start(cins, couts, cscr)

        body(*ins, *outs, *scr)

        for frac, stage in carry.stages():
            @pl.when(step == int(steps * frac))
            def _(stage=stage):
                stage(cins, couts, cscr)

        @pl.when(step == steps - 1)
        def _():
            carry.finish(cins, couts, cscr)

    outs = pl.pallas_call(
        wrapped, grid=grid, in_specs=list(in_specs) + carry.in_specs, out_specs=list(out_specs) + carry.out_specs,
        out_shape=list(out_shape) + carry.out_shape, scratch_shapes=list(scratch_shapes) + carry.scratch,
        compiler_params=_cp(("arbitrary",) * len(grid)), name=name)(*args, *carry.in_arrays)
    return outs[:n_out], outs[n_out:]


FC = 256


def _resident(shape):
    return pl.BlockSpec(shape, lambda *_: (0,) * len(shape), pipeline_mode=pl.Buffered(1))


def _mix_out_ffn_loss(h1, attn, conv, wout, gain, wg, wu, wd, target, name):
    T = h1.shape[0]
    tm = 512
    nt = T // tm

    def body(h1_ref, at_ref, cv_ref, wo_ref, gain_ref, wg_ref, wu_ref, wd_ref, t_ref,
             h2_ref, n_ref, g_ref, u_ref, dout_ref, dyb_ref, sq_ref, a_hbm, a_scr, a_sem):
        t = pl.program_id(0)
        a_out = lambda i: pltpu.make_async_copy(a_scr, a_hbm.at[pl.ds(pl.multiple_of(i * tm, tm), tm), :], a_sem)

        @pl.when(t > 0)
        def _():
            a_out(t - 1).wait()

        xv = (h1_ref[...]
              + jnp.dot(at_ref[...], wo_ref[0:DA, :], preferred_element_type=F32)
              + jnp.dot(cv_ref[...], wo_ref[DA:D, :], preferred_element_type=F32))
        h2_ref[...] = xv
        r = lax.rsqrt(jnp.mean(xv * xv, axis=-1, keepdims=True) + EPS)
        n_ref[...] = (xv * r * gain_ref[...]).astype(BF16)
        for c in range(FF // FC):
            cols = slice(c * FC, (c + 1) * FC)
            nb = n_ref[...]
            g = lax.dot_general(nb, wg_ref[cols, :], NT, preferred_element_type=F32)
            u = lax.dot_general(nb, wu_ref[cols, :], NT, preferred_element_type=F32)
            g_ref[:, cols] = g.astype(BF16)
            u_ref[:, cols] = u.astype(BF16)
            a_scr[:, cols] = (g * _sigmoid(g) * u).astype(BF16)
        a_out(t).start()
        e = h2_ref[...] + 0.5 * jnp.dot(a_scr[...], wd_ref[...], preferred_element_type=F32) - t_ref[...]
        dout = e * (1.0 / D)
        dout_ref[...] = dout
        dyb_ref[...] = (0.5 * dout).astype(BF16)
        sq_ref[...] = jnp.sum(e * e, axis=0, keepdims=True)[None]

        @pl.when(t == nt - 1)
        def _():
            a_out(t).wait()

    row = pl.BlockSpec((tm, D), lambda t: (t, 0))
    half = pl.BlockSpec((tm, DA), lambda t: (t, 0))
    wide = pl.BlockSpec((tm, FF), lambda t: (t, 0))
    outs, _ = _pallas(
        body, (h1, attn, conv, wout, gain, wg, wu, wd, target), grid=(nt,),
        in_specs=[row, half, half, _resident((D, D)), _resident((1, D)), _resident((FF, D)), _resident((FF, D)),
                  _resident((FF, D)), row],
        out_specs=[row, row, wide, wide, row, row, pl.BlockSpec((1, 1, D), lambda t: (t, 0, 0)), HBM],
        out_shape=[jax.ShapeDtypeStruct((T, D), F32), jax.ShapeDtypeStruct((T, D), BF16)]
                  + [jax.ShapeDtypeStruct((T, FF), BF16)] * 2
                  + [jax.ShapeDtypeStruct((T, D), F32), jax.ShapeDtypeStruct((T, D), BF16),
                     jax.ShapeDtypeStruct((nt, 1, D), F32), jax.ShapeDtypeStruct((T, FF), BF16)],
        scratch_shapes=[pltpu.VMEM((tm, FF), BF16), pltpu.SemaphoreType.DMA(())],
        sem=("arbitrary",), name=name)
    return outs


def _ffn_gate_up(x, gain, wg, wu, name, carry=None):
    T = x.shape[0]
    tm = 512

    def body(x_ref, gain_ref, wg_ref, wu_ref, n_ref, g_ref, u_ref, a_ref):
        xv = x_ref[...]
        r = lax.rsqrt(jnp.mean(xv * xv, axis=-1, keepdims=True) + EPS)
        n_ref[...] = (xv * r * gain_ref[...]).astype(BF16)
        for c in range(FF // FC):
            cols = slice(c * FC, (c + 1) * FC)
            nb = n_ref[...]
            g = lax.dot_general(nb, wg_ref[cols, :], NT, preferred_element_type=F32)
            u = lax.dot_general(nb, wu_ref[cols, :], NT, preferred_element_type=F32)
            g_ref[:, cols] = g.astype(BF16)
            u_ref[:, cols] = u.astype(BF16)
            a_ref[:, cols] = (g * _sigmoid(g) * u).astype(BF16)

    row = pl.BlockSpec((tm, D), lambda t: (t, 0))
    wide = pl.BlockSpec((tm, FF), lambda t: (t, 0))
    return _pallas(
        body, (x, gain, wg, wu), grid=(T // tm,),
        in_specs=[row, _resident((1, D)), _resident((FF, D)), _resident((FF, D))],
        out_specs=[row, wide, wide, wide],
        out_shape=[jax.ShapeDtypeStruct((T, D), BF16)] + [jax.ShapeDtypeStruct((T, FF), BF16)] * 3,
        scratch_shapes=[], sem=("parallel",), name=name, carry=carry)


def _ffn_down_mix_in(x, a, wd, gain, win, name):
    T = x.shape[0]
    tm = 512

    def body(x_ref, a_ref, wd_ref, gain_ref, win_ref, h_ref, u_ref, n_ref):
        hv = x_ref[...] + 0.5 * jnp.dot(a_ref[...], wd_ref[...], preferred_element_type=F32)
        h_ref[...] = hv
        r = lax.rsqrt(jnp.mean(hv * hv, axis=-1, keepdims=True) + EPS)
        n_ref[...] = (hv * r * gain_ref[...]).astype(BF16)
        u_ref[...] = lax.dot_general(n_ref[...], win_ref[...], NT, preferred_element_type=F32).astype(BF16)

    row = pl.BlockSpec((tm, D), lambda t: (t, 0))
    wide = pl.BlockSpec((tm, FF), lambda t: (t, 0))
    outs, _ = _pallas(
        body, (x, a, wd, gain, win), grid=(T // tm,),
        in_specs=[row, wide, _resident((FF, D)), _resident((1, D)), _resident((DIN, D))],
        out_specs=[row, pl.BlockSpec((tm, DIN), lambda t: (t, 0)), row],
        out_shape=[jax.ShapeDtypeStruct((T, D), F32), jax.ShapeDtypeStruct((T, DIN), BF16),
                   jax.ShapeDtypeStruct((T, D), BF16)],
        scratch_shapes=[], sem=("parallel",), name=name)
    return outs


def _ffn_bwd_act(dyb, g, u, x, dout, gain, wg, wu, wd, name, carry=None):
    T = x.shape[0]
    tm = 256
    nt = T // tm

    def body(dy_ref, g_ref, u_ref, x_ref, dout_ref, gain_ref, wg_ref, wu_ref, wd_ref,
             dg_ref, du_ref, dx_ref, dgn_ref):
        for c in range(FF // FC):
            cols = slice(c * FC, (c + 1) * FC)
            da = lax.dot_general(dy_ref[...], wd_ref[cols, :], NT, preferred_element_type=F32)
            gv = g_ref[:, cols].astype(F32)
            uv = u_ref[:, cols].astype(F32)
            sg = _sigmoid(gv)
            dg_ref[:, cols] = (da * uv * (sg * (1.0 + gv * (1.0 - sg)))).astype(BF16)
            du_ref[:, cols] = (da * (gv * sg)).astype(BF16)
        dn = (jnp.dot(dg_ref[...], wg_ref[...], preferred_element_type=F32)
              + jnp.dot(du_ref[...], wu_ref[...], preferred_element_type=F32))
        dx, dgain = _rms_bwd_rows(dn, x_ref[...], gain_ref[...])
        dx_ref[...] = dout_ref[...] + dx
        dgn_ref[...] = dgain[None]

    row = pl.BlockSpec((tm, D), lambda t: (t, 0))
    wide = pl.BlockSpec((tm, FF), lambda t: (t, 0))
    return _pallas(
        body, (dyb, g, u, x, dout, gain, wg, wu, wd), grid=(nt,),
        in_specs=[row, wide, wide, row, row, _resident((1, D)), _resident((FF, D)), _resident((FF, D)),
                  _resident((FF, D))],
        out_specs=[wide, wide, row, pl.BlockSpec((1, 1, D), lambda t: (t, 0, 0))],
        out_shape=[jax.ShapeDtypeStruct((T, FF), BF16)] * 2
                  + [jax.ShapeDtypeStruct((T, D), F32), jax.ShapeDtypeStruct((nt, 1, D), F32)],
        scratch_shapes=[], sem=("parallel",), name=name, carry=carry)


def _ffn_bwd_w(lhs, rhs, name, carry=None):
    T = rhs.shape[0]
    tf = 256

    def body(l_ref, r_ref, dw_ref):
        dw_ref[...] = lax.dot_general(l_ref[...], r_ref[...], TN, preferred_element_type=F32).astype(BF16)

    (dw,), got = _pallas(
        body, (lhs, rhs), grid=(FF // tf,),
        in_specs=[pl.BlockSpec((T, tf), lambda f: (0, f)), _resident((T, D))],
        out_specs=[pl.BlockSpec((tf, D), lambda f: (f, 0))], out_shape=[jax.ShapeDtypeStruct((FF, D), BF16)],
        scratch_shapes=[], sem=("parallel",), name=name, carry=carry)
    return dw, got


def _rms_bwd_rows(dn, xv, gain):
    r = lax.rsqrt(jnp.mean(xv * xv, axis=-1, keepdims=True) + EPS)
    xhat = xv * r
    dxhat = dn * gain
    dx = r * (dxhat - xhat * jnp.mean(dxhat * xhat, axis=-1, keepdims=True))
    return dx, jnp.sum(dn * xhat, axis=0, keepdims=True)


def _mix_out_bwd(dh, attn, conv, wout):
    T = dh.shape[0]
    tm = 512
    nt = T // tm

    def body(dh_ref, a_ref, c_ref, w_ref, da_ref, dc_ref, dw_ref, acc_scr):
        t = pl.program_id(0)

        @pl.when(t == 0)
        def _():
            acc_scr[...] = jnp.zeros_like(acc_scr)

        dhb = dh_ref[...].astype(BF16)
        dmix = lax.dot_general(dhb, w_ref[...], NT, preferred_element_type=F32)
        da_ref[...] = dmix[:, 0:DA].astype(BF16)
        dc_ref[...] = dmix[:, DA:D].astype(BF16)
        acc_scr[0:DA, :] += lax.dot_general(a_ref[...], dhb, TN, preferred_element_type=F32)
        acc_scr[DA:D, :] += lax.dot_general(c_ref[...], dhb, TN, preferred_element_type=F32)

        @pl.when(t == nt - 1)
        def _():
            dw_ref[...] = acc_scr[...].astype(BF16)

    row = pl.BlockSpec((tm, D), lambda t: (t, 0))
    half = pl.BlockSpec((tm, DA), lambda t: (t, 0))
    full = pl.BlockSpec((D, D), lambda t: (0, 0))
    return pl.pallas_call(
        body, grid=(nt,), in_specs=[row, half, half, full], out_specs=[half, half, full],
        out_shape=[jax.ShapeDtypeStruct((T, DA), BF16)] * 2 + [jax.ShapeDtypeStruct((D, D), BF16)],
        scratch_shapes=[pltpu.VMEM((D, D), F32)],
        compiler_params=_cp(("arbitrary",)), name="mix_out_bwd")(dh, attn, conv, wout)


def _mix_in_bwd(dparts, win, nb, h, dh, gain):
    T = h.shape[0]
    tm = 512
    nt = T // tm

    def body(d0, d1, d2, d3, d4, w_ref, n_ref, h_ref, dh_ref, gain_ref,
             dw_ref, dx_ref, dyb_ref, dg_ref, acc_scr):
        t = pl.program_id(0)

        @pl.when(t == 0)
        def _():
            acc_scr[...] = jnp.zeros_like(acc_scr)

        n = n_ref[...]
        dn = jnp.zeros((tm, D), F32)
        for i, d_ref in enumerate((d0, d1, d2, d3, d4)):
            dv = d_ref[...]
            dn = dn + jnp.dot(dv, w_ref[i * DA:(i + 1) * DA, :], preferred_element_type=F32)
            acc_scr[i * DA:(i + 1) * DA, :] += lax.dot_general(dv, n, TN, preferred_element_type=F32)
        dx, dgain = _rms_bwd_rows(dn, h_ref[...], gain_ref[...])
        tot = dh_ref[...] + dx
        dx_ref[...] = tot
        dyb_ref[...] = (0.5 * tot).astype(BF16)
        dg_ref[...] = dgain[None]

        @pl.when(t == nt - 1)
        def _():
            dw_ref[...] = acc_scr[...].astype(BF16)

    row = pl.BlockSpec((tm, D), lambda t: (t, 0))
    half = pl.BlockSpec((tm, DA), lambda t: (t, 0))
    full = pl.BlockSpec((DIN, D), lambda t: (0, 0))
    return pl.pallas_call(
        body, grid=(nt,),
        in_specs=[half] * 5 + [full, row, row, row, pl.BlockSpec((1, D), lambda t: (0, 0))],
        out_specs=[full, row, row, pl.BlockSpec((1, 1, D), lambda t: (t, 0, 0))],
        out_shape=[jax.ShapeDtypeStruct((DIN, D), BF16), jax.ShapeDtypeStruct((T, D), F32),
                   jax.ShapeDtypeStruct((T, D), BF16), jax.ShapeDtypeStruct((nt, 1, D), F32)],
        scratch_shapes=[pltpu.VMEM((DIN, D), F32)],
        compiler_params=_cp(("arbitrary",)), name="mix_in_bwd")(*dparts, win, nb, h, dh, gain)


def _head_masks():
    lane = lax.broadcasted_iota(jnp.int32, (1, 2 * HD), 1)
    m0 = lane < HD
    return m0, jnp.logical_not(m0)


def _stack_heads(v, m0, m1):
    z = jnp.zeros_like(v)
    return jnp.concatenate([jnp.where(m0, v, z), jnp.where(m1, v, z)], axis=0)


def _unstack_heads(v2, m0):
    return jnp.where(m0, v2[0:BLK], v2[BLK:2 * BLK])


def _head_sums(xv):
    ri = lax.broadcasted_iota(jnp.int32, (2 * HD, 2 * HD), 0)
    ci = lax.broadcasted_iota(jnp.int32, (2 * HD, 2 * HD), 1)
    ones = jnp.where((ri < HD) == (ci < HD), 1.0, 0.0).astype(BF16)
    hi = xv.astype(BF16)
    lo = (xv - hi.astype(F32)).astype(BF16)
    return (jnp.dot(hi, ones, preferred_element_type=F32) + jnp.dot(lo, ones, preferred_element_type=F32))


def _head_rms(xv):
    return lax.rsqrt(_head_sums(xv * xv) * (1.0 / HD) + EPS)


def _band_mask(first):
    qi = lax.broadcasted_iota(jnp.int32, (BLK, 2 * BLK), 0)
    ci = lax.broadcasted_iota(jnp.int32, (BLK, 2 * BLK), 1)
    band = (ci >= qi) & (ci <= qi + BLK)
    return band & ((ci >= BLK) | jnp.logical_not(first))


def _block_rows(j, d, seg):
    r, n = j // seg, j % seg
    start = r + (d * BLK) * n
    first = n == 0
    prev = jnp.where(first, start, start - d * BLK)
    return pl.ds(start, BLK, stride=d), pl.ds(prev, BLK, stride=d), first


def _block_keys(refs, cur, prev, first, single):
    if single:
        qi = lax.broadcasted_iota(jnp.int32, (BLK, BLK), 0)
        ci = lax.broadcasted_iota(jnp.int32, (BLK, BLK), 1)
        return [r[cur, :].astype(BF16) for r in refs], ci <= qi
    return ([jnp.concatenate([r[prev, :], r[cur, :]], axis=0).astype(BF16) for r in refs], _band_mask(first))


def _attn_fwd(u, qg2, kg2, B, S, carry=None):
    T = B * S
    NB = S // BLK
    scale = HD ** -0.5

    def body(q_ref, k_ref, v_ref, qg_ref, kg_ref, o_ref, lse_ref, qn, kn, vn, os_, ls_):
        m0, m1 = _head_masks()
        qv = q_ref[...].astype(F32)
        qn[...] = qv * _head_rms(qv) * (qg_ref[...] * scale)
        kv = k_ref[...].astype(F32)
        kn[...] = kv * _head_rms(kv) * kg_ref[...]
        vn[...] = v_ref[...].astype(F32)

        for i, d in enumerate(DILS):
            seg = NB // d

            def blk(j, c, i=i, d=d, seg=seg):
                cur, prev, first = _block_rows(j, d, seg)
                q2 = _stack_heads(qn[cur, :].astype(BF16), m0, m1)
                (kk, vv), mask = _block_keys((kn, vn), cur, prev, first, False)
                s = lax.dot_general(q2, kk, NT, preferred_element_type=F32)
                s = jnp.where(jnp.concatenate([mask, mask], axis=0), s, -1e30)
                mx = jnp.max(s, axis=-1, keepdims=True)
                p = jnp.exp(s - mx)
                l = jnp.sum(p, axis=-1, keepdims=True)
                o2 = jnp.dot((p * (1.0 / l)).astype(BF16), vv, preferred_element_type=F32)
                os_[i, cur, :] = _unstack_heads(o2, m0)
                ls_[i, cur, :] = _unstack_heads(mx + jnp.log(l), m0)
                return c

            lax.fori_loop(0, NB, blk, 0, unroll=8)

        def comb(c, carry):
            rows = pl.ds(pl.multiple_of(c * 256, 256), 256)
            l0, l1, l2 = ls_[0, rows, :], ls_[1, rows, :], ls_[2, rows, :]
            mx = jnp.maximum(jnp.maximum(l0, l1), l2)
            e0, e1, e2 = jnp.exp(l0 - mx), jnp.exp(l1 - mx), jnp.exp(l2 - mx)
            tot = e0 + e1 + e2
            inv = 1.0 / tot
            o = (e0 * os_[0, rows, :] + e1 * os_[1, rows, :] + e2 * os_[2, rows, :]) * inv
            o_ref[rows, :] = o.astype(BF16)
            lse_ref[rows, :] = mx + jnp.log(tot)
            return carry

        lax.fori_loop(0, S // 256, comb, 0)

    pair = 2 * HD
    blk_spec = lambda off: pl.BlockSpec((S, pair), lambda b, p, off=off: (b, off + p))
    gspec = pl.BlockSpec((1, pair), lambda b, p: (0, 0))
    return _pallas(
        body, (u, u, u, qg2, kg2), grid=(B, DA // pair),
        in_specs=[blk_spec(0), blk_spec(DA // pair), blk_spec(2 * DA // pair), gspec, gspec],
        out_specs=[blk_spec(0), blk_spec(0)],
        out_shape=[jax.ShapeDtypeStruct((T, DA), BF16), jax.ShapeDtypeStruct((T, DA), F32)],
        scratch_shapes=[pltpu.VMEM((S, pair), F32)] * 3 + [pltpu.VMEM((3, S, pair), F32)] * 2,
        sem=("parallel", "parallel"), name="attn_fwd", carry=carry)


def _attn_bwd(u, attn, dattn, lse, qg2, kg2, B, S, carry=None):
    T = B * S
    NB = S // BLK
    scale = HD ** -0.5
    pair = 2 * HD

    def body(q_ref, k_ref, v_ref, o_ref, do_ref, lse_ref, qg_ref, kg_ref,
             dq_ref, dk_ref, dv_ref, dgn_ref,
             qn, kn, vn, don, ldl, accq, acck, accv, rq, rk):
        m0, m1 = _head_masks()
        lane = lax.broadcasted_iota(jnp.int32, (1, pair), 1)
        qv = q_ref[...].astype(F32)
        rq[...] = _head_rms(qv)
        qn[...] = qv * rq[...] * (qg_ref[...] * scale)
        kv = k_ref[...].astype(F32)
        rk[...] = _head_rms(kv)
        kn[...] = kv * rk[...] * kg_ref[...]
        vn[...] = v_ref[...].astype(F32)
        dov = do_ref[...].astype(F32)
        don[...] = dov
        ldl[...] = jnp.where((lane % HD) < HD // 2, lse_ref[...], _head_sums(dov * o_ref[...].astype(F32)))

        for i, d in enumerate(DILS):
            seg = NB // d

            def blk(j, c, i=i, d=d, seg=seg):
                cur, prev, first = _block_rows(j, d, seg)
                q2 = _stack_heads(qn[cur, :].astype(BF16), m0, m1)
                do2 = _stack_heads(don[cur, :].astype(BF16), m0, m1)
                (kk, vv), mask = _block_keys((kn, vn), cur, prev, first, seg == 1)
                ldv = ldl[cur, :]
                lse2 = jnp.concatenate([ldv[:, 0:1], ldv[:, HD:HD + 1]], axis=0)
                dl2 = jnp.concatenate([ldv[:, HD // 2:HD // 2 + 1], ldv[:, HD + HD // 2:HD + HD // 2 + 1]], axis=0)
                s = lax.dot_general(q2, kk, NT, preferred_element_type=F32)
                p = jnp.where(jnp.concatenate([mask, mask], axis=0), jnp.exp(s - lse2), 0.0)
                dp = lax.dot_general(do2, vv, NT, preferred_element_type=F32)
                ds = (p * (dp - dl2)).astype(BF16)
                dq_acc = _unstack_heads(jnp.dot(ds, kk, preferred_element_type=F32), m0)
                dk_acc = lax.dot_general(ds, q2, TN, preferred_element_type=F32)
                dv_acc = lax.dot_general(p.astype(BF16), do2, TN, preferred_element_type=F32)
                if i == 0:
                    accq[cur, :] = dq_acc
                    acck[cur, :] = dk_acc[BLK:2 * BLK]
                    accv[cur, :] = dv_acc[BLK:2 * BLK]
                    acck[prev, :] += dk_acc[0:BLK]
                    accv[prev, :] += dv_acc[0:BLK]
                elif seg == 1:
                    accq[cur, :] += dq_acc
                    acck[cur, :] += dk_acc
                    accv[cur, :] += dv_acc
                else:
                    accq[cur, :] += dq_acc
                    acck[prev, :] += dk_acc[0:BLK]
                    acck[cur, :] += dk_acc[BLK:2 * BLK]
                    accv[prev, :] += dv_acc[0:BLK]
                    accv[cur, :] += dv_acc[BLK:2 * BLK]
                return c

            lax.fori_loop(0, NB, blk, 0, unroll=8)

        def norm_bwd(x_ref, r_ref, dn, gain):
            r = r_ref[...]
            xhat = x_ref[...].astype(F32) * r
            dxhat = dn * gain
            dx = r * (dxhat - xhat * (_head_sums(dxhat * xhat) * (1.0 / HD)))
            return dx, jnp.sum(dn * xhat, axis=0, keepdims=True)

        dq, dgq = norm_bwd(q_ref, rq, accq[...], qg_ref[...] * scale)
        dk, dgk = norm_bwd(k_ref, rk, acck[...], kg_ref[...])
        dq_ref[...] = dq.astype(BF16)
        dk_ref[...] = dk.astype(BF16)
        dv_ref[...] = accv[...].astype(BF16)
        dgn_ref[...] = jnp.concatenate([dgq * scale, dgk, jnp.zeros((6, pair), F32)], axis=0)[None]

    blk_spec = lambda off: pl.BlockSpec((S, pair), lambda b, p, off=off: (b, off + p))
    gspec = pl.BlockSpec((1, pair), lambda b, p: (0, 0))
    np_ = DA // pair
    return _pallas(
        body, (u, u, u, attn, dattn, lse, qg2, kg2), grid=(B, np_),
        in_specs=[blk_spec(0), blk_spec(np_), blk_spec(2 * np_), blk_spec(0), blk_spec(0), blk_spec(0),
                  gspec, gspec],
        out_specs=[blk_spec(0), blk_spec(0), blk_spec(0),
                   pl.BlockSpec((1, 8, pair), lambda b, p: (b * np_ + p, 0, 0))],
        out_shape=[jax.ShapeDtypeStruct((T, DA), BF16)] * 3 + [jax.ShapeDtypeStruct((B * np_, 8, pair), F32)],
        scratch_shapes=[pltpu.VMEM((S, pair), F32)] * 10,
        sem=("parallel", "parallel"), name="attn_bwd", carry=carry)


CT = 32
CPAD = 32


def _shifted(win, offsets):
    rolled, out = {}, {}
    n = win.shape[0]
    for o in offsets:
        sub = o % 8
        if sub not in rolled:
            rolled[sub] = win if sub == 0 else pltpu.roll(win, n - sub, 0)
        out[o] = rolled[sub][o - sub:o - sub + CT, :]
    return out


def _ln_fwd(y, g, b):
    mu = jnp.mean(y, axis=-1, keepdims=True)
    yc = y - mu
    rstd = lax.rsqrt(jnp.mean(yc * yc, axis=-1, keepdims=True) + EPS)
    xhat = yc * rstd
    return xhat, rstd, xhat * g + b


def _fill_glu(ca_ref, cg_ref, glu, S):
    glu[pl.ds(0, CPAD), :] = jnp.zeros((CPAD, DC), F32)

    def fill(i, c):
        rows = pl.ds(pl.multiple_of(i * 256, 256), 256)
        a = ca_ref[rows, :].astype(F32)
        gt = cg_ref[rows, :].astype(F32)
        glu[pl.ds(pl.multiple_of(CPAD + i * 256, CT), 256), :] = a * _sigmoid(gt)
        return c

    lax.fori_loop(0, S // 256, fill, 0)


def _conv_fwd(u, cw, cb, lg, lb, B, S):
    T = B * S

    def body(ca_ref, cg_ref, w_ref, b_ref, lg_ref, lb_ref, o_ref, y_ref, glu):
        _fill_glu(ca_ref, cg_ref, glu, S)

        def step(i, c):
            t0 = pl.multiple_of(i * CT, CT)
            win = glu[pl.ds(t0, 2 * CT), :]
            acc = jnp.zeros((CT, DC), F32) + b_ref[...]
            taps = _shifted(win, [k + 2 for k in range(CK)])
            for k in range(CK):
                acc = acc + taps[k + 2] * w_ref[k:k + 1, :]
            y_ref[pl.ds(t0, CT), :] = acc
            _, _, z = _ln_fwd(acc, lg_ref[...], lb_ref[...])
            o_ref[pl.ds(t0, CT), :] = (z * _sigmoid(z)).astype(BF16)
            return c

        lax.fori_loop(0, S // CT, step, 0, unroll=4)

    vec = pl.BlockSpec((1, DC), lambda b: (0, 0))
    return pl.pallas_call(
        body, grid=(B,),
        in_specs=[pl.BlockSpec((S, DC), lambda b: (b, 3)), pl.BlockSpec((S, DC), lambda b: (b, 4)),
                  pl.BlockSpec((CT, DC), lambda b: (0, 0)), vec, vec, vec],
        out_specs=[pl.BlockSpec((S, DC), lambda b: (b, 0))] * 2,
        out_shape=[jax.ShapeDtypeStruct((T, DC), BF16), jax.ShapeDtypeStruct((T, DC), F32)],
        scratch_shapes=[pltpu.VMEM((CPAD + S, DC), F32)],
        compiler_params=_cp(("parallel",)), name="conv_fwd")(u, u, cw, cb, lg, lb)


def _conv_bwd(u, y, dconv, cw, lg, lb, B, S):
    T = B * S

    def body(ca_ref, cg_ref, y_ref, dc_ref, w_ref, lg_ref, lb_ref,
             dca_ref, dcg_ref, dw_ref, ds_ref, glu, dyp, dwacc):
        _fill_glu(ca_ref, cg_ref, glu, S)
        dyp[pl.ds(S, CPAD), :] = jnp.zeros((CPAD, DC), F32)
        lgv, lbv = lg_ref[...], lb_ref[...]

        def sum8(v):
            return functools.reduce(jnp.add, [v[r:r + 8] for r in range(0, v.shape[0], 8)])

        P1 = 4 * CT

        def p1(i, carry):
            sb, sg, sl = carry
            t0 = pl.multiple_of(i * P1, P1)
            xhat, rstd, z = _ln_fwd(y_ref[pl.ds(t0, P1), :], lgv, lbv)
            sz = _sigmoid(z)
            dz = dc_ref[pl.ds(t0, P1), :].astype(F32) * (sz * (1.0 + z * (1.0 - sz)))
            dxhat = dz * lgv
            dy = rstd * (dxhat - jnp.mean(dxhat, axis=-1, keepdims=True)
                         - xhat * jnp.mean(dxhat * xhat, axis=-1, keepdims=True))
            dyp[pl.ds(t0, P1), :] = dy
            return sb + sum8(dy), sg + sum8(dz * xhat), sl + sum8(dz)

        z8 = jnp.zeros((8, DC), F32)
        sb, sg, sl = lax.fori_loop(0, S // P1, p1, (z8, z8, z8))
        rs = lambda v: jnp.sum(v, axis=0, keepdims=True)
        ds_ref[...] = jnp.concatenate([rs(sb), rs(sg), rs(sl), jnp.zeros((5, DC), F32)], axis=0)[None]

        def p2(i, c):
            t0 = pl.multiple_of(i * CT, CT)
            win = dyp[pl.ds(t0, 2 * CT), :]
            acc = jnp.zeros((CT, DC), F32)
            taps = _shifted(win, [30 - k for k in range(CK)])
            for k in range(CK):
                acc = acc + taps[30 - k] * w_ref[k:k + 1, :]
            a = ca_ref[pl.ds(t0, CT), :].astype(F32)
            sgt = _sigmoid(cg_ref[pl.ds(t0, CT), :].astype(F32))
            dca_ref[pl.ds(t0, CT), :] = (acc * sgt).astype(BF16)
            dcg_ref[pl.ds(t0, CT), :] = (acc * a * sgt * (1.0 - sgt)).astype(BF16)
            return c

        lax.fori_loop(0, S // CT, p2, 0)

        dwacc[...] = jnp.zeros_like(dwacc)

        def p3(i, c):
            t0 = pl.multiple_of(i * CT, CT)
            win = glu[pl.ds(t0, 2 * CT), :]
            dy = dyp[pl.ds(t0, CT), :]
            for k in range(CK):
                dwacc[k] += sum8(dy * win[k + 2:k + 2 + CT, :])
            return c

        lax.fori_loop(0, S // CT, p3, 0)
        dw_ref[...] = jnp.sum(dwacc[...], axis=1)[None]

    vec = pl.BlockSpec((1, DC), lambda b: (0, 0))
    seq = pl.BlockSpec((S, DC), lambda b: (b, 0))
    return pl.pallas_call(
        body, grid=(B,),
        in_specs=[pl.BlockSpec((S, DC), lambda b: (b, 3)), pl.BlockSpec((S, DC), lambda b: (b, 4)),
                  seq, seq, pl.BlockSpec((CT, DC), lambda b: (0, 0)), vec, vec],
        out_specs=[seq, seq, pl.BlockSpec((1, CT, DC), lambda b: (b, 0, 0)),
                   pl.BlockSpec((1, 8, DC), lambda b: (b, 0, 0))],
        out_shape=[jax.ShapeDtypeStruct((T, DC), BF16)] * 2
                  + [jax.ShapeDtypeStruct((B, CT, DC), F32), jax.ShapeDtypeStruct((B, 8, DC), F32)],
        scratch_shapes=[pltpu.VMEM((CPAD + S, DC), F32), pltpu.VMEM((S + CPAD, DC), F32),
                        pltpu.VMEM((CT, 8, DC), F32)],
        compiler_params=_cp(("parallel",)), name="conv_bwd")(u, u, y, dconv, cw, lg, lb)


def _local_step(x, target, norms, W, B, S, comm=None):
    qg2 = jnp.concatenate([norms["q_norm"], norms["q_norm"]], axis=1)
    kg2 = jnp.concatenate([norms["k_norm"], norms["k_norm"]], axis=1)
    cw = jnp.concatenate([W["conv_w"], jnp.zeros((1, DC), F32)], axis=0)

    W = dict(W)
    (n1, g1, u1, act1), got = _ffn_gate_up(x, norms["ffn1_norm"], W["wg1"], W["wu1"], "ffn1_gate_up",
                                           carry=comm.gathers["down_in"] if comm else None)
    if comm:
        W.update(comm.gathered("down_in", got))
    h1, u, n2 = _ffn_down_mix_in(x, act1, W["wd1"], norms["mix_norm"], W["win"], "ffn1_down_mix_in")
    (attn, lse), got = _attn_fwd(u, qg2, kg2, B, S, carry=comm.gathers["ffn2"] if comm else None)
    if comm:
        W = dict(W, **comm.gathered("ffn2", got))
    conv, y = _conv_fwd(u, cw, norms["conv_b"], norms["conv_ln_g"], norms["conv_ln_b"], B, S)
    h2, n3, g2, u2, dout, dyb, sq, act2 = _mix_out_ffn_loss(h1, attn, conv, W["wout"], norms["ffn2_norm"],
                                                            W["wg2"], W["wu2"], W["wd2"], target, "ffn2_fwd")
    loss = (0.5 / D) * jnp.sum(sq)

    (dg2, du2, dh2, dgn_ffn2), _ = _ffn_bwd_act(dyb, g2, u2, h2, dout, norms["ffn2_norm"],
                                               W["wg2"], W["wu2"], W["wd2"], "ffn2_bwd_act")
    dwd2, _ = _ffn_bwd_w(act2, dyb, "ffn2_bwd_wd")
    dwg2, _ = _ffn_bwd_w(dg2, n3, "ffn2_bwd_wg")
    dwu2, _ = _ffn_bwd_w(du2, n3, "ffn2_bwd_wu")
    dattn, dconv, dwout = _mix_out_bwd(dh2, attn, conv, W["wout"])
    carry = comm.reduce_start({"wg2": dwg2, "wu2": dwu2, "wd2": dwd2, "wout": dwout}) if comm else None
    (dq, dk, dv, dgn_qk), got = _attn_bwd(u, attn, dattn, lse, qg2, kg2, B, S, carry=carry)
    if comm:
        comm.reduce_done(carry, got)
    dca, dcg, dcw, dcs = _conv_bwd(u, y, dconv, cw, norms["conv_ln_g"], norms["conv_ln_b"], B, S)
    dwin, dh1, dyb1, dgn_mix = _mix_in_bwd((dq, dk, dv, dca, dcg), W["win"], n2, h1, dh2, norms["mix_norm"])
    carry = comm.update_start(("wg2", "wu2", "wd2", "wout")) if comm else None
    (dg1, du1, gx, dgn_ffn1), got = _ffn_bwd_act(dyb1, g1, u1, x, dh1, norms["ffn1_norm"],
                                                W["wg1"], W["wu1"], W["wd1"], "ffn1_bwd_act", carry=carry)
    if comm:
        comm.update_done(carry, got)
    carry = comm.reduce_start({"win": dwin}) if comm else None
    dwd1, got = _ffn_bwd_w(act1, dyb1, "ffn1_bwd_wd", carry=carry)
    if comm:
        comm.reduce_done(carry, got)
        carry = comm.reduce_start({"wd1": dwd1})
    dwg1, got = _ffn_bwd_w(dg1, n1, "ffn1_bwd_wg", carry=carry)
    if comm:
        comm.reduce_done(carry, got)
        carry = comm.reduce_start({"wg1": dwg1})
    dwu1, got = _ffn_bwd_w(du1, n1, "ffn1_bwd_wu", carry=carry)
    if comm:
        comm.reduce_done(carry, got)
        comm.last = comm.reduce_start({"wu1": dwu1})

    qk = jnp.sum(dgn_qk, axis=0)
    cs = jnp.sum(dcs, axis=0)
    small = {
        "ffn1_norm": jnp.sum(dgn_ffn1, axis=0),
        "mix_norm": jnp.sum(dgn_mix, axis=0),
        "q_norm": qk[0:1, 0:HD] + qk[0:1, HD:2 * HD],
        "k_norm": qk[1:2, 0:HD] + qk[1:2, HD:2 * HD],
        "conv_w": jnp.sum(dcw, axis=0)[0:CK],
        "conv_b": cs[0:1],
        "conv_ln_g": cs[1:2],
        "conv_ln_b": cs[2:3],
        "ffn2_norm": jnp.sum(dgn_ffn2, axis=0),
    }
    big = {"wg1": dwg1, "wu1": dwu1, "wd1": dwd1, "win": dwin, "wout": dwout,
           "wg2": dwg2, "wu2": dwu2, "wd2": dwd2}
    return loss, gx, big, small


HBM = pl.BlockSpec(memory_space=pltpu.HBM)
VMEM = pl.BlockSpec(memory_space=pltpu.VMEM)


def _place():
    return lax.axis_index("x"), lax.axis_index("y"), lax.axis_index("c")


class _GatherCarry:
    def __init__(self, shards, mid_at=0.5):
        nt = len(shards)
        self.mid_at = mid_at
        self.shards = shards
        self.in_arrays = [s for s, _ in shards]
        self.in_specs = [VMEM] * nt
        self.out_shape = [jax.ShapeDtypeStruct((NDEV * s.shape[0], s.shape[1]), dt) for s, dt in shards]
        self.out_specs = [HBM] * nt
        self.scratch = ([pltpu.VMEM(s.shape, dt) for s, dt in shards]
                        + [pltpu.SemaphoreType.DMA((nt, 7)), pltpu.SemaphoreType.DMA((nt, 7)),
                           pltpu.SemaphoreType.DMA((nt,))])

    def _copies(self, outs, scr):
        nt = len(self.shards)
        stages = scr[:nt]
        send_sems, recv_sems, local_sems = scr[nt:]
        x, y, c = _place()
        me, sibling = (x, y, c), (x, y, 1 - c)
        xn, yn, diag = (1 - x, y, c), (x, 1 - y, c), (1 - x, 1 - y, c)
        via = (x ^ c, y ^ (1 - c), c)
        onto = (x ^ (1 - c), y ^ c, c)

        def rows(t, px, py, pc):
            r = self.shards[t][0].shape[0]
            return outs[t].at[pl.ds((4 * px + 2 * py + pc) * r, r), :]

        def copy(t, k, block, to, src=None):
            return pltpu.make_async_remote_copy(
                src_ref=rows(t, *block) if src is None else src, dst_ref=rows(t, *block),
                send_sem=send_sems.at[t, k], recv_sem=recv_sems.at[t, k],
                device_id=to, device_id_type=MESH)

        sib = lambda b: (b[0], b[1], 1 - c)
        return dict(
            local=[pltpu.make_async_copy(stages[t], rows(t, *me), local_sems.at[t]) for t in range(nt)],
            own=[[copy(t, 0, me, sibling, src=stages[t]), copy(t, 1, me, xn, src=stages[t]),
                  copy(t, 2, me, yn, src=stages[t])] for t in range(nt)],
            relay=[copy(t, 3, via, onto) for t in range(nt)],
            down=[[copy(t, 4, xn, sibling), copy(t, 5, yn, sibling)] for t in range(nt)],
            down_diag=[copy(t, 6, diag, sibling) for t in range(nt)],
            got_xy=[[copy(t, 1, xn, me), copy(t, 2, yn, me)] for t in range(nt)],
            got_diag=[copy(t, 3, diag, me) for t in range(nt)],
            got_sib=[[copy(t, 0, sibling, me), copy(t, 4, sib(xn), me), copy(t, 5, sib(yn), me),
                      copy(t, 6, sib(diag), me)] for t in range(nt)])

    def start(self, ins, outs, scr):
        cps = self._copies(outs, scr)
        for t, (_, dt) in enumerate(self.shards):
            scr[t][...] = ins[t][...].astype(dt)
            for cp in [cps["local"][t]] + cps["own"][t]:
                cp.start()

    def stages(self):
        sizes = [s.size * jnp.dtype(dt).itemsize for s, dt in self.shards]
        done = [sum(sizes[:t + 1]) / sum(sizes) for t in range(len(sizes))]
        return [(self.mid_at * f, functools.partial(self.mid, t)) for t, f in enumerate(done)]

    def mid(self, t, ins, outs, scr):
        cps = self._copies(outs, scr)
        for cp in cps["got_xy"][t]:
            cp.wait_recv()
        for cp in [cps["relay"][t]] + cps["down"][t]:
            cp.start()

    def finish(self, ins, outs, scr):
        cps = self._copies(outs, scr)
        for t in range(len(self.shards)):
            cps["got_diag"][t].wait_recv()
            cps["down_diag"][t].start()
        for t in range(len(self.shards)):
            for cp in cps["got_sib"][t]:
                cp.wait_recv()
            for cp in cps["own"][t] + [cps["relay"][t]] + cps["down"][t] + [cps["down_diag"][t]]:
                cp.wait_send()
            cps["local"][t].wait()


def _run_carry(carry, name):
    def body(*refs):
        n_in, n_out = len(carry.in_arrays), len(carry.out_shape)
        ins, outs, scr = refs[:n_in], refs[n_in:n_in + n_out], refs[n_in + n_out:]
        carry.start(ins, outs, scr)
        for _, stage in carry.stages():
            stage(ins, outs, scr)
        carry.finish(ins, outs, scr)

    return pl.pallas_call(
        body, in_specs=carry.in_specs, out_specs=carry.out_specs, out_shape=carry.out_shape,
        scratch_shapes=carry.scratch, compiler_params=pltpu.CompilerParams(vmem_limit_bytes=VMEM_LIMIT),
        name=name)(*carry.in_arrays)


class _ExchangeCarry:
    def __init__(self, names, grads, mid_at=0.5):
        nt = len(grads)
        self.mid_at = mid_at
        self.names = names
        self.in_arrays = [g.reshape(4, 2, g.shape[0] // NDEV, g.shape[1]) for g in grads]
        self.in_specs = [HBM] * nt
        blocks = [g.shape[2:] for g in self.in_arrays]
        self.out_shape = [jax.ShapeDtypeStruct((3,) + b, BF16) for b in blocks]
        self.out_specs = [HBM] * nt
        self.scratch = ([pltpu.VMEM((4,) + b, BF16) for b in blocks] * 2 + [pltpu.VMEM(b, BF16) for b in blocks]
                        + [pltpu.SemaphoreType.DMA((nt, 3)), pltpu.SemaphoreType.DMA((nt, 3))]
                        + [pltpu.SemaphoreType.DMA((nt,))] * 4)

    def _copies(self, ins, outs, scr):
        nt = len(ins)
        theirs, own, relayed = scr[:nt], scr[nt:2 * nt], scr[2 * nt:3 * nt]
        send_sems, recv_sems, keep_sems, load_sems, swap_send, swap_recv = scr[3 * nt:]
        x, y, c = _place()
        q = lambda cx, cy: 2 * cx + cy
        near, far = (x ^ c, y ^ (1 - c)), (x ^ (1 - c), y ^ c)

        def remote(t, k, src, dst, chip):
            return pltpu.make_async_remote_copy(
                src_ref=src, dst_ref=dst, send_sem=send_sems.at[t, k], recv_sem=recv_sems.at[t, k],
                device_id=(*chip, c), device_id_type=MESH)

        return dict(
            swap=[pltpu.make_async_remote_copy(
                src_ref=ins[t].at[:, 1 - c], dst_ref=theirs[t], send_sem=swap_send.at[t], recv_sem=swap_recv.at[t],
                device_id=(x, y, 1 - c), device_id_type=MESH) for t in range(nt)],
            load=[pltpu.make_async_copy(ins[t].at[:, c], own[t], load_sems.at[t]) for t in range(nt)],
            keep=[pltpu.make_async_copy(own[t].at[q(x, y)], outs[t].at[0], keep_sems.at[t]) for t in range(nt)],
            direct=[remote(t, 0, own[t].at[q(*near)], outs[t].at[1], near) for t in range(nt)],
            relay=[remote(t, 1, own[t].at[q(1 - x, 1 - y)], relayed[t], near) for t in range(nt)],
            merged=[remote(t, 2, own[t].at[q(*far)], outs[t].at[2], far) for t in range(nt)],
            theirs=theirs, own=own, relayed=relayed, far=q(*far))

    EARLY_AT = 0.1

    def stages(self):
        return [(self.EARLY_AT, self.early), (self.mid_at, self.mid)]

    def start(self, ins, outs, scr):
        cps = self._copies(ins, outs, scr)
        for t in range(len(ins)):
            cps["swap"][t].start()
            cps["load"][t].start()

    def early(self, ins, outs, scr):
        cps = self._copies(ins, outs, scr)
        for t in range(len(ins)):
            cps["load"][t].wait()
            cps["swap"][t].wait_recv()
            own, theirs = cps["own"][t], cps["theirs"][t]
            for j in range(4):
                own[j] = (own[j].astype(F32) + theirs[j].astype(F32)).astype(BF16)
            for kind in ("relay", "direct", "keep"):
                cps[kind][t].start()

    def mid(self, ins, outs, scr):
        cps = self._copies(ins, outs, scr)
        for t in range(len(ins)):
            cps["relay"][t].wait_recv()
            own, far = cps["own"][t], cps["far"]
            own[far] = (own[far].astype(F32) + cps["relayed"][t][...].astype(F32)).astype(BF16)
            cps["merged"][t].start()

    def finish(self, ins, outs, scr):
        cps = self._copies(ins, outs, scr)
        for t in range(len(ins)):
            cps["swap"][t].wait_send()
            cps["direct"][t].wait()
            cps["relay"][t].wait_send()
            cps["merged"][t].wait()
            cps["keep"][t].wait()


class _Comm:
    def __init__(self, groups, opt):
        self.names = {tag: list(g) for tag, (g, _) in groups.items()}
        self.gathers = {tag: _GatherCarry(list(g.values()), mid_at) for tag, (g, mid_at) in groups.items()}
        self.reduced = {}
        self.last = None
        self.opt = opt
        self.updated = {}

    def gathered(self, tag, outs):
        return dict(zip(self.names[tag], outs))

    def reduce_start(self, grads):
        names = list(grads)
        return _ExchangeCarry(names, [grads[n] for n in names])

    def reduce_done(self, carry, outs):
        self.reduced.update(zip(carry.names, outs))

    def update_start(self, names):
        w, m, v = zip(*[self.opt[n] for n in names])
        return _AdamWCarry(names, [self.reduced[n] for n in names], w, m, v)

    def update_done(self, carry, outs):
        self.updated.update({n: outs[4 * k:4 * k + 4] for k, n in enumerate(carry.names)})


def _adamw_math(w, g, m, v):
    m = B1 * m + (1.0 - B1) * g
    v = B2 * v + (1.0 - B2) * (g * g)
    m_hat = m / (1.0 - B1 ** STEP)
    v_hat = v / (1.0 - B2 ** STEP)
    delta = -LR * (m_hat / (jnp.sqrt(v_hat) + AEPS) + WD * w)
    return delta, m, v


def _adamw_big(recvs, ws, ms, vs, name):
    nw = len(ws)

    def body(*refs):
        ins, outs = refs[:4 * nw], refs[4 * nw:]
        for k in range(nw):
            @pl.when(pl.program_id(0) // 2 == k)
            def _(k=k):
                r_ref, w_ref, m_ref, v_ref = ins[4 * k:4 * k + 4]
                g_ref, d_ref, mo_ref, vo_ref = outs[4 * k:4 * k + 4]
                g = r_ref[0].astype(F32)
                for q in range(1, 3):
                    g = g + r_ref[q].astype(F32)
                d, mn, vn = _adamw_math(w_ref[...], g, m_ref[...], v_ref[...])
                g_ref[...] = g
                d_ref[...] = d
                mo_ref[...] = mn
                vo_ref[...] = vn

    in_specs, out_specs, out_shape, args = [], [], [], []
    for k, (r, w, m, v) in enumerate(zip(recvs, ws, ms, vs)):
        rows, n = w.shape
        tr = rows // 2
        tile = lambda s, k=k: jnp.clip(s - 2 * k, 0, 1)
        row = pl.BlockSpec((tr, n), lambda s, tile=tile: (tile(s), 0))
        in_specs += [pl.BlockSpec((3, tr, n), lambda s, tile=tile: (0, tile(s), 0)), row, row, row]
        out_specs += [row] * 4
        out_shape += [jax.ShapeDtypeStruct(w.shape, F32)] * 4
        args += [r, w, m, v]
    res = pl.pallas_call(
        body, grid=(2 * nw,), in_specs=in_specs, out_specs=out_specs, out_shape=out_shape,
        compiler_params=_cp(("arbitrary",)), name=name)(*args)
    return [res[4 * k:4 * k + 4] for k in range(nw)]


class _AdamWCarry:
    def __init__(self, names, recvs, ws, ms, vs):
        self.names = names
        nw = len(ws)
        self.halves = [(k, j, w.shape[0] // 2) for k, w in enumerate(ws) for j in range(2)]
        self.in_arrays = [a for quad in zip(recvs, ws, ms, vs) for a in quad]
        self.in_specs = [HBM] * (4 * nw)
        self.out_shape = [jax.ShapeDtypeStruct(w.shape, F32) for w in ws for _ in range(4)]
        self.out_specs = [HBM] * (4 * nw)
        tr, n = max(h[2] for h in self.halves), ws[0].shape[1]
        self.scratch = [pltpu.VMEM((2, 3, tr, n), BF16), pltpu.VMEM((2, 3, tr, n), F32),
                        pltpu.VMEM((2, 4, tr, n), F32), pltpu.SemaphoreType.DMA((2, 4)),
                        pltpu.SemaphoreType.DMA((2, 4))]

    def _copies(self, c, ins, outs, scr):
        rbuf, fbuf, obuf, in_sems, out_sems = scr
        k, j, tr = self.halves[c]
        s, rows = c % 2, pl.ds(j * tr, tr)
        loads = [pltpu.make_async_copy(ins[4 * k].at[:, rows, :], rbuf.at[s, :, pl.ds(0, tr), :], in_sems.at[s, 0])]
        loads += [pltpu.make_async_copy(ins[4 * k + i].at[rows, :], fbuf.at[s, i - 1, pl.ds(0, tr), :],
                                        in_sems.at[s, i]) for i in range(1, 4)]
        stores = [pltpu.make_async_copy(obuf.at[s, q, pl.ds(0, tr), :], outs[4 * k + q].at[rows, :],
                                        out_sems.at[s, q]) for q in range(4)]
        return loads, stores

    def start(self, ins, outs, scr):
        for c in range(2):
            for cp in self._copies(c, ins, outs, scr)[0]:
                cp.start()

    def stages(self):
        n = len(self.halves)
        return [((c + 1) / (n + 1), functools.partial(self.half, c)) for c in range(n)]

    def half(self, c, ins, outs, scr):
        rbuf, fbuf, obuf = scr[:3]
        _, _, tr = self.halves[c]
        s = c % 2
        loads, stores = self._copies(c, ins, outs, scr)
        for cp in loads:
            cp.wait()
        if c >= 2:
            for cp in self._copies(c - 2, ins, outs, scr)[1]:
                cp.wait()
        g = rbuf[s, 0, 0:tr].astype(F32)
        for q in range(1, 3):
            g = g + rbuf[s, q, 0:tr].astype(F32)
        d, mn, vn = _adamw_math(fbuf[s, 0, 0:tr], g, fbuf[s, 1, 0:tr], fbuf[s, 2, 0:tr])
        for q, val in enumerate((g, d, mn, vn)):
            obuf[s, q, 0:tr] = val
        for cp in stores:
            cp.start()
        if c + 2 < len(self.halves):
            for cp in self._copies(c + 2, ins, outs, scr)[0]:
                cp.start()

    def finish(self, ins, outs, scr):
        n = len(self.halves)
        for c in range(max(n - 2, 0), n):
            for cp in self._copies(c, ins, outs, scr)[1]:
                cp.wait()


SMALL_NAMES = ("ffn1_norm", "mix_norm", "ffn2_norm", "conv_b", "conv_ln_g", "conv_ln_b", "q_norm", "k_norm")
SROWS = 16
LOSS_ROW = len(SMALL_NAMES)
CWF = 2


def _small_sums(gs, loss_row, gcw, carry=None):
    ns = len(SMALL_NAMES)
    widths = [g.shape[1] for g in gs]

    def body(*refs):
        it = iter(refs)
        take = lambda n: [next(it) for _ in range(n)]
        g_refs, (loss_ref, gcw_ref) = take(ns), take(2)
        cins = take(len(carry.in_arrays)) if carry else []
        tot_ref, ctot_ref = take(2)
        couts = take(len(carry.out_shape)) if carry else []
        send, slots, cslots, send_sems, recv_sems, csend_sems, crecv_sems = take(7)
        cscr = list(it)
        x, y, c = _place()
        me = 4 * x + 2 * y + c
        send[...] = jnp.zeros_like(send)
        for k in range(ns):
            send[k:k + 1, 0:widths[k]] = g_refs[k][...]
        send[LOSS_ROW:LOSS_ROW + 1, 0:128] = loss_ref[...]
        slots[me] = send[...]
        cslots[me] = gcw_ref[...]
        cps = []
        for k in range(1, NDEV):
            peer = (x ^ ((k >> 2) & 1), y ^ ((k >> 1) & 1), c ^ (k & 1))
            cps.append(pltpu.make_async_remote_copy(
                src_ref=send, dst_ref=slots.at[me], send_sem=send_sems.at[k - 1], recv_sem=recv_sems.at[k - 1],
                device_id=peer, device_id_type=MESH))
            cps.append(pltpu.make_async_remote_copy(
                src_ref=gcw_ref, dst_ref=cslots.at[me], send_sem=csend_sems.at[k - 1],
                recv_sem=crecv_sems.at[k - 1], device_id=peer, device_id_type=MESH))
        for cp in cps:
            cp.start()
        stages = [stage for _, stage in carry.stages()] if carry else []
        if carry:
            carry.start(cins, couts, cscr)
        for stage in stages[:1]:
            stage(cins, couts, cscr)
        for cp in cps:
            cp.wait()
        tot = slots[0]
        ctot = cslots[0, me]
        for j in range(1, NDEV):
            tot = tot + slots[j]
            ctot = ctot + cslots[j, me]
        tot_ref[...] = tot
        ctot_ref[...] = ctot
        for stage in stages[1:]:
            stage(cins, couts, cscr)
        if carry:
            carry.finish(cins, couts, cscr)

    args = [*gs, loss_row, gcw]
    out_shape = [jax.ShapeDtypeStruct((SROWS, D), F32), jax.ShapeDtypeStruct((CWF, D), F32)]
    res = pl.pallas_call(
        body, in_specs=[VMEM] * len(args) + (carry.in_specs if carry else []),
        out_specs=[VMEM] * 2 + (carry.out_specs if carry else []),
        out_shape=out_shape + (carry.out_shape if carry else []),
        scratch_shapes=[pltpu.VMEM((SROWS, D), F32), pltpu.VMEM((NDEV, SROWS, D), F32),
                        pltpu.VMEM((NDEV, NDEV, CWF, D), F32)]
                       + [pltpu.SemaphoreType.DMA((NDEV - 1,))] * 4 + (carry.scratch if carry else []),
        name="small_sums")(*args, *(carry.in_arrays if carry else []))
    return res[0], res[1], res[2:]


def _adamw_small(tot, ctot, ws, ms, vs, wcw, mcw, vcw):
    ns = len(SMALL_NAMES)
    widths = [w.shape[1] for w in ws]

    def body(*refs):
        it = iter(refs)
        take = lambda n: [next(it) for _ in range(n)]
        (tot_ref, ctot_ref), w_refs, m_refs, v_refs = take(2), take(ns), take(ns), take(ns)
        wcw_ref, mcw_ref, vcw_ref = take(3)
        outs = [take(4) for _ in range(ns)]
        cw_outs, (loss_out,) = take(4), take(1)

        def step(g, w_ref, m_ref, v_ref, o):
            d, mn, vn = _adamw_math(w_ref[...], g, m_ref[...], v_ref[...])
            o[0][...], o[1][...], o[2][...], o[3][...] = g, d, mn, vn

        for k in range(ns):
            step(tot_ref[k:k + 1, 0:widths[k]], w_refs[k], m_refs[k], v_refs[k], outs[k])
        step(ctot_ref[...], wcw_ref, mcw_ref, vcw_ref, cw_outs)
        loss_out[...] = tot_ref[LOSS_ROW:LOSS_ROW + 1, 0:128]

    args = [tot, ctot, *ws, *ms, *vs, wcw, mcw, vcw]
    out_shape = ([jax.ShapeDtypeStruct((1, n), F32) for n in widths for _ in range(4)]
                 + [jax.ShapeDtypeStruct((CWF, D), F32)] * 4 + [jax.ShapeDtypeStruct((1, 128), F32)])
    res = pl.pallas_call(
        body, in_specs=[VMEM] * len(args), out_specs=[VMEM] * len(out_shape), out_shape=out_shape,
        name="adamw_small")(*args)
    per = [res[4 * k:4 * k + 4] for k in range(ns)]
    return per, res[4 * ns:4 * ns + 4], res[-1]


def _pack_cw(a):
    flat = a.reshape(a.shape[:-2] + (CK * HD,))
    pad = [(0, 0)] * (flat.ndim - 1) + [(0, CWF * D - CK * HD)]
    return jnp.pad(flat, pad).reshape(a.shape[:-2] + (CWF, D))


def _unpack_cw(v):
    return v.reshape(-1)[:CK * HD].reshape(1, CK, HD)


def kernel(x, ffn1_norm, ffn1_w_gate, ffn1_w_up, ffn1_w_down, mix_norm, w_in, q_norm, k_norm, conv_w, conv_b, conv_ln_g, conv_ln_b, w_out, ffn2_norm, ffn2_w_gate, ffn2_w_up, ffn2_w_down, loss_target, m_ffn1_norm, m_ffn1_w_gate, m_ffn1_w_up, m_ffn1_w_down, m_mix_norm, m_w_in, m_q_norm, m_k_norm, m_conv_w, m_conv_b, m_conv_ln_g, m_conv_ln_b, m_w_out, m_ffn2_norm, m_ffn2_w_gate, m_ffn2_w_up, m_ffn2_w_down, v_ffn1_norm, v_ffn1_w_gate, v_ffn1_w_up, v_ffn1_w_down, v_mix_norm, v_w_in, v_q_norm, v_k_norm, v_conv_w, v_conv_b, v_conv_ln_g, v_conv_ln_b, v_w_out, v_ffn2_norm, v_ffn2_w_gate, v_ffn2_w_up, v_ffn2_w_down):
    P = dict(ffn1_norm=ffn1_norm, ffn1_w_gate=ffn1_w_gate, ffn1_w_up=ffn1_w_up, ffn1_w_down=ffn1_w_down,
             mix_norm=mix_norm, w_in=w_in, q_norm=q_norm, k_norm=k_norm, conv_w=conv_w, conv_b=conv_b,
             conv_ln_g=conv_ln_g, conv_ln_b=conv_ln_b, w_out=w_out, ffn2_norm=ffn2_norm,
             ffn2_w_gate=ffn2_w_gate, ffn2_w_up=ffn2_w_up, ffn2_w_down=ffn2_w_down)
    M = dict(ffn1_norm=m_ffn1_norm, ffn1_w_gate=m_ffn1_w_gate, ffn1_w_up=m_ffn1_w_up, ffn1_w_down=m_ffn1_w_down,
             mix_norm=m_mix_norm, w_in=m_w_in, q_norm=m_q_norm, k_norm=m_k_norm, conv_w=m_conv_w, conv_b=m_conv_b,
             conv_ln_g=m_conv_ln_g, conv_ln_b=m_conv_ln_b, w_out=m_w_out, ffn2_norm=m_ffn2_norm,
             ffn2_w_gate=m_ffn2_w_gate, ffn2_w_up=m_ffn2_w_up, ffn2_w_down=m_ffn2_w_down)
    V = dict(ffn1_norm=v_ffn1_norm, ffn1_w_gate=v_ffn1_w_gate, ffn1_w_up=v_ffn1_w_up, ffn1_w_down=v_ffn1_w_down,
             mix_norm=v_mix_norm, w_in=v_w_in, q_norm=v_q_norm, k_norm=v_k_norm, conv_w=v_conv_w, conv_b=v_conv_b,
             conv_ln_g=v_conv_ln_g, conv_ln_b=v_conv_ln_b, w_out=v_w_out, ffn2_norm=v_ffn2_norm,
             ffn2_w_gate=v_ffn2_w_gate, ffn2_w_up=v_ffn2_w_up, ffn2_w_down=v_ffn2_w_down)
    order = ["ffn1_norm", "ffn1_w_gate", "ffn1_w_up", "ffn1_w_down", "mix_norm", "w_in", "q_norm", "k_norm",
             "conv_w", "conv_b", "conv_ln_g", "conv_ln_b", "w_out", "ffn2_norm", "ffn2_w_gate", "ffn2_w_up",
             "ffn2_w_down"]
    B, S, _ = x.shape
    T = B * S

    bigs = [("wg1", "ffn1_w_gate", True), ("wu1", "ffn1_w_up", True), ("wd1", "ffn1_w_down", False),
            ("win", "w_in", True), ("wout", "w_out", False),
            ("wg2", "ffn2_w_gate", True), ("wu2", "ffn2_w_up", True), ("wd2", "ffn2_w_down", False)]
    hm = lambda a, tr: jnp.transpose(a[0]) if tr else a[0]
    cw_pad = jnp.zeros((32, 128), F32).at[0:CK, 0:HD].set(conv_w[0])
    shard = {ln: (hm(P[pn], tr), BF16) for ln, pn, tr in bigs}
    gathered = _run_carry(_GatherCarry([shard["wg1"], shard["wu1"], (cw_pad, F32)]), "gather_first")
    W = {"wg1": gathered[0], "wu1": gathered[1]}
    cwg = gathered[2].reshape(NDEV, 32, 128)[:, 0:CK, 0:HD]
    W["conv_w"] = jnp.transpose(cwg, (1, 0, 2)).reshape(CK, DC)
    norms = {n: P[n] for n in SMALL_NAMES}
    comm = _Comm({"down_in": ({n: shard[n] for n in ("wd1", "win")}, 0.5),
                  "ffn2": ({n: shard[n] for n in ("wg2", "wu2", "wd2", "wout")}, 0.5)},
                 opt={ln: (hm(P[pn], tr), hm(M[pn], tr), hm(V[pn], tr)) for ln, pn, tr in bigs})

    loss_part, gx, _, small = _local_step(x.reshape(T, D), loss_target.reshape(T, D), norms, W, B, S, comm)

    G, Dl, Mn, Vn = {}, {}, {}, {}
    dcw = small["conv_w"].reshape(CK, NDEV, HD).transpose(1, 0, 2)
    loss_row = jnp.zeros((1, 128), F32).at[0, 0].set(loss_part)
    tot, ctot, got = _small_sums([small[n] for n in SMALL_NAMES], loss_row, _pack_cw(dcw), carry=comm.last)
    comm.reduce_done(comm.last, got)
    per, cw_outs, loss_out = _adamw_small(
        tot, ctot, [P[n] for n in SMALL_NAMES], [M[n] for n in SMALL_NAMES], [V[n] for n in SMALL_NAMES],
        _pack_cw(P["conv_w"][0]), _pack_cw(M["conv_w"][0]), _pack_cw(V["conv_w"][0]))
    loss = loss_out[0, 0]
    for n, outs in zip(SMALL_NAMES, per):
        G[n], Dl[n], Mn[n], Vn[n] = outs
    G["conv_w"], Dl["conv_w"], Mn["conv_w"], Vn["conv_w"] = [_unpack_cw(o) for o in cw_outs]

    for group in (("win", "wd1"), ("wg1", "wu1")):
        w, m, v = zip(*[comm.opt[ln] for ln in group])
        res = _adamw_big([comm.reduced[ln] for ln in group], w, m, v, "adamw_" + "_".join(group))
        comm.updated.update(zip(group, res))
    for ln, pn, tr in bigs:
        G[pn], Dl[pn], Mn[pn], Vn[pn] = [(jnp.transpose(o) if tr else o)[None] for o in comm.updated[ln]]

    return (loss, gx.reshape(B, S, D), *[G[n] for n in order], *[Dl[n] for n in order],
            *[Mn[n] for n in order], *[Vn[n] for n in order])
```

```python
import functools

import jax
import jax.numpy as jnp
from jax import lax
from jax.experimental import pallas as pl
from jax.experimental.pallas import tpu as pltpu

F32 = jnp.float32
BF16 = jnp.bfloat16

D = 1024
FF = 2816
HD = 64
DA = 512
DC = 512
DIN = 2560
CK = 31
BLK = 128
DILS = (1, 4, 16)
EPS = 1e-6
NDEV = 8
MESH = pl.DeviceIdType.MESH

LR, B1, B2, AEPS, WD, STEP = 0.001, 0.9, 0.999, 1e-08, 0.01, 10

NT = (((1,), (1,)), ((), ()))
TN = (((0,), (0,)), ((), ()))

VMEM_LIMIT = 60 * 1024 * 1024


def _cp(sem=None):
    return pltpu.CompilerParams(dimension_semantics=sem, vmem_limit_bytes=VMEM_LIMIT)


def _sigmoid(x):
    return 0.5 * (jnp.tanh(0.5 * x) + 1.0)


def _pallas(body, args, *, grid, in_specs, out_specs, out_shape, scratch_shapes, sem, name, carry=None):
    if carry is None:
        outs = pl.pallas_call(body, grid=grid, in_specs=in_specs, out_specs=out_specs, out_shape=out_shape,
                              scratch_shapes=scratch_shapes, compiler_params=_cp(sem), name=name)(*args)
        return outs, None
    n_in, n_out, n_scr = len(in_specs), len(out_shape), len(scratch_shapes)
    c_in, c_out = len(carry.in_arrays), len(carry.out_shape)

    def wrapped(*refs):
        ins, refs = refs[:n_in], refs[n_in:]
        cins, refs = refs[:c_in], refs[c_in:]
        outs, refs = refs[:n_out], refs[n_out:]
        couts, refs = refs[:c_out], refs[c_out:]
        scr, cscr = refs[:n_scr], refs[n_scr:]
        ids = [pl.program_id(a) for a in range(len(grid))]
        step = ids[0]
        for i, n in zip(ids[1:], grid[1:]):
            step = step * n + i
        steps = functools.reduce(lambda a, b: a * b, grid)

        @pl.when(step == 0)
        def _():
            carry.start(cins, couts, cscr)

        body(*ins, *outs, *scr)

        for frac, stage in carry.stages():
            @pl.when(step == int(steps * frac))
            def _(stage=stage):
                stage(cins, couts, cscr)

        @pl.when(step == steps - 1)
        def _():
            carry.finish(cins, couts, cscr)

    outs = pl.pallas_call(
        wrapped, grid=grid, in_specs=list(in_specs) + carry.in_specs, out_specs=list(out_specs) + carry.out_specs,
        out_shape=list(out_shape) + carry.out_shape, scratch_shapes=list(scratch_shapes) + carry.scratch,
        compiler_params=_cp(("arbitrary",) * len(grid)), name=name)(*args, *carry.in_arrays)
    return outs[:n_out], outs[n_out:]


FC = 256


def _resident(shape):
    return pl.BlockSpec(shape, lambda *_: (0,) * len(shape), pipeline_mode=pl.Buffered(1))


def _mix_out_ffn_loss(h1, attn, conv, wout, gain, wg, wu, wd, target, name):
    T = h1.shape[0]
    tm = 512
    nt = T // tm

    def body(h1_ref, at_ref, cv_ref, wo_ref, gain_ref, wg_ref, wu_ref, wd_ref, t_ref,
             h2_ref, n_ref, g_ref, u_ref, dout_ref, dyb_ref, sq_ref, a_hbm, a_scr, a_sem):
        t = pl.program_id(0)
        a_out = lambda i: pltpu.make_async_copy(a_scr, a_hbm.at[pl.ds(pl.multiple_of(i * tm, tm), tm), :], a_sem)

        @pl.when(t > 0)
        def _():
            a_out(t - 1).wait()

        xv = (h1_ref[...]
              + jnp.dot(at_ref[...], wo_ref[0:DA, :], preferred_element_type=F32)
              + jnp.dot(cv_ref[...], wo_ref[DA:D, :], preferred_element_type=F32))
        h2_ref[...] = xv
        r = lax.rsqrt(jnp.mean(xv * xv, axis=-1, keepdims=True) + EPS)
        n_ref[...] = (xv * r * gain_ref[...]).astype(BF16)
        for c in range(FF // FC):
            cols = slice(c * FC, (c + 1) * FC)
            nb = n_ref[...]
            g = lax.dot_general(nb, wg_ref[cols, :], NT, preferred_element_type=F32)
            u = lax.dot_general(nb, wu_ref[cols, :], NT, preferred_element_type=F32)
            g_ref[:, cols] = g.astype(BF16)
            u_ref[:, cols] = u.astype(BF16)
            a_scr[:, cols] = (g * _sigmoid(g) * u).astype(BF16)
        a_out(t).start()
        e = h2_ref[...] + 0.5 * jnp.dot(a_scr[...], wd_ref[...], preferred_element_type=F32) - t_ref[...]
        dout = e * (1.0 / D)
        dout_ref[...] = dout
        dyb_ref[...] = (0.5 * dout).astype(BF16)
        sq_ref[...] = jnp.sum(e * e, axis=0, keepdims=True)[None]

        @pl.when(t == nt - 1)
        def _():
            a_out(t).wait()

    row = pl.BlockSpec((tm, D), lambda t: (t, 0))
    half = pl.BlockSpec((tm, DA), lambda t: (t, 0))
    wide = pl.BlockSpec((tm, FF), lambda t: (t, 0))
    outs, _ = _pallas(
        body, (h1, attn, conv, wout, gain, wg, wu, wd, target), grid=(nt,),
        in_specs=[row, half, half, _resident((D, D)), _resident((1, D)), _resident((FF, D)), _resident((FF, D)),
                  _resident((FF, D)), row],
        out_specs=[row, row, wide, wide, row, row, pl.BlockSpec((1, 1, D), lambda t: (t, 0, 0)), HBM],
        out_shape=[jax.ShapeDtypeStruct((T, D), F32), jax.ShapeDtypeStruct((T, D), BF16)]
                  + [jax.ShapeDtypeStruct((T, FF), BF16)] * 2
                  + [jax.ShapeDtypeStruct((T, D), F32), jax.ShapeDtypeStruct((T, D), BF16),
                     jax.ShapeDtypeStruct((nt, 1, D), F32), jax.ShapeDtypeStruct((T, FF), BF16)],
        scratch_shapes=[pltpu.VMEM((tm, FF), BF16), pltpu.SemaphoreType.DMA(())],
        sem=("arbitrary",), name=name)
    return outs


def _ffn_gate_up(x, gain, wg, wu, name, carry=None):
    T = x.shape[0]
    tm = 512

    def body(x_ref, gain_ref, wg_ref, wu_ref, n_ref, g_ref, u_ref, a_ref):
        xv = x_ref[...]
        r = lax.rsqrt(jnp.mean(xv * xv, axis=-1, keepdims=True) + EPS)
        n_ref[...] = (xv * r * gain_ref[...]).astype(BF16)
        for c in range(FF // FC):
            cols = slice(c * FC, (c + 1) * FC)
            nb = n_ref[...]
            g = lax.dot_general(nb, wg_ref[cols, :], NT, preferred_element_type=F32)
            u = lax.dot_general(nb, wu_ref[cols, :], NT, preferred_element_type=F32)
            g_ref[:, cols] = g.astype(BF16)
            u_ref[:, cols] = u.astype(BF16)
            a_ref[:, cols] = (g * _sigmoid(g) * u).astype(BF16)

    row = pl.BlockSpec((tm, D), lambda t: (t, 0))
    wide = pl.BlockSpec((tm, FF), lambda t: (t, 0))
    return _pallas(
        body, (x, gain, wg, wu), grid=(T // tm,),
        in_specs=[row, _resident((1, D)), _resident((FF, D)), _resident((FF, D))],
        out_specs=[row, wide, wide, wide],
        out_shape=[jax.ShapeDtypeStruct((T, D), BF16)] + [jax.ShapeDtypeStruct((T, FF), BF16)] * 3,
        scratch_shapes=[], sem=("parallel",), name=name, carry=carry)


def _ffn_down_mix_in(x, a, wd, gain, win, name):
    T = x.shape[0]
    tm = 512

    def body(x_ref, a_ref, wd_ref, gain_ref, win_ref, h_ref, u_ref, n_ref):
        hv = x_ref[...] + 0.5 * jnp.dot(a_ref[...], wd_ref[...], preferred_element_type=F32)
        h_ref[...] = hv
        r = lax.rsqrt(jnp.mean(hv * hv, axis=-1, keepdims=True) + EPS)
        n_ref[...] = (hv * r * gain_ref[...]).astype(BF16)
        u_ref[...] = lax.dot_general(n_ref[...], win_ref[...], NT, preferred_element_type=F32).astype(BF16)

    row = pl.BlockSpec((tm, D), lambda t: (t, 0))
    wide = pl.BlockSpec((tm, FF), lambda t: (t, 0))
    outs, _ = _pallas(
        body, (x, a, wd, gain, win), grid=(T // tm,),
        in_specs=[row, wide, _resident((FF, D)), _resident((1, D)), _resident((DIN, D))],
        out_specs=[row, pl.BlockSpec((tm, DIN), lambda t: (t, 0)), row],
        out_shape=[jax.ShapeDtypeStruct((T, D), F32), jax.ShapeDtypeStruct((T, DIN), BF16),
                   jax.ShapeDtypeStruct((T, D), BF16)],
        scratch_shapes=[], sem=("parallel",), name=name)
    return outs


def _ffn_bwd_act(dyb, g, u, x, dout, gain, wg, wu, wd, name, carry=None):
    T = x.shape[0]
    tm = 256
    nt = T // tm

    def body(dy_ref, g_ref, u_ref, x_ref, dout_ref, gain_ref, wg_ref, wu_ref, wd_ref,
             dg_ref, du_ref, dx_ref, dgn_ref):
        for c in range(FF // FC):
            cols = slice(c * FC, (c + 1) * FC)
            da = lax.dot_general(dy_ref[...], wd_ref[cols, :], NT, preferred_element_type=F32)
            gv = g_ref[:, cols].astype(F32)
            uv = u_ref[:, cols].astype(F32)
            sg = _sigmoid(gv)
            dg_ref[:, cols] = (da * uv * (sg * (1.0 + gv * (1.0 - sg)))).astype(BF16)
            du_ref[:, cols] = (da * (gv * sg)).astype(BF16)
        dn = (jnp.dot(dg_ref[...], wg_ref[...], preferred_element_type=F32)
              + jnp.dot(du_ref[...], wu_ref[...], preferred_element_type=F32))
        dx, dgain = _rms_bwd_rows(dn, x_ref[...], gain_ref[...])
        dx_ref[...] = dout_ref[...] + dx

        @pl.when(pl.program_id(0) == 0)
        def _():
            dgn_ref[...] = jnp.zeros_like(dgn_ref)

        dgn_ref[...] += dgain[None]

    row = pl.BlockSpec((tm, D), lambda t: (t, 0))
    wide = pl.BlockSpec((tm, FF), lambda t: (t, 0))
    return _pallas(
        body, (dyb, g, u, x, dout, gain, wg, wu, wd), grid=(nt,),
        in_specs=[row, wide, wide, row, row, _resident((1, D)), _resident((FF, D)), _resident((FF, D)),
                  _resident((FF, D))],
        out_specs=[wide, wide, row, pl.BlockSpec((1, 1, D), lambda t: (0, 0, 0))],
        out_shape=[jax.ShapeDtypeStruct((T, FF), BF16)] * 2
                  + [jax.ShapeDtypeStruct((T, D), F32), jax.ShapeDtypeStruct((1, 1, D), F32)],
        scratch_shapes=[], sem=("arbitrary",), name=name, carry=carry)


def _ffn_bwd_w(lhs, rhs, name, carry=None):
    T = rhs.shape[0]
    tf = 256

    def body(l_ref, r_ref, dw_ref):
        dw_ref[...] = lax.dot_general(l_ref[...], r_ref[...], TN, preferred_element_type=F32).astype(BF16)

    (dw,), got = _pallas(
        body, (lhs, rhs), grid=(FF // tf,),
        in_specs=[pl.BlockSpec((T, tf), lambda f: (0, f)), _resident((T, D))],
        out_specs=[pl.BlockSpec((tf, D), lambda f: (f, 0))], out_shape=[jax.ShapeDtypeStruct((FF, D), BF16)],
        scratch_shapes=[], sem=("parallel",), name=name, carry=carry)
    return dw, got


def _rms_bwd_rows(dn, xv, gain):
    r = lax.rsqrt(jnp.mean(xv * xv, axis=-1, keepdims=True) + EPS)
    xhat = xv * r
    dxhat = dn * gain
    dx = r * (dxhat - xhat * jnp.mean(dxhat * xhat, axis=-1, keepdims=True))
    return dx, jnp.sum(dn * xhat, axis=0, keepdims=True)


def _mix_out_bwd(dh, attn, conv, wout):
    T = dh.shape[0]
    tm = 512
    nt = T // tm

    def body(dh_ref, a_ref, c_ref, w_ref, da_ref, dc_ref, dw_ref, acc_scr):
        t = pl.program_id(0)

        @pl.when(t == 0)
        def _():
            acc_scr[...] = jnp.zeros_like(acc_scr)

        dhb = dh_ref[...].astype(BF16)
        dmix = lax.dot_general(dhb, w_ref[...], NT, preferred_element_type=F32)
        da_ref[...] = dmix[:, 0:DA].astype(BF16)
        dc_ref[...] = dmix[:, DA:D].astype(BF16)
        acc_scr[0:DA, :] += lax.dot_general(a_ref[...], dhb, TN, preferred_element_type=F32)
        acc_scr[DA:D, :] += lax.dot_general(c_ref[...], dhb, TN, preferred_element_type=F32)

        @pl.when(t == nt - 1)
        def _():
            dw_ref[...] = acc_scr[...].astype(BF16)

    row = pl.BlockSpec((tm, D), lambda t: (t, 0))
    half = pl.BlockSpec((tm, DA), lambda t: (t, 0))
    full = pl.BlockSpec((D, D), lambda t: (0, 0))
    return pl.pallas_call(
        body, grid=(nt,), in_specs=[row, half, half, full], out_specs=[half, half, full],
        out_shape=[jax.ShapeDtypeStruct((T, DA), BF16)] * 2 + [jax.ShapeDtypeStruct((D, D), BF16)],
        scratch_shapes=[pltpu.VMEM((D, D), F32)],
        compiler_params=_cp(("arbitrary",)), name="mix_out_bwd")(dh, attn, conv, wout)


def _mix_in_bwd(dparts, win, nb, h, dh, gain):
    T = h.shape[0]
    tm = 512
    nt = T // tm

    def body(d0, d1, d2, d3, d4, w_ref, n_ref, h_ref, dh_ref, gain_ref,
             dw_ref, dx_ref, dyb_ref, dg_ref, acc_scr):
        t = pl.program_id(0)

        @pl.when(t == 0)
        def _():
            acc_scr[...] = jnp.zeros_like(acc_scr)

        n = n_ref[...]
        dn = jnp.zeros((tm, D), F32)
        for i, d_ref in enumerate((d0, d1, d2, d3, d4)):
            dv = d_ref[...]
            dn = dn + jnp.dot(dv, w_ref[i * DA:(i + 1) * DA, :], preferred_element_type=F32)
            acc_scr[i * DA:(i + 1) * DA, :] += lax.dot_general(dv, n, TN, preferred_element_type=F32)
        dx, dgain = _rms_bwd_rows(dn, h_ref[...], gain_ref[...])
        tot = dh_ref[...] + dx
        dx_ref[...] = tot
        dyb_ref[...] = (0.5 * tot).astype(BF16)
        dg_ref[...] = dgain[None]

        @pl.when(t == nt - 1)
        def _():
            dw_ref[...] = acc_scr[...].astype(BF16)

    row = pl.BlockSpec((tm, D), lambda t: (t, 0))
    half = pl.BlockSpec((tm, DA), lambda t: (t, 0))
    full = pl.BlockSpec((DIN, D), lambda t: (0, 0))
    return pl.pallas_call(
        body, grid=(nt,),
        in_specs=[half] * 5 + [full, row, row, row, pl.BlockSpec((1, D), lambda t: (0, 0))],
        out_specs=[full, row, row, pl.BlockSpec((1, 1, D), lambda t: (t, 0, 0))],
        out_shape=[jax.ShapeDtypeStruct((DIN, D), BF16), jax.ShapeDtypeStruct((T, D), F32),
                   jax.ShapeDtypeStruct((T, D), BF16), jax.ShapeDtypeStruct((nt, 1, D), F32)],
        scratch_shapes=[pltpu.VMEM((DIN, D), F32)],
        compiler_params=_cp(("arbitrary",)), name="mix_in_bwd")(*dparts, win, nb, h, dh, gain)


def _head_masks():
    lane = lax.broadcasted_iota(jnp.int32, (1, 2 * HD), 1)
    m0 = lane < HD
    return m0, jnp.logical_not(m0)


def _stack_heads(v, m0, m1):
    z = jnp.zeros_like(v)
    return jnp.concatenate([jnp.where(m0, v, z), jnp.where(m1, v, z)], axis=0)


def _unstack_heads(v2, m0):
    return jnp.where(m0, v2[0:BLK], v2[BLK:2 * BLK])


def _head_sums(xv):
    ri = lax.broadcasted_iota(jnp.int32, (2 * HD, 2 * HD), 0)
    ci = lax.broadcasted_iota(jnp.int32, (2 * HD, 2 * HD), 1)
    ones = jnp.where((ri < HD) == (ci < HD), 1.0, 0.0).astype(BF16)
    hi = xv.astype(BF16)
    lo = (xv - hi.astype(F32)).astype(BF16)
    return (jnp.dot(hi, ones, preferred_element_type=F32) + jnp.dot(lo, ones, preferred_element_type=F32))


def _head_rms(xv):
    return lax.rsqrt(_head_sums(xv * xv) * (1.0 / HD) + EPS)


def _band_mask(first):
    qi = lax.broadcasted_iota(jnp.int32, (BLK, 2 * BLK), 0)
    ci = lax.broadcasted_iota(jnp.int32, (BLK, 2 * BLK), 1)
    band = (ci >= qi) & (ci <= qi + BLK)
    return band & ((ci >= BLK) | jnp.logical_not(first))


def _block_rows(j, d, seg):
    r, n = j // seg, j % seg
    start = r + (d * BLK) * n
    first = n == 0
    prev = jnp.where(first, start, start - d * BLK)
    return pl.ds(start, BLK, stride=d), pl.ds(prev, BLK, stride=d), first


def _block_keys(refs, cur, prev, first, single):
    if single:
        qi = lax.broadcasted_iota(jnp.int32, (BLK, BLK), 0)
        ci = lax.broadcasted_iota(jnp.int32, (BLK, BLK), 1)
        return [r[cur, :].astype(BF16) for r in refs], ci <= qi
    return ([jnp.concatenate([r[prev, :], r[cur, :]], axis=0).astype(BF16) for r in refs], _band_mask(first))


def _attn_fwd(u, qg2, kg2, B, S, carry=None):
    T = B * S
    NB = S // BLK
    scale = HD ** -0.5

    def body(q_ref, k_ref, v_ref, qg_ref, kg_ref, o_ref, lse_ref, qn, kn, vn, os_, ls_):
        m0, m1 = _head_masks()
        qv = q_ref[...].astype(F32)
        qn[...] = qv * _head_rms(qv) * (qg_ref[...] * scale)
        kv = k_ref[...].astype(F32)
        kn[...] = kv * _head_rms(kv) * kg_ref[...]
        vn[...] = v_ref[...].astype(F32)

        for i, d in enumerate(DILS):
            seg = NB // d

            def blk(j, c, i=i, d=d, seg=seg):
                cur, prev, first = _block_rows(j, d, seg)
                q2 = _stack_heads(qn[cur, :].astype(BF16), m0, m1)
                (kk, vv), mask = _block_keys((kn, vn), cur, prev, first, False)
                s = lax.dot_general(q2, kk, NT, preferred_element_type=F32)
                s = jnp.where(jnp.concatenate([mask, mask], axis=0), s, -1e30)
                mx = jnp.max(s, axis=-1, keepdims=True)
                p = jnp.exp(s - mx)
                l = jnp.sum(p, axis=-1, keepdims=True)
                o2 = jnp.dot((p * (1.0 / l)).astype(BF16), vv, preferred_element_type=F32)
                os_[i, cur, :] = _unstack_heads(o2, m0)
                ls_[i, cur, :] = _unstack_heads(mx + jnp.log(l), m0)
                return c

            lax.fori_loop(0, NB, blk, 0, unroll=8)

        def comb(c, carry):
            rows = pl.ds(pl.multiple_of(c * 256, 256), 256)
            l0, l1, l2 = ls_[0, rows, :], ls_[1, rows, :], ls_[2, rows, :]
            mx = jnp.maximum(jnp.maximum(l0, l1), l2)
            e0, e1, e2 = jnp.exp(l0 - mx), jnp.exp(l1 - mx), jnp.exp(l2 - mx)
            tot = e0 + e1 + e2
            inv = 1.0 / tot
            o = (e0 * os_[0, rows, :] + e1 * os_[1, rows, :] + e2 * os_[2, rows, :]) * inv
            o_ref[rows, :] = o.astype(BF16)
            lse_ref[rows, :] = mx + jnp.log(tot)
            return carry

        lax.fori_loop(0, S // 256, comb, 0)

    pair = 2 * HD
    blk_spec = lambda off: pl.BlockSpec((S, pair), lambda b, p, off=off: (b, off + p))
    gspec = pl.BlockSpec((1, pair), lambda b, p: (0, 0))
    return _pallas(
        body, (u, u, u, qg2, kg2), grid=(B, DA // pair),
        in_specs=[blk_spec(0), blk_spec(DA // pair), blk_spec(2 * DA // pair), gspec, gspec],
        out_specs=[blk_spec(0), blk_spec(0)],
        out_shape=[jax.ShapeDtypeStruct((T, DA), BF16), jax.ShapeDtypeStruct((T, DA), F32)],
        scratch_shapes=[pltpu.VMEM((S, pair), F32)] * 3 + [pltpu.VMEM((3, S, pair), F32)] * 2,
        sem=("parallel", "parallel"), name="attn_fwd", carry=carry)


def _attn_bwd(u, attn, dattn, lse, qg2, kg2, B, S, carry=None):
    T = B * S
    NB = S // BLK
    scale = HD ** -0.5
    pair = 2 * HD

    def body(q_ref, k_ref, v_ref, o_ref, do_ref, lse_ref, qg_ref, kg_ref,
             dq_ref, dk_ref, dv_ref, dgn_ref,
             qn, kn, vn, don, ldl, accq, acck, accv, rq, rk):
        m0, m1 = _head_masks()
        lane = lax.broadcasted_iota(jnp.int32, (1, pair), 1)
        qv = q_ref[...].astype(F32)
        rq[...] = _head_rms(qv)
        qn[...] = qv * rq[...] * (qg_ref[...] * scale)
        kv = k_ref[...].astype(F32)
        rk[...] = _head_rms(kv)
        kn[...] = kv * rk[...] * kg_ref[...]
        vn[...] = v_ref[...].astype(F32)
        dov = do_ref[...].astype(F32)
        don[...] = dov
        ldl[...] = jnp.where((lane % HD) < HD // 2, lse_ref[...], _head_sums(dov * o_ref[...].astype(F32)))

        for i, d in enumerate(DILS):
            seg = NB // d

            def blk(j, c, i=i, d=d, seg=seg):
                cur, prev, first = _block_rows(j, d, seg)
                q2 = _stack_heads(qn[cur, :].astype(BF16), m0, m1)
                do2 = _stack_heads(don[cur, :].astype(BF16), m0, m1)
                (kk, vv), mask = _block_keys((kn, vn), cur, prev, first, seg == 1)
                ldv = ldl[cur, :]
                lse2 = jnp.concatenate([ldv[:, 0:1], ldv[:, HD:HD + 1]], axis=0)
                dl2 = jnp.concatenate([ldv[:, HD // 2:HD // 2 + 1], ldv[:, HD + HD // 2:HD + HD // 2 + 1]], axis=0)
                s = lax.dot_general(q2, kk, NT, preferred_element_type=F32)
                p = jnp.where(jnp.concatenate([mask, mask], axis=0), jnp.exp(s - lse2), 0.0)
                dp = lax.dot_general(do2, vv, NT, preferred_element_type=F32)
                ds = (p * (dp - dl2)).astype(BF16)
                dq_acc = _unstack_heads(jnp.dot(ds, kk, preferred_element_type=F32), m0)
                dk_acc = lax.dot_general(ds, q2, TN, preferred_element_type=F32)
                dv_acc = lax.dot_general(p.astype(BF16), do2, TN, preferred_element_type=F32)
                if i == 0:
                    accq[cur, :] = dq_acc
                    acck[cur, :] = dk_acc[BLK:2 * BLK]
                    accv[cur, :] = dv_acc[BLK:2 * BLK]
                    acck[prev, :] += dk_acc[0:BLK]
                    accv[prev, :] += dv_acc[0:BLK]
                elif seg == 1:
                    accq[cur, :] += dq_acc
                    acck[cur, :] += dk_acc
                    accv[cur, :] += dv_acc
                else:
                    accq[cur, :] += dq_acc
                    acck[prev, :] += dk_acc[0:BLK]
                    acck[cur, :] += dk_acc[BLK:2 * BLK]
                    accv[prev, :] += dv_acc[0:BLK]
                    accv[cur, :] += dv_acc[BLK:2 * BLK]
                return c

            lax.fori_loop(0, NB, blk, 0, unroll=8)

        def norm_bwd(x_ref, r_ref, dn, gain):
            r = r_ref[...]
            xhat = x_ref[...].astype(F32) * r
            dxhat = dn * gain
            dx = r * (dxhat - xhat * (_head_sums(dxhat * xhat) * (1.0 / HD)))
            return dx, jnp.sum(dn * xhat, axis=0, keepdims=True)

        dq, dgq = norm_bwd(q_ref, rq, accq[...], qg_ref[...] * scale)
        dk, dgk = norm_bwd(k_ref, rk, acck[...], kg_ref[...])
        dq_ref[...] = dq.astype(BF16)
        dk_ref[...] = dk.astype(BF16)
        dv_ref[...] = accv[...].astype(BF16)
        dgn_ref[...] = jnp.concatenate([dgq * scale, dgk, jnp.zeros((6, pair), F32)], axis=0)[None]

    blk_spec = lambda off: pl.BlockSpec((S, pair), lambda b, p, off=off: (b, off + p))
    gspec = pl.BlockSpec((1, pair), lambda b, p: (0, 0))
    np_ = DA // pair
    return _pallas(
        body, (u, u, u, attn, dattn, lse, qg2, kg2), grid=(B, np_),
        in_specs=[blk_spec(0), blk_spec(np_), blk_spec(2 * np_), blk_spec(0), blk_spec(0), blk_spec(0),
                  gspec, gspec],
        out_specs=[blk_spec(0), blk_spec(0), blk_spec(0),
                   pl.BlockSpec((1, 8, pair), lambda b, p: (b * np_ + p, 0, 0))],
        out_shape=[jax.ShapeDtypeStruct((T, DA), BF16)] * 3 + [jax.ShapeDtypeStruct((B * np_, 8, pair), F32)],
        scratch_shapes=[pltpu.VMEM((S, pair), F32)] * 10,
        sem=("parallel", "parallel"), name="attn_bwd", carry=carry)


CT = 32
CPAD = 32


def _shifted(win, offsets):
    rolled, out = {}, {}
    n = win.shape[0]
    for o in offsets:
        sub = o % 8
        if sub not in rolled:
            rolled[sub] = win if sub == 0 else pltpu.roll(win, n - sub, 0)
        out[o] = rolled[sub][o - sub:o - sub + CT, :]
    return out


def _ln_fwd(y, g, b):
    mu = jnp.mean(y, axis=-1, keepdims=True)
    yc = y - mu
    rstd = lax.rsqrt(jnp.mean(yc * yc, axis=-1, keepdims=True) + EPS)
    xhat = yc * rstd
    return xhat, rstd, xhat * g + b


def _fill_glu(ca_ref, cg_ref, glu, S):
    glu[pl.ds(0, CPAD), :] = jnp.zeros((CPAD, DC), F32)

    def fill(i, c):
        rows = pl.ds(pl.multiple_of(i * 256, 256), 256)
        a = ca_ref[rows, :].astype(F32)
        gt = cg_ref[rows, :].astype(F32)
        glu[pl.ds(pl.multiple_of(CPAD + i * 256, CT), 256), :] = a * _sigmoid(gt)
        return c

    lax.fori_loop(0, S // 256, fill, 0)


def _conv_fwd(u, cw, cb, lg, lb, B, S):
    T = B * S

    def body(ca_ref, cg_ref, w_ref, b_ref, lg_ref, lb_ref, o_ref, y_ref, glu):
        _fill_glu(ca_ref, cg_ref, glu, S)

        def step(i, c):
            t0 = pl.multiple_of(i * CT, CT)
            win = glu[pl.ds(t0, 2 * CT), :]
            acc = jnp.zeros((CT, DC), F32) + b_ref[...]
            taps = _shifted(win, [k + 2 for k in range(CK)])
            for k in range(CK):
                acc = acc + taps[k + 2] * w_ref[k:k + 1, :]
            y_ref[pl.ds(t0, CT), :] = acc
            _, _, z = _ln_fwd(acc, lg_ref[...], lb_ref[...])
            o_ref[pl.ds(t0, CT), :] = (z * _sigmoid(z)).astype(BF16)
            return c

        lax.fori_loop(0, S // CT, step, 0, unroll=4)

    vec = pl.BlockSpec((1, DC), lambda b: (0, 0))
    return pl.pallas_call(
        body, grid=(B,),
        in_specs=[pl.BlockSpec((S, DC), lambda b: (b, 3)), pl.BlockSpec((S, DC), lambda b: (b, 4)),
                  pl.BlockSpec((CT, DC), lambda b: (0, 0)), vec, vec, vec],
        out_specs=[pl.BlockSpec((S, DC), lambda b: (b, 0))] * 2,
        out_shape=[jax.ShapeDtypeStruct((T, DC), BF16), jax.ShapeDtypeStruct((T, DC), F32)],
        scratch_shapes=[pltpu.VMEM((CPAD + S, DC), F32)],
        compiler_params=_cp(("parallel",)), name="conv_fwd")(u, u, cw, cb, lg, lb)


def _conv_bwd(u, y, dconv, cw, lg, lb, B, S):
    T = B * S

    def body(ca_ref, cg_ref, y_ref, dc_ref, w_ref, lg_ref, lb_ref,
             dca_ref, dcg_ref, dw_ref, ds_ref, glu, dyp, dwacc):
        _fill_glu(ca_ref, cg_ref, glu, S)
        dyp[pl.ds(S, CPAD), :] = jnp.zeros((CPAD, DC), F32)
        lgv, lbv = lg_ref[...], lb_ref[...]

        def sum8(v):
            return functools.reduce(jnp.add, [v[r:r + 8] for r in range(0, v.shape[0], 8)])

        P1 = 4 * CT

        def p1(i, carry):
            sb, sg, sl = carry
            t0 = pl.multiple_of(i * P1, P1)
            xhat, rstd, z = _ln_fwd(y_ref[pl.ds(t0, P1), :], lgv, lbv)
            sz = _sigmoid(z)
            dz = dc_ref[pl.ds(t0, P1), :].astype(F32) * (sz * (1.0 + z * (1.0 - sz)))
            dxhat = dz * lgv
            dy = rstd * (dxhat - jnp.mean(dxhat, axis=-1, keepdims=True)
                         - xhat * jnp.mean(dxhat * xhat, axis=-1, keepdims=True))
            dyp[pl.ds(t0, P1), :] = dy
            return sb + sum8(dy), sg + sum8(dz * xhat), sl + sum8(dz)

        z8 = jnp.zeros((8, DC), F32)
        sb, sg, sl = lax.fori_loop(0, S // P1, p1, (z8, z8, z8))
        rs = lambda v: jnp.sum(v, axis=0, keepdims=True)
        ds_ref[...] = jnp.concatenate([rs(sb), rs(sg), rs(sl), jnp.zeros((5, DC), F32)], axis=0)[None]

        def p2(i, c):
            t0 = pl.multiple_of(i * CT, CT)
            win = dyp[pl.ds(t0, 2 * CT), :]
            acc = jnp.zeros((CT, DC), F32)
            taps = _shifted(win, [30 - k for k in range(CK)])
            for k in range(CK):
                acc = acc + taps[30 - k] * w_ref[k:k + 1, :]
            a = ca_ref[pl.ds(t0, CT), :].astype(F32)
            sgt = _sigmoid(cg_ref[pl.ds(t0, CT), :].astype(F32))
            dca_ref[pl.ds(t0, CT), :] = (acc * sgt).astype(BF16)
            dcg_ref[pl.ds(t0, CT), :] = (acc * a * sgt * (1.0 - sgt)).astype(BF16)
            return c

        lax.fori_loop(0, S // CT, p2, 0)

        dwacc[...] = jnp.zeros_like(dwacc)

        def p3(i, c):
            t0 = pl.multiple_of(i * CT, CT)
            win = glu[pl.ds(t0, 2 * CT), :]
            dy = dyp[pl.ds(t0, CT), :]
            for k in range(CK):
                dwacc[k] += sum8(dy * win[k + 2:k + 2 + CT, :])
            return c

        lax.fori_loop(0, S // CT, p3, 0)
        dw_ref[...] = jnp.sum(dwacc[...], axis=1)[None]

    vec = pl.BlockSpec((1, DC), lambda b: (0, 0))
    seq = pl.BlockSpec((S, DC), lambda b: (b, 0))
    return pl.pallas_call(
        body, grid=(B,),
        in_specs=[pl.BlockSpec((S, DC), lambda b: (b, 3)), pl.BlockSpec((S, DC), lambda b: (b, 4)),
                  seq, seq, pl.BlockSpec((CT, DC), lambda b: (0, 0)), vec, vec],
        out_specs=[seq, seq, pl.BlockSpec((1, CT, DC), lambda b: (b, 0, 0)),
                   pl.BlockSpec((1, 8, DC), lambda b: (b, 0, 0))],
        out_shape=[jax.ShapeDtypeStruct((T, DC), BF16)] * 2
                  + [jax.ShapeDtypeStruct((B, CT, DC), F32), jax.ShapeDtypeStruct((B, 8, DC), F32)],
        scratch_shapes=[pltpu.VMEM((CPAD + S, DC), F32), pltpu.VMEM((S + CPAD, DC), F32),
                        pltpu.VMEM((CT, 8, DC), F32)],
        compiler_params=_cp(("parallel",)), name="conv_bwd")(u, u, y, dconv, cw, lg, lb)


def _local_step(x, target, norms, W, B, S, comm=None):
    qg2 = jnp.concatenate([norms["q_norm"], norms["q_norm"]], axis=1)
    kg2 = jnp.concatenate([norms["k_norm"], norms["k_norm"]], axis=1)
    cw = jnp.concatenate([W["conv_w"], jnp.zeros((1, DC), F32)], axis=0)

    W = dict(W)
    (n1, g1, u1, act1), got = _ffn_gate_up(x, norms["ffn1_norm"], W["wg1"], W["wu1"], "ffn1_gate_up",
                                           carry=comm.gathers["down_in"] if comm else None)
    if comm:
        W.update(comm.gathered("down_in", got))
    h1, u, n2 = _ffn_down_mix_in(x, act1, W["wd1"], norms["mix_norm"], W["win"], "ffn1_down_mix_in")
    (attn, lse), got = _attn_fwd(u, qg2, kg2, B, S, carry=comm.gathers["ffn2"] if comm else None)
    if comm:
        W = dict(W, **comm.gathered("ffn2", got))
    conv, y = _conv_fwd(u, cw, norms["conv_b"], norms["conv_ln_g"], norms["conv_ln_b"], B, S)
    h2, n3, g2, u2, dout, dyb, sq, act2 = _mix_out_ffn_loss(h1, attn, conv, W["wout"], norms["ffn2_norm"],
                                                            W["wg2"], W["wu2"], W["wd2"], target, "ffn2_fwd")
    loss = (0.5 / D) * jnp.sum(sq)

    (dg2, du2, dh2, dgn_ffn2), _ = _ffn_bwd_act(dyb, g2, u2, h2, dout, norms["ffn2_norm"],
                                               W["wg2"], W["wu2"], W["wd2"], "ffn2_bwd_act")
    dwd2, _ = _ffn_bwd_w(act2, dyb, "ffn2_bwd_wd")
    dwg2, _ = _ffn_bwd_w(dg2, n3, "ffn2_bwd_wg")
    dwu2, _ = _ffn_bwd_w(du2, n3, "ffn2_bwd_wu")
    dattn, dconv, dwout = _mix_out_bwd(dh2, attn, conv, W["wout"])
    carry = comm.reduce_start({"wg2": dwg2, "wu2": dwu2, "wd2": dwd2, "wout": dwout}) if comm else None
    (dq, dk, dv, dgn_qk), got = _attn_bwd(u, attn, dattn, lse, qg2, kg2, B, S, carry=carry)
    if comm:
        comm.reduce_done(carry, got)
    dca, dcg, dcw, dcs = _conv_bwd(u, y, dconv, cw, norms["conv_ln_g"], norms["conv_ln_b"], B, S)
    dwin, dh1, dyb1, dgn_mix = _mix_in_bwd((dq, dk, dv, dca, dcg), W["win"], n2, h1, dh2, norms["mix_norm"])
    carry = comm.reduce_start({"win": dwin}) if comm else None
    dwd1, got = _ffn_bwd_w(act1, dyb1, "ffn1_bwd_wd", carry=carry)
    if comm:
        comm.reduce_done(carry, got)
    carry = comm.update_start(("wg2", "wu2", "wd2", "wout", "win")) if comm else None
    (dg1, du1, gx, dgn_ffn1), got = _ffn_bwd_act(dyb1, g1, u1, x, dh1, norms["ffn1_norm"],
                                                W["wg1"], W["wu1"], W["wd1"], "ffn1_bwd_act", carry=carry)
    if comm:
        comm.update_done(carry, got)
        carry = comm.reduce_start({"wd1": dwd1})
    dwg1, got = _ffn_bwd_w(dg1, n1, "ffn1_bwd_wg", carry=carry)
    if comm:
        comm.reduce_done(carry, got)
        carry = comm.reduce_start({"wg1": dwg1})
    dwu1, got = _ffn_bwd_w(du1, n1, "ffn1_bwd_wu", carry=carry)
    if comm:
        comm.reduce_done(carry, got)
        comm.last = comm.reduce_start({"wu1": dwu1})

    qk = jnp.sum(dgn_qk, axis=0)
    cs = jnp.sum(dcs, axis=0)
    small = {
        "ffn1_norm": jnp.sum(dgn_ffn1, axis=0),
        "mix_norm": jnp.sum(dgn_mix, axis=0),
        "q_norm": qk[0:1, 0:HD] + qk[0:1, HD:2 * HD],
        "k_norm": qk[1:2, 0:HD] + qk[1:2, HD:2 * HD],
        "conv_w": jnp.sum(dcw, axis=0)[0:CK],
        "conv_b": cs[0:1],
        "conv_ln_g": cs[1:2],
        "conv_ln_b": cs[2:3],
        "ffn2_norm": jnp.sum(dgn_ffn2, axis=0),
    }
    big = {"wg1": dwg1, "wu1": dwu1, "wd1": dwd1, "win": dwin, "wout": dwout,
           "wg2": dwg2, "wu2": dwu2, "wd2": dwd2}
    return loss, gx, big, small


HBM = pl.BlockSpec(memory_space=pltpu.HBM)
VMEM = pl.BlockSpec(memory_space=pltpu.VMEM)


def _place():
    return lax.axis_index("x"), lax.axis_index("y"), lax.axis_index("c")


class _GatherCarry:
    def __init__(self, shards, mid_at=0.5):
        nt = len(shards)
        self.mid_at = mid_at
        self.shards = shards
        self.in_arrays = [s for s, _ in shards]
        self.in_specs = [VMEM] * nt
        self.out_shape = [jax.ShapeDtypeStruct((NDEV * s.shape[0], s.shape[1]), dt) for s, dt in shards]
        self.out_specs = [HBM] * nt
        self.scratch = ([pltpu.VMEM(s.shape, dt) for s, dt in shards]
                        + [pltpu.SemaphoreType.DMA((nt, 7)), pltpu.SemaphoreType.DMA((nt, 7)),
                           pltpu.SemaphoreType.DMA((nt,))])

    def _copies(self, outs, scr):
        nt = len(self.shards)
        stages = scr[:nt]
        send_sems, recv_sems, local_sems = scr[nt:]
        x, y, c = _place()
        me, sibling = (x, y, c), (x, y, 1 - c)
        xn, yn, diag = (1 - x, y, c), (x, 1 - y, c), (1 - x, 1 - y, c)
        via = (x ^ c, y ^ (1 - c), c)
        onto = (x ^ (1 - c), y ^ c, c)

        def rows(t, px, py, pc):
            r = self.shards[t][0].shape[0]
            return outs[t].at[pl.ds((4 * px + 2 * py + pc) * r, r), :]

        def copy(t, k, block, to, src=None):
            return pltpu.make_async_remote_copy(
                src_ref=rows(t, *block) if src is None else src, dst_ref=rows(t, *block),
                send_sem=send_sems.at[t, k], recv_sem=recv_sems.at[t, k],
                device_id=to, device_id_type=MESH)

        sib = lambda b: (b[0], b[1], 1 - c)
        return dict(
            local=[pltpu.make_async_copy(stages[t], rows(t, *me), local_sems.at[t]) for t in range(nt)],
            own=[[copy(t, 0, me, sibling, src=stages[t]), copy(t, 1, me, xn, src=stages[t]),
                  copy(t, 2, me, yn, src=stages[t])] for t in range(nt)],
            relay=[copy(t, 3, via, onto) for t in range(nt)],
            down=[[copy(t, 4, xn, sibling), copy(t, 5, yn, sibling)] for t in range(nt)],
            down_diag=[copy(t, 6, diag, sibling) for t in range(nt)],
            got_xy=[[copy(t, 1, xn, me), copy(t, 2, yn, me)] for t in range(nt)],
            got_diag=[copy(t, 3, diag, me) for t in range(nt)],
            got_sib=[[copy(t, 0, sibling, me), copy(t, 4, sib(xn), me), copy(t, 5, sib(yn), me),
                      copy(t, 6, sib(diag), me)] for t in range(nt)])

    def start(self, ins, outs, scr):
        cps = self._copies(outs, scr)
        for t, (_, dt) in enumerate(self.shards):
            scr[t][...] = ins[t][...].astype(dt)
            for cp in [cps["local"][t]] + cps["own"][t]:
                cp.start()

    def stages(self):
        sizes = [s.size * jnp.dtype(dt).itemsize for s, dt in self.shards]
        done = [sum(sizes[:t + 1]) / sum(sizes) for t in range(len(sizes))]
        return [(self.mid_at * f, functools.partial(self.mid, t)) for t, f in enumerate(done)]

    def mid(self, t, ins, outs, scr):
        cps = self._copies(outs, scr)
        for cp in cps["got_xy"][t]:
            cp.wait_recv()
        for cp in [cps["relay"][t]] + cps["down"][t]:
            cp.start()

    def finish(self, ins, outs, scr):
        cps = self._copies(outs, scr)
        for t in range(len(self.shards)):
            cps["got_diag"][t].wait_recv()
            cps["down_diag"][t].start()
        for t in range(len(self.shards)):
            for cp in cps["got_sib"][t]:
                cp.wait_recv()
            for cp in cps["own"][t] + [cps["relay"][t]] + cps["down"][t] + [cps["down_diag"][t]]:
                cp.wait_send()
            cps["local"][t].wait()


def _run_carry(carry, name):
    def body(*refs):
        n_in, n_out = len(carry.in_arrays), len(carry.out_shape)
        ins, outs, scr = refs[:n_in], refs[n_in:n_in + n_out], refs[n_in + n_out:]
        carry.start(ins, outs, scr)
        for _, stage in carry.stages():
            stage(ins, outs, scr)
        carry.finish(ins, outs, scr)

    return pl.pallas_call(
        body, in_specs=carry.in_specs, out_specs=carry.out_specs, out_shape=carry.out_shape,
        scratch_shapes=carry.scratch, compiler_params=pltpu.CompilerParams(vmem_limit_bytes=VMEM_LIMIT),
        name=name)(*carry.in_arrays)


class _ExchangeCarry:
    def __init__(self, names, grads, mid_at=0.5):
        nt = len(grads)
        self.mid_at = mid_at
        self.names = names
        self.in_arrays = [g.reshape(4, 2, g.shape[0] // NDEV, g.shape[1]) for g in grads]
        self.in_specs = [HBM] * nt
        blocks = [g.shape[2:] for g in self.in_arrays]
        self.out_shape = [jax.ShapeDtypeStruct((3,) + b, BF16) for b in blocks]
        self.out_specs = [HBM] * nt
        self.scratch = ([pltpu.VMEM((4,) + b, BF16) for b in blocks] * 2 + [pltpu.VMEM(b, BF16) for b in blocks]
                        + [pltpu.SemaphoreType.DMA((nt, 3)), pltpu.SemaphoreType.DMA((nt, 3))]
                        + [pltpu.SemaphoreType.DMA((nt,))] * 4)

    def _copies(self, ins, outs, scr):
        nt = len(ins)
        theirs, own, relayed = scr[:nt], scr[nt:2 * nt], scr[2 * nt:3 * nt]
        send_sems, recv_sems, keep_sems, load_sems, swap_send, swap_recv = scr[3 * nt:]
        x, y, c = _place()
        q = lambda cx, cy: 2 * cx + cy
        near, far = (x ^ c, y ^ (1 - c)), (x ^ (1 - c), y ^ c)

        def remote(t, k, src, dst, chip):
            return pltpu.make_async_remote_copy(
                src_ref=src, dst_ref=dst, send_sem=send_sems.at[t, k], recv_sem=recv_sems.at[t, k],
                device_id=(*chip, c), device_id_type=MESH)

        return dict(
            swap=[pltpu.make_async_remote_copy(
                src_ref=ins[t].at[:, 1 - c], dst_ref=theirs[t], send_sem=swap_send.at[t], recv_sem=swap_recv.at[t],
                device_id=(x, y, 1 - c), device_id_type=MESH) for t in range(nt)],
            load=[pltpu.make_async_copy(ins[t].at[:, c], own[t], load_sems.at[t]) for t in range(nt)],
            keep=[pltpu.make_async_copy(own[t].at[q(x, y)], outs[t].at[0], keep_sems.at[t]) for t in range(nt)],
            direct=[remote(t, 0, own[t].at[q(*near)], outs[t].at[1], near) for t in range(nt)],
            relay=[remote(t, 1, own[t].at[q(1 - x, 1 - y)], relayed[t], near) for t in range(nt)],
            merged=[remote(t, 2, own[t].at[q(*far)], outs[t].at[2], far) for t in range(nt)],
            theirs=theirs, own=own, relayed=relayed, far=q(*far))

    EARLY_AT = 0.1

    def stages(self):
        return [(self.EARLY_AT, self.early), (self.mid_at, self.mid)]

    def start(self, ins, outs, scr):
        cps = self._copies(ins, outs, scr)
        for t in range(len(ins)):
            cps["swap"][t].start()
            cps["load"][t].start()

    def early(self, ins, outs, scr):
        cps = self._copies(ins, outs, scr)
        for t in range(len(ins)):
            cps["load"][t].wait()
            cps["swap"][t].wait_recv()
            own, theirs = cps["own"][t], cps["theirs"][t]
            for j in range(4):
                own[j] = (own[j].astype(F32) + theirs[j].astype(F32)).astype(BF16)
            for kind in ("relay", "direct", "keep"):
                cps[kind][t].start()

    def mid(self, ins, outs, scr):
        cps = self._copies(ins, outs, scr)
        for t in range(len(ins)):
            cps["relay"][t].wait_recv()
            own, far = cps["own"][t], cps["far"]
            own[far] = (own[far].astype(F32) + cps["relayed"][t][...].astype(F32)).astype(BF16)
            cps["merged"][t].start()

    def finish(self, ins, outs, scr):
        cps = self._copies(ins, outs, scr)
        for t in range(len(ins)):
            cps["swap"][t].wait_send()
            cps["direct"][t].wait()
            cps["relay"][t].wait_send()
            cps["merged"][t].wait()
            cps["keep"][t].wait()


class _Comm:
    def __init__(self, groups, opt):
        self.names = {tag: list(g) for tag, (g, _) in groups.items()}
        self.gathers = {tag: _GatherCarry(list(g.values()), mid_at) for tag, (g, mid_at) in groups.items()}
        self.reduced = {}
        self.last = None
        self.opt = opt
        self.updated = {}

    def gathered(self, tag, outs):
        return dict(zip(self.names[tag], outs))

    def reduce_start(self, grads):
        names = list(grads)
        return _ExchangeCarry(names, [grads[n] for n in names])

    def reduce_done(self, carry, outs):
        self.reduced.update(zip(carry.names, outs))

    def update_start(self, names):
        w, m, v = zip(*[self.opt[n] for n in names])
        return _AdamWCarry(names, [self.reduced[n] for n in names], w, m, v)

    def update_done(self, carry, outs):
        self.updated.update({n: outs[4 * k:4 * k + 4] for k, n in enumerate(carry.names)})


def _adamw_math(w, g, m, v):
    m = B1 * m + (1.0 - B1) * g
    v = B2 * v + (1.0 - B2) * (g * g)
    m_hat = m / (1.0 - B1 ** STEP)
    v_hat = v / (1.0 - B2 ** STEP)
    delta = -LR * (m_hat / (jnp.sqrt(v_hat) + AEPS) + WD * w)
    return delta, m, v


def _adamw_big(recvs, ws, ms, vs, name):
    nw = len(ws)

    def body(*refs):
        ins, outs = refs[:4 * nw], refs[4 * nw:]
        for k in range(nw):
            @pl.when(pl.program_id(0) // 2 == k)
            def _(k=k):
                r_ref, w_ref, m_ref, v_ref = ins[4 * k:4 * k + 4]
                g_ref, d_ref, mo_ref, vo_ref = outs[4 * k:4 * k + 4]
                g = r_ref[0].astype(F32)
                for q in range(1, 3):
                    g = g + r_ref[q].astype(F32)
                d, mn, vn = _adamw_math(w_ref[...], g, m_ref[...], v_ref[...])
                g_ref[...] = g
                d_ref[...] = d
                mo_ref[...] = mn
                vo_ref[...] = vn

    in_specs, out_specs, out_shape, args = [], [], [], []
    for k, (r, w, m, v) in enumerate(zip(recvs, ws, ms, vs)):
        rows, n = w.shape
        tr = rows // 2
        tile = lambda s, k=k: jnp.clip(s - 2 * k, 0, 1)
        row = pl.BlockSpec((tr, n), lambda s, tile=tile: (tile(s), 0))
        in_specs += [pl.BlockSpec((3, tr, n), lambda s, tile=tile: (0, tile(s), 0)), row, row, row]
        out_specs += [row] * 4
        out_shape += [jax.ShapeDtypeStruct(w.shape, F32)] * 4
        args += [r, w, m, v]
    res = pl.pallas_call(
        body, grid=(2 * nw,), in_specs=in_specs, out_specs=out_specs, out_shape=out_shape,
        compiler_params=_cp(("arbitrary",)), name=name)(*args)
    return [res[4 * k:4 * k + 4] for k in range(nw)]


class _AdamWCarry:
    def __init__(self, names, recvs, ws, ms, vs):
        self.names = names
        nw = len(ws)
        self.halves = [(k, j, w.shape[0] // 2) for k, w in enumerate(ws) for j in range(2)]
        self.in_arrays = [a for quad in zip(recvs, ws, ms, vs) for a in quad]
        self.in_specs = [HBM] * (4 * nw)
        self.out_shape = [jax.ShapeDtypeStruct(w.shape, F32) for w in ws for _ in range(4)]
        self.out_specs = [HBM] * (4 * nw)
        tr, n = max(h[2] for h in self.halves), ws[0].shape[1]
        self.scratch = [pltpu.VMEM((2, 3, tr, n), BF16), pltpu.VMEM((2, 3, tr, n), F32),
                        pltpu.VMEM((2, 4, tr, n), F32), pltpu.SemaphoreType.DMA((2, 4)),
                        pltpu.SemaphoreType.DMA((2, 4))]

    def _copies(self, c, ins, outs, scr):
        rbuf, fbuf, obuf, in_sems, out_sems = scr
        k, j, tr = self.halves[c]
        s, rows = c % 2, pl.ds(j * tr, tr)
        loads = [pltpu.make_async_copy(ins[4 * k].at[:, rows, :], rbuf.at[s, :, pl.ds(0, tr), :], in_sems.at[s, 0])]
        loads += [pltpu.make_async_copy(ins[4 * k + i].at[rows, :], fbuf.at[s, i - 1, pl.ds(0, tr), :],
                                        in_sems.at[s, i]) for i in range(1, 4)]
        stores = [pltpu.make_async_copy(obuf.at[s, q, pl.ds(0, tr), :], outs[4 * k + q].at[rows, :],
                                        out_sems.at[s, q]) for q in range(4)]
        return loads, stores

    def start(self, ins, outs, scr):
        for c in range(2):
            for cp in self._copies(c, ins, outs, scr)[0]:
                cp.start()

    def stages(self):
        n = len(self.halves)
        return [((c + 1) / (n + 1), functools.partial(self.half, c)) for c in range(n)]

    def half(self, c, ins, outs, scr):
        rbuf, fbuf, obuf = scr[:3]
        _, _, tr = self.halves[c]
        s = c % 2
        loads, stores = self._copies(c, ins, outs, scr)
        for cp in loads:
            cp.wait()
        if c >= 2:
            for cp in self._copies(c - 2, ins, outs, scr)[1]:
                cp.wait()
        g = rbuf[s, 0, 0:tr].astype(F32)
        for q in range(1, 3):
            g = g + rbuf[s, q, 0:tr].astype(F32)
        d, mn, vn = _adamw_math(fbuf[s, 0, 0:tr], g, fbuf[s, 1, 0:tr], fbuf[s, 2, 0:tr])
        for q, val in enumerate((g, d, mn, vn)):
            obuf[s, q, 0:tr] = val
        for cp in stores:
            cp.start()
        if c + 2 < len(self.halves):
            for cp in self._copies(c + 2, ins, outs, scr)[0]:
                cp.start()

    def finish(self, ins, outs, scr):
        n = len(self.halves)
        for c in range(max(n - 2, 0), n):
            for cp in self._copies(c, ins, outs, scr)[1]:
                cp.wait()


SMALL_NAMES = ("ffn1_norm", "mix_norm", "ffn2_norm", "conv_b", "conv_ln_g", "conv_ln_b", "q_norm", "k_norm")
SROWS = 16
LOSS_ROW = len(SMALL_NAMES)
CWF = 2


def _small_sums(gs, loss_row, gcw, carry=None):
    ns = len(SMALL_NAMES)
    widths = [g.shape[1] for g in gs]

    def body(*refs):
        it = iter(refs)
        take = lambda n: [next(it) for _ in range(n)]
        g_refs, (loss_ref, gcw_ref) = take(ns), take(2)
        cins = take(len(carry.in_arrays)) if carry else []
        tot_ref, ctot_ref = take(2)
        couts = take(len(carry.out_shape)) if carry else []
        send, slots, cslots, send_sems, recv_sems, csend_sems, crecv_sems = take(7)
        cscr = list(it)
        x, y, c = _place()
        me = 4 * x + 2 * y + c
        send[...] = jnp.zeros_like(send)
        for k in range(ns):
            send[k:k + 1, 0:widths[k]] = g_refs[k][...]
        send[LOSS_ROW:LOSS_ROW + 1, 0:128] = loss_ref[...]
        slots[me] = send[...]
        cslots[me] = gcw_ref[...]
        cps = []
        for k in range(1, NDEV):
            peer = (x ^ ((k >> 2) & 1), y ^ ((k >> 1) & 1), c ^ (k & 1))
            cps.append(pltpu.make_async_remote_copy(
                src_ref=send, dst_ref=slots.at[me], send_sem=send_sems.at[k - 1], recv_sem=recv_sems.at[k - 1],
                device_id=peer, device_id_type=MESH))
            cps.append(pltpu.make_async_remote_copy(
                src_ref=gcw_ref, dst_ref=cslots.at[me], send_sem=csend_sems.at[k - 1],
                recv_sem=crecv_sems.at[k - 1], device_id=peer, device_id_type=MESH))
        for cp in cps:
            cp.start()
        stages = [stage for _, stage in carry.stages()] if carry else []
        if carry:
            carry.start(cins, couts, cscr)
        for stage in stages[:1]:
            stage(cins, couts, cscr)
        for cp in cps:
            cp.wait()
        tot = slots[0]
        ctot = cslots[0, me]
        for j in range(1, NDEV):
            tot = tot + slots[j]
            ctot = ctot + cslots[j, me]
        tot_ref[...] = tot
        ctot_ref[...] = ctot
        for stage in stages[1:]:
            stage(cins, couts, cscr)
        if carry:
            carry.finish(cins, couts, cscr)

    args = [*gs, loss_row, gcw]
    out_shape = [jax.ShapeDtypeStruct((SROWS, D), F32), jax.ShapeDtypeStruct((CWF, D), F32)]
    res = pl.pallas_call(
        body, in_specs=[VMEM] * len(args) + (carry.in_specs if carry else []),
        out_specs=[VMEM] * 2 + (carry.out_specs if carry else []),
        out_shape=out_shape + (carry.out_shape if carry else []),
        scratch_shapes=[pltpu.VMEM((SROWS, D), F32), pltpu.VMEM((NDEV, SROWS, D), F32),
                        pltpu.VMEM((NDEV, NDEV, CWF, D), F32)]
                       + [pltpu.SemaphoreType.DMA((NDEV - 1,))] * 4 + (carry.scratch if carry else []),
        name="small_sums")(*args, *(carry.in_arrays if carry else []))
    return res[0], res[1], res[2:]


def _adamw_small(tot, ctot, ws, ms, vs, wcw, mcw, vcw):
    ns = len(SMALL_NAMES)
    widths = [w.shape[1] for w in ws]

    def body(*refs):
        it = iter(refs)
        take = lambda n: [next(it) for _ in range(n)]
        (tot_ref, ctot_ref), w_refs, m_refs, v_refs = take(2), take(ns), take(ns), take(ns)
        wcw_ref, mcw_ref, vcw_ref = take(3)
        outs = [take(4) for _ in range(ns)]
        cw_outs, (loss_out,) = take(4), take(1)

        def step(g, w_ref, m_ref, v_ref, o):
            d, mn, vn = _adamw_math(w_ref[...], g, m_ref[...], v_ref[...])
            o[0][...], o[1][...], o[2][...], o[3][...] = g, d, mn, vn

        for k in range(ns):
            step(tot_ref[k:k + 1, 0:widths[k]], w_refs[k], m_refs[k], v_refs[k], outs[k])
        step(ctot_ref[...], wcw_ref, mcw_ref, vcw_ref, cw_outs)
        loss_out[...] = tot_ref[LOSS_ROW:LOSS_ROW + 1, 0:128]

    args = [tot, ctot, *ws, *ms, *vs, wcw, mcw, vcw]
    out_shape = ([jax.ShapeDtypeStruct((1, n), F32) for n in widths for _ in range(4)]
                 + [jax.ShapeDtypeStruct((CWF, D), F32)] * 4 + [jax.ShapeDtypeStruct((1, 128), F32)])
    res = pl.pallas_call(
        body, in_specs=[VMEM] * len(args), out_specs=[VMEM] * len(out_shape), out_shape=out_shape,
        name="adamw_small")(*args)
    per = [res[4 * k:4 * k + 4] for k in range(ns)]
    return per, res[4 * ns:4 * ns + 4], res[-1]


def _pack_cw(a):
    flat = a.reshape(a.shape[:-2] + (CK * HD,))
    pad = [(0, 0)] * (flat.ndim - 1) + [(0, CWF * D - CK * HD)]
    return jnp.pad(flat, pad).reshape(a.shape[:-2] + (CWF, D))


def _unpack_cw(v):
    return v.reshape(-1)[:CK * HD].reshape(1, CK, HD)


def kernel(x, ffn1_norm, ffn1_w_gate, ffn1_w_up, ffn1_w_down, mix_norm, w_in, q_norm, k_norm, conv_w, conv_b, conv_ln_g, conv_ln_b, w_out, ffn2_norm, ffn2_w_gate, ffn2_w_up, ffn2_w_down, loss_target, m_ffn1_norm, m_ffn1_w_gate, m_ffn1_w_up, m_ffn1_w_down, m_mix_norm, m_w_in, m_q_norm, m_k_norm, m_conv_w, m_conv_b, m_conv_ln_g, m_conv_ln_b, m_w_out, m_ffn2_norm, m_ffn2_w_gate, m_ffn2_w_up, m_ffn2_w_down, v_ffn1_norm, v_ffn1_w_gate, v_ffn1_w_up, v_ffn1_w_down, v_mix_norm, v_w_in, v_q_norm, v_k_norm, v_conv_w, v_conv_b, v_conv_ln_g, v_conv_ln_b, v_w_out, v_ffn2_norm, v_ffn2_w_gate, v_ffn2_w_up, v_ffn2_w_down):
    P = dict(ffn1_norm=ffn1_norm, ffn1_w_gate=ffn1_w_gate, ffn1_w_up=ffn1_w_up, ffn1_w_down=ffn1_w_down,
             mix_norm=mix_norm, w_in=w_in, q_norm=q_norm, k_norm=k_norm, conv_w=conv_w, conv_b=conv_b,
             conv_ln_g=conv_ln_g, conv_ln_b=conv_ln_b, w_out=w_out, ffn2_norm=ffn2_norm,
             ffn2_w_gate=ffn2_w_gate, ffn2_w_up=ffn2_w_up, ffn2_w_down=ffn2_w_down)
    M = dict(ffn1_norm=m_ffn1_norm, ffn1_w_gate=m_ffn1_w_gate, ffn1_w_up=m_ffn1_w_up, ffn1_w_down=m_ffn1_w_down,
             mix_norm=m_mix_norm, w_in=m_w_in, q_norm=m_q_norm, k_norm=m_k_norm, conv_w=m_conv_w, conv_b=m_conv_b,
             conv_ln_g=m_conv_ln_g, conv_ln_b=m_conv_ln_b, w_out=m_w_out, ffn2_norm=m_ffn2_norm,
             ffn2_w_gate=m_ffn2_w_gate, ffn2_w_up=m_ffn2_w_up, ffn2_w_down=m_ffn2_w_down)
    V = dict(ffn1_norm=v_ffn1_norm, ffn1_w_gate=v_ffn1_w_gate, ffn1_w_up=v_ffn1_w_up, ffn1_w_down=v_ffn1_w_down,
             mix_norm=v_mix_norm, w_in=v_w_in, q_norm=v_q_norm, k_norm=v_k_norm, conv_w=v_conv_w, conv_b=v_conv_b,
             conv_ln_g=v_conv_ln_g, conv_ln_b=v_conv_ln_b, w_out=v_w_out, ffn2_norm=v_ffn2_norm,
             ffn2_w_gate=v_ffn2_w_gate, ffn2_w_up=v_ffn2_w_up, ffn2_w_down=v_ffn2_w_down)
    order = ["ffn1_norm", "ffn1_w_gate", "ffn1_w_up", "ffn1_w_down", "mix_norm", "w_in", "q_norm", "k_norm",
             "conv_w", "conv_b", "conv_ln_g", "conv_ln_b", "w_out", "ffn2_norm", "ffn2_w_gate", "ffn2_w_up",
             "ffn2_w_down"]
    B, S, _ = x.shape
    T = B * S

    bigs = [("wg1", "ffn1_w_gate", True), ("wu1", "ffn1_w_up", True), ("wd1", "ffn1_w_down", False),
            ("win", "w_in", True), ("wout", "w_out", False),
            ("wg2", "ffn2_w_gate", True), ("wu2", "ffn2_w_up", True), ("wd2", "ffn2_w_down", False)]
    hm = lambda a, tr: jnp.transpose(a[0]) if tr else a[0]
    cw_pad = jnp.zeros((32, 128), F32).at[0:CK, 0:HD].set(conv_w[0])
    shard = {ln: (hm(P[pn], tr), BF16) for ln, pn, tr in bigs}
    gathered = _run_carry(_GatherCarry([shard["wg1"], shard["wu1"], (cw_pad, F32)]), "gather_first")
    W = {"wg1": gathered[0], "wu1": gathered[1]}
    cwg = gathered[2].reshape(NDEV, 32, 128)[:, 0:CK, 0:HD]
    W["conv_w"] = jnp.transpose(cwg, (1, 0, 2)).reshape(CK, DC)
    norms = {n: P[n] for n in SMALL_NAMES}
    comm = _Comm({"down_in": ({n: shard[n] for n in ("wd1", "win")}, 0.5),
                  "ffn2": ({n: shard[n] for n in ("wg2", "wu2", "wd2", "wout")}, 0.5)},
                 opt={ln: (hm(P[pn], tr), hm(M[pn], tr), hm(V[pn], tr)) for ln, pn, tr in bigs})

    loss_part, gx, _, small = _local_step(x.reshape(T, D), loss_target.reshape(T, D), norms, W, B, S, comm)

    G, Dl, Mn, Vn = {}, {}, {}, {}
    dcw = small["conv_w"].reshape(CK, NDEV, HD).transpose(1, 0, 2)
    loss_row = jnp.zeros((1, 128), F32).at[0, 0].set(loss_part)
    tot, ctot, got = _small_sums([small[n] for n in SMALL_NAMES], loss_row, _pack_cw(dcw), carry=comm.last)
    comm.reduce_done(comm.last, got)
    per, cw_outs, loss_out = _adamw_small(
        tot, ctot, [P[n] for n in SMALL_NAMES], [M[n] for n in SMALL_NAMES], [V[n] for n in SMALL_NAMES],
        _pack_cw(P["conv_w"][0]), _pack_cw(M["conv_w"][0]), _pack_cw(V["conv_w"][0]))
    loss = loss_out[0, 0]
    for n, outs in zip(SMALL_NAMES, per):
        G[n], Dl[n], Mn[n], Vn[n] = outs
    G["conv_w"], Dl["conv_w"], Mn["conv_w"], Vn["conv_w"] = [_unpack_cw(o) for o in cw_outs]

    group = ("wd1", "wg1", "wu1")
    w, m, v = zip(*[comm.opt[ln] for ln in group])
    comm.updated.update(zip(group, _adamw_big([comm.reduced[ln] for ln in group], w, m, v, "adamw_ffn1")))
    for ln, pn, tr in bigs:
        G[pn], Dl[pn], Mn[pn], Vn[pn] = [(jnp.transpose(o) if tr else o)[None] for o in comm.updated[ln]]

    return (loss, gx.reshape(B, S, D), *[G[n] for n in order], *[Dl[n] for n in order],
            *[Mn[n] for n in order], *[Vn[n] for n in order])
```

```python
import functools

import jax
import jax.numpy as jnp
from jax import lax
from jax.experimental import pallas as pl
from jax.experimental.pallas import tpu as pltpu

F32 = jnp.float32
BF16 = jnp.bfloat16

D = 1024
FF = 2816
HD = 64
DA = 512
DC = 512
DIN = 2560
CK = 31
BLK = 128
DILS = (1, 4, 16)
EPS = 1e-6
NDEV = 8
MESH = pl.DeviceIdType.MESH

LR, B1, B2, AEPS, WD, STEP = 0.001, 0.9, 0.999, 1e-08, 0.01, 10

NT = (((1,), (1,)), ((), ()))
TN = (((0,), (0,)), ((), ()))

VMEM_LIMIT = 60 * 1024 * 1024


def _cp(sem=None):
    return pltpu.CompilerParams(dimension_semantics=sem, vmem_limit_bytes=VMEM_LIMIT)


def _sigmoid(x):
    return 0.5 * (jnp.tanh(0.5 * x) + 1.0)


def _pallas(body, args, *, grid, in_specs, out_specs, out_shape, scratch_shapes, sem, name, carry=None):
    if carry is None:
        outs = pl.pallas_call(body, grid=grid, in_specs=in_specs, out_specs=out_specs, out_shape=out_shape,
                              scratch_shapes=scratch_shapes, compiler_params=_cp(sem), name=name)(*args)
        return outs, None
    n_in, n_out, n_scr = len(in_specs), len(out_shape), len(scratch_shapes)
    c_in, c_out = len(carry.in_arrays), len(carry.out_shape)

    def wrapped(*refs):
        ins, refs = refs[:n_in], refs[n_in:]
        cins, refs = refs[:c_in], refs[c_in:]
        outs, refs = refs[:n_out], refs[n_out:]
        couts, refs = refs[:c_out], refs[c_out:]
        scr, cscr = refs[:n_scr], refs[n_scr:]
        ids = [pl.program_id(a) for a in range(len(grid))]
        step = ids[0]
        for i, n in zip(ids[1:], grid[1:]):
            step = step * n + i
        steps = functools.reduce(lambda a, b: a * b, grid)

        @pl.when(step == 0)
        def _():
            carry.start(cins, couts, cscr)

        body(*ins, *outs, *scr)

        for frac, stage in carry.stages():
            @pl.when(step == int(steps * frac))
            def _(stage=stage):
                stage(cins, couts, cscr)

        @pl.when(step == steps - 1)
        def _():
            carry.finish(cins, couts, cscr)

    outs = pl.pallas_call(
        wrapped, grid=grid, in_specs=list(in_specs) + carry.in_specs, out_specs=list(out_specs) + carry.out_specs,
        out_shape=list(out_shape) + carry.out_shape, scratch_shapes=list(scratch_shapes) + carry.scratch,
        compiler_params=_cp(("arbitrary",) * len(grid)), name=name)(*args, *carry.in_arrays)
    return outs[:n_out], outs[n_out:]


FC = 256


def _resident(shape):
    return pl.BlockSpec(shape, lambda *_: (0,) * len(shape), pipeline_mode=pl.Buffered(1))


def _mix_out_ffn_loss(h1, attn, conv, wout, gain, wg, wu, wd, target, name):
    T = h1.shape[0]
    tm = 512
    nt = T // tm

    def body(h1_ref, at_ref, cv_ref, wo_ref, gain_ref, wg_ref, wu_ref, wd_ref, t_ref,
             h2_ref, n_ref, g_ref, u_ref, dout_ref, dyb_ref, sq_ref, a_hbm, a_scr, a_sem):
        t = pl.program_id(0)
        a_out = lambda i: pltpu.make_async_copy(a_scr, a_hbm.at[pl.ds(pl.multiple_of(i * tm, tm), tm), :], a_sem)

        @pl.when(t > 0)
        def _():
            a_out(t - 1).wait()

        xv = (h1_ref[...]
              + jnp.dot(at_ref[...], wo_ref[0:DA, :], preferred_element_type=F32)
              + jnp.dot(cv_ref[...], wo_ref[DA:D, :], preferred_element_type=F32))
        h2_ref[...] = xv
        r = lax.rsqrt(jnp.mean(xv * xv, axis=-1, keepdims=True) + EPS)
        n_ref[...] = (xv * r * gain_ref[...]).astype(BF16)
        for c in range(FF // FC):
            cols = slice(c * FC, (c + 1) * FC)
            nb = n_ref[...]
            g = lax.dot_general(nb, wg_ref[cols, :], NT, preferred_element_type=F32)
            u = lax.dot_general(nb, wu_ref[cols, :], NT, preferred_element_type=F32)
            g_ref[:, cols] = g.astype(BF16)
            u_ref[:, cols] = u.astype(BF16)
            a_scr[:, cols] = (g * _sigmoid(g) * u).astype(BF16)
        a_out(t).start()
        e = h2_ref[...] + 0.5 * jnp.dot(a_scr[...], wd_ref[...], preferred_element_type=F32) - t_ref[...]
        dout = e * (1.0 / D)
        dout_ref[...] = dout
        dyb_ref[...] = (0.5 * dout).astype(BF16)
        sq_ref[...] = jnp.sum(e * e, axis=0, keepdims=True)[None]

        @pl.when(t == nt - 1)
        def _():
            a_out(t).wait()

    row = pl.BlockSpec((tm, D), lambda t: (t, 0))
    half = pl.BlockSpec((tm, DA), lambda t: (t, 0))
    wide = pl.BlockSpec((tm, FF), lambda t: (t, 0))
    outs, _ = _pallas(
        body, (h1, attn, conv, wout, gain, wg, wu, wd, target), grid=(nt,),
        in_specs=[row, half, half, _resident((D, D)), _resident((1, D)), _resident((FF, D)), _resident((FF, D)),
                  _resident((FF, D)), row],
        out_specs=[row, row, wide, wide, row, row, pl.BlockSpec((1, 1, D), lambda t: (t, 0, 0)), HBM],
        out_shape=[jax.ShapeDtypeStruct((T, D), F32), jax.ShapeDtypeStruct((T, D), BF16)]
                  + [jax.ShapeDtypeStruct((T, FF), BF16)] * 2
                  + [jax.ShapeDtypeStruct((T, D), F32), jax.ShapeDtypeStruct((T, D), BF16),
                     jax.ShapeDtypeStruct((nt, 1, D), F32), jax.ShapeDtypeStruct((T, FF), BF16)],
        scratch_shapes=[pltpu.VMEM((tm, FF), BF16), pltpu.SemaphoreType.DMA(())],
        sem=("arbitrary",), name=name)
    return outs


def _ffn_gate_up(x, gain, wg, wu, name, carry=None):
    T = x.shape[0]
    tm = 512

    def body(x_ref, gain_ref, wg_ref, wu_ref, n_ref, g_ref, u_ref, a_ref):
        xv = x_ref[...]
        r = lax.rsqrt(jnp.mean(xv * xv, axis=-1, keepdims=True) + EPS)
        n_ref[...] = (xv * r * gain_ref[...]).astype(BF16)
        for c in range(FF // FC):
            cols = slice(c * FC, (c + 1) * FC)
            nb = n_ref[...]
            g = lax.dot_general(nb, wg_ref[cols, :], NT, preferred_element_type=F32)
            u = lax.dot_general(nb, wu_ref[cols, :], NT, preferred_element_type=F32)
            g_ref[:, cols] = g.astype(BF16)
            u_ref[:, cols] = u.astype(BF16)
            a_ref[:, cols] = (g * _sigmoid(g) * u).astype(BF16)

    row = pl.BlockSpec((tm, D), lambda t: (t, 0))
    wide = pl.BlockSpec((tm, FF), lambda t: (t, 0))
    return _pallas(
        body, (x, gain, wg, wu), grid=(T // tm,),
        in_specs=[row, _resident((1, D)), _resident((FF, D)), _resident((FF, D))],
        out_specs=[row, wide, wide, wide],
        out_shape=[jax.ShapeDtypeStruct((T, D), BF16)] + [jax.ShapeDtypeStruct((T, FF), BF16)] * 3,
        scratch_shapes=[], sem=("parallel",), name=name, carry=carry)


def _ffn_down_mix_in(x, a, wd, gain, win, name):
    T = x.shape[0]
    tm = 512

    def body(x_ref, a_ref, wd_ref, gain_ref, win_ref, h_ref, u_ref, n_ref):
        hv = x_ref[...] + 0.5 * jnp.dot(a_ref[...], wd_ref[...], preferred_element_type=F32)
        h_ref[...] = hv
        r = lax.rsqrt(jnp.mean(hv * hv, axis=-1, keepdims=True) + EPS)
        n_ref[...] = (hv * r * gain_ref[...]).astype(BF16)
        u_ref[...] = lax.dot_general(n_ref[...], win_ref[...], NT, preferred_element_type=F32).astype(BF16)

    row = pl.BlockSpec((tm, D), lambda t: (t, 0))
    wide = pl.BlockSpec((tm, FF), lambda t: (t, 0))
    outs, _ = _pallas(
        body, (x, a, wd, gain, win), grid=(T // tm,),
        in_specs=[row, wide, _resident((FF, D)), _resident((1, D)), _resident((DIN, D))],
        out_specs=[row, pl.BlockSpec((tm, DIN), lambda t: (t, 0)), row],
        out_shape=[jax.ShapeDtypeStruct((T, D), F32), jax.ShapeDtypeStruct((T, DIN), BF16),
                   jax.ShapeDtypeStruct((T, D), BF16)],
        scratch_shapes=[], sem=("parallel",), name=name)
    return outs


def _ffn_bwd_act(dyb, g, u, x, dout, gain, wg, wu, wd, name, carry=None):
    T = x.shape[0]
    tm = 256
    nt = T // tm

    def body(dy_ref, g_ref, u_ref, x_ref, dout_ref, gain_ref, wg_ref, wu_ref, wd_ref,
             dg_ref, du_ref, dx_ref, dgn_ref):
        for c in range(FF // FC):
            cols = slice(c * FC, (c + 1) * FC)
            da = lax.dot_general(dy_ref[...], wd_ref[cols, :], NT, preferred_element_type=F32)
            gv = g_ref[:, cols].astype(F32)
            uv = u_ref[:, cols].astype(F32)
            sg = _sigmoid(gv)
            dg_ref[:, cols] = (da * uv * (sg * (1.0 + gv * (1.0 - sg)))).astype(BF16)
            du_ref[:, cols] = (da * (gv * sg)).astype(BF16)
        dn = (jnp.dot(dg_ref[...], wg_ref[...], preferred_element_type=F32)
              + jnp.dot(du_ref[...], wu_ref[...], preferred_element_type=F32))
        dx, dgain = _rms_bwd_rows(dn, x_ref[...], gain_ref[...])
        dx_ref[...] = dout_ref[...] + dx

        @pl.when(pl.program_id(0) == 0)
        def _():
            dgn_ref[...] = jnp.zeros_like(dgn_ref)

        dgn_ref[...] += dgain[None]

    row = pl.BlockSpec((tm, D), lambda t: (t, 0))
    wide = pl.BlockSpec((tm, FF), lambda t: (t, 0))
    return _pallas(
        body, (dyb, g, u, x, dout, gain, wg, wu, wd), grid=(nt,),
        in_specs=[row, wide, wide, row, row, _resident((1, D)), _resident((FF, D)), _resident((FF, D)),
                  _resident((FF, D))],
        out_specs=[wide, wide, row, pl.BlockSpec((1, 1, D), lambda t: (0, 0, 0))],
        out_shape=[jax.ShapeDtypeStruct((T, FF), BF16)] * 2
                  + [jax.ShapeDtypeStruct((T, D), F32), jax.ShapeDtypeStruct((1, 1, D), F32)],
        scratch_shapes=[], sem=("arbitrary",), name=name, carry=carry)


def _ffn_bwd_w(lhs, rhs, name, carry=None):
    T = rhs.shape[0]
    tf = 256

    def body(l_ref, r_ref, dw_ref):
        dw_ref[...] = lax.dot_general(l_ref[...], r_ref[...], TN, preferred_element_type=F32).astype(BF16)

    (dw,), got = _pallas(
        body, (lhs, rhs), grid=(FF // tf,),
        in_specs=[pl.BlockSpec((T, tf), lambda f: (0, f)), _resident((T, D))],
        out_specs=[pl.BlockSpec((tf, D), lambda f: (f, 0))], out_shape=[jax.ShapeDtypeStruct((FF, D), BF16)],
        scratch_shapes=[], sem=("parallel",), name=name, carry=carry)
    return dw, got


def _rms_bwd_rows(dn, xv, gain):
    r = lax.rsqrt(jnp.mean(xv * xv, axis=-1, keepdims=True) + EPS)
    xhat = xv * r
    dxhat = dn * gain
    dx = r * (dxhat - xhat * jnp.mean(dxhat * xhat, axis=-1, keepdims=True))
    return dx, jnp.sum(dn * xhat, axis=0, keepdims=True)


def _mix_out_bwd(dh, attn, conv, wout):
    T = dh.shape[0]
    tm = 512
    nt = T // tm

    def body(dh_ref, a_ref, c_ref, w_ref, da_ref, dc_ref, dw_ref, acc_scr):
        t = pl.program_id(0)

        @pl.when(t == 0)
        def _():
            acc_scr[...] = jnp.zeros_like(acc_scr)

        dhb = dh_ref[...].astype(BF16)
        dmix = lax.dot_general(dhb, w_ref[...], NT, preferred_element_type=F32)
        da_ref[...] = dmix[:, 0:DA].astype(BF16)
        dc_ref[...] = dmix[:, DA:D].astype(BF16)
        acc_scr[0:DA, :] += lax.dot_general(a_ref[...], dhb, TN, preferred_element_type=F32)
        acc_scr[DA:D, :] += lax.dot_general(c_ref[...], dhb, TN, preferred_element_type=F32)

        @pl.when(t == nt - 1)
        def _():
            dw_ref[...] = acc_scr[...].astype(BF16)

    row = pl.BlockSpec((tm, D), lambda t: (t, 0))
    half = pl.BlockSpec((tm, DA), lambda t: (t, 0))
    full = pl.BlockSpec((D, D), lambda t: (0, 0))
    return pl.pallas_call(
        body, grid=(nt,), in_specs=[row, half, half, full], out_specs=[half, half, full],
        out_shape=[jax.ShapeDtypeStruct((T, DA), BF16)] * 2 + [jax.ShapeDtypeStruct((D, D), BF16)],
        scratch_shapes=[pltpu.VMEM((D, D), F32)],
        compiler_params=_cp(("arbitrary",)), name="mix_out_bwd")(dh, attn, conv, wout)


def _mix_in_bwd(dparts, win, nb, h, dh, gain):
    T = h.shape[0]
    tm = 512
    nt = T // tm

    def body(d0, d1, d2, d3, d4, w_ref, n_ref, h_ref, dh_ref, gain_ref,
             dw_ref, dx_ref, dyb_ref, dg_ref, acc_scr):
        t = pl.program_id(0)

        @pl.when(t == 0)
        def _():
            acc_scr[...] = jnp.zeros_like(acc_scr)

        n = n_ref[...]
        dn = jnp.zeros((tm, D), F32)
        for i, d_ref in enumerate((d0, d1, d2, d3, d4)):
            dv = d_ref[...]
            dn = dn + jnp.dot(dv, w_ref[i * DA:(i + 1) * DA, :], preferred_element_type=F32)
            acc_scr[i * DA:(i + 1) * DA, :] += lax.dot_general(dv, n, TN, preferred_element_type=F32)
        dx, dgain = _rms_bwd_rows(dn, h_ref[...], gain_ref[...])
        tot = dh_ref[...] + dx
        dx_ref[...] = tot
        dyb_ref[...] = (0.5 * tot).astype(BF16)
        dg_ref[...] = dgain[None]

        @pl.when(t == nt - 1)
        def _():
            dw_ref[...] = acc_scr[...].astype(BF16)

    row = pl.BlockSpec((tm, D), lambda t: (t, 0))
    half = pl.BlockSpec((tm, DA), lambda t: (t, 0))
    full = pl.BlockSpec((DIN, D), lambda t: (0, 0))
    return pl.pallas_call(
        body, grid=(nt,),
        in_specs=[half] * 5 + [full, row, row, row, pl.BlockSpec((1, D), lambda t: (0, 0))],
        out_specs=[full, row, row, pl.BlockSpec((1, 1, D), lambda t: (t, 0, 0))],
        out_shape=[jax.ShapeDtypeStruct((DIN, D), BF16), jax.ShapeDtypeStruct((T, D), F32),
                   jax.ShapeDtypeStruct((T, D), BF16), jax.ShapeDtypeStruct((nt, 1, D), F32)],
        scratch_shapes=[pltpu.VMEM((DIN, D), F32)],
        compiler_params=_cp(("arbitrary",)), name="mix_in_bwd")(*dparts, win, nb, h, dh, gain)


def _head_masks():
    lane = lax.broadcasted_iota(jnp.int32, (1, 2 * HD), 1)
    m0 = lane < HD
    return m0, jnp.logical_not(m0)


def _stack_heads(v, m0, m1):
    z = jnp.zeros_like(v)
    return jnp.concatenate([jnp.where(m0, v, z), jnp.where(m1, v, z)], axis=0)


def _unstack_heads(v2, m0):
    return jnp.where(m0, v2[0:BLK], v2[BLK:2 * BLK])


def _head_sums(xv):
    ri = lax.broadcasted_iota(jnp.int32, (2 * HD, 2 * HD), 0)
    ci = lax.broadcasted_iota(jnp.int32, (2 * HD, 2 * HD), 1)
    ones = jnp.where((ri < HD) == (ci < HD), 1.0, 0.0).astype(BF16)
    hi = xv.astype(BF16)
    lo = (xv - hi.astype(F32)).astype(BF16)
    return (jnp.dot(hi, ones, preferred_element_type=F32) + jnp.dot(lo, ones, preferred_element_type=F32))


def _head_rms(xv):
    return lax.rsqrt(_head_sums(xv * xv) * (1.0 / HD) + EPS)


def _band_mask(first):
    qi = lax.broadcasted_iota(jnp.int32, (BLK, 2 * BLK), 0)
    ci = lax.broadcasted_iota(jnp.int32, (BLK, 2 * BLK), 1)
    band = (ci >= qi) & (ci <= qi + BLK)
    return band & ((ci >= BLK) | jnp.logical_not(first))


def _block_rows(j, d, seg):
    r, n = j // seg, j % seg
    start = r + (d * BLK) * n
    first = n == 0
    prev = jnp.where(first, start, start - d * BLK)
    return pl.ds(start, BLK, stride=d), pl.ds(prev, BLK, stride=d), first


def _block_keys(refs, cur, prev, first, single):
    if single:
        qi = lax.broadcasted_iota(jnp.int32, (BLK, BLK), 0)
        ci = lax.broadcasted_iota(jnp.int32, (BLK, BLK), 1)
        return [r[cur, :].astype(BF16) for r in refs], ci <= qi
    return ([jnp.concatenate([r[prev, :], r[cur, :]], axis=0).astype(BF16) for r in refs], _band_mask(first))


def _attn_fwd(u, qg2, kg2, B, S, carry=None):
    T = B * S
    NB = S // BLK
    scale = HD ** -0.5

    def body(q_ref, k_ref, v_ref, qg_ref, kg_ref, o_ref, lse_ref, qn, kn, vn, os_, ls_):
        m0, m1 = _head_masks()
        qv = q_ref[...].astype(F32)
        qn[...] = qv * _head_rms(qv) * (qg_ref[...] * scale)
        kv = k_ref[...].astype(F32)
        kn[...] = kv * _head_rms(kv) * kg_ref[...]
        vn[...] = v_ref[...].astype(F32)

        for i, d in enumerate(DILS):
            seg = NB // d

            def blk(j, c, i=i, d=d, seg=seg):
                cur, prev, first = _block_rows(j, d, seg)
                q2 = _stack_heads(qn[cur, :].astype(BF16), m0, m1)
                (kk, vv), mask = _block_keys((kn, vn), cur, prev, first, False)
                s = lax.dot_general(q2, kk, NT, preferred_element_type=F32)
                s = jnp.where(jnp.concatenate([mask, mask], axis=0), s, -1e30)
                mx = jnp.max(s, axis=-1, keepdims=True)
                p = jnp.exp(s - mx)
                l = jnp.sum(p, axis=-1, keepdims=True)
                o2 = jnp.dot((p * (1.0 / l)).astype(BF16), vv, preferred_element_type=F32)
                os_[i, cur, :] = _unstack_heads(o2, m0)
                ls_[i, cur, :] = _unstack_heads(mx + jnp.log(l), m0)
                return c

            lax.fori_loop(0, NB, blk, 0, unroll=8)

        def comb(c, carry):
            rows = pl.ds(pl.multiple_of(c * 256, 256), 256)
            l0, l1, l2 = ls_[0, rows, :], ls_[1, rows, :], ls_[2, rows, :]
            mx = jnp.maximum(jnp.maximum(l0, l1), l2)
            e0, e1, e2 = jnp.exp(l0 - mx), jnp.exp(l1 - mx), jnp.exp(l2 - mx)
            tot = e0 + e1 + e2
            inv = 1.0 / tot
            o = (e0 * os_[0, rows, :] + e1 * os_[1, rows, :] + e2 * os_[2, rows, :]) * inv
            o_ref[rows, :] = o.astype(BF16)
            lse_ref[rows, :] = mx + jnp.log(tot)
            return carry

        lax.fori_loop(0, S // 256, comb, 0)

    pair = 2 * HD
    blk_spec = lambda off: pl.BlockSpec((S, pair), lambda b, p, off=off: (b, off + p))
    gspec = pl.BlockSpec((1, pair), lambda b, p: (0, 0))
    return _pallas(
        body, (u, u, u, qg2, kg2), grid=(B, DA // pair),
        in_specs=[blk_spec(0), blk_spec(DA // pair), blk_spec(2 * DA // pair), gspec, gspec],
        out_specs=[blk_spec(0), blk_spec(0)],
        out_shape=[jax.ShapeDtypeStruct((T, DA), BF16), jax.ShapeDtypeStruct((T, DA), F32)],
        scratch_shapes=[pltpu.VMEM((S, pair), F32)] * 3 + [pltpu.VMEM((3, S, pair), F32)] * 2,
        sem=("parallel", "parallel"), name="attn_fwd", carry=carry)


def _attn_bwd(u, attn, dattn, lse, qg2, kg2, B, S, carry=None):
    T = B * S
    NB = S // BLK
    scale = HD ** -0.5
    pair = 2 * HD

    def body(q_ref, k_ref, v_ref, o_ref, do_ref, lse_ref, qg_ref, kg_ref,
             dq_ref, dk_ref, dv_ref, dgn_ref,
             qn, kn, vn, don, ldl, accq, acck, accv, rq, rk):
        m0, m1 = _head_masks()
        lane = lax.broadcasted_iota(jnp.int32, (1, pair), 1)
        qv = q_ref[...].astype(F32)
        rq[...] = _head_rms(qv)
        qn[...] = qv * rq[...] * (qg_ref[...] * scale)
        kv = k_ref[...].astype(F32)
        rk[...] = _head_rms(kv)
        kn[...] = kv * rk[...] * kg_ref[...]
        vn[...] = v_ref[...].astype(F32)
        dov = do_ref[...].astype(F32)
        don[...] = dov
        ldl[...] = jnp.where((lane % HD) < HD // 2, lse_ref[...], _head_sums(dov * o_ref[...].astype(F32)))

        for i, d in enumerate(DILS):
            seg = NB // d

            def blk(j, c, i=i, d=d, seg=seg):
                cur, prev, first = _block_rows(j, d, seg)
                q2 = _stack_heads(qn[cur, :].astype(BF16), m0, m1)
                do2 = _stack_heads(don[cur, :].astype(BF16), m0, m1)
                (kk, vv), mask = _block_keys((kn, vn), cur, prev, first, seg == 1)
                ldv = ldl[cur, :]
                lse2 = jnp.concatenate([ldv[:, 0:1], ldv[:, HD:HD + 1]], axis=0)
                dl2 = jnp.concatenate([ldv[:, HD // 2:HD // 2 + 1], ldv[:, HD + HD // 2:HD + HD // 2 + 1]], axis=0)
                s = lax.dot_general(q2, kk, NT, preferred_element_type=F32)
                p = jnp.where(jnp.concatenate([mask, mask], axis=0), jnp.exp(s - lse2), 0.0)
                dp = lax.dot_general(do2, vv, NT, preferred_element_type=F32)
                ds = (p * (dp - dl2)).astype(BF16)
                dq_acc = _unstack_heads(jnp.dot(ds, kk, preferred_element_type=F32), m0)
                dk_acc = lax.dot_general(ds, q2, TN, preferred_element_type=F32)
                dv_acc = lax.dot_general(p.astype(BF16), do2, TN, preferred_element_type=F32)
                if i == 0:
                    accq[cur, :] = dq_acc
                    acck[cur, :] = dk_acc[BLK:2 * BLK]
                    accv[cur, :] = dv_acc[BLK:2 * BLK]
                    acck[prev, :] += dk_acc[0:BLK]
                    accv[prev, :] += dv_acc[0:BLK]
                elif seg == 1:
                    accq[cur, :] += dq_acc
                    acck[cur, :] += dk_acc
                    accv[cur, :] += dv_acc
                else:
                    accq[cur, :] += dq_acc
                    acck[prev, :] += dk_acc[0:BLK]
                    acck[cur, :] += dk_acc[BLK:2 * BLK]
                    accv[prev, :] += dv_acc[0:BLK]
                    accv[cur, :] += dv_acc[BLK:2 * BLK]
                return c

            lax.fori_loop(0, NB, blk, 0, unroll=8)

        def norm_bwd(x_ref, r_ref, dn, gain):
            r = r_ref[...]
            xhat = x_ref[...].astype(F32) * r
            dxhat = dn * gain
            dx = r * (dxhat - xhat * (_head_sums(dxhat * xhat) * (1.0 / HD)))
            return dx, jnp.sum(dn * xhat, axis=0, keepdims=True)

        dq, dgq = norm_bwd(q_ref, rq, accq[...], qg_ref[...] * scale)
        dk, dgk = norm_bwd(k_ref, rk, acck[...], kg_ref[...])
        dq_ref[...] = dq.astype(BF16)
        dk_ref[...] = dk.astype(BF16)
        dv_ref[...] = accv[...].astype(BF16)
        dgn_ref[...] = jnp.concatenate([dgq * scale, dgk, jnp.zeros((6, pair), F32)], axis=0)[None]

    blk_spec = lambda off: pl.BlockSpec((S, pair), lambda b, p, off=off: (b, off + p))
    gspec = pl.BlockSpec((1, pair), lambda b, p: (0, 0))
    np_ = DA // pair
    return _pallas(
        body, (u, u, u, attn, dattn, lse, qg2, kg2), grid=(B, np_),
        in_specs=[blk_spec(0), blk_spec(np_), blk_spec(2 * np_), blk_spec(0), blk_spec(0), blk_spec(0),
                  gspec, gspec],
        out_specs=[blk_spec(0), blk_spec(0), blk_spec(0),
                   pl.BlockSpec((1, 8, pair), lambda b, p: (b * np_ + p, 0, 0))],
        out_shape=[jax.ShapeDtypeStruct((T, DA), BF16)] * 3 + [jax.ShapeDtypeStruct((B * np_, 8, pair), F32)],
        scratch_shapes=[pltpu.VMEM((S, pair), F32)] * 10,
        sem=("parallel", "parallel"), name="attn_bwd", carry=carry)


CT = 32
CPAD = 32


def _shifted(win, offsets):
    rolled, out = {}, {}
    n = win.shape[0]
    for o in offsets:
        sub = o % 8
        if sub not in rolled:
            rolled[sub] = win if sub == 0 else pltpu.roll(win, n - sub, 0)
        out[o] = rolled[sub][o - sub:o - sub + CT, :]
    return out


def _ln_fwd(y, g, b):
    mu = jnp.mean(y, axis=-1, keepdims=True)
    yc = y - mu
    rstd = lax.rsqrt(jnp.mean(yc * yc, axis=-1, keepdims=True) + EPS)
    xhat = yc * rstd
    return xhat, rstd, xhat * g + b


def _fill_glu(ca_ref, cg_ref, glu, S):
    glu[pl.ds(0, CPAD), :] = jnp.zeros((CPAD, DC), F32)

    def fill(i, c):
        rows = pl.ds(pl.multiple_of(i * 256, 256), 256)
        a = ca_ref[rows, :].astype(F32)
        gt = cg_ref[rows, :].astype(F32)
        glu[pl.ds(pl.multiple_of(CPAD + i * 256, CT), 256), :] = a * _sigmoid(gt)
        return c

    lax.fori_loop(0, S // 256, fill, 0)


def _conv_fwd(u, cw, cb, lg, lb, B, S):
    T = B * S

    def body(ca_ref, cg_ref, w_ref, b_ref, lg_ref, lb_ref, o_ref, y_ref, glu):
        _fill_glu(ca_ref, cg_ref, glu, S)

        def step(i, c):
            t0 = pl.multiple_of(i * CT, CT)
            win = glu[pl.ds(t0, 2 * CT), :]
            acc = jnp.zeros((CT, DC), F32) + b_ref[...]
            taps = _shifted(win, [k + 2 for k in range(CK)])
            for k in range(CK):
                acc = acc + taps[k + 2] * w_ref[k:k + 1, :]
            y_ref[pl.ds(t0, CT), :] = acc
            _, _, z = _ln_fwd(acc, lg_ref[...], lb_ref[...])
            o_ref[pl.ds(t0, CT), :] = (z * _sigmoid(z)).astype(BF16)
            return c

        lax.fori_loop(0, S // CT, step, 0, unroll=4)

    vec = pl.BlockSpec((1, DC), lambda b: (0, 0))
    return pl.pallas_call(
        body, grid=(B,),
        in_specs=[pl.BlockSpec((S, DC), lambda b: (b, 3)), pl.BlockSpec((S, DC), lambda b: (b, 4)),
                  pl.BlockSpec((CT, DC), lambda b: (0, 0)), vec, vec, vec],
        out_specs=[pl.BlockSpec((S, DC), lambda b: (b, 0))] * 2,
        out_shape=[jax.ShapeDtypeStruct((T, DC), BF16), jax.ShapeDtypeStruct((T, DC), F32)],
        scratch_shapes=[pltpu.VMEM((CPAD + S, DC), F32)],
        compiler_params=_cp(("parallel",)), name="conv_fwd")(u, u, cw, cb, lg, lb)


def _conv_bwd(u, y, dconv, cw, lg, lb, B, S):
    T = B * S

    def body(ca_ref, cg_ref, y_ref, dc_ref, w_ref, lg_ref, lb_ref,
             dca_ref, dcg_ref, dw_ref, ds_ref, glu, dyp, dwacc):
        _fill_glu(ca_ref, cg_ref, glu, S)
        dyp[pl.ds(S, CPAD), :] = jnp.zeros((CPAD, DC), F32)
        lgv, lbv = lg_ref[...], lb_ref[...]

        def sum8(v):
            return functools.reduce(jnp.add, [v[r:r + 8] for r in range(0, v.shape[0], 8)])

        P1 = 4 * CT

        def p1(i, carry):
            sb, sg, sl = carry
            t0 = pl.multiple_of(i * P1, P1)
            xhat, rstd, z = _ln_fwd(y_ref[pl.ds(t0, P1), :], lgv, lbv)
            sz = _sigmoid(z)
            dz = dc_ref[pl.ds(t0, P1), :].astype(F32) * (sz * (1.0 + z * (1.0 - sz)))
            dxhat = dz * lgv
            dy = rstd * (dxhat - jnp.mean(dxhat, axis=-1, keepdims=True)
                         - xhat * jnp.mean(dxhat * xhat, axis=-1, keepdims=True))
            dyp[pl.ds(t0, P1), :] = dy
            return sb + sum8(dy), sg + sum8(dz * xhat), sl + sum8(dz)

        z8 = jnp.zeros((8, DC), F32)
        sb, sg, sl = lax.fori_loop(0, S // P1, p1, (z8, z8, z8))
        rs = lambda v: jnp.sum(v, axis=0, keepdims=True)
        ds_ref[...] = jnp.concatenate([rs(sb), rs(sg), rs(sl), jnp.zeros((5, DC), F32)], axis=0)[None]

        def p2(i, c):
            t0 = pl.multiple_of(i * CT, CT)
            win = dyp[pl.ds(t0, 2 * CT), :]
            acc = jnp.zeros((CT, DC), F32)
            taps = _shifted(win, [30 - k for k in range(CK)])
            for k in range(CK):
                acc = acc + taps[30 - k] * w_ref[k:k + 1, :]
            a = ca_ref[pl.ds(t0, CT), :].astype(F32)
            sgt = _sigmoid(cg_ref[pl.ds(t0, CT), :].astype(F32))
            dca_ref[pl.ds(t0, CT), :] = (acc * sgt).astype(BF16)
            dcg_ref[pl.ds(t0, CT), :] = (acc * a * sgt * (1.0 - sgt)).astype(BF16)
            return c

        lax.fori_loop(0, S // CT, p2, 0)

        dwacc[...] = jnp.zeros_like(dwacc)

        def p3(i, c):
            t0 = pl.multiple_of(i * CT, CT)
            win = glu[pl.ds(t0, 2 * CT), :]
            dy = dyp[pl.ds(t0, CT), :]
            for k in range(CK):
                dwacc[k] += sum8(dy * win[k + 2:k + 2 + CT, :])
            return c

        lax.fori_loop(0, S // CT, p3, 0)
        dw_ref[...] = jnp.sum(dwacc[...], axis=1)[None]

    vec = pl.BlockSpec((1, DC), lambda b: (0, 0))
    seq = pl.BlockSpec((S, DC), lambda b: (b, 0))
    return pl.pallas_call(
        body, grid=(B,),
        in_specs=[pl.BlockSpec((S, DC), lambda b: (b, 3)), pl.BlockSpec((S, DC), lambda b: (b, 4)),
                  seq, seq, pl.BlockSpec((CT, DC), lambda b: (0, 0)), vec, vec],
        out_specs=[seq, seq, pl.BlockSpec((1, CT, DC), lambda b: (b, 0, 0)),
                   pl.BlockSpec((1, 8, DC), lambda b: (b, 0, 0))],
        out_shape=[jax.ShapeDtypeStruct((T, DC), BF16)] * 2
                  + [jax.ShapeDtypeStruct((B, CT, DC), F32), jax.ShapeDtypeStruct((B, 8, DC), F32)],
        scratch_shapes=[pltpu.VMEM((CPAD + S, DC), F32), pltpu.VMEM((S + CPAD, DC), F32),
                        pltpu.VMEM((CT, 8, DC), F32)],
        compiler_params=_cp(("parallel",)), name="conv_bwd")(u, u, y, dconv, cw, lg, lb)


def _local_step(x, target, norms, W, B, S, comm=None):
    qg2 = jnp.concatenate([norms["q_norm"], norms["q_norm"]], axis=1)
    kg2 = jnp.concatenate([norms["k_norm"], norms["k_norm"]], axis=1)
    cw = jnp.concatenate([W["conv_w"], jnp.zeros((1, DC), F32)], axis=0)

    W = dict(W)
    (n1, g1, u1, act1), got = _ffn_gate_up(x, norms["ffn1_norm"], W["wg1"], W["wu1"], "ffn1_gate_up",
                                           carry=comm.gathers["down_in"] if comm else None)
    if comm:
        W.update(comm.gathered("down_in", got))
    h1, u, n2 = _ffn_down_mix_in(x, act1, W["wd1"], norms["mix_norm"], W["win"], "ffn1_down_mix_in")
    (attn, lse), got = _attn_fwd(u, qg2, kg2, B, S, carry=comm.gathers["ffn2"] if comm else None)
    if comm:
        W = dict(W, **comm.gathered("ffn2", got))
    conv, y = _conv_fwd(u, cw, norms["conv_b"], norms["conv_ln_g"], norms["conv_ln_b"], B, S)
    h2, n3, g2, u2, dout, dyb, sq, act2 = _mix_out_ffn_loss(h1, attn, conv, W["wout"], norms["ffn2_norm"],
                                                            W["wg2"], W["wu2"], W["wd2"], target, "ffn2_fwd")
    loss = (0.5 / D) * jnp.sum(sq)

    (dg2, du2, dh2, dgn_ffn2), _ = _ffn_bwd_act(dyb, g2, u2, h2, dout, norms["ffn2_norm"],
                                               W["wg2"], W["wu2"], W["wd2"], "ffn2_bwd_act")
    dwd2, _ = _ffn_bwd_w(act2, dyb, "ffn2_bwd_wd")
    dwg2, _ = _ffn_bwd_w(dg2, n3, "ffn2_bwd_wg")
    dwu2, _ = _ffn_bwd_w(du2, n3, "ffn2_bwd_wu")
    dattn, dconv, dwout = _mix_out_bwd(dh2, attn, conv, W["wout"])
    carry = comm.reduce_start({"wg2": dwg2, "wu2": dwu2, "wd2": dwd2, "wout": dwout}) if comm else None
    (dq, dk, dv, dgn_qk), got = _attn_bwd(u, attn, dattn, lse, qg2, kg2, B, S, carry=carry)
    if comm:
        comm.reduce_done(carry, got)
    dca, dcg, dcw, dcs = _conv_bwd(u, y, dconv, cw, norms["conv_ln_g"], norms["conv_ln_b"], B, S)
    dwin, dh1, dyb1, dgn_mix = _mix_in_bwd((dq, dk, dv, dca, dcg), W["win"], n2, h1, dh2, norms["mix_norm"])
    carry = comm.reduce_start({"win": dwin}) if comm else None
    dwd1, got = _ffn_bwd_w(act1, dyb1, "ffn1_bwd_wd", carry=carry)
    if comm:
        comm.reduce_done(carry, got)
    carry = comm.update_start(("wg2", "wu2", "wd2", "wout", "win")) if comm else None
    (dg1, du1, gx, dgn_ffn1), got = _ffn_bwd_act(dyb1, g1, u1, x, dh1, norms["ffn1_norm"],
                                                W["wg1"], W["wu1"], W["wd1"], "ffn1_bwd_act", carry=carry)
    if comm:
        comm.update_done(carry, got)
        carry = comm.reduce_start({"wd1": dwd1})
    dwg1, got = _ffn_bwd_w(dg1, n1, "ffn1_bwd_wg", carry=carry)
    if comm:
        comm.reduce_done(carry, got)
        carry = comm.reduce_start({"wg1": dwg1})
    dwu1, got = _ffn_bwd_w(du1, n1, "ffn1_bwd_wu", carry=carry)
    if comm:
        comm.reduce_done(carry, got)
        comm.last = comm.reduce_start({"wu1": dwu1})

    qk = jnp.sum(dgn_qk, axis=0)
    cs = jnp.sum(dcs, axis=0)
    small = {
        "ffn1_norm": jnp.sum(dgn_ffn1, axis=0),
        "mix_norm": jnp.sum(dgn_mix, axis=0),
        "q_norm": qk[0:1, 0:HD] + qk[0:1, HD:2 * HD],
        "k_norm": qk[1:2, 0:HD] + qk[1:2, HD:2 * HD],
        "conv_w": jnp.sum(dcw, axis=0)[0:CK],
        "conv_b": cs[0:1],
        "conv_ln_g": cs[1:2],
        "conv_ln_b": cs[2:3],
        "ffn2_norm": jnp.sum(dgn_ffn2, axis=0),
    }
    big = {"wg1": dwg1, "wu1": dwu1, "wd1": dwd1, "win": dwin, "wout": dwout,
           "wg2": dwg2, "wu2": dwu2, "wd2": dwd2}
    return loss, gx, big, small


HBM = pl.BlockSpec(memory_space=pltpu.HBM)
VMEM = pl.BlockSpec(memory_space=pltpu.VMEM)


def _place():
    return lax.axis_index("x"), lax.axis_index("y"), lax.axis_index("c")


class _GatherCarry:
    def __init__(self, shards, mid_at=0.5):
        nt = len(shards)
        self.mid_at = mid_at
        self.shards = shards
        self.in_arrays = [s for s, _ in shards]
        self.in_specs = [VMEM] * nt
        self.out_shape = [jax.ShapeDtypeStruct((NDEV * s.shape[0], s.shape[1]), dt) for s, dt in shards]
        self.out_specs = [HBM] * nt
        self.scratch = ([pltpu.VMEM(s.shape, dt) for s, dt in shards]
                        + [pltpu.SemaphoreType.DMA((nt, 7)), pltpu.SemaphoreType.DMA((nt, 7)),
                           pltpu.SemaphoreType.DMA((nt,))])

    def _copies(self, outs, scr):
        nt = len(self.shards)
        stages = scr[:nt]
        send_sems, recv_sems, local_sems = scr[nt:]
        x, y, c = _place()
        me, sibling = (x, y, c), (x, y, 1 - c)
        xn, yn, diag = (1 - x, y, c), (x, 1 - y, c), (1 - x, 1 - y, c)
        via = (x ^ c, y ^ (1 - c), c)
        onto = (x ^ (1 - c), y ^ c, c)

        def rows(t, px, py, pc):
            r = self.shards[t][0].shape[0]
            return outs[t].at[pl.ds((4 * px + 2 * py + pc) * r, r), :]

        def copy(t, k, block, to, src=None):
            return pltpu.make_async_remote_copy(
                src_ref=rows(t, *block) if src is None else src, dst_ref=rows(t, *block),
                send_sem=send_sems.at[t, k], recv_sem=recv_sems.at[t, k],
                device_id=to, device_id_type=MESH)

        sib = lambda b: (b[0], b[1], 1 - c)
        return dict(
            local=[pltpu.make_async_copy(stages[t], rows(t, *me), local_sems.at[t]) for t in range(nt)],
            own=[[copy(t, 0, me, sibling, src=stages[t]), copy(t, 1, me, xn, src=stages[t]),
                  copy(t, 2, me, yn, src=stages[t])] for t in range(nt)],
            relay=[copy(t, 3, via, onto) for t in range(nt)],
            down=[[copy(t, 4, xn, sibling), copy(t, 5, yn, sibling)] for t in range(nt)],
            down_diag=[copy(t, 6, diag, sibling) for t in range(nt)],
            got_xy=[[copy(t, 1, xn, me), copy(t, 2, yn, me)] for t in range(nt)],
            got_diag=[copy(t, 3, diag, me) for t in range(nt)],
            got_sib=[[copy(t, 0, sibling, me), copy(t, 4, sib(xn), me), copy(t, 5, sib(yn), me),
                      copy(t, 6, sib(diag), me)] for t in range(nt)])

    def start(self, ins, outs, scr):
        cps = self._copies(outs, scr)
        for t, (_, dt) in enumerate(self.shards):
            scr[t][...] = ins[t][...].astype(dt)
            for cp in [cps["local"][t]] + cps["own"][t]:
                cp.start()

    def stages(self):
        sizes = [s.size * jnp.dtype(dt).itemsize for s, dt in self.shards]
        done = [sum(sizes[:t + 1]) / sum(sizes) for t in range(len(sizes))]
        return [(self.mid_at * f, functools.partial(self.mid, t)) for t, f in enumerate(done)]

    def mid(self, t, ins, outs, scr):
        cps = self._copies(outs, scr)
        for cp in cps["got_xy"][t]:
            cp.wait_recv()
        for cp in [cps["relay"][t]] + cps["down"][t]:
            cp.start()

    def finish(self, ins, outs, scr):
        cps = self._copies(outs, scr)
        for t in range(len(self.shards)):
            cps["got_diag"][t].wait_recv()
            cps["down_diag"][t].start()
        for t in range(len(self.shards)):
            for cp in cps["got_sib"][t]:
                cp.wait_recv()
            for cp in cps["own"][t] + [cps["relay"][t]] + cps["down"][t] + [cps["down_diag"][t]]:
                cp.wait_send()
            cps["local"][t].wait()


def _run_carry(carry, name):
    def body(*refs):
        n_in, n_out = len(carry.in_arrays), len(carry.out_shape)
        ins, outs, scr = refs[:n_in], refs[n_in:n_in + n_out], refs[n_in + n_out:]
        carry.start(ins, outs, scr)
        for _, stage in carry.stages():
            stage(ins, outs, scr)
        carry.finish(ins, outs, scr)

    return pl.pallas_call(
        body, in_specs=carry.in_specs, out_specs=carry.out_specs, out_shape=carry.out_shape,
        scratch_shapes=carry.scratch, compiler_params=pltpu.CompilerParams(vmem_limit_bytes=VMEM_LIMIT),
        name=name)(*carry.in_arrays)


class _ExchangeCarry:
    def __init__(self, names, grads, mid_at=0.5):
        nt = len(grads)
        self.mid_at = mid_at
        self.names = names
        self.in_arrays = [g.reshape(4, 2, g.shape[0] // NDEV, g.shape[1]) for g in grads]
        self.in_specs = [HBM] * nt
        blocks = [g.shape[2:] for g in self.in_arrays]
        self.out_shape = [jax.ShapeDtypeStruct((3,) + b, BF16) for b in blocks]
        self.out_specs = [HBM] * nt
        self.scratch = ([pltpu.VMEM((4,) + b, BF16) for b in blocks] * 2 + [pltpu.VMEM(b, BF16) for b in blocks]
                        + [pltpu.SemaphoreType.DMA((nt, 3)), pltpu.SemaphoreType.DMA((nt, 3))]
                        + [pltpu.SemaphoreType.DMA((nt,))] * 4)

    def _copies(self, ins, outs, scr):
        nt = len(ins)
        theirs, own, relayed = scr[:nt], scr[nt:2 * nt], scr[2 * nt:3 * nt]
        send_sems, recv_sems, keep_sems, load_sems, swap_send, swap_recv = scr[3 * nt:]
        x, y, c = _place()
        q = lambda cx, cy: 2 * cx + cy
        near, far = (x ^ c, y ^ (1 - c)), (x ^ (1 - c), y ^ c)

        def remote(t, k, src, dst, chip):
            return pltpu.make_async_remote_copy(
                src_ref=src, dst_ref=dst, send_sem=send_sems.at[t, k], recv_sem=recv_sems.at[t, k],
                device_id=(*chip, c), device_id_type=MESH)

        return dict(
            swap=[pltpu.make_async_remote_copy(
                src_ref=ins[t].at[:, 1 - c], dst_ref=theirs[t], send_sem=swap_send.at[t], recv_sem=swap_recv.at[t],
                device_id=(x, y, 1 - c), device_id_type=MESH) for t in range(nt)],
            load=[pltpu.make_async_copy(ins[t].at[:, c], own[t], load_sems.at[t]) for t in range(nt)],
            keep=[pltpu.make_async_copy(own[t].at[q(x, y)], outs[t].at[0], keep_sems.at[t]) for t in range(nt)],
            direct=[remote(t, 0, own[t].at[q(*near)], outs[t].at[1], near) for t in range(nt)],
            relay=[remote(t, 1, own[t].at[q(1 - x, 1 - y)], relayed[t], near) for t in range(nt)],
            merged=[remote(t, 2, own[t].at[q(*far)], outs[t].at[2], far) for t in range(nt)],
            theirs=theirs, own=own, relayed=relayed, far=q(*far))

    EARLY_AT = 0.1

    def stages(self):
        return [(self.EARLY_AT, self.early), (self.mid_at, self.mid)]

    def start(self, ins, outs, scr):
        cps = self._copies(ins, outs, scr)
        for t in range(len(ins)):
            cps["swap"][t].start()
            cps["load"][t].start()

    def early(self, ins, outs, scr):
        cps = self._copies(ins, outs, scr)
        for t in range(len(ins)):
            cps["load"][t].wait()
            cps["swap"][t].wait_recv()
            own, theirs = cps["own"][t], cps["theirs"][t]
            for j in range(4):
                own[j] = (own[j].astype(F32) + theirs[j].astype(F32)).astype(BF16)
            for kind in ("relay", "direct", "keep"):
                cps[kind][t].start()

    def mid(self, ins, outs, scr):
        cps = self._copies(ins, outs, scr)
        for t in range(len(ins)):
            cps["relay"][t].wait_recv()
            own, far = cps["own"][t], cps["far"]
            own[far] = (own[far].astype(F32) + cps["relayed"][t][...].astype(F32)).astype(BF16)
            cps["merged"][t].start()

    def finish(self, ins, outs, scr):
        cps = self._copies(ins, outs, scr)
        for t in range(len(ins)):
            cps["swap"][t].wait_send()
            cps["direct"][t].wait()
            cps["relay"][t].wait_send()
            cps["merged"][t].wait()
            cps["keep"][t].wait()


class _Comm:
    def __init__(self, groups, opt):
        self.names = {tag: list(g) for tag, (g, _) in groups.items()}
        self.gathers = {tag: _GatherCarry(list(g.values()), mid_at) for tag, (g, mid_at) in groups.items()}
        self.reduced = {}
        self.last = None
        self.opt = opt
        self.updated = {}

    def gathered(self, tag, outs):
        return dict(zip(self.names[tag], outs))

    def reduce_start(self, grads):
        names = list(grads)
        return _ExchangeCarry(names, [grads[n] for n in names])

    def reduce_done(self, carry, outs):
        self.reduced.update(zip(carry.names, outs))

    def update_start(self, names):
        w, m, v = zip(*[self.opt[n] for n in names])
        return _AdamWCarry(names, [self.reduced[n] for n in names], w, m, v)

    def update_done(self, carry, outs):
        self.updated.update({n: outs[4 * k:4 * k + 4] for k, n in enumerate(carry.names)})


def _adamw_math(w, g, m, v):
    m = B1 * m + (1.0 - B1) * g
    v = B2 * v + (1.0 - B2) * (g * g)
    m_hat = m / (1.0 - B1 ** STEP)
    v_hat = v / (1.0 - B2 ** STEP)
    delta = -LR * (m_hat / (jnp.sqrt(v_hat) + AEPS) + WD * w)
    return delta, m, v


def _adamw_big(recvs, ws, ms, vs, name):
    nw = len(ws)

    def body(*refs):
        ins, outs = refs[:4 * nw], refs[4 * nw:]
        for k in range(nw):
            @pl.when(pl.program_id(0) // 2 == k)
            def _(k=k):
                r_ref, w_ref, m_ref, v_ref = ins[4 * k:4 * k + 4]
                g_ref, d_ref, mo_ref, vo_ref = outs[4 * k:4 * k + 4]
                g = r_ref[0].astype(F32)
                for q in range(1, 3):
                    g = g + r_ref[q].astype(F32)
                d, mn, vn = _adamw_math(w_ref[...], g, m_ref[...], v_ref[...])
                g_ref[...] = g
                d_ref[...] = d
                mo_ref[...] = mn
                vo_ref[...] = vn

    in_specs, out_specs, out_shape, args = [], [], [], []
    for k, (r, w, m, v) in enumerate(zip(recvs, ws, ms, vs)):
        rows, n = w.shape
        tr = rows // 2
        tile = lambda s, k=k: jnp.clip(s - 2 * k, 0, 1)
        row = pl.BlockSpec((tr, n), lambda s, tile=tile: (tile(s), 0))
        in_specs += [pl.BlockSpec((3, tr, n), lambda s, tile=tile: (0, tile(s), 0)), row, row, row]
        out_specs += [row] * 4
        out_shape += [jax.ShapeDtypeStruct(w.shape, F32)] * 4
        args += [r, w, m, v]
    res = pl.pallas_call(
        body, grid=(2 * nw,), in_specs=in_specs, out_specs=out_specs, out_shape=out_shape,
        compiler_params=_cp(("arbitrary",)), name=name)(*args)
    return [res[4 * k:4 * k + 4] for k in range(nw)]


class _AdamWCarry:
    def __init__(self, names, recvs, ws, ms, vs):
        self.names = names
        nw = len(ws)
        self.halves = [(k, j, w.shape[0] // 2) for k, w in enumerate(ws) for j in range(2)]
        self.in_arrays = [a for quad in zip(recvs, ws, ms, vs) for a in quad]
        self.in_specs = [HBM] * (4 * nw)
        self.out_shape = [jax.ShapeDtypeStruct(w.shape, F32) for w in ws for _ in range(4)]
        self.out_specs = [HBM] * (4 * nw)
        tr, n = max(h[2] for h in self.halves), ws[0].shape[1]
        self.scratch = [pltpu.VMEM((2, 3, tr, n), BF16), pltpu.VMEM((2, 3, tr, n), F32),
                        pltpu.VMEM((2, 4, tr, n), F32), pltpu.SemaphoreType.DMA((2, 4)),
                        pltpu.SemaphoreType.DMA((2, 4))]

    def _copies(self, c, ins, outs, scr):
        rbuf, fbuf, obuf, in_sems, out_sems = scr
        k, j, tr = self.halves[c]
        s, rows = c % 2, pl.ds(j * tr, tr)
        loads = [pltpu.make_async_copy(ins[4 * k].at[:, rows, :], rbuf.at[s, :, pl.ds(0, tr), :], in_sems.at[s, 0])]
        loads += [pltpu.make_async_copy(ins[4 * k + i].at[rows, :], fbuf.at[s, i - 1, pl.ds(0, tr), :],
                                        in_sems.at[s, i]) for i in range(1, 4)]
        stores = [pltpu.make_async_copy(obuf.at[s, q, pl.ds(0, tr), :], outs[4 * k + q].at[rows, :],
                                        out_sems.at[s, q]) for q in range(4)]
        return loads, stores

    def start(self, ins, outs, scr):
        for c in range(2):
            for cp in self._copies(c, ins, outs, scr)[0]:
                cp.start()

    def stages(self):
        n = len(self.halves)
        return [((c + 1) / (n + 1), functools.partial(self.half, c)) for c in range(n)]

    def half(self, c, ins, outs, scr):
        rbuf, fbuf, obuf = scr[:3]
        _, _, tr = self.halves[c]
        s = c % 2
        loads, stores = self._copies(c, ins, outs, scr)
        for cp in loads:
            cp.wait()
        if c >= 2:
            for cp in self._copies(c - 2, ins, outs, scr)[1]:
                cp.wait()
        g = rbuf[s, 0, 0:tr].astype(F32)
        for q in range(1, 3):
            g = g + rbuf[s, q, 0:tr].astype(F32)
        d, mn, vn = _adamw_math(fbuf[s, 0, 0:tr], g, fbuf[s, 1, 0:tr], fbuf[s, 2, 0:tr])
        for q, val in enumerate((g, d, mn, vn)):
            obuf[s, q, 0:tr] = val
        for cp in stores:
            cp.start()
        if c + 2 < len(self.halves):
            for cp in self._copies(c + 2, ins, outs, scr)[0]:
                cp.start()

    def finish(self, ins, outs, scr):
        n = len(self.halves)
        for c in range(max(n - 2, 0), n):
            for cp in self._copies(c, ins, outs, scr)[1]:
                cp.wait()


SMALL_NAMES = ("ffn1_norm", "mix_norm", "ffn2_norm", "conv_b", "conv_ln_g", "conv_ln_b", "q_norm", "k_norm")
SROWS = 16
LOSS_ROW = len(SMALL_NAMES)
CWT = (32, 128)


def _small_sums(gs, loss_row, gcw, carry=None):
    ns = len(SMALL_NAMES)
    widths = [g.shape[1] for g in gs]

    def body(*refs):
        it = iter(refs)
        take = lambda n: [next(it) for _ in range(n)]
        g_refs, (loss_ref, gcw_ref) = take(ns), take(2)
        cins = take(len(carry.in_arrays)) if carry else []
        tot_ref, ctot_ref = take(2)
        couts = take(len(carry.out_shape)) if carry else []
        send, slots, cslots, send_sems, recv_sems, csend_sems, crecv_sems = take(7)
        cscr = list(it)
        x, y, c = _place()
        me = 4 * x + 2 * y + c
        send[...] = jnp.zeros_like(send)
        for k in range(ns):
            send[k:k + 1, 0:widths[k]] = g_refs[k][...]
        send[LOSS_ROW:LOSS_ROW + 1, 0:128] = loss_ref[...]
        slots[me] = send[...]
        cslots[me] = gcw_ref[...]
        cps = []
        for k in range(1, NDEV):
            peer = (x ^ ((k >> 2) & 1), y ^ ((k >> 1) & 1), c ^ (k & 1))
            cps.append(pltpu.make_async_remote_copy(
                src_ref=send, dst_ref=slots.at[me], send_sem=send_sems.at[k - 1], recv_sem=recv_sems.at[k - 1],
                device_id=peer, device_id_type=MESH))
            cps.append(pltpu.make_async_remote_copy(
                src_ref=gcw_ref, dst_ref=cslots.at[me], send_sem=csend_sems.at[k - 1],
                recv_sem=crecv_sems.at[k - 1], device_id=peer, device_id_type=MESH))
        for cp in cps:
            cp.start()
        stages = [stage for _, stage in carry.stages()] if carry else []
        if carry:
            carry.start(cins, couts, cscr)
        for stage in stages[:1]:
            stage(cins, couts, cscr)
        for cp in cps:
            cp.wait()
        tot = slots[0]
        ctot = cslots[0, me]
        for j in range(1, NDEV):
            tot = tot + slots[j]
            ctot = ctot + cslots[j, me]
        tot_ref[...] = tot
        ctot_ref[...] = ctot
        for stage in stages[1:]:
            stage(cins, couts, cscr)
        if carry:
            carry.finish(cins, couts, cscr)

    args = [*gs, loss_row, gcw]
    out_shape = [jax.ShapeDtypeStruct((SROWS, D), F32), jax.ShapeDtypeStruct(CWT, F32)]
    res = pl.pallas_call(
        body, in_specs=[VMEM] * len(args) + (carry.in_specs if carry else []),
        out_specs=[VMEM] * 2 + (carry.out_specs if carry else []),
        out_shape=out_shape + (carry.out_shape if carry else []),
        scratch_shapes=[pltpu.VMEM((SROWS, D), F32), pltpu.VMEM((NDEV, SROWS, D), F32),
                        pltpu.VMEM((NDEV, NDEV) + CWT, F32)]
                       + [pltpu.SemaphoreType.DMA((NDEV - 1,))] * 4 + (carry.scratch if carry else []),
        name="small_sums")(*args, *(carry.in_arrays if carry else []))
    return res[0], res[1], res[2:]


def _adamw_small(tot, ctot, ws, ms, vs, wcw, mcw, vcw):
    ns = len(SMALL_NAMES)
    widths = [w.shape[1] for w in ws]

    def body(*refs):
        it = iter(refs)
        take = lambda n: [next(it) for _ in range(n)]
        (tot_ref, ctot_ref), w_refs, m_refs, v_refs = take(2), take(ns), take(ns), take(ns)
        wcw_ref, mcw_ref, vcw_ref = take(3)
        outs = [take(4) for _ in range(ns)]
        cw_outs, (loss_out,) = take(4), take(1)

        def step(g, w_ref, m_ref, v_ref, o):
            d, mn, vn = _adamw_math(w_ref[...], g, m_ref[...], v_ref[...])
            o[0][...], o[1][...], o[2][...], o[3][...] = g, d, mn, vn

        for k in range(ns):
            step(tot_ref[k:k + 1, 0:widths[k]], w_refs[k], m_refs[k], v_refs[k], outs[k])
        step(ctot_ref[0:CK, 0:HD][None], wcw_ref, mcw_ref, vcw_ref, cw_outs)
        loss_out[...] = tot_ref[LOSS_ROW:LOSS_ROW + 1, 0:128]

    args = [tot, ctot, *ws, *ms, *vs, wcw, mcw, vcw]
    out_shape = ([jax.ShapeDtypeStruct((1, n), F32) for n in widths for _ in range(4)]
                 + [jax.ShapeDtypeStruct((1, CK, HD), F32)] * 4 + [jax.ShapeDtypeStruct((1, 128), F32)])
    res = pl.pallas_call(
        body, in_specs=[VMEM] * len(args), out_specs=[VMEM] * len(out_shape), out_shape=out_shape,
        name="adamw_small")(*args)
    per = [res[4 * k:4 * k + 4] for k in range(ns)]
    return per, res[4 * ns:4 * ns + 4], res[-1]


def kernel(x, ffn1_norm, ffn1_w_gate, ffn1_w_up, ffn1_w_down, mix_norm, w_in, q_norm, k_norm, conv_w, conv_b, conv_ln_g, conv_ln_b, w_out, ffn2_norm, ffn2_w_gate, ffn2_w_up, ffn2_w_down, loss_target, m_ffn1_norm, m_ffn1_w_gate, m_ffn1_w_up, m_ffn1_w_down, m_mix_norm, m_w_in, m_q_norm, m_k_norm, m_conv_w, m_conv_b, m_conv_ln_g, m_conv_ln_b, m_w_out, m_ffn2_norm, m_ffn2_w_gate, m_ffn2_w_up, m_ffn2_w_down, v_ffn1_norm, v_ffn1_w_gate, v_ffn1_w_up, v_ffn1_w_down, v_mix_norm, v_w_in, v_q_norm, v_k_norm, v_conv_w, v_conv_b, v_conv_ln_g, v_conv_ln_b, v_w_out, v_ffn2_norm, v_ffn2_w_gate, v_ffn2_w_up, v_ffn2_w_down):
    P = dict(ffn1_norm=ffn1_norm, ffn1_w_gate=ffn1_w_gate, ffn1_w_up=ffn1_w_up, ffn1_w_down=ffn1_w_down,
             mix_norm=mix_norm, w_in=w_in, q_norm=q_norm, k_norm=k_norm, conv_w=conv_w, conv_b=conv_b,
             conv_ln_g=conv_ln_g, conv_ln_b=conv_ln_b, w_out=w_out, ffn2_norm=ffn2_norm,
             ffn2_w_gate=ffn2_w_gate, ffn2_w_up=ffn2_w_up, ffn2_w_down=ffn2_w_down)
    M = dict(ffn1_norm=m_ffn1_norm, ffn1_w_gate=m_ffn1_w_gate, ffn1_w_up=m_ffn1_w_up, ffn1_w_down=m_ffn1_w_down,
             mix_norm=m_mix_norm, w_in=m_w_in, q_norm=m_q_norm, k_norm=m_k_norm, conv_w=m_conv_w, conv_b=m_conv_b,
             conv_ln_g=m_conv_ln_g, conv_ln_b=m_conv_ln_b, w_out=m_w_out, ffn2_norm=m_ffn2_norm,
             ffn2_w_gate=m_ffn2_w_gate, ffn2_w_up=m_ffn2_w_up, ffn2_w_down=m_ffn2_w_down)
    V = dict(ffn1_norm=v_ffn1_norm, ffn1_w_gate=v_ffn1_w_gate, ffn1_w_up=v_ffn1_w_up, ffn1_w_down=v_ffn1_w_down,
             mix_norm=v_mix_norm, w_in=v_w_in, q_norm=v_q_norm, k_norm=v_k_norm, conv_w=v_conv_w, conv_b=v_conv_b,
             conv_ln_g=v_conv_ln_g, conv_ln_b=v_conv_ln_b, w_out=v_w_out, ffn2_norm=v_ffn2_norm,
             ffn2_w_gate=v_ffn2_w_gate, ffn2_w_up=v_ffn2_w_up, ffn2_w_down=v_ffn2_w_down)
    order = ["ffn1_norm", "ffn1_w_gate", "ffn1_w_up", "ffn1_w_down", "mix_norm", "w_in", "q_norm", "k_norm",
             "conv_w", "conv_b", "conv_ln_g", "conv_ln_b", "w_out", "ffn2_norm", "ffn2_w_gate", "ffn2_w_up",
             "ffn2_w_down"]
    B, S, _ = x.shape
    T = B * S

    bigs = [("wg1", "ffn1_w_gate", True), ("wu1", "ffn1_w_up", True), ("wd1", "ffn1_w_down", False),
            ("win", "w_in", True), ("wout", "w_out", False),
            ("wg2", "ffn2_w_gate", True), ("wu2", "ffn2_w_up", True), ("wd2", "ffn2_w_down", False)]
    hm = lambda a, tr: jnp.transpose(a[0]) if tr else a[0]
    cw_pad = jnp.zeros((32, 128), F32).at[0:CK, 0:HD].set(conv_w[0])
    shard = {ln: (hm(P[pn], tr), BF16) for ln, pn, tr in bigs}
    gathered = _run_carry(_GatherCarry([shard["wg1"], shard["wu1"], (cw_pad, F32)]), "gather_first")
    W = {"wg1": gathered[0], "wu1": gathered[1]}
    cwg = gathered[2].reshape(NDEV, 32, 128)[:, 0:CK, 0:HD]
    W["conv_w"] = jnp.transpose(cwg, (1, 0, 2)).reshape(CK, DC)
    norms = {n: P[n] for n in SMALL_NAMES}
    comm = _Comm({"down_in": ({n: shard[n] for n in ("wd1", "win")}, 0.5),
                  "ffn2": ({n: shard[n] for n in ("wg2", "wu2", "wd2", "wout")}, 0.5)},
                 opt={ln: (hm(P[pn], tr), hm(M[pn], tr), hm(V[pn], tr)) for ln, pn, tr in bigs})

    loss_part, gx, _, small = _local_step(x.reshape(T, D), loss_target.reshape(T, D), norms, W, B, S, comm)

    G, Dl, Mn, Vn = {}, {}, {}, {}
    dcw = small["conv_w"].reshape(CK, NDEV, HD).transpose(1, 0, 2)
    loss_row = jnp.zeros((1, 128), F32).at[0, 0].set(loss_part)
    gcw = jnp.pad(dcw, ((0, 0), (0, CWT[0] - CK), (0, CWT[1] - HD)))
    tot, ctot, got = _small_sums([small[n] for n in SMALL_NAMES], loss_row, gcw, carry=comm.last)
    comm.reduce_done(comm.last, got)
    per, cw_outs, loss_out = _adamw_small(
        tot, ctot, [P[n] for n in SMALL_NAMES], [M[n] for n in SMALL_NAMES], [V[n] for n in SMALL_NAMES],
        P["conv_w"], M["conv_w"], V["conv_w"])
    loss = loss_out[0, 0]
    for n, outs in zip(SMALL_NAMES, per):
        G[n], Dl[n], Mn[n], Vn[n] = outs
    G["conv_w"], Dl["conv_w"], Mn["conv_w"], Vn["conv_w"] = cw_outs

    group = ("wd1", "wg1", "wu1")
    w, m, v = zip(*[comm.opt[ln] for ln in group])
    comm.updated.update(zip(group, _adamw_big([comm.reduced[ln] for ln in group], w, m, v, "adamw_ffn1")))
    for ln, pn, tr in bigs:
        G[pn], Dl[pn], Mn[pn], Vn[pn] = [(jnp.transpose(o) if tr else o)[None] for o in comm.updated[ln]]

    return (loss, gx.reshape(B, S, D), *[G[n] for n in order], *[Dl[n] for n in order],
            *[Mn[n] for n in order], *[Vn[n] for n in order])
```

```python
import functools

import jax
import jax.numpy as jnp
from jax import lax
from jax.experimental import pallas as pl
from jax.experimental.pallas import tpu as pltpu

F32 = jnp.float32
BF16 = jnp.bfloat16

D = 1024
FF = 2816
HD = 64
DA = 512
DC = 512
DIN = 2560
CK = 31
BLK = 128
DILS = (1, 4, 16)
EPS = 1e-6
NDEV = 8
MESH = pl.DeviceIdType.MESH

LR, B1, B2, AEPS, WD, STEP = 0.001, 0.9, 0.999, 1e-08, 0.01, 10

NT = (((1,), (1,)), ((), ()))
TN = (((0,), (0,)), ((), ()))

VMEM_LIMIT = 60 * 1024 * 1024


def _cp(sem=None):
    return pltpu.CompilerParams(dimension_semantics=sem, vmem_limit_bytes=VMEM_LIMIT)


def _sigmoid(x):
    return 0.5 * (jnp.tanh(0.5 * x) + 1.0)


def _pallas(body, args, *, grid, in_specs, out_specs, out_shape, scratch_shapes, sem, name, carry=None):
    if carry is None:
        outs = pl.pallas_call(body, grid=grid, in_specs=in_specs, out_specs=out_specs, out_shape=out_shape,
                              scratch_shapes=scratch_shapes, compiler_params=_cp(sem), name=name)(*args)
        return outs, None
    n_in, n_out, n_scr = len(in_specs), len(out_shape), len(scratch_shapes)
    c_in, c_out = len(carry.in_arrays), len(carry.out_shape)

    def wrapped(*refs):
        ins, refs = refs[:n_in], refs[n_in:]
        cins, refs = refs[:c_in], refs[c_in:]
        outs, refs = refs[:n_out], refs[n_out:]
        couts, refs = refs[:c_out], refs[c_out:]
        scr, cscr = refs[:n_scr], refs[n_scr:]
        ids = [pl.program_id(a) for a in range(len(grid))]
        step = ids[0]
        for i, n in zip(ids[1:], grid[1:]):
            step = step * n + i
        steps = functools.reduce(lambda a, b: a * b, grid)

        @pl.when(step == 0)
        def _():
            carry.start(cins, couts, cscr)

        body(*ins, *outs, *scr)

        for frac, stage in carry.stages():
            @pl.when(step == int(steps * frac))
            def _(stage=stage):
                stage(cins, couts, cscr)

        @pl.when(step == steps - 1)
        def _():
            carry.finish(cins, couts, cscr)

    outs = pl.pallas_call(
        wrapped, grid=grid, in_specs=list(in_specs) + carry.in_specs, out_specs=list(out_specs) + carry.out_specs,
        out_shape=list(out_shape) + carry.out_shape, scratch_shapes=list(scratch_shapes) + carry.scratch,
        compiler_params=_cp(("arbitrary",) * len(grid)), name=name)(*args, *carry.in_arrays)
    return outs[:n_out], outs[n_out:]


FC = 256


def _resident(shape):
    return pl.BlockSpec(shape, lambda *_: (0,) * len(shape), pipeline_mode=pl.Buffered(1))


def _mix_out_ffn_loss(h1, attn, conv, wout, gain, wg, wu, wd, target, name):
    T = h1.shape[0]
    tm = 512
    nt = T // tm

    def body(h1_ref, at_ref, cv_ref, wo_ref, gain_ref, wg_ref, wu_ref, wd_ref, t_ref,
             h2_ref, n_ref, g_ref, u_ref, dout_ref, dyb_ref, sq_ref, a_hbm, a_scr, a_sem):
        t = pl.program_id(0)
        a_out = lambda i: pltpu.make_async_copy(a_scr, a_hbm.at[pl.ds(pl.multiple_of(i * tm, tm), tm), :], a_sem)

        @pl.when(t > 0)
        def _():
            a_out(t - 1).wait()

        xv = (h1_ref[...]
              + jnp.dot(at_ref[...], wo_ref[0:DA, :], preferred_element_type=F32)
              + jnp.dot(cv_ref[...], wo_ref[DA:D, :], preferred_element_type=F32))
        h2_ref[...] = xv
        r = lax.rsqrt(jnp.mean(xv * xv, axis=-1, keepdims=True) + EPS)
        n_ref[...] = (xv * r * gain_ref[...]).astype(BF16)
        for c in range(FF // FC):
            cols = slice(c * FC, (c + 1) * FC)
            nb = n_ref[...]
            g = lax.dot_general(nb, wg_ref[cols, :], NT, preferred_element_type=F32)
            u = lax.dot_general(nb, wu_ref[cols, :], NT, preferred_element_type=F32)
            g_ref[:, cols] = g.astype(BF16)
            u_ref[:, cols] = u.astype(BF16)
            a_scr[:, cols] = (g * _sigmoid(g) * u).astype(BF16)
        a_out(t).start()
        e = h2_ref[...] + 0.5 * jnp.dot(a_scr[...], wd_ref[...], preferred_element_type=F32) - t_ref[...]
        dout = e * (1.0 / D)
        dout_ref[...] = dout
        dyb_ref[...] = (0.5 * dout).astype(BF16)
        sq_ref[...] = jnp.sum(e * e, axis=0, keepdims=True)[None]

        @pl.when(t == nt - 1)
        def _():
            a_out(t).wait()

    row = pl.BlockSpec((tm, D), lambda t: (t, 0))
    half = pl.BlockSpec((tm, DA), lambda t: (t, 0))
    wide = pl.BlockSpec((tm, FF), lambda t: (t, 0))
    outs, _ = _pallas(
        body, (h1, attn, conv, wout, gain, wg, wu, wd, target), grid=(nt,),
        in_specs=[row, half, half, _resident((D, D)), _resident((1, D)), _resident((FF, D)), _resident((FF, D)),
                  _resident((FF, D)), row],
        out_specs=[row, row, wide, wide, row, row, pl.BlockSpec((1, 1, D), lambda t: (t, 0, 0)), HBM],
        out_shape=[jax.ShapeDtypeStruct((T, D), F32), jax.ShapeDtypeStruct((T, D), BF16)]
                  + [jax.ShapeDtypeStruct((T, FF), BF16)] * 2
                  + [jax.ShapeDtypeStruct((T, D), F32), jax.ShapeDtypeStruct((T, D), BF16),
                     jax.ShapeDtypeStruct((nt, 1, D), F32), jax.ShapeDtypeStruct((T, FF), BF16)],
        scratch_shapes=[pltpu.VMEM((tm, FF), BF16), pltpu.SemaphoreType.DMA(())],
        sem=("arbitrary",), name=name)
    return outs


def _ffn_gate_up(x, gain, wg, wu, name, carry=None):
    T = x.shape[0]
    tm = 512

    def body(x_ref, gain_ref, wg_ref, wu_ref, n_ref, g_ref, u_ref, a_ref):
        xv = x_ref[...]
        r = lax.rsqrt(jnp.mean(xv * xv, axis=-1, keepdims=True) + EPS)
        n_ref[...] = (xv * r * gain_ref[...]).astype(BF16)
        for c in range(FF // FC):
            cols = slice(c * FC, (c + 1) * FC)
            nb = n_ref[...]
            g = lax.dot_general(nb, wg_ref[cols, :], NT, preferred_element_type=F32)
            u = lax.dot_general(nb, wu_ref[cols, :], NT, preferred_element_type=F32)
            g_ref[:, cols] = g.astype(BF16)
            u_ref[:, cols] = u.astype(BF16)
            a_ref[:, cols] = (g * _sigmoid(g) * u).astype(BF16)

    row = pl.BlockSpec((tm, D), lambda t: (t, 0))
    wide = pl.BlockSpec((tm, FF), lambda t: (t, 0))
    return _pallas(
        body, (x, gain, wg, wu), grid=(T // tm,),
        in_specs=[row, _resident((1, D)), _resident((FF, D)), _resident((FF, D))],
        out_specs=[row, wide, wide, wide],
        out_shape=[jax.ShapeDtypeStruct((T, D), BF16)] + [jax.ShapeDtypeStruct((T, FF), BF16)] * 3,
        scratch_shapes=[], sem=("parallel",), name=name, carry=carry)


def _ffn_down_mix_in(x, a, wd, gain, win, name):
    T = x.shape[0]
    tm = 512

    def body(x_ref, a_ref, wd_ref, gain_ref, win_ref, h_ref, u_ref, n_ref):
        hv = x_ref[...] + 0.5 * jnp.dot(a_ref[...], wd_ref[...], preferred_element_type=F32)
        h_ref[...] = hv
        r = lax.rsqrt(jnp.mean(hv * hv, axis=-1, keepdims=True) + EPS)
        n_ref[...] = (hv * r * gain_ref[...]).astype(BF16)
        u_ref[...] = lax.dot_general(n_ref[...], win_ref[...], NT, preferred_element_type=F32).astype(BF16)

    row = pl.BlockSpec((tm, D), lambda t: (t, 0))
    wide = pl.BlockSpec((tm, FF), lambda t: (t, 0))
    outs, _ = _pallas(
        body, (x, a, wd, gain, win), grid=(T // tm,),
        in_specs=[row, wide, _resident((FF, D)), _resident((1, D)), _resident((DIN, D))],
        out_specs=[row, pl.BlockSpec((tm, DIN), lambda t: (t, 0)), row],
        out_shape=[jax.ShapeDtypeStruct((T, D), F32), jax.ShapeDtypeStruct((T, DIN), BF16),
                   jax.ShapeDtypeStruct((T, D), BF16)],
        scratch_shapes=[], sem=("parallel",), name=name)
    return outs


def _ffn_bwd_act(dyb, g, u, x, dout, gain, wg, wu, wd, name, carry=None):
    T = x.shape[0]
    tm = 256
    nt = T // tm

    def body(dy_ref, g_ref, u_ref, x_ref, dout_ref, gain_ref, wg_ref, wu_ref, wd_ref,
             dg_ref, du_ref, dx_ref, dgn_ref):
        for c in range(FF // FC):
            cols = slice(c * FC, (c + 1) * FC)
            da = lax.dot_general(dy_ref[...], wd_ref[cols, :], NT, preferred_element_type=F32)
            gv = g_ref[:, cols].astype(F32)
            uv = u_ref[:, cols].astype(F32)
            sg = _sigmoid(gv)
            dg_ref[:, cols] = (da * uv * (sg * (1.0 + gv * (1.0 - sg)))).astype(BF16)
            du_ref[:, cols] = (da * (gv * sg)).astype(BF16)
        dn = (jnp.dot(dg_ref[...], wg_ref[...], preferred_element_type=F32)
              + jnp.dot(du_ref[...], wu_ref[...], preferred_element_type=F32))
        dx, dgain = _rms_bwd_rows(dn, x_ref[...], gain_ref[...])
        dx_ref[...] = dout_ref[...] + dx

        @pl.when(pl.program_id(0) == 0)
        def _():
            dgn_ref[...] = jnp.zeros_like(dgn_ref)

        dgn_ref[...] += dgain[None]

    row = pl.BlockSpec((tm, D), lambda t: (t, 0))
    wide = pl.BlockSpec((tm, FF), lambda t: (t, 0))
    return _pallas(
        body, (dyb, g, u, x, dout, gain, wg, wu, wd), grid=(nt,),
        in_specs=[row, wide, wide, row, row, _resident((1, D)), _resident((FF, D)), _resident((FF, D)),
                  _resident((FF, D))],
        out_specs=[wide, wide, row, pl.BlockSpec((1, 1, D), lambda t: (0, 0, 0))],
        out_shape=[jax.ShapeDtypeStruct((T, FF), BF16)] * 2
                  + [jax.ShapeDtypeStruct((T, D), F32), jax.ShapeDtypeStruct((1, 1, D), F32)],
        scratch_shapes=[], sem=("arbitrary",), name=name, carry=carry)


def _ffn_bwd_w(lhs, rhs, name, carry=None):
    T = rhs.shape[0]
    tf = 256

    def body(l_ref, r_ref, dw_ref):
        dw_ref[...] = lax.dot_general(l_ref[...], r_ref[...], TN, preferred_element_type=F32).astype(BF16)

    (dw,), got = _pallas(
        body, (lhs, rhs), grid=(FF // tf,),
        in_specs=[pl.BlockSpec((T, tf), lambda f: (0, f)), _resident((T, D))],
        out_specs=[pl.BlockSpec((tf, D), lambda f: (f, 0))], out_shape=[jax.ShapeDtypeStruct((FF, D), BF16)],
        scratch_shapes=[], sem=("parallel",), name=name, carry=carry)
    return dw, got


def _rms_bwd_rows(dn, xv, gain):
    r = lax.rsqrt(jnp.mean(xv * xv, axis=-1, keepdims=True) + EPS)
    xhat = xv * r
    dxhat = dn * gain
    dx = r * (dxhat - xhat * jnp.mean(dxhat * xhat, axis=-1, keepdims=True))
    return dx, jnp.sum(dn * xhat, axis=0, keepdims=True)


def _mix_out_bwd(dh, attn, conv, wout):
    T = dh.shape[0]
    tm = 512
    nt = T // tm

    def body(dh_ref, a_ref, c_ref, w_ref, da_ref, dc_ref, dw_ref, acc_scr):
        t = pl.program_id(0)

        @pl.when(t == 0)
        def _():
            acc_scr[...] = jnp.zeros_like(acc_scr)

        dhb = dh_ref[...].astype(BF16)
        dmix = lax.dot_general(dhb, w_ref[...], NT, preferred_element_type=F32)
        da_ref[...] = dmix[:, 0:DA].astype(BF16)
        dc_ref[...] = dmix[:, DA:D].astype(BF16)
        acc_scr[0:DA, :] += lax.dot_general(a_ref[...], dhb, TN, preferred_element_type=F32)
        acc_scr[DA:D, :] += lax.dot_general(c_ref[...], dhb, TN, preferred_element_type=F32)

        @pl.when(t == nt - 1)
        def _():
            dw_ref[...] = acc_scr[...].astype(BF16)

    row = pl.BlockSpec((tm, D), lambda t: (t, 0))
    half = pl.BlockSpec((tm, DA), lambda t: (t, 0))
    full = pl.BlockSpec((D, D), lambda t: (0, 0))
    return pl.pallas_call(
        body, grid=(nt,), in_specs=[row, half, half, full], out_specs=[half, half, full],
        out_shape=[jax.ShapeDtypeStruct((T, DA), BF16)] * 2 + [jax.ShapeDtypeStruct((D, D), BF16)],
        scratch_shapes=[pltpu.VMEM((D, D), F32)],
        compiler_params=_cp(("arbitrary",)), name="mix_out_bwd")(dh, attn, conv, wout)


def _mix_in_bwd(dparts, win, nb, h, dh, gain):
    T = h.shape[0]
    tm = 512
    nt = T // tm

    def body(d0, d1, d2, d3, d4, w_ref, n_ref, h_ref, dh_ref, gain_ref,
             dw_ref, dx_ref, dyb_ref, dg_ref, acc_scr):
        t = pl.program_id(0)

        @pl.when(t == 0)
        def _():
            acc_scr[...] = jnp.zeros_like(acc_scr)

        n = n_ref[...]
        dn = jnp.zeros((tm, D), F32)
        for i, d_ref in enumerate((d0, d1, d2, d3, d4)):
            dv = d_ref[...]
            dn = dn + jnp.dot(dv, w_ref[i * DA:(i + 1) * DA, :], preferred_element_type=F32)
            acc_scr[i * DA:(i + 1) * DA, :] += lax.dot_general(dv, n, TN, preferred_element_type=F32)
        dx, dgain = _rms_bwd_rows(dn, h_ref[...], gain_ref[...])
        tot = dh_ref[...] + dx
        dx_ref[...] = tot
        dyb_ref[...] = (0.5 * tot).astype(BF16)
        dg_ref[...] = dgain[None]

        @pl.when(t == nt - 1)
        def _():
            dw_ref[...] = acc_scr[...].astype(BF16)

    row = pl.BlockSpec((tm, D), lambda t: (t, 0))
    half = pl.BlockSpec((tm, DA), lambda t: (t, 0))
    full = pl.BlockSpec((DIN, D), lambda t: (0, 0))
    return pl.pallas_call(
        body, grid=(nt,),
        in_specs=[half] * 5 + [full, row, row, row, pl.BlockSpec((1, D), lambda t: (0, 0))],
        out_specs=[full, row, row, pl.BlockSpec((1, 1, D), lambda t: (t, 0, 0))],
        out_shape=[jax.ShapeDtypeStruct((DIN, D), BF16), jax.ShapeDtypeStruct((T, D), F32),
                   jax.ShapeDtypeStruct((T, D), BF16), jax.ShapeDtypeStruct((nt, 1, D), F32)],
        scratch_shapes=[pltpu.VMEM((DIN, D), F32)],
        compiler_params=_cp(("arbitrary",)), name="mix_in_bwd")(*dparts, win, nb, h, dh, gain)


def _head_masks():
    lane = lax.broadcasted_iota(jnp.int32, (1, 2 * HD), 1)
    m0 = lane < HD
    return m0, jnp.logical_not(m0)


def _stack_heads(v, m0, m1):
    z = jnp.zeros_like(v)
    return jnp.concatenate([jnp.where(m0, v, z), jnp.where(m1, v, z)], axis=0)


def _unstack_heads(v2, m0):
    return jnp.where(m0, v2[0:BLK], v2[BLK:2 * BLK])


def _head_sums(xv):
    ri = lax.broadcasted_iota(jnp.int32, (2 * HD, 2 * HD), 0)
    ci = lax.broadcasted_iota(jnp.int32, (2 * HD, 2 * HD), 1)
    ones = jnp.where((ri < HD) == (ci < HD), 1.0, 0.0).astype(BF16)
    hi = xv.astype(BF16)
    lo = (xv - hi.astype(F32)).astype(BF16)
    return (jnp.dot(hi, ones, preferred_element_type=F32) + jnp.dot(lo, ones, preferred_element_type=F32))


def _head_rms(xv):
    return lax.rsqrt(_head_sums(xv * xv) * (1.0 / HD) + EPS)


def _band_mask(first):
    qi = lax.broadcasted_iota(jnp.int32, (BLK, 2 * BLK), 0)
    ci = lax.broadcasted_iota(jnp.int32, (BLK, 2 * BLK), 1)
    band = (ci >= qi) & (ci <= qi + BLK)
    return band & ((ci >= BLK) | jnp.logical_not(first))


def _block_rows(j, d, seg):
    r, n = j // seg, j % seg
    start = r + (d * BLK) * n
    first = n == 0
    prev = jnp.where(first, start, start - d * BLK)
    return pl.ds(start, BLK, stride=d), pl.ds(prev, BLK, stride=d), first


def _block_keys(refs, cur, prev, first, single):
    if single:
        qi = lax.broadcasted_iota(jnp.int32, (BLK, BLK), 0)
        ci = lax.broadcasted_iota(jnp.int32, (BLK, BLK), 1)
        return [r[cur, :].astype(BF16) for r in refs], ci <= qi
    return ([jnp.concatenate([r[prev, :], r[cur, :]], axis=0).astype(BF16) for r in refs], _band_mask(first))


def _attn_fwd(u, qg2, kg2, B, S, carry=None):
    T = B * S
    NB = S // BLK
    scale = HD ** -0.5

    def body(q_ref, k_ref, v_ref, qg_ref, kg_ref, o_ref, lse_ref, qn, kn, vn, os_, ls_):
        m0, m1 = _head_masks()
        qv = q_ref[...].astype(F32)
        qn[...] = qv * _head_rms(qv) * (qg_ref[...] * scale)
        kv = k_ref[...].astype(F32)
        kn[...] = kv * _head_rms(kv) * kg_ref[...]
        vn[...] = v_ref[...].astype(F32)

        for i, d in enumerate(DILS):
            seg = NB // d

            def blk(j, c, i=i, d=d, seg=seg):
                cur, prev, first = _block_rows(j, d, seg)
                q2 = _stack_heads(qn[cur, :].astype(BF16), m0, m1)
                (kk, vv), mask = _block_keys((kn, vn), cur, prev, first, False)
                s = lax.dot_general(q2, kk, NT, preferred_element_type=F32)
                s = jnp.where(jnp.concatenate([mask, mask], axis=0), s, -1e30)
                mx = jnp.max(s, axis=-1, keepdims=True)
                p = jnp.exp(s - mx)
                l = jnp.sum(p, axis=-1, keepdims=True)
                o2 = jnp.dot((p * (1.0 / l)).astype(BF16), vv, preferred_element_type=F32)
                os_[i, cur, :] = _unstack_heads(o2, m0)
                ls_[i, cur, :] = _unstack_heads(mx + jnp.log(l), m0)
                return c

            lax.fori_loop(0, NB, blk, 0, unroll=8)

        def comb(c, carry):
            rows = pl.ds(pl.multiple_of(c * 256, 256), 256)
            l0, l1, l2 = ls_[0, rows, :], ls_[1, rows, :], ls_[2, rows, :]
            mx = jnp.maximum(jnp.maximum(l0, l1), l2)
            e0, e1, e2 = jnp.exp(l0 - mx), jnp.exp(l1 - mx), jnp.exp(l2 - mx)
            tot = e0 + e1 + e2
            inv = 1.0 / tot
            o = (e0 * os_[0, rows, :] + e1 * os_[1, rows, :] + e2 * os_[2, rows, :]) * inv
            o_ref[rows, :] = o.astype(BF16)
            lse_ref[rows, :] = mx + jnp.log(tot)
            return carry

        lax.fori_loop(0, S // 256, comb, 0)

    pair = 2 * HD
    blk_spec = lambda off: pl.BlockSpec((S, pair), lambda b, p, off=off: (b, off + p))
    gspec = pl.BlockSpec((1, pair), lambda b, p: (0, 0))
    return _pallas(
        body, (u, u, u, qg2, kg2), grid=(B, DA // pair),
        in_specs=[blk_spec(0), blk_spec(DA // pair), blk_spec(2 * DA // pair), gspec, gspec],
        out_specs=[blk_spec(0), blk_spec(0)],
        out_shape=[jax.ShapeDtypeStruct((T, DA), BF16), jax.ShapeDtypeStruct((T, DA), F32)],
        scratch_shapes=[pltpu.VMEM((S, pair), F32)] * 3 + [pltpu.VMEM((3, S, pair), F32)] * 2,
        sem=("parallel", "parallel"), name="attn_fwd", carry=carry)


def _attn_bwd(u, attn, dattn, lse, qg2, kg2, B, S, carry=None):
    T = B * S
    NB = S // BLK
    scale = HD ** -0.5
    pair = 2 * HD

    def body(q_ref, k_ref, v_ref, o_ref, do_ref, lse_ref, qg_ref, kg_ref,
             dq_ref, dk_ref, dv_ref, dgn_ref,
             qn, kn, vn, don, ldl, accq, acck, accv, rq, rk):
        m0, m1 = _head_masks()
        lane = lax.broadcasted_iota(jnp.int32, (1, pair), 1)
        qv = q_ref[...].astype(F32)
        rq[...] = _head_rms(qv)
        qn[...] = qv * rq[...] * (qg_ref[...] * scale)
        kv = k_ref[...].astype(F32)
        rk[...] = _head_rms(kv)
        kn[...] = kv * rk[...] * kg_ref[...]
        vn[...] = v_ref[...].astype(F32)
        dov = do_ref[...].astype(F32)
        don[...] = dov
        ldl[...] = jnp.where((lane % HD) < HD // 2, lse_ref[...], _head_sums(dov * o_ref[...].astype(F32)))

        for i, d in enumerate(DILS):
            seg = NB // d

            def blk(j, c, i=i, d=d, seg=seg):
                cur, prev, first = _block_rows(j, d, seg)
                q2 = _stack_heads(qn[cur, :].astype(BF16), m0, m1)
                do2 = _stack_heads(don[cur, :].astype(BF16), m0, m1)
                (kk, vv), mask = _block_keys((kn, vn), cur, prev, first, seg == 1)
                ldv = ldl[cur, :]
                lse2 = jnp.concatenate([ldv[:, 0:1], ldv[:, HD:HD + 1]], axis=0)
                dl2 = jnp.concatenate([ldv[:, HD // 2:HD // 2 + 1], ldv[:, HD + HD // 2:HD + HD // 2 + 1]], axis=0)
                s = lax.dot_general(q2, kk, NT, preferred_element_type=F32)
                p = jnp.where(jnp.concatenate([mask, mask], axis=0), jnp.exp(s - lse2), 0.0)
                dp = lax.dot_general(do2, vv, NT, preferred_element_type=F32)
                ds = (p * (dp - dl2)).astype(BF16)
                dq_acc = _unstack_heads(jnp.dot(ds, kk, preferred_element_type=F32), m0)
                dk_acc = lax.dot_general(ds, q2, TN, preferred_element_type=F32)
                dv_acc = lax.dot_general(p.astype(BF16), do2, TN, preferred_element_type=F32)
                if i == 0:
                    accq[cur, :] = dq_acc
                    acck[cur, :] = dk_acc[BLK:2 * BLK]
                    accv[cur, :] = dv_acc[BLK:2 * BLK]
                    acck[prev, :] += dk_acc[0:BLK]
                    accv[prev, :] += dv_acc[0:BLK]
                elif seg == 1:
                    accq[cur, :] += dq_acc
                    acck[cur, :] += dk_acc
                    accv[cur, :] += dv_acc
                else:
                    accq[cur, :] += dq_acc
                    acck[prev, :] += dk_acc[0:BLK]
                    acck[cur, :] += dk_acc[BLK:2 * BLK]
                    accv[prev, :] += dv_acc[0:BLK]
                    accv[cur, :] += dv_acc[BLK:2 * BLK]
                return c

            lax.fori_loop(0, NB, blk, 0, unroll=8)

        def norm_bwd(x_ref, r_ref, dn, gain):
            r = r_ref[...]
            xhat = x_ref[...].astype(F32) * r
            dxhat = dn * gain
            dx = r * (dxhat - xhat * (_head_sums(dxhat * xhat) * (1.0 / HD)))
            return dx, jnp.sum(dn * xhat, axis=0, keepdims=True)

        dq, dgq = norm_bwd(q_ref, rq, accq[...], qg_ref[...] * scale)
        dk, dgk = norm_bwd(k_ref, rk, acck[...], kg_ref[...])
        dq_ref[...] = dq.astype(BF16)
        dk_ref[...] = dk.astype(BF16)
        dv_ref[...] = accv[...].astype(BF16)
        dgn_ref[...] = jnp.concatenate([dgq * scale, dgk, jnp.zeros((6, pair), F32)], axis=0)[None]

    blk_spec = lambda off: pl.BlockSpec((S, pair), lambda b, p, off=off: (b, off + p))
    gspec = pl.BlockSpec((1, pair), lambda b, p: (0, 0))
    np_ = DA // pair
    return _pallas(
        body, (u, u, u, attn, dattn, lse, qg2, kg2), grid=(B, np_),
        in_specs=[blk_spec(0), blk_spec(np_), blk_spec(2 * np_), blk_spec(0), blk_spec(0), blk_spec(0),
                  gspec, gspec],
        out_specs=[blk_spec(0), blk_spec(0), blk_spec(0),
                   pl.BlockSpec((1, 8, pair), lambda b, p: (b * np_ + p, 0, 0))],
        out_shape=[jax.ShapeDtypeStruct((T, DA), BF16)] * 3 + [jax.ShapeDtypeStruct((B * np_, 8, pair), F32)],
        scratch_shapes=[pltpu.VMEM((S, pair), F32)] * 10,
        sem=("parallel", "parallel"), name="attn_bwd", carry=carry)


CT = 32
CPAD = 32


def _shifted(win, offsets):
    rolled, out = {}, {}
    n = win.shape[0]
    for o in offsets:
        sub = o % 8
        if sub not in rolled:
            rolled[sub] = win if sub == 0 else pltpu.roll(win, n - sub, 0)
        out[o] = rolled[sub][o - sub:o - sub + CT, :]
    return out


def _ln_fwd(y, g, b):
    mu = jnp.mean(y, axis=-1, keepdims=True)
    yc = y - mu
    rstd = lax.rsqrt(jnp.mean(yc * yc, axis=-1, keepdims=True) + EPS)
    xhat = yc * rstd
    return xhat, rstd, xhat * g + b


def _fill_glu(ca_ref, cg_ref, glu, S):
    glu[pl.ds(0, CPAD), :] = jnp.zeros((CPAD, DC), F32)

    def fill(i, c):
        rows = pl.ds(pl.multiple_of(i * 256, 256), 256)
        a = ca_ref[rows, :].astype(F32)
        gt = cg_ref[rows, :].astype(F32)
        glu[pl.ds(pl.multiple_of(CPAD + i * 256, CT), 256), :] = a * _sigmoid(gt)
        return c

    lax.fori_loop(0, S // 256, fill, 0)


def _conv_fwd(u, cw, cb, lg, lb, B, S):
    T = B * S

    def body(ca_ref, cg_ref, w_ref, b_ref, lg_ref, lb_ref, o_ref, y_ref, glu):
        _fill_glu(ca_ref, cg_ref, glu, S)

        def step(i, c):
            t0 = pl.multiple_of(i * CT, CT)
            win = glu[pl.ds(t0, 2 * CT), :]
            acc = jnp.zeros((CT, DC), F32) + b_ref[...]
            taps = _shifted(win, [k + 2 for k in range(CK)])
            for k in range(CK):
                acc = acc + taps[k + 2] * w_ref[k:k + 1, :]
            y_ref[pl.ds(t0, CT), :] = acc
            _, _, z = _ln_fwd(acc, lg_ref[...], lb_ref[...])
            o_ref[pl.ds(t0, CT), :] = (z * _sigmoid(z)).astype(BF16)
            return c

        lax.fori_loop(0, S // CT, step, 0, unroll=4)

    vec = pl.BlockSpec((1, DC), lambda b: (0, 0))
    return pl.pallas_call(
        body, grid=(B,),
        in_specs=[pl.BlockSpec((S, DC), lambda b: (b, 3)), pl.BlockSpec((S, DC), lambda b: (b, 4)),
                  pl.BlockSpec((CT, DC), lambda b: (0, 0)), vec, vec, vec],
        out_specs=[pl.BlockSpec((S, DC), lambda b: (b, 0))] * 2,
        out_shape=[jax.ShapeDtypeStruct((T, DC), BF16), jax.ShapeDtypeStruct((T, DC), F32)],
        scratch_shapes=[pltpu.VMEM((CPAD + S, DC), F32)],
        compiler_params=_cp(("parallel",)), name="conv_fwd")(u, u, cw, cb, lg, lb)


def _conv_bwd(u, y, dconv, cw, lg, lb, B, S):
    T = B * S

    def body(ca_ref, cg_ref, y_ref, dc_ref, w_ref, lg_ref, lb_ref,
             dca_ref, dcg_ref, dw_ref, ds_ref, glu, dyp, dwacc):
        _fill_glu(ca_ref, cg_ref, glu, S)
        dyp[pl.ds(S, CPAD), :] = jnp.zeros((CPAD, DC), F32)
        lgv, lbv = lg_ref[...], lb_ref[...]

        def sum8(v):
            return functools.reduce(jnp.add, [v[r:r + 8] for r in range(0, v.shape[0], 8)])

        P1 = 4 * CT

        def p1(i, carry):
            sb, sg, sl = carry
            t0 = pl.multiple_of(i * P1, P1)
            xhat, rstd, z = _ln_fwd(y_ref[pl.ds(t0, P1), :], lgv, lbv)
            sz = _sigmoid(z)
            dz = dc_ref[pl.ds(t0, P1), :].astype(F32) * (sz * (1.0 + z * (1.0 - sz)))
            dxhat = dz * lgv
            dy = rstd * (dxhat - jnp.mean(dxhat, axis=-1, keepdims=True)
                         - xhat * jnp.mean(dxhat * xhat, axis=-1, keepdims=True))
            dyp[pl.ds(t0, P1), :] = dy
            return sb + sum8(dy), sg + sum8(dz * xhat), sl + sum8(dz)

        z8 = jnp.zeros((8, DC), F32)
        sb, sg, sl = lax.fori_loop(0, S // P1, p1, (z8, z8, z8))
        rs = lambda v: jnp.sum(v, axis=0, keepdims=True)
        ds_ref[...] = jnp.concatenate([rs(sb), rs(sg), rs(sl), jnp.zeros((5, DC), F32)], axis=0)[None]

        def p2(i, c):
            t0 = pl.multiple_of(i * CT, CT)
            win = dyp[pl.ds(t0, 2 * CT), :]
            acc = jnp.zeros((CT, DC), F32)
            taps = _shifted(win, [30 - k for k in range(CK)])
            for k in range(CK):
                acc = acc + taps[30 - k] * w_ref[k:k + 1, :]
            a = ca_ref[pl.ds(t0, CT), :].astype(F32)
            sgt = _sigmoid(cg_ref[pl.ds(t0, CT), :].astype(F32))
            dca_ref[pl.ds(t0, CT), :] = (acc * sgt).astype(BF16)
            dcg_ref[pl.ds(t0, CT), :] = (acc * a * sgt * (1.0 - sgt)).astype(BF16)
            return c

        lax.fori_loop(0, S // CT, p2, 0)

        dwacc[...] = jnp.zeros_like(dwacc)

        def p3(i, c):
            t0 = pl.multiple_of(i * CT, CT)
            win = glu[pl.ds(t0, 2 * CT), :]
            dy = dyp[pl.ds(t0, CT), :]
            for k in range(CK):
                dwacc[k] += sum8(dy * win[k + 2:k + 2 + CT, :])
            return c

        lax.fori_loop(0, S // CT, p3, 0)
        dw_ref[...] = jnp.sum(dwacc[...], axis=1)[None]

    vec = pl.BlockSpec((1, DC), lambda b: (0, 0))
    seq = pl.BlockSpec((S, DC), lambda b: (b, 0))
    return pl.pallas_call(
        body, grid=(B,),
        in_specs=[pl.BlockSpec((S, DC), lambda b: (b, 3)), pl.BlockSpec((S, DC), lambda b: (b, 4)),
                  seq, seq, pl.BlockSpec((CT, DC), lambda b: (0, 0)), vec, vec],
        out_specs=[seq, seq, pl.BlockSpec((1, CT, DC), lambda b: (b, 0, 0)),
                   pl.BlockSpec((1, 8, DC), lambda b: (b, 0, 0))],
        out_shape=[jax.ShapeDtypeStruct((T, DC), BF16)] * 2
                  + [jax.ShapeDtypeStruct((B, CT, DC), F32), jax.ShapeDtypeStruct((B, 8, DC), F32)],
        scratch_shapes=[pltpu.VMEM((CPAD + S, DC), F32), pltpu.VMEM((S + CPAD, DC), F32),
                        pltpu.VMEM((CT, 8, DC), F32)],
        compiler_params=_cp(("parallel",)), name="conv_bwd")(u, u, y, dconv, cw, lg, lb)


def _local_step(x, target, norms, W, B, S, comm=None):
    qg2 = jnp.concatenate([norms["q_norm"], norms["q_norm"]], axis=1)
    kg2 = jnp.concatenate([norms["k_norm"], norms["k_norm"]], axis=1)
    cw = jnp.concatenate([W["conv_w"], jnp.zeros((1, DC), F32)], axis=0)

    W = dict(W)
    (n1, g1, u1, act1), got = _ffn_gate_up(x, norms["ffn1_norm"], W["wg1"], W["wu1"], "ffn1_gate_up",
                                           carry=comm.gathers["down_in"] if comm else None)
    if comm:
        W.update(comm.gathered("down_in", got))
    h1, u, n2 = _ffn_down_mix_in(x, act1, W["wd1"], norms["mix_norm"], W["win"], "ffn1_down_mix_in")
    (attn, lse), got = _attn_fwd(u, qg2, kg2, B, S, carry=comm.gathers["ffn2"] if comm else None)
    if comm:
        W = dict(W, **comm.gathered("ffn2", got))
    conv, y = _conv_fwd(u, cw, norms["conv_b"], norms["conv_ln_g"], norms["conv_ln_b"], B, S)
    h2, n3, g2, u2, dout, dyb, sq, act2 = _mix_out_ffn_loss(h1, attn, conv, W["wout"], norms["ffn2_norm"],
                                                            W["wg2"], W["wu2"], W["wd2"], target, "ffn2_fwd")
    loss = (0.5 / D) * jnp.sum(sq)

    (dg2, du2, dh2, dgn_ffn2), _ = _ffn_bwd_act(dyb, g2, u2, h2, dout, norms["ffn2_norm"],
                                               W["wg2"], W["wu2"], W["wd2"], "ffn2_bwd_act")
    dwd2, _ = _ffn_bwd_w(act2, dyb, "ffn2_bwd_wd")
    dwg2, _ = _ffn_bwd_w(dg2, n3, "ffn2_bwd_wg")
    dwu2, _ = _ffn_bwd_w(du2, n3, "ffn2_bwd_wu")
    dattn, dconv, dwout = _mix_out_bwd(dh2, attn, conv, W["wout"])
    carry = comm.reduce_start({"wg2": dwg2, "wu2": dwu2, "wd2": dwd2, "wout": dwout}) if comm else None
    (dq, dk, dv, dgn_qk), got = _attn_bwd(u, attn, dattn, lse, qg2, kg2, B, S, carry=carry)
    if comm:
        comm.reduce_done(carry, got)
    dca, dcg, dcw, dcs = _conv_bwd(u, y, dconv, cw, norms["conv_ln_g"], norms["conv_ln_b"], B, S)
    dwin, dh1, dyb1, dgn_mix = _mix_in_bwd((dq, dk, dv, dca, dcg), W["win"], n2, h1, dh2, norms["mix_norm"])
    carry = comm.reduce_start({"win": dwin}) if comm else None
    dwd1, got = _ffn_bwd_w(act1, dyb1, "ffn1_bwd_wd", carry=carry)
    if comm:
        comm.reduce_done(carry, got)
    carry = comm.update_start(("wg2", "wu2", "wd2", "wout", "win")) if comm else None
    (dg1, du1, gx, dgn_ffn1), got = _ffn_bwd_act(dyb1, g1, u1, x, dh1, norms["ffn1_norm"],
                                                W["wg1"], W["wu1"], W["wd1"], "ffn1_bwd_act", carry=carry)
    if comm:
        comm.update_done(carry, got)
        carry = comm.reduce_start({"wd1": dwd1})
    dwg1, got = _ffn_bwd_w(dg1, n1, "ffn1_bwd_wg", carry=carry)
    if comm:
        comm.reduce_done(carry, got)
        carry = comm.reduce_start({"wg1": dwg1})
    dwu1, got = _ffn_bwd_w(du1, n1, "ffn1_bwd_wu", carry=carry)
    if comm:
        comm.reduce_done(carry, got)
        comm.last = comm.reduce_start({"wu1": dwu1})

    qk = jnp.sum(dgn_qk, axis=0)
    cs = jnp.sum(dcs, axis=0)
    small = {
        "ffn1_norm": jnp.sum(dgn_ffn1, axis=0),
        "mix_norm": jnp.sum(dgn_mix, axis=0),
        "q_norm": qk[0:1, 0:HD] + qk[0:1, HD:2 * HD],
        "k_norm": qk[1:2, 0:HD] + qk[1:2, HD:2 * HD],
        "conv_w": jnp.sum(dcw, axis=0)[0:CK],
        "conv_b": cs[0:1],
        "conv_ln_g": cs[1:2],
        "conv_ln_b": cs[2:3],
        "ffn2_norm": jnp.sum(dgn_ffn2, axis=0),
    }
    big = {"wg1": dwg1, "wu1": dwu1, "wd1": dwd1, "win": dwin, "wout": dwout,
           "wg2": dwg2, "wu2": dwu2, "wd2": dwd2}
    return loss, gx, big, small


HBM = pl.BlockSpec(memory_space=pltpu.HBM)
VMEM = pl.BlockSpec(memory_space=pltpu.VMEM)


def _place():
    return lax.axis_index("x"), lax.axis_index("y"), lax.axis_index("c")


class _GatherCarry:
    def __init__(self, shards, mid_at=0.5):
        nt = len(shards)
        self.mid_at = mid_at
        self.shards = shards
        self.in_arrays = [s for s, _ in shards]
        self.in_specs = [VMEM] * nt
        self.out_shape = [jax.ShapeDtypeStruct((NDEV * s.shape[0], s.shape[1]), dt) for s, dt in shards]
        self.out_specs = [HBM] * nt
        self.scratch = ([pltpu.VMEM(s.shape, dt) for s, dt in shards]
                        + [pltpu.SemaphoreType.DMA((nt, 7)), pltpu.SemaphoreType.DMA((nt, 7)),
                           pltpu.SemaphoreType.DMA((nt,))])

    def _copies(self, outs, scr):
        nt = len(self.shards)
        stages = scr[:nt]
        send_sems, recv_sems, local_sems = scr[nt:]
        x, y, c = _place()
        me, sibling = (x, y, c), (x, y, 1 - c)
        xn, yn, diag = (1 - x, y, c), (x, 1 - y, c), (1 - x, 1 - y, c)
        via = (x ^ c, y ^ (1 - c), c)
        onto = (x ^ (1 - c), y ^ c, c)

        def rows(t, px, py, pc):
            r = self.shards[t][0].shape[0]
            return outs[t].at[pl.ds((4 * px + 2 * py + pc) * r, r), :]

        def copy(t, k, block, to, src=None):
            return pltpu.make_async_remote_copy(
                src_ref=rows(t, *block) if src is None else src, dst_ref=rows(t, *block),
                send_sem=send_sems.at[t, k], recv_sem=recv_sems.at[t, k],
                device_id=to, device_id_type=MESH)

        sib = lambda b: (b[0], b[1], 1 - c)
        return dict(
            local=[pltpu.make_async_copy(stages[t], rows(t, *me), local_sems.at[t]) for t in range(nt)],
            own=[[copy(t, 0, me, sibling, src=stages[t]), copy(t, 1, me, xn, src=stages[t]),
                  copy(t, 2, me, yn, src=stages[t])] for t in range(nt)],
            relay=[copy(t, 3, via, onto) for t in range(nt)],
            down=[[copy(t, 4, xn, sibling), copy(t, 5, yn, sibling)] for t in range(nt)],
            down_diag=[copy(t, 6, diag, sibling) for t in range(nt)],
            got_xy=[[copy(t, 1, xn, me), copy(t, 2, yn, me)] for t in range(nt)],
            got_diag=[copy(t, 3, diag, me) for t in range(nt)],
            got_sib=[[copy(t, 0, sibling, me), copy(t, 4, sib(xn), me), copy(t, 5, sib(yn), me),
                      copy(t, 6, sib(diag), me)] for t in range(nt)])

    def start(self, ins, outs, scr):
        cps = self._copies(outs, scr)
        for t, (_, dt) in enumerate(self.shards):
            scr[t][...] = ins[t][...].astype(dt)
            for cp in [cps["local"][t]] + cps["own"][t]:
                cp.start()

    def stages(self):
        sizes = [s.size * jnp.dtype(dt).itemsize for s, dt in self.shards]
        done = [sum(sizes[:t + 1]) / sum(sizes) for t in range(len(sizes))]
        return [(self.mid_at * f, functools.partial(self.mid, t)) for t, f in enumerate(done)]

    def mid(self, t, ins, outs, scr):
        cps = self._copies(outs, scr)
        for cp in cps["got_xy"][t]:
            cp.wait_recv()
        for cp in [cps["relay"][t]] + cps["down"][t]:
            cp.start()

    def finish(self, ins, outs, scr):
        cps = self._copies(outs, scr)
        for t in range(len(self.shards)):
            cps["got_diag"][t].wait_recv()
            cps["down_diag"][t].start()
        for t in range(len(self.shards)):
            for cp in cps["got_sib"][t]:
                cp.wait_recv()
            for cp in cps["own"][t] + [cps["relay"][t]] + cps["down"][t] + [cps["down_diag"][t]]:
                cp.wait_send()
            cps["local"][t].wait()


def _run_carry(carry, name):
    def body(*refs):
        n_in, n_out = len(carry.in_arrays), len(carry.out_shape)
        ins, outs, scr = refs[:n_in], refs[n_in:n_in + n_out], refs[n_in + n_out:]
        carry.start(ins, outs, scr)
        for _, stage in carry.stages():
            stage(ins, outs, scr)
        carry.finish(ins, outs, scr)

    return pl.pallas_call(
        body, in_specs=carry.in_specs, out_specs=carry.out_specs, out_shape=carry.out_shape,
        scratch_shapes=carry.scratch, compiler_params=pltpu.CompilerParams(vmem_limit_bytes=VMEM_LIMIT),
        name=name)(*carry.in_arrays)


class _ExchangeCarry:
    def __init__(self, names, grads, mid_at=0.5):
        nt = len(grads)
        self.mid_at = mid_at
        self.names = names
        self.in_arrays = [g.reshape(4, 2, g.shape[0] // NDEV, g.shape[1]) for g in grads]
        self.in_specs = [HBM] * nt
        blocks = [g.shape[2:] for g in self.in_arrays]
        self.out_shape = [jax.ShapeDtypeStruct((3,) + b, BF16) for b in blocks]
        self.out_specs = [HBM] * nt
        self.scratch = ([pltpu.VMEM((4,) + b, BF16) for b in blocks] * 2 + [pltpu.VMEM(b, BF16) for b in blocks]
                        + [pltpu.SemaphoreType.DMA((nt, 3)), pltpu.SemaphoreType.DMA((nt, 3))]
                        + [pltpu.SemaphoreType.DMA((nt,))] * 4)

    def _copies(self, ins, outs, scr):
        nt = len(ins)
        theirs, own, relayed = scr[:nt], scr[nt:2 * nt], scr[2 * nt:3 * nt]
        send_sems, recv_sems, keep_sems, load_sems, swap_send, swap_recv = scr[3 * nt:]
        x, y, c = _place()
        q = lambda cx, cy: 2 * cx + cy
        near, far = (x ^ c, y ^ (1 - c)), (x ^ (1 - c), y ^ c)

        def remote(t, k, src, dst, chip):
            return pltpu.make_async_remote_copy(
                src_ref=src, dst_ref=dst, send_sem=send_sems.at[t, k], recv_sem=recv_sems.at[t, k],
                device_id=(*chip, c), device_id_type=MESH)

        return dict(
            swap=[pltpu.make_async_remote_copy(
                src_ref=ins[t].at[:, 1 - c], dst_ref=theirs[t], send_sem=swap_send.at[t], recv_sem=swap_recv.at[t],
                device_id=(x, y, 1 - c), device_id_type=MESH) for t in range(nt)],
            load=[pltpu.make_async_copy(ins[t].at[:, c], own[t], load_sems.at[t]) for t in range(nt)],
            keep=[pltpu.make_async_copy(own[t].at[q(x, y)], outs[t].at[0], keep_sems.at[t]) for t in range(nt)],
            direct=[remote(t, 0, own[t].at[q(*near)], outs[t].at[1], near) for t in range(nt)],
            relay=[remote(t, 1, own[t].at[q(1 - x, 1 - y)], relayed[t], near) for t in range(nt)],
            merged=[remote(t, 2, own[t].at[q(*far)], outs[t].at[2], far) for t in range(nt)],
            theirs=theirs, own=own, relayed=relayed, far=q(*far))

    EARLY_AT = 0.1

    def stages(self):
        return [(self.EARLY_AT, self.early), (self.mid_at, self.mid)]

    def start(self, ins, outs, scr):
        cps = self._copies(ins, outs, scr)
        for t in range(len(ins)):
            cps["swap"][t].start()
            cps["load"][t].start()

    def early(self, ins, outs, scr):
        cps = self._copies(ins, outs, scr)
        for t in range(len(ins)):
            cps["load"][t].wait()
            cps["swap"][t].wait_recv()
            own, theirs = cps["own"][t], cps["theirs"][t]
            for j in range(4):
                own[j] = (own[j].astype(F32) + theirs[j].astype(F32)).astype(BF16)
            for kind in ("relay", "direct", "keep"):
                cps[kind][t].start()

    def mid(self, ins, outs, scr):
        cps = self._copies(ins, outs, scr)
        for t in range(len(ins)):
            cps["relay"][t].wait_recv()
            own, far = cps["own"][t], cps["far"]
            own[far] = (own[far].astype(F32) + cps["relayed"][t][...].astype(F32)).astype(BF16)
            cps["merged"][t].start()

    def finish(self, ins, outs, scr):
        cps = self._copies(ins, outs, scr)
        for t in range(len(ins)):
            cps["swap"][t].wait_send()
            cps["direct"][t].wait()
            cps["relay"][t].wait_send()
            cps["merged"][t].wait()
            cps["keep"][t].wait()


class _Comm:
    def __init__(self, groups, opt):
        self.names = {tag: list(g) for tag, (g, _) in groups.items()}
        self.gathers = {tag: _GatherCarry(list(g.values()), mid_at) for tag, (g, mid_at) in groups.items()}
        self.reduced = {}
        self.last = None
        self.opt = opt
        self.updated = {}

    def gathered(self, tag, outs):
        return dict(zip(self.names[tag], outs))

    def reduce_start(self, grads):
        names = list(grads)
        return _ExchangeCarry(names, [grads[n] for n in names])

    def reduce_done(self, carry, outs):
        self.reduced.update(zip(carry.names, outs))

    def update_start(self, names):
        w, m, v = zip(*[self.opt[n] for n in names])
        return _AdamWCarry(names, [self.reduced[n] for n in names], w, m, v)

    def update_done(self, carry, outs):
        self.updated.update({n: outs[4 * k:4 * k + 4] for k, n in enumerate(carry.names)})


def _adamw_math(w, g, m, v):
    m = B1 * m + (1.0 - B1) * g
    v = B2 * v + (1.0 - B2) * (g * g)
    m_hat = m / (1.0 - B1 ** STEP)
    v_hat = v / (1.0 - B2 ** STEP)
    delta = -LR * (m_hat / (jnp.sqrt(v_hat) + AEPS) + WD * w)
    return delta, m, v


def _adamw_big(recvs, ws, ms, vs, name):
    nw = len(ws)

    def body(*refs):
        ins, outs = refs[:4 * nw], refs[4 * nw:]
        for k in range(nw):
            @pl.when(pl.program_id(0) // 2 == k)
            def _(k=k):
                r_ref, w_ref, m_ref, v_ref = ins[4 * k:4 * k + 4]
                g_ref, d_ref, mo_ref, vo_ref = outs[4 * k:4 * k + 4]
                g = r_ref[0].astype(F32)
                for q in range(1, 3):
                    g = g + r_ref[q].astype(F32)
                d, mn, vn = _adamw_math(w_ref[...], g, m_ref[...], v_ref[...])
                g_ref[...] = g
                d_ref[...] = d
                mo_ref[...] = mn
                vo_ref[...] = vn

    in_specs, out_specs, out_shape, args = [], [], [], []
    for k, (r, w, m, v) in enumerate(zip(recvs, ws, ms, vs)):
        rows, n = w.shape
        tr = rows // 2
        tile = lambda s, k=k: jnp.clip(s - 2 * k, 0, 1)
        row = pl.BlockSpec((tr, n), lambda s, tile=tile: (tile(s), 0))
        in_specs += [pl.BlockSpec((3, tr, n), lambda s, tile=tile: (0, tile(s), 0)), row, row, row]
        out_specs += [row] * 4
        out_shape += [jax.ShapeDtypeStruct(w.shape, F32)] * 4
        args += [r, w, m, v]
    res = pl.pallas_call(
        body, grid=(2 * nw,), in_specs=in_specs, out_specs=out_specs, out_shape=out_shape,
        compiler_params=_cp(("arbitrary",)), name=name)(*args)
    return [res[4 * k:4 * k + 4] for k in range(nw)]


class _AdamWCarry:
    def __init__(self, names, recvs, ws, ms, vs):
        self.names = names
        nw = len(ws)
        self.halves = [(k, j, w.shape[0] // 2) for k, w in enumerate(ws) for j in range(2)]
        self.in_arrays = [a for quad in zip(recvs, ws, ms, vs) for a in quad]
        self.in_specs = [HBM] * (4 * nw)
        self.out_shape = [jax.ShapeDtypeStruct(w.shape, F32) for w in ws for _ in range(4)]
        self.out_specs = [HBM] * (4 * nw)
        tr, n = max(h[2] for h in self.halves), ws[0].shape[1]
        self.scratch = [pltpu.VMEM((2, 3, tr, n), BF16), pltpu.VMEM((2, 3, tr, n), F32),
                        pltpu.VMEM((2, 4, tr, n), F32), pltpu.SemaphoreType.DMA((2, 4)),
                        pltpu.SemaphoreType.DMA((2, 4))]

    def _copies(self, c, ins, outs, scr):
        rbuf, fbuf, obuf, in_sems, out_sems = scr
        k, j, tr = self.halves[c]
        s, rows = c % 2, pl.ds(j * tr, tr)
        loads = [pltpu.make_async_copy(ins[4 * k].at[:, rows, :], rbuf.at[s, :, pl.ds(0, tr), :], in_sems.at[s, 0])]
        loads += [pltpu.make_async_copy(ins[4 * k + i].at[rows, :], fbuf.at[s, i - 1, pl.ds(0, tr), :],
                                        in_sems.at[s, i]) for i in range(1, 4)]
        stores = [pltpu.make_async_copy(obuf.at[s, q, pl.ds(0, tr), :], outs[4 * k + q].at[rows, :],
                                        out_sems.at[s, q]) for q in range(4)]
        return loads, stores

    def start(self, ins, outs, scr):
        for c in range(2):
            for cp in self._copies(c, ins, outs, scr)[0]:
                cp.start()

    def stages(self):
        n = len(self.halves)
        return [((c + 1) / (n + 1), functools.partial(self.half, c)) for c in range(n)]

    def half(self, c, ins, outs, scr):
        rbuf, fbuf, obuf = scr[:3]
        _, _, tr = self.halves[c]
        s = c % 2
        loads, stores = self._copies(c, ins, outs, scr)
        for cp in loads:
            cp.wait()
        if c >= 2:
            for cp in self._copies(c - 2, ins, outs, scr)[1]:
                cp.wait()
        g = rbuf[s, 0, 0:tr].astype(F32)
        for q in range(1, 3):
            g = g + rbuf[s, q, 0:tr].astype(F32)
        d, mn, vn = _adamw_math(fbuf[s, 0, 0:tr], g, fbuf[s, 1, 0:tr], fbuf[s, 2, 0:tr])
        for q, val in enumerate((g, d, mn, vn)):
            obuf[s, q, 0:tr] = val
        for cp in stores:
            cp.start()
        if c + 2 < len(self.halves):
            for cp in self._copies(c + 2, ins, outs, scr)[0]:
                cp.start()

    def finish(self, ins, outs, scr):
        n = len(self.halves)
        for c in range(max(n - 2, 0), n):
            for cp in self._copies(c, ins, outs, scr)[1]:
                cp.wait()


SMALL_NAMES = ("ffn1_norm", "mix_norm", "ffn2_norm", "conv_b", "conv_ln_g", "conv_ln_b", "q_norm", "k_norm")
SROWS = 16
LOSS_ROW = len(SMALL_NAMES)
CWT = (32, 128)


def _small_sums(gs, loss_row, gcw, carry=None):
    ns = len(SMALL_NAMES)
    widths = [g.shape[1] for g in gs]

    def body(*refs):
        it = iter(refs)
        take = lambda n: [next(it) for _ in range(n)]
        g_refs, (loss_ref, gcw_ref) = take(ns), take(2)
        cins = take(len(carry.in_arrays)) if carry else []
        tot_ref, ctot_ref = take(2)
        couts = take(len(carry.out_shape)) if carry else []
        send, slots, cslots, send_sems, recv_sems, csend_sems, crecv_sems = take(7)
        cscr = list(it)
        x, y, c = _place()
        me = 4 * x + 2 * y + c
        send[...] = jnp.zeros_like(send)
        for k in range(ns):
            send[k:k + 1, 0:widths[k]] = g_refs[k][...]
        send[LOSS_ROW:LOSS_ROW + 1, 0:128] = loss_ref[...]
        slots[me] = send[...]
        cslots[me] = gcw_ref[me]
        cps = []
        for k in range(1, NDEV):
            peer = (x ^ ((k >> 2) & 1), y ^ ((k >> 1) & 1), c ^ (k & 1))
            cps.append(pltpu.make_async_remote_copy(
                src_ref=send, dst_ref=slots.at[me], send_sem=send_sems.at[k - 1], recv_sem=recv_sems.at[k - 1],
                device_id=peer, device_id_type=MESH))
            cps.append(pltpu.make_async_remote_copy(
                src_ref=gcw_ref.at[4 * peer[0] + 2 * peer[1] + peer[2]], dst_ref=cslots.at[me],
                send_sem=csend_sems.at[k - 1], recv_sem=crecv_sems.at[k - 1], device_id=peer, device_id_type=MESH))
        for cp in cps:
            cp.start()
        stages = [stage for _, stage in carry.stages()] if carry else []
        if carry:
            carry.start(cins, couts, cscr)
        for stage in stages[:1]:
            stage(cins, couts, cscr)
        for cp in cps:
            cp.wait()
        tot = slots[0]
        ctot = cslots[0]
        for j in range(1, NDEV):
            tot = tot + slots[j]
            ctot = ctot + cslots[j]
        tot_ref[...] = tot
        ctot_ref[...] = ctot
        for stage in stages[1:]:
            stage(cins, couts, cscr)
        if carry:
            carry.finish(cins, couts, cscr)

    args = [*gs, loss_row, gcw]
    out_shape = [jax.ShapeDtypeStruct((SROWS, D), F32), jax.ShapeDtypeStruct(CWT, F32)]
    res = pl.pallas_call(
        body, in_specs=[VMEM] * len(args) + (carry.in_specs if carry else []),
        out_specs=[VMEM] * 2 + (carry.out_specs if carry else []),
        out_shape=out_shape + (carry.out_shape if carry else []),
        scratch_shapes=[pltpu.VMEM((SROWS, D), F32), pltpu.VMEM((NDEV, SROWS, D), F32),
                        pltpu.VMEM((NDEV,) + CWT, F32)]
                       + [pltpu.SemaphoreType.DMA((NDEV - 1,))] * 4 + (carry.scratch if carry else []),
        name="small_sums")(*args, *(carry.in_arrays if carry else []))
    return res[0], res[1], res[2:]


def _adamw_small(tot, ctot, ws, ms, vs, wcw, mcw, vcw):
    ns = len(SMALL_NAMES)
    widths = [w.shape[1] for w in ws]

    def body(*refs):
        it = iter(refs)
        take = lambda n: [next(it) for _ in range(n)]
        (tot_ref, ctot_ref), w_refs, m_refs, v_refs = take(2), take(ns), take(ns), take(ns)
        wcw_ref, mcw_ref, vcw_ref = take(3)
        outs = [take(4) for _ in range(ns)]
        cw_outs, (loss_out,) = take(4), take(1)

        def step(g, w_ref, m_ref, v_ref, o):
            d, mn, vn = _adamw_math(w_ref[...], g, m_ref[...], v_ref[...])
            o[0][...], o[1][...], o[2][...], o[3][...] = g, d, mn, vn

        for k in range(ns):
            step(tot_ref[k:k + 1, 0:widths[k]], w_refs[k], m_refs[k], v_refs[k], outs[k])
        step(ctot_ref[0:CK, 0:HD][None], wcw_ref, mcw_ref, vcw_ref, cw_outs)
        loss_out[...] = tot_ref[LOSS_ROW:LOSS_ROW + 1, 0:128]

    args = [tot, ctot, *ws, *ms, *vs, wcw, mcw, vcw]
    out_shape = ([jax.ShapeDtypeStruct((1, n), F32) for n in widths for _ in range(4)]
                 + [jax.ShapeDtypeStruct((1, CK, HD), F32)] * 4 + [jax.ShapeDtypeStruct((1, 128), F32)])
    res = pl.pallas_call(
        body, in_specs=[VMEM] * len(args), out_specs=[VMEM] * len(out_shape), out_shape=out_shape,
        name="adamw_small")(*args)
    per = [res[4 * k:4 * k + 4] for k in range(ns)]
    return per, res[4 * ns:4 * ns + 4], res[-1]


def kernel(x, ffn1_norm, ffn1_w_gate, ffn1_w_up, ffn1_w_down, mix_norm, w_in, q_norm, k_norm, conv_w, conv_b, conv_ln_g, conv_ln_b, w_out, ffn2_norm, ffn2_w_gate, ffn2_w_up, ffn2_w_down, loss_target, m_ffn1_norm, m_ffn1_w_gate, m_ffn1_w_up, m_ffn1_w_down, m_mix_norm, m_w_in, m_q_norm, m_k_norm, m_conv_w, m_conv_b, m_conv_ln_g, m_conv_ln_b, m_w_out, m_ffn2_norm, m_ffn2_w_gate, m_ffn2_w_up, m_ffn2_w_down, v_ffn1_norm, v_ffn1_w_gate, v_ffn1_w_up, v_ffn1_w_down, v_mix_norm, v_w_in, v_q_norm, v_k_norm, v_conv_w, v_conv_b, v_conv_ln_g, v_conv_ln_b, v_w_out, v_ffn2_norm, v_ffn2_w_gate, v_ffn2_w_up, v_ffn2_w_down):
    P = dict(ffn1_norm=ffn1_norm, ffn1_w_gate=ffn1_w_gate, ffn1_w_up=ffn1_w_up, ffn1_w_down=ffn1_w_down,
             mix_norm=mix_norm, w_in=w_in, q_norm=q_norm, k_norm=k_norm, conv_w=conv_w, conv_b=conv_b,
             conv_ln_g=conv_ln_g, conv_ln_b=conv_ln_b, w_out=w_out, ffn2_norm=ffn2_norm,
             ffn2_w_gate=ffn2_w_gate, ffn2_w_up=ffn2_w_up, ffn2_w_down=ffn2_w_down)
    M = dict(ffn1_norm=m_ffn1_norm, ffn1_w_gate=m_ffn1_w_gate, ffn1_w_up=m_ffn1_w_up, ffn1_w_down=m_ffn1_w_down,
             mix_norm=m_mix_norm, w_in=m_w_in, q_norm=m_q_norm, k_norm=m_k_norm, conv_w=m_conv_w, conv_b=m_conv_b,
             conv_ln_g=m_conv_ln_g, conv_ln_b=m_conv_ln_b, w_out=m_w_out, ffn2_norm=m_ffn2_norm,
             ffn2_w_gate=m_ffn2_w_gate, ffn2_w_up=m_ffn2_w_up, ffn2_w_down=m_ffn2_w_down)
    V = dict(ffn1_norm=v_ffn1_norm, ffn1_w_gate=v_ffn1_w_gate, ffn1_w_up=v_ffn1_w_up, ffn1_w_down=v_ffn1_w_down,
             mix_norm=v_mix_norm, w_in=v_w_in, q_norm=v_q_norm, k_norm=v_k_norm, conv_w=v_conv_w, conv_b=v_conv_b,
             conv_ln_g=v_conv_ln_g, conv_ln_b=v_conv_ln_b, w_out=v_w_out, ffn2_norm=v_ffn2_norm,
             ffn2_w_gate=v_ffn2_w_gate, ffn2_w_up=v_ffn2_w_up, ffn2_w_down=v_ffn2_w_down)
    order = ["ffn1_norm", "ffn1_w_gate", "ffn1_w_up", "ffn1_w_down", "mix_norm", "w_in", "q_norm", "k_norm",
             "conv_w", "conv_b", "conv_ln_g", "conv_ln_b", "w_out", "ffn2_norm", "ffn2_w_gate", "ffn2_w_up",
             "ffn2_w_down"]
    B, S, _ = x.shape
    T = B * S

    bigs = [("wg1", "ffn1_w_gate", True), ("wu1", "ffn1_w_up", True), ("wd1", "ffn1_w_down", False),
            ("win", "w_in", True), ("wout", "w_out", False),
            ("wg2", "ffn2_w_gate", True), ("wu2", "ffn2_w_up", True), ("wd2", "ffn2_w_down", False)]
    hm = lambda a, tr: jnp.transpose(a[0]) if tr else a[0]
    cw_pad = jnp.zeros((32, 128), F32).at[0:CK, 0:HD].set(conv_w[0])
    shard = {ln: (hm(P[pn], tr), BF16) for ln, pn, tr in bigs}
    gathered = _run_carry(_GatherCarry([shard["wg1"], shard["wu1"], (cw_pad, F32)]), "gather_first")
    W = {"wg1": gathered[0], "wu1": gathered[1]}
    cwg = gathered[2].reshape(NDEV, 32, 128)[:, 0:CK, 0:HD]
    W["conv_w"] = jnp.transpose(cwg, (1, 0, 2)).reshape(CK, DC)
    norms = {n: P[n] for n in SMALL_NAMES}
    comm = _Comm({"down_in": ({n: shard[n] for n in ("wd1", "win")}, 0.5),
                  "ffn2": ({n: shard[n] for n in ("wg2", "wu2", "wd2", "wout")}, 0.5)},
                 opt={ln: (hm(P[pn], tr), hm(M[pn], tr), hm(V[pn], tr)) for ln, pn, tr in bigs})

    loss_part, gx, _, small = _local_step(x.reshape(T, D), loss_target.reshape(T, D), norms, W, B, S, comm)

    G, Dl, Mn, Vn = {}, {}, {}, {}
    dcw = small["conv_w"].reshape(CK, NDEV, HD).transpose(1, 0, 2)
    loss_row = jnp.zeros((1, 128), F32).at[0, 0].set(loss_part)
    gcw = jnp.pad(dcw, ((0, 0), (0, CWT[0] - CK), (0, CWT[1] - HD)))
    tot, ctot, got = _small_sums([small[n] for n in SMALL_NAMES], loss_row, gcw, carry=comm.last)
    comm.reduce_done(comm.last, got)
    per, cw_outs, loss_out = _adamw_small(
        tot, ctot, [P[n] for n in SMALL_NAMES], [M[n] for n in SMALL_NAMES], [V[n] for n in SMALL_NAMES],
        P["conv_w"], M["conv_w"], V["conv_w"])
    loss = loss_out[0, 0]
    for n, outs in zip(SMALL_NAMES, per):
        G[n], Dl[n], Mn[n], Vn[n] = outs
    G["conv_w"], Dl["conv_w"], Mn["conv_w"], Vn["conv_w"] = cw_outs

    group = ("wd1", "wg1", "wu1")
    w, m, v = zip(*[comm.opt[ln] for ln in group])
    comm.updated.update(zip(group, _adamw_big([comm.reduced[ln] for ln in group], w, m, v, "adamw_ffn1")))
    for ln, pn, tr in bigs:
        G[pn], Dl[pn], Mn[pn], Vn[pn] = [(jnp.transpose(o) if tr else o)[None] for o in comm.updated[ln]]

    return (loss, gx.reshape(B, S, D), *[G[n] for n in order], *[Dl[n] for n in order],
            *[Mn[n] for n in order], *[Vn[n] for n in order])
```

```python
import functools

import jax
import jax.numpy as jnp
from jax import lax
from jax.experimental import pallas as pl
from jax.experimental.pallas import tpu as pltpu

F32 = jnp.float32
BF16 = jnp.bfloat16

D = 1024
FF = 2816
HD = 64
DA = 512
DC = 512
DIN = 2560
CK = 31
BLK = 128
DILS = (1, 4, 16)
EPS = 1e-6
NDEV = 8
MESH = pl.DeviceIdType.MESH

LR, B1, B2, AEPS, WD, STEP = 0.001, 0.9, 0.999, 1e-08, 0.01, 10

NT = (((1,), (1,)), ((), ()))
TN = (((0,), (0,)), ((), ()))

VMEM_LIMIT = 60 * 1024 * 1024


def _cp(sem=None):
    return pltpu.CompilerParams(dimension_semantics=sem, vmem_limit_bytes=VMEM_LIMIT)


def _sigmoid(x):
    return 0.5 * (jnp.tanh(0.5 * x) + 1.0)


def _pallas(body, args, *, grid, in_specs, out_specs, out_shape, scratch_shapes, sem, name, carry=None):
    if carry is None:
        outs = pl.pallas_call(body, grid=grid, in_specs=in_specs, out_specs=out_specs, out_shape=out_shape,
                              scratch_shapes=scratch_shapes, compiler_params=_cp(sem), name=name)(*args)
        return outs, None
    n_in, n_out, n_scr = len(in_specs), len(out_shape), len(scratch_shapes)
    c_in, c_out = len(carry.in_arrays), len(carry.out_shape)

    def wrapped(*refs):
        ins, refs = refs[:n_in], refs[n_in:]
        cins, refs = refs[:c_in], refs[c_in:]
        outs, refs = refs[:n_out], refs[n_out:]
        couts, refs = refs[:c_out], refs[c_out:]
        scr, cscr = refs[:n_scr], refs[n_scr:]
        ids = [pl.program_id(a) for a in range(len(grid))]
        step = ids[0]
        for i, n in zip(ids[1:], grid[1:]):
            step = step * n + i
        steps = functools.reduce(lambda a, b: a * b, grid)

        @pl.when(step == 0)
        def _():
            carry.start(cins, couts, cscr)

        body(*ins, *outs, *scr)

        for frac, stage in carry.stages():
            @pl.when(step == int(steps * frac))
            def _(stage=stage):
                stage(cins, couts, cscr)

        @pl.when(step == steps - 1)
        def _():
            carry.finish(cins, couts, cscr)

    outs = pl.pallas_call(
        wrapped, grid=grid, in_specs=list(in_specs) + carry.in_specs, out_specs=list(out_specs) + carry.out_specs,
        out_shape=list(out_shape) + carry.out_shape, scratch_shapes=list(scratch_shapes) + carry.scratch,
        compiler_params=_cp(("arbitrary",) * len(grid)), name=name)(*args, *carry.in_arrays)
    return outs[:n_out], outs[n_out:]


FC = 256


def _resident(shape):
    return pl.BlockSpec(shape, lambda *_: (0,) * len(shape), pipeline_mode=pl.Buffered(1))


def _mix_out_ffn_loss(h1, attn, conv, wout, gain, wg, wu, wd, target, name):
    T = h1.shape[0]
    tm = 512
    nt = T // tm

    def body(h1_ref, at_ref, cv_ref, wo_ref, gain_ref, wg_ref, wu_ref, wd_ref, t_ref,
             h2_ref, n_ref, g_ref, u_ref, dout_ref, dyb_ref, sq_ref, a_hbm, a_scr, a_sem):
        t = pl.program_id(0)
        a_out = lambda i: pltpu.make_async_copy(a_scr, a_hbm.at[pl.ds(pl.multiple_of(i * tm, tm), tm), :], a_sem)

        @pl.when(t > 0)
        def _():
            a_out(t - 1).wait()

        xv = (h1_ref[...]
              + jnp.dot(at_ref[...], wo_ref[0:DA, :], preferred_element_type=F32)
              + jnp.dot(cv_ref[...], wo_ref[DA:D, :], preferred_element_type=F32))
        h2_ref[...] = xv
        r = lax.rsqrt(jnp.mean(xv * xv, axis=-1, keepdims=True) + EPS)
        n_ref[...] = (xv * r * gain_ref[...]).astype(BF16)
        for c in range(FF // FC):
            cols = slice(c * FC, (c + 1) * FC)
            nb = n_ref[...]
            g = lax.dot_general(nb, wg_ref[cols, :], NT, preferred_element_type=F32)
            u = lax.dot_general(nb, wu_ref[cols, :], NT, preferred_element_type=F32)
            g_ref[:, cols] = g.astype(BF16)
            u_ref[:, cols] = u.astype(BF16)
            a_scr[:, cols] = (g * _sigmoid(g) * u).astype(BF16)
        a_out(t).start()
        e = h2_ref[...] + 0.5 * jnp.dot(a_scr[...], wd_ref[...], preferred_element_type=F32) - t_ref[...]
        dout = e * (1.0 / D)
        dout_ref[...] = dout
        dyb_ref[...] = (0.5 * dout).astype(BF16)
        sq_ref[...] = jnp.sum(e * e, axis=0, keepdims=True)[None]

        @pl.when(t == nt - 1)
        def _():
            a_out(t).wait()

    row = pl.BlockSpec((tm, D), lambda t: (t, 0))
    half = pl.BlockSpec((tm, DA), lambda t: (t, 0))
    wide = pl.BlockSpec((tm, FF), lambda t: (t, 0))
    outs, _ = _pallas(
        body, (h1, attn, conv, wout, gain, wg, wu, wd, target), grid=(nt,),
        in_specs=[row, half, half, _resident((D, D)), _resident((1, D)), _resident((FF, D)), _resident((FF, D)),
                  _resident((FF, D)), row],
        out_specs=[row, row, wide, wide, row, row, pl.BlockSpec((1, 1, D), lambda t: (t, 0, 0)), HBM],
        out_shape=[jax.ShapeDtypeStruct((T, D), F32), jax.ShapeDtypeStruct((T, D), BF16)]
                  + [jax.ShapeDtypeStruct((T, FF), BF16)] * 2
                  + [jax.ShapeDtypeStruct((T, D), F32), jax.ShapeDtypeStruct((T, D), BF16),
                     jax.ShapeDtypeStruct((nt, 1, D), F32), jax.ShapeDtypeStruct((T, FF), BF16)],
        scratch_shapes=[pltpu.VMEM((tm, FF), BF16), pltpu.SemaphoreType.DMA(())],
        sem=("arbitrary",), name=name)
    return outs


def _ffn_gate_up(x, gain, wg, wu, name, carry=None):
    T = x.shape[0]
    tm = 512

    def body(x_ref, gain_ref, wg_ref, wu_ref, n_ref, g_ref, u_ref, a_ref):
        xv = x_ref[...]
        r = lax.rsqrt(jnp.mean(xv * xv, axis=-1, keepdims=True) + EPS)
        n_ref[...] = (xv * r * gain_ref[...]).astype(BF16)
        for c in range(FF // FC):
            cols = slice(c * FC, (c + 1) * FC)
            nb = n_ref[...]
            g = lax.dot_general(nb, wg_ref[cols, :], NT, preferred_element_type=F32)
            u = lax.dot_general(nb, wu_ref[cols, :], NT, preferred_element_type=F32)
            g_ref[:, cols] = g.astype(BF16)
            u_ref[:, cols] = u.astype(BF16)
            a_ref[:, cols] = (g * _sigmoid(g) * u).astype(BF16)

    row = pl.BlockSpec((tm, D), lambda t: (t, 0))
    wide = pl.BlockSpec((tm, FF), lambda t: (t, 0))
    return _pallas(
        body, (x, gain, wg, wu), grid=(T // tm,),
        in_specs=[row, _resident((1, D)), _resident((FF, D)), _resident((FF, D))],
        out_specs=[row, wide, wide, wide],
        out_shape=[jax.ShapeDtypeStruct((T, D), BF16)] + [jax.ShapeDtypeStruct((T, FF), BF16)] * 3,
        scratch_shapes=[], sem=("parallel",), name=name, carry=carry)


def _ffn_down_mix_in(x, a, wd, gain, win, name):
    T = x.shape[0]
    tm = 512

    def body(x_ref, a_ref, wd_ref, gain_ref, win_ref, h_ref, u_ref, n_ref):
        hv = x_ref[...] + 0.5 * jnp.dot(a_ref[...], wd_ref[...], preferred_element_type=F32)
        h_ref[...] = hv
        r = lax.rsqrt(jnp.mean(hv * hv, axis=-1, keepdims=True) + EPS)
        n_ref[...] = (hv * r * gain_ref[...]).astype(BF16)
        u_ref[...] = lax.dot_general(n_ref[...], win_ref[...], NT, preferred_element_type=F32).astype(BF16)

    row = pl.BlockSpec((tm, D), lambda t: (t, 0))
    wide = pl.BlockSpec((tm, FF), lambda t: (t, 0))
    outs, _ = _pallas(
        body, (x, a, wd, gain, win), grid=(T // tm,),
        in_specs=[row, wide, _resident((FF, D)), _resident((1, D)), _resident((DIN, D))],
        out_specs=[row, pl.BlockSpec((tm, DIN), lambda t: (t, 0)), row],
        out_shape=[jax.ShapeDtypeStruct((T, D), F32), jax.ShapeDtypeStruct((T, DIN), BF16),
                   jax.ShapeDtypeStruct((T, D), BF16)],
        scratch_shapes=[], sem=("parallel",), name=name)
    return outs


def _ffn_bwd_act(dyb, g, u, x, dout, gain, wg, wu, wd, name, carry=None):
    T = x.shape[0]
    tm = 256
    nt = T // tm

    def body(dy_ref, g_ref, u_ref, x_ref, dout_ref, gain_ref, wg_ref, wu_ref, wd_ref,
             dg_ref, du_ref, dx_ref, dgn_ref):
        for c in range(FF // FC):
            cols = slice(c * FC, (c + 1) * FC)
            da = lax.dot_general(dy_ref[...], wd_ref[cols, :], NT, preferred_element_type=F32)
            gv = g_ref[:, cols].astype(F32)
            uv = u_ref[:, cols].astype(F32)
            sg = _sigmoid(gv)
            dg_ref[:, cols] = (da * uv * (sg * (1.0 + gv * (1.0 - sg)))).astype(BF16)
            du_ref[:, cols] = (da * (gv * sg)).astype(BF16)
        dn = (jnp.dot(dg_ref[...], wg_ref[...], preferred_element_type=F32)
              + jnp.dot(du_ref[...], wu_ref[...], preferred_element_type=F32))
        dx, dgain = _rms_bwd_rows(dn, x_ref[...], gain_ref[...])
        dx_ref[...] = dout_ref[...] + dx

        @pl.when(pl.program_id(0) == 0)
        def _():
            dgn_ref[...] = jnp.zeros_like(dgn_ref)

        dgn_ref[...] += dgain[None]

    row = pl.BlockSpec((tm, D), lambda t: (t, 0))
    wide = pl.BlockSpec((tm, FF), lambda t: (t, 0))
    return _pallas(
        body, (dyb, g, u, x, dout, gain, wg, wu, wd), grid=(nt,),
        in_specs=[row, wide, wide, row, row, _resident((1, D)), _resident((FF, D)), _resident((FF, D)),
                  _resident((FF, D))],
        out_specs=[wide, wide, row, pl.BlockSpec((1, 1, D), lambda t: (0, 0, 0))],
        out_shape=[jax.ShapeDtypeStruct((T, FF), BF16)] * 2
                  + [jax.ShapeDtypeStruct((T, D), F32), jax.ShapeDtypeStruct((1, 1, D), F32)],
        scratch_shapes=[], sem=("arbitrary",), name=name, carry=carry)


def _ffn_bwd_w(lhs, rhs, name, carry=None):
    T = rhs.shape[0]
    tf = 256

    def body(l_ref, r_ref, dw_ref):
        dw_ref[...] = lax.dot_general(l_ref[...], r_ref[...], TN, preferred_element_type=F32).astype(BF16)

    (dw,), got = _pallas(
        body, (lhs, rhs), grid=(FF // tf,),
        in_specs=[pl.BlockSpec((T, tf), lambda f: (0, f)), _resident((T, D))],
        out_specs=[pl.BlockSpec((tf, D), lambda f: (f, 0))], out_shape=[jax.ShapeDtypeStruct((FF, D), BF16)],
        scratch_shapes=[], sem=("parallel",), name=name, carry=carry)
    return dw, got


def _rms_bwd_rows(dn, xv, gain):
    r = lax.rsqrt(jnp.mean(xv * xv, axis=-1, keepdims=True) + EPS)
    xhat = xv * r
    dxhat = dn * gain
    dx = r * (dxhat - xhat * jnp.mean(dxhat * xhat, axis=-1, keepdims=True))
    return dx, jnp.sum(dn * xhat, axis=0, keepdims=True)


def _mix_out_bwd(dh, attn, conv, wout):
    T = dh.shape[0]
    tm = 512
    nt = T // tm

    def body(dh_ref, a_ref, c_ref, w_ref, da_ref, dc_ref, dw_ref, acc_scr):
        t = pl.program_id(0)

        @pl.when(t == 0)
        def _():
            acc_scr[...] = jnp.zeros_like(acc_scr)

        dhb = dh_ref[...].astype(BF16)
        dmix = lax.dot_general(dhb, w_ref[...], NT, preferred_element_type=F32)
        da_ref[...] = dmix[:, 0:DA].astype(BF16)
        dc_ref[...] = dmix[:, DA:D].astype(BF16)
        acc_scr[0:DA, :] += lax.dot_general(a_ref[...], dhb, TN, preferred_element_type=F32)
        acc_scr[DA:D, :] += lax.dot_general(c_ref[...], dhb, TN, preferred_element_type=F32)

        @pl.when(t == nt - 1)
        def _():
            dw_ref[...] = acc_scr[...].astype(BF16)

    row = pl.BlockSpec((tm, D), lambda t: (t, 0))
    half = pl.BlockSpec((tm, DA), lambda t: (t, 0))
    full = pl.BlockSpec((D, D), lambda t: (0, 0))
    return pl.pallas_call(
        body, grid=(nt,), in_specs=[row, half, half, full], out_specs=[half, half, full],
        out_shape=[jax.ShapeDtypeStruct((T, DA), BF16)] * 2 + [jax.ShapeDtypeStruct((D, D), BF16)],
        scratch_shapes=[pltpu.VMEM((D, D), F32)],
        compiler_params=_cp(("arbitrary",)), name="mix_out_bwd")(dh, attn, conv, wout)


def _mix_in_bwd(dparts, win, nb, h, dh, gain):
    T = h.shape[0]
    tm = 512
    nt = T // tm

    def body(d0, d1, d2, d3, d4, w_ref, n_ref, h_ref, dh_ref, gain_ref,
             dw_ref, dx_ref, dyb_ref, dg_ref, acc_scr):
        t = pl.program_id(0)

        @pl.when(t == 0)
        def _():
            acc_scr[...] = jnp.zeros_like(acc_scr)

        n = n_ref[...]
        dn = jnp.zeros((tm, D), F32)
        for i, d_ref in enumerate((d0, d1, d2, d3, d4)):
            dv = d_ref[...]
            dn = dn + jnp.dot(dv, w_ref[i * DA:(i + 1) * DA, :], preferred_element_type=F32)
            acc_scr[i * DA:(i + 1) * DA, :] += lax.dot_general(dv, n, TN, preferred_element_type=F32)
        dx, dgain = _rms_bwd_rows(dn, h_ref[...], gain_ref[...])
        tot = dh_ref[...] + dx
        dx_ref[...] = tot
        dyb_ref[...] = (0.5 * tot).astype(BF16)
        dg_ref[...] = dgain[None]

        @pl.when(t == nt - 1)
        def _():
            dw_ref[...] = acc_scr[...].astype(BF16)

    row = pl.BlockSpec((tm, D), lambda t: (t, 0))
    half = pl.BlockSpec((tm, DA), lambda t: (t, 0))
    full = pl.BlockSpec((DIN, D), lambda t: (0, 0))
    return pl.pallas_call(
        body, grid=(nt,),
        in_specs=[half] * 5 + [full, row, row, row, pl.BlockSpec((1, D), lambda t: (0, 0))],
        out_specs=[full, row, row, pl.BlockSpec((1, 1, D), lambda t: (t, 0, 0))],
        out_shape=[jax.ShapeDtypeStruct((DIN, D), BF16), jax.ShapeDtypeStruct((T, D), F32),
                   jax.ShapeDtypeStruct((T, D), BF16), jax.ShapeDtypeStruct((nt, 1, D), F32)],
        scratch_shapes=[pltpu.VMEM((DIN, D), F32)],
        compiler_params=_cp(("arbitrary",)), name="mix_in_bwd")(*dparts, win, nb, h, dh, gain)


def _head_masks():
    lane = lax.broadcasted_iota(jnp.int32, (1, 2 * HD), 1)
    m0 = lane < HD
    return m0, jnp.logical_not(m0)


def _stack_heads(v, m0, m1):
    z = jnp.zeros_like(v)
    return jnp.concatenate([jnp.where(m0, v, z), jnp.where(m1, v, z)], axis=0)


def _unstack_heads(v2, m0):
    return jnp.where(m0, v2[0:BLK], v2[BLK:2 * BLK])


def _head_sums(xv):
    ri = lax.broadcasted_iota(jnp.int32, (2 * HD, 2 * HD), 0)
    ci = lax.broadcasted_iota(jnp.int32, (2 * HD, 2 * HD), 1)
    ones = jnp.where((ri < HD) == (ci < HD), 1.0, 0.0).astype(BF16)
    hi = xv.astype(BF16)
    lo = (xv - hi.astype(F32)).astype(BF16)
    return (jnp.dot(hi, ones, preferred_element_type=F32) + jnp.dot(lo, ones, preferred_element_type=F32))


def _head_rms(xv):
    return lax.rsqrt(_head_sums(xv * xv) * (1.0 / HD) + EPS)


def _band_mask(first):
    qi = lax.broadcasted_iota(jnp.int32, (BLK, 2 * BLK), 0)
    ci = lax.broadcasted_iota(jnp.int32, (BLK, 2 * BLK), 1)
    band = (ci >= qi) & (ci <= qi + BLK)
    return band & ((ci >= BLK) | jnp.logical_not(first))


def _block_rows(j, d, seg):
    r, n = j // seg, j % seg
    start = r + (d * BLK) * n
    first = n == 0
    prev = jnp.where(first, start, start - d * BLK)
    return pl.ds(start, BLK, stride=d), pl.ds(prev, BLK, stride=d), first


def _block_keys(refs, cur, prev, first, single):
    if single:
        qi = lax.broadcasted_iota(jnp.int32, (BLK, BLK), 0)
        ci = lax.broadcasted_iota(jnp.int32, (BLK, BLK), 1)
        return [r[cur, :].astype(BF16) for r in refs], ci <= qi
    return ([jnp.concatenate([r[prev, :], r[cur, :]], axis=0).astype(BF16) for r in refs], _band_mask(first))


def _attn_fwd(u, qg2, kg2, B, S, carry=None):
    T = B * S
    NB = S // BLK
    scale = HD ** -0.5

    def body(q_ref, k_ref, v_ref, qg_ref, kg_ref, o_ref, lse_ref, qn, kn, vn, os_, ls_):
        m0, m1 = _head_masks()
        qv = q_ref[...].astype(F32)
        qn[...] = qv * _head_rms(qv) * (qg_ref[...] * scale)
        kv = k_ref[...].astype(F32)
        kn[...] = kv * _head_rms(kv) * kg_ref[...]
        vn[...] = v_ref[...].astype(F32)

        for i, d in enumerate(DILS):
            seg = NB // d

            def blk(j, c, i=i, d=d, seg=seg):
                cur, prev, first = _block_rows(j, d, seg)
                q2 = _stack_heads(qn[cur, :].astype(BF16), m0, m1)
                (kk, vv), mask = _block_keys((kn, vn), cur, prev, first, False)
                s = lax.dot_general(q2, kk, NT, preferred_element_type=F32)
                s = jnp.where(jnp.concatenate([mask, mask], axis=0), s, -1e30)
                mx = jnp.max(s, axis=-1, keepdims=True)
                p = jnp.exp(s - mx)
                l = jnp.sum(p, axis=-1, keepdims=True)
                o2 = jnp.dot((p * (1.0 / l)).astype(BF16), vv, preferred_element_type=F32)
                os_[i, cur, :] = _unstack_heads(o2, m0)
                ls_[i, cur, :] = _unstack_heads(mx + jnp.log(l), m0)
                return c

            lax.fori_loop(0, NB, blk, 0, unroll=8)

        def comb(c, carry):
            rows = pl.ds(pl.multiple_of(c * 256, 256), 256)
            l0, l1, l2 = ls_[0, rows, :], ls_[1, rows, :], ls_[2, rows, :]
            mx = jnp.maximum(jnp.maximum(l0, l1), l2)
            e0, e1, e2 = jnp.exp(l0 - mx), jnp.exp(l1 - mx), jnp.exp(l2 - mx)
            tot = e0 + e1 + e2
            inv = 1.0 / tot
            o = (e0 * os_[0, rows, :] + e1 * os_[1, rows, :] + e2 * os_[2, rows, :]) * inv
            o_ref[rows, :] = o.astype(BF16)
            lse_ref[rows, :] = mx + jnp.log(tot)
            return carry

        lax.fori_loop(0, S // 256, comb, 0)

    pair = 2 * HD
    blk_spec = lambda off: pl.BlockSpec((S, pair), lambda b, p, off=off: (b, off + p))
    gspec = pl.BlockSpec((1, pair), lambda b, p: (0, 0))
    return _pallas(
        body, (u, u, u, qg2, kg2), grid=(B, DA // pair),
        in_specs=[blk_spec(0), blk_spec(DA // pair), blk_spec(2 * DA // pair), gspec, gspec],
        out_specs=[blk_spec(0), blk_spec(0)],
        out_shape=[jax.ShapeDtypeStruct((T, DA), BF16), jax.ShapeDtypeStruct((T, DA), F32)],
        scratch_shapes=[pltpu.VMEM((S, pair), F32)] * 3 + [pltpu.VMEM((3, S, pair), F32)] * 2,
        sem=("parallel", "parallel"), name="attn_fwd", carry=carry)


def _attn_bwd(u, attn, dattn, lse, qg2, kg2, B, S, carry=None):
    T = B * S
    NB = S // BLK
    scale = HD ** -0.5
    pair = 2 * HD

    def body(q_ref, k_ref, v_ref, o_ref, do_ref, lse_ref, qg_ref, kg_ref,
             dq_ref, dk_ref, dv_ref, dgn_ref,
             qn, kn, vn, don, ldl, accq, acck, accv, rq, rk):
        m0, m1 = _head_masks()
        lane = lax.broadcasted_iota(jnp.int32, (1, pair), 1)
        qv = q_ref[...].astype(F32)
        rq[...] = _head_rms(qv)
        qn[...] = qv * rq[...] * (qg_ref[...] * scale)
        kv = k_ref[...].astype(F32)
        rk[...] = _head_rms(kv)
        kn[...] = kv * rk[...] * kg_ref[...]
        vn[...] = v_ref[...].astype(F32)
        dov = do_ref[...].astype(F32)
        don[...] = dov
        ldl[...] = jnp.where((lane % HD) < HD // 2, lse_ref[...], _head_sums(dov * o_ref[...].astype(F32)))

        for i, d in enumerate(DILS):
            seg = NB // d

            def blk(j, c, i=i, d=d, seg=seg):
                cur, prev, first = _block_rows(j, d, seg)
                q2 = _stack_heads(qn[cur, :].astype(BF16), m0, m1)
                do2 = _stack_heads(don[cur, :].astype(BF16), m0, m1)
                (kk, vv), mask = _block_keys((kn, vn), cur, prev, first, seg == 1)
                ldv = ldl[cur, :]
                lse2 = jnp.concatenate([ldv[:, 0:1], ldv[:, HD:HD + 1]], axis=0)
                dl2 = jnp.concatenate([ldv[:, HD // 2:HD // 2 + 1], ldv[:, HD + HD // 2:HD + HD // 2 + 1]], axis=0)
                s = lax.dot_general(q2, kk, NT, preferred_element_type=F32)
                p = jnp.where(jnp.concatenate([mask, mask], axis=0), jnp.exp(s - lse2), 0.0)
                dp = lax.dot_general(do2, vv, NT, preferred_element_type=F32)
                ds = (p * (dp - dl2)).astype(BF16)
                dq_acc = _unstack_heads(jnp.dot(ds, kk, preferred_element_type=F32), m0)
                dk_acc = lax.dot_general(ds, q2, TN, preferred_element_type=F32)
                dv_acc = lax.dot_general(p.astype(BF16), do2, TN, preferred_element_type=F32)
                if i == 0:
                    accq[cur, :] = dq_acc
                    acck[cur, :] = dk_acc[BLK:2 * BLK]
                    accv[cur, :] = dv_acc[BLK:2 * BLK]
                    acck[prev, :] += dk_acc[0:BLK]
                    accv[prev, :] += dv_acc[0:BLK]
                elif seg == 1:
                    accq[cur, :] += dq_acc
                    acck[cur, :] += dk_acc
                    accv[cur, :] += dv_acc
                else:
                    accq[cur, :] += dq_acc
                    acck[prev, :] += dk_acc[0:BLK]
                    acck[cur, :] += dk_acc[BLK:2 * BLK]
                    accv[prev, :] += dv_acc[0:BLK]
                    accv[cur, :] += dv_acc[BLK:2 * BLK]
                return c

            lax.fori_loop(0, NB, blk, 0, unroll=8)

        def norm_bwd(x_ref, r_ref, dn, gain):
            r = r_ref[...]
            xhat = x_ref[...].astype(F32) * r
            dxhat = dn * gain
            dx = r * (dxhat - xhat * (_head_sums(dxhat * xhat) * (1.0 / HD)))
            return dx, jnp.sum(dn * xhat, axis=0, keepdims=True)

        dq, dgq = norm_bwd(q_ref, rq, accq[...], qg_ref[...] * scale)
        dk, dgk = norm_bwd(k_ref, rk, acck[...], kg_ref[...])
        dq_ref[...] = dq.astype(BF16)
        dk_ref[...] = dk.astype(BF16)
        dv_ref[...] = accv[...].astype(BF16)
        dgn_ref[...] = jnp.concatenate([dgq * scale, dgk, jnp.zeros((6, pair), F32)], axis=0)[None]

    blk_spec = lambda off: pl.BlockSpec((S, pair), lambda b, p, off=off: (b, off + p))
    gspec = pl.BlockSpec((1, pair), lambda b, p: (0, 0))
    np_ = DA // pair
    return _pallas(
        body, (u, u, u, attn, dattn, lse, qg2, kg2), grid=(B, np_),
        in_specs=[blk_spec(0), blk_spec(np_), blk_spec(2 * np_), blk_spec(0), blk_spec(0), blk_spec(0),
                  gspec, gspec],
        out_specs=[blk_spec(0), blk_spec(0), blk_spec(0),
                   pl.BlockSpec((1, 8, pair), lambda b, p: (b * np_ + p, 0, 0))],
        out_shape=[jax.ShapeDtypeStruct((T, DA), BF16)] * 3 + [jax.ShapeDtypeStruct((B * np_, 8, pair), F32)],
        scratch_shapes=[pltpu.VMEM((S, pair), F32)] * 10,
        sem=("parallel", "parallel"), name="attn_bwd", carry=carry)


CT = 32
CPAD = 32


def _shifted(win, offsets):
    rolled, out = {}, {}
    n = win.shape[0]
    for o in offsets:
        sub = o % 8
        if sub not in rolled:
            rolled[sub] = win if sub == 0 else pltpu.roll(win, n - sub, 0)
        out[o] = rolled[sub][o - sub:o - sub + CT, :]
    return out


def _ln_fwd(y, g, b):
    mu = jnp.mean(y, axis=-1, keepdims=True)
    yc = y - mu
    rstd = lax.rsqrt(jnp.mean(yc * yc, axis=-1, keepdims=True) + EPS)
    xhat = yc * rstd
    return xhat, rstd, xhat * g + b


def _fill_glu(ca_ref, cg_ref, glu, S):
    glu[pl.ds(0, CPAD), :] = jnp.zeros((CPAD, DC), F32)

    def fill(i, c):
        rows = pl.ds(pl.multiple_of(i * 256, 256), 256)
        a = ca_ref[rows, :].astype(F32)
        gt = cg_ref[rows, :].astype(F32)
        glu[pl.ds(pl.multiple_of(CPAD + i * 256, CT), 256), :] = a * _sigmoid(gt)
        return c

    lax.fori_loop(0, S // 256, fill, 0)


def _conv_fwd(u, cw, cb, lg, lb, B, S):
    T = B * S

    def body(ca_ref, cg_ref, w_ref, b_ref, lg_ref, lb_ref, o_ref, y_ref, glu):
        _fill_glu(ca_ref, cg_ref, glu, S)

        def step(i, c):
            t0 = pl.multiple_of(i * CT, CT)
            win = glu[pl.ds(t0, 2 * CT), :]
            acc = jnp.zeros((CT, DC), F32) + b_ref[...]
            taps = _shifted(win, [k + 2 for k in range(CK)])
            for k in range(CK):
                acc = acc + taps[k + 2] * w_ref[k:k + 1, :]
            y_ref[pl.ds(t0, CT), :] = acc
            _, _, z = _ln_fwd(acc, lg_ref[...], lb_ref[...])
            o_ref[pl.ds(t0, CT), :] = (z * _sigmoid(z)).astype(BF16)
            return c

        lax.fori_loop(0, S // CT, step, 0, unroll=4)

    vec = pl.BlockSpec((1, DC), lambda b: (0, 0))
    return pl.pallas_call(
        body, grid=(B,),
        in_specs=[pl.BlockSpec((S, DC), lambda b: (b, 3)), pl.BlockSpec((S, DC), lambda b: (b, 4)),
                  pl.BlockSpec((CT, DC), lambda b: (0, 0)), vec, vec, vec],
        out_specs=[pl.BlockSpec((S, DC), lambda b: (b, 0))] * 2,
        out_shape=[jax.ShapeDtypeStruct((T, DC), BF16), jax.ShapeDtypeStruct((T, DC), F32)],
        scratch_shapes=[pltpu.VMEM((CPAD + S, DC), F32)],
        compiler_params=_cp(("parallel",)), name="conv_fwd")(u, u, cw, cb, lg, lb)


def _conv_bwd(u, y, dconv, cw, lg, lb, B, S):
    T = B * S

    def body(ca_ref, cg_ref, y_ref, dc_ref, w_ref, lg_ref, lb_ref,
             dca_ref, dcg_ref, dw_ref, ds_ref, glu, dyp, dwacc):
        _fill_glu(ca_ref, cg_ref, glu, S)
        dyp[pl.ds(S, CPAD), :] = jnp.zeros((CPAD, DC), F32)
        lgv, lbv = lg_ref[...], lb_ref[...]

        def sum8(v):
            return functools.reduce(jnp.add, [v[r:r + 8] for r in range(0, v.shape[0], 8)])

        P1 = 4 * CT

        def p1(i, carry):
            sb, sg, sl = carry
            t0 = pl.multiple_of(i * P1, P1)
            xhat, rstd, z = _ln_fwd(y_ref[pl.ds(t0, P1), :], lgv, lbv)
            sz = _sigmoid(z)
            dz = dc_ref[pl.ds(t0, P1), :].astype(F32) * (sz * (1.0 + z * (1.0 - sz)))
            dxhat = dz * lgv
            dy = rstd * (dxhat - jnp.mean(dxhat, axis=-1, keepdims=True)
                         - xhat * jnp.mean(dxhat * xhat, axis=-1, keepdims=True))
            dyp[pl.ds(t0, P1), :] = dy
            return sb + sum8(dy), sg + sum8(dz * xhat), sl + sum8(dz)

        z8 = jnp.zeros((8, DC), F32)
        sb, sg, sl = lax.fori_loop(0, S // P1, p1, (z8, z8, z8))
        rs = lambda v: jnp.sum(v, axis=0, keepdims=True)
        ds_ref[...] = jnp.concatenate([rs(sb), rs(sg), rs(sl), jnp.zeros((5, DC), F32)], axis=0)[None]

        def p2(i, c):
            t0 = pl.multiple_of(i * CT, CT)
            win = dyp[pl.ds(t0, 2 * CT), :]
            acc = jnp.zeros((CT, DC), F32)
            taps = _shifted(win, [30 - k for k in range(CK)])
            for k in range(CK):
                acc = acc + taps[30 - k] * w_ref[k:k + 1, :]
            a = ca_ref[pl.ds(t0, CT), :].astype(F32)
            sgt = _sigmoid(cg_ref[pl.ds(t0, CT), :].astype(F32))
            dca_ref[pl.ds(t0, CT), :] = (acc * sgt).astype(BF16)
            dcg_ref[pl.ds(t0, CT), :] = (acc * a * sgt * (1.0 - sgt)).astype(BF16)
            return c

        lax.fori_loop(0, S // CT, p2, 0)

        dwacc[...] = jnp.zeros_like(dwacc)

        def p3(i, c):
            t0 = pl.multiple_of(i * CT, CT)
            win = glu[pl.ds(t0, 2 * CT), :]
            dy = dyp[pl.ds(t0, CT), :]
            for k in range(CK):
                dwacc[k] += sum8(dy * win[k + 2:k + 2 + CT, :])
            return c

        lax.fori_loop(0, S // CT, p3, 0)
        dw_ref[...] = jnp.sum(dwacc[...], axis=1)[None]

    vec = pl.BlockSpec((1, DC), lambda b: (0, 0))
    seq = pl.BlockSpec((S, DC), lambda b: (b, 0))
    return pl.pallas_call(
        body, grid=(B,),
        in_specs=[pl.BlockSpec((S, DC), lambda b: (b, 3)), pl.BlockSpec((S, DC), lambda b: (b, 4)),
                  seq, seq, pl.BlockSpec((CT, DC), lambda b: (0, 0)), vec, vec],
        out_specs=[seq, seq, pl.BlockSpec((1, CT, DC), lambda b: (b, 0, 0)),
                   pl.BlockSpec((1, 8, DC), lambda b: (b, 0, 0))],
        out_shape=[jax.ShapeDtypeStruct((T, DC), BF16)] * 2
                  + [jax.ShapeDtypeStruct((B, CT, DC), F32), jax.ShapeDtypeStruct((B, 8, DC), F32)],
        scratch_shapes=[pltpu.VMEM((CPAD + S, DC), F32), pltpu.VMEM((S + CPAD, DC), F32),
                        pltpu.VMEM((CT, 8, DC), F32)],
        compiler_params=_cp(("parallel",)), name="conv_bwd")(u, u, y, dconv, cw, lg, lb)


def _local_step(x, target, norms, W, B, S, comm=None):
    qg2 = jnp.concatenate([norms["q_norm"], norms["q_norm"]], axis=1)
    kg2 = jnp.concatenate([norms["k_norm"], norms["k_norm"]], axis=1)
    cw = jnp.concatenate([W["conv_w"], jnp.zeros((1, DC), F32)], axis=0)

    W = dict(W)
    (n1, g1, u1, act1), got = _ffn_gate_up(x, norms["ffn1_norm"], W["wg1"], W["wu1"], "ffn1_gate_up",
                                           carry=comm.gathers["down_in"] if comm else None)
    if comm:
        W.update(comm.gathered("down_in", got))
    h1, u, n2 = _ffn_down_mix_in(x, act1, W["wd1"], norms["mix_norm"], W["win"], "ffn1_down_mix_in")
    (attn, lse), got = _attn_fwd(u, qg2, kg2, B, S, carry=comm.gathers["ffn2"] if comm else None)
    if comm:
        W = dict(W, **comm.gathered("ffn2", got))
    conv, y = _conv_fwd(u, cw, norms["conv_b"], norms["conv_ln_g"], norms["conv_ln_b"], B, S)
    h2, n3, g2, u2, dout, dyb, sq, act2 = _mix_out_ffn_loss(h1, attn, conv, W["wout"], norms["ffn2_norm"],
                                                            W["wg2"], W["wu2"], W["wd2"], target, "ffn2_fwd")
    loss = (0.5 / D) * jnp.sum(sq)

    (dg2, du2, dh2, dgn_ffn2), _ = _ffn_bwd_act(dyb, g2, u2, h2, dout, norms["ffn2_norm"],
                                               W["wg2"], W["wu2"], W["wd2"], "ffn2_bwd_act")
    dwd2, _ = _ffn_bwd_w(act2, dyb, "ffn2_bwd_wd")
    dwg2, _ = _ffn_bwd_w(dg2, n3, "ffn2_bwd_wg")
    dwu2, _ = _ffn_bwd_w(du2, n3, "ffn2_bwd_wu")
    dattn, dconv, dwout = _mix_out_bwd(dh2, attn, conv, W["wout"])
    carry = comm.reduce_start({"wg2": dwg2, "wu2": dwu2, "wd2": dwd2, "wout": dwout}) if comm else None
    (dq, dk, dv, dgn_qk), got = _attn_bwd(u, attn, dattn, lse, qg2, kg2, B, S, carry=carry)
    if comm:
        comm.reduce_done(carry, got)
    dca, dcg, dcw, dcs = _conv_bwd(u, y, dconv, cw, norms["conv_ln_g"], norms["conv_ln_b"], B, S)
    dwin, dh1, dyb1, dgn_mix = _mix_in_bwd((dq, dk, dv, dca, dcg), W["win"], n2, h1, dh2, norms["mix_norm"])
    carry = comm.reduce_start({"win": dwin}) if comm else None
    dwd1, got = _ffn_bwd_w(act1, dyb1, "ffn1_bwd_wd", carry=carry)
    if comm:
        comm.reduce_done(carry, got)
    carry = comm.update_start(("wg2", "wu2", "wd2", "wout", "win")) if comm else None
    (dg1, du1, gx, dgn_ffn1), got = _ffn_bwd_act(dyb1, g1, u1, x, dh1, norms["ffn1_norm"],
                                                W["wg1"], W["wu1"], W["wd1"], "ffn1_bwd_act", carry=carry)
    if comm:
        comm.update_done(carry, got)
        carry = comm.reduce_start({"wd1": dwd1})
    dwg1, got = _ffn_bwd_w(dg1, n1, "ffn1_bwd_wg", carry=carry)
    if comm:
        comm.reduce_done(carry, got)
        carry = comm.reduce_start({"wg1": dwg1})
    dwu1, got = _ffn_bwd_w(du1, n1, "ffn1_bwd_wu", carry=carry)
    if comm:
        comm.reduce_done(carry, got)
        comm.last = comm.reduce_start({"wu1": dwu1})

    qk = jnp.sum(dgn_qk, axis=0)
    cs = jnp.sum(dcs, axis=0)
    small = {
        "ffn1_norm": jnp.sum(dgn_ffn1, axis=0),
        "mix_norm": jnp.sum(dgn_mix, axis=0),
        "q_norm": qk[0:1, 0:HD] + qk[0:1, HD:2 * HD],
        "k_norm": qk[1:2, 0:HD] + qk[1:2, HD:2 * HD],
        "conv_w": jnp.sum(dcw, axis=0)[0:CK],
        "conv_b": cs[0:1],
        "conv_ln_g": cs[1:2],
        "conv_ln_b": cs[2:3],
        "ffn2_norm": jnp.sum(dgn_ffn2, axis=0),
    }
    big = {"wg1": dwg1, "wu1": dwu1, "wd1": dwd1, "win": dwin, "wout": dwout,
           "wg2": dwg2, "wu2": dwu2, "wd2": dwd2}
    return loss, gx, big, small


HBM = pl.BlockSpec(memory_space=pltpu.HBM)
VMEM = pl.BlockSpec(memory_space=pltpu.VMEM)


def _place():
    return lax.axis_index("x"), lax.axis_index("y"), lax.axis_index("c")


class _GatherCarry:
    def __init__(self, shards, mid_at=0.5):
        nt = len(shards)
        self.mid_at = mid_at
        self.shards = shards
        self.in_arrays = [s for s, _ in shards]
        self.in_specs = [VMEM] * nt
        self.out_shape = [jax.ShapeDtypeStruct((NDEV * s.shape[0], s.shape[1]), dt) for s, dt in shards]
        self.out_specs = [HBM] * nt
        self.scratch = ([pltpu.VMEM(s.shape, dt) for s, dt in shards]
                        + [pltpu.SemaphoreType.DMA((nt, 7)), pltpu.SemaphoreType.DMA((nt, 7)),
                           pltpu.SemaphoreType.DMA((nt,))])

    def _copies(self, outs, scr):
        nt = len(self.shards)
        stages = scr[:nt]
        send_sems, recv_sems, local_sems = scr[nt:]
        x, y, c = _place()
        me, sibling = (x, y, c), (x, y, 1 - c)
        xn, yn, diag = (1 - x, y, c), (x, 1 - y, c), (1 - x, 1 - y, c)
        via = (x ^ c, y ^ (1 - c), c)
        onto = (x ^ (1 - c), y ^ c, c)

        def rows(t, px, py, pc):
            r = self.shards[t][0].shape[0]
            return outs[t].at[pl.ds((4 * px + 2 * py + pc) * r, r), :]

        def copy(t, k, block, to, src=None):
            return pltpu.make_async_remote_copy(
                src_ref=rows(t, *block) if src is None else src, dst_ref=rows(t, *block),
                send_sem=send_sems.at[t, k], recv_sem=recv_sems.at[t, k],
                device_id=to, device_id_type=MESH)

        sib = lambda b: (b[0], b[1], 1 - c)
        return dict(
            local=[pltpu.make_async_copy(stages[t], rows(t, *me), local_sems.at[t]) for t in range(nt)],
            own=[[copy(t, 0, me, sibling, src=stages[t]), copy(t, 1, me, xn, src=stages[t]),
                  copy(t, 2, me, yn, src=stages[t])] for t in range(nt)],
            relay=[copy(t, 3, via, onto) for t in range(nt)],
            down=[[copy(t, 4, xn, sibling), copy(t, 5, yn, sibling)] for t in range(nt)],
            down_diag=[copy(t, 6, diag, sibling) for t in range(nt)],
            got_xy=[[copy(t, 1, xn, me), copy(t, 2, yn, me)] for t in range(nt)],
            got_diag=[copy(t, 3, diag, me) for t in range(nt)],
            got_sib=[[copy(t, 0, sibling, me), copy(t, 4, sib(xn), me), copy(t, 5, sib(yn), me),
                      copy(t, 6, sib(diag), me)] for t in range(nt)])

    def start(self, ins, outs, scr):
        cps = self._copies(outs, scr)
        for t, (_, dt) in enumerate(self.shards):
            scr[t][...] = ins[t][...].astype(dt)
            for cp in [cps["local"][t]] + cps["own"][t]:
                cp.start()

    def stages(self):
        sizes = [s.size * jnp.dtype(dt).itemsize for s, dt in self.shards]
        done = [sum(sizes[:t + 1]) / sum(sizes) for t in range(len(sizes))]
        return [(self.mid_at * f, functools.partial(self.mid, t)) for t, f in enumerate(done)]

    def mid(self, t, ins, outs, scr):
        cps = self._copies(outs, scr)
        for cp in cps["got_xy"][t]:
            cp.wait_recv()
        for cp in [cps["relay"][t]] + cps["down"][t]:
            cp.start()

    def finish(self, ins, outs, scr):
        cps = self._copies(outs, scr)
        for t in range(len(self.shards)):
            cps["got_diag"][t].wait_recv()
            cps["down_diag"][t].start()
        for t in range(len(self.shards)):
            for cp in cps["got_sib"][t]:
                cp.wait_recv()
            for cp in cps["own"][t] + [cps["relay"][t]] + cps["down"][t] + [cps["down_diag"][t]]:
                cp.wait_send()
            cps["local"][t].wait()


def _run_carry(carry, name):
    def body(*refs):
        n_in, n_out = len(carry.in_arrays), len(carry.out_shape)
        ins, outs, scr = refs[:n_in], refs[n_in:n_in + n_out], refs[n_in + n_out:]
        carry.start(ins, outs, scr)
        for _, stage in carry.stages():
            stage(ins, outs, scr)
        carry.finish(ins, outs, scr)

    return pl.pallas_call(
        body, in_specs=carry.in_specs, out_specs=carry.out_specs, out_shape=carry.out_shape,
        scratch_shapes=carry.scratch, compiler_params=pltpu.CompilerParams(vmem_limit_bytes=VMEM_LIMIT),
        name=name)(*carry.in_arrays)


class _ExchangeCarry:
    def __init__(self, names, grads, mid_at=0.5):
        nt = len(grads)
        self.mid_at = mid_at
        self.names = names
        self.in_arrays = [g.reshape(4, 2, g.shape[0] // NDEV, g.shape[1]) for g in grads]
        self.in_specs = [HBM] * nt
        blocks = [g.shape[2:] for g in self.in_arrays]
        self.out_shape = [jax.ShapeDtypeStruct((3,) + b, BF16) for b in blocks]
        self.out_specs = [HBM] * nt
        self.scratch = ([pltpu.VMEM((4,) + b, BF16) for b in blocks] * 2 + [pltpu.VMEM(b, BF16) for b in blocks]
                        + [pltpu.SemaphoreType.DMA((nt, 3)), pltpu.SemaphoreType.DMA((nt, 3))]
                        + [pltpu.SemaphoreType.DMA((nt,))] * 4)

    def _copies(self, ins, outs, scr):
        nt = len(ins)
        theirs, own, relayed = scr[:nt], scr[nt:2 * nt], scr[2 * nt:3 * nt]
        send_sems, recv_sems, keep_sems, load_sems, swap_send, swap_recv = scr[3 * nt:]
        x, y, c = _place()
        q = lambda cx, cy: 2 * cx + cy
        near, far = (x ^ c, y ^ (1 - c)), (x ^ (1 - c), y ^ c)

        def remote(t, k, src, dst, chip):
            return pltpu.make_async_remote_copy(
                src_ref=src, dst_ref=dst, send_sem=send_sems.at[t, k], recv_sem=recv_sems.at[t, k],
                device_id=(*chip, c), device_id_type=MESH)

        return dict(
            swap=[pltpu.make_async_remote_copy(
                src_ref=ins[t].at[:, 1 - c], dst_ref=theirs[t], send_sem=swap_send.at[t], recv_sem=swap_recv.at[t],
                device_id=(x, y, 1 - c), device_id_type=MESH) for t in range(nt)],
            load=[pltpu.make_async_copy(ins[t].at[:, c], own[t], load_sems.at[t]) for t in range(nt)],
            keep=[pltpu.make_async_copy(own[t].at[q(x, y)], outs[t].at[0], keep_sems.at[t]) for t in range(nt)],
            direct=[remote(t, 0, own[t].at[q(*near)], outs[t].at[1], near) for t in range(nt)],
            relay=[remote(t, 1, own[t].at[q(1 - x, 1 - y)], relayed[t], near) for t in range(nt)],
            merged=[remote(t, 2, own[t].at[q(*far)], outs[t].at[2], far) for t in range(nt)],
            theirs=theirs, own=own, relayed=relayed, far=q(*far))

    EARLY_AT = 0.1

    def stages(self):
        return [(self.EARLY_AT, self.early), (self.mid_at, self.mid)]

    def start(self, ins, outs, scr):
        cps = self._copies(ins, outs, scr)
        for t in range(len(ins)):
            cps["swap"][t].start()
            cps["load"][t].start()

    def early(self, ins, outs, scr):
        cps = self._copies(ins, outs, scr)
        for t in range(len(ins)):
            cps["load"][t].wait()
            cps["swap"][t].wait_recv()
            own, theirs = cps["own"][t], cps["theirs"][t]
            for j in range(4):
                own[j] = (own[j].astype(F32) + theirs[j].astype(F32)).astype(BF16)
            for kind in ("relay", "direct", "keep"):
                cps[kind][t].start()

    def mid(self, ins, outs, scr):
        cps = self._copies(ins, outs, scr)
        for t in range(len(ins)):
            cps["relay"][t].wait_recv()
            own, far = cps["own"][t], cps["far"]
            own[far] = (own[far].astype(F32) + cps["relayed"][t][...].astype(F32)).astype(BF16)
            cps["merged"][t].start()

    def finish(self, ins, outs, scr):
        cps = self._copies(ins, outs, scr)
        for t in range(len(ins)):
            cps["swap"][t].wait_send()
            cps["direct"][t].wait()
            cps["relay"][t].wait_send()
            cps["merged"][t].wait()
            cps["keep"][t].wait()


class _Comm:
    def __init__(self, groups, opt):
        self.names = {tag: list(g) for tag, (g, _) in groups.items()}
        self.gathers = {tag: _GatherCarry(list(g.values()), mid_at) for tag, (g, mid_at) in groups.items()}
        self.reduced = {}
        self.last = None
        self.opt = opt
        self.updated = {}

    def gathered(self, tag, outs):
        return dict(zip(self.names[tag], outs))

    def reduce_start(self, grads):
        names = list(grads)
        return _ExchangeCarry(names, [grads[n] for n in names])

    def reduce_done(self, carry, outs):
        self.reduced.update(zip(carry.names, outs))

    def update_start(self, names):
        w, m, v = zip(*[self.opt[n] for n in names])
        return _AdamWCarry(names, [self.reduced[n] for n in names], w, m, v)

    def update_done(self, carry, outs):
        self.updated.update({n: outs[4 * k:4 * k + 4] for k, n in enumerate(carry.names)})


def _adamw_math(w, g, m, v):
    m = B1 * m + (1.0 - B1) * g
    v = B2 * v + (1.0 - B2) * (g * g)
    m_hat = m / (1.0 - B1 ** STEP)
    v_hat = v / (1.0 - B2 ** STEP)
    delta = -LR * (m_hat / (jnp.sqrt(v_hat) + AEPS) + WD * w)
    return delta, m, v


def _adamw_big(recvs, ws, ms, vs, name):
    nw = len(ws)

    def body(*refs):
        ins, outs = refs[:4 * nw], refs[4 * nw:]
        for k in range(nw):
            @pl.when(pl.program_id(0) // 2 == k)
            def _(k=k):
                r_ref, w_ref, m_ref, v_ref = ins[4 * k:4 * k + 4]
                g_ref, d_ref, mo_ref, vo_ref = outs[4 * k:4 * k + 4]
                g = r_ref[0].astype(F32)
                for q in range(1, 3):
                    g = g + r_ref[q].astype(F32)
                d, mn, vn = _adamw_math(w_ref[...], g, m_ref[...], v_ref[...])
                g_ref[...] = g
                d_ref[...] = d
                mo_ref[...] = mn
                vo_ref[...] = vn

    in_specs, out_specs, out_shape, args = [], [], [], []
    for k, (r, w, m, v) in enumerate(zip(recvs, ws, ms, vs)):
        rows, n = w.shape
        tr = rows // 2
        tile = lambda s, k=k: jnp.clip(s - 2 * k, 0, 1)
        row = pl.BlockSpec((tr, n), lambda s, tile=tile: (tile(s), 0))
        in_specs += [pl.BlockSpec((3, tr, n), lambda s, tile=tile: (0, tile(s), 0)), row, row, row]
        out_specs += [row] * 4
        out_shape += [jax.ShapeDtypeStruct(w.shape, F32)] * 4
        args += [r, w, m, v]
    res = pl.pallas_call(
        body, grid=(2 * nw,), in_specs=in_specs, out_specs=out_specs, out_shape=out_shape,
        compiler_params=_cp(("arbitrary",)), name=name)(*args)
    return [res[4 * k:4 * k + 4] for k in range(nw)]


class _AdamWCarry:
    def __init__(self, names, recvs, ws, ms, vs):
        self.names = names
        nw = len(ws)
        self.halves = [(k, j, w.shape[0] // 2) for k, w in enumerate(ws) for j in range(2)]
        self.in_arrays = [a for quad in zip(recvs, ws, ms, vs) for a in quad]
        self.in_specs = [HBM] * (4 * nw)
        self.out_shape = [jax.ShapeDtypeStruct(w.shape, F32) for w in ws for _ in range(4)]
        self.out_specs = [HBM] * (4 * nw)
        tr, n = max(h[2] for h in self.halves), ws[0].shape[1]
        self.scratch = [pltpu.VMEM((2, 3, tr, n), BF16), pltpu.VMEM((2, 3, tr, n), F32),
                        pltpu.VMEM((2, 4, tr, n), F32), pltpu.SemaphoreType.DMA((2, 4)),
                        pltpu.SemaphoreType.DMA((2, 4))]

    def _copies(self, c, ins, outs, scr):
        rbuf, fbuf, obuf, in_sems, out_sems = scr
        k, j, tr = self.halves[c]
        s, rows = c % 2, pl.ds(j * tr, tr)
        loads = [pltpu.make_async_copy(ins[4 * k].at[:, rows, :], rbuf.at[s, :, pl.ds(0, tr), :], in_sems.at[s, 0])]
        loads += [pltpu.make_async_copy(ins[4 * k + i].at[rows, :], fbuf.at[s, i - 1, pl.ds(0, tr), :],
                                        in_sems.at[s, i]) for i in range(1, 4)]
        stores = [pltpu.make_async_copy(obuf.at[s, q, pl.ds(0, tr), :], outs[4 * k + q].at[rows, :],
                                        out_sems.at[s, q]) for q in range(4)]
        return loads, stores

    def start(self, ins, outs, scr):
        for c in range(2):
            for cp in self._copies(c, ins, outs, scr)[0]:
                cp.start()

    def stages(self):
        n = len(self.halves)
        return [((c + 1) / (n + 1), functools.partial(self.half, c)) for c in range(n)]

    def half(self, c, ins, outs, scr):
        rbuf, fbuf, obuf = scr[:3]
        _, _, tr = self.halves[c]
        s = c % 2
        loads, stores = self._copies(c, ins, outs, scr)
        for cp in loads:
            cp.wait()
        if c >= 2:
            for cp in self._copies(c - 2, ins, outs, scr)[1]:
                cp.wait()
        g = rbuf[s, 0, 0:tr].astype(F32)
        for q in range(1, 3):
            g = g + rbuf[s, q, 0:tr].astype(F32)
        d, mn, vn = _adamw_math(fbuf[s, 0, 0:tr], g, fbuf[s, 1, 0:tr], fbuf[s, 2, 0:tr])
        for q, val in enumerate((g, d, mn, vn)):
            obuf[s, q, 0:tr] = val
        for cp in stores:
            cp.start()
        if c + 2 < len(self.halves):
            for cp in self._copies(c + 2, ins, outs, scr)[0]:
                cp.start()

    def finish(self, ins, outs, scr):
        n = len(self.halves)
        for c in range(max(n - 2, 0), n):
            for cp in self._copies(c, ins, outs, scr)[1]:
                cp.wait()


SMALL_NAMES = ("ffn1_norm", "mix_norm", "ffn2_norm", "conv_b", "conv_ln_g", "conv_ln_b", "q_norm", "k_norm")
SROWS = 8
SMALL_AT = ((0, 0), (1, 0), (2, 0), (3, 0), (3, 512), (4, 0), (4, 512), (4, 640))
LOSS_AT = (4, 768)
CWT = (32, 128)


def _small_sums(gs, loss_row, gcw, carry=None):
    ns = len(SMALL_NAMES)
    widths = [g.shape[1] for g in gs]

    def body(*refs):
        it = iter(refs)
        take = lambda n: [next(it) for _ in range(n)]
        g_refs, (loss_ref, gcw_ref) = take(ns), take(2)
        cins = take(len(carry.in_arrays)) if carry else []
        tot_ref, ctot_ref = take(2)
        couts = take(len(carry.out_shape)) if carry else []
        send, slots, cslots, send_sems, recv_sems, csend_sems, crecv_sems = take(7)
        cscr = list(it)
        x, y, c = _place()
        me = 4 * x + 2 * y + c
        send[...] = jnp.zeros_like(send)
        for k, (r, o) in enumerate(SMALL_AT):
            send[r:r + 1, o:o + widths[k]] = g_refs[k][...]
        send[LOSS_AT[0]:LOSS_AT[0] + 1, LOSS_AT[1]:LOSS_AT[1] + 128] = loss_ref[...]
        slots[me] = send[...]
        cslots[me] = gcw_ref[me]
        cps = []
        for k in range(1, NDEV):
            peer = (x ^ ((k >> 2) & 1), y ^ ((k >> 1) & 1), c ^ (k & 1))
            cps.append(pltpu.make_async_remote_copy(
                src_ref=send, dst_ref=slots.at[me], send_sem=send_sems.at[k - 1], recv_sem=recv_sems.at[k - 1],
                device_id=peer, device_id_type=MESH))
            cps.append(pltpu.make_async_remote_copy(
                src_ref=gcw_ref.at[4 * peer[0] + 2 * peer[1] + peer[2]], dst_ref=cslots.at[me],
                send_sem=csend_sems.at[k - 1], recv_sem=crecv_sems.at[k - 1], device_id=peer, device_id_type=MESH))
        for cp in cps:
            cp.start()
        stages = [stage for _, stage in carry.stages()] if carry else []
        if carry:
            carry.start(cins, couts, cscr)
        for stage in stages[:1]:
            stage(cins, couts, cscr)
        for cp in cps:
            cp.wait()
        tot = slots[0]
        ctot = cslots[0]
        for j in range(1, NDEV):
            tot = tot + slots[j]
            ctot = ctot + cslots[j]
        tot_ref[...] = tot
        ctot_ref[...] = ctot
        for stage in stages[1:]:
            stage(cins, couts, cscr)
        if carry:
            carry.finish(cins, couts, cscr)

    args = [*gs, loss_row, gcw]
    out_shape = [jax.ShapeDtypeStruct((SROWS, D), F32), jax.ShapeDtypeStruct(CWT, F32)]
    res = pl.pallas_call(
        body, in_specs=[VMEM] * len(args) + (carry.in_specs if carry else []),
        out_specs=[VMEM] * 2 + (carry.out_specs if carry else []),
        out_shape=out_shape + (carry.out_shape if carry else []),
        scratch_shapes=[pltpu.VMEM((SROWS, D), F32), pltpu.VMEM((NDEV, SROWS, D), F32),
                        pltpu.VMEM((NDEV,) + CWT, F32)]
                       + [pltpu.SemaphoreType.DMA((NDEV - 1,))] * 4 + (carry.scratch if carry else []),
        name="small_sums")(*args, *(carry.in_arrays if carry else []))
    return res[0], res[1], res[2:]


def _adamw_small(tot, ctot, ws, ms, vs, wcw, mcw, vcw):
    ns = len(SMALL_NAMES)
    widths = [w.shape[1] for w in ws]

    def body(*refs):
        it = iter(refs)
        take = lambda n: [next(it) for _ in range(n)]
        (tot_ref, ctot_ref), w_refs, m_refs, v_refs = take(2), take(ns), take(ns), take(ns)
        wcw_ref, mcw_ref, vcw_ref = take(3)
        outs = [take(4) for _ in range(ns)]
        cw_outs, (loss_out,) = take(4), take(1)

        def step(g, w_ref, m_ref, v_ref, o):
            d, mn, vn = _adamw_math(w_ref[...], g, m_ref[...], v_ref[...])
            o[0][...], o[1][...], o[2][...], o[3][...] = g, d, mn, vn

        for k, (r, o) in enumerate(SMALL_AT):
            step(tot_ref[r:r + 1, o:o + widths[k]], w_refs[k], m_refs[k], v_refs[k], outs[k])
        step(ctot_ref[0:CK, 0:HD][None], wcw_ref, mcw_ref, vcw_ref, cw_outs)
        loss_out[...] = tot_ref[LOSS_AT[0]:LOSS_AT[0] + 1, LOSS_AT[1]:LOSS_AT[1] + 128]

    args = [tot, ctot, *ws, *ms, *vs, wcw, mcw, vcw]
    out_shape = ([jax.ShapeDtypeStruct((1, n), F32) for n in widths for _ in range(4)]
                 + [jax.ShapeDtypeStruct((1, CK, HD), F32)] * 4 + [jax.ShapeDtypeStruct((1, 128), F32)])
    res = pl.pallas_call(
        body, in_specs=[VMEM] * len(args), out_specs=[VMEM] * len(out_shape), out_shape=out_shape,
        name="adamw_small")(*args)
    per = [res[4 * k:4 * k + 4] for k in range(ns)]
    return per, res[4 * ns:4 * ns + 4], res[-1]


def kernel(x, ffn1_norm, ffn1_w_gate, ffn1_w_up, ffn1_w_down, mix_norm, w_in, q_norm, k_norm, conv_w, conv_b, conv_ln_g, conv_ln_b, w_out, ffn2_norm, ffn2_w_gate, ffn2_w_up, ffn2_w_down, loss_target, m_ffn1_norm, m_ffn1_w_gate, m_ffn1_w_up, m_ffn1_w_down, m_mix_norm, m_w_in, m_q_norm, m_k_norm, m_conv_w, m_conv_b, m_conv_ln_g, m_conv_ln_b, m_w_out, m_ffn2_norm, m_ffn2_w_gate, m_ffn2_w_up, m_ffn2_w_down, v_ffn1_norm, v_ffn1_w_gate, v_ffn1_w_up, v_ffn1_w_down, v_mix_norm, v_w_in, v_q_norm, v_k_norm, v_conv_w, v_conv_b, v_conv_ln_g, v_conv_ln_b, v_w_out, v_ffn2_norm, v_ffn2_w_gate, v_ffn2_w_up, v_ffn2_w_down):
    P = dict(ffn1_norm=ffn1_norm, ffn1_w_gate=ffn1_w_gate, ffn1_w_up=ffn1_w_up, ffn1_w_down=ffn1_w_down,
             mix_norm=mix_norm, w_in=w_in, q_norm=q_norm, k_norm=k_norm, conv_w=conv_w, conv_b=conv_b,
             conv_ln_g=conv_ln_g, conv_ln_b=conv_ln_b, w_out=w_out, ffn2_norm=ffn2_norm,
             ffn2_w_gate=ffn2_w_gate, ffn2_w_up=ffn2_w_up, ffn2_w_down=ffn2_w_down)
    M = dict(ffn1_norm=m_ffn1_norm, ffn1_w_gate=m_ffn1_w_gate, ffn1_w_up=m_ffn1_w_up, ffn1_w_down=m_ffn1_w_down,
             mix_norm=m_mix_norm, w_in=m_w_in, q_norm=m_q_norm, k_norm=m_k_norm, conv_w=m_conv_w, conv_b=m_conv_b,
             conv_ln_g=m_conv_ln_g, conv_ln_b=m_conv_ln_b, w_out=m_w_out, ffn2_norm=m_ffn2_norm,
             ffn2_w_gate=m_ffn2_w_gate, ffn2_w_up=m_ffn2_w_up, ffn2_w_down=m_ffn2_w_down)
    V = dict(ffn1_norm=v_ffn1_norm, ffn1_w_gate=v_ffn1_w_gate, ffn1_w_up=v_ffn1_w_up, ffn1_w_down=v_ffn1_w_down,
             mix_norm=v_mix_norm, w_in=v_w_in, q_norm=v_q_norm, k_norm=v_k_norm, conv_w=v_conv_w, conv_b=v_conv_b,
             conv_ln_g=v_conv_ln_g, conv_ln_b=v_conv_ln_b, w_out=v_w_out, ffn2_norm=v_ffn2_norm,
             ffn2_w_gate=v_ffn2_w_gate, ffn2_w_up=v_ffn2_w_up, ffn2_w_down=v_ffn2_w_down)
    order = ["ffn1_norm", "ffn1_w_gate", "ffn1_w_up", "ffn1_w_down", "mix_norm", "w_in", "q_norm", "k_norm",
             "conv_w", "conv_b", "conv_ln_g", "conv_ln_b", "w_out", "ffn2_norm", "ffn2_w_gate", "ffn2_w_up",
             "ffn2_w_down"]
    B, S, _ = x.shape
    T = B * S

    bigs = [("wg1", "ffn1_w_gate", True), ("wu1", "ffn1_w_up", True), ("wd1", "ffn1_w_down", False),
            ("win", "w_in", True), ("wout", "w_out", False),
            ("wg2", "ffn2_w_gate", True), ("wu2", "ffn2_w_up", True), ("wd2", "ffn2_w_down", False)]
    hm = lambda a, tr: jnp.transpose(a[0]) if tr else a[0]
    cw_pad = jnp.zeros((32, 128), F32).at[0:CK, 0:HD].set(conv_w[0])
    shard = {ln: (hm(P[pn], tr), BF16) for ln, pn, tr in bigs}
    gathered = _run_carry(_GatherCarry([shard["wg1"], shard["wu1"], (cw_pad, F32)]), "gather_first")
    W = {"wg1": gathered[0], "wu1": gathered[1]}
    cwg = gathered[2].reshape(NDEV, 32, 128)[:, 0:CK, 0:HD]
    W["conv_w"] = jnp.transpose(cwg, (1, 0, 2)).reshape(CK, DC)
    norms = {n: P[n] for n in SMALL_NAMES}
    comm = _Comm({"down_in": ({n: shard[n] for n in ("wd1", "win")}, 0.5),
                  "ffn2": ({n: shard[n] for n in ("wg2", "wu2", "wd2", "wout")}, 0.5)},
                 opt={ln: (hm(P[pn], tr), hm(M[pn], tr), hm(V[pn], tr)) for ln, pn, tr in bigs})

    loss_part, gx, _, small = _local_step(x.reshape(T, D), loss_target.reshape(T, D), norms, W, B, S, comm)

    G, Dl, Mn, Vn = {}, {}, {}, {}
    dcw = small["conv_w"].reshape(CK, NDEV, HD).transpose(1, 0, 2)
    loss_row = jnp.zeros((1, 128), F32).at[0, 0].set(loss_part)
    gcw = jnp.pad(dcw, ((0, 0), (0, CWT[0] - CK), (0, CWT[1] - HD)))
    tot, ctot, got = _small_sums([small[n] for n in SMALL_NAMES], loss_row, gcw, carry=comm.last)
    comm.reduce_done(comm.last, got)
    per, cw_outs, loss_out = _adamw_small(
        tot, ctot, [P[n] for n in SMALL_NAMES], [M[n] for n in SMALL_NAMES], [V[n] for n in SMALL_NAMES],
        P["conv_w"], M["conv_w"], V["conv_w"])
    loss = loss_out[0, 0]
    for n, outs in zip(SMALL_NAMES, per):
        G[n], Dl[n], Mn[n], Vn[n] = outs
    G["conv_w"], Dl["conv_w"], Mn["conv_w"], Vn["conv_w"] = cw_outs

    group = ("wd1", "wg1", "wu1")
    w, m, v = zip(*[comm.opt[ln] for ln in group])
    comm.updated.update(zip(group, _adamw_big([comm.reduced[ln] for ln in group], w, m, v, "adamw_ffn1")))
    for ln, pn, tr in bigs:
        G[pn], Dl[pn], Mn[pn], Vn[pn] = [(jnp.transpose(o) if tr else o)[None] for o in comm.updated[ln]]

    return (loss, gx.reshape(B, S, D), *[G[n] for n in order], *[Dl[n] for n in order],
            *[Mn[n] for n in order], *[Vn[n] for n in order])
```

```python
import functools

import jax
import jax.numpy as jnp
from jax import lax
from jax.experimental import pallas as pl
from jax.experimental.pallas import tpu as pltpu

F32 = jnp.float32
BF16 = jnp.bfloat16

D = 1024
FF = 2816
HD = 64
DA = 512
DC = 512
DIN = 2560
CK = 31
BLK = 128
DILS = (1, 4, 16)
EPS = 1e-6
NDEV = 8
MESH = pl.DeviceIdType.MESH

LR, B1, B2, AEPS, WD, STEP = 0.001, 0.9, 0.999, 1e-08, 0.01, 10

NT = (((1,), (1,)), ((), ()))
TN = (((0,), (0,)), ((), ()))

VMEM_LIMIT = 60 * 1024 * 1024


def _cp(sem=None):
    return pltpu.CompilerParams(dimension_semantics=sem, vmem_limit_bytes=VMEM_LIMIT)


def _sigmoid(x):
    return 0.5 * (jnp.tanh(0.5 * x) + 1.0)


def _pallas(body, args, *, grid, in_specs, out_specs, out_shape, scratch_shapes, sem, name, carry=None):
    if carry is None:
        outs = pl.pallas_call(body, grid=grid, in_specs=in_specs, out_specs=out_specs, out_shape=out_shape,
                              scratch_shapes=scratch_shapes, compiler_params=_cp(sem), name=name)(*args)
        return outs, None
    n_in, n_out, n_scr = len(in_specs), len(out_shape), len(scratch_shapes)
    c_in, c_out = len(carry.in_arrays), len(carry.out_shape)

    def wrapped(*refs):
        ins, refs = refs[:n_in], refs[n_in:]
        cins, refs = refs[:c_in], refs[c_in:]
        outs, refs = refs[:n_out], refs[n_out:]
        couts, refs = refs[:c_out], refs[c_out:]
        scr, cscr = refs[:n_scr], refs[n_scr:]
        ids = [pl.program_id(a) for a in range(len(grid))]
        step = ids[0]
        for i, n in zip(ids[1:], grid[1:]):
            step = step * n + i
        steps = functools.reduce(lambda a, b: a * b, grid)

        @pl.when(step == 0)
        def _():
            carry.start(cins, couts, cscr)

        body(*ins, *outs, *scr)

        for frac, stage in carry.stages():
            @pl.when(step == int(steps * frac))
            def _(stage=stage):
                stage(cins, couts, cscr)

        @pl.when(step == steps - 1)
        def _():
            carry.finish(cins, couts, cscr)

    outs = pl.pallas_call(
        wrapped, grid=grid, in_specs=list(in_specs) + carry.in_specs, out_specs=list(out_specs) + carry.out_specs,
        out_shape=list(out_shape) + carry.out_shape, scratch_shapes=list(scratch_shapes) + carry.scratch,
        compiler_params=_cp(("arbitrary",) * len(grid)), name=name)(*args, *carry.in_arrays)
    return outs[:n_out], outs[n_out:]


FC = 256


def _resident(shape):
    return pl.BlockSpec(shape, lambda *_: (0,) * len(shape), pipeline_mode=pl.Buffered(1))


def _mix_out_ffn_loss(h1, attn, conv, wout, gain, wg, wu, wd, target, name):
    T = h1.shape[0]
    tm = 512
    nt = T // tm

    def body(h1_ref, at_ref, cv_ref, wo_ref, gain_ref, wg_ref, wu_ref, wd_ref, t_ref,
             h2_ref, n_ref, g_ref, u_ref, dout_ref, dyb_ref, sq_ref, a_hbm, a_scr, a_sem):
        t = pl.program_id(0)
        a_out = lambda i: pltpu.make_async_copy(a_scr, a_hbm.at[pl.ds(pl.multiple_of(i * tm, tm), tm), :], a_sem)

        @pl.when(t > 0)
        def _():
            a_out(t - 1).wait()

        xv = (h1_ref[...]
              + jnp.dot(at_ref[...], wo_ref[0:DA, :], preferred_element_type=F32)
              + jnp.dot(cv_ref[...], wo_ref[DA:D, :], preferred_element_type=F32))
        h2_ref[...] = xv
        r = lax.rsqrt(jnp.mean(xv * xv, axis=-1, keepdims=True) + EPS)
        n_ref[...] = (xv * r * gain_ref[...]).astype(BF16)
        for c in range(FF // FC):
            cols = slice(c * FC, (c + 1) * FC)
            nb = n_ref[...]
            g = lax.dot_general(nb, wg_ref[cols, :], NT, preferred_element_type=F32)
            u = lax.dot_general(nb, wu_ref[cols, :], NT, preferred_element_type=F32)
            g_ref[:, cols] = g.astype(BF16)
            u_ref[:, cols] = u.astype(BF16)
            a_scr[:, cols] = (g * _sigmoid(g) * u).astype(BF16)
        a_out(t).start()
        e = h2_ref[...] + 0.5 * jnp.dot(a_scr[...], wd_ref[...], preferred_element_type=F32) - t_ref[...]
        dout = e * (1.0 / D)
        dout_ref[...] = dout
        dyb_ref[...] = (0.5 * dout).astype(BF16)
        sq_ref[...] = jnp.sum(e * e, axis=0, keepdims=True)[None]

        @pl.when(t == nt - 1)
        def _():
            a_out(t).wait()

    row = pl.BlockSpec((tm, D), lambda t: (t, 0))
    half = pl.BlockSpec((tm, DA), lambda t: (t, 0))
    wide = pl.BlockSpec((tm, FF), lambda t: (t, 0))
    outs, _ = _pallas(
        body, (h1, attn, conv, wout, gain, wg, wu, wd, target), grid=(nt,),
        in_specs=[row, half, half, _resident((D, D)), _resident((1, D)), _resident((FF, D)), _resident((FF, D)),
                  _resident((FF, D)), row],
        out_specs=[row, row, wide, wide, row, row, pl.BlockSpec((1, 1, D), lambda t: (t, 0, 0)), HBM],
        out_shape=[jax.ShapeDtypeStruct((T, D), F32), jax.ShapeDtypeStruct((T, D), BF16)]
                  + [jax.ShapeDtypeStruct((T, FF), BF16)] * 2
                  + [jax.ShapeDtypeStruct((T, D), F32), jax.ShapeDtypeStruct((T, D), BF16),
                     jax.ShapeDtypeStruct((nt, 1, D), F32), jax.ShapeDtypeStruct((T, FF), BF16)],
        scratch_shapes=[pltpu.VMEM((tm, FF), BF16), pltpu.SemaphoreType.DMA(())],
        sem=("arbitrary",), name=name)
    return outs


def _ffn_gate_up(x, gain, wg, wu, name, carry=None):
    T = x.shape[0]
    tm = 512

    def body(x_ref, gain_ref, wg_ref, wu_ref, n_ref, g_ref, u_ref, a_ref):
        xv = x_ref[...]
        r = lax.rsqrt(jnp.mean(xv * xv, axis=-1, keepdims=True) + EPS)
        n_ref[...] = (xv * r * gain_ref[...]).astype(BF16)
        for c in range(FF // FC):
            cols = slice(c * FC, (c + 1) * FC)
            nb = n_ref[...]
            g = lax.dot_general(nb, wg_ref[cols, :], NT, preferred_element_type=F32)
            u = lax.dot_general(nb, wu_ref[cols, :], NT, preferred_element_type=F32)
            g_ref[:, cols] = g.astype(BF16)
            u_ref[:, cols] = u.astype(BF16)
            a_ref[:, cols] = (g * _sigmoid(g) * u).astype(BF16)

    row = pl.BlockSpec((tm, D), lambda t: (t, 0))
    wide = pl.BlockSpec((tm, FF), lambda t: (t, 0))
    return _pallas(
        body, (x, gain, wg, wu), grid=(T // tm,),
        in_specs=[row, _resident((1, D)), _resident((FF, D)), _resident((FF, D))],
        out_specs=[row, wide, wide, wide],
        out_shape=[jax.ShapeDtypeStruct((T, D), BF16)] + [jax.ShapeDtypeStruct((T, FF), BF16)] * 3,
        scratch_shapes=[], sem=("parallel",), name=name, carry=carry)


def _ffn_down_mix_in(x, a, wd, gain, win, name):
    T = x.shape[0]
    tm = 512

    def body(x_ref, a_ref, wd_ref, gain_ref, win_ref, h_ref, u_ref, n_ref):
        hv = x_ref[...] + 0.5 * jnp.dot(a_ref[...], wd_ref[...], preferred_element_type=F32)
        h_ref[...] = hv
        r = lax.rsqrt(jnp.mean(hv * hv, axis=-1, keepdims=True) + EPS)
        n_ref[...] = (hv * r * gain_ref[...]).astype(BF16)
        u_ref[...] = lax.dot_general(n_ref[...], win_ref[...], NT, preferred_element_type=F32).astype(BF16)

    row = pl.BlockSpec((tm, D), lambda t: (t, 0))
    wide = pl.BlockSpec((tm, FF), lambda t: (t, 0))
    outs, _ = _pallas(
        body, (x, a, wd, gain, win), grid=(T // tm,),
        in_specs=[row, wide, _resident((FF, D)), _resident((1, D)), _resident((DIN, D))],
        out_specs=[row, pl.BlockSpec((tm, DIN), lambda t: (t, 0)), row],
        out_shape=[jax.ShapeDtypeStruct((T, D), F32), jax.ShapeDtypeStruct((T, DIN), BF16),
                   jax.ShapeDtypeStruct((T, D), BF16)],
        scratch_shapes=[], sem=("parallel",), name=name)
    return outs


def _ffn_bwd_act(dyb, g, u, x, dout, gain, wg, wu, wd, name, carry=None):
    T = x.shape[0]
    tm = 256
    nt = T // tm

    def body(dy_ref, g_ref, u_ref, x_ref, dout_ref, gain_ref, wg_ref, wu_ref, wd_ref,
             dg_ref, du_ref, dx_ref, dgn_ref):
        for c in range(FF // FC):
            cols = slice(c * FC, (c + 1) * FC)
            da = lax.dot_general(dy_ref[...], wd_ref[cols, :], NT, preferred_element_type=F32)
            gv = g_ref[:, cols].astype(F32)
            uv = u_ref[:, cols].astype(F32)
            sg = _sigmoid(gv)
            dg_ref[:, cols] = (da * uv * (sg * (1.0 + gv * (1.0 - sg)))).astype(BF16)
            du_ref[:, cols] = (da * (gv * sg)).astype(BF16)
        dn = (jnp.dot(dg_ref[...], wg_ref[...], preferred_element_type=F32)
              + jnp.dot(du_ref[...], wu_ref[...], preferred_element_type=F32))
        dx, dgain = _rms_bwd_rows(dn, x_ref[...], gain_ref[...])
        dx_ref[...] = dout_ref[...] + dx

        @pl.when(pl.program_id(0) == 0)
        def _():
            dgn_ref[...] = jnp.zeros_like(dgn_ref)

        dgn_ref[...] += dgain[None]

    row = pl.BlockSpec((tm, D), lambda t: (t, 0))
    wide = pl.BlockSpec((tm, FF), lambda t: (t, 0))
    return _pallas(
        body, (dyb, g, u, x, dout, gain, wg, wu, wd), grid=(nt,),
        in_specs=[row, wide, wide, row, row, _resident((1, D)), _resident((FF, D)), _resident((FF, D)),
                  _resident((FF, D))],
        out_specs=[wide, wide, row, pl.BlockSpec((1, 1, D), lambda t: (0, 0, 0))],
        out_shape=[jax.ShapeDtypeStruct((T, FF), BF16)] * 2
                  + [jax.ShapeDtypeStruct((T, D), F32), jax.ShapeDtypeStruct((1, 1, D), F32)],
        scratch_shapes=[], sem=("arbitrary",), name=name, carry=carry)


def _ffn_bwd_w(lhs, rhs, name, carry=None):
    T = rhs.shape[0]
    tf = 256

    def body(l_ref, r_ref, dw_ref):
        dw_ref[...] = lax.dot_general(l_ref[...], r_ref[...], TN, preferred_element_type=F32).astype(BF16)

    (dw,), got = _pallas(
        body, (lhs, rhs), grid=(FF // tf,),
        in_specs=[pl.BlockSpec((T, tf), lambda f: (0, f)), _resident((T, D))],
        out_specs=[pl.BlockSpec((tf, D), lambda f: (f, 0))], out_shape=[jax.ShapeDtypeStruct((FF, D), BF16)],
        scratch_shapes=[], sem=("parallel",), name=name, carry=carry)
    return dw, got


def _rms_bwd_rows(dn, xv, gain):
    r = lax.rsqrt(jnp.mean(xv * xv, axis=-1, keepdims=True) + EPS)
    xhat = xv * r
    dxhat = dn * gain
    dx = r * (dxhat - xhat * jnp.mean(dxhat * xhat, axis=-1, keepdims=True))
    return dx, jnp.sum(dn * xhat, axis=0, keepdims=True)


def _mix_out_bwd(dh, attn, conv, wout):
    T = dh.shape[0]
    tm = 512
    nt = T // tm

    def body(dh_ref, a_ref, c_ref, w_ref, da_ref, dc_ref, dw_ref, acc_scr):
        t = pl.program_id(0)

        @pl.when(t == 0)
        def _():
            acc_scr[...] = jnp.zeros_like(acc_scr)

        dhb = dh_ref[...].astype(BF16)
        dmix = lax.dot_general(dhb, w_ref[...], NT, preferred_element_type=F32)
        da_ref[...] = dmix[:, 0:DA].astype(BF16)
        dc_ref[...] = dmix[:, DA:D].astype(BF16)
        acc_scr[0:DA, :] += lax.dot_general(a_ref[...], dhb, TN, preferred_element_type=F32)
        acc_scr[DA:D, :] += lax.dot_general(c_ref[...], dhb, TN, preferred_element_type=F32)

        @pl.when(t == nt - 1)
        def _():
            dw_ref[...] = acc_scr[...].astype(BF16)

    row = pl.BlockSpec((tm, D), lambda t: (t, 0))
    half = pl.BlockSpec((tm, DA), lambda t: (t, 0))
    full = pl.BlockSpec((D, D), lambda t: (0, 0))
    return pl.pallas_call(
        body, grid=(nt,), in_specs=[row, half, half, full], out_specs=[half, half, full],
        out_shape=[jax.ShapeDtypeStruct((T, DA), BF16)] * 2 + [jax.ShapeDtypeStruct((D, D), BF16)],
        scratch_shapes=[pltpu.VMEM((D, D), F32)],
        compiler_params=_cp(("arbitrary",)), name="mix_out_bwd")(dh, attn, conv, wout)


def _mix_in_bwd(dparts, win, nb, h, dh, gain):
    T = h.shape[0]
    tm = 512
    nt = T // tm

    def body(d0, d1, d2, d3, d4, w_ref, n_ref, h_ref, dh_ref, gain_ref,
             dw_ref, dx_ref, dyb_ref, dg_ref, acc_scr):
        t = pl.program_id(0)

        @pl.when(t == 0)
        def _():
            acc_scr[...] = jnp.zeros_like(acc_scr)

        n = n_ref[...]
        dn = jnp.zeros((tm, D), F32)
        for i, d_ref in enumerate((d0, d1, d2, d3, d4)):
            dv = d_ref[...]
            dn = dn + jnp.dot(dv, w_ref[i * DA:(i + 1) * DA, :], preferred_element_type=F32)
            acc_scr[i * DA:(i + 1) * DA, :] += lax.dot_general(dv, n, TN, preferred_element_type=F32)
        dx, dgain = _rms_bwd_rows(dn, h_ref[...], gain_ref[...])
        tot = dh_ref[...] + dx
        dx_ref[...] = tot
        dyb_ref[...] = (0.5 * tot).astype(BF16)
        dg_ref[...] = dgain[None]

        @pl.when(t == nt - 1)
        def _():
            dw_ref[...] = acc_scr[...].astype(BF16)

    row = pl.BlockSpec((tm, D), lambda t: (t, 0))
    half = pl.BlockSpec((tm, DA), lambda t: (t, 0))
    full = pl.BlockSpec((DIN, D), lambda t: (0, 0))
    return pl.pallas_call(
        body, grid=(nt,),
        in_specs=[half] * 5 + [full, row, row, row, pl.BlockSpec((1, D), lambda t: (0, 0))],
        out_specs=[full, row, row, pl.BlockSpec((1, 1, D), lambda t: (t, 0, 0))],
        out_shape=[jax.ShapeDtypeStruct((DIN, D), BF16), jax.ShapeDtypeStruct((T, D), F32),
                   jax.ShapeDtypeStruct((T, D), BF16), jax.ShapeDtypeStruct((nt, 1, D), F32)],
        scratch_shapes=[pltpu.VMEM((DIN, D), F32)],
        compiler_params=_cp(("arbitrary",)), name="mix_in_bwd")(*dparts, win, nb, h, dh, gain)


def _head_masks():
    lane = lax.broadcasted_iota(jnp.int32, (1, 2 * HD), 1)
    m0 = lane < HD
    return m0, jnp.logical_not(m0)


def _stack_heads(v, m0, m1):
    z = jnp.zeros_like(v)
    return jnp.concatenate([jnp.where(m0, v, z), jnp.where(m1, v, z)], axis=0)


def _unstack_heads(v2, m0):
    return jnp.where(m0, v2[0:BLK], v2[BLK:2 * BLK])


def _head_sums(xv):
    ri = lax.broadcasted_iota(jnp.int32, (2 * HD, 2 * HD), 0)
    ci = lax.broadcasted_iota(jnp.int32, (2 * HD, 2 * HD), 1)
    ones = jnp.where((ri < HD) == (ci < HD), 1.0, 0.0).astype(BF16)
    hi = xv.astype(BF16)
    lo = (xv - hi.astype(F32)).astype(BF16)
    return (jnp.dot(hi, ones, preferred_element_type=F32) + jnp.dot(lo, ones, preferred_element_type=F32))


def _head_rms(xv):
    return lax.rsqrt(_head_sums(xv * xv) * (1.0 / HD) + EPS)


def _band_mask(first):
    qi = lax.broadcasted_iota(jnp.int32, (BLK, 2 * BLK), 0)
    ci = lax.broadcasted_iota(jnp.int32, (BLK, 2 * BLK), 1)
    band = (ci >= qi) & (ci <= qi + BLK)
    return band & ((ci >= BLK) | jnp.logical_not(first))


def _block_rows(j, d, seg):
    r, n = j // seg, j % seg
    start = r + (d * BLK) * n
    first = n == 0
    prev = jnp.where(first, start, start - d * BLK)
    return pl.ds(start, BLK, stride=d), pl.ds(prev, BLK, stride=d), first


def _block_keys(refs, cur, prev, first, single):
    if single:
        qi = lax.broadcasted_iota(jnp.int32, (BLK, BLK), 0)
        ci = lax.broadcasted_iota(jnp.int32, (BLK, BLK), 1)
        return [r[cur, :].astype(BF16) for r in refs], ci <= qi
    return ([jnp.concatenate([r[prev, :], r[cur, :]], axis=0).astype(BF16) for r in refs], _band_mask(first))


def _attn_fwd(u, qg2, kg2, B, S, carry=None):
    T = B * S
    NB = S // BLK
    scale = HD ** -0.5

    def body(q_ref, k_ref, v_ref, qg_ref, kg_ref, o_ref, lse_ref, qn, kn, vn, os_, ls_):
        m0, m1 = _head_masks()
        qv = q_ref[...].astype(F32)
        qn[...] = qv * _head_rms(qv) * (qg_ref[...] * scale)
        kv = k_ref[...].astype(F32)
        kn[...] = kv * _head_rms(kv) * kg_ref[...]
        vn[...] = v_ref[...].astype(F32)

        for i, d in enumerate(DILS):
            seg = NB // d

            def blk(j, c, i=i, d=d, seg=seg):
                cur, prev, first = _block_rows(j, d, seg)
                q2 = _stack_heads(qn[cur, :].astype(BF16), m0, m1)
                (kk, vv), mask = _block_keys((kn, vn), cur, prev, first, False)
                s = lax.dot_general(q2, kk, NT, preferred_element_type=F32)
                s = jnp.where(jnp.concatenate([mask, mask], axis=0), s, -1e30)
                mx = jnp.max(s, axis=-1, keepdims=True)
                p = jnp.exp(s - mx)
                l = jnp.sum(p, axis=-1, keepdims=True)
                o2 = jnp.dot((p * (1.0 / l)).astype(BF16), vv, preferred_element_type=F32)
                os_[i, cur, :] = _unstack_heads(o2, m0)
                ls_[i, cur, :] = _unstack_heads(mx + jnp.log(l), m0)
                return c

            lax.fori_loop(0, NB, blk, 0, unroll=8)

        def comb(c, carry):
            rows = pl.ds(pl.multiple_of(c * 256, 256), 256)
            l0, l1, l2 = ls_[0, rows, :], ls_[1, rows, :], ls_[2, rows, :]
            mx = jnp.maximum(jnp.maximum(l0, l1), l2)
            e0, e1, e2 = jnp.exp(l0 - mx), jnp.exp(l1 - mx), jnp.exp(l2 - mx)
            tot = e0 + e1 + e2
            inv = 1.0 / tot
            o = (e0 * os_[0, rows, :] + e1 * os_[1, rows, :] + e2 * os_[2, rows, :]) * inv
            o_ref[rows, :] = o.astype(BF16)
            lse_ref[rows, :] = mx + jnp.log(tot)
            return carry

        lax.fori_loop(0, S // 256, comb, 0)

    pair = 2 * HD
    blk_spec = lambda off: pl.BlockSpec((S, pair), lambda b, p, off=off: (b, off + p))
    gspec = pl.BlockSpec((1, pair), lambda b, p: (0, 0))
    return _pallas(
        body, (u, u, u, qg2, kg2), grid=(B, DA // pair),
        in_specs=[blk_spec(0), blk_spec(DA // pair), blk_spec(2 * DA // pair), gspec, gspec],
        out_specs=[blk_spec(0), blk_spec(0)],
        out_shape=[jax.ShapeDtypeStruct((T, DA), BF16), jax.ShapeDtypeStruct((T, DA), F32)],
        scratch_shapes=[pltpu.VMEM((S, pair), F32)] * 3 + [pltpu.VMEM((3, S, pair), F32)] * 2,
        sem=("parallel", "parallel"), name="attn_fwd", carry=carry)


def _attn_bwd(u, attn, dattn, lse, qg2, kg2, B, S, carry=None):
    T = B * S
    NB = S // BLK
    scale = HD ** -0.5
    pair = 2 * HD

    def body(q_ref, k_ref, v_ref, o_ref, do_ref, lse_ref, qg_ref, kg_ref,
             dq_ref, dk_ref, dv_ref, dgn_ref,
             qn, kn, vn, don, ldl, accq, acck, accv, rq, rk):
        m0, m1 = _head_masks()
        lane = lax.broadcasted_iota(jnp.int32, (1, pair), 1)
        qv = q_ref[...].astype(F32)
        rq[...] = _head_rms(qv)
        qn[...] = qv * rq[...] * (qg_ref[...] * scale)
        kv = k_ref[...].astype(F32)
        rk[...] = _head_rms(kv)
        kn[...] = kv * rk[...] * kg_ref[...]
        vn[...] = v_ref[...].astype(F32)
        dov = do_ref[...].astype(F32)
        don[...] = dov
        ldl[...] = jnp.where((lane % HD) < HD // 2, lse_ref[...], _head_sums(dov * o_ref[...].astype(F32)))

        for i, d in enumerate(DILS):
            seg = NB // d

            def blk(j, c, i=i, d=d, seg=seg):
                cur, prev, first = _block_rows(j, d, seg)
                q2 = _stack_heads(qn[cur, :].astype(BF16), m0, m1)
                do2 = _stack_heads(don[cur, :].astype(BF16), m0, m1)
                (kk, vv), mask = _block_keys((kn, vn), cur, prev, first, seg == 1)
                ldv = ldl[cur, :]
                lse2 = jnp.concatenate([ldv[:, 0:1], ldv[:, HD:HD + 1]], axis=0)
                dl2 = jnp.concatenate([ldv[:, HD // 2:HD // 2 + 1], ldv[:, HD + HD // 2:HD + HD // 2 + 1]], axis=0)
                s = lax.dot_general(q2, kk, NT, preferred_element_type=F32)
                p = jnp.where(jnp.concatenate([mask, mask], axis=0), jnp.exp(s - lse2), 0.0)
                dp = lax.dot_general(do2, vv, NT, preferred_element_type=F32)
                ds = (p * (dp - dl2)).astype(BF16)
                dq_acc = _unstack_heads(jnp.dot(ds, kk, preferred_element_type=F32), m0)
                dk_acc = lax.dot_general(ds, q2, TN, preferred_element_type=F32)
                dv_acc = lax.dot_general(p.astype(BF16), do2, TN, preferred_element_type=F32)
                if i == 0:
                    accq[cur, :] = dq_acc
                    acck[cur, :] = dk_acc[BLK:2 * BLK]
                    accv[cur, :] = dv_acc[BLK:2 * BLK]
                    acck[prev, :] += dk_acc[0:BLK]
                    accv[prev, :] += dv_acc[0:BLK]
                elif seg == 1:
                    accq[cur, :] += dq_acc
                    acck[cur, :] += dk_acc
                    accv[cur, :] += dv_acc
                else:
                    accq[cur, :] += dq_acc
                    acck[prev, :] += dk_acc[0:BLK]
                    acck[cur, :] += dk_acc[BLK:2 * BLK]
                    accv[prev, :] += dv_acc[0:BLK]
                    accv[cur, :] += dv_acc[BLK:2 * BLK]
                return c

            lax.fori_loop(0, NB, blk, 0, unroll=8)

        def norm_bwd(x_ref, r_ref, dn, gain):
            r = r_ref[...]
            xhat = x_ref[...].astype(F32) * r
            dxhat = dn * gain
            dx = r * (dxhat - xhat * (_head_sums(dxhat * xhat) * (1.0 / HD)))
            return dx, jnp.sum(dn * xhat, axis=0, keepdims=True)

        dq, dgq = norm_bwd(q_ref, rq, accq[...], qg_ref[...] * scale)
        dk, dgk = norm_bwd(k_ref, rk, acck[...], kg_ref[...])
        dq_ref[...] = dq.astype(BF16)
        dk_ref[...] = dk.astype(BF16)
        dv_ref[...] = accv[...].astype(BF16)
        dgn_ref[...] = jnp.concatenate([dgq * scale, dgk, jnp.zeros((6, pair), F32)], axis=0)[None]

    blk_spec = lambda off: pl.BlockSpec((S, pair), lambda b, p, off=off: (b, off + p))
    gspec = pl.BlockSpec((1, pair), lambda b, p: (0, 0))
    np_ = DA // pair
    return _pallas(
        body, (u, u, u, attn, dattn, lse, qg2, kg2), grid=(B, np_),
        in_specs=[blk_spec(0), blk_spec(np_), blk_spec(2 * np_), blk_spec(0), blk_spec(0), blk_spec(0),
                  gspec, gspec],
        out_specs=[blk_spec(0), blk_spec(0), blk_spec(0),
                   pl.BlockSpec((1, 8, pair), lambda b, p: (b * np_ + p, 0, 0))],
        out_shape=[jax.ShapeDtypeStruct((T, DA), BF16)] * 3 + [jax.ShapeDtypeStruct((B * np_, 8, pair), F32)],
        scratch_shapes=[pltpu.VMEM((S, pair), F32)] * 10,
        sem=("parallel", "parallel"), name="attn_bwd", carry=carry)


CT = 32
CPAD = 32


def _shifted(win, offsets):
    rolled, out = {}, {}
    n = win.shape[0]
    for o in offsets:
        sub = o % 8
        if sub not in rolled:
            rolled[sub] = win if sub == 0 else pltpu.roll(win, n - sub, 0)
        out[o] = rolled[sub][o - sub:o - sub + CT, :]
    return out


def _ln_fwd(y, g, b):
    mu = jnp.mean(y, axis=-1, keepdims=True)
    yc = y - mu
    rstd = lax.rsqrt(jnp.mean(yc * yc, axis=-1, keepdims=True) + EPS)
    xhat = yc * rstd
    return xhat, rstd, xhat * g + b


def _fill_glu(ca_ref, cg_ref, glu, S):
    glu[pl.ds(0, CPAD), :] = jnp.zeros((CPAD, DC), F32)

    def fill(i, c):
        rows = pl.ds(pl.multiple_of(i * 256, 256), 256)
        a = ca_ref[rows, :].astype(F32)
        gt = cg_ref[rows, :].astype(F32)
        glu[pl.ds(pl.multiple_of(CPAD + i * 256, CT), 256), :] = a * _sigmoid(gt)
        return c

    lax.fori_loop(0, S // 256, fill, 0)


def _conv_fwd(u, cw, cb, lg, lb, B, S):
    T = B * S

    def body(ca_ref, cg_ref, w_ref, b_ref, lg_ref, lb_ref, o_ref, y_ref, glu):
        _fill_glu(ca_ref, cg_ref, glu, S)

        def step(i, c):
            t0 = pl.multiple_of(i * CT, CT)
            win = glu[pl.ds(t0, 2 * CT), :]
            acc = jnp.zeros((CT, DC), F32) + b_ref[...]
            taps = _shifted(win, [k + 2 for k in range(CK)])
            for k in range(CK):
                acc = acc + taps[k + 2] * w_ref[k:k + 1, :]
            y_ref[pl.ds(t0, CT), :] = acc
            _, _, z = _ln_fwd(acc, lg_ref[...], lb_ref[...])
            o_ref[pl.ds(t0, CT), :] = (z * _sigmoid(z)).astype(BF16)
            return c

        lax.fori_loop(0, S // CT, step, 0, unroll=4)

    vec = pl.BlockSpec((1, DC), lambda b: (0, 0))
    return pl.pallas_call(
        body, grid=(B,),
        in_specs=[pl.BlockSpec((S, DC), lambda b: (b, 3)), pl.BlockSpec((S, DC), lambda b: (b, 4)),
                  pl.BlockSpec((CT, DC), lambda b: (0, 0)), vec, vec, vec],
        out_specs=[pl.BlockSpec((S, DC), lambda b: (b, 0))] * 2,
        out_shape=[jax.ShapeDtypeStruct((T, DC), BF16), jax.ShapeDtypeStruct((T, DC), F32)],
        scratch_shapes=[pltpu.VMEM((CPAD + S, DC), F32)],
        compiler_params=_cp(("parallel",)), name="conv_fwd")(u, u, cw, cb, lg, lb)


def _conv_bwd(u, y, dconv, cw, lg, lb, B, S):
    T = B * S

    def body(ca_ref, cg_ref, y_ref, dc_ref, w_ref, lg_ref, lb_ref,
             dca_ref, dcg_ref, dw_ref, ds_ref, glu, dyp, dwacc):
        _fill_glu(ca_ref, cg_ref, glu, S)
        dyp[pl.ds(S, CPAD), :] = jnp.zeros((CPAD, DC), F32)
        lgv, lbv = lg_ref[...], lb_ref[...]

        def sum8(v):
            return functools.reduce(jnp.add, [v[r:r + 8] for r in range(0, v.shape[0], 8)])

        P1 = 4 * CT

        def p1(i, carry):
            sb, sg, sl = carry
            t0 = pl.multiple_of(i * P1, P1)
            xhat, rstd, z = _ln_fwd(y_ref[pl.ds(t0, P1), :], lgv, lbv)
            sz = _sigmoid(z)
            dz = dc_ref[pl.ds(t0, P1), :].astype(F32) * (sz * (1.0 + z * (1.0 - sz)))
            dxhat = dz * lgv
            dy = rstd * (dxhat - jnp.mean(dxhat, axis=-1, keepdims=True)
                         - xhat * jnp.mean(dxhat * xhat, axis=-1, keepdims=True))
            dyp[pl.ds(t0, P1), :] = dy
            return sb + sum8(dy), sg + sum8(dz * xhat), sl + sum8(dz)

        z8 = jnp.zeros((8, DC), F32)
        sb, sg, sl = lax.fori_loop(0, S // P1, p1, (z8, z8, z8))
        rs = lambda v: jnp.sum(v, axis=0, keepdims=True)
        ds_ref[...] = jnp.concatenate([rs(sb), rs(sg), rs(sl), jnp.zeros((5, DC), F32)], axis=0)[None]

        def p2(i, c):
            t0 = pl.multiple_of(i * CT, CT)
            win = dyp[pl.ds(t0, 2 * CT), :]
            acc = jnp.zeros((CT, DC), F32)
            taps = _shifted(win, [30 - k for k in range(CK)])
            for k in range(CK):
                acc = acc + taps[30 - k] * w_ref[k:k + 1, :]
            a = ca_ref[pl.ds(t0, CT), :].astype(F32)
            sgt = _sigmoid(cg_ref[pl.ds(t0, CT), :].astype(F32))
            dca_ref[pl.ds(t0, CT), :] = (acc * sgt).astype(BF16)
            dcg_ref[pl.ds(t0, CT), :] = (acc * a * sgt * (1.0 - sgt)).astype(BF16)
            return c

        lax.fori_loop(0, S // CT, p2, 0)

        dwacc[...] = jnp.zeros_like(dwacc)

        def p3(i, c):
            t0 = pl.multiple_of(i * CT, CT)
            win = glu[pl.ds(t0, 2 * CT), :]
            dy = dyp[pl.ds(t0, CT), :]
            for k in range(CK):
                dwacc[k] += sum8(dy * win[k + 2:k + 2 + CT, :])
            return c

        lax.fori_loop(0, S // CT, p3, 0)
        dw_ref[...] = jnp.sum(dwacc[...], axis=1)[None]

    vec = pl.BlockSpec((1, DC), lambda b: (0, 0))
    seq = pl.BlockSpec((S, DC), lambda b: (b, 0))
    return pl.pallas_call(
        body, grid=(B,),
        in_specs=[pl.BlockSpec((S, DC), lambda b: (b, 3)), pl.BlockSpec((S, DC), lambda b: (b, 4)),
                  seq, seq, pl.BlockSpec((CT, DC), lambda b: (0, 0)), vec, vec],
        out_specs=[seq, seq, pl.BlockSpec((1, CT, DC), lambda b: (b, 0, 0)),
                   pl.BlockSpec((1, 8, DC), lambda b: (b, 0, 0))],
        out_shape=[jax.ShapeDtypeStruct((T, DC), BF16)] * 2
                  + [jax.ShapeDtypeStruct((B, CT, DC), F32), jax.ShapeDtypeStruct((B, 8, DC), F32)],
        scratch_shapes=[pltpu.VMEM((CPAD + S, DC), F32), pltpu.VMEM((S + CPAD, DC), F32),
                        pltpu.VMEM((CT, 8, DC), F32)],
        compiler_params=_cp(("parallel",)), name="conv_bwd")(u, u, y, dconv, cw, lg, lb)


def _local_step(x, target, norms, W, B, S, comm=None):
    qg2 = jnp.concatenate([norms["q_norm"], norms["q_norm"]], axis=1)
    kg2 = jnp.concatenate([norms["k_norm"], norms["k_norm"]], axis=1)
    cw = jnp.concatenate([W["conv_w"], jnp.zeros((1, DC), F32)], axis=0)

    W = dict(W)
    (n1, g1, u1, act1), got = _ffn_gate_up(x, norms["ffn1_norm"], W["wg1"], W["wu1"], "ffn1_gate_up",
                                           carry=comm.gathers["down_in"] if comm else None)
    if comm:
        W.update(comm.gathered("down_in", got))
    h1, u, n2 = _ffn_down_mix_in(x, act1, W["wd1"], norms["mix_norm"], W["win"], "ffn1_down_mix_in")
    (attn, lse), got = _attn_fwd(u, qg2, kg2, B, S, carry=comm.gathers["ffn2"] if comm else None)
    if comm:
        W = dict(W, **comm.gathered("ffn2", got))
    conv, y = _conv_fwd(u, cw, norms["conv_b"], norms["conv_ln_g"], norms["conv_ln_b"], B, S)
    h2, n3, g2, u2, dout, dyb, sq, act2 = _mix_out_ffn_loss(h1, attn, conv, W["wout"], norms["ffn2_norm"],
                                                            W["wg2"], W["wu2"], W["wd2"], target, "ffn2_fwd")
    loss = (0.5 / D) * jnp.sum(sq)

    (dg2, du2, dh2, dgn_ffn2), _ = _ffn_bwd_act(dyb, g2, u2, h2, dout, norms["ffn2_norm"],
                                               W["wg2"], W["wu2"], W["wd2"], "ffn2_bwd_act")
    dwd2, _ = _ffn_bwd_w(act2, dyb, "ffn2_bwd_wd")
    dwg2, _ = _ffn_bwd_w(dg2, n3, "ffn2_bwd_wg")
    dwu2, _ = _ffn_bwd_w(du2, n3, "ffn2_bwd_wu")
    dattn, dconv, dwout = _mix_out_bwd(dh2, attn, conv, W["wout"])
    carry = comm.reduce_start({"wg2": dwg2, "wu2": dwu2, "wd2": dwd2, "wout": dwout}) if comm else None
    (dq, dk, dv, dgn_qk), got = _attn_bwd(u, attn, dattn, lse, qg2, kg2, B, S, carry=carry)
    if comm:
        comm.reduce_done(carry, got)
    dca, dcg, dcw, dcs = _conv_bwd(u, y, dconv, cw, norms["conv_ln_g"], norms["conv_ln_b"], B, S)
    dwin, dh1, dyb1, dgn_mix = _mix_in_bwd((dq, dk, dv, dca, dcg), W["win"], n2, h1, dh2, norms["mix_norm"])
    carry = comm.reduce_start({"win": dwin}) if comm else None
    dwd1, got = _ffn_bwd_w(act1, dyb1, "ffn1_bwd_wd", carry=carry)
    if comm:
        comm.reduce_done(carry, got)
    carry = comm.update_start(("wg2", "wu2", "wd2", "wout", "win")) if comm else None
    (dg1, du1, gx, dgn_ffn1), got = _ffn_bwd_act(dyb1, g1, u1, x, dh1, norms["ffn1_norm"],
                                                W["wg1"], W["wu1"], W["wd1"], "ffn1_bwd_act", carry=carry)
    if comm:
        comm.update_done(carry, got)
        carry = comm.reduce_start({"wd1": dwd1})
    dwg1, got = _ffn_bwd_w(dg1, n1, "ffn1_bwd_wg", carry=carry)
    if comm:
        comm.reduce_done(carry, got)
        carry = comm.reduce_start({"wg1": dwg1})
    dwu1, got = _ffn_bwd_w(du1, n1, "ffn1_bwd_wu", carry=carry)
    if comm:
        comm.reduce_done(carry, got)
        comm.last = comm.reduce_start({"wu1": dwu1})

    qk = jnp.sum(dgn_qk, axis=0)
    cs = jnp.sum(dcs, axis=0)
    small = {
        "ffn1_norm": jnp.sum(dgn_ffn1, axis=0),
        "mix_norm": jnp.sum(dgn_mix, axis=0),
        "q_norm": qk[0:1, 0:HD] + qk[0:1, HD:2 * HD],
        "k_norm": qk[1:2, 0:HD] + qk[1:2, HD:2 * HD],
        "conv_w": jnp.sum(dcw, axis=0)[0:CK],
        "conv_b": cs[0:1],
        "conv_ln_g": cs[1:2],
        "conv_ln_b": cs[2:3],
        "ffn2_norm": jnp.sum(dgn_ffn2, axis=0),
    }
    big = {"wg1": dwg1, "wu1": dwu1, "wd1": dwd1, "win": dwin, "wout": dwout,
           "wg2": dwg2, "wu2": dwu2, "wd2": dwd2}
    return loss, gx, big, small


HBM = pl.BlockSpec(memory_space=pltpu.HBM)
VMEM = pl.BlockSpec(memory_space=pltpu.VMEM)


def _place():
    return lax.axis_index("x"), lax.axis_index("y"), lax.axis_index("c")


class _GatherCarry:
    def __init__(self, shards, mid_at=0.5):
        nt = len(shards)
        self.mid_at = mid_at
        self.shards = shards
        self.in_arrays = [s for s, _ in shards]
        self.in_specs = [VMEM] * nt
        self.out_shape = [jax.ShapeDtypeStruct((NDEV * s.shape[0], s.shape[1]), dt) for s, dt in shards]
        self.out_specs = [HBM] * nt
        self.scratch = ([pltpu.VMEM(s.shape, dt) for s, dt in shards]
                        + [pltpu.SemaphoreType.DMA((nt, 7)), pltpu.SemaphoreType.DMA((nt, 7)),
                           pltpu.SemaphoreType.DMA((nt,))])

    def _copies(self, outs, scr):
        nt = len(self.shards)
        stages = scr[:nt]
        send_sems, recv_sems, local_sems = scr[nt:]
        x, y, c = _place()
        me, sibling = (x, y, c), (x, y, 1 - c)
        xn, yn, diag = (1 - x, y, c), (x, 1 - y, c), (1 - x, 1 - y, c)
        via = (x ^ c, y ^ (1 - c), c)
        onto = (x ^ (1 - c), y ^ c, c)

        def rows(t, px, py, pc):
            r = self.shards[t][0].shape[0]
            return outs[t].at[pl.ds((4 * px + 2 * py + pc) * r, r), :]

        def copy(t, k, block, to, src=None):
            return pltpu.make_async_remote_copy(
                src_ref=rows(t, *block) if src is None else src, dst_ref=rows(t, *block),
                send_sem=send_sems.at[t, k], recv_sem=recv_sems.at[t, k],
                device_id=to, device_id_type=MESH)

        sib = lambda b: (b[0], b[1], 1 - c)
        return dict(
            local=[pltpu.make_async_copy(stages[t], rows(t, *me), local_sems.at[t]) for t in range(nt)],
            own=[[copy(t, 0, me, sibling, src=stages[t]), copy(t, 1, me, xn, src=stages[t]),
                  copy(t, 2, me, yn, src=stages[t])] for t in range(nt)],
            relay=[copy(t, 3, via, onto) for t in range(nt)],
            down=[[copy(t, 4, xn, sibling), copy(t, 5, yn, sibling)] for t in range(nt)],
            down_diag=[copy(t, 6, diag, sibling) for t in range(nt)],
            got_xy=[[copy(t, 1, xn, me), copy(t, 2, yn, me)] for t in range(nt)],
            got_diag=[copy(t, 3, diag, me) for t in range(nt)],
            got_sib=[[copy(t, 0, sibling, me), copy(t, 4, sib(xn), me), copy(t, 5, sib(yn), me),
                      copy(t, 6, sib(diag), me)] for t in range(nt)])

    def start(self, ins, outs, scr):
        cps = self._copies(outs, scr)
        for t, (_, dt) in enumerate(self.shards):
            scr[t][...] = ins[t][...].astype(dt)
            for cp in [cps["local"][t]] + cps["own"][t]:
                cp.start()

    def stages(self):
        sizes = [s.size * jnp.dtype(dt).itemsize for s, dt in self.shards]
        done = [sum(sizes[:t + 1]) / sum(sizes) for t in range(len(sizes))]
        return [(self.mid_at * f, functools.partial(self.mid, t)) for t, f in enumerate(done)]

    def mid(self, t, ins, outs, scr):
        cps = self._copies(outs, scr)
        for cp in cps["got_xy"][t]:
            cp.wait_recv()
        for cp in [cps["relay"][t]] + cps["down"][t]:
            cp.start()

    def finish(self, ins, outs, scr):
        cps = self._copies(outs, scr)
        for t in range(len(self.shards)):
            cps["got_diag"][t].wait_recv()
            cps["down_diag"][t].start()
        for t in range(len(self.shards)):
            for cp in cps["got_sib"][t]:
                cp.wait_recv()
            for cp in cps["own"][t] + [cps["relay"][t]] + cps["down"][t] + [cps["down_diag"][t]]:
                cp.wait_send()
            cps["local"][t].wait()


def _run_carry(carry, name):
    def body(*refs):
        n_in, n_out = len(carry.in_arrays), len(carry.out_shape)
        ins, outs, scr = refs[:n_in], refs[n_in:n_in + n_out], refs[n_in + n_out:]
        carry.start(ins, outs, scr)
        for _, stage in carry.stages():
            stage(ins, outs, scr)
        carry.finish(ins, outs, scr)

    return pl.pallas_call(
        body, in_specs=carry.in_specs, out_specs=carry.out_specs, out_shape=carry.out_shape,
        scratch_shapes=carry.scratch, compiler_params=pltpu.CompilerParams(vmem_limit_bytes=VMEM_LIMIT),
        name=name)(*carry.in_arrays)


class _ExchangeCarry:
    def __init__(self, names, grads, mid_at=0.5):
        nt = len(grads)
        self.mid_at = mid_at
        self.names = names
        self.in_arrays = [g.reshape(4, 2, g.shape[0] // NDEV, g.shape[1]) for g in grads]
        self.in_specs = [HBM] * nt
        blocks = [g.shape[2:] for g in self.in_arrays]
        self.out_shape = [jax.ShapeDtypeStruct((3,) + b, BF16) for b in blocks]
        self.out_specs = [HBM] * nt
        self.scratch = ([pltpu.VMEM((4,) + b, BF16) for b in blocks] * 2 + [pltpu.VMEM(b, BF16) for b in blocks]
                        + [pltpu.SemaphoreType.DMA((nt, 3)), pltpu.SemaphoreType.DMA((nt, 3))]
                        + [pltpu.SemaphoreType.DMA((nt,))] * 4)

    def _copies(self, ins, outs, scr):
        nt = len(ins)
        theirs, own, relayed = scr[:nt], scr[nt:2 * nt], scr[2 * nt:3 * nt]
        send_sems, recv_sems, keep_sems, load_sems, swap_send, swap_recv = scr[3 * nt:]
        x, y, c = _place()
        q = lambda cx, cy: 2 * cx + cy
        near, far = (x ^ c, y ^ (1 - c)), (x ^ (1 - c), y ^ c)

        def remote(t, k, src, dst, chip):
            return pltpu.make_async_remote_copy(
                src_ref=src, dst_ref=dst, send_sem=send_sems.at[t, k], recv_sem=recv_sems.at[t, k],
                device_id=(*chip, c), device_id_type=MESH)

        return dict(
            swap=[pltpu.make_async_remote_copy(
                src_ref=ins[t].at[:, 1 - c], dst_ref=theirs[t], send_sem=swap_send.at[t], recv_sem=swap_recv.at[t],
                device_id=(x, y, 1 - c), device_id_type=MESH) for t in range(nt)],
            load=[pltpu.make_async_copy(ins[t].at[:, c], own[t], load_sems.at[t]) for t in range(nt)],
            keep=[pltpu.make_async_copy(own[t].at[q(x, y)], outs[t].at[0], keep_sems.at[t]) for t in range(nt)],
            direct=[remote(t, 0, own[t].at[q(*near)], outs[t].at[1], near) for t in range(nt)],
            relay=[remote(t, 1, own[t].at[q(1 - x, 1 - y)], relayed[t], near) for t in range(nt)],
            merged=[remote(t, 2, own[t].at[q(*far)], outs[t].at[2], far) for t in range(nt)],
            theirs=theirs, own=own, relayed=relayed, far=q(*far))

    EARLY_AT = 0.1

    def stages(self):
        return [(self.EARLY_AT, self.early), (self.mid_at, self.mid)]

    def start(self, ins, outs, scr):
        cps = self._copies(ins, outs, scr)
        for t in range(len(ins)):
            cps["swap"][t].start()
            cps["load"][t].start()

    def early(self, ins, outs, scr):
        cps = self._copies(ins, outs, scr)
        for t in range(len(ins)):
            cps["load"][t].wait()
            cps["swap"][t].wait_recv()
            own, theirs = cps["own"][t], cps["theirs"][t]
            for j in range(4):
                own[j] = (own[j].astype(F32) + theirs[j].astype(F32)).astype(BF16)
            for kind in ("relay", "direct", "keep"):
                cps[kind][t].start()

    def mid(self, ins, outs, scr):
        cps = self._copies(ins, outs, scr)
        for t in range(len(ins)):
            cps["relay"][t].wait_recv()
            own, far = cps["own"][t], cps["far"]
            own[far] = (own[far].astype(F32) + cps["relayed"][t][...].astype(F32)).astype(BF16)
            cps["merged"][t].start()

    def finish(self, ins, outs, scr):
        cps = self._copies(ins, outs, scr)
        for t in range(len(ins)):
            cps["swap"][t].wait_send()
            cps["direct"][t].wait()
            cps["relay"][t].wait_send()
            cps["merged"][t].wait()
            cps["keep"][t].wait()


class _Comm:
    def __init__(self, groups, opt):
        self.names = {tag: list(g) for tag, (g, _) in groups.items()}
        self.gathers = {tag: _GatherCarry(list(g.values()), mid_at) for tag, (g, mid_at) in groups.items()}
        self.reduced = {}
        self.last = None
        self.opt = opt
        self.updated = {}

    def gathered(self, tag, outs):
        return dict(zip(self.names[tag], outs))

    def reduce_start(self, grads):
        names = list(grads)
        return _ExchangeCarry(names, [grads[n] for n in names])

    def reduce_done(self, carry, outs):
        self.reduced.update(zip(carry.names, outs))

    def update_start(self, names):
        w, m, v = zip(*[self.opt[n] for n in names])
        return _AdamWCarry(names, [self.reduced[n] for n in names], w, m, v)

    def update_done(self, carry, outs):
        self.updated.update({n: outs[4 * k:4 * k + 4] for k, n in enumerate(carry.names)})


def _adamw_math(w, g, m, v):
    m = B1 * m + (1.0 - B1) * g
    v = B2 * v + (1.0 - B2) * (g * g)
    m_hat = m / (1.0 - B1 ** STEP)
    v_hat = v / (1.0 - B2 ** STEP)
    delta = -LR * (m_hat / (jnp.sqrt(v_hat) + AEPS) + WD * w)
    return delta, m, v


def _adamw_big(recvs, ws, ms, vs, name):
    nw = len(ws)

    def body(*refs):
        ins, outs = refs[:4 * nw], refs[4 * nw:]
        for k in range(nw):
            @pl.when(pl.program_id(0) // 2 == k)
            def _(k=k):
                r_ref, w_ref, m_ref, v_ref = ins[4 * k:4 * k + 4]
                g_ref, d_ref, mo_ref, vo_ref = outs[4 * k:4 * k + 4]
                g = r_ref[0].astype(F32)
                for q in range(1, 3):
                    g = g + r_ref[q].astype(F32)
                d, mn, vn = _adamw_math(w_ref[...], g, m_ref[...], v_ref[...])
                g_ref[...] = g
                d_ref[...] = d
                mo_ref[...] = mn
                vo_ref[...] = vn

    in_specs, out_specs, out_shape, args = [], [], [], []
    for k, (r, w, m, v) in enumerate(zip(recvs, ws, ms, vs)):
        rows, n = w.shape
        tr = rows // 2
        tile = lambda s, k=k: jnp.clip(s - 2 * k, 0, 1)
        row = pl.BlockSpec((tr, n), lambda s, tile=tile: (tile(s), 0))
        in_specs += [pl.BlockSpec((3, tr, n), lambda s, tile=tile: (0, tile(s), 0)), row, row, row]
        out_specs += [row] * 4
        out_shape += [jax.ShapeDtypeStruct(w.shape, F32)] * 4
        args += [r, w, m, v]
    res = pl.pallas_call(
        body, grid=(2 * nw,), in_specs=in_specs, out_specs=out_specs, out_shape=out_shape,
        compiler_params=_cp(("arbitrary",)), name=name)(*args)
    return [res[4 * k:4 * k + 4] for k in range(nw)]


class _AdamWCarry:
    def __init__(self, names, recvs, ws, ms, vs):
        self.names = names
        nw = len(ws)
        self.halves = [(k, j, w.shape[0] // 2) for k, w in enumerate(ws) for j in range(2)]
        self.in_arrays = [a for quad in zip(recvs, ws, ms, vs) for a in quad]
        self.in_specs = [HBM] * (4 * nw)
        self.out_shape = [jax.ShapeDtypeStruct(w.shape, F32) for w in ws for _ in range(4)]
        self.out_specs = [HBM] * (4 * nw)
        tr, n = max(h[2] for h in self.halves), ws[0].shape[1]
        self.scratch = [pltpu.VMEM((2, 3, tr, n), BF16), pltpu.VMEM((2, 3, tr, n), F32),
                        pltpu.VMEM((2, 4, tr, n), F32), pltpu.SemaphoreType.DMA((2, 4)),
                        pltpu.SemaphoreType.DMA((2, 4))]

    def _copies(self, c, ins, outs, scr):
        rbuf, fbuf, obuf, in_sems, out_sems = scr
        k, j, tr = self.halves[c]
        s, rows = c % 2, pl.ds(j * tr, tr)
        loads = [pltpu.make_async_copy(ins[4 * k].at[:, rows, :], rbuf.at[s, :, pl.ds(0, tr), :], in_sems.at[s, 0])]
        loads += [pltpu.make_async_copy(ins[4 * k + i].at[rows, :], fbuf.at[s, i - 1, pl.ds(0, tr), :],
                                        in_sems.at[s, i]) for i in range(1, 4)]
        stores = [pltpu.make_async_copy(obuf.at[s, q, pl.ds(0, tr), :], outs[4 * k + q].at[rows, :],
                                        out_sems.at[s, q]) for q in range(4)]
        return loads, stores

    def start(self, ins, outs, scr):
        for c in range(2):
            for cp in self._copies(c, ins, outs, scr)[0]:
                cp.start()

    def stages(self):
        n = len(self.halves)
        return [((c + 1) / (n + 1), functools.partial(self.half, c)) for c in range(n)]

    def half(self, c, ins, outs, scr):
        rbuf, fbuf, obuf = scr[:3]
        _, _, tr = self.halves[c]
        s = c % 2
        loads, stores = self._copies(c, ins, outs, scr)
        for cp in loads:
            cp.wait()
        if c >= 2:
            for cp in self._copies(c - 2, ins, outs, scr)[1]:
                cp.wait()
        g = rbuf[s, 0, 0:tr].astype(F32)
        for q in range(1, 3):
            g = g + rbuf[s, q, 0:tr].astype(F32)
        d, mn, vn = _adamw_math(fbuf[s, 0, 0:tr], g, fbuf[s, 1, 0:tr], fbuf[s, 2, 0:tr])
        for q, val in enumerate((g, d, mn, vn)):
            obuf[s, q, 0:tr] = val
        for cp in stores:
            cp.start()
        if c + 2 < len(self.halves):
            for cp in self._copies(c + 2, ins, outs, scr)[0]:
                cp.start()

    def finish(self, ins, outs, scr):
        n = len(self.halves)
        for c in range(max(n - 2, 0), n):
            for cp in self._copies(c, ins, outs, scr)[1]:
                cp.wait()


SMALL_NAMES = ("ffn1_norm", "mix_norm", "ffn2_norm", "conv_b", "conv_ln_g", "conv_ln_b", "q_norm", "k_norm")
SROWS = 16
LOSS_ROW = len(SMALL_NAMES)
CWT = (32, 1, 128)


def _small_sums(gs, loss_row, gcw, carry=None):
    ns = len(SMALL_NAMES)
    widths = [g.shape[1] for g in gs]

    def body(*refs):
        it = iter(refs)
        take = lambda n: [next(it) for _ in range(n)]
        g_refs, (loss_ref, gcw_ref) = take(ns), take(2)
        cins = take(len(carry.in_arrays)) if carry else []
        tot_ref, ctot_ref = take(2)
        couts = take(len(carry.out_shape)) if carry else []
        send, slots, cslots, send_sems, recv_sems, csend_sems, crecv_sems = take(7)
        cscr = list(it)
        x, y, c = _place()
        me = 4 * x + 2 * y + c
        send[...] = jnp.zeros_like(send)
        for k in range(ns):
            send[k:k + 1, 0:widths[k]] = g_refs[k][...]
        send[LOSS_ROW:LOSS_ROW + 1, 0:128] = loss_ref[...]
        slots[me] = send[...]
        cslots[me] = gcw_ref[me]
        cps = []
        for k in range(1, NDEV):
            peer = (x ^ ((k >> 2) & 1), y ^ ((k >> 1) & 1), c ^ (k & 1))
            cps.append(pltpu.make_async_remote_copy(
                src_ref=send, dst_ref=slots.at[me], send_sem=send_sems.at[k - 1], recv_sem=recv_sems.at[k - 1],
                device_id=peer, device_id_type=MESH))
            cps.append(pltpu.make_async_remote_copy(
                src_ref=gcw_ref.at[4 * peer[0] + 2 * peer[1] + peer[2]], dst_ref=cslots.at[me],
                send_sem=csend_sems.at[k - 1], recv_sem=crecv_sems.at[k - 1], device_id=peer, device_id_type=MESH))
        for cp in cps:
            cp.start()
        stages = [stage for _, stage in carry.stages()] if carry else []
        if carry:
            carry.start(cins, couts, cscr)
        for stage in stages[:1]:
            stage(cins, couts, cscr)
        for cp in cps:
            cp.wait()
        tot = slots[0]
        ctot = cslots[0]
        for j in range(1, NDEV):
            tot = tot + slots[j]
            ctot = ctot + cslots[j]
        tot_ref[...] = tot
        ctot_ref[...] = ctot
        for stage in stages[1:]:
            stage(cins, couts, cscr)
        if carry:
            carry.finish(cins, couts, cscr)

    args = [*gs, loss_row, gcw]
    out_shape = [jax.ShapeDtypeStruct((SROWS, D), F32), jax.ShapeDtypeStruct(CWT, F32)]
    res = pl.pallas_call(
        body, in_specs=[VMEM] * len(args) + (carry.in_specs if carry else []),
        out_specs=[VMEM] * 2 + (carry.out_specs if carry else []),
        out_shape=out_shape + (carry.out_shape if carry else []),
        scratch_shapes=[pltpu.VMEM((SROWS, D), F32), pltpu.VMEM((NDEV, SROWS, D), F32),
                        pltpu.VMEM((NDEV,) + CWT, F32)]
                       + [pltpu.SemaphoreType.DMA((NDEV - 1,))] * 4 + (carry.scratch if carry else []),
        name="small_sums")(*args, *(carry.in_arrays if carry else []))
    return res[0], res[1], res[2:]


def _adamw_small(tot, ctot, ws, ms, vs, wcw, mcw, vcw):
    ns = len(SMALL_NAMES)
    widths = [w.shape[1] for w in ws]

    def body(*refs):
        it = iter(refs)
        take = lambda n: [next(it) for _ in range(n)]
        (tot_ref, ctot_ref), w_refs, m_refs, v_refs = take(2), take(ns), take(ns), take(ns)
        wcw_ref, mcw_ref, vcw_ref = take(3)
        outs = [take(4) for _ in range(ns)]
        cw_outs, (loss_out,) = take(4), take(1)

        def step(g, w_ref, m_ref, v_ref, o):
            d, mn, vn = _adamw_math(w_ref[...], g, m_ref[...], v_ref[...])
            o[0][...], o[1][...], o[2][...], o[3][...] = g, d, mn, vn

        for k in range(ns):
            step(tot_ref[k:k + 1, 0:widths[k]], w_refs[k], m_refs[k], v_refs[k], outs[k])
        step(ctot_ref[0:CK, :, 0:HD], wcw_ref, mcw_ref, vcw_ref, cw_outs)
        loss_out[...] = tot_ref[LOSS_ROW:LOSS_ROW + 1, 0:128]

    args = [tot, ctot, *ws, *ms, *vs, wcw, mcw, vcw]
    out_shape = ([jax.ShapeDtypeStruct((1, n), F32) for n in widths for _ in range(4)]
                 + [jax.ShapeDtypeStruct((CK, 1, HD), F32)] * 4 + [jax.ShapeDtypeStruct((1, 128), F32)])
    res = pl.pallas_call(
        body, in_specs=[VMEM] * len(args), out_specs=[VMEM] * len(out_shape), out_shape=out_shape,
        name="adamw_small")(*args)
    per = [res[4 * k:4 * k + 4] for k in range(ns)]
    return per, res[4 * ns:4 * ns + 4], res[-1]


def kernel(x, ffn1_norm, ffn1_w_gate, ffn1_w_up, ffn1_w_down, mix_norm, w_in, q_norm, k_norm, conv_w, conv_b, conv_ln_g, conv_ln_b, w_out, ffn2_norm, ffn2_w_gate, ffn2_w_up, ffn2_w_down, loss_target, m_ffn1_norm, m_ffn1_w_gate, m_ffn1_w_up, m_ffn1_w_down, m_mix_norm, m_w_in, m_q_norm, m_k_norm, m_conv_w, m_conv_b, m_conv_ln_g, m_conv_ln_b, m_w_out, m_ffn2_norm, m_ffn2_w_gate, m_ffn2_w_up, m_ffn2_w_down, v_ffn1_norm, v_ffn1_w_gate, v_ffn1_w_up, v_ffn1_w_down, v_mix_norm, v_w_in, v_q_norm, v_k_norm, v_conv_w, v_conv_b, v_conv_ln_g, v_conv_ln_b, v_w_out, v_ffn2_norm, v_ffn2_w_gate, v_ffn2_w_up, v_ffn2_w_down):
    P = dict(ffn1_norm=ffn1_norm, ffn1_w_gate=ffn1_w_gate, ffn1_w_up=ffn1_w_up, ffn1_w_down=ffn1_w_down,
             mix_norm=mix_norm, w_in=w_in, q_norm=q_norm, k_norm=k_norm, conv_w=conv_w, conv_b=conv_b,
             conv_ln_g=conv_ln_g, conv_ln_b=conv_ln_b, w_out=w_out, ffn2_norm=ffn2_norm,
             ffn2_w_gate=ffn2_w_gate, ffn2_w_up=ffn2_w_up, ffn2_w_down=ffn2_w_down)
    M = dict(ffn1_norm=m_ffn1_norm, ffn1_w_gate=m_ffn1_w_gate, ffn1_w_up=m_ffn1_w_up, ffn1_w_down=m_ffn1_w_down,
             mix_norm=m_mix_norm, w_in=m_w_in, q_norm=m_q_norm, k_norm=m_k_norm, conv_w=m_conv_w, conv_b=m_conv_b,
             conv_ln_g=m_conv_ln_g, conv_ln_b=m_conv_ln_b, w_out=m_w_out, ffn2_norm=m_ffn2_norm,
             ffn2_w_gate=m_ffn2_w_gate, ffn2_w_up=m_ffn2_w_up, ffn2_w_down=m_ffn2_w_down)
    V = dict(ffn1_norm=v_ffn1_norm, ffn1_w_gate=v_ffn1_w_gate, ffn1_w_up=v_ffn1_w_up, ffn1_w_down=v_ffn1_w_down,
             mix_norm=v_mix_norm, w_in=v_w_in, q_norm=v_q_norm, k_norm=v_k_norm, conv_w=v_conv_w, conv_b=v_conv_b,
             conv_ln_g=v_conv_ln_g, conv_ln_b=v_conv_ln_b, w_out=v_w_out, ffn2_norm=v_ffn2_norm,
             ffn2_w_gate=v_ffn2_w_gate, ffn2_w_up=v_ffn2_w_up, ffn2_w_down=v_ffn2_w_down)
    order = ["ffn1_norm", "ffn1_w_gate", "ffn1_w_up", "ffn1_w_down", "mix_norm", "w_in", "q_norm", "k_norm",
             "conv_w", "conv_b", "conv_ln_g", "conv_ln_b", "w_out", "ffn2_norm", "ffn2_w_gate", "ffn2_w_up",
             "ffn2_w_down"]
    B, S, _ = x.shape
    T = B * S

    bigs = [("wg1", "ffn1_w_gate", True), ("wu1", "ffn1_w_up", True), ("wd1", "ffn1_w_down", False),
            ("win", "w_in", True), ("wout", "w_out", False),
            ("wg2", "ffn2_w_gate", True), ("wu2", "ffn2_w_up", True), ("wd2", "ffn2_w_down", False)]
    hm = lambda a, tr: jnp.transpose(a[0]) if tr else a[0]
    cw_pad = jnp.zeros((32, 128), F32).at[0:CK, 0:HD].set(conv_w[0])
    shard = {ln: (hm(P[pn], tr), BF16) for ln, pn, tr in bigs}
    gathered = _run_carry(_GatherCarry([shard["wg1"], shard["wu1"], (cw_pad, F32)]), "gather_first")
    W = {"wg1": gathered[0], "wu1": gathered[1]}
    cwg = gathered[2].reshape(NDEV, 32, 128)[:, 0:CK, 0:HD]
    W["conv_w"] = jnp.transpose(cwg, (1, 0, 2)).reshape(CK, DC)
    norms = {n: P[n] for n in SMALL_NAMES}
    comm = _Comm({"down_in": ({n: shard[n] for n in ("wd1", "win")}, 0.5),
                  "ffn2": ({n: shard[n] for n in ("wg2", "wu2", "wd2", "wout")}, 0.5)},
                 opt={ln: (hm(P[pn], tr), hm(M[pn], tr), hm(V[pn], tr)) for ln, pn, tr in bigs})

    loss_part, gx, _, small = _local_step(x.reshape(T, D), loss_target.reshape(T, D), norms, W, B, S, comm)

    G, Dl, Mn, Vn = {}, {}, {}, {}
    dcw = small["conv_w"].reshape(CK, NDEV, HD).transpose(1, 0, 2)
    loss_row = jnp.zeros((1, 128), F32).at[0, 0].set(loss_part)
    gcw = jnp.pad(dcw[:, :, None, :], ((0, 0), (0, CWT[0] - CK), (0, 0), (0, CWT[2] - HD)))
    taps_first = lambda a: jnp.transpose(a, (1, 0, 2))
    tot, ctot, got = _small_sums([small[n] for n in SMALL_NAMES], loss_row, gcw, carry=comm.last)
    comm.reduce_done(comm.last, got)
    per, cw_outs, loss_out = _adamw_small(
        tot, ctot, [P[n] for n in SMALL_NAMES], [M[n] for n in SMALL_NAMES], [V[n] for n in SMALL_NAMES],
        taps_first(P["conv_w"]), taps_first(M["conv_w"]), taps_first(V["conv_w"]))
    loss = loss_out[0, 0]
    for n, outs in zip(SMALL_NAMES, per):
        G[n], Dl[n], Mn[n], Vn[n] = outs
    G["conv_w"], Dl["conv_w"], Mn["conv_w"], Vn["conv_w"] = [taps_first(o) for o in cw_outs]

    group = ("wd1", "wg1", "wu1")
    w, m, v = zip(*[comm.opt[ln] for ln in group])
    comm.updated.update(zip(group, _adamw_big([comm.reduced[ln] for ln in group], w, m, v, "adamw_ffn1")))
    for ln, pn, tr in bigs:
        G[pn], Dl[pn], Mn[pn], Vn[pn] = [(jnp.transpose(o) if tr else o)[None] for o in comm.updated[ln]]

    return (loss, gx.reshape(B, S, D), *[G[n] for n in order], *[Dl[n] for n in order],
            *[Mn[n] for n in order], *[Vn[n] for n in order])
```

```python
import functools

import jax
import jax.numpy as jnp
from jax import lax
from jax.experimental import pallas as pl
from jax.experimental.pallas import tpu as pltpu

F32 = jnp.float32
BF16 = jnp.bfloat16

D = 1024
FF = 2816
HD = 64
DA = 512
DC = 512
DIN = 2560
CK = 31
BLK = 128
DILS = (1, 4, 16)
EPS = 1e-6
NDEV = 8
MESH = pl.DeviceIdType.MESH

LR, B1, B2, AEPS, WD, STEP = 0.001, 0.9, 0.999, 1e-08, 0.01, 10

NT = (((1,), (1,)), ((), ()))
TN = (((0,), (0,)), ((), ()))

VMEM_LIMIT = 60 * 1024 * 1024


def _cp(sem=None):
    return pltpu.CompilerParams(dimension_semantics=sem, vmem_limit_bytes=VMEM_LIMIT)


def _sigmoid(x):
    return 0.5 * (jnp.tanh(0.5 * x) + 1.0)


def _pallas(body, args, *, grid, in_specs, out_specs, out_shape, scratch_shapes, sem, name, carry=None):
    if carry is None:
        outs = pl.pallas_call(body, grid=grid, in_specs=in_specs, out_specs=out_specs, out_shape=out_shape,
                              scratch_shapes=scratch_shapes, compiler_params=_cp(sem), name=name)(*args)
        return outs, None
    n_in, n_out, n_scr = len(in_specs), len(out_shape), len(scratch_shapes)
    c_in, c_out = len(carry.in_arrays), len(carry.out_shape)

    def wrapped(*refs):
        ins, refs = refs[:n_in], refs[n_in:]
        cins, refs = refs[:c_in], refs[c_in:]
        outs, refs = refs[:n_out], refs[n_out:]
        couts, refs = refs[:c_out], refs[c_out:]
        scr, cscr = refs[:n_scr], refs[n_scr:]
        ids = [pl.program_id(a) for a in range(len(grid))]
        step = ids[0]
        for i, n in zip(ids[1:], grid[1:]):
            step = step * n + i
        steps = functools.reduce(lambda a, b: a * b, grid)

        @pl.when(step == 0)
        def _():
            carry.start(cins, couts, cscr)

        body(*ins, *outs, *scr)

        for frac, stage in carry.stages():
            @pl.when(step == int(steps * frac))
            def _(stage=stage):
                stage(cins, couts, cscr)

        @pl.when(step == steps - 1)
        def _():
            carry.finish(cins, couts, cscr)

    outs = pl.pallas_call(
        wrapped, grid=grid, in_specs=list(in_specs) + carry.in_specs, out_specs=list(out_specs) + carry.out_specs,
        out_shape=list(out_shape) + carry.out_shape, scratch_shapes=list(scratch_shapes) + carry.scratch,
        compiler_params=_cp(("arbitrary",) * len(grid)), name=name)(*args, *carry.in_arrays)
    return outs[:n_out], outs[n_out:]


FC = 256


def _resident(shape):
    return pl.BlockSpec(shape, lambda *_: (0,) * len(shape), pipeline_mode=pl.Buffered(1))


def _mix_out_ffn_loss(h1, attn, conv, wout, gain, wg, wu, wd, target, name):
    T = h1.shape[0]
    tm = 512
    nt = T // tm

    def body(h1_ref, at_ref, cv_ref, wo_ref, gain_ref, wg_ref, wu_ref, wd_ref, t_ref,
             h2_ref, n_ref, g_ref, u_ref, dout_ref, dyb_ref, sq_ref, a_hbm, a_scr, a_sem):
        t = pl.program_id(0)
        a_out = lambda i: pltpu.make_async_copy(a_scr, a_hbm.at[pl.ds(pl.multiple_of(i * tm, tm), tm), :], a_sem)

        @pl.when(t > 0)
        def _():
            a_out(t - 1).wait()

        xv = (h1_ref[...]
              + jnp.dot(at_ref[...], wo_ref[0:DA, :], preferred_element_type=F32)
              + jnp.dot(cv_ref[...], wo_ref[DA:D, :], preferred_element_type=F32))
        h2_ref[...] = xv
        r = lax.rsqrt(jnp.mean(xv * xv, axis=-1, keepdims=True) + EPS)
        n_ref[...] = (xv * r * gain_ref[...]).astype(BF16)
        for c in range(FF // FC):
            cols = slice(c * FC, (c + 1) * FC)
            nb = n_ref[...]
            g = lax.dot_general(nb, wg_ref[cols, :], NT, preferred_element_type=F32)
            u = lax.dot_general(nb, wu_ref[cols, :], NT, preferred_element_type=F32)
            g_ref[:, cols] = g.astype(BF16)
            u_ref[:, cols] = u.astype(BF16)
            a_scr[:, cols] = (g * _sigmoid(g) * u).astype(BF16)
        a_out(t).start()
        e = h2_ref[...] + 0.5 * jnp.dot(a_scr[...], wd_ref[...], preferred_element_type=F32) - t_ref[...]
        dout = e * (1.0 / D)
        dout_ref[...] = dout
        dyb_ref[...] = (0.5 * dout).astype(BF16)
        sq_ref[...] = jnp.sum(e * e, axis=0, keepdims=True)[None]

        @pl.when(t == nt - 1)
        def _():
            a_out(t).wait()

    row = pl.BlockSpec((tm, D), lambda t: (t, 0))
    half = pl.BlockSpec((tm, DA), lambda t: (t, 0))
    wide = pl.BlockSpec((tm, FF), lambda t: (t, 0))
    outs, _ = _pallas(
        body, (h1, attn, conv, wout, gain, wg, wu, wd, target), grid=(nt,),
        in_specs=[row, half, half, _resident((D, D)), _resident((1, D)), _resident((FF, D)), _resident((FF, D)),
                  _resident((FF, D)), row],
        out_specs=[row, row, wide, wide, row, row, pl.BlockSpec((1, 1, D), lambda t: (t, 0, 0)), HBM],
        out_shape=[jax.ShapeDtypeStruct((T, D), F32), jax.ShapeDtypeStruct((T, D), BF16)]
                  + [jax.ShapeDtypeStruct((T, FF), BF16)] * 2
                  + [jax.ShapeDtypeStruct((T, D), F32), jax.ShapeDtypeStruct((T, D), BF16),
                     jax.ShapeDtypeStruct((nt, 1, D), F32), jax.ShapeDtypeStruct((T, FF), BF16)],
        scratch_shapes=[pltpu.VMEM((tm, FF), BF16), pltpu.SemaphoreType.DMA(())],
        sem=("arbitrary",), name=name)
    return outs


def _ffn_gate_up(x, gain, wg, wu, name, carry=None):
    T = x.shape[0]
    tm = 512

    def body(x_ref, gain_ref, wg_ref, wu_ref, n_ref, g_ref, u_ref, a_ref):
        xv = x_ref[...]
        r = lax.rsqrt(jnp.mean(xv * xv, axis=-1, keepdims=True) + EPS)
        n_ref[...] = (xv * r * gain_ref[...]).astype(BF16)
        for c in range(FF // FC):
            cols = slice(c * FC, (c + 1) * FC)
            nb = n_ref[...]
            g = lax.dot_general(nb, wg_ref[cols, :], NT, preferred_element_type=F32)
            u = lax.dot_general(nb, wu_ref[cols, :], NT, preferred_element_type=F32)
            g_ref[:, cols] = g.astype(BF16)
            u_ref[:, cols] = u.astype(BF16)
            a_ref[:, cols] = (g * _sigmoid(g) * u).astype(BF16)

    row = pl.BlockSpec((tm, D), lambda t: (t, 0))
    wide = pl.BlockSpec((tm, FF), lambda t: (t, 0))
    return _pallas(
        body, (x, gain, wg, wu), grid=(T // tm,),
        in_specs=[row, _resident((1, D)), _resident((FF, D)), _resident((FF, D))],
        out_specs=[row, wide, wide, wide],
        out_shape=[jax.ShapeDtypeStruct((T, D), BF16)] + [jax.ShapeDtypeStruct((T, FF), BF16)] * 3,
        scratch_shapes=[], sem=("parallel",), name=name, carry=carry)


def _ffn_down_mix_in(x, a, wd, gain, win, name):
    T = x.shape[0]
    tm = 512

    def body(x_ref, a_ref, wd_ref, gain_ref, win_ref, h_ref, u_ref, n_ref):
        hv = x_ref[...] + 0.5 * jnp.dot(a_ref[...], wd_ref[...], preferred_element_type=F32)
        h_ref[...] = hv
        r = lax.rsqrt(jnp.mean(hv * hv, axis=-1, keepdims=True) + EPS)
        n_ref[...] = (hv * r * gain_ref[...]).astype(BF16)
        u_ref[...] = lax.dot_general(n_ref[...], win_ref[...], NT, preferred_element_type=F32).astype(BF16)

    row = pl.BlockSpec((tm, D), lambda t: (t, 0))
    wide = pl.BlockSpec((tm, FF), lambda t: (t, 0))
    outs, _ = _pallas(
        body, (x, a, wd, gain, win), grid=(T // tm,),
        in_specs=[row, wide, _resident((FF, D)), _resident((1, D)), _resident((DIN, D))],
        out_specs=[row, pl.BlockSpec((tm, DIN), lambda t: (t, 0)), row],
        out_shape=[jax.ShapeDtypeStruct((T, D), F32), jax.ShapeDtypeStruct((T, DIN), BF16),
                   jax.ShapeDtypeStruct((T, D), BF16)],
        scratch_shapes=[], sem=("parallel",), name=name)
    return outs


def _ffn_bwd_act(dyb, g, u, x, dout, gain, wg, wu, wd, name, carry=None):
    T = x.shape[0]
    tm = 256
    nt = T // tm

    def body(dy_ref, g_ref, u_ref, x_ref, dout_ref, gain_ref, wg_ref, wu_ref, wd_ref,
             dg_ref, du_ref, dx_ref, dgn_ref):
        for c in range(FF // FC):
            cols = slice(c * FC, (c + 1) * FC)
            da = lax.dot_general(dy_ref[...], wd_ref[cols, :], NT, preferred_element_type=F32)
            gv = g_ref[:, cols].astype(F32)
            uv = u_ref[:, cols].astype(F32)
            sg = _sigmoid(gv)
            dg_ref[:, cols] = (da * uv * (sg * (1.0 + gv * (1.0 - sg)))).astype(BF16)
            du_ref[:, cols] = (da * (gv * sg)).astype(BF16)
        dn = (jnp.dot(dg_ref[...], wg_ref[...], preferred_element_type=F32)
              + jnp.dot(du_ref[...], wu_ref[...], preferred_element_type=F32))
        dx, dgain = _rms_bwd_rows(dn, x_ref[...], gain_ref[...])
        dx_ref[...] = dout_ref[...] + dx

        @pl.when(pl.program_id(0) == 0)
        def _():
            dgn_ref[...] = jnp.zeros_like(dgn_ref)

        dgn_ref[...] += dgain[None]

    row = pl.BlockSpec((tm, D), lambda t: (t, 0))
    wide = pl.BlockSpec((tm, FF), lambda t: (t, 0))
    return _pallas(
        body, (dyb, g, u, x, dout, gain, wg, wu, wd), grid=(nt,),
        in_specs=[row, wide, wide, row, row, _resident((1, D)), _resident((FF, D)), _resident((FF, D)),
                  _resident((FF, D))],
        out_specs=[wide, wide, row, pl.BlockSpec((1, 1, D), lambda t: (0, 0, 0))],
        out_shape=[jax.ShapeDtypeStruct((T, FF), BF16)] * 2
                  + [jax.ShapeDtypeStruct((T, D), F32), jax.ShapeDtypeStruct((1, 1, D), F32)],
        scratch_shapes=[], sem=("arbitrary",), name=name, carry=carry)


def _ffn_bwd_w(lhs, rhs, name, carry=None):
    T = rhs.shape[0]
    tf = 256

    def body(l_ref, r_ref, dw_ref):
        dw_ref[...] = lax.dot_general(l_ref[...], r_ref[...], TN, preferred_element_type=F32).astype(BF16)

    (dw,), got = _pallas(
        body, (lhs, rhs), grid=(FF // tf,),
        in_specs=[pl.BlockSpec((T, tf), lambda f: (0, f)), _resident((T, D))],
        out_specs=[pl.BlockSpec((tf, D), lambda f: (f, 0))], out_shape=[jax.ShapeDtypeStruct((FF, D), BF16)],
        scratch_shapes=[], sem=("parallel",), name=name, carry=carry)
    return dw, got


def _rms_bwd_rows(dn, xv, gain):
    r = lax.rsqrt(jnp.mean(xv * xv, axis=-1, keepdims=True) + EPS)
    xhat = xv * r
    dxhat = dn * gain
    dx = r * (dxhat - xhat * jnp.mean(dxhat * xhat, axis=-1, keepdims=True))
    return dx, jnp.sum(dn * xhat, axis=0, keepdims=True)


def _mix_out_bwd(dh, attn, conv, wout):
    T = dh.shape[0]
    tm = 512
    nt = T // tm

    def body(dh_ref, a_ref, c_ref, w_ref, da_ref, dc_ref, dw_ref, acc_scr):
        t = pl.program_id(0)

        @pl.when(t == 0)
        def _():
            acc_scr[...] = jnp.zeros_like(acc_scr)

        dhb = dh_ref[...].astype(BF16)
        dmix = lax.dot_general(dhb, w_ref[...], NT, preferred_element_type=F32)
        da_ref[...] = dmix[:, 0:DA].astype(BF16)
        dc_ref[...] = dmix[:, DA:D].astype(BF16)
        acc_scr[0:DA, :] += lax.dot_general(a_ref[...], dhb, TN, preferred_element_type=F32)
        acc_scr[DA:D, :] += lax.dot_general(c_ref[...], dhb, TN, preferred_element_type=F32)

        @pl.when(t == nt - 1)
        def _():
            dw_ref[...] = acc_scr[...].astype(BF16)

    row = pl.BlockSpec((tm, D), lambda t: (t, 0))
    half = pl.BlockSpec((tm, DA), lambda t: (t, 0))
    full = pl.BlockSpec((D, D), lambda t: (0, 0))
    return pl.pallas_call(
        body, grid=(nt,), in_specs=[row, half, half, full], out_specs=[half, half, full],
        out_shape=[jax.ShapeDtypeStruct((T, DA), BF16)] * 2 + [jax.ShapeDtypeStruct((D, D), BF16)],
        scratch_shapes=[pltpu.VMEM((D, D), F32)],
        compiler_params=_cp(("arbitrary",)), name="mix_out_bwd")(dh, attn, conv, wout)


def _mix_in_bwd(dparts, win, nb, h, dh, gain):
    T = h.shape[0]
    tm = 512
    nt = T // tm

    def body(d0, d1, d2, d3, d4, w_ref, n_ref, h_ref, dh_ref, gain_ref,
             dw_ref, dx_ref, dyb_ref, dg_ref, acc_scr):
        t = pl.program_id(0)

        @pl.when(t == 0)
        def _():
            acc_scr[...] = jnp.zeros_like(acc_scr)

        n = n_ref[...]
        dn = jnp.zeros((tm, D), F32)
        for i, d_ref in enumerate((d0, d1, d2, d3, d4)):
            dv = d_ref[...]
            dn = dn + jnp.dot(dv, w_ref[i * DA:(i + 1) * DA, :], preferred_element_type=F32)
            acc_scr[i * DA:(i + 1) * DA, :] += lax.dot_general(dv, n, TN, preferred_element_type=F32)
        dx, dgain = _rms_bwd_rows(dn, h_ref[...], gain_ref[...])
        tot = dh_ref[...] + dx
        dx_ref[...] = tot
        dyb_ref[...] = (0.5 * tot).astype(BF16)
        dg_ref[...] = dgain[None]

        @pl.when(t == nt - 1)
        def _():
            dw_ref[...] = acc_scr[...].astype(BF16)

    row = pl.BlockSpec((tm, D), lambda t: (t, 0))
    half = pl.BlockSpec((tm, DA), lambda t: (t, 0))
    full = pl.BlockSpec((DIN, D), lambda t: (0, 0))
    return pl.pallas_call(
        body, grid=(nt,),
        in_specs=[half] * 5 + [full, row, row, row, pl.BlockSpec((1, D), lambda t: (0, 0))],
        out_specs=[full, row, row, pl.BlockSpec((1, 1, D), lambda t: (t, 0, 0))],
        out_shape=[jax.ShapeDtypeStruct((DIN, D), BF16), jax.ShapeDtypeStruct((T, D), F32),
                   jax.ShapeDtypeStruct((T, D), BF16), jax.ShapeDtypeStruct((nt, 1, D), F32)],
        scratch_shapes=[pltpu.VMEM((DIN, D), F32)],
        compiler_params=_cp(("arbitrary",)), name="mix_in_bwd")(*dparts, win, nb, h, dh, gain)


def _head_masks():
    lane = lax.broadcasted_iota(jnp.int32, (1, 2 * HD), 1)
    m0 = lane < HD
    return m0, jnp.logical_not(m0)


def _stack_heads(v, m0, m1):
    z = jnp.zeros_like(v)
    return jnp.concatenate([jnp.where(m0, v, z), jnp.where(m1, v, z)], axis=0)


def _unstack_heads(v2, m0):
    return jnp.where(m0, v2[0:BLK], v2[BLK:2 * BLK])


def _head_sums(xv):
    ri = lax.broadcasted_iota(jnp.int32, (2 * HD, 2 * HD), 0)
    ci = lax.broadcasted_iota(jnp.int32, (2 * HD, 2 * HD), 1)
    ones = jnp.where((ri < HD) == (ci < HD), 1.0, 0.0).astype(BF16)
    hi = xv.astype(BF16)
    lo = (xv - hi.astype(F32)).astype(BF16)
    return (jnp.dot(hi, ones, preferred_element_type=F32) + jnp.dot(lo, ones, preferred_element_type=F32))


def _head_rms(xv):
    return lax.rsqrt(_head_sums(xv * xv) * (1.0 / HD) + EPS)


def _band_mask(first):
    qi = lax.broadcasted_iota(jnp.int32, (BLK, 2 * BLK), 0)
    ci = lax.broadcasted_iota(jnp.int32, (BLK, 2 * BLK), 1)
    band = (ci >= qi) & (ci <= qi + BLK)
    return band & ((ci >= BLK) | jnp.logical_not(first))


def _block_rows(j, d, seg):
    r, n = j // seg, j % seg
    start = r + (d * BLK) * n
    first = n == 0
    prev = jnp.where(first, start, start - d * BLK)
    return pl.ds(start, BLK, stride=d), pl.ds(prev, BLK, stride=d), first


def _block_keys(refs, cur, prev, first, single):
    if single:
        qi = lax.broadcasted_iota(jnp.int32, (BLK, BLK), 0)
        ci = lax.broadcasted_iota(jnp.int32, (BLK, BLK), 1)
        return [r[cur, :].astype(BF16) for r in refs], ci <= qi
    return ([jnp.concatenate([r[prev, :], r[cur, :]], axis=0).astype(BF16) for r in refs], _band_mask(first))


def _attn_fwd(u, qg2, kg2, B, S, carry=None):
    T = B * S
    NB = S // BLK
    scale = HD ** -0.5

    def body(q_ref, k_ref, v_ref, qg_ref, kg_ref, o_ref, lse_ref, qn, kn, vn, os_, ls_):
        m0, m1 = _head_masks()
        qv = q_ref[...].astype(F32)
        qn[...] = qv * _head_rms(qv) * (qg_ref[...] * scale)
        kv = k_ref[...].astype(F32)
        kn[...] = kv * _head_rms(kv) * kg_ref[...]
        vn[...] = v_ref[...].astype(F32)

        for i, d in enumerate(DILS):
            seg = NB // d

            def blk(j, c, i=i, d=d, seg=seg):
                cur, prev, first = _block_rows(j, d, seg)
                q2 = _stack_heads(qn[cur, :].astype(BF16), m0, m1)
                (kk, vv), mask = _block_keys((kn, vn), cur, prev, first, False)
                s = lax.dot_general(q2, kk, NT, preferred_element_type=F32)
                s = jnp.where(jnp.concatenate([mask, mask], axis=0), s, -1e30)
                mx = jnp.max(s, axis=-1, keepdims=True)
                p = jnp.exp(s - mx)
                l = jnp.sum(p, axis=-1, keepdims=True)
                o2 = jnp.dot((p * (1.0 / l)).astype(BF16), vv, preferred_element_type=F32)
                os_[i, cur, :] = _unstack_heads(o2, m0)
                ls_[i, cur, :] = _unstack_heads(mx + jnp.log(l), m0)
                return c

            lax.fori_loop(0, NB, blk, 0, unroll=8)

        def comb(c, carry):
            rows = pl.ds(pl.multiple_of(c * 256, 256), 256)
            l0, l1, l2 = ls_[0, rows, :], ls_[1, rows, :], ls_[2, rows, :]
            mx = jnp.maximum(jnp.maximum(l0, l1), l2)
            e0, e1, e2 = jnp.exp(l0 - mx), jnp.exp(l1 - mx), jnp.exp(l2 - mx)
            tot = e0 + e1 + e2
            inv = 1.0 / tot
            o = (e0 * os_[0, rows, :] + e1 * os_[1, rows, :] + e2 * os_[2, rows, :]) * inv
            o_ref[rows, :] = o.astype(BF16)
            lse_ref[rows, :] = mx + jnp.log(tot)
            return carry

        lax.fori_loop(0, S // 256, comb, 0)

    pair = 2 * HD
    blk_spec = lambda off: pl.BlockSpec((S, pair), lambda b, p, off=off: (b, off + p))
    gspec = pl.BlockSpec((1, pair), lambda b, p: (0, 0))
    return _pallas(
        body, (u, u, u, qg2, kg2), grid=(B, DA // pair),
        in_specs=[blk_spec(0), blk_spec(DA // pair), blk_spec(2 * DA // pair), gspec, gspec],
        out_specs=[blk_spec(0), blk_spec(0)],
        out_shape=[jax.ShapeDtypeStruct((T, DA), BF16), jax.ShapeDtypeStruct((T, DA), F32)],
        scratch_shapes=[pltpu.VMEM((S, pair), F32)] * 3 + [pltpu.VMEM((3, S, pair), F32)] * 2,
        sem=("parallel", "parallel"), name="attn_fwd", carry=carry)


def _attn_bwd(u, attn, dattn, lse, qg2, kg2, B, S, carry=None):
    T = B * S
    NB = S // BLK
    scale = HD ** -0.5
    pair = 2 * HD

    def body(q_ref, k_ref, v_ref, o_ref, do_ref, lse_ref, qg_ref, kg_ref,
             dq_ref, dk_ref, dv_ref, dgn_ref,
             qn, kn, vn, don, ldl, accq, acck, accv, rq, rk):
        m0, m1 = _head_masks()
        lane = lax.broadcasted_iota(jnp.int32, (1, pair), 1)
        qv = q_ref[...].astype(F32)
        rq[...] = _head_rms(qv)
        qn[...] = qv * rq[...] * (qg_ref[...] * scale)
        kv = k_ref[...].astype(F32)
        rk[...] = _head_rms(kv)
        kn[...] = kv * rk[...] * kg_ref[...]
        vn[...] = v_ref[...].astype(F32)
        dov = do_ref[...].astype(F32)
        don[...] = dov
        ldl[...] = jnp.where((lane % HD) < HD // 2, lse_ref[...], _head_sums(dov * o_ref[...].astype(F32)))

        for i, d in enumerate(DILS):
            seg = NB // d

            def blk(j, c, i=i, d=d, seg=seg):
                cur, prev, first = _block_rows(j, d, seg)
                q2 = _stack_heads(qn[cur, :].astype(BF16), m0, m1)
                do2 = _stack_heads(don[cur, :].astype(BF16), m0, m1)
                (kk, vv), mask = _block_keys((kn, vn), cur, prev, first, seg == 1)
                ldv = ldl[cur, :]
                lse2 = jnp.concatenate([ldv[:, 0:1], ldv[:, HD:HD + 1]], axis=0)
                dl2 = jnp.concatenate([ldv[:, HD // 2:HD // 2 + 1], ldv[:, HD + HD // 2:HD + HD // 2 + 1]], axis=0)
                s = lax.dot_general(q2, kk, NT, preferred_element_type=F32)
                p = jnp.where(jnp.concatenate([mask, mask], axis=0), jnp.exp(s - lse2), 0.0)
                dp = lax.dot_general(do2, vv, NT, preferred_element_type=F32)
                ds = (p * (dp - dl2)).astype(BF16)
                dq_acc = _unstack_heads(jnp.dot(ds, kk, preferred_element_type=F32), m0)
                dk_acc = lax.dot_general(ds, q2, TN, preferred_element_type=F32)
                dv_acc = lax.dot_general(p.astype(BF16), do2, TN, preferred_element_type=F32)
                if i == 0:
                    accq[cur, :] = dq_acc
                    acck[cur, :] = dk_acc[BLK:2 * BLK]
                    accv[cur, :] = dv_acc[BLK:2 * BLK]
                    acck[prev, :] += dk_acc[0:BLK]
                    accv[prev, :] += dv_acc[0:BLK]
                elif seg == 1:
                    accq[cur, :] += dq_acc
                    acck[cur, :] += dk_acc
                    accv[cur, :] += dv_acc
                else:
                    accq[cur, :] += dq_acc
                    acck[prev, :] += dk_acc[0:BLK]
                    acck[cur, :] += dk_acc[BLK:2 * BLK]
                    accv[prev, :] += dv_acc[0:BLK]
                    accv[cur, :] += dv_acc[BLK:2 * BLK]
                return c

            lax.fori_loop(0, NB, blk, 0, unroll=8)

        def norm_bwd(x_ref, r_ref, dn, gain):
            r = r_ref[...]
            xhat = x_ref[...].astype(F32) * r
            dxhat = dn * gain
            dx = r * (dxhat - xhat * (_head_sums(dxhat * xhat) * (1.0 / HD)))
            return dx, jnp.sum(dn * xhat, axis=0, keepdims=True)

        dq, dgq = norm_bwd(q_ref, rq, accq[...], qg_ref[...] * scale)
        dk, dgk = norm_bwd(k_ref, rk, acck[...], kg_ref[...])
        dq_ref[...] = dq.astype(BF16)
        dk_ref[...] = dk.astype(BF16)
        dv_ref[...] = accv[...].astype(BF16)
        dgn_ref[...] = jnp.concatenate([dgq * scale, dgk, jnp.zeros((6, pair), F32)], axis=0)[None]

    blk_spec = lambda off: pl.BlockSpec((S, pair), lambda b, p, off=off: (b, off + p))
    gspec = pl.BlockSpec((1, pair), lambda b, p: (0, 0))
    np_ = DA // pair
    return _pallas(
        body, (u, u, u, attn, dattn, lse, qg2, kg2), grid=(B, np_),
        in_specs=[blk_spec(0), blk_spec(np_), blk_spec(2 * np_), blk_spec(0), blk_spec(0), blk_spec(0),
                  gspec, gspec],
        out_specs=[blk_spec(0), blk_spec(0), blk_spec(0),
                   pl.BlockSpec((1, 8, pair), lambda b, p: (b * np_ + p, 0, 0))],
        out_shape=[jax.ShapeDtypeStruct((T, DA), BF16)] * 3 + [jax.ShapeDtypeStruct((B * np_, 8, pair), F32)],
        scratch_shapes=[pltpu.VMEM((S, pair), F32)] * 10,
        sem=("parallel", "parallel"), name="attn_bwd", carry=carry)


CT = 32
CPAD = 32


def _shifted(win, offsets):
    rolled, out = {}, {}
    n = win.shape[0]
    for o in offsets:
        sub = o % 8
        if sub not in rolled:
            rolled[sub] = win if sub == 0 else pltpu.roll(win, n - sub, 0)
        out[o] = rolled[sub][o - sub:o - sub + CT, :]
    return out


def _ln_fwd(y, g, b):
    mu = jnp.mean(y, axis=-1, keepdims=True)
    yc = y - mu
    rstd = lax.rsqrt(jnp.mean(yc * yc, axis=-1, keepdims=True) + EPS)
    xhat = yc * rstd
    return xhat, rstd, xhat * g + b


def _fill_glu(ca_ref, cg_ref, glu, S):
    glu[pl.ds(0, CPAD), :] = jnp.zeros((CPAD, DC), F32)

    def fill(i, c):
        rows = pl.ds(pl.multiple_of(i * 256, 256), 256)
        a = ca_ref[rows, :].astype(F32)
        gt = cg_ref[rows, :].astype(F32)
        glu[pl.ds(pl.multiple_of(CPAD + i * 256, CT), 256), :] = a * _sigmoid(gt)
        return c

    lax.fori_loop(0, S // 256, fill, 0)


def _conv_fwd(u, cw, cb, lg, lb, B, S):
    T = B * S

    def body(ca_ref, cg_ref, w_ref, b_ref, lg_ref, lb_ref, o_ref, y_ref, glu):
        _fill_glu(ca_ref, cg_ref, glu, S)

        def step(i, c):
            t0 = pl.multiple_of(i * CT, CT)
            win = glu[pl.ds(t0, 2 * CT), :]
            acc = jnp.zeros((CT, DC), F32) + b_ref[...]
            taps = _shifted(win, [k + 2 for k in range(CK)])
            for k in range(CK):
                acc = acc + taps[k + 2] * w_ref[k:k + 1, :]
            y_ref[pl.ds(t0, CT), :] = acc
            _, _, z = _ln_fwd(acc, lg_ref[...], lb_ref[...])
            o_ref[pl.ds(t0, CT), :] = (z * _sigmoid(z)).astype(BF16)
            return c

        lax.fori_loop(0, S // CT, step, 0, unroll=4)

    vec = pl.BlockSpec((1, DC), lambda b: (0, 0))
    return pl.pallas_call(
        body, grid=(B,),
        in_specs=[pl.BlockSpec((S, DC), lambda b: (b, 3)), pl.BlockSpec((S, DC), lambda b: (b, 4)),
                  pl.BlockSpec((CT, DC), lambda b: (0, 0)), vec, vec, vec],
        out_specs=[pl.BlockSpec((S, DC), lambda b: (b, 0))] * 2,
        out_shape=[jax.ShapeDtypeStruct((T, DC), BF16), jax.ShapeDtypeStruct((T, DC), F32)],
        scratch_shapes=[pltpu.VMEM((CPAD + S, DC), F32)],
        compiler_params=_cp(("parallel",)), name="conv_fwd")(u, u, cw, cb, lg, lb)


def _conv_bwd(u, y, dconv, cw, lg, lb, B, S):
    T = B * S

    def body(ca_ref, cg_ref, y_ref, dc_ref, w_ref, lg_ref, lb_ref,
             dca_ref, dcg_ref, dw_ref, ds_ref, glu, dyp, dwacc):
        _fill_glu(ca_ref, cg_ref, glu, S)
        dyp[pl.ds(S, CPAD), :] = jnp.zeros((CPAD, DC), F32)
        lgv, lbv = lg_ref[...], lb_ref[...]

        def sum8(v):
            return functools.reduce(jnp.add, [v[r:r + 8] for r in range(0, v.shape[0], 8)])

        P1 = 4 * CT

        def p1(i, carry):
            sb, sg, sl = carry
            t0 = pl.multiple_of(i * P1, P1)
            xhat, rstd, z = _ln_fwd(y_ref[pl.ds(t0, P1), :], lgv, lbv)
            sz = _sigmoid(z)
            dz = dc_ref[pl.ds(t0, P1), :].astype(F32) * (sz * (1.0 + z * (1.0 - sz)))
            dxhat = dz * lgv
            dy = rstd * (dxhat - jnp.mean(dxhat, axis=-1, keepdims=True)
                         - xhat * jnp.mean(dxhat * xhat, axis=-1, keepdims=True))
            dyp[pl.ds(t0, P1), :] = dy
            return sb + sum8(dy), sg + sum8(dz * xhat), sl + sum8(dz)

        z8 = jnp.zeros((8, DC), F32)
        sb, sg, sl = lax.fori_loop(0, S // P1, p1, (z8, z8, z8))
        rs = lambda v: jnp.sum(v, axis=0, keepdims=True)
        ds_ref[...] = jnp.concatenate([rs(sb), rs(sg), rs(sl), jnp.zeros((5, DC), F32)], axis=0)[None]

        def p2(i, c):
            t0 = pl.multiple_of(i * CT, CT)
            win = dyp[pl.ds(t0, 2 * CT), :]
            acc = jnp.zeros((CT, DC), F32)
            taps = _shifted(win, [30 - k for k in range(CK)])
            for k in range(CK):
                acc = acc + taps[30 - k] * w_ref[k:k + 1, :]
            a = ca_ref[pl.ds(t0, CT), :].astype(F32)
            sgt = _sigmoid(cg_ref[pl.ds(t0, CT), :].astype(F32))
            dca_ref[pl.ds(t0, CT), :] = (acc * sgt).astype(BF16)
            dcg_ref[pl.ds(t0, CT), :] = (acc * a * sgt * (1.0 - sgt)).astype(BF16)
            return c

        lax.fori_loop(0, S // CT, p2, 0)

        dwacc[...] = jnp.zeros_like(dwacc)

        def p3(i, c):
            t0 = pl.multiple_of(i * CT, CT)
            win = glu[pl.ds(t0, 2 * CT), :]
            dy = dyp[pl.ds(t0, CT), :]
            for k in range(CK):
                dwacc[k] += sum8(dy * win[k + 2:k + 2 + CT, :])
            return c

        lax.fori_loop(0, S // CT, p3, 0)
        dw_ref[...] = jnp.sum(dwacc[...], axis=1)[None]

    vec = pl.BlockSpec((1, DC), lambda b: (0, 0))
    seq = pl.BlockSpec((S, DC), lambda b: (b, 0))
    return pl.pallas_call(
        body, grid=(B,),
        in_specs=[pl.BlockSpec((S, DC), lambda b: (b, 3)), pl.BlockSpec((S, DC), lambda b: (b, 4)),
                  seq, seq, pl.BlockSpec((CT, DC), lambda b: (0, 0)), vec, vec],
        out_specs=[seq, seq, pl.BlockSpec((1, CT, DC), lambda b: (b, 0, 0)),
                   pl.BlockSpec((1, 8, DC), lambda b: (b, 0, 0))],
        out_shape=[jax.ShapeDtypeStruct((T, DC), BF16)] * 2
                  + [jax.ShapeDtypeStruct((B, CT, DC), F32), jax.ShapeDtypeStruct((B, 8, DC), F32)],
        scratch_shapes=[pltpu.VMEM((CPAD + S, DC), F32), pltpu.VMEM((S + CPAD, DC), F32),
                        pltpu.VMEM((CT, 8, DC), F32)],
        compiler_params=_cp(("parallel",)), name="conv_bwd")(u, u, y, dconv, cw, lg, lb)


def _local_step(x, target, norms, W, B, S, comm=None):
    qg2 = jnp.concatenate([norms["q_norm"], norms["q_norm"]], axis=1)
    kg2 = jnp.concatenate([norms["k_norm"], norms["k_norm"]], axis=1)
    cw = jnp.concatenate([W["conv_w"], jnp.zeros((1, DC), F32)], axis=0)

    W = dict(W)
    (n1, g1, u1, act1), got = _ffn_gate_up(x, norms["ffn1_norm"], W["wg1"], W["wu1"], "ffn1_gate_up",
                                           carry=comm.gathers["down_in"] if comm else None)
    if comm:
        W.update(comm.gathered("down_in", got))
    h1, u, n2 = _ffn_down_mix_in(x, act1, W["wd1"], norms["mix_norm"], W["win"], "ffn1_down_mix_in")
    (attn, lse), got = _attn_fwd(u, qg2, kg2, B, S, carry=comm.gathers["ffn2"] if comm else None)
    if comm:
        W = dict(W, **comm.gathered("ffn2", got))
    conv, y = _conv_fwd(u, cw, norms["conv_b"], norms["conv_ln_g"], norms["conv_ln_b"], B, S)
    h2, n3, g2, u2, dout, dyb, sq, act2 = _mix_out_ffn_loss(h1, attn, conv, W["wout"], norms["ffn2_norm"],
                                                            W["wg2"], W["wu2"], W["wd2"], target, "ffn2_fwd")
    loss = (0.5 / D) * jnp.sum(sq)

    (dg2, du2, dh2, dgn_ffn2), _ = _ffn_bwd_act(dyb, g2, u2, h2, dout, norms["ffn2_norm"],
                                               W["wg2"], W["wu2"], W["wd2"], "ffn2_bwd_act")
    dwd2, _ = _ffn_bwd_w(act2, dyb, "ffn2_bwd_wd")
    dwg2, _ = _ffn_bwd_w(dg2, n3, "ffn2_bwd_wg")
    dwu2, _ = _ffn_bwd_w(du2, n3, "ffn2_bwd_wu")
    dattn, dconv, dwout = _mix_out_bwd(dh2, attn, conv, W["wout"])
    carry = comm.reduce_start({"wg2": dwg2, "wu2": dwu2, "wd2": dwd2, "wout": dwout}) if comm else None
    (dq, dk, dv, dgn_qk), got = _attn_bwd(u, attn, dattn, lse, qg2, kg2, B, S, carry=carry)
    if comm:
        comm.reduce_done(carry, got)
    dca, dcg, dcw, dcs = _conv_bwd(u, y, dconv, cw, norms["conv_ln_g"], norms["conv_ln_b"], B, S)
    dwin, dh1, dyb1, dgn_mix = _mix_in_bwd((dq, dk, dv, dca, dcg), W["win"], n2, h1, dh2, norms["mix_norm"])
    carry = comm.reduce_start({"win": dwin}) if comm else None
    dwd1, got = _ffn_bwd_w(act1, dyb1, "ffn1_bwd_wd", carry=carry)
    if comm:
        comm.reduce_done(carry, got)
    carry = comm.update_start(("wg2", "wu2", "wd2", "wout", "win")) if comm else None
    (dg1, du1, gx, dgn_ffn1), got = _ffn_bwd_act(dyb1, g1, u1, x, dh1, norms["ffn1_norm"],
                                                W["wg1"], W["wu1"], W["wd1"], "ffn1_bwd_act", carry=carry)
    if comm:
        comm.update_done(carry, got)
        carry = comm.reduce_start({"wd1": dwd1})
    dwg1, got = _ffn_bwd_w(dg1, n1, "ffn1_bwd_wg", carry=carry)
    if comm:
        comm.reduce_done(carry, got)
        carry = comm.reduce_start({"wg1": dwg1})
    dwu1, got = _ffn_bwd_w(du1, n1, "ffn1_bwd_wu", carry=carry)
    if comm:
        comm.reduce_done(carry, got)
        comm.last = comm.reduce_start({"wu1": dwu1})

    qk = jnp.sum(dgn_qk, axis=0)
    cs = jnp.sum(dcs, axis=0)
    small = {
        "ffn1_norm": jnp.sum(dgn_ffn1, axis=0),
        "mix_norm": jnp.sum(dgn_mix, axis=0),
        "q_norm": qk[0:1, 0:HD] + qk[0:1, HD:2 * HD],
        "k_norm": qk[1:2, 0:HD] + qk[1:2, HD:2 * HD],
        "conv_w": jnp.sum(dcw, axis=0)[0:CK],
        "conv_b": cs[0:1],
        "conv_ln_g": cs[1:2],
        "conv_ln_b": cs[2:3],
        "ffn2_norm": jnp.sum(dgn_ffn2, axis=0),
    }
    big = {"wg1": dwg1, "wu1": dwu1, "wd1": dwd1, "win": dwin, "wout": dwout,
           "wg2": dwg2, "wu2": dwu2, "wd2": dwd2}
    return loss, gx, big, small


HBM = pl.BlockSpec(memory_space=pltpu.HBM)
VMEM = pl.BlockSpec(memory_space=pltpu.VMEM)


def _place():
    return lax.axis_index("x"), lax.axis_index("y"), lax.axis_index("c")


class _GatherCarry:
    def __init__(self, shards, mid_at=0.5):
        nt = len(shards)
        self.mid_at = mid_at
        self.shards = shards
        self.in_arrays = [s for s, _ in shards]
        self.in_specs = [VMEM] * nt
        self.out_shape = [jax.ShapeDtypeStruct((NDEV * s.shape[0], s.shape[1]), dt) for s, dt in shards]
        self.out_specs = [HBM] * nt
        self.scratch = ([pltpu.VMEM(s.shape, dt) for s, dt in shards]
                        + [pltpu.SemaphoreType.DMA((nt, 7)), pltpu.SemaphoreType.DMA((nt, 7)),
                           pltpu.SemaphoreType.DMA((nt,))])

    def _copies(self, outs, scr):
        nt = len(self.shards)
        stages = scr[:nt]
        send_sems, recv_sems, local_sems = scr[nt:]
        x, y, c = _place()
        me, sibling = (x, y, c), (x, y, 1 - c)
        xn, yn, diag = (1 - x, y, c), (x, 1 - y, c), (1 - x, 1 - y, c)
        via = (x ^ c, y ^ (1 - c), c)
        onto = (x ^ (1 - c), y ^ c, c)

        def rows(t, px, py, pc):
            r = self.shards[t][0].shape[0]
            return outs[t].at[pl.ds((4 * px + 2 * py + pc) * r, r), :]

        def copy(t, k, block, to, src=None):
            return pltpu.make_async_remote_copy(
                src_ref=rows(t, *block) if src is None else src, dst_ref=rows(t, *block),
                send_sem=send_sems.at[t, k], recv_sem=recv_sems.at[t, k],
                device_id=to, device_id_type=MESH)

        sib = lambda b: (b[0], b[1], 1 - c)
        return dict(
            local=[pltpu.make_async_copy(stages[t], rows(t, *me), local_sems.at[t]) for t in range(nt)],
            own=[[copy(t, 0, me, sibling, src=stages[t]), copy(t, 1, me, xn, src=stages[t]),
                  copy(t, 2, me, yn, src=stages[t])] for t in range(nt)],
            relay=[copy(t, 3, via, onto) for t in range(nt)],
            down=[[copy(t, 4, xn, sibling), copy(t, 5, yn, sibling)] for t in range(nt)],
            down_diag=[copy(t, 6, diag, sibling) for t in range(nt)],
            got_xy=[[copy(t, 1, xn, me), copy(t, 2, yn, me)] for t in range(nt)],
            got_diag=[copy(t, 3, diag, me) for t in range(nt)],
            got_sib=[[copy(t, 0, sibling, me), copy(t, 4, sib(xn), me), copy(t, 5, sib(yn), me),
                      copy(t, 6, sib(diag), me)] for t in range(nt)])

    def start(self, ins, outs, scr):
        cps = self._copies(outs, scr)
        for t, (_, dt) in enumerate(self.shards):
            scr[t][...] = ins[t][...].astype(dt)
            for cp in [cps["local"][t]] + cps["own"][t]:
                cp.start()

    def stages(self):
        sizes = [s.size * jnp.dtype(dt).itemsize for s, dt in self.shards]
        done = [sum(sizes[:t + 1]) / sum(sizes) for t in range(len(sizes))]
        return [(self.mid_at * f, functools.partial(self.mid, t)) for t, f in enumerate(done)] + [(self.LATE_AT, self.late)]

    LATE_AT = 0.85

    def late(self, ins, outs, scr):
        cps = self._copies(outs, scr)
        for t in range(len(self.shards)):
            cps["got_diag"][t].wait_recv()
            cps["down_diag"][t].start()

    def mid(self, t, ins, outs, scr):
        cps = self._copies(outs, scr)
        for cp in cps["got_xy"][t]:
            cp.wait_recv()
        for cp in [cps["relay"][t]] + cps["down"][t]:
            cp.start()

    def finish(self, ins, outs, scr):
        cps = self._copies(outs, scr)
        for t in range(len(self.shards)):
            for cp in cps["got_sib"][t]:
                cp.wait_recv()
            for cp in cps["own"][t] + [cps["relay"][t]] + cps["down"][t] + [cps["down_diag"][t]]:
                cp.wait_send()
            cps["local"][t].wait()


def _run_carry(carry, name):
    def body(*refs):
        n_in, n_out = len(carry.in_arrays), len(carry.out_shape)
        ins, outs, scr = refs[:n_in], refs[n_in:n_in + n_out], refs[n_in + n_out:]
        carry.start(ins, outs, scr)
        for _, stage in carry.stages():
            stage(ins, outs, scr)
        carry.finish(ins, outs, scr)

    return pl.pallas_call(
        body, in_specs=carry.in_specs, out_specs=carry.out_specs, out_shape=carry.out_shape,
        scratch_shapes=carry.scratch, compiler_params=pltpu.CompilerParams(vmem_limit_bytes=VMEM_LIMIT),
        name=name)(*carry.in_arrays)


class _ExchangeCarry:
    def __init__(self, names, grads, mid_at=0.5):
        nt = len(grads)
        self.mid_at = mid_at
        self.names = names
        self.in_arrays = [g.reshape(4, 2, g.shape[0] // NDEV, g.shape[1]) for g in grads]
        self.in_specs = [HBM] * nt
        blocks = [g.shape[2:] for g in self.in_arrays]
        self.out_shape = [jax.ShapeDtypeStruct((3,) + b, BF16) for b in blocks]
        self.out_specs = [HBM] * nt
        self.scratch = ([pltpu.VMEM((4,) + b, BF16) for b in blocks] * 2 + [pltpu.VMEM(b, BF16) for b in blocks]
                        + [pltpu.SemaphoreType.DMA((nt, 3)), pltpu.SemaphoreType.DMA((nt, 3))]
                        + [pltpu.SemaphoreType.DMA((nt,))] * 4)

    def _copies(self, ins, outs, scr):
        nt = len(ins)
        theirs, own, relayed = scr[:nt], scr[nt:2 * nt], scr[2 * nt:3 * nt]
        send_sems, recv_sems, keep_sems, load_sems, swap_send, swap_recv = scr[3 * nt:]
        x, y, c = _place()
        q = lambda cx, cy: 2 * cx + cy
        near, far = (x ^ c, y ^ (1 - c)), (x ^ (1 - c), y ^ c)

        def remote(t, k, src, dst, chip):
            return pltpu.make_async_remote_copy(
                src_ref=src, dst_ref=dst, send_sem=send_sems.at[t, k], recv_sem=recv_sems.at[t, k],
                device_id=(*chip, c), device_id_type=MESH)

        return dict(
            swap=[pltpu.make_async_remote_copy(
                src_ref=ins[t].at[:, 1 - c], dst_ref=theirs[t], send_sem=swap_send.at[t], recv_sem=swap_recv.at[t],
                device_id=(x, y, 1 - c), device_id_type=MESH) for t in range(nt)],
            load=[pltpu.make_async_copy(ins[t].at[:, c], own[t], load_sems.at[t]) for t in range(nt)],
            keep=[pltpu.make_async_copy(own[t].at[q(x, y)], outs[t].at[0], keep_sems.at[t]) for t in range(nt)],
            direct=[remote(t, 0, own[t].at[q(*near)], outs[t].at[1], near) for t in range(nt)],
            relay=[remote(t, 1, own[t].at[q(1 - x, 1 - y)], relayed[t], near) for t in range(nt)],
            merged=[remote(t, 2, own[t].at[q(*far)], outs[t].at[2], far) for t in range(nt)],
            theirs=theirs, own=own, relayed=relayed, far=q(*far))

    EARLY_AT = 0.1

    def stages(self):
        return [(self.EARLY_AT, self.early), (self.mid_at, self.mid)]

    def start(self, ins, outs, scr):
        cps = self._copies(ins, outs, scr)
        for t in range(len(ins)):
            cps["swap"][t].start()
            cps["load"][t].start()

    def early(self, ins, outs, scr):
        cps = self._copies(ins, outs, scr)
        for t in range(len(ins)):
            cps["load"][t].wait()
            cps["swap"][t].wait_recv()
            own, theirs = cps["own"][t], cps["theirs"][t]
            for j in range(4):
                own[j] = (own[j].astype(F32) + theirs[j].astype(F32)).astype(BF16)
            for kind in ("relay", "direct", "keep"):
                cps[kind][t].start()

    def mid(self, ins, outs, scr):
        cps = self._copies(ins, outs, scr)
        for t in range(len(ins)):
            cps["relay"][t].wait_recv()
            own, far = cps["own"][t], cps["far"]
            own[far] = (own[far].astype(F32) + cps["relayed"][t][...].astype(F32)).astype(BF16)
            cps["merged"][t].start()

    def finish(self, ins, outs, scr):
        cps = self._copies(ins, outs, scr)
        for t in range(len(ins)):
            cps["swap"][t].wait_send()
            cps["direct"][t].wait()
            cps["relay"][t].wait_send()
            cps["merged"][t].wait()
            cps["keep"][t].wait()


class _Comm:
    def __init__(self, groups, opt):
        self.names = {tag: list(g) for tag, (g, _) in groups.items()}
        self.gathers = {tag: _GatherCarry(list(g.values()), mid_at) for tag, (g, mid_at) in groups.items()}
        self.reduced = {}
        self.last = None
        self.opt = opt
        self.updated = {}

    def gathered(self, tag, outs):
        return dict(zip(self.names[tag], outs))

    def reduce_start(self, grads):
        names = list(grads)
        return _ExchangeCarry(names, [grads[n] for n in names])

    def reduce_done(self, carry, outs):
        self.reduced.update(zip(carry.names, outs))

    def update_start(self, names):
        w, m, v = zip(*[self.opt[n] for n in names])
        return _AdamWCarry(names, [self.reduced[n] for n in names], w, m, v)

    def update_done(self, carry, outs):
        self.updated.update({n: outs[4 * k:4 * k + 4] for k, n in enumerate(carry.names)})


def _adamw_math(w, g, m, v):
    m = B1 * m + (1.0 - B1) * g
    v = B2 * v + (1.0 - B2) * (g * g)
    m_hat = m / (1.0 - B1 ** STEP)
    v_hat = v / (1.0 - B2 ** STEP)
    delta = -LR * (m_hat / (jnp.sqrt(v_hat) + AEPS) + WD * w)
    return delta, m, v


def _adamw_big(recvs, ws, ms, vs, name):
    nw = len(ws)

    def body(*refs):
        ins, outs = refs[:4 * nw], refs[4 * nw:]
        for k in range(nw):
            @pl.when(pl.program_id(0) // 2 == k)
            def _(k=k):
                r_ref, w_ref, m_ref, v_ref = ins[4 * k:4 * k + 4]
                g_ref, d_ref, mo_ref, vo_ref = outs[4 * k:4 * k + 4]
                g = r_ref[0].astype(F32)
                for q in range(1, 3):
                    g = g + r_ref[q].astype(F32)
                d, mn, vn = _adamw_math(w_ref[...], g, m_ref[...], v_ref[...])
                g_ref[...] = g
                d_ref[...] = d
                mo_ref[...] = mn
                vo_ref[...] = vn

    in_specs, out_specs, out_shape, args = [], [], [], []
    for k, (r, w, m, v) in enumerate(zip(recvs, ws, ms, vs)):
        rows, n = w.shape
        tr = rows // 2
        tile = lambda s, k=k: jnp.clip(s - 2 * k, 0, 1)
        row = pl.BlockSpec((tr, n), lambda s, tile=tile: (tile(s), 0))
        in_specs += [pl.BlockSpec((3, tr, n), lambda s, tile=tile: (0, tile(s), 0)), row, row, row]
        out_specs += [row] * 4
        out_shape += [jax.ShapeDtypeStruct(w.shape, F32)] * 4
        args += [r, w, m, v]
    res = pl.pallas_call(
        body, grid=(2 * nw,), in_specs=in_specs, out_specs=out_specs, out_shape=out_shape,
        compiler_params=_cp(("arbitrary",)), name=name)(*args)
    return [res[4 * k:4 * k + 4] for k in range(nw)]


class _AdamWCarry:
    def __init__(self, names, recvs, ws, ms, vs):
        self.names = names
        nw = len(ws)
        self.halves = [(k, j, w.shape[0] // 2) for k, w in enumerate(ws) for j in range(2)]
        self.in_arrays = [a for quad in zip(recvs, ws, ms, vs) for a in quad]
        self.in_specs = [HBM] * (4 * nw)
        self.out_shape = [jax.ShapeDtypeStruct(w.shape, F32) for w in ws for _ in range(4)]
        self.out_specs = [HBM] * (4 * nw)
        tr, n = max(h[2] for h in self.halves), ws[0].shape[1]
        self.scratch = [pltpu.VMEM((2, 3, tr, n), BF16), pltpu.VMEM((2, 3, tr, n), F32),
                        pltpu.VMEM((2, 4, tr, n), F32), pltpu.SemaphoreType.DMA((2, 4)),
                        pltpu.SemaphoreType.DMA((2, 4))]

    def _copies(self, c, ins, outs, scr):
        rbuf, fbuf, obuf, in_sems, out_sems = scr
        k, j, tr = self.halves[c]
        s, rows = c % 2, pl.ds(j * tr, tr)
        loads = [pltpu.make_async_copy(ins[4 * k].at[:, rows, :], rbuf.at[s, :, pl.ds(0, tr), :], in_sems.at[s, 0])]
        loads += [pltpu.make_async_copy(ins[4 * k + i].at[rows, :], fbuf.at[s, i - 1, pl.ds(0, tr), :],
                                        in_sems.at[s, i]) for i in range(1, 4)]
        stores = [pltpu.make_async_copy(obuf.at[s, q, pl.ds(0, tr), :], outs[4 * k + q].at[rows, :],
                                        out_sems.at[s, q]) for q in range(4)]
        return loads, stores

    def start(self, ins, outs, scr):
        for c in range(2):
            for cp in self._copies(c, ins, outs, scr)[0]:
                cp.start()

    def stages(self):
        n = len(self.halves)
        return [((c + 1) / (n + 1), functools.partial(self.half, c)) for c in range(n)]

    def half(self, c, ins, outs, scr):
        rbuf, fbuf, obuf = scr[:3]
        _, _, tr = self.halves[c]
        s = c % 2
        loads, stores = self._copies(c, ins, outs, scr)
        for cp in loads:
            cp.wait()
        if c >= 2:
            for cp in self._copies(c - 2, ins, outs, scr)[1]:
                cp.wait()
        g = rbuf[s, 0, 0:tr].astype(F32)
        for q in range(1, 3):
            g = g + rbuf[s, q, 0:tr].astype(F32)
        d, mn, vn = _adamw_math(fbuf[s, 0, 0:tr], g, fbuf[s, 1, 0:tr], fbuf[s, 2, 0:tr])
        for q, val in enumerate((g, d, mn, vn)):
            obuf[s, q, 0:tr] = val
        for cp in stores:
            cp.start()
        if c + 2 < len(self.halves):
            for cp in self._copies(c + 2, ins, outs, scr)[0]:
                cp.start()

    def finish(self, ins, outs, scr):
        n = len(self.halves)
        for c in range(max(n - 2, 0), n):
            for cp in self._copies(c, ins, outs, scr)[1]:
                cp.wait()


SMALL_NAMES = ("ffn1_norm", "mix_norm", "ffn2_norm", "conv_b", "conv_ln_g", "conv_ln_b", "q_norm", "k_norm")
SROWS = 16
LOSS_ROW = len(SMALL_NAMES)
CWT = (32, 1, 128)


def _small_sums(gs, loss_row, gcw, carry=None):
    ns = len(SMALL_NAMES)
    widths = [g.shape[1] for g in gs]

    def body(*refs):
        it = iter(refs)
        take = lambda n: [next(it) for _ in range(n)]
        g_refs, (loss_ref, gcw_ref) = take(ns), take(2)
        cins = take(len(carry.in_arrays)) if carry else []
        tot_ref, ctot_ref = take(2)
        couts = take(len(carry.out_shape)) if carry else []
        send, slots, cslots, send_sems, recv_sems, csend_sems, crecv_sems = take(7)
        cscr = list(it)
        x, y, c = _place()
        me = 4 * x + 2 * y + c
        send[...] = jnp.zeros_like(send)
        for k in range(ns):
            send[k:k + 1, 0:widths[k]] = g_refs[k][...]
        send[LOSS_ROW:LOSS_ROW + 1, 0:128] = loss_ref[...]
        slots[me] = send[...]
        cslots[me] = gcw_ref[me]
        cps = []
        for k in range(1, NDEV):
            peer = (x ^ ((k >> 2) & 1), y ^ ((k >> 1) & 1), c ^ (k & 1))
            cps.append(pltpu.make_async_remote_copy(
                src_ref=send, dst_ref=slots.at[me], send_sem=send_sems.at[k - 1], recv_sem=recv_sems.at[k - 1],
                device_id=peer, device_id_type=MESH))
            cps.append(pltpu.make_async_remote_copy(
                src_ref=gcw_ref.at[4 * peer[0] + 2 * peer[1] + peer[2]], dst_ref=cslots.at[me],
                send_sem=csend_sems.at[k - 1], recv_sem=crecv_sems.at[k - 1], device_id=peer, device_id_type=MESH))
        for cp in cps:
            cp.start()
        stages = [stage for _, stage in carry.stages()] if carry else []
        if carry:
            carry.start(cins, couts, cscr)
        for stage in stages[:1]:
            stage(cins, couts, cscr)
        for cp in cps:
            cp.wait()
        tot = slots[0]
        ctot = cslots[0]
        for j in range(1, NDEV):
            tot = tot + slots[j]
            ctot = ctot + cslots[j]
        tot_ref[...] = tot
        ctot_ref[...] = ctot
        for stage in stages[1:]:
            stage(cins, couts, cscr)
        if carry:
            carry.finish(cins, couts, cscr)

    args = [*gs, loss_row, gcw]
    out_shape = [jax.ShapeDtypeStruct((SROWS, D), F32), jax.ShapeDtypeStruct(CWT, F32)]
    res = pl.pallas_call(
        body, in_specs=[VMEM] * len(args) + (carry.in_specs if carry else []),
        out_specs=[VMEM] * 2 + (carry.out_specs if carry else []),
        out_shape=out_shape + (carry.out_shape if carry else []),
        scratch_shapes=[pltpu.VMEM((SROWS, D), F32), pltpu.VMEM((NDEV, SROWS, D), F32),
                        pltpu.VMEM((NDEV,) + CWT, F32)]
                       + [pltpu.SemaphoreType.DMA((NDEV - 1,))] * 4 + (carry.scratch if carry else []),
        name="small_sums")(*args, *(carry.in_arrays if carry else []))
    return res[0], res[1], res[2:]


def _adamw_small(tot, ctot, ws, ms, vs, wcw, mcw, vcw):
    ns = len(SMALL_NAMES)
    widths = [w.shape[1] for w in ws]

    def body(*refs):
        it = iter(refs)
        take = lambda n: [next(it) for _ in range(n)]
        (tot_ref, ctot_ref), w_refs, m_refs, v_refs = take(2), take(ns), take(ns), take(ns)
        wcw_ref, mcw_ref, vcw_ref = take(3)
        outs = [take(4) for _ in range(ns)]
        cw_outs, (loss_out,) = take(4), take(1)

        def step(g, w_ref, m_ref, v_ref, o):
            d, mn, vn = _adamw_math(w_ref[...], g, m_ref[...], v_ref[...])
            o[0][...], o[1][...], o[2][...], o[3][...] = g, d, mn, vn

        for k in range(ns):
            step(tot_ref[k:k + 1, 0:widths[k]], w_refs[k], m_refs[k], v_refs[k], outs[k])
        step(ctot_ref[0:CK, :, 0:HD], wcw_ref, mcw_ref, vcw_ref, cw_outs)
        loss_out[...] = tot_ref[LOSS_ROW:LOSS_ROW + 1, 0:128]

    args = [tot, ctot, *ws, *ms, *vs, wcw, mcw, vcw]
    out_shape = ([jax.ShapeDtypeStruct((1, n), F32) for n in widths for _ in range(4)]
                 + [jax.ShapeDtypeStruct((CK, 1, HD), F32)] * 4 + [jax.ShapeDtypeStruct((1, 128), F32)])
    res = pl.pallas_call(
        body, in_specs=[VMEM] * len(args), out_specs=[VMEM] * len(out_shape), out_shape=out_shape,
        name="adamw_small")(*args)
    per = [res[4 * k:4 * k + 4] for k in range(ns)]
    return per, res[4 * ns:4 * ns + 4], res[-1]


def kernel(x, ffn1_norm, ffn1_w_gate, ffn1_w_up, ffn1_w_down, mix_norm, w_in, q_norm, k_norm, conv_w, conv_b, conv_ln_g, conv_ln_b, w_out, ffn2_norm, ffn2_w_gate, ffn2_w_up, ffn2_w_down, loss_target, m_ffn1_norm, m_ffn1_w_gate, m_ffn1_w_up, m_ffn1_w_down, m_mix_norm, m_w_in, m_q_norm, m_k_norm, m_conv_w, m_conv_b, m_conv_ln_g, m_conv_ln_b, m_w_out, m_ffn2_norm, m_ffn2_w_gate, m_ffn2_w_up, m_ffn2_w_down, v_ffn1_norm, v_ffn1_w_gate, v_ffn1_w_up, v_ffn1_w_down, v_mix_norm, v_w_in, v_q_norm, v_k_norm, v_conv_w, v_conv_b, v_conv_ln_g, v_conv_ln_b, v_w_out, v_ffn2_norm, v_ffn2_w_gate, v_ffn2_w_up, v_ffn2_w_down):
    P = dict(ffn1_norm=ffn1_norm, ffn1_w_gate=ffn1_w_gate, ffn1_w_up=ffn1_w_up, ffn1_w_down=ffn1_w_down,
             mix_norm=mix_norm, w_in=w_in, q_norm=q_norm, k_norm=k_norm, conv_w=conv_w, conv_b=conv_b,
             conv_ln_g=conv_ln_g, conv_ln_b=conv_ln_b, w_out=w_out, ffn2_norm=ffn2_norm,
             ffn2_w_gate=ffn2_w_gate, ffn2_w_up=ffn2_w_up, ffn2_w_down=ffn2_w_down)
    M = dict(ffn1_norm=m_ffn1_norm, ffn1_w_gate=m_ffn1_w_gate, ffn1_w_up=m_ffn1_w_up, ffn1_w_down=m_ffn1_w_down,
             mix_norm=m_mix_norm, w_in=m_w_in, q_norm=m_q_norm, k_norm=m_k_norm, conv_w=m_conv_w, conv_b=m_conv_b,
             conv_ln_g=m_conv_ln_g, conv_ln_b=m_conv_ln_b, w_out=m_w_out, ffn2_norm=m_ffn2_norm,
             ffn2_w_gate=m_ffn2_w_gate, ffn2_w_up=m_ffn2_w_up, ffn2_w_down=m_ffn2_w_down)
    V = dict(ffn1_norm=v_ffn1_norm, ffn1_w_gate=v_ffn1_w_gate, ffn1_w_up=v_ffn1_w_up, ffn1_w_down=v_ffn1_w_down,
             mix_norm=v_mix_norm, w_in=v_w_in, q_norm=v_q_norm, k_norm=v_k_norm, conv_w=v_conv_w, conv_b=v_conv_b,
             conv_ln_g=v_conv_ln_g, conv_ln_b=v_conv_ln_b, w_out=v_w_out, ffn2_norm=v_ffn2_norm,
             ffn2_w_gate=v_ffn2_w_gate, ffn2_w_up=v_ffn2_w_up, ffn2_w_down=v_ffn2_w_down)
    order = ["ffn1_norm", "ffn1_w_gate", "ffn1_w_up", "ffn1_w_down", "mix_norm", "w_in", "q_norm", "k_norm",
             "conv_w", "conv_b", "conv_ln_g", "conv_ln_b", "w_out", "ffn2_norm", "ffn2_w_gate", "ffn2_w_up",
             "ffn2_w_down"]
    B, S, _ = x.shape
    T = B * S

    bigs = [("wg1", "ffn1_w_gate", True), ("wu1", "ffn1_w_up", True), ("wd1", "ffn1_w_down", False),
            ("win", "w_in", True), ("wout", "w_out", False),
            ("wg2", "ffn2_w_gate", True), ("wu2", "ffn2_w_up", True), ("wd2", "ffn2_w_down", False)]
    hm = lambda a, tr: jnp.transpose(a[0]) if tr else a[0]
    cw_pad = jnp.zeros((32, 128), F32).at[0:CK, 0:HD].set(conv_w[0])
    shard = {ln: (hm(P[pn], tr), BF16) for ln, pn, tr in bigs}
    gathered = _run_carry(_GatherCarry([shard["wg1"], shard["wu1"], (cw_pad, F32)]), "gather_first")
    W = {"wg1": gathered[0], "wu1": gathered[1]}
    cwg = gathered[2].reshape(NDEV, 32, 128)[:, 0:CK, 0:HD]
    W["conv_w"] = jnp.transpose(cwg, (1, 0, 2)).reshape(CK, DC)
    norms = {n: P[n] for n in SMALL_NAMES}
    comm = _Comm({"down_in": ({n: shard[n] for n in ("wd1", "win")}, 0.5),
                  "ffn2": ({n: shard[n] for n in ("wg2", "wu2", "wd2", "wout")}, 0.5)},
                 opt={ln: (hm(P[pn], tr), hm(M[pn], tr), hm(V[pn], tr)) for ln, pn, tr in bigs})

    loss_part, gx, _, small = _local_step(x.reshape(T, D), loss_target.reshape(T, D), norms, W, B, S, comm)

    G, Dl, Mn, Vn = {}, {}, {}, {}
    dcw = small["conv_w"].reshape(CK, NDEV, HD).transpose(1, 0, 2)
    loss_row = jnp.zeros((1, 128), F32).at[0, 0].set(loss_part)
    gcw = jnp.pad(dcw[:, :, None, :], ((0, 0), (0, CWT[0] - CK), (0, 0), (0, CWT[2] - HD)))
    taps_first = lambda a: jnp.transpose(a, (1, 0, 2))
    tot, ctot, got = _small_sums([small[n] for n in SMALL_NAMES], loss_row, gcw, carry=comm.last)
    comm.reduce_done(comm.last, got)
    per, cw_outs, loss_out = _adamw_small(
        tot, ctot, [P[n] for n in SMALL_NAMES], [M[n] for n in SMALL_NAMES], [V[n] for n in SMALL_NAMES],
        taps_first(P["conv_w"]), taps_first(M["conv_w"]), taps_first(V["conv_w"]))
    loss = loss_out[0, 0]
    for n, outs in zip(SMALL_NAMES, per):
        G[n], Dl[n], Mn[n], Vn[n] = outs
    G["conv_w"], Dl["conv_w"], Mn["conv_w"], Vn["conv_w"] = [taps_first(o) for o in cw_outs]

    group = ("wd1", "wg1", "wu1")
    w, m, v = zip(*[comm.opt[ln] for ln in group])
    comm.updated.update(zip(group, _adamw_big([comm.reduced[ln] for ln in group], w, m, v, "adamw_ffn1")))
    for ln, pn, tr in bigs:
        G[pn], Dl[pn], Mn[pn], Vn[pn] = [(jnp.transpose(o) if tr else o)[None] for o in comm.updated[ln]]

    return (loss, gx.reshape(B, S, D), *[G[n] for n in order], *[Dl[n] for n in order],
            *[Mn[n] for n in order], *[Vn[n] for n in order])
```

```python
import functools

import jax
import jax.numpy as jnp
from jax import lax
from jax.experimental import pallas as pl
from jax.experimental.pallas import tpu as pltpu

F32 = jnp.float32
BF16 = jnp.bfloat16

D = 1024
FF = 2816
HD = 64
DA = 512
DC = 512
DIN = 2560
CK = 31
BLK = 128
DILS = (1, 4, 16)
EPS = 1e-6
NDEV = 8
MESH = pl.DeviceIdType.MESH

LR, B1, B2, AEPS, WD, STEP = 0.001, 0.9, 0.999, 1e-08, 0.01, 10

NT = (((1,), (1,)), ((), ()))
TN = (((0,), (0,)), ((), ()))

VMEM_LIMIT = 60 * 1024 * 1024


def _cp(sem=None):
    return pltpu.CompilerParams(dimension_semantics=sem, vmem_limit_bytes=VMEM_LIMIT)


def _sigmoid(x):
    return 0.5 * (jnp.tanh(0.5 * x) + 1.0)


def _pallas(body, args, *, grid, in_specs, out_specs, out_shape, scratch_shapes, sem, name, carry=None):
    if carry is None:
        outs = pl.pallas_call(body, grid=grid, in_specs=in_specs, out_specs=out_specs, out_shape=out_shape,
                              scratch_shapes=scratch_shapes, compiler_params=_cp(sem), name=name)(*args)
        return outs, None
    n_in, n_out, n_scr = len(in_specs), len(out_shape), len(scratch_shapes)
    c_in, c_out = len(carry.in_arrays), len(carry.out_shape)

    def wrapped(*refs):
        ins, refs = refs[:n_in], refs[n_in:]
        cins, refs = refs[:c_in], refs[c_in:]
        outs, refs = refs[:n_out], refs[n_out:]
        couts, refs = refs[:c_out], refs[c_out:]
        scr, cscr = refs[:n_scr], refs[n_scr:]
        ids = [pl.program_id(a) for a in range(len(grid))]
        step = ids[0]
        for i, n in zip(ids[1:], grid[1:]):
            step = step * n + i
        steps = functools.reduce(lambda a, b: a * b, grid)

        @pl.when(step == 0)
        def _():
            carry.start(cins, couts, cscr)

        body(*ins, *outs, *scr)

        for frac, stage in carry.stages():
            @pl.when(step == int(steps * frac))
            def _(stage=stage):
                stage(cins, couts, cscr)

        @pl.when(step == steps - 1)
        def _():
            carry.finish(cins, couts, cscr)

    outs = pl.pallas_call(
        wrapped, grid=grid, in_specs=list(in_specs) + carry.in_specs, out_specs=list(out_specs) + carry.out_specs,
        out_shape=list(out_shape) + carry.out_shape, scratch_shapes=list(scratch_shapes) + carry.scratch,
        compiler_params=_cp(("arbitrary",) * len(grid)), name=name)(*args, *carry.in_arrays)
    return outs[:n_out], outs[n_out:]


FC = 256


def _resident(shape):
    return pl.BlockSpec(shape, lambda *_: (0,) * len(shape), pipeline_mode=pl.Buffered(1))


def _mix_out_ffn_loss(h1, attn, conv, wout, gain, wg, wu, wd, target, name):
    T = h1.shape[0]
    tm = 512
    nt = T // tm

    def body(h1_ref, at_ref, cv_ref, wo_ref, gain_ref, wg_ref, wu_ref, wd_ref, t_ref,
             h2_ref, n_ref, g_ref, u_ref, dout_ref, dyb_ref, sq_ref, a_hbm, a_scr, a_sem):
        t = pl.program_id(0)
        a_out = lambda i: pltpu.make_async_copy(a_scr, a_hbm.at[pl.ds(pl.multiple_of(i * tm, tm), tm), :], a_sem)

        @pl.when(t > 0)
        def _():
            a_out(t - 1).wait()

        xv = (h1_ref[...]
              + jnp.dot(at_ref[...], wo_ref[0:DA, :], preferred_element_type=F32)
              + jnp.dot(cv_ref[...], wo_ref[DA:D, :], preferred_element_type=F32))
        h2_ref[...] = xv
        r = lax.rsqrt(jnp.mean(xv * xv, axis=-1, keepdims=True) + EPS)
        n_ref[...] = (xv * r * gain_ref[...]).astype(BF16)
        for c in range(FF // FC):
            cols = slice(c * FC, (c + 1) * FC)
            nb = n_ref[...]
            g = lax.dot_general(nb, wg_ref[cols, :], NT, preferred_element_type=F32)
            u = lax.dot_general(nb, wu_ref[cols, :], NT, preferred_element_type=F32)
            g_ref[:, cols] = g.astype(BF16)
            u_ref[:, cols] = u.astype(BF16)
            a_scr[:, cols] = (g * _sigmoid(g) * u).astype(BF16)
        a_out(t).start()
        e = h2_ref[...] + 0.5 * jnp.dot(a_scr[...], wd_ref[...], preferred_element_type=F32) - t_ref[...]
        dout = e * (1.0 / D)
        dout_ref[...] = dout
        dyb_ref[...] = (0.5 * dout).astype(BF16)
        sq_ref[...] = jnp.sum(e * e, axis=0, keepdims=True)[None]

        @pl.when(t == nt - 1)
        def _():
            a_out(t).wait()

    row = pl.BlockSpec((tm, D), lambda t: (t, 0))
    half = pl.BlockSpec((tm, DA), lambda t: (t, 0))
    wide = pl.BlockSpec((tm, FF), lambda t: (t, 0))
    outs, _ = _pallas(
        body, (h1, attn, conv, wout, gain, wg, wu, wd, target), grid=(nt,),
        in_specs=[row, half, half, _resident((D, D)), _resident((1, D)), _resident((FF, D)), _resident((FF, D)),
                  _resident((FF, D)), row],
        out_specs=[row, row, wide, wide, row, row, pl.BlockSpec((1, 1, D), lambda t: (t, 0, 0)), HBM],
        out_shape=[jax.ShapeDtypeStruct((T, D), F32), jax.ShapeDtypeStruct((T, D), BF16)]
                  + [jax.ShapeDtypeStruct((T, FF), BF16)] * 2
                  + [jax.ShapeDtypeStruct((T, D), F32), jax.ShapeDtypeStruct((T, D), BF16),
                     jax.ShapeDtypeStruct((nt, 1, D), F32), jax.ShapeDtypeStruct((T, FF), BF16)],
        scratch_shapes=[pltpu.VMEM((tm, FF), BF16), pltpu.SemaphoreType.DMA(())],
        sem=("arbitrary",), name=name)
    return outs


def _ffn_gate_up(x, gain, wg, wu, name, carry=None):
    T = x.shape[0]
    tm = 512

    def body(x_ref, gain_ref, wg_ref, wu_ref, n_ref, g_ref, u_ref, a_ref):
        xv = x_ref[...]
        r = lax.rsqrt(jnp.mean(xv * xv, axis=-1, keepdims=True) + EPS)
        n_ref[...] = (xv * r * gain_ref[...]).astype(BF16)
        for c in range(FF // FC):
            cols = slice(c * FC, (c + 1) * FC)
            nb = n_ref[...]
            g = lax.dot_general(nb, wg_ref[cols, :], NT, preferred_element_type=F32)
            u = lax.dot_general(nb, wu_ref[cols, :], NT, preferred_element_type=F32)
            g_ref[:, cols] = g.astype(BF16)
            u_ref[:, cols] = u.astype(BF16)
            a_ref[:, cols] = (g * _sigmoid(g) * u).astype(BF16)

    row = pl.BlockSpec((tm, D), lambda t: (t, 0))
    wide = pl.BlockSpec((tm, FF), lambda t: (t, 0))
    return _pallas(
        body, (x, gain, wg, wu), grid=(T // tm,),
        in_specs=[row, _resident((1, D)), _resident((FF, D)), _resident((FF, D))],
        out_specs=[row, wide, wide, wide],
        out_shape=[jax.ShapeDtypeStruct((T, D), BF16)] + [jax.ShapeDtypeStruct((T, FF), BF16)] * 3,
        scratch_shapes=[], sem=("parallel",), name=name, carry=carry)


def _ffn_down_mix_in(x, a, wd, gain, win, name):
    T = x.shape[0]
    tm = 512

    def body(x_ref, a_ref, wd_ref, gain_ref, win_ref, h_ref, u_ref, n_ref):
        hv = x_ref[...] + 0.5 * jnp.dot(a_ref[...], wd_ref[...], preferred_element_type=F32)
        h_ref[...] = hv
        r = lax.rsqrt(jnp.mean(hv * hv, axis=-1, keepdims=True) + EPS)
        n_ref[...] = (hv * r * gain_ref[...]).astype(BF16)
        u_ref[...] = lax.dot_general(n_ref[...], win_ref[...], NT, preferred_element_type=F32).astype(BF16)

    row = pl.BlockSpec((tm, D), lambda t: (t, 0))
    wide = pl.BlockSpec((tm, FF), lambda t: (t, 0))
    outs, _ = _pallas(
        body, (x, a, wd, gain, win), grid=(T // tm,),
        in_specs=[row, wide, _resident((FF, D)), _resident((1, D)), _resident((DIN, D))],
        out_specs=[row, pl.BlockSpec((tm, DIN), lambda t: (t, 0)), row],
        out_shape=[jax.ShapeDtypeStruct((T, D), F32), jax.ShapeDtypeStruct((T, DIN), BF16),
                   jax.ShapeDtypeStruct((T, D), BF16)],
        scratch_shapes=[], sem=("parallel",), name=name)
    return outs


def _ffn_bwd_act(dyb, g, u, x, dout, gain, wg, wu, wd, name, carry=None):
    T = x.shape[0]
    tm = 256
    nt = T // tm

    def body(dy_ref, g_ref, u_ref, x_ref, dout_ref, gain_ref, wg_ref, wu_ref, wd_ref,
             dg_ref, du_ref, dx_ref, dgn_ref):
        for c in range(FF // FC):
            cols = slice(c * FC, (c + 1) * FC)
            da = lax.dot_general(dy_ref[...], wd_ref[cols, :], NT, preferred_element_type=F32)
            gv = g_ref[:, cols].astype(F32)
            uv = u_ref[:, cols].astype(F32)
            sg = _sigmoid(gv)
            dg_ref[:, cols] = (da * uv * (sg * (1.0 + gv * (1.0 - sg)))).astype(BF16)
            du_ref[:, cols] = (da * (gv * sg)).astype(BF16)
        dn = (jnp.dot(dg_ref[...], wg_ref[...], preferred_element_type=F32)
              + jnp.dot(du_ref[...], wu_ref[...], preferred_element_type=F32))
        dx, dgain = _rms_bwd_rows(dn, x_ref[...], gain_ref[...])
        dx_ref[...] = dout_ref[...] + dx

        @pl.when(pl.program_id(0) == 0)
        def _():
            dgn_ref[...] = jnp.zeros_like(dgn_ref)

        dgn_ref[...] += dgain[None]

    row = pl.BlockSpec((tm, D), lambda t: (t, 0))
    wide = pl.BlockSpec((tm, FF), lambda t: (t, 0))
    return _pallas(
        body, (dyb, g, u, x, dout, gain, wg, wu, wd), grid=(nt,),
        in_specs=[row, wide, wide, row, row, _resident((1, D)), _resident((FF, D)), _resident((FF, D)),
                  _resident((FF, D))],
        out_specs=[wide, wide, row, pl.BlockSpec((1, 1, D), lambda t: (0, 0, 0))],
        out_shape=[jax.ShapeDtypeStruct((T, FF), BF16)] * 2
                  + [jax.ShapeDtypeStruct((T, D), F32), jax.ShapeDtypeStruct((1, 1, D), F32)],
        scratch_shapes=[], sem=("arbitrary",), name=name, carry=carry)


def _ffn_bwd_w(lhs, rhs, name, carry=None):
    T = rhs.shape[0]
    tf = 256

    def body(l_ref, r_ref, dw_ref):
        dw_ref[...] = lax.dot_general(l_ref[...], r_ref[...], TN, preferred_element_type=F32).astype(BF16)

    (dw,), got = _pallas(
        body, (lhs, rhs), grid=(FF // tf,),
        in_specs=[pl.BlockSpec((T, tf), lambda f: (0, f)), _resident((T, D))],
        out_specs=[pl.BlockSpec((tf, D), lambda f: (f, 0))], out_shape=[jax.ShapeDtypeStruct((FF, D), BF16)],
        scratch_shapes=[], sem=("parallel",), name=name, carry=carry)
    return dw, got


def _rms_bwd_rows(dn, xv, gain):
    r = lax.rsqrt(jnp.mean(xv * xv, axis=-1, keepdims=True) + EPS)
    xhat = xv * r
    dxhat = dn * gain
    dx = r * (dxhat - xhat * jnp.mean(dxhat * xhat, axis=-1, keepdims=True))
    return dx, jnp.sum(dn * xhat, axis=0, keepdims=True)


def _mix_out_bwd(dh, attn, conv, wout):
    T = dh.shape[0]
    tm = 512
    nt = T // tm

    def body(dh_ref, a_ref, c_ref, w_ref, da_ref, dc_ref, dw_ref, acc_scr):
        t = pl.program_id(0)

        @pl.when(t == 0)
        def _():
            acc_scr[...] = jnp.zeros_like(acc_scr)

        dhb = dh_ref[...].astype(BF16)
        dmix = lax.dot_general(dhb, w_ref[...], NT, preferred_element_type=F32)
        da_ref[...] = dmix[:, 0:DA].astype(BF16)
        dc_ref[...] = dmix[:, DA:D].astype(BF16)
        acc_scr[0:DA, :] += lax.dot_general(a_ref[...], dhb, TN, preferred_element_type=F32)
        acc_scr[DA:D, :] += lax.dot_general(c_ref[...], dhb, TN, preferred_element_type=F32)

        @pl.when(t == nt - 1)
        def _():
            dw_ref[...] = acc_scr[...].astype(BF16)

    row = pl.BlockSpec((tm, D), lambda t: (t, 0))
    half = pl.BlockSpec((tm, DA), lambda t: (t, 0))
    full = pl.BlockSpec((D, D), lambda t: (0, 0))
    return pl.pallas_call(
        body, grid=(nt,), in_specs=[row, half, half, full], out_specs=[half, half, full],
        out_shape=[jax.ShapeDtypeStruct((T, DA), BF16)] * 2 + [jax.ShapeDtypeStruct((D, D), BF16)],
        scratch_shapes=[pltpu.VMEM((D, D), F32)],
        compiler_params=_cp(("arbitrary",)), name="mix_out_bwd")(dh, attn, conv, wout)


def _mix_in_bwd(dparts, win, nb, h, dh, gain):
    T = h.shape[0]
    tm = 512
    nt = T // tm

    def body(d0, d1, d2, d3, d4, w_ref, n_ref, h_ref, dh_ref, gain_ref,
             dw_ref, dx_ref, dyb_ref, dg_ref, acc_scr):
        t = pl.program_id(0)

        @pl.when(t == 0)
        def _():
            acc_scr[...] = jnp.zeros_like(acc_scr)

        n = n_ref[...]
        dn = jnp.zeros((tm, D), F32)
        for i, d_ref in enumerate((d0, d1, d2, d3, d4)):
            dv = d_ref[...]
            dn = dn + jnp.dot(dv, w_ref[i * DA:(i + 1) * DA, :], preferred_element_type=F32)
            acc_scr[i * DA:(i + 1) * DA, :] += lax.dot_general(dv, n, TN, preferred_element_type=F32)
        dx, dgain = _rms_bwd_rows(dn, h_ref[...], gain_ref[...])
        tot = dh_ref[...] + dx
        dx_ref[...] = tot
        dyb_ref[...] = (0.5 * tot).astype(BF16)
        dg_ref[...] = dgain[None]

        @pl.when(t == nt - 1)
        def _():
            dw_ref[...] = acc_scr[...].astype(BF16)

    row = pl.BlockSpec((tm, D), lambda t: (t, 0))
    half = pl.BlockSpec((tm, DA), lambda t: (t, 0))
    full = pl.BlockSpec((DIN, D), lambda t: (0, 0))
    return pl.pallas_call(
        body, grid=(nt,),
        in_specs=[half] * 5 + [full, row, row, row, pl.BlockSpec((1, D), lambda t: (0, 0))],
        out_specs=[full, row, row, pl.BlockSpec((1, 1, D), lambda t: (t, 0, 0))],
        out_shape=[jax.ShapeDtypeStruct((DIN, D), BF16), jax.ShapeDtypeStruct((T, D), F32),
                   jax.ShapeDtypeStruct((T, D), BF16), jax.ShapeDtypeStruct((nt, 1, D), F32)],
        scratch_shapes=[pltpu.VMEM((DIN, D), F32)],
        compiler_params=_cp(("arbitrary",)), name="mix_in_bwd")(*dparts, win, nb, h, dh, gain)


def _head_masks():
    lane = lax.broadcasted_iota(jnp.int32, (1, 2 * HD), 1)
    m0 = lane < HD
    return m0, jnp.logical_not(m0)


def _stack_heads(v, m0, m1):
    z = jnp.zeros_like(v)
    return jnp.concatenate([jnp.where(m0, v, z), jnp.where(m1, v, z)], axis=0)


def _unstack_heads(v2, m0):
    return jnp.where(m0, v2[0:BLK], v2[BLK:2 * BLK])


def _head_sums(xv):
    ri = lax.broadcasted_iota(jnp.int32, (2 * HD, 2 * HD), 0)
    ci = lax.broadcasted_iota(jnp.int32, (2 * HD, 2 * HD), 1)
    ones = jnp.where((ri < HD) == (ci < HD), 1.0, 0.0).astype(BF16)
    hi = xv.astype(BF16)
    lo = (xv - hi.astype(F32)).astype(BF16)
    return (jnp.dot(hi, ones, preferred_element_type=F32) + jnp.dot(lo, ones, preferred_element_type=F32))


def _head_rms(xv):
    return lax.rsqrt(_head_sums(xv * xv) * (1.0 / HD) + EPS)


def _band_mask(first):
    qi = lax.broadcasted_iota(jnp.int32, (BLK, 2 * BLK), 0)
    ci = lax.broadcasted_iota(jnp.int32, (BLK, 2 * BLK), 1)
    band = (ci >= qi) & (ci <= qi + BLK)
    return band & ((ci >= BLK) | jnp.logical_not(first))


def _block_rows(j, d, seg):
    r, n = j // seg, j % seg
    start = r + (d * BLK) * n
    first = n == 0
    prev = jnp.where(first, start, start - d * BLK)
    return pl.ds(start, BLK, stride=d), pl.ds(prev, BLK, stride=d), first


def _block_keys(refs, cur, prev, first, single):
    if single:
        qi = lax.broadcasted_iota(jnp.int32, (BLK, BLK), 0)
        ci = lax.broadcasted_iota(jnp.int32, (BLK, BLK), 1)
        return [r[cur, :].astype(BF16) for r in refs], ci <= qi
    return ([jnp.concatenate([r[prev, :], r[cur, :]], axis=0).astype(BF16) for r in refs], _band_mask(first))


def _attn_fwd(u, qg2, kg2, B, S, carry=None):
    T = B * S
    NB = S // BLK
    scale = HD ** -0.5

    def body(q_ref, k_ref, v_ref, qg_ref, kg_ref, o_ref, lse_ref, qn, kn, vn, os_, ls_):
        m0, m1 = _head_masks()
        qv = q_ref[...].astype(F32)
        qn[...] = qv * _head_rms(qv) * (qg_ref[...] * scale)
        kv = k_ref[...].astype(F32)
        kn[...] = kv * _head_rms(kv) * kg_ref[...]
        vn[...] = v_ref[...].astype(F32)

        for i, d in enumerate(DILS):
            seg = NB // d

            def blk(j, c, i=i, d=d, seg=seg):
                cur, prev, first = _block_rows(j, d, seg)
                q2 = _stack_heads(qn[cur, :].astype(BF16), m0, m1)
                (kk, vv), mask = _block_keys((kn, vn), cur, prev, first, False)
                s = lax.dot_general(q2, kk, NT, preferred_element_type=F32)
                s = jnp.where(jnp.concatenate([mask, mask], axis=0), s, -1e30)
                mx = jnp.max(s, axis=-1, keepdims=True)
                p = jnp.exp(s - mx)
                l = jnp.sum(p, axis=-1, keepdims=True)
                o2 = jnp.dot((p * (1.0 / l)).astype(BF16), vv, preferred_element_type=F32)
                os_[i, cur, :] = _unstack_heads(o2, m0)
                ls_[i, cur, :] = _unstack_heads(mx + jnp.log(l), m0)
                return c

            lax.fori_loop(0, NB, blk, 0, unroll=8)

        def comb(c, carry):
            rows = pl.ds(pl.multiple_of(c * 256, 256), 256)
            l0, l1, l2 = ls_[0, rows, :], ls_[1, rows, :], ls_[2, rows, :]
            mx = jnp.maximum(jnp.maximum(l0, l1), l2)
            e0, e1, e2 = jnp.exp(l0 - mx), jnp.exp(l1 - mx), jnp.exp(l2 - mx)
            tot = e0 + e1 + e2
            inv = 1.0 / tot
            o = (e0 * os_[0, rows, :] + e1 * os_[1, rows, :] + e2 * os_[2, rows, :]) * inv
            o_ref[rows, :] = o.astype(BF16)
            lse_ref[rows, :] = mx + jnp.log(tot)
            return carry

        lax.fori_loop(0, S // 256, comb, 0)

    pair = 2 * HD
    blk_spec = lambda off: pl.BlockSpec((S, pair), lambda b, p, off=off: (b, off + p))
    gspec = pl.BlockSpec((1, pair), lambda b, p: (0, 0))
    return _pallas(
        body, (u, u, u, qg2, kg2), grid=(B, DA // pair),
        in_specs=[blk_spec(0), blk_spec(DA // pair), blk_spec(2 * DA // pair), gspec, gspec],
        out_specs=[blk_spec(0), blk_spec(0)],
        out_shape=[jax.ShapeDtypeStruct((T, DA), BF16), jax.ShapeDtypeStruct((T, DA), F32)],
        scratch_shapes=[pltpu.VMEM((S, pair), F32)] * 3 + [pltpu.VMEM((3, S, pair), F32)] * 2,
        sem=("parallel", "parallel"), name="attn_fwd", carry=carry)


def _attn_bwd(u, attn, dattn, lse, qg2, kg2, B, S, carry=None):
    T = B * S
    NB = S // BLK
    scale = HD ** -0.5
    pair = 2 * HD

    def body(q_ref, k_ref, v_ref, o_ref, do_ref, lse_ref, qg_ref, kg_ref,
             dq_ref, dk_ref, dv_ref, dgn_ref,
             qn, kn, vn, don, ldl, accq, acck, accv, rq, rk):
        m0, m1 = _head_masks()
        lane = lax.broadcasted_iota(jnp.int32, (1, pair), 1)
        qv = q_ref[...].astype(F32)
        rq[...] = _head_rms(qv)
        qn[...] = qv * rq[...] * (qg_ref[...] * scale)
        kv = k_ref[...].astype(F32)
        rk[...] = _head_rms(kv)
        kn[...] = kv * rk[...] * kg_ref[...]
        vn[...] = v_ref[...].astype(F32)
        dov = do_ref[...].astype(F32)
        don[...] = dov
        ldl[...] = jnp.where((lane % HD) < HD // 2, lse_ref[...], _head_sums(dov * o_ref[...].astype(F32)))

        for i, d in enumerate(DILS):
            seg = NB // d

            def blk(j, c, i=i, d=d, seg=seg):
                cur, prev, first = _block_rows(j, d, seg)
                q2 = _stack_heads(qn[cur, :].astype(BF16), m0, m1)
                do2 = _stack_heads(don[cur, :].astype(BF16), m0, m1)
                (kk, vv), mask = _block_keys((kn, vn), cur, prev, first, seg == 1)
                ldv = ldl[cur, :]
                lse2 = jnp.concatenate([ldv[:, 0:1], ldv[:, HD:HD + 1]], axis=0)
                dl2 = jnp.concatenate([ldv[:, HD // 2:HD // 2 + 1], ldv[:, HD + HD // 2:HD + HD // 2 + 1]], axis=0)
                s = lax.dot_general(q2, kk, NT, preferred_element_type=F32)
                p = jnp.where(jnp.concatenate([mask, mask], axis=0), jnp.exp(s - lse2), 0.0)
                dp = lax.dot_general(do2, vv, NT, preferred_element_type=F32)
                ds = (p * (dp - dl2)).astype(BF16)
                dq_acc = _unstack_heads(jnp.dot(ds, kk, preferred_element_type=F32), m0)
                dk_acc = lax.dot_general(ds, q2, TN, preferred_element_type=F32)
                dv_acc = lax.dot_general(p.astype(BF16), do2, TN, preferred_element_type=F32)
                if i == 0:
                    accq[cur, :] = dq_acc
                    acck[cur, :] = dk_acc[BLK:2 * BLK]
                    accv[cur, :] = dv_acc[BLK:2 * BLK]
                    acck[prev, :] += dk_acc[0:BLK]
                    accv[prev, :] += dv_acc[0:BLK]
                elif seg == 1:
                    accq[cur, :] += dq_acc
                    acck[cur, :] += dk_acc
                    accv[cur, :] += dv_acc
                else:
                    accq[cur, :] += dq_acc
                    acck[prev, :] += dk_acc[0:BLK]
                    acck[cur, :] += dk_acc[BLK:2 * BLK]
                    accv[prev, :] += dv_acc[0:BLK]
                    accv[cur, :] += dv_acc[BLK:2 * BLK]
                return c

            lax.fori_loop(0, NB, blk, 0, unroll=8)

        def norm_bwd(x_ref, r_ref, dn, gain):
            r = r_ref[...]
            xhat = x_ref[...].astype(F32) * r
            dxhat = dn * gain
            dx = r * (dxhat - xhat * (_head_sums(dxhat * xhat) * (1.0 / HD)))
            return dx, jnp.sum(dn * xhat, axis=0, keepdims=True)

        dq, dgq = norm_bwd(q_ref, rq, accq[...], qg_ref[...] * scale)
        dk, dgk = norm_bwd(k_ref, rk, acck[...], kg_ref[...])
        dq_ref[...] = dq.astype(BF16)
        dk_ref[...] = dk.astype(BF16)
        dv_ref[...] = accv[...].astype(BF16)
        dgn_ref[...] = jnp.concatenate([dgq * scale, dgk, jnp.zeros((6, pair), F32)], axis=0)[None]

    blk_spec = lambda off: pl.BlockSpec((S, pair), lambda b, p, off=off: (b, off + p))
    gspec = pl.BlockSpec((1, pair), lambda b, p: (0, 0))
    np_ = DA // pair
    return _pallas(
        body, (u, u, u, attn, dattn, lse, qg2, kg2), grid=(B, np_),
        in_specs=[blk_spec(0), blk_spec(np_), blk_spec(2 * np_), blk_spec(0), blk_spec(0), blk_spec(0),
                  gspec, gspec],
        out_specs=[blk_spec(0), blk_spec(0), blk_spec(0),
                   pl.BlockSpec((1, 8, pair), lambda b, p: (b * np_ + p, 0, 0))],
        out_shape=[jax.ShapeDtypeStruct((T, DA), BF16)] * 3 + [jax.ShapeDtypeStruct((B * np_, 8, pair), F32)],
        scratch_shapes=[pltpu.VMEM((S, pair), F32)] * 10,
        sem=("parallel", "parallel"), name="attn_bwd", carry=carry)


CT = 32
CPAD = 32


def _shifted(win, offsets):
    rolled, out = {}, {}
    n = win.shape[0]
    for o in offsets:
        sub = o % 8
        if sub not in rolled:
            rolled[sub] = win if sub == 0 else pltpu.roll(win, n - sub, 0)
        out[o] = rolled[sub][o - sub:o - sub + CT, :]
    return out


def _ln_fwd(y, g, b):
    mu = jnp.mean(y, axis=-1, keepdims=True)
    yc = y - mu
    rstd = lax.rsqrt(jnp.mean(yc * yc, axis=-1, keepdims=True) + EPS)
    xhat = yc * rstd
    return xhat, rstd, xhat * g + b


def _fill_glu(ca_ref, cg_ref, glu, S):
    glu[pl.ds(0, CPAD), :] = jnp.zeros((CPAD, DC), F32)

    def fill(i, c):
        rows = pl.ds(pl.multiple_of(i * 256, 256), 256)
        a = ca_ref[rows, :].astype(F32)
        gt = cg_ref[rows, :].astype(F32)
        glu[pl.ds(pl.multiple_of(CPAD + i * 256, CT), 256), :] = a * _sigmoid(gt)
        return c

    lax.fori_loop(0, S // 256, fill, 0)


def _conv_fwd(u, cw, cb, lg, lb, B, S):
    T = B * S

    def body(ca_ref, cg_ref, w_ref, b_ref, lg_ref, lb_ref, o_ref, y_ref, glu):
        _fill_glu(ca_ref, cg_ref, glu, S)

        def step(i, c):
            t0 = pl.multiple_of(i * CT, CT)
            win = glu[pl.ds(t0, 2 * CT), :]
            acc = jnp.zeros((CT, DC), F32) + b_ref[...]
            taps = _shifted(win, [k + 2 for k in range(CK)])
            for k in range(CK):
                acc = acc + taps[k + 2] * w_ref[k:k + 1, :]
            y_ref[pl.ds(t0, CT), :] = acc
            _, _, z = _ln_fwd(acc, lg_ref[...], lb_ref[...])
            o_ref[pl.ds(t0, CT), :] = (z * _sigmoid(z)).astype(BF16)
            return c

        lax.fori_loop(0, S // CT, step, 0, unroll=4)

    vec = pl.BlockSpec((1, DC), lambda b: (0, 0))
    return pl.pallas_call(
        body, grid=(B,),
        in_specs=[pl.BlockSpec((S, DC), lambda b: (b, 3)), pl.BlockSpec((S, DC), lambda b: (b, 4)),
                  pl.BlockSpec((CT, DC), lambda b: (0, 0)), vec, vec, vec],
        out_specs=[pl.BlockSpec((S, DC), lambda b: (b, 0))] * 2,
        out_shape=[jax.ShapeDtypeStruct((T, DC), BF16), jax.ShapeDtypeStruct((T, DC), F32)],
        scratch_shapes=[pltpu.VMEM((CPAD + S, DC), F32)],
        compiler_params=_cp(("parallel",)), name="conv_fwd")(u, u, cw, cb, lg, lb)


def _conv_bwd(u, y, dconv, cw, lg, lb, B, S):
    T = B * S

    def body(ca_ref, cg_ref, y_ref, dc_ref, w_ref, lg_ref, lb_ref,
             dca_ref, dcg_ref, dw_ref, ds_ref, glu, dyp, dwacc):
        _fill_glu(ca_ref, cg_ref, glu, S)
        dyp[pl.ds(S, CPAD), :] = jnp.zeros((CPAD, DC), F32)
        lgv, lbv = lg_ref[...], lb_ref[...]

        def sum8(v):
            return functools.reduce(jnp.add, [v[r:r + 8] for r in range(0, v.shape[0], 8)])

        P1 = 4 * CT

        def p1(i, carry):
            sb, sg, sl = carry
            t0 = pl.multiple_of(i * P1, P1)
            xhat, rstd, z = _ln_fwd(y_ref[pl.ds(t0, P1), :], lgv, lbv)
            sz = _sigmoid(z)
            dz = dc_ref[pl.ds(t0, P1), :].astype(F32) * (sz * (1.0 + z * (1.0 - sz)))
            dxhat = dz * lgv
            dy = rstd * (dxhat - jnp.mean(dxhat, axis=-1, keepdims=True)
                         - xhat * jnp.mean(dxhat * xhat, axis=-1, keepdims=True))
            dyp[pl.ds(t0, P1), :] = dy
            return sb + sum8(dy), sg + sum8(dz * xhat), sl + sum8(dz)

        z8 = jnp.zeros((8, DC), F32)
        sb, sg, sl = lax.fori_loop(0, S // P1, p1, (z8, z8, z8))
        rs = lambda v: jnp.sum(v, axis=0, keepdims=True)
        ds_ref[...] = jnp.concatenate([rs(sb), rs(sg), rs(sl), jnp.zeros((5, DC), F32)], axis=0)[None]

        def p2(i, c):
            t0 = pl.multiple_of(i * CT, CT)
            win = dyp[pl.ds(t0, 2 * CT), :]
            acc = jnp.zeros((CT, DC), F32)
            taps = _shifted(win, [30 - k for k in range(CK)])
            for k in range(CK):
                acc = acc + taps[30 - k] * w_ref[k:k + 1, :]
            a = ca_ref[pl.ds(t0, CT), :].astype(F32)
            sgt = _sigmoid(cg_ref[pl.ds(t0, CT), :].astype(F32))
            dca_ref[pl.ds(t0, CT), :] = (acc * sgt).astype(BF16)
            dcg_ref[pl.ds(t0, CT), :] = (acc * a * sgt * (1.0 - sgt)).astype(BF16)
            return c

        lax.fori_loop(0, S // CT, p2, 0)

        dwacc[...] = jnp.zeros_like(dwacc)

        def p3(i, c):
            t0 = pl.multiple_of(i * CT, CT)
            win = glu[pl.ds(t0, 2 * CT), :]
            dy = dyp[pl.ds(t0, CT), :]
            for k in range(CK):
                dwacc[k] += sum8(dy * win[k + 2:k + 2 + CT, :])
            return c

        lax.fori_loop(0, S // CT, p3, 0)
        dw_ref[...] = jnp.sum(dwacc[...], axis=1)[None]

    vec = pl.BlockSpec((1, DC), lambda b: (0, 0))
    seq = pl.BlockSpec((S, DC), lambda b: (b, 0))
    return pl.pallas_call(
        body, grid=(B,),
        in_specs=[pl.BlockSpec((S, DC), lambda b: (b, 3)), pl.BlockSpec((S, DC), lambda b: (b, 4)),
                  seq, seq, pl.BlockSpec((CT, DC), lambda b: (0, 0)), vec, vec],
        out_specs=[seq, seq, pl.BlockSpec((1, CT, DC), lambda b: (b, 0, 0)),
                   pl.BlockSpec((1, 8, DC), lambda b: (b, 0, 0))],
        out_shape=[jax.ShapeDtypeStruct((T, DC), BF16)] * 2
                  + [jax.ShapeDtypeStruct((B, CT, DC), F32), jax.ShapeDtypeStruct((B, 8, DC), F32)],
        scratch_shapes=[pltpu.VMEM((CPAD + S, DC), F32), pltpu.VMEM((S + CPAD, DC), F32),
                        pltpu.VMEM((CT, 8, DC), F32)],
        compiler_params=_cp(("parallel",)), name="conv_bwd")(u, u, y, dconv, cw, lg, lb)


def _local_step(x, target, norms, W, B, S, comm=None):
    qg2 = jnp.concatenate([norms["q_norm"], norms["q_norm"]], axis=1)
    kg2 = jnp.concatenate([norms["k_norm"], norms["k_norm"]], axis=1)
    cw = jnp.concatenate([W["conv_w"], jnp.zeros((1, DC), F32)], axis=0)

    W = dict(W)
    (n1, g1, u1, act1), got = _ffn_gate_up(x, norms["ffn1_norm"], W["wg1"], W["wu1"], "ffn1_gate_up",
                                           carry=comm.gathers["down_in"] if comm else None)
    if comm:
        W.update(comm.gathered("down_in", got))
    h1, u, n2 = _ffn_down_mix_in(x, act1, W["wd1"], norms["mix_norm"], W["win"], "ffn1_down_mix_in")
    (attn, lse), got = _attn_fwd(u, qg2, kg2, B, S, carry=comm.gathers["ffn2"] if comm else None)
    if comm:
        W = dict(W, **comm.gathered("ffn2", got))
    conv, y = _conv_fwd(u, cw, norms["conv_b"], norms["conv_ln_g"], norms["conv_ln_b"], B, S)
    h2, n3, g2, u2, dout, dyb, sq, act2 = _mix_out_ffn_loss(h1, attn, conv, W["wout"], norms["ffn2_norm"],
                                                            W["wg2"], W["wu2"], W["wd2"], target, "ffn2_fwd")
    loss = (0.5 / D) * jnp.sum(sq)

    (dg2, du2, dh2, dgn_ffn2), _ = _ffn_bwd_act(dyb, g2, u2, h2, dout, norms["ffn2_norm"],
                                               W["wg2"], W["wu2"], W["wd2"], "ffn2_bwd_act")
    dwd2, _ = _ffn_bwd_w(act2, dyb, "ffn2_bwd_wd")
    dwg2, _ = _ffn_bwd_w(dg2, n3, "ffn2_bwd_wg")
    dwu2, _ = _ffn_bwd_w(du2, n3, "ffn2_bwd_wu")
    dattn, dconv, dwout = _mix_out_bwd(dh2, attn, conv, W["wout"])
    carry = comm.reduce_start({"wg2": dwg2, "wu2": dwu2, "wd2": dwd2, "wout": dwout}) if comm else None
    (dq, dk, dv, dgn_qk), got = _attn_bwd(u, attn, dattn, lse, qg2, kg2, B, S, carry=carry)
    if comm:
        comm.reduce_done(carry, got)
    dca, dcg, dcw, dcs = _conv_bwd(u, y, dconv, cw, norms["conv_ln_g"], norms["conv_ln_b"], B, S)
    dwin, dh1, dyb1, dgn_mix = _mix_in_bwd((dq, dk, dv, dca, dcg), W["win"], n2, h1, dh2, norms["mix_norm"])
    carry = comm.reduce_start({"win": dwin}) if comm else None
    dwd1, got = _ffn_bwd_w(act1, dyb1, "ffn1_bwd_wd", carry=carry)
    if comm:
        comm.reduce_done(carry, got)
    carry = comm.update_start(("wg2", "wu2", "wd2", "wout", "win")) if comm else None
    (dg1, du1, gx, dgn_ffn1), got = _ffn_bwd_act(dyb1, g1, u1, x, dh1, norms["ffn1_norm"],
                                                W["wg1"], W["wu1"], W["wd1"], "ffn1_bwd_act", carry=carry)
    if comm:
        comm.update_done(carry, got)
        carry = comm.reduce_start({"wd1": dwd1})
    dwg1, got = _ffn_bwd_w(dg1, n1, "ffn1_bwd_wg", carry=carry)
    if comm:
        comm.reduce_done(carry, got)
        carry = comm.reduce_start({"wg1": dwg1})
    dwu1, got = _ffn_bwd_w(du1, n1, "ffn1_bwd_wu", carry=carry)
    if comm:
        comm.reduce_done(carry, got)
        comm.last = comm.reduce_start({"wu1": dwu1})

    qk = jnp.sum(dgn_qk, axis=0)
    cs = jnp.sum(dcs, axis=0)
    small = {
        "ffn1_norm": jnp.sum(dgn_ffn1, axis=0),
        "mix_norm": jnp.sum(dgn_mix, axis=0),
        "q_norm": qk[0:1, 0:HD] + qk[0:1, HD:2 * HD],
        "k_norm": qk[1:2, 0:HD] + qk[1:2, HD:2 * HD],
        "conv_w": jnp.sum(dcw, axis=0)[0:CK],
        "conv_b": cs[0:1],
        "conv_ln_g": cs[1:2],
        "conv_ln_b": cs[2:3],
        "ffn2_norm": jnp.sum(dgn_ffn2, axis=0),
    }
    big = {"wg1": dwg1, "wu1": dwu1, "wd1": dwd1, "win": dwin, "wout": dwout,
           "wg2": dwg2, "wu2": dwu2, "wd2": dwd2}
    return loss, gx, big, small


HBM = pl.BlockSpec(memory_space=pltpu.HBM)
VMEM = pl.BlockSpec(memory_space=pltpu.VMEM)


def _place():
    return lax.axis_index("x"), lax.axis_index("y"), lax.axis_index("c")


class _GatherCarry:
    def __init__(self, shards, mid_at=0.5):
        nt = len(shards)
        self.mid_at = mid_at
        self.shards = shards
        self.in_arrays = [s for s, _ in shards]
        self.in_specs = [VMEM] * nt
        self.out_shape = [jax.ShapeDtypeStruct((NDEV * s.shape[0], s.shape[1]), dt) for s, dt in shards]
        self.out_specs = [HBM] * nt
        self.scratch = ([pltpu.VMEM(s.shape, dt) for s, dt in shards]
                        + [pltpu.SemaphoreType.DMA((nt, 7)), pltpu.SemaphoreType.DMA((nt, 7)),
                           pltpu.SemaphoreType.DMA((nt,))])

    def _copies(self, outs, scr):
        nt = len(self.shards)
        stages = scr[:nt]
        send_sems, recv_sems, local_sems = scr[nt:]
        x, y, c = _place()
        me, sibling = (x, y, c), (x, y, 1 - c)
        xn, yn, diag = (1 - x, y, c), (x, 1 - y, c), (1 - x, 1 - y, c)
        via = (x ^ c, y ^ (1 - c), c)
        onto = (x ^ (1 - c), y ^ c, c)

        def rows(t, px, py, pc):
            r = self.shards[t][0].shape[0]
            return outs[t].at[pl.ds((4 * px + 2 * py + pc) * r, r), :]

        def copy(t, k, block, to, src=None):
            return pltpu.make_async_remote_copy(
                src_ref=rows(t, *block) if src is None else src, dst_ref=rows(t, *block),
                send_sem=send_sems.at[t, k], recv_sem=recv_sems.at[t, k],
                device_id=to, device_id_type=MESH)

        sib = lambda b: (b[0], b[1], 1 - c)
        return dict(
            local=[pltpu.make_async_copy(stages[t], rows(t, *me), local_sems.at[t]) for t in range(nt)],
            own=[[copy(t, 0, me, sibling, src=stages[t]), copy(t, 1, me, xn, src=stages[t]),
                  copy(t, 2, me, yn, src=stages[t])] for t in range(nt)],
            relay=[copy(t, 3, via, onto) for t in range(nt)],
            down=[[copy(t, 4, xn, sibling), copy(t, 5, yn, sibling)] for t in range(nt)],
            down_diag=[copy(t, 6, diag, sibling) for t in range(nt)],
            got_xy=[[copy(t, 1, xn, me), copy(t, 2, yn, me)] for t in range(nt)],
            got_diag=[copy(t, 3, diag, me) for t in range(nt)],
            got_sib=[[copy(t, 0, sibling, me), copy(t, 4, sib(xn), me), copy(t, 5, sib(yn), me),
                      copy(t, 6, sib(diag), me)] for t in range(nt)])

    def start(self, ins, outs, scr):
        cps = self._copies(outs, scr)
        for t, (_, dt) in enumerate(self.shards):
            scr[t][...] = ins[t][...].astype(dt)
            for cp in [cps["local"][t]] + cps["own"][t]:
                cp.start()

    def stages(self):
        sizes = [s.size * jnp.dtype(dt).itemsize for s, dt in self.shards]
        done = [sum(sizes[:t + 1]) / sum(sizes) for t in range(len(sizes))]
        return [(self.mid_at * f, functools.partial(self.mid, t)) for t, f in enumerate(done)] + [(self.LATE_AT, self.late)]

    LATE_AT = 0.85

    def late(self, ins, outs, scr):
        cps = self._copies(outs, scr)
        for t in range(len(self.shards)):
            cps["got_diag"][t].wait_recv()
            cps["down_diag"][t].start()

    def mid(self, t, ins, outs, scr):
        cps = self._copies(outs, scr)
        for cp in cps["got_xy"][t]:
            cp.wait_recv()
        for cp in [cps["relay"][t]] + cps["down"][t]:
            cp.start()

    def finish(self, ins, outs, scr):
        cps = self._copies(outs, scr)
        for t in range(len(self.shards)):
            for cp in cps["got_sib"][t]:
                cp.wait_recv()
            for cp in cps["own"][t] + [cps["relay"][t]] + cps["down"][t] + [cps["down_diag"][t]]:
                cp.wait_send()
            cps["local"][t].wait()


def _run_carry(carry, name):
    def body(*refs):
        n_in, n_out = len(carry.in_arrays), len(carry.out_shape)
        ins, outs, scr = refs[:n_in], refs[n_in:n_in + n_out], refs[n_in + n_out:]
        carry.start(ins, outs, scr)
        for _, stage in carry.stages():
            stage(ins, outs, scr)
        carry.finish(ins, outs, scr)

    return pl.pallas_call(
        body, in_specs=carry.in_specs, out_specs=carry.out_specs, out_shape=carry.out_shape,
        scratch_shapes=carry.scratch, compiler_params=pltpu.CompilerParams(vmem_limit_bytes=VMEM_LIMIT),
        name=name)(*carry.in_arrays)


class _ExchangeCarry:
    def __init__(self, names, grads, mid_at=0.5):
        nt = len(grads)
        self.mid_at = mid_at
        self.names = names
        self.in_arrays = [g.reshape(4, 2, g.shape[0] // NDEV, g.shape[1]) for g in grads]
        self.in_specs = [HBM] * nt
        blocks = [g.shape[2:] for g in self.in_arrays]
        self.out_shape = [jax.ShapeDtypeStruct((3,) + b, BF16) for b in blocks]
        self.out_specs = [HBM] * nt
        self.scratch = ([pltpu.VMEM((4,) + b, BF16) for b in blocks] * 2 + [pltpu.VMEM(b, BF16) for b in blocks]
                        + [pltpu.SemaphoreType.DMA((nt, 3)), pltpu.SemaphoreType.DMA((nt, 3))]
                        + [pltpu.SemaphoreType.DMA((nt,))] * 4)

    def _copies(self, ins, outs, scr):
        nt = len(ins)
        theirs, own, relayed = scr[:nt], scr[nt:2 * nt], scr[2 * nt:3 * nt]
        send_sems, recv_sems, keep_sems, load_sems, swap_send, swap_recv = scr[3 * nt:]
        x, y, c = _place()
        q = lambda cx, cy: 2 * cx + cy
        near, far = (x ^ c, y ^ (1 - c)), (x ^ (1 - c), y ^ c)

        def remote(t, k, src, dst, chip):
            return pltpu.make_async_remote_copy(
                src_ref=src, dst_ref=dst, send_sem=send_sems.at[t, k], recv_sem=recv_sems.at[t, k],
                device_id=(*chip, c), device_id_type=MESH)

        return dict(
            swap=[pltpu.make_async_remote_copy(
                src_ref=ins[t].at[:, 1 - c], dst_ref=theirs[t], send_sem=swap_send.at[t], recv_sem=swap_recv.at[t],
                device_id=(x, y, 1 - c), device_id_type=MESH) for t in range(nt)],
            load=[pltpu.make_async_copy(ins[t].at[:, c], own[t], load_sems.at[t]) for t in range(nt)],
            keep=[pltpu.make_async_copy(own[t].at[q(x, y)], outs[t].at[0], keep_sems.at[t]) for t in range(nt)],
            direct=[remote(t, 0, own[t].at[q(*near)], outs[t].at[1], near) for t in range(nt)],
            relay=[remote(t, 1, own[t].at[q(1 - x, 1 - y)], relayed[t], near) for t in range(nt)],
            merged=[remote(t, 2, own[t].at[q(*far)], outs[t].at[2], far) for t in range(nt)],
            theirs=theirs, own=own, relayed=relayed, far=q(*far))

    EARLY_AT = 0.1

    def stages(self):
        return [(self.EARLY_AT, self.early), (self.mid_at, self.mid)]

    def start(self, ins, outs, scr):
        cps = self._copies(ins, outs, scr)
        for t in range(len(ins)):
            cps["swap"][t].start()
            cps["load"][t].start()

    def early(self, ins, outs, scr):
        cps = self._copies(ins, outs, scr)
        for t in range(len(ins)):
            cps["load"][t].wait()
            cps["swap"][t].wait_recv()
            own, theirs = cps["own"][t], cps["theirs"][t]
            for j in range(4):
                own[j] = (own[j].astype(F32) + theirs[j].astype(F32)).astype(BF16)
            for kind in ("relay", "direct", "keep"):
                cps[kind][t].start()

    def mid(self, ins, outs, scr):
        cps = self._copies(ins, outs, scr)
        for t in range(len(ins)):
            cps["relay"][t].wait_recv()
            own, far = cps["own"][t], cps["far"]
            own[far] = (own[far].astype(F32) + cps["relayed"][t][...].astype(F32)).astype(BF16)
            cps["merged"][t].start()

    def finish(self, ins, outs, scr):
        cps = self._copies(ins, outs, scr)
        for t in range(len(ins)):
            cps["swap"][t].wait_send()
            cps["direct"][t].wait()
            cps["relay"][t].wait_send()
            cps["merged"][t].wait()
            cps["keep"][t].wait()


class _Comm:
    def __init__(self, groups, opt):
        self.names = {tag: list(g) for tag, (g, _) in groups.items()}
        self.gathers = {tag: _GatherCarry(list(g.values()), mid_at) for tag, (g, mid_at) in groups.items()}
        self.reduced = {}
        self.last = None
        self.opt = opt
        self.updated = {}

    def gathered(self, tag, outs):
        return dict(zip(self.names[tag], outs))

    def reduce_start(self, grads):
        names = list(grads)
        return _ExchangeCarry(names, [grads[n] for n in names])

    def reduce_done(self, carry, outs):
        self.reduced.update(zip(carry.names, outs))

    def update_start(self, names):
        w, m, v = zip(*[self.opt[n] for n in names])
        return _AdamWCarry(names, [self.reduced[n] for n in names], w, m, v)

    def update_done(self, carry, outs):
        self.updated.update({n: outs[4 * k:4 * k + 4] for k, n in enumerate(carry.names)})


def _adamw_math(w, g, m, v):
    m = B1 * m + (1.0 - B1) * g
    v = B2 * v + (1.0 - B2) * (g * g)
    m_hat = m / (1.0 - B1 ** STEP)
    v_hat = v / (1.0 - B2 ** STEP)
    delta = -LR * (m_hat / (jnp.sqrt(v_hat) + AEPS) + WD * w)
    return delta, m, v


def _adamw_big(recvs, ws, ms, vs, name):
    nw = len(ws)

    def body(*refs):
        ins, outs = refs[:4 * nw], refs[4 * nw:]
        for k in range(nw):
            @pl.when(pl.program_id(0) // 2 == k)
            def _(k=k):
                r_ref, w_ref, m_ref, v_ref = ins[4 * k:4 * k + 4]
                g_ref, d_ref, mo_ref, vo_ref = outs[4 * k:4 * k + 4]
                g = r_ref[0].astype(F32)
                for q in range(1, 3):
                    g = g + r_ref[q].astype(F32)
                d, mn, vn = _adamw_math(w_ref[...], g, m_ref[...], v_ref[...])
                g_ref[...] = g
                d_ref[...] = d
                mo_ref[...] = mn
                vo_ref[...] = vn

    in_specs, out_specs, out_shape, args = [], [], [], []
    for k, (r, w, m, v) in enumerate(zip(recvs, ws, ms, vs)):
        rows, n = w.shape
        tr = rows // 2
        tile = lambda s, k=k: jnp.clip(s - 2 * k, 0, 1)
        row = pl.BlockSpec((tr, n), lambda s, tile=tile: (tile(s), 0))
        in_specs += [pl.BlockSpec((3, tr, n), lambda s, tile=tile: (0, tile(s), 0)), row, row, row]
        out_specs += [row] * 4
        out_shape += [jax.ShapeDtypeStruct(w.shape, F32)] * 4
        args += [r, w, m, v]
    res = pl.pallas_call(
        body, grid=(2 * nw,), in_specs=in_specs, out_specs=out_specs, out_shape=out_shape,
        compiler_params=_cp(("arbitrary",)), name=name)(*args)
    return [res[4 * k:4 * k + 4] for k in range(nw)]


class _AdamWCarry:
    def __init__(self, names, recvs, ws, ms, vs):
        self.names = names
        nw = len(ws)
        self.halves = [(k, j, w.shape[0] // 2) for k, w in enumerate(ws) for j in range(2)]
        self.in_arrays = [a for quad in zip(recvs, ws, ms, vs) for a in quad]
        self.in_specs = [HBM] * (4 * nw)
        self.out_shape = [jax.ShapeDtypeStruct(w.shape, F32) for w in ws for _ in range(4)]
        self.out_specs = [HBM] * (4 * nw)
        tr, n = max(h[2] for h in self.halves), ws[0].shape[1]
        self.scratch = [pltpu.VMEM((2, 3, tr, n), BF16), pltpu.VMEM((2, 3, tr, n), F32),
                        pltpu.VMEM((2, 4, tr, n), F32), pltpu.SemaphoreType.DMA((2, 4)),
                        pltpu.SemaphoreType.DMA((2, 4))]

    def _copies(self, c, ins, outs, scr):
        rbuf, fbuf, obuf, in_sems, out_sems = scr
        k, j, tr = self.halves[c]
        s, rows = c % 2, pl.ds(j * tr, tr)
        loads = [pltpu.make_async_copy(ins[4 * k].at[:, rows, :], rbuf.at[s, :, pl.ds(0, tr), :], in_sems.at[s, 0])]
        loads += [pltpu.make_async_copy(ins[4 * k + i].at[rows, :], fbuf.at[s, i - 1, pl.ds(0, tr), :],
                                        in_sems.at[s, i]) for i in range(1, 4)]
        stores = [pltpu.make_async_copy(obuf.at[s, q, pl.ds(0, tr), :], outs[4 * k + q].at[rows, :],
                                        out_sems.at[s, q]) for q in range(4)]
        return loads, stores

    def start(self, ins, outs, scr):
        for c in range(2):
            for cp in self._copies(c, ins, outs, scr)[0]:
                cp.start()

    def stages(self):
        n = len(self.halves)
        return [((c + 1) / (n + 1), functools.partial(self.half, c)) for c in range(n)]

    def half(self, c, ins, outs, scr):
        rbuf, fbuf, obuf = scr[:3]
        _, _, tr = self.halves[c]
        s = c % 2
        loads, stores = self._copies(c, ins, outs, scr)
        for cp in loads:
            cp.wait()
        if c >= 2:
            for cp in self._copies(c - 2, ins, outs, scr)[1]:
                cp.wait()
        g = rbuf[s, 0, 0:tr].astype(F32)
        for q in range(1, 3):
            g = g + rbuf[s, q, 0:tr].astype(F32)
        d, mn, vn = _adamw_math(fbuf[s, 0, 0:tr], g, fbuf[s, 1, 0:tr], fbuf[s, 2, 0:tr])
        for q, val in enumerate((g, d, mn, vn)):
            obuf[s, q, 0:tr] = val
        for cp in stores:
            cp.start()
        if c + 2 < len(self.halves):
            for cp in self._copies(c + 2, ins, outs, scr)[0]:
                cp.start()

    def finish(self, ins, outs, scr):
        n = len(self.halves)
        for c in range(max(n - 2, 0), n):
            for cp in self._copies(c, ins, outs, scr)[1]:
                cp.wait()


SMALL_NAMES = ("ffn1_norm", "mix_norm", "ffn2_norm", "conv_b", "conv_ln_g", "conv_ln_b", "q_norm", "k_norm")
SROWS = 16
LOSS_ROW = len(SMALL_NAMES)
CWT = (32, 1, 128)


def _small_sums(gs, loss_row, gcw, carry=None):
    ns = len(SMALL_NAMES)
    widths = [g.shape[1] for g in gs]

    def body(*refs):
        it = iter(refs)
        take = lambda n: [next(it) for _ in range(n)]
        g_refs, (loss_ref, gcw_ref) = take(ns), take(2)
        cins = take(len(carry.in_arrays)) if carry else []
        tot_ref, ctot_ref = take(2)
        couts = take(len(carry.out_shape)) if carry else []
        send, slots, cslots, send_sems, recv_sems, csend_sems, crecv_sems = take(7)
        cscr = list(it)
        x, y, c = _place()
        me = 4 * x + 2 * y + c
        send[...] = jnp.zeros_like(send)
        for k in range(ns):
            send[k:k + 1, 0:widths[k]] = g_refs[k][...]
        send[LOSS_ROW:LOSS_ROW + 1, 0:128] = loss_ref[...]
        slots[me] = send[...]
        cslots[me] = gcw_ref[me]
        cps = []
        for k in range(1, NDEV):
            peer = (x ^ ((k >> 2) & 1), y ^ ((k >> 1) & 1), c ^ (k & 1))
            cps.append(pltpu.make_async_remote_copy(
                src_ref=send, dst_ref=slots.at[me], send_sem=send_sems.at[k - 1], recv_sem=recv_sems.at[k - 1],
                device_id=peer, device_id_type=MESH))
            cps.append(pltpu.make_async_remote_copy(
                src_ref=gcw_ref.at[4 * peer[0] + 2 * peer[1] + peer[2]], dst_ref=cslots.at[me],
                send_sem=csend_sems.at[k - 1], recv_sem=crecv_sems.at[k - 1], device_id=peer, device_id_type=MESH))
        for cp in cps:
            cp.start()
        stages = [stage for _, stage in carry.stages()] if carry else []
        if carry:
            carry.start(cins, couts, cscr)
        for stage in stages[:1]:
            stage(cins, couts, cscr)
        for cp in cps:
            cp.wait()
        tot = slots[0]
        ctot = cslots[0]
        for j in range(1, NDEV):
            tot = tot + slots[j]
            ctot = ctot + cslots[j]
        tot_ref[...] = tot
        ctot_ref[...] = ctot
        for stage in stages[1:]:
            stage(cins, couts, cscr)
        if carry:
            carry.finish(cins, couts, cscr)

    args = [*gs, loss_row, gcw]
    out_shape = [jax.ShapeDtypeStruct((SROWS, D), F32), jax.ShapeDtypeStruct(CWT, F32)]
    res = pl.pallas_call(
        body, in_specs=[VMEM] * len(args) + (carry.in_specs if carry else []),
        out_specs=[VMEM] * 2 + (carry.out_specs if carry else []),
        out_shape=out_shape + (carry.out_shape if carry else []),
        scratch_shapes=[pltpu.VMEM((SROWS, D), F32), pltpu.VMEM((NDEV, SROWS, D), F32),
                        pltpu.VMEM((NDEV,) + CWT, F32)]
                       + [pltpu.SemaphoreType.DMA((NDEV - 1,))] * 4 + (carry.scratch if carry else []),
        name="small_sums")(*args, *(carry.in_arrays if carry else []))
    return res[0], res[1], res[2:]


def _adamw_small(tot, ctot, ws, ms, vs, wcw, mcw, vcw):
    ns = len(SMALL_NAMES)
    widths = [w.shape[1] for w in ws]

    def body(*refs):
        it = iter(refs)
        take = lambda n: [next(it) for _ in range(n)]
        (tot_ref, ctot_ref), w_refs, m_refs, v_refs = take(2), take(ns), take(ns), take(ns)
        wcw_ref, mcw_ref, vcw_ref = take(3)
        outs = [take(4) for _ in range(ns)]
        cw_outs, (loss_out,) = take(4), take(1)

        def step(g, w_ref, m_ref, v_ref, o):
            d, mn, vn = _adamw_math(w_ref[...], g, m_ref[...], v_ref[...])
            o[0][...], o[1][...], o[2][...], o[3][...] = g, d, mn, vn

        for k in range(ns):
            step(tot_ref[k:k + 1, 0:widths[k]], w_refs[k], m_refs[k], v_refs[k], outs[k])
        step(ctot_ref[0:CK, :, 0:HD], wcw_ref, mcw_ref, vcw_ref, cw_outs)
        loss_out[...] = tot_ref[LOSS_ROW:LOSS_ROW + 1, 0:128]

    args = [tot, ctot, *ws, *ms, *vs, wcw, mcw, vcw]
    out_shape = ([jax.ShapeDtypeStruct((1, n), F32) for n in widths for _ in range(4)]
                 + [jax.ShapeDtypeStruct((CK, 1, HD), F32)] * 4 + [jax.ShapeDtypeStruct((1, 128), F32)])
    res = pl.pallas_call(
        body, in_specs=[VMEM] * len(args), out_specs=[VMEM] * len(out_shape), out_shape=out_shape,
        name="adamw_small")(*args)
    per = [res[4 * k:4 * k + 4] for k in range(ns)]
    return per, res[4 * ns:4 * ns + 4], res[-1]


def kernel(x, ffn1_norm, ffn1_w_gate, ffn1_w_up, ffn1_w_down, mix_norm, w_in, q_norm, k_norm, conv_w, conv_b, conv_ln_g, conv_ln_b, w_out, ffn2_norm, ffn2_w_gate, ffn2_w_up, ffn2_w_down, loss_target, m_ffn1_norm, m_ffn1_w_gate, m_ffn1_w_up, m_ffn1_w_down, m_mix_norm, m_w_in, m_q_norm, m_k_norm, m_conv_w, m_conv_b, m_conv_ln_g, m_conv_ln_b, m_w_out, m_ffn2_norm, m_ffn2_w_gate, m_ffn2_w_up, m_ffn2_w_down, v_ffn1_norm, v_ffn1_w_gate, v_ffn1_w_up, v_ffn1_w_down, v_mix_norm, v_w_in, v_q_norm, v_k_norm, v_conv_w, v_conv_b, v_conv_ln_g, v_conv_ln_b, v_w_out, v_ffn2_norm, v_ffn2_w_gate, v_ffn2_w_up, v_ffn2_w_down):
    P = dict(ffn1_norm=ffn1_norm, ffn1_w_gate=ffn1_w_gate, ffn1_w_up=ffn1_w_up, ffn1_w_down=ffn1_w_down,
             mix_norm=mix_norm, w_in=w_in, q_norm=q_norm, k_norm=k_norm, conv_w=conv_w, conv_b=conv_b,
             conv_ln_g=conv_ln_g, conv_ln_b=conv_ln_b, w_out=w_out, ffn2_norm=ffn2_norm,
             ffn2_w_gate=ffn2_w_gate, ffn2_w_up=ffn2_w_up, ffn2_w_down=ffn2_w_down)
    M = dict(ffn1_norm=m_ffn1_norm, ffn1_w_gate=m_ffn1_w_gate, ffn1_w_up=m_ffn1_w_up, ffn1_w_down=m_ffn1_w_down,
             mix_norm=m_mix_norm, w_in=m_w_in, q_norm=m_q_norm, k_norm=m_k_norm, conv_w=m_conv_w, conv_b=m_conv_b,
             conv_ln_g=m_conv_ln_g, conv_ln_b=m_conv_ln_b, w_out=m_w_out, ffn2_norm=m_ffn2_norm,
             ffn2_w_gate=m_ffn2_w_gate, ffn2_w_up=m_ffn2_w_up, ffn2_w_down=m_ffn2_w_down)
    V = dict(ffn1_norm=v_ffn1_norm, ffn1_w_gate=v_ffn1_w_gate, ffn1_w_up=v_ffn1_w_up, ffn1_w_down=v_ffn1_w_down,
             mix_norm=v_mix_norm, w_in=v_w_in, q_norm=v_q_norm, k_norm=v_k_norm, conv_w=v_conv_w, conv_b=v_conv_b,
             conv_ln_g=v_conv_ln_g, conv_ln_b=v_conv_ln_b, w_out=v_w_out, ffn2_norm=v_ffn2_norm,
             ffn2_w_gate=v_ffn2_w_gate, ffn2_w_up=v_ffn2_w_up, ffn2_w_down=v_ffn2_w_down)
    order = ["ffn1_norm", "ffn1_w_gate", "ffn1_w_up", "ffn1_w_down", "mix_norm", "w_in", "q_norm", "k_norm",
             "conv_w", "conv_b", "conv_ln_g", "conv_ln_b", "w_out", "ffn2_norm", "ffn2_w_gate", "ffn2_w_up",
             "ffn2_w_down"]
    B, S, _ = x.shape
    T = B * S

    bigs = [("wg1", "ffn1_w_gate", True), ("wu1", "ffn1_w_up", True), ("wd1", "ffn1_w_down", False),
            ("win", "w_in", True), ("wout", "w_out", False),
            ("wg2", "ffn2_w_gate", True), ("wu2", "ffn2_w_up", True), ("wd2", "ffn2_w_down", False)]
    hm = lambda a, tr: jnp.transpose(a[0]) if tr else a[0]
    cw_pad = jnp.zeros((32, 128), F32).at[0:CK, 0:HD].set(conv_w[0])
    shard = {ln: (hm(P[pn], tr), BF16) for ln, pn, tr in bigs}
    gathered = _run_carry(_GatherCarry([shard["wg1"], shard["wu1"], (cw_pad, F32)]), "gather_first")
    W = {"wg1": gathered[0], "wu1": gathered[1]}
    cwg = gathered[2].reshape(NDEV, 32, 128)[:, 0:CK, 0:HD]
    W["conv_w"] = jnp.transpose(cwg, (1, 0, 2)).reshape(CK, DC)
    norms = {n: P[n] for n in SMALL_NAMES}
    comm = _Comm({"down_in": ({n: shard[n] for n in ("wd1", "win")}, 0.45),
                  "ffn2": ({n: shard[n] for n in ("wg2", "wu2", "wd2", "wout")}, 0.45)},
                 opt={ln: (hm(P[pn], tr), hm(M[pn], tr), hm(V[pn], tr)) for ln, pn, tr in bigs})

    loss_part, gx, _, small = _local_step(x.reshape(T, D), loss_target.reshape(T, D), norms, W, B, S, comm)

    G, Dl, Mn, Vn = {}, {}, {}, {}
    dcw = small["conv_w"].reshape(CK, NDEV, HD).transpose(1, 0, 2)
    loss_row = jnp.zeros((1, 128), F32).at[0, 0].set(loss_part)
    gcw = jnp.pad(dcw[:, :, None, :], ((0, 0), (0, CWT[0] - CK), (0, 0), (0, CWT[2] - HD)))
    taps_first = lambda a: jnp.transpose(a, (1, 0, 2))
    tot, ctot, got = _small_sums([small[n] for n in SMALL_NAMES], loss_row, gcw, carry=comm.last)
    comm.reduce_done(comm.last, got)
    per, cw_outs, loss_out = _adamw_small(
        tot, ctot, [P[n] for n in SMALL_NAMES], [M[n] for n in SMALL_NAMES], [V[n] for n in SMALL_NAMES],
        taps_first(P["conv_w"]), taps_first(M["conv_w"]), taps_first(V["conv_w"]))
    loss = loss_out[0, 0]
    for n, outs in zip(SMALL_NAMES, per):
        G[n], Dl[n], Mn[n], Vn[n] = outs
    G["conv_w"], Dl["conv_w"], Mn["conv_w"], Vn["conv_w"] = [taps_first(o) for o in cw_outs]

    group = ("wd1", "wg1", "wu1")
    w, m, v = zip(*[comm.opt[ln] for ln in group])
    comm.updated.update(zip(group, _adamw_big([comm.reduced[ln] for ln in group], w, m, v, "adamw_ffn1")))
    for ln, pn, tr in bigs:
        G[pn], Dl[pn], Mn[pn], Vn[pn] = [(jnp.transpose(o) if tr else o)[None] for o in comm.updated[ln]]

    return (loss, gx.reshape(B, S, D), *[G[n] for n in order], *[Dl[n] for n in order],
            *[Mn[n] for n in order], *[Vn[n] for n in order])
```

```python
import functools

import jax
import jax.numpy as jnp
from jax import lax
from jax.experimental import pallas as pl
from jax.experimental.pallas import tpu as pltpu

F32 = jnp.float32
BF16 = jnp.bfloat16

D = 1024
FF = 2816
HD = 64
DA = 512
DC = 512
DIN = 2560
CK = 31
BLK = 128
DILS = (1, 4, 16)
EPS = 1e-6
NDEV = 8
MESH = pl.DeviceIdType.MESH

LR, B1, B2, AEPS, WD, STEP = 0.001, 0.9, 0.999, 1e-08, 0.01, 10

NT = (((1,), (1,)), ((), ()))
TN = (((0,), (0,)), ((), ()))

VMEM_LIMIT = 60 * 1024 * 1024


def _cp(sem=None):
    return pltpu.CompilerParams(dimension_semantics=sem, vmem_limit_bytes=VMEM_LIMIT)


def _sigmoid(x):
    return 0.5 * (jnp.tanh(0.5 * x) + 1.0)


def _pallas(body, args, *, grid, in_specs, out_specs, out_shape, scratch_shapes, sem, name, carry=None):
    if carry is None:
        outs = pl.pallas_call(body, grid=grid, in_specs=in_specs, out_specs=out_specs, out_shape=out_shape,
                              scratch_shapes=scratch_shapes, compiler_params=_cp(sem), name=name)(*args)
        return outs, None
    n_in, n_out, n_scr = len(in_specs), len(out_shape), len(scratch_shapes)
    c_in, c_out = len(carry.in_arrays), len(carry.out_shape)

    def wrapped(*refs):
        ins, refs = refs[:n_in], refs[n_in:]
        cins, refs = refs[:c_in], refs[c_in:]
        outs, refs = refs[:n_out], refs[n_out:]
        couts, refs = refs[:c_out], refs[c_out:]
        scr, cscr = refs[:n_scr], refs[n_scr:]
        ids = [pl.program_id(a) for a in range(len(grid))]
        step = ids[0]
        for i, n in zip(ids[1:], grid[1:]):
            step = step * n + i
        steps = functools.reduce(lambda a, b: a * b, grid)

        @pl.when(step == 0)
        def _():
            carry.start(cins, couts, cscr)

        body(*ins, *outs, *scr)

        for frac, stage in carry.stages():
            @pl.when(step == int(steps * frac))
            def _(stage=stage):
                stage(cins, couts, cscr)

        @pl.when(step == steps - 1)
        def _():
            carry.finish(cins, couts, cscr)

    outs = pl.pallas_call(
        wrapped, grid=grid, in_specs=list(in_specs) + carry.in_specs, out_specs=list(out_specs) + carry.out_specs,
        out_shape=list(out_shape) + carry.out_shape, scratch_shapes=list(scratch_shapes) + carry.scratch,
        compiler_params=_cp(("arbitrary",) * len(grid)), name=name)(*args, *carry.in_arrays)
    return outs[:n_out], outs[n_out:]


FC = 256


def _resident(shape):
    return pl.BlockSpec(shape, lambda *_: (0,) * len(shape), pipeline_mode=pl.Buffered(1))


def _mix_out_ffn_loss(h1, attn, conv, wout, gain, wg, wu, wd, target, name):
    T = h1.shape[0]
    tm = 512
    nt = T // tm

    def body(h1_ref, at_ref, cv_ref, wo_ref, gain_ref, wg_ref, wu_ref, wd_ref, t_ref,
             h2_ref, n_ref, g_ref, u_ref, dout_ref, dyb_ref, sq_ref, a_hbm, a_scr, a_sem):
        t = pl.program_id(0)
        a_out = lambda i: pltpu.make_async_copy(a_scr, a_hbm.at[pl.ds(pl.multiple_of(i * tm, tm), tm), :], a_sem)

        @pl.when(t > 0)
        def _():
            a_out(t - 1).wait()

        xv = (h1_ref[...]
              + jnp.dot(at_ref[...], wo_ref[0:DA, :], preferred_element_type=F32)
              + jnp.dot(cv_ref[...], wo_ref[DA:D, :], preferred_element_type=F32))
        h2_ref[...] = xv
        r = lax.rsqrt(jnp.mean(xv * xv, axis=-1, keepdims=True) + EPS)
        n_ref[...] = (xv * r * gain_ref[...]).astype(BF16)
        for c in range(FF // FC):
            cols = slice(c * FC, (c + 1) * FC)
            nb = n_ref[...]
            g = lax.dot_general(nb, wg_ref[cols, :], NT, preferred_element_type=F32)
            u = lax.dot_general(nb, wu_ref[cols, :], NT, preferred_element_type=F32)
            g_ref[:, cols] = g.astype(BF16)
            u_ref[:, cols] = u.astype(BF16)
            a_scr[:, cols] = (g * _sigmoid(g) * u).astype(BF16)
        a_out(t).start()
        e = h2_ref[...] + 0.5 * jnp.dot(a_scr[...], wd_ref[...], preferred_element_type=F32) - t_ref[...]
        dout = e * (1.0 / D)
        dout_ref[...] = dout
        dyb_ref[...] = (0.5 * dout).astype(BF16)
        sq_ref[...] = jnp.sum(e * e, axis=0, keepdims=True)[None]

        @pl.when(t == nt - 1)
        def _():
            a_out(t).wait()

    row = pl.BlockSpec((tm, D), lambda t: (t, 0))
    half = pl.BlockSpec((tm, DA), lambda t: (t, 0))
    wide = pl.BlockSpec((tm, FF), lambda t: (t, 0))
    outs, _ = _pallas(
        body, (h1, attn, conv, wout, gain, wg, wu, wd, target), grid=(nt,),
        in_specs=[row, half, half, _resident((D, D)), _resident((1, D)), _resident((FF, D)), _resident((FF, D)),
                  _resident((FF, D)), row],
        out_specs=[row, row, wide, wide, row, row, pl.BlockSpec((1, 1, D), lambda t: (t, 0, 0)), HBM],
        out_shape=[jax.ShapeDtypeStruct((T, D), F32), jax.ShapeDtypeStruct((T, D), BF16)]
                  + [jax.ShapeDtypeStruct((T, FF), BF16)] * 2
                  + [jax.ShapeDtypeStruct((T, D), F32), jax.ShapeDtypeStruct((T, D), BF16),
                     jax.ShapeDtypeStruct((nt, 1, D), F32), jax.ShapeDtypeStruct((T, FF), BF16)],
        scratch_shapes=[pltpu.VMEM((tm, FF), BF16), pltpu.SemaphoreType.DMA(())],
        sem=("arbitrary",), name=name)
    return outs


def _ffn_gate_up(x, gain, wg, wu, name, carry=None):
    T = x.shape[0]
    tm = 512

    def body(x_ref, gain_ref, wg_ref, wu_ref, n_ref, g_ref, u_ref, a_ref):
        xv = x_ref[...]
        r = lax.rsqrt(jnp.mean(xv * xv, axis=-1, keepdims=True) + EPS)
        n_ref[...] = (xv * r * gain_ref[...]).astype(BF16)
        for c in range(FF // FC):
            cols = slice(c * FC, (c + 1) * FC)
            nb = n_ref[...]
            g = lax.dot_general(nb, wg_ref[cols, :], NT, preferred_element_type=F32)
            u = lax.dot_general(nb, wu_ref[cols, :], NT, preferred_element_type=F32)
            g_ref[:, cols] = g.astype(BF16)
            u_ref[:, cols] = u.astype(BF16)
            a_ref[:, cols] = (g * _sigmoid(g) * u).astype(BF16)

    row = pl.BlockSpec((tm, D), lambda t: (t, 0))
    wide = pl.BlockSpec((tm, FF), lambda t: (t, 0))
    return _pallas(
        body, (x, gain, wg, wu), grid=(T // tm,),
        in_specs=[row, _resident((1, D)), _resident((FF, D)), _resident((FF, D))],
        out_specs=[row, wide, wide, wide],
        out_shape=[jax.ShapeDtypeStruct((T, D), BF16)] + [jax.ShapeDtypeStruct((T, FF), BF16)] * 3,
        scratch_shapes=[], sem=("parallel",), name=name, carry=carry)


def _ffn_down_mix_in(x, a, wd, gain, win, name):
    T = x.shape[0]
    tm = 512

    def body(x_ref, a_ref, wd_ref, gain_ref, win_ref, h_ref, u_ref, n_ref):
        hv = x_ref[...] + 0.5 * jnp.dot(a_ref[...], wd_ref[...], preferred_element_type=F32)
        h_ref[...] = hv
        r = lax.rsqrt(jnp.mean(hv * hv, axis=-1, keepdims=True) + EPS)
        n_ref[...] = (hv * r * gain_ref[...]).astype(BF16)
        u_ref[...] = lax.dot_general(n_ref[...], win_ref[...], NT, preferred_element_type=F32).astype(BF16)

    row = pl.BlockSpec((tm, D), lambda t: (t, 0))
    wide = pl.BlockSpec((tm, FF), lambda t: (t, 0))
    outs, _ = _pallas(
        body, (x, a, wd, gain, win), grid=(T // tm,),
        in_specs=[row, wide, _resident((FF, D)), _resident((1, D)), _resident((DIN, D))],
        out_specs=[row, pl.BlockSpec((tm, DIN), lambda t: (t, 0)), row],
        out_shape=[jax.ShapeDtypeStruct((T, D), F32), jax.ShapeDtypeStruct((T, DIN), BF16),
                   jax.ShapeDtypeStruct((T, D), BF16)],
        scratch_shapes=[], sem=("parallel",), name=name)
    return outs


def _ffn_bwd_act(dyb, g, u, x, dout, gain, wg, wu, wd, name, carry=None):
    T = x.shape[0]
    tm = 256
    nt = T // tm

    def body(dy_ref, g_ref, u_ref, x_ref, dout_ref, gain_ref, wg_ref, wu_ref, wd_ref,
             dg_ref, du_ref, dx_ref, dgn_ref):
        for c in range(FF // FC):
            cols = slice(c * FC, (c + 1) * FC)
            da = lax.dot_general(dy_ref[...], wd_ref[cols, :], NT, preferred_element_type=F32)
            gv = g_ref[:, cols].astype(F32)
            uv = u_ref[:, cols].astype(F32)
            sg = _sigmoid(gv)
            dg_ref[:, cols] = (da * uv * (sg * (1.0 + gv * (1.0 - sg)))).astype(BF16)
            du_ref[:, cols] = (da * (gv * sg)).astype(BF16)
        dn = (jnp.dot(dg_ref[...], wg_ref[...], preferred_element_type=F32)
              + jnp.dot(du_ref[...], wu_ref[...], preferred_element_type=F32))
        dx, dgain = _rms_bwd_rows(dn, x_ref[...], gain_ref[...])
        dx_ref[...] = dout_ref[...] + dx

        @pl.when(pl.program_id(0) == 0)
        def _():
            dgn_ref[...] = jnp.zeros_like(dgn_ref)

        dgn_ref[...] += dgain[None]

    row = pl.BlockSpec((tm, D), lambda t: (t, 0))
    wide = pl.BlockSpec((tm, FF), lambda t: (t, 0))
    return _pallas(
        body, (dyb, g, u, x, dout, gain, wg, wu, wd), grid=(nt,),
        in_specs=[row, wide, wide, row, row, _resident((1, D)), _resident((FF, D)), _resident((FF, D)),
                  _resident((FF, D))],
        out_specs=[wide, wide, row, pl.BlockSpec((1, 1, D), lambda t: (0, 0, 0))],
        out_shape=[jax.ShapeDtypeStruct((T, FF), BF16)] * 2
                  + [jax.ShapeDtypeStruct((T, D), F32), jax.ShapeDtypeStruct((1, 1, D), F32)],
        scratch_shapes=[], sem=("arbitrary",), name=name, carry=carry)


def _ffn_bwd_w(lhs, rhs, name, carry=None):
    T = rhs.shape[0]
    tf = 256

    def body(l_ref, r_ref, dw_ref):
        dw_ref[...] = lax.dot_general(l_ref[...], r_ref[...], TN, preferred_element_type=F32).astype(BF16)

    (dw,), got = _pallas(
        body, (lhs, rhs), grid=(FF // tf,),
        in_specs=[pl.BlockSpec((T, tf), lambda f: (0, f)), _resident((T, D))],
        out_specs=[pl.BlockSpec((tf, D), lambda f: (f, 0))], out_shape=[jax.ShapeDtypeStruct((FF, D), BF16)],
        scratch_shapes=[], sem=("parallel",), name=name, carry=carry)
    return dw, got


def _rms_bwd_rows(dn, xv, gain):
    r = lax.rsqrt(jnp.mean(xv * xv, axis=-1, keepdims=True) + EPS)
    xhat = xv * r
    dxhat = dn * gain
    dx = r * (dxhat - xhat * jnp.mean(dxhat * xhat, axis=-1, keepdims=True))
    return dx, jnp.sum(dn * xhat, axis=0, keepdims=True)


def _mix_out_bwd(dh, attn, conv, wout):
    T = dh.shape[0]
    tm = 512
    nt = T // tm

    def body(dh_ref, a_ref, c_ref, w_ref, da_ref, dc_ref, dw_ref, acc_scr):
        t = pl.program_id(0)

        @pl.when(t == 0)
        def _():
            acc_scr[...] = jnp.zeros_like(acc_scr)

        dhb = dh_ref[...].astype(BF16)
        dmix = lax.dot_general(dhb, w_ref[...], NT, preferred_element_type=F32)
        da_ref[...] = dmix[:, 0:DA].astype(BF16)
        dc_ref[...] = dmix[:, DA:D].astype(BF16)
        acc_scr[0:DA, :] += lax.dot_general(a_ref[...], dhb, TN, preferred_element_type=F32)
        acc_scr[DA:D, :] += lax.dot_general(c_ref[...], dhb, TN, preferred_element_type=F32)

        @pl.when(t == nt - 1)
        def _():
            dw_ref[...] = acc_scr[...].astype(BF16)

    row = pl.BlockSpec((tm, D), lambda t: (t, 0))
    half = pl.BlockSpec((tm, DA), lambda t: (t, 0))
    full = pl.BlockSpec((D, D), lambda t: (0, 0))
    return pl.pallas_call(
        body, grid=(nt,), in_specs=[row, half, half, full], out_specs=[half, half, full],
        out_shape=[jax.ShapeDtypeStruct((T, DA), BF16)] * 2 + [jax.ShapeDtypeStruct((D, D), BF16)],
        scratch_shapes=[pltpu.VMEM((D, D), F32)],
        compiler_params=_cp(("arbitrary",)), name="mix_out_bwd")(dh, attn, conv, wout)


def _mix_in_bwd(dparts, win, nb, h, dh, gain):
    T = h.shape[0]
    tm = 512
    nt = T // tm

    def body(d0, d1, d2, d3, d4, w_ref, n_ref, h_ref, dh_ref, gain_ref,
             dw_ref, dx_ref, dyb_ref, dg_ref, acc_scr):
        t = pl.program_id(0)

        @pl.when(t == 0)
        def _():
            acc_scr[...] = jnp.zeros_like(acc_scr)

        n = n_ref[...]
        dn = jnp.zeros((tm, D), F32)
        for i, d_ref in enumerate((d0, d1, d2, d3, d4)):
            dv = d_ref[...]
            dn = dn + jnp.dot(dv, w_ref[i * DA:(i + 1) * DA, :], preferred_element_type=F32)
            acc_scr[i * DA:(i + 1) * DA, :] += lax.dot_general(dv, n, TN, preferred_element_type=F32)
        dx, dgain = _rms_bwd_rows(dn, h_ref[...], gain_ref[...])
        tot = dh_ref[...] + dx
        dx_ref[...] = tot
        dyb_ref[...] = (0.5 * tot).astype(BF16)
        dg_ref[...] = dgain[None]

        @pl.when(t == nt - 1)
        def _():
            dw_ref[...] = acc_scr[...].astype(BF16)

    row = pl.BlockSpec((tm, D), lambda t: (t, 0))
    half = pl.BlockSpec((tm, DA), lambda t: (t, 0))
    full = pl.BlockSpec((DIN, D), lambda t: (0, 0))
    return pl.pallas_call(
        body, grid=(nt,),
        in_specs=[half] * 5 + [full, row, row, row, pl.BlockSpec((1, D), lambda t: (0, 0))],
        out_specs=[full, row, row, pl.BlockSpec((1, 1, D), lambda t: (t, 0, 0))],
        out_shape=[jax.ShapeDtypeStruct((DIN, D), BF16), jax.ShapeDtypeStruct((T, D), F32),
                   jax.ShapeDtypeStruct((T, D), BF16), jax.ShapeDtypeStruct((nt, 1, D), F32)],
        scratch_shapes=[pltpu.VMEM((DIN, D), F32)],
        compiler_params=_cp(("arbitrary",)), name="mix_in_bwd")(*dparts, win, nb, h, dh, gain)


def _head_masks():
    lane = lax.broadcasted_iota(jnp.int32, (1, 2 * HD), 1)
    m0 = lane < HD
    return m0, jnp.logical_not(m0)


def _stack_heads(v, m0, m1):
    z = jnp.zeros_like(v)
    return jnp.concatenate([jnp.where(m0, v, z), jnp.where(m1, v, z)], axis=0)


def _unstack_heads(v2, m0):
    return jnp.where(m0, v2[0:BLK], v2[BLK:2 * BLK])


def _head_sums(xv):
    ri = lax.broadcasted_iota(jnp.int32, (2 * HD, 2 * HD), 0)
    ci = lax.broadcasted_iota(jnp.int32, (2 * HD, 2 * HD), 1)
    ones = jnp.where((ri < HD) == (ci < HD), 1.0, 0.0).astype(BF16)
    hi = xv.astype(BF16)
    lo = (xv - hi.astype(F32)).astype(BF16)
    return (jnp.dot(hi, ones, preferred_element_type=F32) + jnp.dot(lo, ones, preferred_element_type=F32))


def _head_rms(xv):
    return lax.rsqrt(_head_sums(xv * xv) * (1.0 / HD) + EPS)


def _band_mask(first):
    qi = lax.broadcasted_iota(jnp.int32, (BLK, 2 * BLK), 0)
    ci = lax.broadcasted_iota(jnp.int32, (BLK, 2 * BLK), 1)
    band = (ci >= qi) & (ci <= qi + BLK)
    return band & ((ci >= BLK) | jnp.logical_not(first))


def _block_rows(j, d, seg):
    r, n = j // seg, j % seg
    start = r + (d * BLK) * n
    first = n == 0
    prev = jnp.where(first, start, start - d * BLK)
    return pl.ds(start, BLK, stride=d), pl.ds(prev, BLK, stride=d), first


def _block_keys(refs, cur, prev, first, single):
    if single:
        qi = lax.broadcasted_iota(jnp.int32, (BLK, BLK), 0)
        ci = lax.broadcasted_iota(jnp.int32, (BLK, BLK), 1)
        return [r[cur, :].astype(BF16) for r in refs], ci <= qi
    return ([jnp.concatenate([r[prev, :], r[cur, :]], axis=0).astype(BF16) for r in refs], _band_mask(first))


def _attn_fwd(u, qg2, kg2, B, S, carry=None):
    T = B * S
    NB = S // BLK
    scale = HD ** -0.5

    def body(q_ref, k_ref, v_ref, qg_ref, kg_ref, o_ref, lse_ref, qn, kn, vn, os_, ls_):
        m0, m1 = _head_masks()
        qv = q_ref[...].astype(F32)
        qn[...] = qv * _head_rms(qv) * (qg_ref[...] * scale)
        kv = k_ref[...].astype(F32)
        kn[...] = kv * _head_rms(kv) * kg_ref[...]
        vn[...] = v_ref[...].astype(F32)

        for i, d in enumerate(DILS):
            seg = NB // d

            def blk(j, c, i=i, d=d, seg=seg):
                cur, prev, first = _block_rows(j, d, seg)
                q2 = _stack_heads(qn[cur, :].astype(BF16), m0, m1)
                (kk, vv), mask = _block_keys((kn, vn), cur, prev, first, False)
                s = lax.dot_general(q2, kk, NT, preferred_element_type=F32)
                s = jnp.where(jnp.concatenate([mask, mask], axis=0), s, -1e30)
                mx = jnp.max(s, axis=-1, keepdims=True)
                p = jnp.exp(s - mx)
                l = jnp.sum(p, axis=-1, keepdims=True)
                o2 = jnp.dot((p * (1.0 / l)).astype(BF16), vv, preferred_element_type=F32)
                os_[i, cur, :] = _unstack_heads(o2, m0)
                ls_[i, cur, :] = _unstack_heads(mx + jnp.log(l), m0)
                return c

            lax.fori_loop(0, NB, blk, 0, unroll=8)

        def comb(c, carry):
            rows = pl.ds(pl.multiple_of(c * 256, 256), 256)
            l0, l1, l2 = ls_[0, rows, :], ls_[1, rows, :], ls_[2, rows, :]
            mx = jnp.maximum(jnp.maximum(l0, l1), l2)
            e0, e1, e2 = jnp.exp(l0 - mx), jnp.exp(l1 - mx), jnp.exp(l2 - mx)
            tot = e0 + e1 + e2
            inv = 1.0 / tot
            o = (e0 * os_[0, rows, :] + e1 * os_[1, rows, :] + e2 * os_[2, rows, :]) * inv
            o_ref[rows, :] = o.astype(BF16)
            lse_ref[rows, :] = mx + jnp.log(tot)
            return carry

        lax.fori_loop(0, S // 256, comb, 0)

    pair = 2 * HD
    blk_spec = lambda off: pl.BlockSpec((S, pair), lambda b, p, off=off: (b, off + p))
    gspec = pl.BlockSpec((1, pair), lambda b, p: (0, 0))
    return _pallas(
        body, (u, u, u, qg2, kg2), grid=(B, DA // pair),
        in_specs=[blk_spec(0), blk_spec(DA // pair), blk_spec(2 * DA // pair), gspec, gspec],
        out_specs=[blk_spec(0), blk_spec(0)],
        out_shape=[jax.ShapeDtypeStruct((T, DA), BF16), jax.ShapeDtypeStruct((T, DA), F32)],
        scratch_shapes=[pltpu.VMEM((S, pair), F32)] * 3 + [pltpu.VMEM((3, S, pair), F32)] * 2,
        sem=("parallel", "parallel"), name="attn_fwd", carry=carry)


def _attn_bwd(u, attn, dattn, lse, qg2, kg2, B, S, carry=None):
    T = B * S
    NB = S // BLK
    scale = HD ** -0.5
    pair = 2 * HD

    def body(q_ref, k_ref, v_ref, o_ref, do_ref, lse_ref, qg_ref, kg_ref,
             dq_ref, dk_ref, dv_ref, dgn_ref,
             qn, kn, vn, don, ldl, accq, acck, accv, rq, rk):
        m0, m1 = _head_masks()
        lane = lax.broadcasted_iota(jnp.int32, (1, pair), 1)
        qv = q_ref[...].astype(F32)
        rq[...] = _head_rms(qv)
        qn[...] = qv * rq[...] * (qg_ref[...] * scale)
        kv = k_ref[...].astype(F32)
        rk[...] = _head_rms(kv)
        kn[...] = kv * rk[...] * kg_ref[...]
        vn[...] = v_ref[...].astype(F32)
        dov = do_ref[...].astype(F32)
        don[...] = dov
        ldl[...] = jnp.where((lane % HD) < HD // 2, lse_ref[...], _head_sums(dov * o_ref[...].astype(F32)))

        for i, d in enumerate(DILS):
            seg = NB // d

            def blk(j, c, i=i, d=d, seg=seg):
                cur, prev, first = _block_rows(j, d, seg)
                q2 = _stack_heads(qn[cur, :].astype(BF16), m0, m1)
                do2 = _stack_heads(don[cur, :].astype(BF16), m0, m1)
                (kk, vv), mask = _block_keys((kn, vn), cur, prev, first, seg == 1)
                ldv = ldl[cur, :]
                lse2 = jnp.concatenate([ldv[:, 0:1], ldv[:, HD:HD + 1]], axis=0)
                dl2 = jnp.concatenate([ldv[:, HD // 2:HD // 2 + 1], ldv[:, HD + HD // 2:HD + HD // 2 + 1]], axis=0)
                s = lax.dot_general(q2, kk, NT, preferred_element_type=F32)
                p = jnp.where(jnp.concatenate([mask, mask], axis=0), jnp.exp(s - lse2), 0.0)
                dp = lax.dot_general(do2, vv, NT, preferred_element_type=F32)
                ds = (p * (dp - dl2)).astype(BF16)
                dq_acc = _unstack_heads(jnp.dot(ds, kk, preferred_element_type=F32), m0)
                dk_acc = lax.dot_general(ds, q2, TN, preferred_element_type=F32)
                dv_acc = lax.dot_general(p.astype(BF16), do2, TN, preferred_element_type=F32)
                if i == 0:
                    accq[cur, :] = dq_acc
                    acck[cur, :] = dk_acc[BLK:2 * BLK]
                    accv[cur, :] = dv_acc[BLK:2 * BLK]
                    acck[prev, :] += dk_acc[0:BLK]
                    accv[prev, :] += dv_acc[0:BLK]
                elif seg == 1:
                    accq[cur, :] += dq_acc
                    acck[cur, :] += dk_acc
                    accv[cur, :] += dv_acc
                else:
                    accq[cur, :] += dq_acc
                    acck[prev, :] += dk_acc[0:BLK]
                    acck[cur, :] += dk_acc[BLK:2 * BLK]
                    accv[prev, :] += dv_acc[0:BLK]
                    accv[cur, :] += dv_acc[BLK:2 * BLK]
                return c

            lax.fori_loop(0, NB, blk, 0, unroll=8)

        def norm_bwd(x_ref, r_ref, dn, gain):
            r = r_ref[...]
            xhat = x_ref[...].astype(F32) * r
            dxhat = dn * gain
            dx = r * (dxhat - xhat * (_head_sums(dxhat * xhat) * (1.0 / HD)))
            return dx, jnp.sum(dn * xhat, axis=0, keepdims=True)

        dq, dgq = norm_bwd(q_ref, rq, accq[...], qg_ref[...] * scale)
        dk, dgk = norm_bwd(k_ref, rk, acck[...], kg_ref[...])
        dq_ref[...] = dq.astype(BF16)
        dk_ref[...] = dk.astype(BF16)
        dv_ref[...] = accv[...].astype(BF16)
        dgn_ref[...] = jnp.concatenate([dgq * scale, dgk, jnp.zeros((6, pair), F32)], axis=0)[None]

    blk_spec = lambda off: pl.BlockSpec((S, pair), lambda b, p, off=off: (b, off + p))
    gspec = pl.BlockSpec((1, pair), lambda b, p: (0, 0))
    np_ = DA // pair
    return _pallas(
        body, (u, u, u, attn, dattn, lse, qg2, kg2), grid=(B, np_),
        in_specs=[blk_spec(0), blk_spec(np_), blk_spec(2 * np_), blk_spec(0), blk_spec(0), blk_spec(0),
                  gspec, gspec],
        out_specs=[blk_spec(0), blk_spec(0), blk_spec(0),
                   pl.BlockSpec((1, 8, pair), lambda b, p: (b * np_ + p, 0, 0))],
        out_shape=[jax.ShapeDtypeStruct((T, DA), BF16)] * 3 + [jax.ShapeDtypeStruct((B * np_, 8, pair), F32)],
        scratch_shapes=[pltpu.VMEM((S, pair), F32)] * 10,
        sem=("parallel", "parallel"), name="attn_bwd", carry=carry)


CT = 32
CPAD = 32


def _shifted(win, offsets):
    rolled, out = {}, {}
    n = win.shape[0]
    for o in offsets:
        sub = o % 8
        if sub not in rolled:
            rolled[sub] = win if sub == 0 else pltpu.roll(win, n - sub, 0)
        out[o] = rolled[sub][o - sub:o - sub + CT, :]
    return out


def _ln_fwd(y, g, b):
    mu = jnp.mean(y, axis=-1, keepdims=True)
    yc = y - mu
    rstd = lax.rsqrt(jnp.mean(yc * yc, axis=-1, keepdims=True) + EPS)
    xhat = yc * rstd
    return xhat, rstd, xhat * g + b


def _fill_glu(ca_ref, cg_ref, glu, S):
    glu[pl.ds(0, CPAD), :] = jnp.zeros((CPAD, DC), F32)

    def fill(i, c):
        rows = pl.ds(pl.multiple_of(i * 256, 256), 256)
        a = ca_ref[rows, :].astype(F32)
        gt = cg_ref[rows, :].astype(F32)
        glu[pl.ds(pl.multiple_of(CPAD + i * 256, CT), 256), :] = a * _sigmoid(gt)
        return c

    lax.fori_loop(0, S // 256, fill, 0)


def _conv_fwd(u, cw, cb, lg, lb, B, S):
    T = B * S

    def body(ca_ref, cg_ref, w_ref, b_ref, lg_ref, lb_ref, o_ref, y_ref, glu):
        _fill_glu(ca_ref, cg_ref, glu, S)

        def step(i, c):
            t0 = pl.multiple_of(i * CT, CT)
            win = glu[pl.ds(t0, 2 * CT), :]
            acc = jnp.zeros((CT, DC), F32) + b_ref[...]
            taps = _shifted(win, [k + 2 for k in range(CK)])
            for k in range(CK):
                acc = acc + taps[k + 2] * w_ref[k:k + 1, :]
            y_ref[pl.ds(t0, CT), :] = acc
            _, _, z = _ln_fwd(acc, lg_ref[...], lb_ref[...])
            o_ref[pl.ds(t0, CT), :] = (z * _sigmoid(z)).astype(BF16)
            return c

        lax.fori_loop(0, S // CT, step, 0, unroll=4)

    vec = pl.BlockSpec((1, DC), lambda b: (0, 0))
    return pl.pallas_call(
        body, grid=(B,),
        in_specs=[pl.BlockSpec((S, DC), lambda b: (b, 3)), pl.BlockSpec((S, DC), lambda b: (b, 4)),
                  pl.BlockSpec((CT, DC), lambda b: (0, 0)), vec, vec, vec],
        out_specs=[pl.BlockSpec((S, DC), lambda b: (b, 0))] * 2,
        out_shape=[jax.ShapeDtypeStruct((T, DC), BF16), jax.ShapeDtypeStruct((T, DC), F32)],
        scratch_shapes=[pltpu.VMEM((CPAD + S, DC), F32)],
        compiler_params=_cp(("parallel",)), name="conv_fwd")(u, u, cw, cb, lg, lb)


def _conv_bwd(u, y, dconv, cw, lg, lb, B, S):
    T = B * S

    def body(ca_ref, cg_ref, y_ref, dc_ref, w_ref, lg_ref, lb_ref,
             dca_ref, dcg_ref, dw_ref, ds_ref, glu, dyp, dwacc):
        _fill_glu(ca_ref, cg_ref, glu, S)
        dyp[pl.ds(S, CPAD), :] = jnp.zeros((CPAD, DC), F32)
        lgv, lbv = lg_ref[...], lb_ref[...]

        def sum8(v):
            return functools.reduce(jnp.add, [v[r:r + 8] for r in range(0, v.shape[0], 8)])

        P1 = 4 * CT

        def p1(i, carry):
            sb, sg, sl = carry
            t0 = pl.multiple_of(i * P1, P1)
            xhat, rstd, z = _ln_fwd(y_ref[pl.ds(t0, P1), :], lgv, lbv)
            sz = _sigmoid(z)
            dz = dc_ref[pl.ds(t0, P1), :].astype(F32) * (sz * (1.0 + z * (1.0 - sz)))
            dxhat = dz * lgv
            dy = rstd * (dxhat - jnp.mean(dxhat, axis=-1, keepdims=True)
                         - xhat * jnp.mean(dxhat * xhat, axis=-1, keepdims=True))
            dyp[pl.ds(t0, P1), :] = dy
            return sb + sum8(dy), sg + sum8(dz * xhat), sl + sum8(dz)

        z8 = jnp.zeros((8, DC), F32)
        sb, sg, sl = lax.fori_loop(0, S // P1, p1, (z8, z8, z8))
        rs = lambda v: jnp.sum(v, axis=0, keepdims=True)
        ds_ref[...] = jnp.concatenate([rs(sb), rs(sg), rs(sl), jnp.zeros((5, DC), F32)], axis=0)[None]

        def p2(i, c):
            t0 = pl.multiple_of(i * CT, CT)
            win = dyp[pl.ds(t0, 2 * CT), :]
            acc = jnp.zeros((CT, DC), F32)
            taps = _shifted(win, [30 - k for k in range(CK)])
            for k in range(CK):
                acc = acc + taps[30 - k] * w_ref[k:k + 1, :]
            a = ca_ref[pl.ds(t0, CT), :].astype(F32)
            sgt = _sigmoid(cg_ref[pl.ds(t0, CT), :].astype(F32))
            dca_ref[pl.ds(t0, CT), :] = (acc * sgt).astype(BF16)
            dcg_ref[pl.ds(t0, CT), :] = (acc * a * sgt * (1.0 - sgt)).astype(BF16)
            return c

        lax.fori_loop(0, S // CT, p2, 0)

        dwacc[...] = jnp.zeros_like(dwacc)

        def p3(i, c):
            t0 = pl.multiple_of(i * CT, CT)
            win = glu[pl.ds(t0, 2 * CT), :]
            dy = dyp[pl.ds(t0, CT), :]
            for k in range(CK):
                dwacc[k] += sum8(dy * win[k + 2:k + 2 + CT, :])
            return c

        lax.fori_loop(0, S // CT, p3, 0)
        dw_ref[...] = jnp.sum(dwacc[...], axis=1)[None]

    vec = pl.BlockSpec((1, DC), lambda b: (0, 0))
    seq = pl.BlockSpec((S, DC), lambda b: (b, 0))
    return pl.pallas_call(
        body, grid=(B,),
        in_specs=[pl.BlockSpec((S, DC), lambda b: (b, 3)), pl.BlockSpec((S, DC), lambda b: (b, 4)),
                  seq, seq, pl.BlockSpec((CT, DC), lambda b: (0, 0)), vec, vec],
        out_specs=[seq, seq, pl.BlockSpec((1, CT, DC), lambda b: (b, 0, 0)),
                   pl.BlockSpec((1, 8, DC), lambda b: (b, 0, 0))],
        out_shape=[jax.ShapeDtypeStruct((T, DC), BF16)] * 2
                  + [jax.ShapeDtypeStruct((B, CT, DC), F32), jax.ShapeDtypeStruct((B, 8, DC), F32)],
        scratch_shapes=[pltpu.VMEM((CPAD + S, DC), F32), pltpu.VMEM((S + CPAD, DC), F32),
                        pltpu.VMEM((CT, 8, DC), F32)],
        compiler_params=_cp(("parallel",)), name="conv_bwd")(u, u, y, dconv, cw, lg, lb)


def _local_step(x, target, norms, W, B, S, comm=None):
    qg2 = jnp.concatenate([norms["q_norm"], norms["q_norm"]], axis=1)
    kg2 = jnp.concatenate([norms["k_norm"], norms["k_norm"]], axis=1)
    cw = jnp.concatenate([W["conv_w"], jnp.zeros((1, DC), F32)], axis=0)

    W = dict(W)
    (n1, g1, u1, act1), got = _ffn_gate_up(x, norms["ffn1_norm"], W["wg1"], W["wu1"], "ffn1_gate_up",
                                           carry=comm.gathers["down_in"] if comm else None)
    if comm:
        W.update(comm.gathered("down_in", got))
    h1, u, n2 = _ffn_down_mix_in(x, act1, W["wd1"], norms["mix_norm"], W["win"], "ffn1_down_mix_in")
    (attn, lse), got = _attn_fwd(u, qg2, kg2, B, S, carry=comm.gathers["ffn2"] if comm else None)
    if comm:
        W = dict(W, **comm.gathered("ffn2", got))
    conv, y = _conv_fwd(u, cw, norms["conv_b"], norms["conv_ln_g"], norms["conv_ln_b"], B, S)
    h2, n3, g2, u2, dout, dyb, sq, act2 = _mix_out_ffn_loss(h1, attn, conv, W["wout"], norms["ffn2_norm"],
                                                            W["wg2"], W["wu2"], W["wd2"], target, "ffn2_fwd")
    loss = (0.5 / D) * jnp.sum(sq)

    (dg2, du2, dh2, dgn_ffn2), _ = _ffn_bwd_act(dyb, g2, u2, h2, dout, norms["ffn2_norm"],
                                               W["wg2"], W["wu2"], W["wd2"], "ffn2_bwd_act")
    dwd2, _ = _ffn_bwd_w(act2, dyb, "ffn2_bwd_wd")
    dwg2, _ = _ffn_bwd_w(dg2, n3, "ffn2_bwd_wg")
    dwu2, _ = _ffn_bwd_w(du2, n3, "ffn2_bwd_wu")
    dattn, dconv, dwout = _mix_out_bwd(dh2, attn, conv, W["wout"])
    carry = comm.reduce_start({"wg2": dwg2, "wu2": dwu2, "wd2": dwd2, "wout": dwout}) if comm else None
    (dq, dk, dv, dgn_qk), got = _attn_bwd(u, attn, dattn, lse, qg2, kg2, B, S, carry=carry)
    if comm:
        comm.reduce_done(carry, got)
    dca, dcg, dcw, dcs = _conv_bwd(u, y, dconv, cw, norms["conv_ln_g"], norms["conv_ln_b"], B, S)
    dwin, dh1, dyb1, dgn_mix = _mix_in_bwd((dq, dk, dv, dca, dcg), W["win"], n2, h1, dh2, norms["mix_norm"])
    carry = comm.reduce_start({"win": dwin}) if comm else None
    dwd1, got = _ffn_bwd_w(act1, dyb1, "ffn1_bwd_wd", carry=carry)
    if comm:
        comm.reduce_done(carry, got)
    carry = comm.update_start(("wg2", "wu2", "wd2", "wout", "win")) if comm else None
    (dg1, du1, gx, dgn_ffn1), got = _ffn_bwd_act(dyb1, g1, u1, x, dh1, norms["ffn1_norm"],
                                                W["wg1"], W["wu1"], W["wd1"], "ffn1_bwd_act", carry=carry)
    if comm:
        comm.update_done(carry, got)
        carry = comm.reduce_start({"wd1": dwd1})
    dwg1, got = _ffn_bwd_w(dg1, n1, "ffn1_bwd_wg", carry=carry)
    if comm:
        comm.reduce_done(carry, got)
        carry = comm.reduce_start({"wg1": dwg1})
    dwu1, got = _ffn_bwd_w(du1, n1, "ffn1_bwd_wu", carry=carry)
    if comm:
        comm.reduce_done(carry, got)
        comm.last = comm.reduce_start({"wu1": dwu1})

    qk = jnp.sum(dgn_qk, axis=0)
    cs = jnp.sum(dcs, axis=0)
    small = {
        "ffn1_norm": jnp.sum(dgn_ffn1, axis=0),
        "mix_norm": jnp.sum(dgn_mix, axis=0),
        "q_norm": qk[0:1, 0:HD] + qk[0:1, HD:2 * HD],
        "k_norm": qk[1:2, 0:HD] + qk[1:2, HD:2 * HD],
        "conv_w": jnp.sum(dcw, axis=0)[0:CK],
        "conv_b": cs[0:1],
        "conv_ln_g": cs[1:2],
        "conv_ln_b": cs[2:3],
        "ffn2_norm": jnp.sum(dgn_ffn2, axis=0),
    }
    big = {"wg1": dwg1, "wu1": dwu1, "wd1": dwd1, "win": dwin, "wout": dwout,
           "wg2": dwg2, "wu2": dwu2, "wd2": dwd2}
    return loss, gx, big, small


LOCAL = 1
HBM = pl.BlockSpec(memory_space=pltpu.HBM)
VMEM = pl.BlockSpec(memory_space=pltpu.VMEM)


def _place():
    return lax.axis_index("x"), lax.axis_index("y"), lax.axis_index("c")


class _GatherCarry:
    def __init__(self, shards, mid_at=0.5):
        nt = len(shards)
        self.mid_at = mid_at
        self.shards = shards
        self.in_arrays = [s for s, _ in shards]
        self.in_specs = [VMEM] * nt
        self.out_shape = [jax.ShapeDtypeStruct((NDEV * s.shape[0], s.shape[1]), dt) for s, dt in shards]
        self.out_specs = [HBM] * nt
        self.scratch = ([pltpu.VMEM(s.shape, dt) for s, dt in shards]
                        + [pltpu.SemaphoreType.DMA((nt, 7)), pltpu.SemaphoreType.DMA((nt, 7)),
                           pltpu.SemaphoreType.DMA((nt,))])

    def _copies(self, outs, scr):
        nt = len(self.shards)
        stages = scr[:nt]
        send_sems, recv_sems, local_sems = scr[nt:]
        x, y, c = _place()
        me, sibling = (x, y, c), (x, y, 1 - c)
        xn, yn, diag = (1 - x, y, c), (x, 1 - y, c), (1 - x, 1 - y, c)
        via = (x ^ c, y ^ (1 - c), c)
        onto = (x ^ (1 - c), y ^ c, c)

        def rows(t, px, py, pc):
            r = self.shards[t][0].shape[0]
            return outs[t].at[pl.ds((4 * px + 2 * py + pc) * r, r), :]

        def copy(t, k, block, to, src=None):
            return pltpu.make_async_remote_copy(
                src_ref=rows(t, *block) if src is None else src, dst_ref=rows(t, *block),
                send_sem=send_sems.at[t, k], recv_sem=recv_sems.at[t, k],
                device_id=to, device_id_type=MESH)

        sib = lambda b: (b[0], b[1], 1 - c)
        return dict(
            local=[pltpu.make_async_copy(stages[t], rows(t, *me), local_sems.at[t]) for t in range(nt)],
            own=[[copy(t, 0, me, sibling, src=stages[t]), copy(t, 1, me, xn, src=stages[t]),
                  copy(t, 2, me, yn, src=stages[t])] for t in range(nt)],
            relay=[copy(t, 3, via, onto) for t in range(nt)],
            down=[[copy(t, 4, xn, sibling), copy(t, 5, yn, sibling)] for t in range(nt)],
            down_diag=[copy(t, 6, diag, sibling) for t in range(nt)],
            got_xy=[[copy(t, 1, xn, me), copy(t, 2, yn, me)] for t in range(nt)],
            got_diag=[copy(t, 3, diag, me) for t in range(nt)],
            got_sib=[[copy(t, 0, sibling, me), copy(t, 4, sib(xn), me), copy(t, 5, sib(yn), me),
                      copy(t, 6, sib(diag), me)] for t in range(nt)])

    def start(self, ins, outs, scr):
        cps = self._copies(outs, scr)
        for t, (_, dt) in enumerate(self.shards):
            scr[t][...] = ins[t][...].astype(dt)
            cps["local"][t].start(priority=LOCAL)
            for cp in cps["own"][t]:
                cp.start()

    def stages(self):
        sizes = [s.size * jnp.dtype(dt).itemsize for s, dt in self.shards]
        done = [sum(sizes[:t + 1]) / sum(sizes) for t in range(len(sizes))]
        return [(self.mid_at * f, functools.partial(self.mid, t)) for t, f in enumerate(done)] + [(self.LATE_AT, self.late)]

    LATE_AT = 0.85

    def late(self, ins, outs, scr):
        cps = self._copies(outs, scr)
        for t in range(len(self.shards)):
            cps["got_diag"][t].wait_recv()
            cps["down_diag"][t].start()

    def mid(self, t, ins, outs, scr):
        cps = self._copies(outs, scr)
        for cp in cps["got_xy"][t]:
            cp.wait_recv()
        for cp in [cps["relay"][t]] + cps["down"][t]:
            cp.start()

    def finish(self, ins, outs, scr):
        cps = self._copies(outs, scr)
        for t in range(len(self.shards)):
            for cp in cps["got_sib"][t]:
                cp.wait_recv()
            for cp in cps["own"][t] + [cps["relay"][t]] + cps["down"][t] + [cps["down_diag"][t]]:
                cp.wait_send()
            cps["local"][t].wait()


def _run_carry(carry, name):
    def body(*refs):
        n_in, n_out = len(carry.in_arrays), len(carry.out_shape)
        ins, outs, scr = refs[:n_in], refs[n_in:n_in + n_out], refs[n_in + n_out:]
        carry.start(ins, outs, scr)
        for _, stage in carry.stages():
            stage(ins, outs, scr)
        carry.finish(ins, outs, scr)

    return pl.pallas_call(
        body, in_specs=carry.in_specs, out_specs=carry.out_specs, out_shape=carry.out_shape,
        scratch_shapes=carry.scratch, compiler_params=pltpu.CompilerParams(vmem_limit_bytes=VMEM_LIMIT),
        name=name)(*carry.in_arrays)


class _ExchangeCarry:
    def __init__(self, names, grads, mid_at=0.5):
        nt = len(grads)
        self.mid_at = mid_at
        self.names = names
        self.in_arrays = [g.reshape(4, 2, g.shape[0] // NDEV, g.shape[1]) for g in grads]
        self.in_specs = [HBM] * nt
        blocks = [g.shape[2:] for g in self.in_arrays]
        self.out_shape = [jax.ShapeDtypeStruct((3,) + b, BF16) for b in blocks]
        self.out_specs = [HBM] * nt
        self.scratch = ([pltpu.VMEM((4,) + b, BF16) for b in blocks] * 2 + [pltpu.VMEM(b, BF16) for b in blocks]
                        + [pltpu.SemaphoreType.DMA((nt, 3)), pltpu.SemaphoreType.DMA((nt, 3))]
                        + [pltpu.SemaphoreType.DMA((nt,))] * 4)

    def _copies(self, ins, outs, scr):
        nt = len(ins)
        theirs, own, relayed = scr[:nt], scr[nt:2 * nt], scr[2 * nt:3 * nt]
        send_sems, recv_sems, keep_sems, load_sems, swap_send, swap_recv = scr[3 * nt:]
        x, y, c = _place()
        q = lambda cx, cy: 2 * cx + cy
        near, far = (x ^ c, y ^ (1 - c)), (x ^ (1 - c), y ^ c)

        def remote(t, k, src, dst, chip):
            return pltpu.make_async_remote_copy(
                src_ref=src, dst_ref=dst, send_sem=send_sems.at[t, k], recv_sem=recv_sems.at[t, k],
                device_id=(*chip, c), device_id_type=MESH)

        return dict(
            swap=[pltpu.make_async_remote_copy(
                src_ref=ins[t].at[:, 1 - c], dst_ref=theirs[t], send_sem=swap_send.at[t], recv_sem=swap_recv.at[t],
                device_id=(x, y, 1 - c), device_id_type=MESH) for t in range(nt)],
            load=[pltpu.make_async_copy(ins[t].at[:, c], own[t], load_sems.at[t]) for t in range(nt)],
            keep=[pltpu.make_async_copy(own[t].at[q(x, y)], outs[t].at[0], keep_sems.at[t]) for t in range(nt)],
            direct=[remote(t, 0, own[t].at[q(*near)], outs[t].at[1], near) for t in range(nt)],
            relay=[remote(t, 1, own[t].at[q(1 - x, 1 - y)], relayed[t], near) for t in range(nt)],
            merged=[remote(t, 2, own[t].at[q(*far)], outs[t].at[2], far) for t in range(nt)],
            theirs=theirs, own=own, relayed=relayed, far=q(*far))

    EARLY_AT = 0.1

    def stages(self):
        return [(self.EARLY_AT, self.early), (self.mid_at, self.mid)]

    def start(self, ins, outs, scr):
        cps = self._copies(ins, outs, scr)
        for t in range(len(ins)):
            cps["swap"][t].start()
            cps["load"][t].start(priority=LOCAL)

    def early(self, ins, outs, scr):
        cps = self._copies(ins, outs, scr)
        for t in range(len(ins)):
            cps["load"][t].wait()
            cps["swap"][t].wait_recv()
            own, theirs = cps["own"][t], cps["theirs"][t]
            for j in range(4):
                own[j] = (own[j].astype(F32) + theirs[j].astype(F32)).astype(BF16)
            for kind in ("relay", "direct"):
                cps[kind][t].start()
            cps["keep"][t].start(priority=LOCAL)

    def mid(self, ins, outs, scr):
        cps = self._copies(ins, outs, scr)
        for t in range(len(ins)):
            cps["relay"][t].wait_recv()
            own, far = cps["own"][t], cps["far"]
            own[far] = (own[far].astype(F32) + cps["relayed"][t][...].astype(F32)).astype(BF16)
            cps["merged"][t].start()

    def finish(self, ins, outs, scr):
        cps = self._copies(ins, outs, scr)
        for t in range(len(ins)):
            cps["swap"][t].wait_send()
            cps["direct"][t].wait()
            cps["relay"][t].wait_send()
            cps["merged"][t].wait()
            cps["keep"][t].wait()


class _Comm:
    def __init__(self, groups, opt):
        self.names = {tag: list(g) for tag, (g, _) in groups.items()}
        self.gathers = {tag: _GatherCarry(list(g.values()), mid_at) for tag, (g, mid_at) in groups.items()}
        self.reduced = {}
        self.last = None
        self.opt = opt
        self.updated = {}

    def gathered(self, tag, outs):
        return dict(zip(self.names[tag], outs))

    def reduce_start(self, grads):
        names = list(grads)
        return _ExchangeCarry(names, [grads[n] for n in names])

    def reduce_done(self, carry, outs):
        self.reduced.update(zip(carry.names, outs))

    def update_start(self, names):
        w, m, v = zip(*[self.opt[n] for n in names])
        return _AdamWCarry(names, [self.reduced[n] for n in names], w, m, v)

    def update_done(self, carry, outs):
        self.updated.update({n: outs[4 * k:4 * k + 4] for k, n in enumerate(carry.names)})


def _adamw_math(w, g, m, v):
    m = B1 * m + (1.0 - B1) * g
    v = B2 * v + (1.0 - B2) * (g * g)
    m_hat = m / (1.0 - B1 ** STEP)
    v_hat = v / (1.0 - B2 ** STEP)
    delta = -LR * (m_hat / (jnp.sqrt(v_hat) + AEPS) + WD * w)
    return delta, m, v


def _adamw_big(recvs, ws, ms, vs, name):
    nw = len(ws)

    def body(*refs):
        ins, outs = refs[:4 * nw], refs[4 * nw:]
        for k in range(nw):
            @pl.when(pl.program_id(0) // 2 == k)
            def _(k=k):
                r_ref, w_ref, m_ref, v_ref = ins[4 * k:4 * k + 4]
                g_ref, d_ref, mo_ref, vo_ref = outs[4 * k:4 * k + 4]
                g = r_ref[0].astype(F32)
                for q in range(1, 3):
                    g = g + r_ref[q].astype(F32)
                d, mn, vn = _adamw_math(w_ref[...], g, m_ref[...], v_ref[...])
                g_ref[...] = g
                d_ref[...] = d
                mo_ref[...] = mn
                vo_ref[...] = vn

    in_specs, out_specs, out_shape, args = [], [], [], []
    for k, (r, w, m, v) in enumerate(zip(recvs, ws, ms, vs)):
        rows, n = w.shape
        tr = rows // 2
        tile = lambda s, k=k: jnp.clip(s - 2 * k, 0, 1)
        row = pl.BlockSpec((tr, n), lambda s, tile=tile: (tile(s), 0))
        in_specs += [pl.BlockSpec((3, tr, n), lambda s, tile=tile: (0, tile(s), 0)), row, row, row]
        out_specs += [row] * 4
        out_shape += [jax.ShapeDtypeStruct(w.shape, F32)] * 4
        args += [r, w, m, v]
    res = pl.pallas_call(
        body, grid=(2 * nw,), in_specs=in_specs, out_specs=out_specs, out_shape=out_shape,
        compiler_params=_cp(("arbitrary",)), name=name)(*args)
    return [res[4 * k:4 * k + 4] for k in range(nw)]


class _AdamWCarry:
    def __init__(self, names, recvs, ws, ms, vs):
        self.names = names
        nw = len(ws)
        self.halves = [(k, j, w.shape[0] // 2) for k, w in enumerate(ws) for j in range(2)]
        self.in_arrays = [a for quad in zip(recvs, ws, ms, vs) for a in quad]
        self.in_specs = [HBM] * (4 * nw)
        self.out_shape = [jax.ShapeDtypeStruct(w.shape, F32) for w in ws for _ in range(4)]
        self.out_specs = [HBM] * (4 * nw)
        tr, n = max(h[2] for h in self.halves), ws[0].shape[1]
        self.scratch = [pltpu.VMEM((2, 3, tr, n), BF16), pltpu.VMEM((2, 3, tr, n), F32),
                        pltpu.VMEM((2, 4, tr, n), F32), pltpu.SemaphoreType.DMA((2, 4)),
                        pltpu.SemaphoreType.DMA((2, 4))]

    def _copies(self, c, ins, outs, scr):
        rbuf, fbuf, obuf, in_sems, out_sems = scr
        k, j, tr = self.halves[c]
        s, rows = c % 2, pl.ds(j * tr, tr)
        loads = [pltpu.make_async_copy(ins[4 * k].at[:, rows, :], rbuf.at[s, :, pl.ds(0, tr), :], in_sems.at[s, 0])]
        loads += [pltpu.make_async_copy(ins[4 * k + i].at[rows, :], fbuf.at[s, i - 1, pl.ds(0, tr), :],
                                        in_sems.at[s, i]) for i in range(1, 4)]
        stores = [pltpu.make_async_copy(obuf.at[s, q, pl.ds(0, tr), :], outs[4 * k + q].at[rows, :],
                                        out_sems.at[s, q]) for q in range(4)]
        return loads, stores

    def start(self, ins, outs, scr):
        for c in range(2):
            for cp in self._copies(c, ins, outs, scr)[0]:
                cp.start()

    def stages(self):
        n = len(self.halves)
        return [((c + 1) / (n + 1), functools.partial(self.half, c)) for c in range(n)]

    def half(self, c, ins, outs, scr):
        rbuf, fbuf, obuf = scr[:3]
        _, _, tr = self.halves[c]
        s = c % 2
        loads, stores = self._copies(c, ins, outs, scr)
        for cp in loads:
            cp.wait()
        if c >= 2:
            for cp in self._copies(c - 2, ins, outs, scr)[1]:
                cp.wait()
        g = rbuf[s, 0, 0:tr].astype(F32)
        for q in range(1, 3):
            g = g + rbuf[s, q, 0:tr].astype(F32)
        d, mn, vn = _adamw_math(fbuf[s, 0, 0:tr], g, fbuf[s, 1, 0:tr], fbuf[s, 2, 0:tr])
        for q, val in enumerate((g, d, mn, vn)):
            obuf[s, q, 0:tr] = val
        for cp in stores:
            cp.start()
        if c + 2 < len(self.halves):
            for cp in self._copies(c + 2, ins, outs, scr)[0]:
                cp.start()

    def finish(self, ins, outs, scr):
        n = len(self.halves)
        for c in range(max(n - 2, 0), n):
            for cp in self._copies(c, ins, outs, scr)[1]:
                cp.wait()


SMALL_NAMES = ("ffn1_norm", "mix_norm", "ffn2_norm", "conv_b", "conv_ln_g", "conv_ln_b", "q_norm", "k_norm")
SROWS = 16
LOSS_ROW = len(SMALL_NAMES)
CWT = (32, 1, 128)


def _small_sums(gs, loss_row, gcw, carry=None):
    ns = len(SMALL_NAMES)
    widths = [g.shape[1] for g in gs]

    def body(*refs):
        it = iter(refs)
        take = lambda n: [next(it) for _ in range(n)]
        g_refs, (loss_ref, gcw_ref) = take(ns), take(2)
        cins = take(len(carry.in_arrays)) if carry else []
        tot_ref, ctot_ref = take(2)
        couts = take(len(carry.out_shape)) if carry else []
        send, slots, cslots, send_sems, recv_sems, csend_sems, crecv_sems = take(7)
        cscr = list(it)
        x, y, c = _place()
        me = 4 * x + 2 * y + c
        send[...] = jnp.zeros_like(send)
        for k in range(ns):
            send[k:k + 1, 0:widths[k]] = g_refs[k][...]
        send[LOSS_ROW:LOSS_ROW + 1, 0:128] = loss_ref[...]
        slots[me] = send[...]
        cslots[me] = gcw_ref[me]
        cps = []
        for k in range(1, NDEV):
            peer = (x ^ ((k >> 2) & 1), y ^ ((k >> 1) & 1), c ^ (k & 1))
            cps.append(pltpu.make_async_remote_copy(
                src_ref=send, dst_ref=slots.at[me], send_sem=send_sems.at[k - 1], recv_sem=recv_sems.at[k - 1],
                device_id=peer, device_id_type=MESH))
            cps.append(pltpu.make_async_remote_copy(
                src_ref=gcw_ref.at[4 * peer[0] + 2 * peer[1] + peer[2]], dst_ref=cslots.at[me],
                send_sem=csend_sems.at[k - 1], recv_sem=crecv_sems.at[k - 1], device_id=peer, device_id_type=MESH))
        for cp in cps:
            cp.start()
        stages = [stage for _, stage in carry.stages()] if carry else []
        if carry:
            carry.start(cins, couts, cscr)
        for stage in stages[:1]:
            stage(cins, couts, cscr)
        for cp in cps:
            cp.wait()
        tot = slots[0]
        ctot = cslots[0]
        for j in range(1, NDEV):
            tot = tot + slots[j]
            ctot = ctot + cslots[j]
        tot_ref[...] = tot
        ctot_ref[...] = ctot
        for stage in stages[1:]:
            stage(cins, couts, cscr)
        if carry:
            carry.finish(cins, couts, cscr)

    args = [*gs, loss_row, gcw]
    out_shape = [jax.ShapeDtypeStruct((SROWS, D), F32), jax.ShapeDtypeStruct(CWT, F32)]
    res = pl.pallas_call(
        body, in_specs=[VMEM] * len(args) + (carry.in_specs if carry else []),
        out_specs=[VMEM] * 2 + (carry.out_specs if carry else []),
        out_shape=out_shape + (carry.out_shape if carry else []),
        scratch_shapes=[pltpu.VMEM((SROWS, D), F32), pltpu.VMEM((NDEV, SROWS, D), F32),
                        pltpu.VMEM((NDEV,) + CWT, F32)]
                       + [pltpu.SemaphoreType.DMA((NDEV - 1,))] * 4 + (carry.scratch if carry else []),
        name="small_sums")(*args, *(carry.in_arrays if carry else []))
    return res[0], res[1], res[2:]


def _adamw_small(tot, ctot, ws, ms, vs, wcw, mcw, vcw):
    ns = len(SMALL_NAMES)
    widths = [w.shape[1] for w in ws]

    def body(*refs):
        it = iter(refs)
        take = lambda n: [next(it) for _ in range(n)]
        (tot_ref, ctot_ref), w_refs, m_refs, v_refs = take(2), take(ns), take(ns), take(ns)
        wcw_ref, mcw_ref, vcw_ref = take(3)
        outs = [take(4) for _ in range(ns)]
        cw_outs, (loss_out,) = take(4), take(1)

        def step(g, w_ref, m_ref, v_ref, o):
            d, mn, vn = _adamw_math(w_ref[...], g, m_ref[...], v_ref[...])
            o[0][...], o[1][...], o[2][...], o[3][...] = g, d, mn, vn

        for k in range(ns):
            step(tot_ref[k:k + 1, 0:widths[k]], w_refs[k], m_refs[k], v_refs[k], outs[k])
        step(ctot_ref[0:CK, :, 0:HD], wcw_ref, mcw_ref, vcw_ref, cw_outs)
        loss_out[...] = tot_ref[LOSS_ROW:LOSS_ROW + 1, 0:128]

    args = [tot, ctot, *ws, *ms, *vs, wcw, mcw, vcw]
    out_shape = ([jax.ShapeDtypeStruct((1, n), F32) for n in widths for _ in range(4)]
                 + [jax.ShapeDtypeStruct((CK, 1, HD), F32)] * 4 + [jax.ShapeDtypeStruct((1, 128), F32)])
    res = pl.pallas_call(
        body, in_specs=[VMEM] * len(args), out_specs=[VMEM] * len(out_shape), out_shape=out_shape,
        name="adamw_small")(*args)
    per = [res[4 * k:4 * k + 4] for k in range(ns)]
    return per, res[4 * ns:4 * ns + 4], res[-1]


def kernel(x, ffn1_norm, ffn1_w_gate, ffn1_w_up, ffn1_w_down, mix_norm, w_in, q_norm, k_norm, conv_w, conv_b, conv_ln_g, conv_ln_b, w_out, ffn2_norm, ffn2_w_gate, ffn2_w_up, ffn2_w_down, loss_target, m_ffn1_norm, m_ffn1_w_gate, m_ffn1_w_up, m_ffn1_w_down, m_mix_norm, m_w_in, m_q_norm, m_k_norm, m_conv_w, m_conv_b, m_conv_ln_g, m_conv_ln_b, m_w_out, m_ffn2_norm, m_ffn2_w_gate, m_ffn2_w_up, m_ffn2_w_down, v_ffn1_norm, v_ffn1_w_gate, v_ffn1_w_up, v_ffn1_w_down, v_mix_norm, v_w_in, v_q_norm, v_k_norm, v_conv_w, v_conv_b, v_conv_ln_g, v_conv_ln_b, v_w_out, v_ffn2_norm, v_ffn2_w_gate, v_ffn2_w_up, v_ffn2_w_down):
    P = dict(ffn1_norm=ffn1_norm, ffn1_w_gate=ffn1_w_gate, ffn1_w_up=ffn1_w_up, ffn1_w_down=ffn1_w_down,
             mix_norm=mix_norm, w_in=w_in, q_norm=q_norm, k_norm=k_norm, conv_w=conv_w, conv_b=conv_b,
             conv_ln_g=conv_ln_g, conv_ln_b=conv_ln_b, w_out=w_out, ffn2_norm=ffn2_norm,
             ffn2_w_gate=ffn2_w_gate, ffn2_w_up=ffn2_w_up, ffn2_w_down=ffn2_w_down)
    M = dict(ffn1_norm=m_ffn1_norm, ffn1_w_gate=m_ffn1_w_gate, ffn1_w_up=m_ffn1_w_up, ffn1_w_down=m_ffn1_w_down,
             mix_norm=m_mix_norm, w_in=m_w_in, q_norm=m_q_norm, k_norm=m_k_norm, conv_w=m_conv_w, conv_b=m_conv_b,
             conv_ln_g=m_conv_ln_g, conv_ln_b=m_conv_ln_b, w_out=m_w_out, ffn2_norm=m_ffn2_norm,
             ffn2_w_gate=m_ffn2_w_gate, ffn2_w_up=m_ffn2_w_up, ffn2_w_down=m_ffn2_w_down)
    V = dict(ffn1_norm=v_ffn1_norm, ffn1_w_gate=v_ffn1_w_gate, ffn1_w_up=v_ffn1_w_up, ffn1_w_down=v_ffn1_w_down,
             mix_norm=v_mix_norm, w_in=v_w_in, q_norm=v_q_norm, k_norm=v_k_norm, conv_w=v_conv_w, conv_b=v_conv_b,
             conv_ln_g=v_conv_ln_g, conv_ln_b=v_conv_ln_b, w_out=v_w_out, ffn2_norm=v_ffn2_norm,
             ffn2_w_gate=v_ffn2_w_gate, ffn2_w_up=v_ffn2_w_up, ffn2_w_down=v_ffn2_w_down)
    order = ["ffn1_norm", "ffn1_w_gate", "ffn1_w_up", "ffn1_w_down", "mix_norm", "w_in", "q_norm", "k_norm",
             "conv_w", "conv_b", "conv_ln_g", "conv_ln_b", "w_out", "ffn2_norm", "ffn2_w_gate", "ffn2_w_up",
             "ffn2_w_down"]
    B, S, _ = x.shape
    T = B * S

    bigs = [("wg1", "ffn1_w_gate", True), ("wu1", "ffn1_w_up", True), ("wd1", "ffn1_w_down", False),
            ("win", "w_in", True), ("wout", "w_out", False),
            ("wg2", "ffn2_w_gate", True), ("wu2", "ffn2_w_up", True), ("wd2", "ffn2_w_down", False)]
    hm = lambda a, tr: jnp.transpose(a[0]) if tr else a[0]
    cw_pad = jnp.zeros((32, 128), F32).at[0:CK, 0:HD].set(conv_w[0])
    shard = {ln: (hm(P[pn], tr), BF16) for ln, pn, tr in bigs}
    gathered = _run_carry(_GatherCarry([shard["wg1"], shard["wu1"], (cw_pad, F32)]), "gather_first")
    W = {"wg1": gathered[0], "wu1": gathered[1]}
    cwg = gathered[2].reshape(NDEV, 32, 128)[:, 0:CK, 0:HD]
    W["conv_w"] = jnp.transpose(cwg, (1, 0, 2)).reshape(CK, DC)
    norms = {n: P[n] for n in SMALL_NAMES}
    comm = _Comm({"down_in": ({n: shard[n] for n in ("wd1", "win")}, 0.5),
                  "ffn2": ({n: shard[n] for n in ("wg2", "wu2", "wd2", "wout")}, 0.5)},
                 opt={ln: (hm(P[pn], tr), hm(M[pn], tr), hm(V[pn], tr)) for ln, pn, tr in bigs})

    loss_part, gx, _, small = _local_step(x.reshape(T, D), loss_target.reshape(T, D), norms, W, B, S, comm)

    G, Dl, Mn, Vn = {}, {}, {}, {}
    dcw = small["conv_w"].reshape(CK, NDEV, HD).transpose(1, 0, 2)
    loss_row = jnp.zeros((1, 128), F32).at[0, 0].set(loss_part)
    gcw = jnp.pad(dcw[:, :, None, :], ((0, 0), (0, CWT[0] - CK), (0, 0), (0, CWT[2] - HD)))
    taps_first = lambda a: jnp.transpose(a, (1, 0, 2))
    tot, ctot, got = _small_sums([small[n] for n in SMALL_NAMES], loss_row, gcw, carry=comm.last)
    comm.reduce_done(comm.last, got)
    per, cw_outs, loss_out = _adamw_small(
        tot, ctot, [P[n] for n in SMALL_NAMES], [M[n] for n in SMALL_NAMES], [V[n] for n in SMALL_NAMES],
        taps_first(P["conv_w"]), taps_first(M["conv_w"]), taps_first(V["conv_w"]))
    loss = loss_out[0, 0]
    for n, outs in zip(SMALL_NAMES, per):
        G[n], Dl[n], Mn[n], Vn[n] = outs
    G["conv_w"], Dl["conv_w"], Mn["conv_w"], Vn["conv_w"] = [taps_first(o) for o in cw_outs]

    group = ("wd1", "wg1", "wu1")
    w, m, v = zip(*[comm.opt[ln] for ln in group])
    comm.updated.update(zip(group, _adamw_big([comm.reduced[ln] for ln in group], w, m, v, "adamw_ffn1")))
    for ln, pn, tr in bigs:
        G[pn], Dl[pn], Mn[pn], Vn[pn] = [(jnp.transpose(o) if tr else o)[None] for o in comm.updated[ln]]

    return (loss, gx.reshape(B, S, D), *[G[n] for n in order], *[Dl[n] for n in order],
            *[Mn[n] for n in order], *[Vn[n] for n in order])
```
